```python
import jax, jax.numpy as jnp
from jax import lax
import numpy as np

D_MODEL = 1024
BATCH = 8
SEQ = 8192
DEPTH = 1

CTX_LEN = 256
GRID_W = 64
D_CONV = 1024
CONV_W = 3
N_Q_HEADS = 8
N_KV_HEADS = 2
HEAD_DIM = 128
GROUP = N_Q_HEADS // N_KV_HEADS
ROPE_THETA = 10000.0
D_FF = 2816
Q_BLOCK = 128
EPS = 1e-6
N_MOD = 9

Q_W = N_Q_HEADS * HEAD_DIM
KV_W = N_KV_HEADS * HEAD_DIM
PIECE_WIDTHS = (D_CONV, D_CONV, D_CONV, Q_W, KV_W, KV_W, 2 * D_MODEL)
D_IN = int(sum(PIECE_WIDTHS))
SPLITS = tuple(int(s) for s in np.cumsum(PIECE_WIDTHS)[:-1])

kernel_name = "hybrid_shortconv_gqa_macaron_dit_layer"


def rms_norm(x, g):
    xf = x.astype(jnp.float32)
    y = xf * lax.rsqrt(jnp.mean(xf * xf, axis=-1, keepdims=True) + EPS)
    return (y * g.astype(jnp.float32)).astype(x.dtype)


def modulate(h, shift, scale):
    return h * (1.0 + scale) + shift


def swiglu(h, w_in, w_out):
    a, b = jnp.split(h @ w_in, 2, axis=-1)
    return (jax.nn.silu(a) * b) @ w_out


def axial_rope_tables(n_tokens):
    rows = n_tokens // GRID_W
    row = jnp.repeat(jnp.arange(rows), GRID_W).astype(jnp.float32)
    col = jnp.tile(jnp.arange(GRID_W), rows).astype(jnp.float32)
    n_freq = HEAD_DIM // 4
    inv = ROPE_THETA ** (-jnp.arange(n_freq, dtype=jnp.float32) / n_freq)
    ang = jnp.stack([row[:, None] * inv, col[:, None] * inv], axis=1)
    return jnp.cos(ang), jnp.sin(ang)


def apply_axial_rope(x, cos, sin):
    xs = x.reshape(x.shape[:3] + (2, 2, HEAD_DIM // 4))
    x1, x2 = xs[..., 0, :], xs[..., 1, :]
    c = cos[None, :, None].astype(x.dtype)
    s = sin[None, :, None].astype(x.dtype)
    out = jnp.stack([x1 * c - x2 * s, x2 * c + x1 * s], axis=-2)
    return out.reshape(x.shape)


def short_conv(u, w):
    L = u.shape[1]
    pad = CONV_W // 2
    up = jnp.pad(u, ((0, 0), (pad, pad), (0, 0)))
    y = up[:, 0:L] * w[0]
    for j in range(1, CONV_W):
        y = y + up[:, j:j + L] * w[j]
    return y


def conv_branch(bg, cg, vc, conv_w):
    return bg * short_conv(cg * vc, conv_w)


def gqa_attend(q, k, v):
    s = jnp.einsum('btkgd,blkd->bkgtl', q, k, preferred_element_type=jnp.float32) * (HEAD_DIM ** -0.5)
    p = jax.nn.softmax(s, axis=-1).astype(v.dtype)
    return jnp.einsum('bkgtl,blkd->btkgd', p, v)


def latent_attention(q_lat, k_all, v_all):
    B, S = q_lat.shape[:2]
    nblk = S // Q_BLOCK
    qb = q_lat.reshape(B, nblk, Q_BLOCK, N_KV_HEADS, GROUP, HEAD_DIM).swapaxes(0, 1)
    ob = lax.map(lambda qi: gqa_attend(qi, k_all, v_all), qb)
    return ob.swapaxes(0, 1).reshape(B, S, Q_W)


def split_heads(q, k, v, q_g, k_g):
    B, L = q.shape[:2]
    q = rms_norm(q.reshape(B, L, N_Q_HEADS, HEAD_DIM), q_g)
    k = rms_norm(k.reshape(B, L, N_KV_HEADS, HEAD_DIM), k_g)
    v = v.reshape(B, L, N_KV_HEADS, HEAD_DIM)
    return q, k, v


def merge_branches(y_conv, o_attn, gate_logits, w_bc, w_ba, w_out):
    g_conv, g_attn = jnp.split(jax.nn.sigmoid(gate_logits), 2, axis=-1)
    return (g_conv * (y_conv @ w_bc) + g_attn * (o_attn @ w_ba)) @ w_out


def _fwd_setup_inputs(seed: int = 0) -> dict:
    key = jax.random.key(seed)
    ks = jax.random.split(key, 24)

    def nrm(k, shape, scale):
        return jax.random.normal(k, shape, jnp.float32) * scale

    def gain(k, shape):
        return 1.0 + 0.1 * jax.random.normal(k, shape, jnp.float32)

    L = DEPTH
    return {
        "x": nrm(ks[0], (BATCH, SEQ, D_MODEL), 1.0),
        "c": nrm(ks[1], (BATCH, D_MODEL), 1.0),
        "ctx": nrm(ks[2], (BATCH, CTX_LEN, D_MODEL), 1.0),
        "c_ctx": nrm(ks[3], (D_MODEL,), 1.0),
        "w_mod": nrm(ks[4], (L, D_MODEL, N_MOD * D_MODEL), 0.5 * D_MODEL ** -0.5),
        "b_mod": nrm(ks[5], (L, N_MOD * D_MODEL), 0.02),
        "norm1_g": gain(ks[6], (L, D_MODEL)),
        "norm2_g": gain(ks[7], (L, D_MODEL)),
        "norm3_g": gain(ks[8], (L, D_MODEL)),
        "ffn1_w_in": nrm(ks[9], (L, D_MODEL, 2 * D_FF), D_MODEL ** -0.5),
        "ffn1_w_out": nrm(ks[10], (L, D_FF, D_MODEL), D_FF ** -0.5),
        "w_in": nrm(ks[11], (L, D_MODEL, D_IN), D_MODEL ** -0.5),
        "conv_w": nrm(ks[12], (L, CONV_W, D_CONV), CONV_W ** -0.5),
        "q_norm_g": gain(ks[13], (L, HEAD_DIM)),
        "k_norm_g": gain(ks[14], (L, HEAD_DIM)),
        "w_branch_conv": nrm(ks[15], (L, D_CONV, D_MODEL), D_CONV ** -0.5),
        "w_branch_attn": nrm(ks[16], (L, Q_W, D_MODEL), Q_W ** -0.5),
        "w_out": nrm(ks[17], (L, D_MODEL, D_MODEL), D_MODEL ** -0.5),
        "ffn2_w_in": nrm(ks[18], (L, D_MODEL, 2 * D_FF), D_MODEL ** -0.5),
        "ffn2_w_out": nrm(ks[19], (L, D_FF, D_MODEL), D_FF ** -0.5),
        "final_g": gain(ks[20], (D_MODEL,)),
    }


def _fwd_reference(x, c, ctx, c_ctx, w_mod, b_mod, norm1_g, norm2_g, norm3_g,
              ffn1_w_in, ffn1_w_out, w_in, conv_w, q_norm_g, k_norm_g,
              w_branch_conv, w_branch_attn, w_out, ffn2_w_in, ffn2_w_out, final_g):
    B, S, _ = x.shape
    cos, sin = axial_rope_tables(S)
    cx = ctx
    for layer in range(DEPTH):
        last = layer == DEPTH - 1
        mod_lat = (jax.nn.silu(c) @ w_mod[layer] + b_mod[layer])[:, None, :]
        mod_ctx = (jax.nn.silu(c_ctx) @ w_mod[layer] + b_mod[layer])[None, None, :]
        ml = jnp.split(mod_lat, N_MOD, axis=-1)
        mc = jnp.split(mod_ctx, N_MOD, axis=-1)

        x = x + 0.5 * ml[2] * swiglu(modulate(rms_norm(x, norm1_g[layer]), ml[0], ml[1]),
                                      ffn1_w_in[layer], ffn1_w_out[layer])
        cx = cx + 0.5 * mc[2] * swiglu(modulate(rms_norm(cx, norm1_g[layer]), mc[0], mc[1]),
                                        ffn1_w_in[layer], ffn1_w_out[layer])

        hx = modulate(rms_norm(x, norm2_g[layer]), ml[3], ml[4])
        hc = modulate(rms_norm(cx, norm2_g[layer]), mc[3], mc[4])
        bg_l, cg_l, vc_l, q_l, k_l, v_l, gt_l = jnp.split(hx @ w_in[layer], SPLITS, axis=-1)
        bg_c, cg_c, vc_c, q_c, k_c, v_c, gt_c = jnp.split(hc @ w_in[layer], SPLITS, axis=-1)

        y_conv_l = conv_branch(bg_l, cg_l, vc_l, conv_w[layer])

        q_l, k_l, v_l = split_heads(q_l, k_l, v_l, q_norm_g[layer], k_norm_g[layer])
        q_c, k_c, v_c = split_heads(q_c, k_c, v_c, q_norm_g[layer], k_norm_g[layer])
        q_l = apply_axial_rope(q_l, cos, sin)
        k_l = apply_axial_rope(k_l, cos, sin)
        k_all = jnp.concatenate([k_c, k_l], axis=1)
        v_all = jnp.concatenate([v_c, v_l], axis=1)
        o_l = latent_attention(q_l, k_all, v_all)

        x = x + ml[5] * merge_branches(y_conv_l, o_l, gt_l, w_branch_conv[layer],
                                       w_branch_attn[layer], w_out[layer])

        if not last:
            y_conv_c = conv_branch(bg_c, cg_c, vc_c, conv_w[layer])
            o_c = gqa_attend(q_c.reshape(B, cx.shape[1], N_KV_HEADS, GROUP, HEAD_DIM), k_c, v_c)
            o_c = o_c.reshape(B, cx.shape[1], Q_W)
            cx = cx + mc[5] * merge_branches(y_conv_c, o_c, gt_c, w_branch_conv[layer],
                                             w_branch_attn[layer], w_out[layer])
            cx = cx + 0.5 * mc[8] * swiglu(modulate(rms_norm(cx, norm3_g[layer]), mc[6], mc[7]),
                                            ffn2_w_in[layer], ffn2_w_out[layer])

        x = x + 0.5 * ml[8] * swiglu(modulate(rms_norm(x, norm3_g[layer]), ml[6], ml[7]),
                                      ffn2_w_in[layer], ffn2_w_out[layer])
    return rms_norm(x, final_g)


import jax as _jax
import jax.numpy as _jnp

TWIN_FORMAT = 'train_step'
FWD_PARAMS = ['x', 'c', 'ctx', 'c_ctx', 'w_mod', 'b_mod', 'norm1_g', 'norm2_g', 'norm3_g', 'ffn1_w_in', 'ffn1_w_out', 'w_in', 'conv_w', 'q_norm_g', 'k_norm_g', 'w_branch_conv', 'w_branch_attn', 'w_out', 'ffn2_w_in', 'ffn2_w_out', 'final_g']
TWIN_WEIGHTS = ['c_ctx', 'w_mod', 'b_mod', 'norm1_g', 'norm2_g', 'norm3_g', 'ffn1_w_in', 'ffn1_w_out', 'w_in', 'conv_w', 'q_norm_g', 'k_norm_g', 'w_branch_conv', 'w_branch_attn', 'w_out', 'ffn2_w_in', 'ffn2_w_out', 'final_g']
TWIN_DIFF_INPUT = 'x'
TWIN_INPUTS = ['x', 'c', 'ctx', 'c_ctx', 'w_mod', 'b_mod', 'norm1_g', 'norm2_g', 'norm3_g', 'ffn1_w_in', 'ffn1_w_out', 'w_in', 'conv_w', 'q_norm_g', 'k_norm_g', 'w_branch_conv', 'w_branch_attn', 'w_out', 'ffn2_w_in', 'ffn2_w_out', 'final_g', 'loss_target', 'm_c_ctx', 'm_w_mod', 'm_b_mod', 'm_norm1_g', 'm_norm2_g', 'm_norm3_g', 'm_ffn1_w_in', 'm_ffn1_w_out', 'm_w_in', 'm_conv_w', 'm_q_norm_g', 'm_k_norm_g', 'm_w_branch_conv', 'm_w_branch_attn', 'm_w_out', 'm_ffn2_w_in', 'm_ffn2_w_out', 'm_final_g', 'v_c_ctx', 'v_w_mod', 'v_b_mod', 'v_norm1_g', 'v_norm2_g', 'v_norm3_g', 'v_ffn1_w_in', 'v_ffn1_w_out', 'v_w_in', 'v_conv_w', 'v_q_norm_g', 'v_k_norm_g', 'v_w_branch_conv', 'v_w_branch_attn', 'v_w_out', 'v_ffn2_w_in', 'v_ffn2_w_out', 'v_final_g']
TWIN_OUTPUTS = ['loss', 'grad_x', 'grad_c_ctx', 'grad_w_mod', 'grad_b_mod', 'grad_norm1_g', 'grad_norm2_g', 'grad_norm3_g', 'grad_ffn1_w_in', 'grad_ffn1_w_out', 'grad_w_in', 'grad_conv_w', 'grad_q_norm_g', 'grad_k_norm_g', 'grad_w_branch_conv', 'grad_w_branch_attn', 'grad_w_out', 'grad_ffn2_w_in', 'grad_ffn2_w_out', 'grad_final_g', 'delta_c_ctx', 'delta_w_mod', 'delta_b_mod', 'delta_norm1_g', 'delta_norm2_g', 'delta_norm3_g', 'delta_ffn1_w_in', 'delta_ffn1_w_out', 'delta_w_in', 'delta_conv_w', 'delta_q_norm_g', 'delta_k_norm_g', 'delta_w_branch_conv', 'delta_w_branch_attn', 'delta_w_out', 'delta_ffn2_w_in', 'delta_ffn2_w_out', 'delta_final_g', 'new_m_c_ctx', 'new_m_w_mod', 'new_m_b_mod', 'new_m_norm1_g', 'new_m_norm2_g', 'new_m_norm3_g', 'new_m_ffn1_w_in', 'new_m_ffn1_w_out', 'new_m_w_in', 'new_m_conv_w', 'new_m_q_norm_g', 'new_m_k_norm_g', 'new_m_w_branch_conv', 'new_m_w_branch_attn', 'new_m_w_out', 'new_m_ffn2_w_in', 'new_m_ffn2_w_out', 'new_m_final_g', 'new_v_c_ctx', 'new_v_w_mod', 'new_v_b_mod', 'new_v_norm1_g', 'new_v_norm2_g', 'new_v_norm3_g', 'new_v_ffn1_w_in', 'new_v_ffn1_w_out', 'new_v_w_in', 'new_v_conv_w', 'new_v_q_norm_g', 'new_v_k_norm_g', 'new_v_w_branch_conv', 'new_v_w_branch_attn', 'new_v_w_out', 'new_v_ffn2_w_in', 'new_v_ffn2_w_out', 'new_v_final_g']
TWIN_LEAF_KINDS = {'loss': 'loss', 'grad_x': 'grad_x', 'grad_c_ctx': 'grad_w', 'grad_w_mod': 'grad_w', 'grad_b_mod': 'grad_w', 'grad_norm1_g': 'grad_w', 'grad_norm2_g': 'grad_w', 'grad_norm3_g': 'grad_w', 'grad_ffn1_w_in': 'grad_w', 'grad_ffn1_w_out': 'grad_w', 'grad_w_in': 'grad_w', 'grad_conv_w': 'grad_w', 'grad_q_norm_g': 'grad_w', 'grad_k_norm_g': 'grad_w', 'grad_w_branch_conv': 'grad_w', 'grad_w_branch_attn': 'grad_w', 'grad_w_out': 'grad_w', 'grad_ffn2_w_in': 'grad_w', 'grad_ffn2_w_out': 'grad_w', 'grad_final_g': 'grad_w', 'delta_c_ctx': 'delta_w', 'delta_w_mod': 'delta_w', 'delta_b_mod': 'delta_w', 'delta_norm1_g': 'delta_w', 'delta_norm2_g': 'delta_w', 'delta_norm3_g': 'delta_w', 'delta_ffn1_w_in': 'delta_w', 'delta_ffn1_w_out': 'delta_w', 'delta_w_in': 'delta_w', 'delta_conv_w': 'delta_w', 'delta_q_norm_g': 'delta_w', 'delta_k_norm_g': 'delta_w', 'delta_w_branch_conv': 'delta_w', 'delta_w_branch_attn': 'delta_w', 'delta_w_out': 'delta_w', 'delta_ffn2_w_in': 'delta_w', 'delta_ffn2_w_out': 'delta_w', 'delta_final_g': 'delta_w', 'new_m_c_ctx': 'new_m', 'new_m_w_mod': 'new_m', 'new_m_b_mod': 'new_m', 'new_m_norm1_g': 'new_m', 'new_m_norm2_g': 'new_m', 'new_m_norm3_g': 'new_m', 'new_m_ffn1_w_in': 'new_m', 'new_m_ffn1_w_out': 'new_m', 'new_m_w_in': 'new_m', 'new_m_conv_w': 'new_m', 'new_m_q_norm_g': 'new_m', 'new_m_k_norm_g': 'new_m', 'new_m_w_branch_conv': 'new_m', 'new_m_w_branch_attn': 'new_m', 'new_m_w_out': 'new_m', 'new_m_ffn2_w_in': 'new_m', 'new_m_ffn2_w_out': 'new_m', 'new_m_final_g': 'new_m', 'new_v_c_ctx': 'new_v', 'new_v_w_mod': 'new_v', 'new_v_b_mod': 'new_v', 'new_v_norm1_g': 'new_v', 'new_v_norm2_g': 'new_v', 'new_v_norm3_g': 'new_v', 'new_v_ffn1_w_in': 'new_v', 'new_v_ffn1_w_out': 'new_v', 'new_v_w_in': 'new_v', 'new_v_conv_w': 'new_v', 'new_v_q_norm_g': 'new_v', 'new_v_k_norm_g': 'new_v', 'new_v_w_branch_conv': 'new_v', 'new_v_w_branch_attn': 'new_v', 'new_v_w_out': 'new_v', 'new_v_ffn2_w_in': 'new_v', 'new_v_ffn2_w_out': 'new_v', 'new_v_final_g': 'new_v'}


def _forward(args):
    return _fwd_reference(*[args[k] for k in FWD_PARAMS])


def _output_shape():
    def fwd():
        inp = _fwd_setup_inputs(0)
        return _fwd_reference(*[inp[k] for k in FWD_PARAMS])
    out = _jax.eval_shape(fwd)
    return out.shape, out.dtype

N_MICROBATCH = 1
ADAM_LR = 0.001
ADAM_B1 = 0.9
ADAM_B2 = 0.999
ADAM_EPS = 1e-08
ADAM_WD = 0.01
ADAM_STEP = 10
PER_EXAMPLE_BATCH_AXIS = {'x': 0, 'c': 0, 'ctx': 0, 'loss_target': 0}
SHARED_INPUTS = []
_WEIGHT_DTYPES = {'c_ctx': _jnp.float32, 'w_mod': _jnp.float32, 'b_mod': _jnp.float32, 'norm1_g': _jnp.float32, 'norm2_g': _jnp.float32, 'norm3_g': _jnp.float32, 'ffn1_w_in': _jnp.float32, 'ffn1_w_out': _jnp.float32, 'w_in': _jnp.float32, 'conv_w': _jnp.float32, 'q_norm_g': _jnp.float32, 'k_norm_g': _jnp.float32, 'w_branch_conv': _jnp.float32, 'w_branch_attn': _jnp.float32, 'w_out': _jnp.float32, 'ffn2_w_in': _jnp.float32, 'ffn2_w_out': _jnp.float32, 'final_g': _jnp.float32}
MOMENT_SCALE = {'c_ctx': 6.265952e-03, 'w_mod': 1.002246e-01, 'b_mod': 2.028359e-01, 'norm1_g': 4.039178e-02, 'norm2_g': 9.125426e-02, 'norm3_g': 3.787674e-02, 'ffn1_w_in': 1.819948e-02, 'ffn1_w_out': 2.984973e-02, 'w_in': 3.731866e-02, 'conv_w': 5.423642e-02, 'q_norm_g': 1.013451e-02, 'k_norm_g': 1.027249e-02, 'w_branch_conv': 5.365724e-02, 'w_branch_attn': 1.343256e-02, 'w_out': 5.600329e-02, 'ffn2_w_in': 1.700371e-02, 'ffn2_w_out': 2.793941e-02, 'final_g': 6.461465e+01}


def _to_microbatches(a, axis):
    t = _jnp.moveaxis(a, axis, 0)
    t = t.reshape((N_MICROBATCH, t.shape[0] // N_MICROBATCH) + t.shape[1:])
    return _jnp.moveaxis(t, 1, axis + 1)


def setup_inputs(seed: int = 0) -> dict:
    inp = _fwd_setup_inputs(seed)
    key = _jax.random.fold_in(_jax.random.key(seed), 7919)
    shape, _ = _output_shape()
    out = dict(inp)
    out["loss_target"] = _jax.random.normal(_jax.random.fold_in(key, 0), shape, _jnp.float32)
    for i, name in enumerate(TWIN_WEIGHTS):
        w = inp[name].astype(_jnp.float32)
        if MOMENT_SCALE is None:
            s = _jnp.sqrt(_jnp.mean(_jnp.square(w)) + 1e-30)
        else:
            s = MOMENT_SCALE[name]
        km, kv = _jax.random.split(_jax.random.fold_in(key, i + 1))
        out[name] = w
        out["m_" + name] = s * _jax.random.normal(km, w.shape, _jnp.float32)
        out["v_" + name] = (s * s) * _jax.random.uniform(kv, w.shape, _jnp.float32, 0.5, 1.5)
    if N_MICROBATCH > 1:
        for name, axis in PER_EXAMPLE_BATCH_AXIS.items():
            out[name] = _to_microbatches(out[name], axis)
    return {'x': out['x'], 'c': out['c'], 'ctx': out['ctx'], 'c_ctx': out['c_ctx'], 'w_mod': out['w_mod'], 'b_mod': out['b_mod'], 'norm1_g': out['norm1_g'], 'norm2_g': out['norm2_g'], 'norm3_g': out['norm3_g'], 'ffn1_w_in': out['ffn1_w_in'], 'ffn1_w_out': out['ffn1_w_out'], 'w_in': out['w_in'], 'conv_w': out['conv_w'], 'q_norm_g': out['q_norm_g'], 'k_norm_g': out['k_norm_g'], 'w_branch_conv': out['w_branch_conv'], 'w_branch_attn': out['w_branch_attn'], 'w_out': out['w_out'], 'ffn2_w_in': out['ffn2_w_in'], 'ffn2_w_out': out['ffn2_w_out'], 'final_g': out['final_g'], 'loss_target': out['loss_target'], 'm_c_ctx': out['m_c_ctx'], 'm_w_mod': out['m_w_mod'], 'm_b_mod': out['m_b_mod'], 'm_norm1_g': out['m_norm1_g'], 'm_norm2_g': out['m_norm2_g'], 'm_norm3_g': out['m_norm3_g'], 'm_ffn1_w_in': out['m_ffn1_w_in'], 'm_ffn1_w_out': out['m_ffn1_w_out'], 'm_w_in': out['m_w_in'], 'm_conv_w': out['m_conv_w'], 'm_q_norm_g': out['m_q_norm_g'], 'm_k_norm_g': out['m_k_norm_g'], 'm_w_branch_conv': out['m_w_branch_conv'], 'm_w_branch_attn': out['m_w_branch_attn'], 'm_w_out': out['m_w_out'], 'm_ffn2_w_in': out['m_ffn2_w_in'], 'm_ffn2_w_out': out['m_ffn2_w_out'], 'm_final_g': out['m_final_g'], 'v_c_ctx': out['v_c_ctx'], 'v_w_mod': out['v_w_mod'], 'v_b_mod': out['v_b_mod'], 'v_norm1_g': out['v_norm1_g'], 'v_norm2_g': out['v_norm2_g'], 'v_norm3_g': out['v_norm3_g'], 'v_ffn1_w_in': out['v_ffn1_w_in'], 'v_ffn1_w_out': out['v_ffn1_w_out'], 'v_w_in': out['v_w_in'], 'v_conv_w': out['v_conv_w'], 'v_q_norm_g': out['v_q_norm_g'], 'v_k_norm_g': out['v_k_norm_g'], 'v_w_branch_conv': out['v_w_branch_conv'], 'v_w_branch_attn': out['v_w_branch_attn'], 'v_w_out': out['v_w_out'], 'v_ffn2_w_in': out['v_ffn2_w_in'], 'v_ffn2_w_out': out['v_ffn2_w_out'], 'v_final_g': out['v_final_g']}


def _loss(weights, diff, rest, loss_target):
    with _jax.named_scope("forward"):
        args = {**rest, TWIN_DIFF_INPUT: diff, **{k: w.astype(_WEIGHT_DTYPES[k]) for k, w in weights.items()}}
        y = _forward(args)
    with _jax.named_scope("loss_head"):
        err = _jnp.square(y.astype(_jnp.float32) - loss_target)
        return 0.5 * _jnp.sum(_jnp.mean(err, axis=-1)) if err.ndim else 0.5 * err


def _adamw(w, g, m, v):
    m = ADAM_B1 * m + (1.0 - ADAM_B1) * g
    v = ADAM_B2 * v + (1.0 - ADAM_B2) * _jnp.square(g)
    m_hat = m / (1.0 - ADAM_B1 ** ADAM_STEP)
    v_hat = v / (1.0 - ADAM_B2 ** ADAM_STEP)
    delta = -ADAM_LR * (m_hat / (_jnp.sqrt(v_hat) + ADAM_EPS) + ADAM_WD * w)
    return delta, m, v


def reference(x, c, ctx, c_ctx, w_mod, b_mod, norm1_g, norm2_g, norm3_g, ffn1_w_in, ffn1_w_out, w_in, conv_w, q_norm_g, k_norm_g, w_branch_conv, w_branch_attn, w_out, ffn2_w_in, ffn2_w_out, final_g, loss_target, m_c_ctx, m_w_mod, m_b_mod, m_norm1_g, m_norm2_g, m_norm3_g, m_ffn1_w_in, m_ffn1_w_out, m_w_in, m_conv_w, m_q_norm_g, m_k_norm_g, m_w_branch_conv, m_w_branch_attn, m_w_out, m_ffn2_w_in, m_ffn2_w_out, m_final_g, v_c_ctx, v_w_mod, v_b_mod, v_norm1_g, v_norm2_g, v_norm3_g, v_ffn1_w_in, v_ffn1_w_out, v_w_in, v_conv_w, v_q_norm_g, v_k_norm_g, v_w_branch_conv, v_w_branch_attn, v_w_out, v_ffn2_w_in, v_ffn2_w_out, v_final_g):
    given = dict(x=x, c=c, ctx=ctx, c_ctx=c_ctx, w_mod=w_mod, b_mod=b_mod, norm1_g=norm1_g, norm2_g=norm2_g, norm3_g=norm3_g, ffn1_w_in=ffn1_w_in, ffn1_w_out=ffn1_w_out, w_in=w_in, conv_w=conv_w, q_norm_g=q_norm_g, k_norm_g=k_norm_g, w_branch_conv=w_branch_conv, w_branch_attn=w_branch_attn, w_out=w_out, ffn2_w_in=ffn2_w_in, ffn2_w_out=ffn2_w_out, final_g=final_g, loss_target=loss_target, m_c_ctx=m_c_ctx, m_w_mod=m_w_mod, m_b_mod=m_b_mod, m_norm1_g=m_norm1_g, m_norm2_g=m_norm2_g, m_norm3_g=m_norm3_g, m_ffn1_w_in=m_ffn1_w_in, m_ffn1_w_out=m_ffn1_w_out, m_w_in=m_w_in, m_conv_w=m_conv_w, m_q_norm_g=m_q_norm_g, m_k_norm_g=m_k_norm_g, m_w_branch_conv=m_w_branch_conv, m_w_branch_attn=m_w_branch_attn, m_w_out=m_w_out, m_ffn2_w_in=m_ffn2_w_in, m_ffn2_w_out=m_ffn2_w_out, m_final_g=m_final_g, v_c_ctx=v_c_ctx, v_w_mod=v_w_mod, v_b_mod=v_b_mod, v_norm1_g=v_norm1_g, v_norm2_g=v_norm2_g, v_norm3_g=v_norm3_g, v_ffn1_w_in=v_ffn1_w_in, v_ffn1_w_out=v_ffn1_w_out, v_w_in=v_w_in, v_conv_w=v_conv_w, v_q_norm_g=v_q_norm_g, v_k_norm_g=v_k_norm_g, v_w_branch_conv=v_w_branch_conv, v_w_branch_attn=v_w_branch_attn, v_w_out=v_w_out, v_ffn2_w_in=v_ffn2_w_in, v_ffn2_w_out=v_ffn2_w_out, v_final_g=v_final_g)
    weights = {n: given[n] for n in TWIN_WEIGHTS}
    shared = {n: given[n] for n in SHARED_INPUTS}
    per_example = {n: given[n] for n in ['x', 'c', 'ctx']}
    grad_fn = _jax.value_and_grad(_loss, argnums=(0, 1))

    def one_microbatch(ex, loss_target):
        ex = dict(ex)
        diff = ex.pop(TWIN_DIFF_INPUT)
        return grad_fn(weights, diff, {**shared, **ex}, loss_target)

    if N_MICROBATCH == 1:
        loss, (grad_w, grad_x) = one_microbatch(per_example, given["loss_target"])
    else:
        def body(carry, xs):
            loss_sum, grad_sum = carry
            l_k, (gw_k, gx_k) = one_microbatch(xs[0], xs[1])
            with _jax.named_scope("update"):
                return (loss_sum + l_k, _jax.tree.map(_jnp.add, grad_sum, gw_k)), gx_k

        init = (_jnp.zeros((), _jnp.float32), _jax.tree.map(_jnp.zeros_like, weights))
        (loss, grad_w), grad_x = _jax.lax.scan(body, init, (per_example, given["loss_target"]))
    with _jax.named_scope("update"):
        delta_w, new_m, new_v = {}, {}, {}
        for n in TWIN_WEIGHTS:
            delta_w[n], new_m[n], new_v[n] = _adamw(weights[n], grad_w[n], given["m_" + n], given["v_" + n])
    return (loss, grad_x, *[grad_w[n] for n in TWIN_WEIGHTS], *[delta_w[n] for n in TWIN_WEIGHTS],
            *[new_m[n] for n in TWIN_WEIGHTS], *[new_v[n] for n in TWIN_WEIGHTS])
```

```python
import functools

import jax
import jax.numpy as jnp
from jax import lax
from jax.experimental import pallas as pl
from jax.experimental.pallas import tpu as pltpu

F32 = jnp.float32
BF16 = jnp.bfloat16

HEAD_DIM = 128
N_Q_HEADS = 8
N_KV_HEADS = 2
GROUP = N_Q_HEADS // N_KV_HEADS
GRID_W = 64
ROPE_THETA = 10000.0
EPS = 1e-6
ATTN_SCALE = HEAD_DIM ** -0.5

ADAM_LR = 0.001
ADAM_B1 = 0.9
ADAM_B2 = 0.999
ADAM_EPS = 1e-08
ADAM_WD = 0.01
ADAM_STEP = 10

ROW = 256
HALO = 8
ACC_ROWS = 8
N_CHIPS = 4
N_DEV = 8
MESH = pl.DeviceIdType.MESH
VMEM_LIMIT = 48 * 1024 * 1024
ADAMW_BLOCK_BYTES = 1024 * 1024


def _pick(n, prefs):
    for p in prefs:
        if n % p == 0:
            return p
    return n


def _params(sem):
    return pltpu.CompilerParams(dimension_semantics=sem, vmem_limit_bytes=VMEM_LIMIT)


def _stream(i):
    return jnp.minimum(i, 1)


def _matmul(a, b, mode, out_dtype, name, tm=None, tn=None, tk=None):
    if mode == "nn":
        (M, K), (K2, N) = a.shape, b.shape
    elif mode == "nt":
        (M, K), (N, K2) = a.shape, b.shape
    else:
        (K, M), (K2, N) = a.shape, b.shape
    assert K == K2, (a.shape, b.shape, mode)
    tm = tm or _pick(M, (768, 512, 256, 128))
    tn = tn or _pick(N, (1024, 512, 256, 128))
    tk = tk or _pick(K, (1024, 768, 512, 256, 128))
    nk = K // tk
    if mode == "tn":
        a_spec = pl.BlockSpec((tk, tm), lambda i, j, k: (k, i))
    else:
        a_spec = pl.BlockSpec((tm, tk), lambda i, j, k: (i, k))
    if mode == "nt":
        b_spec = pl.BlockSpec((tn, tk), lambda i, j, k: (j, k))
    else:
        b_spec = pl.BlockSpec((tk, tn), lambda i, j, k: (k, j))
    dims = {"nn": ((1,), (0,)), "nt": ((1,), (1,)), "tn": ((0,), (0,))}[mode]
    use_scratch = nk > 1 and out_dtype != F32

    def body(a_ref, b_ref, o_ref, *scratch):
        p = lax.dot_general(a_ref[...].astype(BF16), b_ref[...].astype(BF16), (dims, ((), ())),
                            preferred_element_type=F32)
        if nk == 1:
            o_ref[...] = p.astype(o_ref.dtype)
            return
        acc_ref = scratch[0] if use_scratch else o_ref
        k = pl.program_id(2)

        @pl.when(k == 0)
        def _():
            acc_ref[...] = p

        @pl.when(k > 0)
        def _():
            acc_ref[...] += p

        if use_scratch:
            @pl.when(k == nk - 1)
            def _():
                o_ref[...] = acc_ref[...].astype(o_ref.dtype)

    return pl.pallas_call(
        body, name=name,
        grid=(M // tm, N // tn, nk),
        in_specs=[a_spec, b_spec],
        out_specs=pl.BlockSpec((tm, tn), lambda i, j, k: (i, j)),
        out_shape=jax.ShapeDtypeStruct((M, N), out_dtype),
        scratch_shapes=[pltpu.VMEM((tm, tn), F32)] if use_scratch else [],
        compiler_params=_params(("parallel", "parallel", "arbitrary")),
    )(a, b)


def _row_spec(width, col=0):
    return pl.BlockSpec((ROW, width), lambda i, col=col: (i, col))


def _mods_spec(D):
    return pl.BlockSpec((1, 16, D), lambda i: (_stream(i), 0, 0))


def _acc_spec(D):
    return pl.BlockSpec((1, ACC_ROWS, D), lambda i: (_stream(i), 0, 0))


def _vec_spec(rows, D):
    return pl.BlockSpec((rows, D), lambda i: (0, 0))


def _acc_init(acc_ref):
    i = pl.program_id(0)

    @pl.when(i <= 1)
    def _():
        acc_ref[...] = jnp.zeros_like(acc_ref)


def _acc_add(acc_ref, row, val):
    acc_ref[0, row:row + 1, :] += jnp.sum(val, axis=0, keepdims=True)


def _resid_rmsmod_fwd(xprev, branch, mods, g, gate, shift_idx, scale_idx, name):
    T, D = xprev.shape
    has_res = branch is not None

    def body(*refs):
        if has_res:
            x_ref, f_ref, m_ref, g_ref, xo_ref, h_ref = refs
        else:
            x_ref, m_ref, g_ref, h_ref = refs
        m = m_ref[0]
        x = x_ref[...]
        if has_res:
            gate_idx, fac = gate
            x = x + (fac * m[gate_idx:gate_idx + 1, :]) * f_ref[...]
            xo_ref[...] = x
        inv = lax.rsqrt(jnp.mean(x * x, axis=-1, keepdims=True) + EPS)
        y = (x * inv) * g_ref[...]
        h = y * (1.0 + m[scale_idx:scale_idx + 1, :]) + m[shift_idx:shift_idx + 1, :]
        h_ref[...] = h.astype(BF16)

    in_specs = [_row_spec(D)] + ([_row_spec(D)] if has_res else []) + [_mods_spec(D), _vec_spec(1, D)]
    args = [xprev] + ([branch] if has_res else []) + [mods, g]
    out_specs = ([_row_spec(D)] if has_res else []) + [_row_spec(D)]
    out_shape = ([jax.ShapeDtypeStruct((T, D), F32)] if has_res else []) + [jax.ShapeDtypeStruct((T, D), BF16)]
    out = pl.pallas_call(
        body, name=name, grid=(T // ROW,), in_specs=in_specs, out_specs=out_specs, out_shape=out_shape,
        compiler_params=_params(("parallel",)),
    )(*args)
    return out if has_res else (None, out[0])


def _loss_head(x2, f2, mods, final_g, target, name):
    T, D = x2.shape
    nt = T // ROW

    def body(x_ref, f_ref, m_ref, g_ref, t_ref, dx_ref, df_ref, acc_ref):
        _acc_init(acc_ref)
        i = pl.program_id(0)
        m = m_ref[0]
        gate = 0.5 * m[8:9, :]
        f = f_ref[...]
        x = x_ref[...] + gate * f
        inv = lax.rsqrt(jnp.mean(x * x, axis=-1, keepdims=True) + EPS)
        xn = x * inv
        fg = g_ref[...]
        lat = (i > 0).astype(F32)
        e = (xn * fg - t_ref[...]) * lat
        dy = e * (1.0 / D)
        dxn = dy * fg
        dx = inv * (dxn - xn * jnp.mean(dxn * xn, axis=-1, keepdims=True))
        dx_ref[...] = dx
        df_ref[...] = (gate * dx).astype(BF16)
        _acc_add(acc_ref, 0, (0.5 / D) * e * e)
        _acc_add(acc_ref, 1, dy * xn)
        _acc_add(acc_ref, 2, 0.5 * dx * f)

    return pl.pallas_call(
        body, name=name, grid=(nt,),
        in_specs=[_row_spec(D), _row_spec(D), _mods_spec(D), _vec_spec(1, D),
                  pl.BlockSpec((ROW, D), lambda i: (jnp.maximum(i - 1, 0), 0))],
        out_specs=[_row_spec(D), _row_spec(D), _acc_spec(D)],
        out_shape=[jax.ShapeDtypeStruct((T, D), F32), jax.ShapeDtypeStruct((T, D), BF16),
                   jax.ShapeDtypeStruct((2, ACC_ROWS, D), F32)],
        compiler_params=_params(("arbitrary",)),
    )(x2, f2, mods, final_g, target)


def _rmsmod_bwd(x, dh, dres, mods, g, shift_idx, scale_idx, gate, branch, name, skip_first_tile=False):
    T, D = x.shape
    nt = T // ROW
    has_gate = gate is not None

    def body(*refs):
        if has_gate:
            x_ref, dh_ref, dr_ref, b_ref, m_ref, g_ref, dx_ref, db_ref, acc_ref = refs
        else:
            x_ref, dh_ref, dr_ref, m_ref, g_ref, dx_ref, acc_ref = refs
        _acc_init(acc_ref)
        m = m_ref[0]
        x = x_ref[...]
        dh = dh_ref[...]
        gg = g_ref[...]
        inv = lax.rsqrt(jnp.mean(x * x, axis=-1, keepdims=True) + EPS)
        xn = x * inv
        y = xn * gg
        dy = dh * (1.0 + m[scale_idx:scale_idx + 1, :])
        dxn = dy * gg
        dx = inv * (dxn - xn * jnp.mean(dxn * xn, axis=-1, keepdims=True)) + dr_ref[...]
        dx_ref[...] = dx
        _acc_add(acc_ref, 0, dh)
        _acc_add(acc_ref, 1, dh * y)
        _acc_add(acc_ref, 2, dy * xn)
        if has_gate:
            gate_idx, fac = gate
            b = b_ref[...]
            db_ref[...] = ((fac * m[gate_idx:gate_idx + 1, :]) * dx).astype(BF16)
            _acc_add(acc_ref, 3, fac * dx * b)

    in_specs = [_row_spec(D), _row_spec(D), _row_spec(D)] + ([_row_spec(D)] if has_gate else []) + \
               [_mods_spec(D), _vec_spec(1, D)]
    args = [x, dh, dres] + ([branch] if has_gate else []) + [mods, g]
    if skip_first_tile:
        dx_spec = pl.BlockSpec((ROW, D), lambda i: (jnp.maximum(i - 1, 0), 0))
        dx_shape = jax.ShapeDtypeStruct((T - ROW, D), F32)
    else:
        dx_spec = _row_spec(D)
        dx_shape = jax.ShapeDtypeStruct((T, D), F32)
    out_specs = [dx_spec] + ([_row_spec(D)] if has_gate else []) + [_acc_spec(D)]
    out_shape = [dx_shape] + ([jax.ShapeDtypeStruct((T, D), BF16)] if has_gate else []) + \
                [jax.ShapeDtypeStruct((2, ACC_ROWS, D), F32)]
    out = pl.pallas_call(
        body, name=name, grid=(nt,), in_specs=in_specs, out_specs=out_specs, out_shape=out_shape,
        compiler_params=_params(("arbitrary",)),
    )(*args)
    if has_gate:
        return out
    return out[0], None, out[1]


def _swiglu_fwd(u, name):
    T, F2 = u.shape
    F = F2 // 2

    def body(u_ref, s_ref):
        a = u_ref[:, :F]
        b = u_ref[:, F:]
        s_ref[...] = ((a * jax.nn.sigmoid(a)) * b).astype(BF16)

    return pl.pallas_call(
        body, name=name, grid=(T // ROW,),
        in_specs=[_row_spec(F2)], out_specs=_row_spec(F),
        out_shape=jax.ShapeDtypeStruct((T, F), BF16),
        compiler_params=_params(("parallel",)),
    )(u)


def _swiglu_bwd(u, ds, name):
    T, F2 = u.shape
    F = F2 // 2

    def body(u_ref, ds_ref, du_ref):
        a = u_ref[:, :F]
        b = u_ref[:, F:]
        ds = ds_ref[...]
        sig = jax.nn.sigmoid(a)
        silu = a * sig
        du_ref[:, :F] = (ds * b * (sig * (1.0 + a * (1.0 - sig)))).astype(BF16)
        du_ref[:, F:] = (ds * silu).astype(BF16)

    return pl.pallas_call(
        body, name=name, grid=(T // ROW,),
        in_specs=[_row_spec(F2), _row_spec(F)], out_specs=_row_spec(F2),
        out_shape=jax.ShapeDtypeStruct((T, F2), BF16),
        compiler_params=_params(("parallel",)),
    )(u, ds)


def _halo_specs(width, col, nt):
    per = ROW // HALO
    prev = pl.BlockSpec((HALO, width), lambda i, col=col: (jnp.maximum(i * per - 1, 0), col))
    nxt = pl.BlockSpec((HALO, width), lambda i, col=col: (jnp.minimum((i + 1) * per, nt * per - 1), col))
    return prev, nxt


def _shift_rows(v, prev_row, next_row):
    rows = lax.broadcasted_iota(jnp.int32, v.shape, 0)
    down = jnp.where(rows == 0, prev_row, pltpu.roll(v, 1, 0))
    up = jnp.where(rows == v.shape[0] - 1, next_row, pltpu.roll(v, v.shape[0] - 1, 0))
    return down, up


def _conv_fwd(P, conv_w, D, name):
    T = P.shape[0]
    nt = T // ROW
    cg_p, cg_n = _halo_specs(D, 1, nt)
    vc_p, vc_n = _halo_specs(D, 2, nt)

    def body(bg_ref, cg_ref, vc_ref, cgp_ref, vcp_ref, cgn_ref, vcn_ref, w_ref, y_ref):
        i = pl.program_id(0)
        has_prev = (i != 1).astype(F32)
        has_next = (i != nt - 1).astype(F32)
        u = cg_ref[...] * vc_ref[...]
        up_row = cgp_ref[HALO - 1:HALO, :] * vcp_ref[HALO - 1:HALO, :] * has_prev
        un_row = cgn_ref[0:1, :] * vcn_ref[0:1, :] * has_next
        um1, up1 = _shift_rows(u, up_row, un_row)
        w = w_ref[...]
        conv = um1 * w[0:1, :] + u * w[1:2, :] + up1 * w[2:3, :]
        y_ref[...] = (bg_ref[...] * conv).astype(BF16)

    return pl.pallas_call(
        body, name=name, grid=(nt,),
        in_specs=[_row_spec(D, 0), _row_spec(D, 1), _row_spec(D, 2), cg_p, vc_p, cg_n, vc_n, _vec_spec(3, D)],
        out_specs=_row_spec(D),
        out_shape=jax.ShapeDtypeStruct((T, D), BF16),
        compiler_params=_params(("parallel",)),
    )(P, P, P, P, P, P, P, conv_w)


def _conv_bwd(P, dy, conv_w, D, name):
    T = P.shape[0]
    nt = T // ROW
    bg_p, bg_n = _halo_specs(D, 0, nt)
    cg_p, cg_n = _halo_specs(D, 1, nt)
    vc_p, vc_n = _halo_specs(D, 2, nt)
    dy_p, dy_n = _halo_specs(D, 0, nt)

    def body(bg_ref, cg_ref, vc_ref, dy_ref, bgp_ref, cgp_ref, vcp_ref, dyp_ref,
             bgn_ref, cgn_ref, vcn_ref, dyn_ref, w_ref, o_ref, acc_ref):
        _acc_init(acc_ref)
        i = pl.program_id(0)
        lat = (i > 0).astype(F32)
        has_prev = (i != 1).astype(F32)
        has_next = (i != nt - 1).astype(F32)
        last = HALO - 1
        bg = bg_ref[...]
        cg = cg_ref[...]
        vc = vc_ref[...]
        dyv = dy_ref[...] * lat
        u = cg * vc
        up_row = cgp_ref[last:HALO, :] * vcp_ref[last:HALO, :] * has_prev
        un_row = cgn_ref[0:1, :] * vcn_ref[0:1, :] * has_next
        um1, up1 = _shift_rows(u, up_row, un_row)
        w = w_ref[...]
        conv = um1 * w[0:1, :] + u * w[1:2, :] + up1 * w[2:3, :]
        dc = dyv * bg
        dcp_row = dyp_ref[last:HALO, :] * bgp_ref[last:HALO, :] * has_prev
        dcn_row = dyn_ref[0:1, :] * bgn_ref[0:1, :] * has_next
        dcm1, dcp1 = _shift_rows(dc, dcp_row, dcn_row)
        du = dcp1 * w[0:1, :] + dc * w[1:2, :] + dcm1 * w[2:3, :]
        o_ref[:, 0:D] = (dyv * conv).astype(BF16)
        o_ref[:, D:2 * D] = (du * vc * lat).astype(BF16)
        o_ref[:, 2 * D:3 * D] = (du * cg * lat).astype(BF16)
        _acc_add(acc_ref, 0, dc * um1)
        _acc_add(acc_ref, 1, dc * u)
        _acc_add(acc_ref, 2, dc * up1)

    return pl.pallas_call(
        body, name=name, grid=(nt,),
        in_specs=[_row_spec(D, 0), _row_spec(D, 1), _row_spec(D, 2), _row_spec(D, 0),
                  bg_p, cg_p, vc_p, dy_p, bg_n, cg_n, vc_n, dy_n, _vec_spec(3, D)],
        out_specs=[_row_spec(3 * D), _acc_spec(D)],
        out_shape=[jax.ShapeDtypeStruct((T, 3 * D), BF16), jax.ShapeDtypeStruct((2, ACC_ROWS, D), F32)],
        compiler_params=_params(("arbitrary",)),
    )(P, P, P, dy, P, P, P, dy, P, P, P, dy, conv_w)


def _rope_tables(ctx_len, seq):
    n_freq = HEAD_DIM // 4
    rows = seq // GRID_W
    row = jnp.repeat(jnp.arange(rows), GRID_W).astype(F32)
    col = jnp.tile(jnp.arange(GRID_W), rows).astype(F32)
    inv = ROPE_THETA ** (-jnp.arange(n_freq, dtype=F32) / n_freq)
    ar = row[:, None] * inv
    ac = col[:, None] * inv
    cos_t = jnp.concatenate([jnp.cos(ar), jnp.cos(ar), jnp.cos(ac), jnp.cos(ac)], axis=1)
    sin_t = jnp.concatenate([-jnp.sin(ar), jnp.sin(ar), -jnp.sin(ac), jnp.sin(ac)], axis=1)
    cos_t = jnp.concatenate([jnp.ones((ctx_len, HEAD_DIM), F32), cos_t], axis=0)
    sin_t = jnp.concatenate([jnp.zeros((ctx_len, HEAD_DIM), F32), sin_t], axis=0)
    return cos_t, sin_t


def _swap_halves(y):
    lanes = lax.broadcasted_iota(jnp.int32, y.shape, 1)
    first = (lanes % 64) < 32
    return jnp.where(first, pltpu.roll(y, HEAD_DIM - 32, 1), pltpu.roll(y, 32, 1))


def _qk_fwd(P, gq, gk, cos_t, sin_t, D, name):
    T = P.shape[0]
    QW = N_Q_HEADS * HEAD_DIM
    KW = N_KV_HEADS * HEAD_DIM
    q_col = (3 * D) // QW
    k_col = (3 * D + QW) // KW
    v_col = k_col + 1

    def body(q_ref, k_ref, v_ref, gq_ref, gk_ref, c_ref, s_ref, qo_ref, ko_ref, vo_ref):
        c = c_ref[...]
        s = s_ref[...]

        def head(x, g):
            inv = lax.rsqrt(jnp.mean(x * x, axis=-1, keepdims=True) + EPS)
            y = (x * inv) * g
            return y * c + _swap_halves(y) * s

        for h in range(N_Q_HEADS):
            sl = slice(h * HEAD_DIM, (h + 1) * HEAD_DIM)
            qo_ref[:, sl] = head(q_ref[:, sl], gq_ref[...]).astype(BF16)
        for h in range(N_KV_HEADS):
            sl = slice(h * HEAD_DIM, (h + 1) * HEAD_DIM)
            ko_ref[:, sl] = head(k_ref[:, sl], gk_ref[...]).astype(BF16)
        vo_ref[...] = v_ref[...].astype(BF16)

    return pl.pallas_call(
        body, name=name, grid=(T // ROW,),
        in_specs=[_row_spec(QW, q_col), _row_spec(KW, k_col), _row_spec(KW, v_col),
                  _vec_spec(1, HEAD_DIM), _vec_spec(1, HEAD_DIM), _row_spec(HEAD_DIM), _row_spec(HEAD_DIM)],
        out_specs=[_row_spec(QW), _row_spec(KW), _row_spec(KW)],
        out_shape=[jax.ShapeDtypeStruct((T, QW), BF16), jax.ShapeDtypeStruct((T, KW), BF16),
                   jax.ShapeDtypeStruct((T, KW), BF16)],
        compiler_params=_params(("parallel",)),
    )(P, P, P, gq, gk, cos_t, sin_t)


def _qk_bwd(P, dq, dk, dv, gq, gk, cos_t, sin_t, D, name):
    T = P.shape[0]
    QW = N_Q_HEADS * HEAD_DIM
    KW = N_KV_HEADS * HEAD_DIM
    q_col = (3 * D) // QW
    k_col = (3 * D + QW) // KW

    def body(q_ref, k_ref, dq_ref, dk_ref, dv_ref, gq_ref, gk_ref, c_ref, s_ref, o_ref, acc_ref):
        _acc_init(acc_ref)
        c = c_ref[...]
        s = s_ref[...]

        def head(x, d, g):
            dyv = d * c + _swap_halves(d * s)
            inv = lax.rsqrt(jnp.mean(x * x, axis=-1, keepdims=True) + EPS)
            xn = x * inv
            dxn = dyv * g
            dx = inv * (dxn - xn * jnp.mean(dxn * xn, axis=-1, keepdims=True))
            return dx, jnp.sum(dyv * xn, axis=0, keepdims=True)

        dgq = jnp.zeros((1, HEAD_DIM), F32)
        for h in range(N_Q_HEADS):
            sl = slice(h * HEAD_DIM, (h + 1) * HEAD_DIM)
            dx, dg = head(q_ref[:, sl], dq_ref[:, sl], gq_ref[...])
            o_ref[:, sl] = dx.astype(BF16)
            dgq = dgq + dg
        dgk = jnp.zeros((1, HEAD_DIM), F32)
        for h in range(N_KV_HEADS):
            sl = slice(h * HEAD_DIM, (h + 1) * HEAD_DIM)
            dx, dg = head(k_ref[:, sl], dk_ref[:, sl], gk_ref[...])
            o_ref[:, QW + h * HEAD_DIM:QW + (h + 1) * HEAD_DIM] = dx.astype(BF16)
            dgk = dgk + dg
        o_ref[:, QW + KW:QW + 2 * KW] = dv_ref[...].astype(BF16)
        acc_ref[0, 0:1, 0:HEAD_DIM] += dgq
        acc_ref[0, 1:2, 0:HEAD_DIM] += dgk

    return pl.pallas_call(
        body, name=name, grid=(T // ROW,),
        in_specs=[_row_spec(QW, q_col), _row_spec(KW, k_col), _row_spec(QW), _row_spec(KW), _row_spec(KW),
                  _vec_spec(1, HEAD_DIM), _vec_spec(1, HEAD_DIM), _row_spec(HEAD_DIM), _row_spec(HEAD_DIM)],
        out_specs=[_row_spec(QW + 2 * KW), _acc_spec(D)],
        out_shape=[jax.ShapeDtypeStruct((T, QW + 2 * KW), BF16), jax.ShapeDtypeStruct((2, ACC_ROWS, D), F32)],
        compiler_params=_params(("arbitrary",)),
    )(P, P, dq, dk, dv, gq, gk, cos_t, sin_t)


def _to_row(col, n):
    return jnp.transpose(jnp.broadcast_to(col, (n, HEAD_DIM)))[0:1, :]


def _flash_fwd(q, k, v, name, tq=ROW, tk=None):
    T = q.shape[0]
    tk = tk or _pick(T, (512, 256))
    nk = T // tk
    GW = GROUP * HEAD_DIM

    def body(q_ref, k_ref, v_ref, o_ref, lse_ref, qs_ref, m_ref, l_ref, acc_ref):
        ki = pl.program_id(2)

        @pl.when(ki == 0)
        def _():
            for g in range(GROUP):
                qs_ref[g * tq:(g + 1) * tq, :] = q_ref[:, g * HEAD_DIM:(g + 1) * HEAD_DIM]
            m_ref[...] = jnp.full(m_ref.shape, -jnp.inf, F32)
            l_ref[...] = jnp.zeros(l_ref.shape, F32)
            acc_ref[...] = jnp.zeros(acc_ref.shape, F32)

        s = lax.dot_general(qs_ref[...], k_ref[...], (((1,), (1,)), ((), ())),
                            preferred_element_type=F32) * ATTN_SCALE
        m_prev = m_ref[...]
        m_new = jnp.maximum(m_prev, jnp.max(s, axis=1, keepdims=True))
        alpha = jnp.exp(m_prev - m_new)
        p = jnp.exp(s - m_new)
        l_ref[...] = alpha * l_ref[...] + jnp.sum(p, axis=1, keepdims=True)
        acc_ref[...] = alpha * acc_ref[...] + jnp.dot(p.astype(BF16), v_ref[...], preferred_element_type=F32)
        m_ref[...] = m_new

        @pl.when(ki == nk - 1)
        def _():
            out = acc_ref[...] / l_ref[...]
            lse = m_ref[...] + jnp.log(l_ref[...])
            for g in range(GROUP):
                o_ref[:, g * HEAD_DIM:(g + 1) * HEAD_DIM] = out[g * tq:(g + 1) * tq, :]
                lse_ref[0, g:g + 1, :] = _to_row(lse[g * tq:(g + 1) * tq, :], tq)

    return pl.pallas_call(
        body, name=name, grid=(N_KV_HEADS, T // tq, nk),
        in_specs=[pl.BlockSpec((tq, GW), lambda h, i, j: (i, h)),
                  pl.BlockSpec((tk, HEAD_DIM), lambda h, i, j: (j, h)),
                  pl.BlockSpec((tk, HEAD_DIM), lambda h, i, j: (j, h))],
        out_specs=[pl.BlockSpec((tq, GW), lambda h, i, j: (i, h)),
                   pl.BlockSpec((1, GROUP, tq), lambda h, i, j: (h, 0, i))],
        out_shape=[jax.ShapeDtypeStruct((T, N_Q_HEADS * HEAD_DIM), F32),
                   jax.ShapeDtypeStruct((N_KV_HEADS, GROUP, T), F32)],
        scratch_shapes=[pltpu.VMEM((GROUP * tq, HEAD_DIM), BF16), pltpu.VMEM((GROUP * tq, 1), F32),
                        pltpu.VMEM((GROUP * tq, 1), F32), pltpu.VMEM((GROUP * tq, HEAD_DIM), F32)],
        compiler_params=_params(("parallel", "parallel", "arbitrary")),
    )(q, k, v)


def _attn_delta(do, o, name):
    T, QW = do.shape

    def body(do_ref, o_ref, dob_ref, dl_ref):
        dov = do_ref[...]
        dob_ref[...] = dov.astype(BF16)
        prod = dov * o_ref[...]
        for h in range(N_Q_HEADS):
            d = jnp.sum(prod[:, h * HEAD_DIM:(h + 1) * HEAD_DIM], axis=1, keepdims=True)
            dl_ref[h // GROUP, (h % GROUP):(h % GROUP) + 1, :] = _to_row(d, ROW)

    return pl.pallas_call(
        body, name=name, grid=(T // ROW,),
        in_specs=[_row_spec(QW), _row_spec(QW)],
        out_specs=[_row_spec(QW), pl.BlockSpec((N_KV_HEADS, GROUP, ROW), lambda i: (0, 0, i))],
        out_shape=[jax.ShapeDtypeStruct((T, QW), BF16), jax.ShapeDtypeStruct((N_KV_HEADS, GROUP, T), F32)],
        compiler_params=_params(("parallel",)),
    )(do, o)


def _flash_bwd(q, k, v, do, lse, delta, name, tq=ROW, tk=None):
    T = q.shape[0]
    tk = tk or _pick(T, (512, 256))
    nk = T // tk
    GW = GROUP * HEAD_DIM
    nt = (((1,), (1,)), ((), ()))

    def body(q_ref, do_ref, k_ref, v_ref, lse_ref, dl_ref, dq_ref, dk_ref, dv_ref, qs_ref, dos_ref, dqt_ref):
        qi = pl.program_id(1)
        ki = pl.program_id(2)

        @pl.when(ki == 0)
        def _():
            for g in range(GROUP):
                qs_ref[g * tq:(g + 1) * tq, :] = q_ref[:, g * HEAD_DIM:(g + 1) * HEAD_DIM]
                dos_ref[g * tq:(g + 1) * tq, :] = do_ref[:, g * HEAD_DIM:(g + 1) * HEAD_DIM]
            dqt_ref[...] = jnp.zeros(dqt_ref.shape, F32)

        lse_row = jnp.concatenate([lse_ref[0, g:g + 1, :] for g in range(GROUP)], axis=1)
        dl_row = jnp.concatenate([dl_ref[0, g:g + 1, :] for g in range(GROUP)], axis=1)
        kk = k_ref[...]
        qs = qs_ref[...]
        dos = dos_ref[...]
        st = lax.dot_general(kk, qs, nt, preferred_element_type=F32) * ATTN_SCALE
        pt = jnp.exp(st - lse_row)
        dpt = lax.dot_general(v_ref[...], dos, nt, preferred_element_type=F32)
        dst = ((pt * (dpt - dl_row)) * ATTN_SCALE).astype(BF16)
        dv_c = jnp.dot(pt.astype(BF16), dos, preferred_element_type=F32)
        dk_c = jnp.dot(dst, qs, preferred_element_type=F32)
        rows = pl.ds(pl.multiple_of(ki * tk, tk), tk)

        @pl.when(qi == 0)
        def _():
            dk_ref[rows, :] = dk_c
            dv_ref[rows, :] = dv_c

        @pl.when(qi > 0)
        def _():
            dk_ref[rows, :] += dk_c
            dv_ref[rows, :] += dv_c

        dqt_ref[...] += lax.dot_general(kk, dst, (((0,), (0,)), ((), ())), preferred_element_type=F32)

        @pl.when(ki == nk - 1)
        def _():
            dqv = jnp.transpose(dqt_ref[...])
            for g in range(GROUP):
                dq_ref[:, g * HEAD_DIM:(g + 1) * HEAD_DIM] = dqv[g * tq:(g + 1) * tq, :]

    return pl.pallas_call(
        body, name=name, grid=(N_KV_HEADS, T // tq, nk),
        in_specs=[pl.BlockSpec((tq, GW), lambda h, i, j: (i, h)),
                  pl.BlockSpec((tq, GW), lambda h, i, j: (i, h)),
                  pl.BlockSpec((tk, HEAD_DIM), lambda h, i, j: (j, h)),
                  pl.BlockSpec((tk, HEAD_DIM), lambda h, i, j: (j, h)),
                  pl.BlockSpec((1, GROUP, tq), lambda h, i, j: (h, 0, i)),
                  pl.BlockSpec((1, GROUP, tq), lambda h, i, j: (h, 0, i))],
        out_specs=[pl.BlockSpec((tq, GW), lambda h, i, j: (i, h)),
                   pl.BlockSpec((T, HEAD_DIM), lambda h, i, j: (0, h)),
                   pl.BlockSpec((T, HEAD_DIM), lambda h, i, j: (0, h))],
        out_shape=[jax.ShapeDtypeStruct((T, N_Q_HEADS * HEAD_DIM), F32),
                   jax.ShapeDtypeStruct((T, N_KV_HEADS * HEAD_DIM), F32),
                   jax.ShapeDtypeStruct((T, N_KV_HEADS * HEAD_DIM), F32)],
        scratch_shapes=[pltpu.VMEM((GROUP * tq, HEAD_DIM), BF16), pltpu.VMEM((GROUP * tq, HEAD_DIM), BF16),
                        pltpu.VMEM((HEAD_DIM, GROUP * tq), F32)],
        compiler_params=_params(("arbitrary", "arbitrary", "arbitrary")),
    )(q, do, k, v, lse, delta)


def _gate_specs(D):
    w = D // 2
    first = (3 * D + (N_Q_HEADS + 2 * N_KV_HEADS) * HEAD_DIM) // w
    return [pl.BlockSpec((ROW, w), lambda i, c=first + j: (i, c)) for j in range(4)]


def _merge_fwd(a1, a2, P, D, name):
    T = a1.shape[0]
    w = D // 2

    def body(a1_ref, a2_ref, g0, g1, g2, g3, z_ref):
        for j, (gc, ga) in enumerate(((g0, g2), (g1, g3))):
            sl = slice(j * w, (j + 1) * w)
            z = jax.nn.sigmoid(gc[...]) * a1_ref[:, sl] + jax.nn.sigmoid(ga[...]) * a2_ref[:, sl]
            z_ref[:, sl] = z.astype(BF16)

    return pl.pallas_call(
        body, name=name, grid=(T // ROW,),
        in_specs=[_row_spec(D), _row_spec(D)] + _gate_specs(D),
        out_specs=_row_spec(D), out_shape=jax.ShapeDtypeStruct((T, D), BF16),
        compiler_params=_params(("parallel",)),
    )(a1, a2, P, P, P, P)


def _merge_bwd(dz, a1, a2, P, D, name):
    T = a1.shape[0]
    w = D // 2

    def body(dz_ref, a1_ref, a2_ref, g0, g1, g2, g3, d1_ref, d2_ref, dg_ref):
        for j, (gc, ga) in enumerate(((g0, g2), (g1, g3))):
            sl = slice(j * w, (j + 1) * w)
            dz = dz_ref[:, sl]
            sc = jax.nn.sigmoid(gc[...])
            sa = jax.nn.sigmoid(ga[...])
            d1_ref[:, sl] = (dz * sc).astype(BF16)
            d2_ref[:, sl] = (dz * sa).astype(BF16)
            dg_ref[:, j * w:(j + 1) * w] = (dz * a1_ref[:, sl] * (sc * (1.0 - sc))).astype(BF16)
            dg_ref[:, D + j * w:D + (j + 1) * w] = (dz * a2_ref[:, sl] * (sa * (1.0 - sa))).astype(BF16)

    return pl.pallas_call(
        body, name=name, grid=(T // ROW,),
        in_specs=[_row_spec(D), _row_spec(D), _row_spec(D)] + _gate_specs(D),
        out_specs=[_row_spec(D), _row_spec(D), _row_spec(2 * D)],
        out_shape=[jax.ShapeDtypeStruct((T, D), BF16), jax.ShapeDtypeStruct((T, D), BF16),
                   jax.ShapeDtypeStruct((T, 2 * D), BF16)],
        compiler_params=_params(("parallel",)),
    )(dz, a1, a2, P, P, P, P)


def _adamw_math(w, g, m, v):
    m = ADAM_B1 * m + (1.0 - ADAM_B1) * g
    v = ADAM_B2 * v + (1.0 - ADAM_B2) * (g * g)
    m_hat = m / (1.0 - ADAM_B1 ** ADAM_STEP)
    v_hat = v / (1.0 - ADAM_B2 ** ADAM_STEP)
    delta = -ADAM_LR * (m_hat / (jnp.sqrt(v_hat) + ADAM_EPS) + ADAM_WD * w)
    return delta, m, v


def _adamw(w, g, m, v, name):
    R, C = w.shape
    tr = _pick(R, tuple(t for t in (256, 128, 64, 32, 16, 8) if t * C * 4 <= ADAMW_BLOCK_BYTES))

    def body(w_ref, g_ref, m_ref, v_ref, d_ref, mo_ref, vo_ref):
        d, mn, vn = _adamw_math(w_ref[...], g_ref[...], m_ref[...], v_ref[...])
        d_ref[...] = d
        mo_ref[...] = mn
        vo_ref[...] = vn

    spec = pl.BlockSpec((tr, C), lambda i: (i, 0))
    return pl.pallas_call(
        body, name=name, grid=(R // tr,),
        in_specs=[spec] * 4, out_specs=[spec] * 3,
        out_shape=[jax.ShapeDtypeStruct((R, C), F32)] * 3,
        compiler_params=_params(("parallel",)),
    )(w, g, m, v)


def _local_step(xcat, target, mods, norm_g, final_g, gq, gk, conv_w, wts, ctx_len):
    T, D = xcat.shape
    w1i, w1o, wi, wbc, wba, wo, w2i, w2o = wts
    g1, g2, g3 = norm_g
    cos_t, sin_t = _rope_tables(ctx_len, T - ctx_len)

    _, h1 = _resid_rmsmod_fwd(xcat, None, mods, g1, None, 0, 1, "f_norm1")
    u1 = _matmul(h1, w1i, "nt", F32, "f_ffn1_in")
    s1 = _swiglu_fwd(u1, "f_ffn1_act")
    f1 = _matmul(s1, w1o, "nn", F32, "f_ffn1_out")
    x1, h2 = _resid_rmsmod_fwd(xcat, f1, mods, g2, (2, 0.5), 3, 4, "f_norm2")
    P = _matmul(h2, wi, "nt", F32, "f_mix_in")
    yc = _conv_fwd(P, conv_w, D, "f_conv")
    qn, kn, vb = _qk_fwd(P, gq, gk, cos_t, sin_t, D, "f_qk")
    o, lse = _flash_fwd(qn, kn, vb, "f_attn")
    a1 = _matmul(yc, wbc, "nn", F32, "f_branch_conv")
    a2 = _matmul(o, wba, "nn", F32, "f_branch_attn")
    z = _merge_fwd(a1, a2, P, D, "f_merge")
    mo = _matmul(z, wo, "nn", F32, "f_mix_out")
    x2, h3 = _resid_rmsmod_fwd(x1, mo, mods, g3, (5, 1.0), 6, 7, "f_norm3")
    u2 = _matmul(h3, w2i, "nt", F32, "f_ffn2_in")
    s2 = _swiglu_fwd(u2, "f_ffn2_act")
    f2 = _matmul(s2, w2o, "nn", F32, "f_ffn2_out")
    dx3, df2, acc_head = _loss_head(x2, f2, mods, final_g, target, "loss_head")

    ds2 = _matmul(df2, w2o, "nt", F32, "b_ffn2_out_dx")
    du2 = _swiglu_bwd(u2, ds2, "b_ffn2_act")
    g_w2o = _matmul(s2, df2, "tn", BF16, "b_ffn2_out_dw")
    g_w2i = _matmul(du2, h3, "tn", BF16, "b_ffn2_in_dw")
    dh3 = _matmul(du2, w2i, "nn", F32, "b_ffn2_in_dx")
    dx2, dmo, acc_n3 = _rmsmod_bwd(x2, dh3, dx3, mods, g3, 6, 7, (5, 1.0), mo, "b_norm3")

    dz = _matmul(dmo, wo, "nt", F32, "b_mix_out_dx")
    g_wo = _matmul(z, dmo, "tn", BF16, "b_mix_out_dw")
    da1, da2, dgt = _merge_bwd(dz, a1, a2, P, D, "b_merge")
    dyc = _matmul(da1, wbc, "nt", F32, "b_branch_conv_dx")
    do = _matmul(da2, wba, "nt", F32, "b_branch_attn_dx")
    g_wbc = _matmul(yc, da1, "tn", BF16, "b_branch_conv_dw")
    g_wba = _matmul(o, da2, "tn", BF16, "b_branch_attn_dw")
    dob, delta = _attn_delta(do, o, "b_attn_delta")
    dq, dk, dv = _flash_bwd(qn, kn, vb, dob, lse, delta, "b_attn")
    dqkv, acc_qk = _qk_bwd(P, dq, dk, dv, gq, gk, cos_t, sin_t, D, "b_qk")
    dconv, acc_conv = _conv_bwd(P, dyc, conv_w, D, "b_conv")
    dP = jnp.concatenate([dconv, dqkv, dgt], axis=1)
    dh2 = _matmul(dP, wi, "nn", F32, "b_mix_in_dx")
    g_wi = _matmul(dP, h2, "tn", BF16, "b_mix_in_dw")
    dx1, df1, acc_n2 = _rmsmod_bwd(x1, dh2, dx2, mods, g2, 3, 4, (2, 0.5), f1, "b_norm2")

    ds1 = _matmul(df1, w1o, "nt", F32, "b_ffn1_out_dx")
    du1 = _swiglu_bwd(u1, ds1, "b_ffn1_act")
    g_w1o = _matmul(s1, df1, "tn", BF16, "b_ffn1_out_dw")
    g_w1i = _matmul(du1, h1, "tn", BF16, "b_ffn1_in_dw")
    dh1 = _matmul(du1, w1i, "nn", F32, "b_ffn1_in_dx")
    grad_x, _, acc_n1 = _rmsmod_bwd(xcat, dh1, dx1, mods, g1, 0, 1, None, None, "b_norm1", skip_first_tile=True)

    grads = (g_w1i, g_w1o, g_wi, g_wbc, g_wba, g_wo, g_w2i, g_w2o)
    accs = (acc_head, acc_n3, acc_n2, acc_n1, acc_conv, acc_qk)
    return grad_x, grads, accs


def _place():
    return lax.axis_index("x"), lax.axis_index("y"), lax.axis_index("c")


def _other_chips(x, y):
    return [(1 - x, y), (x, 1 - y), (1 - x, 1 - y)]


def _allgather8(v, name):
    R, N = v.shape

    def body(v_ref, out_ref, send_sems, recv_sems, local_sem):
        x, y, c = _place()
        me, sibling = (x, y, c), (x, y, 1 - c)
        chips = _other_chips(x, y)

        def blk(px, py, pc):
            return out_ref.at[4 * px + 2 * py + pc]

        def copy(k, block, to, src=None):
            return pltpu.make_async_remote_copy(
                src_ref=blk(*block) if src is None else src, dst_ref=blk(*block),
                send_sem=send_sems.at[k], recv_sem=recv_sems.at[k], device_id=to, device_id_type=MESH)

        mine = pltpu.make_async_copy(v_ref, blk(*me), local_sem)
        mine.start()
        first = [copy(0, me, sibling, src=v_ref)]
        first += [copy(1 + j, me, (*chip, c), src=v_ref) for j, chip in enumerate(chips)]
        for cp in first:
            cp.start()
        passed = [copy(4 + j, (*chip, c), sibling) for j, chip in enumerate(chips)]
        for j, chip in enumerate(chips):
            copy(1 + j, (*chip, c), me).wait_recv()
            passed[j].start()
        copy(0, sibling, me).wait_recv()
        for j, chip in enumerate(chips):
            copy(4 + j, (*chip, 1 - c), me).wait_recv()
        for cp in first + passed:
            cp.wait_send()
        mine.wait()

    return pl.pallas_call(
        body, name=name,
        out_shape=jax.ShapeDtypeStruct((N_DEV, R, N), v.dtype),
        in_specs=[pl.BlockSpec(memory_space=pltpu.VMEM)],
        out_specs=pl.BlockSpec(memory_space=pltpu.VMEM),
        scratch_shapes=[pltpu.SemaphoreType.DMA((7,)), pltpu.SemaphoreType.DMA((7,)), pltpu.SemaphoreType.DMA],
        compiler_params=pltpu.CompilerParams(vmem_limit_bytes=VMEM_LIMIT),
    )(v)


def _any_specs(n):
    return [pl.BlockSpec(memory_space=pl.ANY)] * n


def _weights_allgather(shards, name):
    n = len(shards)

    def body(*refs):
        sh, full = refs[:n], refs[n:2 * n]
        send_sems, recv_sems, local_sems = refs[2 * n:]
        x, y, c = _place()
        sibling = (x, y, 1 - c)
        chips = _other_chips(x, y)

        def piece(t, px, py, h):
            rs = shards[t].shape[0]
            return full[t].at[pl.ds((2 * px + py) * rs + h * (rs // 2), rs // 2), :]

        def copy(k, t, block, to, src=None):
            return pltpu.make_async_remote_copy(
                src_ref=piece(t, *block) if src is None else src, dst_ref=piece(t, *block),
                send_sem=send_sems.at[k], recv_sem=recv_sems.at[k], device_id=to, device_id_type=MESH)

        own = []
        for t in range(n):
            rs = shards[t].shape[0]
            cp = pltpu.make_async_copy(sh[t], full[t].at[pl.ds((2 * x + y) * rs, rs), :], local_sems.at[t])
            cp.start()
            own.append(cp)
        first = []
        for t in range(n):
            half = shards[t].shape[0] // 2
            for j, chip in enumerate(chips):
                cp = copy(3 * t + j, t, (x, y, c), (*chip, c), src=sh[t].at[pl.ds(c * half, half), :])
                cp.start()
                first.append(cp)
        passed = []
        for t in range(n):
            for j, chip in enumerate(chips):
                copy(3 * t + j, t, (*chip, c), (x, y, c)).wait_recv()
                cp = copy(3 * n + 3 * t + j, t, (*chip, c), sibling)
                cp.start()
                passed.append(cp)
        for t in range(n):
            for j, chip in enumerate(chips):
                copy(3 * n + 3 * t + j, t, (*chip, 1 - c), (x, y, c)).wait_recv()
        for cp in first + passed:
            cp.wait_send()
        for cp in own:
            cp.wait()

    return pl.pallas_call(
        body, name=name,
        out_shape=[jax.ShapeDtypeStruct((N_CHIPS * s.shape[0], s.shape[1]), s.dtype) for s in shards],
        in_specs=_any_specs(n), out_specs=_any_specs(n),
        scratch_shapes=[pltpu.SemaphoreType.DMA((6 * n,)), pltpu.SemaphoreType.DMA((6 * n,)),
                        pltpu.SemaphoreType.DMA((n,))],
    )(*shards)


def _pair_exchange(grads, name):
    n = len(grads)

    def body(*refs):
        g, land = refs[:n], refs[n:2 * n]
        send_sems, recv_sems = refs[2 * n:]
        x, y, c = _place()
        sibling = (x, y, 1 - c)
        copies = []
        for t in range(n):
            half = grads[t].shape[0] // (2 * N_CHIPS)
            for s in range(N_CHIPS):
                cp = pltpu.make_async_remote_copy(
                    src_ref=g[t].at[pl.ds((2 * s + 1 - c) * half, half), :], dst_ref=land[t].at[s],
                    send_sem=send_sems.at[N_CHIPS * t + s], recv_sem=recv_sems.at[N_CHIPS * t + s],
                    device_id=sibling, device_id_type=MESH)
                cp.start()
                copies.append(cp)
        for cp in copies:
            cp.wait_recv()
        for cp in copies:
            cp.wait_send()

    return pl.pallas_call(
        body, name=name,
        out_shape=[jax.ShapeDtypeStruct((N_CHIPS, a.shape[0] // (2 * N_CHIPS), a.shape[1]), a.dtype) for a in grads],
        in_specs=_any_specs(n), out_specs=_any_specs(n),
        scratch_shapes=[pltpu.SemaphoreType.DMA((N_CHIPS * n,)), pltpu.SemaphoreType.DMA((N_CHIPS * n,))],
    )(*grads)


def _pair_sum(g, landed, idx, name):
    _, half, D = landed.shape
    g4 = g.reshape(N_CHIPS, 2, half, D)
    tr = _pick(half, (416, 352, 128))

    def body(idx_ref, g_ref, l_ref, o_ref):
        o_ref[...] = (g_ref[0].astype(F32) + l_ref[...].astype(F32)).astype(BF16)

    return pl.pallas_call(
        body, name=name,
        grid_spec=pltpu.PrefetchScalarGridSpec(
            num_scalar_prefetch=1, grid=(N_CHIPS, half // tr),
            in_specs=[pl.BlockSpec((1, 1, tr, D), lambda s, i, idx: (idx[1 + s], idx[0], i, 0)),
                      pl.BlockSpec((1, tr, D), lambda s, i, idx: (idx[1 + s], i, 0))],
            out_specs=pl.BlockSpec((1, tr, D), lambda s, i, idx: (s, i, 0))),
        out_shape=jax.ShapeDtypeStruct((N_CHIPS, half, D), BF16),
        compiler_params=_params(("arbitrary", "arbitrary")),
    )(idx, g4, landed)


def _chip_exchange(sums, name):
    n = len(sums)

    def body(*refs):
        ps, land = refs[:n], refs[n:2 * n]
        send_sems, recv_sems = refs[2 * n:]
        x, y, c = _place()
        copies = []
        for t in range(n):
            for j, chip in enumerate(_other_chips(x, y)):
                cp = pltpu.make_async_remote_copy(
                    src_ref=ps[t].at[1 + j], dst_ref=land[t].at[j],
                    send_sem=send_sems.at[3 * t + j], recv_sem=recv_sems.at[3 * t + j],
                    device_id=(*chip, c), device_id_type=MESH)
                cp.start()
                copies.append(cp)
        for cp in copies:
            cp.wait_recv()
        for cp in copies:
            cp.wait_send()

    return pl.pallas_call(
        body, name=name,
        out_shape=[jax.ShapeDtypeStruct((3,) + a.shape[1:], a.dtype) for a in sums],
        in_specs=_any_specs(n), out_specs=_any_specs(n),
        scratch_shapes=[pltpu.SemaphoreType.DMA((3 * n,)), pltpu.SemaphoreType.DMA((3 * n,))],
    )(*sums)


def _chip_sum(ps, landed, name):
    _, half, D = ps.shape
    tr = _pick(half, (416, 352, 128))

    def body(p_ref, l_ref, o_ref):
        acc = p_ref[0].astype(F32)
        for j in range(3):
            acc = acc + l_ref[j].astype(F32)
        o_ref[...] = acc

    return pl.pallas_call(
        body, name=name, grid=(half // tr,),
        in_specs=[pl.BlockSpec((1, tr, D), lambda i: (0, i, 0)), pl.BlockSpec((3, tr, D), lambda i: (0, i, 0))],
        out_specs=pl.BlockSpec((tr, D), lambda i: (i, 0)),
        out_shape=jax.ShapeDtypeStruct((half, D), F32),
        compiler_params=_params(("parallel",)),
    )(ps, landed)


def _pair_allgather(halves, name):
    n = len(halves)

    def body(*refs):
        hv, full = refs[:n], refs[n:2 * n]
        send_sems, recv_sems, local_sems = refs[2 * n:]
        x, y, c = _place()
        sibling = (x, y, 1 - c)
        copies, own = [], []
        for t in range(n):
            half = halves[t].shape[0]
            mine = full[t].at[pl.ds(c * half, half), :]
            lc = pltpu.make_async_copy(hv[t], mine, local_sems.at[t])
            lc.start()
            own.append(lc)
            cp = pltpu.make_async_remote_copy(src_ref=hv[t], dst_ref=mine, send_sem=send_sems.at[t],
                                              recv_sem=recv_sems.at[t], device_id=sibling, device_id_type=MESH)
            cp.start()
            copies.append(cp)
        for t in range(n):
            half = halves[t].shape[0]
            theirs = full[t].at[pl.ds((1 - c) * half, half), :]
            pltpu.make_async_remote_copy(src_ref=theirs, dst_ref=theirs, send_sem=send_sems.at[t],
                                         recv_sem=recv_sems.at[t], device_id=sibling, device_id_type=MESH).wait_recv()
        for cp in copies:
            cp.wait_send()
        for lc in own:
            lc.wait()

    return pl.pallas_call(
        body, name=name,
        out_shape=[jax.ShapeDtypeStruct((2 * a.shape[0], a.shape[1]), a.dtype) for a in halves],
        in_specs=_any_specs(n), out_specs=_any_specs(n),
        scratch_shapes=[pltpu.SemaphoreType.DMA((n,)), pltpu.SemaphoreType.DMA((n,)), pltpu.SemaphoreType.DMA((n,))],
    )(*halves)


def _reduce_scatter(grads, idx):
    landed = _pair_exchange(grads, "rs_pair_exchange")
    sums = [_pair_sum(g, l, idx, f"rs_pair_sum_{t}") for t, (g, l) in enumerate(zip(grads, landed))]
    landed2 = _chip_exchange(sums, "rs_chip_exchange")
    halves = [_chip_sum(p, l, f"rs_chip_sum_{t}") for t, (p, l) in enumerate(zip(sums, landed2))]
    return _pair_allgather(halves, "rs_pair_allgather")


N_MOD = 9
PACK_HEAD, PACK_N3, PACK_N2, PACK_N1, PACK_CONV, PACK_QK = 0, 16, 32, 48, 64, 80
PACK_ROWS = 96
MOD_SRC = ((PACK_N1, 0), (PACK_N1, 1), (PACK_N2, 3), (PACK_N2, 0), (PACK_N2, 1),
           (PACK_N3, 3), (PACK_N3, 0), (PACK_N3, 1), (PACK_HEAD, 2))
CTX_ROW = 8


def _silu(v):
    return v * jax.nn.sigmoid(v)


def _whole(n):
    return [pl.BlockSpec(memory_space=pltpu.VMEM)] * n


def _mod_rows(cin, w_sh, b_sh, name):
    def body(c_ref, w_ref, b_ref, o_ref):
        a = _silu(c_ref[...]).astype(BF16)
        o_ref[...] = jnp.dot(a, w_ref[...].astype(BF16), preferred_element_type=F32) + b_ref[...]

    return pl.pallas_call(
        body, name=name, out_shape=jax.ShapeDtypeStruct((cin.shape[0], w_sh.shape[1]), F32),
        in_specs=_whole(3), out_specs=pl.BlockSpec(memory_space=pltpu.VMEM),
        compiler_params=pltpu.CompilerParams(vmem_limit_bytes=VMEM_LIMIT),
    )(cin, w_sh, b_sh)


def _small_reduce(gathered, name):
    _, _, D = gathered.shape

    def body(g_ref, loss_ref, db_ref, gn_ref, cv_ref, qk_ref, dm_ref):
        tot = g_ref[0]
        for r in range(1, N_DEV):
            tot = tot + g_ref[r]

        def both(block, row):
            return tot[block + row:block + row + 1, :] + tot[block + 8 + row:block + 8 + row + 1, :]

        loss = jnp.sum(both(PACK_HEAD, 0), axis=1, keepdims=True)
        loss_ref[...] = jnp.broadcast_to(loss, loss_ref.shape)
        db_ref[...] = jnp.zeros(db_ref.shape, F32)
        dm_ref[...] = jnp.zeros(dm_ref.shape, F32)
        for j, (block, row) in enumerate(MOD_SRC):
            db_ref[j:j + 1, :] = both(block, row)
            dm_ref[CTX_ROW, j:j + 1, :] = tot[block + row:block + row + 1, :]
            for r in range(N_DEV):
                dm_ref[r, j:j + 1, :] = g_ref[r, block + 8 + row:block + 8 + row + 1, :]
        gn_ref[...] = jnp.zeros(gn_ref.shape, F32)
        gn_ref[0:1, :] = both(PACK_N1, 2)
        gn_ref[8:9, :] = both(PACK_N2, 2)
        gn_ref[16:17, :] = both(PACK_N3, 2)
        gn_ref[24:25, :] = both(PACK_HEAD, 1)
        cv_ref[...] = jnp.zeros(cv_ref.shape, F32)
        for r in range(3):
            cv_ref[r:r + 1, :] = both(PACK_CONV, r)
        qk_ref[...] = jnp.zeros(qk_ref.shape, F32)
        qk_ref[0:1, 0:HEAD_DIM] = both(PACK_QK, 0)[:, 0:HEAD_DIM]
        qk_ref[0:1, HEAD_DIM:2 * HEAD_DIM] = both(PACK_QK, 1)[:, 0:HEAD_DIM]

    return pl.pallas_call(
        body, name=name,
        out_shape=[jax.ShapeDtypeStruct((8, 128), F32), jax.ShapeDtypeStruct((16, D), F32),
                   jax.ShapeDtypeStruct((32, D), F32), jax.ShapeDtypeStruct((8, D), F32),
                   jax.ShapeDtypeStruct((8, D), F32), jax.ShapeDtypeStruct((16, 16, D), F32)],
        in_specs=_whole(1), out_specs=_whole(6),
        compiler_params=pltpu.CompilerParams(vmem_limit_bytes=VMEM_LIMIT),
    )(gathered)


def _wmod_grad(cin, dm_sh, w_sh, name):
    def body(c_ref, d_ref, w_ref, gw_ref, cp_ref):
        a = _silu(c_ref[...]).astype(BF16)
        d = d_ref[...].astype(BF16)
        gw_ref[...] = lax.dot_general(a, d, (((0,), (0,)), ((), ())), preferred_element_type=F32)
        cp_ref[...] = lax.dot_general(d, w_ref[...].astype(BF16), (((1,), (1,)), ((), ())),
                                      preferred_element_type=F32)

    return pl.pallas_call(
        body, name=name,
        out_shape=[jax.ShapeDtypeStruct(w_sh.shape, F32), jax.ShapeDtypeStruct(cin.shape, F32)],
        in_specs=_whole(3), out_specs=_whole(2),
        compiler_params=pltpu.CompilerParams(vmem_limit_bytes=VMEM_LIMIT),
    )(cin, dm_sh, w_sh)


def _cctx_grad(parts, c_ctx8, name):
    def body(p_ref, c_ref, o_ref):
        tot = p_ref[0] + p_ref[2] + p_ref[4] + p_ref[6]
        cv = c_ref[...]
        sig = jax.nn.sigmoid(cv)
        rows = lax.broadcasted_iota(jnp.int32, tot.shape, 0)
        o_ref[...] = jnp.where(rows == 0, tot * (sig * (1.0 + cv * (1.0 - sig))), 0.0)

    return pl.pallas_call(
        body, name=name, out_shape=jax.ShapeDtypeStruct(c_ctx8.shape, F32),
        in_specs=_whole(2), out_specs=pl.BlockSpec(memory_space=pltpu.VMEM),
    )(parts, c_ctx8)


def _pad_rows(a, rows):
    return jnp.pad(a, ((0, rows - a.shape[0]), (0, 0)))


def _pack_small(c_ctx, b_mod, n1, n2, n3, final_g, gq, gk, conv_sh, D):
    misc = jnp.concatenate([gq, gk, conv_sh.reshape(1, -1)], axis=1)
    return jnp.concatenate([_pad_rows(c_ctx[None], 8), _pad_rows(b_mod.reshape(N_MOD, D), 16), _pad_rows(n1, 8),
                            _pad_rows(n2, 8), _pad_rows(n3, 8), _pad_rows(final_g[None], 8), _pad_rows(misc, 8)], axis=0)


def _unpack_small(p, D, conv_shape):
    misc = p[56:57]
    return dict(c_ctx=p[0], b_mod=p[8:8 + N_MOD].reshape(1, N_MOD * D), norm1_g=p[24:25], norm2_g=p[32:33],
                norm3_g=p[40:41], final_g=p[48], q_norm_g=misc[:, 0:HEAD_DIM], k_norm_g=misc[:, HEAD_DIM:2 * HEAD_DIM],
                conv_w=misc[:, 2 * HEAD_DIM:].reshape(conv_shape))


WEIGHT_ORDER = ("c_ctx", "w_mod", "b_mod", "norm1_g", "norm2_g", "norm3_g", "ffn1_w_in", "ffn1_w_out", "w_in",
                "conv_w", "q_norm_g", "k_norm_g", "w_branch_conv", "w_branch_attn", "w_out", "ffn2_w_in",
                "ffn2_w_out", "final_g")
BIG = ("ffn1_w_in", "ffn1_w_out", "w_in", "w_branch_conv", "w_branch_attn", "w_out", "ffn2_w_in", "ffn2_w_out")
COLUMN_SHARDED = ("ffn1_w_in", "w_in", "ffn2_w_in")


def kernel(x, c, ctx, c_ctx, w_mod, b_mod, norm1_g, norm2_g, norm3_g, ffn1_w_in, ffn1_w_out, w_in, conv_w, q_norm_g, k_norm_g, w_branch_conv, w_branch_attn, w_out, ffn2_w_in, ffn2_w_out, final_g, loss_target, m_c_ctx, m_w_mod, m_b_mod, m_norm1_g, m_norm2_g, m_norm3_g, m_ffn1_w_in, m_ffn1_w_out, m_w_in, m_conv_w, m_q_norm_g, m_k_norm_g, m_w_branch_conv, m_w_branch_attn, m_w_out, m_ffn2_w_in, m_ffn2_w_out, m_final_g, v_c_ctx, v_w_mod, v_b_mod, v_norm1_g, v_norm2_g, v_norm3_g, v_ffn1_w_in, v_ffn1_w_out, v_w_in, v_conv_w, v_q_norm_g, v_k_norm_g, v_w_branch_conv, v_w_branch_attn, v_w_out, v_ffn2_w_in, v_ffn2_w_out, v_final_g):
    w = dict(c_ctx=c_ctx, w_mod=w_mod, b_mod=b_mod, norm1_g=norm1_g, norm2_g=norm2_g, norm3_g=norm3_g,
             ffn1_w_in=ffn1_w_in, ffn1_w_out=ffn1_w_out, w_in=w_in, conv_w=conv_w, q_norm_g=q_norm_g,
             k_norm_g=k_norm_g, w_branch_conv=w_branch_conv, w_branch_attn=w_branch_attn, w_out=w_out,
             ffn2_w_in=ffn2_w_in, ffn2_w_out=ffn2_w_out, final_g=final_g)
    m = dict(c_ctx=m_c_ctx, w_mod=m_w_mod, b_mod=m_b_mod, norm1_g=m_norm1_g, norm2_g=m_norm2_g, norm3_g=m_norm3_g,
             ffn1_w_in=m_ffn1_w_in, ffn1_w_out=m_ffn1_w_out, w_in=m_w_in, conv_w=m_conv_w, q_norm_g=m_q_norm_g,
             k_norm_g=m_k_norm_g, w_branch_conv=m_w_branch_conv, w_branch_attn=m_w_branch_attn, w_out=m_w_out,
             ffn2_w_in=m_ffn2_w_in, ffn2_w_out=m_ffn2_w_out, final_g=m_final_g)
    v = dict(c_ctx=v_c_ctx, w_mod=v_w_mod, b_mod=v_b_mod, norm1_g=v_norm1_g, norm2_g=v_norm2_g, norm3_g=v_norm3_g,
             ffn1_w_in=v_ffn1_w_in, ffn1_w_out=v_ffn1_w_out, w_in=v_w_in, conv_w=v_conv_w, q_norm_g=v_q_norm_g,
             k_norm_g=v_k_norm_g, w_branch_conv=v_w_branch_conv, w_branch_attn=v_w_branch_attn, w_out=v_w_out,
             ffn2_w_in=v_ffn2_w_in, ffn2_w_out=v_ffn2_w_out, final_g=v_final_g)

    xi, yi, ci = _place()
    dev = 4 * xi + 2 * yi + ci
    shard = 2 * xi + yi
    idx = jnp.stack([ci, shard, 2 * (1 - xi) + yi, 2 * xi + (1 - yi), 2 * (1 - xi) + (1 - yi)]).astype(jnp.int32)
    D = x.shape[-1]
    ctx_len = ctx.shape[1]
    assert ctx_len == ROW and c.shape == (1, D)
    mcols = w_mod.shape[2]
    ccols = conv_w.shape[2]

    c_all = _allgather8(jnp.broadcast_to(c, (8, D)), "ag_c")[:, 0, :]
    cin = jnp.concatenate([c_all, _pad_rows(c_ctx[None], 8)], axis=0)
    b_sh = lax.dynamic_slice(b_mod, (0, shard * mcols), (1, mcols))
    mod_sh = _mod_rows(cin, w_mod[0], b_sh, "mod_rows")
    conv_rows = jnp.pad(conv_w[0], ((0, 8 - conv_w.shape[1]), (0, mcols - ccols)))
    mod_all = _allgather8(jnp.concatenate([mod_sh, conv_rows], axis=0), "ag_mod")
    mod_full = jnp.concatenate([mod_all[2 * s, :16] for s in range(N_CHIPS)], axis=1)
    conv_full = jnp.concatenate([mod_all[2 * s, 16:16 + conv_w.shape[1], :ccols] for s in range(N_CHIPS)], axis=1)
    mod_lat = lax.dynamic_slice(mod_full, (dev, 0), (1, N_MOD * D)).reshape(N_MOD, D)
    mod_ctx = mod_full[CTX_ROW].reshape(N_MOD, D)
    mods = jnp.stack([_pad_rows(mod_ctx, 16), _pad_rows(mod_lat, 16)])

    shards = [w[n][0].T.astype(BF16) if n in COLUMN_SHARDED else w[n][0].astype(BF16) for n in BIG]
    wts = _weights_allgather(shards, "ag_weights")

    xcat = jnp.concatenate([ctx[0], x[0]], axis=0)
    grad_x, grads, accs = _local_step(xcat, loss_target[0], mods, (norm1_g, norm2_g, norm3_g), final_g[None],
                                      q_norm_g, k_norm_g, conv_full, wts, ctx_len)

    g = {}
    reduced = _reduce_scatter(list(grads), idx)
    for n, r in zip(BIG, reduced):
        g[n] = (r.T if n in COLUMN_SHARDED else r)[None]

    pack = jnp.concatenate([a.reshape(2 * ACC_ROWS, D) for a in accs], axis=0)
    gathered = _allgather8(pack, "ag_small")
    loss8, db_mod, g_norms, g_conv, g_qk, dm = _small_reduce(gathered, "small_reduce")
    dm_sh = lax.dynamic_slice(dm[:, :N_MOD, :].reshape(16, N_MOD * D), (0, shard * mcols), (16, mcols))
    g_wmod, cpart = _wmod_grad(cin, dm_sh, w_mod[0], "wmod_grad")
    g["w_mod"] = g_wmod[None]
    cparts = _allgather8(cpart[CTX_ROW:CTX_ROW + 8], "ag_cctx")
    g_cctx = _cctx_grad(cparts, _pad_rows(c_ctx[None], 8), "cctx_grad")
    g_conv_sh = lax.dynamic_slice(g_conv, (0, shard * ccols), (conv_w.shape[1], ccols))
    g_misc = jnp.concatenate([g_qk[0:1, 0:2 * HEAD_DIM], g_conv_sh.reshape(1, -1)], axis=1)
    g_pack = jnp.concatenate([g_cctx, db_mod, g_norms, _pad_rows(g_misc, 8)], axis=0)

    def packed(p):
        return _pack_small(p["c_ctx"], p["b_mod"], p["norm1_g"], p["norm2_g"], p["norm3_g"], p["final_g"],
                           p["q_norm_g"], p["k_norm_g"], p["conv_w"][0], D)

    d_pack, m_pack, v_pack = _adamw(packed(w), g_pack, packed(m), packed(v), "adamw_small")
    g.update(_unpack_small(g_pack, D, conv_w.shape))
    delta = _unpack_small(d_pack, D, conv_w.shape)
    new_m = _unpack_small(m_pack, D, conv_w.shape)
    new_v = _unpack_small(v_pack, D, conv_w.shape)
    for n in BIG + ("w_mod",):
        d2, m2, v2 = _adamw(w[n][0], g[n][0], m[n][0], v[n][0], "adamw_" + n)
        delta[n], new_m[n], new_v[n] = d2[None], m2[None], v2[None]

    loss = loss8[0, 0]
    return (loss, grad_x[None], *[g[n] for n in WEIGHT_ORDER], *[delta[n] for n in WEIGHT_ORDER],
            *[new_m[n] for n in WEIGHT_ORDER], *[new_v[n] for n in WEIGHT_ORDER])
```

```python
import functools

import jax
import jax.numpy as jnp
from jax import lax
from jax.experimental import pallas as pl
from jax.experimental.pallas import tpu as pltpu

F32 = jnp.float32
BF16 = jnp.bfloat16

HEAD_DIM = 128
N_Q_HEADS = 8
N_KV_HEADS = 2
GROUP = N_Q_HEADS // N_KV_HEADS
GRID_W = 64
ROPE_THETA = 10000.0
EPS = 1e-6
ATTN_SCALE = HEAD_DIM ** -0.5

ADAM_LR = 0.001
ADAM_B1 = 0.9
ADAM_B2 = 0.999
ADAM_EPS = 1e-08
ADAM_WD = 0.01
ADAM_STEP = 10

ROW = 256
HALO = 8
ACC_ROWS = 8
N_CHIPS = 4
N_DEV = 8
MESH = pl.DeviceIdType.MESH
VMEM_LIMIT = 48 * 1024 * 1024
ADAMW_BLOCK_BYTES = 1024 * 1024


def _pick(n, prefs):
    for p in prefs:
        if n % p == 0:
            return p
    return n


def _params(sem):
    return pltpu.CompilerParams(dimension_semantics=sem, vmem_limit_bytes=VMEM_LIMIT)


def _stream(i):
    return jnp.minimum(i, 1)


def _matmul(a, b, mode, out_dtype, name, tm=None, tn=None, tk=None):
    if mode == "nn":
        (M, K), (K2, N) = a.shape, b.shape
    elif mode == "nt":
        (M, K), (N, K2) = a.shape, b.shape
    else:
        (K, M), (K2, N) = a.shape, b.shape
    assert K == K2, (a.shape, b.shape, mode)
    tm = tm or _pick(M, (768, 512, 256, 128))
    tn = tn or _pick(N, (1024, 512, 256, 128))
    tk = tk or _pick(K, (1024, 768, 512, 256, 128))
    nk = K // tk
    if mode == "tn":
        a_spec = pl.BlockSpec((tk, tm), lambda i, j, k: (k, i))
    else:
        a_spec = pl.BlockSpec((tm, tk), lambda i, j, k: (i, k))
    if mode == "nt":
        b_spec = pl.BlockSpec((tn, tk), lambda i, j, k: (j, k))
    else:
        b_spec = pl.BlockSpec((tk, tn), lambda i, j, k: (k, j))
    dims = {"nn": ((1,), (0,)), "nt": ((1,), (1,)), "tn": ((0,), (0,))}[mode]
    use_scratch = nk > 1 and out_dtype != F32

    def body(a_ref, b_ref, o_ref, *scratch):
        p = lax.dot_general(a_ref[...].astype(BF16), b_ref[...].astype(BF16), (dims, ((), ())),
                            preferred_element_type=F32)
        if nk == 1:
            o_ref[...] = p.astype(o_ref.dtype)
            return
        acc_ref = scratch[0] if use_scratch else o_ref
        k = pl.program_id(2)

        @pl.when(k == 0)
        def _():
            acc_ref[...] = p

        @pl.when(k > 0)
        def _():
            acc_ref[...] += p

        if use_scratch:
            @pl.when(k == nk - 1)
            def _():
                o_ref[...] = acc_ref[...].astype(o_ref.dtype)

    return pl.pallas_call(
        body, name=name,
        grid=(M // tm, N // tn, nk),
        in_specs=[a_spec, b_spec],
        out_specs=pl.BlockSpec((tm, tn), lambda i, j, k: (i, j)),
        out_shape=jax.ShapeDtypeStruct((M, N), out_dtype),
        scratch_shapes=[pltpu.VMEM((tm, tn), F32)] if use_scratch else [],
        compiler_params=_params(("parallel", "parallel", "arbitrary")),
    )(a, b)


def _row_spec(width, col=0):
    return pl.BlockSpec((ROW, width), lambda i, col=col: (i, col))


def _mods_spec(D):
    return pl.BlockSpec((1, 16, D), lambda i: (_stream(i), 0, 0))


def _acc_spec(D):
    return pl.BlockSpec((1, ACC_ROWS, D), lambda i: (_stream(i), 0, 0))


def _vec_spec(rows, D):
    return pl.BlockSpec((rows, D), lambda i: (0, 0))


def _acc_init(acc_ref):
    i = pl.program_id(0)

    @pl.when(i <= 1)
    def _():
        acc_ref[...] = jnp.zeros_like(acc_ref)


def _acc_add(acc_ref, row, val):
    acc_ref[0, row:row + 1, :] += jnp.sum(val, axis=0, keepdims=True)


def _resid_rmsmod_fwd(xprev, branch, mods, g, gate, shift_idx, scale_idx, name):
    T, D = xprev.shape
    has_res = branch is not None

    def body(*refs):
        if has_res:
            x_ref, f_ref, m_ref, g_ref, xo_ref, h_ref = refs
        else:
            x_ref, m_ref, g_ref, h_ref = refs
        m = m_ref[0]
        x = x_ref[...]
        if has_res:
            gate_idx, fac = gate
            x = x + (fac * m[gate_idx:gate_idx + 1, :]) * f_ref[...]
            xo_ref[...] = x
        inv = lax.rsqrt(jnp.mean(x * x, axis=-1, keepdims=True) + EPS)
        y = (x * inv) * g_ref[...]
        h = y * (1.0 + m[scale_idx:scale_idx + 1, :]) + m[shift_idx:shift_idx + 1, :]
        h_ref[...] = h.astype(BF16)

    in_specs = [_row_spec(D)] + ([_row_spec(D)] if has_res else []) + [_mods_spec(D), _vec_spec(1, D)]
    args = [xprev] + ([branch] if has_res else []) + [mods, g]
    out_specs = ([_row_spec(D)] if has_res else []) + [_row_spec(D)]
    out_shape = ([jax.ShapeDtypeStruct((T, D), F32)] if has_res else []) + [jax.ShapeDtypeStruct((T, D), BF16)]
    out = pl.pallas_call(
        body, name=name, grid=(T // ROW,), in_specs=in_specs, out_specs=out_specs, out_shape=out_shape,
        compiler_params=_params(("parallel",)),
    )(*args)
    return out if has_res else (None, out[0])


def _loss_head(x2, f2, mods, final_g, target, name):
    T, D = x2.shape
    nt = T // ROW

    def body(x_ref, f_ref, m_ref, g_ref, t_ref, dx_ref, df_ref, acc_ref):
        _acc_init(acc_ref)
        i = pl.program_id(0)
        m = m_ref[0]
        gate = 0.5 * m[8:9, :]
        f = f_ref[...]
        x = x_ref[...] + gate * f
        inv = lax.rsqrt(jnp.mean(x * x, axis=-1, keepdims=True) + EPS)
        xn = x * inv
        fg = g_ref[...]
        lat = (i > 0).astype(F32)
        e = (xn * fg - t_ref[...]) * lat
        dy = e * (1.0 / D)
        dxn = dy * fg
        dx = inv * (dxn - xn * jnp.mean(dxn * xn, axis=-1, keepdims=True))
        dx_ref[...] = dx
        df_ref[...] = (gate * dx).astype(BF16)
        _acc_add(acc_ref, 0, (0.5 / D) * e * e)
        _acc_add(acc_ref, 1, dy * xn)
        _acc_add(acc_ref, 2, 0.5 * dx * f)

    return pl.pallas_call(
        body, name=name, grid=(nt,),
        in_specs=[_row_spec(D), _row_spec(D), _mods_spec(D), _vec_spec(1, D),
                  pl.BlockSpec((ROW, D), lambda i: (jnp.maximum(i - 1, 0), 0))],
        out_specs=[_row_spec(D), _row_spec(D), _acc_spec(D)],
        out_shape=[jax.ShapeDtypeStruct((T, D), F32), jax.ShapeDtypeStruct((T, D), BF16),
                   jax.ShapeDtypeStruct((2, ACC_ROWS, D), F32)],
        compiler_params=_params(("arbitrary",)),
    )(x2, f2, mods, final_g, target)


def _rmsmod_bwd(x, dh, dres, mods, g, shift_idx, scale_idx, gate, branch, name, skip_first_tile=False):
    T, D = x.shape
    nt = T // ROW
    has_gate = gate is not None

    def body(*refs):
        if has_gate:
            x_ref, dh_ref, dr_ref, b_ref, m_ref, g_ref, dx_ref, db_ref, acc_ref = refs
        else:
            x_ref, dh_ref, dr_ref, m_ref, g_ref, dx_ref, acc_ref = refs
        _acc_init(acc_ref)
        m = m_ref[0]
        x = x_ref[...]
        dh = dh_ref[...]
        gg = g_ref[...]
        inv = lax.rsqrt(jnp.mean(x * x, axis=-1, keepdims=True) + EPS)
        xn = x * inv
        y = xn * gg
        dy = dh * (1.0 + m[scale_idx:scale_idx + 1, :])
        dxn = dy * gg
        dx = inv * (dxn - xn * jnp.mean(dxn * xn, axis=-1, keepdims=True)) + dr_ref[...]
        dx_ref[...] = dx
        _acc_add(acc_ref, 0, dh)
        _acc_add(acc_ref, 1, dh * y)
        _acc_add(acc_ref, 2, dy * xn)
        if has_gate:
            gate_idx, fac = gate
            b = b_ref[...]
            db_ref[...] = ((fac * m[gate_idx:gate_idx + 1, :]) * dx).astype(BF16)
            _acc_add(acc_ref, 3, fac * dx * b)

    in_specs = [_row_spec(D), _row_spec(D), _row_spec(D)] + ([_row_spec(D)] if has_gate else []) + \
               [_mods_spec(D), _vec_spec(1, D)]
    args = [x, dh, dres] + ([branch] if has_gate else []) + [mods, g]
    if skip_first_tile:
        dx_spec = pl.BlockSpec((ROW, D), lambda i: (jnp.maximum(i - 1, 0), 0))
        dx_shape = jax.ShapeDtypeStruct((T - ROW, D), F32)
    else:
        dx_spec = _row_spec(D)
        dx_shape = jax.ShapeDtypeStruct((T, D), F32)
    out_specs = [dx_spec] + ([_row_spec(D)] if has_gate else []) + [_acc_spec(D)]
    out_shape = [dx_shape] + ([jax.ShapeDtypeStruct((T, D), BF16)] if has_gate else []) + \
                [jax.ShapeDtypeStruct((2, ACC_ROWS, D), F32)]
    out = pl.pallas_call(
        body, name=name, grid=(nt,), in_specs=in_specs, out_specs=out_specs, out_shape=out_shape,
        compiler_params=_params(("arbitrary",)),
    )(*args)
    if has_gate:
        return out
    return out[0], None, out[1]


def _swiglu_fwd(u, name):
    T, F2 = u.shape
    F = F2 // 2

    def body(u_ref, s_ref):
        a = u_ref[:, :F]
        b = u_ref[:, F:]
        s_ref[...] = ((a * jax.nn.sigmoid(a)) * b).astype(BF16)

    return pl.pallas_call(
        body, name=name, grid=(T // ROW,),
        in_specs=[_row_spec(F2)], out_specs=_row_spec(F),
        out_shape=jax.ShapeDtypeStruct((T, F), BF16),
        compiler_params=_params(("parallel",)),
    )(u)


def _swiglu_bwd(u, ds, name):
    T, F2 = u.shape
    F = F2 // 2

    def body(u_ref, ds_ref, du_ref):
        a = u_ref[:, :F]
        b = u_ref[:, F:]
        ds = ds_ref[...]
        sig = jax.nn.sigmoid(a)
        silu = a * sig
        du_ref[:, :F] = (ds * b * (sig * (1.0 + a * (1.0 - sig)))).astype(BF16)
        du_ref[:, F:] = (ds * silu).astype(BF16)

    return pl.pallas_call(
        body, name=name, grid=(T // ROW,),
        in_specs=[_row_spec(F2), _row_spec(F)], out_specs=_row_spec(F2),
        out_shape=jax.ShapeDtypeStruct((T, F2), BF16),
        compiler_params=_params(("parallel",)),
    )(u, ds)


def _halo_specs(width, col, nt):
    per = ROW // HALO
    prev = pl.BlockSpec((HALO, width), lambda i, col=col: (jnp.maximum(i * per - 1, 0), col))
    nxt = pl.BlockSpec((HALO, width), lambda i, col=col: (jnp.minimum((i + 1) * per, nt * per - 1), col))
    return prev, nxt


def _shift_rows(v, prev_row, next_row):
    rows = lax.broadcasted_iota(jnp.int32, v.shape, 0)
    down = jnp.where(rows == 0, prev_row, pltpu.roll(v, 1, 0))
    up = jnp.where(rows == v.shape[0] - 1, next_row, pltpu.roll(v, v.shape[0] - 1, 0))
    return down, up


def _conv_fwd(P, conv_w, D, name):
    T = P.shape[0]
    nt = T // ROW
    cg_p, cg_n = _halo_specs(D, 1, nt)
    vc_p, vc_n = _halo_specs(D, 2, nt)

    def body(bg_ref, cg_ref, vc_ref, cgp_ref, vcp_ref, cgn_ref, vcn_ref, w_ref, y_ref):
        i = pl.program_id(0)
        has_prev = (i != 1).astype(F32)
        has_next = (i != nt - 1).astype(F32)
        u = cg_ref[...] * vc_ref[...]
        up_row = cgp_ref[HALO - 1:HALO, :] * vcp_ref[HALO - 1:HALO, :] * has_prev
        un_row = cgn_ref[0:1, :] * vcn_ref[0:1, :] * has_next
        um1, up1 = _shift_rows(u, up_row, un_row)
        w = w_ref[...]
        conv = um1 * w[0:1, :] + u * w[1:2, :] + up1 * w[2:3, :]
        y_ref[...] = (bg_ref[...] * conv).astype(BF16)

    return pl.pallas_call(
        body, name=name, grid=(nt,),
        in_specs=[_row_spec(D, 0), _row_spec(D, 1), _row_spec(D, 2), cg_p, vc_p, cg_n, vc_n, _vec_spec(3, D)],
        out_specs=_row_spec(D),
        out_shape=jax.ShapeDtypeStruct((T, D), BF16),
        compiler_params=_params(("parallel",)),
    )(P, P, P, P, P, P, P, conv_w)


def _conv_bwd(P, dy, conv_w, D, name):
    T = P.shape[0]
    nt = T // ROW
    bg_p, bg_n = _halo_specs(D, 0, nt)
    cg_p, cg_n = _halo_specs(D, 1, nt)
    vc_p, vc_n = _halo_specs(D, 2, nt)
    dy_p, dy_n = _halo_specs(D, 0, nt)

    def body(bg_ref, cg_ref, vc_ref, dy_ref, bgp_ref, cgp_ref, vcp_ref, dyp_ref,
             bgn_ref, cgn_ref, vcn_ref, dyn_ref, w_ref, o_ref, acc_ref):
        _acc_init(acc_ref)
        i = pl.program_id(0)
        lat = (i > 0).astype(F32)
        has_prev = (i != 1).astype(F32)
        has_next = (i != nt - 1).astype(F32)
        last = HALO - 1
        bg = bg_ref[...]
        cg = cg_ref[...]
        vc = vc_ref[...]
        dyv = dy_ref[...] * lat
        u = cg * vc
        up_row = cgp_ref[last:HALO, :] * vcp_ref[last:HALO, :] * has_prev
        un_row = cgn_ref[0:1, :] * vcn_ref[0:1, :] * has_next
        um1, up1 = _shift_rows(u, up_row, un_row)
        w = w_ref[...]
        conv = um1 * w[0:1, :] + u * w[1:2, :] + up1 * w[2:3, :]
        dc = dyv * bg
        dcp_row = dyp_ref[last:HALO, :] * bgp_ref[last:HALO, :] * has_prev
        dcn_row = dyn_ref[0:1, :] * bgn_ref[0:1, :] * has_next
        dcm1, dcp1 = _shift_rows(dc, dcp_row, dcn_row)
        du = dcp1 * w[0:1, :] + dc * w[1:2, :] + dcm1 * w[2:3, :]
        o_ref[:, 0:D] = (dyv * conv).astype(BF16)
        o_ref[:, D:2 * D] = (du * vc * lat).astype(BF16)
        o_ref[:, 2 * D:3 * D] = (du * cg * lat).astype(BF16)
        _acc_add(acc_ref, 0, dc * um1)
        _acc_add(acc_ref, 1, dc * u)
        _acc_add(acc_ref, 2, dc * up1)

    return pl.pallas_call(
        body, name=name, grid=(nt,),
        in_specs=[_row_spec(D, 0), _row_spec(D, 1), _row_spec(D, 2), _row_spec(D, 0),
                  bg_p, cg_p, vc_p, dy_p, bg_n, cg_n, vc_n, dy_n, _vec_spec(3, D)],
        out_specs=[_row_spec(3 * D), _acc_spec(D)],
        out_shape=[jax.ShapeDtypeStruct((T, 3 * D), BF16), jax.ShapeDtypeStruct((2, ACC_ROWS, D), F32)],
        compiler_params=_params(("arbitrary",)),
    )(P, P, P, dy, P, P, P, dy, P, P, P, dy, conv_w)


def _rope_tables(ctx_len, seq):
    n_freq = HEAD_DIM // 4
    rows = seq // GRID_W
    row = jnp.repeat(jnp.arange(rows), GRID_W).astype(F32)
    col = jnp.tile(jnp.arange(GRID_W), rows).astype(F32)
    inv = ROPE_THETA ** (-jnp.arange(n_freq, dtype=F32) / n_freq)
    ar = row[:, None] * inv
    ac = col[:, None] * inv
    cos_t = jnp.concatenate([jnp.cos(ar), jnp.cos(ar), jnp.cos(ac), jnp.cos(ac)], axis=1)
    sin_t = jnp.concatenate([-jnp.sin(ar), jnp.sin(ar), -jnp.sin(ac), jnp.sin(ac)], axis=1)
    cos_t = jnp.concatenate([jnp.ones((ctx_len, HEAD_DIM), F32), cos_t], axis=0)
    sin_t = jnp.concatenate([jnp.zeros((ctx_len, HEAD_DIM), F32), sin_t], axis=0)
    return cos_t, sin_t


def _swap_halves(y):
    lanes = lax.broadcasted_iota(jnp.int32, y.shape, 1)
    first = (lanes % 64) < 32
    return jnp.where(first, pltpu.roll(y, HEAD_DIM - 32, 1), pltpu.roll(y, 32, 1))


def _qk_fwd(P, gq, gk, cos_t, sin_t, D, name):
    T = P.shape[0]
    QW = N_Q_HEADS * HEAD_DIM
    KW = N_KV_HEADS * HEAD_DIM
    q_col = (3 * D) // QW
    k_col = (3 * D + QW) // KW
    v_col = k_col + 1

    def body(q_ref, k_ref, v_ref, gq_ref, gk_ref, c_ref, s_ref, qo_ref, ko_ref, vo_ref):
        c = c_ref[...]
        s = s_ref[...]

        def head(x, g):
            inv = lax.rsqrt(jnp.mean(x * x, axis=-1, keepdims=True) + EPS)
            y = (x * inv) * g
            return y * c + _swap_halves(y) * s

        for h in range(N_Q_HEADS):
            sl = slice(h * HEAD_DIM, (h + 1) * HEAD_DIM)
            qo_ref[:, sl] = head(q_ref[:, sl], gq_ref[...]).astype(BF16)
        for h in range(N_KV_HEADS):
            sl = slice(h * HEAD_DIM, (h + 1) * HEAD_DIM)
            ko_ref[:, sl] = head(k_ref[:, sl], gk_ref[...]).astype(BF16)
        vo_ref[...] = v_ref[...].astype(BF16)

    return pl.pallas_call(
        body, name=name, grid=(T // ROW,),
        in_specs=[_row_spec(QW, q_col), _row_spec(KW, k_col), _row_spec(KW, v_col),
                  _vec_spec(1, HEAD_DIM), _vec_spec(1, HEAD_DIM), _row_spec(HEAD_DIM), _row_spec(HEAD_DIM)],
        out_specs=[_row_spec(QW), _row_spec(KW), _row_spec(KW)],
        out_shape=[jax.ShapeDtypeStruct((T, QW), BF16), jax.ShapeDtypeStruct((T, KW), BF16),
                   jax.ShapeDtypeStruct((T, KW), BF16)],
        compiler_params=_params(("parallel",)),
    )(P, P, P, gq, gk, cos_t, sin_t)


def _qk_bwd(P, dq, dk, dv, gq, gk, cos_t, sin_t, D, name):
    T = P.shape[0]
    QW = N_Q_HEADS * HEAD_DIM
    KW = N_KV_HEADS * HEAD_DIM
    q_col = (3 * D) // QW
    k_col = (3 * D + QW) // KW

    def body(q_ref, k_ref, dq_ref, dk_ref, dv_ref, gq_ref, gk_ref, c_ref, s_ref, o_ref, acc_ref):
        _acc_init(acc_ref)
        c = c_ref[...]
        s = s_ref[...]

        def head(x, d, g):
            dyv = d * c + _swap_halves(d * s)
            inv = lax.rsqrt(jnp.mean(x * x, axis=-1, keepdims=True) + EPS)
            xn = x * inv
            dxn = dyv * g
            dx = inv * (dxn - xn * jnp.mean(dxn * xn, axis=-1, keepdims=True))
            return dx, jnp.sum(dyv * xn, axis=0, keepdims=True)

        dgq = jnp.zeros((1, HEAD_DIM), F32)
        for h in range(N_Q_HEADS):
            sl = slice(h * HEAD_DIM, (h + 1) * HEAD_DIM)
            dx, dg = head(q_ref[:, sl], dq_ref[:, sl], gq_ref[...])
            o_ref[:, sl] = dx.astype(BF16)
            dgq = dgq + dg
        dgk = jnp.zeros((1, HEAD_DIM), F32)
        for h in range(N_KV_HEADS):
            sl = slice(h * HEAD_DIM, (h + 1) * HEAD_DIM)
            dx, dg = head(k_ref[:, sl], dk_ref[:, sl], gk_ref[...])
            o_ref[:, QW + h * HEAD_DIM:QW + (h + 1) * HEAD_DIM] = dx.astype(BF16)
            dgk = dgk + dg
        o_ref[:, QW + KW:QW + 2 * KW] = dv_ref[...].astype(BF16)
        acc_ref[0, 0:1, 0:HEAD_DIM] += dgq
        acc_ref[0, 1:2, 0:HEAD_DIM] += dgk

    return pl.pallas_call(
        body, name=name, grid=(T // ROW,),
        in_specs=[_row_spec(QW, q_col), _row_spec(KW, k_col), _row_spec(QW), _row_spec(KW), _row_spec(KW),
                  _vec_spec(1, HEAD_DIM), _vec_spec(1, HEAD_DIM), _row_spec(HEAD_DIM), _row_spec(HEAD_DIM)],
        out_specs=[_row_spec(QW + 2 * KW), _acc_spec(D)],
        out_shape=[jax.ShapeDtypeStruct((T, QW + 2 * KW), BF16), jax.ShapeDtypeStruct((2, ACC_ROWS, D), F32)],
        compiler_params=_params(("arbitrary",)),
    )(P, P, dq, dk, dv, gq, gk, cos_t, sin_t)


def _to_row(col, n):
    return jnp.transpose(jnp.broadcast_to(col, (n, HEAD_DIM)))[0:1, :]


def _flash_fwd(q, k, v, name, tq=ROW, tk=None):
    T = q.shape[0]
    tk = tk or _pick(T, (768, 512, 256))
    nk = T // tk
    GW = GROUP * HEAD_DIM

    def body(q_ref, k_ref, v_ref, o_ref, lse_ref, qs_ref, m_ref, l_ref, acc_ref):
        ki = pl.program_id(2)

        @pl.when(ki == 0)
        def _():
            for g in range(GROUP):
                qs_ref[g * tq:(g + 1) * tq, :] = q_ref[:, g * HEAD_DIM:(g + 1) * HEAD_DIM]
            m_ref[...] = jnp.full(m_ref.shape, -jnp.inf, F32)
            l_ref[...] = jnp.zeros(l_ref.shape, F32)
            acc_ref[...] = jnp.zeros(acc_ref.shape, F32)

        st = lax.dot_general(k_ref[...], qs_ref[...], (((1,), (1,)), ((), ())),
                             preferred_element_type=F32) * ATTN_SCALE
        m_prev = m_ref[...]
        m_new = jnp.maximum(m_prev, jnp.max(st, axis=0, keepdims=True))
        alpha = jnp.exp(m_prev - m_new)
        pt = jnp.exp(st - m_new)
        l_ref[...] = alpha * l_ref[...] + jnp.sum(pt, axis=0, keepdims=True)
        acc_ref[...] = alpha * acc_ref[...] + lax.dot_general(
            v_ref[...], pt.astype(BF16), (((0,), (0,)), ((), ())), preferred_element_type=F32)
        m_ref[...] = m_new

        @pl.when(ki == nk - 1)
        def _():
            out = jnp.transpose(acc_ref[...] / l_ref[...])
            lse = m_ref[...] + jnp.log(l_ref[...])
            for g in range(GROUP):
                o_ref[:, g * HEAD_DIM:(g + 1) * HEAD_DIM] = out[g * tq:(g + 1) * tq, :]
                lse_ref[0, g:g + 1, :] = lse[:, g * tq:(g + 1) * tq]

    return pl.pallas_call(
        body, name=name, grid=(N_KV_HEADS, T // tq, nk),
        in_specs=[pl.BlockSpec((tq, GW), lambda h, i, j: (i, h)),
                  pl.BlockSpec((tk, HEAD_DIM), lambda h, i, j: (j, h)),
                  pl.BlockSpec((tk, HEAD_DIM), lambda h, i, j: (j, h))],
        out_specs=[pl.BlockSpec((tq, GW), lambda h, i, j: (i, h)),
                   pl.BlockSpec((1, GROUP, tq), lambda h, i, j: (h, 0, i))],
        out_shape=[jax.ShapeDtypeStruct((T, N_Q_HEADS * HEAD_DIM), F32),
                   jax.ShapeDtypeStruct((N_KV_HEADS, GROUP, T), F32)],
        scratch_shapes=[pltpu.VMEM((GROUP * tq, HEAD_DIM), BF16), pltpu.VMEM((1, GROUP * tq), F32),
                        pltpu.VMEM((1, GROUP * tq), F32), pltpu.VMEM((HEAD_DIM, GROUP * tq), F32)],
        compiler_params=_params(("parallel", "parallel", "arbitrary")),
    )(q, k, v)


def _attn_delta(do, o, name):
    T, QW = do.shape

    def body(do_ref, o_ref, dob_ref, dl_ref):
        dov = do_ref[...]
        dob_ref[...] = dov.astype(BF16)
        prod = dov * o_ref[...]
        for h in range(N_Q_HEADS):
            d = jnp.sum(prod[:, h * HEAD_DIM:(h + 1) * HEAD_DIM], axis=1, keepdims=True)
            dl_ref[h // GROUP, (h % GROUP):(h % GROUP) + 1, :] = _to_row(d, ROW)

    return pl.pallas_call(
        body, name=name, grid=(T // ROW,),
        in_specs=[_row_spec(QW), _row_spec(QW)],
        out_specs=[_row_spec(QW), pl.BlockSpec((N_KV_HEADS, GROUP, ROW), lambda i: (0, 0, i))],
        out_shape=[jax.ShapeDtypeStruct((T, QW), BF16), jax.ShapeDtypeStruct((N_KV_HEADS, GROUP, T), F32)],
        compiler_params=_params(("parallel",)),
    )(do, o)


def _flash_bwd(q, k, v, do, lse, delta, name, tq=ROW, tk=None):
    T = q.shape[0]
    tk = tk or _pick(T, (768, 512, 256))
    nk = T // tk
    GW = GROUP * HEAD_DIM
    nt = (((1,), (1,)), ((), ()))

    def body(q_ref, do_ref, k_ref, v_ref, lse_ref, dl_ref, dq_ref, dk_ref, dv_ref, qs_ref, dos_ref, dqt_ref):
        qi = pl.program_id(1)
        ki = pl.program_id(2)

        @pl.when(ki == 0)
        def _():
            for g in range(GROUP):
                qs_ref[g * tq:(g + 1) * tq, :] = q_ref[:, g * HEAD_DIM:(g + 1) * HEAD_DIM]
                dos_ref[g * tq:(g + 1) * tq, :] = do_ref[:, g * HEAD_DIM:(g + 1) * HEAD_DIM]
            dqt_ref[...] = jnp.zeros(dqt_ref.shape, F32)

        lse_row = jnp.concatenate([lse_ref[0, g:g + 1, :] for g in range(GROUP)], axis=1)
        dl_row = jnp.concatenate([dl_ref[0, g:g + 1, :] for g in range(GROUP)], axis=1)
        kk = k_ref[...]
        qs = qs_ref[...]
        dos = dos_ref[...]
        st = lax.dot_general(kk, qs, nt, preferred_element_type=F32) * ATTN_SCALE
        pt = jnp.exp(st - lse_row)
        dpt = lax.dot_general(v_ref[...], dos, nt, preferred_element_type=F32)
        dst = ((pt * (dpt - dl_row)) * ATTN_SCALE).astype(BF16)
        dv_c = jnp.dot(pt.astype(BF16), dos, preferred_element_type=F32)
        dk_c = jnp.dot(dst, qs, preferred_element_type=F32)
        rows = pl.ds(pl.multiple_of(ki * tk, tk), tk)

        @pl.when(qi == 0)
        def _():
            dk_ref[rows, :] = dk_c
            dv_ref[rows, :] = dv_c

        @pl.when(qi > 0)
        def _():
            dk_ref[rows, :] += dk_c
            dv_ref[rows, :] += dv_c

        dqt_ref[...] += lax.dot_general(kk, dst, (((0,), (0,)), ((), ())), preferred_element_type=F32)

        @pl.when(ki == nk - 1)
        def _():
            dqv = jnp.transpose(dqt_ref[...])
            for g in range(GROUP):
                dq_ref[:, g * HEAD_DIM:(g + 1) * HEAD_DIM] = dqv[g * tq:(g + 1) * tq, :]

    return pl.pallas_call(
        body, name=name, grid=(N_KV_HEADS, T // tq, nk),
        in_specs=[pl.BlockSpec((tq, GW), lambda h, i, j: (i, h)),
                  pl.BlockSpec((tq, GW), lambda h, i, j: (i, h)),
                  pl.BlockSpec((tk, HEAD_DIM), lambda h, i, j: (j, h)),
                  pl.BlockSpec((tk, HEAD_DIM), lambda h, i, j: (j, h)),
                  pl.BlockSpec((1, GROUP, tq), lambda h, i, j: (h, 0, i)),
                  pl.BlockSpec((1, GROUP, tq), lambda h, i, j: (h, 0, i))],
        out_specs=[pl.BlockSpec((tq, GW), lambda h, i, j: (i, h)),
                   pl.BlockSpec((T, HEAD_DIM), lambda h, i, j: (0, h)),
                   pl.BlockSpec((T, HEAD_DIM), lambda h, i, j: (0, h))],
        out_shape=[jax.ShapeDtypeStruct((T, N_Q_HEADS * HEAD_DIM), F32),
                   jax.ShapeDtypeStruct((T, N_KV_HEADS * HEAD_DIM), F32),
                   jax.ShapeDtypeStruct((T, N_KV_HEADS * HEAD_DIM), F32)],
        scratch_shapes=[pltpu.VMEM((GROUP * tq, HEAD_DIM), BF16), pltpu.VMEM((GROUP * tq, HEAD_DIM), BF16),
                        pltpu.VMEM((HEAD_DIM, GROUP * tq), F32)],
        compiler_params=_params(("arbitrary", "arbitrary", "arbitrary")),
    )(q, do, k, v, lse, delta)


def _gate_specs(D):
    w = D // 2
    first = (3 * D + (N_Q_HEADS + 2 * N_KV_HEADS) * HEAD_DIM) // w
    return [pl.BlockSpec((ROW, w), lambda i, c=first + j: (i, c)) for j in range(4)]


def _merge_fwd(a1, a2, P, D, name):
    T = a1.shape[0]
    w = D // 2

    def body(a1_ref, a2_ref, g0, g1, g2, g3, z_ref):
        for j, (gc, ga) in enumerate(((g0, g2), (g1, g3))):
            sl = slice(j * w, (j + 1) * w)
            z = jax.nn.sigmoid(gc[...]) * a1_ref[:, sl] + jax.nn.sigmoid(ga[...]) * a2_ref[:, sl]
            z_ref[:, sl] = z.astype(BF16)

    return pl.pallas_call(
        body, name=name, grid=(T // ROW,),
        in_specs=[_row_spec(D), _row_spec(D)] + _gate_specs(D),
        out_specs=_row_spec(D), out_shape=jax.ShapeDtypeStruct((T, D), BF16),
        compiler_params=_params(("parallel",)),
    )(a1, a2, P, P, P, P)


def _merge_bwd(dz, a1, a2, P, D, name):
    T = a1.shape[0]
    w = D // 2

    def body(dz_ref, a1_ref, a2_ref, g0, g1, g2, g3, d1_ref, d2_ref, dg_ref):
        for j, (gc, ga) in enumerate(((g0, g2), (g1, g3))):
            sl = slice(j * w, (j + 1) * w)
            dz = dz_ref[:, sl]
            sc = jax.nn.sigmoid(gc[...])
            sa = jax.nn.sigmoid(ga[...])
            d1_ref[:, sl] = (dz * sc).astype(BF16)
            d2_ref[:, sl] = (dz * sa).astype(BF16)
            dg_ref[:, j * w:(j + 1) * w] = (dz * a1_ref[:, sl] * (sc * (1.0 - sc))).astype(BF16)
            dg_ref[:, D + j * w:D + (j + 1) * w] = (dz * a2_ref[:, sl] * (sa * (1.0 - sa))).astype(BF16)

    return pl.pallas_call(
        body, name=name, grid=(T // ROW,),
        in_specs=[_row_spec(D), _row_spec(D), _row_spec(D)] + _gate_specs(D),
        out_specs=[_row_spec(D), _row_spec(D), _row_spec(2 * D)],
        out_shape=[jax.ShapeDtypeStruct((T, D), BF16), jax.ShapeDtypeStruct((T, D), BF16),
                   jax.ShapeDtypeStruct((T, 2 * D), BF16)],
        compiler_params=_params(("parallel",)),
    )(dz, a1, a2, P, P, P, P)


def _adamw_math(w, g, m, v):
    m = ADAM_B1 * m + (1.0 - ADAM_B1) * g
    v = ADAM_B2 * v + (1.0 - ADAM_B2) * (g * g)
    m_hat = m / (1.0 - ADAM_B1 ** ADAM_STEP)
    v_hat = v / (1.0 - ADAM_B2 ** ADAM_STEP)
    delta = -ADAM_LR * (m_hat / (jnp.sqrt(v_hat) + ADAM_EPS) + ADAM_WD * w)
    return delta, m, v


def _adamw(w, g, m, v, name):
    R, C = w.shape
    tr = _pick(R, tuple(t for t in (256, 128, 64, 32, 16, 8) if t * C * 4 <= ADAMW_BLOCK_BYTES))

    def body(w_ref, g_ref, m_ref, v_ref, d_ref, mo_ref, vo_ref):
        d, mn, vn = _adamw_math(w_ref[...], g_ref[...], m_ref[...], v_ref[...])
        d_ref[...] = d
        mo_ref[...] = mn
        vo_ref[...] = vn

    spec = pl.BlockSpec((tr, C), lambda i: (i, 0))
    return pl.pallas_call(
        body, name=name, grid=(R // tr,),
        in_specs=[spec] * 4, out_specs=[spec] * 3,
        out_shape=[jax.ShapeDtypeStruct((R, C), F32)] * 3,
        compiler_params=_params(("parallel",)),
    )(w, g, m, v)


def _local_step(xcat, target, mods, norm_g, final_g, gq, gk, conv_w, wts, ctx_len):
    T, D = xcat.shape
    w1i, w1o, wi, wbc, wba, wo, w2i, w2o = wts
    g1, g2, g3 = norm_g
    cos_t, sin_t = _rope_tables(ctx_len, T - ctx_len)

    _, h1 = _resid_rmsmod_fwd(xcat, None, mods, g1, None, 0, 1, "f_norm1")
    u1 = _matmul(h1, w1i, "nt", F32, "f_ffn1_in")
    s1 = _swiglu_fwd(u1, "f_ffn1_act")
    f1 = _matmul(s1, w1o, "nn", F32, "f_ffn1_out")
    x1, h2 = _resid_rmsmod_fwd(xcat, f1, mods, g2, (2, 0.5), 3, 4, "f_norm2")
    P = _matmul(h2, wi, "nt", F32, "f_mix_in")
    yc = _conv_fwd(P, conv_w, D, "f_conv")
    qn, kn, vb = _qk_fwd(P, gq, gk, cos_t, sin_t, D, "f_qk")
    o, lse = _flash_fwd(qn, kn, vb, "f_attn")
    a1 = _matmul(yc, wbc, "nn", F32, "f_branch_conv")
    a2 = _matmul(o, wba, "nn", F32, "f_branch_attn")
    z = _merge_fwd(a1, a2, P, D, "f_merge")
    mo = _matmul(z, wo, "nn", F32, "f_mix_out")
    x2, h3 = _resid_rmsmod_fwd(x1, mo, mods, g3, (5, 1.0), 6, 7, "f_norm3")
    u2 = _matmul(h3, w2i, "nt", F32, "f_ffn2_in")
    s2 = _swiglu_fwd(u2, "f_ffn2_act")
    f2 = _matmul(s2, w2o, "nn", F32, "f_ffn2_out")
    dx3, df2, acc_head = _loss_head(x2, f2, mods, final_g, target, "loss_head")

    ds2 = _matmul(df2, w2o, "nt", F32, "b_ffn2_out_dx")
    du2 = _swiglu_bwd(u2, ds2, "b_ffn2_act")
    g_w2o = _matmul(s2, df2, "tn", BF16, "b_ffn2_out_dw")
    g_w2i = _matmul(du2, h3, "tn", BF16, "b_ffn2_in_dw")
    dh3 = _matmul(du2, w2i, "nn", F32, "b_ffn2_in_dx")
    dx2, dmo, acc_n3 = _rmsmod_bwd(x2, dh3, dx3, mods, g3, 6, 7, (5, 1.0), mo, "b_norm3")

    dz = _matmul(dmo, wo, "nt", F32, "b_mix_out_dx")
    g_wo = _matmul(z, dmo, "tn", BF16, "b_mix_out_dw")
    da1, da2, dgt = _merge_bwd(dz, a1, a2, P, D, "b_merge")
    dyc = _matmul(da1, wbc, "nt", F32, "b_branch_conv_dx")
    do = _matmul(da2, wba, "nt", F32, "b_branch_attn_dx")
    g_wbc = _matmul(yc, da1, "tn", BF16, "b_branch_conv_dw")
    g_wba = _matmul(o, da2, "tn", BF16, "b_branch_attn_dw")
    dob, delta = _attn_delta(do, o, "b_attn_delta")
    dq, dk, dv = _flash_bwd(qn, kn, vb, dob, lse, delta, "b_attn")
    dqkv, acc_qk = _qk_bwd(P, dq, dk, dv, gq, gk, cos_t, sin_t, D, "b_qk")
    dconv, acc_conv = _conv_bwd(P, dyc, conv_w, D, "b_conv")
    dP = jnp.concatenate([dconv, dqkv, dgt], axis=1)
    dh2 = _matmul(dP, wi, "nn", F32, "b_mix_in_dx")
    g_wi = _matmul(dP, h2, "tn", BF16, "b_mix_in_dw")
    dx1, df1, acc_n2 = _rmsmod_bwd(x1, dh2, dx2, mods, g2, 3, 4, (2, 0.5), f1, "b_norm2")

    ds1 = _matmul(df1, w1o, "nt", F32, "b_ffn1_out_dx")
    du1 = _swiglu_bwd(u1, ds1, "b_ffn1_act")
    g_w1o = _matmul(s1, df1, "tn", BF16, "b_ffn1_out_dw")
    g_w1i = _matmul(du1, h1, "tn", BF16, "b_ffn1_in_dw")
    dh1 = _matmul(du1, w1i, "nn", F32, "b_ffn1_in_dx")
    grad_x, _, acc_n1 = _rmsmod_bwd(xcat, dh1, dx1, mods, g1, 0, 1, None, None, "b_norm1", skip_first_tile=True)

    grads = (g_w1i, g_w1o, g_wi, g_wbc, g_wba, g_wo, g_w2i, g_w2o)
    accs = (acc_head, acc_n3, acc_n2, acc_n1, acc_conv, acc_qk)
    return grad_x, grads, accs


def _place():
    return lax.axis_index("x"), lax.axis_index("y"), lax.axis_index("c")


def _other_chips(x, y):
    return [(1 - x, y), (x, 1 - y), (1 - x, 1 - y)]


def _allgather8(v, name):
    R, N = v.shape

    def body(v_ref, out_ref, send_sems, recv_sems, local_sem):
        x, y, c = _place()
        me, sibling = (x, y, c), (x, y, 1 - c)
        chips = _other_chips(x, y)

        def blk(px, py, pc):
            return out_ref.at[4 * px + 2 * py + pc]

        def copy(k, block, to, src=None):
            return pltpu.make_async_remote_copy(
                src_ref=blk(*block) if src is None else src, dst_ref=blk(*block),
                send_sem=send_sems.at[k], recv_sem=recv_sems.at[k], device_id=to, device_id_type=MESH)

        mine = pltpu.make_async_copy(v_ref, blk(*me), local_sem)
        mine.start()
        first = [copy(0, me, sibling, src=v_ref)]
        first += [copy(1 + j, me, (*chip, c), src=v_ref) for j, chip in enumerate(chips)]
        for cp in first:
            cp.start()
        passed = [copy(4 + j, (*chip, c), sibling) for j, chip in enumerate(chips)]
        for j, chip in enumerate(chips):
            copy(1 + j, (*chip, c), me).wait_recv()
            passed[j].start()
        copy(0, sibling, me).wait_recv()
        for j, chip in enumerate(chips):
            copy(4 + j, (*chip, 1 - c), me).wait_recv()
        for cp in first + passed:
            cp.wait_send()
        mine.wait()

    return pl.pallas_call(
        body, name=name,
        out_shape=jax.ShapeDtypeStruct((N_DEV, R, N), v.dtype),
        in_specs=[pl.BlockSpec(memory_space=pltpu.VMEM)],
        out_specs=pl.BlockSpec(memory_space=pltpu.VMEM),
        scratch_shapes=[pltpu.SemaphoreType.DMA((7,)), pltpu.SemaphoreType.DMA((7,)), pltpu.SemaphoreType.DMA],
        compiler_params=pltpu.CompilerParams(vmem_limit_bytes=VMEM_LIMIT),
    )(v)


def _any_specs(n):
    return [pl.BlockSpec(memory_space=pl.ANY)] * n


def _weights_allgather(fulls, name):
    n = len(fulls)

    def body(*refs):
        full = refs[n:2 * n]
        send_sems, recv_sems = refs[2 * n:]
        x, y, c = _place()
        sibling = (x, y, 1 - c)
        chips = _other_chips(x, y)

        def piece(t, px, py, h):
            rs = fulls[t].shape[0] // N_CHIPS
            return full[t].at[pl.ds((2 * px + py) * rs + h * (rs // 2), rs // 2), :]

        def copy(k, t, block, to):
            return pltpu.make_async_remote_copy(
                src_ref=piece(t, *block), dst_ref=piece(t, *block),
                send_sem=send_sems.at[k], recv_sem=recv_sems.at[k], device_id=to, device_id_type=MESH)

        first = []
        for t in range(n):
            for j, chip in enumerate(chips):
                cp = copy(3 * t + j, t, (x, y, c), (*chip, c))
                cp.start()
                first.append(cp)
        passed = []
        for t in range(n):
            for j, chip in enumerate(chips):
                copy(3 * t + j, t, (*chip, c), (x, y, c)).wait_recv()
                cp = copy(3 * n + 3 * t + j, t, (*chip, c), sibling)
                cp.start()
                passed.append(cp)
        for t in range(n):
            for j, chip in enumerate(chips):
                copy(3 * n + 3 * t + j, t, (*chip, 1 - c), (x, y, c)).wait_recv()
        for cp in first + passed:
            cp.wait_send()

    return pl.pallas_call(
        body, name=name,
        out_shape=[jax.ShapeDtypeStruct(f.shape, f.dtype) for f in fulls],
        in_specs=_any_specs(n), out_specs=_any_specs(n),
        input_output_aliases={t: t for t in range(n)},
        scratch_shapes=[pltpu.SemaphoreType.DMA((6 * n,)), pltpu.SemaphoreType.DMA((6 * n,))],
    )(*fulls)


def _pair_exchange(grads, name):
    n = len(grads)

    def body(*refs):
        g, land = refs[:n], refs[n:2 * n]
        send_sems, recv_sems = refs[2 * n:]
        x, y, c = _place()
        sibling = (x, y, 1 - c)
        copies = []
        for t in range(n):
            half = grads[t].shape[0] // (2 * N_CHIPS)
            for s in range(N_CHIPS):
                cp = pltpu.make_async_remote_copy(
                    src_ref=g[t].at[pl.ds((2 * s + 1 - c) * half, half), :], dst_ref=land[t].at[s],
                    send_sem=send_sems.at[N_CHIPS * t + s], recv_sem=recv_sems.at[N_CHIPS * t + s],
                    device_id=sibling, device_id_type=MESH)
                cp.start()
                copies.append(cp)
        for cp in copies:
            cp.wait_recv()
        for cp in copies:
            cp.wait_send()

    return pl.pallas_call(
        body, name=name,
        out_shape=[jax.ShapeDtypeStruct((N_CHIPS, a.shape[0] // (2 * N_CHIPS), a.shape[1]), a.dtype) for a in grads],
        in_specs=_any_specs(n), out_specs=_any_specs(n),
        scratch_shapes=[pltpu.SemaphoreType.DMA((N_CHIPS * n,)), pltpu.SemaphoreType.DMA((N_CHIPS * n,))],
    )(*grads)


def _pair_sum(g, landed, idx, name):
    _, half, D = landed.shape
    g4 = g.reshape(N_CHIPS, 2, half, D)
    tr = _pick(half, (416, 352, 128))

    def body(idx_ref, g_ref, l_ref, o_ref):
        o_ref[...] = (g_ref[0].astype(F32) + l_ref[...].astype(F32)).astype(BF16)

    return pl.pallas_call(
        body, name=name,
        grid_spec=pltpu.PrefetchScalarGridSpec(
            num_scalar_prefetch=1, grid=(N_CHIPS, half // tr),
            in_specs=[pl.BlockSpec((1, 1, tr, D), lambda s, i, idx: (idx[1 + s], idx[0], i, 0)),
                      pl.BlockSpec((1, tr, D), lambda s, i, idx: (idx[1 + s], i, 0))],
            out_specs=pl.BlockSpec((1, tr, D), lambda s, i, idx: (s, i, 0))),
        out_shape=jax.ShapeDtypeStruct((N_CHIPS, half, D), BF16),
        compiler_params=_params(("arbitrary", "arbitrary")),
    )(idx, g4, landed)


def _chip_exchange(sums, name):
    n = len(sums)

    def body(*refs):
        ps, land = refs[:n], refs[n:2 * n]
        send_sems, recv_sems = refs[2 * n:]
        x, y, c = _place()
        copies = []
        for t in range(n):
            for j, chip in enumerate(_other_chips(x, y)):
                cp = pltpu.make_async_remote_copy(
                    src_ref=ps[t].at[1 + j], dst_ref=land[t].at[j],
                    send_sem=send_sems.at[3 * t + j], recv_sem=recv_sems.at[3 * t + j],
                    device_id=(*chip, c), device_id_type=MESH)
                cp.start()
                copies.append(cp)
        for cp in copies:
            cp.wait_recv()
        for cp in copies:
            cp.wait_send()

    return pl.pallas_call(
        body, name=name,
        out_shape=[jax.ShapeDtypeStruct((3,) + a.shape[1:], a.dtype) for a in sums],
        in_specs=_any_specs(n), out_specs=_any_specs(n),
        scratch_shapes=[pltpu.SemaphoreType.DMA((3 * n,)), pltpu.SemaphoreType.DMA((3 * n,))],
    )(*sums)


def _chip_sum(ps, landed, name):
    _, half, D = ps.shape
    tr = _pick(half, (416, 352, 128))

    def body(p_ref, l_ref, o_ref):
        acc = p_ref[0].astype(F32)
        for j in range(3):
            acc = acc + l_ref[j].astype(F32)
        o_ref[...] = acc

    return pl.pallas_call(
        body, name=name, grid=(half // tr,),
        in_specs=[pl.BlockSpec((1, tr, D), lambda i: (0, i, 0)), pl.BlockSpec((3, tr, D), lambda i: (0, i, 0))],
        out_specs=pl.BlockSpec((tr, D), lambda i: (i, 0)),
        out_shape=jax.ShapeDtypeStruct((half, D), F32),
        compiler_params=_params(("parallel",)),
    )(ps, landed)


def _pair_swap(halves, name):
    n = len(halves)

    def body(*refs):
        hv, other = refs[:n], refs[n:2 * n]
        send_sems, recv_sems = refs[2 * n:]
        x, y, c = _place()
        copies = []
        for t in range(n):
            cp = pltpu.make_async_remote_copy(src_ref=hv[t], dst_ref=other[t], send_sem=send_sems.at[t],
                                              recv_sem=recv_sems.at[t], device_id=(x, y, 1 - c), device_id_type=MESH)
            cp.start()
            copies.append(cp)
        for cp in copies:
            cp.wait_recv()
        for cp in copies:
            cp.wait_send()

    return pl.pallas_call(
        body, name=name,
        out_shape=[jax.ShapeDtypeStruct(a.shape, a.dtype) for a in halves],
        in_specs=_any_specs(n), out_specs=_any_specs(n),
        scratch_shapes=[pltpu.SemaphoreType.DMA((n,)), pltpu.SemaphoreType.DMA((n,))],
    )(*halves)


def _reduce_scatter(grads, idx):
    landed = _pair_exchange(grads, "rs_pair_exchange")
    sums = [_pair_sum(g, l, idx, f"rs_pair_sum_{t}") for t, (g, l) in enumerate(zip(grads, landed))]
    landed2 = _chip_exchange(sums, "rs_chip_exchange")
    halves = [_chip_sum(p, l, f"rs_chip_sum_{t}") for t, (p, l) in enumerate(zip(sums, landed2))]
    others = _pair_swap(halves, "rs_pair_swap")
    south = lax.axis_index("c") == 0
    return [jnp.where(south, jnp.concatenate([h, o], axis=0), jnp.concatenate([o, h], axis=0))
            for h, o in zip(halves, others)]


N_MOD = 9
PACK_HEAD, PACK_N3, PACK_N2, PACK_N1, PACK_CONV, PACK_QK = 0, 16, 32, 48, 64, 80
PACK_ROWS = 96
MOD_SRC = ((PACK_N1, 0), (PACK_N1, 1), (PACK_N2, 3), (PACK_N2, 0), (PACK_N2, 1),
           (PACK_N3, 3), (PACK_N3, 0), (PACK_N3, 1), (PACK_HEAD, 2))
CTX_ROW = 8


def _silu(v):
    return v * jax.nn.sigmoid(v)


def _whole(n):
    return [pl.BlockSpec(memory_space=pltpu.VMEM)] * n


def _mod_rows(cin, w_sh, b_sh, name):
    def body(c_ref, w_ref, b_ref, o_ref):
        a = _silu(c_ref[...]).astype(BF16)
        o_ref[...] = jnp.dot(a, w_ref[...].astype(BF16), preferred_element_type=F32) + b_ref[...]

    return pl.pallas_call(
        body, name=name, out_shape=jax.ShapeDtypeStruct((cin.shape[0], w_sh.shape[1]), F32),
        in_specs=_whole(3), out_specs=pl.BlockSpec(memory_space=pltpu.VMEM),
        compiler_params=pltpu.CompilerParams(vmem_limit_bytes=VMEM_LIMIT),
    )(cin, w_sh, b_sh)


def _small_reduce(gathered, name):
    _, _, D = gathered.shape

    def body(g_ref, loss_ref, db_ref, gn_ref, cv_ref, qk_ref, dm_ref):
        tot = g_ref[0]
        for r in range(1, N_DEV):
            tot = tot + g_ref[r]

        def both(block, row):
            return tot[block + row:block + row + 1, :] + tot[block + 8 + row:block + 8 + row + 1, :]

        loss = jnp.sum(both(PACK_HEAD, 0), axis=1, keepdims=True)
        loss_ref[...] = jnp.broadcast_to(loss, loss_ref.shape)
        db_ref[...] = jnp.zeros(db_ref.shape, F32)
        dm_ref[...] = jnp.zeros(dm_ref.shape, F32)
        for j, (block, row) in enumerate(MOD_SRC):
            db_ref[j:j + 1, :] = both(block, row)
            dm_ref[CTX_ROW, j:j + 1, :] = tot[block + row:block + row + 1, :]
            for r in range(N_DEV):
                dm_ref[r, j:j + 1, :] = g_ref[r, block + 8 + row:block + 8 + row + 1, :]
        gn_ref[...] = jnp.zeros(gn_ref.shape, F32)
        gn_ref[0:1, :] = both(PACK_N1, 2)
        gn_ref[8:9, :] = both(PACK_N2, 2)
        gn_ref[16:17, :] = both(PACK_N3, 2)
        gn_ref[24:25, :] = both(PACK_HEAD, 1)
        cv_ref[...] = jnp.zeros(cv_ref.shape, F32)
        for r in range(3):
            cv_ref[r:r + 1, :] = both(PACK_CONV, r)
        qk_ref[...] = jnp.zeros(qk_ref.shape, F32)
        qk_ref[0:1, 0:HEAD_DIM] = both(PACK_QK, 0)[:, 0:HEAD_DIM]
        qk_ref[0:1, HEAD_DIM:2 * HEAD_DIM] = both(PACK_QK, 1)[:, 0:HEAD_DIM]

    return pl.pallas_call(
        body, name=name,
        out_shape=[jax.ShapeDtypeStruct((8, 128), F32), jax.ShapeDtypeStruct((16, D), F32),
                   jax.ShapeDtypeStruct((32, D), F32), jax.ShapeDtypeStruct((8, D), F32),
                   jax.ShapeDtypeStruct((8, D), F32), jax.ShapeDtypeStruct((16, 16, D), F32)],
        in_specs=_whole(1), out_specs=_whole(6),
        compiler_params=pltpu.CompilerParams(vmem_limit_bytes=VMEM_LIMIT),
    )(gathered)


def _wmod_grad(cin, dm_sh, w_sh, name):
    def body(c_ref, d_ref, w_ref, gw_ref, cp_ref):
        a = _silu(c_ref[...]).astype(BF16)
        d = d_ref[...].astype(BF16)
        gw_ref[...] = lax.dot_general(a, d, (((0,), (0,)), ((), ())), preferred_element_type=F32)
        cp_ref[...] = lax.dot_general(d, w_ref[...].astype(BF16), (((1,), (1,)), ((), ())),
                                      preferred_element_type=F32)

    return pl.pallas_call(
        body, name=name,
        out_shape=[jax.ShapeDtypeStruct(w_sh.shape, F32), jax.ShapeDtypeStruct(cin.shape, F32)],
        in_specs=_whole(3), out_specs=_whole(2),
        compiler_params=pltpu.CompilerParams(vmem_limit_bytes=VMEM_LIMIT),
    )(cin, dm_sh, w_sh)


def _cctx_grad(parts, c_ctx8, name):
    def body(p_ref, c_ref, o_ref):
        tot = p_ref[0] + p_ref[2] + p_ref[4] + p_ref[6]
        cv = c_ref[...]
        sig = jax.nn.sigmoid(cv)
        rows = lax.broadcasted_iota(jnp.int32, tot.shape, 0)
        o_ref[...] = jnp.where(rows == 0, tot * (sig * (1.0 + cv * (1.0 - sig))), 0.0)

    return pl.pallas_call(
        body, name=name, out_shape=jax.ShapeDtypeStruct(c_ctx8.shape, F32),
        in_specs=_whole(2), out_specs=pl.BlockSpec(memory_space=pltpu.VMEM),
    )(parts, c_ctx8)


def _pad_rows(a, rows):
    return jnp.pad(a, ((0, rows - a.shape[0]), (0, 0)))


def _pack_small(c_ctx, b_mod, n1, n2, n3, final_g, gq, gk, conv_sh, D):
    misc = jnp.concatenate([gq, gk, conv_sh.reshape(1, -1)], axis=1)
    return jnp.concatenate([_pad_rows(c_ctx[None], 8), _pad_rows(b_mod.reshape(N_MOD, D), 16), _pad_rows(n1, 8),
                            _pad_rows(n2, 8), _pad_rows(n3, 8), _pad_rows(final_g[None], 8), _pad_rows(misc, 8)], axis=0)


def _unpack_small(p, D, conv_shape):
    misc = p[56:57]
    return dict(c_ctx=p[0], b_mod=p[8:8 + N_MOD].reshape(1, N_MOD * D), norm1_g=p[24:25], norm2_g=p[32:33],
                norm3_g=p[40:41], final_g=p[48], q_norm_g=misc[:, 0:HEAD_DIM], k_norm_g=misc[:, HEAD_DIM:2 * HEAD_DIM],
                conv_w=misc[:, 2 * HEAD_DIM:].reshape(conv_shape))


WEIGHT_ORDER = ("c_ctx", "w_mod", "b_mod", "norm1_g", "norm2_g", "norm3_g", "ffn1_w_in", "ffn1_w_out", "w_in",
                "conv_w", "q_norm_g", "k_norm_g", "w_branch_conv", "w_branch_attn", "w_out", "ffn2_w_in",
                "ffn2_w_out", "final_g")
BIG = ("ffn1_w_in", "ffn1_w_out", "w_in", "w_branch_conv", "w_branch_attn", "w_out", "ffn2_w_in", "ffn2_w_out")
COLUMN_SHARDED = ("ffn1_w_in", "w_in", "ffn2_w_in")


def kernel(x, c, ctx, c_ctx, w_mod, b_mod, norm1_g, norm2_g, norm3_g, ffn1_w_in, ffn1_w_out, w_in, conv_w, q_norm_g, k_norm_g, w_branch_conv, w_branch_attn, w_out, ffn2_w_in, ffn2_w_out, final_g, loss_target, m_c_ctx, m_w_mod, m_b_mod, m_norm1_g, m_norm2_g, m_norm3_g, m_ffn1_w_in, m_ffn1_w_out, m_w_in, m_conv_w, m_q_norm_g, m_k_norm_g, m_w_branch_conv, m_w_branch_attn, m_w_out, m_ffn2_w_in, m_ffn2_w_out, m_final_g, v_c_ctx, v_w_mod, v_b_mod, v_norm1_g, v_norm2_g, v_norm3_g, v_ffn1_w_in, v_ffn1_w_out, v_w_in, v_conv_w, v_q_norm_g, v_k_norm_g, v_w_branch_conv, v_w_branch_attn, v_w_out, v_ffn2_w_in, v_ffn2_w_out, v_final_g):
    w = dict(c_ctx=c_ctx, w_mod=w_mod, b_mod=b_mod, norm1_g=norm1_g, norm2_g=norm2_g, norm3_g=norm3_g,
             ffn1_w_in=ffn1_w_in, ffn1_w_out=ffn1_w_out, w_in=w_in, conv_w=conv_w, q_norm_g=q_norm_g,
             k_norm_g=k_norm_g, w_branch_conv=w_branch_conv, w_branch_attn=w_branch_attn, w_out=w_out,
             ffn2_w_in=ffn2_w_in, ffn2_w_out=ffn2_w_out, final_g=final_g)
    m = dict(c_ctx=m_c_ctx, w_mod=m_w_mod, b_mod=m_b_mod, norm1_g=m_norm1_g, norm2_g=m_norm2_g, norm3_g=m_norm3_g,
             ffn1_w_in=m_ffn1_w_in, ffn1_w_out=m_ffn1_w_out, w_in=m_w_in, conv_w=m_conv_w, q_norm_g=m_q_norm_g,
             k_norm_g=m_k_norm_g, w_branch_conv=m_w_branch_conv, w_branch_attn=m_w_branch_attn, w_out=m_w_out,
             ffn2_w_in=m_ffn2_w_in, ffn2_w_out=m_ffn2_w_out, final_g=m_final_g)
    v = dict(c_ctx=v_c_ctx, w_mod=v_w_mod, b_mod=v_b_mod, norm1_g=v_norm1_g, norm2_g=v_norm2_g, norm3_g=v_norm3_g,
             ffn1_w_in=v_ffn1_w_in, ffn1_w_out=v_ffn1_w_out, w_in=v_w_in, conv_w=v_conv_w, q_norm_g=v_q_norm_g,
             k_norm_g=v_k_norm_g, w_branch_conv=v_w_branch_conv, w_branch_attn=v_w_branch_attn, w_out=v_w_out,
             ffn2_w_in=v_ffn2_w_in, ffn2_w_out=v_ffn2_w_out, final_g=v_final_g)

    xi, yi, ci = _place()
    dev = 4 * xi + 2 * yi + ci
    shard = 2 * xi + yi
    idx = jnp.stack([ci, shard, 2 * (1 - xi) + yi, 2 * xi + (1 - yi), 2 * (1 - xi) + (1 - yi)]).astype(jnp.int32)
    D = x.shape[-1]
    ctx_len = ctx.shape[1]
    assert ctx_len == ROW and c.shape == (1, D)
    mcols = w_mod.shape[2]
    ccols = conv_w.shape[2]

    c_all = _allgather8(jnp.broadcast_to(c, (8, D)), "ag_c")[:, 0, :]
    cin = jnp.concatenate([c_all, _pad_rows(c_ctx[None], 8)], axis=0)
    b_sh = lax.dynamic_slice(b_mod, (0, shard * mcols), (1, mcols))
    mod_sh = _mod_rows(cin, w_mod[0], b_sh, "mod_rows")
    conv_rows = jnp.pad(conv_w[0], ((0, 8 - conv_w.shape[1]), (0, mcols - ccols)))
    mod_all = _allgather8(jnp.concatenate([mod_sh, conv_rows], axis=0), "ag_mod")
    mod_full = jnp.concatenate([mod_all[2 * s, :16] for s in range(N_CHIPS)], axis=1)
    conv_full = jnp.concatenate([mod_all[2 * s, 16:16 + conv_w.shape[1], :ccols] for s in range(N_CHIPS)], axis=1)
    mod_lat = lax.dynamic_slice(mod_full, (dev, 0), (1, N_MOD * D)).reshape(N_MOD, D)
    mod_ctx = mod_full[CTX_ROW].reshape(N_MOD, D)
    mods = jnp.stack([_pad_rows(mod_ctx, 16), _pad_rows(mod_lat, 16)])

    shards = [w[n][0].T.astype(BF16) if n in COLUMN_SHARDED else w[n][0].astype(BF16) for n in BIG]
    fulls = [lax.dynamic_update_slice(lax.empty((N_CHIPS * s.shape[0], D), BF16), s, (shard * s.shape[0], 0))
             for s in shards]
    wts = _weights_allgather(fulls, "ag_weights")

    xcat = jnp.concatenate([ctx[0], x[0]], axis=0)
    grad_x, grads, accs = _local_step(xcat, loss_target[0], mods, (norm1_g, norm2_g, norm3_g), final_g[None],
                                      q_norm_g, k_norm_g, conv_full, wts, ctx_len)

    g = {}
    reduced = _reduce_scatter(list(grads), idx)
    for n, r in zip(BIG, reduced):
        g[n] = (r.T if n in COLUMN_SHARDED else r)[None]

    pack = jnp.concatenate([a.reshape(2 * ACC_ROWS, D) for a in accs], axis=0)
    gathered = _allgather8(pack, "ag_small")
    loss8, db_mod, g_norms, g_conv, g_qk, dm = _small_reduce(gathered, "small_reduce")
    dm_sh = lax.dynamic_slice(dm[:, :N_MOD, :].reshape(16, N_MOD * D), (0, shard * mcols), (16, mcols))
    g_wmod, cpart = _wmod_grad(cin, dm_sh, w_mod[0], "wmod_grad")
    g["w_mod"] = g_wmod[None]
    cparts = _allgather8(cpart[CTX_ROW:CTX_ROW + 8], "ag_cctx")
    g_cctx = _cctx_grad(cparts, _pad_rows(c_ctx[None], 8), "cctx_grad")
    g_conv_sh = lax.dynamic_slice(g_conv, (0, shard * ccols), (conv_w.shape[1], ccols))
    g_misc = jnp.concatenate([g_qk[0:1, 0:2 * HEAD_DIM], g_conv_sh.reshape(1, -1)], axis=1)
    g_pack = jnp.concatenate([g_cctx, db_mod, g_norms, _pad_rows(g_misc, 8)], axis=0)

    def packed(p):
        return _pack_small(p["c_ctx"], p["b_mod"], p["norm1_g"], p["norm2_g"], p["norm3_g"], p["final_g"],
                           p["q_norm_g"], p["k_norm_g"], p["conv_w"][0], D)

    d_pack, m_pack, v_pack = _adamw(packed(w), g_pack, packed(m), packed(v), "adamw_small")
    g.update(_unpack_small(g_pack, D, conv_w.shape))
    delta = _unpack_small(d_pack, D, conv_w.shape)
    new_m = _unpack_small(m_pack, D, conv_w.shape)
    new_v = _unpack_small(v_pack, D, conv_w.shape)
    for n in BIG + ("w_mod",):
        d2, m2, v2 = _adamw(w[n][0], g[n][0], m[n][0], v[n][0], "adamw_" + n)
        delta[n], new_m[n], new_v[n] = d2[None], m2[None], v2[None]

    loss = loss8[0, 0]
    return (loss, grad_x[None], *[g[n] for n in WEIGHT_ORDER], *[delta[n] for n in WEIGHT_ORDER],
            *[new_m[n] for n in WEIGHT_ORDER], *[new_v[n] for n in WEIGHT_ORDER])
```

```python
import functools

import jax
import jax.numpy as jnp
from jax import lax
from jax.experimental import pallas as pl
from jax.experimental.pallas import tpu as pltpu

F32 = jnp.float32
BF16 = jnp.bfloat16

HEAD_DIM = 128
N_Q_HEADS = 8
N_KV_HEADS = 2
GROUP = N_Q_HEADS // N_KV_HEADS
GRID_W = 64
ROPE_THETA = 10000.0
EPS = 1e-6
ATTN_SCALE = HEAD_DIM ** -0.5

ADAM_LR = 0.001
ADAM_B1 = 0.9
ADAM_B2 = 0.999
ADAM_EPS = 1e-08
ADAM_WD = 0.01
ADAM_STEP = 10

ROW = 256
HALO = 8
ACC_ROWS = 8
N_CHIPS = 4
N_DEV = 8
MESH = pl.DeviceIdType.MESH
VMEM_LIMIT = 48 * 1024 * 1024
ADAMW_BLOCK_BYTES = 1024 * 1024


def _pick(n, prefs):
    for p in prefs:
        if n % p == 0:
            return p
    return n


def _params(sem):
    return pltpu.CompilerParams(dimension_semantics=sem, vmem_limit_bytes=VMEM_LIMIT)


def _stream(i):
    return jnp.minimum(i, 1)


def _matmul(a, b, mode, out_dtype, name, tm=None, tn=None, tk=None):
    if mode == "nn":
        (M, K), (K2, N) = a.shape, b.shape
    elif mode == "nt":
        (M, K), (N, K2) = a.shape, b.shape
    else:
        (K, M), (K2, N) = a.shape, b.shape
    assert K == K2, (a.shape, b.shape, mode)
    tm = tm or _pick(M, (1664, 1408, 1024, 512, 256, 128) if mode == "tn" else (1408, 768, 512, 256, 128))
    tn = tn or _pick(N, (1664, 1408, 1024, 512, 256, 128))
    tk = tk or _pick(K, (1664, 1408, 1024, 768, 512, 256, 128))
    nk = K // tk
    if mode == "tn":
        a_spec = pl.BlockSpec((tk, tm), lambda i, j, k: (k, i))
    else:
        a_spec = pl.BlockSpec((tm, tk), lambda i, j, k: (i, k))
    if mode == "nt":
        b_spec = pl.BlockSpec((tn, tk), lambda i, j, k: (j, k))
    else:
        b_spec = pl.BlockSpec((tk, tn), lambda i, j, k: (k, j))
    dims = {"nn": ((1,), (0,)), "nt": ((1,), (1,)), "tn": ((0,), (0,))}[mode]
    use_scratch = nk > 1 and out_dtype != F32

    def body(a_ref, b_ref, o_ref, *scratch):
        p = lax.dot_general(a_ref[...].astype(BF16), b_ref[...].astype(BF16), (dims, ((), ())),
                            preferred_element_type=F32)
        if nk == 1:
            o_ref[...] = p.astype(o_ref.dtype)
            return
        acc_ref = scratch[0] if use_scratch else o_ref
        k = pl.program_id(2)

        @pl.when(k == 0)
        def _():
            acc_ref[...] = p

        @pl.when(k > 0)
        def _():
            acc_ref[...] += p

        if use_scratch:
            @pl.when(k == nk - 1)
            def _():
                o_ref[...] = acc_ref[...].astype(o_ref.dtype)

    return pl.pallas_call(
        body, name=name,
        grid=(M // tm, N // tn, nk),
        in_specs=[a_spec, b_spec],
        out_specs=pl.BlockSpec((tm, tn), lambda i, j, k: (i, j)),
        out_shape=jax.ShapeDtypeStruct((M, N), out_dtype),
        scratch_shapes=[pltpu.VMEM((tm, tn), F32)] if use_scratch else [],
        compiler_params=_params(("parallel", "parallel", "arbitrary")),
    )(a, b)


def _row_spec(width, col=0):
    return pl.BlockSpec((ROW, width), lambda i, col=col: (i, col))


def _mods_spec(D):
    return pl.BlockSpec((1, 16, D), lambda i: (_stream(i), 0, 0))


def _acc_spec(D):
    return pl.BlockSpec((1, ACC_ROWS, D), lambda i: (_stream(i), 0, 0))


def _vec_spec(rows, D):
    return pl.BlockSpec((rows, D), lambda i: (0, 0))


def _acc_init(acc_ref):
    i = pl.program_id(0)

    @pl.when(i <= 1)
    def _():
        acc_ref[...] = jnp.zeros_like(acc_ref)


def _acc_add(acc_ref, row, val):
    acc_ref[0, row:row + 1, :] += jnp.sum(val, axis=0, keepdims=True)


def _resid_rmsmod_fwd(xprev, branch, mods, g, gate, shift_idx, scale_idx, name):
    T, D = xprev.shape
    has_res = branch is not None

    def body(*refs):
        if has_res:
            x_ref, f_ref, m_ref, g_ref, xo_ref, h_ref = refs
        else:
            x_ref, m_ref, g_ref, h_ref = refs
        m = m_ref[0]
        x = x_ref[...]
        if has_res:
            gate_idx, fac = gate
            x = x + (fac * m[gate_idx:gate_idx + 1, :]) * f_ref[...]
            xo_ref[...] = x
        inv = lax.rsqrt(jnp.mean(x * x, axis=-1, keepdims=True) + EPS)
        y = (x * inv) * g_ref[...]
        h = y * (1.0 + m[scale_idx:scale_idx + 1, :]) + m[shift_idx:shift_idx + 1, :]
        h_ref[...] = h.astype(BF16)

    in_specs = [_row_spec(D)] + ([_row_spec(D)] if has_res else []) + [_mods_spec(D), _vec_spec(1, D)]
    args = [xprev] + ([branch] if has_res else []) + [mods, g]
    out_specs = ([_row_spec(D)] if has_res else []) + [_row_spec(D)]
    out_shape = ([jax.ShapeDtypeStruct((T, D), F32)] if has_res else []) + [jax.ShapeDtypeStruct((T, D), BF16)]
    out = pl.pallas_call(
        body, name=name, grid=(T // ROW,), in_specs=in_specs, out_specs=out_specs, out_shape=out_shape,
        compiler_params=_params(("parallel",)),
    )(*args)
    return out if has_res else (None, out[0])


def _loss_head(x2, f2, mods, final_g, target, name):
    T, D = x2.shape
    nt = T // ROW

    def body(x_ref, f_ref, m_ref, g_ref, t_ref, dx_ref, df_ref, acc_ref):
        _acc_init(acc_ref)
        i = pl.program_id(0)
        m = m_ref[0]
        gate = 0.5 * m[8:9, :]
        f = f_ref[...]
        x = x_ref[...] + gate * f
        inv = lax.rsqrt(jnp.mean(x * x, axis=-1, keepdims=True) + EPS)
        xn = x * inv
        fg = g_ref[...]
        lat = (i > 0).astype(F32)
        e = (xn * fg - t_ref[...]) * lat
        dy = e * (1.0 / D)
        dxn = dy * fg
        dx = inv * (dxn - xn * jnp.mean(dxn * xn, axis=-1, keepdims=True))
        dx_ref[...] = dx
        df_ref[...] = (gate * dx).astype(BF16)
        _acc_add(acc_ref, 0, (0.5 / D) * e * e)
        _acc_add(acc_ref, 1, dy * xn)
        _acc_add(acc_ref, 2, 0.5 * dx * f)

    return pl.pallas_call(
        body, name=name, grid=(nt,),
        in_specs=[_row_spec(D), _row_spec(D), _mods_spec(D), _vec_spec(1, D),
                  pl.BlockSpec((ROW, D), lambda i: (jnp.maximum(i - 1, 0), 0))],
        out_specs=[_row_spec(D), _row_spec(D), _acc_spec(D)],
        out_shape=[jax.ShapeDtypeStruct((T, D), F32), jax.ShapeDtypeStruct((T, D), BF16),
                   jax.ShapeDtypeStruct((2, ACC_ROWS, D), F32)],
        compiler_params=_params(("arbitrary",)),
    )(x2, f2, mods, final_g, target)


def _rmsmod_bwd(x, dh, dres, mods, g, shift_idx, scale_idx, gate, branch, name, skip_first_tile=False):
    T, D = x.shape
    nt = T // ROW
    has_gate = gate is not None

    def body(*refs):
        if has_gate:
            x_ref, dh_ref, dr_ref, b_ref, m_ref, g_ref, dx_ref, db_ref, acc_ref = refs
        else:
            x_ref, dh_ref, dr_ref, m_ref, g_ref, dx_ref, acc_ref = refs
        _acc_init(acc_ref)
        m = m_ref[0]
        x = x_ref[...]
        dh = dh_ref[...]
        gg = g_ref[...]
        inv = lax.rsqrt(jnp.mean(x * x, axis=-1, keepdims=True) + EPS)
        xn = x * inv
        y = xn * gg
        dy = dh * (1.0 + m[scale_idx:scale_idx + 1, :])
        dxn = dy * gg
        dx = inv * (dxn - xn * jnp.mean(dxn * xn, axis=-1, keepdims=True)) + dr_ref[...]
        dx_ref[...] = dx
        _acc_add(acc_ref, 0, dh)
        _acc_add(acc_ref, 1, dh * y)
        _acc_add(acc_ref, 2, dy * xn)
        if has_gate:
            gate_idx, fac = gate
            b = b_ref[...]
            db_ref[...] = ((fac * m[gate_idx:gate_idx + 1, :]) * dx).astype(BF16)
            _acc_add(acc_ref, 3, fac * dx * b)

    in_specs = [_row_spec(D), _row_spec(D), _row_spec(D)] + ([_row_spec(D)] if has_gate else []) + \
               [_mods_spec(D), _vec_spec(1, D)]
    args = [x, dh, dres] + ([branch] if has_gate else []) + [mods, g]
    if skip_first_tile:
        dx_spec = pl.BlockSpec((ROW, D), lambda i: (jnp.maximum(i - 1, 0), 0))
        dx_shape = jax.ShapeDtypeStruct((T - ROW, D), F32)
    else:
        dx_spec = _row_spec(D)
        dx_shape = jax.ShapeDtypeStruct((T, D), F32)
    out_specs = [dx_spec] + ([_row_spec(D)] if has_gate else []) + [_acc_spec(D)]
    out_shape = [dx_shape] + ([jax.ShapeDtypeStruct((T, D), BF16)] if has_gate else []) + \
                [jax.ShapeDtypeStruct((2, ACC_ROWS, D), F32)]
    out = pl.pallas_call(
        body, name=name, grid=(nt,), in_specs=in_specs, out_specs=out_specs, out_shape=out_shape,
        compiler_params=_params(("arbitrary",)),
    )(*args)
    if has_gate:
        return out
    return out[0], None, out[1]


FFN_ROWS = 384
_NT = (((1,), (1,)), ((), ()))


def _ffn_chunk(F):
    return _pick(F, (1408, 512, 256, 128))


def _resident():
    return pl.BlockSpec(memory_space=pltpu.VMEM)


def _ffn_fwd(h, w_in_t, w_out, name):
    T, D = h.shape
    F = w_out.shape[0]
    cw = _ffn_chunk(F)
    tm = _pick(T, (FFN_ROWS, ROW))

    def body(h_ref, wi_ref, wo_ref, u_ref, s_ref, f_ref):
        hv = h_ref[...]
        acc = jnp.zeros((tm, D), F32)
        for j in range(F // cw):
            a = lax.dot_general(hv, wi_ref[j * cw:(j + 1) * cw, :], _NT, preferred_element_type=F32)
            b = lax.dot_general(hv, wi_ref[F + j * cw:F + (j + 1) * cw, :], _NT, preferred_element_type=F32)
            s = ((a * jax.nn.sigmoid(a)) * b).astype(BF16)
            u_ref[:, j * cw:(j + 1) * cw] = a.astype(BF16)
            u_ref[:, F + j * cw:F + (j + 1) * cw] = b.astype(BF16)
            s_ref[:, j * cw:(j + 1) * cw] = s
            acc = acc + jnp.dot(s, wo_ref[j * cw:(j + 1) * cw, :], preferred_element_type=F32)
        f_ref[...] = acc

    row = lambda w: pl.BlockSpec((tm, w), lambda i: (i, 0))
    return pl.pallas_call(
        body, name=name, grid=(T // tm,),
        in_specs=[row(D), _resident(), _resident()],
        out_specs=[row(2 * F), row(F), row(D)],
        out_shape=[jax.ShapeDtypeStruct((T, 2 * F), BF16), jax.ShapeDtypeStruct((T, F), BF16),
                   jax.ShapeDtypeStruct((T, D), F32)],
        compiler_params=_params(("parallel",)),
    )(h, w_in_t, w_out)


def _ffn_bwd(df, u, w_in_t, w_out, name):
    T, D = df.shape
    F = w_out.shape[0]
    cw = _ffn_chunk(F)
    tm = _pick(T, (FFN_ROWS, ROW))

    def body(df_ref, u_ref, wi_ref, wo_ref, du_ref, dh_ref):
        dfv = df_ref[...]
        acc = jnp.zeros((tm, D), F32)
        for j in range(F // cw):
            ds = lax.dot_general(dfv, wo_ref[j * cw:(j + 1) * cw, :], _NT, preferred_element_type=F32)
            a = u_ref[:, j * cw:(j + 1) * cw].astype(F32)
            b = u_ref[:, F + j * cw:F + (j + 1) * cw].astype(F32)
            sig = jax.nn.sigmoid(a)
            da = (ds * b * (sig * (1.0 + a * (1.0 - sig)))).astype(BF16)
            db = (ds * (a * sig)).astype(BF16)
            du_ref[:, j * cw:(j + 1) * cw] = da
            du_ref[:, F + j * cw:F + (j + 1) * cw] = db
            acc = acc + jnp.dot(da, wi_ref[j * cw:(j + 1) * cw, :], preferred_element_type=F32)
            acc = acc + jnp.dot(db, wi_ref[F + j * cw:F + (j + 1) * cw, :], preferred_element_type=F32)
        dh_ref[...] = acc

    row = lambda w: pl.BlockSpec((tm, w), lambda i: (i, 0))
    return pl.pallas_call(
        body, name=name, grid=(T // tm,),
        in_specs=[row(D), row(2 * F), _resident(), _resident()],
        out_specs=[row(2 * F), row(D)],
        out_shape=[jax.ShapeDtypeStruct((T, 2 * F), BF16), jax.ShapeDtypeStruct((T, D), F32)],
        compiler_params=_params(("parallel",)),
    )(df, u, w_in_t, w_out)


def _halo_specs(width, col, nt):
    per = ROW // HALO
    prev = pl.BlockSpec((HALO, width), lambda i, col=col: (jnp.maximum(i * per - 1, 0), col))
    nxt = pl.BlockSpec((HALO, width), lambda i, col=col: (jnp.minimum((i + 1) * per, nt * per - 1), col))
    return prev, nxt


def _shift_rows(v, prev_row, next_row):
    rows = lax.broadcasted_iota(jnp.int32, v.shape, 0)
    down = jnp.where(rows == 0, prev_row, pltpu.roll(v, 1, 0))
    up = jnp.where(rows == v.shape[0] - 1, next_row, pltpu.roll(v, v.shape[0] - 1, 0))
    return down, up


def _conv_fwd(P, conv_w, D, name):
    T = P.shape[0]
    nt = T // ROW
    cg_p, cg_n = _halo_specs(D, 1, nt)
    vc_p, vc_n = _halo_specs(D, 2, nt)

    def body(bg_ref, cg_ref, vc_ref, cgp_ref, vcp_ref, cgn_ref, vcn_ref, w_ref, y_ref):
        i = pl.program_id(0)
        has_prev = (i != 1).astype(F32)
        has_next = (i != nt - 1).astype(F32)
        u = cg_ref[...] * vc_ref[...]
        up_row = cgp_ref[HALO - 1:HALO, :] * vcp_ref[HALO - 1:HALO, :] * has_prev
        un_row = cgn_ref[0:1, :] * vcn_ref[0:1, :] * has_next
        um1, up1 = _shift_rows(u, up_row, un_row)
        w = w_ref[...]
        conv = um1 * w[0:1, :] + u * w[1:2, :] + up1 * w[2:3, :]
        y_ref[...] = (bg_ref[...] * conv).astype(BF16)

    return pl.pallas_call(
        body, name=name, grid=(nt,),
        in_specs=[_row_spec(D, 0), _row_spec(D, 1), _row_spec(D, 2), cg_p, vc_p, cg_n, vc_n, _vec_spec(3, D)],
        out_specs=_row_spec(D),
        out_shape=jax.ShapeDtypeStruct((T, D), BF16),
        compiler_params=_params(("parallel",)),
    )(P, P, P, P, P, P, P, conv_w)


def _conv_bwd(P, dy, conv_w, D, name):
    T = P.shape[0]
    nt = T // ROW
    bg_p, bg_n = _halo_specs(D, 0, nt)
    cg_p, cg_n = _halo_specs(D, 1, nt)
    vc_p, vc_n = _halo_specs(D, 2, nt)
    dy_p, dy_n = _halo_specs(D, 0, nt)

    def body(bg_ref, cg_ref, vc_ref, dy_ref, bgp_ref, cgp_ref, vcp_ref, dyp_ref,
             bgn_ref, cgn_ref, vcn_ref, dyn_ref, w_ref, o_ref, acc_ref):
        _acc_init(acc_ref)
        i = pl.program_id(0)
        lat = (i > 0).astype(F32)
        has_prev = (i != 1).astype(F32)
        has_next = (i != nt - 1).astype(F32)
        last = HALO - 1
        bg = bg_ref[...]
        cg = cg_ref[...]
        vc = vc_ref[...]
        dyv = dy_ref[...] * lat
        u = cg * vc
        up_row = cgp_ref[last:HALO, :] * vcp_ref[last:HALO, :] * has_prev
        un_row = cgn_ref[0:1, :] * vcn_ref[0:1, :] * has_next
        um1, up1 = _shift_rows(u, up_row, un_row)
        w = w_ref[...]
        conv = um1 * w[0:1, :] + u * w[1:2, :] + up1 * w[2:3, :]
        dc = dyv * bg
        dcp_row = dyp_ref[last:HALO, :] * bgp_ref[last:HALO, :] * has_prev
        dcn_row = dyn_ref[0:1, :] * bgn_ref[0:1, :] * has_next
        dcm1, dcp1 = _shift_rows(dc, dcp_row, dcn_row)
        du = dcp1 * w[0:1, :] + dc * w[1:2, :] + dcm1 * w[2:3, :]
        o_ref[:, 0:D] = (dyv * conv).astype(BF16)
        o_ref[:, D:2 * D] = (du * vc * lat).astype(BF16)
        o_ref[:, 2 * D:3 * D] = (du * cg * lat).astype(BF16)
        _acc_add(acc_ref, 0, dc * um1)
        _acc_add(acc_ref, 1, dc * u)
        _acc_add(acc_ref, 2, dc * up1)

    return pl.pallas_call(
        body, name=name, grid=(nt,),
        in_specs=[_row_spec(D, 0), _row_spec(D, 1), _row_spec(D, 2), _row_spec(D, 0),
                  bg_p, cg_p, vc_p, dy_p, bg_n, cg_n, vc_n, dy_n, _vec_spec(3, D)],
        out_specs=[_row_spec(3 * D), _acc_spec(D)],
        out_shape=[jax.ShapeDtypeStruct((T, 3 * D), BF16), jax.ShapeDtypeStruct((2, ACC_ROWS, D), F32)],
        compiler_params=_params(("arbitrary",)),
    )(P, P, P, dy, P, P, P, dy, P, P, P, dy, conv_w)


def _rope_tables(ctx_len, seq):
    n_freq = HEAD_DIM // 4
    rows = seq // GRID_W
    row = jnp.repeat(jnp.arange(rows), GRID_W).astype(F32)
    col = jnp.tile(jnp.arange(GRID_W), rows).astype(F32)
    inv = ROPE_THETA ** (-jnp.arange(n_freq, dtype=F32) / n_freq)
    ar = row[:, None] * inv
    ac = col[:, None] * inv
    cos_t = jnp.concatenate([jnp.cos(ar), jnp.cos(ar), jnp.cos(ac), jnp.cos(ac)], axis=1)
    sin_t = jnp.concatenate([-jnp.sin(ar), jnp.sin(ar), -jnp.sin(ac), jnp.sin(ac)], axis=1)
    cos_t = jnp.concatenate([jnp.ones((ctx_len, HEAD_DIM), F32), cos_t], axis=0)
    sin_t = jnp.concatenate([jnp.zeros((ctx_len, HEAD_DIM), F32), sin_t], axis=0)
    return cos_t, sin_t


def _swap_halves(y):
    lanes = lax.broadcasted_iota(jnp.int32, y.shape, 1)
    first = (lanes % 64) < 32
    return jnp.where(first, pltpu.roll(y, HEAD_DIM - 32, 1), pltpu.roll(y, 32, 1))


def _qk_fwd(P, gq, gk, cos_t, sin_t, D, name):
    T = P.shape[0]
    QW = N_Q_HEADS * HEAD_DIM
    KW = N_KV_HEADS * HEAD_DIM
    q_col = (3 * D) // QW
    k_col = (3 * D + QW) // KW
    v_col = k_col + 1

    def body(q_ref, k_ref, v_ref, gq_ref, gk_ref, c_ref, s_ref, qo_ref, ko_ref, vo_ref):
        c = c_ref[...]
        s = s_ref[...]

        def head(x, g):
            inv = lax.rsqrt(jnp.mean(x * x, axis=-1, keepdims=True) + EPS)
            y = (x * inv) * g
            return y * c + _swap_halves(y) * s

        for h in range(N_Q_HEADS):
            sl = slice(h * HEAD_DIM, (h + 1) * HEAD_DIM)
            qo_ref[:, sl] = head(q_ref[:, sl], gq_ref[...]).astype(BF16)
        for h in range(N_KV_HEADS):
            sl = slice(h * HEAD_DIM, (h + 1) * HEAD_DIM)
            ko_ref[:, sl] = head(k_ref[:, sl], gk_ref[...]).astype(BF16)
        vo_ref[...] = v_ref[...].astype(BF16)

    return pl.pallas_call(
        body, name=name, grid=(T // ROW,),
        in_specs=[_row_spec(QW, q_col), _row_spec(KW, k_col), _row_spec(KW, v_col),
                  _vec_spec(1, HEAD_DIM), _vec_spec(1, HEAD_DIM), _row_spec(HEAD_DIM), _row_spec(HEAD_DIM)],
        out_specs=[_row_spec(QW), _row_spec(KW), _row_spec(KW)],
        out_shape=[jax.ShapeDtypeStruct((T, QW), BF16), jax.ShapeDtypeStruct((T, KW), BF16),
                   jax.ShapeDtypeStruct((T, KW), BF16)],
        compiler_params=_params(("parallel",)),
    )(P, P, P, gq, gk, cos_t, sin_t)


def _qk_bwd(P, dq, dk, dv, gq, gk, cos_t, sin_t, D, name):
    T = P.shape[0]
    QW = N_Q_HEADS * HEAD_DIM
    KW = N_KV_HEADS * HEAD_DIM
    q_col = (3 * D) // QW
    k_col = (3 * D + QW) // KW

    def body(q_ref, k_ref, dq_ref, dk_ref, dv_ref, gq_ref, gk_ref, c_ref, s_ref, o_ref, acc_ref):
        _acc_init(acc_ref)
        c = c_ref[...]
        s = s_ref[...]

        def head(x, d, g):
            dyv = d * c + _swap_halves(d * s)
            inv = lax.rsqrt(jnp.mean(x * x, axis=-1, keepdims=True) + EPS)
            xn = x * inv
            dxn = dyv * g
            dx = inv * (dxn - xn * jnp.mean(dxn * xn, axis=-1, keepdims=True))
            return dx, jnp.sum(dyv * xn, axis=0, keepdims=True)

        dgq = jnp.zeros((1, HEAD_DIM), F32)
        for h in range(N_Q_HEADS):
            sl = slice(h * HEAD_DIM, (h + 1) * HEAD_DIM)
            dx, dg = head(q_ref[:, sl], dq_ref[:, sl], gq_ref[...])
            o_ref[:, sl] = dx.astype(BF16)
            dgq = dgq + dg
        dgk = jnp.zeros((1, HEAD_DIM), F32)
        for h in range(N_KV_HEADS):
            sl = slice(h * HEAD_DIM, (h + 1) * HEAD_DIM)
            dx, dg = head(k_ref[:, sl], dk_ref[:, sl], gk_ref[...])
            o_ref[:, QW + h * HEAD_DIM:QW + (h + 1) * HEAD_DIM] = dx.astype(BF16)
            dgk = dgk + dg
        o_ref[:, QW + KW:QW + 2 * KW] = dv_ref[...].astype(BF16)
        acc_ref[0, 0:1, 0:HEAD_DIM] += dgq
        acc_ref[0, 1:2, 0:HEAD_DIM] += dgk

    return pl.pallas_call(
        body, name=name, grid=(T // ROW,),
        in_specs=[_row_spec(QW, q_col), _row_spec(KW, k_col), _row_spec(QW), _row_spec(KW), _row_spec(KW),
                  _vec_spec(1, HEAD_DIM), _vec_spec(1, HEAD_DIM), _row_spec(HEAD_DIM), _row_spec(HEAD_DIM)],
        out_specs=[_row_spec(QW + 2 * KW), _acc_spec(D)],
        out_shape=[jax.ShapeDtypeStruct((T, QW + 2 * KW), BF16), jax.ShapeDtypeStruct((2, ACC_ROWS, D), F32)],
        compiler_params=_params(("arbitrary",)),
    )(P, P, dq, dk, dv, gq, gk, cos_t, sin_t)


def _to_row(col, n):
    return jnp.transpose(jnp.broadcast_to(col, (n, HEAD_DIM)))[0:1, :]


LOG2E = 1.4426950408889634
ATTN_SPLIT = 2


def _flash_fwd(q, k, v, name, tq=ROW, tk=None):
    T = q.shape[0]
    tk = tk or _pick(T, (768, 512, 256))
    nk = T // tk
    GW = GROUP * HEAD_DIM

    def body(q_ref, k_ref, v_ref, o_ref, lse_ref, qs_ref, m_ref, l_ref, acc_ref):
        ki = pl.program_id(2)

        @pl.when(ki == 0)
        def _():
            for g in range(GROUP):
                qs_ref[g * tq:(g + 1) * tq, :] = q_ref[:, g * HEAD_DIM:(g + 1) * HEAD_DIM]
            m_ref[...] = jnp.full(m_ref.shape, -jnp.inf, F32)
            l_ref[...] = jnp.zeros(l_ref.shape, F32)
            acc_ref[...] = jnp.zeros(acc_ref.shape, F32)

        kk = k_ref[...]
        vv = v_ref[...]
        w = GROUP * tq // ATTN_SPLIT
        for part in range(ATTN_SPLIT):
            sl = slice(part * w, (part + 1) * w)
            st = lax.dot_general(kk, qs_ref[sl, :], _NT, preferred_element_type=F32) * (ATTN_SCALE * LOG2E)
            m_prev = m_ref[:, sl]
            m_new = jnp.maximum(m_prev, jnp.max(st, axis=0, keepdims=True))
            alpha = jnp.exp2(m_prev - m_new)
            pt = jnp.exp2(st - m_new)
            l_ref[:, sl] = alpha * l_ref[:, sl] + jnp.sum(pt, axis=0, keepdims=True)
            acc_ref[:, sl] = alpha * acc_ref[:, sl] + lax.dot_general(
                vv, pt.astype(BF16), (((0,), (0,)), ((), ())), preferred_element_type=F32)
            m_ref[:, sl] = m_new

        @pl.when(ki == nk - 1)
        def _():
            out = jnp.transpose(acc_ref[...] / l_ref[...])
            lse = m_ref[...] + jnp.log2(l_ref[...])
            for g in range(GROUP):
                o_ref[:, g * HEAD_DIM:(g + 1) * HEAD_DIM] = out[g * tq:(g + 1) * tq, :]
                lse_ref[0, g:g + 1, :] = lse[:, g * tq:(g + 1) * tq]

    return pl.pallas_call(
        body, name=name, grid=(N_KV_HEADS, T // tq, nk),
        in_specs=[pl.BlockSpec((tq, GW), lambda h, i, j: (i, h)),
                  pl.BlockSpec((tk, HEAD_DIM), lambda h, i, j: (j, h)),
                  pl.BlockSpec((tk, HEAD_DIM), lambda h, i, j: (j, h))],
        out_specs=[pl.BlockSpec((tq, GW), lambda h, i, j: (i, h)),
                   pl.BlockSpec((1, GROUP, tq), lambda h, i, j: (h, 0, i))],
        out_shape=[jax.ShapeDtypeStruct((T, N_Q_HEADS * HEAD_DIM), F32),
                   jax.ShapeDtypeStruct((N_KV_HEADS, GROUP, T), F32)],
        scratch_shapes=[pltpu.VMEM((GROUP * tq, HEAD_DIM), BF16), pltpu.VMEM((1, GROUP * tq), F32),
                        pltpu.VMEM((1, GROUP * tq), F32), pltpu.VMEM((HEAD_DIM, GROUP * tq), F32)],
        compiler_params=_params(("parallel", "parallel", "arbitrary")),
    )(q, k, v)


def _attn_delta(do, o, name):
    T, QW = do.shape

    def body(do_ref, o_ref, dob_ref, dl_ref):
        dov = do_ref[...]
        dob_ref[...] = dov.astype(BF16)
        prod = dov * o_ref[...]
        for h in range(N_Q_HEADS):
            d = jnp.sum(prod[:, h * HEAD_DIM:(h + 1) * HEAD_DIM], axis=1, keepdims=True)
            dl_ref[h // GROUP, (h % GROUP):(h % GROUP) + 1, :] = _to_row(d, ROW)

    return pl.pallas_call(
        body, name=name, grid=(T // ROW,),
        in_specs=[_row_spec(QW), _row_spec(QW)],
        out_specs=[_row_spec(QW), pl.BlockSpec((N_KV_HEADS, GROUP, ROW), lambda i: (0, 0, i))],
        out_shape=[jax.ShapeDtypeStruct((T, QW), BF16), jax.ShapeDtypeStruct((N_KV_HEADS, GROUP, T), F32)],
        compiler_params=_params(("parallel",)),
    )(do, o)


def _flash_bwd(q, k, v, do, lse, delta, name, tq=ROW, tk=None):
    T = q.shape[0]
    tk = tk or _pick(T, (768, 512, 256))
    nk = T // tk
    GW = GROUP * HEAD_DIM
    nt = (((1,), (1,)), ((), ()))

    def body(q_ref, do_ref, k_ref, v_ref, lse_ref, dl_ref, dq_ref, dk_ref, dv_ref, qs_ref, dos_ref, dqt_ref):
        qi = pl.program_id(1)
        ki = pl.program_id(2)

        @pl.when(ki == 0)
        def _():
            for g in range(GROUP):
                qs_ref[g * tq:(g + 1) * tq, :] = q_ref[:, g * HEAD_DIM:(g + 1) * HEAD_DIM]
                dos_ref[g * tq:(g + 1) * tq, :] = do_ref[:, g * HEAD_DIM:(g + 1) * HEAD_DIM]
            dqt_ref[...] = jnp.zeros(dqt_ref.shape, F32)

        kk = k_ref[...]
        vv = v_ref[...]
        heads = GROUP // ATTN_SPLIT
        w = heads * tq
        dk_c = jnp.zeros((tk, HEAD_DIM), F32)
        dv_c = jnp.zeros((tk, HEAD_DIM), F32)
        for part in range(ATTN_SPLIT):
            sl = slice(part * w, (part + 1) * w)
            hs = range(part * heads, (part + 1) * heads)
            lse_row = jnp.concatenate([lse_ref[0, g:g + 1, :] for g in hs], axis=1)
            dl_row = jnp.concatenate([dl_ref[0, g:g + 1, :] for g in hs], axis=1)
            qs = qs_ref[sl, :]
            dos = dos_ref[sl, :]
            st = lax.dot_general(kk, qs, nt, preferred_element_type=F32) * (ATTN_SCALE * LOG2E)
            pt = jnp.exp2(st - lse_row)
            dpt = lax.dot_general(vv, dos, nt, preferred_element_type=F32)
            dst = ((pt * (dpt - dl_row)) * ATTN_SCALE).astype(BF16)
            dv_c = dv_c + jnp.dot(pt.astype(BF16), dos, preferred_element_type=F32)
            dk_c = dk_c + jnp.dot(dst, qs, preferred_element_type=F32)
            dqt_ref[:, sl] += lax.dot_general(kk, dst, (((0,), (0,)), ((), ())), preferred_element_type=F32)
        rows = pl.ds(pl.multiple_of(ki * tk, tk), tk)

        @pl.when(qi == 0)
        def _():
            dk_ref[rows, :] = dk_c
            dv_ref[rows, :] = dv_c

        @pl.when(qi > 0)
        def _():
            dk_ref[rows, :] += dk_c
            dv_ref[rows, :] += dv_c

        @pl.when(ki == nk - 1)
        def _():
            dqv = jnp.transpose(dqt_ref[...])
            for g in range(GROUP):
                dq_ref[:, g * HEAD_DIM:(g + 1) * HEAD_DIM] = dqv[g * tq:(g + 1) * tq, :]

    return pl.pallas_call(
        body, name=name, grid=(N_KV_HEADS, T // tq, nk),
        in_specs=[pl.BlockSpec((tq, GW), lambda h, i, j: (i, h)),
                  pl.BlockSpec((tq, GW), lambda h, i, j: (i, h)),
                  pl.BlockSpec((tk, HEAD_DIM), lambda h, i, j: (j, h)),
                  pl.BlockSpec((tk, HEAD_DIM), lambda h, i, j: (j, h)),
                  pl.BlockSpec((1, GROUP, tq), lambda h, i, j: (h, 0, i)),
                  pl.BlockSpec((1, GROUP, tq), lambda h, i, j: (h, 0, i))],
        out_specs=[pl.BlockSpec((tq, GW), lambda h, i, j: (i, h)),
                   pl.BlockSpec((T, HEAD_DIM), lambda h, i, j: (0, h)),
                   pl.BlockSpec((T, HEAD_DIM), lambda h, i, j: (0, h))],
        out_shape=[jax.ShapeDtypeStruct((T, N_Q_HEADS * HEAD_DIM), F32),
                   jax.ShapeDtypeStruct((T, N_KV_HEADS * HEAD_DIM), F32),
                   jax.ShapeDtypeStruct((T, N_KV_HEADS * HEAD_DIM), F32)],
        scratch_shapes=[pltpu.VMEM((GROUP * tq, HEAD_DIM), BF16), pltpu.VMEM((GROUP * tq, HEAD_DIM), BF16),
                        pltpu.VMEM((HEAD_DIM, GROUP * tq), F32)],
        compiler_params=_params(("arbitrary", "arbitrary", "arbitrary")),
    )(q, do, k, v, lse, delta)


def _gate_specs(D):
    w = D // 2
    first = (3 * D + (N_Q_HEADS + 2 * N_KV_HEADS) * HEAD_DIM) // w
    return [pl.BlockSpec((ROW, w), lambda i, c=first + j: (i, c)) for j in range(4)]


def _merge_fwd(a1, a2, P, D, name):
    T = a1.shape[0]
    w = D // 2

    def body(a1_ref, a2_ref, g0, g1, g2, g3, z_ref):
        for j, (gc, ga) in enumerate(((g0, g2), (g1, g3))):
            sl = slice(j * w, (j + 1) * w)
            z = jax.nn.sigmoid(gc[...]) * a1_ref[:, sl] + jax.nn.sigmoid(ga[...]) * a2_ref[:, sl]
            z_ref[:, sl] = z.astype(BF16)

    return pl.pallas_call(
        body, name=name, grid=(T // ROW,),
        in_specs=[_row_spec(D), _row_spec(D)] + _gate_specs(D),
        out_specs=_row_spec(D), out_shape=jax.ShapeDtypeStruct((T, D), BF16),
        compiler_params=_params(("parallel",)),
    )(a1, a2, P, P, P, P)


def _merge_bwd(dz, a1, a2, P, D, name):
    T = a1.shape[0]
    w = D // 2

    def body(dz_ref, a1_ref, a2_ref, g0, g1, g2, g3, d1_ref, d2_ref, dg_ref):
        for j, (gc, ga) in enumerate(((g0, g2), (g1, g3))):
            sl = slice(j * w, (j + 1) * w)
            dz = dz_ref[:, sl]
            sc = jax.nn.sigmoid(gc[...])
            sa = jax.nn.sigmoid(ga[...])
            d1_ref[:, sl] = (dz * sc).astype(BF16)
            d2_ref[:, sl] = (dz * sa).astype(BF16)
            dg_ref[:, j * w:(j + 1) * w] = (dz * a1_ref[:, sl] * (sc * (1.0 - sc))).astype(BF16)
            dg_ref[:, D + j * w:D + (j + 1) * w] = (dz * a2_ref[:, sl] * (sa * (1.0 - sa))).astype(BF16)

    return pl.pallas_call(
        body, name=name, grid=(T // ROW,),
        in_specs=[_row_spec(D), _row_spec(D), _row_spec(D)] + _gate_specs(D),
        out_specs=[_row_spec(D), _row_spec(D), _row_spec(2 * D)],
        out_shape=[jax.ShapeDtypeStruct((T, D), BF16), jax.ShapeDtypeStruct((T, D), BF16),
                   jax.ShapeDtypeStruct((T, 2 * D), BF16)],
        compiler_params=_params(("parallel",)),
    )(dz, a1, a2, P, P, P, P)


def _adamw_math(w, g, m, v):
    m = ADAM_B1 * m + (1.0 - ADAM_B1) * g
    v = ADAM_B2 * v + (1.0 - ADAM_B2) * (g * g)
    m_hat = m / (1.0 - ADAM_B1 ** ADAM_STEP)
    v_hat = v / (1.0 - ADAM_B2 ** ADAM_STEP)
    delta = -ADAM_LR * (m_hat / (jnp.sqrt(v_hat) + ADAM_EPS) + ADAM_WD * w)
    return delta, m, v


def _adamw(w, g, m, v, name):
    R, C = w.shape
    tr = _pick(R, tuple(t for t in (256, 128, 64, 32, 16, 8) if t * C * 4 <= ADAMW_BLOCK_BYTES))

    def body(w_ref, g_ref, m_ref, v_ref, d_ref, mo_ref, vo_ref):
        d, mn, vn = _adamw_math(w_ref[...], g_ref[...], m_ref[...], v_ref[...])
        d_ref[...] = d
        mo_ref[...] = mn
        vo_ref[...] = vn

    spec = pl.BlockSpec((tr, C), lambda i: (i, 0))
    return pl.pallas_call(
        body, name=name, grid=(R // tr,),
        in_specs=[spec] * 4, out_specs=[spec] * 3,
        out_shape=[jax.ShapeDtypeStruct((R, C), F32)] * 3,
        compiler_params=_params(("parallel",)),
    )(w, g, m, v)


def _local_step(xcat, target, mods, norm_g, final_g, gq, gk, conv_w, wts, ctx_len):
    T, D = xcat.shape
    w1i, w1o, wi, wbc, wba, wo, w2i, w2o = wts
    g1, g2, g3 = norm_g
    cos_t, sin_t = _rope_tables(ctx_len, T - ctx_len)

    _, h1 = _resid_rmsmod_fwd(xcat, None, mods, g1, None, 0, 1, "f_norm1")
    u1, s1, f1 = _ffn_fwd(h1, w1i, w1o, "f_ffn1")
    x1, h2 = _resid_rmsmod_fwd(xcat, f1, mods, g2, (2, 0.5), 3, 4, "f_norm2")
    P = _matmul(h2, wi, "nt", F32, "f_mix_in")
    yc = _conv_fwd(P, conv_w, D, "f_conv")
    qn, kn, vb = _qk_fwd(P, gq, gk, cos_t, sin_t, D, "f_qk")
    o, lse = _flash_fwd(qn, kn, vb, "f_attn")
    a1 = _matmul(yc, wbc, "nn", F32, "f_branch_conv")
    a2 = _matmul(o, wba, "nn", F32, "f_branch_attn")
    z = _merge_fwd(a1, a2, P, D, "f_merge")
    mo = _matmul(z, wo, "nn", F32, "f_mix_out")
    x2, h3 = _resid_rmsmod_fwd(x1, mo, mods, g3, (5, 1.0), 6, 7, "f_norm3")
    u2, s2, f2 = _ffn_fwd(h3, w2i, w2o, "f_ffn2")
    dx3, df2, acc_head = _loss_head(x2, f2, mods, final_g, target, "loss_head")

    du2, dh3 = _ffn_bwd(df2, u2, w2i, w2o, "b_ffn2")
    g_w2o = _matmul(s2, df2, "tn", BF16, "b_ffn2_out_dw")
    g_w2i = _matmul(du2, h3, "tn", BF16, "b_ffn2_in_dw")
    dx2, dmo, acc_n3 = _rmsmod_bwd(x2, dh3, dx3, mods, g3, 6, 7, (5, 1.0), mo, "b_norm3")

    dz = _matmul(dmo, wo, "nt", F32, "b_mix_out_dx")
    g_wo = _matmul(z, dmo, "tn", BF16, "b_mix_out_dw")
    da1, da2, dgt = _merge_bwd(dz, a1, a2, P, D, "b_merge")
    dyc = _matmul(da1, wbc, "nt", F32, "b_branch_conv_dx")
    do = _matmul(da2, wba, "nt", F32, "b_branch_attn_dx")
    g_wbc = _matmul(yc, da1, "tn", BF16, "b_branch_conv_dw")
    g_wba = _matmul(o, da2, "tn", BF16, "b_branch_attn_dw")
    dob, delta = _attn_delta(do, o, "b_attn_delta")
    dq, dk, dv = _flash_bwd(qn, kn, vb, dob, lse, delta, "b_attn")
    dqkv, acc_qk = _qk_bwd(P, dq, dk, dv, gq, gk, cos_t, sin_t, D, "b_qk")
    dconv, acc_conv = _conv_bwd(P, dyc, conv_w, D, "b_conv")
    dP = jnp.concatenate([dconv, dqkv, dgt], axis=1)
    dh2 = _matmul(dP, wi, "nn", F32, "b_mix_in_dx")
    g_wi = _matmul(dP, h2, "tn", BF16, "b_mix_in_dw")
    dx1, df1, acc_n2 = _rmsmod_bwd(x1, dh2, dx2, mods, g2, 3, 4, (2, 0.5), f1, "b_norm2")

    du1, dh1 = _ffn_bwd(df1, u1, w1i, w1o, "b_ffn1")
    g_w1o = _matmul(s1, df1, "tn", BF16, "b_ffn1_out_dw")
    g_w1i = _matmul(du1, h1, "tn", BF16, "b_ffn1_in_dw")
    grad_x, _, acc_n1 = _rmsmod_bwd(xcat, dh1, dx1, mods, g1, 0, 1, None, None, "b_norm1", skip_first_tile=True)

    grads = (g_w1i, g_w1o, g_wi, g_wbc, g_wba, g_wo, g_w2i, g_w2o)
    accs = (acc_head, acc_n3, acc_n2, acc_n1, acc_conv, acc_qk)
    return grad_x, grads, accs


def _place():
    return lax.axis_index("x"), lax.axis_index("y"), lax.axis_index("c")


def _other_chips(x, y):
    return [(1 - x, y), (x, 1 - y), (1 - x, 1 - y)]


def _allgather8(v, name):
    R, N = v.shape

    def body(v_ref, out_ref, send_sems, recv_sems, local_sem):
        x, y, c = _place()
        me, sibling = (x, y, c), (x, y, 1 - c)
        chips = _other_chips(x, y)

        def blk(px, py, pc):
            return out_ref.at[4 * px + 2 * py + pc]

        def copy(k, block, to, src=None):
            return pltpu.make_async_remote_copy(
                src_ref=blk(*block) if src is None else src, dst_ref=blk(*block),
                send_sem=send_sems.at[k], recv_sem=recv_sems.at[k], device_id=to, device_id_type=MESH)

        mine = pltpu.make_async_copy(v_ref, blk(*me), local_sem)
        mine.start()
        first = [copy(0, me, sibling, src=v_ref)]
        first += [copy(1 + j, me, (*chip, c), src=v_ref) for j, chip in enumerate(chips)]
        for cp in first:
            cp.start()
        passed = [copy(4 + j, (*chip, c), sibling) for j, chip in enumerate(chips)]
        for j, chip in enumerate(chips):
            copy(1 + j, (*chip, c), me).wait_recv()
            passed[j].start()
        copy(0, sibling, me).wait_recv()
        for j, chip in enumerate(chips):
            copy(4 + j, (*chip, 1 - c), me).wait_recv()
        for cp in first + passed:
            cp.wait_send()
        mine.wait()

    return pl.pallas_call(
        body, name=name,
        out_shape=jax.ShapeDtypeStruct((N_DEV, R, N), v.dtype),
        in_specs=[pl.BlockSpec(memory_space=pltpu.VMEM)],
        out_specs=pl.BlockSpec(memory_space=pltpu.VMEM),
        scratch_shapes=[pltpu.SemaphoreType.DMA((7,)), pltpu.SemaphoreType.DMA((7,)), pltpu.SemaphoreType.DMA],
        compiler_params=pltpu.CompilerParams(vmem_limit_bytes=VMEM_LIMIT),
    )(v)


def _any_specs(n):
    return [pl.BlockSpec(memory_space=pl.ANY)] * n


def _weights_allgather(fulls, name):
    n = len(fulls)

    def body(*refs):
        full = refs[n:2 * n]
        send_sems, recv_sems = refs[2 * n:]
        x, y, c = _place()
        sibling = (x, y, 1 - c)
        chips = _other_chips(x, y)

        def piece(t, px, py, h):
            rs = fulls[t].shape[0] // N_CHIPS
            return full[t].at[pl.ds((2 * px + py) * rs + h * (rs // 2), rs // 2), :]

        def copy(k, t, block, to):
            return pltpu.make_async_remote_copy(
                src_ref=piece(t, *block), dst_ref=piece(t, *block),
                send_sem=send_sems.at[k], recv_sem=recv_sems.at[k], device_id=to, device_id_type=MESH)

        first = []
        for t in range(n):
            for j, chip in enumerate(chips):
                cp = copy(3 * t + j, t, (x, y, c), (*chip, c))
                cp.start()
                first.append(cp)
        passed = []
        for t in range(n):
            for j, chip in enumerate(chips):
                copy(3 * t + j, t, (*chip, c), (x, y, c)).wait_recv()
                cp = copy(3 * n + 3 * t + j, t, (*chip, c), sibling)
                cp.start()
                passed.append(cp)
        for t in range(n):
            for j, chip in enumerate(chips):
                copy(3 * n + 3 * t + j, t, (*chip, 1 - c), (x, y, c)).wait_recv()
        for cp in first + passed:
            cp.wait_send()

    return pl.pallas_call(
        body, name=name,
        out_shape=[jax.ShapeDtypeStruct(f.shape, f.dtype) for f in fulls],
        in_specs=_any_specs(n), out_specs=_any_specs(n),
        input_output_aliases={t: t for t in range(n)},
        scratch_shapes=[pltpu.SemaphoreType.DMA((6 * n,)), pltpu.SemaphoreType.DMA((6 * n,))],
    )(*fulls)


def _pair_exchange(grads, name):
    n = len(grads)

    def body(*refs):
        g, land = refs[:n], refs[n:2 * n]
        send_sems, recv_sems = refs[2 * n:]
        x, y, c = _place()
        sibling = (x, y, 1 - c)
        copies = []
        for t in range(n):
            half = grads[t].shape[0] // (2 * N_CHIPS)
            for s in range(N_CHIPS):
                cp = pltpu.make_async_remote_copy(
                    src_ref=g[t].at[pl.ds((2 * s + 1 - c) * half, half), :], dst_ref=land[t].at[s],
                    send_sem=send_sems.at[N_CHIPS * t + s], recv_sem=recv_sems.at[N_CHIPS * t + s],
                    device_id=sibling, device_id_type=MESH)
                cp.start()
                copies.append(cp)
        for cp in copies:
            cp.wait_recv()
        for cp in copies:
            cp.wait_send()

    return pl.pallas_call(
        body, name=name,
        out_shape=[jax.ShapeDtypeStruct((N_CHIPS, a.shape[0] // (2 * N_CHIPS), a.shape[1]), a.dtype) for a in grads],
        in_specs=_any_specs(n), out_specs=_any_specs(n),
        scratch_shapes=[pltpu.SemaphoreType.DMA((N_CHIPS * n,)), pltpu.SemaphoreType.DMA((N_CHIPS * n,))],
    )(*grads)


def _pair_sum(g, landed, idx, name):
    _, half, D = landed.shape
    g4 = g.reshape(N_CHIPS, 2, half, D)
    tr = _pick(half, (416, 352, 128))

    def body(idx_ref, g_ref, l_ref, o_ref):
        o_ref[...] = (g_ref[0].astype(F32) + l_ref[...].astype(F32)).astype(BF16)

    return pl.pallas_call(
        body, name=name,
        grid_spec=pltpu.PrefetchScalarGridSpec(
            num_scalar_prefetch=1, grid=(N_CHIPS, half // tr),
            in_specs=[pl.BlockSpec((1, 1, tr, D), lambda s, i, idx: (idx[1 + s], idx[0], i, 0)),
                      pl.BlockSpec((1, tr, D), lambda s, i, idx: (idx[1 + s], i, 0))],
            out_specs=pl.BlockSpec((1, tr, D), lambda s, i, idx: (s, i, 0))),
        out_shape=jax.ShapeDtypeStruct((N_CHIPS, half, D), BF16),
        compiler_params=_params(("arbitrary", "arbitrary")),
    )(idx, g4, landed)


def _chip_exchange(sums, name):
    n = len(sums)

    def body(*refs):
        ps, land = refs[:n], refs[n:2 * n]
        send_sems, recv_sems = refs[2 * n:]
        x, y, c = _place()
        copies = []
        for t in range(n):
            for j, chip in enumerate(_other_chips(x, y)):
                cp = pltpu.make_async_remote_copy(
                    src_ref=ps[t].at[1 + j], dst_ref=land[t].at[j],
                    send_sem=send_sems.at[3 * t + j], recv_sem=recv_sems.at[3 * t + j],
                    device_id=(*chip, c), device_id_type=MESH)
                cp.start()
                copies.append(cp)
        for cp in copies:
            cp.wait_recv()
        for cp in copies:
            cp.wait_send()

    return pl.pallas_call(
        body, name=name,
        out_shape=[jax.ShapeDtypeStruct((3,) + a.shape[1:], a.dtype) for a in sums],
        in_specs=_any_specs(n), out_specs=_any_specs(n),
        scratch_shapes=[pltpu.SemaphoreType.DMA((3 * n,)), pltpu.SemaphoreType.DMA((3 * n,))],
    )(*sums)


def _chip_sum(ps, landed, name):
    _, half, D = ps.shape
    tr = _pick(half, (416, 352, 128))

    def body(p_ref, l_ref, o_ref):
        acc = p_ref[0].astype(F32)
        for j in range(3):
            acc = acc + l_ref[j].astype(F32)
        o_ref[...] = acc

    return pl.pallas_call(
        body, name=name, grid=(half // tr,),
        in_specs=[pl.BlockSpec((1, tr, D), lambda i: (0, i, 0)), pl.BlockSpec((3, tr, D), lambda i: (0, i, 0))],
        out_specs=pl.BlockSpec((tr, D), lambda i: (i, 0)),
        out_shape=jax.ShapeDtypeStruct((half, D), F32),
        compiler_params=_params(("parallel",)),
    )(ps, landed)


def _pair_swap(halves, name):
    n = len(halves)

    def body(*refs):
        hv, other = refs[:n], refs[n:2 * n]
        send_sems, recv_sems = refs[2 * n:]
        x, y, c = _place()
        copies = []
        for t in range(n):
            cp = pltpu.make_async_remote_copy(src_ref=hv[t], dst_ref=other[t], send_sem=send_sems.at[t],
                                              recv_sem=recv_sems.at[t], device_id=(x, y, 1 - c), device_id_type=MESH)
            cp.start()
            copies.append(cp)
        for cp in copies:
            cp.wait_recv()
        for cp in copies:
            cp.wait_send()

    return pl.pallas_call(
        body, name=name,
        out_shape=[jax.ShapeDtypeStruct(a.shape, a.dtype) for a in halves],
        in_specs=_any_specs(n), out_specs=_any_specs(n),
        scratch_shapes=[pltpu.SemaphoreType.DMA((n,)), pltpu.SemaphoreType.DMA((n,))],
    )(*halves)


def _reduce_scatter(grads, idx):
    landed = _pair_exchange(grads, "rs_pair_exchange")
    sums = [_pair_sum(g, l, idx, f"rs_pair_sum_{t}") for t, (g, l) in enumerate(zip(grads, landed))]
    landed2 = _chip_exchange(sums, "rs_chip_exchange")
    halves = [_chip_sum(p, l, f"rs_chip_sum_{t}") for t, (p, l) in enumerate(zip(sums, landed2))]
    others = _pair_swap(halves, "rs_pair_swap")
    south = lax.axis_index("c") == 0
    return [jnp.where(south, jnp.concatenate([h, o], axis=0), jnp.concatenate([o, h], axis=0))
            for h, o in zip(halves, others)]


N_MOD = 9
PACK_HEAD, PACK_N3, PACK_N2, PACK_N1, PACK_CONV, PACK_QK = 0, 16, 32, 48, 64, 80
PACK_ROWS = 96
MOD_SRC = ((PACK_N1, 0), (PACK_N1, 1), (PACK_N2, 3), (PACK_N2, 0), (PACK_N2, 1),
           (PACK_N3, 3), (PACK_N3, 0), (PACK_N3, 1), (PACK_HEAD, 2))
CTX_ROW = 8


def _silu(v):
    return v * jax.nn.sigmoid(v)


def _whole(n):
    return [pl.BlockSpec(memory_space=pltpu.VMEM)] * n


def _mod_rows(cin, w_sh, b_sh, name):
    def body(c_ref, w_ref, b_ref, o_ref):
        a = _silu(c_ref[...]).astype(BF16)
        o_ref[...] = jnp.dot(a, w_ref[...].astype(BF16), preferred_element_type=F32) + b_ref[...]

    return pl.pallas_call(
        body, name=name, out_shape=jax.ShapeDtypeStruct((cin.shape[0], w_sh.shape[1]), F32),
        in_specs=_whole(3), out_specs=pl.BlockSpec(memory_space=pltpu.VMEM),
        compiler_params=pltpu.CompilerParams(vmem_limit_bytes=VMEM_LIMIT),
    )(cin, w_sh, b_sh)


def _small_reduce(gathered, name):
    _, _, D = gathered.shape

    def body(g_ref, loss_ref, db_ref, gn_ref, cv_ref, qk_ref, dm_ref):
        tot = g_ref[0]
        for r in range(1, N_DEV):
            tot = tot + g_ref[r]

        def both(block, row):
            return tot[block + row:block + row + 1, :] + tot[block + 8 + row:block + 8 + row + 1, :]

        loss = jnp.sum(both(PACK_HEAD, 0), axis=1, keepdims=True)
        loss_ref[...] = jnp.broadcast_to(loss, loss_ref.shape)
        db_ref[...] = jnp.zeros(db_ref.shape, F32)
        dm_ref[...] = jnp.zeros(dm_ref.shape, F32)
        for j, (block, row) in enumerate(MOD_SRC):
            db_ref[j:j + 1, :] = both(block, row)
            dm_ref[CTX_ROW, j:j + 1, :] = tot[block + row:block + row + 1, :]
            for r in range(N_DEV):
                dm_ref[r, j:j + 1, :] = g_ref[r, block + 8 + row:block + 8 + row + 1, :]
        gn_ref[...] = jnp.zeros(gn_ref.shape, F32)
        gn_ref[0:1, :] = both(PACK_N1, 2)
        gn_ref[8:9, :] = both(PACK_N2, 2)
        gn_ref[16:17, :] = both(PACK_N3, 2)
        gn_ref[24:25, :] = both(PACK_HEAD, 1)
        cv_ref[...] = jnp.zeros(cv_ref.shape, F32)
        for r in range(3):
            cv_ref[r:r + 1, :] = both(PACK_CONV, r)
        qk_ref[...] = jnp.zeros(qk_ref.shape, F32)
        qk_ref[0:1, 0:HEAD_DIM] = both(PACK_QK, 0)[:, 0:HEAD_DIM]
        qk_ref[0:1, HEAD_DIM:2 * HEAD_DIM] = both(PACK_QK, 1)[:, 0:HEAD_DIM]

    return pl.pallas_call(
        body, name=name,
        out_shape=[jax.ShapeDtypeStruct((8, 128), F32), jax.ShapeDtypeStruct((16, D), F32),
                   jax.ShapeDtypeStruct((32, D), F32), jax.ShapeDtypeStruct((8, D), F32),
                   jax.ShapeDtypeStruct((8, D), F32), jax.ShapeDtypeStruct((16, 16, D), F32)],
        in_specs=_whole(1), out_specs=_whole(6),
        compiler_params=pltpu.CompilerParams(vmem_limit_bytes=VMEM_LIMIT),
    )(gathered)


def _wmod_grad(cin, dm_sh, w_sh, name):
    def body(c_ref, d_ref, w_ref, gw_ref, cp_ref):
        a = _silu(c_ref[...]).astype(BF16)
        d = d_ref[...].astype(BF16)
        gw_ref[...] = lax.dot_general(a, d, (((0,), (0,)), ((), ())), preferred_element_type=F32)
        cp_ref[...] = lax.dot_general(d, w_ref[...].astype(BF16), (((1,), (1,)), ((), ())),
                                      preferred_element_type=F32)

    return pl.pallas_call(
        body, name=name,
        out_shape=[jax.ShapeDtypeStruct(w_sh.shape, F32), jax.ShapeDtypeStruct(cin.shape, F32)],
        in_specs=_whole(3), out_specs=_whole(2),
        compiler_params=pltpu.CompilerParams(vmem_limit_bytes=VMEM_LIMIT),
    )(cin, dm_sh, w_sh)


def _cctx_grad(parts, c_ctx8, name):
    def body(p_ref, c_ref, o_ref):
        tot = p_ref[0] + p_ref[2] + p_ref[4] + p_ref[6]
        cv = c_ref[...]
        sig = jax.nn.sigmoid(cv)
        rows = lax.broadcasted_iota(jnp.int32, tot.shape, 0)
        o_ref[...] = jnp.where(rows == 0, tot * (sig * (1.0 + cv * (1.0 - sig))), 0.0)

    return pl.pallas_call(
        body, name=name, out_shape=jax.ShapeDtypeStruct(c_ctx8.shape, F32),
        in_specs=_whole(2), out_specs=pl.BlockSpec(memory_space=pltpu.VMEM),
    )(parts, c_ctx8)


def _pad_rows(a, rows):
    return jnp.pad(a, ((0, rows - a.shape[0]), (0, 0)))


def _pack_small(c_ctx, b_mod, n1, n2, n3, final_g, gq, gk, conv_sh, D):
    misc = jnp.concatenate([gq, gk, conv_sh.reshape(1, -1)], axis=1)
    return jnp.concatenate([_pad_rows(c_ctx[None], 8), _pad_rows(b_mod.reshape(N_MOD, D), 16), _pad_rows(n1, 8),
                            _pad_rows(n2, 8), _pad_rows(n3, 8), _pad_rows(final_g[None], 8), _pad_rows(misc, 8)], axis=0)


def _unpack_small(p, D, conv_shape):
    misc = p[56:57]
    return dict(c_ctx=p[0], b_mod=p[8:8 + N_MOD].reshape(1, N_MOD * D), norm1_g=p[24:25], norm2_g=p[32:33],
                norm3_g=p[40:41], final_g=p[48], q_norm_g=misc[:, 0:HEAD_DIM], k_norm_g=misc[:, HEAD_DIM:2 * HEAD_DIM],
                conv_w=misc[:, 2 * HEAD_DIM:].reshape(conv_shape))


WEIGHT_ORDER = ("c_ctx", "w_mod", "b_mod", "norm1_g", "norm2_g", "norm3_g", "ffn1_w_in", "ffn1_w_out", "w_in",
                "conv_w", "q_norm_g", "k_norm_g", "w_branch_conv", "w_branch_attn", "w_out", "ffn2_w_in",
                "ffn2_w_out", "final_g")
BIG = ("ffn1_w_in", "ffn1_w_out", "w_in", "w_branch_conv", "w_branch_attn", "w_out", "ffn2_w_in", "ffn2_w_out")
COLUMN_SHARDED = ("ffn1_w_in", "w_in", "ffn2_w_in")


def kernel(x, c, ctx, c_ctx, w_mod, b_mod, norm1_g, norm2_g, norm3_g, ffn1_w_in, ffn1_w_out, w_in, conv_w, q_norm_g, k_norm_g, w_branch_conv, w_branch_attn, w_out, ffn2_w_in, ffn2_w_out, final_g, loss_target, m_c_ctx, m_w_mod, m_b_mod, m_norm1_g, m_norm2_g, m_norm3_g, m_ffn1_w_in, m_ffn1_w_out, m_w_in, m_conv_w, m_q_norm_g, m_k_norm_g, m_w_branch_conv, m_w_branch_attn, m_w_out, m_ffn2_w_in, m_ffn2_w_out, m_final_g, v_c_ctx, v_w_mod, v_b_mod, v_norm1_g, v_norm2_g, v_norm3_g, v_ffn1_w_in, v_ffn1_w_out, v_w_in, v_conv_w, v_q_norm_g, v_k_norm_g, v_w_branch_conv, v_w_branch_attn, v_w_out, v_ffn2_w_in, v_ffn2_w_out, v_final_g):
    w = dict(c_ctx=c_ctx, w_mod=w_mod, b_mod=b_mod, norm1_g=norm1_g, norm2_g=norm2_g, norm3_g=norm3_g,
             ffn1_w_in=ffn1_w_in, ffn1_w_out=ffn1_w_out, w_in=w_in, conv_w=conv_w, q_norm_g=q_norm_g,
             k_norm_g=k_norm_g, w_branch_conv=w_branch_conv, w_branch_attn=w_branch_attn, w_out=w_out,
             ffn2_w_in=ffn2_w_in, ffn2_w_out=ffn2_w_out, final_g=final_g)
    m = dict(c_ctx=m_c_ctx, w_mod=m_w_mod, b_mod=m_b_mod, norm1_g=m_norm1_g, norm2_g=m_norm2_g, norm3_g=m_norm3_g,
             ffn1_w_in=m_ffn1_w_in, ffn1_w_out=m_ffn1_w_out, w_in=m_w_in, conv_w=m_conv_w, q_norm_g=m_q_norm_g,
             k_norm_g=m_k_norm_g, w_branch_conv=m_w_branch_conv, w_branch_attn=m_w_branch_attn, w_out=m_w_out,
             ffn2_w_in=m_ffn2_w_in, ffn2_w_out=m_ffn2_w_out, final_g=m_final_g)
    v = dict(c_ctx=v_c_ctx, w_mod=v_w_mod, b_mod=v_b_mod, norm1_g=v_norm1_g, norm2_g=v_norm2_g, norm3_g=v_norm3_g,
             ffn1_w_in=v_ffn1_w_in, ffn1_w_out=v_ffn1_w_out, w_in=v_w_in, conv_w=v_conv_w, q_norm_g=v_q_norm_g,
             k_norm_g=v_k_norm_g, w_branch_conv=v_w_branch_conv, w_branch_attn=v_w_branch_attn, w_out=v_w_out,
             ffn2_w_in=v_ffn2_w_in, ffn2_w_out=v_ffn2_w_out, final_g=v_final_g)

    xi, yi, ci = _place()
    dev = 4 * xi + 2 * yi + ci
    shard = 2 * xi + yi
    idx = jnp.stack([ci, shard, 2 * (1 - xi) + yi, 2 * xi + (1 - yi), 2 * (1 - xi) + (1 - yi)]).astype(jnp.int32)
    D = x.shape[-1]
    ctx_len = ctx.shape[1]
    assert ctx_len == ROW and c.shape == (1, D)
    mcols = w_mod.shape[2]
    ccols = conv_w.shape[2]

    c_all = _allgather8(jnp.broadcast_to(c, (8, D)), "ag_c")[:, 0, :]
    cin = jnp.concatenate([c_all, _pad_rows(c_ctx[None], 8)], axis=0)
    b_sh = lax.dynamic_slice(b_mod, (0, shard * mcols), (1, mcols))
    mod_sh = _mod_rows(cin, w_mod[0], b_sh, "mod_rows")
    conv_rows = jnp.pad(conv_w[0], ((0, 8 - conv_w.shape[1]), (0, mcols - ccols)))
    mod_all = _allgather8(jnp.concatenate([mod_sh, conv_rows], axis=0), "ag_mod")
    mod_full = jnp.concatenate([mod_all[2 * s, :16] for s in range(N_CHIPS)], axis=1)
    conv_full = jnp.concatenate([mod_all[2 * s, 16:16 + conv_w.shape[1], :ccols] for s in range(N_CHIPS)], axis=1)
    mod_lat = lax.dynamic_slice(mod_full, (dev, 0), (1, N_MOD * D)).reshape(N_MOD, D)
    mod_ctx = mod_full[CTX_ROW].reshape(N_MOD, D)
    mods = jnp.stack([_pad_rows(mod_ctx, 16), _pad_rows(mod_lat, 16)])

    shards = [w[n][0].T.astype(BF16) if n in COLUMN_SHARDED else w[n][0].astype(BF16) for n in BIG]
    fulls = [lax.dynamic_update_slice(lax.empty((N_CHIPS * s.shape[0], D), BF16), s, (shard * s.shape[0], 0))
             for s in shards]
    wts = _weights_allgather(fulls, "ag_weights")

    xcat = jnp.concatenate([ctx[0], x[0]], axis=0)
    grad_x, grads, accs = _local_step(xcat, loss_target[0], mods, (norm1_g, norm2_g, norm3_g), final_g[None],
                                      q_norm_g, k_norm_g, conv_full, wts, ctx_len)

    g = {}
    reduced = _reduce_scatter(list(grads), idx)
    for n, r in zip(BIG, reduced):
        g[n] = (r.T if n in COLUMN_SHARDED else r)[None]

    pack = jnp.concatenate([a.reshape(2 * ACC_ROWS, D) for a in accs], axis=0)
    gathered = _allgather8(pack, "ag_small")
    loss8, db_mod, g_norms, g_conv, g_qk, dm = _small_reduce(gathered, "small_reduce")
    dm_sh = lax.dynamic_slice(dm[:, :N_MOD, :].reshape(16, N_MOD * D), (0, shard * mcols), (16, mcols))
    g_wmod, cpart = _wmod_grad(cin, dm_sh, w_mod[0], "wmod_grad")
    g["w_mod"] = g_wmod[None]
    cparts = _allgather8(cpart[CTX_ROW:CTX_ROW + 8], "ag_cctx")
    g_cctx = _cctx_grad(cparts, _pad_rows(c_ctx[None], 8), "cctx_grad")
    g_conv_sh = lax.dynamic_slice(g_conv, (0, shard * ccols), (conv_w.shape[1], ccols))
    g_misc = jnp.concatenate([g_qk[0:1, 0:2 * HEAD_DIM], g_conv_sh.reshape(1, -1)], axis=1)
    g_pack = jnp.concatenate([g_cctx, db_mod, g_norms, _pad_rows(g_misc, 8)], axis=0)

    def packed(p):
        return _pack_small(p["c_ctx"], p["b_mod"], p["norm1_g"], p["norm2_g"], p["norm3_g"], p["final_g"],
                           p["q_norm_g"], p["k_norm_g"], p["conv_w"][0], D)

    d_pack, m_pack, v_pack = _adamw(packed(w), g_pack, packed(m), packed(v), "adamw_small")
    g.update(_unpack_small(g_pack, D, conv_w.shape))
    delta = _unpack_small(d_pack, D, conv_w.shape)
    new_m = _unpack_small(m_pack, D, conv_w.shape)
    new_v = _unpack_small(v_pack, D, conv_w.shape)
    for n in BIG + ("w_mod",):
        d2, m2, v2 = _adamw(w[n][0], g[n][0], m[n][0], v[n][0], "adamw_" + n)
        delta[n], new_m[n], new_v[n] = d2[None], m2[None], v2[None]

    loss = loss8[0, 0]
    return (loss, grad_x[None], *[g[n] for n in WEIGHT_ORDER], *[delta[n] for n in WEIGHT_ORDER],
            *[new_m[n] for n in WEIGHT_ORDER], *[new_v[n] for n in WEIGHT_ORDER])
```

```python
import functools

import jax
import jax.numpy as jnp
from jax import lax
from jax.experimental import pallas as pl
from jax.experimental.pallas import tpu as pltpu

F32 = jnp.float32
BF16 = jnp.bfloat16

HEAD_DIM = 128
N_Q_HEADS = 8
N_KV_HEADS = 2
GROUP = N_Q_HEADS // N_KV_HEADS
GRID_W = 64
ROPE_THETA = 10000.0
EPS = 1e-6
ATTN_SCALE = HEAD_DIM ** -0.5

ADAM_LR = 0.001
ADAM_B1 = 0.9
ADAM_B2 = 0.999
ADAM_EPS = 1e-08
ADAM_WD = 0.01
ADAM_STEP = 10

ROW = 256
HALO = 8
ACC_ROWS = 8
N_CHIPS = 4
N_DEV = 8
MESH = pl.DeviceIdType.MESH
VMEM_LIMIT = 48 * 1024 * 1024
ADAMW_BLOCK_BYTES = 1024 * 1024


def _pick(n, prefs):
    for p in prefs:
        if n % p == 0:
            return p
    return n


def _params(sem):
    return pltpu.CompilerParams(dimension_semantics=sem, vmem_limit_bytes=VMEM_LIMIT)


def _stream(i):
    return jnp.minimum(i, 1)


def _matmul(a, b, mode, out_dtype, name, tm=None, tn=None, tk=None):
    if mode == "nn":
        (M, K), (K2, N) = a.shape, b.shape
    elif mode == "nt":
        (M, K), (N, K2) = a.shape, b.shape
    else:
        (K, M), (K2, N) = a.shape, b.shape
    assert K == K2, (a.shape, b.shape, mode)
    tm = tm or _pick(M, (1664, 1408, 1024, 512, 256, 128) if mode == "tn" else (1408, 768, 512, 256, 128))
    tn = tn or _pick(N, (1664, 1408, 1024, 512, 256, 128))
    tk = tk or _pick(K, (1664, 1408, 1024, 768, 512, 256, 128))
    nk = K // tk
    if mode == "tn":
        a_spec = pl.BlockSpec((tk, tm), lambda i, j, k: (k, i))
    else:
        a_spec = pl.BlockSpec((tm, tk), lambda i, j, k: (i, k))
    if mode == "nt":
        b_spec = pl.BlockSpec((tn, tk), lambda i, j, k: (j, k))
    else:
        b_spec = pl.BlockSpec((tk, tn), lambda i, j, k: (k, j))
    dims = {"nn": ((1,), (0,)), "nt": ((1,), (1,)), "tn": ((0,), (0,))}[mode]
    use_scratch = nk > 1 and out_dtype != F32

    def body(a_ref, b_ref, o_ref, *scratch):
        p = lax.dot_general(a_ref[...].astype(BF16), b_ref[...].astype(BF16), (dims, ((), ())),
                            preferred_element_type=F32)
        if nk == 1:
            o_ref[...] = p.astype(o_ref.dtype)
            return
        acc_ref = scratch[0] if use_scratch else o_ref
        k = pl.program_id(2)

        @pl.when(k == 0)
        def _():
            acc_ref[...] = p

        @pl.when(k > 0)
        def _():
            acc_ref[...] += p

        if use_scratch:
            @pl.when(k == nk - 1)
            def _():
                o_ref[...] = acc_ref[...].astype(o_ref.dtype)

    return pl.pallas_call(
        body, name=name,
        grid=(M // tm, N // tn, nk),
        in_specs=[a_spec, b_spec],
        out_specs=pl.BlockSpec((tm, tn), lambda i, j, k: (i, j)),
        out_shape=jax.ShapeDtypeStruct((M, N), out_dtype),
        scratch_shapes=[pltpu.VMEM((tm, tn), F32)] if use_scratch else [],
        compiler_params=_params(("parallel", "parallel", "arbitrary")),
    )(a, b)


def _row_spec(width, col=0):
    return pl.BlockSpec((ROW, width), lambda i, col=col: (i, col))


def _mods_spec(D):
    return pl.BlockSpec((1, 16, D), lambda i: (_stream(i), 0, 0))


def _acc_spec(D):
    return pl.BlockSpec((1, ACC_ROWS, D), lambda i: (_stream(i), 0, 0))


def _vec_spec(rows, D):
    return pl.BlockSpec((rows, D), lambda i: (0, 0))


def _acc_init(acc_ref):
    i = pl.program_id(0)

    @pl.when(i <= 1)
    def _():
        acc_ref[...] = jnp.zeros_like(acc_ref)


def _acc_add(acc_ref, row, val):
    acc_ref[0, row:row + 1, :] += jnp.sum(val, axis=0, keepdims=True)


def _resid_rmsmod_fwd(xprev, branch, mods, g, gate, shift_idx, scale_idx, name):
    T, D = xprev.shape
    has_res = branch is not None

    def body(*refs):
        if has_res:
            x_ref, f_ref, m_ref, g_ref, xo_ref, h_ref = refs
        else:
            x_ref, m_ref, g_ref, h_ref = refs
        m = m_ref[0]
        x = x_ref[...]
        if has_res:
            gate_idx, fac = gate
            x = x + (fac * m[gate_idx:gate_idx + 1, :]) * f_ref[...]
            xo_ref[...] = x
        inv = lax.rsqrt(jnp.mean(x * x, axis=-1, keepdims=True) + EPS)
        y = (x * inv) * g_ref[...]
        h = y * (1.0 + m[scale_idx:scale_idx + 1, :]) + m[shift_idx:shift_idx + 1, :]
        h_ref[...] = h.astype(BF16)

    in_specs = [_row_spec(D)] + ([_row_spec(D)] if has_res else []) + [_mods_spec(D), _vec_spec(1, D)]
    args = [xprev] + ([branch] if has_res else []) + [mods, g]
    out_specs = ([_row_spec(D)] if has_res else []) + [_row_spec(D)]
    out_shape = ([jax.ShapeDtypeStruct((T, D), F32)] if has_res else []) + [jax.ShapeDtypeStruct((T, D), BF16)]
    out = pl.pallas_call(
        body, name=name, grid=(T // ROW,), in_specs=in_specs, out_specs=out_specs, out_shape=out_shape,
        compiler_params=_params(("parallel",)),
    )(*args)
    return out if has_res else (None, out[0])


def _loss_head(x2, f2, mods, final_g, target, name):
    T, D = x2.shape
    nt = T // ROW

    def body(x_ref, f_ref, m_ref, g_ref, t_ref, dx_ref, df_ref, acc_ref):
        _acc_init(acc_ref)
        i = pl.program_id(0)
        m = m_ref[0]
        gate = 0.5 * m[8:9, :]
        f = f_ref[...]
        x = x_ref[...] + gate * f
        inv = lax.rsqrt(jnp.mean(x * x, axis=-1, keepdims=True) + EPS)
        xn = x * inv
        fg = g_ref[...]
        lat = (i > 0).astype(F32)
        e = (xn * fg - t_ref[...]) * lat
        dy = e * (1.0 / D)
        dxn = dy * fg
        dx = inv * (dxn - xn * jnp.mean(dxn * xn, axis=-1, keepdims=True))
        dx_ref[...] = dx
        df_ref[...] = (gate * dx).astype(BF16)
        _acc_add(acc_ref, 0, (0.5 / D) * e * e)
        _acc_add(acc_ref, 1, dy * xn)
        _acc_add(acc_ref, 2, 0.5 * dx * f)

    return pl.pallas_call(
        body, name=name, grid=(nt,),
        in_specs=[_row_spec(D), _row_spec(D), _mods_spec(D), _vec_spec(1, D),
                  pl.BlockSpec((ROW, D), lambda i: (jnp.maximum(i - 1, 0), 0))],
        out_specs=[_row_spec(D), _row_spec(D), _acc_spec(D)],
        out_shape=[jax.ShapeDtypeStruct((T, D), F32), jax.ShapeDtypeStruct((T, D), BF16),
                   jax.ShapeDtypeStruct((2, ACC_ROWS, D), F32)],
        compiler_params=_params(("arbitrary",)),
    )(x2, f2, mods, final_g, target)


def _rmsmod_bwd(x, dh, dres, mods, g, shift_idx, scale_idx, gate, branch, name, skip_first_tile=False):
    T, D = x.shape
    nt = T // ROW
    has_gate = gate is not None

    def body(*refs):
        if has_gate:
            x_ref, dh_ref, dr_ref, b_ref, m_ref, g_ref, dx_ref, db_ref, acc_ref = refs
        else:
            x_ref, dh_ref, dr_ref, m_ref, g_ref, dx_ref, acc_ref = refs
        _acc_init(acc_ref)
        m = m_ref[0]
        x = x_ref[...]
        dh = dh_ref[...]
        gg = g_ref[...]
        inv = lax.rsqrt(jnp.mean(x * x, axis=-1, keepdims=True) + EPS)
        xn = x * inv
        y = xn * gg
        dy = dh * (1.0 + m[scale_idx:scale_idx + 1, :])
        dxn = dy * gg
        dx = inv * (dxn - xn * jnp.mean(dxn * xn, axis=-1, keepdims=True)) + dr_ref[...]
        dx_ref[...] = dx
        _acc_add(acc_ref, 0, dh)
        _acc_add(acc_ref, 1, dh * y)
        _acc_add(acc_ref, 2, dy * xn)
        if has_gate:
            gate_idx, fac = gate
            b = b_ref[...]
            db_ref[...] = ((fac * m[gate_idx:gate_idx + 1, :]) * dx).astype(BF16)
            _acc_add(acc_ref, 3, fac * dx * b)

    in_specs = [_row_spec(D), _row_spec(D), _row_spec(D)] + ([_row_spec(D)] if has_gate else []) + \
               [_mods_spec(D), _vec_spec(1, D)]
    args = [x, dh, dres] + ([branch] if has_gate else []) + [mods, g]
    if skip_first_tile:
        dx_spec = pl.BlockSpec((ROW, D), lambda i: (jnp.maximum(i - 1, 0), 0))
        dx_shape = jax.ShapeDtypeStruct((T - ROW, D), F32)
    else:
        dx_spec = _row_spec(D)
        dx_shape = jax.ShapeDtypeStruct((T, D), F32)
    out_specs = [dx_spec] + ([_row_spec(D)] if has_gate else []) + [_acc_spec(D)]
    out_shape = [dx_shape] + ([jax.ShapeDtypeStruct((T, D), BF16)] if has_gate else []) + \
                [jax.ShapeDtypeStruct((2, ACC_ROWS, D), F32)]
    out = pl.pallas_call(
        body, name=name, grid=(nt,), in_specs=in_specs, out_specs=out_specs, out_shape=out_shape,
        compiler_params=_params(("arbitrary",)),
    )(*args)
    if has_gate:
        return out
    return out[0], None, out[1]


FFN_ROWS = 384
_NT = (((1,), (1,)), ((), ()))


def _ffn_chunk(F):
    return _pick(F, (1408, 512, 256, 128))


def _resident():
    return pl.BlockSpec(memory_space=pltpu.VMEM)


def _ffn_fwd(h, w_in_t, w_out, name):
    T, D = h.shape
    F = w_out.shape[0]
    cw = _ffn_chunk(F)
    tm = _pick(T, (FFN_ROWS, ROW))

    def body(h_ref, wi_ref, wo_ref, u_ref, s_ref, f_ref):
        hv = h_ref[...]
        acc = jnp.zeros((tm, D), F32)
        for j in range(F // cw):
            a = lax.dot_general(hv, wi_ref[j * cw:(j + 1) * cw, :], _NT, preferred_element_type=F32)
            b = lax.dot_general(hv, wi_ref[F + j * cw:F + (j + 1) * cw, :], _NT, preferred_element_type=F32)
            s = ((a * jax.nn.sigmoid(a)) * b).astype(BF16)
            u_ref[:, j * cw:(j + 1) * cw] = a.astype(BF16)
            u_ref[:, F + j * cw:F + (j + 1) * cw] = b.astype(BF16)
            s_ref[:, j * cw:(j + 1) * cw] = s
            acc = acc + jnp.dot(s, wo_ref[j * cw:(j + 1) * cw, :], preferred_element_type=F32)
        f_ref[...] = acc

    row = lambda w: pl.BlockSpec((tm, w), lambda i: (i, 0))
    return pl.pallas_call(
        body, name=name, grid=(T // tm,),
        in_specs=[row(D), _resident(), _resident()],
        out_specs=[row(2 * F), row(F), row(D)],
        out_shape=[jax.ShapeDtypeStruct((T, 2 * F), BF16), jax.ShapeDtypeStruct((T, F), BF16),
                   jax.ShapeDtypeStruct((T, D), F32)],
        compiler_params=_params(("parallel",)),
    )(h, w_in_t, w_out)


def _ffn_bwd(df, u, w_in_t, w_out, name):
    T, D = df.shape
    F = w_out.shape[0]
    cw = _ffn_chunk(F)
    tm = _pick(T, (FFN_ROWS, ROW))

    def body(df_ref, u_ref, wi_ref, wo_ref, du_ref, dh_ref):
        dfv = df_ref[...]
        acc = jnp.zeros((tm, D), F32)
        for j in range(F // cw):
            ds = lax.dot_general(dfv, wo_ref[j * cw:(j + 1) * cw, :], _NT, preferred_element_type=F32)
            a = u_ref[:, j * cw:(j + 1) * cw].astype(F32)
            b = u_ref[:, F + j * cw:F + (j + 1) * cw].astype(F32)
            sig = jax.nn.sigmoid(a)
            da = (ds * b * (sig * (1.0 + a * (1.0 - sig)))).astype(BF16)
            db = (ds * (a * sig)).astype(BF16)
            du_ref[:, j * cw:(j + 1) * cw] = da
            du_ref[:, F + j * cw:F + (j + 1) * cw] = db
            acc = acc + jnp.dot(da, wi_ref[j * cw:(j + 1) * cw, :], preferred_element_type=F32)
            acc = acc + jnp.dot(db, wi_ref[F + j * cw:F + (j + 1) * cw, :], preferred_element_type=F32)
        dh_ref[...] = acc

    row = lambda w: pl.BlockSpec((tm, w), lambda i: (i, 0))
    return pl.pallas_call(
        body, name=name, grid=(T // tm,),
        in_specs=[row(D), row(2 * F), _resident(), _resident()],
        out_specs=[row(2 * F), row(D)],
        out_shape=[jax.ShapeDtypeStruct((T, 2 * F), BF16), jax.ShapeDtypeStruct((T, D), F32)],
        compiler_params=_params(("parallel",)),
    )(df, u, w_in_t, w_out)


def _halo_specs(width, col, nt):
    per = ROW // HALO
    prev = pl.BlockSpec((HALO, width), lambda i, col=col: (jnp.maximum(i * per - 1, 0), col))
    nxt = pl.BlockSpec((HALO, width), lambda i, col=col: (jnp.minimum((i + 1) * per, nt * per - 1), col))
    return prev, nxt


def _shift_rows(v, prev_row, next_row):
    rows = lax.broadcasted_iota(jnp.int32, v.shape, 0)
    down = jnp.where(rows == 0, prev_row, pltpu.roll(v, 1, 0))
    up = jnp.where(rows == v.shape[0] - 1, next_row, pltpu.roll(v, v.shape[0] - 1, 0))
    return down, up


def _conv_fwd(P, conv_w, D, name):
    T = P.shape[0]
    nt = T // ROW
    cg_p, cg_n = _halo_specs(D, 1, nt)
    vc_p, vc_n = _halo_specs(D, 2, nt)

    def body(bg_ref, cg_ref, vc_ref, cgp_ref, vcp_ref, cgn_ref, vcn_ref, w_ref, y_ref):
        i = pl.program_id(0)
        has_prev = (i != 1).astype(F32)
        has_next = (i != nt - 1).astype(F32)
        u = cg_ref[...] * vc_ref[...]
        up_row = cgp_ref[HALO - 1:HALO, :] * vcp_ref[HALO - 1:HALO, :] * has_prev
        un_row = cgn_ref[0:1, :] * vcn_ref[0:1, :] * has_next
        um1, up1 = _shift_rows(u, up_row, un_row)
        w = w_ref[...]
        conv = um1 * w[0:1, :] + u * w[1:2, :] + up1 * w[2:3, :]
        y_ref[...] = (bg_ref[...] * conv).astype(BF16)

    return pl.pallas_call(
        body, name=name, grid=(nt,),
        in_specs=[_row_spec(D, 0), _row_spec(D, 1), _row_spec(D, 2), cg_p, vc_p, cg_n, vc_n, _vec_spec(3, D)],
        out_specs=_row_spec(D),
        out_shape=jax.ShapeDtypeStruct((T, D), BF16),
        compiler_params=_params(("parallel",)),
    )(P, P, P, P, P, P, P, conv_w)


def _conv_bwd(P, dy, conv_w, D, name):
    T = P.shape[0]
    nt = T // ROW
    bg_p, bg_n = _halo_specs(D, 0, nt)
    cg_p, cg_n = _halo_specs(D, 1, nt)
    vc_p, vc_n = _halo_specs(D, 2, nt)
    dy_p, dy_n = _halo_specs(D, 0, nt)

    def body(bg_ref, cg_ref, vc_ref, dy_ref, bgp_ref, cgp_ref, vcp_ref, dyp_ref,
             bgn_ref, cgn_ref, vcn_ref, dyn_ref, w_ref, o_ref, acc_ref):
        _acc_init(acc_ref)
        i = pl.program_id(0)
        lat = (i > 0).astype(F32)
        has_prev = (i != 1).astype(F32)
        has_next = (i != nt - 1).astype(F32)
        last = HALO - 1
        bg = bg_ref[...]
        cg = cg_ref[...]
        vc = vc_ref[...]
        dyv = dy_ref[...] * lat
        u = cg * vc
        up_row = cgp_ref[last:HALO, :] * vcp_ref[last:HALO, :] * has_prev
        un_row = cgn_ref[0:1, :] * vcn_ref[0:1, :] * has_next
        um1, up1 = _shift_rows(u, up_row, un_row)
        w = w_ref[...]
        conv = um1 * w[0:1, :] + u * w[1:2, :] + up1 * w[2:3, :]
        dc = dyv * bg
        dcp_row = dyp_ref[last:HALO, :] * bgp_ref[last:HALO, :] * has_prev
        dcn_row = dyn_ref[0:1, :] * bgn_ref[0:1, :] * has_next
        dcm1, dcp1 = _shift_rows(dc, dcp_row, dcn_row)
        du = dcp1 * w[0:1, :] + dc * w[1:2, :] + dcm1 * w[2:3, :]
        o_ref[:, 0:D] = (dyv * conv).astype(BF16)
        o_ref[:, D:2 * D] = (du * vc * lat).astype(BF16)
        o_ref[:, 2 * D:3 * D] = (du * cg * lat).astype(BF16)
        _acc_add(acc_ref, 0, dc * um1)
        _acc_add(acc_ref, 1, dc * u)
        _acc_add(acc_ref, 2, dc * up1)

    return pl.pallas_call(
        body, name=name, grid=(nt,),
        in_specs=[_row_spec(D, 0), _row_spec(D, 1), _row_spec(D, 2), _row_spec(D, 0),
                  bg_p, cg_p, vc_p, dy_p, bg_n, cg_n, vc_n, dy_n, _vec_spec(3, D)],
        out_specs=[_row_spec(3 * D), _acc_spec(D)],
        out_shape=[jax.ShapeDtypeStruct((T, 3 * D), BF16), jax.ShapeDtypeStruct((2, ACC_ROWS, D), F32)],
        compiler_params=_params(("arbitrary",)),
    )(P, P, P, dy, P, P, P, dy, P, P, P, dy, conv_w)


def _rope_tables(ctx_len, seq):
    n_freq = HEAD_DIM // 4
    rows = seq // GRID_W
    row = jnp.repeat(jnp.arange(rows), GRID_W).astype(F32)
    col = jnp.tile(jnp.arange(GRID_W), rows).astype(F32)
    inv = ROPE_THETA ** (-jnp.arange(n_freq, dtype=F32) / n_freq)
    ar = row[:, None] * inv
    ac = col[:, None] * inv
    cos_t = jnp.concatenate([jnp.cos(ar), jnp.cos(ar), jnp.cos(ac), jnp.cos(ac)], axis=1)
    sin_t = jnp.concatenate([-jnp.sin(ar), jnp.sin(ar), -jnp.sin(ac), jnp.sin(ac)], axis=1)
    cos_t = jnp.concatenate([jnp.ones((ctx_len, HEAD_DIM), F32), cos_t], axis=0)
    sin_t = jnp.concatenate([jnp.zeros((ctx_len, HEAD_DIM), F32), sin_t], axis=0)
    return cos_t, sin_t


def _swap_halves(y):
    lanes = lax.broadcasted_iota(jnp.int32, y.shape, 1)
    first = (lanes % 64) < 32
    return jnp.where(first, pltpu.roll(y, HEAD_DIM - 32, 1), pltpu.roll(y, 32, 1))


def _qk_fwd(P, gq, gk, cos_t, sin_t, D, name):
    T = P.shape[0]
    QW = N_Q_HEADS * HEAD_DIM
    KW = N_KV_HEADS * HEAD_DIM
    q_col = (3 * D) // QW
    k_col = (3 * D + QW) // KW
    v_col = k_col + 1

    def body(q_ref, k_ref, v_ref, gq_ref, gk_ref, c_ref, s_ref, qo_ref, ko_ref, vo_ref):
        c = c_ref[...]
        s = s_ref[...]

        def head(x, g):
            inv = lax.rsqrt(jnp.mean(x * x, axis=-1, keepdims=True) + EPS)
            y = (x * inv) * g
            return y * c + _swap_halves(y) * s

        for h in range(N_Q_HEADS):
            sl = slice(h * HEAD_DIM, (h + 1) * HEAD_DIM)
            qo_ref[:, sl] = head(q_ref[:, sl], gq_ref[...]).astype(BF16)
        for h in range(N_KV_HEADS):
            sl = slice(h * HEAD_DIM, (h + 1) * HEAD_DIM)
            ko_ref[:, sl] = head(k_ref[:, sl], gk_ref[...]).astype(BF16)
        vo_ref[...] = v_ref[...].astype(BF16)

    return pl.pallas_call(
        body, name=name, grid=(T // ROW,),
        in_specs=[_row_spec(QW, q_col), _row_spec(KW, k_col), _row_spec(KW, v_col),
                  _vec_spec(1, HEAD_DIM), _vec_spec(1, HEAD_DIM), _row_spec(HEAD_DIM), _row_spec(HEAD_DIM)],
        out_specs=[_row_spec(QW), _row_spec(KW), _row_spec(KW)],
        out_shape=[jax.ShapeDtypeStruct((T, QW), BF16), jax.ShapeDtypeStruct((T, KW), BF16),
                   jax.ShapeDtypeStruct((T, KW), BF16)],
        compiler_params=_params(("parallel",)),
    )(P, P, P, gq, gk, cos_t, sin_t)


def _qk_bwd(P, dq, dk, dv, gq, gk, cos_t, sin_t, D, name):
    T = P.shape[0]
    QW = N_Q_HEADS * HEAD_DIM
    KW = N_KV_HEADS * HEAD_DIM
    q_col = (3 * D) // QW
    k_col = (3 * D + QW) // KW

    def body(q_ref, k_ref, dq_ref, dk_ref, dv_ref, gq_ref, gk_ref, c_ref, s_ref, o_ref, acc_ref):
        _acc_init(acc_ref)
        c = c_ref[...]
        s = s_ref[...]

        def head(x, d, g):
            dyv = d * c + _swap_halves(d * s)
            inv = lax.rsqrt(jnp.mean(x * x, axis=-1, keepdims=True) + EPS)
            xn = x * inv
            dxn = dyv * g
            dx = inv * (dxn - xn * jnp.mean(dxn * xn, axis=-1, keepdims=True))
            return dx, jnp.sum(dyv * xn, axis=0, keepdims=True)

        dgq = jnp.zeros((1, HEAD_DIM), F32)
        for h in range(N_Q_HEADS):
            sl = slice(h * HEAD_DIM, (h + 1) * HEAD_DIM)
            dx, dg = head(q_ref[:, sl], dq_ref[:, sl], gq_ref[...])
            o_ref[:, sl] = dx.astype(BF16)
            dgq = dgq + dg
        dgk = jnp.zeros((1, HEAD_DIM), F32)
        for h in range(N_KV_HEADS):
            sl = slice(h * HEAD_DIM, (h + 1) * HEAD_DIM)
            dx, dg = head(k_ref[:, sl], dk_ref[:, sl], gk_ref[...])
            o_ref[:, QW + h * HEAD_DIM:QW + (h + 1) * HEAD_DIM] = dx.astype(BF16)
            dgk = dgk + dg
        o_ref[:, QW + KW:QW + 2 * KW] = dv_ref[...].astype(BF16)
        acc_ref[0, 0:1, 0:HEAD_DIM] += dgq
        acc_ref[0, 1:2, 0:HEAD_DIM] += dgk

    return pl.pallas_call(
        body, name=name, grid=(T // ROW,),
        in_specs=[_row_spec(QW, q_col), _row_spec(KW, k_col), _row_spec(QW), _row_spec(KW), _row_spec(KW),
                  _vec_spec(1, HEAD_DIM), _vec_spec(1, HEAD_DIM), _row_spec(HEAD_DIM), _row_spec(HEAD_DIM)],
        out_specs=[_row_spec(QW + 2 * KW), _acc_spec(D)],
        out_shape=[jax.ShapeDtypeStruct((T, QW + 2 * KW), BF16), jax.ShapeDtypeStruct((2, ACC_ROWS, D), F32)],
        compiler_params=_params(("arbitrary",)),
    )(P, P, dq, dk, dv, gq, gk, cos_t, sin_t)


def _to_row(col, n):
    return jnp.transpose(jnp.broadcast_to(col, (n, HEAD_DIM)))[0:1, :]


LOG2E = 1.4426950408889634
ATTN_SPLIT = 2
ATTN_PARTS = 4
ATTN_KEY_CHUNK = 768


def _flash_fwd(q, k, v, name, tq=ROW, tk=None):
    T = q.shape[0]
    tk = tk or _pick(T, (1408, 768, 512, 256))
    ck = tk
    nk = T // tk
    GW = GROUP * HEAD_DIM

    def body(q_ref, k_ref, v_ref, o_ref, lse_ref, qs_ref, m_ref, l_ref, acc_ref, st_ref):
        ki = pl.program_id(2)

        @pl.when(ki == 0)
        def _():
            for g in range(GROUP):
                qs_ref[g * tq:(g + 1) * tq, :] = q_ref[:, g * HEAD_DIM:(g + 1) * HEAD_DIM]
            m_ref[...] = jnp.full(m_ref.shape, -jnp.inf, F32)
            l_ref[...] = jnp.zeros(l_ref.shape, F32)
            acc_ref[...] = jnp.zeros(acc_ref.shape, F32)

        w = GROUP * tq // ATTN_PARTS
        nck = tk // ck

        def lanes(p):
            return slice(p * w, (p + 1) * w)

        def keys(c):
            return slice(c * ck, (c + 1) * ck)

        def fold(a):
            return a.reshape(ck // 8, 8, w)

        def scores(p, c):
            st = lax.dot_general(k_ref[keys(c), :], qs_ref[lanes(p), :], _NT,
                                 preferred_element_type=F32) * (ATTN_SCALE * LOG2E)
            st_ref[keys(c), lanes(p)] = st
            return jnp.max(fold(st), axis=0)

        def new_max(p, partial):
            m_prev = m_ref[:, lanes(p)]
            m_new = jnp.maximum(m_prev, jnp.max(functools.reduce(jnp.maximum, partial), axis=0, keepdims=True))
            m_ref[:, lanes(p)] = m_new
            return m_new, jnp.exp2(m_prev - m_new)

        def weights(p, c, m_new):
            pt = jnp.exp2(st_ref[keys(c), lanes(p)] - m_new)
            pv = lax.dot_general(v_ref[keys(c), :], pt.astype(BF16), (((0,), (0,)), ((), ())),
                                 preferred_element_type=F32)
            return jnp.sum(fold(pt), axis=0), pv

        partial = [scores(0, c) for c in range(nck)]
        for p in range(ATTN_PARTS):
            m_new, alpha = new_max(p, partial)
            partial, sums, pvs = [], [], []
            for c in range(nck):
                if p + 1 < ATTN_PARTS:
                    partial.append(scores(p + 1, c))
                s8, pv = weights(p, c, m_new)
                sums.append(s8)
                pvs.append(pv)
            l_ref[:, lanes(p)] = alpha * l_ref[:, lanes(p)] + jnp.sum(sum(sums), axis=0, keepdims=True)
            acc_ref[:, lanes(p)] = alpha * acc_ref[:, lanes(p)] + sum(pvs)

        @pl.when(ki == nk - 1)
        def _():
            out = jnp.transpose(acc_ref[...] / l_ref[...])
            lse = m_ref[...] + jnp.log2(l_ref[...])
            for g in range(GROUP):
                o_ref[:, g * HEAD_DIM:(g + 1) * HEAD_DIM] = out[g * tq:(g + 1) * tq, :]
                lse_ref[0, g:g + 1, :] = lse[:, g * tq:(g + 1) * tq]

    return pl.pallas_call(
        body, name=name, grid=(N_KV_HEADS, T // tq, nk),
        in_specs=[pl.BlockSpec((tq, GW), lambda h, i, j: (i, h)),
                  pl.BlockSpec((tk, HEAD_DIM), lambda h, i, j: (j, h)),
                  pl.BlockSpec((tk, HEAD_DIM), lambda h, i, j: (j, h))],
        out_specs=[pl.BlockSpec((tq, GW), lambda h, i, j: (i, h)),
                   pl.BlockSpec((1, GROUP, tq), lambda h, i, j: (h, 0, i))],
        out_shape=[jax.ShapeDtypeStruct((T, N_Q_HEADS * HEAD_DIM), F32),
                   jax.ShapeDtypeStruct((N_KV_HEADS, GROUP, T), F32)],
        scratch_shapes=[pltpu.VMEM((GROUP * tq, HEAD_DIM), BF16), pltpu.VMEM((1, GROUP * tq), F32),
                        pltpu.VMEM((1, GROUP * tq), F32), pltpu.VMEM((HEAD_DIM, GROUP * tq), F32),
                        pltpu.VMEM((tk, GROUP * tq), F32)],
        compiler_params=_params(("parallel", "parallel", "arbitrary")),
    )(q, k, v)


def _attn_delta(do, o, name):
    T, QW = do.shape

    def body(do_ref, o_ref, dob_ref, dl_ref):
        dov = do_ref[...]
        dob_ref[...] = dov.astype(BF16)
        prod = dov * o_ref[...]
        for h in range(N_Q_HEADS):
            d = jnp.sum(prod[:, h * HEAD_DIM:(h + 1) * HEAD_DIM], axis=1, keepdims=True)
            dl_ref[h // GROUP, (h % GROUP):(h % GROUP) + 1, :] = _to_row(d, ROW)

    return pl.pallas_call(
        body, name=name, grid=(T // ROW,),
        in_specs=[_row_spec(QW), _row_spec(QW)],
        out_specs=[_row_spec(QW), pl.BlockSpec((N_KV_HEADS, GROUP, ROW), lambda i: (0, 0, i))],
        out_shape=[jax.ShapeDtypeStruct((T, QW), BF16), jax.ShapeDtypeStruct((N_KV_HEADS, GROUP, T), F32)],
        compiler_params=_params(("parallel",)),
    )(do, o)


def _flash_bwd(q, k, v, do, lse, delta, name, tq=ROW, tk=None):
    T = q.shape[0]
    tk = tk or _pick(T, (1408, 768, 512, 256))
    nk = T // tk
    GW = GROUP * HEAD_DIM
    nt = (((1,), (1,)), ((), ()))

    def body(q_ref, do_ref, k_ref, v_ref, lse_ref, dl_ref, dq_ref, dk_ref, dv_ref, qs_ref, dos_ref, dqt_ref):
        qi = pl.program_id(1)
        ki = pl.program_id(2)

        @pl.when(ki == 0)
        def _():
            for g in range(GROUP):
                qs_ref[g * tq:(g + 1) * tq, :] = q_ref[:, g * HEAD_DIM:(g + 1) * HEAD_DIM]
                dos_ref[g * tq:(g + 1) * tq, :] = do_ref[:, g * HEAD_DIM:(g + 1) * HEAD_DIM]
            dqt_ref[...] = jnp.zeros(dqt_ref.shape, F32)

        kk = k_ref[...]
        vv = v_ref[...]

        def lanes(p):
            return slice(p * tq, (p + 1) * tq)

        def products(p):
            st = lax.dot_general(kk, qs_ref[lanes(p), :], nt, preferred_element_type=F32)
            dpt = lax.dot_general(vv, dos_ref[lanes(p), :], nt, preferred_element_type=F32)
            return st, dpt

        dk_c = jnp.zeros((tk, HEAD_DIM), F32)
        dv_c = jnp.zeros((tk, HEAD_DIM), F32)
        ahead = products(0)
        for p in range(GROUP):
            st, dpt = ahead
            if p + 1 < GROUP:
                ahead = products(p + 1)
            pt = jnp.exp2(st * (ATTN_SCALE * LOG2E) - lse_ref[0, p:p + 1, :])
            dst = ((pt * (dpt - dl_ref[0, p:p + 1, :])) * ATTN_SCALE).astype(BF16)
            dv_c = dv_c + jnp.dot(pt.astype(BF16), dos_ref[lanes(p), :], preferred_element_type=F32)
            dk_c = dk_c + jnp.dot(dst, qs_ref[lanes(p), :], preferred_element_type=F32)
            dqt_ref[:, lanes(p)] += lax.dot_general(kk, dst, (((0,), (0,)), ((), ())), preferred_element_type=F32)
        rows = pl.ds(pl.multiple_of(ki * tk, tk), tk)

        @pl.when(qi == 0)
        def _():
            dk_ref[rows, :] = dk_c
            dv_ref[rows, :] = dv_c

        @pl.when(qi > 0)
        def _():
            dk_ref[rows, :] += dk_c
            dv_ref[rows, :] += dv_c

        @pl.when(ki == nk - 1)
        def _():
            dqv = jnp.transpose(dqt_ref[...])
            for g in range(GROUP):
                dq_ref[:, g * HEAD_DIM:(g + 1) * HEAD_DIM] = dqv[g * tq:(g + 1) * tq, :]

    return pl.pallas_call(
        body, name=name, grid=(N_KV_HEADS, T // tq, nk),
        in_specs=[pl.BlockSpec((tq, GW), lambda h, i, j: (i, h)),
                  pl.BlockSpec((tq, GW), lambda h, i, j: (i, h)),
                  pl.BlockSpec((tk, HEAD_DIM), lambda h, i, j: (j, h)),
                  pl.BlockSpec((tk, HEAD_DIM), lambda h, i, j: (j, h)),
                  pl.BlockSpec((1, GROUP, tq), lambda h, i, j: (h, 0, i)),
                  pl.BlockSpec((1, GROUP, tq), lambda h, i, j: (h, 0, i))],
        out_specs=[pl.BlockSpec((tq, GW), lambda h, i, j: (i, h)),
                   pl.BlockSpec((T, HEAD_DIM), lambda h, i, j: (0, h)),
                   pl.BlockSpec((T, HEAD_DIM), lambda h, i, j: (0, h))],
        out_shape=[jax.ShapeDtypeStruct((T, N_Q_HEADS * HEAD_DIM), F32),
                   jax.ShapeDtypeStruct((T, N_KV_HEADS * HEAD_DIM), F32),
                   jax.ShapeDtypeStruct((T, N_KV_HEADS * HEAD_DIM), F32)],
        scratch_shapes=[pltpu.VMEM((GROUP * tq, HEAD_DIM), BF16), pltpu.VMEM((GROUP * tq, HEAD_DIM), BF16),
                        pltpu.VMEM((HEAD_DIM, GROUP * tq), F32)],
        compiler_params=_params(("arbitrary", "arbitrary", "arbitrary")),
    )(q, do, k, v, lse, delta)


def _gate_specs(D):
    w = D // 2
    first = (3 * D + (N_Q_HEADS + 2 * N_KV_HEADS) * HEAD_DIM) // w
    return [pl.BlockSpec((ROW, w), lambda i, c=first + j: (i, c)) for j in range(4)]


def _merge_fwd(a1, a2, P, D, name):
    T = a1.shape[0]
    w = D // 2

    def body(a1_ref, a2_ref, g0, g1, g2, g3, z_ref):
        for j, (gc, ga) in enumerate(((g0, g2), (g1, g3))):
            sl = slice(j * w, (j + 1) * w)
            z = jax.nn.sigmoid(gc[...]) * a1_ref[:, sl] + jax.nn.sigmoid(ga[...]) * a2_ref[:, sl]
            z_ref[:, sl] = z.astype(BF16)

    return pl.pallas_call(
        body, name=name, grid=(T // ROW,),
        in_specs=[_row_spec(D), _row_spec(D)] + _gate_specs(D),
        out_specs=_row_spec(D), out_shape=jax.ShapeDtypeStruct((T, D), BF16),
        compiler_params=_params(("parallel",)),
    )(a1, a2, P, P, P, P)


def _merge_bwd(dz, a1, a2, P, D, name):
    T = a1.shape[0]
    w = D // 2

    def body(dz_ref, a1_ref, a2_ref, g0, g1, g2, g3, d1_ref, d2_ref, dg_ref):
        for j, (gc, ga) in enumerate(((g0, g2), (g1, g3))):
            sl = slice(j * w, (j + 1) * w)
            dz = dz_ref[:, sl]
            sc = jax.nn.sigmoid(gc[...])
            sa = jax.nn.sigmoid(ga[...])
            d1_ref[:, sl] = (dz * sc).astype(BF16)
            d2_ref[:, sl] = (dz * sa).astype(BF16)
            dg_ref[:, j * w:(j + 1) * w] = (dz * a1_ref[:, sl] * (sc * (1.0 - sc))).astype(BF16)
            dg_ref[:, D + j * w:D + (j + 1) * w] = (dz * a2_ref[:, sl] * (sa * (1.0 - sa))).astype(BF16)

    return pl.pallas_call(
        body, name=name, grid=(T // ROW,),
        in_specs=[_row_spec(D), _row_spec(D), _row_spec(D)] + _gate_specs(D),
        out_specs=[_row_spec(D), _row_spec(D), _row_spec(2 * D)],
        out_shape=[jax.ShapeDtypeStruct((T, D), BF16), jax.ShapeDtypeStruct((T, D), BF16),
                   jax.ShapeDtypeStruct((T, 2 * D), BF16)],
        compiler_params=_params(("parallel",)),
    )(dz, a1, a2, P, P, P, P)


def _adamw_math(w, g, m, v):
    m = ADAM_B1 * m + (1.0 - ADAM_B1) * g
    v = ADAM_B2 * v + (1.0 - ADAM_B2) * (g * g)
    m_hat = m / (1.0 - ADAM_B1 ** ADAM_STEP)
    v_hat = v / (1.0 - ADAM_B2 ** ADAM_STEP)
    delta = -ADAM_LR * (m_hat / (jnp.sqrt(v_hat) + ADAM_EPS) + ADAM_WD * w)
    return delta, m, v


def _adamw(w, g, m, v, name):
    R, C = w.shape
    tr = _pick(R, tuple(t for t in (256, 128, 64, 32, 16, 8) if t * C * 4 <= ADAMW_BLOCK_BYTES))

    def body(w_ref, g_ref, m_ref, v_ref, d_ref, mo_ref, vo_ref):
        d, mn, vn = _adamw_math(w_ref[...], g_ref[...], m_ref[...], v_ref[...])
        d_ref[...] = d
        mo_ref[...] = mn
        vo_ref[...] = vn

    spec = pl.BlockSpec((tr, C), lambda i: (i, 0))
    return pl.pallas_call(
        body, name=name, grid=(R // tr,),
        in_specs=[spec] * 4, out_specs=[spec] * 3,
        out_shape=[jax.ShapeDtypeStruct((R, C), F32)] * 3,
        compiler_params=_params(("parallel",)),
    )(w, g, m, v)


def _adamw_transposed(w, gt, m, v, name):
    R, C = w.shape
    tc = 128

    def body(w_ref, g_ref, m_ref, v_ref, go_ref, d_ref, mo_ref, vo_ref):
        g = jnp.transpose(g_ref[...])
        d, mn, vn = _adamw_math(w_ref[...], g, m_ref[...], v_ref[...])
        go_ref[...] = g
        d_ref[...] = d
        mo_ref[...] = mn
        vo_ref[...] = vn

    spec = pl.BlockSpec((R, tc), lambda j: (0, j))
    return pl.pallas_call(
        body, name=name, grid=(C // tc,),
        in_specs=[spec, pl.BlockSpec((tc, R), lambda j: (j, 0)), spec, spec], out_specs=[spec] * 4,
        out_shape=[jax.ShapeDtypeStruct((R, C), F32)] * 4,
        compiler_params=_params(("parallel",)),
    )(w, gt, m, v)


def _local_step(xcat, target, mods, norm_g, final_g, gq, gk, conv_w, wts, ctx_len):
    T, D = xcat.shape
    w1i, w1o, wi, wbc, wba, wo, w2i, w2o = wts
    g1, g2, g3 = norm_g
    cos_t, sin_t = _rope_tables(ctx_len, T - ctx_len)

    _, h1 = _resid_rmsmod_fwd(xcat, None, mods, g1, None, 0, 1, "f_norm1")
    u1, s1, f1 = _ffn_fwd(h1, w1i, w1o, "f_ffn1")
    x1, h2 = _resid_rmsmod_fwd(xcat, f1, mods, g2, (2, 0.5), 3, 4, "f_norm2")
    P = _matmul(h2, wi, "nt", F32, "f_mix_in")
    yc = _conv_fwd(P, conv_w, D, "f_conv")
    qn, kn, vb = _qk_fwd(P, gq, gk, cos_t, sin_t, D, "f_qk")
    o, lse = _flash_fwd(qn, kn, vb, "f_attn")
    a1 = _matmul(yc, wbc, "nn", F32, "f_branch_conv")
    a2 = _matmul(o, wba, "nn", F32, "f_branch_attn")
    z = _merge_fwd(a1, a2, P, D, "f_merge")
    mo = _matmul(z, wo, "nn", F32, "f_mix_out")
    x2, h3 = _resid_rmsmod_fwd(x1, mo, mods, g3, (5, 1.0), 6, 7, "f_norm3")
    u2, s2, f2 = _ffn_fwd(h3, w2i, w2o, "f_ffn2")
    dx3, df2, acc_head = _loss_head(x2, f2, mods, final_g, target, "loss_head")

    du2, dh3 = _ffn_bwd(df2, u2, w2i, w2o, "b_ffn2")
    g_w2o = _matmul(s2, df2, "tn", BF16, "b_ffn2_out_dw")
    g_w2i = _matmul(du2, h3, "tn", BF16, "b_ffn2_in_dw")
    dx2, dmo, acc_n3 = _rmsmod_bwd(x2, dh3, dx3, mods, g3, 6, 7, (5, 1.0), mo, "b_norm3")

    dz = _matmul(dmo, wo, "nt", F32, "b_mix_out_dx")
    g_wo = _matmul(z, dmo, "tn", BF16, "b_mix_out_dw")
    da1, da2, dgt = _merge_bwd(dz, a1, a2, P, D, "b_merge")
    dyc = _matmul(da1, wbc, "nt", F32, "b_branch_conv_dx")
    do = _matmul(da2, wba, "nt", F32, "b_branch_attn_dx")
    g_wbc = _matmul(yc, da1, "tn", BF16, "b_branch_conv_dw")
    g_wba = _matmul(o, da2, "tn", BF16, "b_branch_attn_dw")
    dob, delta = _attn_delta(do, o, "b_attn_delta")
    dq, dk, dv = _flash_bwd(qn, kn, vb, dob, lse, delta, "b_attn")
    dqkv, acc_qk = _qk_bwd(P, dq, dk, dv, gq, gk, cos_t, sin_t, D, "b_qk")
    dconv, acc_conv = _conv_bwd(P, dyc, conv_w, D, "b_conv")
    dP = jnp.concatenate([dconv, dqkv, dgt], axis=1)
    dh2 = _matmul(dP, wi, "nn", F32, "b_mix_in_dx")
    g_wi = _matmul(dP, h2, "tn", BF16, "b_mix_in_dw")
    dx1, df1, acc_n2 = _rmsmod_bwd(x1, dh2, dx2, mods, g2, 3, 4, (2, 0.5), f1, "b_norm2")

    du1, dh1 = _ffn_bwd(df1, u1, w1i, w1o, "b_ffn1")
    g_w1o = _matmul(s1, df1, "tn", BF16, "b_ffn1_out_dw")
    g_w1i = _matmul(du1, h1, "tn", BF16, "b_ffn1_in_dw")
    grad_x, _, acc_n1 = _rmsmod_bwd(xcat, dh1, dx1, mods, g1, 0, 1, None, None, "b_norm1", skip_first_tile=True)

    grads = (g_w1i, g_w1o, g_wi, g_wbc, g_wba, g_wo, g_w2i, g_w2o)
    accs = (acc_head, acc_n3, acc_n2, acc_n1, acc_conv, acc_qk)
    return grad_x, grads, accs


def _place():
    return lax.axis_index("x"), lax.axis_index("y"), lax.axis_index("c")


def _other_chips(x, y):
    return [(1 - x, y), (x, 1 - y), (1 - x, 1 - y)]


def _allgather8(v, name):
    R, N = v.shape

    def body(v_ref, out_ref, send_sems, recv_sems, local_sem):
        x, y, c = _place()
        me, sibling = (x, y, c), (x, y, 1 - c)
        chips = _other_chips(x, y)

        def blk(px, py, pc):
            return out_ref.at[4 * px + 2 * py + pc]

        def copy(k, block, to, src=None):
            return pltpu.make_async_remote_copy(
                src_ref=blk(*block) if src is None else src, dst_ref=blk(*block),
                send_sem=send_sems.at[k], recv_sem=recv_sems.at[k], device_id=to, device_id_type=MESH)

        mine = pltpu.make_async_copy(v_ref, blk(*me), local_sem)
        mine.start()
        first = [copy(0, me, sibling, src=v_ref)]
        first += [copy(1 + j, me, (*chip, c), src=v_ref) for j, chip in enumerate(chips)]
        for cp in first:
            cp.start()
        passed = [copy(4 + j, (*chip, c), sibling) for j, chip in enumerate(chips)]
        for j, chip in enumerate(chips):
            copy(1 + j, (*chip, c), me).wait_recv()
            passed[j].start()
        copy(0, sibling, me).wait_recv()
        for j, chip in enumerate(chips):
            copy(4 + j, (*chip, 1 - c), me).wait_recv()
        for cp in first + passed:
            cp.wait_send()
        mine.wait()

    return pl.pallas_call(
        body, name=name,
        out_shape=jax.ShapeDtypeStruct((N_DEV, R, N), v.dtype),
        in_specs=[pl.BlockSpec(memory_space=pltpu.VMEM)],
        out_specs=pl.BlockSpec(memory_space=pltpu.VMEM),
        scratch_shapes=[pltpu.SemaphoreType.DMA((7,)), pltpu.SemaphoreType.DMA((7,)), pltpu.SemaphoreType.DMA],
        compiler_params=pltpu.CompilerParams(vmem_limit_bytes=VMEM_LIMIT),
    )(v)


def _any_specs(n):
    return [pl.BlockSpec(memory_space=pl.ANY)] * n


def _weights_allgather(fulls, name):
    n = len(fulls)

    def body(*refs):
        full = refs[n:2 * n]
        send_sems, recv_sems = refs[2 * n:]
        x, y, c = _place()
        sibling = (x, y, 1 - c)
        chips = _other_chips(x, y)

        def piece(t, px, py, h):
            rs = fulls[t].shape[0] // N_CHIPS
            return full[t].at[pl.ds((2 * px + py) * rs + h * (rs // 2), rs // 2), :]

        def copy(k, t, block, to):
            return pltpu.make_async_remote_copy(
                src_ref=piece(t, *block), dst_ref=piece(t, *block),
                send_sem=send_sems.at[k], recv_sem=recv_sems.at[k], device_id=to, device_id_type=MESH)

        first = []
        for t in range(n):
            for j, chip in enumerate(chips):
                cp = copy(3 * t + j, t, (x, y, c), (*chip, c))
                cp.start()
                first.append(cp)
        passed = []
        for t in range(n):
            for j, chip in enumerate(chips):
                copy(3 * t + j, t, (*chip, c), (x, y, c)).wait_recv()
                cp = copy(3 * n + 3 * t + j, t, (*chip, c), sibling)
                cp.start()
                passed.append(cp)
        for t in range(n):
            for j, chip in enumerate(chips):
                copy(3 * n + 3 * t + j, t, (*chip, 1 - c), (x, y, c)).wait_recv()
        for cp in first + passed:
            cp.wait_send()

    return pl.pallas_call(
        body, name=name,
        out_shape=[jax.ShapeDtypeStruct(f.shape, f.dtype) for f in fulls],
        in_specs=_any_specs(n), out_specs=_any_specs(n),
        input_output_aliases={t: t for t in range(n)},
        scratch_shapes=[pltpu.SemaphoreType.DMA((6 * n,)), pltpu.SemaphoreType.DMA((6 * n,))],
    )(*fulls)


def _pair_exchange(grads, name):
    n = len(grads)

    def body(*refs):
        g, land = refs[:n], refs[n:2 * n]
        send_sems, recv_sems = refs[2 * n:]
        x, y, c = _place()
        sibling = (x, y, 1 - c)
        copies = []
        for t in range(n):
            half = grads[t].shape[0] // (2 * N_CHIPS)
            for s in range(N_CHIPS):
                cp = pltpu.make_async_remote_copy(
                    src_ref=g[t].at[pl.ds((2 * s + 1 - c) * half, half), :], dst_ref=land[t].at[s],
                    send_sem=send_sems.at[N_CHIPS * t + s], recv_sem=recv_sems.at[N_CHIPS * t + s],
                    device_id=sibling, device_id_type=MESH)
                cp.start()
                copies.append(cp)
        for cp in copies:
            cp.wait_recv()
        for cp in copies:
            cp.wait_send()

    return pl.pallas_call(
        body, name=name,
        out_shape=[jax.ShapeDtypeStruct((N_CHIPS, a.shape[0] // (2 * N_CHIPS), a.shape[1]), a.dtype) for a in grads],
        in_specs=_any_specs(n), out_specs=_any_specs(n),
        scratch_shapes=[pltpu.SemaphoreType.DMA((N_CHIPS * n,)), pltpu.SemaphoreType.DMA((N_CHIPS * n,))],
    )(*grads)


def _place_shard(w2, idx, transpose, name):
    if transpose:
        D, rs = w2.shape
        tr = 128
        in_spec = pl.BlockSpec((D, tr), lambda i, idx: (0, i))
    else:
        rs, D = w2.shape
        tr = _pick(rs, (352, 256, 128, 64, 32, 16))
        in_spec = pl.BlockSpec((tr, D), lambda i, idx: (i, 0))
    steps = rs // tr

    def body(idx_ref, w_ref, o_ref):
        v = w_ref[...]
        o_ref[...] = (jnp.transpose(v) if transpose else v).astype(BF16)

    return pl.pallas_call(
        body, name=name,
        grid_spec=pltpu.PrefetchScalarGridSpec(
            num_scalar_prefetch=1, grid=(steps,), in_specs=[in_spec],
            out_specs=pl.BlockSpec((tr, D), lambda i, idx: (idx[1] * steps + i, 0))),
        out_shape=jax.ShapeDtypeStruct((N_CHIPS * rs, D), BF16),
        compiler_params=_params(("arbitrary",)),
    )(idx, w2)


def _pair_sum(g, landed, idx, name):
    _, half, D = landed.shape
    g4 = g.reshape(N_CHIPS, 2, half, D)
    tr = _pick(half, (416, 352, 128))

    def body(idx_ref, g_ref, l_ref, o_ref):
        o_ref[...] = (g_ref[0].astype(F32) + l_ref[...].astype(F32)).astype(BF16)

    return pl.pallas_call(
        body, name=name,
        grid_spec=pltpu.PrefetchScalarGridSpec(
            num_scalar_prefetch=1, grid=(N_CHIPS, half // tr),
            in_specs=[pl.BlockSpec((1, 1, tr, D), lambda s, i, idx: (idx[1 + s], idx[0], i, 0)),
                      pl.BlockSpec((1, tr, D), lambda s, i, idx: (idx[1 + s], i, 0))],
            out_specs=pl.BlockSpec((1, tr, D), lambda s, i, idx: (s, i, 0))),
        out_shape=jax.ShapeDtypeStruct((N_CHIPS, half, D), BF16),
        compiler_params=_params(("arbitrary", "arbitrary")),
    )(idx, g4, landed)


def _chip_exchange(sums, name):
    n = len(sums)

    def body(*refs):
        ps, land = refs[:n], refs[n:2 * n]
        send_sems, recv_sems = refs[2 * n:]
        x, y, c = _place()
        copies = []
        for t in range(n):
            for j, chip in enumerate(_other_chips(x, y)):
                cp = pltpu.make_async_remote_copy(
                    src_ref=ps[t].at[1 + j], dst_ref=land[t].at[j],
                    send_sem=send_sems.at[3 * t + j], recv_sem=recv_sems.at[3 * t + j],
                    device_id=(*chip, c), device_id_type=MESH)
                cp.start()
                copies.append(cp)
        for cp in copies:
            cp.wait_recv()
        for cp in copies:
            cp.wait_send()

    return pl.pallas_call(
        body, name=name,
        out_shape=[jax.ShapeDtypeStruct((3,) + a.shape[1:], a.dtype) for a in sums],
        in_specs=_any_specs(n), out_specs=_any_specs(n),
        scratch_shapes=[pltpu.SemaphoreType.DMA((3 * n,)), pltpu.SemaphoreType.DMA((3 * n,))],
    )(*sums)


def _chip_sum(ps, landed, idx, name):
    _, half, D = ps.shape
    tr = _pick(half, (416, 352, 128))
    steps = half // tr

    def body(idx_ref, p_ref, l_ref, o_ref):
        acc = p_ref[0].astype(F32)
        for j in range(3):
            acc = acc + l_ref[j].astype(F32)
        o_ref[...] = acc

    return pl.pallas_call(
        body, name=name,
        grid_spec=pltpu.PrefetchScalarGridSpec(
            num_scalar_prefetch=1, grid=(steps,),
            in_specs=[pl.BlockSpec((1, tr, D), lambda i, idx: (0, i, 0)),
                      pl.BlockSpec((3, tr, D), lambda i, idx: (0, i, 0))],
            out_specs=pl.BlockSpec((tr, D), lambda i, idx: (idx[0] * steps + i, 0))),
        out_shape=jax.ShapeDtypeStruct((2 * half, D), F32),
        compiler_params=_params(("arbitrary",)),
    )(idx, ps, landed)


def _pair_swap(shards, name):
    n = len(shards)

    def body(*refs):
        full = refs[n:2 * n]
        send_sems, recv_sems = refs[2 * n:]
        x, y, c = _place()

        def half(t, h):
            rows = shards[t].shape[0] // 2
            return full[t].at[pl.ds(h * rows, rows), :]

        def copy(t, h):
            return pltpu.make_async_remote_copy(src_ref=half(t, h), dst_ref=half(t, h), send_sem=send_sems.at[t],
                                                recv_sem=recv_sems.at[t], device_id=(x, y, 1 - c),
                                                device_id_type=MESH)

        for t in range(n):
            copy(t, c).start()
        for t in range(n):
            copy(t, 1 - c).wait_recv()
        for t in range(n):
            copy(t, c).wait_send()

    return pl.pallas_call(
        body, name=name,
        out_shape=[jax.ShapeDtypeStruct(a.shape, a.dtype) for a in shards],
        in_specs=_any_specs(n), out_specs=_any_specs(n),
        input_output_aliases={t: t for t in range(n)},
        scratch_shapes=[pltpu.SemaphoreType.DMA((n,)), pltpu.SemaphoreType.DMA((n,))],
    )(*shards)


def _reduce_scatter(grads, idx):
    landed = _pair_exchange(grads, "rs_pair_exchange")
    sums = [_pair_sum(g, l, idx, f"rs_pair_sum_{t}") for t, (g, l) in enumerate(zip(grads, landed))]
    landed2 = _chip_exchange(sums, "rs_chip_exchange")
    halves = [_chip_sum(p, l, idx, f"rs_chip_sum_{t}") for t, (p, l) in enumerate(zip(sums, landed2))]
    return _pair_swap(halves, "rs_pair_swap")


N_MOD = 9
PACK_HEAD, PACK_N3, PACK_N2, PACK_N1, PACK_CONV, PACK_QK = 0, 16, 32, 48, 64, 80
PACK_ROWS = 96
MOD_SRC = ((PACK_N1, 0), (PACK_N1, 1), (PACK_N2, 3), (PACK_N2, 0), (PACK_N2, 1),
           (PACK_N3, 3), (PACK_N3, 0), (PACK_N3, 1), (PACK_HEAD, 2))
CTX_ROW = 8


def _silu(v):
    return v * jax.nn.sigmoid(v)


def _whole(n):
    return [pl.BlockSpec(memory_space=pltpu.VMEM)] * n


def _mod_rows(cin, w_sh, b_sh, name):
    def body(c_ref, w_ref, b_ref, o_ref):
        a = _silu(c_ref[...]).astype(BF16)
        o_ref[...] = jnp.dot(a, w_ref[...].astype(BF16), preferred_element_type=F32) + b_ref[...]

    return pl.pallas_call(
        body, name=name, out_shape=jax.ShapeDtypeStruct((cin.shape[0], w_sh.shape[1]), F32),
        in_specs=_whole(3), out_specs=pl.BlockSpec(memory_space=pltpu.VMEM),
        compiler_params=pltpu.CompilerParams(vmem_limit_bytes=VMEM_LIMIT),
    )(cin, w_sh, b_sh)


def _small_reduce(gathered, name):
    _, _, D = gathered.shape

    def body(g_ref, loss_ref, db_ref, gn_ref, cv_ref, qk_ref, dm_ref):
        tot = g_ref[0]
        for r in range(1, N_DEV):
            tot = tot + g_ref[r]

        def both(block, row):
            return tot[block + row:block + row + 1, :] + tot[block + 8 + row:block + 8 + row + 1, :]

        loss = jnp.sum(both(PACK_HEAD, 0), axis=1, keepdims=True)
        loss_ref[...] = jnp.broadcast_to(loss, loss_ref.shape)
        db_ref[...] = jnp.zeros(db_ref.shape, F32)
        dm_ref[...] = jnp.zeros(dm_ref.shape, F32)
        for j, (block, row) in enumerate(MOD_SRC):
            db_ref[j:j + 1, :] = both(block, row)
            dm_ref[CTX_ROW, j:j + 1, :] = tot[block + row:block + row + 1, :]
            for r in range(N_DEV):
                dm_ref[r, j:j + 1, :] = g_ref[r, block + 8 + row:block + 8 + row + 1, :]
        gn_ref[...] = jnp.zeros(gn_ref.shape, F32)
        gn_ref[0:1, :] = both(PACK_N1, 2)
        gn_ref[8:9, :] = both(PACK_N2, 2)
        gn_ref[16:17, :] = both(PACK_N3, 2)
        gn_ref[24:25, :] = both(PACK_HEAD, 1)
        cv_ref[...] = jnp.zeros(cv_ref.shape, F32)
        for r in range(3):
            cv_ref[r:r + 1, :] = both(PACK_CONV, r)
        qk_ref[...] = jnp.zeros(qk_ref.shape, F32)
        qk_ref[0:1, 0:HEAD_DIM] = both(PACK_QK, 0)[:, 0:HEAD_DIM]
        qk_ref[0:1, HEAD_DIM:2 * HEAD_DIM] = both(PACK_QK, 1)[:, 0:HEAD_DIM]

    return pl.pallas_call(
        body, name=name,
        out_shape=[jax.ShapeDtypeStruct((8, 128), F32), jax.ShapeDtypeStruct((16, D), F32),
                   jax.ShapeDtypeStruct((32, D), F32), jax.ShapeDtypeStruct((8, D), F32),
                   jax.ShapeDtypeStruct((8, D), F32), jax.ShapeDtypeStruct((16, 16, D), F32)],
        in_specs=_whole(1), out_specs=_whole(6),
        compiler_params=pltpu.CompilerParams(vmem_limit_bytes=VMEM_LIMIT),
    )(gathered)


def _wmod_grad(cin, dm_sh, w_sh, name):
    def body(c_ref, d_ref, w_ref, gw_ref, cp_ref):
        a = _silu(c_ref[...]).astype(BF16)
        d = d_ref[...].astype(BF16)
        gw_ref[...] = lax.dot_general(a, d, (((0,), (0,)), ((), ())), preferred_element_type=F32)
        cp_ref[...] = lax.dot_general(d, w_ref[...].astype(BF16), (((1,), (1,)), ((), ())),
                                      preferred_element_type=F32)

    return pl.pallas_call(
        body, name=name,
        out_shape=[jax.ShapeDtypeStruct(w_sh.shape, F32), jax.ShapeDtypeStruct(cin.shape, F32)],
        in_specs=_whole(3), out_specs=_whole(2),
        compiler_params=pltpu.CompilerParams(vmem_limit_bytes=VMEM_LIMIT),
    )(cin, dm_sh, w_sh)


def _cctx_grad(parts, c_ctx8, name):
    def body(p_ref, c_ref, o_ref):
        tot = p_ref[0] + p_ref[2] + p_ref[4] + p_ref[6]
        cv = c_ref[...]
        sig = jax.nn.sigmoid(cv)
        rows = lax.broadcasted_iota(jnp.int32, tot.shape, 0)
        o_ref[...] = jnp.where(rows == 0, tot * (sig * (1.0 + cv * (1.0 - sig))), 0.0)

    return pl.pallas_call(
        body, name=name, out_shape=jax.ShapeDtypeStruct(c_ctx8.shape, F32),
        in_specs=_whole(2), out_specs=pl.BlockSpec(memory_space=pltpu.VMEM),
    )(parts, c_ctx8)


def _pad_rows(a, rows):
    return jnp.pad(a, ((0, rows - a.shape[0]), (0, 0)))


def _pack_small(c_ctx, b_mod, n1, n2, n3, final_g, gq, gk, conv_sh, D):
    misc = jnp.concatenate([gq, gk, conv_sh.reshape(1, -1)], axis=1)
    return jnp.concatenate([_pad_rows(c_ctx[None], 8), _pad_rows(b_mod.reshape(N_MOD, D), 16), _pad_rows(n1, 8),
                            _pad_rows(n2, 8), _pad_rows(n3, 8), _pad_rows(final_g[None], 8), _pad_rows(misc, 8)], axis=0)


def _unpack_small(p, D, conv_shape):
    misc = p[56:57]
    return dict(c_ctx=p[0], b_mod=p[8:8 + N_MOD].reshape(1, N_MOD * D), norm1_g=p[24:25], norm2_g=p[32:33],
                norm3_g=p[40:41], final_g=p[48], q_norm_g=misc[:, 0:HEAD_DIM], k_norm_g=misc[:, HEAD_DIM:2 * HEAD_DIM],
                conv_w=misc[:, 2 * HEAD_DIM:].reshape(conv_shape))


WEIGHT_ORDER = ("c_ctx", "w_mod", "b_mod", "norm1_g", "norm2_g", "norm3_g", "ffn1_w_in", "ffn1_w_out", "w_in",
                "conv_w", "q_norm_g", "k_norm_g", "w_branch_conv", "w_branch_attn", "w_out", "ffn2_w_in",
                "ffn2_w_out", "final_g")
BIG = ("ffn1_w_in", "ffn1_w_out", "w_in", "w_branch_conv", "w_branch_attn", "w_out", "ffn2_w_in", "ffn2_w_out")
COLUMN_SHARDED = ("ffn1_w_in", "w_in", "ffn2_w_in")


def kernel(x, c, ctx, c_ctx, w_mod, b_mod, norm1_g, norm2_g, norm3_g, ffn1_w_in, ffn1_w_out, w_in, conv_w, q_norm_g, k_norm_g, w_branch_conv, w_branch_attn, w_out, ffn2_w_in, ffn2_w_out, final_g, loss_target, m_c_ctx, m_w_mod, m_b_mod, m_norm1_g, m_norm2_g, m_norm3_g, m_ffn1_w_in, m_ffn1_w_out, m_w_in, m_conv_w, m_q_norm_g, m_k_norm_g, m_w_branch_conv, m_w_branch_attn, m_w_out, m_ffn2_w_in, m_ffn2_w_out, m_final_g, v_c_ctx, v_w_mod, v_b_mod, v_norm1_g, v_norm2_g, v_norm3_g, v_ffn1_w_in, v_ffn1_w_out, v_w_in, v_conv_w, v_q_norm_g, v_k_norm_g, v_w_branch_conv, v_w_branch_attn, v_w_out, v_ffn2_w_in, v_ffn2_w_out, v_final_g):
    w = dict(c_ctx=c_ctx, w_mod=w_mod, b_mod=b_mod, norm1_g=norm1_g, norm2_g=norm2_g, norm3_g=norm3_g,
             ffn1_w_in=ffn1_w_in, ffn1_w_out=ffn1_w_out, w_in=w_in, conv_w=conv_w, q_norm_g=q_norm_g,
             k_norm_g=k_norm_g, w_branch_conv=w_branch_conv, w_branch_attn=w_branch_attn, w_out=w_out,
             ffn2_w_in=ffn2_w_in, ffn2_w_out=ffn2_w_out, final_g=final_g)
    m = dict(c_ctx=m_c_ctx, w_mod=m_w_mod, b_mod=m_b_mod, norm1_g=m_norm1_g, norm2_g=m_norm2_g, norm3_g=m_norm3_g,
             ffn1_w_in=m_ffn1_w_in, ffn1_w_out=m_ffn1_w_out, w_in=m_w_in, conv_w=m_conv_w, q_norm_g=m_q_norm_g,
             k_norm_g=m_k_norm_g, w_branch_conv=m_w_branch_conv, w_branch_attn=m_w_branch_attn, w_out=m_w_out,
             ffn2_w_in=m_ffn2_w_in, ffn2_w_out=m_ffn2_w_out, final_g=m_final_g)
    v = dict(c_ctx=v_c_ctx, w_mod=v_w_mod, b_mod=v_b_mod, norm1_g=v_norm1_g, norm2_g=v_norm2_g, norm3_g=v_norm3_g,
             ffn1_w_in=v_ffn1_w_in, ffn1_w_out=v_ffn1_w_out, w_in=v_w_in, conv_w=v_conv_w, q_norm_g=v_q_norm_g,
             k_norm_g=v_k_norm_g, w_branch_conv=v_w_branch_conv, w_branch_attn=v_w_branch_attn, w_out=v_w_out,
             ffn2_w_in=v_ffn2_w_in, ffn2_w_out=v_ffn2_w_out, final_g=v_final_g)

    xi, yi, ci = _place()
    dev = 4 * xi + 2 * yi + ci
    shard = 2 * xi + yi
    idx = jnp.stack([ci, shard, 2 * (1 - xi) + yi, 2 * xi + (1 - yi), 2 * (1 - xi) + (1 - yi)]).astype(jnp.int32)
    D = x.shape[-1]
    ctx_len = ctx.shape[1]
    assert ctx_len == ROW and c.shape == (1, D)
    mcols = w_mod.shape[2]
    ccols = conv_w.shape[2]

    c_all = _allgather8(jnp.broadcast_to(c, (8, D)), "ag_c")[:, 0, :]
    cin = jnp.concatenate([c_all, _pad_rows(c_ctx[None], 8)], axis=0)
    b_sh = lax.dynamic_slice(b_mod, (0, shard * mcols), (1, mcols))
    mod_sh = _mod_rows(cin, w_mod[0], b_sh, "mod_rows")
    conv_rows = jnp.pad(conv_w[0], ((0, 8 - conv_w.shape[1]), (0, mcols - ccols)))
    mod_all = _allgather8(jnp.concatenate([mod_sh, conv_rows], axis=0), "ag_mod")
    mod_full = jnp.concatenate([mod_all[2 * s, :16] for s in range(N_CHIPS)], axis=1)
    conv_full = jnp.concatenate([mod_all[2 * s, 16:16 + conv_w.shape[1], :ccols] for s in range(N_CHIPS)], axis=1)
    mod_lat = lax.dynamic_slice(mod_full, (dev, 0), (1, N_MOD * D)).reshape(N_MOD, D)
    mod_ctx = mod_full[CTX_ROW].reshape(N_MOD, D)
    mods = jnp.stack([_pad_rows(mod_ctx, 16), _pad_rows(mod_lat, 16)])

    fulls = [_place_shard(w[n][0], idx, n in COLUMN_SHARDED, "place_" + n) for n in BIG]
    wts = _weights_allgather(fulls, "ag_weights")

    xcat = jnp.concatenate([ctx[0], x[0]], axis=0)
    grad_x, grads, accs = _local_step(xcat, loss_target[0], mods, (norm1_g, norm2_g, norm3_g), final_g[None],
                                      q_norm_g, k_norm_g, conv_full, wts, ctx_len)

    g = {}
    reduced = dict(zip(BIG, _reduce_scatter(list(grads), idx)))

    pack = jnp.concatenate([a.reshape(2 * ACC_ROWS, D) for a in accs], axis=0)
    gathered = _allgather8(pack, "ag_small")
    loss8, db_mod, g_norms, g_conv, g_qk, dm = _small_reduce(gathered, "small_reduce")
    dm_sh = lax.dynamic_slice(dm[:, :N_MOD, :].reshape(16, N_MOD * D), (0, shard * mcols), (16, mcols))
    g_wmod, cpart = _wmod_grad(cin, dm_sh, w_mod[0], "wmod_grad")
    g["w_mod"] = g_wmod[None]
    cparts = _allgather8(cpart[CTX_ROW:CTX_ROW + 8], "ag_cctx")
    g_cctx = _cctx_grad(cparts, _pad_rows(c_ctx[None], 8), "cctx_grad")
    g_conv_sh = lax.dynamic_slice(g_conv, (0, shard * ccols), (conv_w.shape[1], ccols))
    g_misc = jnp.concatenate([g_qk[0:1, 0:2 * HEAD_DIM], g_conv_sh.reshape(1, -1)], axis=1)
    g_pack = jnp.concatenate([g_cctx, db_mod, g_norms, _pad_rows(g_misc, 8)], axis=0)

    def packed(p):
        return _pack_small(p["c_ctx"], p["b_mod"], p["norm1_g"], p["norm2_g"], p["norm3_g"], p["final_g"],
                           p["q_norm_g"], p["k_norm_g"], p["conv_w"][0], D)

    d_pack, m_pack, v_pack = _adamw(packed(w), g_pack, packed(m), packed(v), "adamw_small")
    g.update(_unpack_small(g_pack, D, conv_w.shape))
    delta = _unpack_small(d_pack, D, conv_w.shape)
    new_m = _unpack_small(m_pack, D, conv_w.shape)
    new_v = _unpack_small(v_pack, D, conv_w.shape)
    for n in BIG + ("w_mod",):
        if n in COLUMN_SHARDED:
            g2, d2, m2, v2 = _adamw_transposed(w[n][0], reduced[n], m[n][0], v[n][0], "adamw_" + n)
        else:
            g2 = reduced[n] if n in reduced else g[n][0]
            d2, m2, v2 = _adamw(w[n][0], g2, m[n][0], v[n][0], "adamw_" + n)
        g[n], delta[n], new_m[n], new_v[n] = g2[None], d2[None], m2[None], v2[None]

    loss = loss8[0, 0]
    return (loss, grad_x[None], *[g[n] for n in WEIGHT_ORDER], *[delta[n] for n in WEIGHT_ORDER],
            *[new_m[n] for n in WEIGHT_ORDER], *[new_v[n] for n in WEIGHT_ORDER])
```

```python
import functools

import jax
import jax.numpy as jnp
from jax import lax
from jax.experimental import pallas as pl
from jax.experimental.pallas import tpu as pltpu

F32 = jnp.float32
BF16 = jnp.bfloat16

HEAD_DIM = 128
N_Q_HEADS = 8
N_KV_HEADS = 2
GROUP = N_Q_HEADS // N_KV_HEADS
GRID_W = 64
ROPE_THETA = 10000.0
EPS = 1e-6
ATTN_SCALE = HEAD_DIM ** -0.5

ADAM_LR = 0.001
ADAM_B1 = 0.9
ADAM_B2 = 0.999
ADAM_EPS = 1e-08
ADAM_WD = 0.01
ADAM_STEP = 10

ROW = 256
HALO = 8
ACC_ROWS = 8
N_CHIPS = 4
N_DEV = 8
MESH = pl.DeviceIdType.MESH
VMEM_LIMIT = 48 * 1024 * 1024
ADAMW_BLOCK_BYTES = 1024 * 1024


def _pick(n, prefs):
    for p in prefs:
        if n % p == 0:
            return p
    return n


def _params(sem):
    return pltpu.CompilerParams(dimension_semantics=sem, vmem_limit_bytes=VMEM_LIMIT)


def _stream(i):
    return jnp.minimum(i, 1)


def _matmul(a, b, mode, out_dtype, name, tm=None, tn=None, tk=None):
    if mode == "nn":
        (M, K), (K2, N) = a.shape, b.shape
    elif mode == "nt":
        (M, K), (N, K2) = a.shape, b.shape
    else:
        (K, M), (K2, N) = a.shape, b.shape
    assert K == K2, (a.shape, b.shape, mode)
    tm = tm or _pick(M, (1664, 1408, 1024, 512, 256, 128) if mode == "tn" else (1408, 768, 512, 256, 128))
    tn = tn or _pick(N, (1664, 1408, 1024, 512, 256, 128))
    tk = tk or _pick(K, (1664, 1408, 1024, 768, 512, 256, 128))
    nk = K // tk
    if mode == "tn":
        a_spec = pl.BlockSpec((tk, tm), lambda i, j, k: (k, i))
    else:
        a_spec = pl.BlockSpec((tm, tk), lambda i, j, k: (i, k))
    if mode == "nt":
        b_spec = pl.BlockSpec((tn, tk), lambda i, j, k: (j, k))
    else:
        b_spec = pl.BlockSpec((tk, tn), lambda i, j, k: (k, j))
    dims = {"nn": ((1,), (0,)), "nt": ((1,), (1,)), "tn": ((0,), (0,))}[mode]
    use_scratch = nk > 1 and out_dtype != F32

    def body(a_ref, b_ref, o_ref, *scratch):
        p = lax.dot_general(a_ref[...].astype(BF16), b_ref[...].astype(BF16), (dims, ((), ())),
                            preferred_element_type=F32)
        if nk == 1:
            o_ref[...] = p.astype(o_ref.dtype)
            return
        acc_ref = scratch[0] if use_scratch else o_ref
        k = pl.program_id(2)

        @pl.when(k == 0)
        def _():
            acc_ref[...] = p

        @pl.when(k > 0)
        def _():
            acc_ref[...] += p

        if use_scratch:
            @pl.when(k == nk - 1)
            def _():
                o_ref[...] = acc_ref[...].astype(o_ref.dtype)

    return pl.pallas_call(
        body, name=name,
        grid=(M // tm, N // tn, nk),
        in_specs=[a_spec, b_spec],
        out_specs=pl.BlockSpec((tm, tn), lambda i, j, k: (i, j)),
        out_shape=jax.ShapeDtypeStruct((M, N), out_dtype),
        scratch_shapes=[pltpu.VMEM((tm, tn), F32)] if use_scratch else [],
        compiler_params=_params(("parallel", "parallel", "arbitrary")),
    )(a, b)


def _row_spec(width, col=0):
    return pl.BlockSpec((ROW, width), lambda i, col=col: (i, col))


def _mods_spec(D):
    return pl.BlockSpec((1, 16, D), lambda i: (_stream(i), 0, 0))


def _acc_spec(D):
    return pl.BlockSpec((1, ACC_ROWS, D), lambda i: (_stream(i), 0, 0))


def _vec_spec(rows, D):
    return pl.BlockSpec((rows, D), lambda i: (0, 0))


def _acc_init(acc_ref):
    i = pl.program_id(0)

    @pl.when(i <= 1)
    def _():
        acc_ref[...] = jnp.zeros_like(acc_ref)


def _acc_add(acc_ref, row, val):
    acc_ref[0, row:row + 1, :] += jnp.sum(val, axis=0, keepdims=True)


def _resid_rmsmod_fwd(xprev, branch, mods, g, gate, shift_idx, scale_idx, name):
    T, D = xprev.shape
    has_res = branch is not None

    def body(*refs):
        if has_res:
            x_ref, f_ref, m_ref, g_ref, xo_ref, h_ref = refs
        else:
            x_ref, m_ref, g_ref, h_ref = refs
        m = m_ref[0]
        x = x_ref[...]
        if has_res:
            gate_idx, fac = gate
            x = x + (fac * m[gate_idx:gate_idx + 1, :]) * f_ref[...]
            xo_ref[...] = x
        inv = lax.rsqrt(jnp.mean(x * x, axis=-1, keepdims=True) + EPS)
        y = (x * inv) * g_ref[...]
        h = y * (1.0 + m[scale_idx:scale_idx + 1, :]) + m[shift_idx:shift_idx + 1, :]
        h_ref[...] = h.astype(BF16)

    in_specs = [_row_spec(D)] + ([_row_spec(D)] if has_res else []) + [_mods_spec(D), _vec_spec(1, D)]
    args = [xprev] + ([branch] if has_res else []) + [mods, g]
    out_specs = ([_row_spec(D)] if has_res else []) + [_row_spec(D)]
    out_shape = ([jax.ShapeDtypeStruct((T, D), F32)] if has_res else []) + [jax.ShapeDtypeStruct((T, D), BF16)]
    out = pl.pallas_call(
        body, name=name, grid=(T // ROW,), in_specs=in_specs, out_specs=out_specs, out_shape=out_shape,
        compiler_params=_params(("parallel",)),
    )(*args)
    return out if has_res else (None, out[0])


def _loss_head(x2, f2, mods, final_g, target, name):
    T, D = x2.shape
    nt = T // ROW

    def body(x_ref, f_ref, m_ref, g_ref, t_ref, dx_ref, df_ref, acc_ref):
        _acc_init(acc_ref)
        i = pl.program_id(0)
        m = m_ref[0]
        gate = 0.5 * m[8:9, :]
        f = f_ref[...]
        x = x_ref[...] + gate * f
        inv = lax.rsqrt(jnp.mean(x * x, axis=-1, keepdims=True) + EPS)
        xn = x * inv
        fg = g_ref[...]
        lat = (i > 0).astype(F32)
        e = (xn * fg - t_ref[...]) * lat
        dy = e * (1.0 / D)
        dxn = dy * fg
        dx = inv * (dxn - xn * jnp.mean(dxn * xn, axis=-1, keepdims=True))
        dx_ref[...] = dx
        df_ref[...] = (gate * dx).astype(BF16)
        _acc_add(acc_ref, 0, (0.5 / D) * e * e)
        _acc_add(acc_ref, 1, dy * xn)
        _acc_add(acc_ref, 2, 0.5 * dx * f)

    return pl.pallas_call(
        body, name=name, grid=(nt,),
        in_specs=[_row_spec(D), _row_spec(D), _mods_spec(D), _vec_spec(1, D),
                  pl.BlockSpec((ROW, D), lambda i: (jnp.maximum(i - 1, 0), 0))],
        out_specs=[_row_spec(D), _row_spec(D), _acc_spec(D)],
        out_shape=[jax.ShapeDtypeStruct((T, D), F32), jax.ShapeDtypeStruct((T, D), BF16),
                   jax.ShapeDtypeStruct((2, ACC_ROWS, D), F32)],
        compiler_params=_params(("arbitrary",)),
    )(x2, f2, mods, final_g, target)


def _rmsmod_bwd(x, dh, dres, mods, g, shift_idx, scale_idx, gate, branch, name, skip_first_tile=False):
    T, D = x.shape
    nt = T // ROW
    has_gate = gate is not None

    def body(*refs):
        if has_gate:
            x_ref, dh_ref, dr_ref, b_ref, m_ref, g_ref, dx_ref, db_ref, acc_ref = refs
        else:
            x_ref, dh_ref, dr_ref, m_ref, g_ref, dx_ref, acc_ref = refs
        _acc_init(acc_ref)
        m = m_ref[0]
        x = x_ref[...]
        dh = dh_ref[...]
        gg = g_ref[...]
        inv = lax.rsqrt(jnp.mean(x * x, axis=-1, keepdims=True) + EPS)
        xn = x * inv
        y = xn * gg
        dy = dh * (1.0 + m[scale_idx:scale_idx + 1, :])
        dxn = dy * gg
        dx = inv * (dxn - xn * jnp.mean(dxn * xn, axis=-1, keepdims=True)) + dr_ref[...]
        dx_ref[...] = dx
        _acc_add(acc_ref, 0, dh)
        _acc_add(acc_ref, 1, dh * y)
        _acc_add(acc_ref, 2, dy * xn)
        if has_gate:
            gate_idx, fac = gate
            b = b_ref[...]
            db_ref[...] = ((fac * m[gate_idx:gate_idx + 1, :]) * dx).astype(BF16)
            _acc_add(acc_ref, 3, fac * dx * b)

    in_specs = [_row_spec(D), _row_spec(D), _row_spec(D)] + ([_row_spec(D)] if has_gate else []) + \
               [_mods_spec(D), _vec_spec(1, D)]
    args = [x, dh, dres] + ([branch] if has_gate else []) + [mods, g]
    if skip_first_tile:
        dx_spec = pl.BlockSpec((ROW, D), lambda i: (jnp.maximum(i - 1, 0), 0))
        dx_shape = jax.ShapeDtypeStruct((T - ROW, D), F32)
    else:
        dx_spec = _row_spec(D)
        dx_shape = jax.ShapeDtypeStruct((T, D), F32)
    out_specs = [dx_spec] + ([_row_spec(D)] if has_gate else []) + [_acc_spec(D)]
    out_shape = [dx_shape] + ([jax.ShapeDtypeStruct((T, D), BF16)] if has_gate else []) + \
                [jax.ShapeDtypeStruct((2, ACC_ROWS, D), F32)]
    out = pl.pallas_call(
        body, name=name, grid=(nt,), in_specs=in_specs, out_specs=out_specs, out_shape=out_shape,
        compiler_params=_params(("arbitrary",)),
    )(*args)
    if has_gate:
        return out
    return out[0], None, out[1]


FFN_ROWS = 384
_NT = (((1,), (1,)), ((), ()))


def _ffn_chunk(F):
    return _pick(F, (1408, 512, 256, 128))


def _resident():
    return pl.BlockSpec(memory_space=pltpu.VMEM)


def _ffn_fwd(h, w_in_t, w_out, name):
    T, D = h.shape
    F = w_out.shape[0]
    cw = _ffn_chunk(F)
    tm = _pick(T, (FFN_ROWS, ROW))

    def body(h_ref, wi_ref, wo_ref, u_ref, s_ref, f_ref):
        hv = h_ref[...]
        acc = jnp.zeros((tm, D), F32)
        for j in range(F // cw):
            a = lax.dot_general(hv, wi_ref[j * cw:(j + 1) * cw, :], _NT, preferred_element_type=F32)
            b = lax.dot_general(hv, wi_ref[F + j * cw:F + (j + 1) * cw, :], _NT, preferred_element_type=F32)
            s = ((a * jax.nn.sigmoid(a)) * b).astype(BF16)
            u_ref[:, j * cw:(j + 1) * cw] = a.astype(BF16)
            u_ref[:, F + j * cw:F + (j + 1) * cw] = b.astype(BF16)
            s_ref[:, j * cw:(j + 1) * cw] = s
            acc = acc + jnp.dot(s, wo_ref[j * cw:(j + 1) * cw, :], preferred_element_type=F32)
        f_ref[...] = acc

    row = lambda w: pl.BlockSpec((tm, w), lambda i: (i, 0))
    return pl.pallas_call(
        body, name=name, grid=(T // tm,),
        in_specs=[row(D), _resident(), _resident()],
        out_specs=[row(2 * F), row(F), row(D)],
        out_shape=[jax.ShapeDtypeStruct((T, 2 * F), BF16), jax.ShapeDtypeStruct((T, F), BF16),
                   jax.ShapeDtypeStruct((T, D), F32)],
        compiler_params=_params(("parallel",)),
    )(h, w_in_t, w_out)


def _ffn_bwd(df, u, w_in_t, w_out, name):
    T, D = df.shape
    F = w_out.shape[0]
    cw = _ffn_chunk(F)
    tm = _pick(T, (FFN_ROWS, ROW))

    def body(df_ref, u_ref, wi_ref, wo_ref, du_ref, dh_ref):
        dfv = df_ref[...]
        acc = jnp.zeros((tm, D), F32)
        for j in range(F // cw):
            ds = lax.dot_general(dfv, wo_ref[j * cw:(j + 1) * cw, :], _NT, preferred_element_type=F32)
            a = u_ref[:, j * cw:(j + 1) * cw].astype(F32)
            b = u_ref[:, F + j * cw:F + (j + 1) * cw].astype(F32)
            sig = jax.nn.sigmoid(a)
            da = (ds * b * (sig * (1.0 + a * (1.0 - sig)))).astype(BF16)
            db = (ds * (a * sig)).astype(BF16)
            du_ref[:, j * cw:(j + 1) * cw] = da
            du_ref[:, F + j * cw:F + (j + 1) * cw] = db
            acc = acc + jnp.dot(da, wi_ref[j * cw:(j + 1) * cw, :], preferred_element_type=F32)
            acc = acc + jnp.dot(db, wi_ref[F + j * cw:F + (j + 1) * cw, :], preferred_element_type=F32)
        dh_ref[...] = acc

    row = lambda w: pl.BlockSpec((tm, w), lambda i: (i, 0))
    return pl.pallas_call(
        body, name=name, grid=(T // tm,),
        in_specs=[row(D), row(2 * F), _resident(), _resident()],
        out_specs=[row(2 * F), row(D)],
        out_shape=[jax.ShapeDtypeStruct((T, 2 * F), BF16), jax.ShapeDtypeStruct((T, D), F32)],
        compiler_params=_params(("parallel",)),
    )(df, u, w_in_t, w_out)


def _halo_specs(width, col, nt):
    per = ROW // HALO
    prev = pl.BlockSpec((HALO, width), lambda i, col=col: (jnp.maximum(i * per - 1, 0), col))
    nxt = pl.BlockSpec((HALO, width), lambda i, col=col: (jnp.minimum((i + 1) * per, nt * per - 1), col))
    return prev, nxt


def _shift_rows(v, prev_row, next_row):
    rows = lax.broadcasted_iota(jnp.int32, v.shape, 0)
    down = jnp.where(rows == 0, prev_row, pltpu.roll(v, 1, 0))
    up = jnp.where(rows == v.shape[0] - 1, next_row, pltpu.roll(v, v.shape[0] - 1, 0))
    return down, up


def _conv_fwd(P, conv_w, D, name):
    T = P.shape[0]
    nt = T // ROW
    cg_p, cg_n = _halo_specs(D, 1, nt)
    vc_p, vc_n = _halo_specs(D, 2, nt)

    def body(bg_ref, cg_ref, vc_ref, cgp_ref, vcp_ref, cgn_ref, vcn_ref, w_ref, y_ref):
        i = pl.program_id(0)
        has_prev = (i != 1).astype(F32)
        has_next = (i != nt - 1).astype(F32)
        u = cg_ref[...] * vc_ref[...]
        up_row = cgp_ref[HALO - 1:HALO, :] * vcp_ref[HALO - 1:HALO, :] * has_prev
        un_row = cgn_ref[0:1, :] * vcn_ref[0:1, :] * has_next
        um1, up1 = _shift_rows(u, up_row, un_row)
        w = w_ref[...]
        conv = um1 * w[0:1, :] + u * w[1:2, :] + up1 * w[2:3, :]
        y_ref[...] = (bg_ref[...] * conv).astype(BF16)

    return pl.pallas_call(
        body, name=name, grid=(nt,),
        in_specs=[_row_spec(D, 0), _row_spec(D, 1), _row_spec(D, 2), cg_p, vc_p, cg_n, vc_n, _vec_spec(3, D)],
        out_specs=_row_spec(D),
        out_shape=jax.ShapeDtypeStruct((T, D), BF16),
        compiler_params=_params(("parallel",)),
    )(P, P, P, P, P, P, P, conv_w)


def _conv_bwd(P, dy, conv_w, D, name):
    T = P.shape[0]
    nt = T // ROW
    bg_p, bg_n = _halo_specs(D, 0, nt)
    cg_p, cg_n = _halo_specs(D, 1, nt)
    vc_p, vc_n = _halo_specs(D, 2, nt)
    dy_p, dy_n = _halo_specs(D, 0, nt)

    def body(bg_ref, cg_ref, vc_ref, dy_ref, bgp_ref, cgp_ref, vcp_ref, dyp_ref,
             bgn_ref, cgn_ref, vcn_ref, dyn_ref, w_ref, o_ref, acc_ref):
        _acc_init(acc_ref)
        i = pl.program_id(0)
        lat = (i > 0).astype(F32)
        has_prev = (i != 1).astype(F32)
        has_next = (i != nt - 1).astype(F32)
        last = HALO - 1
        bg = bg_ref[...]
        cg = cg_ref[...]
        vc = vc_ref[...]
        dyv = dy_ref[...] * lat
        u = cg * vc
        up_row = cgp_ref[last:HALO, :] * vcp_ref[last:HALO, :] * has_prev
        un_row = cgn_ref[0:1, :] * vcn_ref[0:1, :] * has_next
        um1, up1 = _shift_rows(u, up_row, un_row)
        w = w_ref[...]
        conv = um1 * w[0:1, :] + u * w[1:2, :] + up1 * w[2:3, :]
        dc = dyv * bg
        dcp_row = dyp_ref[last:HALO, :] * bgp_ref[last:HALO, :] * has_prev
        dcn_row = dyn_ref[0:1, :] * bgn_ref[0:1, :] * has_next
        dcm1, dcp1 = _shift_rows(dc, dcp_row, dcn_row)
        du = dcp1 * w[0:1, :] + dc * w[1:2, :] + dcm1 * w[2:3, :]
        o_ref[:, 0:D] = (dyv * conv).astype(BF16)
        o_ref[:, D:2 * D] = (du * vc * lat).astype(BF16)
        o_ref[:, 2 * D:3 * D] = (du * cg * lat).astype(BF16)
        _acc_add(acc_ref, 0, dc * um1)
        _acc_add(acc_ref, 1, dc * u)
        _acc_add(acc_ref, 2, dc * up1)

    return pl.pallas_call(
        body, name=name, grid=(nt,),
        in_specs=[_row_spec(D, 0), _row_spec(D, 1), _row_spec(D, 2), _row_spec(D, 0),
                  bg_p, cg_p, vc_p, dy_p, bg_n, cg_n, vc_n, dy_n, _vec_spec(3, D)],
        out_specs=[_row_spec(3 * D), _acc_spec(D)],
        out_shape=[jax.ShapeDtypeStruct((T, 3 * D), BF16), jax.ShapeDtypeStruct((2, ACC_ROWS, D), F32)],
        compiler_params=_params(("arbitrary",)),
    )(P, P, P, dy, P, P, P, dy, P, P, P, dy, conv_w)


def _rope_tables(ctx_len, seq):
    n_freq = HEAD_DIM // 4
    rows = seq // GRID_W
    inv = ROPE_THETA ** (-jnp.arange(n_freq, dtype=F32) / n_freq)
    ar = jnp.arange(rows, dtype=F32)[:, None] * inv
    ac = jnp.arange(GRID_W, dtype=F32)[:, None] * inv

    def per_row(a):
        return jnp.repeat(a, GRID_W, axis=0)

    def per_col(a):
        return jnp.tile(a, (rows, 1))

    cos_t = jnp.concatenate([per_row(jnp.cos(ar)), per_row(jnp.cos(ar)), per_col(jnp.cos(ac)), per_col(jnp.cos(ac))], axis=1)
    sin_t = jnp.concatenate([per_row(-jnp.sin(ar)), per_row(jnp.sin(ar)), per_col(-jnp.sin(ac)), per_col(jnp.sin(ac))], axis=1)
    cos_t = jnp.concatenate([jnp.ones((ctx_len, HEAD_DIM), F32), cos_t], axis=0)
    sin_t = jnp.concatenate([jnp.zeros((ctx_len, HEAD_DIM), F32), sin_t], axis=0)
    return cos_t, sin_t


def _swap_halves(y):
    lanes = lax.broadcasted_iota(jnp.int32, y.shape, 1)
    first = (lanes % 64) < 32
    return jnp.where(first, pltpu.roll(y, HEAD_DIM - 32, 1), pltpu.roll(y, 32, 1))


def _qk_fwd(P, gq, gk, cos_t, sin_t, D, name):
    T = P.shape[0]
    QW = N_Q_HEADS * HEAD_DIM
    KW = N_KV_HEADS * HEAD_DIM
    q_col = (3 * D) // QW
    k_col = (3 * D + QW) // KW
    v_col = k_col + 1

    def body(q_ref, k_ref, v_ref, gq_ref, gk_ref, c_ref, s_ref, qo_ref, ko_ref, vo_ref):
        c = c_ref[...]
        s = s_ref[...]

        def head(x, g):
            inv = lax.rsqrt(jnp.mean(x * x, axis=-1, keepdims=True) + EPS)
            y = (x * inv) * g
            return y * c + _swap_halves(y) * s

        for h in range(N_Q_HEADS):
            sl = slice(h * HEAD_DIM, (h + 1) * HEAD_DIM)
            qo_ref[:, sl] = head(q_ref[:, sl], gq_ref[...]).astype(BF16)
        for h in range(N_KV_HEADS):
            sl = slice(h * HEAD_DIM, (h + 1) * HEAD_DIM)
            ko_ref[:, sl] = head(k_ref[:, sl], gk_ref[...]).astype(BF16)
        vo_ref[...] = v_ref[...].astype(BF16)

    return pl.pallas_call(
        body, name=name, grid=(T // ROW,),
        in_specs=[_row_spec(QW, q_col), _row_spec(KW, k_col), _row_spec(KW, v_col),
                  _vec_spec(1, HEAD_DIM), _vec_spec(1, HEAD_DIM), _row_spec(HEAD_DIM), _row_spec(HEAD_DIM)],
        out_specs=[_row_spec(QW), _row_spec(KW), _row_spec(KW)],
        out_shape=[jax.ShapeDtypeStruct((T, QW), BF16), jax.ShapeDtypeStruct((T, KW), BF16),
                   jax.ShapeDtypeStruct((T, KW), BF16)],
        compiler_params=_params(("parallel",)),
    )(P, P, P, gq, gk, cos_t, sin_t)


def _qk_bwd(P, dq, dk, dv, gq, gk, cos_t, sin_t, D, name):
    T = P.shape[0]
    QW = N_Q_HEADS * HEAD_DIM
    KW = N_KV_HEADS * HEAD_DIM
    q_col = (3 * D) // QW
    k_col = (3 * D + QW) // KW

    def body(q_ref, k_ref, dq_ref, dk_ref, dv_ref, gq_ref, gk_ref, c_ref, s_ref, o_ref, acc_ref):
        _acc_init(acc_ref)
        c = c_ref[...]
        s = s_ref[...]

        def head(x, d, g):
            dyv = d * c + _swap_halves(d * s)
            inv = lax.rsqrt(jnp.mean(x * x, axis=-1, keepdims=True) + EPS)
            xn = x * inv
            dxn = dyv * g
            dx = inv * (dxn - xn * jnp.mean(dxn * xn, axis=-1, keepdims=True))
            return dx, jnp.sum(dyv * xn, axis=0, keepdims=True)

        dgq = jnp.zeros((1, HEAD_DIM), F32)
        for h in range(N_Q_HEADS):
            sl = slice(h * HEAD_DIM, (h + 1) * HEAD_DIM)
            dx, dg = head(q_ref[:, sl], dq_ref[:, sl], gq_ref[...])
            o_ref[:, sl] = dx.astype(BF16)
            dgq = dgq + dg
        dgk = jnp.zeros((1, HEAD_DIM), F32)
        for h in range(N_KV_HEADS):
            sl = slice(h * HEAD_DIM, (h + 1) * HEAD_DIM)
            dx, dg = head(k_ref[:, sl], dk_ref[:, sl], gk_ref[...])
            o_ref[:, QW + h * HEAD_DIM:QW + (h + 1) * HEAD_DIM] = dx.astype(BF16)
            dgk = dgk + dg
        o_ref[:, QW + KW:QW + 2 * KW] = dv_ref[...].astype(BF16)
        acc_ref[0, 0:1, 0:HEAD_DIM] += dgq
        acc_ref[0, 1:2, 0:HEAD_DIM] += dgk

    return pl.pallas_call(
        body, name=name, grid=(T // ROW,),
        in_specs=[_row_spec(QW, q_col), _row_spec(KW, k_col), _row_spec(QW), _row_spec(KW), _row_spec(KW),
                  _vec_spec(1, HEAD_DIM), _vec_spec(1, HEAD_DIM), _row_spec(HEAD_DIM), _row_spec(HEAD_DIM)],
        out_specs=[_row_spec(QW + 2 * KW), _acc_spec(D)],
        out_shape=[jax.ShapeDtypeStruct((T, QW + 2 * KW), BF16), jax.ShapeDtypeStruct((2, ACC_ROWS, D), F32)],
        compiler_params=_params(("arbitrary",)),
    )(P, P, dq, dk, dv, gq, gk, cos_t, sin_t)


def _to_row(col, n):
    return jnp.transpose(jnp.broadcast_to(col, (n, HEAD_DIM)))[0:1, :]


LOG2E = 1.4426950408889634
ATTN_SPLIT = 2
ATTN_PARTS = 4
ATTN_KEY_CHUNK = 768


def _flash_fwd(q, k, v, name, tq=ROW, tk=None):
    T = q.shape[0]
    tk = tk or _pick(T, (1408, 768, 512, 256))
    ck = tk
    nk = T // tk
    GW = GROUP * HEAD_DIM

    def body(q_ref, k_ref, v_ref, o_ref, lse_ref, qs_ref, m_ref, l_ref, acc_ref, st_ref):
        ki = pl.program_id(2)

        @pl.when(ki == 0)
        def _():
            for g in range(GROUP):
                qs_ref[g * tq:(g + 1) * tq, :] = q_ref[:, g * HEAD_DIM:(g + 1) * HEAD_DIM]
            m_ref[...] = jnp.full(m_ref.shape, -jnp.inf, F32)
            l_ref[...] = jnp.zeros(l_ref.shape, F32)
            acc_ref[...] = jnp.zeros(acc_ref.shape, F32)

        w = GROUP * tq // ATTN_PARTS
        nck = tk // ck

        def lanes(p):
            return slice(p * w, (p + 1) * w)

        def keys(c):
            return slice(c * ck, (c + 1) * ck)

        def fold(a):
            return a.reshape(ck // 8, 8, w)

        def scores(p, c):
            st = lax.dot_general(k_ref[keys(c), :], qs_ref[lanes(p), :], _NT,
                                 preferred_element_type=F32) * (ATTN_SCALE * LOG2E)
            st_ref[keys(c), lanes(p)] = st
            return jnp.max(fold(st), axis=0)

        def new_max(p, partial):
            m_prev = m_ref[:, lanes(p)]
            m_new = jnp.maximum(m_prev, jnp.max(functools.reduce(jnp.maximum, partial), axis=0, keepdims=True))
            m_ref[:, lanes(p)] = m_new
            return m_new, jnp.exp2(m_prev - m_new)

        def weights(p, c, m_new):
            pt = jnp.exp2(st_ref[keys(c), lanes(p)] - m_new)
            pv = lax.dot_general(v_ref[keys(c), :], pt.astype(BF16), (((0,), (0,)), ((), ())),
                                 preferred_element_type=F32)
            return jnp.sum(fold(pt), axis=0), pv

        partial = [scores(0, c) for c in range(nck)]
        for p in range(ATTN_PARTS):
            m_new, alpha = new_max(p, partial)
            partial, sums, pvs = [], [], []
            for c in range(nck):
                if p + 1 < ATTN_PARTS:
                    partial.append(scores(p + 1, c))
                s8, pv = weights(p, c, m_new)
                sums.append(s8)
                pvs.append(pv)
            l_ref[:, lanes(p)] = alpha * l_ref[:, lanes(p)] + jnp.sum(sum(sums), axis=0, keepdims=True)
            acc_ref[:, lanes(p)] = alpha * acc_ref[:, lanes(p)] + sum(pvs)

        @pl.when(ki == nk - 1)
        def _():
            out = jnp.transpose(acc_ref[...] / l_ref[...])
            lse = m_ref[...] + jnp.log2(l_ref[...])
            for g in range(GROUP):
                o_ref[:, g * HEAD_DIM:(g + 1) * HEAD_DIM] = out[g * tq:(g + 1) * tq, :]
                lse_ref[0, g:g + 1, :] = lse[:, g * tq:(g + 1) * tq]

    return pl.pallas_call(
        body, name=name, grid=(N_KV_HEADS, T // tq, nk),
        in_specs=[pl.BlockSpec((tq, GW), lambda h, i, j: (i, h)),
                  pl.BlockSpec((tk, HEAD_DIM), lambda h, i, j: (j, h)),
                  pl.BlockSpec((tk, HEAD_DIM), lambda h, i, j: (j, h))],
        out_specs=[pl.BlockSpec((tq, GW), lambda h, i, j: (i, h)),
                   pl.BlockSpec((1, GROUP, tq), lambda h, i, j: (h, 0, i))],
        out_shape=[jax.ShapeDtypeStruct((T, N_Q_HEADS * HEAD_DIM), F32),
                   jax.ShapeDtypeStruct((N_KV_HEADS, GROUP, T), F32)],
        scratch_shapes=[pltpu.VMEM((GROUP * tq, HEAD_DIM), BF16), pltpu.VMEM((1, GROUP * tq), F32),
                        pltpu.VMEM((1, GROUP * tq), F32), pltpu.VMEM((HEAD_DIM, GROUP * tq), F32),
                        pltpu.VMEM((tk, GROUP * tq), F32)],
        compiler_params=_params(("parallel", "parallel", "arbitrary")),
    )(q, k, v)


def _attn_delta(do, o, name):
    T, QW = do.shape

    def body(do_ref, o_ref, dob_ref, dl_ref):
        dov = do_ref[...]
        dob_ref[...] = dov.astype(BF16)
        prod = dov * o_ref[...]
        for h in range(N_Q_HEADS):
            d = jnp.sum(prod[:, h * HEAD_DIM:(h + 1) * HEAD_DIM], axis=1, keepdims=True)
            dl_ref[h // GROUP, (h % GROUP):(h % GROUP) + 1, :] = _to_row(d, ROW)

    return pl.pallas_call(
        body, name=name, grid=(T // ROW,),
        in_specs=[_row_spec(QW), _row_spec(QW)],
        out_specs=[_row_spec(QW), pl.BlockSpec((N_KV_HEADS, GROUP, ROW), lambda i: (0, 0, i))],
        out_shape=[jax.ShapeDtypeStruct((T, QW), BF16), jax.ShapeDtypeStruct((N_KV_HEADS, GROUP, T), F32)],
        compiler_params=_params(("parallel",)),
    )(do, o)


def _flash_bwd(q, k, v, do, lse, delta, name, tq=ROW, tk=None):
    T = q.shape[0]
    tk = tk or _pick(T, (1408, 768, 512, 256))
    nk = T // tk
    GW = GROUP * HEAD_DIM
    nt = (((1,), (1,)), ((), ()))

    def body(q_ref, do_ref, k_ref, v_ref, lse_ref, dl_ref, dq_ref, dk_ref, dv_ref, qs_ref, dos_ref, dqt_ref):
        qi = pl.program_id(1)
        ki = pl.program_id(2)

        @pl.when(ki == 0)
        def _():
            for g in range(GROUP):
                qs_ref[g * tq:(g + 1) * tq, :] = q_ref[:, g * HEAD_DIM:(g + 1) * HEAD_DIM]
                dos_ref[g * tq:(g + 1) * tq, :] = do_ref[:, g * HEAD_DIM:(g + 1) * HEAD_DIM]
            dqt_ref[...] = jnp.zeros(dqt_ref.shape, F32)

        kk = k_ref[...]
        vv = v_ref[...]

        def lanes(p):
            return slice(p * tq, (p + 1) * tq)

        def products(p):
            st = lax.dot_general(kk, qs_ref[lanes(p), :], nt, preferred_element_type=F32)
            dpt = lax.dot_general(vv, dos_ref[lanes(p), :], nt, preferred_element_type=F32)
            return st, dpt

        dk_c = jnp.zeros((tk, HEAD_DIM), F32)
        dv_c = jnp.zeros((tk, HEAD_DIM), F32)
        ahead = products(0)
        for p in range(GROUP):
            st, dpt = ahead
            if p + 1 < GROUP:
                ahead = products(p + 1)
            pt = jnp.exp2(st * (ATTN_SCALE * LOG2E) - lse_ref[0, p:p + 1, :])
            dst = ((pt * (dpt - dl_ref[0, p:p + 1, :])) * ATTN_SCALE).astype(BF16)
            dv_c = dv_c + jnp.dot(pt.astype(BF16), dos_ref[lanes(p), :], preferred_element_type=F32)
            dk_c = dk_c + jnp.dot(dst, qs_ref[lanes(p), :], preferred_element_type=F32)
            dqt_ref[:, lanes(p)] += lax.dot_general(kk, dst, (((0,), (0,)), ((), ())), preferred_element_type=F32)
        rows = pl.ds(pl.multiple_of(ki * tk, tk), tk)

        @pl.when(qi == 0)
        def _():
            dk_ref[rows, :] = dk_c
            dv_ref[rows, :] = dv_c

        @pl.when(qi > 0)
        def _():
            dk_ref[rows, :] += dk_c
            dv_ref[rows, :] += dv_c

        @pl.when(ki == nk - 1)
        def _():
            dqv = jnp.transpose(dqt_ref[...])
            for g in range(GROUP):
                dq_ref[:, g * HEAD_DIM:(g + 1) * HEAD_DIM] = dqv[g * tq:(g + 1) * tq, :]

    return pl.pallas_call(
        body, name=name, grid=(N_KV_HEADS, T // tq, nk),
        in_specs=[pl.BlockSpec((tq, GW), lambda h, i, j: (i, h)),
                  pl.BlockSpec((tq, GW), lambda h, i, j: (i, h)),
                  pl.BlockSpec((tk, HEAD_DIM), lambda h, i, j: (j, h)),
                  pl.BlockSpec((tk, HEAD_DIM), lambda h, i, j: (j, h)),
                  pl.BlockSpec((1, GROUP, tq), lambda h, i, j: (h, 0, i)),
                  pl.BlockSpec((1, GROUP, tq), lambda h, i, j: (h, 0, i))],
        out_specs=[pl.BlockSpec((tq, GW), lambda h, i, j: (i, h)),
                   pl.BlockSpec((T, HEAD_DIM), lambda h, i, j: (0, h)),
                   pl.BlockSpec((T, HEAD_DIM), lambda h, i, j: (0, h))],
        out_shape=[jax.ShapeDtypeStruct((T, N_Q_HEADS * HEAD_DIM), F32),
                   jax.ShapeDtypeStruct((T, N_KV_HEADS * HEAD_DIM), F32),
                   jax.ShapeDtypeStruct((T, N_KV_HEADS * HEAD_DIM), F32)],
        scratch_shapes=[pltpu.VMEM((GROUP * tq, HEAD_DIM), BF16), pltpu.VMEM((GROUP * tq, HEAD_DIM), BF16),
                        pltpu.VMEM((HEAD_DIM, GROUP * tq), F32)],
        compiler_params=_params(("arbitrary", "arbitrary", "arbitrary")),
    )(q, do, k, v, lse, delta)


def _gate_specs(D):
    w = D // 2
    first = (3 * D + (N_Q_HEADS + 2 * N_KV_HEADS) * HEAD_DIM) // w
    return [pl.BlockSpec((ROW, w), lambda i, c=first + j: (i, c)) for j in range(4)]


def _merge_fwd(a1, a2, P, D, name):
    T = a1.shape[0]
    w = D // 2

    def body(a1_ref, a2_ref, g0, g1, g2, g3, z_ref):
        for j, (gc, ga) in enumerate(((g0, g2), (g1, g3))):
            sl = slice(j * w, (j + 1) * w)
            z = jax.nn.sigmoid(gc[...]) * a1_ref[:, sl] + jax.nn.sigmoid(ga[...]) * a2_ref[:, sl]
            z_ref[:, sl] = z.astype(BF16)

    return pl.pallas_call(
        body, name=name, grid=(T // ROW,),
        in_specs=[_row_spec(D), _row_spec(D)] + _gate_specs(D),
        out_specs=_row_spec(D), out_shape=jax.ShapeDtypeStruct((T, D), BF16),
        compiler_params=_params(("parallel",)),
    )(a1, a2, P, P, P, P)


def _merge_bwd(dz, a1, a2, P, D, name):
    T = a1.shape[0]
    w = D // 2

    def body(dz_ref, a1_ref, a2_ref, g0, g1, g2, g3, d1_ref, d2_ref, dg_ref):
        for j, (gc, ga) in enumerate(((g0, g2), (g1, g3))):
            sl = slice(j * w, (j + 1) * w)
            dz = dz_ref[:, sl]
            sc = jax.nn.sigmoid(gc[...])
            sa = jax.nn.sigmoid(ga[...])
            d1_ref[:, sl] = (dz * sc).astype(BF16)
            d2_ref[:, sl] = (dz * sa).astype(BF16)
            dg_ref[:, j * w:(j + 1) * w] = (dz * a1_ref[:, sl] * (sc * (1.0 - sc))).astype(BF16)
            dg_ref[:, D + j * w:D + (j + 1) * w] = (dz * a2_ref[:, sl] * (sa * (1.0 - sa))).astype(BF16)

    return pl.pallas_call(
        body, name=name, grid=(T // ROW,),
        in_specs=[_row_spec(D), _row_spec(D), _row_spec(D)] + _gate_specs(D),
        out_specs=[_row_spec(D), _row_spec(D), _row_spec(2 * D)],
        out_shape=[jax.ShapeDtypeStruct((T, D), BF16), jax.ShapeDtypeStruct((T, D), BF16),
                   jax.ShapeDtypeStruct((T, 2 * D), BF16)],
        compiler_params=_params(("parallel",)),
    )(dz, a1, a2, P, P, P, P)


def _adamw_math(w, g, m, v):
    m = ADAM_B1 * m + (1.0 - ADAM_B1) * g
    v = ADAM_B2 * v + (1.0 - ADAM_B2) * (g * g)
    m_hat = m / (1.0 - ADAM_B1 ** ADAM_STEP)
    v_hat = v / (1.0 - ADAM_B2 ** ADAM_STEP)
    delta = -ADAM_LR * (m_hat / (jnp.sqrt(v_hat) + ADAM_EPS) + ADAM_WD * w)
    return delta, m, v


def _adamw(w, g, m, v, name):
    R, C = w.shape
    tr = _pick(R, tuple(t for t in (256, 128, 64, 32, 16, 8) if t * C * 4 <= ADAMW_BLOCK_BYTES))

    def body(w_ref, g_ref, m_ref, v_ref, d_ref, mo_ref, vo_ref):
        d, mn, vn = _adamw_math(w_ref[...], g_ref[...], m_ref[...], v_ref[...])
        d_ref[...] = d
        mo_ref[...] = mn
        vo_ref[...] = vn

    spec = pl.BlockSpec((tr, C), lambda i: (i, 0))
    return pl.pallas_call(
        body, name=name, grid=(R // tr,),
        in_specs=[spec] * 4, out_specs=[spec] * 3,
        out_shape=[jax.ShapeDtypeStruct((R, C), F32)] * 3,
        compiler_params=_params(("parallel",)),
    )(w, g, m, v)


def _concat_k_matmul(parts, b, name):
    T = parts[0].shape[0]
    N = b.shape[1]
    tm = _pick(T, (768, 512, 256))
    offs = [0]
    for p in parts:
        offs.append(offs[-1] + p.shape[1])
    assert offs[-1] == b.shape[0]

    def body(*refs):
        b_ref, o_ref = refs[len(parts)], refs[len(parts) + 1]
        acc = None
        for i, a_ref in enumerate(refs[:len(parts)]):
            d = jnp.dot(a_ref[...], b_ref[offs[i]:offs[i + 1], :], preferred_element_type=F32)
            acc = d if acc is None else acc + d
        o_ref[...] = acc

    return pl.pallas_call(
        body, name=name, grid=(T // tm,),
        in_specs=[pl.BlockSpec((tm, p.shape[1]), lambda i: (i, 0)) for p in parts] + [_resident()],
        out_specs=pl.BlockSpec((tm, N), lambda i: (i, 0)),
        out_shape=jax.ShapeDtypeStruct((T, N), F32),
        compiler_params=_params(("parallel",)),
    )(*parts, b)


def _adamw_transposed(w, gt, m, v, name):
    R, C = w.shape
    tc = 128

    def body(w_ref, g_ref, m_ref, v_ref, go_ref, d_ref, mo_ref, vo_ref):
        g = jnp.transpose(g_ref[...])
        d, mn, vn = _adamw_math(w_ref[...], g, m_ref[...], v_ref[...])
        go_ref[...] = g
        d_ref[...] = d
        mo_ref[...] = mn
        vo_ref[...] = vn

    spec = pl.BlockSpec((R, tc), lambda j: (0, j))
    return pl.pallas_call(
        body, name=name, grid=(C // tc,),
        in_specs=[spec, pl.BlockSpec((tc, R), lambda j: (j, 0)), spec, spec], out_specs=[spec] * 4,
        out_shape=[jax.ShapeDtypeStruct((R, C), F32)] * 4,
        compiler_params=_params(("parallel",)),
    )(w, gt, m, v)


class _NoExchange:
    def __init__(self, rest):
        self.rest = rest

    def rest_weights(self, after):
        return self.rest

    def reduce_early(self, grads, tag):
        return None


def _local_step(xcat, target, mods, norm_g, final_g, gq, gk, conv_w, ffn1_w, hooks, ctx_len):
    T, D = xcat.shape
    w1i, w1o = ffn1_w
    g1, g2, g3 = norm_g
    cos_t, sin_t = _rope_tables(ctx_len, T - ctx_len)

    def after(value, token):
        return value if token is None else _after(value, token)

    _, h1 = _resid_rmsmod_fwd(xcat, None, mods, g1, None, 0, 1, "f_norm1")
    u1, s1, f1 = _ffn_fwd(h1, w1i, w1o, "f_ffn1")
    wi, wbc, wba, wo, w2i, w2o = hooks.rest_weights(f1)
    x1, h2 = _resid_rmsmod_fwd(xcat, f1, mods, g2, (2, 0.5), 3, 4, "f_norm2")
    P = _matmul(h2, wi, "nt", F32, "f_mix_in")
    yc = _conv_fwd(P, conv_w, D, "f_conv")
    qn, kn, vb = _qk_fwd(P, gq, gk, cos_t, sin_t, D, "f_qk")
    o, lse = _flash_fwd(qn, kn, vb, "f_attn")
    a1 = _matmul(yc, wbc, "nn", F32, "f_branch_conv")
    a2 = _matmul(o, wba, "nn", F32, "f_branch_attn")
    z = _merge_fwd(a1, a2, P, D, "f_merge")
    mo = _matmul(z, wo, "nn", F32, "f_mix_out")
    x2, h3 = _resid_rmsmod_fwd(x1, mo, mods, g3, (5, 1.0), 6, 7, "f_norm3")
    u2, s2, f2 = _ffn_fwd(h3, w2i, w2o, "f_ffn2")
    dx3, df2, acc_head = _loss_head(x2, f2, mods, final_g, target, "loss_head")

    du2, dh3 = _ffn_bwd(df2, u2, w2i, w2o, "b_ffn2")
    g_w2o = _matmul(s2, df2, "tn", BF16, "b_ffn2_out_dw")
    g_w2i = _matmul(du2, h3, "tn", BF16, "b_ffn2_in_dw")
    dx2, dmo, acc_n3 = _rmsmod_bwd(x2, dh3, dx3, mods, g3, 6, 7, (5, 1.0), mo, "b_norm3")

    dz = _matmul(dmo, wo, "nt", F32, "b_mix_out_dx")
    g_wo = _matmul(z, dmo, "tn", BF16, "b_mix_out_dw")
    da1, da2, dgt = _merge_bwd(dz, a1, a2, P, D, "b_merge")
    dyc = _matmul(da1, wbc, "nt", F32, "b_branch_conv_dx")
    do = _matmul(da2, wba, "nt", F32, "b_branch_attn_dx")
    g_wbc = _matmul(yc, da1, "tn", BF16, "b_branch_conv_dw")
    g_wba = _matmul(o, da2, "tn", BF16, "b_branch_attn_dw")
    do = after(do, hooks.reduce_early([g_wbc, g_wba, g_wo, g_w2i, g_w2o], "a"))
    dob, delta = _attn_delta(do, o, "b_attn_delta")
    dq, dk, dv = _flash_bwd(qn, kn, vb, dob, lse, delta, "b_attn")
    dqkv, acc_qk = _qk_bwd(P, dq, dk, dv, gq, gk, cos_t, sin_t, D, "b_qk")
    dconv, acc_conv = _conv_bwd(P, dyc, conv_w, D, "b_conv")
    d_parts = (dconv, dqkv, dgt)
    dh2 = _concat_k_matmul(d_parts, wi, "b_mix_in_dx")
    g_wi = jnp.concatenate([_matmul(dp, h2, "tn", BF16, f"b_mix_in_dw_{i}") for i, dp in enumerate(d_parts)], axis=0)
    dh2 = after(dh2, hooks.reduce_early([g_wi], "b"))
    dx1, df1, acc_n2 = _rmsmod_bwd(x1, dh2, dx2, mods, g2, 3, 4, (2, 0.5), f1, "b_norm2")

    du1, dh1 = _ffn_bwd(df1, u1, w1i, w1o, "b_ffn1")
    g_w1o = _matmul(s1, df1, "tn", BF16, "b_ffn1_out_dw")
    g_w1i = _matmul(du1, h1, "tn", BF16, "b_ffn1_in_dw")
    grad_x, _, acc_n1 = _rmsmod_bwd(xcat, dh1, dx1, mods, g1, 0, 1, None, None, "b_norm1", skip_first_tile=True)

    grads = (g_w1i, g_w1o, g_wi, g_wbc, g_wba, g_wo, g_w2i, g_w2o)
    accs = (acc_head, acc_n3, acc_n2, acc_n1, acc_conv, acc_qk)
    return grad_x, grads, accs


def _place():
    return lax.axis_index("x"), lax.axis_index("y"), lax.axis_index("c")


def _other_chips(x, y):
    return [(1 - x, y), (x, 1 - y), (1 - x, 1 - y)]


def _allgather8(v, name):
    R, N = v.shape

    def body(v_ref, out_ref, send_sems, recv_sems, local_sem):
        x, y, c = _place()
        me, sibling = (x, y, c), (x, y, 1 - c)
        chips = _other_chips(x, y)

        def blk(px, py, pc):
            return out_ref.at[4 * px + 2 * py + pc]

        def copy(k, block, to, src=None):
            return pltpu.make_async_remote_copy(
                src_ref=blk(*block) if src is None else src, dst_ref=blk(*block),
                send_sem=send_sems.at[k], recv_sem=recv_sems.at[k], device_id=to, device_id_type=MESH)

        mine = pltpu.make_async_copy(v_ref, blk(*me), local_sem)
        mine.start()
        first = [copy(0, me, sibling, src=v_ref)]
        first += [copy(1 + j, me, (*chip, c), src=v_ref) for j, chip in enumerate(chips)]
        for cp in first:
            cp.start()
        passed = [copy(4 + j, (*chip, c), sibling) for j, chip in enumerate(chips)]
        for j, chip in enumerate(chips):
            copy(1 + j, (*chip, c), me).wait_recv()
            passed[j].start()
        copy(0, sibling, me).wait_recv()
        for j, chip in enumerate(chips):
            copy(4 + j, (*chip, 1 - c), me).wait_recv()
        for cp in first + passed:
            cp.wait_send()
        mine.wait()

    return pl.pallas_call(
        body, name=name,
        out_shape=jax.ShapeDtypeStruct((N_DEV, R, N), v.dtype),
        in_specs=[pl.BlockSpec(memory_space=pltpu.VMEM)],
        out_specs=pl.BlockSpec(memory_space=pltpu.VMEM),
        scratch_shapes=[pltpu.SemaphoreType.DMA((7,)), pltpu.SemaphoreType.DMA((7,)), pltpu.SemaphoreType.DMA],
        compiler_params=pltpu.CompilerParams(vmem_limit_bytes=VMEM_LIMIT),
    )(v)


def _any_specs(n):
    return [pl.BlockSpec(memory_space=pl.ANY)] * n


def _weights_allgather(fulls, name):
    n = len(fulls)

    def body(*refs):
        full = refs[n:2 * n]
        send_sems, recv_sems = refs[2 * n:]
        x, y, c = _place()
        sibling = (x, y, 1 - c)
        chips = _other_chips(x, y)

        def piece(t, px, py, h):
            rs = fulls[t].shape[0] // N_CHIPS
            return full[t].at[pl.ds((2 * px + py) * rs + h * (rs // 2), rs // 2), :]

        def copy(k, t, block, to):
            return pltpu.make_async_remote_copy(
                src_ref=piece(t, *block), dst_ref=piece(t, *block),
                send_sem=send_sems.at[k], recv_sem=recv_sems.at[k], device_id=to, device_id_type=MESH)

        first = []
        for t in range(n):
            for j, chip in enumerate(chips):
                cp = copy(3 * t + j, t, (x, y, c), (*chip, c))
                cp.start()
                first.append(cp)
        passed = []
        for t in range(n):
            for j, chip in enumerate(chips):
                copy(3 * t + j, t, (*chip, c), (x, y, c)).wait_recv()
                cp = copy(3 * n + 3 * t + j, t, (*chip, c), sibling)
                cp.start()
                passed.append(cp)
        for t in range(n):
            for j, chip in enumerate(chips):
                copy(3 * n + 3 * t + j, t, (*chip, 1 - c), (x, y, c)).wait_recv()
        for cp in first + passed:
            cp.wait_send()

    return pl.pallas_call(
        body, name=name,
        out_shape=[jax.ShapeDtypeStruct(f.shape, f.dtype) for f in fulls],
        in_specs=_any_specs(n), out_specs=_any_specs(n),
        input_output_aliases={t: t for t in range(n)},
        scratch_shapes=[pltpu.SemaphoreType.DMA((6 * n,)), pltpu.SemaphoreType.DMA((6 * n,))],
    )(*fulls)


def _pair_exchange(grads, name):
    n = len(grads)

    def body(*refs):
        g, land = refs[:n], refs[n:2 * n]
        send_sems, recv_sems = refs[2 * n:]
        x, y, c = _place()
        sibling = (x, y, 1 - c)
        copies = []
        for t in range(n):
            half = grads[t].shape[0] // (2 * N_CHIPS)
            for s in range(N_CHIPS):
                cp = pltpu.make_async_remote_copy(
                    src_ref=g[t].at[pl.ds((2 * s + 1 - c) * half, half), :], dst_ref=land[t].at[s],
                    send_sem=send_sems.at[N_CHIPS * t + s], recv_sem=recv_sems.at[N_CHIPS * t + s],
                    device_id=sibling, device_id_type=MESH)
                cp.start()
                copies.append(cp)
        for cp in copies:
            cp.wait_recv()
        for cp in copies:
            cp.wait_send()

    return pl.pallas_call(
        body, name=name,
        out_shape=[jax.ShapeDtypeStruct((N_CHIPS, a.shape[0] // (2 * N_CHIPS), a.shape[1]), a.dtype) for a in grads],
        in_specs=_any_specs(n), out_specs=_any_specs(n),
        scratch_shapes=[pltpu.SemaphoreType.DMA((N_CHIPS * n,)), pltpu.SemaphoreType.DMA((N_CHIPS * n,))],
    )(*grads)


def _place_shard(w2, idx, transpose, name):
    if transpose:
        D, rs = w2.shape
        tr = 128
        in_spec = pl.BlockSpec((D, tr), lambda i, idx: (0, i))
    else:
        rs, D = w2.shape
        tr = _pick(rs, (352, 256, 128, 64, 32, 16))
        in_spec = pl.BlockSpec((tr, D), lambda i, idx: (i, 0))
    steps = rs // tr

    def body(idx_ref, w_ref, o_ref):
        v = w_ref[...]
        o_ref[...] = (jnp.transpose(v) if transpose else v).astype(BF16)

    return pl.pallas_call(
        body, name=name,
        grid_spec=pltpu.PrefetchScalarGridSpec(
            num_scalar_prefetch=1, grid=(steps,), in_specs=[in_spec],
            out_specs=pl.BlockSpec((tr, D), lambda i, idx: (idx[1] * steps + i, 0))),
        out_shape=jax.ShapeDtypeStruct((N_CHIPS * rs, D), BF16),
        compiler_params=_params(("arbitrary",)),
    )(idx, w2)


def _pair_sum(g, landed, idx, name):
    _, half, D = landed.shape
    g4 = g.reshape(N_CHIPS, 2, half, D)
    tr = _pick(half, (416, 352, 128))

    def body(idx_ref, g_ref, l_ref, o_ref):
        o_ref[...] = (g_ref[0].astype(F32) + l_ref[...].astype(F32)).astype(BF16)

    return pl.pallas_call(
        body, name=name,
        grid_spec=pltpu.PrefetchScalarGridSpec(
            num_scalar_prefetch=1, grid=(N_CHIPS, half // tr),
            in_specs=[pl.BlockSpec((1, 1, tr, D), lambda s, i, idx: (idx[1 + s], idx[0], i, 0)),
                      pl.BlockSpec((1, tr, D), lambda s, i, idx: (idx[1 + s], i, 0))],
            out_specs=pl.BlockSpec((1, tr, D), lambda s, i, idx: (s, i, 0))),
        out_shape=jax.ShapeDtypeStruct((N_CHIPS, half, D), BF16),
        compiler_params=_params(("arbitrary", "arbitrary")),
    )(idx, g4, landed)


def _chip_exchange(sums, name):
    n = len(sums)

    def body(*refs):
        ps, land = refs[:n], refs[n:2 * n]
        send_sems, recv_sems = refs[2 * n:]
        x, y, c = _place()
        copies = []
        for t in range(n):
            for j, chip in enumerate(_other_chips(x, y)):
                cp = pltpu.make_async_remote_copy(
                    src_ref=ps[t].at[1 + j], dst_ref=land[t].at[j],
                    send_sem=send_sems.at[3 * t + j], recv_sem=recv_sems.at[3 * t + j],
                    device_id=(*chip, c), device_id_type=MESH)
                cp.start()
                copies.append(cp)
        for cp in copies:
            cp.wait_recv()
        for cp in copies:
            cp.wait_send()

    return pl.pallas_call(
        body, name=name,
        out_shape=[jax.ShapeDtypeStruct((3,) + a.shape[1:], a.dtype) for a in sums],
        in_specs=_any_specs(n), out_specs=_any_specs(n),
        scratch_shapes=[pltpu.SemaphoreType.DMA((3 * n,)), pltpu.SemaphoreType.DMA((3 * n,))],
    )(*sums)


_HBM = pl.BlockSpec(memory_space=pltpu.HBM)
_SEM = pl.BlockSpec(memory_space=pltpu.SEMAPHORE)
_EFFECT = pltpu.SideEffectType.DATAFLOW_SIDE_EFFECTING


def _in_hbm(a):
    return pltpu.with_memory_space_constraint(a, pltpu.HBM)


def _split_copies(n, per, make):
    def start(nbuf, name, bufs):
        def body(*refs):
            ins = refs[:nbuf]
            send_sems, recv_sems = refs[nbuf], refs[nbuf + 1]
            token = refs[-1]
            for t in range(n):
                for j in range(per):
                    make(ins, t, j, send_sems.at[per * t + j], recv_sems.at[per * t + j]).start()
            token[...] = jnp.zeros(token.shape, token.dtype)

        out = pl.pallas_call(
            body, name=name,
            out_shape=(pltpu.SemaphoreType.DMA((per * n,)), pltpu.SemaphoreType.DMA((per * n,)),
                       *[pltpu.HBM(b.shape, b.dtype) for b in bufs], jax.ShapeDtypeStruct((8, 128), F32)),
            in_specs=[_HBM] * nbuf,
            out_specs=(_SEM, _SEM, *[_HBM] * nbuf, pl.BlockSpec(memory_space=pltpu.VMEM)),
            input_output_aliases={i: 2 + i for i in range(nbuf)},
            compiler_params=pltpu.CompilerParams(has_side_effects=_EFFECT),
        )(*[_in_hbm(b) for b in bufs])
        return out[0], out[1], list(out[2:2 + nbuf]), out[-1]

    def wait(nbuf, name, send_sems, recv_sems, bufs, after):
        def body(*refs):
            ins = refs[:nbuf]
            ss, rs = refs[nbuf], refs[nbuf + 1]
            for t in range(n):
                for j in range(per):
                    cp = make(ins, t, j, ss.at[per * t + j], rs.at[per * t + j])
                    cp.wait_send()
                    cp.wait_recv()

        return pl.pallas_call(
            body, name=name,
            out_shape=[pltpu.HBM(b.shape, b.dtype) for b in bufs],
            in_specs=[_HBM] * nbuf + [_SEM, _SEM, pl.BlockSpec(memory_space=pl.ANY)],
            out_specs=[_HBM] * nbuf,
            input_output_aliases={i: i for i in range(nbuf)},
            compiler_params=pltpu.CompilerParams(has_side_effects=_EFFECT),
        )(*bufs, send_sems, recv_sems, after)

    return start, wait


def _chip_exchange_split(n):
    def make(bufs, t, j, send_sem, recv_sem):
        x, y, c = _place()
        chip = _other_chips(x, y)[j]
        return pltpu.make_async_remote_copy(src_ref=bufs[t].at[1 + j], dst_ref=bufs[n + t].at[j], send_sem=send_sem,
                                            recv_sem=recv_sem, device_id=(*chip, c), device_id_type=MESH)

    return _split_copies(n, 3, make)


def _weights_gather_split(fulls):
    def make(bufs, t, j, send_sem, recv_sem):
        x, y, c = _place()
        chip = _other_chips(x, y)[j]
        rs = fulls[t].shape[0] // N_CHIPS
        rows = bufs[t].at[pl.ds((2 * x + y) * rs + c * (rs // 2), rs // 2), :]
        return pltpu.make_async_remote_copy(src_ref=rows, dst_ref=rows, send_sem=send_sem, recv_sem=recv_sem,
                                            device_id=(*chip, c), device_id_type=MESH)

    return _split_copies(len(fulls), 3, make)


def _weights_pass_on(fulls, name):
    n = len(fulls)

    def body(*refs):
        full = refs[n:2 * n]
        send_sems, recv_sems = refs[2 * n:]
        x, y, c = _place()
        chips = _other_chips(x, y)

        def copy(t, j, h):
            rs = fulls[t].shape[0] // N_CHIPS
            px, py = chips[j]
            rows = full[t].at[pl.ds((2 * px + py) * rs + h * (rs // 2), rs // 2), :]
            return pltpu.make_async_remote_copy(src_ref=rows, dst_ref=rows, send_sem=send_sems.at[3 * t + j],
                                                recv_sem=recv_sems.at[3 * t + j], device_id=(x, y, 1 - c),
                                                device_id_type=MESH)

        for t in range(n):
            for j in range(3):
                copy(t, j, c).start()
        for t in range(n):
            for j in range(3):
                copy(t, j, 1 - c).wait_recv()
        for t in range(n):
            for j in range(3):
                copy(t, j, c).wait_send()

    return pl.pallas_call(
        body, name=name,
        out_shape=[jax.ShapeDtypeStruct(f.shape, f.dtype) for f in fulls],
        in_specs=_any_specs(n), out_specs=_any_specs(n),
        input_output_aliases={t: t for t in range(n)},
        scratch_shapes=[pltpu.SemaphoreType.DMA((3 * n,)), pltpu.SemaphoreType.DMA((3 * n,))],
    )(*fulls)


def _after(value, token):
    return lax.optimization_barrier((value, token))[0]


def _chip_sum(ps, landed, idx, name):
    _, half, D = ps.shape
    tr = _pick(half, (416, 352, 128))
    steps = half // tr

    def body(idx_ref, p_ref, l_ref, o_ref):
        acc = p_ref[0].astype(F32)
        for j in range(3):
            acc = acc + l_ref[j].astype(F32)
        o_ref[...] = acc

    return pl.pallas_call(
        body, name=name,
        grid_spec=pltpu.PrefetchScalarGridSpec(
            num_scalar_prefetch=1, grid=(steps,),
            in_specs=[pl.BlockSpec((1, tr, D), lambda i, idx: (0, i, 0)),
                      pl.BlockSpec((3, tr, D), lambda i, idx: (0, i, 0))],
            out_specs=pl.BlockSpec((tr, D), lambda i, idx: (idx[0] * steps + i, 0))),
        out_shape=jax.ShapeDtypeStruct((2 * half, D), F32),
        compiler_params=_params(("arbitrary",)),
    )(idx, ps, landed)


def _pair_swap(shards, name):
    n = len(shards)

    def body(*refs):
        full = refs[n:2 * n]
        send_sems, recv_sems = refs[2 * n:]
        x, y, c = _place()

        def half(t, h):
            rows = shards[t].shape[0] // 2
            return full[t].at[pl.ds(h * rows, rows), :]

        def copy(t, h):
            return pltpu.make_async_remote_copy(src_ref=half(t, h), dst_ref=half(t, h), send_sem=send_sems.at[t],
                                                recv_sem=recv_sems.at[t], device_id=(x, y, 1 - c),
                                                device_id_type=MESH)

        for t in range(n):
            copy(t, c).start()
        for t in range(n):
            copy(t, 1 - c).wait_recv()
        for t in range(n):
            copy(t, c).wait_send()

    return pl.pallas_call(
        body, name=name,
        out_shape=[jax.ShapeDtypeStruct(a.shape, a.dtype) for a in shards],
        in_specs=_any_specs(n), out_specs=_any_specs(n),
        input_output_aliases={t: t for t in range(n)},
        scratch_shapes=[pltpu.SemaphoreType.DMA((n,)), pltpu.SemaphoreType.DMA((n,))],
    )(*shards)


class _Exchanges:
    def __init__(self, fulls_rest, idx):
        self.idx = idx
        start, self._gather_wait = _weights_gather_split(fulls_rest)
        self._gather = start(len(fulls_rest), "ag_rest_start", fulls_rest)
        self.token = self._gather[3]
        self._early = []

    def rest_weights(self, after):
        send_sems, recv_sems, bufs, _ = self._gather
        landed = self._gather_wait(len(bufs), "ag_rest_wait", send_sems, recv_sems, bufs, after)
        return _weights_pass_on(landed, "ag_rest_pass_on")

    def _pair_sums(self, grads, tag):
        landed = _pair_exchange(grads, "rs_pair_exchange_" + tag)
        return [_pair_sum(g, l, self.idx, f"rs_pair_sum_{tag}{t}") for t, (g, l) in enumerate(zip(grads, landed))]

    def reduce_early(self, grads, tag):
        sums = self._pair_sums(grads, tag)
        zones = [lax.empty((3,) + s.shape[1:], s.dtype) for s in sums]
        start, wait = _chip_exchange_split(len(sums))
        send_sems, recv_sems, bufs, token = start(2 * len(sums), "rs_chip_start_" + tag, sums + zones)
        self._early.append((tag, wait, send_sems, recv_sems, bufs))
        return token

    def finish_early(self, after):
        halves = []
        for tag, wait, send_sems, recv_sems, bufs in self._early:
            n = len(bufs) // 2
            done = wait(len(bufs), "rs_chip_wait_" + tag, send_sems, recv_sems, bufs, after)
            halves += [_chip_sum(p, l, self.idx, f"rs_chip_sum_{tag}{t}")
                       for t, (p, l) in enumerate(zip(done[:n], done[n:]))]
        return halves

    def reduce_late(self, grads, tag):
        sums = self._pair_sums(grads, tag)
        landed = _chip_exchange(sums, "rs_chip_exchange_" + tag)
        return [_chip_sum(p, l, self.idx, f"rs_chip_sum_{tag}{t}") for t, (p, l) in enumerate(zip(sums, landed))]


N_MOD = 9
PACK_HEAD, PACK_N3, PACK_N2, PACK_N1, PACK_CONV, PACK_QK = 0, 16, 32, 48, 64, 80
PACK_ROWS = 96
MOD_SRC = ((PACK_N1, 0), (PACK_N1, 1), (PACK_N2, 3), (PACK_N2, 0), (PACK_N2, 1),
           (PACK_N3, 3), (PACK_N3, 0), (PACK_N3, 1), (PACK_HEAD, 2))
CTX_ROW = 8


def _silu(v):
    return v * jax.nn.sigmoid(v)


def _whole(n):
    return [pl.BlockSpec(memory_space=pltpu.VMEM)] * n


def _mod_rows(cin, w_sh, b_sh, name):
    def body(c_ref, w_ref, b_ref, o_ref):
        a = _silu(c_ref[...]).astype(BF16)
        o_ref[...] = jnp.dot(a, w_ref[...].astype(BF16), preferred_element_type=F32) + b_ref[...]

    return pl.pallas_call(
        body, name=name, out_shape=jax.ShapeDtypeStruct((cin.shape[0], w_sh.shape[1]), F32),
        in_specs=_whole(3), out_specs=pl.BlockSpec(memory_space=pltpu.VMEM),
        compiler_params=pltpu.CompilerParams(vmem_limit_bytes=VMEM_LIMIT),
    )(cin, w_sh, b_sh)


def _small_reduce(gathered, name):
    _, _, D = gathered.shape

    def body(g_ref, loss_ref, db_ref, gn_ref, cv_ref, qk_ref, dm_ref):
        tot = g_ref[0]
        for r in range(1, N_DEV):
            tot = tot + g_ref[r]

        def both(block, row):
            return tot[block + row:block + row + 1, :] + tot[block + 8 + row:block + 8 + row + 1, :]

        loss = jnp.sum(both(PACK_HEAD, 0), axis=1, keepdims=True)
        loss_ref[...] = jnp.broadcast_to(loss, loss_ref.shape)
        db_ref[...] = jnp.zeros(db_ref.shape, F32)
        dm_ref[...] = jnp.zeros(dm_ref.shape, F32)
        for j, (block, row) in enumerate(MOD_SRC):
            db_ref[j:j + 1, :] = both(block, row)
            dm_ref[CTX_ROW, j:j + 1, :] = tot[block + row:block + row + 1, :]
            for r in range(N_DEV):
                dm_ref[r, j:j + 1, :] = g_ref[r, block + 8 + row:block + 8 + row + 1, :]
        gn_ref[...] = jnp.zeros(gn_ref.shape, F32)
        gn_ref[0:1, :] = both(PACK_N1, 2)
        gn_ref[8:9, :] = both(PACK_N2, 2)
        gn_ref[16:17, :] = both(PACK_N3, 2)
        gn_ref[24:25, :] = both(PACK_HEAD, 1)
        cv_ref[...] = jnp.zeros(cv_ref.shape, F32)
        for r in range(3):
            cv_ref[r:r + 1, :] = both(PACK_CONV, r)
        qk_ref[...] = jnp.zeros(qk_ref.shape, F32)
        qk_ref[0:1, 0:HEAD_DIM] = both(PACK_QK, 0)[:, 0:HEAD_DIM]
        qk_ref[0:1, HEAD_DIM:2 * HEAD_DIM] = both(PACK_QK, 1)[:, 0:HEAD_DIM]

    return pl.pallas_call(
        body, name=name,
        out_shape=[jax.ShapeDtypeStruct((8, 128), F32), jax.ShapeDtypeStruct((16, D), F32),
                   jax.ShapeDtypeStruct((32, D), F32), jax.ShapeDtypeStruct((8, D), F32),
                   jax.ShapeDtypeStruct((8, D), F32), jax.ShapeDtypeStruct((16, 16, D), F32)],
        in_specs=_whole(1), out_specs=_whole(6),
        compiler_params=pltpu.CompilerParams(vmem_limit_bytes=VMEM_LIMIT),
    )(gathered)


def _wmod_grad(cin, dm_sh, w_sh, name):
    def body(c_ref, d_ref, w_ref, gw_ref, cp_ref):
        a = _silu(c_ref[...]).astype(BF16)
        d = d_ref[...].astype(BF16)
        gw_ref[...] = lax.dot_general(a, d, (((0,), (0,)), ((), ())), preferred_element_type=F32)
        cp_ref[...] = lax.dot_general(d, w_ref[...].astype(BF16), (((1,), (1,)), ((), ())),
                                      preferred_element_type=F32)

    return pl.pallas_call(
        body, name=name,
        out_shape=[jax.ShapeDtypeStruct(w_sh.shape, F32), jax.ShapeDtypeStruct(cin.shape, F32)],
        in_specs=_whole(3), out_specs=_whole(2),
        compiler_params=pltpu.CompilerParams(vmem_limit_bytes=VMEM_LIMIT),
    )(cin, dm_sh, w_sh)


def _cctx_grad(parts, c_ctx8, name):
    def body(p_ref, c_ref, o_ref):
        tot = p_ref[0] + p_ref[2] + p_ref[4] + p_ref[6]
        cv = c_ref[...]
        sig = jax.nn.sigmoid(cv)
        rows = lax.broadcasted_iota(jnp.int32, tot.shape, 0)
        o_ref[...] = jnp.where(rows == 0, tot * (sig * (1.0 + cv * (1.0 - sig))), 0.0)

    return pl.pallas_call(
        body, name=name, out_shape=jax.ShapeDtypeStruct(c_ctx8.shape, F32),
        in_specs=_whole(2), out_specs=pl.BlockSpec(memory_space=pltpu.VMEM),
    )(parts, c_ctx8)


def _pad_rows(a, rows):
    return jnp.pad(a, ((0, rows - a.shape[0]), (0, 0)))


def _pack_small(c_ctx, b_mod, n1, n2, n3, final_g, gq, gk, conv_sh, D):
    misc = jnp.concatenate([gq, gk, conv_sh.reshape(1, -1)], axis=1)
    return jnp.concatenate([_pad_rows(c_ctx[None], 8), _pad_rows(b_mod.reshape(N_MOD, D), 16), _pad_rows(n1, 8),
                            _pad_rows(n2, 8), _pad_rows(n3, 8), _pad_rows(final_g[None], 8), _pad_rows(misc, 8)], axis=0)


def _unpack_small(p, D, conv_shape):
    misc = p[56:57]
    return dict(c_ctx=p[0], b_mod=p[8:8 + N_MOD].reshape(1, N_MOD * D), norm1_g=p[24:25], norm2_g=p[32:33],
                norm3_g=p[40:41], final_g=p[48], q_norm_g=misc[:, 0:HEAD_DIM], k_norm_g=misc[:, HEAD_DIM:2 * HEAD_DIM],
                conv_w=misc[:, 2 * HEAD_DIM:].reshape(conv_shape))


WEIGHT_ORDER = ("c_ctx", "w_mod", "b_mod", "norm1_g", "norm2_g", "norm3_g", "ffn1_w_in", "ffn1_w_out", "w_in",
                "conv_w", "q_norm_g", "k_norm_g", "w_branch_conv", "w_branch_attn", "w_out", "ffn2_w_in",
                "ffn2_w_out", "final_g")
BIG = ("ffn1_w_in", "ffn1_w_out", "w_in", "w_branch_conv", "w_branch_attn", "w_out", "ffn2_w_in", "ffn2_w_out")
COLUMN_SHARDED = ("ffn1_w_in", "w_in", "ffn2_w_in")


def kernel(x, c, ctx, c_ctx, w_mod, b_mod, norm1_g, norm2_g, norm3_g, ffn1_w_in, ffn1_w_out, w_in, conv_w, q_norm_g, k_norm_g, w_branch_conv, w_branch_attn, w_out, ffn2_w_in, ffn2_w_out, final_g, loss_target, m_c_ctx, m_w_mod, m_b_mod, m_norm1_g, m_norm2_g, m_norm3_g, m_ffn1_w_in, m_ffn1_w_out, m_w_in, m_conv_w, m_q_norm_g, m_k_norm_g, m_w_branch_conv, m_w_branch_attn, m_w_out, m_ffn2_w_in, m_ffn2_w_out, m_final_g, v_c_ctx, v_w_mod, v_b_mod, v_norm1_g, v_norm2_g, v_norm3_g, v_ffn1_w_in, v_ffn1_w_out, v_w_in, v_conv_w, v_q_norm_g, v_k_norm_g, v_w_branch_conv, v_w_branch_attn, v_w_out, v_ffn2_w_in, v_ffn2_w_out, v_final_g):
    w = dict(c_ctx=c_ctx, w_mod=w_mod, b_mod=b_mod, norm1_g=norm1_g, norm2_g=norm2_g, norm3_g=norm3_g,
             ffn1_w_in=ffn1_w_in, ffn1_w_out=ffn1_w_out, w_in=w_in, conv_w=conv_w, q_norm_g=q_norm_g,
             k_norm_g=k_norm_g, w_branch_conv=w_branch_conv, w_branch_attn=w_branch_attn, w_out=w_out,
             ffn2_w_in=ffn2_w_in, ffn2_w_out=ffn2_w_out, final_g=final_g)
    m = dict(c_ctx=m_c_ctx, w_mod=m_w_mod, b_mod=m_b_mod, norm1_g=m_norm1_g, norm2_g=m_norm2_g, norm3_g=m_norm3_g,
             ffn1_w_in=m_ffn1_w_in, ffn1_w_out=m_ffn1_w_out, w_in=m_w_in, conv_w=m_conv_w, q_norm_g=m_q_norm_g,
             k_norm_g=m_k_norm_g, w_branch_conv=m_w_branch_conv, w_branch_attn=m_w_branch_attn, w_out=m_w_out,
             ffn2_w_in=m_ffn2_w_in, ffn2_w_out=m_ffn2_w_out, final_g=m_final_g)
    v = dict(c_ctx=v_c_ctx, w_mod=v_w_mod, b_mod=v_b_mod, norm1_g=v_norm1_g, norm2_g=v_norm2_g, norm3_g=v_norm3_g,
             ffn1_w_in=v_ffn1_w_in, ffn1_w_out=v_ffn1_w_out, w_in=v_w_in, conv_w=v_conv_w, q_norm_g=v_q_norm_g,
             k_norm_g=v_k_norm_g, w_branch_conv=v_w_branch_conv, w_branch_attn=v_w_branch_attn, w_out=v_w_out,
             ffn2_w_in=v_ffn2_w_in, ffn2_w_out=v_ffn2_w_out, final_g=v_final_g)

    xi, yi, ci = _place()
    dev = 4 * xi + 2 * yi + ci
    shard = 2 * xi + yi
    idx = jnp.stack([ci, shard, 2 * (1 - xi) + yi, 2 * xi + (1 - yi), 2 * (1 - xi) + (1 - yi)]).astype(jnp.int32)
    D = x.shape[-1]
    ctx_len = ctx.shape[1]
    assert ctx_len == ROW and c.shape == (1, D)
    mcols = w_mod.shape[2]
    ccols = conv_w.shape[2]

    c_all = _allgather8(jnp.broadcast_to(c, (8, D)), "ag_c")[:, 0, :]
    cin = jnp.concatenate([c_all, _pad_rows(c_ctx[None], 8)], axis=0)
    b_sh = lax.dynamic_slice(b_mod, (0, shard * mcols), (1, mcols))
    mod_sh = _mod_rows(cin, w_mod[0], b_sh, "mod_rows")
    conv_rows = jnp.pad(conv_w[0], ((0, 8 - conv_w.shape[1]), (0, mcols - ccols)))
    mod_all = _allgather8(jnp.concatenate([mod_sh, conv_rows], axis=0), "ag_mod")
    mod_full = jnp.concatenate([mod_all[2 * s, :16] for s in range(N_CHIPS)], axis=1)
    conv_full = jnp.concatenate([mod_all[2 * s, 16:16 + conv_w.shape[1], :ccols] for s in range(N_CHIPS)], axis=1)
    mod_lat = lax.dynamic_slice(mod_full, (dev, 0), (1, N_MOD * D)).reshape(N_MOD, D)
    mod_ctx = mod_full[CTX_ROW].reshape(N_MOD, D)
    mods = jnp.stack([_pad_rows(mod_ctx, 16), _pad_rows(mod_lat, 16)])

    fulls = [_place_shard(w[n][0], idx, n in COLUMN_SHARDED, "place_" + n) for n in BIG]
    ffn1_w = _weights_allgather(fulls[:2], "ag_weights_ffn1")
    hooks = _Exchanges(fulls[2:], idx)

    xcat = _after(jnp.concatenate([ctx[0], x[0]], axis=0), hooks.token)
    grad_x, grads, accs = _local_step(xcat, loss_target[0], mods, (norm1_g, norm2_g, norm3_g), final_g[None],
                                      q_norm_g, k_norm_g, conv_full, ffn1_w, hooks, ctx_len)

    g = {}
    late = hooks.reduce_late(list(grads[:2]), "c")
    h_wbc, h_wba, h_wo, h_w2i, h_w2o, h_wi = hooks.finish_early(grad_x)
    halves = late + [h_wi, h_wbc, h_wba, h_wo, h_w2i, h_w2o]
    reduced = dict(zip(BIG, _pair_swap(halves, "rs_pair_swap")))

    pack = jnp.concatenate([a.reshape(2 * ACC_ROWS, D) for a in accs], axis=0)
    gathered = _allgather8(pack, "ag_small")
    loss8, db_mod, g_norms, g_conv, g_qk, dm = _small_reduce(gathered, "small_reduce")
    dm_sh = lax.dynamic_slice(dm[:, :N_MOD, :].reshape(16, N_MOD * D), (0, shard * mcols), (16, mcols))
    g_wmod, cpart = _wmod_grad(cin, dm_sh, w_mod[0], "wmod_grad")
    g["w_mod"] = g_wmod[None]
    cparts = _allgather8(cpart[CTX_ROW:CTX_ROW + 8], "ag_cctx")
    g_cctx = _cctx_grad(cparts, _pad_rows(c_ctx[None], 8), "cctx_grad")
    g_conv_sh = lax.dynamic_slice(g_conv, (0, shard * ccols), (conv_w.shape[1], ccols))
    g_misc = jnp.concatenate([g_qk[0:1, 0:2 * HEAD_DIM], g_conv_sh.reshape(1, -1)], axis=1)
    g_pack = jnp.concatenate([g_cctx, db_mod, g_norms, _pad_rows(g_misc, 8)], axis=0)

    def packed(p):
        return _pack_small(p["c_ctx"], p["b_mod"], p["norm1_g"], p["norm2_g"], p["norm3_g"], p["final_g"],
                           p["q_norm_g"], p["k_norm_g"], p["conv_w"][0], D)

    d_pack, m_pack, v_pack = _adamw(packed(w), g_pack, packed(m), packed(v), "adamw_small")
    g.update(_unpack_small(g_pack, D, conv_w.shape))
    delta = _unpack_small(d_pack, D, conv_w.shape)
    new_m = _unpack_small(m_pack, D, conv_w.shape)
    new_v = _unpack_small(v_pack, D, conv_w.shape)
    for n in BIG + ("w_mod",):
        if n in COLUMN_SHARDED:
            g2, d2, m2, v2 = _adamw_transposed(w[n][0], reduced[n], m[n][0], v[n][0], "adamw_" + n)
        else:
            g2 = reduced[n] if n in reduced else g[n][0]
            d2, m2, v2 = _adamw(w[n][0], g2, m[n][0], v[n][0], "adamw_" + n)
        g[n], delta[n], new_m[n], new_v[n] = g2[None], d2[None], m2[None], v2[None]

    loss = loss8[0, 0]
    return (loss, grad_x[None], *[g[n] for n in WEIGHT_ORDER], *[delta[n] for n in WEIGHT_ORDER],
            *[new_m[n] for n in WEIGHT_ORDER], *[new_v[n] for n in WEIGHT_ORDER])
```

```python
import functools

import jax
import jax.numpy as jnp
from jax import lax
from jax.experimental import pallas as pl
from jax.experimental.pallas import tpu as pltpu

F32 = jnp.float32
BF16 = jnp.bfloat16

HEAD_DIM = 128
N_Q_HEADS = 8
N_KV_HEADS = 2
GROUP = N_Q_HEADS // N_KV_HEADS
GRID_W = 64
ROPE_THETA = 10000.0
EPS = 1e-6
ATTN_SCALE = HEAD_DIM ** -0.5

ADAM_LR = 0.001
ADAM_B1 = 0.9
ADAM_B2 = 0.999
ADAM_EPS = 1e-08
ADAM_WD = 0.01
ADAM_STEP = 10

ROW = 256
HALO = 16
ACC_ROWS = 8
N_CHIPS = 4
N_DEV = 8
MESH = pl.DeviceIdType.MESH
VMEM_LIMIT = 48 * 1024 * 1024
ADAMW_BLOCK_BYTES = 1024 * 1024


def _pick(n, prefs):
    for p in prefs:
        if n % p == 0:
            return p
    return n


def _params(sem):
    return pltpu.CompilerParams(dimension_semantics=sem, vmem_limit_bytes=VMEM_LIMIT)


def _stream(i):
    return jnp.minimum(i, 1)


def _matmul(a, b, mode, out_dtype, name, tm=None, tn=None, tk=None):
    if mode == "nn":
        (M, K), (K2, N) = a.shape, b.shape
    elif mode == "nt":
        (M, K), (N, K2) = a.shape, b.shape
    else:
        (K, M), (K2, N) = a.shape, b.shape
    assert K == K2, (a.shape, b.shape, mode)
    tm = tm or _pick(M, (1664, 1408, 1024, 512, 256, 128) if mode == "tn" else (1408, 768, 512, 256, 128))
    tn = tn or _pick(N, (1664, 1408, 1024, 512, 256, 128))
    tk = tk or _pick(K, (1664, 1408, 1024, 768, 512, 256, 128))
    nk = K // tk
    if mode == "tn":
        a_spec = pl.BlockSpec((tk, tm), lambda i, j, k: (k, i))
    else:
        a_spec = pl.BlockSpec((tm, tk), lambda i, j, k: (i, k))
    if mode == "nt":
        b_spec = pl.BlockSpec((tn, tk), lambda i, j, k: (j, k))
    else:
        b_spec = pl.BlockSpec((tk, tn), lambda i, j, k: (k, j))
    dims = {"nn": ((1,), (0,)), "nt": ((1,), (1,)), "tn": ((0,), (0,))}[mode]
    use_scratch = nk > 1 and out_dtype != F32

    def body(a_ref, b_ref, o_ref, *scratch):
        p = lax.dot_general(a_ref[...].astype(BF16), b_ref[...].astype(BF16), (dims, ((), ())),
                            preferred_element_type=F32)
        if nk == 1:
            o_ref[...] = p.astype(o_ref.dtype)
            return
        acc_ref = scratch[0] if use_scratch else o_ref
        k = pl.program_id(2)

        @pl.when(k == 0)
        def _():
            acc_ref[...] = p

        @pl.when(k > 0)
        def _():
            acc_ref[...] += p

        if use_scratch:
            @pl.when(k == nk - 1)
            def _():
                o_ref[...] = acc_ref[...].astype(o_ref.dtype)

    return pl.pallas_call(
        body, name=name,
        grid=(M // tm, N // tn, nk),
        in_specs=[a_spec, b_spec],
        out_specs=pl.BlockSpec((tm, tn), lambda i, j, k: (i, j)),
        out_shape=jax.ShapeDtypeStruct((M, N), out_dtype),
        scratch_shapes=[pltpu.VMEM((tm, tn), F32)] if use_scratch else [],
        compiler_params=_params(("parallel", "parallel", "arbitrary")),
    )(a, b)


def _row_spec(width, col=0):
    return pl.BlockSpec((ROW, width), lambda i, col=col: (i, col))


def _mods_spec(D):
    return pl.BlockSpec((1, 16, D), lambda i: (_stream(i), 0, 0))


def _acc_spec(D):
    return pl.BlockSpec((1, ACC_ROWS, D), lambda i: (_stream(i), 0, 0))


def _vec_spec(rows, D):
    return pl.BlockSpec((rows, D), lambda i: (0, 0))


def _acc_init(acc_ref):
    i = pl.program_id(0)

    @pl.when(i <= 1)
    def _():
        acc_ref[...] = jnp.zeros_like(acc_ref)


def _acc_add(acc_ref, row, val):
    acc_ref[0, row:row + 1, :] += jnp.sum(val, axis=0, keepdims=True)


def _resid_rmsmod_fwd(xprev, branch, mods, g, gate, shift_idx, scale_idx, name):
    T, D = xprev.shape
    has_res = branch is not None

    def body(*refs):
        if has_res:
            x_ref, f_ref, m_ref, g_ref, xo_ref, h_ref = refs
        else:
            x_ref, m_ref, g_ref, h_ref = refs
        m = m_ref[0]
        x = x_ref[...]
        if has_res:
            gate_idx, fac = gate
            x = x + (fac * m[gate_idx:gate_idx + 1, :]) * f_ref[...]
            xo_ref[...] = x
        inv = lax.rsqrt(jnp.mean(x * x, axis=-1, keepdims=True) + EPS)
        y = (x * inv) * g_ref[...]
        h = y * (1.0 + m[scale_idx:scale_idx + 1, :]) + m[shift_idx:shift_idx + 1, :]
        h_ref[...] = h.astype(BF16)

    in_specs = [_row_spec(D)] + ([_row_spec(D)] if has_res else []) + [_mods_spec(D), _vec_spec(1, D)]
    args = [xprev] + ([branch] if has_res else []) + [mods, g]
    out_specs = ([_row_spec(D)] if has_res else []) + [_row_spec(D)]
    out_shape = ([jax.ShapeDtypeStruct((T, D), F32)] if has_res else []) + [jax.ShapeDtypeStruct((T, D), BF16)]
    out = pl.pallas_call(
        body, name=name, grid=(T // ROW,), in_specs=in_specs, out_specs=out_specs, out_shape=out_shape,
        compiler_params=_params(("parallel",)),
    )(*args)
    return out if has_res else (None, out[0])


def _loss_head(x2, f2, mods, final_g, target, name):
    T, D = x2.shape
    nt = T // ROW

    def body(x_ref, f_ref, m_ref, g_ref, t_ref, dx_ref, df_ref, acc_ref):
        _acc_init(acc_ref)
        i = pl.program_id(0)
        m = m_ref[0]
        gate = 0.5 * m[8:9, :]
        f = f_ref[...]
        x = x_ref[...] + gate * f
        inv = lax.rsqrt(jnp.mean(x * x, axis=-1, keepdims=True) + EPS)
        xn = x * inv
        fg = g_ref[...]
        lat = (i > 0).astype(F32)
        e = (xn * fg - t_ref[...]) * lat
        dy = e * (1.0 / D)
        dxn = dy * fg
        dx = inv * (dxn - xn * jnp.mean(dxn * xn, axis=-1, keepdims=True))
        dx_ref[...] = dx
        df_ref[...] = (gate * dx).astype(BF16)
        _acc_add(acc_ref, 0, (0.5 / D) * e * e)
        _acc_add(acc_ref, 1, dy * xn)
        _acc_add(acc_ref, 2, 0.5 * dx * f)

    return pl.pallas_call(
        body, name=name, grid=(nt,),
        in_specs=[_row_spec(D), _row_spec(D), _mods_spec(D), _vec_spec(1, D),
                  pl.BlockSpec((ROW, D), lambda i: (jnp.maximum(i - 1, 0), 0))],
        out_specs=[_row_spec(D), _row_spec(D), _acc_spec(D)],
        out_shape=[jax.ShapeDtypeStruct((T, D), F32), jax.ShapeDtypeStruct((T, D), BF16),
                   jax.ShapeDtypeStruct((2, ACC_ROWS, D), F32)],
        compiler_params=_params(("arbitrary",)),
    )(x2, f2, mods, final_g, target)


def _rmsmod_bwd(x, dh, dres, mods, g, shift_idx, scale_idx, gate, branch, name, skip_first_tile=False):
    T, D = x.shape
    nt = T // ROW
    has_gate = gate is not None

    def body(*refs):
        if has_gate:
            x_ref, dh_ref, dr_ref, b_ref, m_ref, g_ref, dx_ref, db_ref, acc_ref = refs
        else:
            x_ref, dh_ref, dr_ref, m_ref, g_ref, dx_ref, acc_ref = refs
        _acc_init(acc_ref)
        m = m_ref[0]
        x = x_ref[...]
        dh = dh_ref[...]
        gg = g_ref[...]
        inv = lax.rsqrt(jnp.mean(x * x, axis=-1, keepdims=True) + EPS)
        xn = x * inv
        y = xn * gg
        dy = dh * (1.0 + m[scale_idx:scale_idx + 1, :])
        dxn = dy * gg
        dx = inv * (dxn - xn * jnp.mean(dxn * xn, axis=-1, keepdims=True)) + dr_ref[...]
        dx_ref[...] = dx
        _acc_add(acc_ref, 0, dh)
        _acc_add(acc_ref, 1, dh * y)
        _acc_add(acc_ref, 2, dy * xn)
        if has_gate:
            gate_idx, fac = gate
            b = b_ref[...]
            db_ref[...] = ((fac * m[gate_idx:gate_idx + 1, :]) * dx).astype(BF16)
            _acc_add(acc_ref, 3, fac * dx * b)

    in_specs = [_row_spec(D), _row_spec(D), _row_spec(D)] + ([_row_spec(D)] if has_gate else []) + \
               [_mods_spec(D), _vec_spec(1, D)]
    args = [x, dh, dres] + ([branch] if has_gate else []) + [mods, g]
    if skip_first_tile:
        dx_spec = pl.BlockSpec((ROW, D), lambda i: (jnp.maximum(i - 1, 0), 0))
        dx_shape = jax.ShapeDtypeStruct((T - ROW, D), F32)
    else:
        dx_spec = _row_spec(D)
        dx_shape = jax.ShapeDtypeStruct((T, D), F32)
    out_specs = [dx_spec] + ([_row_spec(D)] if has_gate else []) + [_acc_spec(D)]
    out_shape = [dx_shape] + ([jax.ShapeDtypeStruct((T, D), BF16)] if has_gate else []) + \
                [jax.ShapeDtypeStruct((2, ACC_ROWS, D), F32)]
    out = pl.pallas_call(
        body, name=name, grid=(nt,), in_specs=in_specs, out_specs=out_specs, out_shape=out_shape,
        compiler_params=_params(("arbitrary",)),
    )(*args)
    if has_gate:
        return out
    return out[0], None, out[1]


FFN_ROWS = 384
_NT = (((1,), (1,)), ((), ()))


def _ffn_chunk(F):
    return _pick(F, (1408, 512, 256, 128))


def _resident():
    return pl.BlockSpec(memory_space=pltpu.VMEM)


def _ffn_fwd(h, w_in_t, w_out, name):
    T, D = h.shape
    F = w_out.shape[0]
    cw = _ffn_chunk(F)
    tm = _pick(T, (FFN_ROWS, ROW))

    def body(h_ref, wi_ref, wo_ref, u_ref, s_ref, f_ref):
        hv = h_ref[...]
        acc = jnp.zeros((tm, D), F32)
        for j in range(F // cw):
            a = lax.dot_general(hv, wi_ref[j * cw:(j + 1) * cw, :], _NT, preferred_element_type=F32)
            b = lax.dot_general(hv, wi_ref[F + j * cw:F + (j + 1) * cw, :], _NT, preferred_element_type=F32)
            s = ((a * jax.nn.sigmoid(a)) * b).astype(BF16)
            u_ref[:, j * cw:(j + 1) * cw] = a.astype(BF16)
            u_ref[:, F + j * cw:F + (j + 1) * cw] = b.astype(BF16)
            s_ref[:, j * cw:(j + 1) * cw] = s
            acc = acc + jnp.dot(s, wo_ref[j * cw:(j + 1) * cw, :], preferred_element_type=F32)
        f_ref[...] = acc

    row = lambda w: pl.BlockSpec((tm, w), lambda i: (i, 0))
    return pl.pallas_call(
        body, name=name, grid=(T // tm,),
        in_specs=[row(D), _resident(), _resident()],
        out_specs=[row(2 * F), row(F), row(D)],
        out_shape=[jax.ShapeDtypeStruct((T, 2 * F), BF16), jax.ShapeDtypeStruct((T, F), BF16),
                   jax.ShapeDtypeStruct((T, D), F32)],
        compiler_params=_params(("parallel",)),
    )(h, w_in_t, w_out)


def _ffn_bwd(df, u, w_in_t, w_out, name):
    T, D = df.shape
    F = w_out.shape[0]
    cw = _ffn_chunk(F)
    tm = _pick(T, (FFN_ROWS, ROW))

    def body(df_ref, u_ref, wi_ref, wo_ref, du_ref, dh_ref):
        dfv = df_ref[...]
        acc = jnp.zeros((tm, D), F32)
        for j in range(F // cw):
            ds = lax.dot_general(dfv, wo_ref[j * cw:(j + 1) * cw, :], _NT, preferred_element_type=F32)
            a = u_ref[:, j * cw:(j + 1) * cw].astype(F32)
            b = u_ref[:, F + j * cw:F + (j + 1) * cw].astype(F32)
            sig = jax.nn.sigmoid(a)
            da = (ds * b * (sig * (1.0 + a * (1.0 - sig)))).astype(BF16)
            db = (ds * (a * sig)).astype(BF16)
            du_ref[:, j * cw:(j + 1) * cw] = da
            du_ref[:, F + j * cw:F + (j + 1) * cw] = db
            acc = acc + jnp.dot(da, wi_ref[j * cw:(j + 1) * cw, :], preferred_element_type=F32)
            acc = acc + jnp.dot(db, wi_ref[F + j * cw:F + (j + 1) * cw, :], preferred_element_type=F32)
        dh_ref[...] = acc

    row = lambda w: pl.BlockSpec((tm, w), lambda i: (i, 0))
    return pl.pallas_call(
        body, name=name, grid=(T // tm,),
        in_specs=[row(D), row(2 * F), _resident(), _resident()],
        out_specs=[row(2 * F), row(D)],
        out_shape=[jax.ShapeDtypeStruct((T, 2 * F), BF16), jax.ShapeDtypeStruct((T, D), F32)],
        compiler_params=_params(("parallel",)),
    )(df, u, w_in_t, w_out)


def _halo_specs(width, col, nt):
    per = ROW // HALO
    prev = pl.BlockSpec((HALO, width), lambda i, col=col: (jnp.maximum(i * per - 1, 0), col))
    nxt = pl.BlockSpec((HALO, width), lambda i, col=col: (jnp.minimum((i + 1) * per, nt * per - 1), col))
    return prev, nxt


def _f32(ref):
    return ref[...].astype(F32)


def _last_row(halo_ref):
    return halo_ref[HALO - 1:HALO, :].astype(F32)


def _first_row(halo_ref):
    return halo_ref[0:1, :].astype(F32)


def _shift_rows(v, prev_row, next_row):
    rows = lax.broadcasted_iota(jnp.int32, v.shape, 0)
    down = jnp.where(rows == 0, prev_row, pltpu.roll(v, 1, 0))
    up = jnp.where(rows == v.shape[0] - 1, next_row, pltpu.roll(v, v.shape[0] - 1, 0))
    return down, up


def _conv_fwd(P, conv_w, D, name):
    T = P.shape[0]
    nt = T // ROW
    cg_p, cg_n = _halo_specs(D, 1, nt)
    vc_p, vc_n = _halo_specs(D, 2, nt)

    def body(bg_ref, cg_ref, vc_ref, cgp_ref, vcp_ref, cgn_ref, vcn_ref, w_ref, y_ref):
        i = pl.program_id(0)
        has_prev = (i != 1).astype(F32)
        has_next = (i != nt - 1).astype(F32)
        u = _f32(cg_ref) * _f32(vc_ref)
        up_row = _last_row(cgp_ref) * _last_row(vcp_ref) * has_prev
        un_row = _first_row(cgn_ref) * _first_row(vcn_ref) * has_next
        um1, up1 = _shift_rows(u, up_row, un_row)
        w = w_ref[...]
        conv = um1 * w[0:1, :] + u * w[1:2, :] + up1 * w[2:3, :]
        y_ref[...] = (_f32(bg_ref) * conv).astype(BF16)

    return pl.pallas_call(
        body, name=name, grid=(nt,),
        in_specs=[_row_spec(D, 0), _row_spec(D, 1), _row_spec(D, 2), cg_p, vc_p, cg_n, vc_n, _vec_spec(3, D)],
        out_specs=_row_spec(D),
        out_shape=jax.ShapeDtypeStruct((T, D), BF16),
        compiler_params=_params(("parallel",)),
    )(P, P, P, P, P, P, P, conv_w)


def _conv_bwd(P, dy, conv_w, D, name):
    T = P.shape[0]
    nt = T // ROW
    bg_p, bg_n = _halo_specs(D, 0, nt)
    cg_p, cg_n = _halo_specs(D, 1, nt)
    vc_p, vc_n = _halo_specs(D, 2, nt)
    dy_p, dy_n = _halo_specs(D, 0, nt)

    def body(bg_ref, cg_ref, vc_ref, dy_ref, bgp_ref, cgp_ref, vcp_ref, dyp_ref,
             bgn_ref, cgn_ref, vcn_ref, dyn_ref, w_ref, o_ref, acc_ref):
        _acc_init(acc_ref)
        i = pl.program_id(0)
        lat = (i > 0).astype(F32)
        has_prev = (i != 1).astype(F32)
        has_next = (i != nt - 1).astype(F32)
        bg = _f32(bg_ref)
        cg = _f32(cg_ref)
        vc = _f32(vc_ref)
        dyv = dy_ref[...] * lat
        u = cg * vc
        up_row = _last_row(cgp_ref) * _last_row(vcp_ref) * has_prev
        un_row = _first_row(cgn_ref) * _first_row(vcn_ref) * has_next
        um1, up1 = _shift_rows(u, up_row, un_row)
        w = w_ref[...]
        conv = um1 * w[0:1, :] + u * w[1:2, :] + up1 * w[2:3, :]
        dc = dyv * bg
        dcp_row = _last_row(dyp_ref) * _last_row(bgp_ref) * has_prev
        dcn_row = _first_row(dyn_ref) * _first_row(bgn_ref) * has_next
        dcm1, dcp1 = _shift_rows(dc, dcp_row, dcn_row)
        du = dcp1 * w[0:1, :] + dc * w[1:2, :] + dcm1 * w[2:3, :]
        o_ref[:, 0:D] = (dyv * conv).astype(BF16)
        o_ref[:, D:2 * D] = (du * vc * lat).astype(BF16)
        o_ref[:, 2 * D:3 * D] = (du * cg * lat).astype(BF16)
        _acc_add(acc_ref, 0, dc * um1)
        _acc_add(acc_ref, 1, dc * u)
        _acc_add(acc_ref, 2, dc * up1)

    return pl.pallas_call(
        body, name=name, grid=(nt,),
        in_specs=[_row_spec(D, 0), _row_spec(D, 1), _row_spec(D, 2), _row_spec(D, 0),
                  bg_p, cg_p, vc_p, dy_p, bg_n, cg_n, vc_n, dy_n, _vec_spec(3, D)],
        out_specs=[_row_spec(3 * D), _acc_spec(D)],
        out_shape=[jax.ShapeDtypeStruct((T, 3 * D), BF16), jax.ShapeDtypeStruct((2, ACC_ROWS, D), F32)],
        compiler_params=_params(("arbitrary",)),
    )(P, P, P, dy, P, P, P, dy, P, P, P, dy, conv_w)


def _rope_tables(ctx_len, seq):
    n_freq = HEAD_DIM // 4
    rows = seq // GRID_W
    inv = ROPE_THETA ** (-jnp.arange(n_freq, dtype=F32) / n_freq)
    ar = jnp.arange(rows, dtype=F32)[:, None] * inv
    ac = jnp.arange(GRID_W, dtype=F32)[:, None] * inv

    def per_row(a):
        return jnp.repeat(a, GRID_W, axis=0)

    def per_col(a):
        return jnp.tile(a, (rows, 1))

    cos_t = jnp.concatenate([per_row(jnp.cos(ar)), per_row(jnp.cos(ar)), per_col(jnp.cos(ac)), per_col(jnp.cos(ac))], axis=1)
    sin_t = jnp.concatenate([per_row(-jnp.sin(ar)), per_row(jnp.sin(ar)), per_col(-jnp.sin(ac)), per_col(jnp.sin(ac))], axis=1)
    cos_t = jnp.concatenate([jnp.ones((ctx_len, HEAD_DIM), F32), cos_t], axis=0)
    sin_t = jnp.concatenate([jnp.zeros((ctx_len, HEAD_DIM), F32), sin_t], axis=0)
    return cos_t, sin_t


def _swap_halves(y):
    lanes = lax.broadcasted_iota(jnp.int32, y.shape, 1)
    first = (lanes % 64) < 32
    return jnp.where(first, pltpu.roll(y, HEAD_DIM - 32, 1), pltpu.roll(y, 32, 1))


def _qk_fwd(P, gq, gk, cos_t, sin_t, D, name):
    T = P.shape[0]
    QW = N_Q_HEADS * HEAD_DIM
    KW = N_KV_HEADS * HEAD_DIM
    q_col = (3 * D) // QW
    k_col = (3 * D + QW) // KW
    v_col = k_col + 1

    def body(q_ref, k_ref, v_ref, gq_ref, gk_ref, c_ref, s_ref, qo_ref, ko_ref, vo_ref):
        c = c_ref[...]
        s = s_ref[...]

        def head(x, g):
            inv = lax.rsqrt(jnp.mean(x * x, axis=-1, keepdims=True) + EPS)
            y = (x * inv) * g
            return y * c + _swap_halves(y) * s

        for h in range(N_Q_HEADS):
            sl = slice(h * HEAD_DIM, (h + 1) * HEAD_DIM)
            qo_ref[:, sl] = head(q_ref[:, sl].astype(F32), gq_ref[...]).astype(BF16)
        for h in range(N_KV_HEADS):
            sl = slice(h * HEAD_DIM, (h + 1) * HEAD_DIM)
            ko_ref[:, sl] = head(k_ref[:, sl].astype(F32), gk_ref[...]).astype(BF16)
        vo_ref[...] = v_ref[...].astype(BF16)

    return pl.pallas_call(
        body, name=name, grid=(T // ROW,),
        in_specs=[_row_spec(QW, q_col), _row_spec(KW, k_col), _row_spec(KW, v_col),
                  _vec_spec(1, HEAD_DIM), _vec_spec(1, HEAD_DIM), _row_spec(HEAD_DIM), _row_spec(HEAD_DIM)],
        out_specs=[_row_spec(QW), _row_spec(KW), _row_spec(KW)],
        out_shape=[jax.ShapeDtypeStruct((T, QW), BF16), jax.ShapeDtypeStruct((T, KW), BF16),
                   jax.ShapeDtypeStruct((T, KW), BF16)],
        compiler_params=_params(("parallel",)),
    )(P, P, P, gq, gk, cos_t, sin_t)


def _qk_bwd(P, dq, dk, dv, gq, gk, cos_t, sin_t, D, name):
    T = P.shape[0]
    QW = N_Q_HEADS * HEAD_DIM
    KW = N_KV_HEADS * HEAD_DIM
    q_col = (3 * D) // QW
    k_col = (3 * D + QW) // KW

    def body(q_ref, k_ref, dq_ref, dk_ref, dv_ref, gq_ref, gk_ref, c_ref, s_ref, o_ref, acc_ref):
        _acc_init(acc_ref)
        c = c_ref[...]
        s = s_ref[...]

        def head(x, d, g):
            dyv = d * c + _swap_halves(d * s)
            inv = lax.rsqrt(jnp.mean(x * x, axis=-1, keepdims=True) + EPS)
            xn = x * inv
            dxn = dyv * g
            dx = inv * (dxn - xn * jnp.mean(dxn * xn, axis=-1, keepdims=True))
            return dx, jnp.sum(dyv * xn, axis=0, keepdims=True)

        dgq = jnp.zeros((1, HEAD_DIM), F32)
        for h in range(N_Q_HEADS):
            sl = slice(h * HEAD_DIM, (h + 1) * HEAD_DIM)
            dx, dg = head(q_ref[:, sl].astype(F32), dq_ref[:, sl], gq_ref[...])
            o_ref[:, sl] = dx.astype(BF16)
            dgq = dgq + dg
        dgk = jnp.zeros((1, HEAD_DIM), F32)
        for h in range(N_KV_HEADS):
            sl = slice(h * HEAD_DIM, (h + 1) * HEAD_DIM)
            dx, dg = head(k_ref[:, sl].astype(F32), dk_ref[:, sl], gk_ref[...])
            o_ref[:, QW + h * HEAD_DIM:QW + (h + 1) * HEAD_DIM] = dx.astype(BF16)
            dgk = dgk + dg
        o_ref[:, QW + KW:QW + 2 * KW] = dv_ref[...].astype(BF16)
        acc_ref[0, 0:1, 0:HEAD_DIM] += dgq
        acc_ref[0, 1:2, 0:HEAD_DIM] += dgk

    return pl.pallas_call(
        body, name=name, grid=(T // ROW,),
        in_specs=[_row_spec(QW, q_col), _row_spec(KW, k_col), _row_spec(QW), _row_spec(KW), _row_spec(KW),
                  _vec_spec(1, HEAD_DIM), _vec_spec(1, HEAD_DIM), _row_spec(HEAD_DIM), _row_spec(HEAD_DIM)],
        out_specs=[_row_spec(QW + 2 * KW), _acc_spec(D)],
        out_shape=[jax.ShapeDtypeStruct((T, QW + 2 * KW), BF16), jax.ShapeDtypeStruct((2, ACC_ROWS, D), F32)],
        compiler_params=_params(("arbitrary",)),
    )(P, P, dq, dk, dv, gq, gk, cos_t, sin_t)


def _to_row(col, n):
    return jnp.transpose(jnp.broadcast_to(col, (n, HEAD_DIM)))[0:1, :]


LOG2E = 1.4426950408889634
ATTN_PARTS = 4


def _flash_fwd(q, k, v, name, tq=ROW, tk=None):
    T = q.shape[0]
    tk = tk or _pick(T, (1408, 768, 512, 256))
    ck = tk
    nk = T // tk
    GW = GROUP * HEAD_DIM

    def body(q_ref, k_ref, v_ref, o_ref, lse_ref, qs_ref, m_ref, l_ref, acc_ref, st_ref):
        ki = pl.program_id(2)

        @pl.when(ki == 0)
        def _():
            for g in range(GROUP):
                qs_ref[g * tq:(g + 1) * tq, :] = q_ref[:, g * HEAD_DIM:(g + 1) * HEAD_DIM]
            m_ref[...] = jnp.full(m_ref.shape, -jnp.inf, F32)
            l_ref[...] = jnp.zeros(l_ref.shape, F32)
            acc_ref[...] = jnp.zeros(acc_ref.shape, F32)

        w = GROUP * tq // ATTN_PARTS
        nck = tk // ck

        def lanes(p):
            return slice(p * w, (p + 1) * w)

        def keys(c):
            return slice(c * ck, (c + 1) * ck)

        def fold(a):
            return a.reshape(ck // 8, 8, w)

        def scores(p, c):
            st = lax.dot_general(k_ref[keys(c), :], qs_ref[lanes(p), :], _NT,
                                 preferred_element_type=F32) * (ATTN_SCALE * LOG2E)
            st_ref[keys(c), lanes(p)] = st
            return jnp.max(fold(st), axis=0)

        def new_max(p, partial):
            m_prev = m_ref[:, lanes(p)]
            m_new = jnp.maximum(m_prev, jnp.max(functools.reduce(jnp.maximum, partial), axis=0, keepdims=True))
            m_ref[:, lanes(p)] = m_new
            return m_new, jnp.exp2(m_prev - m_new)

        def weights(p, c, m_new):
            pt = jnp.exp2(st_ref[keys(c), lanes(p)] - m_new)
            pv = lax.dot_general(v_ref[keys(c), :], pt.astype(BF16), (((0,), (0,)), ((), ())),
                                 preferred_element_type=F32)
            return jnp.sum(fold(pt), axis=0), pv

        partial = [scores(0, c) for c in range(nck)]
        for p in range(ATTN_PARTS):
            m_new, alpha = new_max(p, partial)
            partial, sums, pvs = [], [], []
            for c in range(nck):
                if p + 1 < ATTN_PARTS:
                    partial.append(scores(p + 1, c))
                s8, pv = weights(p, c, m_new)
                sums.append(s8)
                pvs.append(pv)
            l_ref[:, lanes(p)] = alpha * l_ref[:, lanes(p)] + jnp.sum(sum(sums), axis=0, keepdims=True)
            acc_ref[:, lanes(p)] = alpha * acc_ref[:, lanes(p)] + sum(pvs)

        @pl.when(ki == nk - 1)
        def _():
            out = jnp.transpose(acc_ref[...] / l_ref[...])
            lse = m_ref[...] + jnp.log2(l_ref[...])
            for g in range(GROUP):
                o_ref[:, g * HEAD_DIM:(g + 1) * HEAD_DIM] = out[g * tq:(g + 1) * tq, :]
                lse_ref[0, g:g + 1, :] = lse[:, g * tq:(g + 1) * tq]

    return pl.pallas_call(
        body, name=name, grid=(N_KV_HEADS, T // tq, nk),
        in_specs=[pl.BlockSpec((tq, GW), lambda h, i, j: (i, h)),
                  pl.BlockSpec((tk, HEAD_DIM), lambda h, i, j: (j, h)),
                  pl.BlockSpec((tk, HEAD_DIM), lambda h, i, j: (j, h))],
        out_specs=[pl.BlockSpec((tq, GW), lambda h, i, j: (i, h)),
                   pl.BlockSpec((1, GROUP, tq), lambda h, i, j: (h, 0, i))],
        out_shape=[jax.ShapeDtypeStruct((T, N_Q_HEADS * HEAD_DIM), F32),
                   jax.ShapeDtypeStruct((N_KV_HEADS, GROUP, T), F32)],
        scratch_shapes=[pltpu.VMEM((GROUP * tq, HEAD_DIM), BF16), pltpu.VMEM((1, GROUP * tq), F32),
                        pltpu.VMEM((1, GROUP * tq), F32), pltpu.VMEM((HEAD_DIM, GROUP * tq), F32),
                        pltpu.VMEM((tk, GROUP * tq), F32)],
        compiler_params=_params(("parallel", "parallel", "arbitrary")),
    )(q, k, v)


def _attn_delta(do, o, name):
    T, QW = do.shape

    def body(do_ref, o_ref, dob_ref, dl_ref):
        dov = do_ref[...]
        dob_ref[...] = dov.astype(BF16)
        prod = dov * o_ref[...]
        for h in range(N_Q_HEADS):
            d = jnp.sum(prod[:, h * HEAD_DIM:(h + 1) * HEAD_DIM], axis=1, keepdims=True)
            dl_ref[h // GROUP, (h % GROUP):(h % GROUP) + 1, :] = _to_row(d, ROW)

    return pl.pallas_call(
        body, name=name, grid=(T // ROW,),
        in_specs=[_row_spec(QW), _row_spec(QW)],
        out_specs=[_row_spec(QW), pl.BlockSpec((N_KV_HEADS, GROUP, ROW), lambda i: (0, 0, i))],
        out_shape=[jax.ShapeDtypeStruct((T, QW), BF16), jax.ShapeDtypeStruct((N_KV_HEADS, GROUP, T), F32)],
        compiler_params=_params(("parallel",)),
    )(do, o)


def _flash_bwd(q, k, v, do, lse, delta, name, tq=ROW, tk=None):
    T = q.shape[0]
    tk = tk or _pick(T, (1408, 768, 512, 256))
    nk = T // tk
    GW = GROUP * HEAD_DIM
    nt = (((1,), (1,)), ((), ()))

    def body(q_ref, do_ref, k_ref, v_ref, lse_ref, dl_ref, dq_ref, dk_ref, dv_ref, qs_ref, dos_ref, dqt_ref):
        qi = pl.program_id(1)
        ki = pl.program_id(2)

        @pl.when(ki == 0)
        def _():
            for g in range(GROUP):
                qs_ref[g * tq:(g + 1) * tq, :] = q_ref[:, g * HEAD_DIM:(g + 1) * HEAD_DIM]
                dos_ref[g * tq:(g + 1) * tq, :] = do_ref[:, g * HEAD_DIM:(g + 1) * HEAD_DIM]
            dqt_ref[...] = jnp.zeros(dqt_ref.shape, F32)

        kk = k_ref[...]
        vv = v_ref[...]

        def lanes(p):
            return slice(p * tq, (p + 1) * tq)

        def products(p):
            st = lax.dot_general(kk, qs_ref[lanes(p), :], nt, preferred_element_type=F32)
            dpt = lax.dot_general(vv, dos_ref[lanes(p), :], nt, preferred_element_type=F32)
            return st, dpt

        dk_c = jnp.zeros((tk, HEAD_DIM), F32)
        dv_c = jnp.zeros((tk, HEAD_DIM), F32)
        ahead = products(0)
        for p in range(GROUP):
            st, dpt = ahead
            if p + 1 < GROUP:
                ahead = products(p + 1)
            pt = jnp.exp2(st * (ATTN_SCALE * LOG2E) - lse_ref[0, p:p + 1, :])
            dst = ((pt * (dpt - dl_ref[0, p:p + 1, :])) * ATTN_SCALE).astype(BF16)
            dv_c = dv_c + jnp.dot(pt.astype(BF16), dos_ref[lanes(p), :], preferred_element_type=F32)
            dk_c = dk_c + jnp.dot(dst, qs_ref[lanes(p), :], preferred_element_type=F32)
            dqt_ref[:, lanes(p)] += lax.dot_general(kk, dst, (((0,), (0,)), ((), ())), preferred_element_type=F32)
        rows = pl.ds(pl.multiple_of(ki * tk, tk), tk)

        @pl.when(qi == 0)
        def _():
            dk_ref[rows, :] = dk_c
            dv_ref[rows, :] = dv_c

        @pl.when(qi > 0)
        def _():
            dk_ref[rows, :] += dk_c
            dv_ref[rows, :] += dv_c

        @pl.when(ki == nk - 1)
        def _():
            dqv = jnp.transpose(dqt_ref[...])
            for g in range(GROUP):
                dq_ref[:, g * HEAD_DIM:(g + 1) * HEAD_DIM] = dqv[g * tq:(g + 1) * tq, :]

    return pl.pallas_call(
        body, name=name, grid=(N_KV_HEADS, T // tq, nk),
        in_specs=[pl.BlockSpec((tq, GW), lambda h, i, j: (i, h)),
                  pl.BlockSpec((tq, GW), lambda h, i, j: (i, h)),
                  pl.BlockSpec((tk, HEAD_DIM), lambda h, i, j: (j, h)),
                  pl.BlockSpec((tk, HEAD_DIM), lambda h, i, j: (j, h)),
                  pl.BlockSpec((1, GROUP, tq), lambda h, i, j: (h, 0, i)),
                  pl.BlockSpec((1, GROUP, tq), lambda h, i, j: (h, 0, i))],
        out_specs=[pl.BlockSpec((tq, GW), lambda h, i, j: (i, h)),
                   pl.BlockSpec((T, HEAD_DIM), lambda h, i, j: (0, h)),
                   pl.BlockSpec((T, HEAD_DIM), lambda h, i, j: (0, h))],
        out_shape=[jax.ShapeDtypeStruct((T, N_Q_HEADS * HEAD_DIM), F32),
                   jax.ShapeDtypeStruct((T, N_KV_HEADS * HEAD_DIM), F32),
                   jax.ShapeDtypeStruct((T, N_KV_HEADS * HEAD_DIM), F32)],
        scratch_shapes=[pltpu.VMEM((GROUP * tq, HEAD_DIM), BF16), pltpu.VMEM((GROUP * tq, HEAD_DIM), BF16),
                        pltpu.VMEM((HEAD_DIM, GROUP * tq), F32)],
        compiler_params=_params(("arbitrary", "arbitrary", "arbitrary")),
    )(q, do, k, v, lse, delta)


def _gate_specs(D):
    w = D // 2
    first = (3 * D + (N_Q_HEADS + 2 * N_KV_HEADS) * HEAD_DIM) // w
    return [pl.BlockSpec((ROW, w), lambda i, c=first + j: (i, c)) for j in range(4)]


def _merge_fwd(a1, a2, P, D, name):
    T = a1.shape[0]
    w = D // 2

    def body(a1_ref, a2_ref, g0, g1, g2, g3, z_ref):
        for j, (gc, ga) in enumerate(((g0, g2), (g1, g3))):
            sl = slice(j * w, (j + 1) * w)
            z = jax.nn.sigmoid(_f32(gc)) * a1_ref[:, sl] + jax.nn.sigmoid(_f32(ga)) * a2_ref[:, sl]
            z_ref[:, sl] = z.astype(BF16)

    return pl.pallas_call(
        body, name=name, grid=(T // ROW,),
        in_specs=[_row_spec(D), _row_spec(D)] + _gate_specs(D),
        out_specs=_row_spec(D), out_shape=jax.ShapeDtypeStruct((T, D), BF16),
        compiler_params=_params(("parallel",)),
    )(a1, a2, P, P, P, P)


def _merge_bwd(dz, a1, a2, P, D, name):
    T = a1.shape[0]
    w = D // 2

    def body(dz_ref, a1_ref, a2_ref, g0, g1, g2, g3, d1_ref, d2_ref, dg_ref):
        for j, (gc, ga) in enumerate(((g0, g2), (g1, g3))):
            sl = slice(j * w, (j + 1) * w)
            dz = dz_ref[:, sl]
            sc = jax.nn.sigmoid(_f32(gc))
            sa = jax.nn.sigmoid(_f32(ga))
            d1_ref[:, sl] = (dz * sc).astype(BF16)
            d2_ref[:, sl] = (dz * sa).astype(BF16)
            dg_ref[:, j * w:(j + 1) * w] = (dz * a1_ref[:, sl] * (sc * (1.0 - sc))).astype(BF16)
            dg_ref[:, D + j * w:D + (j + 1) * w] = (dz * a2_ref[:, sl] * (sa * (1.0 - sa))).astype(BF16)

    return pl.pallas_call(
        body, name=name, grid=(T // ROW,),
        in_specs=[_row_spec(D), _row_spec(D), _row_spec(D)] + _gate_specs(D),
        out_specs=[_row_spec(D), _row_spec(D), _row_spec(2 * D)],
        out_shape=[jax.ShapeDtypeStruct((T, D), BF16), jax.ShapeDtypeStruct((T, D), BF16),
                   jax.ShapeDtypeStruct((T, 2 * D), BF16)],
        compiler_params=_params(("parallel",)),
    )(dz, a1, a2, P, P, P, P)


def _adamw_math(w, g, m, v):
    m = ADAM_B1 * m + (1.0 - ADAM_B1) * g
    v = ADAM_B2 * v + (1.0 - ADAM_B2) * (g * g)
    m_hat = m / (1.0 - ADAM_B1 ** ADAM_STEP)
    v_hat = v / (1.0 - ADAM_B2 ** ADAM_STEP)
    delta = -ADAM_LR * (m_hat / (jnp.sqrt(v_hat) + ADAM_EPS) + ADAM_WD * w)
    return delta, m, v


def _adamw(w, g, m, v, name):
    R, C = w.shape
    tr = _pick(R, tuple(t for t in (256, 128, 64, 32, 16, 8) if t * C * 4 <= ADAMW_BLOCK_BYTES))

    def body(w_ref, g_ref, m_ref, v_ref, d_ref, mo_ref, vo_ref):
        d, mn, vn = _adamw_math(w_ref[...], g_ref[...], m_ref[...], v_ref[...])
        d_ref[...] = d
        mo_ref[...] = mn
        vo_ref[...] = vn

    spec = pl.BlockSpec((tr, C), lambda i: (i, 0))
    return pl.pallas_call(
        body, name=name, grid=(R // tr,),
        in_specs=[spec] * 4, out_specs=[spec] * 3,
        out_shape=[jax.ShapeDtypeStruct((R, C), F32)] * 3,
        compiler_params=_params(("parallel",)),
    )(w, g, m, v)


def _concat_k_matmul(parts, b, name):
    T = parts[0].shape[0]
    N = b.shape[1]
    tm = _pick(T, (768, 512, 256))
    offs = [0]
    for p in parts:
        offs.append(offs[-1] + p.shape[1])
    assert offs[-1] == b.shape[0]

    def body(*refs):
        b_ref, o_ref = refs[len(parts)], refs[len(parts) + 1]
        acc = None
        for i, a_ref in enumerate(refs[:len(parts)]):
            d = jnp.dot(a_ref[...], b_ref[offs[i]:offs[i + 1], :], preferred_element_type=F32)
            acc = d if acc is None else acc + d
        o_ref[...] = acc

    return pl.pallas_call(
        body, name=name, grid=(T // tm,),
        in_specs=[pl.BlockSpec((tm, p.shape[1]), lambda i: (i, 0)) for p in parts] + [_resident()],
        out_specs=pl.BlockSpec((tm, N), lambda i: (i, 0)),
        out_shape=jax.ShapeDtypeStruct((T, N), F32),
        compiler_params=_params(("parallel",)),
    )(*parts, b)


def _adamw_transposed(w, gt, m, v, name):
    R, C = w.shape
    tc = 128

    def body(w_ref, g_ref, m_ref, v_ref, go_ref, d_ref, mo_ref, vo_ref):
        g = jnp.transpose(g_ref[...])
        d, mn, vn = _adamw_math(w_ref[...], g, m_ref[...], v_ref[...])
        go_ref[...] = g
        d_ref[...] = d
        mo_ref[...] = mn
        vo_ref[...] = vn

    spec = pl.BlockSpec((R, tc), lambda j: (0, j))
    return pl.pallas_call(
        body, name=name, grid=(C // tc,),
        in_specs=[spec, pl.BlockSpec((tc, R), lambda j: (j, 0)), spec, spec], out_specs=[spec] * 4,
        out_shape=[jax.ShapeDtypeStruct((R, C), F32)] * 4,
        compiler_params=_params(("parallel",)),
    )(w, gt, m, v)


class _NoExchange:
    def __init__(self, rest):
        self.rest = rest

    def rest_weights(self, after):
        return self.rest

    def reduce_early(self, grads, tag):
        return None


def _local_step(xcat, target, mods, norm_g, final_g, gq, gk, conv_w, ffn1_w, hooks, ctx_len):
    T, D = xcat.shape
    w1i, w1o = ffn1_w
    g1, g2, g3 = norm_g
    cos_t, sin_t = _rope_tables(ctx_len, T - ctx_len)

    def after(value, token):
        return value if token is None else _after(value, token)

    _, h1 = _resid_rmsmod_fwd(xcat, None, mods, g1, None, 0, 1, "f_norm1")
    u1, s1, f1 = _ffn_fwd(h1, w1i, w1o, "f_ffn1")
    wi, wbc, wba, wo, w2i, w2o = hooks.rest_weights(f1)
    x1, h2 = _resid_rmsmod_fwd(xcat, f1, mods, g2, (2, 0.5), 3, 4, "f_norm2")
    P = _matmul(h2, wi, "nt", BF16, "f_mix_in")
    yc = _conv_fwd(P, conv_w, D, "f_conv")
    qn, kn, vb = _qk_fwd(P, gq, gk, cos_t, sin_t, D, "f_qk")
    o, lse = _flash_fwd(qn, kn, vb, "f_attn")
    a1 = _matmul(yc, wbc, "nn", F32, "f_branch_conv")
    a2 = _matmul(o, wba, "nn", F32, "f_branch_attn")
    z = _merge_fwd(a1, a2, P, D, "f_merge")
    mo = _matmul(z, wo, "nn", F32, "f_mix_out")
    x2, h3 = _resid_rmsmod_fwd(x1, mo, mods, g3, (5, 1.0), 6, 7, "f_norm3")
    u2, s2, f2 = _ffn_fwd(h3, w2i, w2o, "f_ffn2")
    dx3, df2, acc_head = _loss_head(x2, f2, mods, final_g, target, "loss_head")

    du2, dh3 = _ffn_bwd(df2, u2, w2i, w2o, "b_ffn2")
    g_w2o = _matmul(s2, df2, "tn", BF16, "b_ffn2_out_dw")
    g_w2i = _matmul(du2, h3, "tn", BF16, "b_ffn2_in_dw")
    dx2, dmo, acc_n3 = _rmsmod_bwd(x2, dh3, dx3, mods, g3, 6, 7, (5, 1.0), mo, "b_norm3")

    dz = _matmul(dmo, wo, "nt", F32, "b_mix_out_dx")
    g_wo = _matmul(z, dmo, "tn", BF16, "b_mix_out_dw")
    da1, da2, dgt = _merge_bwd(dz, a1, a2, P, D, "b_merge")
    dyc = _matmul(da1, wbc, "nt", F32, "b_branch_conv_dx")
    do = _matmul(da2, wba, "nt", F32, "b_branch_attn_dx")
    g_wbc = _matmul(yc, da1, "tn", BF16, "b_branch_conv_dw")
    g_wba = _matmul(o, da2, "tn", BF16, "b_branch_attn_dw")
    do = after(do, hooks.reduce_early([g_wbc, g_wba, g_wo, g_w2i, g_w2o], "a"))
    dob, delta = _attn_delta(do, o, "b_attn_delta")
    dq, dk, dv = _flash_bwd(qn, kn, vb, dob, lse, delta, "b_attn")
    dqkv, acc_qk = _qk_bwd(P, dq, dk, dv, gq, gk, cos_t, sin_t, D, "b_qk")
    dconv, acc_conv = _conv_bwd(P, dyc, conv_w, D, "b_conv")
    d_parts = (dconv, dqkv, dgt)
    dh2 = _concat_k_matmul(d_parts, wi, "b_mix_in_dx")
    g_wi = jnp.concatenate([_matmul(dp, h2, "tn", BF16, f"b_mix_in_dw_{i}") for i, dp in enumerate(d_parts)], axis=0)
    dh2 = after(dh2, hooks.reduce_early([g_wi], "b"))
    dx1, df1, acc_n2 = _rmsmod_bwd(x1, dh2, dx2, mods, g2, 3, 4, (2, 0.5), f1, "b_norm2")

    du1, dh1 = _ffn_bwd(df1, u1, w1i, w1o, "b_ffn1")
    g_w1o = _matmul(s1, df1, "tn", BF16, "b_ffn1_out_dw")
    du1 = after(du1, hooks.reduce_early([g_w1o], "c"))
    g_w1i = _matmul(du1, h1, "tn", BF16, "b_ffn1_in_dw")
    dh1 = after(dh1, hooks.reduce_early([g_w1i], "d"))
    grad_x, _, acc_n1 = _rmsmod_bwd(xcat, dh1, dx1, mods, g1, 0, 1, None, None, "b_norm1", skip_first_tile=True)

    grads = (g_w1i, g_w1o, g_wi, g_wbc, g_wba, g_wo, g_w2i, g_w2o)
    accs = (acc_head, acc_n3, acc_n2, acc_n1, acc_conv, acc_qk)
    return grad_x, grads, accs


def _place():
    return lax.axis_index("x"), lax.axis_index("y"), lax.axis_index("c")


def _other_chips(x, y):
    return [(1 - x, y), (x, 1 - y), (1 - x, 1 - y)]


def _allgather8(v, name):
    R, N = v.shape

    def body(v_ref, out_ref, send_sems, recv_sems, local_sem):
        x, y, c = _place()
        me, sibling = (x, y, c), (x, y, 1 - c)
        chips = _other_chips(x, y)

        def blk(px, py, pc):
            return out_ref.at[4 * px + 2 * py + pc]

        def copy(k, block, to, src=None):
            return pltpu.make_async_remote_copy(
                src_ref=blk(*block) if src is None else src, dst_ref=blk(*block),
                send_sem=send_sems.at[k], recv_sem=recv_sems.at[k], device_id=to, device_id_type=MESH)

        mine = pltpu.make_async_copy(v_ref, blk(*me), local_sem)
        mine.start()
        first = [copy(0, me, sibling, src=v_ref)]
        first += [copy(1 + j, me, (*chip, c), src=v_ref) for j, chip in enumerate(chips)]
        for cp in first:
            cp.start()
        passed = [copy(4 + j, (*chip, c), sibling) for j, chip in enumerate(chips)]
        for j, chip in enumerate(chips):
            copy(1 + j, (*chip, c), me).wait_recv()
            passed[j].start()
        copy(0, sibling, me).wait_recv()
        for j, chip in enumerate(chips):
            copy(4 + j, (*chip, 1 - c), me).wait_recv()
        for cp in first + passed:
            cp.wait_send()
        mine.wait()

    return pl.pallas_call(
        body, name=name,
        out_shape=jax.ShapeDtypeStruct((N_DEV, R, N), v.dtype),
        in_specs=[pl.BlockSpec(memory_space=pltpu.VMEM)],
        out_specs=pl.BlockSpec(memory_space=pltpu.VMEM),
        scratch_shapes=[pltpu.SemaphoreType.DMA((7,)), pltpu.SemaphoreType.DMA((7,)), pltpu.SemaphoreType.DMA],
        compiler_params=pltpu.CompilerParams(vmem_limit_bytes=VMEM_LIMIT),
    )(v)


def _any_specs(n):
    return [pl.BlockSpec(memory_space=pl.ANY)] * n


def _pair_exchange(grads, name):
    n = len(grads)

    def body(*refs):
        g, land = refs[:n], refs[n:2 * n]
        send_sems, recv_sems = refs[2 * n:]
        x, y, c = _place()
        sibling = (x, y, 1 - c)
        copies = []
        for t in range(n):
            half = grads[t].shape[0] // (2 * N_CHIPS)
            for s in range(N_CHIPS):
                cp = pltpu.make_async_remote_copy(
                    src_ref=g[t].at[pl.ds((2 * s + 1 - c) * half, half), :], dst_ref=land[t].at[s],
                    send_sem=send_sems.at[N_CHIPS * t + s], recv_sem=recv_sems.at[N_CHIPS * t + s],
                    device_id=sibling, device_id_type=MESH)
                cp.start()
                copies.append(cp)
        for cp in copies:
            cp.wait_recv()
        for cp in copies:
            cp.wait_send()

    return pl.pallas_call(
        body, name=name,
        out_shape=[jax.ShapeDtypeStruct((N_CHIPS, a.shape[0] // (2 * N_CHIPS), a.shape[1]), a.dtype) for a in grads],
        in_specs=_any_specs(n), out_specs=_any_specs(n),
        scratch_shapes=[pltpu.SemaphoreType.DMA((N_CHIPS * n,)), pltpu.SemaphoreType.DMA((N_CHIPS * n,))],
    )(*grads)


def _place_shard(w2, idx, transpose, name):
    if transpose:
        D, rs = w2.shape
        tr = 128
        in_spec = pl.BlockSpec((D, tr), lambda i, idx: (0, i))
    else:
        rs, D = w2.shape
        tr = _pick(rs, (352, 256, 128, 64, 32, 16))
        in_spec = pl.BlockSpec((tr, D), lambda i, idx: (i, 0))
    steps = rs // tr

    def body(idx_ref, w_ref, o_ref):
        v = w_ref[...]
        o_ref[...] = (jnp.transpose(v) if transpose else v).astype(BF16)

    return pl.pallas_call(
        body, name=name,
        grid_spec=pltpu.PrefetchScalarGridSpec(
            num_scalar_prefetch=1, grid=(steps,), in_specs=[in_spec],
            out_specs=pl.BlockSpec((tr, D), lambda i, idx: (idx[1] * steps + i, 0))),
        out_shape=jax.ShapeDtypeStruct((N_CHIPS * rs, D), BF16),
        compiler_params=_params(("arbitrary",)),
    )(idx, w2)


def _pair_sum(g, landed, idx, name):
    _, half, D = landed.shape
    g4 = g.reshape(N_CHIPS, 2, half, D)
    tr = _pick(half, (416, 352, 128))

    def body(idx_ref, g_ref, l_ref, o_ref):
        o_ref[...] = (g_ref[0].astype(F32) + l_ref[...].astype(F32)).astype(BF16)

    return pl.pallas_call(
        body, name=name,
        grid_spec=pltpu.PrefetchScalarGridSpec(
            num_scalar_prefetch=1, grid=(N_CHIPS, half // tr),
            in_specs=[pl.BlockSpec((1, 1, tr, D), lambda s, i, idx: (idx[1 + s], idx[0], i, 0)),
                      pl.BlockSpec((1, tr, D), lambda s, i, idx: (idx[1 + s], i, 0))],
            out_specs=pl.BlockSpec((1, tr, D), lambda s, i, idx: (s, i, 0))),
        out_shape=jax.ShapeDtypeStruct((N_CHIPS, half, D), BF16),
        compiler_params=_params(("arbitrary", "arbitrary")),
    )(idx, g4, landed)


_HBM = pl.BlockSpec(memory_space=pltpu.HBM)
_SEM = pl.BlockSpec(memory_space=pltpu.SEMAPHORE)
_EFFECT = pltpu.SideEffectType.DATAFLOW_SIDE_EFFECTING


def _in_hbm(a):
    return pltpu.with_memory_space_constraint(a, pltpu.HBM)


def _split_copies(n, per, make):
    def start(nbuf, name, bufs):
        def body(*refs):
            ins = refs[:nbuf]
            send_sems, recv_sems = refs[nbuf], refs[nbuf + 1]
            token = refs[-1]
            for t in range(n):
                for j in range(per):
                    make(ins, t, j, send_sems.at[per * t + j], recv_sems.at[per * t + j]).start()
            token[...] = jnp.zeros(token.shape, token.dtype)

        out = pl.pallas_call(
            body, name=name,
            out_shape=(pltpu.SemaphoreType.DMA((per * n,)), pltpu.SemaphoreType.DMA((per * n,)),
                       *[pltpu.HBM(b.shape, b.dtype) for b in bufs], jax.ShapeDtypeStruct((8, 128), F32)),
            in_specs=[_HBM] * nbuf,
            out_specs=(_SEM, _SEM, *[_HBM] * nbuf, pl.BlockSpec(memory_space=pltpu.VMEM)),
            input_output_aliases={i: 2 + i for i in range(nbuf)},
            compiler_params=pltpu.CompilerParams(has_side_effects=_EFFECT),
        )(*[_in_hbm(b) for b in bufs])
        return out[0], out[1], list(out[2:2 + nbuf]), out[-1]

    def wait(nbuf, name, send_sems, recv_sems, bufs, after):
        def body(*refs):
            ins = refs[:nbuf]
            ss, rs = refs[nbuf], refs[nbuf + 1]
            for t in range(n):
                for j in range(per):
                    cp = make(ins, t, j, ss.at[per * t + j], rs.at[per * t + j])
                    cp.wait_send()
                    cp.wait_recv()

        return pl.pallas_call(
            body, name=name,
            out_shape=[pltpu.HBM(b.shape, b.dtype) for b in bufs],
            in_specs=[_HBM] * nbuf + [_SEM, _SEM, pl.BlockSpec(memory_space=pl.ANY)],
            out_specs=[_HBM] * nbuf,
            input_output_aliases={i: i for i in range(nbuf)},
            compiler_params=pltpu.CompilerParams(has_side_effects=_EFFECT),
        )(*bufs, send_sems, recv_sems, after)

    return start, wait


def _chip_exchange_split(n):
    def make(bufs, t, j, send_sem, recv_sem):
        x, y, c = _place()
        chip = _other_chips(x, y)[j]
        return pltpu.make_async_remote_copy(src_ref=bufs[t].at[1 + j], dst_ref=bufs[n + t].at[j], send_sem=send_sem,
                                            recv_sem=recv_sem, device_id=(*chip, c), device_id_type=MESH)

    return _split_copies(n, 3, make)


def _weights_gather_split(fulls):
    def make(bufs, t, j, send_sem, recv_sem):
        x, y, c = _place()
        chip = _other_chips(x, y)[j]
        rs = fulls[t].shape[0] // N_CHIPS
        rows = bufs[t].at[pl.ds((2 * x + y) * rs + c * (rs // 2), rs // 2), :]
        return pltpu.make_async_remote_copy(src_ref=rows, dst_ref=rows, send_sem=send_sem, recv_sem=recv_sem,
                                            device_id=(*chip, c), device_id_type=MESH)

    return _split_copies(len(fulls), 3, make)


def _weights_pass_on(fulls, name):
    n = len(fulls)

    def body(*refs):
        full = refs[n:2 * n]
        send_sems, recv_sems = refs[2 * n:]
        x, y, c = _place()
        chips = _other_chips(x, y)

        def copy(t, j, h):
            rs = fulls[t].shape[0] // N_CHIPS
            px, py = chips[j]
            rows = full[t].at[pl.ds((2 * px + py) * rs + h * (rs // 2), rs // 2), :]
            return pltpu.make_async_remote_copy(src_ref=rows, dst_ref=rows, send_sem=send_sems.at[3 * t + j],
                                                recv_sem=recv_sems.at[3 * t + j], device_id=(x, y, 1 - c),
                                                device_id_type=MESH)

        for t in range(n):
            for j in range(3):
                copy(t, j, c).start()
        for t in range(n):
            for j in range(3):
                copy(t, j, 1 - c).wait_recv()
        for t in range(n):
            for j in range(3):
                copy(t, j, c).wait_send()

    return pl.pallas_call(
        body, name=name,
        out_shape=[jax.ShapeDtypeStruct(f.shape, f.dtype) for f in fulls],
        in_specs=_any_specs(n), out_specs=_any_specs(n),
        input_output_aliases={t: t for t in range(n)},
        scratch_shapes=[pltpu.SemaphoreType.DMA((3 * n,)), pltpu.SemaphoreType.DMA((3 * n,))],
    )(*fulls)


def _after(value, token):
    return lax.optimization_barrier((value, token))[0]


def _chip_sum(ps, landed, idx, name):
    _, half, D = ps.shape
    tr = _pick(half, (416, 352, 128))
    steps = half // tr

    def body(idx_ref, p_ref, l_ref, o_ref):
        acc = p_ref[0].astype(F32)
        for j in range(3):
            acc = acc + l_ref[j].astype(F32)
        o_ref[...] = acc

    return pl.pallas_call(
        body, name=name,
        grid_spec=pltpu.PrefetchScalarGridSpec(
            num_scalar_prefetch=1, grid=(steps,),
            in_specs=[pl.BlockSpec((1, tr, D), lambda i, idx: (0, i, 0)),
                      pl.BlockSpec((3, tr, D), lambda i, idx: (0, i, 0))],
            out_specs=pl.BlockSpec((tr, D), lambda i, idx: (idx[0] * steps + i, 0))),
        out_shape=jax.ShapeDtypeStruct((2 * half, D), F32),
        compiler_params=_params(("arbitrary",)),
    )(idx, ps, landed)


def _pair_swap(shards, name):
    n = len(shards)

    def body(*refs):
        full = refs[n:2 * n]
        send_sems, recv_sems = refs[2 * n:]
        x, y, c = _place()

        def half(t, h):
            rows = shards[t].shape[0] // 2
            return full[t].at[pl.ds(h * rows, rows), :]

        def copy(t, h):
            return pltpu.make_async_remote_copy(src_ref=half(t, h), dst_ref=half(t, h), send_sem=send_sems.at[t],
                                                recv_sem=recv_sems.at[t], device_id=(x, y, 1 - c),
                                                device_id_type=MESH)

        for t in range(n):
            copy(t, c).start()
        for t in range(n):
            copy(t, 1 - c).wait_recv()
        for t in range(n):
            copy(t, c).wait_send()

    return pl.pallas_call(
        body, name=name,
        out_shape=[jax.ShapeDtypeStruct(a.shape, a.dtype) for a in shards],
        in_specs=_any_specs(n), out_specs=_any_specs(n),
        input_output_aliases={t: t for t in range(n)},
        scratch_shapes=[pltpu.SemaphoreType.DMA((n,)), pltpu.SemaphoreType.DMA((n,))],
    )(*shards)


def _gather_begin(fulls, tag):
    start, wait = _weights_gather_split(fulls)
    send_sems, recv_sems, bufs, token = start(len(fulls), f"ag_{tag}_start", fulls)
    return (wait, send_sems, recv_sems, bufs), token


def _gather_end(state, after, tag):
    wait, send_sems, recv_sems, bufs = state
    landed = wait(len(bufs), f"ag_{tag}_wait", send_sems, recv_sems, bufs, after)
    return _weights_pass_on(landed, f"ag_{tag}_pass_on")


class _Exchanges:
    def __init__(self, fulls_rest, idx):
        self.idx = idx
        self._rest, self.token = _gather_begin(fulls_rest, "rest")
        self._early = []

    def rest_weights(self, after):
        return _gather_end(self._rest, after, "rest")

    def _pair_sums(self, grads, tag):
        landed = _pair_exchange(grads, "rs_pair_exchange_" + tag)
        return [_pair_sum(g, l, self.idx, f"rs_pair_sum_{tag}{t}") for t, (g, l) in enumerate(zip(grads, landed))]

    def reduce_early(self, grads, tag):
        sums = self._pair_sums(grads, tag)
        zones = [lax.empty((3,) + s.shape[1:], s.dtype) for s in sums]
        start, wait = _chip_exchange_split(len(sums))
        send_sems, recv_sems, bufs, token = start(2 * len(sums), "rs_chip_start_" + tag, sums + zones)
        self._early.append((tag, wait, send_sems, recv_sems, bufs))
        return token

    def finish_early(self, after):
        halves = []
        for tag, wait, send_sems, recv_sems, bufs in self._early:
            n = len(bufs) // 2
            done = wait(len(bufs), "rs_chip_wait_" + tag, send_sems, recv_sems, bufs, after)
            halves += [_chip_sum(p, l, self.idx, f"rs_chip_sum_{tag}{t}")
                       for t, (p, l) in enumerate(zip(done[:n], done[n:]))]
        return halves


N_MOD = 9
PACK_HEAD, PACK_N3, PACK_N2, PACK_N1, PACK_CONV, PACK_QK = 0, 16, 32, 48, 64, 80
PACK_ROWS = 96
MOD_SRC = ((PACK_N1, 0), (PACK_N1, 1), (PACK_N2, 3), (PACK_N2, 0), (PACK_N2, 1),
           (PACK_N3, 3), (PACK_N3, 0), (PACK_N3, 1), (PACK_HEAD, 2))
CTX_ROW = 8


def _silu(v):
    return v * jax.nn.sigmoid(v)


def _whole(n):
    return [pl.BlockSpec(memory_space=pltpu.VMEM)] * n


def _mod_rows(cin, w_sh, b_sh, name):
    def body(c_ref, w_ref, b_ref, o_ref):
        a = _silu(c_ref[...]).astype(BF16)
        o_ref[...] = jnp.dot(a, w_ref[...].astype(BF16), preferred_element_type=F32) + b_ref[...]

    return pl.pallas_call(
        body, name=name, out_shape=jax.ShapeDtypeStruct((cin.shape[0], w_sh.shape[1]), F32),
        in_specs=_whole(3), out_specs=pl.BlockSpec(memory_space=pltpu.VMEM),
        compiler_params=pltpu.CompilerParams(vmem_limit_bytes=VMEM_LIMIT),
    )(cin, w_sh, b_sh)


def _small_reduce(gathered, name):
    _, _, D = gathered.shape

    def body(g_ref, loss_ref, db_ref, gn_ref, cv_ref, qk_ref, dm_ref):
        tot = g_ref[0]
        for r in range(1, N_DEV):
            tot = tot + g_ref[r]

        def both(block, row):
            return tot[block + row:block + row + 1, :] + tot[block + 8 + row:block + 8 + row + 1, :]

        loss = jnp.sum(both(PACK_HEAD, 0), axis=1, keepdims=True)
        loss_ref[...] = jnp.broadcast_to(loss, loss_ref.shape)
        db_ref[...] = jnp.zeros(db_ref.shape, F32)
        dm_ref[...] = jnp.zeros(dm_ref.shape, F32)
        for j, (block, row) in enumerate(MOD_SRC):
            db_ref[j:j + 1, :] = both(block, row)
            dm_ref[CTX_ROW, j:j + 1, :] = tot[block + row:block + row + 1, :]
            for r in range(N_DEV):
                dm_ref[r, j:j + 1, :] = g_ref[r, block + 8 + row:block + 8 + row + 1, :]
        gn_ref[...] = jnp.zeros(gn_ref.shape, F32)
        gn_ref[0:1, :] = both(PACK_N1, 2)
        gn_ref[8:9, :] = both(PACK_N2, 2)
        gn_ref[16:17, :] = both(PACK_N3, 2)
        gn_ref[24:25, :] = both(PACK_HEAD, 1)
        cv_ref[...] = jnp.zeros(cv_ref.shape, F32)
        for r in range(3):
            cv_ref[r:r + 1, :] = both(PACK_CONV, r)
        qk_ref[...] = jnp.zeros(qk_ref.shape, F32)
        qk_ref[0:1, 0:HEAD_DIM] = both(PACK_QK, 0)[:, 0:HEAD_DIM]
        qk_ref[0:1, HEAD_DIM:2 * HEAD_DIM] = both(PACK_QK, 1)[:, 0:HEAD_DIM]

    return pl.pallas_call(
        body, name=name,
        out_shape=[jax.ShapeDtypeStruct((8, 128), F32), jax.ShapeDtypeStruct((16, D), F32),
                   jax.ShapeDtypeStruct((32, D), F32), jax.ShapeDtypeStruct((8, D), F32),
                   jax.ShapeDtypeStruct((8, D), F32), jax.ShapeDtypeStruct((16, 16, D), F32)],
        in_specs=_whole(1), out_specs=_whole(6),
        compiler_params=pltpu.CompilerParams(vmem_limit_bytes=VMEM_LIMIT),
    )(gathered)


def _wmod_grad(cin, dm_sh, w_sh, name):
    def body(c_ref, d_ref, w_ref, gw_ref, cp_ref):
        a = _silu(c_ref[...]).astype(BF16)
        d = d_ref[...].astype(BF16)
        gw_ref[...] = lax.dot_general(a, d, (((0,), (0,)), ((), ())), preferred_element_type=F32)
        cp_ref[...] = lax.dot_general(d, w_ref[...].astype(BF16), (((1,), (1,)), ((), ())),
                                      preferred_element_type=F32)

    return pl.pallas_call(
        body, name=name,
        out_shape=[jax.ShapeDtypeStruct(w_sh.shape, F32), jax.ShapeDtypeStruct(cin.shape, F32)],
        in_specs=_whole(3), out_specs=_whole(2),
        compiler_params=pltpu.CompilerParams(vmem_limit_bytes=VMEM_LIMIT),
    )(cin, dm_sh, w_sh)


def _cctx_grad(parts, c_ctx8, name):
    def body(p_ref, c_ref, o_ref):
        tot = p_ref[0] + p_ref[2] + p_ref[4] + p_ref[6]
        cv = c_ref[...]
        sig = jax.nn.sigmoid(cv)
        rows = lax.broadcasted_iota(jnp.int32, tot.shape, 0)
        o_ref[...] = jnp.where(rows == 0, tot * (sig * (1.0 + cv * (1.0 - sig))), 0.0)

    return pl.pallas_call(
        body, name=name, out_shape=jax.ShapeDtypeStruct(c_ctx8.shape, F32),
        in_specs=_whole(2), out_specs=pl.BlockSpec(memory_space=pltpu.VMEM),
    )(parts, c_ctx8)


def _pad_rows(a, rows):
    return jnp.pad(a, ((0, rows - a.shape[0]), (0, 0)))


def _pack_small(c_ctx, b_mod, n1, n2, n3, final_g, gq, gk, conv_sh, D):
    misc = jnp.concatenate([gq, gk, conv_sh.reshape(1, -1)], axis=1)
    return jnp.concatenate([_pad_rows(c_ctx[None], 8), _pad_rows(b_mod.reshape(N_MOD, D), 16), _pad_rows(n1, 8),
                            _pad_rows(n2, 8), _pad_rows(n3, 8), _pad_rows(final_g[None], 8), _pad_rows(misc, 8)], axis=0)


def _unpack_small(p, D, conv_shape):
    misc = p[56:57]
    return dict(c_ctx=p[0], b_mod=p[8:8 + N_MOD].reshape(1, N_MOD * D), norm1_g=p[24:25], norm2_g=p[32:33],
                norm3_g=p[40:41], final_g=p[48], q_norm_g=misc[:, 0:HEAD_DIM], k_norm_g=misc[:, HEAD_DIM:2 * HEAD_DIM],
                conv_w=misc[:, 2 * HEAD_DIM:].reshape(conv_shape))


WEIGHT_ORDER = ("c_ctx", "w_mod", "b_mod", "norm1_g", "norm2_g", "norm3_g", "ffn1_w_in", "ffn1_w_out", "w_in",
                "conv_w", "q_norm_g", "k_norm_g", "w_branch_conv", "w_branch_attn", "w_out", "ffn2_w_in",
                "ffn2_w_out", "final_g")
BIG = ("ffn1_w_in", "ffn1_w_out", "w_in", "w_branch_conv", "w_branch_attn", "w_out", "ffn2_w_in", "ffn2_w_out")
COLUMN_SHARDED = ("ffn1_w_in", "w_in", "ffn2_w_in")


def kernel(x, c, ctx, c_ctx, w_mod, b_mod, norm1_g, norm2_g, norm3_g, ffn1_w_in, ffn1_w_out, w_in, conv_w, q_norm_g, k_norm_g, w_branch_conv, w_branch_attn, w_out, ffn2_w_in, ffn2_w_out, final_g, loss_target, m_c_ctx, m_w_mod, m_b_mod, m_norm1_g, m_norm2_g, m_norm3_g, m_ffn1_w_in, m_ffn1_w_out, m_w_in, m_conv_w, m_q_norm_g, m_k_norm_g, m_w_branch_conv, m_w_branch_attn, m_w_out, m_ffn2_w_in, m_ffn2_w_out, m_final_g, v_c_ctx, v_w_mod, v_b_mod, v_norm1_g, v_norm2_g, v_norm3_g, v_ffn1_w_in, v_ffn1_w_out, v_w_in, v_conv_w, v_q_norm_g, v_k_norm_g, v_w_branch_conv, v_w_branch_attn, v_w_out, v_ffn2_w_in, v_ffn2_w_out, v_final_g):
    w = dict(c_ctx=c_ctx, w_mod=w_mod, b_mod=b_mod, norm1_g=norm1_g, norm2_g=norm2_g, norm3_g=norm3_g,
             ffn1_w_in=ffn1_w_in, ffn1_w_out=ffn1_w_out, w_in=w_in, conv_w=conv_w, q_norm_g=q_norm_g,
             k_norm_g=k_norm_g, w_branch_conv=w_branch_conv, w_branch_attn=w_branch_attn, w_out=w_out,
             ffn2_w_in=ffn2_w_in, ffn2_w_out=ffn2_w_out, final_g=final_g)
    m = dict(c_ctx=m_c_ctx, w_mod=m_w_mod, b_mod=m_b_mod, norm1_g=m_norm1_g, norm2_g=m_norm2_g, norm3_g=m_norm3_g,
             ffn1_w_in=m_ffn1_w_in, ffn1_w_out=m_ffn1_w_out, w_in=m_w_in, conv_w=m_conv_w, q_norm_g=m_q_norm_g,
             k_norm_g=m_k_norm_g, w_branch_conv=m_w_branch_conv, w_branch_attn=m_w_branch_attn, w_out=m_w_out,
             ffn2_w_in=m_ffn2_w_in, ffn2_w_out=m_ffn2_w_out, final_g=m_final_g)
    v = dict(c_ctx=v_c_ctx, w_mod=v_w_mod, b_mod=v_b_mod, norm1_g=v_norm1_g, norm2_g=v_norm2_g, norm3_g=v_norm3_g,
             ffn1_w_in=v_ffn1_w_in, ffn1_w_out=v_ffn1_w_out, w_in=v_w_in, conv_w=v_conv_w, q_norm_g=v_q_norm_g,
             k_norm_g=v_k_norm_g, w_branch_conv=v_w_branch_conv, w_branch_attn=v_w_branch_attn, w_out=v_w_out,
             ffn2_w_in=v_ffn2_w_in, ffn2_w_out=v_ffn2_w_out, final_g=v_final_g)

    xi, yi, ci = _place()
    dev = 4 * xi + 2 * yi + ci
    shard = 2 * xi + yi
    idx = jnp.stack([ci, shard, 2 * (1 - xi) + yi, 2 * xi + (1 - yi), 2 * (1 - xi) + (1 - yi)]).astype(jnp.int32)
    D = x.shape[-1]
    ctx_len = ctx.shape[1]
    assert ctx_len == ROW and c.shape == (1, D)
    mcols = w_mod.shape[2]
    ccols = conv_w.shape[2]

    def place(names):
        return [_place_shard(w[n][0], idx, n in COLUMN_SHARDED, "place_" + n) for n in names]

    ffn1_gather, ffn1_token = _gather_begin(place(BIG[:2]), "ffn1")
    fulls_rest = place(BIG[2:])

    c_all = _allgather8(_after(jnp.broadcast_to(c, (8, D)), ffn1_token), "ag_c")[:, 0, :]
    cin = jnp.concatenate([c_all, _pad_rows(c_ctx[None], 8)], axis=0)
    b_sh = lax.dynamic_slice(b_mod, (0, shard * mcols), (1, mcols))
    mod_sh = _mod_rows(cin, w_mod[0], b_sh, "mod_rows")
    conv_rows = jnp.pad(conv_w[0], ((0, 8 - conv_w.shape[1]), (0, mcols - ccols)))
    mod_all = _allgather8(jnp.concatenate([mod_sh, conv_rows], axis=0), "ag_mod")
    mod_full = jnp.concatenate([mod_all[2 * s, :16] for s in range(N_CHIPS)], axis=1)
    conv_full = jnp.concatenate([mod_all[2 * s, 16:16 + conv_w.shape[1], :ccols] for s in range(N_CHIPS)], axis=1)
    mod_lat = lax.dynamic_slice(mod_full, (dev, 0), (1, N_MOD * D)).reshape(N_MOD, D)
    mod_ctx = mod_full[CTX_ROW].reshape(N_MOD, D)
    mods = jnp.stack([_pad_rows(mod_ctx, 16), _pad_rows(mod_lat, 16)])

    ffn1_w = _gather_end(ffn1_gather, mods, "ffn1")
    hooks = _Exchanges(fulls_rest, idx)

    xcat = _after(jnp.concatenate([ctx[0], x[0]], axis=0), hooks.token)
    grad_x, _, accs = _local_step(xcat, loss_target[0], mods, (norm1_g, norm2_g, norm3_g), final_g[None],
                                  q_norm_g, k_norm_g, conv_full, ffn1_w, hooks, ctx_len)
    g = {}

    pack = jnp.concatenate([a.reshape(2 * ACC_ROWS, D) for a in accs], axis=0)
    gathered = _allgather8(pack, "ag_small")
    loss8, db_mod, g_norms, g_conv, g_qk, dm = _small_reduce(gathered, "small_reduce")
    dm_sh = lax.dynamic_slice(dm[:, :N_MOD, :].reshape(16, N_MOD * D), (0, shard * mcols), (16, mcols))
    g_wmod, cpart = _wmod_grad(cin, dm_sh, w_mod[0], "wmod_grad")
    g["w_mod"] = g_wmod[None]
    cparts = _allgather8(cpart[CTX_ROW:CTX_ROW + 8], "ag_cctx")
    g_cctx = _cctx_grad(cparts, _pad_rows(c_ctx[None], 8), "cctx_grad")
    g_conv_sh = lax.dynamic_slice(g_conv, (0, shard * ccols), (conv_w.shape[1], ccols))
    g_misc = jnp.concatenate([g_qk[0:1, 0:2 * HEAD_DIM], g_conv_sh.reshape(1, -1)], axis=1)
    g_pack = jnp.concatenate([g_cctx, db_mod, g_norms, _pad_rows(g_misc, 8)], axis=0)

    def packed(p):
        return _pack_small(p["c_ctx"], p["b_mod"], p["norm1_g"], p["norm2_g"], p["norm3_g"], p["final_g"],
                           p["q_norm_g"], p["k_norm_g"], p["conv_w"][0], D)

    d_pack, m_pack, v_pack = _adamw(packed(w), g_pack, packed(m), packed(v), "adamw_small")

    h_wbc, h_wba, h_wo, h_w2i, h_w2o, h_wi, h_w1o, h_w1i = hooks.finish_early(d_pack)
    halves = [h_w1i, h_w1o, h_wi, h_wbc, h_wba, h_wo, h_w2i, h_w2o]
    reduced = dict(zip(BIG, _pair_swap(halves, "rs_pair_swap")))
    g.update(_unpack_small(g_pack, D, conv_w.shape))
    delta = _unpack_small(d_pack, D, conv_w.shape)
    new_m = _unpack_small(m_pack, D, conv_w.shape)
    new_v = _unpack_small(v_pack, D, conv_w.shape)
    for n in BIG + ("w_mod",):
        if n in COLUMN_SHARDED:
            g2, d2, m2, v2 = _adamw_transposed(w[n][0], reduced[n], m[n][0], v[n][0], "adamw_" + n)
        else:
            g2 = reduced[n] if n in reduced else g[n][0]
            d2, m2, v2 = _adamw(w[n][0], g2, m[n][0], v[n][0], "adamw_" + n)
        g[n], delta[n], new_m[n], new_v[n] = g2[None], d2[None], m2[None], v2[None]

    loss = loss8[0, 0]
    return (loss, grad_x[None], *[g[n] for n in WEIGHT_ORDER], *[delta[n] for n in WEIGHT_ORDER],
            *[new_m[n] for n in WEIGHT_ORDER], *[new_v[n] for n in WEIGHT_ORDER])
```

```python
import functools

import jax
import jax.numpy as jnp
from jax import lax
from jax.experimental import pallas as pl
from jax.experimental.pallas import tpu as pltpu

F32 = jnp.float32
BF16 = jnp.bfloat16

HEAD_DIM = 128
N_Q_HEADS = 8
N_KV_HEADS = 2
GROUP = N_Q_HEADS // N_KV_HEADS
GRID_W = 64
ROPE_THETA = 10000.0
EPS = 1e-6
ATTN_SCALE = HEAD_DIM ** -0.5

ADAM_LR = 0.001
ADAM_B1 = 0.9
ADAM_B2 = 0.999
ADAM_EPS = 1e-08
ADAM_WD = 0.01
ADAM_STEP = 10

ROW = 256
HALO = 16
ACC_ROWS = 8
N_CHIPS = 4
N_DEV = 8
MESH = pl.DeviceIdType.MESH
VMEM_LIMIT = 48 * 1024 * 1024
ADAMW_BLOCK_BYTES = 1024 * 1024


def _pick(n, prefs):
    for p in prefs:
        if n % p == 0:
            return p
    return n


def _params(sem):
    return pltpu.CompilerParams(dimension_semantics=sem, vmem_limit_bytes=VMEM_LIMIT)


def _stream(i):
    return jnp.minimum(i, 1)


def _matmul(a, b, mode, out_dtype, name, tm=None, tn=None, tk=None, token=None):
    if mode == "nn":
        (M, K), (K2, N) = a.shape, b.shape
    elif mode == "nt":
        (M, K), (N, K2) = a.shape, b.shape
    else:
        (K, M), (K2, N) = a.shape, b.shape
    assert K == K2, (a.shape, b.shape, mode)
    tm = tm or _pick(M, (1664, 1408, 1024, 512, 256, 128) if mode == "tn" else (1408, 768, 512, 256, 128))
    tn = tn or _pick(N, (1664, 1408, 1024, 512, 256, 128))
    tk = tk or _pick(K, (1664, 1408, 1024, 768, 512, 256, 128))
    nk = K // tk
    if mode == "tn":
        a_spec = pl.BlockSpec((tk, tm), lambda i, j, k: (k, i))
    else:
        a_spec = pl.BlockSpec((tm, tk), lambda i, j, k: (i, k))
    if mode == "nt":
        b_spec = pl.BlockSpec((tn, tk), lambda i, j, k: (j, k))
    else:
        b_spec = pl.BlockSpec((tk, tn), lambda i, j, k: (k, j))
    dims = {"nn": ((1,), (0,)), "nt": ((1,), (1,)), "tn": ((0,), (0,))}[mode]
    use_scratch = nk > 1 and out_dtype != F32

    extra = [] if token is None else [token]

    def body(a_ref, b_ref, *rest):
        o_ref, scratch = rest[len(extra)], rest[len(extra) + 1:]
        p = lax.dot_general(a_ref[...].astype(BF16), b_ref[...].astype(BF16), (dims, ((), ())),
                            preferred_element_type=F32)
        if nk == 1:
            o_ref[...] = p.astype(o_ref.dtype)
            return
        acc_ref = scratch[0] if use_scratch else o_ref
        k = pl.program_id(2)

        @pl.when(k == 0)
        def _():
            acc_ref[...] = p

        @pl.when(k > 0)
        def _():
            acc_ref[...] += p

        if use_scratch:
            @pl.when(k == nk - 1)
            def _():
                o_ref[...] = acc_ref[...].astype(o_ref.dtype)

    return pl.pallas_call(
        body, name=name,
        grid=(M // tm, N // tn, nk),
        in_specs=[a_spec, b_spec] + [pl.BlockSpec(t.shape, lambda i, j, k: (0, 0)) for t in extra],
        out_specs=pl.BlockSpec((tm, tn), lambda i, j, k: (i, j)),
        out_shape=jax.ShapeDtypeStruct((M, N), out_dtype),
        scratch_shapes=[pltpu.VMEM((tm, tn), F32)] if use_scratch else [],
        compiler_params=_params(("parallel", "parallel", "arbitrary")),
    )(a, b, *extra)


def _row_spec(width, col=0):
    return pl.BlockSpec((ROW, width), lambda i, col=col: (i, col))


def _mods_spec(D):
    return pl.BlockSpec((1, 16, D), lambda i: (_stream(i), 0, 0))


def _acc_spec(D):
    return pl.BlockSpec((1, ACC_ROWS, D), lambda i: (_stream(i), 0, 0))


def _vec_spec(rows, D):
    return pl.BlockSpec((rows, D), lambda i: (0, 0))


def _acc_init(acc_ref):
    i = pl.program_id(0)

    @pl.when(i <= 1)
    def _():
        acc_ref[...] = jnp.zeros_like(acc_ref)


def _acc_add(acc_ref, row, val):
    acc_ref[0, row:row + 1, :] += jnp.sum(val, axis=0, keepdims=True)


def _resid_rmsmod_fwd(xprev, branch, mods, g, gate, shift_idx, scale_idx, name):
    T, D = xprev.shape
    has_res = branch is not None

    def body(*refs):
        if has_res:
            x_ref, f_ref, m_ref, g_ref, xo_ref, h_ref = refs
        else:
            x_ref, m_ref, g_ref, h_ref = refs
        m = m_ref[0]
        x = x_ref[...]
        if has_res:
            gate_idx, fac = gate
            x = x + (fac * m[gate_idx:gate_idx + 1, :]) * f_ref[...]
            xo_ref[...] = x
        inv = lax.rsqrt(jnp.mean(x * x, axis=-1, keepdims=True) + EPS)
        y = (x * inv) * g_ref[...]
        h = y * (1.0 + m[scale_idx:scale_idx + 1, :]) + m[shift_idx:shift_idx + 1, :]
        h_ref[...] = h.astype(BF16)

    in_specs = [_row_spec(D)] + ([_row_spec(D)] if has_res else []) + [_mods_spec(D), _vec_spec(1, D)]
    args = [xprev] + ([branch] if has_res else []) + [mods, g]
    out_specs = ([_row_spec(D)] if has_res else []) + [_row_spec(D)]
    out_shape = ([jax.ShapeDtypeStruct((T, D), F32)] if has_res else []) + [jax.ShapeDtypeStruct((T, D), BF16)]
    out = pl.pallas_call(
        body, name=name, grid=(T // ROW,), in_specs=in_specs, out_specs=out_specs, out_shape=out_shape,
        compiler_params=_params(("parallel",)),
    )(*args)
    return out if has_res else (None, out[0])


def _loss_head(x2, f2, mods, final_g, target, name):
    T, D = x2.shape
    nt = T // ROW

    def body(x_ref, f_ref, m_ref, g_ref, t_ref, dx_ref, df_ref, acc_ref):
        _acc_init(acc_ref)
        i = pl.program_id(0)
        m = m_ref[0]
        gate = 0.5 * m[8:9, :]
        f = f_ref[...]
        x = x_ref[...] + gate * f
        inv = lax.rsqrt(jnp.mean(x * x, axis=-1, keepdims=True) + EPS)
        xn = x * inv
        fg = g_ref[...]
        lat = (i > 0).astype(F32)
        e = (xn * fg - t_ref[...]) * lat
        dy = e * (1.0 / D)
        dxn = dy * fg
        dx = inv * (dxn - xn * jnp.mean(dxn * xn, axis=-1, keepdims=True))
        dx_ref[...] = dx
        df_ref[...] = (gate * dx).astype(BF16)
        _acc_add(acc_ref, 0, (0.5 / D) * e * e)
        _acc_add(acc_ref, 1, dy * xn)
        _acc_add(acc_ref, 2, 0.5 * dx * f)

    return pl.pallas_call(
        body, name=name, grid=(nt,),
        in_specs=[_row_spec(D), _row_spec(D), _mods_spec(D), _vec_spec(1, D),
                  pl.BlockSpec((ROW, D), lambda i: (jnp.maximum(i - 1, 0), 0))],
        out_specs=[_row_spec(D), _row_spec(D), _acc_spec(D)],
        out_shape=[jax.ShapeDtypeStruct((T, D), F32), jax.ShapeDtypeStruct((T, D), BF16),
                   jax.ShapeDtypeStruct((2, ACC_ROWS, D), F32)],
        compiler_params=_params(("arbitrary",)),
    )(x2, f2, mods, final_g, target)


def _rmsmod_bwd(x, dh, dres, mods, g, shift_idx, scale_idx, gate, branch, name, skip_first_tile=False):
    T, D = x.shape
    nt = T // ROW
    has_gate = gate is not None

    def body(*refs):
        if has_gate:
            x_ref, dh_ref, dr_ref, b_ref, m_ref, g_ref, dx_ref, db_ref, acc_ref = refs
        else:
            x_ref, dh_ref, dr_ref, m_ref, g_ref, dx_ref, acc_ref = refs
        _acc_init(acc_ref)
        m = m_ref[0]
        x = x_ref[...]
        dh = dh_ref[...]
        gg = g_ref[...]
        inv = lax.rsqrt(jnp.mean(x * x, axis=-1, keepdims=True) + EPS)
        xn = x * inv
        y = xn * gg
        dy = dh * (1.0 + m[scale_idx:scale_idx + 1, :])
        dxn = dy * gg
        dx = inv * (dxn - xn * jnp.mean(dxn * xn, axis=-1, keepdims=True)) + dr_ref[...]
        dx_ref[...] = dx
        _acc_add(acc_ref, 0, dh)
        _acc_add(acc_ref, 1, dh * y)
        _acc_add(acc_ref, 2, dy * xn)
        if has_gate:
            gate_idx, fac = gate
            b = b_ref[...]
            db_ref[...] = ((fac * m[gate_idx:gate_idx + 1, :]) * dx).astype(BF16)
            _acc_add(acc_ref, 3, fac * dx * b)

    in_specs = [_row_spec(D), _row_spec(D), _row_spec(D)] + ([_row_spec(D)] if has_gate else []) + \
               [_mods_spec(D), _vec_spec(1, D)]
    args = [x, dh, dres] + ([branch] if has_gate else []) + [mods, g]
    if skip_first_tile:
        dx_spec = pl.BlockSpec((ROW, D), lambda i: (jnp.maximum(i - 1, 0), 0))
        dx_shape = jax.ShapeDtypeStruct((T - ROW, D), F32)
    else:
        dx_spec = _row_spec(D)
        dx_shape = jax.ShapeDtypeStruct((T, D), F32)
    out_specs = [dx_spec] + ([_row_spec(D)] if has_gate else []) + [_acc_spec(D)]
    out_shape = [dx_shape] + ([jax.ShapeDtypeStruct((T, D), BF16)] if has_gate else []) + \
                [jax.ShapeDtypeStruct((2, ACC_ROWS, D), F32)]
    out = pl.pallas_call(
        body, name=name, grid=(nt,), in_specs=in_specs, out_specs=out_specs, out_shape=out_shape,
        compiler_params=_params(("arbitrary",)),
    )(*args)
    if has_gate:
        return out
    return out[0], None, out[1]


FFN_ROWS = 384
_NT = (((1,), (1,)), ((), ()))


def _ffn_chunk(F):
    return _pick(F, (1408, 512, 256, 128))


def _resident():
    return pl.BlockSpec(memory_space=pltpu.VMEM)


def _ffn_fwd(h, w_in_t, w_out, name):
    T, D = h.shape
    F = w_out.shape[0]
    cw = _ffn_chunk(F)
    tm = _pick(T, (FFN_ROWS, ROW))

    def body(h_ref, wi_ref, wo_ref, u_ref, s_ref, f_ref):
        hv = h_ref[...]
        acc = jnp.zeros((tm, D), F32)
        for j in range(F // cw):
            a = lax.dot_general(hv, wi_ref[j * cw:(j + 1) * cw, :], _NT, preferred_element_type=F32)
            b = lax.dot_general(hv, wi_ref[F + j * cw:F + (j + 1) * cw, :], _NT, preferred_element_type=F32)
            s = ((a * jax.nn.sigmoid(a)) * b).astype(BF16)
            u_ref[:, j * cw:(j + 1) * cw] = a.astype(BF16)
            u_ref[:, F + j * cw:F + (j + 1) * cw] = b.astype(BF16)
            s_ref[:, j * cw:(j + 1) * cw] = s
            acc = acc + jnp.dot(s, wo_ref[j * cw:(j + 1) * cw, :], preferred_element_type=F32)
        f_ref[...] = acc

    row = lambda w: pl.BlockSpec((tm, w), lambda i: (i, 0))
    return pl.pallas_call(
        body, name=name, grid=(T // tm,),
        in_specs=[row(D), _resident(), _resident()],
        out_specs=[row(2 * F), row(F), row(D)],
        out_shape=[jax.ShapeDtypeStruct((T, 2 * F), BF16), jax.ShapeDtypeStruct((T, F), BF16),
                   jax.ShapeDtypeStruct((T, D), F32)],
        compiler_params=_params(("parallel",)),
    )(h, w_in_t, w_out)


def _ffn_bwd(df, u, w_in_t, w_out, name):
    T, D = df.shape
    F = w_out.shape[0]
    cw = _ffn_chunk(F)
    tm = _pick(T, (FFN_ROWS, ROW))

    def body(df_ref, u_ref, wi_ref, wo_ref, du_ref, dh_ref):
        dfv = df_ref[...]
        acc = jnp.zeros((tm, D), F32)
        for j in range(F // cw):
            ds = lax.dot_general(dfv, wo_ref[j * cw:(j + 1) * cw, :], _NT, preferred_element_type=F32)
            a = u_ref[:, j * cw:(j + 1) * cw].astype(F32)
            b = u_ref[:, F + j * cw:F + (j + 1) * cw].astype(F32)
            sig = jax.nn.sigmoid(a)
            da = (ds * b * (sig * (1.0 + a * (1.0 - sig)))).astype(BF16)
            db = (ds * (a * sig)).astype(BF16)
            du_ref[:, j * cw:(j + 1) * cw] = da
            du_ref[:, F + j * cw:F + (j + 1) * cw] = db
            acc = acc + jnp.dot(da, wi_ref[j * cw:(j + 1) * cw, :], preferred_element_type=F32)
            acc = acc + jnp.dot(db, wi_ref[F + j * cw:F + (j + 1) * cw, :], preferred_element_type=F32)
        dh_ref[...] = acc

    row = lambda w: pl.BlockSpec((tm, w), lambda i: (i, 0))
    return pl.pallas_call(
        body, name=name, grid=(T // tm,),
        in_specs=[row(D), row(2 * F), _resident(), _resident()],
        out_specs=[row(2 * F), row(D)],
        out_shape=[jax.ShapeDtypeStruct((T, 2 * F), BF16), jax.ShapeDtypeStruct((T, D), F32)],
        compiler_params=_params(("parallel",)),
    )(df, u, w_in_t, w_out)


def _halo_specs(width, col, nt):
    per = ROW // HALO
    prev = pl.BlockSpec((HALO, width), lambda i, col=col: (jnp.maximum(i * per - 1, 0), col))
    nxt = pl.BlockSpec((HALO, width), lambda i, col=col: (jnp.minimum((i + 1) * per, nt * per - 1), col))
    return prev, nxt


def _f32(ref):
    return ref[...].astype(F32)


def _last_row(halo_ref):
    return halo_ref[HALO - 1:HALO, :].astype(F32)


def _first_row(halo_ref):
    return halo_ref[0:1, :].astype(F32)


def _shift_rows(v, prev_row, next_row):
    rows = lax.broadcasted_iota(jnp.int32, v.shape, 0)
    down = jnp.where(rows == 0, prev_row, pltpu.roll(v, 1, 0))
    up = jnp.where(rows == v.shape[0] - 1, next_row, pltpu.roll(v, v.shape[0] - 1, 0))
    return down, up


def _conv_fwd(P, conv_w, D, name):
    T = P.shape[0]
    nt = T // ROW
    cg_p, cg_n = _halo_specs(D, 1, nt)
    vc_p, vc_n = _halo_specs(D, 2, nt)

    def body(bg_ref, cg_ref, vc_ref, cgp_ref, vcp_ref, cgn_ref, vcn_ref, w_ref, y_ref):
        i = pl.program_id(0)
        has_prev = (i != 1).astype(F32)
        has_next = (i != nt - 1).astype(F32)
        u = _f32(cg_ref) * _f32(vc_ref)
        up_row = _last_row(cgp_ref) * _last_row(vcp_ref) * has_prev
        un_row = _first_row(cgn_ref) * _first_row(vcn_ref) * has_next
        um1, up1 = _shift_rows(u, up_row, un_row)
        w = w_ref[...]
        conv = um1 * w[0:1, :] + u * w[1:2, :] + up1 * w[2:3, :]
        y_ref[...] = (_f32(bg_ref) * conv).astype(BF16)

    return pl.pallas_call(
        body, name=name, grid=(nt,),
        in_specs=[_row_spec(D, 0), _row_spec(D, 1), _row_spec(D, 2), cg_p, vc_p, cg_n, vc_n, _vec_spec(3, D)],
        out_specs=_row_spec(D),
        out_shape=jax.ShapeDtypeStruct((T, D), BF16),
        compiler_params=_params(("parallel",)),
    )(P, P, P, P, P, P, P, conv_w)


def _conv_bwd(P, dy, conv_w, D, name):
    T = P.shape[0]
    nt = T // ROW
    bg_p, bg_n = _halo_specs(D, 0, nt)
    cg_p, cg_n = _halo_specs(D, 1, nt)
    vc_p, vc_n = _halo_specs(D, 2, nt)
    dy_p, dy_n = _halo_specs(D, 0, nt)

    def body(bg_ref, cg_ref, vc_ref, dy_ref, bgp_ref, cgp_ref, vcp_ref, dyp_ref,
             bgn_ref, cgn_ref, vcn_ref, dyn_ref, w_ref, o_ref, acc_ref):
        _acc_init(acc_ref)
        i = pl.program_id(0)
        lat = (i > 0).astype(F32)
        has_prev = (i != 1).astype(F32)
        has_next = (i != nt - 1).astype(F32)
        bg = _f32(bg_ref)
        cg = _f32(cg_ref)
        vc = _f32(vc_ref)
        dyv = dy_ref[...] * lat
        u = cg * vc
        up_row = _last_row(cgp_ref) * _last_row(vcp_ref) * has_prev
        un_row = _first_row(cgn_ref) * _first_row(vcn_ref) * has_next
        um1, up1 = _shift_rows(u, up_row, un_row)
        w = w_ref[...]
        conv = um1 * w[0:1, :] + u * w[1:2, :] + up1 * w[2:3, :]
        dc = dyv * bg
        dcp_row = _last_row(dyp_ref) * _last_row(bgp_ref) * has_prev
        dcn_row = _first_row(dyn_ref) * _first_row(bgn_ref) * has_next
        dcm1, dcp1 = _shift_rows(dc, dcp_row, dcn_row)
        du = dcp1 * w[0:1, :] + dc * w[1:2, :] + dcm1 * w[2:3, :]
        o_ref[:, 0:D] = (dyv * conv).astype(BF16)
        o_ref[:, D:2 * D] = (du * vc * lat).astype(BF16)
        o_ref[:, 2 * D:3 * D] = (du * cg * lat).astype(BF16)
        _acc_add(acc_ref, 0, dc * um1)
        _acc_add(acc_ref, 1, dc * u)
        _acc_add(acc_ref, 2, dc * up1)

    return pl.pallas_call(
        body, name=name, grid=(nt,),
        in_specs=[_row_spec(D, 0), _row_spec(D, 1), _row_spec(D, 2), _row_spec(D, 0),
                  bg_p, cg_p, vc_p, dy_p, bg_n, cg_n, vc_n, dy_n, _vec_spec(3, D)],
        out_specs=[_row_spec(3 * D), _acc_spec(D)],
        out_shape=[jax.ShapeDtypeStruct((T, 3 * D), BF16), jax.ShapeDtypeStruct((2, ACC_ROWS, D), F32)],
        compiler_params=_params(("arbitrary",)),
    )(P, P, P, dy, P, P, P, dy, P, P, P, dy, conv_w)


def _rope_tables(ctx_len, seq):
    n_freq = HEAD_DIM // 4
    rows = seq // GRID_W
    inv = ROPE_THETA ** (-jnp.arange(n_freq, dtype=F32) / n_freq)
    ar = jnp.arange(rows, dtype=F32)[:, None] * inv
    ac = jnp.arange(GRID_W, dtype=F32)[:, None] * inv

    def per_row(a):
        return jnp.repeat(a, GRID_W, axis=0)

    def per_col(a):
        return jnp.tile(a, (rows, 1))

    cos_t = jnp.concatenate([per_row(jnp.cos(ar)), per_row(jnp.cos(ar)), per_col(jnp.cos(ac)), per_col(jnp.cos(ac))], axis=1)
    sin_t = jnp.concatenate([per_row(-jnp.sin(ar)), per_row(jnp.sin(ar)), per_col(-jnp.sin(ac)), per_col(jnp.sin(ac))], axis=1)
    cos_t = jnp.concatenate([jnp.ones((ctx_len, HEAD_DIM), F32), cos_t], axis=0)
    sin_t = jnp.concatenate([jnp.zeros((ctx_len, HEAD_DIM), F32), sin_t], axis=0)
    return cos_t, sin_t


def _swap_halves(y):
    lanes = lax.broadcasted_iota(jnp.int32, y.shape, 1)
    first = (lanes % 64) < 32
    return jnp.where(first, pltpu.roll(y, HEAD_DIM - 32, 1), pltpu.roll(y, 32, 1))


def _qk_fwd(P, gq, gk, cos_t, sin_t, D, name):
    T = P.shape[0]
    QW = N_Q_HEADS * HEAD_DIM
    KW = N_KV_HEADS * HEAD_DIM
    q_col = (3 * D) // QW
    k_col = (3 * D + QW) // KW
    v_col = k_col + 1

    def body(q_ref, k_ref, v_ref, gq_ref, gk_ref, c_ref, s_ref, qo_ref, ko_ref, vo_ref):
        c = c_ref[...]
        s = s_ref[...]

        def head(x, g):
            inv = lax.rsqrt(jnp.mean(x * x, axis=-1, keepdims=True) + EPS)
            y = (x * inv) * g
            return y * c + _swap_halves(y) * s

        for h in range(N_Q_HEADS):
            sl = slice(h * HEAD_DIM, (h + 1) * HEAD_DIM)
            qo_ref[:, sl] = head(q_ref[:, sl].astype(F32), gq_ref[...]).astype(BF16)
        for h in range(N_KV_HEADS):
            sl = slice(h * HEAD_DIM, (h + 1) * HEAD_DIM)
            ko_ref[:, sl] = head(k_ref[:, sl].astype(F32), gk_ref[...]).astype(BF16)
        vo_ref[...] = v_ref[...].astype(BF16)

    return pl.pallas_call(
        body, name=name, grid=(T // ROW,),
        in_specs=[_row_spec(QW, q_col), _row_spec(KW, k_col), _row_spec(KW, v_col),
                  _vec_spec(1, HEAD_DIM), _vec_spec(1, HEAD_DIM), _row_spec(HEAD_DIM), _row_spec(HEAD_DIM)],
        out_specs=[_row_spec(QW), _row_spec(KW), _row_spec(KW)],
        out_shape=[jax.ShapeDtypeStruct((T, QW), BF16), jax.ShapeDtypeStruct((T, KW), BF16),
                   jax.ShapeDtypeStruct((T, KW), BF16)],
        compiler_params=_params(("parallel",)),
    )(P, P, P, gq, gk, cos_t, sin_t)


def _qk_bwd(P, dq, dk, dv, gq, gk, cos_t, sin_t, D, name):
    T = P.shape[0]
    QW = N_Q_HEADS * HEAD_DIM
    KW = N_KV_HEADS * HEAD_DIM
    q_col = (3 * D) // QW
    k_col = (3 * D + QW) // KW

    def body(q_ref, k_ref, dq_ref, dk_ref, dv_ref, gq_ref, gk_ref, c_ref, s_ref, o_ref, acc_ref):
        _acc_init(acc_ref)
        c = c_ref[...]
        s = s_ref[...]

        def head(x, d, g):
            dyv = d * c + _swap_halves(d * s)
            inv = lax.rsqrt(jnp.mean(x * x, axis=-1, keepdims=True) + EPS)
            xn = x * inv
            dxn = dyv * g
            dx = inv * (dxn - xn * jnp.mean(dxn * xn, axis=-1, keepdims=True))
            return dx, jnp.sum(dyv * xn, axis=0, keepdims=True)

        dgq = jnp.zeros((1, HEAD_DIM), F32)
        for h in range(N_Q_HEADS):
            sl = slice(h * HEAD_DIM, (h + 1) * HEAD_DIM)
            dx, dg = head(q_ref[:, sl].astype(F32), dq_ref[:, sl], gq_ref[...])
            o_ref[:, sl] = dx.astype(BF16)
            dgq = dgq + dg
        dgk = jnp.zeros((1, HEAD_DIM), F32)
        for h in range(N_KV_HEADS):
            sl = slice(h * HEAD_DIM, (h + 1) * HEAD_DIM)
            dx, dg = head(k_ref[:, sl].astype(F32), dk_ref[:, sl], gk_ref[...])
            o_ref[:, QW + h * HEAD_DIM:QW + (h + 1) * HEAD_DIM] = dx.astype(BF16)
            dgk = dgk + dg
        o_ref[:, QW + KW:QW + 2 * KW] = dv_ref[...].astype(BF16)
        acc_ref[0, 0:1, 0:HEAD_DIM] += dgq
        acc_ref[0, 1:2, 0:HEAD_DIM] += dgk

    return pl.pallas_call(
        body, name=name, grid=(T // ROW,),
        in_specs=[_row_spec(QW, q_col), _row_spec(KW, k_col), _row_spec(QW), _row_spec(KW), _row_spec(KW),
                  _vec_spec(1, HEAD_DIM), _vec_spec(1, HEAD_DIM), _row_spec(HEAD_DIM), _row_spec(HEAD_DIM)],
        out_specs=[_row_spec(QW + 2 * KW), _acc_spec(D)],
        out_shape=[jax.ShapeDtypeStruct((T, QW + 2 * KW), BF16), jax.ShapeDtypeStruct((2, ACC_ROWS, D), F32)],
        compiler_params=_params(("arbitrary",)),
    )(P, P, dq, dk, dv, gq, gk, cos_t, sin_t)


def _to_row(col, n):
    return jnp.transpose(jnp.broadcast_to(col, (n, HEAD_DIM)))[0:1, :]


LOG2E = 1.4426950408889634
ATTN_PARTS = 4


def _flash_fwd(q, k, v, name, tq=ROW, tk=None):
    T = q.shape[0]
    tk = tk or _pick(T, (1408, 768, 512, 256))
    ck = tk
    nk = T // tk
    GW = GROUP * HEAD_DIM

    def body(q_ref, k_ref, v_ref, o_ref, lse_ref, qs_ref, m_ref, l_ref, acc_ref, st_ref):
        ki = pl.program_id(2)

        @pl.when(ki == 0)
        def _():
            for g in range(GROUP):
                qs_ref[g * tq:(g + 1) * tq, :] = q_ref[:, g * HEAD_DIM:(g + 1) * HEAD_DIM]
            m_ref[...] = jnp.full(m_ref.shape, -jnp.inf, F32)
            l_ref[...] = jnp.zeros(l_ref.shape, F32)
            acc_ref[...] = jnp.zeros(acc_ref.shape, F32)

        w = GROUP * tq // ATTN_PARTS
        nck = tk // ck

        def lanes(p):
            return slice(p * w, (p + 1) * w)

        def keys(c):
            return slice(c * ck, (c + 1) * ck)

        def fold(a):
            return a.reshape(ck // 8, 8, w)

        def scores(p, c):
            st = lax.dot_general(k_ref[keys(c), :], qs_ref[lanes(p), :], _NT,
                                 preferred_element_type=F32) * (ATTN_SCALE * LOG2E)
            st_ref[keys(c), lanes(p)] = st
            return jnp.max(fold(st), axis=0)

        def new_max(p, partial):
            m_prev = m_ref[:, lanes(p)]
            m_new = jnp.maximum(m_prev, jnp.max(functools.reduce(jnp.maximum, partial), axis=0, keepdims=True))
            m_ref[:, lanes(p)] = m_new
            return m_new, jnp.exp2(m_prev - m_new)

        def weights(p, c, m_new):
            pt = jnp.exp2(st_ref[keys(c), lanes(p)] - m_new)
            pv = lax.dot_general(v_ref[keys(c), :], pt.astype(BF16), (((0,), (0,)), ((), ())),
                                 preferred_element_type=F32)
            return jnp.sum(fold(pt), axis=0), pv

        partial = [scores(0, c) for c in range(nck)]
        for p in range(ATTN_PARTS):
            m_new, alpha = new_max(p, partial)
            partial, sums, pvs = [], [], []
            for c in range(nck):
                if p + 1 < ATTN_PARTS:
                    partial.append(scores(p + 1, c))
                s8, pv = weights(p, c, m_new)
                sums.append(s8)
                pvs.append(pv)
            l_ref[:, lanes(p)] = alpha * l_ref[:, lanes(p)] + jnp.sum(sum(sums), axis=0, keepdims=True)
            acc_ref[:, lanes(p)] = alpha * acc_ref[:, lanes(p)] + sum(pvs)

        @pl.when(ki == nk - 1)
        def _():
            out = jnp.transpose(acc_ref[...] / l_ref[...])
            lse = m_ref[...] + jnp.log2(l_ref[...])
            for g in range(GROUP):
                o_ref[:, g * HEAD_DIM:(g + 1) * HEAD_DIM] = out[g * tq:(g + 1) * tq, :]
                lse_ref[0, g:g + 1, :] = lse[:, g * tq:(g + 1) * tq]

    return pl.pallas_call(
        body, name=name, grid=(N_KV_HEADS, T // tq, nk),
        in_specs=[pl.BlockSpec((tq, GW), lambda h, i, j: (i, h)),
                  pl.BlockSpec((tk, HEAD_DIM), lambda h, i, j: (j, h)),
                  pl.BlockSpec((tk, HEAD_DIM), lambda h, i, j: (j, h))],
        out_specs=[pl.BlockSpec((tq, GW), lambda h, i, j: (i, h)),
                   pl.BlockSpec((1, GROUP, tq), lambda h, i, j: (h, 0, i))],
        out_shape=[jax.ShapeDtypeStruct((T, N_Q_HEADS * HEAD_DIM), F32),
                   jax.ShapeDtypeStruct((N_KV_HEADS, GROUP, T), F32)],
        scratch_shapes=[pltpu.VMEM((GROUP * tq, HEAD_DIM), BF16), pltpu.VMEM((1, GROUP * tq), F32),
                        pltpu.VMEM((1, GROUP * tq), F32), pltpu.VMEM((HEAD_DIM, GROUP * tq), F32),
                        pltpu.VMEM((tk, GROUP * tq), F32)],
        compiler_params=_params(("parallel", "parallel", "arbitrary")),
    )(q, k, v)


def _attn_delta(do, o, name, token=None):
    T, QW = do.shape
    extra = [] if token is None else [token]

    def body(do_ref, o_ref, *rest):
        dob_ref, dl_ref = rest[len(extra):]
        dov = do_ref[...]
        dob_ref[...] = dov.astype(BF16)
        prod = dov * o_ref[...]
        for h in range(N_Q_HEADS):
            d = jnp.sum(prod[:, h * HEAD_DIM:(h + 1) * HEAD_DIM], axis=1, keepdims=True)
            dl_ref[h // GROUP, (h % GROUP):(h % GROUP) + 1, :] = _to_row(d, ROW)

    return pl.pallas_call(
        body, name=name, grid=(T // ROW,),
        in_specs=[_row_spec(QW), _row_spec(QW)] + [pl.BlockSpec(t.shape, lambda i: (0, 0)) for t in extra],
        out_specs=[_row_spec(QW), pl.BlockSpec((N_KV_HEADS, GROUP, ROW), lambda i: (0, 0, i))],
        out_shape=[jax.ShapeDtypeStruct((T, QW), BF16), jax.ShapeDtypeStruct((N_KV_HEADS, GROUP, T), F32)],
        compiler_params=_params(("parallel",)),
    )(do, o, *extra)


def _flash_bwd(q, k, v, do, lse, delta, name, tq=ROW, tk=None):
    T = q.shape[0]
    tk = tk or _pick(T, (1408, 768, 512, 256))
    nk = T // tk
    GW = GROUP * HEAD_DIM
    nt = (((1,), (1,)), ((), ()))

    def body(q_ref, do_ref, k_ref, v_ref, lse_ref, dl_ref, dq_ref, dk_ref, dv_ref, qs_ref, dos_ref, dqt_ref):
        qi = pl.program_id(1)
        ki = pl.program_id(2)

        @pl.when(ki == 0)
        def _():
            for g in range(GROUP):
                qs_ref[g * tq:(g + 1) * tq, :] = q_ref[:, g * HEAD_DIM:(g + 1) * HEAD_DIM]
                dos_ref[g * tq:(g + 1) * tq, :] = do_ref[:, g * HEAD_DIM:(g + 1) * HEAD_DIM]
            dqt_ref[...] = jnp.zeros(dqt_ref.shape, F32)

        kk = k_ref[...]
        vv = v_ref[...]

        def lanes(p):
            return slice(p * tq, (p + 1) * tq)

        def products(p):
            st = lax.dot_general(kk, qs_ref[lanes(p), :], nt, preferred_element_type=F32)
            dpt = lax.dot_general(vv, dos_ref[lanes(p), :], nt, preferred_element_type=F32)
            return st, dpt

        dk_c = jnp.zeros((tk, HEAD_DIM), F32)
        dv_c = jnp.zeros((tk, HEAD_DIM), F32)
        ahead = products(0)
        for p in range(GROUP):
            st, dpt = ahead
            if p + 1 < GROUP:
                ahead = products(p + 1)
            pt = jnp.exp2(st * (ATTN_SCALE * LOG2E) - lse_ref[0, p:p + 1, :])
            dst = ((pt * (dpt - dl_ref[0, p:p + 1, :])) * ATTN_SCALE).astype(BF16)
            dv_c = dv_c + jnp.dot(pt.astype(BF16), dos_ref[lanes(p), :], preferred_element_type=F32)
            dk_c = dk_c + jnp.dot(dst, qs_ref[lanes(p), :], preferred_element_type=F32)
            dqt_ref[:, lanes(p)] += lax.dot_general(kk, dst, (((0,), (0,)), ((), ())), preferred_element_type=F32)
        rows = pl.ds(pl.multiple_of(ki * tk, tk), tk)

        @pl.when(qi == 0)
        def _():
            dk_ref[rows, :] = dk_c
            dv_ref[rows, :] = dv_c

        @pl.when(qi > 0)
        def _():
            dk_ref[rows, :] += dk_c
            dv_ref[rows, :] += dv_c

        @pl.when(ki == nk - 1)
        def _():
            dqv = jnp.transpose(dqt_ref[...])
            for g in range(GROUP):
                dq_ref[:, g * HEAD_DIM:(g + 1) * HEAD_DIM] = dqv[g * tq:(g + 1) * tq, :]

    return pl.pallas_call(
        body, name=name, grid=(N_KV_HEADS, T // tq, nk),
        in_specs=[pl.BlockSpec((tq, GW), lambda h, i, j: (i, h)),
                  pl.BlockSpec((tq, GW), lambda h, i, j: (i, h)),
                  pl.BlockSpec((tk, HEAD_DIM), lambda h, i, j: (j, h)),
                  pl.BlockSpec((tk, HEAD_DIM), lambda h, i, j: (j, h)),
                  pl.BlockSpec((1, GROUP, tq), lambda h, i, j: (h, 0, i)),
                  pl.BlockSpec((1, GROUP, tq), lambda h, i, j: (h, 0, i))],
        out_specs=[pl.BlockSpec((tq, GW), lambda h, i, j: (i, h)),
                   pl.BlockSpec((T, HEAD_DIM), lambda h, i, j: (0, h)),
                   pl.BlockSpec((T, HEAD_DIM), lambda h, i, j: (0, h))],
        out_shape=[jax.ShapeDtypeStruct((T, N_Q_HEADS * HEAD_DIM), F32),
                   jax.ShapeDtypeStruct((T, N_KV_HEADS * HEAD_DIM), F32),
                   jax.ShapeDtypeStruct((T, N_KV_HEADS * HEAD_DIM), F32)],
        scratch_shapes=[pltpu.VMEM((GROUP * tq, HEAD_DIM), BF16), pltpu.VMEM((GROUP * tq, HEAD_DIM), BF16),
                        pltpu.VMEM((HEAD_DIM, GROUP * tq), F32)],
        compiler_params=_params(("arbitrary", "arbitrary", "arbitrary")),
    )(q, do, k, v, lse, delta)


def _gate_specs(D):
    w = D // 2
    first = (3 * D + (N_Q_HEADS + 2 * N_KV_HEADS) * HEAD_DIM) // w
    return [pl.BlockSpec((ROW, w), lambda i, c=first + j: (i, c)) for j in range(4)]


def _merge_fwd(a1, a2, P, D, name):
    T = a1.shape[0]
    w = D // 2

    def body(a1_ref, a2_ref, g0, g1, g2, g3, z_ref):
        for j, (gc, ga) in enumerate(((g0, g2), (g1, g3))):
            sl = slice(j * w, (j + 1) * w)
            z = jax.nn.sigmoid(_f32(gc)) * a1_ref[:, sl] + jax.nn.sigmoid(_f32(ga)) * a2_ref[:, sl]
            z_ref[:, sl] = z.astype(BF16)

    return pl.pallas_call(
        body, name=name, grid=(T // ROW,),
        in_specs=[_row_spec(D), _row_spec(D)] + _gate_specs(D),
        out_specs=_row_spec(D), out_shape=jax.ShapeDtypeStruct((T, D), BF16),
        compiler_params=_params(("parallel",)),
    )(a1, a2, P, P, P, P)


def _merge_bwd(dz, a1, a2, P, D, name):
    T = a1.shape[0]
    w = D // 2

    def body(dz_ref, a1_ref, a2_ref, g0, g1, g2, g3, d1_ref, d2_ref, dg_ref):
        for j, (gc, ga) in enumerate(((g0, g2), (g1, g3))):
            sl = slice(j * w, (j + 1) * w)
            dz = dz_ref[:, sl]
            sc = jax.nn.sigmoid(_f32(gc))
            sa = jax.nn.sigmoid(_f32(ga))
            d1_ref[:, sl] = (dz * sc).astype(BF16)
            d2_ref[:, sl] = (dz * sa).astype(BF16)
            dg_ref[:, j * w:(j + 1) * w] = (dz * a1_ref[:, sl] * (sc * (1.0 - sc))).astype(BF16)
            dg_ref[:, D + j * w:D + (j + 1) * w] = (dz * a2_ref[:, sl] * (sa * (1.0 - sa))).astype(BF16)

    return pl.pallas_call(
        body, name=name, grid=(T // ROW,),
        in_specs=[_row_spec(D), _row_spec(D), _row_spec(D)] + _gate_specs(D),
        out_specs=[_row_spec(D), _row_spec(D), _row_spec(2 * D)],
        out_shape=[jax.ShapeDtypeStruct((T, D), BF16), jax.ShapeDtypeStruct((T, D), BF16),
                   jax.ShapeDtypeStruct((T, 2 * D), BF16)],
        compiler_params=_params(("parallel",)),
    )(dz, a1, a2, P, P, P, P)


def _adamw_math(w, g, m, v):
    m = ADAM_B1 * m + (1.0 - ADAM_B1) * g
    v = ADAM_B2 * v + (1.0 - ADAM_B2) * (g * g)
    m_hat = m / (1.0 - ADAM_B1 ** ADAM_STEP)
    v_hat = v / (1.0 - ADAM_B2 ** ADAM_STEP)
    delta = -ADAM_LR * (m_hat / (jnp.sqrt(v_hat) + ADAM_EPS) + ADAM_WD * w)
    return delta, m, v


def _adamw(w, g, m, v, name):
    R, C = w.shape
    tr = _pick(R, tuple(t for t in (256, 128, 64, 32, 16, 8) if t * C * 4 <= ADAMW_BLOCK_BYTES))

    def body(w_ref, g_ref, m_ref, v_ref, d_ref, mo_ref, vo_ref):
        d, mn, vn = _adamw_math(w_ref[...], g_ref[...], m_ref[...], v_ref[...])
        d_ref[...] = d
        mo_ref[...] = mn
        vo_ref[...] = vn

    spec = pl.BlockSpec((tr, C), lambda i: (i, 0))
    return pl.pallas_call(
        body, name=name, grid=(R // tr,),
        in_specs=[spec] * 4, out_specs=[spec] * 3,
        out_shape=[jax.ShapeDtypeStruct((R, C), F32)] * 3,
        compiler_params=_params(("parallel",)),
    )(w, g, m, v)


def _concat_k_matmul(parts, b, name):
    T = parts[0].shape[0]
    N = b.shape[1]
    tm = _pick(T, (768, 512, 256))
    offs = [0]
    for p in parts:
        offs.append(offs[-1] + p.shape[1])
    assert offs[-1] == b.shape[0]

    def body(*refs):
        b_ref, o_ref = refs[len(parts)], refs[len(parts) + 1]
        acc = None
        for i, a_ref in enumerate(refs[:len(parts)]):
            d = jnp.dot(a_ref[...], b_ref[offs[i]:offs[i + 1], :], preferred_element_type=F32)
            acc = d if acc is None else acc + d
        o_ref[...] = acc

    return pl.pallas_call(
        body, name=name, grid=(T // tm,),
        in_specs=[pl.BlockSpec((tm, p.shape[1]), lambda i: (i, 0)) for p in parts] + [_resident()],
        out_specs=pl.BlockSpec((tm, N), lambda i: (i, 0)),
        out_shape=jax.ShapeDtypeStruct((T, N), F32),
        compiler_params=_params(("parallel",)),
    )(*parts, b)


def _adamw_transposed(w, gt, m, v, name):
    R, C = w.shape
    tc = 128

    def body(w_ref, g_ref, m_ref, v_ref, go_ref, d_ref, mo_ref, vo_ref):
        g = jnp.transpose(g_ref[...])
        d, mn, vn = _adamw_math(w_ref[...], g, m_ref[...], v_ref[...])
        go_ref[...] = g
        d_ref[...] = d
        mo_ref[...] = mn
        vo_ref[...] = vn

    spec = pl.BlockSpec((R, tc), lambda j: (0, j))
    return pl.pallas_call(
        body, name=name, grid=(C // tc,),
        in_specs=[spec, pl.BlockSpec((tc, R), lambda j: (j, 0)), spec, spec], out_specs=[spec] * 4,
        out_shape=[jax.ShapeDtypeStruct((R, C), F32)] * 4,
        compiler_params=_params(("parallel",)),
    )(w, gt, m, v)


class _NoExchange:
    def __init__(self, rest):
        self.rest = rest

    def rest_weights(self, after):
        return self.rest

    def reduce_early(self, grads, tag):
        return None


def _local_step(xcat, target, mods, norm_g, final_g, gq, gk, conv_w, ffn1_w, hooks, ctx_len):
    T, D = xcat.shape
    w1i, w1o = ffn1_w
    g1, g2, g3 = norm_g
    cos_t, sin_t = _rope_tables(ctx_len, T - ctx_len)

    def after(value, token, name):
        return value if token is None else _after(value, token, name)

    _, h1 = _resid_rmsmod_fwd(xcat, None, mods, g1, None, 0, 1, "f_norm1")
    u1, s1, f1 = _ffn_fwd(h1, w1i, w1o, "f_ffn1")
    wi, wbc, wba, wo, w2i, w2o = hooks.rest_weights(f1)
    x1, h2 = _resid_rmsmod_fwd(xcat, f1, mods, g2, (2, 0.5), 3, 4, "f_norm2")
    P = _matmul(h2, wi, "nt", BF16, "f_mix_in")
    yc = _conv_fwd(P, conv_w, D, "f_conv")
    qn, kn, vb = _qk_fwd(P, gq, gk, cos_t, sin_t, D, "f_qk")
    o, lse = _flash_fwd(qn, kn, vb, "f_attn")
    a1 = _matmul(yc, wbc, "nn", F32, "f_branch_conv")
    a2 = _matmul(o, wba, "nn", F32, "f_branch_attn")
    z = _merge_fwd(a1, a2, P, D, "f_merge")
    mo = _matmul(z, wo, "nn", F32, "f_mix_out")
    x2, h3 = _resid_rmsmod_fwd(x1, mo, mods, g3, (5, 1.0), 6, 7, "f_norm3")
    u2, s2, f2 = _ffn_fwd(h3, w2i, w2o, "f_ffn2")
    dx3, df2, acc_head = _loss_head(x2, f2, mods, final_g, target, "loss_head")

    du2, dh3 = _ffn_bwd(df2, u2, w2i, w2o, "b_ffn2")
    g_w2o = _matmul(s2, df2, "tn", BF16, "b_ffn2_out_dw")
    g_w2i = _matmul(du2, h3, "tn", BF16, "b_ffn2_in_dw")
    dx2, dmo, acc_n3 = _rmsmod_bwd(x2, dh3, dx3, mods, g3, 6, 7, (5, 1.0), mo, "b_norm3")

    dz = _matmul(dmo, wo, "nt", F32, "b_mix_out_dx")
    g_wo = _matmul(z, dmo, "tn", BF16, "b_mix_out_dw")
    da1, da2, dgt = _merge_bwd(dz, a1, a2, P, D, "b_merge")
    dyc = _matmul(da1, wbc, "nt", F32, "b_branch_conv_dx")
    do = _matmul(da2, wba, "nt", F32, "b_branch_attn_dx")
    g_wbc = _matmul(yc, da1, "tn", BF16, "b_branch_conv_dw")
    g_wba = _matmul(o, da2, "tn", BF16, "b_branch_attn_dw")
    token_a = hooks.reduce_early([g_wbc, g_wba, g_wo, g_w2i, g_w2o], "a")
    dob, delta = _attn_delta(do, o, "b_attn_delta", token_a)
    dq, dk, dv = _flash_bwd(qn, kn, vb, dob, lse, delta, "b_attn")
    dqkv, acc_qk = _qk_bwd(P, dq, dk, dv, gq, gk, cos_t, sin_t, D, "b_qk")
    dconv, acc_conv = _conv_bwd(P, dyc, conv_w, D, "b_conv")
    d_parts = (dconv, dqkv, dgt)
    dh2 = _concat_k_matmul(d_parts, wi, "b_mix_in_dx")
    g_wi = jnp.concatenate([_matmul(dp, h2, "tn", BF16, f"b_mix_in_dw_{i}") for i, dp in enumerate(d_parts)], axis=0)
    g2_b = after(g2, hooks.reduce_early([g_wi], "b"), "after_rs_b")
    dx1, df1, acc_n2 = _rmsmod_bwd(x1, dh2, dx2, mods, g2_b, 3, 4, (2, 0.5), f1, "b_norm2")

    du1, dh1 = _ffn_bwd(df1, u1, w1i, w1o, "b_ffn1")
    g_w1o = _matmul(s1, df1, "tn", BF16, "b_ffn1_out_dw")
    g_w1i = _matmul(du1, h1, "tn", BF16, "b_ffn1_in_dw", token=hooks.reduce_early([g_w1o], "c"))
    g1_d = after(g1, hooks.reduce_early([g_w1i], "d"), "after_rs_d")
    grad_x, _, acc_n1 = _rmsmod_bwd(xcat, dh1, dx1, mods, g1_d, 0, 1, None, None, "b_norm1", skip_first_tile=True)

    grads = (g_w1i, g_w1o, g_wi, g_wbc, g_wba, g_wo, g_w2i, g_w2o)
    accs = (acc_head, acc_n3, acc_n2, acc_n1, acc_conv, acc_qk)
    return grad_x, grads, accs


def _place():
    return lax.axis_index("x"), lax.axis_index("y"), lax.axis_index("c")


def _other_chips(x, y):
    return [(1 - x, y), (x, 1 - y), (1 - x, 1 - y)]


def _allgather8(v, name):
    R, N = v.shape

    def body(v_ref, out_ref, send_sems, recv_sems, local_sem):
        x, y, c = _place()
        me, sibling = (x, y, c), (x, y, 1 - c)
        chips = _other_chips(x, y)

        def blk(px, py, pc):
            return out_ref.at[4 * px + 2 * py + pc]

        def copy(k, block, to, src=None):
            return pltpu.make_async_remote_copy(
                src_ref=blk(*block) if src is None else src, dst_ref=blk(*block),
                send_sem=send_sems.at[k], recv_sem=recv_sems.at[k], device_id=to, device_id_type=MESH)

        mine = pltpu.make_async_copy(v_ref, blk(*me), local_sem)
        mine.start()
        first = [copy(0, me, sibling, src=v_ref)]
        first += [copy(1 + j, me, (*chip, c), src=v_ref) for j, chip in enumerate(chips)]
        for cp in first:
            cp.start()
        passed = [copy(4 + j, (*chip, c), sibling) for j, chip in enumerate(chips)]
        for j, chip in enumerate(chips):
            copy(1 + j, (*chip, c), me).wait_recv()
            passed[j].start()
        copy(0, sibling, me).wait_recv()
        for j, chip in enumerate(chips):
            copy(4 + j, (*chip, 1 - c), me).wait_recv()
        for cp in first + passed:
            cp.wait_send()
        mine.wait()

    return pl.pallas_call(
        body, name=name,
        out_shape=jax.ShapeDtypeStruct((N_DEV, R, N), v.dtype),
        in_specs=[pl.BlockSpec(memory_space=pltpu.VMEM)],
        out_specs=pl.BlockSpec(memory_space=pltpu.VMEM),
        scratch_shapes=[pltpu.SemaphoreType.DMA((7,)), pltpu.SemaphoreType.DMA((7,)), pltpu.SemaphoreType.DMA],
        compiler_params=pltpu.CompilerParams(vmem_limit_bytes=VMEM_LIMIT),
    )(v)


def _any_specs(n):
    return [pl.BlockSpec(memory_space=pl.ANY)] * n


def _pair_exchange(grads, name):
    n = len(grads)

    def body(*refs):
        g, land = refs[:n], refs[n:2 * n]
        send_sems, recv_sems = refs[2 * n:]
        x, y, c = _place()
        sibling = (x, y, 1 - c)
        copies = []
        for t in range(n):
            half = grads[t].shape[0] // (2 * N_CHIPS)
            for s in range(N_CHIPS):
                cp = pltpu.make_async_remote_copy(
                    src_ref=g[t].at[pl.ds((2 * s + 1 - c) * half, half), :], dst_ref=land[t].at[s],
                    send_sem=send_sems.at[N_CHIPS * t + s], recv_sem=recv_sems.at[N_CHIPS * t + s],
                    device_id=sibling, device_id_type=MESH)
                cp.start()
                copies.append(cp)
        for cp in copies:
            cp.wait_recv()
        for cp in copies:
            cp.wait_send()

    return pl.pallas_call(
        body, name=name,
        out_shape=[jax.ShapeDtypeStruct((N_CHIPS, a.shape[0] // (2 * N_CHIPS), a.shape[1]), a.dtype) for a in grads],
        in_specs=_any_specs(n), out_specs=_any_specs(n),
        scratch_shapes=[pltpu.SemaphoreType.DMA((N_CHIPS * n,)), pltpu.SemaphoreType.DMA((N_CHIPS * n,))],
    )(*grads)


def _place_shard(w2, idx, transpose, name):
    if transpose:
        D, rs = w2.shape
        tr = 128
        in_spec = pl.BlockSpec((D, tr), lambda i, idx: (0, i))
    else:
        rs, D = w2.shape
        tr = _pick(rs, (352, 256, 128, 64, 32, 16))
        in_spec = pl.BlockSpec((tr, D), lambda i, idx: (i, 0))
    steps = rs // tr

    def body(idx_ref, w_ref, o_ref):
        v = w_ref[...]
        o_ref[...] = (jnp.transpose(v) if transpose else v).astype(BF16)

    return pl.pallas_call(
        body, name=name,
        grid_spec=pltpu.PrefetchScalarGridSpec(
            num_scalar_prefetch=1, grid=(steps,), in_specs=[in_spec],
            out_specs=pl.BlockSpec((tr, D), lambda i, idx: (idx[1] * steps + i, 0))),
        out_shape=jax.ShapeDtypeStruct((N_CHIPS * rs, D), BF16),
        compiler_params=_params(("arbitrary",)),
    )(idx, w2)


def _pair_sum(g, landed, idx, name):
    _, half, D = landed.shape
    g4 = g.reshape(N_CHIPS, 2, half, D)
    tr = _pick(half, (416, 352, 128))

    def body(idx_ref, g_ref, l_ref, o_ref):
        o_ref[...] = (g_ref[0].astype(F32) + l_ref[...].astype(F32)).astype(BF16)

    return pl.pallas_call(
        body, name=name,
        grid_spec=pltpu.PrefetchScalarGridSpec(
            num_scalar_prefetch=1, grid=(N_CHIPS, half // tr),
            in_specs=[pl.BlockSpec((1, 1, tr, D), lambda s, i, idx: (idx[1 + s], idx[0], i, 0)),
                      pl.BlockSpec((1, tr, D), lambda s, i, idx: (idx[1 + s], i, 0))],
            out_specs=pl.BlockSpec((1, tr, D), lambda s, i, idx: (s, i, 0))),
        out_shape=jax.ShapeDtypeStruct((N_CHIPS, half, D), BF16),
        compiler_params=_params(("arbitrary", "arbitrary")),
    )(idx, g4, landed)


_HBM = pl.BlockSpec(memory_space=pltpu.HBM)
_SEM = pl.BlockSpec(memory_space=pltpu.SEMAPHORE)
_EFFECT = pltpu.SideEffectType.DATAFLOW_SIDE_EFFECTING


def _in_hbm(a):
    return pltpu.with_memory_space_constraint(a, pltpu.HBM)


def _split_copies(n, per, make):
    def start(nbuf, name, bufs):
        def body(*refs):
            ins = refs[:nbuf]
            send_sems, recv_sems = refs[nbuf], refs[nbuf + 1]
            token = refs[-1]
            for t in range(n):
                for j in range(per):
                    make(ins, t, j, send_sems.at[per * t + j], recv_sems.at[per * t + j]).start()
            token[...] = jnp.zeros(token.shape, token.dtype)

        out = pl.pallas_call(
            body, name=name,
            out_shape=(pltpu.SemaphoreType.DMA((per * n,)), pltpu.SemaphoreType.DMA((per * n,)),
                       *[pltpu.HBM(b.shape, b.dtype) for b in bufs], jax.ShapeDtypeStruct((8, 128), F32)),
            in_specs=[_HBM] * nbuf,
            out_specs=(_SEM, _SEM, *[_HBM] * nbuf, pl.BlockSpec(memory_space=pltpu.VMEM)),
            input_output_aliases={i: 2 + i for i in range(nbuf)},
            compiler_params=pltpu.CompilerParams(has_side_effects=_EFFECT),
        )(*[_in_hbm(b) for b in bufs])
        return out[0], out[1], list(out[2:2 + nbuf]), out[-1]

    def wait(nbuf, name, send_sems, recv_sems, bufs, after):
        def body(*refs):
            ins = refs[:nbuf]
            ss, rs = refs[nbuf], refs[nbuf + 1]
            for t in range(n):
                for j in range(per):
                    cp = make(ins, t, j, ss.at[per * t + j], rs.at[per * t + j])
                    cp.wait_send()
                    cp.wait_recv()

        return pl.pallas_call(
            body, name=name,
            out_shape=[pltpu.HBM(b.shape, b.dtype) for b in bufs],
            in_specs=[_HBM] * nbuf + [_SEM, _SEM, pl.BlockSpec(memory_space=pl.ANY)],
            out_specs=[_HBM] * nbuf,
            input_output_aliases={i: i for i in range(nbuf)},
            compiler_params=pltpu.CompilerParams(has_side_effects=_EFFECT),
        )(*bufs, send_sems, recv_sems, after)

    return start, wait


def _chip_exchange_split(n):
    def make(bufs, t, j, send_sem, recv_sem):
        x, y, c = _place()
        chip = _other_chips(x, y)[j]
        return pltpu.make_async_remote_copy(src_ref=bufs[t].at[1 + j], dst_ref=bufs[n + t].at[j], send_sem=send_sem,
                                            recv_sem=recv_sem, device_id=(*chip, c), device_id_type=MESH)

    return _split_copies(n, 3, make)


def _weights_gather_split(fulls):
    def make(bufs, t, j, send_sem, recv_sem):
        x, y, c = _place()
        chip = _other_chips(x, y)[j]
        rs = fulls[t].shape[0] // N_CHIPS
        rows = bufs[t].at[pl.ds((2 * x + y) * rs + c * (rs // 2), rs // 2), :]
        return pltpu.make_async_remote_copy(src_ref=rows, dst_ref=rows, send_sem=send_sem, recv_sem=recv_sem,
                                            device_id=(*chip, c), device_id_type=MESH)

    return _split_copies(len(fulls), 3, make)


def _weights_pass_on(fulls, name):
    n = len(fulls)

    def body(*refs):
        full = refs[n:2 * n]
        send_sems, recv_sems = refs[2 * n:]
        x, y, c = _place()
        chips = _other_chips(x, y)

        def copy(t, j, h):
            rs = fulls[t].shape[0] // N_CHIPS
            px, py = chips[j]
            rows = full[t].at[pl.ds((2 * px + py) * rs + h * (rs // 2), rs // 2), :]
            return pltpu.make_async_remote_copy(src_ref=rows, dst_ref=rows, send_sem=send_sems.at[3 * t + j],
                                                recv_sem=recv_sems.at[3 * t + j], device_id=(x, y, 1 - c),
                                                device_id_type=MESH)

        for t in range(n):
            for j in range(3):
                copy(t, j, c).start()
        for t in range(n):
            for j in range(3):
                copy(t, j, 1 - c).wait_recv()
        for t in range(n):
            for j in range(3):
                copy(t, j, c).wait_send()

    return pl.pallas_call(
        body, name=name,
        out_shape=[jax.ShapeDtypeStruct(f.shape, f.dtype) for f in fulls],
        in_specs=_any_specs(n), out_specs=_any_specs(n),
        input_output_aliases={t: t for t in range(n)},
        scratch_shapes=[pltpu.SemaphoreType.DMA((3 * n,)), pltpu.SemaphoreType.DMA((3 * n,))],
    )(*fulls)


def _after(value, token, name):
    def body(v_ref, t_ref, o_ref):
        o_ref[...] = v_ref[...]

    return pl.pallas_call(
        body, name=name, out_shape=jax.ShapeDtypeStruct(value.shape, value.dtype),
        in_specs=_whole(2), out_specs=pl.BlockSpec(memory_space=pltpu.VMEM),
    )(value, token)


def _chip_sum(ps, landed, idx, name):
    _, half, D = ps.shape
    tr = _pick(half, (416, 352, 128))
    steps = half // tr

    def body(idx_ref, p_ref, l_ref, o_ref):
        acc = p_ref[0].astype(F32)
        for j in range(3):
            acc = acc + l_ref[j].astype(F32)
        o_ref[...] = acc

    return pl.pallas_call(
        body, name=name,
        grid_spec=pltpu.PrefetchScalarGridSpec(
            num_scalar_prefetch=1, grid=(steps,),
            in_specs=[pl.BlockSpec((1, tr, D), lambda i, idx: (0, i, 0)),
                      pl.BlockSpec((3, tr, D), lambda i, idx: (0, i, 0))],
            out_specs=pl.BlockSpec((tr, D), lambda i, idx: (idx[0] * steps + i, 0))),
        out_shape=jax.ShapeDtypeStruct((2 * half, D), F32),
        compiler_params=_params(("arbitrary",)),
    )(idx, ps, landed)


def _pair_swap(shards, name):
    n = len(shards)

    def body(*refs):
        full = refs[n:2 * n]
        send_sems, recv_sems = refs[2 * n:]
        x, y, c = _place()

        def half(t, h):
            rows = shards[t].shape[0] // 2
            return full[t].at[pl.ds(h * rows, rows), :]

        def copy(t, h):
            return pltpu.make_async_remote_copy(src_ref=half(t, h), dst_ref=half(t, h), send_sem=send_sems.at[t],
                                                recv_sem=recv_sems.at[t], device_id=(x, y, 1 - c),
                                                device_id_type=MESH)

        for t in range(n):
            copy(t, c).start()
        for t in range(n):
            copy(t, 1 - c).wait_recv()
        for t in range(n):
            copy(t, c).wait_send()

    return pl.pallas_call(
        body, name=name,
        out_shape=[jax.ShapeDtypeStruct(a.shape, a.dtype) for a in shards],
        in_specs=_any_specs(n), out_specs=_any_specs(n),
        input_output_aliases={t: t for t in range(n)},
        scratch_shapes=[pltpu.SemaphoreType.DMA((n,)), pltpu.SemaphoreType.DMA((n,))],
    )(*shards)


def _gather_begin(fulls, tag):
    start, wait = _weights_gather_split(fulls)
    send_sems, recv_sems, bufs, token = start(len(fulls), f"ag_{tag}_start", fulls)
    return (wait, send_sems, recv_sems, bufs), token


def _gather_end(state, after, tag):
    wait, send_sems, recv_sems, bufs = state
    landed = wait(len(bufs), f"ag_{tag}_wait", send_sems, recv_sems, bufs, after)
    return _weights_pass_on(landed, f"ag_{tag}_pass_on")


class _Exchanges:
    def __init__(self, fulls_rest, idx):
        self.idx = idx
        self._rest, self.token = _gather_begin(fulls_rest, "rest")
        self._early = []

    def rest_weights(self, after):
        return _gather_end(self._rest, after, "rest")

    def _pair_sums(self, grads, tag):
        landed = _pair_exchange(grads, "rs_pair_exchange_" + tag)
        return [_pair_sum(g, l, self.idx, f"rs_pair_sum_{tag}{t}") for t, (g, l) in enumerate(zip(grads, landed))]

    def reduce_early(self, grads, tag):
        sums = self._pair_sums(grads, tag)
        zones = [lax.empty((3,) + s.shape[1:], s.dtype) for s in sums]
        start, wait = _chip_exchange_split(len(sums))
        send_sems, recv_sems, bufs, token = start(2 * len(sums), "rs_chip_start_" + tag, sums + zones)
        self._early.append((tag, wait, send_sems, recv_sems, bufs))
        return token

    def finish_early(self, after):
        halves = []
        for tag, wait, send_sems, recv_sems, bufs in self._early:
            n = len(bufs) // 2
            done = wait(len(bufs), "rs_chip_wait_" + tag, send_sems, recv_sems, bufs, after)
            halves += [_chip_sum(p, l, self.idx, f"rs_chip_sum_{tag}{t}")
                       for t, (p, l) in enumerate(zip(done[:n], done[n:]))]
        return halves


N_MOD = 9
PACK_HEAD, PACK_N3, PACK_N2, PACK_N1, PACK_CONV, PACK_QK = 0, 16, 32, 48, 64, 80
PACK_ROWS = 96
MOD_SRC = ((PACK_N1, 0), (PACK_N1, 1), (PACK_N2, 3), (PACK_N2, 0), (PACK_N2, 1),
           (PACK_N3, 3), (PACK_N3, 0), (PACK_N3, 1), (PACK_HEAD, 2))
CTX_ROW = 8


def _silu(v):
    return v * jax.nn.sigmoid(v)


def _whole(n):
    return [pl.BlockSpec(memory_space=pltpu.VMEM)] * n


def _mod_rows(cin, w_sh, b_sh, name):
    def body(c_ref, w_ref, b_ref, o_ref):
        a = _silu(c_ref[...]).astype(BF16)
        o_ref[...] = jnp.dot(a, w_ref[...].astype(BF16), preferred_element_type=F32) + b_ref[...]

    return pl.pallas_call(
        body, name=name, out_shape=jax.ShapeDtypeStruct((cin.shape[0], w_sh.shape[1]), F32),
        in_specs=_whole(3), out_specs=pl.BlockSpec(memory_space=pltpu.VMEM),
        compiler_params=pltpu.CompilerParams(vmem_limit_bytes=VMEM_LIMIT),
    )(cin, w_sh, b_sh)


def _small_reduce(gathered, name):
    _, _, D = gathered.shape

    def body(g_ref, loss_ref, db_ref, gn_ref, cv_ref, qk_ref, dm_ref):
        tot = g_ref[0]
        for r in range(1, N_DEV):
            tot = tot + g_ref[r]

        def both(block, row):
            return tot[block + row:block + row + 1, :] + tot[block + 8 + row:block + 8 + row + 1, :]

        loss = jnp.sum(both(PACK_HEAD, 0), axis=1, keepdims=True)
        loss_ref[...] = jnp.broadcast_to(loss, loss_ref.shape)
        db_ref[...] = jnp.zeros(db_ref.shape, F32)
        dm_ref[...] = jnp.zeros(dm_ref.shape, F32)
        for j, (block, row) in enumerate(MOD_SRC):
            db_ref[j:j + 1, :] = both(block, row)
            dm_ref[CTX_ROW, j:j + 1, :] = tot[block + row:block + row + 1, :]
            for r in range(N_DEV):
                dm_ref[r, j:j + 1, :] = g_ref[r, block + 8 + row:block + 8 + row + 1, :]
        gn_ref[...] = jnp.zeros(gn_ref.shape, F32)
        gn_ref[0:1, :] = both(PACK_N1, 2)
        gn_ref[8:9, :] = both(PACK_N2, 2)
        gn_ref[16:17, :] = both(PACK_N3, 2)
        gn_ref[24:25, :] = both(PACK_HEAD, 1)
        cv_ref[...] = jnp.zeros(cv_ref.shape, F32)
        for r in range(3):
            cv_ref[r:r + 1, :] = both(PACK_CONV, r)
        qk_ref[...] = jnp.zeros(qk_ref.shape, F32)
        qk_ref[0:1, 0:HEAD_DIM] = both(PACK_QK, 0)[:, 0:HEAD_DIM]
        qk_ref[0:1, HEAD_DIM:2 * HEAD_DIM] = both(PACK_QK, 1)[:, 0:HEAD_DIM]

    return pl.pallas_call(
        body, name=name,
        out_shape=[jax.ShapeDtypeStruct((8, 128), F32), jax.ShapeDtypeStruct((16, D), F32),
                   jax.ShapeDtypeStruct((32, D), F32), jax.ShapeDtypeStruct((8, D), F32),
                   jax.ShapeDtypeStruct((8, D), F32), jax.ShapeDtypeStruct((16, 16, D), F32)],
        in_specs=_whole(1), out_specs=_whole(6),
        compiler_params=pltpu.CompilerParams(vmem_limit_bytes=VMEM_LIMIT),
    )(gathered)


def _wmod_grad(cin, dm_sh, w_sh, name):
    def body(c_ref, d_ref, w_ref, gw_ref, cp_ref):
        a = _silu(c_ref[...]).astype(BF16)
        d = d_ref[...].astype(BF16)
        gw_ref[...] = lax.dot_general(a, d, (((0,), (0,)), ((), ())), preferred_element_type=F32)
        cp_ref[...] = lax.dot_general(d, w_ref[...].astype(BF16), (((1,), (1,)), ((), ())),
                                      preferred_element_type=F32)

    return pl.pallas_call(
        body, name=name,
        out_shape=[jax.ShapeDtypeStruct(w_sh.shape, F32), jax.ShapeDtypeStruct(cin.shape, F32)],
        in_specs=_whole(3), out_specs=_whole(2),
        compiler_params=pltpu.CompilerParams(vmem_limit_bytes=VMEM_LIMIT),
    )(cin, dm_sh, w_sh)


def _cctx_grad(parts, c_ctx8, name):
    def body(p_ref, c_ref, o_ref):
        tot = p_ref[0] + p_ref[2] + p_ref[4] + p_ref[6]
        cv = c_ref[...]
        sig = jax.nn.sigmoid(cv)
        rows = lax.broadcasted_iota(jnp.int32, tot.shape, 0)
        o_ref[...] = jnp.where(rows == 0, tot * (sig * (1.0 + cv * (1.0 - sig))), 0.0)

    return pl.pallas_call(
        body, name=name, out_shape=jax.ShapeDtypeStruct(c_ctx8.shape, F32),
        in_specs=_whole(2), out_specs=pl.BlockSpec(memory_space=pltpu.VMEM),
    )(parts, c_ctx8)


def _pad_rows(a, rows):
    return jnp.pad(a, ((0, rows - a.shape[0]), (0, 0)))


def _pack_small(c_ctx, b_mod, n1, n2, n3, final_g, gq, gk, conv_sh, D):
    misc = jnp.concatenate([gq, gk, conv_sh.reshape(1, -1)], axis=1)
    return jnp.concatenate([_pad_rows(c_ctx[None], 8), _pad_rows(b_mod.reshape(N_MOD, D), 16), _pad_rows(n1, 8),
                            _pad_rows(n2, 8), _pad_rows(n3, 8), _pad_rows(final_g[None], 8), _pad_rows(misc, 8)], axis=0)


def _unpack_small(p, D, conv_shape):
    misc = p[56:57]
    return dict(c_ctx=p[0], b_mod=p[8:8 + N_MOD].reshape(1, N_MOD * D), norm1_g=p[24:25], norm2_g=p[32:33],
                norm3_g=p[40:41], final_g=p[48], q_norm_g=misc[:, 0:HEAD_DIM], k_norm_g=misc[:, HEAD_DIM:2 * HEAD_DIM],
                conv_w=misc[:, 2 * HEAD_DIM:].reshape(conv_shape))


WEIGHT_ORDER = ("c_ctx", "w_mod", "b_mod", "norm1_g", "norm2_g", "norm3_g", "ffn1_w_in", "ffn1_w_out", "w_in",
                "conv_w", "q_norm_g", "k_norm_g", "w_branch_conv", "w_branch_attn", "w_out", "ffn2_w_in",
                "ffn2_w_out", "final_g")
BIG = ("ffn1_w_in", "ffn1_w_out", "w_in", "w_branch_conv", "w_branch_attn", "w_out", "ffn2_w_in", "ffn2_w_out")
COLUMN_SHARDED = ("ffn1_w_in", "w_in", "ffn2_w_in")


def kernel(x, c, ctx, c_ctx, w_mod, b_mod, norm1_g, norm2_g, norm3_g, ffn1_w_in, ffn1_w_out, w_in, conv_w, q_norm_g, k_norm_g, w_branch_conv, w_branch_attn, w_out, ffn2_w_in, ffn2_w_out, final_g, loss_target, m_c_ctx, m_w_mod, m_b_mod, m_norm1_g, m_norm2_g, m_norm3_g, m_ffn1_w_in, m_ffn1_w_out, m_w_in, m_conv_w, m_q_norm_g, m_k_norm_g, m_w_branch_conv, m_w_branch_attn, m_w_out, m_ffn2_w_in, m_ffn2_w_out, m_final_g, v_c_ctx, v_w_mod, v_b_mod, v_norm1_g, v_norm2_g, v_norm3_g, v_ffn1_w_in, v_ffn1_w_out, v_w_in, v_conv_w, v_q_norm_g, v_k_norm_g, v_w_branch_conv, v_w_branch_attn, v_w_out, v_ffn2_w_in, v_ffn2_w_out, v_final_g):
    w = dict(c_ctx=c_ctx, w_mod=w_mod, b_mod=b_mod, norm1_g=norm1_g, norm2_g=norm2_g, norm3_g=norm3_g,
             ffn1_w_in=ffn1_w_in, ffn1_w_out=ffn1_w_out, w_in=w_in, conv_w=conv_w, q_norm_g=q_norm_g,
             k_norm_g=k_norm_g, w_branch_conv=w_branch_conv, w_branch_attn=w_branch_attn, w_out=w_out,
             ffn2_w_in=ffn2_w_in, ffn2_w_out=ffn2_w_out, final_g=final_g)
    m = dict(c_ctx=m_c_ctx, w_mod=m_w_mod, b_mod=m_b_mod, norm1_g=m_norm1_g, norm2_g=m_norm2_g, norm3_g=m_norm3_g,
             ffn1_w_in=m_ffn1_w_in, ffn1_w_out=m_ffn1_w_out, w_in=m_w_in, conv_w=m_conv_w, q_norm_g=m_q_norm_g,
             k_norm_g=m_k_norm_g, w_branch_conv=m_w_branch_conv, w_branch_attn=m_w_branch_attn, w_out=m_w_out,
             ffn2_w_in=m_ffn2_w_in, ffn2_w_out=m_ffn2_w_out, final_g=m_final_g)
    v = dict(c_ctx=v_c_ctx, w_mod=v_w_mod, b_mod=v_b_mod, norm1_g=v_norm1_g, norm2_g=v_norm2_g, norm3_g=v_norm3_g,
             ffn1_w_in=v_ffn1_w_in, ffn1_w_out=v_ffn1_w_out, w_in=v_w_in, conv_w=v_conv_w, q_norm_g=v_q_norm_g,
             k_norm_g=v_k_norm_g, w_branch_conv=v_w_branch_conv, w_branch_attn=v_w_branch_attn, w_out=v_w_out,
             ffn2_w_in=v_ffn2_w_in, ffn2_w_out=v_ffn2_w_out, final_g=v_final_g)

    xi, yi, ci = _place()
    dev = 4 * xi + 2 * yi + ci
    shard = 2 * xi + yi
    idx = jnp.stack([ci, shard, 2 * (1 - xi) + yi, 2 * xi + (1 - yi), 2 * (1 - xi) + (1 - yi)]).astype(jnp.int32)
    D = x.shape[-1]
    ctx_len = ctx.shape[1]
    assert ctx_len == ROW and c.shape == (1, D)
    mcols = w_mod.shape[2]
    ccols = conv_w.shape[2]

    def place(names):
        return [_place_shard(w[n][0], idx, n in COLUMN_SHARDED, "place_" + n) for n in names]

    ffn1_gather, ffn1_token = _gather_begin(place(BIG[:2]), "ffn1")
    fulls_rest = place(BIG[2:])

    c_all = _allgather8(_after(jnp.broadcast_to(c, (8, D)), ffn1_token, "after_ag_ffn1"), "ag_c")[:, 0, :]
    cin = jnp.concatenate([c_all, _pad_rows(c_ctx[None], 8)], axis=0)
    b_sh = lax.dynamic_slice(b_mod, (0, shard * mcols), (1, mcols))
    mod_sh = _mod_rows(cin, w_mod[0], b_sh, "mod_rows")
    conv_rows = jnp.pad(conv_w[0], ((0, 8 - conv_w.shape[1]), (0, mcols - ccols)))
    mod_all = _allgather8(jnp.concatenate([mod_sh, conv_rows], axis=0), "ag_mod")
    mod_full = jnp.concatenate([mod_all[2 * s, :16] for s in range(N_CHIPS)], axis=1)
    conv_full = jnp.concatenate([mod_all[2 * s, 16:16 + conv_w.shape[1], :ccols] for s in range(N_CHIPS)], axis=1)
    mod_lat = lax.dynamic_slice(mod_full, (dev, 0), (1, N_MOD * D)).reshape(N_MOD, D)
    mod_ctx = mod_full[CTX_ROW].reshape(N_MOD, D)
    mods = jnp.stack([_pad_rows(mod_ctx, 16), _pad_rows(mod_lat, 16)])

    ffn1_w = _gather_end(ffn1_gather, mods, "ffn1")
    hooks = _Exchanges(fulls_rest, idx)

    xcat = jnp.concatenate([ctx[0], x[0]], axis=0)
    norm1_first = _after(norm1_g, hooks.token, "after_ag_rest")
    grad_x, _, accs = _local_step(xcat, loss_target[0], mods, (norm1_first, norm2_g, norm3_g), final_g[None],
                                  q_norm_g, k_norm_g, conv_full, ffn1_w, hooks, ctx_len)
    g = {}

    pack = jnp.concatenate([a.reshape(2 * ACC_ROWS, D) for a in accs], axis=0)
    gathered = _allgather8(pack, "ag_small")
    loss8, db_mod, g_norms, g_conv, g_qk, dm = _small_reduce(gathered, "small_reduce")
    dm_sh = lax.dynamic_slice(dm[:, :N_MOD, :].reshape(16, N_MOD * D), (0, shard * mcols), (16, mcols))
    g_wmod, cpart = _wmod_grad(cin, dm_sh, w_mod[0], "wmod_grad")
    g["w_mod"] = g_wmod[None]
    cparts = _allgather8(cpart[CTX_ROW:CTX_ROW + 8], "ag_cctx")
    g_cctx = _cctx_grad(cparts, _pad_rows(c_ctx[None], 8), "cctx_grad")
    g_conv_sh = lax.dynamic_slice(g_conv, (0, shard * ccols), (conv_w.shape[1], ccols))
    g_misc = jnp.concatenate([g_qk[0:1, 0:2 * HEAD_DIM], g_conv_sh.reshape(1, -1)], axis=1)
    g_pack = jnp.concatenate([g_cctx, db_mod, g_norms, _pad_rows(g_misc, 8)], axis=0)

    def packed(p):
        return _pack_small(p["c_ctx"], p["b_mod"], p["norm1_g"], p["norm2_g"], p["norm3_g"], p["final_g"],
                           p["q_norm_g"], p["k_norm_g"], p["conv_w"][0], D)

    d_pack, m_pack, v_pack = _adamw(packed(w), g_pack, packed(m), packed(v), "adamw_small")

    h_wbc, h_wba, h_wo, h_w2i, h_w2o, h_wi, h_w1o, h_w1i = hooks.finish_early(d_pack)
    halves = [h_w1i, h_w1o, h_wi, h_wbc, h_wba, h_wo, h_w2i, h_w2o]
    reduced = dict(zip(BIG, _pair_swap(halves, "rs_pair_swap")))
    g.update(_unpack_small(g_pack, D, conv_w.shape))
    delta = _unpack_small(d_pack, D, conv_w.shape)
    new_m = _unpack_small(m_pack, D, conv_w.shape)
    new_v = _unpack_small(v_pack, D, conv_w.shape)
    for n in BIG + ("w_mod",):
        if n in COLUMN_SHARDED:
            g2, d2, m2, v2 = _adamw_transposed(w[n][0], reduced[n], m[n][0], v[n][0], "adamw_" + n)
        else:
            g2 = reduced[n] if n in reduced else g[n][0]
            d2, m2, v2 = _adamw(w[n][0], g2, m[n][0], v[n][0], "adamw_" + n)
        g[n], delta[n], new_m[n], new_v[n] = g2[None], d2[None], m2[None], v2[None]

    loss = loss8[0, 0]
    return (loss, grad_x[None], *[g[n] for n in WEIGHT_ORDER], *[delta[n] for n in WEIGHT_ORDER],
            *[new_m[n] for n in WEIGHT_ORDER], *[new_v[n] for n in WEIGHT_ORDER])
```

```python
import functools

import jax
import jax.numpy as jnp
from jax import lax
from jax.experimental import pallas as pl
from jax.experimental.pallas import tpu as pltpu

F32 = jnp.float32
BF16 = jnp.bfloat16

HEAD_DIM = 128
N_Q_HEADS = 8
N_KV_HEADS = 2
GROUP = N_Q_HEADS // N_KV_HEADS
GRID_W = 64
ROPE_THETA = 10000.0
EPS = 1e-6
ATTN_SCALE = HEAD_DIM ** -0.5

ADAM_LR = 0.001
ADAM_B1 = 0.9
ADAM_B2 = 0.999
ADAM_EPS = 1e-08
ADAM_WD = 0.01
ADAM_STEP = 10

ROW = 256
HALO = 16
ACC_ROWS = 8
N_CHIPS = 4
N_DEV = 8
MESH = pl.DeviceIdType.MESH
VMEM_LIMIT = 48 * 1024 * 1024
ADAMW_BLOCK_BYTES = 1024 * 1024


def _pick(n, prefs):
    for p in prefs:
        if n % p == 0:
            return p
    return n


def _params(sem):
    return pltpu.CompilerParams(dimension_semantics=sem, vmem_limit_bytes=VMEM_LIMIT)


def _stream(i):
    return jnp.minimum(i, 1)


def _matmul(a, b, mode, out_dtype, name, tm=None, tn=None, tk=None, token=None):
    if mode == "nn":
        (M, K), (K2, N) = a.shape, b.shape
    elif mode == "nt":
        (M, K), (N, K2) = a.shape, b.shape
    else:
        (K, M), (K2, N) = a.shape, b.shape
    assert K == K2, (a.shape, b.shape, mode)
    tm = tm or _pick(M, (1664, 1408, 1024, 512, 256, 128) if mode == "tn" else (1408, 768, 512, 256, 128))
    tn = tn or _pick(N, (1664, 1408, 1024, 512, 256, 128))
    tk = tk or _pick(K, (1664, 1408, 1024, 768, 512, 256, 128))
    nk = K // tk
    if mode == "tn":
        a_spec = pl.BlockSpec((tk, tm), lambda i, j, k: (k, i))
    else:
        a_spec = pl.BlockSpec((tm, tk), lambda i, j, k: (i, k))
    if mode == "nt":
        b_spec = pl.BlockSpec((tn, tk), lambda i, j, k: (j, k))
    else:
        b_spec = pl.BlockSpec((tk, tn), lambda i, j, k: (k, j))
    dims = {"nn": ((1,), (0,)), "nt": ((1,), (1,)), "tn": ((0,), (0,))}[mode]
    use_scratch = nk > 1 and out_dtype != F32

    extra = [] if token is None else [token]

    def body(a_ref, b_ref, *rest):
        o_ref, scratch = rest[len(extra)], rest[len(extra) + 1:]
        p = lax.dot_general(a_ref[...].astype(BF16), b_ref[...].astype(BF16), (dims, ((), ())),
                            preferred_element_type=F32)
        if nk == 1:
            o_ref[...] = p.astype(o_ref.dtype)
            return
        acc_ref = scratch[0] if use_scratch else o_ref
        k = pl.program_id(2)

        @pl.when(k == 0)
        def _():
            acc_ref[...] = p

        @pl.when(k > 0)
        def _():
            acc_ref[...] += p

        if use_scratch:
            @pl.when(k == nk - 1)
            def _():
                o_ref[...] = acc_ref[...].astype(o_ref.dtype)

    return pl.pallas_call(
        body, name=name,
        grid=(M // tm, N // tn, nk),
        in_specs=[a_spec, b_spec] + [pl.BlockSpec(t.shape, lambda i, j, k: (0, 0)) for t in extra],
        out_specs=pl.BlockSpec((tm, tn), lambda i, j, k: (i, j)),
        out_shape=jax.ShapeDtypeStruct((M, N), out_dtype),
        scratch_shapes=[pltpu.VMEM((tm, tn), F32)] if use_scratch else [],
        compiler_params=_params(("parallel", "parallel", "arbitrary")),
    )(a, b, *extra)


def _row_spec(width, col=0):
    return pl.BlockSpec((ROW, width), lambda i, col=col: (i, col))


def _mods_spec(D):
    return pl.BlockSpec((1, 16, D), lambda i: (_stream(i), 0, 0))


def _acc_spec(D):
    return pl.BlockSpec((1, ACC_ROWS, D), lambda i: (_stream(i), 0, 0))


def _vec_spec(rows, D):
    return pl.BlockSpec((rows, D), lambda i: (0, 0))


def _acc_init(acc_ref):
    i = pl.program_id(0)

    @pl.when(i <= 1)
    def _():
        acc_ref[...] = jnp.zeros_like(acc_ref)


def _acc_add(acc_ref, row, val):
    acc_ref[0, row:row + 1, :] += jnp.sum(val, axis=0, keepdims=True)


def _norm_tile_fwd(x, m, g, shift_idx, scale_idx):
    inv = lax.rsqrt(jnp.mean(x * x, axis=-1, keepdims=True) + EPS)
    y = (x * inv) * g
    return (y * (1.0 + m[scale_idx:scale_idx + 1, :]) + m[shift_idx:shift_idx + 1, :]).astype(BF16)


def _norm_tile_bwd(x, dh, dres, m, g, shift_idx, scale_idx, acc_ref):
    inv = lax.rsqrt(jnp.mean(x * x, axis=-1, keepdims=True) + EPS)
    xn = x * inv
    dy = dh * (1.0 + m[scale_idx:scale_idx + 1, :])
    dxn = dy * g
    _acc_add(acc_ref, 0, dh)
    _acc_add(acc_ref, 1, dh * (xn * g))
    _acc_add(acc_ref, 2, dy * xn)
    return inv * (dxn - xn * jnp.mean(dxn * xn, axis=-1, keepdims=True)) + dres


def _gate_tile_bwd(dx, branch, m, gate, acc_ref):
    gate_idx, fac = gate
    _acc_add(acc_ref, 3, fac * dx * branch)
    return ((fac * m[gate_idx:gate_idx + 1, :]) * dx).astype(BF16)


_NT = (((1,), (1,)), ((), ()))


def _ffn_chunk(F):
    return _pick(F, (1408, 512, 256, 128))


def _resident():
    return pl.BlockSpec(memory_space=pltpu.VMEM)


def _ffn_tile_fwd(hv, wi_ref, wo_ref, u_ref, s_ref, F, cw):
    acc = jnp.zeros((hv.shape[0], wo_ref.shape[1]), F32)
    for j in range(F // cw):
        a = lax.dot_general(hv, wi_ref[j * cw:(j + 1) * cw, :], _NT, preferred_element_type=F32)
        b = lax.dot_general(hv, wi_ref[F + j * cw:F + (j + 1) * cw, :], _NT, preferred_element_type=F32)
        s = ((a * jax.nn.sigmoid(a)) * b).astype(BF16)
        u_ref[:, j * cw:(j + 1) * cw] = a.astype(BF16)
        u_ref[:, F + j * cw:F + (j + 1) * cw] = b.astype(BF16)
        s_ref[:, j * cw:(j + 1) * cw] = s
        acc = acc + jnp.dot(s, wo_ref[j * cw:(j + 1) * cw, :], preferred_element_type=F32)
    return acc


def _norm_ffn_fwd(xprev, branch, mods, g, gate, shift_idx, scale_idx, w_in_t, w_out, name, head=None):
    T, D = xprev.shape
    F = w_out.shape[0]
    cw = _ffn_chunk(F)
    has_res = branch is not None
    n_in = 1 + int(has_res) + 4 + (2 if head else 0)

    def body(*refs):
        ins, outs = list(refs[:n_in]), list(refs[n_in:])
        x_ref = ins.pop(0)
        f_ref = ins.pop(0) if has_res else None
        m_ref, g_ref, wi_ref, wo_ref = ins[:4]
        xo_ref = outs.pop(0) if has_res else None
        h_ref, u_ref, s_ref = outs[:3]
        m = m_ref[0]
        x = x_ref[...]
        if has_res:
            gate_idx, fac = gate
            x = x + (fac * m[gate_idx:gate_idx + 1, :]) * f_ref[...]
            xo_ref[...] = x
        hv = _norm_tile_fwd(x, m, g_ref[...], shift_idx, scale_idx)
        h_ref[...] = hv
        f = _ffn_tile_fwd(hv, wi_ref, wo_ref, u_ref, s_ref, F, cw)
        if head is None:
            outs[3][...] = f
            return
        fg_ref, t_ref = ins[4:6]
        dx_ref, df_ref, acc_ref = outs[3:6]
        _acc_init(acc_ref)
        lat = (pl.program_id(0) > 0).astype(F32)
        gate8 = 0.5 * m[8:9, :]
        x3 = x + gate8 * f
        inv3 = lax.rsqrt(jnp.mean(x3 * x3, axis=-1, keepdims=True) + EPS)
        xn = x3 * inv3
        fg = fg_ref[...]
        e = (xn * fg - t_ref[...]) * lat
        dy = e * (1.0 / D)
        dxn = dy * fg
        dx = inv3 * (dxn - xn * jnp.mean(dxn * xn, axis=-1, keepdims=True))
        dx_ref[...] = dx
        df_ref[...] = (gate8 * dx).astype(BF16)
        _acc_add(acc_ref, 0, (0.5 / D) * e * e)
        _acc_add(acc_ref, 1, dy * xn)
        _acc_add(acc_ref, 2, 0.5 * dx * f)

    in_specs = [_row_spec(D)] + ([_row_spec(D)] if has_res else []) + \
               [_mods_spec(D), _vec_spec(1, D), _resident(), _resident()]
    args = [xprev] + ([branch] if has_res else []) + [mods, g, w_in_t, w_out]
    out_specs = ([_row_spec(D)] if has_res else []) + [_row_spec(D), _row_spec(2 * F), _row_spec(F)]
    out_shape = ([jax.ShapeDtypeStruct((T, D), F32)] if has_res else []) + \
                [jax.ShapeDtypeStruct((T, D), BF16), jax.ShapeDtypeStruct((T, 2 * F), BF16),
                 jax.ShapeDtypeStruct((T, F), BF16)]
    if head is None:
        out_specs += [_row_spec(D)]
        out_shape += [jax.ShapeDtypeStruct((T, D), F32)]
    else:
        in_specs += [_vec_spec(1, D), pl.BlockSpec((ROW, D), lambda i: (jnp.maximum(i - 1, 0), 0))]
        args += list(head)
        out_specs += [_row_spec(D), _row_spec(D), _acc_spec(D)]
        out_shape += [jax.ShapeDtypeStruct((T, D), F32), jax.ShapeDtypeStruct((T, D), BF16),
                      jax.ShapeDtypeStruct((2, ACC_ROWS, D), F32)]
    out = pl.pallas_call(
        body, name=name, grid=(T // ROW,), in_specs=in_specs, out_specs=out_specs, out_shape=out_shape,
        compiler_params=_params(("arbitrary",) if head else ("parallel",)),
    )(*args)
    return tuple(out) if has_res else (None,) + tuple(out)


def _ffn_norm_bwd(df, u, w_in_t, w_out, x, dres, mods, g, shift_idx, scale_idx, gate, branch, name,
                  skip_first_tile=False):
    T, D = df.shape
    F = w_out.shape[0]
    cw = _ffn_chunk(F)
    nt = T // ROW
    has_gate = gate is not None
    n_in = 8 + int(has_gate)

    def body(*refs):
        ins, outs = list(refs[:n_in]), list(refs[n_in:])
        df_ref, u_ref, wi_ref, wo_ref, x_ref, dr_ref = ins[:6]
        b_ref = ins[6] if has_gate else None
        m_ref, g_ref = ins[-2:]
        du_ref, dx_ref = outs[:2]
        db_ref = outs[2] if has_gate else None
        acc_ref = outs[-1]
        _acc_init(acc_ref)
        dfv = df_ref[...]
        dh = jnp.zeros((ROW, D), F32)
        for j in range(F // cw):
            ds = lax.dot_general(dfv, wo_ref[j * cw:(j + 1) * cw, :], _NT, preferred_element_type=F32)
            a = u_ref[:, j * cw:(j + 1) * cw].astype(F32)
            b = u_ref[:, F + j * cw:F + (j + 1) * cw].astype(F32)
            sig = jax.nn.sigmoid(a)
            da = (ds * b * (sig * (1.0 + a * (1.0 - sig)))).astype(BF16)
            db = (ds * (a * sig)).astype(BF16)
            du_ref[:, j * cw:(j + 1) * cw] = da
            du_ref[:, F + j * cw:F + (j + 1) * cw] = db
            dh = dh + jnp.dot(da, wi_ref[j * cw:(j + 1) * cw, :], preferred_element_type=F32)
            dh = dh + jnp.dot(db, wi_ref[F + j * cw:F + (j + 1) * cw, :], preferred_element_type=F32)
        m = m_ref[0]
        dx = _norm_tile_bwd(x_ref[...], dh, dr_ref[...], m, g_ref[...], shift_idx, scale_idx, acc_ref)
        dx_ref[...] = dx
        if has_gate:
            db_ref[...] = _gate_tile_bwd(dx, b_ref[...], m, gate, acc_ref)

    in_specs = [_row_spec(D), _row_spec(2 * F), _resident(), _resident(), _row_spec(D), _row_spec(D)] + \
               ([_row_spec(D)] if has_gate else []) + [_mods_spec(D), _vec_spec(1, D)]
    args = [df, u, w_in_t, w_out, x, dres] + ([branch] if has_gate else []) + [mods, g]
    if skip_first_tile:
        dx_spec = pl.BlockSpec((ROW, D), lambda i: (jnp.maximum(i - 1, 0), 0))
        dx_shape = jax.ShapeDtypeStruct((T - ROW, D), F32)
    else:
        dx_spec = _row_spec(D)
        dx_shape = jax.ShapeDtypeStruct((T, D), F32)
    out_specs = [_row_spec(2 * F), dx_spec] + ([_row_spec(D)] if has_gate else []) + [_acc_spec(D)]
    out_shape = [jax.ShapeDtypeStruct((T, 2 * F), BF16), dx_shape] + \
                ([jax.ShapeDtypeStruct((T, D), BF16)] if has_gate else []) + \
                [jax.ShapeDtypeStruct((2, ACC_ROWS, D), F32)]
    out = pl.pallas_call(
        body, name=name, grid=(nt,), in_specs=in_specs, out_specs=out_specs, out_shape=out_shape,
        compiler_params=_params(("arbitrary",)),
    )(*args)
    if has_gate:
        return tuple(out)
    return out[0], out[1], None, out[2]


def _halo_specs(width, col, nt):
    per = ROW // HALO
    prev = pl.BlockSpec((HALO, width), lambda i, col=col: (jnp.maximum(i * per - 1, 0), col))
    nxt = pl.BlockSpec((HALO, width), lambda i, col=col: (jnp.minimum((i + 1) * per, nt * per - 1), col))
    return prev, nxt


def _f32(ref):
    return ref[...].astype(F32)


def _last_row(halo_ref):
    return halo_ref[HALO - 1:HALO, :].astype(F32)


def _first_row(halo_ref):
    return halo_ref[0:1, :].astype(F32)


def _shift_rows(v, prev_row, next_row):
    rows = lax.broadcasted_iota(jnp.int32, v.shape, 0)
    down = jnp.where(rows == 0, prev_row, pltpu.roll(v, 1, 0))
    up = jnp.where(rows == v.shape[0] - 1, next_row, pltpu.roll(v, v.shape[0] - 1, 0))
    return down, up


def _conv_fwd(P, conv_w, D, name):
    T = P.shape[0]
    nt = T // ROW
    cg_p, cg_n = _halo_specs(D, 1, nt)
    vc_p, vc_n = _halo_specs(D, 2, nt)

    def body(bg_ref, cg_ref, vc_ref, cgp_ref, vcp_ref, cgn_ref, vcn_ref, w_ref, y_ref):
        i = pl.program_id(0)
        has_prev = (i != 1).astype(F32)
        has_next = (i != nt - 1).astype(F32)
        u = _f32(cg_ref) * _f32(vc_ref)
        up_row = _last_row(cgp_ref) * _last_row(vcp_ref) * has_prev
        un_row = _first_row(cgn_ref) * _first_row(vcn_ref) * has_next
        um1, up1 = _shift_rows(u, up_row, un_row)
        w = w_ref[...]
        conv = um1 * w[0:1, :] + u * w[1:2, :] + up1 * w[2:3, :]
        y_ref[...] = (_f32(bg_ref) * conv).astype(BF16)

    return pl.pallas_call(
        body, name=name, grid=(nt,),
        in_specs=[_row_spec(D, 0), _row_spec(D, 1), _row_spec(D, 2), cg_p, vc_p, cg_n, vc_n, _vec_spec(3, D)],
        out_specs=_row_spec(D),
        out_shape=jax.ShapeDtypeStruct((T, D), BF16),
        compiler_params=_params(("parallel",)),
    )(P, P, P, P, P, P, P, conv_w)


def _conv_bwd(P, dy, conv_w, D, name):
    T = P.shape[0]
    nt = T // ROW
    bg_p, bg_n = _halo_specs(D, 0, nt)
    cg_p, cg_n = _halo_specs(D, 1, nt)
    vc_p, vc_n = _halo_specs(D, 2, nt)
    dy_p, dy_n = _halo_specs(D, 0, nt)

    def body(bg_ref, cg_ref, vc_ref, dy_ref, bgp_ref, cgp_ref, vcp_ref, dyp_ref,
             bgn_ref, cgn_ref, vcn_ref, dyn_ref, w_ref, o_ref, acc_ref):
        _acc_init(acc_ref)
        i = pl.program_id(0)
        lat = (i > 0).astype(F32)
        has_prev = (i != 1).astype(F32)
        has_next = (i != nt - 1).astype(F32)
        bg = _f32(bg_ref)
        cg = _f32(cg_ref)
        vc = _f32(vc_ref)
        dyv = dy_ref[...] * lat
        u = cg * vc
        up_row = _last_row(cgp_ref) * _last_row(vcp_ref) * has_prev
        un_row = _first_row(cgn_ref) * _first_row(vcn_ref) * has_next
        um1, up1 = _shift_rows(u, up_row, un_row)
        w = w_ref[...]
        conv = um1 * w[0:1, :] + u * w[1:2, :] + up1 * w[2:3, :]
        dc = dyv * bg
        dcp_row = _last_row(dyp_ref) * _last_row(bgp_ref) * has_prev
        dcn_row = _first_row(dyn_ref) * _first_row(bgn_ref) * has_next
        dcm1, dcp1 = _shift_rows(dc, dcp_row, dcn_row)
        du = dcp1 * w[0:1, :] + dc * w[1:2, :] + dcm1 * w[2:3, :]
        o_ref[:, 0:D] = (dyv * conv).astype(BF16)
        o_ref[:, D:2 * D] = (du * vc * lat).astype(BF16)
        o_ref[:, 2 * D:3 * D] = (du * cg * lat).astype(BF16)
        _acc_add(acc_ref, 0, dc * um1)
        _acc_add(acc_ref, 1, dc * u)
        _acc_add(acc_ref, 2, dc * up1)

    return pl.pallas_call(
        body, name=name, grid=(nt,),
        in_specs=[_row_spec(D, 0), _row_spec(D, 1), _row_spec(D, 2), _row_spec(D, 0),
                  bg_p, cg_p, vc_p, dy_p, bg_n, cg_n, vc_n, dy_n, _vec_spec(3, D)],
        out_specs=[_row_spec(3 * D), _acc_spec(D)],
        out_shape=[jax.ShapeDtypeStruct((T, 3 * D), BF16), jax.ShapeDtypeStruct((2, ACC_ROWS, D), F32)],
        compiler_params=_params(("arbitrary",)),
    )(P, P, P, dy, P, P, P, dy, P, P, P, dy, conv_w)


def _rope_tables(ctx_len, seq):
    n_freq = HEAD_DIM // 4
    rows = seq // GRID_W
    inv = ROPE_THETA ** (-jnp.arange(n_freq, dtype=F32) / n_freq)
    ar = jnp.arange(rows, dtype=F32)[:, None] * inv
    ac = jnp.arange(GRID_W, dtype=F32)[:, None] * inv

    def per_row(a):
        return jnp.repeat(a, GRID_W, axis=0)

    def per_col(a):
        return jnp.tile(a, (rows, 1))

    cos_t = jnp.concatenate([per_row(jnp.cos(ar)), per_row(jnp.cos(ar)), per_col(jnp.cos(ac)), per_col(jnp.cos(ac))], axis=1)
    sin_t = jnp.concatenate([per_row(-jnp.sin(ar)), per_row(jnp.sin(ar)), per_col(-jnp.sin(ac)), per_col(jnp.sin(ac))], axis=1)
    cos_t = jnp.concatenate([jnp.ones((ctx_len, HEAD_DIM), F32), cos_t], axis=0)
    sin_t = jnp.concatenate([jnp.zeros((ctx_len, HEAD_DIM), F32), sin_t], axis=0)
    return cos_t, sin_t


def _swap_halves(y):
    lanes = lax.broadcasted_iota(jnp.int32, y.shape, 1)
    first = (lanes % 64) < 32
    return jnp.where(first, pltpu.roll(y, HEAD_DIM - 32, 1), pltpu.roll(y, 32, 1))


def _qk_fwd(P, gq, gk, cos_t, sin_t, D, name):
    T = P.shape[0]
    QW = N_Q_HEADS * HEAD_DIM
    KW = N_KV_HEADS * HEAD_DIM
    q_col = (3 * D) // QW
    k_col = (3 * D + QW) // KW
    v_col = k_col + 1

    def body(q_ref, k_ref, v_ref, gq_ref, gk_ref, c_ref, s_ref, qo_ref, ko_ref, vo_ref):
        c = c_ref[...]
        s = s_ref[...]

        def head(x, g):
            inv = lax.rsqrt(jnp.mean(x * x, axis=-1, keepdims=True) + EPS)
            y = (x * inv) * g
            return y * c + _swap_halves(y) * s

        for h in range(N_Q_HEADS):
            sl = slice(h * HEAD_DIM, (h + 1) * HEAD_DIM)
            qo_ref[:, sl] = head(q_ref[:, sl].astype(F32), gq_ref[...]).astype(BF16)
        for h in range(N_KV_HEADS):
            sl = slice(h * HEAD_DIM, (h + 1) * HEAD_DIM)
            ko_ref[:, sl] = head(k_ref[:, sl].astype(F32), gk_ref[...]).astype(BF16)
        vo_ref[...] = v_ref[...].astype(BF16)

    return pl.pallas_call(
        body, name=name, grid=(T // ROW,),
        in_specs=[_row_spec(QW, q_col), _row_spec(KW, k_col), _row_spec(KW, v_col),
                  _vec_spec(1, HEAD_DIM), _vec_spec(1, HEAD_DIM), _row_spec(HEAD_DIM), _row_spec(HEAD_DIM)],
        out_specs=[_row_spec(QW), _row_spec(KW), _row_spec(KW)],
        out_shape=[jax.ShapeDtypeStruct((T, QW), BF16), jax.ShapeDtypeStruct((T, KW), BF16),
                   jax.ShapeDtypeStruct((T, KW), BF16)],
        compiler_params=_params(("parallel",)),
    )(P, P, P, gq, gk, cos_t, sin_t)


def _qk_bwd(P, dq, dk, dv, gq, gk, cos_t, sin_t, D, name):
    T = P.shape[0]
    QW = N_Q_HEADS * HEAD_DIM
    KW = N_KV_HEADS * HEAD_DIM
    q_col = (3 * D) // QW
    k_col = (3 * D + QW) // KW

    def body(q_ref, k_ref, dq_ref, dk_ref, dv_ref, gq_ref, gk_ref, c_ref, s_ref, o_ref, acc_ref):
        _acc_init(acc_ref)
        c = c_ref[...]
        s = s_ref[...]

        def head(x, d, g):
            dyv = d * c + _swap_halves(d * s)
            inv = lax.rsqrt(jnp.mean(x * x, axis=-1, keepdims=True) + EPS)
            xn = x * inv
            dxn = dyv * g
            dx = inv * (dxn - xn * jnp.mean(dxn * xn, axis=-1, keepdims=True))
            return dx, jnp.sum(dyv * xn, axis=0, keepdims=True)

        dgq = jnp.zeros((1, HEAD_DIM), F32)
        for h in range(N_Q_HEADS):
            sl = slice(h * HEAD_DIM, (h + 1) * HEAD_DIM)
            dx, dg = head(q_ref[:, sl].astype(F32), dq_ref[:, sl], gq_ref[...])
            o_ref[:, sl] = dx.astype(BF16)
            dgq = dgq + dg
        dgk = jnp.zeros((1, HEAD_DIM), F32)
        for h in range(N_KV_HEADS):
            sl = slice(h * HEAD_DIM, (h + 1) * HEAD_DIM)
            dx, dg = head(k_ref[:, sl].astype(F32), dk_ref[:, sl], gk_ref[...])
            o_ref[:, QW + h * HEAD_DIM:QW + (h + 1) * HEAD_DIM] = dx.astype(BF16)
            dgk = dgk + dg
        o_ref[:, QW + KW:QW + 2 * KW] = dv_ref[...].astype(BF16)
        acc_ref[0, 0:1, 0:HEAD_DIM] += dgq
        acc_ref[0, 1:2, 0:HEAD_DIM] += dgk

    return pl.pallas_call(
        body, name=name, grid=(T // ROW,),
        in_specs=[_row_spec(QW, q_col), _row_spec(KW, k_col), _row_spec(QW), _row_spec(KW), _row_spec(KW),
                  _vec_spec(1, HEAD_DIM), _vec_spec(1, HEAD_DIM), _row_spec(HEAD_DIM), _row_spec(HEAD_DIM)],
        out_specs=[_row_spec(QW + 2 * KW), _acc_spec(D)],
        out_shape=[jax.ShapeDtypeStruct((T, QW + 2 * KW), BF16), jax.ShapeDtypeStruct((2, ACC_ROWS, D), F32)],
        compiler_params=_params(("arbitrary",)),
    )(P, P, dq, dk, dv, gq, gk, cos_t, sin_t)


def _to_row(col, n):
    return jnp.transpose(jnp.broadcast_to(col, (n, HEAD_DIM)))[0:1, :]


LOG2E = 1.4426950408889634
ATTN_PARTS = 4


def _flash_fwd(q, k, v, name, tq=ROW, tk=None):
    T = q.shape[0]
    tk = tk or _pick(T, (1408, 768, 512, 256))
    ck = tk
    nk = T // tk
    GW = GROUP * HEAD_DIM

    def body(q_ref, k_ref, v_ref, o_ref, lse_ref, qs_ref, m_ref, l_ref, acc_ref, st_ref):
        ki = pl.program_id(2)

        @pl.when(ki == 0)
        def _():
            for g in range(GROUP):
                qs_ref[g * tq:(g + 1) * tq, :] = q_ref[:, g * HEAD_DIM:(g + 1) * HEAD_DIM]
            m_ref[...] = jnp.full(m_ref.shape, -jnp.inf, F32)
            l_ref[...] = jnp.zeros(l_ref.shape, F32)
            acc_ref[...] = jnp.zeros(acc_ref.shape, F32)

        w = GROUP * tq // ATTN_PARTS
        nck = tk // ck

        def lanes(p):
            return slice(p * w, (p + 1) * w)

        def keys(c):
            return slice(c * ck, (c + 1) * ck)

        def fold(a):
            return a.reshape(ck // 8, 8, w)

        def scores(p, c):
            st = lax.dot_general(k_ref[keys(c), :], qs_ref[lanes(p), :], _NT,
                                 preferred_element_type=F32) * (ATTN_SCALE * LOG2E)
            st_ref[keys(c), lanes(p)] = st
            return jnp.max(fold(st), axis=0)

        def new_max(p, partial):
            m_prev = m_ref[:, lanes(p)]
            m_new = jnp.maximum(m_prev, jnp.max(functools.reduce(jnp.maximum, partial), axis=0, keepdims=True))
            m_ref[:, lanes(p)] = m_new
            return m_new, jnp.exp2(m_prev - m_new)

        def weights(p, c, m_new):
            pt = jnp.exp2(st_ref[keys(c), lanes(p)] - m_new)
            pv = lax.dot_general(v_ref[keys(c), :], pt.astype(BF16), (((0,), (0,)), ((), ())),
                                 preferred_element_type=F32)
            return jnp.sum(fold(pt), axis=0), pv

        partial = [scores(0, c) for c in range(nck)]
        for p in range(ATTN_PARTS):
            m_new, alpha = new_max(p, partial)
            partial, sums, pvs = [], [], []
            for c in range(nck):
                if p + 1 < ATTN_PARTS:
                    partial.append(scores(p + 1, c))
                s8, pv = weights(p, c, m_new)
                sums.append(s8)
                pvs.append(pv)
            l_ref[:, lanes(p)] = alpha * l_ref[:, lanes(p)] + jnp.sum(sum(sums), axis=0, keepdims=True)
            acc_ref[:, lanes(p)] = alpha * acc_ref[:, lanes(p)] + sum(pvs)

        @pl.when(ki == nk - 1)
        def _():
            out = jnp.transpose(acc_ref[...] / l_ref[...])
            lse = m_ref[...] + jnp.log2(l_ref[...])
            for g in range(GROUP):
                o_ref[:, g * HEAD_DIM:(g + 1) * HEAD_DIM] = out[g * tq:(g + 1) * tq, :]
                lse_ref[0, g:g + 1, :] = lse[:, g * tq:(g + 1) * tq]

    return pl.pallas_call(
        body, name=name, grid=(N_KV_HEADS, T // tq, nk),
        in_specs=[pl.BlockSpec((tq, GW), lambda h, i, j: (i, h)),
                  pl.BlockSpec((tk, HEAD_DIM), lambda h, i, j: (j, h)),
                  pl.BlockSpec((tk, HEAD_DIM), lambda h, i, j: (j, h))],
        out_specs=[pl.BlockSpec((tq, GW), lambda h, i, j: (i, h)),
                   pl.BlockSpec((1, GROUP, tq), lambda h, i, j: (h, 0, i))],
        out_shape=[jax.ShapeDtypeStruct((T, N_Q_HEADS * HEAD_DIM), F32),
                   jax.ShapeDtypeStruct((N_KV_HEADS, GROUP, T), F32)],
        scratch_shapes=[pltpu.VMEM((GROUP * tq, HEAD_DIM), BF16), pltpu.VMEM((1, GROUP * tq), F32),
                        pltpu.VMEM((1, GROUP * tq), F32), pltpu.VMEM((HEAD_DIM, GROUP * tq), F32),
                        pltpu.VMEM((tk, GROUP * tq), F32)],
        compiler_params=_params(("parallel", "parallel", "arbitrary")),
    )(q, k, v)


def _attn_delta(do, o, name, token=None):
    T, QW = do.shape
    extra = [] if token is None else [token]

    def body(do_ref, o_ref, *rest):
        dob_ref, dl_ref = rest[len(extra):]
        dov = do_ref[...]
        dob_ref[...] = dov.astype(BF16)
        prod = dov * o_ref[...]
        for h in range(N_Q_HEADS):
            d = jnp.sum(prod[:, h * HEAD_DIM:(h + 1) * HEAD_DIM], axis=1, keepdims=True)
            dl_ref[h // GROUP, (h % GROUP):(h % GROUP) + 1, :] = _to_row(d, ROW)

    return pl.pallas_call(
        body, name=name, grid=(T // ROW,),
        in_specs=[_row_spec(QW), _row_spec(QW)] + [pl.BlockSpec(t.shape, lambda i: (0, 0)) for t in extra],
        out_specs=[_row_spec(QW), pl.BlockSpec((N_KV_HEADS, GROUP, ROW), lambda i: (0, 0, i))],
        out_shape=[jax.ShapeDtypeStruct((T, QW), BF16), jax.ShapeDtypeStruct((N_KV_HEADS, GROUP, T), F32)],
        compiler_params=_params(("parallel",)),
    )(do, o, *extra)


def _flash_bwd(q, k, v, do, lse, delta, name, tq=ROW, tk=None):
    T = q.shape[0]
    tk = tk or _pick(T, (1408, 768, 512, 256))
    nk = T // tk
    GW = GROUP * HEAD_DIM
    nt = (((1,), (1,)), ((), ()))

    def body(q_ref, do_ref, k_ref, v_ref, lse_ref, dl_ref, dq_ref, dk_ref, dv_ref, qs_ref, dos_ref, dqt_ref):
        qi = pl.program_id(1)
        ki = pl.program_id(2)

        @pl.when(ki == 0)
        def _():
            for g in range(GROUP):
                qs_ref[g * tq:(g + 1) * tq, :] = q_ref[:, g * HEAD_DIM:(g + 1) * HEAD_DIM]
                dos_ref[g * tq:(g + 1) * tq, :] = do_ref[:, g * HEAD_DIM:(g + 1) * HEAD_DIM]
            dqt_ref[...] = jnp.zeros(dqt_ref.shape, F32)

        kk = k_ref[...]
        vv = v_ref[...]

        def lanes(p):
            return slice(p * tq, (p + 1) * tq)

        def products(p):
            st = lax.dot_general(kk, qs_ref[lanes(p), :], nt, preferred_element_type=F32)
            dpt = lax.dot_general(vv, dos_ref[lanes(p), :], nt, preferred_element_type=F32)
            return st, dpt

        dk_c = jnp.zeros((tk, HEAD_DIM), F32)
        dv_c = jnp.zeros((tk, HEAD_DIM), F32)
        ahead = products(0)
        for p in range(GROUP):
            st, dpt = ahead
            if p + 1 < GROUP:
                ahead = products(p + 1)
            pt = jnp.exp2(st * (ATTN_SCALE * LOG2E) - lse_ref[0, p:p + 1, :])
            dst = ((pt * (dpt - dl_ref[0, p:p + 1, :])) * ATTN_SCALE).astype(BF16)
            dv_c = dv_c + jnp.dot(pt.astype(BF16), dos_ref[lanes(p), :], preferred_element_type=F32)
            dk_c = dk_c + jnp.dot(dst, qs_ref[lanes(p), :], preferred_element_type=F32)
            dqt_ref[:, lanes(p)] += lax.dot_general(kk, dst, (((0,), (0,)), ((), ())), preferred_element_type=F32)
        rows = pl.ds(pl.multiple_of(ki * tk, tk), tk)

        @pl.when(qi == 0)
        def _():
            dk_ref[rows, :] = dk_c
            dv_ref[rows, :] = dv_c

        @pl.when(qi > 0)
        def _():
            dk_ref[rows, :] += dk_c
            dv_ref[rows, :] += dv_c

        @pl.when(ki == nk - 1)
        def _():
            dqv = jnp.transpose(dqt_ref[...])
            for g in range(GROUP):
                dq_ref[:, g * HEAD_DIM:(g + 1) * HEAD_DIM] = dqv[g * tq:(g + 1) * tq, :]

    return pl.pallas_call(
        body, name=name, grid=(N_KV_HEADS, T // tq, nk),
        in_specs=[pl.BlockSpec((tq, GW), lambda h, i, j: (i, h)),
                  pl.BlockSpec((tq, GW), lambda h, i, j: (i, h)),
                  pl.BlockSpec((tk, HEAD_DIM), lambda h, i, j: (j, h)),
                  pl.BlockSpec((tk, HEAD_DIM), lambda h, i, j: (j, h)),
                  pl.BlockSpec((1, GROUP, tq), lambda h, i, j: (h, 0, i)),
                  pl.BlockSpec((1, GROUP, tq), lambda h, i, j: (h, 0, i))],
        out_specs=[pl.BlockSpec((tq, GW), lambda h, i, j: (i, h)),
                   pl.BlockSpec((T, HEAD_DIM), lambda h, i, j: (0, h)),
                   pl.BlockSpec((T, HEAD_DIM), lambda h, i, j: (0, h))],
        out_shape=[jax.ShapeDtypeStruct((T, N_Q_HEADS * HEAD_DIM), F32),
                   jax.ShapeDtypeStruct((T, N_KV_HEADS * HEAD_DIM), F32),
                   jax.ShapeDtypeStruct((T, N_KV_HEADS * HEAD_DIM), F32)],
        scratch_shapes=[pltpu.VMEM((GROUP * tq, HEAD_DIM), BF16), pltpu.VMEM((GROUP * tq, HEAD_DIM), BF16),
                        pltpu.VMEM((HEAD_DIM, GROUP * tq), F32)],
        compiler_params=_params(("arbitrary", "arbitrary", "arbitrary")),
    )(q, do, k, v, lse, delta)


def _gate_specs(D):
    w = D // 2
    first = (3 * D + (N_Q_HEADS + 2 * N_KV_HEADS) * HEAD_DIM) // w
    return [pl.BlockSpec((ROW, w), lambda i, c=first + j: (i, c)) for j in range(4)]


def _merge_fwd(a1, a2, P, D, name):
    T = a1.shape[0]
    w = D // 2

    def body(a1_ref, a2_ref, g0, g1, g2, g3, z_ref):
        for j, (gc, ga) in enumerate(((g0, g2), (g1, g3))):
            sl = slice(j * w, (j + 1) * w)
            z = jax.nn.sigmoid(_f32(gc)) * a1_ref[:, sl] + jax.nn.sigmoid(_f32(ga)) * a2_ref[:, sl]
            z_ref[:, sl] = z.astype(BF16)

    return pl.pallas_call(
        body, name=name, grid=(T // ROW,),
        in_specs=[_row_spec(D), _row_spec(D)] + _gate_specs(D),
        out_specs=_row_spec(D), out_shape=jax.ShapeDtypeStruct((T, D), BF16),
        compiler_params=_params(("parallel",)),
    )(a1, a2, P, P, P, P)


def _merge_bwd(dz, a1, a2, P, D, name):
    T = a1.shape[0]
    w = D // 2

    def body(dz_ref, a1_ref, a2_ref, g0, g1, g2, g3, d1_ref, d2_ref, dg_ref):
        for j, (gc, ga) in enumerate(((g0, g2), (g1, g3))):
            sl = slice(j * w, (j + 1) * w)
            dz = dz_ref[:, sl]
            sc = jax.nn.sigmoid(_f32(gc))
            sa = jax.nn.sigmoid(_f32(ga))
            d1_ref[:, sl] = (dz * sc).astype(BF16)
            d2_ref[:, sl] = (dz * sa).astype(BF16)
            dg_ref[:, j * w:(j + 1) * w] = (dz * a1_ref[:, sl] * (sc * (1.0 - sc))).astype(BF16)
            dg_ref[:, D + j * w:D + (j + 1) * w] = (dz * a2_ref[:, sl] * (sa * (1.0 - sa))).astype(BF16)

    return pl.pallas_call(
        body, name=name, grid=(T // ROW,),
        in_specs=[_row_spec(D), _row_spec(D), _row_spec(D)] + _gate_specs(D),
        out_specs=[_row_spec(D), _row_spec(D), _row_spec(2 * D)],
        out_shape=[jax.ShapeDtypeStruct((T, D), BF16), jax.ShapeDtypeStruct((T, D), BF16),
                   jax.ShapeDtypeStruct((T, 2 * D), BF16)],
        compiler_params=_params(("parallel",)),
    )(dz, a1, a2, P, P, P, P)


def _adamw_math(w, g, m, v):
    m = ADAM_B1 * m + (1.0 - ADAM_B1) * g
    v = ADAM_B2 * v + (1.0 - ADAM_B2) * (g * g)
    m_hat = m / (1.0 - ADAM_B1 ** ADAM_STEP)
    v_hat = v / (1.0 - ADAM_B2 ** ADAM_STEP)
    delta = -ADAM_LR * (m_hat / (jnp.sqrt(v_hat) + ADAM_EPS) + ADAM_WD * w)
    return delta, m, v


def _adamw(w, g, m, v, name):
    R, C = w.shape
    tr = _pick(R, tuple(t for t in (256, 128, 64, 32, 16, 8) if t * C * 4 <= ADAMW_BLOCK_BYTES))

    def body(w_ref, g_ref, m_ref, v_ref, d_ref, mo_ref, vo_ref):
        d, mn, vn = _adamw_math(w_ref[...], g_ref[...], m_ref[...], v_ref[...])
        d_ref[...] = d
        mo_ref[...] = mn
        vo_ref[...] = vn

    spec = pl.BlockSpec((tr, C), lambda i: (i, 0))
    return pl.pallas_call(
        body, name=name, grid=(R // tr,),
        in_specs=[spec] * 4, out_specs=[spec] * 3,
        out_shape=[jax.ShapeDtypeStruct((R, C), F32)] * 3,
        compiler_params=_params(("parallel",)),
    )(w, g, m, v)


def _norm_mix_in_fwd(xprev, branch, mods, g, gate, shift_idx, scale_idx, w_t, name):
    T, D = xprev.shape
    N = w_t.shape[0]
    cw = _pick(N, (1664, 1024, 512, 256, 128))

    def body(x_ref, f_ref, m_ref, g_ref, w_ref, xo_ref, h_ref, p_ref):
        m = m_ref[0]
        gate_idx, fac = gate
        x = x_ref[...] + (fac * m[gate_idx:gate_idx + 1, :]) * f_ref[...]
        xo_ref[...] = x
        hv = _norm_tile_fwd(x, m, g_ref[...], shift_idx, scale_idx)
        h_ref[...] = hv
        for j in range(N // cw):
            p_ref[:, j * cw:(j + 1) * cw] = lax.dot_general(
                hv, w_ref[j * cw:(j + 1) * cw, :], _NT, preferred_element_type=F32).astype(BF16)

    return pl.pallas_call(
        body, name=name, grid=(T // ROW,),
        in_specs=[_row_spec(D), _row_spec(D), _mods_spec(D), _vec_spec(1, D), _resident()],
        out_specs=[_row_spec(D), _row_spec(D), _row_spec(N)],
        out_shape=[jax.ShapeDtypeStruct((T, D), F32), jax.ShapeDtypeStruct((T, D), BF16),
                   jax.ShapeDtypeStruct((T, N), BF16)],
        compiler_params=_params(("parallel",)),
    )(xprev, branch, mods, g, w_t)


def _mix_in_norm_bwd(parts, w_t, x, dres, mods, g, shift_idx, scale_idx, gate, branch, name):
    T, D = x.shape
    n = len(parts)
    offs = [0]
    for p in parts:
        offs.append(offs[-1] + p.shape[1])
    assert offs[-1] == w_t.shape[0]

    def body(*refs):
        w_ref, x_ref, dr_ref, b_ref, m_ref, g_ref, dx_ref, db_ref, acc_ref = refs[n:]
        _acc_init(acc_ref)
        dh = None
        for i, a_ref in enumerate(refs[:n]):
            d = jnp.dot(a_ref[...], w_ref[offs[i]:offs[i + 1], :], preferred_element_type=F32)
            dh = d if dh is None else dh + d
        m = m_ref[0]
        dx = _norm_tile_bwd(x_ref[...], dh, dr_ref[...], m, g_ref[...], shift_idx, scale_idx, acc_ref)
        dx_ref[...] = dx
        db_ref[...] = _gate_tile_bwd(dx, b_ref[...], m, gate, acc_ref)

    return pl.pallas_call(
        body, name=name, grid=(T // ROW,),
        in_specs=[_row_spec(p.shape[1]) for p in parts] +
                 [_resident(), _row_spec(D), _row_spec(D), _row_spec(D), _mods_spec(D), _vec_spec(1, D)],
        out_specs=[_row_spec(D), _row_spec(D), _acc_spec(D)],
        out_shape=[jax.ShapeDtypeStruct((T, D), F32), jax.ShapeDtypeStruct((T, D), BF16),
                   jax.ShapeDtypeStruct((2, ACC_ROWS, D), F32)],
        compiler_params=_params(("arbitrary",)),
    )(*parts, w_t, x, dres, branch, mods, g)


def _adamw_transposed(w, gt, m, v, name):
    R, C = w.shape
    tc = 128

    def body(w_ref, g_ref, m_ref, v_ref, go_ref, d_ref, mo_ref, vo_ref):
        g = jnp.transpose(g_ref[...])
        d, mn, vn = _adamw_math(w_ref[...], g, m_ref[...], v_ref[...])
        go_ref[...] = g
        d_ref[...] = d
        mo_ref[...] = mn
        vo_ref[...] = vn

    spec = pl.BlockSpec((R, tc), lambda j: (0, j))
    return pl.pallas_call(
        body, name=name, grid=(C // tc,),
        in_specs=[spec, pl.BlockSpec((tc, R), lambda j: (j, 0)), spec, spec], out_specs=[spec] * 4,
        out_shape=[jax.ShapeDtypeStruct((R, C), F32)] * 4,
        compiler_params=_params(("parallel",)),
    )(w, gt, m, v)


class _NoExchange:
    def __init__(self, rest):
        self.rest = rest

    def rest_weights(self, after):
        return self.rest

    def reduce_early(self, grads, tag):
        return None


def _local_step(xcat, target, mods, norm_g, final_g, gq, gk, conv_w, ffn1_w, hooks, ctx_len):
    T, D = xcat.shape
    w1i, w1o = ffn1_w
    g1, g2, g3 = norm_g
    cos_t, sin_t = _rope_tables(ctx_len, T - ctx_len)

    def after(value, token, name):
        return value if token is None else _after(value, token, name)

    _, h1, u1, s1, f1 = _norm_ffn_fwd(xcat, None, mods, g1, None, 0, 1, w1i, w1o, "f_ffn1")
    wi, wbc, wba, wo, w2i, w2o = hooks.rest_weights(f1)
    x1, h2, P = _norm_mix_in_fwd(xcat, f1, mods, g2, (2, 0.5), 3, 4, wi, "f_mix_in")
    yc = _conv_fwd(P, conv_w, D, "f_conv")
    qn, kn, vb = _qk_fwd(P, gq, gk, cos_t, sin_t, D, "f_qk")
    o, lse = _flash_fwd(qn, kn, vb, "f_attn")
    a1 = _matmul(yc, wbc, "nn", F32, "f_branch_conv")
    a2 = _matmul(o, wba, "nn", F32, "f_branch_attn")
    z = _merge_fwd(a1, a2, P, D, "f_merge")
    mo = _matmul(z, wo, "nn", F32, "f_mix_out")
    x2, h3, u2, s2, dx3, df2, acc_head = _norm_ffn_fwd(x1, mo, mods, g3, (5, 1.0), 6, 7, w2i, w2o, "f_ffn2",
                                                       head=(final_g, target))

    du2, dx2, dmo, acc_n3 = _ffn_norm_bwd(df2, u2, w2i, w2o, x2, dx3, mods, g3, 6, 7, (5, 1.0), mo, "b_ffn2")
    g_w2o = _matmul(s2, df2, "tn", BF16, "b_ffn2_out_dw")
    g_w2i = _matmul(du2, h3, "tn", BF16, "b_ffn2_in_dw")

    dz = _matmul(dmo, wo, "nt", F32, "b_mix_out_dx")
    g_wo = _matmul(z, dmo, "tn", BF16, "b_mix_out_dw")
    da1, da2, dgt = _merge_bwd(dz, a1, a2, P, D, "b_merge")
    dyc = _matmul(da1, wbc, "nt", F32, "b_branch_conv_dx")
    do = _matmul(da2, wba, "nt", F32, "b_branch_attn_dx")
    g_wbc = _matmul(yc, da1, "tn", BF16, "b_branch_conv_dw")
    g_wba = _matmul(o, da2, "tn", BF16, "b_branch_attn_dw")
    token_a = hooks.reduce_early([g_wbc, g_wba, g_wo, g_w2i, g_w2o], "a")
    dob, delta = _attn_delta(do, o, "b_attn_delta", token_a)
    dq, dk, dv = _flash_bwd(qn, kn, vb, dob, lse, delta, "b_attn")
    dqkv, acc_qk = _qk_bwd(P, dq, dk, dv, gq, gk, cos_t, sin_t, D, "b_qk")
    dconv, acc_conv = _conv_bwd(P, dyc, conv_w, D, "b_conv")
    d_parts = (dconv, dqkv, dgt)
    dx1, df1, acc_n2 = _mix_in_norm_bwd(d_parts, wi, x1, dx2, mods, g2, 3, 4, (2, 0.5), f1, "b_mix_in")
    g_wi = jnp.concatenate([_matmul(dp, h2, "tn", BF16, f"b_mix_in_dw_{i}") for i, dp in enumerate(d_parts)], axis=0)
    g1_b = after(g1, hooks.reduce_early([g_wi], "b"), "after_rs_b")

    du1, grad_x, _, acc_n1 = _ffn_norm_bwd(df1, u1, w1i, w1o, xcat, dx1, mods, g1_b, 0, 1, None, None, "b_ffn1",
                                           skip_first_tile=True)
    g_w1o = _matmul(s1, df1, "tn", BF16, "b_ffn1_out_dw")
    g_w1i = _matmul(du1, h1, "tn", BF16, "b_ffn1_in_dw", token=hooks.reduce_early([g_w1o], "c"))
    hooks.reduce_early([g_w1i], "d")

    grads = (g_w1i, g_w1o, g_wi, g_wbc, g_wba, g_wo, g_w2i, g_w2o)
    accs = (acc_head, acc_n3, acc_n2, acc_n1, acc_conv, acc_qk)
    return grad_x, grads, accs


def _place():
    return lax.axis_index("x"), lax.axis_index("y"), lax.axis_index("c")


def _other_chips(x, y):
    return [(1 - x, y), (x, 1 - y), (1 - x, 1 - y)]


def _allgather8(v, name):
    R, N = v.shape

    def body(v_ref, out_ref, send_sems, recv_sems, local_sem):
        x, y, c = _place()
        me, sibling = (x, y, c), (x, y, 1 - c)
        chips = _other_chips(x, y)

        def blk(px, py, pc):
            return out_ref.at[4 * px + 2 * py + pc]

        def copy(k, block, to, src=None):
            return pltpu.make_async_remote_copy(
                src_ref=blk(*block) if src is None else src, dst_ref=blk(*block),
                send_sem=send_sems.at[k], recv_sem=recv_sems.at[k], device_id=to, device_id_type=MESH)

        mine = pltpu.make_async_copy(v_ref, blk(*me), local_sem)
        mine.start()
        first = [copy(0, me, sibling, src=v_ref)]
        first += [copy(1 + j, me, (*chip, c), src=v_ref) for j, chip in enumerate(chips)]
        for cp in first:
            cp.start()
        passed = [copy(4 + j, (*chip, c), sibling) for j, chip in enumerate(chips)]
        for j, chip in enumerate(chips):
            copy(1 + j, (*chip, c), me).wait_recv()
            passed[j].start()
        copy(0, sibling, me).wait_recv()
        for j, chip in enumerate(chips):
            copy(4 + j, (*chip, 1 - c), me).wait_recv()
        for cp in first + passed:
            cp.wait_send()
        mine.wait()

    return pl.pallas_call(
        body, name=name,
        out_shape=jax.ShapeDtypeStruct((N_DEV, R, N), v.dtype),
        in_specs=[pl.BlockSpec(memory_space=pltpu.VMEM)],
        out_specs=pl.BlockSpec(memory_space=pltpu.VMEM),
        scratch_shapes=[pltpu.SemaphoreType.DMA((7,)), pltpu.SemaphoreType.DMA((7,)), pltpu.SemaphoreType.DMA],
        compiler_params=pltpu.CompilerParams(vmem_limit_bytes=VMEM_LIMIT),
    )(v)


def _any_specs(n):
    return [pl.BlockSpec(memory_space=pl.ANY)] * n


def _pair_exchange(grads, name):
    n = len(grads)

    def body(*refs):
        g, land = refs[:n], refs[n:2 * n]
        send_sems, recv_sems = refs[2 * n:]
        x, y, c = _place()
        sibling = (x, y, 1 - c)
        copies = []
        for t in range(n):
            half = grads[t].shape[0] // (2 * N_CHIPS)
            for s in range(N_CHIPS):
                cp = pltpu.make_async_remote_copy(
                    src_ref=g[t].at[pl.ds((2 * s + 1 - c) * half, half), :], dst_ref=land[t].at[s],
                    send_sem=send_sems.at[N_CHIPS * t + s], recv_sem=recv_sems.at[N_CHIPS * t + s],
                    device_id=sibling, device_id_type=MESH)
                cp.start()
                copies.append(cp)
        for cp in copies:
            cp.wait_recv()
        for cp in copies:
            cp.wait_send()

    return pl.pallas_call(
        body, name=name,
        out_shape=[jax.ShapeDtypeStruct((N_CHIPS, a.shape[0] // (2 * N_CHIPS), a.shape[1]), a.dtype) for a in grads],
        in_specs=_any_specs(n), out_specs=_any_specs(n),
        scratch_shapes=[pltpu.SemaphoreType.DMA((N_CHIPS * n,)), pltpu.SemaphoreType.DMA((N_CHIPS * n,))],
    )(*grads)


def _place_shard(w2, idx, transpose, name, token):
    if transpose:
        D, rs = w2.shape
        tr = 128
        in_spec = pl.BlockSpec((D, tr), lambda i, idx: (0, i))
    else:
        rs, D = w2.shape
        tr = _pick(rs, (352, 256, 128, 64, 32, 16))
        in_spec = pl.BlockSpec((tr, D), lambda i, idx: (i, 0))
    steps = rs // tr

    def body(idx_ref, w_ref, t_ref, o_ref):
        v = w_ref[...]
        o_ref[...] = (jnp.transpose(v) if transpose else v).astype(BF16)

    return pl.pallas_call(
        body, name=name,
        grid_spec=pltpu.PrefetchScalarGridSpec(
            num_scalar_prefetch=1, grid=(steps,),
            in_specs=[in_spec, pl.BlockSpec(token.shape, lambda i, idx: (0, 0))],
            out_specs=pl.BlockSpec((tr, D), lambda i, idx: (idx[1] * steps + i, 0))),
        out_shape=jax.ShapeDtypeStruct((N_CHIPS * rs, D), BF16),
        compiler_params=_params(("arbitrary",)),
    )(idx, w2, token)


def _pair_sum(g, landed, idx, name):
    _, half, D = landed.shape
    g4 = g.reshape(N_CHIPS, 2, half, D)
    tr = _pick(half, (416, 352, 128))

    def body(idx_ref, g_ref, l_ref, o_ref):
        o_ref[...] = (g_ref[0].astype(F32) + l_ref[...].astype(F32)).astype(BF16)

    return pl.pallas_call(
        body, name=name,
        grid_spec=pltpu.PrefetchScalarGridSpec(
            num_scalar_prefetch=1, grid=(N_CHIPS, half // tr),
            in_specs=[pl.BlockSpec((1, 1, tr, D), lambda s, i, idx: (idx[1 + s], idx[0], i, 0)),
                      pl.BlockSpec((1, tr, D), lambda s, i, idx: (idx[1 + s], i, 0))],
            out_specs=pl.BlockSpec((1, tr, D), lambda s, i, idx: (s, i, 0))),
        out_shape=jax.ShapeDtypeStruct((N_CHIPS, half, D), BF16),
        compiler_params=_params(("arbitrary", "arbitrary")),
    )(idx, g4, landed)


_HBM = pl.BlockSpec(memory_space=pltpu.HBM)
_SEM = pl.BlockSpec(memory_space=pltpu.SEMAPHORE)
_EFFECT = pltpu.SideEffectType.DATAFLOW_SIDE_EFFECTING


def _in_hbm(a):
    return pltpu.with_memory_space_constraint(a, pltpu.HBM)


def _split_copies(n, per, make):
    def start(nbuf, name, bufs):
        def body(*refs):
            ins = refs[:nbuf]
            send_sems, recv_sems = refs[nbuf], refs[nbuf + 1]
            token = refs[-1]
            for t in range(n):
                for j in range(per):
                    make(ins, t, j, send_sems.at[per * t + j], recv_sems.at[per * t + j]).start()
            token[...] = jnp.zeros(token.shape, token.dtype)

        out = pl.pallas_call(
            body, name=name,
            out_shape=(pltpu.SemaphoreType.DMA((per * n,)), pltpu.SemaphoreType.DMA((per * n,)),
                       *[pltpu.HBM(b.shape, b.dtype) for b in bufs], jax.ShapeDtypeStruct((8, 128), F32)),
            in_specs=[_HBM] * nbuf,
            out_specs=(_SEM, _SEM, *[_HBM] * nbuf, pl.BlockSpec(memory_space=pltpu.VMEM)),
            input_output_aliases={i: 2 + i for i in range(nbuf)},
            compiler_params=pltpu.CompilerParams(has_side_effects=_EFFECT),
        )(*[_in_hbm(b) for b in bufs])
        return out[0], out[1], list(out[2:2 + nbuf]), out[-1]

    def wait(nbuf, name, send_sems, recv_sems, bufs, after):
        def body(*refs):
            ins = refs[:nbuf]
            ss, rs = refs[nbuf], refs[nbuf + 1]
            for t in range(n):
                for j in range(per):
                    cp = make(ins, t, j, ss.at[per * t + j], rs.at[per * t + j])
                    cp.wait_send()
                    cp.wait_recv()

        return pl.pallas_call(
            body, name=name,
            out_shape=[pltpu.HBM(b.shape, b.dtype) for b in bufs],
            in_specs=[_HBM] * nbuf + [_SEM, _SEM, pl.BlockSpec(memory_space=pl.ANY)],
            out_specs=[_HBM] * nbuf,
            input_output_aliases={i: i for i in range(nbuf)},
            compiler_params=pltpu.CompilerParams(has_side_effects=_EFFECT),
        )(*bufs, send_sems, recv_sems, after)

    return start, wait


def _chip_exchange_split(n):
    def make(bufs, t, j, send_sem, recv_sem):
        x, y, c = _place()
        chip = _other_chips(x, y)[j]
        return pltpu.make_async_remote_copy(src_ref=bufs[t].at[1 + j], dst_ref=bufs[n + t].at[j], send_sem=send_sem,
                                            recv_sem=recv_sem, device_id=(*chip, c), device_id_type=MESH)

    return _split_copies(n, 3, make)


def _weights_gather_split(fulls):
    def make(bufs, t, j, send_sem, recv_sem):
        x, y, c = _place()
        chip = _other_chips(x, y)[j]
        rs = fulls[t].shape[0] // N_CHIPS
        rows = bufs[t].at[pl.ds((2 * x + y) * rs + c * (rs // 2), rs // 2), :]
        return pltpu.make_async_remote_copy(src_ref=rows, dst_ref=rows, send_sem=send_sem, recv_sem=recv_sem,
                                            device_id=(*chip, c), device_id_type=MESH)

    return _split_copies(len(fulls), 3, make)


def _weights_pass_on(fulls, name):
    n = len(fulls)

    def body(*refs):
        full = refs[n:2 * n]
        send_sems, recv_sems = refs[2 * n:]
        x, y, c = _place()
        chips = _other_chips(x, y)

        def copy(t, j, h):
            rs = fulls[t].shape[0] // N_CHIPS
            px, py = chips[j]
            rows = full[t].at[pl.ds((2 * px + py) * rs + h * (rs // 2), rs // 2), :]
            return pltpu.make_async_remote_copy(src_ref=rows, dst_ref=rows, send_sem=send_sems.at[3 * t + j],
                                                recv_sem=recv_sems.at[3 * t + j], device_id=(x, y, 1 - c),
                                                device_id_type=MESH)

        for t in range(n):
            for j in range(3):
                copy(t, j, c).start()
        for t in range(n):
            for j in range(3):
                copy(t, j, 1 - c).wait_recv()
        for t in range(n):
            for j in range(3):
                copy(t, j, c).wait_send()

    return pl.pallas_call(
        body, name=name,
        out_shape=[jax.ShapeDtypeStruct(f.shape, f.dtype) for f in fulls],
        in_specs=_any_specs(n), out_specs=_any_specs(n),
        input_output_aliases={t: t for t in range(n)},
        scratch_shapes=[pltpu.SemaphoreType.DMA((3 * n,)), pltpu.SemaphoreType.DMA((3 * n,))],
    )(*fulls)


def _after(value, token, name):
    def body(v_ref, t_ref, o_ref):
        o_ref[...] = v_ref[...]

    return pl.pallas_call(
        body, name=name, out_shape=jax.ShapeDtypeStruct(value.shape, value.dtype),
        in_specs=_whole(2), out_specs=pl.BlockSpec(memory_space=pltpu.VMEM),
    )(value, token)


def _chip_sum(ps, landed, idx, name):
    _, half, D = ps.shape
    tr = _pick(half, (416, 352, 128))
    steps = half // tr

    def body(idx_ref, p_ref, l_ref, o_ref):
        acc = p_ref[0].astype(F32)
        for j in range(3):
            acc = acc + l_ref[j].astype(F32)
        o_ref[...] = acc

    return pl.pallas_call(
        body, name=name,
        grid_spec=pltpu.PrefetchScalarGridSpec(
            num_scalar_prefetch=1, grid=(steps,),
            in_specs=[pl.BlockSpec((1, tr, D), lambda i, idx: (0, i, 0)),
                      pl.BlockSpec((3, tr, D), lambda i, idx: (0, i, 0))],
            out_specs=pl.BlockSpec((tr, D), lambda i, idx: (idx[0] * steps + i, 0))),
        out_shape=jax.ShapeDtypeStruct((2 * half, D), F32),
        compiler_params=_params(("arbitrary",)),
    )(idx, ps, landed)


def _pair_swap(shards, name):
    n = len(shards)

    def body(*refs):
        full = refs[n:2 * n]
        send_sems, recv_sems = refs[2 * n:]
        x, y, c = _place()

        def half(t, h):
            rows = shards[t].shape[0] // 2
            return full[t].at[pl.ds(h * rows, rows), :]

        def copy(t, h):
            return pltpu.make_async_remote_copy(src_ref=half(t, h), dst_ref=half(t, h), send_sem=send_sems.at[t],
                                                recv_sem=recv_sems.at[t], device_id=(x, y, 1 - c),
                                                device_id_type=MESH)

        for t in range(n):
            copy(t, c).start()
        for t in range(n):
            copy(t, 1 - c).wait_recv()
        for t in range(n):
            copy(t, c).wait_send()

    return pl.pallas_call(
        body, name=name,
        out_shape=[jax.ShapeDtypeStruct(a.shape, a.dtype) for a in shards],
        in_specs=_any_specs(n), out_specs=_any_specs(n),
        input_output_aliases={t: t for t in range(n)},
        scratch_shapes=[pltpu.SemaphoreType.DMA((n,)), pltpu.SemaphoreType.DMA((n,))],
    )(*shards)


def _gather_begin(fulls, tag):
    start, wait = _weights_gather_split(fulls)
    send_sems, recv_sems, bufs, token = start(len(fulls), f"ag_{tag}_start", fulls)
    return (wait, send_sems, recv_sems, bufs), token


def _gather_end(state, after, tag):
    wait, send_sems, recv_sems, bufs = state
    landed = wait(len(bufs), f"ag_{tag}_wait", send_sems, recv_sems, bufs, after)
    return _weights_pass_on(landed, f"ag_{tag}_pass_on")


class _Exchanges:
    def __init__(self, fulls_rest, idx):
        self.idx = idx
        self._rest, self.token = _gather_begin(fulls_rest, "rest")
        self._early = []

    def rest_weights(self, after):
        return _gather_end(self._rest, after, "rest")

    def _pair_sums(self, grads, tag):
        landed = _pair_exchange(grads, "rs_pair_exchange_" + tag)
        return [_pair_sum(g, l, self.idx, f"rs_pair_sum_{tag}{t}") for t, (g, l) in enumerate(zip(grads, landed))]

    def reduce_early(self, grads, tag):
        sums = self._pair_sums(grads, tag)
        zones = [lax.empty((3,) + s.shape[1:], s.dtype) for s in sums]
        start, wait = _chip_exchange_split(len(sums))
        send_sems, recv_sems, bufs, token = start(2 * len(sums), "rs_chip_start_" + tag, sums + zones)
        self._early.append((tag, wait, send_sems, recv_sems, bufs))
        self.last_token = token
        return token

    def finish_early(self, after):
        halves = []
        for tag, wait, send_sems, recv_sems, bufs in self._early:
            n = len(bufs) // 2
            done = wait(len(bufs), "rs_chip_wait_" + tag, send_sems, recv_sems, bufs, after)
            halves += [_chip_sum(p, l, self.idx, f"rs_chip_sum_{tag}{t}")
                       for t, (p, l) in enumerate(zip(done[:n], done[n:]))]
        return halves


N_MOD = 9
PACK_HEAD, PACK_N3, PACK_N2, PACK_N1, PACK_CONV, PACK_QK = 0, 16, 32, 48, 64, 80
PACK_ROWS = 96
MOD_SRC = ((PACK_N1, 0), (PACK_N1, 1), (PACK_N2, 3), (PACK_N2, 0), (PACK_N2, 1),
           (PACK_N3, 3), (PACK_N3, 0), (PACK_N3, 1), (PACK_HEAD, 2))
CTX_ROW = 8


def _silu(v):
    return v * jax.nn.sigmoid(v)


def _whole(n):
    return [pl.BlockSpec(memory_space=pltpu.VMEM)] * n


def _mod_rows(cin, w_sh, b_sh, name):
    def body(c_ref, w_ref, b_ref, o_ref):
        a = _silu(c_ref[...]).astype(BF16)
        o_ref[...] = jnp.dot(a, w_ref[...].astype(BF16), preferred_element_type=F32) + b_ref[...]

    return pl.pallas_call(
        body, name=name, out_shape=jax.ShapeDtypeStruct((cin.shape[0], w_sh.shape[1]), F32),
        in_specs=_whole(3), out_specs=pl.BlockSpec(memory_space=pltpu.VMEM),
        compiler_params=pltpu.CompilerParams(vmem_limit_bytes=VMEM_LIMIT),
    )(cin, w_sh, b_sh)


def _small_reduce(gathered, name):
    _, _, D = gathered.shape

    def body(g_ref, loss_ref, db_ref, gn_ref, cv_ref, qk_ref, dm_ref):
        tot = g_ref[0]
        for r in range(1, N_DEV):
            tot = tot + g_ref[r]

        def both(block, row):
            return tot[block + row:block + row + 1, :] + tot[block + 8 + row:block + 8 + row + 1, :]

        loss = jnp.sum(both(PACK_HEAD, 0), axis=1, keepdims=True)
        loss_ref[...] = jnp.broadcast_to(loss, loss_ref.shape)
        db_ref[...] = jnp.zeros(db_ref.shape, F32)
        dm_ref[...] = jnp.zeros(dm_ref.shape, F32)
        for j, (block, row) in enumerate(MOD_SRC):
            db_ref[j:j + 1, :] = both(block, row)
            dm_ref[CTX_ROW, j:j + 1, :] = tot[block + row:block + row + 1, :]
            for r in range(N_DEV):
                dm_ref[r, j:j + 1, :] = g_ref[r, block + 8 + row:block + 8 + row + 1, :]
        gn_ref[...] = jnp.zeros(gn_ref.shape, F32)
        gn_ref[0:1, :] = both(PACK_N1, 2)
        gn_ref[8:9, :] = both(PACK_N2, 2)
        gn_ref[16:17, :] = both(PACK_N3, 2)
        gn_ref[24:25, :] = both(PACK_HEAD, 1)
        cv_ref[...] = jnp.zeros(cv_ref.shape, F32)
        for r in range(3):
            cv_ref[r:r + 1, :] = both(PACK_CONV, r)
        qk_ref[...] = jnp.zeros(qk_ref.shape, F32)
        qk_ref[0:1, 0:HEAD_DIM] = both(PACK_QK, 0)[:, 0:HEAD_DIM]
        qk_ref[0:1, HEAD_DIM:2 * HEAD_DIM] = both(PACK_QK, 1)[:, 0:HEAD_DIM]

    return pl.pallas_call(
        body, name=name,
        out_shape=[jax.ShapeDtypeStruct((8, 128), F32), jax.ShapeDtypeStruct((16, D), F32),
                   jax.ShapeDtypeStruct((32, D), F32), jax.ShapeDtypeStruct((8, D), F32),
                   jax.ShapeDtypeStruct((8, D), F32), jax.ShapeDtypeStruct((16, 16, D), F32)],
        in_specs=_whole(1), out_specs=_whole(6),
        compiler_params=pltpu.CompilerParams(vmem_limit_bytes=VMEM_LIMIT),
    )(gathered)


def _wmod_grad(cin, dm_sh, w_sh, name):
    def body(c_ref, d_ref, w_ref, gw_ref, cp_ref):
        a = _silu(c_ref[...]).astype(BF16)
        d = d_ref[...].astype(BF16)
        gw_ref[...] = lax.dot_general(a, d, (((0,), (0,)), ((), ())), preferred_element_type=F32)
        cp_ref[...] = lax.dot_general(d, w_ref[...].astype(BF16), (((1,), (1,)), ((), ())),
                                      preferred_element_type=F32)

    return pl.pallas_call(
        body, name=name,
        out_shape=[jax.ShapeDtypeStruct(w_sh.shape, F32), jax.ShapeDtypeStruct(cin.shape, F32)],
        in_specs=_whole(3), out_specs=_whole(2),
        compiler_params=pltpu.CompilerParams(vmem_limit_bytes=VMEM_LIMIT),
    )(cin, dm_sh, w_sh)


def _cctx_grad(parts, c_ctx8, name):
    def body(p_ref, c_ref, o_ref):
        tot = p_ref[0] + p_ref[2] + p_ref[4] + p_ref[6]
        cv = c_ref[...]
        sig = jax.nn.sigmoid(cv)
        rows = lax.broadcasted_iota(jnp.int32, tot.shape, 0)
        o_ref[...] = jnp.where(rows == 0, tot * (sig * (1.0 + cv * (1.0 - sig))), 0.0)

    return pl.pallas_call(
        body, name=name, out_shape=jax.ShapeDtypeStruct(c_ctx8.shape, F32),
        in_specs=_whole(2), out_specs=pl.BlockSpec(memory_space=pltpu.VMEM),
    )(parts, c_ctx8)


def _pad_rows(a, rows):
    return jnp.pad(a, ((0, rows - a.shape[0]), (0, 0)))


def _pack_small(c_ctx, b_mod, n1, n2, n3, final_g, gq, gk, conv_sh, D):
    misc = jnp.concatenate([gq, gk, conv_sh.reshape(1, -1)], axis=1)
    return jnp.concatenate([_pad_rows(c_ctx[None], 8), _pad_rows(b_mod.reshape(N_MOD, D), 16), _pad_rows(n1, 8),
                            _pad_rows(n2, 8), _pad_rows(n3, 8), _pad_rows(final_g[None], 8), _pad_rows(misc, 8)], axis=0)


def _unpack_small(p, D, conv_shape):
    misc = p[56:57]
    return dict(c_ctx=p[0], b_mod=p[8:8 + N_MOD].reshape(1, N_MOD * D), norm1_g=p[24:25], norm2_g=p[32:33],
                norm3_g=p[40:41], final_g=p[48], q_norm_g=misc[:, 0:HEAD_DIM], k_norm_g=misc[:, HEAD_DIM:2 * HEAD_DIM],
                conv_w=misc[:, 2 * HEAD_DIM:].reshape(conv_shape))


WEIGHT_ORDER = ("c_ctx", "w_mod", "b_mod", "norm1_g", "norm2_g", "norm3_g", "ffn1_w_in", "ffn1_w_out", "w_in",
                "conv_w", "q_norm_g", "k_norm_g", "w_branch_conv", "w_branch_attn", "w_out", "ffn2_w_in",
                "ffn2_w_out", "final_g")
BIG = ("ffn1_w_in", "ffn1_w_out", "w_in", "w_branch_conv", "w_branch_attn", "w_out", "ffn2_w_in", "ffn2_w_out")
COLUMN_SHARDED = ("ffn1_w_in", "w_in", "ffn2_w_in")


def kernel(x, c, ctx, c_ctx, w_mod, b_mod, norm1_g, norm2_g, norm3_g, ffn1_w_in, ffn1_w_out, w_in, conv_w, q_norm_g, k_norm_g, w_branch_conv, w_branch_attn, w_out, ffn2_w_in, ffn2_w_out, final_g, loss_target, m_c_ctx, m_w_mod, m_b_mod, m_norm1_g, m_norm2_g, m_norm3_g, m_ffn1_w_in, m_ffn1_w_out, m_w_in, m_conv_w, m_q_norm_g, m_k_norm_g, m_w_branch_conv, m_w_branch_attn, m_w_out, m_ffn2_w_in, m_ffn2_w_out, m_final_g, v_c_ctx, v_w_mod, v_b_mod, v_norm1_g, v_norm2_g, v_norm3_g, v_ffn1_w_in, v_ffn1_w_out, v_w_in, v_conv_w, v_q_norm_g, v_k_norm_g, v_w_branch_conv, v_w_branch_attn, v_w_out, v_ffn2_w_in, v_ffn2_w_out, v_final_g):
    w = dict(c_ctx=c_ctx, w_mod=w_mod, b_mod=b_mod, norm1_g=norm1_g, norm2_g=norm2_g, norm3_g=norm3_g,
             ffn1_w_in=ffn1_w_in, ffn1_w_out=ffn1_w_out, w_in=w_in, conv_w=conv_w, q_norm_g=q_norm_g,
             k_norm_g=k_norm_g, w_branch_conv=w_branch_conv, w_branch_attn=w_branch_attn, w_out=w_out,
             ffn2_w_in=ffn2_w_in, ffn2_w_out=ffn2_w_out, final_g=final_g)
    m = dict(c_ctx=m_c_ctx, w_mod=m_w_mod, b_mod=m_b_mod, norm1_g=m_norm1_g, norm2_g=m_norm2_g, norm3_g=m_norm3_g,
             ffn1_w_in=m_ffn1_w_in, ffn1_w_out=m_ffn1_w_out, w_in=m_w_in, conv_w=m_conv_w, q_norm_g=m_q_norm_g,
             k_norm_g=m_k_norm_g, w_branch_conv=m_w_branch_conv, w_branch_attn=m_w_branch_attn, w_out=m_w_out,
             ffn2_w_in=m_ffn2_w_in, ffn2_w_out=m_ffn2_w_out, final_g=m_final_g)
    v = dict(c_ctx=v_c_ctx, w_mod=v_w_mod, b_mod=v_b_mod, norm1_g=v_norm1_g, norm2_g=v_norm2_g, norm3_g=v_norm3_g,
             ffn1_w_in=v_ffn1_w_in, ffn1_w_out=v_ffn1_w_out, w_in=v_w_in, conv_w=v_conv_w, q_norm_g=v_q_norm_g,
             k_norm_g=v_k_norm_g, w_branch_conv=v_w_branch_conv, w_branch_attn=v_w_branch_attn, w_out=v_w_out,
             ffn2_w_in=v_ffn2_w_in, ffn2_w_out=v_ffn2_w_out, final_g=v_final_g)

    xi, yi, ci = _place()
    dev = 4 * xi + 2 * yi + ci
    shard = 2 * xi + yi
    idx = jnp.stack([ci, shard, 2 * (1 - xi) + yi, 2 * xi + (1 - yi), 2 * (1 - xi) + (1 - yi)]).astype(jnp.int32)
    D = x.shape[-1]
    ctx_len = ctx.shape[1]
    assert ctx_len == ROW and c.shape == (1, D)
    mcols = w_mod.shape[2]
    ccols = conv_w.shape[2]

    c_all = _allgather8(jnp.broadcast_to(c, (8, D)), "ag_c")[:, 0, :]
    cin = jnp.concatenate([c_all, _pad_rows(c_ctx[None], 8)], axis=0)
    b_sh = lax.dynamic_slice(b_mod, (0, shard * mcols), (1, mcols))
    mod_sh = _mod_rows(cin, w_mod[0], b_sh, "mod_rows")
    conv_rows = jnp.pad(conv_w[0], ((0, 8 - conv_w.shape[1]), (0, mcols - ccols)))
    mod_all = _allgather8(jnp.concatenate([mod_sh, conv_rows], axis=0), "ag_mod")
    mod_full = jnp.concatenate([mod_all[2 * s, :16] for s in range(N_CHIPS)], axis=1)
    conv_full = jnp.concatenate([mod_all[2 * s, 16:16 + conv_w.shape[1], :ccols] for s in range(N_CHIPS)], axis=1)
    mod_lat = lax.dynamic_slice(mod_full, (dev, 0), (1, N_MOD * D)).reshape(N_MOD, D)
    mod_ctx = mod_full[CTX_ROW].reshape(N_MOD, D)
    mods = jnp.stack([_pad_rows(mod_ctx, 16), _pad_rows(mod_lat, 16)])

    def place(names, token):
        return [_place_shard(w[n][0], idx, n in COLUMN_SHARDED, "place_" + n, token) for n in names]

    ffn1_gather, ffn1_token = _gather_begin(place(BIG[:2], mod_all[0, :8, :HEAD_DIM]), "ffn1")
    fulls_rest = place(BIG[2:], ffn1_token)
    ffn1_w = _gather_end(ffn1_gather, fulls_rest[-1][:16, :HEAD_DIM], "ffn1")
    hooks = _Exchanges(fulls_rest, idx)

    xcat = jnp.concatenate([ctx[0], x[0]], axis=0)
    norm1_first = _after(norm1_g, hooks.token, "after_ag_rest")
    grad_x, _, accs = _local_step(xcat, loss_target[0], mods, (norm1_first, norm2_g, norm3_g), final_g[None],
                                  q_norm_g, k_norm_g, conv_full, ffn1_w, hooks, ctx_len)
    g = {}

    pack = jnp.concatenate([a.reshape(2 * ACC_ROWS, D) for a in accs], axis=0)
    pack = _after(pack, hooks.last_token, "after_rs_d")
    gathered = _allgather8(pack, "ag_small")
    loss8, db_mod, g_norms, g_conv, g_qk, dm = _small_reduce(gathered, "small_reduce")
    dm_sh = lax.dynamic_slice(dm[:, :N_MOD, :].reshape(16, N_MOD * D), (0, shard * mcols), (16, mcols))
    g_wmod, cpart = _wmod_grad(cin, dm_sh, w_mod[0], "wmod_grad")
    g["w_mod"] = g_wmod[None]
    cparts = _allgather8(cpart[CTX_ROW:CTX_ROW + 8], "ag_cctx")
    g_cctx = _cctx_grad(cparts, _pad_rows(c_ctx[None], 8), "cctx_grad")
    g_conv_sh = lax.dynamic_slice(g_conv, (0, shard * ccols), (conv_w.shape[1], ccols))
    g_misc = jnp.concatenate([g_qk[0:1, 0:2 * HEAD_DIM], g_conv_sh.reshape(1, -1)], axis=1)
    g_pack = jnp.concatenate([g_cctx, db_mod, g_norms, _pad_rows(g_misc, 8)], axis=0)

    def packed(p):
        return _pack_small(p["c_ctx"], p["b_mod"], p["norm1_g"], p["norm2_g"], p["norm3_g"], p["final_g"],
                           p["q_norm_g"], p["k_norm_g"], p["conv_w"][0], D)

    d_pack, m_pack, v_pack = _adamw(packed(w), g_pack, packed(m), packed(v), "adamw_small")

    h_wbc, h_wba, h_wo, h_w2i, h_w2o, h_wi, h_w1o, h_w1i = hooks.finish_early(d_pack)
    halves = [h_w1i, h_w1o, h_wi, h_wbc, h_wba, h_wo, h_w2i, h_w2o]
    reduced = dict(zip(BIG, _pair_swap(halves, "rs_pair_swap")))
    g.update(_unpack_small(g_pack, D, conv_w.shape))
    delta = _unpack_small(d_pack, D, conv_w.shape)
    new_m = _unpack_small(m_pack, D, conv_w.shape)
    new_v = _unpack_small(v_pack, D, conv_w.shape)
    for n in BIG + ("w_mod",):
        if n in COLUMN_SHARDED:
            g2, d2, m2, v2 = _adamw_transposed(w[n][0], reduced[n], m[n][0], v[n][0], "adamw_" + n)
        else:
            g2 = reduced[n] if n in reduced else g[n][0]
            d2, m2, v2 = _adamw(w[n][0], g2, m[n][0], v[n][0], "adamw_" + n)
        g[n], delta[n], new_m[n], new_v[n] = g2[None], d2[None], m2[None], v2[None]

    loss = loss8[0, 0]
    return (loss, grad_x[None], *[g[n] for n in WEIGHT_ORDER], *[delta[n] for n in WEIGHT_ORDER],
            *[new_m[n] for n in WEIGHT_ORDER], *[new_v[n] for n in WEIGHT_ORDER])
```

```python
import functools

import jax
import jax.numpy as jnp
from jax import lax
from jax.experimental import pallas as pl
from jax.experimental.pallas import tpu as pltpu

F32 = jnp.float32
BF16 = jnp.bfloat16

HEAD_DIM = 128
N_Q_HEADS = 8
N_KV_HEADS = 2
GROUP = N_Q_HEADS // N_KV_HEADS
GRID_W = 64
ROPE_THETA = 10000.0
EPS = 1e-6
ATTN_SCALE = HEAD_DIM ** -0.5

ADAM_LR = 0.001
ADAM_B1 = 0.9
ADAM_B2 = 0.999
ADAM_EPS = 1e-08
ADAM_WD = 0.01
ADAM_STEP = 10

ROW = 256
HALO = 16
ACC_ROWS = 8
N_CHIPS = 4
N_DEV = 8
MESH = pl.DeviceIdType.MESH
VMEM_LIMIT = 48 * 1024 * 1024
ADAMW_BLOCK_BYTES = 1024 * 1024


def _pick(n, prefs):
    for p in prefs:
        if n % p == 0:
            return p
    return n


def _params(sem):
    return pltpu.CompilerParams(dimension_semantics=sem, vmem_limit_bytes=VMEM_LIMIT)


def _stream(i):
    return jnp.minimum(i, 1)


def _matmul(a, b, mode, out_dtype, name, tm=None, tn=None, tk=None, token=None):
    if mode == "nn":
        (M, K), (K2, N) = a.shape, b.shape
    elif mode == "nt":
        (M, K), (N, K2) = a.shape, b.shape
    else:
        (K, M), (K2, N) = a.shape, b.shape
    assert K == K2, (a.shape, b.shape, mode)
    tm = tm or _pick(M, (1664, 1408, 1024, 512, 256, 128) if mode == "tn" else (1408, 768, 512, 256, 128))
    tn = tn or _pick(N, (1664, 1408, 1024, 512, 256, 128))
    tk = tk or _pick(K, (1664, 1408, 1024, 768, 512, 256, 128))
    nk = K // tk
    if mode == "tn":
        a_spec = pl.BlockSpec((tk, tm), lambda i, j, k: (k, i))
    else:
        a_spec = pl.BlockSpec((tm, tk), lambda i, j, k: (i, k))
    if mode == "nt":
        b_spec = pl.BlockSpec((tn, tk), lambda i, j, k: (j, k))
    else:
        b_spec = pl.BlockSpec((tk, tn), lambda i, j, k: (k, j))
    dims = {"nn": ((1,), (0,)), "nt": ((1,), (1,)), "tn": ((0,), (0,))}[mode]
    use_scratch = nk > 1 and out_dtype != F32

    extra = [] if token is None else [token]

    def body(a_ref, b_ref, *rest):
        o_ref, scratch = rest[len(extra)], rest[len(extra) + 1:]
        p = lax.dot_general(a_ref[...].astype(BF16), b_ref[...].astype(BF16), (dims, ((), ())),
                            preferred_element_type=F32)
        if nk == 1:
            o_ref[...] = p.astype(o_ref.dtype)
            return
        acc_ref = scratch[0] if use_scratch else o_ref
        k = pl.program_id(2)

        @pl.when(k == 0)
        def _():
            acc_ref[...] = p

        @pl.when(k > 0)
        def _():
            acc_ref[...] += p

        if use_scratch:
            @pl.when(k == nk - 1)
            def _():
                o_ref[...] = acc_ref[...].astype(o_ref.dtype)

    return pl.pallas_call(
        body, name=name,
        grid=(M // tm, N // tn, nk),
        in_specs=[a_spec, b_spec] + [pl.BlockSpec(t.shape, lambda i, j, k: (0, 0)) for t in extra],
        out_specs=pl.BlockSpec((tm, tn), lambda i, j, k: (i, j)),
        out_shape=jax.ShapeDtypeStruct((M, N), out_dtype),
        scratch_shapes=[pltpu.VMEM((tm, tn), F32)] if use_scratch else [],
        compiler_params=_params(("parallel", "parallel", "arbitrary")),
    )(a, b, *extra)


def _row_spec(width, col=0):
    return pl.BlockSpec((ROW, width), lambda i, col=col: (i, col))


def _mods_spec(D):
    return pl.BlockSpec((1, 16, D), lambda i: (_stream(i), 0, 0))


def _acc_spec(D):
    return pl.BlockSpec((1, ACC_ROWS, D), lambda i: (_stream(i), 0, 0))


def _vec_spec(rows, D):
    return pl.BlockSpec((rows, D), lambda i: (0, 0))


def _acc_init(acc_ref):
    i = pl.program_id(0)

    @pl.when(i <= 1)
    def _():
        acc_ref[...] = jnp.zeros_like(acc_ref)


def _acc_add(acc_ref, row, val):
    acc_ref[0, row:row + 1, :] += jnp.sum(val, axis=0, keepdims=True)


def _rows_operand(x):
    if not isinstance(x, tuple):
        return [_row_spec(x.shape[1])], [x], x.shape
    ctx, lat = x
    D = lat.shape[1]
    assert ctx.shape == (ROW, D)
    specs = [pl.BlockSpec((ROW, D), lambda i: (0, 0)), pl.BlockSpec((ROW, D), lambda i: (jnp.maximum(i - 1, 0), 0))]
    return specs, [ctx, lat], (ROW + lat.shape[0], D)


def _rows_tile(refs):
    if len(refs) == 1:
        return refs[0][...]
    return jnp.where(pl.program_id(0) == 0, refs[0][...], refs[1][...])


def _norm_tile_fwd(x, m, g, shift_idx, scale_idx):
    inv = lax.rsqrt(jnp.mean(x * x, axis=-1, keepdims=True) + EPS)
    y = (x * inv) * g
    return (y * (1.0 + m[scale_idx:scale_idx + 1, :]) + m[shift_idx:shift_idx + 1, :]).astype(BF16)


def _norm_tile_bwd(x, dh, dres, m, g, shift_idx, scale_idx, acc_ref):
    inv = lax.rsqrt(jnp.mean(x * x, axis=-1, keepdims=True) + EPS)
    xn = x * inv
    dy = dh * (1.0 + m[scale_idx:scale_idx + 1, :])
    dxn = dy * g
    _acc_add(acc_ref, 0, dh)
    _acc_add(acc_ref, 1, dh * (xn * g))
    _acc_add(acc_ref, 2, dy * xn)
    return inv * (dxn - xn * jnp.mean(dxn * xn, axis=-1, keepdims=True)) + dres


def _gate_tile_bwd(dx, branch, m, gate, acc_ref):
    gate_idx, fac = gate
    _acc_add(acc_ref, 3, fac * dx * branch)
    return ((fac * m[gate_idx:gate_idx + 1, :]) * dx).astype(BF16)


_NT = (((1,), (1,)), ((), ()))


def _ffn_chunk(F):
    return _pick(F, (1408, 512, 256, 128))


def _resident():
    return pl.BlockSpec(memory_space=pltpu.VMEM)


def _ffn_tile_fwd(hv, wi_ref, wo_ref, u_ref, s_ref, F, cw):
    acc = jnp.zeros((hv.shape[0], wo_ref.shape[1]), F32)
    for j in range(F // cw):
        a = lax.dot_general(hv, wi_ref[j * cw:(j + 1) * cw, :], _NT, preferred_element_type=F32)
        b = lax.dot_general(hv, wi_ref[F + j * cw:F + (j + 1) * cw, :], _NT, preferred_element_type=F32)
        s = ((a * jax.nn.sigmoid(a)) * b).astype(BF16)
        u_ref[:, j * cw:(j + 1) * cw] = a.astype(BF16)
        u_ref[:, F + j * cw:F + (j + 1) * cw] = b.astype(BF16)
        s_ref[:, j * cw:(j + 1) * cw] = s
        acc = acc + jnp.dot(s, wo_ref[j * cw:(j + 1) * cw, :], preferred_element_type=F32)
    return acc


def _norm_ffn_fwd(xprev, branch, mods, g, gate, shift_idx, scale_idx, w_in_t, w_out, name, head=None):
    x_specs, x_args, (T, D) = _rows_operand(xprev)
    F = w_out.shape[0]
    cw = _ffn_chunk(F)
    has_res = branch is not None
    n_in = len(x_args) + int(has_res) + 4 + (2 if head else 0)

    def body(*refs):
        ins, outs = list(refs[:n_in]), list(refs[n_in:])
        x = _rows_tile([ins.pop(0) for _ in x_args])
        f_ref = ins.pop(0) if has_res else None
        m_ref, g_ref, wi_ref, wo_ref = ins[:4]
        xo_ref = outs.pop(0) if has_res else None
        h_ref, u_ref, s_ref = outs[:3]
        m = m_ref[0]
        if has_res:
            gate_idx, fac = gate
            x = x + (fac * m[gate_idx:gate_idx + 1, :]) * f_ref[...]
            xo_ref[...] = x
        hv = _norm_tile_fwd(x, m, g_ref[...], shift_idx, scale_idx)
        h_ref[...] = hv
        f = _ffn_tile_fwd(hv, wi_ref, wo_ref, u_ref, s_ref, F, cw)
        if head is None:
            outs[3][...] = f
            return
        fg_ref, t_ref = ins[4:6]
        dx_ref, df_ref, acc_ref = outs[3:6]
        _acc_init(acc_ref)
        lat = (pl.program_id(0) > 0).astype(F32)
        gate8 = 0.5 * m[8:9, :]
        x3 = x + gate8 * f
        inv3 = lax.rsqrt(jnp.mean(x3 * x3, axis=-1, keepdims=True) + EPS)
        xn = x3 * inv3
        fg = fg_ref[...]
        e = (xn * fg - t_ref[...]) * lat
        dy = e * (1.0 / D)
        dxn = dy * fg
        dx = inv3 * (dxn - xn * jnp.mean(dxn * xn, axis=-1, keepdims=True))
        dx_ref[...] = dx
        df_ref[...] = (gate8 * dx).astype(BF16)
        _acc_add(acc_ref, 0, (0.5 / D) * e * e)
        _acc_add(acc_ref, 1, dy * xn)
        _acc_add(acc_ref, 2, 0.5 * dx * f)

    in_specs = x_specs + ([_row_spec(D)] if has_res else []) + \
               [_mods_spec(D), _vec_spec(1, D), _resident(), _resident()]
    args = x_args + ([branch] if has_res else []) + [mods, g, w_in_t, w_out]
    out_specs = ([_row_spec(D)] if has_res else []) + [_row_spec(D), _row_spec(2 * F), _row_spec(F)]
    out_shape = ([jax.ShapeDtypeStruct((T, D), F32)] if has_res else []) + \
                [jax.ShapeDtypeStruct((T, D), BF16), jax.ShapeDtypeStruct((T, 2 * F), BF16),
                 jax.ShapeDtypeStruct((T, F), BF16)]
    if head is None:
        out_specs += [_row_spec(D)]
        out_shape += [jax.ShapeDtypeStruct((T, D), F32)]
    else:
        in_specs += [_vec_spec(1, D), pl.BlockSpec((ROW, D), lambda i: (jnp.maximum(i - 1, 0), 0))]
        args += list(head)
        out_specs += [_row_spec(D), _row_spec(D), _acc_spec(D)]
        out_shape += [jax.ShapeDtypeStruct((T, D), F32), jax.ShapeDtypeStruct((T, D), BF16),
                      jax.ShapeDtypeStruct((2, ACC_ROWS, D), F32)]
    out = pl.pallas_call(
        body, name=name, grid=(T // ROW,), in_specs=in_specs, out_specs=out_specs, out_shape=out_shape,
        compiler_params=_params(("arbitrary",) if head else ("parallel",)),
    )(*args)
    return tuple(out) if has_res else (None,) + tuple(out)


def _ffn_norm_bwd(df, u, w_in_t, w_out, x, dres, mods, g, shift_idx, scale_idx, gate, branch, name,
                  skip_first_tile=False):
    T, D = df.shape
    F = w_out.shape[0]
    cw = _ffn_chunk(F)
    nt = T // ROW
    has_gate = gate is not None
    x_specs, x_args, _ = _rows_operand(x)
    n_in = 7 + len(x_args) + int(has_gate)

    def body(*refs):
        ins, outs = list(refs[:n_in]), list(refs[n_in:])
        df_ref, u_ref, wi_ref, wo_ref = ins[:4]
        x_refs = ins[4:4 + len(x_args)]
        dr_ref = ins[4 + len(x_args)]
        b_ref = ins[5 + len(x_args)] if has_gate else None
        m_ref, g_ref = ins[-2:]
        du_ref, dx_ref = outs[:2]
        db_ref = outs[2] if has_gate else None
        acc_ref = outs[-1]
        _acc_init(acc_ref)
        dfv = df_ref[...]
        dh = jnp.zeros((ROW, D), F32)
        for j in range(F // cw):
            ds = lax.dot_general(dfv, wo_ref[j * cw:(j + 1) * cw, :], _NT, preferred_element_type=F32)
            a = u_ref[:, j * cw:(j + 1) * cw].astype(F32)
            b = u_ref[:, F + j * cw:F + (j + 1) * cw].astype(F32)
            sig = jax.nn.sigmoid(a)
            da = (ds * b * (sig * (1.0 + a * (1.0 - sig)))).astype(BF16)
            db = (ds * (a * sig)).astype(BF16)
            du_ref[:, j * cw:(j + 1) * cw] = da
            du_ref[:, F + j * cw:F + (j + 1) * cw] = db
            dh = dh + jnp.dot(da, wi_ref[j * cw:(j + 1) * cw, :], preferred_element_type=F32)
            dh = dh + jnp.dot(db, wi_ref[F + j * cw:F + (j + 1) * cw, :], preferred_element_type=F32)
        m = m_ref[0]
        dx = _norm_tile_bwd(_rows_tile(x_refs), dh, dr_ref[...], m, g_ref[...], shift_idx, scale_idx, acc_ref)
        dx_ref[...] = dx
        if has_gate:
            db_ref[...] = _gate_tile_bwd(dx, b_ref[...], m, gate, acc_ref)

    in_specs = [_row_spec(D), _row_spec(2 * F), _resident(), _resident()] + x_specs + [_row_spec(D)] + \
               ([_row_spec(D)] if has_gate else []) + [_mods_spec(D), _vec_spec(1, D)]
    args = [df, u, w_in_t, w_out] + x_args + [dres] + ([branch] if has_gate else []) + [mods, g]
    if skip_first_tile:
        dx_spec = pl.BlockSpec((ROW, D), lambda i: (jnp.maximum(i - 1, 0), 0))
        dx_shape = jax.ShapeDtypeStruct((T - ROW, D), F32)
    else:
        dx_spec = _row_spec(D)
        dx_shape = jax.ShapeDtypeStruct((T, D), F32)
    out_specs = [_row_spec(2 * F), dx_spec] + ([_row_spec(D)] if has_gate else []) + [_acc_spec(D)]
    out_shape = [jax.ShapeDtypeStruct((T, 2 * F), BF16), dx_shape] + \
                ([jax.ShapeDtypeStruct((T, D), BF16)] if has_gate else []) + \
                [jax.ShapeDtypeStruct((2, ACC_ROWS, D), F32)]
    out = pl.pallas_call(
        body, name=name, grid=(nt,), in_specs=in_specs, out_specs=out_specs, out_shape=out_shape,
        compiler_params=_params(("arbitrary",)),
    )(*args)
    if has_gate:
        return tuple(out)
    return out[0], out[1], None, out[2]


def _halo_specs(width, col, nt):
    per = ROW // HALO
    prev = pl.BlockSpec((HALO, width), lambda i, col=col: (jnp.maximum(i * per - 1, 0), col))
    nxt = pl.BlockSpec((HALO, width), lambda i, col=col: (jnp.minimum((i + 1) * per, nt * per - 1), col))
    return prev, nxt


def _f32(ref):
    return ref[...].astype(F32)


def _last_row(halo_ref):
    return halo_ref[HALO - 1:HALO, :].astype(F32)


def _first_row(halo_ref):
    return halo_ref[0:1, :].astype(F32)


def _shift_rows(v, prev_row, next_row):
    rows = lax.broadcasted_iota(jnp.int32, v.shape, 0)
    down = jnp.where(rows == 0, prev_row, pltpu.roll(v, 1, 0))
    up = jnp.where(rows == v.shape[0] - 1, next_row, pltpu.roll(v, v.shape[0] - 1, 0))
    return down, up


def _conv_fwd(P, conv_w, D, name):
    T = P.shape[0]
    nt = T // ROW
    cg_p, cg_n = _halo_specs(D, 1, nt)
    vc_p, vc_n = _halo_specs(D, 2, nt)

    def body(bg_ref, cg_ref, vc_ref, cgp_ref, vcp_ref, cgn_ref, vcn_ref, w_ref, y_ref):
        i = pl.program_id(0)
        has_prev = (i != 1).astype(F32)
        has_next = (i != nt - 1).astype(F32)
        u = _f32(cg_ref) * _f32(vc_ref)
        up_row = _last_row(cgp_ref) * _last_row(vcp_ref) * has_prev
        un_row = _first_row(cgn_ref) * _first_row(vcn_ref) * has_next
        um1, up1 = _shift_rows(u, up_row, un_row)
        w = w_ref[...]
        conv = um1 * w[0:1, :] + u * w[1:2, :] + up1 * w[2:3, :]
        y_ref[...] = (_f32(bg_ref) * conv).astype(BF16)

    return pl.pallas_call(
        body, name=name, grid=(nt,),
        in_specs=[_row_spec(D, 0), _row_spec(D, 1), _row_spec(D, 2), cg_p, vc_p, cg_n, vc_n, _vec_spec(3, D)],
        out_specs=_row_spec(D),
        out_shape=jax.ShapeDtypeStruct((T, D), BF16),
        compiler_params=_params(("parallel",)),
    )(P, P, P, P, P, P, P, conv_w)


def _conv_bwd(P, dy, conv_w, D, name):
    T = P.shape[0]
    nt = T // ROW
    bg_p, bg_n = _halo_specs(D, 0, nt)
    cg_p, cg_n = _halo_specs(D, 1, nt)
    vc_p, vc_n = _halo_specs(D, 2, nt)
    dy_p, dy_n = _halo_specs(D, 0, nt)

    def body(bg_ref, cg_ref, vc_ref, dy_ref, bgp_ref, cgp_ref, vcp_ref, dyp_ref,
             bgn_ref, cgn_ref, vcn_ref, dyn_ref, w_ref, o_ref, acc_ref):
        _acc_init(acc_ref)
        i = pl.program_id(0)
        lat = (i > 0).astype(F32)
        has_prev = (i != 1).astype(F32)
        has_next = (i != nt - 1).astype(F32)
        bg = _f32(bg_ref)
        cg = _f32(cg_ref)
        vc = _f32(vc_ref)
        dyv = dy_ref[...] * lat
        u = cg * vc
        up_row = _last_row(cgp_ref) * _last_row(vcp_ref) * has_prev
        un_row = _first_row(cgn_ref) * _first_row(vcn_ref) * has_next
        um1, up1 = _shift_rows(u, up_row, un_row)
        w = w_ref[...]
        conv = um1 * w[0:1, :] + u * w[1:2, :] + up1 * w[2:3, :]
        dc = dyv * bg
        dcp_row = _last_row(dyp_ref) * _last_row(bgp_ref) * has_prev
        dcn_row = _first_row(dyn_ref) * _first_row(bgn_ref) * has_next
        dcm1, dcp1 = _shift_rows(dc, dcp_row, dcn_row)
        du = dcp1 * w[0:1, :] + dc * w[1:2, :] + dcm1 * w[2:3, :]
        o_ref[:, 0:D] = (dyv * conv).astype(BF16)
        o_ref[:, D:2 * D] = (du * vc * lat).astype(BF16)
        o_ref[:, 2 * D:3 * D] = (du * cg * lat).astype(BF16)
        _acc_add(acc_ref, 0, dc * um1)
        _acc_add(acc_ref, 1, dc * u)
        _acc_add(acc_ref, 2, dc * up1)

    return pl.pallas_call(
        body, name=name, grid=(nt,),
        in_specs=[_row_spec(D, 0), _row_spec(D, 1), _row_spec(D, 2), _row_spec(D, 0),
                  bg_p, cg_p, vc_p, dy_p, bg_n, cg_n, vc_n, dy_n, _vec_spec(3, D)],
        out_specs=[_row_spec(3 * D), _acc_spec(D)],
        out_shape=[jax.ShapeDtypeStruct((T, 3 * D), BF16), jax.ShapeDtypeStruct((2, ACC_ROWS, D), F32)],
        compiler_params=_params(("arbitrary",)),
    )(P, P, P, dy, P, P, P, dy, P, P, P, dy, conv_w)


def _rope_tables(ctx_len, seq):
    n_freq = HEAD_DIM // 4
    rows = seq // GRID_W
    inv = ROPE_THETA ** (-jnp.arange(n_freq, dtype=F32) / n_freq)
    ar = jnp.arange(rows, dtype=F32)[:, None] * inv
    ac = jnp.arange(GRID_W, dtype=F32)[:, None] * inv

    def per_row(a):
        return jnp.repeat(a, GRID_W, axis=0)

    def per_col(a):
        return jnp.tile(a, (rows, 1))

    cos_t = jnp.concatenate([per_row(jnp.cos(ar)), per_row(jnp.cos(ar)), per_col(jnp.cos(ac)), per_col(jnp.cos(ac))], axis=1)
    sin_t = jnp.concatenate([per_row(-jnp.sin(ar)), per_row(jnp.sin(ar)), per_col(-jnp.sin(ac)), per_col(jnp.sin(ac))], axis=1)
    cos_t = jnp.concatenate([jnp.ones((ctx_len, HEAD_DIM), F32), cos_t], axis=0)
    sin_t = jnp.concatenate([jnp.zeros((ctx_len, HEAD_DIM), F32), sin_t], axis=0)
    return cos_t, sin_t


def _swap_halves(y):
    lanes = lax.broadcasted_iota(jnp.int32, y.shape, 1)
    first = (lanes % 64) < 32
    return jnp.where(first, pltpu.roll(y, HEAD_DIM - 32, 1), pltpu.roll(y, 32, 1))


def _qk_fwd(P, gq, gk, cos_t, sin_t, D, name):
    T = P.shape[0]
    QW = N_Q_HEADS * HEAD_DIM
    KW = N_KV_HEADS * HEAD_DIM
    q_col = (3 * D) // QW
    k_col = (3 * D + QW) // KW
    v_col = k_col + 1

    def body(q_ref, k_ref, v_ref, gq_ref, gk_ref, c_ref, s_ref, qo_ref, ko_ref, vo_ref):
        c = c_ref[...]
        s = s_ref[...]

        def head(x, g):
            inv = lax.rsqrt(jnp.mean(x * x, axis=-1, keepdims=True) + EPS)
            y = (x * inv) * g
            return y * c + _swap_halves(y) * s

        for h in range(N_Q_HEADS):
            sl = slice(h * HEAD_DIM, (h + 1) * HEAD_DIM)
            qo_ref[:, sl] = head(q_ref[:, sl].astype(F32), gq_ref[...]).astype(BF16)
        for h in range(N_KV_HEADS):
            sl = slice(h * HEAD_DIM, (h + 1) * HEAD_DIM)
            ko_ref[:, sl] = head(k_ref[:, sl].astype(F32), gk_ref[...]).astype(BF16)
        vo_ref[...] = v_ref[...].astype(BF16)

    return pl.pallas_call(
        body, name=name, grid=(T // ROW,),
        in_specs=[_row_spec(QW, q_col), _row_spec(KW, k_col), _row_spec(KW, v_col),
                  _vec_spec(1, HEAD_DIM), _vec_spec(1, HEAD_DIM), _row_spec(HEAD_DIM), _row_spec(HEAD_DIM)],
        out_specs=[_row_spec(QW), _row_spec(KW), _row_spec(KW)],
        out_shape=[jax.ShapeDtypeStruct((T, QW), BF16), jax.ShapeDtypeStruct((T, KW), BF16),
                   jax.ShapeDtypeStruct((T, KW), BF16)],
        compiler_params=_params(("parallel",)),
    )(P, P, P, gq, gk, cos_t, sin_t)


def _qk_bwd(P, dq, dk, dv, gq, gk, cos_t, sin_t, D, name):
    T = P.shape[0]
    QW = N_Q_HEADS * HEAD_DIM
    KW = N_KV_HEADS * HEAD_DIM
    q_col = (3 * D) // QW
    k_col = (3 * D + QW) // KW

    def body(q_ref, k_ref, dq_ref, dk_ref, dv_ref, gq_ref, gk_ref, c_ref, s_ref, o_ref, acc_ref):
        _acc_init(acc_ref)
        c = c_ref[...]
        s = s_ref[...]

        def head(x, d, g):
            dyv = d * c + _swap_halves(d * s)
            inv = lax.rsqrt(jnp.mean(x * x, axis=-1, keepdims=True) + EPS)
            xn = x * inv
            dxn = dyv * g
            dx = inv * (dxn - xn * jnp.mean(dxn * xn, axis=-1, keepdims=True))
            return dx, jnp.sum(dyv * xn, axis=0, keepdims=True)

        dgq = jnp.zeros((1, HEAD_DIM), F32)
        for h in range(N_Q_HEADS):
            sl = slice(h * HEAD_DIM, (h + 1) * HEAD_DIM)
            dx, dg = head(q_ref[:, sl].astype(F32), dq_ref[:, sl], gq_ref[...])
            o_ref[:, sl] = dx.astype(BF16)
            dgq = dgq + dg
        dgk = jnp.zeros((1, HEAD_DIM), F32)
        for h in range(N_KV_HEADS):
            sl = slice(h * HEAD_DIM, (h + 1) * HEAD_DIM)
            dx, dg = head(k_ref[:, sl].astype(F32), dk_ref[:, sl], gk_ref[...])
            o_ref[:, QW + h * HEAD_DIM:QW + (h + 1) * HEAD_DIM] = dx.astype(BF16)
            dgk = dgk + dg
        o_ref[:, QW + KW:QW + 2 * KW] = dv_ref[...].astype(BF16)
        acc_ref[0, 0:1, 0:HEAD_DIM] += dgq
        acc_ref[0, 1:2, 0:HEAD_DIM] += dgk

    return pl.pallas_call(
        body, name=name, grid=(T // ROW,),
        in_specs=[_row_spec(QW, q_col), _row_spec(KW, k_col), _row_spec(QW), _row_spec(KW), _row_spec(KW),
                  _vec_spec(1, HEAD_DIM), _vec_spec(1, HEAD_DIM), _row_spec(HEAD_DIM), _row_spec(HEAD_DIM)],
        out_specs=[_row_spec(QW + 2 * KW), _acc_spec(D)],
        out_shape=[jax.ShapeDtypeStruct((T, QW + 2 * KW), BF16), jax.ShapeDtypeStruct((2, ACC_ROWS, D), F32)],
        compiler_params=_params(("arbitrary",)),
    )(P, P, dq, dk, dv, gq, gk, cos_t, sin_t)


def _to_row(col, n):
    return jnp.transpose(jnp.broadcast_to(col, (n, HEAD_DIM)))[0:1, :]


LOG2E = 1.4426950408889634
ATTN_PARTS = 4


def _flash_fwd(q, k, v, name, tq=ROW, tk=None):
    T = q.shape[0]
    tk = tk or _pick(T, (1408, 768, 512, 256))
    ck = tk
    nk = T // tk
    GW = GROUP * HEAD_DIM

    def body(q_ref, k_ref, v_ref, o_ref, lse_ref, qs_ref, m_ref, l_ref, acc_ref, st_ref):
        ki = pl.program_id(2)

        @pl.when(ki == 0)
        def _():
            for g in range(GROUP):
                qs_ref[g * tq:(g + 1) * tq, :] = q_ref[:, g * HEAD_DIM:(g + 1) * HEAD_DIM]
            m_ref[...] = jnp.full(m_ref.shape, -jnp.inf, F32)
            l_ref[...] = jnp.zeros(l_ref.shape, F32)
            acc_ref[...] = jnp.zeros(acc_ref.shape, F32)

        w = GROUP * tq // ATTN_PARTS
        nck = tk // ck

        def lanes(p):
            return slice(p * w, (p + 1) * w)

        def keys(c):
            return slice(c * ck, (c + 1) * ck)

        def fold(a):
            return a.reshape(ck // 8, 8, w)

        def scores(p, c):
            st = lax.dot_general(k_ref[keys(c), :], qs_ref[lanes(p), :], _NT,
                                 preferred_element_type=F32) * (ATTN_SCALE * LOG2E)
            st_ref[keys(c), lanes(p)] = st
            return jnp.max(fold(st), axis=0)

        def new_max(p, partial):
            m_prev = m_ref[:, lanes(p)]
            m_new = jnp.maximum(m_prev, jnp.max(functools.reduce(jnp.maximum, partial), axis=0, keepdims=True))
            m_ref[:, lanes(p)] = m_new
            return m_new, jnp.exp2(m_prev - m_new)

        def weights(p, c, m_new):
            pt = jnp.exp2(st_ref[keys(c), lanes(p)] - m_new)
            pv = lax.dot_general(v_ref[keys(c), :], pt.astype(BF16), (((0,), (0,)), ((), ())),
                                 preferred_element_type=F32)
            return jnp.sum(fold(pt), axis=0), pv

        partial = [scores(0, c) for c in range(nck)]
        for p in range(ATTN_PARTS):
            m_new, alpha = new_max(p, partial)
            partial, sums, pvs = [], [], []
            for c in range(nck):
                if p + 1 < ATTN_PARTS:
                    partial.append(scores(p + 1, c))
                s8, pv = weights(p, c, m_new)
                sums.append(s8)
                pvs.append(pv)
            l_ref[:, lanes(p)] = alpha * l_ref[:, lanes(p)] + jnp.sum(sum(sums), axis=0, keepdims=True)
            acc_ref[:, lanes(p)] = alpha * acc_ref[:, lanes(p)] + sum(pvs)

        @pl.when(ki == nk - 1)
        def _():
            out = jnp.transpose(acc_ref[...] / l_ref[...])
            lse = m_ref[...] + jnp.log2(l_ref[...])
            for g in range(GROUP):
                o_ref[:, g * HEAD_DIM:(g + 1) * HEAD_DIM] = out[g * tq:(g + 1) * tq, :]
                lse_ref[0, g:g + 1, :] = lse[:, g * tq:(g + 1) * tq]

    return pl.pallas_call(
        body, name=name, grid=(N_KV_HEADS, T // tq, nk),
        in_specs=[pl.BlockSpec((tq, GW), lambda h, i, j: (i, h)),
                  pl.BlockSpec((tk, HEAD_DIM), lambda h, i, j: (j, h)),
                  pl.BlockSpec((tk, HEAD_DIM), lambda h, i, j: (j, h))],
        out_specs=[pl.BlockSpec((tq, GW), lambda h, i, j: (i, h)),
                   pl.BlockSpec((1, GROUP, tq), lambda h, i, j: (h, 0, i))],
        out_shape=[jax.ShapeDtypeStruct((T, N_Q_HEADS * HEAD_DIM), F32),
                   jax.ShapeDtypeStruct((N_KV_HEADS, GROUP, T), F32)],
        scratch_shapes=[pltpu.VMEM((GROUP * tq, HEAD_DIM), BF16), pltpu.VMEM((1, GROUP * tq), F32),
                        pltpu.VMEM((1, GROUP * tq), F32), pltpu.VMEM((HEAD_DIM, GROUP * tq), F32),
                        pltpu.VMEM((tk, GROUP * tq), F32)],
        compiler_params=_params(("parallel", "parallel", "arbitrary")),
    )(q, k, v)


def _attn_delta(do, o, name, token=None):
    T, QW = do.shape
    extra = [] if token is None else [token]

    def body(do_ref, o_ref, *rest):
        dob_ref, dl_ref = rest[len(extra):]
        dov = do_ref[...]
        dob_ref[...] = dov.astype(BF16)
        prod = dov * o_ref[...]
        for h in range(N_Q_HEADS):
            d = jnp.sum(prod[:, h * HEAD_DIM:(h + 1) * HEAD_DIM], axis=1, keepdims=True)
            dl_ref[h // GROUP, (h % GROUP):(h % GROUP) + 1, :] = _to_row(d, ROW)

    return pl.pallas_call(
        body, name=name, grid=(T // ROW,),
        in_specs=[_row_spec(QW), _row_spec(QW)] + [pl.BlockSpec(t.shape, lambda i: (0, 0)) for t in extra],
        out_specs=[_row_spec(QW), pl.BlockSpec((N_KV_HEADS, GROUP, ROW), lambda i: (0, 0, i))],
        out_shape=[jax.ShapeDtypeStruct((T, QW), BF16), jax.ShapeDtypeStruct((N_KV_HEADS, GROUP, T), F32)],
        compiler_params=_params(("parallel",)),
    )(do, o, *extra)


def _flash_bwd(q, k, v, do, lse, delta, name, tq=ROW, tk=None):
    T = q.shape[0]
    tk = tk or _pick(T, (1408, 768, 512, 256))
    nk = T // tk
    GW = GROUP * HEAD_DIM
    nt = (((1,), (1,)), ((), ()))

    def body(q_ref, do_ref, k_ref, v_ref, lse_ref, dl_ref, dq_ref, dk_ref, dv_ref, qs_ref, dos_ref, dqt_ref):
        qi = pl.program_id(1)
        ki = pl.program_id(2)

        @pl.when(ki == 0)
        def _():
            for g in range(GROUP):
                qs_ref[g * tq:(g + 1) * tq, :] = q_ref[:, g * HEAD_DIM:(g + 1) * HEAD_DIM]
                dos_ref[g * tq:(g + 1) * tq, :] = do_ref[:, g * HEAD_DIM:(g + 1) * HEAD_DIM]
            dqt_ref[...] = jnp.zeros(dqt_ref.shape, F32)

        kk = k_ref[...]
        vv = v_ref[...]

        def lanes(p):
            return slice(p * tq, (p + 1) * tq)

        def products(p):
            st = lax.dot_general(kk, qs_ref[lanes(p), :], nt, preferred_element_type=F32)
            dpt = lax.dot_general(vv, dos_ref[lanes(p), :], nt, preferred_element_type=F32)
            return st, dpt

        dk_c = jnp.zeros((tk, HEAD_DIM), F32)
        dv_c = jnp.zeros((tk, HEAD_DIM), F32)
        ahead = products(0)
        for p in range(GROUP):
            st, dpt = ahead
            if p + 1 < GROUP:
                ahead = products(p + 1)
            pt = jnp.exp2(st * (ATTN_SCALE * LOG2E) - lse_ref[0, p:p + 1, :])
            dst = ((pt * (dpt - dl_ref[0, p:p + 1, :])) * ATTN_SCALE).astype(BF16)
            dv_c = dv_c + jnp.dot(pt.astype(BF16), dos_ref[lanes(p), :], preferred_element_type=F32)
            dk_c = dk_c + jnp.dot(dst, qs_ref[lanes(p), :], preferred_element_type=F32)
            dqt_ref[:, lanes(p)] += lax.dot_general(kk, dst, (((0,), (0,)), ((), ())), preferred_element_type=F32)
        rows = pl.ds(pl.multiple_of(ki * tk, tk), tk)

        @pl.when(qi == 0)
        def _():
            dk_ref[rows, :] = dk_c
            dv_ref[rows, :] = dv_c

        @pl.when(qi > 0)
        def _():
            dk_ref[rows, :] += dk_c
            dv_ref[rows, :] += dv_c

        @pl.when(ki == nk - 1)
        def _():
            dqv = jnp.transpose(dqt_ref[...])
            for g in range(GROUP):
                dq_ref[:, g * HEAD_DIM:(g + 1) * HEAD_DIM] = dqv[g * tq:(g + 1) * tq, :]

    return pl.pallas_call(
        body, name=name, grid=(N_KV_HEADS, T // tq, nk),
        in_specs=[pl.BlockSpec((tq, GW), lambda h, i, j: (i, h)),
                  pl.BlockSpec((tq, GW), lambda h, i, j: (i, h)),
                  pl.BlockSpec((tk, HEAD_DIM), lambda h, i, j: (j, h)),
                  pl.BlockSpec((tk, HEAD_DIM), lambda h, i, j: (j, h)),
                  pl.BlockSpec((1, GROUP, tq), lambda h, i, j: (h, 0, i)),
                  pl.BlockSpec((1, GROUP, tq), lambda h, i, j: (h, 0, i))],
        out_specs=[pl.BlockSpec((tq, GW), lambda h, i, j: (i, h)),
                   pl.BlockSpec((T, HEAD_DIM), lambda h, i, j: (0, h)),
                   pl.BlockSpec((T, HEAD_DIM), lambda h, i, j: (0, h))],
        out_shape=[jax.ShapeDtypeStruct((T, N_Q_HEADS * HEAD_DIM), F32),
                   jax.ShapeDtypeStruct((T, N_KV_HEADS * HEAD_DIM), F32),
                   jax.ShapeDtypeStruct((T, N_KV_HEADS * HEAD_DIM), F32)],
        scratch_shapes=[pltpu.VMEM((GROUP * tq, HEAD_DIM), BF16), pltpu.VMEM((GROUP * tq, HEAD_DIM), BF16),
                        pltpu.VMEM((HEAD_DIM, GROUP * tq), F32)],
        compiler_params=_params(("arbitrary", "arbitrary", "arbitrary")),
    )(q, do, k, v, lse, delta)


def _gate_specs(D):
    w = D // 2
    first = (3 * D + (N_Q_HEADS + 2 * N_KV_HEADS) * HEAD_DIM) // w
    return [pl.BlockSpec((ROW, w), lambda i, c=first + j: (i, c)) for j in range(4)]


def _merge_fwd(a1, a2, P, D, name):
    T = a1.shape[0]
    w = D // 2

    def body(a1_ref, a2_ref, g0, g1, g2, g3, z_ref):
        for j, (gc, ga) in enumerate(((g0, g2), (g1, g3))):
            sl = slice(j * w, (j + 1) * w)
            z = jax.nn.sigmoid(_f32(gc)) * a1_ref[:, sl] + jax.nn.sigmoid(_f32(ga)) * a2_ref[:, sl]
            z_ref[:, sl] = z.astype(BF16)

    return pl.pallas_call(
        body, name=name, grid=(T // ROW,),
        in_specs=[_row_spec(D), _row_spec(D)] + _gate_specs(D),
        out_specs=_row_spec(D), out_shape=jax.ShapeDtypeStruct((T, D), BF16),
        compiler_params=_params(("parallel",)),
    )(a1, a2, P, P, P, P)


def _merge_bwd(dz, a1, a2, P, D, name):
    T = a1.shape[0]
    w = D // 2

    def body(dz_ref, a1_ref, a2_ref, g0, g1, g2, g3, d1_ref, d2_ref, dg_ref):
        for j, (gc, ga) in enumerate(((g0, g2), (g1, g3))):
            sl = slice(j * w, (j + 1) * w)
            dz = dz_ref[:, sl]
            sc = jax.nn.sigmoid(_f32(gc))
            sa = jax.nn.sigmoid(_f32(ga))
            d1_ref[:, sl] = (dz * sc).astype(BF16)
            d2_ref[:, sl] = (dz * sa).astype(BF16)
            dg_ref[:, j * w:(j + 1) * w] = (dz * a1_ref[:, sl] * (sc * (1.0 - sc))).astype(BF16)
            dg_ref[:, D + j * w:D + (j + 1) * w] = (dz * a2_ref[:, sl] * (sa * (1.0 - sa))).astype(BF16)

    return pl.pallas_call(
        body, name=name, grid=(T // ROW,),
        in_specs=[_row_spec(D), _row_spec(D), _row_spec(D)] + _gate_specs(D),
        out_specs=[_row_spec(D), _row_spec(D), _row_spec(2 * D)],
        out_shape=[jax.ShapeDtypeStruct((T, D), BF16), jax.ShapeDtypeStruct((T, D), BF16),
                   jax.ShapeDtypeStruct((T, 2 * D), BF16)],
        compiler_params=_params(("parallel",)),
    )(dz, a1, a2, P, P, P, P)


def _adamw_math(w, g, m, v):
    m = ADAM_B1 * m + (1.0 - ADAM_B1) * g
    v = ADAM_B2 * v + (1.0 - ADAM_B2) * (g * g)
    m_hat = m / (1.0 - ADAM_B1 ** ADAM_STEP)
    v_hat = v / (1.0 - ADAM_B2 ** ADAM_STEP)
    delta = -ADAM_LR * (m_hat / (jnp.sqrt(v_hat) + ADAM_EPS) + ADAM_WD * w)
    return delta, m, v


def _adamw(w, g, m, v, name):
    R, C = w.shape
    tr = _pick(R, tuple(t for t in (256, 128, 64, 32, 16, 8) if t * C * 4 <= ADAMW_BLOCK_BYTES))

    def body(w_ref, g_ref, m_ref, v_ref, d_ref, mo_ref, vo_ref):
        d, mn, vn = _adamw_math(w_ref[...], g_ref[...], m_ref[...], v_ref[...])
        d_ref[...] = d
        mo_ref[...] = mn
        vo_ref[...] = vn

    spec = pl.BlockSpec((tr, C), lambda i: (i, 0))
    return pl.pallas_call(
        body, name=name, grid=(R // tr,),
        in_specs=[spec] * 4, out_specs=[spec] * 3,
        out_shape=[jax.ShapeDtypeStruct((R, C), F32)] * 3,
        compiler_params=_params(("parallel",)),
    )(w, g, m, v)


def _norm_mix_in_fwd(xprev, branch, mods, g, gate, shift_idx, scale_idx, w_t, name):
    x_specs, x_args, (T, D) = _rows_operand(xprev)
    N = w_t.shape[0]
    cw = _pick(N, (1664, 1024, 512, 256, 128))

    def body(*refs):
        f_ref, m_ref, g_ref, w_ref, xo_ref, h_ref, p_ref = refs[len(x_args):]
        m = m_ref[0]
        gate_idx, fac = gate
        x = _rows_tile(refs[:len(x_args)]) + (fac * m[gate_idx:gate_idx + 1, :]) * f_ref[...]
        xo_ref[...] = x
        hv = _norm_tile_fwd(x, m, g_ref[...], shift_idx, scale_idx)
        h_ref[...] = hv
        for j in range(N // cw):
            p_ref[:, j * cw:(j + 1) * cw] = lax.dot_general(
                hv, w_ref[j * cw:(j + 1) * cw, :], _NT, preferred_element_type=F32).astype(BF16)

    return pl.pallas_call(
        body, name=name, grid=(T // ROW,),
        in_specs=x_specs + [_row_spec(D), _mods_spec(D), _vec_spec(1, D), _resident()],
        out_specs=[_row_spec(D), _row_spec(D), _row_spec(N)],
        out_shape=[jax.ShapeDtypeStruct((T, D), F32), jax.ShapeDtypeStruct((T, D), BF16),
                   jax.ShapeDtypeStruct((T, N), BF16)],
        compiler_params=_params(("parallel",)),
    )(*x_args, branch, mods, g, w_t)


def _mix_in_norm_bwd(parts, w_t, x, dres, mods, g, shift_idx, scale_idx, gate, branch, name):
    T, D = x.shape
    n = len(parts)
    offs = [0]
    for p in parts:
        offs.append(offs[-1] + p.shape[1])
    assert offs[-1] == w_t.shape[0]

    def body(*refs):
        w_ref, x_ref, dr_ref, b_ref, m_ref, g_ref, dx_ref, db_ref, acc_ref = refs[n:]
        _acc_init(acc_ref)
        dh = None
        for i, a_ref in enumerate(refs[:n]):
            d = jnp.dot(a_ref[...], w_ref[offs[i]:offs[i + 1], :], preferred_element_type=F32)
            dh = d if dh is None else dh + d
        m = m_ref[0]
        dx = _norm_tile_bwd(x_ref[...], dh, dr_ref[...], m, g_ref[...], shift_idx, scale_idx, acc_ref)
        dx_ref[...] = dx
        db_ref[...] = _gate_tile_bwd(dx, b_ref[...], m, gate, acc_ref)

    return pl.pallas_call(
        body, name=name, grid=(T // ROW,),
        in_specs=[_row_spec(p.shape[1]) for p in parts] +
                 [_resident(), _row_spec(D), _row_spec(D), _row_spec(D), _mods_spec(D), _vec_spec(1, D)],
        out_specs=[_row_spec(D), _row_spec(D), _acc_spec(D)],
        out_shape=[jax.ShapeDtypeStruct((T, D), F32), jax.ShapeDtypeStruct((T, D), BF16),
                   jax.ShapeDtypeStruct((2, ACC_ROWS, D), F32)],
        compiler_params=_params(("arbitrary",)),
    )(*parts, w_t, x, dres, branch, mods, g)


def _adamw_transposed(w, gt, m, v, name):
    R, C = w.shape
    tc = 128

    def body(w_ref, g_ref, m_ref, v_ref, go_ref, d_ref, mo_ref, vo_ref):
        g = jnp.transpose(g_ref[...])
        d, mn, vn = _adamw_math(w_ref[...], g, m_ref[...], v_ref[...])
        go_ref[...] = g
        d_ref[...] = d
        mo_ref[...] = mn
        vo_ref[...] = vn

    spec = pl.BlockSpec((R, tc), lambda j: (0, j))
    return pl.pallas_call(
        body, name=name, grid=(C // tc,),
        in_specs=[spec, pl.BlockSpec((tc, R), lambda j: (j, 0)), spec, spec], out_specs=[spec] * 4,
        out_shape=[jax.ShapeDtypeStruct((R, C), F32)] * 4,
        compiler_params=_params(("parallel",)),
    )(w, gt, m, v)


class _NoExchange:
    def __init__(self, rest):
        self.rest = rest

    def rest_weights(self, after):
        return self.rest

    def reduce_early(self, grads, tag):
        return None


def _local_step(xcat, target, mods, norm_g, final_g, gq, gk, conv_w, ffn1_w, hooks, ctx_len):
    T, D = _rows_operand(xcat)[2]
    w1i, w1o = ffn1_w
    g1, g2, g3 = norm_g
    cos_t, sin_t = _rope_tables(ctx_len, T - ctx_len)

    def after(value, token, name):
        return value if token is None else _after(value, token, name)

    _, h1, u1, s1, f1 = _norm_ffn_fwd(xcat, None, mods, g1, None, 0, 1, w1i, w1o, "f_ffn1")
    wi, wbc, wba, wo, w2i, w2o = hooks.rest_weights(f1)
    x1, h2, P = _norm_mix_in_fwd(xcat, f1, mods, g2, (2, 0.5), 3, 4, wi, "f_mix_in")
    yc = _conv_fwd(P, conv_w, D, "f_conv")
    qn, kn, vb = _qk_fwd(P, gq, gk, cos_t, sin_t, D, "f_qk")
    o, lse = _flash_fwd(qn, kn, vb, "f_attn")
    a1 = _matmul(yc, wbc, "nn", F32, "f_branch_conv")
    a2 = _matmul(o, wba, "nn", F32, "f_branch_attn")
    z = _merge_fwd(a1, a2, P, D, "f_merge")
    mo = _matmul(z, wo, "nn", F32, "f_mix_out")
    x2, h3, u2, s2, dx3, df2, acc_head = _norm_ffn_fwd(x1, mo, mods, g3, (5, 1.0), 6, 7, w2i, w2o, "f_ffn2",
                                                       head=(final_g, target))

    du2, dx2, dmo, acc_n3 = _ffn_norm_bwd(df2, u2, w2i, w2o, x2, dx3, mods, g3, 6, 7, (5, 1.0), mo, "b_ffn2")
    g_w2o = _matmul(s2, df2, "tn", BF16, "b_ffn2_out_dw")
    g_w2i = _matmul(du2, h3, "tn", BF16, "b_ffn2_in_dw")

    dz = _matmul(dmo, wo, "nt", F32, "b_mix_out_dx")
    g_wo = _matmul(z, dmo, "tn", BF16, "b_mix_out_dw")
    da1, da2, dgt = _merge_bwd(dz, a1, a2, P, D, "b_merge")
    dyc = _matmul(da1, wbc, "nt", F32, "b_branch_conv_dx")
    do = _matmul(da2, wba, "nt", F32, "b_branch_attn_dx")
    g_wbc = _matmul(yc, da1, "tn", BF16, "b_branch_conv_dw")
    g_wba = _matmul(o, da2, "tn", BF16, "b_branch_attn_dw")
    token_a = hooks.reduce_early([g_wbc, g_wba, g_wo, g_w2i, g_w2o], "a")
    dob, delta = _attn_delta(do, o, "b_attn_delta", token_a)
    dq, dk, dv = _flash_bwd(qn, kn, vb, dob, lse, delta, "b_attn")
    dqkv, acc_qk = _qk_bwd(P, dq, dk, dv, gq, gk, cos_t, sin_t, D, "b_qk")
    dconv, acc_conv = _conv_bwd(P, dyc, conv_w, D, "b_conv")
    d_parts = (dconv, dqkv, dgt)
    dx1, df1, acc_n2 = _mix_in_norm_bwd(d_parts, wi, x1, dx2, mods, g2, 3, 4, (2, 0.5), f1, "b_mix_in")
    g_wi = jnp.concatenate([_matmul(dp, h2, "tn", BF16, f"b_mix_in_dw_{i}") for i, dp in enumerate(d_parts)], axis=0)
    g1_b = after(g1, hooks.reduce_early([g_wi], "b"), "after_rs_b")

    du1, grad_x, _, acc_n1 = _ffn_norm_bwd(df1, u1, w1i, w1o, xcat, dx1, mods, g1_b, 0, 1, None, None, "b_ffn1",
                                           skip_first_tile=True)
    g_w1o = _matmul(s1, df1, "tn", BF16, "b_ffn1_out_dw")
    g_w1i = _matmul(du1, h1, "tn", BF16, "b_ffn1_in_dw", token=hooks.reduce_early([g_w1o], "c"))

    grads = (g_w1i, g_w1o, g_wi, g_wbc, g_wba, g_wo, g_w2i, g_w2o)
    accs = (acc_head, acc_n3, acc_n2, acc_n1, acc_conv, acc_qk)
    return grad_x, grads, accs


def _place():
    return lax.axis_index("x"), lax.axis_index("y"), lax.axis_index("c")


def _other_chips(x, y):
    return [(1 - x, y), (x, 1 - y), (1 - x, 1 - y)]


def _allgather8(v, name):
    R, N = v.shape

    def body(v_ref, out_ref, send_sems, recv_sems, local_sem):
        x, y, c = _place()
        me, sibling = (x, y, c), (x, y, 1 - c)
        chips = _other_chips(x, y)

        def blk(px, py, pc):
            return out_ref.at[4 * px + 2 * py + pc]

        def copy(k, block, to, src=None):
            return pltpu.make_async_remote_copy(
                src_ref=blk(*block) if src is None else src, dst_ref=blk(*block),
                send_sem=send_sems.at[k], recv_sem=recv_sems.at[k], device_id=to, device_id_type=MESH)

        mine = pltpu.make_async_copy(v_ref, blk(*me), local_sem)
        mine.start()
        first = [copy(0, me, sibling, src=v_ref)]
        first += [copy(1 + j, me, (*chip, c), src=v_ref) for j, chip in enumerate(chips)]
        for cp in first:
            cp.start()
        passed = [copy(4 + j, (*chip, c), sibling) for j, chip in enumerate(chips)]
        for j, chip in enumerate(chips):
            copy(1 + j, (*chip, c), me).wait_recv()
            passed[j].start()
        copy(0, sibling, me).wait_recv()
        for j, chip in enumerate(chips):
            copy(4 + j, (*chip, 1 - c), me).wait_recv()
        for cp in first + passed:
            cp.wait_send()
        mine.wait()

    return pl.pallas_call(
        body, name=name,
        out_shape=jax.ShapeDtypeStruct((N_DEV, R, N), v.dtype),
        in_specs=[pl.BlockSpec(memory_space=pltpu.VMEM)],
        out_specs=pl.BlockSpec(memory_space=pltpu.VMEM),
        scratch_shapes=[pltpu.SemaphoreType.DMA((7,)), pltpu.SemaphoreType.DMA((7,)), pltpu.SemaphoreType.DMA],
        compiler_params=pltpu.CompilerParams(vmem_limit_bytes=VMEM_LIMIT),
    )(v)


def _any_specs(n):
    return [pl.BlockSpec(memory_space=pl.ANY)] * n


def _pair_exchange(grads, name):
    n = len(grads)

    def body(*refs):
        g, land = refs[:n], refs[n:2 * n]
        send_sems, recv_sems = refs[2 * n:]
        x, y, c = _place()
        sibling = (x, y, 1 - c)
        copies = []
        for t in range(n):
            half = grads[t].shape[0] // (2 * N_CHIPS)
            for s in range(N_CHIPS):
                cp = pltpu.make_async_remote_copy(
                    src_ref=g[t].at[pl.ds((2 * s + 1 - c) * half, half), :], dst_ref=land[t].at[s],
                    send_sem=send_sems.at[N_CHIPS * t + s], recv_sem=recv_sems.at[N_CHIPS * t + s],
                    device_id=sibling, device_id_type=MESH)
                cp.start()
                copies.append(cp)
        for cp in copies:
            cp.wait_recv()
        for cp in copies:
            cp.wait_send()

    return pl.pallas_call(
        body, name=name,
        out_shape=[jax.ShapeDtypeStruct((N_CHIPS, a.shape[0] // (2 * N_CHIPS), a.shape[1]), a.dtype) for a in grads],
        in_specs=_any_specs(n), out_specs=_any_specs(n),
        scratch_shapes=[pltpu.SemaphoreType.DMA((N_CHIPS * n,)), pltpu.SemaphoreType.DMA((N_CHIPS * n,))],
    )(*grads)


def _place_shard(w2, idx, transpose, name, token):
    if transpose:
        D, rs = w2.shape
        tr = 128
        in_spec = pl.BlockSpec((D, tr), lambda i, idx: (0, i))
    else:
        rs, D = w2.shape
        tr = _pick(rs, (352, 256, 128, 64, 32, 16))
        in_spec = pl.BlockSpec((tr, D), lambda i, idx: (i, 0))
    steps = rs // tr

    def body(idx_ref, w_ref, t_ref, o_ref):
        v = w_ref[...]
        o_ref[...] = (jnp.transpose(v) if transpose else v).astype(BF16)

    return pl.pallas_call(
        body, name=name,
        grid_spec=pltpu.PrefetchScalarGridSpec(
            num_scalar_prefetch=1, grid=(steps,),
            in_specs=[in_spec, pl.BlockSpec(token.shape, lambda i, idx: (0, 0))],
            out_specs=pl.BlockSpec((tr, D), lambda i, idx: (idx[1] * steps + i, 0))),
        out_shape=jax.ShapeDtypeStruct((N_CHIPS * rs, D), BF16),
        compiler_params=_params(("arbitrary",)),
    )(idx, w2, token)


def _pair_sum(g, landed, idx, name, token=None):
    _, half, D = landed.shape
    g4 = g.reshape(N_CHIPS, 2, half, D)
    tr = _pick(half, (416, 352, 128))
    extra = [] if token is None else [token]

    def body(idx_ref, g_ref, l_ref, *rest):
        rest[-1][...] = (g_ref[0].astype(F32) + l_ref[...].astype(F32)).astype(BF16)

    return pl.pallas_call(
        body, name=name,
        grid_spec=pltpu.PrefetchScalarGridSpec(
            num_scalar_prefetch=1, grid=(N_CHIPS, half // tr),
            in_specs=[pl.BlockSpec((1, 1, tr, D), lambda s, i, idx: (idx[1 + s], idx[0], i, 0)),
                      pl.BlockSpec((1, tr, D), lambda s, i, idx: (idx[1 + s], i, 0))] +
                     [pl.BlockSpec(t.shape, lambda s, i, idx: (0, 0)) for t in extra],
            out_specs=pl.BlockSpec((1, tr, D), lambda s, i, idx: (s, i, 0))),
        out_shape=jax.ShapeDtypeStruct((N_CHIPS, half, D), BF16),
        compiler_params=_params(("arbitrary", "arbitrary")),
    )(idx, g4, landed, *extra)


_HBM = pl.BlockSpec(memory_space=pltpu.HBM)
_SEM = pl.BlockSpec(memory_space=pltpu.SEMAPHORE)
_EFFECT = pltpu.SideEffectType.DATAFLOW_SIDE_EFFECTING


def _in_hbm(a):
    return pltpu.with_memory_space_constraint(a, pltpu.HBM)


def _split_copies(n, per, make):
    def start(nbuf, name, bufs):
        def body(*refs):
            ins = refs[:nbuf]
            send_sems, recv_sems = refs[nbuf], refs[nbuf + 1]
            token = refs[-1]
            for t in range(n):
                for j in range(per):
                    make(ins, t, j, send_sems.at[per * t + j], recv_sems.at[per * t + j]).start()
            token[...] = jnp.zeros(token.shape, token.dtype)

        out = pl.pallas_call(
            body, name=name,
            out_shape=(pltpu.SemaphoreType.DMA((per * n,)), pltpu.SemaphoreType.DMA((per * n,)),
                       *[pltpu.HBM(b.shape, b.dtype) for b in bufs], jax.ShapeDtypeStruct((8, 128), F32)),
            in_specs=[_HBM] * nbuf,
            out_specs=(_SEM, _SEM, *[_HBM] * nbuf, pl.BlockSpec(memory_space=pltpu.VMEM)),
            input_output_aliases={i: 2 + i for i in range(nbuf)},
            compiler_params=pltpu.CompilerParams(has_side_effects=_EFFECT),
        )(*[_in_hbm(b) for b in bufs])
        return out[0], out[1], list(out[2:2 + nbuf]), out[-1]

    def wait(nbuf, name, send_sems, recv_sems, bufs, after):
        def body(*refs):
            ins = refs[:nbuf]
            ss, rs = refs[nbuf], refs[nbuf + 1]
            for t in range(n):
                for j in range(per):
                    cp = make(ins, t, j, ss.at[per * t + j], rs.at[per * t + j])
                    cp.wait_send()
                    cp.wait_recv()

        return pl.pallas_call(
            body, name=name,
            out_shape=[pltpu.HBM(b.shape, b.dtype) for b in bufs],
            in_specs=[_HBM] * nbuf + [_SEM, _SEM, pl.BlockSpec(memory_space=pl.ANY)],
            out_specs=[_HBM] * nbuf,
            input_output_aliases={i: i for i in range(nbuf)},
            compiler_params=pltpu.CompilerParams(has_side_effects=_EFFECT),
        )(*bufs, send_sems, recv_sems, after)

    return start, wait


def _chip_exchange_split(n):
    def make(bufs, t, j, send_sem, recv_sem):
        x, y, c = _place()
        chip = _other_chips(x, y)[j]
        return pltpu.make_async_remote_copy(src_ref=bufs[t].at[1 + j], dst_ref=bufs[n + t].at[j], send_sem=send_sem,
                                            recv_sem=recv_sem, device_id=(*chip, c), device_id_type=MESH)

    return _split_copies(n, 3, make)


def _weights_gather_split(fulls):
    def make(bufs, t, j, send_sem, recv_sem):
        x, y, c = _place()
        chip = _other_chips(x, y)[j]
        rs = fulls[t].shape[0] // N_CHIPS
        rows = bufs[t].at[pl.ds((2 * x + y) * rs + c * (rs // 2), rs // 2), :]
        return pltpu.make_async_remote_copy(src_ref=rows, dst_ref=rows, send_sem=send_sem, recv_sem=recv_sem,
                                            device_id=(*chip, c), device_id_type=MESH)

    return _split_copies(len(fulls), 3, make)


def _weights_pass_on(fulls, name):
    n = len(fulls)

    def body(*refs):
        full = refs[n:2 * n]
        send_sems, recv_sems = refs[2 * n:]
        x, y, c = _place()
        chips = _other_chips(x, y)

        def copy(t, j, h):
            rs = fulls[t].shape[0] // N_CHIPS
            px, py = chips[j]
            rows = full[t].at[pl.ds((2 * px + py) * rs + h * (rs // 2), rs // 2), :]
            return pltpu.make_async_remote_copy(src_ref=rows, dst_ref=rows, send_sem=send_sems.at[3 * t + j],
                                                recv_sem=recv_sems.at[3 * t + j], device_id=(x, y, 1 - c),
                                                device_id_type=MESH)

        for t in range(n):
            for j in range(3):
                copy(t, j, c).start()
        for t in range(n):
            for j in range(3):
                copy(t, j, 1 - c).wait_recv()
        for t in range(n):
            for j in range(3):
                copy(t, j, c).wait_send()

    return pl.pallas_call(
        body, name=name,
        out_shape=[jax.ShapeDtypeStruct(f.shape, f.dtype) for f in fulls],
        in_specs=_any_specs(n), out_specs=_any_specs(n),
        input_output_aliases={t: t for t in range(n)},
        scratch_shapes=[pltpu.SemaphoreType.DMA((3 * n,)), pltpu.SemaphoreType.DMA((3 * n,))],
    )(*fulls)


def _after(value, token, name):
    def body(v_ref, t_ref, o_ref):
        o_ref[...] = v_ref[...]

    return pl.pallas_call(
        body, name=name, out_shape=jax.ShapeDtypeStruct(value.shape, value.dtype),
        in_specs=_whole(2), out_specs=pl.BlockSpec(memory_space=pltpu.VMEM),
    )(value, token)


def _chip_sum(ps, landed, idx, name):
    _, half, D = ps.shape
    tr = _pick(half, (416, 352, 128))
    steps = half // tr

    def body(idx_ref, p_ref, l_ref, o_ref):
        acc = p_ref[0].astype(F32)
        for j in range(3):
            acc = acc + l_ref[j].astype(F32)
        o_ref[...] = acc

    return pl.pallas_call(
        body, name=name,
        grid_spec=pltpu.PrefetchScalarGridSpec(
            num_scalar_prefetch=1, grid=(steps,),
            in_specs=[pl.BlockSpec((1, tr, D), lambda i, idx: (0, i, 0)),
                      pl.BlockSpec((3, tr, D), lambda i, idx: (0, i, 0))],
            out_specs=pl.BlockSpec((tr, D), lambda i, idx: (idx[0] * steps + i, 0))),
        out_shape=jax.ShapeDtypeStruct((2 * half, D), F32),
        compiler_params=_params(("arbitrary",)),
    )(idx, ps, landed)


def _pair_swap(shards, name):
    n = len(shards)

    def body(*refs):
        full = refs[n:2 * n]
        send_sems, recv_sems = refs[2 * n:]
        x, y, c = _place()

        def half(t, h):
            rows = shards[t].shape[0] // 2
            return full[t].at[pl.ds(h * rows, rows), :]

        def copy(t, h):
            return pltpu.make_async_remote_copy(src_ref=half(t, h), dst_ref=half(t, h), send_sem=send_sems.at[t],
                                                recv_sem=recv_sems.at[t], device_id=(x, y, 1 - c),
                                                device_id_type=MESH)

        for t in range(n):
            copy(t, c).start()
        for t in range(n):
            copy(t, 1 - c).wait_recv()
        for t in range(n):
            copy(t, c).wait_send()

    return pl.pallas_call(
        body, name=name,
        out_shape=[jax.ShapeDtypeStruct(a.shape, a.dtype) for a in shards],
        in_specs=_any_specs(n), out_specs=_any_specs(n),
        input_output_aliases={t: t for t in range(n)},
        scratch_shapes=[pltpu.SemaphoreType.DMA((n,)), pltpu.SemaphoreType.DMA((n,))],
    )(*shards)


def _gather_begin(fulls, tag):
    start, wait = _weights_gather_split(fulls)
    send_sems, recv_sems, bufs, token = start(len(fulls), f"ag_{tag}_start", fulls)
    return (wait, send_sems, recv_sems, bufs), token


def _gather_end(state, after, tag):
    wait, send_sems, recv_sems, bufs = state
    landed = wait(len(bufs), f"ag_{tag}_wait", send_sems, recv_sems, bufs, after)
    return _weights_pass_on(landed, f"ag_{tag}_pass_on")


class _Exchanges:
    def __init__(self, fulls_rest, idx):
        self.idx = idx
        self._rest, self.token = _gather_begin(fulls_rest, "rest")
        self._early = []

    def rest_weights(self, after):
        return _gather_end(self._rest, after, "rest")

    def reduce_early(self, grads, tag, token=None):
        landed = _pair_exchange(grads, "rs_pair_exchange_" + tag)
        sums = [_pair_sum(g, l, self.idx, f"rs_pair_sum_{tag}{t}", token)
                for t, (g, l) in enumerate(zip(grads, landed))]
        zones = [lax.empty((3,) + s.shape[1:], s.dtype) for s in sums]
        start, wait = _chip_exchange_split(len(sums))
        send_sems, recv_sems, bufs, token = start(2 * len(sums), "rs_chip_start_" + tag, sums + zones)
        self._early.append((tag, wait, send_sems, recv_sems, bufs))
        return token

    def finish(self, tags, after):
        halves = []
        for tag, wait, send_sems, recv_sems, bufs in self._early:
            if tag in tags:
                n = len(bufs) // 2
                done = wait(len(bufs), "rs_chip_wait_" + tag, send_sems, recv_sems, bufs, after)
                halves += [_chip_sum(p, l, self.idx, f"rs_chip_sum_{tag}{t}")
                           for t, (p, l) in enumerate(zip(done[:n], done[n:]))]
        return halves


N_MOD = 9
PACK_HEAD, PACK_N3, PACK_N2, PACK_N1, PACK_CONV, PACK_QK = 0, 16, 32, 48, 64, 80
PACK_ROWS = 96
MOD_SRC = ((PACK_N1, 0), (PACK_N1, 1), (PACK_N2, 3), (PACK_N2, 0), (PACK_N2, 1),
           (PACK_N3, 3), (PACK_N3, 0), (PACK_N3, 1), (PACK_HEAD, 2))
CTX_ROW = 8


def _silu(v):
    return v * jax.nn.sigmoid(v)


def _whole(n):
    return [pl.BlockSpec(memory_space=pltpu.VMEM)] * n


def _mod_rows(cin, w_sh, b_sh, name):
    def body(c_ref, w_ref, b_ref, o_ref):
        a = _silu(c_ref[...]).astype(BF16)
        o_ref[...] = jnp.dot(a, w_ref[...].astype(BF16), preferred_element_type=F32) + b_ref[...]

    return pl.pallas_call(
        body, name=name, out_shape=jax.ShapeDtypeStruct((cin.shape[0], w_sh.shape[1]), F32),
        in_specs=_whole(3), out_specs=pl.BlockSpec(memory_space=pltpu.VMEM),
        compiler_params=pltpu.CompilerParams(vmem_limit_bytes=VMEM_LIMIT),
    )(cin, w_sh, b_sh)


def _small_reduce(gathered, name):
    _, _, D = gathered.shape

    def body(g_ref, loss_ref, db_ref, gn_ref, cv_ref, qk_ref, dm_ref):
        tot = g_ref[0]
        for r in range(1, N_DEV):
            tot = tot + g_ref[r]

        def both(block, row):
            return tot[block + row:block + row + 1, :] + tot[block + 8 + row:block + 8 + row + 1, :]

        loss = jnp.sum(both(PACK_HEAD, 0), axis=1, keepdims=True)
        loss_ref[...] = jnp.broadcast_to(loss, loss_ref.shape)
        db_ref[...] = jnp.zeros(db_ref.shape, F32)
        dm_ref[...] = jnp.zeros(dm_ref.shape, F32)
        for j, (block, row) in enumerate(MOD_SRC):
            db_ref[j:j + 1, :] = both(block, row)
            dm_ref[CTX_ROW, j:j + 1, :] = tot[block + row:block + row + 1, :]
            for r in range(N_DEV):
                dm_ref[r, j:j + 1, :] = g_ref[r, block + 8 + row:block + 8 + row + 1, :]
        gn_ref[...] = jnp.zeros(gn_ref.shape, F32)
        gn_ref[0:1, :] = both(PACK_N1, 2)
        gn_ref[8:9, :] = both(PACK_N2, 2)
        gn_ref[16:17, :] = both(PACK_N3, 2)
        gn_ref[24:25, :] = both(PACK_HEAD, 1)
        cv_ref[...] = jnp.zeros(cv_ref.shape, F32)
        for r in range(3):
            cv_ref[r:r + 1, :] = both(PACK_CONV, r)
        qk_ref[...] = jnp.zeros(qk_ref.shape, F32)
        qk_ref[0:1, 0:HEAD_DIM] = both(PACK_QK, 0)[:, 0:HEAD_DIM]
        qk_ref[0:1, HEAD_DIM:2 * HEAD_DIM] = both(PACK_QK, 1)[:, 0:HEAD_DIM]

    return pl.pallas_call(
        body, name=name,
        out_shape=[jax.ShapeDtypeStruct((8, 128), F32), jax.ShapeDtypeStruct((16, D), F32),
                   jax.ShapeDtypeStruct((32, D), F32), jax.ShapeDtypeStruct((8, D), F32),
                   jax.ShapeDtypeStruct((8, D), F32), jax.ShapeDtypeStruct((16, 16, D), F32)],
        in_specs=_whole(1), out_specs=_whole(6),
        compiler_params=pltpu.CompilerParams(vmem_limit_bytes=VMEM_LIMIT),
    )(gathered)


def _wmod_grad(cin, dm_sh, w_sh, name):
    def body(c_ref, d_ref, w_ref, gw_ref, cp_ref):
        a = _silu(c_ref[...]).astype(BF16)
        d = d_ref[...].astype(BF16)
        gw_ref[...] = lax.dot_general(a, d, (((0,), (0,)), ((), ())), preferred_element_type=F32)
        cp_ref[...] = lax.dot_general(d, w_ref[...].astype(BF16), (((1,), (1,)), ((), ())),
                                      preferred_element_type=F32)

    return pl.pallas_call(
        body, name=name,
        out_shape=[jax.ShapeDtypeStruct(w_sh.shape, F32), jax.ShapeDtypeStruct(cin.shape, F32)],
        in_specs=_whole(3), out_specs=_whole(2),
        compiler_params=pltpu.CompilerParams(vmem_limit_bytes=VMEM_LIMIT),
    )(cin, dm_sh, w_sh)


def _cctx_grad(parts, c_ctx8, name):
    def body(p_ref, c_ref, o_ref):
        tot = p_ref[0] + p_ref[2] + p_ref[4] + p_ref[6]
        cv = c_ref[...]
        sig = jax.nn.sigmoid(cv)
        rows = lax.broadcasted_iota(jnp.int32, tot.shape, 0)
        o_ref[...] = jnp.where(rows == 0, tot * (sig * (1.0 + cv * (1.0 - sig))), 0.0)

    return pl.pallas_call(
        body, name=name, out_shape=jax.ShapeDtypeStruct(c_ctx8.shape, F32),
        in_specs=_whole(2), out_specs=pl.BlockSpec(memory_space=pltpu.VMEM),
    )(parts, c_ctx8)


def _pad_rows(a, rows):
    return jnp.pad(a, ((0, rows - a.shape[0]), (0, 0)))


def _pack_small(c_ctx, b_mod, n1, n2, n3, final_g, gq, gk, conv_sh, D):
    misc = jnp.concatenate([gq, gk, conv_sh.reshape(1, -1)], axis=1)
    return jnp.concatenate([_pad_rows(c_ctx[None], 8), _pad_rows(b_mod.reshape(N_MOD, D), 16), _pad_rows(n1, 8),
                            _pad_rows(n2, 8), _pad_rows(n3, 8), _pad_rows(final_g[None], 8), _pad_rows(misc, 8)], axis=0)


def _unpack_small(p, D, conv_shape):
    misc = p[56:57]
    return dict(c_ctx=p[0], b_mod=p[8:8 + N_MOD].reshape(1, N_MOD * D), norm1_g=p[24:25], norm2_g=p[32:33],
                norm3_g=p[40:41], final_g=p[48], q_norm_g=misc[:, 0:HEAD_DIM], k_norm_g=misc[:, HEAD_DIM:2 * HEAD_DIM],
                conv_w=misc[:, 2 * HEAD_DIM:].reshape(conv_shape))


WEIGHT_ORDER = ("c_ctx", "w_mod", "b_mod", "norm1_g", "norm2_g", "norm3_g", "ffn1_w_in", "ffn1_w_out", "w_in",
                "conv_w", "q_norm_g", "k_norm_g", "w_branch_conv", "w_branch_attn", "w_out", "ffn2_w_in",
                "ffn2_w_out", "final_g")
BIG = ("ffn1_w_in", "ffn1_w_out", "w_in", "w_branch_conv", "w_branch_attn", "w_out", "ffn2_w_in", "ffn2_w_out")
COLUMN_SHARDED = ("ffn1_w_in", "w_in", "ffn2_w_in")


def kernel(x, c, ctx, c_ctx, w_mod, b_mod, norm1_g, norm2_g, norm3_g, ffn1_w_in, ffn1_w_out, w_in, conv_w, q_norm_g, k_norm_g, w_branch_conv, w_branch_attn, w_out, ffn2_w_in, ffn2_w_out, final_g, loss_target, m_c_ctx, m_w_mod, m_b_mod, m_norm1_g, m_norm2_g, m_norm3_g, m_ffn1_w_in, m_ffn1_w_out, m_w_in, m_conv_w, m_q_norm_g, m_k_norm_g, m_w_branch_conv, m_w_branch_attn, m_w_out, m_ffn2_w_in, m_ffn2_w_out, m_final_g, v_c_ctx, v_w_mod, v_b_mod, v_norm1_g, v_norm2_g, v_norm3_g, v_ffn1_w_in, v_ffn1_w_out, v_w_in, v_conv_w, v_q_norm_g, v_k_norm_g, v_w_branch_conv, v_w_branch_attn, v_w_out, v_ffn2_w_in, v_ffn2_w_out, v_final_g):
    w = dict(c_ctx=c_ctx, w_mod=w_mod, b_mod=b_mod, norm1_g=norm1_g, norm2_g=norm2_g, norm3_g=norm3_g,
             ffn1_w_in=ffn1_w_in, ffn1_w_out=ffn1_w_out, w_in=w_in, conv_w=conv_w, q_norm_g=q_norm_g,
             k_norm_g=k_norm_g, w_branch_conv=w_branch_conv, w_branch_attn=w_branch_attn, w_out=w_out,
             ffn2_w_in=ffn2_w_in, ffn2_w_out=ffn2_w_out, final_g=final_g)
    m = dict(c_ctx=m_c_ctx, w_mod=m_w_mod, b_mod=m_b_mod, norm1_g=m_norm1_g, norm2_g=m_norm2_g, norm3_g=m_norm3_g,
             ffn1_w_in=m_ffn1_w_in, ffn1_w_out=m_ffn1_w_out, w_in=m_w_in, conv_w=m_conv_w, q_norm_g=m_q_norm_g,
             k_norm_g=m_k_norm_g, w_branch_conv=m_w_branch_conv, w_branch_attn=m_w_branch_attn, w_out=m_w_out,
             ffn2_w_in=m_ffn2_w_in, ffn2_w_out=m_ffn2_w_out, final_g=m_final_g)
    v = dict(c_ctx=v_c_ctx, w_mod=v_w_mod, b_mod=v_b_mod, norm1_g=v_norm1_g, norm2_g=v_norm2_g, norm3_g=v_norm3_g,
             ffn1_w_in=v_ffn1_w_in, ffn1_w_out=v_ffn1_w_out, w_in=v_w_in, conv_w=v_conv_w, q_norm_g=v_q_norm_g,
             k_norm_g=v_k_norm_g, w_branch_conv=v_w_branch_conv, w_branch_attn=v_w_branch_attn, w_out=v_w_out,
             ffn2_w_in=v_ffn2_w_in, ffn2_w_out=v_ffn2_w_out, final_g=v_final_g)

    xi, yi, ci = _place()
    dev = 4 * xi + 2 * yi + ci
    shard = 2 * xi + yi
    idx = jnp.stack([ci, shard, 2 * (1 - xi) + yi, 2 * xi + (1 - yi), 2 * (1 - xi) + (1 - yi)]).astype(jnp.int32)
    D = x.shape[-1]
    ctx_len = ctx.shape[1]
    assert ctx_len == ROW and c.shape == (1, D)
    mcols = w_mod.shape[2]
    ccols = conv_w.shape[2]

    c_all = _allgather8(jnp.broadcast_to(c, (8, D)), "ag_c")[:, 0, :]
    cin = jnp.concatenate([c_all, _pad_rows(c_ctx[None], 8)], axis=0)
    b_sh = lax.dynamic_slice(b_mod, (0, shard * mcols), (1, mcols))
    mod_sh = _mod_rows(cin, w_mod[0], b_sh, "mod_rows")
    conv_rows = jnp.pad(conv_w[0], ((0, 8 - conv_w.shape[1]), (0, mcols - ccols)))
    mod_all = _allgather8(jnp.concatenate([mod_sh, conv_rows], axis=0), "ag_mod")
    mod_full = jnp.concatenate([mod_all[2 * s, :16] for s in range(N_CHIPS)], axis=1)
    conv_full = jnp.concatenate([mod_all[2 * s, 16:16 + conv_w.shape[1], :ccols] for s in range(N_CHIPS)], axis=1)
    mod_lat = lax.dynamic_slice(mod_full, (dev, 0), (1, N_MOD * D)).reshape(N_MOD, D)
    mod_ctx = mod_full[CTX_ROW].reshape(N_MOD, D)
    mods = jnp.stack([_pad_rows(mod_ctx, 16), _pad_rows(mod_lat, 16)])

    def place(names, token):
        return [_place_shard(w[n][0], idx, n in COLUMN_SHARDED, "place_" + n, token) for n in names]

    ffn1_gather, ffn1_token = _gather_begin(place(BIG[:2], mod_all[0, :8, :HEAD_DIM]), "ffn1")
    fulls_rest = place(BIG[2:], ffn1_token)
    ffn1_w = _gather_end(ffn1_gather, fulls_rest[-1][:16, :HEAD_DIM], "ffn1")
    hooks = _Exchanges(fulls_rest, idx)

    xcat = (ctx[0], x[0])
    norm1_first = _after(norm1_g, hooks.token, "after_ag_rest")
    grad_x, grads, accs = _local_step(xcat, loss_target[0], mods, (norm1_first, norm2_g, norm3_g), final_g[None],
                                      q_norm_g, k_norm_g, conv_full, ffn1_w, hooks, ctx_len)
    g = {}

    pack = jnp.concatenate([a.reshape(2 * ACC_ROWS, D) for a in accs], axis=0)
    gathered = _allgather8(pack, "ag_small")
    loss8, db_mod, g_norms, g_conv, g_qk, dm = _small_reduce(gathered, "small_reduce")
    dm_sh = lax.dynamic_slice(dm[:, :N_MOD, :].reshape(16, N_MOD * D), (0, shard * mcols), (16, mcols))
    g_wmod, cpart = _wmod_grad(cin, dm_sh, w_mod[0], "wmod_grad")
    g["w_mod"] = g_wmod[None]
    cparts = _allgather8(cpart[CTX_ROW:CTX_ROW + 8], "ag_cctx")
    g_cctx = _cctx_grad(cparts, _pad_rows(c_ctx[None], 8), "cctx_grad")
    g_conv_sh = lax.dynamic_slice(g_conv, (0, shard * ccols), (conv_w.shape[1], ccols))
    g_misc = jnp.concatenate([g_qk[0:1, 0:2 * HEAD_DIM], g_conv_sh.reshape(1, -1)], axis=1)
    g_pack = jnp.concatenate([g_cctx, db_mod, g_norms, _pad_rows(g_misc, 8)], axis=0)

    def packed(p):
        return _pack_small(p["c_ctx"], p["b_mod"], p["norm1_g"], p["norm2_g"], p["norm3_g"], p["final_g"],
                           p["q_norm_g"], p["k_norm_g"], p["conv_w"][0], D)

    d_pack, m_pack, v_pack = _adamw(packed(w), g_pack, packed(m), packed(v), "adamw_small")

    g.update(_unpack_small(g_pack, D, conv_w.shape))
    delta = _unpack_small(d_pack, D, conv_w.shape)
    new_m = _unpack_small(m_pack, D, conv_w.shape)
    new_v = _unpack_small(v_pack, D, conv_w.shape)

    def update(n, g2):
        if n in COLUMN_SHARDED:
            g2, d2, m2, v2 = _adamw_transposed(w[n][0], g2, m[n][0], v[n][0], "adamw_" + n)
        else:
            d2, m2, v2 = _adamw(w[n][0], g2, m[n][0], v[n][0], "adamw_" + n)
        g[n], delta[n], new_m[n], new_v[n] = g2[None], d2[None], m2[None], v2[None]
        return v2

    token_d = hooks.reduce_early([grads[0]], "d", token=d_pack[:8, :HEAD_DIM])
    h_wbc, h_wba, h_wo, h_w2i, h_w2o, h_wi, h_w1o = hooks.finish("abc", token_d)
    done = _pair_swap([h_w1o, h_wi, h_wbc, h_wba, h_wo, h_w2i, h_w2o], "rs_pair_swap")
    last = update("w_mod", g_wmod)
    for n, r in zip(BIG[1:], done):
        last = update(n, r)
    (h_w1i,) = hooks.finish("d", last)
    update(BIG[0], _pair_swap([h_w1i], "rs_pair_swap_d")[0])

    loss = loss8[0, 0]
    return (loss, grad_x[None], *[g[n] for n in WEIGHT_ORDER], *[delta[n] for n in WEIGHT_ORDER],
            *[new_m[n] for n in WEIGHT_ORDER], *[new_v[n] for n in WEIGHT_ORDER])
```

```python
import functools

import jax
import jax.numpy as jnp
from jax import lax
from jax.experimental import pallas as pl
from jax.experimental.pallas import tpu as pltpu

F32 = jnp.float32
BF16 = jnp.bfloat16

HEAD_DIM = 128
N_Q_HEADS = 8
N_KV_HEADS = 2
GROUP = N_Q_HEADS // N_KV_HEADS
GRID_W = 64
ROPE_THETA = 10000.0
EPS = 1e-6
ATTN_SCALE = HEAD_DIM ** -0.5

ADAM_LR = 0.001
ADAM_B1 = 0.9
ADAM_B2 = 0.999
ADAM_EPS = 1e-08
ADAM_WD = 0.01
ADAM_STEP = 10

ROW = 256
HALO = 16
ACC_ROWS = 8
N_CHIPS = 4
N_DEV = 8
MESH = pl.DeviceIdType.MESH
VMEM_LIMIT = 48 * 1024 * 1024
ADAMW_BLOCK_BYTES = 1024 * 1024


def _pick(n, prefs):
    for p in prefs:
        if n % p == 0:
            return p
    return n


def _params(sem):
    return pltpu.CompilerParams(dimension_semantics=sem, vmem_limit_bytes=VMEM_LIMIT)


def _stream(i):
    return jnp.minimum(i, 1)


def _matmul(a, b, mode, out_dtype, name, tm=None, tn=None, tk=None, token=None):
    if mode == "nn":
        (M, K), (K2, N) = a.shape, b.shape
    elif mode == "nt":
        (M, K), (N, K2) = a.shape, b.shape
    else:
        (K, M), (K2, N) = a.shape, b.shape
    assert K == K2, (a.shape, b.shape, mode)
    tm = tm or _pick(M, (1664, 1408, 1024, 512, 256, 128) if mode == "tn" else (1408, 768, 512, 256, 128))
    tn = tn or _pick(N, (1664, 1408, 1024, 512, 256, 128))
    tk = tk or _pick(K, (1664, 1408, 1024, 768, 512, 256, 128))
    nk = K // tk
    if mode == "tn":
        a_spec = pl.BlockSpec((tk, tm), lambda i, j, k: (k, i))
    else:
        a_spec = pl.BlockSpec((tm, tk), lambda i, j, k: (i, k))
    if mode == "nt":
        b_spec = pl.BlockSpec((tn, tk), lambda i, j, k: (j, k))
    else:
        b_spec = pl.BlockSpec((tk, tn), lambda i, j, k: (k, j))
    dims = {"nn": ((1,), (0,)), "nt": ((1,), (1,)), "tn": ((0,), (0,))}[mode]
    use_scratch = nk > 1 and out_dtype != F32

    extra = [] if token is None else [token]

    def body(a_ref, b_ref, *rest):
        o_ref, scratch = rest[len(extra)], rest[len(extra) + 1:]
        p = lax.dot_general(a_ref[...].astype(BF16), b_ref[...].astype(BF16), (dims, ((), ())),
                            preferred_element_type=F32)
        if nk == 1:
            o_ref[...] = p.astype(o_ref.dtype)
            return
        acc_ref = scratch[0] if use_scratch else o_ref
        k = pl.program_id(2)

        @pl.when(k == 0)
        def _():
            acc_ref[...] = p

        @pl.when(k > 0)
        def _():
            acc_ref[...] += p

        if use_scratch:
            @pl.when(k == nk - 1)
            def _():
                o_ref[...] = acc_ref[...].astype(o_ref.dtype)

    return pl.pallas_call(
        body, name=name,
        grid=(M // tm, N // tn, nk),
        in_specs=[a_spec, b_spec] + [pl.BlockSpec(t.shape, lambda i, j, k: (0, 0)) for t in extra],
        out_specs=pl.BlockSpec((tm, tn), lambda i, j, k: (i, j)),
        out_shape=jax.ShapeDtypeStruct((M, N), out_dtype),
        scratch_shapes=[pltpu.VMEM((tm, tn), F32)] if use_scratch else [],
        compiler_params=_params(("parallel", "parallel", "arbitrary")),
    )(a, b, *extra)


def _row_spec(width, col=0):
    return pl.BlockSpec((ROW, width), lambda i, col=col: (i, col))


def _mods_spec(D):
    return pl.BlockSpec((1, 16, D), lambda i: (_stream(i), 0, 0))


def _acc_spec(D):
    return pl.BlockSpec((1, ACC_ROWS, D), lambda i: (_stream(i), 0, 0))


def _vec_spec(rows, D):
    return pl.BlockSpec((rows, D), lambda i: (0, 0))


def _acc_init(acc_ref):
    i = pl.program_id(0)

    @pl.when(i <= 1)
    def _():
        acc_ref[...] = jnp.zeros_like(acc_ref)


def _acc_add(acc_ref, row, val):
    acc_ref[0, row:row + 1, :] += jnp.sum(val, axis=0, keepdims=True)


def _rows_operand(x):
    if not isinstance(x, tuple):
        return [_row_spec(x.shape[1])], [x], x.shape
    ctx, lat = x
    D = lat.shape[1]
    assert ctx.shape == (ROW, D)
    specs = [pl.BlockSpec((ROW, D), lambda i: (0, 0)), pl.BlockSpec((ROW, D), lambda i: (jnp.maximum(i - 1, 0), 0))]
    return specs, [ctx, lat], (ROW + lat.shape[0], D)


def _rows_tile(refs):
    if len(refs) == 1:
        return refs[0][...]
    return jnp.where(pl.program_id(0) == 0, refs[0][...], refs[1][...])


def _norm_tile_fwd(x, m, g, shift_idx, scale_idx):
    inv = lax.rsqrt(jnp.mean(x * x, axis=-1, keepdims=True) + EPS)
    y = (x * inv) * g
    return (y * (1.0 + m[scale_idx:scale_idx + 1, :]) + m[shift_idx:shift_idx + 1, :]).astype(BF16)


def _norm_tile_bwd(x, dh, dres, m, g, shift_idx, scale_idx, acc_ref):
    inv = lax.rsqrt(jnp.mean(x * x, axis=-1, keepdims=True) + EPS)
    xn = x * inv
    dy = dh * (1.0 + m[scale_idx:scale_idx + 1, :])
    dxn = dy * g
    _acc_add(acc_ref, 0, dh)
    _acc_add(acc_ref, 1, dh * (xn * g))
    _acc_add(acc_ref, 2, dy * xn)
    return inv * (dxn - xn * jnp.mean(dxn * xn, axis=-1, keepdims=True)) + dres


def _gate_tile_bwd(dx, branch, m, gate, acc_ref):
    gate_idx, fac = gate
    _acc_add(acc_ref, 3, fac * dx * branch)
    return ((fac * m[gate_idx:gate_idx + 1, :]) * dx).astype(BF16)


_NT = (((1,), (1,)), ((), ()))


def _ffn_chunk(F):
    return _pick(F, (1408, 512, 256, 128))


def _resident():
    return pl.BlockSpec(memory_space=pltpu.VMEM)


def _ffn_tile_fwd(hv, wi_ref, wo_ref, u_ref, s_ref, F, cw):
    acc = jnp.zeros((hv.shape[0], wo_ref.shape[1]), F32)
    for j in range(F // cw):
        a = lax.dot_general(hv, wi_ref[j * cw:(j + 1) * cw, :], _NT, preferred_element_type=F32)
        b = lax.dot_general(hv, wi_ref[F + j * cw:F + (j + 1) * cw, :], _NT, preferred_element_type=F32)
        s = ((a * jax.nn.sigmoid(a)) * b).astype(BF16)
        u_ref[:, j * cw:(j + 1) * cw] = a.astype(BF16)
        u_ref[:, F + j * cw:F + (j + 1) * cw] = b.astype(BF16)
        s_ref[:, j * cw:(j + 1) * cw] = s
        acc = acc + jnp.dot(s, wo_ref[j * cw:(j + 1) * cw, :], preferred_element_type=F32)
    return acc


def _norm_ffn_fwd(xprev, branch, mods, g, gate, shift_idx, scale_idx, w_in_t, w_out, name, head=None):
    x_specs, x_args, (T, D) = _rows_operand(xprev)
    F = w_out.shape[0]
    cw = _ffn_chunk(F)
    has_res = branch is not None
    n_in = len(x_args) + int(has_res) + 4 + (2 if head else 0)

    def body(*refs):
        ins, outs = list(refs[:n_in]), list(refs[n_in:])
        x = _rows_tile([ins.pop(0) for _ in x_args])
        f_ref = ins.pop(0) if has_res else None
        m_ref, g_ref, wi_ref, wo_ref = ins[:4]
        xo_ref = outs.pop(0) if has_res else None
        h_ref, u_ref, s_ref = outs[:3]
        m = m_ref[0]
        if has_res:
            gate_idx, fac = gate
            x = x + (fac * m[gate_idx:gate_idx + 1, :]) * f_ref[...]
            xo_ref[...] = x
        hv = _norm_tile_fwd(x, m, g_ref[...], shift_idx, scale_idx)
        h_ref[...] = hv
        f = _ffn_tile_fwd(hv, wi_ref, wo_ref, u_ref, s_ref, F, cw)
        if head is None:
            outs[3][...] = f
            return
        fg_ref, t_ref = ins[4:6]
        dx_ref, df_ref, acc_ref = outs[3:6]
        _acc_init(acc_ref)
        lat = (pl.program_id(0) > 0).astype(F32)
        gate8 = 0.5 * m[8:9, :]
        x3 = x + gate8 * f
        inv3 = lax.rsqrt(jnp.mean(x3 * x3, axis=-1, keepdims=True) + EPS)
        xn = x3 * inv3
        fg = fg_ref[...]
        e = (xn * fg - t_ref[...]) * lat
        dy = e * (1.0 / D)
        dxn = dy * fg
        dx = inv3 * (dxn - xn * jnp.mean(dxn * xn, axis=-1, keepdims=True))
        dx_ref[...] = dx
        df_ref[...] = (gate8 * dx).astype(BF16)
        _acc_add(acc_ref, 0, (0.5 / D) * e * e)
        _acc_add(acc_ref, 1, dy * xn)
        _acc_add(acc_ref, 2, 0.5 * dx * f)

    in_specs = x_specs + ([_row_spec(D)] if has_res else []) + \
               [_mods_spec(D), _vec_spec(1, D), _resident(), _resident()]
    args = x_args + ([branch] if has_res else []) + [mods, g, w_in_t, w_out]
    out_specs = ([_row_spec(D)] if has_res else []) + [_row_spec(D), _row_spec(2 * F), _row_spec(F)]
    out_shape = ([jax.ShapeDtypeStruct((T, D), F32)] if has_res else []) + \
                [jax.ShapeDtypeStruct((T, D), BF16), jax.ShapeDtypeStruct((T, 2 * F), BF16),
                 jax.ShapeDtypeStruct((T, F), BF16)]
    if head is None:
        out_specs += [_row_spec(D)]
        out_shape += [jax.ShapeDtypeStruct((T, D), F32)]
    else:
        in_specs += [_vec_spec(1, D), pl.BlockSpec((ROW, D), lambda i: (jnp.maximum(i - 1, 0), 0))]
        args += list(head)
        out_specs += [_row_spec(D), _row_spec(D), _acc_spec(D)]
        out_shape += [jax.ShapeDtypeStruct((T, D), F32), jax.ShapeDtypeStruct((T, D), BF16),
                      jax.ShapeDtypeStruct((2, ACC_ROWS, D), F32)]
    out = pl.pallas_call(
        body, name=name, grid=(T // ROW,), in_specs=in_specs, out_specs=out_specs, out_shape=out_shape,
        compiler_params=_params(("arbitrary",) if head else ("parallel",)),
    )(*args)
    return tuple(out) if has_res else (None,) + tuple(out)


def _ffn_norm_bwd(df, u, w_in_t, w_out, x, dres, mods, g, shift_idx, scale_idx, gate, branch, name,
                  skip_first_tile=False):
    T, D = df.shape
    F = w_out.shape[0]
    cw = _ffn_chunk(F)
    nt = T // ROW
    has_gate = gate is not None
    x_specs, x_args, _ = _rows_operand(x)
    n_in = 7 + len(x_args) + int(has_gate)

    def body(*refs):
        ins, outs = list(refs[:n_in]), list(refs[n_in:])
        df_ref, u_ref, wi_ref, wo_ref = ins[:4]
        x_refs = ins[4:4 + len(x_args)]
        dr_ref = ins[4 + len(x_args)]
        b_ref = ins[5 + len(x_args)] if has_gate else None
        m_ref, g_ref = ins[-2:]
        du_ref, dx_ref = outs[:2]
        db_ref = outs[2] if has_gate else None
        acc_ref = outs[-1]
        _acc_init(acc_ref)
        dfv = df_ref[...]
        dh = jnp.zeros((ROW, D), F32)
        for j in range(F // cw):
            ds = lax.dot_general(dfv, wo_ref[j * cw:(j + 1) * cw, :], _NT, preferred_element_type=F32)
            a = u_ref[:, j * cw:(j + 1) * cw].astype(F32)
            b = u_ref[:, F + j * cw:F + (j + 1) * cw].astype(F32)
            sig = jax.nn.sigmoid(a)
            da = (ds * b * (sig * (1.0 + a * (1.0 - sig)))).astype(BF16)
            db = (ds * (a * sig)).astype(BF16)
            du_ref[:, j * cw:(j + 1) * cw] = da
            du_ref[:, F + j * cw:F + (j + 1) * cw] = db
            dh = dh + jnp.dot(da, wi_ref[j * cw:(j + 1) * cw, :], preferred_element_type=F32)
            dh = dh + jnp.dot(db, wi_ref[F + j * cw:F + (j + 1) * cw, :], preferred_element_type=F32)
        m = m_ref[0]
        dx = _norm_tile_bwd(_rows_tile(x_refs), dh, dr_ref[...], m, g_ref[...], shift_idx, scale_idx, acc_ref)
        dx_ref[...] = dx
        if has_gate:
            db_ref[...] = _gate_tile_bwd(dx, b_ref[...], m, gate, acc_ref)

    in_specs = [_row_spec(D), _row_spec(2 * F), _resident(), _resident()] + x_specs + [_row_spec(D)] + \
               ([_row_spec(D)] if has_gate else []) + [_mods_spec(D), _vec_spec(1, D)]
    args = [df, u, w_in_t, w_out] + x_args + [dres] + ([branch] if has_gate else []) + [mods, g]
    if skip_first_tile:
        dx_spec = pl.BlockSpec((ROW, D), lambda i: (jnp.maximum(i - 1, 0), 0))
        dx_shape = jax.ShapeDtypeStruct((T - ROW, D), F32)
    else:
        dx_spec = _row_spec(D)
        dx_shape = jax.ShapeDtypeStruct((T, D), F32)
    out_specs = [_row_spec(2 * F), dx_spec] + ([_row_spec(D)] if has_gate else []) + [_acc_spec(D)]
    out_shape = [jax.ShapeDtypeStruct((T, 2 * F), BF16), dx_shape] + \
                ([jax.ShapeDtypeStruct((T, D), BF16)] if has_gate else []) + \
                [jax.ShapeDtypeStruct((2, ACC_ROWS, D), F32)]
    out = pl.pallas_call(
        body, name=name, grid=(nt,), in_specs=in_specs, out_specs=out_specs, out_shape=out_shape,
        compiler_params=_params(("arbitrary",)),
    )(*args)
    if has_gate:
        return tuple(out)
    return out[0], out[1], None, out[2]


def _halo_specs(width, col, nt):
    per = ROW // HALO
    prev = pl.BlockSpec((HALO, width), lambda i, col=col: (jnp.maximum(i * per - 1, 0), col))
    nxt = pl.BlockSpec((HALO, width), lambda i, col=col: (jnp.minimum((i + 1) * per, nt * per - 1), col))
    return prev, nxt


def _f32(ref):
    return ref[...].astype(F32)


def _last_row(halo_ref):
    return halo_ref[HALO - 1:HALO, :].astype(F32)


def _first_row(halo_ref):
    return halo_ref[0:1, :].astype(F32)


def _shift_rows(v, prev_row, next_row):
    rows = lax.broadcasted_iota(jnp.int32, v.shape, 0)
    down = jnp.where(rows == 0, prev_row, pltpu.roll(v, 1, 0))
    up = jnp.where(rows == v.shape[0] - 1, next_row, pltpu.roll(v, v.shape[0] - 1, 0))
    return down, up


def _conv_fwd(P, conv_w, D, name):
    T = P.shape[0]
    nt = T // ROW
    cg_p, cg_n = _halo_specs(D, 1, nt)
    vc_p, vc_n = _halo_specs(D, 2, nt)

    def body(bg_ref, cg_ref, vc_ref, cgp_ref, vcp_ref, cgn_ref, vcn_ref, w_ref, y_ref):
        i = pl.program_id(0)
        has_prev = (i != 1).astype(F32)
        has_next = (i != nt - 1).astype(F32)
        u = _f32(cg_ref) * _f32(vc_ref)
        up_row = _last_row(cgp_ref) * _last_row(vcp_ref) * has_prev
        un_row = _first_row(cgn_ref) * _first_row(vcn_ref) * has_next
        um1, up1 = _shift_rows(u, up_row, un_row)
        w = w_ref[...]
        conv = um1 * w[0:1, :] + u * w[1:2, :] + up1 * w[2:3, :]
        y_ref[...] = (_f32(bg_ref) * conv).astype(BF16)

    return pl.pallas_call(
        body, name=name, grid=(nt,),
        in_specs=[_row_spec(D, 0), _row_spec(D, 1), _row_spec(D, 2), cg_p, vc_p, cg_n, vc_n, _vec_spec(3, D)],
        out_specs=_row_spec(D),
        out_shape=jax.ShapeDtypeStruct((T, D), BF16),
        compiler_params=_params(("parallel",)),
    )(P, P, P, P, P, P, P, conv_w)


def _conv_bwd(P, dy, conv_w, D, name):
    T = P.shape[0]
    nt = T // ROW
    bg_p, bg_n = _halo_specs(D, 0, nt)
    cg_p, cg_n = _halo_specs(D, 1, nt)
    vc_p, vc_n = _halo_specs(D, 2, nt)
    dy_p, dy_n = _halo_specs(D, 0, nt)

    def body(bg_ref, cg_ref, vc_ref, dy_ref, bgp_ref, cgp_ref, vcp_ref, dyp_ref,
             bgn_ref, cgn_ref, vcn_ref, dyn_ref, w_ref, o_ref, acc_ref):
        _acc_init(acc_ref)
        i = pl.program_id(0)
        lat = (i > 0).astype(F32)
        has_prev = (i != 1).astype(F32)
        has_next = (i != nt - 1).astype(F32)
        bg = _f32(bg_ref)
        cg = _f32(cg_ref)
        vc = _f32(vc_ref)
        dyv = dy_ref[...] * lat
        u = cg * vc
        up_row = _last_row(cgp_ref) * _last_row(vcp_ref) * has_prev
        un_row = _first_row(cgn_ref) * _first_row(vcn_ref) * has_next
        um1, up1 = _shift_rows(u, up_row, un_row)
        w = w_ref[...]
        conv = um1 * w[0:1, :] + u * w[1:2, :] + up1 * w[2:3, :]
        dc = dyv * bg
        dcp_row = _last_row(dyp_ref) * _last_row(bgp_ref) * has_prev
        dcn_row = _first_row(dyn_ref) * _first_row(bgn_ref) * has_next
        dcm1, dcp1 = _shift_rows(dc, dcp_row, dcn_row)
        du = dcp1 * w[0:1, :] + dc * w[1:2, :] + dcm1 * w[2:3, :]
        o_ref[:, 0:D] = (dyv * conv).astype(BF16)
        o_ref[:, D:2 * D] = (du * vc * lat).astype(BF16)
        o_ref[:, 2 * D:3 * D] = (du * cg * lat).astype(BF16)
        _acc_add(acc_ref, 0, dc * um1)
        _acc_add(acc_ref, 1, dc * u)
        _acc_add(acc_ref, 2, dc * up1)

    return pl.pallas_call(
        body, name=name, grid=(nt,),
        in_specs=[_row_spec(D, 0), _row_spec(D, 1), _row_spec(D, 2), _row_spec(D, 0),
                  bg_p, cg_p, vc_p, dy_p, bg_n, cg_n, vc_n, dy_n, _vec_spec(3, D)],
        out_specs=[_row_spec(3 * D), _acc_spec(D)],
        out_shape=[jax.ShapeDtypeStruct((T, 3 * D), BF16), jax.ShapeDtypeStruct((2, ACC_ROWS, D), F32)],
        compiler_params=_params(("arbitrary",)),
    )(P, P, P, dy, P, P, P, dy, P, P, P, dy, conv_w)


def _rope_tables(ctx_len, seq):
    n_freq = HEAD_DIM // 4
    rows = seq // GRID_W
    inv = ROPE_THETA ** (-jnp.arange(n_freq, dtype=F32) / n_freq)
    ar = jnp.arange(rows, dtype=F32)[:, None] * inv
    ac = jnp.arange(GRID_W, dtype=F32)[:, None] * inv

    def per_row(a):
        return jnp.repeat(a, GRID_W, axis=0)

    def per_col(a):
        return jnp.tile(a, (rows, 1))

    cos_t = jnp.concatenate([per_row(jnp.cos(ar)), per_row(jnp.cos(ar)), per_col(jnp.cos(ac)), per_col(jnp.cos(ac))], axis=1)
    sin_t = jnp.concatenate([per_row(-jnp.sin(ar)), per_row(jnp.sin(ar)), per_col(-jnp.sin(ac)), per_col(jnp.sin(ac))], axis=1)
    cos_t = jnp.concatenate([jnp.ones((ctx_len, HEAD_DIM), F32), cos_t], axis=0)
    sin_t = jnp.concatenate([jnp.zeros((ctx_len, HEAD_DIM), F32), sin_t], axis=0)
    return cos_t, sin_t


def _swap_halves(y):
    lanes = lax.broadcasted_iota(jnp.int32, y.shape, 1)
    first = (lanes % 64) < 32
    return jnp.where(first, pltpu.roll(y, HEAD_DIM - 32, 1), pltpu.roll(y, 32, 1))


def _qk_fwd(P, gq, gk, cos_t, sin_t, D, name):
    T = P.shape[0]
    QW = N_Q_HEADS * HEAD_DIM
    KW = N_KV_HEADS * HEAD_DIM
    q_col = (3 * D) // QW
    k_col = (3 * D + QW) // KW
    v_col = k_col + 1

    def body(q_ref, k_ref, v_ref, gq_ref, gk_ref, c_ref, s_ref, qo_ref, ko_ref, vo_ref):
        c = c_ref[...]
        s = s_ref[...]

        def head(x, g):
            inv = lax.rsqrt(jnp.mean(x * x, axis=-1, keepdims=True) + EPS)
            y = (x * inv) * g
            return y * c + _swap_halves(y) * s

        for h in range(N_Q_HEADS):
            sl = slice(h * HEAD_DIM, (h + 1) * HEAD_DIM)
            qo_ref[:, sl] = head(q_ref[:, sl].astype(F32), gq_ref[...]).astype(BF16)
        for h in range(N_KV_HEADS):
            sl = slice(h * HEAD_DIM, (h + 1) * HEAD_DIM)
            ko_ref[:, sl] = head(k_ref[:, sl].astype(F32), gk_ref[...]).astype(BF16)
        vo_ref[...] = v_ref[...].astype(BF16)

    return pl.pallas_call(
        body, name=name, grid=(T // ROW,),
        in_specs=[_row_spec(QW, q_col), _row_spec(KW, k_col), _row_spec(KW, v_col),
                  _vec_spec(1, HEAD_DIM), _vec_spec(1, HEAD_DIM), _row_spec(HEAD_DIM), _row_spec(HEAD_DIM)],
        out_specs=[_row_spec(QW), _row_spec(KW), _row_spec(KW)],
        out_shape=[jax.ShapeDtypeStruct((T, QW), BF16), jax.ShapeDtypeStruct((T, KW), BF16),
                   jax.ShapeDtypeStruct((T, KW), BF16)],
        compiler_params=_params(("parallel",)),
    )(P, P, P, gq, gk, cos_t, sin_t)


def _qk_bwd(P, dq, dk, dv, gq, gk, cos_t, sin_t, D, name):
    T = P.shape[0]
    QW = N_Q_HEADS * HEAD_DIM
    KW = N_KV_HEADS * HEAD_DIM
    q_col = (3 * D) // QW
    k_col = (3 * D + QW) // KW

    def body(q_ref, k_ref, dq_ref, dk_ref, dv_ref, gq_ref, gk_ref, c_ref, s_ref, o_ref, acc_ref):
        _acc_init(acc_ref)
        c = c_ref[...]
        s = s_ref[...]

        def head(x, d, g):
            dyv = d * c + _swap_halves(d * s)
            inv = lax.rsqrt(jnp.mean(x * x, axis=-1, keepdims=True) + EPS)
            xn = x * inv
            dxn = dyv * g
            dx = inv * (dxn - xn * jnp.mean(dxn * xn, axis=-1, keepdims=True))
            return dx, jnp.sum(dyv * xn, axis=0, keepdims=True)

        dgq = jnp.zeros((1, HEAD_DIM), F32)
        for h in range(N_Q_HEADS):
            sl = slice(h * HEAD_DIM, (h + 1) * HEAD_DIM)
            dx, dg = head(q_ref[:, sl].astype(F32), dq_ref[:, sl], gq_ref[...])
            o_ref[:, sl] = dx.astype(BF16)
            dgq = dgq + dg
        dgk = jnp.zeros((1, HEAD_DIM), F32)
        for h in range(N_KV_HEADS):
            sl = slice(h * HEAD_DIM, (h + 1) * HEAD_DIM)
            dx, dg = head(k_ref[:, sl].astype(F32), dk_ref[:, sl], gk_ref[...])
            o_ref[:, QW + h * HEAD_DIM:QW + (h + 1) * HEAD_DIM] = dx.astype(BF16)
            dgk = dgk + dg
        o_ref[:, QW + KW:QW + 2 * KW] = dv_ref[...].astype(BF16)
        acc_ref[0, 0:1, 0:HEAD_DIM] += dgq
        acc_ref[0, 1:2, 0:HEAD_DIM] += dgk

    return pl.pallas_call(
        body, name=name, grid=(T // ROW,),
        in_specs=[_row_spec(QW, q_col), _row_spec(KW, k_col), _row_spec(QW), _row_spec(KW), _row_spec(KW),
                  _vec_spec(1, HEAD_DIM), _vec_spec(1, HEAD_DIM), _row_spec(HEAD_DIM), _row_spec(HEAD_DIM)],
        out_specs=[_row_spec(QW + 2 * KW), _acc_spec(D)],
        out_shape=[jax.ShapeDtypeStruct((T, QW + 2 * KW), BF16), jax.ShapeDtypeStruct((2, ACC_ROWS, D), F32)],
        compiler_params=_params(("arbitrary",)),
    )(P, P, dq, dk, dv, gq, gk, cos_t, sin_t)


def _to_row(col, n):
    return jnp.transpose(jnp.broadcast_to(col, (n, HEAD_DIM)))[0:1, :]


LOG2E = 1.4426950408889634
ATTN_PART_LANES = 256
ATTN_QUERY_ROWS = 768


def _flash_fwd(q, k, v, name, tq=None, tk=None):
    T = q.shape[0]
    tq = tq or _pick(T, (ATTN_QUERY_ROWS, ROW))
    parts = GROUP * tq // ATTN_PART_LANES
    tk = tk or _pick(T, (1408, 768, 512, 256))
    ck = tk
    nk = T // tk
    GW = GROUP * HEAD_DIM

    def body(q_ref, k_ref, v_ref, o_ref, lse_ref, qs_ref, m_ref, l_ref, acc_ref, st_ref):
        ki = pl.program_id(2)

        @pl.when(ki == 0)
        def _():
            for g in range(GROUP):
                qs_ref[g * tq:(g + 1) * tq, :] = q_ref[:, g * HEAD_DIM:(g + 1) * HEAD_DIM]
            m_ref[...] = jnp.full(m_ref.shape, -jnp.inf, F32)
            l_ref[...] = jnp.zeros(l_ref.shape, F32)
            acc_ref[...] = jnp.zeros(acc_ref.shape, F32)

        w = ATTN_PART_LANES
        nck = tk // ck

        def lanes(p):
            return slice(p * w, (p + 1) * w)

        def keys(c):
            return slice(c * ck, (c + 1) * ck)

        def fold(a):
            return a.reshape(ck // 8, 8, w)

        def scores(p, c):
            st = lax.dot_general(k_ref[keys(c), :], qs_ref[lanes(p), :], _NT,
                                 preferred_element_type=F32) * (ATTN_SCALE * LOG2E)
            st_ref[keys(c), lanes(p)] = st
            return jnp.max(fold(st), axis=0)

        def new_max(p, partial):
            m_prev = m_ref[:, lanes(p)]
            m_new = jnp.maximum(m_prev, jnp.max(functools.reduce(jnp.maximum, partial), axis=0, keepdims=True))
            m_ref[:, lanes(p)] = m_new
            return m_new, jnp.exp2(m_prev - m_new)

        def weights(p, c, m_new):
            pt = jnp.exp2(st_ref[keys(c), lanes(p)] - m_new)
            pv = lax.dot_general(v_ref[keys(c), :], pt.astype(BF16), (((0,), (0,)), ((), ())),
                                 preferred_element_type=F32)
            return jnp.sum(fold(pt), axis=0), pv

        partial = [scores(0, c) for c in range(nck)]
        for p in range(parts):
            m_new, alpha = new_max(p, partial)
            partial, sums, pvs = [], [], []
            for c in range(nck):
                if p + 1 < parts:
                    partial.append(scores(p + 1, c))
                s8, pv = weights(p, c, m_new)
                sums.append(s8)
                pvs.append(pv)
            l_ref[:, lanes(p)] = alpha * l_ref[:, lanes(p)] + jnp.sum(sum(sums), axis=0, keepdims=True)
            acc_ref[:, lanes(p)] = alpha * acc_ref[:, lanes(p)] + sum(pvs)

        @pl.when(ki == nk - 1)
        def _():
            out = jnp.transpose(acc_ref[...] / l_ref[...])
            lse = m_ref[...] + jnp.log2(l_ref[...])
            for g in range(GROUP):
                o_ref[:, g * HEAD_DIM:(g + 1) * HEAD_DIM] = out[g * tq:(g + 1) * tq, :]
                lse_ref[0, g:g + 1, :] = lse[:, g * tq:(g + 1) * tq]

    return pl.pallas_call(
        body, name=name, grid=(N_KV_HEADS, T // tq, nk),
        in_specs=[pl.BlockSpec((tq, GW), lambda h, i, j: (i, h)),
                  pl.BlockSpec((tk, HEAD_DIM), lambda h, i, j: (j, h)),
                  pl.BlockSpec((tk, HEAD_DIM), lambda h, i, j: (j, h))],
        out_specs=[pl.BlockSpec((tq, GW), lambda h, i, j: (i, h)),
                   pl.BlockSpec((1, GROUP, tq), lambda h, i, j: (h, 0, i))],
        out_shape=[jax.ShapeDtypeStruct((T, N_Q_HEADS * HEAD_DIM), F32),
                   jax.ShapeDtypeStruct((N_KV_HEADS, GROUP, T), F32)],
        scratch_shapes=[pltpu.VMEM((GROUP * tq, HEAD_DIM), BF16), pltpu.VMEM((1, GROUP * tq), F32),
                        pltpu.VMEM((1, GROUP * tq), F32), pltpu.VMEM((HEAD_DIM, GROUP * tq), F32),
                        pltpu.VMEM((tk, GROUP * tq), F32)],
        compiler_params=_params(("parallel", "parallel", "arbitrary")),
    )(q, k, v)


def _flash_bwd(q, k, v, do, lse, delta, name, tq=None, tk=None, token=None):
    T = q.shape[0]
    tq = tq or _pick(T, (ATTN_QUERY_ROWS, ROW))
    tk = tk or _pick(T, (768, 512, 256))
    nk = T // tk
    GW = GROUP * HEAD_DIM
    nt = (((1,), (1,)), ((), ()))
    extra = [] if token is None else [token]

    def body(q_ref, do_ref, k_ref, v_ref, lse_ref, dl_ref, *rest):
        dq_ref, dk_ref, dv_ref, qs_ref, dos_ref, dqt_ref = rest[len(extra):]
        qi = pl.program_id(1)
        ki = pl.program_id(2)

        @pl.when(ki == 0)
        def _():
            for g in range(GROUP):
                qs_ref[g * tq:(g + 1) * tq, :] = q_ref[:, g * HEAD_DIM:(g + 1) * HEAD_DIM]
                dos_ref[g * tq:(g + 1) * tq, :] = do_ref[:, g * HEAD_DIM:(g + 1) * HEAD_DIM]
            dqt_ref[...] = jnp.zeros(dqt_ref.shape, F32)

        kk = k_ref[...]
        vv = v_ref[...]

        def lanes(p):
            return slice(p * tq, (p + 1) * tq)

        def products(p):
            st = lax.dot_general(kk, qs_ref[lanes(p), :], nt, preferred_element_type=F32)
            dpt = lax.dot_general(vv, dos_ref[lanes(p), :], nt, preferred_element_type=F32)
            return st, dpt

        dk_c = jnp.zeros((tk, HEAD_DIM), F32)
        dv_c = jnp.zeros((tk, HEAD_DIM), F32)
        ahead = products(0)
        for p in range(GROUP):
            st, dpt = ahead
            if p + 1 < GROUP:
                ahead = products(p + 1)
            pt = jnp.exp2(st * (ATTN_SCALE * LOG2E) - lse_ref[0, p:p + 1, :])
            dst = ((pt * (dpt - dl_ref[0, p:p + 1, :])) * ATTN_SCALE).astype(BF16)
            dv_c = dv_c + jnp.dot(pt.astype(BF16), dos_ref[lanes(p), :], preferred_element_type=F32)
            dk_c = dk_c + jnp.dot(dst, qs_ref[lanes(p), :], preferred_element_type=F32)
            dqt_ref[:, lanes(p)] += lax.dot_general(kk, dst, (((0,), (0,)), ((), ())), preferred_element_type=F32)
        rows = pl.ds(pl.multiple_of(ki * tk, tk), tk)

        @pl.when(qi == 0)
        def _():
            dk_ref[rows, :] = dk_c
            dv_ref[rows, :] = dv_c

        @pl.when(qi > 0)
        def _():
            dk_ref[rows, :] += dk_c
            dv_ref[rows, :] += dv_c

        @pl.when(ki == nk - 1)
        def _():
            dqv = jnp.transpose(dqt_ref[...])
            for g in range(GROUP):
                dq_ref[:, g * HEAD_DIM:(g + 1) * HEAD_DIM] = dqv[g * tq:(g + 1) * tq, :]

    return pl.pallas_call(
        body, name=name, grid=(N_KV_HEADS, T // tq, nk),
        in_specs=[pl.BlockSpec((tq, GW), lambda h, i, j: (i, h)),
                  pl.BlockSpec((tq, GW), lambda h, i, j: (i, h)),
                  pl.BlockSpec((tk, HEAD_DIM), lambda h, i, j: (j, h)),
                  pl.BlockSpec((tk, HEAD_DIM), lambda h, i, j: (j, h)),
                  pl.BlockSpec((1, GROUP, tq), lambda h, i, j: (h, 0, i)),
                  pl.BlockSpec((1, GROUP, tq), lambda h, i, j: (h, 0, i))] +
                 [pl.BlockSpec(t.shape, lambda h, i, j: (0, 0)) for t in extra],
        out_specs=[pl.BlockSpec((tq, GW), lambda h, i, j: (i, h)),
                   pl.BlockSpec((T, HEAD_DIM), lambda h, i, j: (0, h)),
                   pl.BlockSpec((T, HEAD_DIM), lambda h, i, j: (0, h))],
        out_shape=[jax.ShapeDtypeStruct((T, N_Q_HEADS * HEAD_DIM), F32),
                   jax.ShapeDtypeStruct((T, N_KV_HEADS * HEAD_DIM), F32),
                   jax.ShapeDtypeStruct((T, N_KV_HEADS * HEAD_DIM), F32)],
        scratch_shapes=[pltpu.VMEM((GROUP * tq, HEAD_DIM), BF16), pltpu.VMEM((GROUP * tq, HEAD_DIM), BF16),
                        pltpu.VMEM((HEAD_DIM, GROUP * tq), F32)],
        compiler_params=_params(("arbitrary", "arbitrary", "arbitrary")),
    )(q, do, k, v, lse, delta, *extra)


def _gate_specs(D):
    w = D // 2
    first = (3 * D + (N_Q_HEADS + 2 * N_KV_HEADS) * HEAD_DIM) // w
    return [pl.BlockSpec((ROW, w), lambda i, c=first + j: (i, c)) for j in range(4)]


def _merge_fwd(yc, o, P, wbc, wba, wo, D, name):
    T = yc.shape[0]
    w = D // 2

    def body(yc_ref, o_ref, g0, g1, g2, g3, wbc_ref, wba_ref, wo_ref, a1_ref, a2_ref, z_ref, mo_ref):
        a1 = jnp.dot(yc_ref[...], wbc_ref[...], preferred_element_type=F32)
        a2 = jnp.dot(o_ref[...].astype(BF16), wba_ref[...], preferred_element_type=F32)
        a1_ref[...] = a1
        a2_ref[...] = a2
        for j, (gc, ga) in enumerate(((g0, g2), (g1, g3))):
            sl = slice(j * w, (j + 1) * w)
            z = jax.nn.sigmoid(_f32(gc)) * a1[:, sl] + jax.nn.sigmoid(_f32(ga)) * a2[:, sl]
            z_ref[:, sl] = z.astype(BF16)
        mo_ref[...] = jnp.dot(z_ref[...], wo_ref[...], preferred_element_type=F32)

    return pl.pallas_call(
        body, name=name, grid=(T // ROW,),
        in_specs=[_row_spec(D), _row_spec(D)] + _gate_specs(D) + [_resident()] * 3,
        out_specs=[_row_spec(D)] * 4,
        out_shape=[jax.ShapeDtypeStruct((T, D), F32), jax.ShapeDtypeStruct((T, D), F32),
                   jax.ShapeDtypeStruct((T, D), BF16), jax.ShapeDtypeStruct((T, D), F32)],
        compiler_params=_params(("parallel",)),
    )(yc, o, P, P, P, P, wbc, wba, wo)


def _merge_bwd(dmo, a1, a2, o, P, wbc, wba, wo, D, name):
    T = a1.shape[0]
    w = D // 2

    def body(dmo_ref, a1_ref, a2_ref, o_ref, g0, g1, g2, g3, wbc_ref, wba_ref, wo_ref,
             d1_ref, d2_ref, dg_ref, dyc_ref, dob_ref, dl_ref):
        dz = lax.dot_general(dmo_ref[...], wo_ref[...], _NT, preferred_element_type=F32)
        for j, (gc, ga) in enumerate(((g0, g2), (g1, g3))):
            sl = slice(j * w, (j + 1) * w)
            dzs = dz[:, sl]
            sc = jax.nn.sigmoid(_f32(gc))
            sa = jax.nn.sigmoid(_f32(ga))
            d1_ref[:, sl] = (dzs * sc).astype(BF16)
            d2_ref[:, sl] = (dzs * sa).astype(BF16)
            dg_ref[:, j * w:(j + 1) * w] = (dzs * a1_ref[:, sl] * (sc * (1.0 - sc))).astype(BF16)
            dg_ref[:, D + j * w:D + (j + 1) * w] = (dzs * a2_ref[:, sl] * (sa * (1.0 - sa))).astype(BF16)
        dyc_ref[...] = lax.dot_general(d1_ref[...], wbc_ref[...], _NT, preferred_element_type=F32)
        dov = lax.dot_general(d2_ref[...], wba_ref[...], _NT, preferred_element_type=F32)
        dob_ref[...] = dov.astype(BF16)
        prod = dov * o_ref[...]
        for h in range(N_Q_HEADS):
            d = jnp.sum(prod[:, h * HEAD_DIM:(h + 1) * HEAD_DIM], axis=1, keepdims=True)
            dl_ref[h // GROUP, (h % GROUP):(h % GROUP) + 1, :] = _to_row(d, ROW)

    return pl.pallas_call(
        body, name=name, grid=(T // ROW,),
        in_specs=[_row_spec(D)] * 4 + _gate_specs(D) + [_resident()] * 3,
        out_specs=[_row_spec(D), _row_spec(D), _row_spec(2 * D), _row_spec(D), _row_spec(D),
                   pl.BlockSpec((N_KV_HEADS, GROUP, ROW), lambda i: (0, 0, i))],
        out_shape=[jax.ShapeDtypeStruct((T, D), BF16), jax.ShapeDtypeStruct((T, D), BF16),
                   jax.ShapeDtypeStruct((T, 2 * D), BF16), jax.ShapeDtypeStruct((T, D), F32),
                   jax.ShapeDtypeStruct((T, D), BF16), jax.ShapeDtypeStruct((N_KV_HEADS, GROUP, T), F32)],
        compiler_params=_params(("parallel",)),
    )(dmo, a1, a2, o, P, P, P, P, wbc, wba, wo)


def _adamw_math(w, g, m, v):
    m = ADAM_B1 * m + (1.0 - ADAM_B1) * g
    v = ADAM_B2 * v + (1.0 - ADAM_B2) * (g * g)
    m_hat = m / (1.0 - ADAM_B1 ** ADAM_STEP)
    v_hat = v / (1.0 - ADAM_B2 ** ADAM_STEP)
    delta = -ADAM_LR * (m_hat / (jnp.sqrt(v_hat) + ADAM_EPS) + ADAM_WD * w)
    return delta, m, v


def _adamw(w, g, m, v, name):
    R, C = w.shape
    tr = _pick(R, tuple(t for t in (256, 128, 64, 32, 16, 8) if t * C * 4 <= ADAMW_BLOCK_BYTES))

    def body(w_ref, g_ref, m_ref, v_ref, d_ref, mo_ref, vo_ref):
        d, mn, vn = _adamw_math(w_ref[...], g_ref[...], m_ref[...], v_ref[...])
        d_ref[...] = d
        mo_ref[...] = mn
        vo_ref[...] = vn

    spec = pl.BlockSpec((tr, C), lambda i: (i, 0))
    return pl.pallas_call(
        body, name=name, grid=(R // tr,),
        in_specs=[spec] * 4, out_specs=[spec] * 3,
        out_shape=[jax.ShapeDtypeStruct((R, C), F32)] * 3,
        compiler_params=_params(("parallel",)),
    )(w, g, m, v)


def _norm_mix_in_fwd(xprev, branch, mods, g, gate, shift_idx, scale_idx, w_t, name):
    x_specs, x_args, (T, D) = _rows_operand(xprev)
    N = w_t.shape[0]
    cw = _pick(N, (1664, 1024, 512, 256, 128))

    def body(*refs):
        f_ref, m_ref, g_ref, w_ref, xo_ref, h_ref, p_ref = refs[len(x_args):]
        m = m_ref[0]
        gate_idx, fac = gate
        x = _rows_tile(refs[:len(x_args)]) + (fac * m[gate_idx:gate_idx + 1, :]) * f_ref[...]
        xo_ref[...] = x
        hv = _norm_tile_fwd(x, m, g_ref[...], shift_idx, scale_idx)
        h_ref[...] = hv
        for j in range(N // cw):
            p_ref[:, j * cw:(j + 1) * cw] = lax.dot_general(
                hv, w_ref[j * cw:(j + 1) * cw, :], _NT, preferred_element_type=F32).astype(BF16)

    return pl.pallas_call(
        body, name=name, grid=(T // ROW,),
        in_specs=x_specs + [_row_spec(D), _mods_spec(D), _vec_spec(1, D), _resident()],
        out_specs=[_row_spec(D), _row_spec(D), _row_spec(N)],
        out_shape=[jax.ShapeDtypeStruct((T, D), F32), jax.ShapeDtypeStruct((T, D), BF16),
                   jax.ShapeDtypeStruct((T, N), BF16)],
        compiler_params=_params(("parallel",)),
    )(*x_args, branch, mods, g, w_t)


def _mix_in_norm_bwd(parts, w_t, x, dres, mods, g, shift_idx, scale_idx, gate, branch, name):
    T, D = x.shape
    n = len(parts)
    offs = [0]
    for p in parts:
        offs.append(offs[-1] + p.shape[1])
    assert offs[-1] == w_t.shape[0]

    def body(*refs):
        w_ref, x_ref, dr_ref, b_ref, m_ref, g_ref, dx_ref, db_ref, acc_ref = refs[n:]
        _acc_init(acc_ref)
        dh = None
        for i, a_ref in enumerate(refs[:n]):
            d = jnp.dot(a_ref[...], w_ref[offs[i]:offs[i + 1], :], preferred_element_type=F32)
            dh = d if dh is None else dh + d
        m = m_ref[0]
        dx = _norm_tile_bwd(x_ref[...], dh, dr_ref[...], m, g_ref[...], shift_idx, scale_idx, acc_ref)
        dx_ref[...] = dx
        db_ref[...] = _gate_tile_bwd(dx, b_ref[...], m, gate, acc_ref)

    return pl.pallas_call(
        body, name=name, grid=(T // ROW,),
        in_specs=[_row_spec(p.shape[1]) for p in parts] +
                 [_resident(), _row_spec(D), _row_spec(D), _row_spec(D), _mods_spec(D), _vec_spec(1, D)],
        out_specs=[_row_spec(D), _row_spec(D), _acc_spec(D)],
        out_shape=[jax.ShapeDtypeStruct((T, D), F32), jax.ShapeDtypeStruct((T, D), BF16),
                   jax.ShapeDtypeStruct((2, ACC_ROWS, D), F32)],
        compiler_params=_params(("arbitrary",)),
    )(*parts, w_t, x, dres, branch, mods, g)


def _adamw_transposed(w, gt, m, v, name):
    R, C = w.shape
    tc = 128

    def body(w_ref, g_ref, m_ref, v_ref, go_ref, d_ref, mo_ref, vo_ref):
        g = jnp.transpose(g_ref[...])
        d, mn, vn = _adamw_math(w_ref[...], g, m_ref[...], v_ref[...])
        go_ref[...] = g
        d_ref[...] = d
        mo_ref[...] = mn
        vo_ref[...] = vn

    spec = pl.BlockSpec((R, tc), lambda j: (0, j))
    return pl.pallas_call(
        body, name=name, grid=(C // tc,),
        in_specs=[spec, pl.BlockSpec((tc, R), lambda j: (j, 0)), spec, spec], out_specs=[spec] * 4,
        out_shape=[jax.ShapeDtypeStruct((R, C), F32)] * 4,
        compiler_params=_params(("parallel",)),
    )(w, gt, m, v)


class _NoExchange:
    def __init__(self, rest):
        self.rest = rest

    def rest_weights(self, after):
        return self.rest

    def reduce_early(self, grads, tag):
        return None


def _local_step(xcat, target, mods, norm_g, final_g, gq, gk, conv_w, ffn1_w, hooks, ctx_len):
    T, D = _rows_operand(xcat)[2]
    w1i, w1o = ffn1_w
    g1, g2, g3 = norm_g
    cos_t, sin_t = _rope_tables(ctx_len, T - ctx_len)

    def after(value, token, name):
        return value if token is None else _after(value, token, name)

    _, h1, u1, s1, f1 = _norm_ffn_fwd(xcat, None, mods, g1, None, 0, 1, w1i, w1o, "f_ffn1")
    wi, wbc, wba, wo, w2i, w2o = hooks.rest_weights(f1)
    x1, h2, P = _norm_mix_in_fwd(xcat, f1, mods, g2, (2, 0.5), 3, 4, wi, "f_mix_in")
    yc = _conv_fwd(P, conv_w, D, "f_conv")
    qn, kn, vb = _qk_fwd(P, gq, gk, cos_t, sin_t, D, "f_qk")
    o, lse = _flash_fwd(qn, kn, vb, "f_attn")
    a1, a2, z, mo = _merge_fwd(yc, o, P, wbc, wba, wo, D, "f_merge")
    x2, h3, u2, s2, dx3, df2, acc_head = _norm_ffn_fwd(x1, mo, mods, g3, (5, 1.0), 6, 7, w2i, w2o, "f_ffn2",
                                                       head=(final_g, target))

    du2, dx2, dmo, acc_n3 = _ffn_norm_bwd(df2, u2, w2i, w2o, x2, dx3, mods, g3, 6, 7, (5, 1.0), mo, "b_ffn2")
    g_w2o = _matmul(s2, df2, "tn", BF16, "b_ffn2_out_dw")
    g_w2i = _matmul(du2, h3, "tn", BF16, "b_ffn2_in_dw")

    g_wo = _matmul(z, dmo, "tn", BF16, "b_mix_out_dw")
    da1, da2, dgt, dyc, dob, delta = _merge_bwd(dmo, a1, a2, o, P, wbc, wba, wo, D, "b_merge")
    g_wbc = _matmul(yc, da1, "tn", BF16, "b_branch_conv_dw")
    g_wba = _matmul(o, da2, "tn", BF16, "b_branch_attn_dw")
    token_a = hooks.reduce_early([g_wbc, g_wba, g_wo, g_w2i, g_w2o], "a")
    dq, dk, dv = _flash_bwd(qn, kn, vb, dob, lse, delta, "b_attn", token=token_a)
    dqkv, acc_qk = _qk_bwd(P, dq, dk, dv, gq, gk, cos_t, sin_t, D, "b_qk")
    dconv, acc_conv = _conv_bwd(P, dyc, conv_w, D, "b_conv")
    d_parts = (dconv, dqkv, dgt)
    dx1, df1, acc_n2 = _mix_in_norm_bwd(d_parts, wi, x1, dx2, mods, g2, 3, 4, (2, 0.5), f1, "b_mix_in")
    g_wi = jnp.concatenate([_matmul(dp, h2, "tn", BF16, f"b_mix_in_dw_{i}") for i, dp in enumerate(d_parts)], axis=0)
    g1_b = after(g1, hooks.reduce_early([g_wi], "b"), "after_rs_b")

    du1, grad_x, _, acc_n1 = _ffn_norm_bwd(df1, u1, w1i, w1o, xcat, dx1, mods, g1_b, 0, 1, None, None, "b_ffn1",
                                           skip_first_tile=True)
    g_w1o = _matmul(s1, df1, "tn", BF16, "b_ffn1_out_dw")
    g_w1i = _matmul(du1, h1, "tn", BF16, "b_ffn1_in_dw", token=hooks.reduce_early([g_w1o], "c"))

    grads = (g_w1i, g_w1o, g_wi, g_wbc, g_wba, g_wo, g_w2i, g_w2o)
    accs = (acc_head, acc_n3, acc_n2, acc_n1, acc_conv, acc_qk)
    return grad_x, grads, accs


def _place():
    return lax.axis_index("x"), lax.axis_index("y"), lax.axis_index("c")


def _other_chips(x, y):
    return [(1 - x, y), (x, 1 - y), (1 - x, 1 - y)]


def _allgather8(v, name):
    R, N = v.shape

    def body(v_ref, out_ref, send_sems, recv_sems, local_sem):
        x, y, c = _place()
        me, sibling = (x, y, c), (x, y, 1 - c)
        chips = _other_chips(x, y)

        def blk(px, py, pc):
            return out_ref.at[4 * px + 2 * py + pc]

        def copy(k, block, to, src=None):
            return pltpu.make_async_remote_copy(
                src_ref=blk(*block) if src is None else src, dst_ref=blk(*block),
                send_sem=send_sems.at[k], recv_sem=recv_sems.at[k], device_id=to, device_id_type=MESH)

        mine = pltpu.make_async_copy(v_ref, blk(*me), local_sem)
        mine.start()
        first = [copy(0, me, sibling, src=v_ref)]
        first += [copy(1 + j, me, (*chip, c), src=v_ref) for j, chip in enumerate(chips)]
        for cp in first:
            cp.start()
        passed = [copy(4 + j, (*chip, c), sibling) for j, chip in enumerate(chips)]
        for j, chip in enumerate(chips):
            copy(1 + j, (*chip, c), me).wait_recv()
            passed[j].start()
        copy(0, sibling, me).wait_recv()
        for j, chip in enumerate(chips):
            copy(4 + j, (*chip, 1 - c), me).wait_recv()
        for cp in first + passed:
            cp.wait_send()
        mine.wait()

    return pl.pallas_call(
        body, name=name,
        out_shape=jax.ShapeDtypeStruct((N_DEV, R, N), v.dtype),
        in_specs=[pl.BlockSpec(memory_space=pltpu.VMEM)],
        out_specs=pl.BlockSpec(memory_space=pltpu.VMEM),
        scratch_shapes=[pltpu.SemaphoreType.DMA((7,)), pltpu.SemaphoreType.DMA((7,)), pltpu.SemaphoreType.DMA],
        compiler_params=pltpu.CompilerParams(vmem_limit_bytes=VMEM_LIMIT),
    )(v)


def _any_specs(n):
    return [pl.BlockSpec(memory_space=pl.ANY)] * n


def _pair_exchange(grads, name):
    n = len(grads)

    def body(*refs):
        g, land = refs[:n], refs[n:2 * n]
        send_sems, recv_sems = refs[2 * n:]
        x, y, c = _place()
        sibling = (x, y, 1 - c)
        copies = []
        for t in range(n):
            half = grads[t].shape[0] // (2 * N_CHIPS)
            for s in range(N_CHIPS):
                cp = pltpu.make_async_remote_copy(
                    src_ref=g[t].at[pl.ds((2 * s + 1 - c) * half, half), :], dst_ref=land[t].at[s],
                    send_sem=send_sems.at[N_CHIPS * t + s], recv_sem=recv_sems.at[N_CHIPS * t + s],
                    device_id=sibling, device_id_type=MESH)
                cp.start()
                copies.append(cp)
        for cp in copies:
            cp.wait_recv()
        for cp in copies:
            cp.wait_send()

    return pl.pallas_call(
        body, name=name,
        out_shape=[jax.ShapeDtypeStruct((N_CHIPS, a.shape[0] // (2 * N_CHIPS), a.shape[1]), a.dtype) for a in grads],
        in_specs=_any_specs(n), out_specs=_any_specs(n),
        scratch_shapes=[pltpu.SemaphoreType.DMA((N_CHIPS * n,)), pltpu.SemaphoreType.DMA((N_CHIPS * n,))],
    )(*grads)


def _place_shard(w2, idx, transpose, name, token):
    if transpose:
        D, rs = w2.shape
        tr = 128
        in_spec = pl.BlockSpec((D, tr), lambda i, idx: (0, i))
    else:
        rs, D = w2.shape
        tr = _pick(rs, (352, 256, 128, 64, 32, 16))
        in_spec = pl.BlockSpec((tr, D), lambda i, idx: (i, 0))
    steps = rs // tr

    def body(idx_ref, w_ref, t_ref, o_ref):
        v = w_ref[...]
        o_ref[...] = (jnp.transpose(v) if transpose else v).astype(BF16)

    return pl.pallas_call(
        body, name=name,
        grid_spec=pltpu.PrefetchScalarGridSpec(
            num_scalar_prefetch=1, grid=(steps,),
            in_specs=[in_spec, pl.BlockSpec(token.shape, lambda i, idx: (0, 0))],
            out_specs=pl.BlockSpec((tr, D), lambda i, idx: (idx[1] * steps + i, 0))),
        out_shape=jax.ShapeDtypeStruct((N_CHIPS * rs, D), BF16),
        compiler_params=_params(("arbitrary",)),
    )(idx, w2, token)


def _pair_sum(g, landed, idx, name, token=None):
    _, half, D = landed.shape
    g4 = g.reshape(N_CHIPS, 2, half, D)
    tr = _pick(half, (416, 352, 128))
    extra = [] if token is None else [token]

    def body(idx_ref, g_ref, l_ref, *rest):
        rest[-1][...] = (g_ref[0].astype(F32) + l_ref[...].astype(F32)).astype(BF16)

    return pl.pallas_call(
        body, name=name,
        grid_spec=pltpu.PrefetchScalarGridSpec(
            num_scalar_prefetch=1, grid=(N_CHIPS, half // tr),
            in_specs=[pl.BlockSpec((1, 1, tr, D), lambda s, i, idx: (idx[1 + s], idx[0], i, 0)),
                      pl.BlockSpec((1, tr, D), lambda s, i, idx: (idx[1 + s], i, 0))] +
                     [pl.BlockSpec(t.shape, lambda s, i, idx: (0, 0)) for t in extra],
            out_specs=pl.BlockSpec((1, tr, D), lambda s, i, idx: (s, i, 0))),
        out_shape=jax.ShapeDtypeStruct((N_CHIPS, half, D), BF16),
        compiler_params=_params(("arbitrary", "arbitrary")),
    )(idx, g4, landed, *extra)


_HBM = pl.BlockSpec(memory_space=pltpu.HBM)
_SEM = pl.BlockSpec(memory_space=pltpu.SEMAPHORE)
_EFFECT = pltpu.SideEffectType.DATAFLOW_SIDE_EFFECTING


def _in_hbm(a):
    return pltpu.with_memory_space_constraint(a, pltpu.HBM)


def _split_copies(n, per, make):
    def start(nbuf, name, bufs):
        def body(*refs):
            ins = refs[:nbuf]
            send_sems, recv_sems = refs[nbuf], refs[nbuf + 1]
            token = refs[-1]
            for t in range(n):
                for j in range(per):
                    make(ins, t, j, send_sems.at[per * t + j], recv_sems.at[per * t + j]).start()
            token[...] = jnp.zeros(token.shape, token.dtype)

        out = pl.pallas_call(
            body, name=name,
            out_shape=(pltpu.SemaphoreType.DMA((per * n,)), pltpu.SemaphoreType.DMA((per * n,)),
                       *[pltpu.HBM(b.shape, b.dtype) for b in bufs], jax.ShapeDtypeStruct((8, 128), F32)),
            in_specs=[_HBM] * nbuf,
            out_specs=(_SEM, _SEM, *[_HBM] * nbuf, pl.BlockSpec(memory_space=pltpu.VMEM)),
            input_output_aliases={i: 2 + i for i in range(nbuf)},
            compiler_params=pltpu.CompilerParams(has_side_effects=_EFFECT),
        )(*[_in_hbm(b) for b in bufs])
        return out[0], out[1], list(out[2:2 + nbuf]), out[-1]

    def wait(nbuf, name, send_sems, recv_sems, bufs, after):
        def body(*refs):
            ins = refs[:nbuf]
            ss, rs = refs[nbuf], refs[nbuf + 1]
            for t in range(n):
                for j in range(per):
                    cp = make(ins, t, j, ss.at[per * t + j], rs.at[per * t + j])
                    cp.wait_send()
                    cp.wait_recv()

        return pl.pallas_call(
            body, name=name,
            out_shape=[pltpu.HBM(b.shape, b.dtype) for b in bufs],
            in_specs=[_HBM] * nbuf + [_SEM, _SEM, pl.BlockSpec(memory_space=pl.ANY)],
            out_specs=[_HBM] * nbuf,
            input_output_aliases={i: i for i in range(nbuf)},
            compiler_params=pltpu.CompilerParams(has_side_effects=_EFFECT),
        )(*bufs, send_sems, recv_sems, after)

    return start, wait


def _chip_exchange_split(n):
    def make(bufs, t, j, send_sem, recv_sem):
        x, y, c = _place()
        chip = _other_chips(x, y)[j]
        return pltpu.make_async_remote_copy(src_ref=bufs[t].at[1 + j], dst_ref=bufs[n + t].at[j], send_sem=send_sem,
                                            recv_sem=recv_sem, device_id=(*chip, c), device_id_type=MESH)

    return _split_copies(n, 3, make)


def _weights_gather_split(fulls):
    def make(bufs, t, j, send_sem, recv_sem):
        x, y, c = _place()
        chip = _other_chips(x, y)[j]
        rs = fulls[t].shape[0] // N_CHIPS
        rows = bufs[t].at[pl.ds((2 * x + y) * rs + c * (rs // 2), rs // 2), :]
        return pltpu.make_async_remote_copy(src_ref=rows, dst_ref=rows, send_sem=send_sem, recv_sem=recv_sem,
                                            device_id=(*chip, c), device_id_type=MESH)

    return _split_copies(len(fulls), 3, make)


def _weights_pass_on(fulls, name):
    n = len(fulls)

    def body(*refs):
        full = refs[n:2 * n]
        send_sems, recv_sems = refs[2 * n:]
        x, y, c = _place()
        chips = _other_chips(x, y)

        def copy(t, j, h):
            rs = fulls[t].shape[0] // N_CHIPS
            px, py = chips[j]
            rows = full[t].at[pl.ds((2 * px + py) * rs + h * (rs // 2), rs // 2), :]
            return pltpu.make_async_remote_copy(src_ref=rows, dst_ref=rows, send_sem=send_sems.at[3 * t + j],
                                                recv_sem=recv_sems.at[3 * t + j], device_id=(x, y, 1 - c),
                                                device_id_type=MESH)

        for t in range(n):
            for j in range(3):
                copy(t, j, c).start()
        for t in range(n):
            for j in range(3):
                copy(t, j, 1 - c).wait_recv()
        for t in range(n):
            for j in range(3):
                copy(t, j, c).wait_send()

    return pl.pallas_call(
        body, name=name,
        out_shape=[jax.ShapeDtypeStruct(f.shape, f.dtype) for f in fulls],
        in_specs=_any_specs(n), out_specs=_any_specs(n),
        input_output_aliases={t: t for t in range(n)},
        scratch_shapes=[pltpu.SemaphoreType.DMA((3 * n,)), pltpu.SemaphoreType.DMA((3 * n,))],
    )(*fulls)


def _after(value, token, name):
    def body(v_ref, t_ref, o_ref):
        o_ref[...] = v_ref[...]

    return pl.pallas_call(
        body, name=name, out_shape=jax.ShapeDtypeStruct(value.shape, value.dtype),
        in_specs=_whole(2), out_specs=pl.BlockSpec(memory_space=pltpu.VMEM),
    )(value, token)


def _chip_sum(ps, landed, idx, name):
    _, half, D = ps.shape
    tr = _pick(half, (416, 352, 128))
    steps = half // tr

    def body(idx_ref, p_ref, l_ref, o_ref):
        acc = p_ref[0].astype(F32)
        for j in range(3):
            acc = acc + l_ref[j].astype(F32)
        o_ref[...] = acc

    return pl.pallas_call(
        body, name=name,
        grid_spec=pltpu.PrefetchScalarGridSpec(
            num_scalar_prefetch=1, grid=(steps,),
            in_specs=[pl.BlockSpec((1, tr, D), lambda i, idx: (0, i, 0)),
                      pl.BlockSpec((3, tr, D), lambda i, idx: (0, i, 0))],
            out_specs=pl.BlockSpec((tr, D), lambda i, idx: (idx[0] * steps + i, 0))),
        out_shape=jax.ShapeDtypeStruct((2 * half, D), F32),
        compiler_params=_params(("arbitrary",)),
    )(idx, ps, landed)


def _pair_swap(shards, name):
    n = len(shards)

    def body(*refs):
        full = refs[n:2 * n]
        send_sems, recv_sems = refs[2 * n:]
        x, y, c = _place()

        def half(t, h):
            rows = shards[t].shape[0] // 2
            return full[t].at[pl.ds(h * rows, rows), :]

        def copy(t, h):
            return pltpu.make_async_remote_copy(src_ref=half(t, h), dst_ref=half(t, h), send_sem=send_sems.at[t],
                                                recv_sem=recv_sems.at[t], device_id=(x, y, 1 - c),
                                                device_id_type=MESH)

        for t in range(n):
            copy(t, c).start()
        for t in range(n):
            copy(t, 1 - c).wait_recv()
        for t in range(n):
            copy(t, c).wait_send()

    return pl.pallas_call(
        body, name=name,
        out_shape=[jax.ShapeDtypeStruct(a.shape, a.dtype) for a in shards],
        in_specs=_any_specs(n), out_specs=_any_specs(n),
        input_output_aliases={t: t for t in range(n)},
        scratch_shapes=[pltpu.SemaphoreType.DMA((n,)), pltpu.SemaphoreType.DMA((n,))],
    )(*shards)


def _gather_begin(fulls, tag):
    start, wait = _weights_gather_split(fulls)
    send_sems, recv_sems, bufs, token = start(len(fulls), f"ag_{tag}_start", fulls)
    return (wait, send_sems, recv_sems, bufs), token


def _gather_end(state, after, tag):
    wait, send_sems, recv_sems, bufs = state
    landed = wait(len(bufs), f"ag_{tag}_wait", send_sems, recv_sems, bufs, after)
    return _weights_pass_on(landed, f"ag_{tag}_pass_on")


class _Exchanges:
    def __init__(self, fulls_rest, idx):
        self.idx = idx
        self._rest, self.token = _gather_begin(fulls_rest, "rest")
        self._early = []

    def rest_weights(self, after):
        return _gather_end(self._rest, after, "rest")

    def reduce_early(self, grads, tag, token=None):
        landed = _pair_exchange(grads, "rs_pair_exchange_" + tag)
        sums = [_pair_sum(g, l, self.idx, f"rs_pair_sum_{tag}{t}", token)
                for t, (g, l) in enumerate(zip(grads, landed))]
        zones = [lax.empty((3,) + s.shape[1:], s.dtype) for s in sums]
        start, wait = _chip_exchange_split(len(sums))
        send_sems, recv_sems, bufs, token = start(2 * len(sums), "rs_chip_start_" + tag, sums + zones)
        self._early.append((tag, wait, send_sems, recv_sems, bufs))
        return token

    def finish(self, tags, after):
        halves = []
        for tag, wait, send_sems, recv_sems, bufs in self._early:
            if tag in tags:
                n = len(bufs) // 2
                done = wait(len(bufs), "rs_chip_wait_" + tag, send_sems, recv_sems, bufs, after)
                halves += [_chip_sum(p, l, self.idx, f"rs_chip_sum_{tag}{t}")
                           for t, (p, l) in enumerate(zip(done[:n], done[n:]))]
        return halves


N_MOD = 9
PACK_HEAD, PACK_N3, PACK_N2, PACK_N1, PACK_CONV, PACK_QK = 0, 16, 32, 48, 64, 80
PACK_ROWS = 96
MOD_SRC = ((PACK_N1, 0), (PACK_N1, 1), (PACK_N2, 3), (PACK_N2, 0), (PACK_N2, 1),
           (PACK_N3, 3), (PACK_N3, 0), (PACK_N3, 1), (PACK_HEAD, 2))
CTX_ROW = 8


def _silu(v):
    return v * jax.nn.sigmoid(v)


def _whole(n):
    return [pl.BlockSpec(memory_space=pltpu.VMEM)] * n


def _mod_rows(cin, w_sh, b_sh, name):
    def body(c_ref, w_ref, b_ref, o_ref):
        a = _silu(c_ref[...]).astype(BF16)
        o_ref[...] = jnp.dot(a, w_ref[...].astype(BF16), preferred_element_type=F32) + b_ref[...]

    return pl.pallas_call(
        body, name=name, out_shape=jax.ShapeDtypeStruct((cin.shape[0], w_sh.shape[1]), F32),
        in_specs=_whole(3), out_specs=pl.BlockSpec(memory_space=pltpu.VMEM),
        compiler_params=pltpu.CompilerParams(vmem_limit_bytes=VMEM_LIMIT),
    )(cin, w_sh, b_sh)


def _small_reduce(gathered, name):
    _, _, D = gathered.shape

    def body(g_ref, loss_ref, db_ref, gn_ref, cv_ref, qk_ref, dm_ref):
        tot = g_ref[0]
        for r in range(1, N_DEV):
            tot = tot + g_ref[r]

        def both(block, row):
            return tot[block + row:block + row + 1, :] + tot[block + 8 + row:block + 8 + row + 1, :]

        loss = jnp.sum(both(PACK_HEAD, 0), axis=1, keepdims=True)
        loss_ref[...] = jnp.broadcast_to(loss, loss_ref.shape)
        db_ref[...] = jnp.zeros(db_ref.shape, F32)
        dm_ref[...] = jnp.zeros(dm_ref.shape, F32)
        for j, (block, row) in enumerate(MOD_SRC):
            db_ref[j:j + 1, :] = both(block, row)
            dm_ref[CTX_ROW, j:j + 1, :] = tot[block + row:block + row + 1, :]
            for r in range(N_DEV):
                dm_ref[r, j:j + 1, :] = g_ref[r, block + 8 + row:block + 8 + row + 1, :]
        gn_ref[...] = jnp.zeros(gn_ref.shape, F32)
        gn_ref[0:1, :] = both(PACK_N1, 2)
        gn_ref[8:9, :] = both(PACK_N2, 2)
        gn_ref[16:17, :] = both(PACK_N3, 2)
        gn_ref[24:25, :] = both(PACK_HEAD, 1)
        cv_ref[...] = jnp.zeros(cv_ref.shape, F32)
        for r in range(3):
            cv_ref[r:r + 1, :] = both(PACK_CONV, r)
        qk_ref[...] = jnp.zeros(qk_ref.shape, F32)
        qk_ref[0:1, 0:HEAD_DIM] = both(PACK_QK, 0)[:, 0:HEAD_DIM]
        qk_ref[0:1, HEAD_DIM:2 * HEAD_DIM] = both(PACK_QK, 1)[:, 0:HEAD_DIM]

    return pl.pallas_call(
        body, name=name,
        out_shape=[jax.ShapeDtypeStruct((8, 128), F32), jax.ShapeDtypeStruct((16, D), F32),
                   jax.ShapeDtypeStruct((32, D), F32), jax.ShapeDtypeStruct((8, D), F32),
                   jax.ShapeDtypeStruct((8, D), F32), jax.ShapeDtypeStruct((16, 16, D), F32)],
        in_specs=_whole(1), out_specs=_whole(6),
        compiler_params=pltpu.CompilerParams(vmem_limit_bytes=VMEM_LIMIT),
    )(gathered)


def _wmod_grad(cin, dm_sh, w_sh, name):
    def body(c_ref, d_ref, w_ref, gw_ref, cp_ref):
        a = _silu(c_ref[...]).astype(BF16)
        d = d_ref[...].astype(BF16)
        gw_ref[...] = lax.dot_general(a, d, (((0,), (0,)), ((), ())), preferred_element_type=F32)
        cp_ref[...] = lax.dot_general(d, w_ref[...].astype(BF16), (((1,), (1,)), ((), ())),
                                      preferred_element_type=F32)

    return pl.pallas_call(
        body, name=name,
        out_shape=[jax.ShapeDtypeStruct(w_sh.shape, F32), jax.ShapeDtypeStruct(cin.shape, F32)],
        in_specs=_whole(3), out_specs=_whole(2),
        compiler_params=pltpu.CompilerParams(vmem_limit_bytes=VMEM_LIMIT),
    )(cin, dm_sh, w_sh)


def _cctx_grad(parts, c_ctx8, name):
    def body(p_ref, c_ref, o_ref):
        tot = p_ref[0] + p_ref[2] + p_ref[4] + p_ref[6]
        cv = c_ref[...]
        sig = jax.nn.sigmoid(cv)
        rows = lax.broadcasted_iota(jnp.int32, tot.shape, 0)
        o_ref[...] = jnp.where(rows == 0, tot * (sig * (1.0 + cv * (1.0 - sig))), 0.0)

    return pl.pallas_call(
        body, name=name, out_shape=jax.ShapeDtypeStruct(c_ctx8.shape, F32),
        in_specs=_whole(2), out_specs=pl.BlockSpec(memory_space=pltpu.VMEM),
    )(parts, c_ctx8)


def _pad_rows(a, rows):
    return jnp.pad(a, ((0, rows - a.shape[0]), (0, 0)))


def _pack_small(c_ctx, b_mod, n1, n2, n3, final_g, gq, gk, conv_sh, D):
    misc = jnp.concatenate([gq, gk, conv_sh.reshape(1, -1)], axis=1)
    return jnp.concatenate([_pad_rows(c_ctx[None], 8), _pad_rows(b_mod.reshape(N_MOD, D), 16), _pad_rows(n1, 8),
                            _pad_rows(n2, 8), _pad_rows(n3, 8), _pad_rows(final_g[None], 8), _pad_rows(misc, 8)], axis=0)


def _unpack_small(p, D, conv_shape):
    misc = p[56:57]
    return dict(c_ctx=p[0], b_mod=p[8:8 + N_MOD].reshape(1, N_MOD * D), norm1_g=p[24:25], norm2_g=p[32:33],
                norm3_g=p[40:41], final_g=p[48], q_norm_g=misc[:, 0:HEAD_DIM], k_norm_g=misc[:, HEAD_DIM:2 * HEAD_DIM],
                conv_w=misc[:, 2 * HEAD_DIM:].reshape(conv_shape))


WEIGHT_ORDER = ("c_ctx", "w_mod", "b_mod", "norm1_g", "norm2_g", "norm3_g", "ffn1_w_in", "ffn1_w_out", "w_in",
                "conv_w", "q_norm_g", "k_norm_g", "w_branch_conv", "w_branch_attn", "w_out", "ffn2_w_in",
                "ffn2_w_out", "final_g")
BIG = ("ffn1_w_in", "ffn1_w_out", "w_in", "w_branch_conv", "w_branch_attn", "w_out", "ffn2_w_in", "ffn2_w_out")
COLUMN_SHARDED = ("ffn1_w_in", "w_in", "ffn2_w_in")


def kernel(x, c, ctx, c_ctx, w_mod, b_mod, norm1_g, norm2_g, norm3_g, ffn1_w_in, ffn1_w_out, w_in, conv_w, q_norm_g, k_norm_g, w_branch_conv, w_branch_attn, w_out, ffn2_w_in, ffn2_w_out, final_g, loss_target, m_c_ctx, m_w_mod, m_b_mod, m_norm1_g, m_norm2_g, m_norm3_g, m_ffn1_w_in, m_ffn1_w_out, m_w_in, m_conv_w, m_q_norm_g, m_k_norm_g, m_w_branch_conv, m_w_branch_attn, m_w_out, m_ffn2_w_in, m_ffn2_w_out, m_final_g, v_c_ctx, v_w_mod, v_b_mod, v_norm1_g, v_norm2_g, v_norm3_g, v_ffn1_w_in, v_ffn1_w_out, v_w_in, v_conv_w, v_q_norm_g, v_k_norm_g, v_w_branch_conv, v_w_branch_attn, v_w_out, v_ffn2_w_in, v_ffn2_w_out, v_final_g):
    w = dict(c_ctx=c_ctx, w_mod=w_mod, b_mod=b_mod, norm1_g=norm1_g, norm2_g=norm2_g, norm3_g=norm3_g,
             ffn1_w_in=ffn1_w_in, ffn1_w_out=ffn1_w_out, w_in=w_in, conv_w=conv_w, q_norm_g=q_norm_g,
             k_norm_g=k_norm_g, w_branch_conv=w_branch_conv, w_branch_attn=w_branch_attn, w_out=w_out,
             ffn2_w_in=ffn2_w_in, ffn2_w_out=ffn2_w_out, final_g=final_g)
    m = dict(c_ctx=m_c_ctx, w_mod=m_w_mod, b_mod=m_b_mod, norm1_g=m_norm1_g, norm2_g=m_norm2_g, norm3_g=m_norm3_g,
             ffn1_w_in=m_ffn1_w_in, ffn1_w_out=m_ffn1_w_out, w_in=m_w_in, conv_w=m_conv_w, q_norm_g=m_q_norm_g,
             k_norm_g=m_k_norm_g, w_branch_conv=m_w_branch_conv, w_branch_attn=m_w_branch_attn, w_out=m_w_out,
             ffn2_w_in=m_ffn2_w_in, ffn2_w_out=m_ffn2_w_out, final_g=m_final_g)
    v = dict(c_ctx=v_c_ctx, w_mod=v_w_mod, b_mod=v_b_mod, norm1_g=v_norm1_g, norm2_g=v_norm2_g, norm3_g=v_norm3_g,
             ffn1_w_in=v_ffn1_w_in, ffn1_w_out=v_ffn1_w_out, w_in=v_w_in, conv_w=v_conv_w, q_norm_g=v_q_norm_g,
             k_norm_g=v_k_norm_g, w_branch_conv=v_w_branch_conv, w_branch_attn=v_w_branch_attn, w_out=v_w_out,
             ffn2_w_in=v_ffn2_w_in, ffn2_w_out=v_ffn2_w_out, final_g=v_final_g)

    xi, yi, ci = _place()
    dev = 4 * xi + 2 * yi + ci
    shard = 2 * xi + yi
    idx = jnp.stack([ci, shard, 2 * (1 - xi) + yi, 2 * xi + (1 - yi), 2 * (1 - xi) + (1 - yi)]).astype(jnp.int32)
    D = x.shape[-1]
    ctx_len = ctx.shape[1]
    assert ctx_len == ROW and c.shape == (1, D)
    mcols = w_mod.shape[2]
    ccols = conv_w.shape[2]

    c_all = _allgather8(jnp.broadcast_to(c, (8, D)), "ag_c")[:, 0, :]
    cin = jnp.concatenate([c_all, _pad_rows(c_ctx[None], 8)], axis=0)
    b_sh = lax.dynamic_slice(b_mod, (0, shard * mcols), (1, mcols))
    mod_sh = _mod_rows(cin, w_mod[0], b_sh, "mod_rows")
    conv_rows = jnp.pad(conv_w[0], ((0, 8 - conv_w.shape[1]), (0, mcols - ccols)))
    mod_all = _allgather8(jnp.concatenate([mod_sh, conv_rows], axis=0), "ag_mod")
    mod_full = jnp.concatenate([mod_all[2 * s, :16] for s in range(N_CHIPS)], axis=1)
    conv_full = jnp.concatenate([mod_all[2 * s, 16:16 + conv_w.shape[1], :ccols] for s in range(N_CHIPS)], axis=1)
    mod_lat = lax.dynamic_slice(mod_full, (dev, 0), (1, N_MOD * D)).reshape(N_MOD, D)
    mod_ctx = mod_full[CTX_ROW].reshape(N_MOD, D)
    mods = jnp.stack([_pad_rows(mod_ctx, 16), _pad_rows(mod_lat, 16)])

    def place(names, token):
        return [_place_shard(w[n][0], idx, n in COLUMN_SHARDED, "place_" + n, token) for n in names]

    ffn1_gather, ffn1_token = _gather_begin(place(BIG[:2], mod_all[0, :8, :HEAD_DIM]), "ffn1")
    fulls_rest = place(BIG[2:], ffn1_token)
    ffn1_w = _gather_end(ffn1_gather, fulls_rest[-1][:16, :HEAD_DIM], "ffn1")
    hooks = _Exchanges(fulls_rest, idx)

    xcat = (ctx[0], x[0])
    norm1_first = _after(norm1_g, hooks.token, "after_ag_rest")
    grad_x, grads, accs = _local_step(xcat, loss_target[0], mods, (norm1_first, norm2_g, norm3_g), final_g[None],
                                      q_norm_g, k_norm_g, conv_full, ffn1_w, hooks, ctx_len)
    g = {}

    pack = jnp.concatenate([a.reshape(2 * ACC_ROWS, D) for a in accs], axis=0)
    gathered = _allgather8(pack, "ag_small")
    loss8, db_mod, g_norms, g_conv, g_qk, dm = _small_reduce(gathered, "small_reduce")
    dm_sh = lax.dynamic_slice(dm[:, :N_MOD, :].reshape(16, N_MOD * D), (0, shard * mcols), (16, mcols))
    g_wmod, cpart = _wmod_grad(cin, dm_sh, w_mod[0], "wmod_grad")
    g["w_mod"] = g_wmod[None]
    cparts = _allgather8(cpart[CTX_ROW:CTX_ROW + 8], "ag_cctx")
    g_cctx = _cctx_grad(cparts, _pad_rows(c_ctx[None], 8), "cctx_grad")
    g_conv_sh = lax.dynamic_slice(g_conv, (0, shard * ccols), (conv_w.shape[1], ccols))
    g_misc = jnp.concatenate([g_qk[0:1, 0:2 * HEAD_DIM], g_conv_sh.reshape(1, -1)], axis=1)
    g_pack = jnp.concatenate([g_cctx, db_mod, g_norms, _pad_rows(g_misc, 8)], axis=0)

    def packed(p):
        return _pack_small(p["c_ctx"], p["b_mod"], p["norm1_g"], p["norm2_g"], p["norm3_g"], p["final_g"],
                           p["q_norm_g"], p["k_norm_g"], p["conv_w"][0], D)

    d_pack, m_pack, v_pack = _adamw(packed(w), g_pack, packed(m), packed(v), "adamw_small")

    g.update(_unpack_small(g_pack, D, conv_w.shape))
    delta = _unpack_small(d_pack, D, conv_w.shape)
    new_m = _unpack_small(m_pack, D, conv_w.shape)
    new_v = _unpack_small(v_pack, D, conv_w.shape)

    def update(n, g2):
        if n in COLUMN_SHARDED:
            g2, d2, m2, v2 = _adamw_transposed(w[n][0], g2, m[n][0], v[n][0], "adamw_" + n)
        else:
            d2, m2, v2 = _adamw(w[n][0], g2, m[n][0], v[n][0], "adamw_" + n)
        g[n], delta[n], new_m[n], new_v[n] = g2[None], d2[None], m2[None], v2[None]
        return v2

    token_d = hooks.reduce_early([grads[0]], "d", token=d_pack[:8, :HEAD_DIM])
    h_wbc, h_wba, h_wo, h_w2i, h_w2o, h_wi, h_w1o = hooks.finish("abc", token_d)
    done = _pair_swap([h_w1o, h_wi, h_wbc, h_wba, h_wo, h_w2i, h_w2o], "rs_pair_swap")
    last = update("w_mod", g_wmod)
    for n, r in zip(BIG[1:], done):
        last = update(n, r)
    (h_w1i,) = hooks.finish("d", last)
    update(BIG[0], _pair_swap([h_w1i], "rs_pair_swap_d")[0])

    loss = loss8[0, 0]
    return (loss, grad_x[None], *[g[n] for n in WEIGHT_ORDER], *[delta[n] for n in WEIGHT_ORDER],
            *[new_m[n] for n in WEIGHT_ORDER], *[new_v[n] for n in WEIGHT_ORDER])
```

```python
import functools

import jax
import jax.numpy as jnp
from jax import lax
from jax.experimental import pallas as pl
from jax.experimental.pallas import tpu as pltpu

F32 = jnp.float32
BF16 = jnp.bfloat16

HEAD_DIM = 128
N_Q_HEADS = 8
N_KV_HEADS = 2
GROUP = N_Q_HEADS // N_KV_HEADS
GRID_W = 64
ROPE_THETA = 10000.0
EPS = 1e-6
ATTN_SCALE = HEAD_DIM ** -0.5

ADAM_LR = 0.001
ADAM_B1 = 0.9
ADAM_B2 = 0.999
ADAM_EPS = 1e-08
ADAM_WD = 0.01
ADAM_STEP = 10

ROW = 256
HALO = 16
ACC_ROWS = 8
N_CHIPS = 4
N_DEV = 8
MESH = pl.DeviceIdType.MESH
VMEM_LIMIT = 48 * 1024 * 1024
ADAMW_BLOCK_BYTES = 1024 * 1024


def _pick(n, prefs):
    for p in prefs:
        if n % p == 0:
            return p
    return n


def _params(sem):
    return pltpu.CompilerParams(dimension_semantics=sem, vmem_limit_bytes=VMEM_LIMIT)


def _stream(i):
    return jnp.minimum(i, 1)


def _matmul(a, b, mode, out_dtype, name, tm=None, tn=None, tk=None, token=None):
    if mode == "nn":
        (M, K), (K2, N) = a.shape, b.shape
    elif mode == "nt":
        (M, K), (N, K2) = a.shape, b.shape
    else:
        (K, M), (K2, N) = a.shape, b.shape
    assert K == K2, (a.shape, b.shape, mode)
    tm = tm or _pick(M, (1664, 1408, 1024, 512, 256, 128) if mode == "tn" else (1408, 768, 512, 256, 128))
    tn = tn or _pick(N, (1664, 1408, 1024, 512, 256, 128))
    tk = tk or _pick(K, (1664, 1408, 1024, 768, 512, 256, 128))
    nk = K // tk
    if mode == "tn":
        a_spec = pl.BlockSpec((tk, tm), lambda i, j, k: (k, i))
    else:
        a_spec = pl.BlockSpec((tm, tk), lambda i, j, k: (i, k))
    if mode == "nt":
        b_spec = pl.BlockSpec((tn, tk), lambda i, j, k: (j, k))
    else:
        b_spec = pl.BlockSpec((tk, tn), lambda i, j, k: (k, j))
    dims = {"nn": ((1,), (0,)), "nt": ((1,), (1,)), "tn": ((0,), (0,))}[mode]
    use_scratch = nk > 1 and out_dtype != F32

    extra = [] if token is None else [token]

    def body(a_ref, b_ref, *rest):
        o_ref, scratch = rest[len(extra)], rest[len(extra) + 1:]
        p = lax.dot_general(a_ref[...].astype(BF16), b_ref[...].astype(BF16), (dims, ((), ())),
                            preferred_element_type=F32)
        if nk == 1:
            o_ref[...] = p.astype(o_ref.dtype)
            return
        acc_ref = scratch[0] if use_scratch else o_ref
        k = pl.program_id(2)

        @pl.when(k == 0)
        def _():
            acc_ref[...] = p

        @pl.when(k > 0)
        def _():
            acc_ref[...] += p

        if use_scratch:
            @pl.when(k == nk - 1)
            def _():
                o_ref[...] = acc_ref[...].astype(o_ref.dtype)

    return pl.pallas_call(
        body, name=name,
        grid=(M // tm, N // tn, nk),
        in_specs=[a_spec, b_spec] + [pl.BlockSpec(t.shape, lambda i, j, k: (0, 0)) for t in extra],
        out_specs=pl.BlockSpec((tm, tn), lambda i, j, k: (i, j)),
        out_shape=jax.ShapeDtypeStruct((M, N), out_dtype),
        scratch_shapes=[pltpu.VMEM((tm, tn), F32)] if use_scratch else [],
        compiler_params=_params(("parallel", "parallel", "arbitrary")),
    )(a, b, *extra)


def _row_spec(width, col=0):
    return pl.BlockSpec((ROW, width), lambda i, col=col: (i, col))


def _mods_spec(D):
    return pl.BlockSpec((1, 16, D), lambda i: (_stream(i), 0, 0))


def _acc_spec(D):
    return pl.BlockSpec((1, ACC_ROWS, D), lambda i: (_stream(i), 0, 0))


def _vec_spec(rows, D):
    return pl.BlockSpec((rows, D), lambda i: (0, 0))


def _acc_init(acc_ref):
    i = pl.program_id(0)

    @pl.when(i <= 1)
    def _():
        acc_ref[...] = jnp.zeros_like(acc_ref)


def _acc_add(acc_ref, row, val):
    acc_ref[0, row:row + 1, :] += jnp.sum(val, axis=0, keepdims=True)


def _rows_operand(x):
    if not isinstance(x, tuple):
        return [_row_spec(x.shape[1])], [x], x.shape
    ctx, lat = x
    D = lat.shape[1]
    assert ctx.shape == (ROW, D)
    specs = [pl.BlockSpec((ROW, D), lambda i: (0, 0)), pl.BlockSpec((ROW, D), lambda i: (jnp.maximum(i - 1, 0), 0))]
    return specs, [ctx, lat], (ROW + lat.shape[0], D)


def _rows_tile(refs):
    if len(refs) == 1:
        return refs[0][...]
    return jnp.where(pl.program_id(0) == 0, refs[0][...], refs[1][...])


def _norm_tile_fwd(x, m, g, shift_idx, scale_idx):
    inv = lax.rsqrt(jnp.mean(x * x, axis=-1, keepdims=True) + EPS)
    y = (x * inv) * g
    return (y * (1.0 + m[scale_idx:scale_idx + 1, :]) + m[shift_idx:shift_idx + 1, :]).astype(BF16)


def _norm_tile_bwd(x, dh, dres, m, g, shift_idx, scale_idx, acc_ref):
    inv = lax.rsqrt(jnp.mean(x * x, axis=-1, keepdims=True) + EPS)
    xn = x * inv
    dy = dh * (1.0 + m[scale_idx:scale_idx + 1, :])
    dxn = dy * g
    _acc_add(acc_ref, 0, dh)
    _acc_add(acc_ref, 1, dh * (xn * g))
    _acc_add(acc_ref, 2, dy * xn)
    return inv * (dxn - xn * jnp.mean(dxn * xn, axis=-1, keepdims=True)) + dres


def _gate_tile_bwd(dx, branch, m, gate, acc_ref):
    gate_idx, fac = gate
    _acc_add(acc_ref, 3, fac * dx * branch)
    return ((fac * m[gate_idx:gate_idx + 1, :]) * dx).astype(BF16)


_NT = (((1,), (1,)), ((), ()))


def _ffn_chunk(F):
    return _pick(F, (1408, 512, 256, 128))


def _resident():
    return pl.BlockSpec(memory_space=pltpu.VMEM)


def _ffn_tile_fwd(hv, wi_ref, wo_ref, u_ref, s_ref, F, cw):
    acc = jnp.zeros((hv.shape[0], wo_ref.shape[1]), F32)
    for j in range(F // cw):
        a = lax.dot_general(hv, wi_ref[j * cw:(j + 1) * cw, :], _NT, preferred_element_type=F32)
        b = lax.dot_general(hv, wi_ref[F + j * cw:F + (j + 1) * cw, :], _NT, preferred_element_type=F32)
        s = ((a * jax.nn.sigmoid(a)) * b).astype(BF16)
        u_ref[:, j * cw:(j + 1) * cw] = a.astype(BF16)
        u_ref[:, F + j * cw:F + (j + 1) * cw] = b.astype(BF16)
        s_ref[:, j * cw:(j + 1) * cw] = s
        acc = acc + jnp.dot(s, wo_ref[j * cw:(j + 1) * cw, :], preferred_element_type=F32)
    return acc


def _norm_ffn_fwd(xprev, branch, mods, g, gate, shift_idx, scale_idx, w_in_t, w_out, name, head=None):
    x_specs, x_args, (T, D) = _rows_operand(xprev)
    F = w_out.shape[0]
    cw = _ffn_chunk(F)
    has_res = branch is not None
    n_in = len(x_args) + int(has_res) + 4 + (2 if head else 0)

    def body(*refs):
        ins, outs = list(refs[:n_in]), list(refs[n_in:])
        x = _rows_tile([ins.pop(0) for _ in x_args])
        f_ref = ins.pop(0) if has_res else None
        m_ref, g_ref, wi_ref, wo_ref = ins[:4]
        xo_ref = outs.pop(0) if has_res else None
        h_ref, u_ref, s_ref = outs[:3]
        m = m_ref[0]
        if has_res:
            gate_idx, fac = gate
            x = x + (fac * m[gate_idx:gate_idx + 1, :]) * f_ref[...]
            xo_ref[...] = x
        hv = _norm_tile_fwd(x, m, g_ref[...], shift_idx, scale_idx)
        h_ref[...] = hv
        f = _ffn_tile_fwd(hv, wi_ref, wo_ref, u_ref, s_ref, F, cw)
        if head is None:
            outs[3][...] = f
            return
        fg_ref, t_ref = ins[4:6]
        dx_ref, df_ref, acc_ref = outs[3:6]
        _acc_init(acc_ref)
        lat = (pl.program_id(0) > 0).astype(F32)
        gate8 = 0.5 * m[8:9, :]
        x3 = x + gate8 * f
        inv3 = lax.rsqrt(jnp.mean(x3 * x3, axis=-1, keepdims=True) + EPS)
        xn = x3 * inv3
        fg = fg_ref[...]
        e = (xn * fg - t_ref[...]) * lat
        dy = e * (1.0 / D)
        dxn = dy * fg
        dx = inv3 * (dxn - xn * jnp.mean(dxn * xn, axis=-1, keepdims=True))
        dx_ref[...] = dx
        df_ref[...] = (gate8 * dx).astype(BF16)
        _acc_add(acc_ref, 0, (0.5 / D) * e * e)
        _acc_add(acc_ref, 1, dy * xn)
        _acc_add(acc_ref, 2, 0.5 * dx * f)

    in_specs = x_specs + ([_row_spec(D)] if has_res else []) + \
               [_mods_spec(D), _vec_spec(1, D), _resident(), _resident()]
    args = x_args + ([branch] if has_res else []) + [mods, g, w_in_t, w_out]
    out_specs = ([_row_spec(D)] if has_res else []) + [_row_spec(D), _row_spec(2 * F), _row_spec(F)]
    out_shape = ([jax.ShapeDtypeStruct((T, D), F32)] if has_res else []) + \
                [jax.ShapeDtypeStruct((T, D), BF16), jax.ShapeDtypeStruct((T, 2 * F), BF16),
                 jax.ShapeDtypeStruct((T, F), BF16)]
    if head is None:
        out_specs += [_row_spec(D)]
        out_shape += [jax.ShapeDtypeStruct((T, D), F32)]
    else:
        in_specs += [_vec_spec(1, D), pl.BlockSpec((ROW, D), lambda i: (jnp.maximum(i - 1, 0), 0))]
        args += list(head)
        out_specs += [_row_spec(D), _row_spec(D), _acc_spec(D)]
        out_shape += [jax.ShapeDtypeStruct((T, D), F32), jax.ShapeDtypeStruct((T, D), BF16),
                      jax.ShapeDtypeStruct((2, ACC_ROWS, D), F32)]
    out = pl.pallas_call(
        body, name=name, grid=(T // ROW,), in_specs=in_specs, out_specs=out_specs, out_shape=out_shape,
        compiler_params=_params(("arbitrary",) if head else ("parallel",)),
    )(*args)
    return tuple(out) if has_res else (None,) + tuple(out)


def _ffn_norm_bwd(df, u, w_in_t, w_out, x, dres, mods, g, shift_idx, scale_idx, gate, branch, name,
                  skip_first_tile=False):
    T, D = df.shape
    F = w_out.shape[0]
    cw = _ffn_chunk(F)
    nt = T // ROW
    has_gate = gate is not None
    x_specs, x_args, _ = _rows_operand(x)
    n_in = 7 + len(x_args) + int(has_gate)

    def body(*refs):
        ins, outs = list(refs[:n_in]), list(refs[n_in:])
        df_ref, u_ref, wi_ref, wo_ref = ins[:4]
        x_refs = ins[4:4 + len(x_args)]
        dr_ref = ins[4 + len(x_args)]
        b_ref = ins[5 + len(x_args)] if has_gate else None
        m_ref, g_ref = ins[-2:]
        du_ref, dx_ref = outs[:2]
        db_ref = outs[2] if has_gate else None
        acc_ref = outs[-1]
        _acc_init(acc_ref)
        dfv = df_ref[...]
        dh = jnp.zeros((ROW, D), F32)
        for j in range(F // cw):
            ds = lax.dot_general(dfv, wo_ref[j * cw:(j + 1) * cw, :], _NT, preferred_element_type=F32)
            a = u_ref[:, j * cw:(j + 1) * cw].astype(F32)
            b = u_ref[:, F + j * cw:F + (j + 1) * cw].astype(F32)
            sig = jax.nn.sigmoid(a)
            da = (ds * b * (sig * (1.0 + a * (1.0 - sig)))).astype(BF16)
            db = (ds * (a * sig)).astype(BF16)
            du_ref[:, j * cw:(j + 1) * cw] = da
            du_ref[:, F + j * cw:F + (j + 1) * cw] = db
            dh = dh + jnp.dot(da, wi_ref[j * cw:(j + 1) * cw, :], preferred_element_type=F32)
            dh = dh + jnp.dot(db, wi_ref[F + j * cw:F + (j + 1) * cw, :], preferred_element_type=F32)
        m = m_ref[0]
        dx = _norm_tile_bwd(_rows_tile(x_refs), dh, dr_ref[...], m, g_ref[...], shift_idx, scale_idx, acc_ref)
        dx_ref[...] = dx
        if has_gate:
            db_ref[...] = _gate_tile_bwd(dx, b_ref[...], m, gate, acc_ref)

    in_specs = [_row_spec(D), _row_spec(2 * F), _resident(), _resident()] + x_specs + [_row_spec(D)] + \
               ([_row_spec(D)] if has_gate else []) + [_mods_spec(D), _vec_spec(1, D)]
    args = [df, u, w_in_t, w_out] + x_args + [dres] + ([branch] if has_gate else []) + [mods, g]
    if skip_first_tile:
        dx_spec = pl.BlockSpec((ROW, D), lambda i: (jnp.maximum(i - 1, 0), 0))
        dx_shape = jax.ShapeDtypeStruct((T - ROW, D), F32)
    else:
        dx_spec = _row_spec(D)
        dx_shape = jax.ShapeDtypeStruct((T, D), F32)
    out_specs = [_row_spec(2 * F), dx_spec] + ([_row_spec(D)] if has_gate else []) + [_acc_spec(D)]
    out_shape = [jax.ShapeDtypeStruct((T, 2 * F), BF16), dx_shape] + \
                ([jax.ShapeDtypeStruct((T, D), BF16)] if has_gate else []) + \
                [jax.ShapeDtypeStruct((2, ACC_ROWS, D), F32)]
    out = pl.pallas_call(
        body, name=name, grid=(nt,), in_specs=in_specs, out_specs=out_specs, out_shape=out_shape,
        compiler_params=_params(("arbitrary",)),
    )(*args)
    if has_gate:
        return tuple(out)
    return out[0], out[1], None, out[2]


def _halo_specs(width, col, nt):
    per = ROW // HALO
    prev = pl.BlockSpec((HALO, width), lambda i, col=col: (jnp.maximum(i * per - 1, 0), col))
    nxt = pl.BlockSpec((HALO, width), lambda i, col=col: (jnp.minimum((i + 1) * per, nt * per - 1), col))
    return prev, nxt


def _f32(ref):
    return ref[...].astype(F32)


def _last_row(halo_ref):
    return halo_ref[HALO - 1:HALO, :].astype(F32)


def _first_row(halo_ref):
    return halo_ref[0:1, :].astype(F32)


def _shift_rows(v, prev_row, next_row):
    rows = lax.broadcasted_iota(jnp.int32, v.shape, 0)
    down = jnp.where(rows == 0, prev_row, pltpu.roll(v, 1, 0))
    up = jnp.where(rows == v.shape[0] - 1, next_row, pltpu.roll(v, v.shape[0] - 1, 0))
    return down, up


def _conv_fwd(P, conv_w, D, name):
    T = P.shape[0]
    nt = T // ROW
    cg_p, cg_n = _halo_specs(D, 1, nt)
    vc_p, vc_n = _halo_specs(D, 2, nt)

    def body(bg_ref, cg_ref, vc_ref, cgp_ref, vcp_ref, cgn_ref, vcn_ref, w_ref, y_ref):
        i = pl.program_id(0)
        has_prev = (i != 1).astype(F32)
        has_next = (i != nt - 1).astype(F32)
        u = _f32(cg_ref) * _f32(vc_ref)
        up_row = _last_row(cgp_ref) * _last_row(vcp_ref) * has_prev
        un_row = _first_row(cgn_ref) * _first_row(vcn_ref) * has_next
        um1, up1 = _shift_rows(u, up_row, un_row)
        w = w_ref[...]
        conv = um1 * w[0:1, :] + u * w[1:2, :] + up1 * w[2:3, :]
        y_ref[...] = (_f32(bg_ref) * conv).astype(BF16)

    return pl.pallas_call(
        body, name=name, grid=(nt,),
        in_specs=[_row_spec(D, 0), _row_spec(D, 1), _row_spec(D, 2), cg_p, vc_p, cg_n, vc_n, _vec_spec(3, D)],
        out_specs=_row_spec(D),
        out_shape=jax.ShapeDtypeStruct((T, D), BF16),
        compiler_params=_params(("parallel",)),
    )(P, P, P, P, P, P, P, conv_w)


def _conv_bwd(P, dy, conv_w, D, name):
    T = P.shape[0]
    nt = T // ROW
    bg_p, bg_n = _halo_specs(D, 0, nt)
    cg_p, cg_n = _halo_specs(D, 1, nt)
    vc_p, vc_n = _halo_specs(D, 2, nt)
    dy_p, dy_n = _halo_specs(D, 0, nt)

    def body(bg_ref, cg_ref, vc_ref, dy_ref, bgp_ref, cgp_ref, vcp_ref, dyp_ref,
             bgn_ref, cgn_ref, vcn_ref, dyn_ref, w_ref, o_ref, acc_ref):
        _acc_init(acc_ref)
        i = pl.program_id(0)
        lat = (i > 0).astype(F32)
        has_prev = (i != 1).astype(F32)
        has_next = (i != nt - 1).astype(F32)
        bg = _f32(bg_ref)
        cg = _f32(cg_ref)
        vc = _f32(vc_ref)
        dyv = dy_ref[...] * lat
        u = cg * vc
        up_row = _last_row(cgp_ref) * _last_row(vcp_ref) * has_prev
        un_row = _first_row(cgn_ref) * _first_row(vcn_ref) * has_next
        um1, up1 = _shift_rows(u, up_row, un_row)
        w = w_ref[...]
        conv = um1 * w[0:1, :] + u * w[1:2, :] + up1 * w[2:3, :]
        dc = dyv * bg
        dcp_row = _last_row(dyp_ref) * _last_row(bgp_ref) * has_prev
        dcn_row = _first_row(dyn_ref) * _first_row(bgn_ref) * has_next
        dcm1, dcp1 = _shift_rows(dc, dcp_row, dcn_row)
        du = dcp1 * w[0:1, :] + dc * w[1:2, :] + dcm1 * w[2:3, :]
        o_ref[:, 0:D] = (dyv * conv).astype(BF16)
        o_ref[:, D:2 * D] = (du * vc * lat).astype(BF16)
        o_ref[:, 2 * D:3 * D] = (du * cg * lat).astype(BF16)
        _acc_add(acc_ref, 0, dc * um1)
        _acc_add(acc_ref, 1, dc * u)
        _acc_add(acc_ref, 2, dc * up1)

    return pl.pallas_call(
        body, name=name, grid=(nt,),
        in_specs=[_row_spec(D, 0), _row_spec(D, 1), _row_spec(D, 2), _row_spec(D, 0),
                  bg_p, cg_p, vc_p, dy_p, bg_n, cg_n, vc_n, dy_n, _vec_spec(3, D)],
        out_specs=[_row_spec(3 * D), _acc_spec(D)],
        out_shape=[jax.ShapeDtypeStruct((T, 3 * D), BF16), jax.ShapeDtypeStruct((2, ACC_ROWS, D), F32)],
        compiler_params=_params(("arbitrary",)),
    )(P, P, P, dy, P, P, P, dy, P, P, P, dy, conv_w)


def _rope_tables(ctx_len, seq):
    n_freq = HEAD_DIM // 4
    rows = seq // GRID_W
    inv = ROPE_THETA ** (-jnp.arange(n_freq, dtype=F32) / n_freq)
    ar = jnp.arange(rows, dtype=F32)[:, None] * inv
    ac = jnp.arange(GRID_W, dtype=F32)[:, None] * inv

    def per_row(a):
        return jnp.repeat(a, GRID_W, axis=0)

    def per_col(a):
        return jnp.tile(a, (rows, 1))

    cos_t = jnp.concatenate([per_row(jnp.cos(ar)), per_row(jnp.cos(ar)), per_col(jnp.cos(ac)), per_col(jnp.cos(ac))], axis=1)
    sin_t = jnp.concatenate([per_row(-jnp.sin(ar)), per_row(jnp.sin(ar)), per_col(-jnp.sin(ac)), per_col(jnp.sin(ac))], axis=1)
    cos_t = jnp.concatenate([jnp.ones((ctx_len, HEAD_DIM), F32), cos_t], axis=0)
    sin_t = jnp.concatenate([jnp.zeros((ctx_len, HEAD_DIM), F32), sin_t], axis=0)
    return cos_t, sin_t


def _swap_halves(y):
    lanes = lax.broadcasted_iota(jnp.int32, y.shape, 1)
    first = (lanes % 64) < 32
    return jnp.where(first, pltpu.roll(y, HEAD_DIM - 32, 1), pltpu.roll(y, 32, 1))


def _to_row(col, n):
    return jnp.transpose(jnp.broadcast_to(col, (n, HEAD_DIM)))[0:1, :]


LOG2E = 1.4426950408889634
ATTN_PART_LANES = 256
ATTN_QUERY_ROWS = 768


def _flash_fwd(q, k, v, name, tq=None, tk=None):
    T = q.shape[0]
    tq = tq or _pick(T, (ATTN_QUERY_ROWS, ROW))
    parts = GROUP * tq // ATTN_PART_LANES
    tk = tk or _pick(T, (1408, 768, 512, 256))
    ck = tk
    nk = T // tk
    GW = GROUP * HEAD_DIM

    def body(q_ref, k_ref, v_ref, o_ref, lse_ref, qs_ref, m_ref, l_ref, acc_ref, st_ref):
        ki = pl.program_id(2)

        @pl.when(ki == 0)
        def _():
            for g in range(GROUP):
                qs_ref[g * tq:(g + 1) * tq, :] = q_ref[:, g * HEAD_DIM:(g + 1) * HEAD_DIM]
            m_ref[...] = jnp.full(m_ref.shape, -jnp.inf, F32)
            l_ref[...] = jnp.zeros(l_ref.shape, F32)
            acc_ref[...] = jnp.zeros(acc_ref.shape, F32)

        w = ATTN_PART_LANES
        nck = tk // ck

        def lanes(p):
            return slice(p * w, (p + 1) * w)

        def keys(c):
            return slice(c * ck, (c + 1) * ck)

        def fold(a):
            return a.reshape(ck // 8, 8, w)

        def scores(p, c):
            st = lax.dot_general(k_ref[keys(c), :], qs_ref[lanes(p), :], _NT,
                                 preferred_element_type=F32) * (ATTN_SCALE * LOG2E)
            st_ref[keys(c), lanes(p)] = st
            return jnp.max(fold(st), axis=0)

        def new_max(p, partial):
            m_prev = m_ref[:, lanes(p)]
            m_new = jnp.maximum(m_prev, jnp.max(functools.reduce(jnp.maximum, partial), axis=0, keepdims=True))
            m_ref[:, lanes(p)] = m_new
            return m_new, jnp.exp2(m_prev - m_new)

        def weights(p, c, m_new):
            pt = jnp.exp2(st_ref[keys(c), lanes(p)] - m_new)
            pv = lax.dot_general(v_ref[keys(c), :], pt.astype(BF16), (((0,), (0,)), ((), ())),
                                 preferred_element_type=F32)
            return jnp.sum(fold(pt), axis=0), pv

        partial = [scores(0, c) for c in range(nck)]
        for p in range(parts):
            m_new, alpha = new_max(p, partial)
            partial, sums, pvs = [], [], []
            for c in range(nck):
                if p + 1 < parts:
                    partial.append(scores(p + 1, c))
                s8, pv = weights(p, c, m_new)
                sums.append(s8)
                pvs.append(pv)
            l_ref[:, lanes(p)] = alpha * l_ref[:, lanes(p)] + jnp.sum(sum(sums), axis=0, keepdims=True)
            acc_ref[:, lanes(p)] = alpha * acc_ref[:, lanes(p)] + sum(pvs)

        @pl.when(ki == nk - 1)
        def _():
            out = jnp.transpose(acc_ref[...] / l_ref[...])
            lse = m_ref[...] + jnp.log2(l_ref[...])
            for g in range(GROUP):
                o_ref[:, g * HEAD_DIM:(g + 1) * HEAD_DIM] = out[g * tq:(g + 1) * tq, :]
                lse_ref[0, g:g + 1, :] = lse[:, g * tq:(g + 1) * tq]

    return pl.pallas_call(
        body, name=name, grid=(N_KV_HEADS, T // tq, nk),
        in_specs=[pl.BlockSpec((tq, GW), lambda h, i, j: (i, h)),
                  pl.BlockSpec((tk, HEAD_DIM), lambda h, i, j: (j, h)),
                  pl.BlockSpec((tk, HEAD_DIM), lambda h, i, j: (j, h))],
        out_specs=[pl.BlockSpec((tq, GW), lambda h, i, j: (i, h)),
                   pl.BlockSpec((1, GROUP, tq), lambda h, i, j: (h, 0, i))],
        out_shape=[jax.ShapeDtypeStruct((T, N_Q_HEADS * HEAD_DIM), F32),
                   jax.ShapeDtypeStruct((N_KV_HEADS, GROUP, T), F32)],
        scratch_shapes=[pltpu.VMEM((GROUP * tq, HEAD_DIM), BF16), pltpu.VMEM((1, GROUP * tq), F32),
                        pltpu.VMEM((1, GROUP * tq), F32), pltpu.VMEM((HEAD_DIM, GROUP * tq), F32),
                        pltpu.VMEM((tk, GROUP * tq), F32)],
        compiler_params=_params(("parallel", "parallel", "arbitrary")),
    )(q, k, v)


def _flash_bwd(q, k, v, do, lse, delta, name, tq=None, tk=None, token=None):
    T = q.shape[0]
    tq = tq or _pick(T, (ATTN_QUERY_ROWS, ROW))
    tk = tk or _pick(T, (768, 512, 256))
    nk = T // tk
    GW = GROUP * HEAD_DIM
    nt = (((1,), (1,)), ((), ()))
    extra = [] if token is None else [token]

    def body(q_ref, do_ref, k_ref, v_ref, lse_ref, dl_ref, *rest):
        dq_ref, dk_ref, dv_ref, qs_ref, dos_ref, dqt_ref = rest[len(extra):]
        qi = pl.program_id(1)
        ki = pl.program_id(2)

        @pl.when(ki == 0)
        def _():
            for g in range(GROUP):
                qs_ref[g * tq:(g + 1) * tq, :] = q_ref[:, g * HEAD_DIM:(g + 1) * HEAD_DIM]
                dos_ref[g * tq:(g + 1) * tq, :] = do_ref[:, g * HEAD_DIM:(g + 1) * HEAD_DIM]
            dqt_ref[...] = jnp.zeros(dqt_ref.shape, F32)

        kk = k_ref[...]
        vv = v_ref[...]

        def lanes(p):
            return slice(p * tq, (p + 1) * tq)

        def products(p):
            st = lax.dot_general(kk, qs_ref[lanes(p), :], nt, preferred_element_type=F32)
            dpt = lax.dot_general(vv, dos_ref[lanes(p), :], nt, preferred_element_type=F32)
            return st, dpt

        dk_c = jnp.zeros((tk, HEAD_DIM), F32)
        dv_c = jnp.zeros((tk, HEAD_DIM), F32)
        ahead = products(0)
        for p in range(GROUP):
            st, dpt = ahead
            if p + 1 < GROUP:
                ahead = products(p + 1)
            pt = jnp.exp2(st * (ATTN_SCALE * LOG2E) - lse_ref[0, p:p + 1, :])
            dst = ((pt * (dpt - dl_ref[0, p:p + 1, :])) * ATTN_SCALE).astype(BF16)
            dv_c = dv_c + jnp.dot(pt.astype(BF16), dos_ref[lanes(p), :], preferred_element_type=F32)
            dk_c = dk_c + jnp.dot(dst, qs_ref[lanes(p), :], preferred_element_type=F32)
            dqt_ref[:, lanes(p)] += lax.dot_general(kk, dst, (((0,), (0,)), ((), ())), preferred_element_type=F32)
        rows = pl.ds(pl.multiple_of(ki * tk, tk), tk)

        @pl.when(qi == 0)
        def _():
            dk_ref[rows, :] = dk_c
            dv_ref[rows, :] = dv_c

        @pl.when(qi > 0)
        def _():
            dk_ref[rows, :] += dk_c
            dv_ref[rows, :] += dv_c

        @pl.when(ki == nk - 1)
        def _():
            dqv = jnp.transpose(dqt_ref[...])
            for g in range(GROUP):
                dq_ref[:, g * HEAD_DIM:(g + 1) * HEAD_DIM] = dqv[g * tq:(g + 1) * tq, :]

    return pl.pallas_call(
        body, name=name, grid=(N_KV_HEADS, T // tq, nk),
        in_specs=[pl.BlockSpec((tq, GW), lambda h, i, j: (i, h)),
                  pl.BlockSpec((tq, GW), lambda h, i, j: (i, h)),
                  pl.BlockSpec((tk, HEAD_DIM), lambda h, i, j: (j, h)),
                  pl.BlockSpec((tk, HEAD_DIM), lambda h, i, j: (j, h)),
                  pl.BlockSpec((1, GROUP, tq), lambda h, i, j: (h, 0, i)),
                  pl.BlockSpec((1, GROUP, tq), lambda h, i, j: (h, 0, i))] +
                 [pl.BlockSpec(t.shape, lambda h, i, j: (0, 0)) for t in extra],
        out_specs=[pl.BlockSpec((tq, GW), lambda h, i, j: (i, h)),
                   pl.BlockSpec((T, HEAD_DIM), lambda h, i, j: (0, h)),
                   pl.BlockSpec((T, HEAD_DIM), lambda h, i, j: (0, h))],
        out_shape=[jax.ShapeDtypeStruct((T, N_Q_HEADS * HEAD_DIM), F32),
                   jax.ShapeDtypeStruct((T, N_KV_HEADS * HEAD_DIM), F32),
                   jax.ShapeDtypeStruct((T, N_KV_HEADS * HEAD_DIM), F32)],
        scratch_shapes=[pltpu.VMEM((GROUP * tq, HEAD_DIM), BF16), pltpu.VMEM((GROUP * tq, HEAD_DIM), BF16),
                        pltpu.VMEM((HEAD_DIM, GROUP * tq), F32)],
        compiler_params=_params(("arbitrary", "arbitrary", "arbitrary")),
    )(q, do, k, v, lse, delta, *extra)


def _gate_specs(D):
    w = D // 2
    first = (3 * D + (N_Q_HEADS + 2 * N_KV_HEADS) * HEAD_DIM) // w
    return [pl.BlockSpec((ROW, w), lambda i, c=first + j: (i, c)) for j in range(4)]


def _merge_fwd(yc, o, P, wbc, wba, wo, D, name):
    T = yc.shape[0]
    w = D // 2

    def body(yc_ref, o_ref, g0, g1, g2, g3, wbc_ref, wba_ref, wo_ref, a1_ref, a2_ref, z_ref, mo_ref):
        a1 = jnp.dot(yc_ref[...], wbc_ref[...], preferred_element_type=F32)
        a2 = jnp.dot(o_ref[...].astype(BF16), wba_ref[...], preferred_element_type=F32)
        a1_ref[...] = a1
        a2_ref[...] = a2
        for j, (gc, ga) in enumerate(((g0, g2), (g1, g3))):
            sl = slice(j * w, (j + 1) * w)
            z = jax.nn.sigmoid(_f32(gc)) * a1[:, sl] + jax.nn.sigmoid(_f32(ga)) * a2[:, sl]
            z_ref[:, sl] = z.astype(BF16)
        mo_ref[...] = jnp.dot(z_ref[...], wo_ref[...], preferred_element_type=F32)

    return pl.pallas_call(
        body, name=name, grid=(T // ROW,),
        in_specs=[_row_spec(D), _row_spec(D)] + _gate_specs(D) + [_resident()] * 3,
        out_specs=[_row_spec(D)] * 4,
        out_shape=[jax.ShapeDtypeStruct((T, D), F32), jax.ShapeDtypeStruct((T, D), F32),
                   jax.ShapeDtypeStruct((T, D), BF16), jax.ShapeDtypeStruct((T, D), F32)],
        compiler_params=_params(("parallel",)),
    )(yc, o, P, P, P, P, wbc, wba, wo)


def _merge_bwd(dmo, a1, a2, o, P, wbc, wba, wo, D, name):
    T = a1.shape[0]
    w = D // 2

    def body(dmo_ref, a1_ref, a2_ref, o_ref, g0, g1, g2, g3, wbc_ref, wba_ref, wo_ref,
             d1_ref, d2_ref, dg_ref, dyc_ref, dob_ref, dl_ref):
        dz = lax.dot_general(dmo_ref[...], wo_ref[...], _NT, preferred_element_type=F32)
        for j, (gc, ga) in enumerate(((g0, g2), (g1, g3))):
            sl = slice(j * w, (j + 1) * w)
            dzs = dz[:, sl]
            sc = jax.nn.sigmoid(_f32(gc))
            sa = jax.nn.sigmoid(_f32(ga))
            d1_ref[:, sl] = (dzs * sc).astype(BF16)
            d2_ref[:, sl] = (dzs * sa).astype(BF16)
            dg_ref[:, j * w:(j + 1) * w] = (dzs * a1_ref[:, sl] * (sc * (1.0 - sc))).astype(BF16)
            dg_ref[:, D + j * w:D + (j + 1) * w] = (dzs * a2_ref[:, sl] * (sa * (1.0 - sa))).astype(BF16)
        dyc_ref[...] = lax.dot_general(d1_ref[...], wbc_ref[...], _NT, preferred_element_type=F32)
        dov = lax.dot_general(d2_ref[...], wba_ref[...], _NT, preferred_element_type=F32)
        dob_ref[...] = dov.astype(BF16)
        prod = dov * o_ref[...]
        for h in range(N_Q_HEADS):
            d = jnp.sum(prod[:, h * HEAD_DIM:(h + 1) * HEAD_DIM], axis=1, keepdims=True)
            dl_ref[h // GROUP, (h % GROUP):(h % GROUP) + 1, :] = _to_row(d, ROW)

    return pl.pallas_call(
        body, name=name, grid=(T // ROW,),
        in_specs=[_row_spec(D)] * 4 + _gate_specs(D) + [_resident()] * 3,
        out_specs=[_row_spec(D), _row_spec(D), _row_spec(2 * D), _row_spec(D), _row_spec(D),
                   pl.BlockSpec((N_KV_HEADS, GROUP, ROW), lambda i: (0, 0, i))],
        out_shape=[jax.ShapeDtypeStruct((T, D), BF16), jax.ShapeDtypeStruct((T, D), BF16),
                   jax.ShapeDtypeStruct((T, 2 * D), BF16), jax.ShapeDtypeStruct((T, D), F32),
                   jax.ShapeDtypeStruct((T, D), BF16), jax.ShapeDtypeStruct((N_KV_HEADS, GROUP, T), F32)],
        compiler_params=_params(("parallel",)),
    )(dmo, a1, a2, o, P, P, P, P, wbc, wba, wo)


def _adamw_math(w, g, m, v):
    m = ADAM_B1 * m + (1.0 - ADAM_B1) * g
    v = ADAM_B2 * v + (1.0 - ADAM_B2) * (g * g)
    m_hat = m / (1.0 - ADAM_B1 ** ADAM_STEP)
    v_hat = v / (1.0 - ADAM_B2 ** ADAM_STEP)
    delta = -ADAM_LR * (m_hat / (jnp.sqrt(v_hat) + ADAM_EPS) + ADAM_WD * w)
    return delta, m, v


def _adamw(w, g, m, v, name):
    R, C = w.shape
    tr = _pick(R, tuple(t for t in (256, 128, 64, 32, 16, 8) if t * C * 4 <= ADAMW_BLOCK_BYTES))

    def body(w_ref, g_ref, m_ref, v_ref, d_ref, mo_ref, vo_ref):
        d, mn, vn = _adamw_math(w_ref[...], g_ref[...], m_ref[...], v_ref[...])
        d_ref[...] = d
        mo_ref[...] = mn
        vo_ref[...] = vn

    spec = pl.BlockSpec((tr, C), lambda i: (i, 0))
    return pl.pallas_call(
        body, name=name, grid=(R // tr,),
        in_specs=[spec] * 4, out_specs=[spec] * 3,
        out_shape=[jax.ShapeDtypeStruct((R, C), F32)] * 3,
        compiler_params=_params(("parallel",)),
    )(w, g, m, v)


def _norm_mix_in_fwd(xprev, branch, mods, g, gate, shift_idx, scale_idx, w_t, gq, gk, cos_t, sin_t, name):
    x_specs, x_args, (T, D) = _rows_operand(xprev)
    N = w_t.shape[0]
    QW = N_Q_HEADS * HEAD_DIM
    KW = N_KV_HEADS * HEAD_DIM
    q0, k0, v0 = 3 * D, 3 * D + QW, 3 * D + QW + KW
    edges = [0, D, 2 * D, q0, k0, v0 + KW] + list(range(v0 + KW + D, N + 1, D))
    assert edges[-1] == N

    def body(*refs):
        f_ref, m_ref, g_ref, w_ref, gq_ref, gk_ref, c_ref, s_ref = refs[len(x_args):len(x_args) + 8]
        xo_ref, h_ref, p_ref, qo_ref, ko_ref, vo_ref = refs[len(x_args) + 8:]
        m = m_ref[0]
        gate_idx, fac = gate
        x = _rows_tile(refs[:len(x_args)]) + (fac * m[gate_idx:gate_idx + 1, :]) * f_ref[...]
        xo_ref[...] = x
        hv = _norm_tile_fwd(x, m, g_ref[...], shift_idx, scale_idx)
        h_ref[...] = hv
        c = c_ref[...]
        s = s_ref[...]

        def head(xh, gain):
            inv = lax.rsqrt(jnp.mean(xh * xh, axis=-1, keepdims=True) + EPS)
            y = (xh * inv) * gain
            return y * c + _swap_halves(y) * s

        for lo, hi in zip(edges[:-1], edges[1:]):
            pb = lax.dot_general(hv, w_ref[lo:hi, :], _NT, preferred_element_type=F32).astype(BF16)
            p_ref[:, lo:hi] = pb
            if lo == q0:
                for h in range(N_Q_HEADS):
                    sl = slice(h * HEAD_DIM, (h + 1) * HEAD_DIM)
                    qo_ref[:, sl] = head(pb[:, sl].astype(F32), gq_ref[...]).astype(BF16)
            elif lo == k0:
                for h in range(N_KV_HEADS):
                    sl = slice(h * HEAD_DIM, (h + 1) * HEAD_DIM)
                    ko_ref[:, sl] = head(pb[:, sl].astype(F32), gk_ref[...]).astype(BF16)
                vo_ref[...] = pb[:, KW:2 * KW]

    return pl.pallas_call(
        body, name=name, grid=(T // ROW,),
        in_specs=x_specs + [_row_spec(D), _mods_spec(D), _vec_spec(1, D), _resident(),
                            _vec_spec(1, HEAD_DIM), _vec_spec(1, HEAD_DIM), _row_spec(HEAD_DIM), _row_spec(HEAD_DIM)],
        out_specs=[_row_spec(D), _row_spec(D), _row_spec(N), _row_spec(QW), _row_spec(KW), _row_spec(KW)],
        out_shape=[jax.ShapeDtypeStruct((T, D), F32), jax.ShapeDtypeStruct((T, D), BF16),
                   jax.ShapeDtypeStruct((T, N), BF16), jax.ShapeDtypeStruct((T, QW), BF16),
                   jax.ShapeDtypeStruct((T, KW), BF16), jax.ShapeDtypeStruct((T, KW), BF16)],
        compiler_params=_params(("parallel",)),
    )(*x_args, branch, mods, g, w_t, gq, gk, cos_t, sin_t)


def _mix_in_norm_bwd(dconv, dgt, P, dq, dk, dv, gq, gk, cos_t, sin_t, w_t, x, dres, mods, g, shift_idx, scale_idx,
                     gate, branch, name):
    T, D = x.shape
    QW = N_Q_HEADS * HEAD_DIM
    KW = N_KV_HEADS * HEAD_DIM
    q0, g0 = 3 * D, 3 * D + QW + 2 * KW
    assert dconv.shape[1] == q0 and g0 + dgt.shape[1] == w_t.shape[0]

    def body(dc_ref, dg_ref, q_ref, k_ref, dq_ref, dk_ref, dv_ref, gq_ref, gk_ref, c_ref, s_ref,
             w_ref, x_ref, dr_ref, b_ref, m_ref, g_ref, dx_ref, db_ref, acc_ref, o_ref, qacc_ref):
        _acc_init(acc_ref)
        _acc_init(qacc_ref)
        dh = jnp.dot(dc_ref[...], w_ref[0:q0, :], preferred_element_type=F32)
        c = c_ref[...]
        s = s_ref[...]

        def head(xh, d, gain):
            dyv = d * c + _swap_halves(d * s)
            inv = lax.rsqrt(jnp.mean(xh * xh, axis=-1, keepdims=True) + EPS)
            xn = xh * inv
            dxn = dyv * gain
            dxh = inv * (dxn - xn * jnp.mean(dxn * xn, axis=-1, keepdims=True))
            return dxh, jnp.sum(dyv * xn, axis=0, keepdims=True)

        dgq = jnp.zeros((1, HEAD_DIM), F32)
        for h in range(N_Q_HEADS):
            sl = slice(h * HEAD_DIM, (h + 1) * HEAD_DIM)
            dxh, dgh = head(q_ref[:, sl].astype(F32), dq_ref[:, sl], gq_ref[...])
            o_ref[:, sl] = dxh.astype(BF16)
            dgq = dgq + dgh
        dh = dh + jnp.dot(dg_ref[...], w_ref[g0:, :], preferred_element_type=F32)
        dgk = jnp.zeros((1, HEAD_DIM), F32)
        for h in range(N_KV_HEADS):
            sl = slice(h * HEAD_DIM, (h + 1) * HEAD_DIM)
            dxh, dgh = head(k_ref[:, sl].astype(F32), dk_ref[:, sl], gk_ref[...])
            o_ref[:, QW + h * HEAD_DIM:QW + (h + 1) * HEAD_DIM] = dxh.astype(BF16)
            dgk = dgk + dgh
        o_ref[:, QW + KW:QW + 2 * KW] = dv_ref[...].astype(BF16)
        qacc_ref[0, 0:1, 0:HEAD_DIM] += dgq
        qacc_ref[0, 1:2, 0:HEAD_DIM] += dgk
        dh = dh + jnp.dot(o_ref[...], w_ref[q0:g0, :], preferred_element_type=F32)
        m = m_ref[0]
        dx = _norm_tile_bwd(x_ref[...], dh, dr_ref[...], m, g_ref[...], shift_idx, scale_idx, acc_ref)
        dx_ref[...] = dx
        db_ref[...] = _gate_tile_bwd(dx, b_ref[...], m, gate, acc_ref)

    return pl.pallas_call(
        body, name=name, grid=(T // ROW,),
        in_specs=[_row_spec(q0), _row_spec(dgt.shape[1]), _row_spec(QW, q0 // QW), _row_spec(KW, (q0 + QW) // KW),
                  _row_spec(QW), _row_spec(KW), _row_spec(KW), _vec_spec(1, HEAD_DIM), _vec_spec(1, HEAD_DIM),
                  _row_spec(HEAD_DIM), _row_spec(HEAD_DIM),
                  _resident(), _row_spec(D), _row_spec(D), _row_spec(D), _mods_spec(D), _vec_spec(1, D)],
        out_specs=[_row_spec(D), _row_spec(D), _acc_spec(D), _row_spec(QW + 2 * KW), _acc_spec(D)],
        out_shape=[jax.ShapeDtypeStruct((T, D), F32), jax.ShapeDtypeStruct((T, D), BF16),
                   jax.ShapeDtypeStruct((2, ACC_ROWS, D), F32), jax.ShapeDtypeStruct((T, QW + 2 * KW), BF16),
                   jax.ShapeDtypeStruct((2, ACC_ROWS, D), F32)],
        compiler_params=_params(("arbitrary",)),
    )(dconv, dgt, P, P, dq, dk, dv, gq, gk, cos_t, sin_t, w_t, x, dres, branch, mods, g)


def _adamw_transposed(w, gt, m, v, name):
    R, C = w.shape
    tc = 128

    def body(w_ref, g_ref, m_ref, v_ref, go_ref, d_ref, mo_ref, vo_ref):
        g = jnp.transpose(g_ref[...])
        d, mn, vn = _adamw_math(w_ref[...], g, m_ref[...], v_ref[...])
        go_ref[...] = g
        d_ref[...] = d
        mo_ref[...] = mn
        vo_ref[...] = vn

    spec = pl.BlockSpec((R, tc), lambda j: (0, j))
    return pl.pallas_call(
        body, name=name, grid=(C // tc,),
        in_specs=[spec, pl.BlockSpec((tc, R), lambda j: (j, 0)), spec, spec], out_specs=[spec] * 4,
        out_shape=[jax.ShapeDtypeStruct((R, C), F32)] * 4,
        compiler_params=_params(("parallel",)),
    )(w, gt, m, v)


class _NoExchange:
    def __init__(self, rest):
        self.rest = rest

    def rest_weights(self, after):
        return self.rest

    def reduce_early(self, grads, tag):
        return None


def _local_step(xcat, target, mods, norm_g, final_g, gq, gk, conv_w, ffn1_w, hooks, ctx_len):
    T, D = _rows_operand(xcat)[2]
    w1i, w1o = ffn1_w
    g1, g2, g3 = norm_g
    cos_t, sin_t = _rope_tables(ctx_len, T - ctx_len)

    def after(value, token, name):
        return value if token is None else _after(value, token, name)

    _, h1, u1, s1, f1 = _norm_ffn_fwd(xcat, None, mods, g1, None, 0, 1, w1i, w1o, "f_ffn1")
    wi, wbc, wba, wo, w2i, w2o = hooks.rest_weights(f1)
    x1, h2, P, qn, kn, vb = _norm_mix_in_fwd(xcat, f1, mods, g2, (2, 0.5), 3, 4, wi, gq, gk, cos_t, sin_t, "f_mix_in")
    yc = _conv_fwd(P, conv_w, D, "f_conv")
    o, lse = _flash_fwd(qn, kn, vb, "f_attn")
    a1, a2, z, mo = _merge_fwd(yc, o, P, wbc, wba, wo, D, "f_merge")
    x2, h3, u2, s2, dx3, df2, acc_head = _norm_ffn_fwd(x1, mo, mods, g3, (5, 1.0), 6, 7, w2i, w2o, "f_ffn2",
                                                       head=(final_g, target))

    du2, dx2, dmo, acc_n3 = _ffn_norm_bwd(df2, u2, w2i, w2o, x2, dx3, mods, g3, 6, 7, (5, 1.0), mo, "b_ffn2")
    g_w2o = _matmul(s2, df2, "tn", BF16, "b_ffn2_out_dw")
    g_w2i = _matmul(du2, h3, "tn", BF16, "b_ffn2_in_dw")

    g_wo = _matmul(z, dmo, "tn", BF16, "b_mix_out_dw")
    da1, da2, dgt, dyc, dob, delta = _merge_bwd(dmo, a1, a2, o, P, wbc, wba, wo, D, "b_merge")
    g_wbc = _matmul(yc, da1, "tn", BF16, "b_branch_conv_dw")
    g_wba = _matmul(o, da2, "tn", BF16, "b_branch_attn_dw")
    token_a = hooks.reduce_early([g_wbc, g_wba, g_wo, g_w2i, g_w2o], "a")
    dq, dk, dv = _flash_bwd(qn, kn, vb, dob, lse, delta, "b_attn", token=token_a)
    dconv, acc_conv = _conv_bwd(P, dyc, conv_w, D, "b_conv")
    dx1, df1, acc_n2, dqkv, acc_qk = _mix_in_norm_bwd(dconv, dgt, P, dq, dk, dv, gq, gk, cos_t, sin_t, wi, x1, dx2,
                                                      mods, g2, 3, 4, (2, 0.5), f1, "b_mix_in")
    d_parts = (dconv, dqkv, dgt)
    g_wi = jnp.concatenate([_matmul(dp, h2, "tn", BF16, f"b_mix_in_dw_{i}") for i, dp in enumerate(d_parts)], axis=0)
    g1_b = after(g1, hooks.reduce_early([g_wi], "b"), "after_rs_b")

    du1, grad_x, _, acc_n1 = _ffn_norm_bwd(df1, u1, w1i, w1o, xcat, dx1, mods, g1_b, 0, 1, None, None, "b_ffn1",
                                           skip_first_tile=True)
    g_w1o = _matmul(s1, df1, "tn", BF16, "b_ffn1_out_dw")
    g_w1i = _matmul(du1, h1, "tn", BF16, "b_ffn1_in_dw", token=hooks.reduce_early([g_w1o], "c"))

    grads = (g_w1i, g_w1o, g_wi, g_wbc, g_wba, g_wo, g_w2i, g_w2o)
    accs = (acc_head, acc_n3, acc_n2, acc_n1, acc_conv, acc_qk)
    return grad_x, grads, accs


def _place():
    return lax.axis_index("x"), lax.axis_index("y"), lax.axis_index("c")


def _other_chips(x, y):
    return [(1 - x, y), (x, 1 - y), (1 - x, 1 - y)]


def _allgather8(v, name):
    R, N = v.shape

    def body(v_ref, out_ref, send_sems, recv_sems, local_sem):
        x, y, c = _place()
        me, sibling = (x, y, c), (x, y, 1 - c)
        chips = _other_chips(x, y)

        def blk(px, py, pc):
            return out_ref.at[4 * px + 2 * py + pc]

        def copy(k, block, to, src=None):
            return pltpu.make_async_remote_copy(
                src_ref=blk(*block) if src is None else src, dst_ref=blk(*block),
                send_sem=send_sems.at[k], recv_sem=recv_sems.at[k], device_id=to, device_id_type=MESH)

        mine = pltpu.make_async_copy(v_ref, blk(*me), local_sem)
        mine.start()
        first = [copy(0, me, sibling, src=v_ref)]
        first += [copy(1 + j, me, (*chip, c), src=v_ref) for j, chip in enumerate(chips)]
        for cp in first:
            cp.start()
        passed = [copy(4 + j, (*chip, c), sibling) for j, chip in enumerate(chips)]
        for j, chip in enumerate(chips):
            copy(1 + j, (*chip, c), me).wait_recv()
            passed[j].start()
        copy(0, sibling, me).wait_recv()
        for j, chip in enumerate(chips):
            copy(4 + j, (*chip, 1 - c), me).wait_recv()
        for cp in first + passed:
            cp.wait_send()
        mine.wait()

    return pl.pallas_call(
        body, name=name,
        out_shape=jax.ShapeDtypeStruct((N_DEV, R, N), v.dtype),
        in_specs=[pl.BlockSpec(memory_space=pltpu.VMEM)],
        out_specs=pl.BlockSpec(memory_space=pltpu.VMEM),
        scratch_shapes=[pltpu.SemaphoreType.DMA((7,)), pltpu.SemaphoreType.DMA((7,)), pltpu.SemaphoreType.DMA],
        compiler_params=pltpu.CompilerParams(vmem_limit_bytes=VMEM_LIMIT),
    )(v)


def _any_specs(n):
    return [pl.BlockSpec(memory_space=pl.ANY)] * n


def _pair_exchange(grads, name):
    n = len(grads)

    def body(*refs):
        g, land = refs[:n], refs[n:2 * n]
        send_sems, recv_sems = refs[2 * n:]
        x, y, c = _place()
        sibling = (x, y, 1 - c)
        copies = []
        for t in range(n):
            half = grads[t].shape[0] // (2 * N_CHIPS)
            for s in range(N_CHIPS):
                cp = pltpu.make_async_remote_copy(
                    src_ref=g[t].at[pl.ds((2 * s + 1 - c) * half, half), :], dst_ref=land[t].at[s],
                    send_sem=send_sems.at[N_CHIPS * t + s], recv_sem=recv_sems.at[N_CHIPS * t + s],
                    device_id=sibling, device_id_type=MESH)
                cp.start()
                copies.append(cp)
        for cp in copies:
            cp.wait_recv()
        for cp in copies:
            cp.wait_send()

    return pl.pallas_call(
        body, name=name,
        out_shape=[jax.ShapeDtypeStruct((N_CHIPS, a.shape[0] // (2 * N_CHIPS), a.shape[1]), a.dtype) for a in grads],
        in_specs=_any_specs(n), out_specs=_any_specs(n),
        scratch_shapes=[pltpu.SemaphoreType.DMA((N_CHIPS * n,)), pltpu.SemaphoreType.DMA((N_CHIPS * n,))],
    )(*grads)


def _place_shard(w2, idx, transpose, name, token):
    if transpose:
        D, rs = w2.shape
        tr = 128
        in_spec = pl.BlockSpec((D, tr), lambda i, idx: (0, i))
    else:
        rs, D = w2.shape
        tr = _pick(rs, (352, 256, 128, 64, 32, 16))
        in_spec = pl.BlockSpec((tr, D), lambda i, idx: (i, 0))
    steps = rs // tr

    def body(idx_ref, w_ref, t_ref, o_ref):
        v = w_ref[...]
        o_ref[...] = (jnp.transpose(v) if transpose else v).astype(BF16)

    return pl.pallas_call(
        body, name=name,
        grid_spec=pltpu.PrefetchScalarGridSpec(
            num_scalar_prefetch=1, grid=(steps,),
            in_specs=[in_spec, pl.BlockSpec(token.shape, lambda i, idx: (0, 0))],
            out_specs=pl.BlockSpec((tr, D), lambda i, idx: (idx[1] * steps + i, 0))),
        out_shape=jax.ShapeDtypeStruct((N_CHIPS * rs, D), BF16),
        compiler_params=_params(("arbitrary",)),
    )(idx, w2, token)


def _pair_sum(g, landed, idx, name, token=None):
    _, half, D = landed.shape
    g4 = g.reshape(N_CHIPS, 2, half, D)
    tr = _pick(half, (416, 352, 128))
    extra = [] if token is None else [token]

    def body(idx_ref, g_ref, l_ref, *rest):
        rest[-1][...] = (g_ref[0].astype(F32) + l_ref[...].astype(F32)).astype(BF16)

    return pl.pallas_call(
        body, name=name,
        grid_spec=pltpu.PrefetchScalarGridSpec(
            num_scalar_prefetch=1, grid=(N_CHIPS, half // tr),
            in_specs=[pl.BlockSpec((1, 1, tr, D), lambda s, i, idx: (idx[1 + s], idx[0], i, 0)),
                      pl.BlockSpec((1, tr, D), lambda s, i, idx: (idx[1 + s], i, 0))] +
                     [pl.BlockSpec(t.shape, lambda s, i, idx: (0, 0)) for t in extra],
            out_specs=pl.BlockSpec((1, tr, D), lambda s, i, idx: (s, i, 0))),
        out_shape=jax.ShapeDtypeStruct((N_CHIPS, half, D), BF16),
        compiler_params=_params(("arbitrary", "arbitrary")),
    )(idx, g4, landed, *extra)


_HBM = pl.BlockSpec(memory_space=pltpu.HBM)
_SEM = pl.BlockSpec(memory_space=pltpu.SEMAPHORE)
_EFFECT = pltpu.SideEffectType.DATAFLOW_SIDE_EFFECTING


def _in_hbm(a):
    return pltpu.with_memory_space_constraint(a, pltpu.HBM)


def _split_copies(n, per, make):
    def start(nbuf, name, bufs):
        def body(*refs):
            ins = refs[:nbuf]
            send_sems, recv_sems = refs[nbuf], refs[nbuf + 1]
            token = refs[-1]
            for t in range(n):
                for j in range(per):
                    make(ins, t, j, send_sems.at[per * t + j], recv_sems.at[per * t + j]).start()
            token[...] = jnp.zeros(token.shape, token.dtype)

        out = pl.pallas_call(
            body, name=name,
            out_shape=(pltpu.SemaphoreType.DMA((per * n,)), pltpu.SemaphoreType.DMA((per * n,)),
                       *[pltpu.HBM(b.shape, b.dtype) for b in bufs], jax.ShapeDtypeStruct((8, 128), F32)),
            in_specs=[_HBM] * nbuf,
            out_specs=(_SEM, _SEM, *[_HBM] * nbuf, pl.BlockSpec(memory_space=pltpu.VMEM)),
            input_output_aliases={i: 2 + i for i in range(nbuf)},
            compiler_params=pltpu.CompilerParams(has_side_effects=_EFFECT),
        )(*[_in_hbm(b) for b in bufs])
        return out[0], out[1], list(out[2:2 + nbuf]), out[-1]

    def wait(nbuf, name, send_sems, recv_sems, bufs, after):
        def body(*refs):
            ins = refs[:nbuf]
            ss, rs = refs[nbuf], refs[nbuf + 1]
            for t in range(n):
                for j in range(per):
                    cp = make(ins, t, j, ss.at[per * t + j], rs.at[per * t + j])
                    cp.wait_send()
                    cp.wait_recv()

        return pl.pallas_call(
            body, name=name,
            out_shape=[pltpu.HBM(b.shape, b.dtype) for b in bufs],
            in_specs=[_HBM] * nbuf + [_SEM, _SEM, pl.BlockSpec(memory_space=pl.ANY)],
            out_specs=[_HBM] * nbuf,
            input_output_aliases={i: i for i in range(nbuf)},
            compiler_params=pltpu.CompilerParams(has_side_effects=_EFFECT),
        )(*bufs, send_sems, recv_sems, after)

    return start, wait


def _chip_exchange_split(n):
    def make(bufs, t, j, send_sem, recv_sem):
        x, y, c = _place()
        chip = _other_chips(x, y)[j]
        return pltpu.make_async_remote_copy(src_ref=bufs[t].at[1 + j], dst_ref=bufs[n + t].at[j], send_sem=send_sem,
                                            recv_sem=recv_sem, device_id=(*chip, c), device_id_type=MESH)

    return _split_copies(n, 3, make)


def _weights_gather_split(fulls):
    def make(bufs, t, j, send_sem, recv_sem):
        x, y, c = _place()
        chip = _other_chips(x, y)[j]
        rs = fulls[t].shape[0] // N_CHIPS
        rows = bufs[t].at[pl.ds((2 * x + y) * rs + c * (rs // 2), rs // 2), :]
        return pltpu.make_async_remote_copy(src_ref=rows, dst_ref=rows, send_sem=send_sem, recv_sem=recv_sem,
                                            device_id=(*chip, c), device_id_type=MESH)

    return _split_copies(len(fulls), 3, make)


def _weights_pass_on(fulls, name):
    n = len(fulls)

    def body(*refs):
        full = refs[n:2 * n]
        send_sems, recv_sems = refs[2 * n:]
        x, y, c = _place()
        chips = _other_chips(x, y)

        def copy(t, j, h):
            rs = fulls[t].shape[0] // N_CHIPS
            px, py = chips[j]
            rows = full[t].at[pl.ds((2 * px + py) * rs + h * (rs // 2), rs // 2), :]
            return pltpu.make_async_remote_copy(src_ref=rows, dst_ref=rows, send_sem=send_sems.at[3 * t + j],
                                                recv_sem=recv_sems.at[3 * t + j], device_id=(x, y, 1 - c),
                                                device_id_type=MESH)

        for t in range(n):
            for j in range(3):
                copy(t, j, c).start()
        for t in range(n):
            for j in range(3):
                copy(t, j, 1 - c).wait_recv()
        for t in range(n):
            for j in range(3):
                copy(t, j, c).wait_send()

    return pl.pallas_call(
        body, name=name,
        out_shape=[jax.ShapeDtypeStruct(f.shape, f.dtype) for f in fulls],
        in_specs=_any_specs(n), out_specs=_any_specs(n),
        input_output_aliases={t: t for t in range(n)},
        scratch_shapes=[pltpu.SemaphoreType.DMA((3 * n,)), pltpu.SemaphoreType.DMA((3 * n,))],
    )(*fulls)


def _after(value, token, name):
    def body(v_ref, t_ref, o_ref):
        o_ref[...] = v_ref[...]

    return pl.pallas_call(
        body, name=name, out_shape=jax.ShapeDtypeStruct(value.shape, value.dtype),
        in_specs=_whole(2), out_specs=pl.BlockSpec(memory_space=pltpu.VMEM),
    )(value, token)


def _chip_sum(ps, landed, idx, name):
    _, half, D = ps.shape
    tr = _pick(half, (416, 352, 128))
    steps = half // tr

    def body(idx_ref, p_ref, l_ref, o_ref):
        acc = p_ref[0].astype(F32)
        for j in range(3):
            acc = acc + l_ref[j].astype(F32)
        o_ref[...] = acc

    return pl.pallas_call(
        body, name=name,
        grid_spec=pltpu.PrefetchScalarGridSpec(
            num_scalar_prefetch=1, grid=(steps,),
            in_specs=[pl.BlockSpec((1, tr, D), lambda i, idx: (0, i, 0)),
                      pl.BlockSpec((3, tr, D), lambda i, idx: (0, i, 0))],
            out_specs=pl.BlockSpec((tr, D), lambda i, idx: (idx[0] * steps + i, 0))),
        out_shape=jax.ShapeDtypeStruct((2 * half, D), F32),
        compiler_params=_params(("arbitrary",)),
    )(idx, ps, landed)


def _pair_swap(shards, name):
    n = len(shards)

    def body(*refs):
        full = refs[n:2 * n]
        send_sems, recv_sems = refs[2 * n:]
        x, y, c = _place()

        def half(t, h):
            rows = shards[t].shape[0] // 2
            return full[t].at[pl.ds(h * rows, rows), :]

        def copy(t, h):
            return pltpu.make_async_remote_copy(src_ref=half(t, h), dst_ref=half(t, h), send_sem=send_sems.at[t],
                                                recv_sem=recv_sems.at[t], device_id=(x, y, 1 - c),
                                                device_id_type=MESH)

        for t in range(n):
            copy(t, c).start()
        for t in range(n):
            copy(t, 1 - c).wait_recv()
        for t in range(n):
            copy(t, c).wait_send()

    return pl.pallas_call(
        body, name=name,
        out_shape=[jax.ShapeDtypeStruct(a.shape, a.dtype) for a in shards],
        in_specs=_any_specs(n), out_specs=_any_specs(n),
        input_output_aliases={t: t for t in range(n)},
        scratch_shapes=[pltpu.SemaphoreType.DMA((n,)), pltpu.SemaphoreType.DMA((n,))],
    )(*shards)


def _gather_begin(fulls, tag):
    start, wait = _weights_gather_split(fulls)
    send_sems, recv_sems, bufs, token = start(len(fulls), f"ag_{tag}_start", fulls)
    return (wait, send_sems, recv_sems, bufs), token


def _gather_end(state, after, tag):
    wait, send_sems, recv_sems, bufs = state
    landed = wait(len(bufs), f"ag_{tag}_wait", send_sems, recv_sems, bufs, after)
    return _weights_pass_on(landed, f"ag_{tag}_pass_on")


class _Exchanges:
    def __init__(self, fulls_rest, idx):
        self.idx = idx
        self._rest, self.token = _gather_begin(fulls_rest, "rest")
        self._early = []

    def rest_weights(self, after):
        return _gather_end(self._rest, after, "rest")

    def reduce_early(self, grads, tag, token=None):
        landed = _pair_exchange(grads, "rs_pair_exchange_" + tag)
        sums = [_pair_sum(g, l, self.idx, f"rs_pair_sum_{tag}{t}", token)
                for t, (g, l) in enumerate(zip(grads, landed))]
        zones = [lax.empty((3,) + s.shape[1:], s.dtype) for s in sums]
        start, wait = _chip_exchange_split(len(sums))
        send_sems, recv_sems, bufs, token = start(2 * len(sums), "rs_chip_start_" + tag, sums + zones)
        self._early.append((tag, wait, send_sems, recv_sems, bufs))
        return token

    def finish(self, tags, after):
        halves = []
        for tag, wait, send_sems, recv_sems, bufs in self._early:
            if tag in tags:
                n = len(bufs) // 2
                done = wait(len(bufs), "rs_chip_wait_" + tag, send_sems, recv_sems, bufs, after)
                halves += [_chip_sum(p, l, self.idx, f"rs_chip_sum_{tag}{t}")
                           for t, (p, l) in enumerate(zip(done[:n], done[n:]))]
        return halves


N_MOD = 9
PACK_HEAD, PACK_N3, PACK_N2, PACK_N1, PACK_CONV, PACK_QK = 0, 16, 32, 48, 64, 80
PACK_ROWS = 96
MOD_SRC = ((PACK_N1, 0), (PACK_N1, 1), (PACK_N2, 3), (PACK_N2, 0), (PACK_N2, 1),
           (PACK_N3, 3), (PACK_N3, 0), (PACK_N3, 1), (PACK_HEAD, 2))
CTX_ROW = 8


def _silu(v):
    return v * jax.nn.sigmoid(v)


def _whole(n):
    return [pl.BlockSpec(memory_space=pltpu.VMEM)] * n


def _mod_rows(cin, w_sh, b_sh, name):
    def body(c_ref, w_ref, b_ref, o_ref):
        a = _silu(c_ref[...]).astype(BF16)
        o_ref[...] = jnp.dot(a, w_ref[...].astype(BF16), preferred_element_type=F32) + b_ref[...]

    return pl.pallas_call(
        body, name=name, out_shape=jax.ShapeDtypeStruct((cin.shape[0], w_sh.shape[1]), F32),
        in_specs=_whole(3), out_specs=pl.BlockSpec(memory_space=pltpu.VMEM),
        compiler_params=pltpu.CompilerParams(vmem_limit_bytes=VMEM_LIMIT),
    )(cin, w_sh, b_sh)


def _small_reduce(gathered, name):
    _, _, D = gathered.shape

    def body(g_ref, loss_ref, db_ref, gn_ref, cv_ref, qk_ref, dm_ref):
        tot = g_ref[0]
        for r in range(1, N_DEV):
            tot = tot + g_ref[r]

        def both(block, row):
            return tot[block + row:block + row + 1, :] + tot[block + 8 + row:block + 8 + row + 1, :]

        loss = jnp.sum(both(PACK_HEAD, 0), axis=1, keepdims=True)
        loss_ref[...] = jnp.broadcast_to(loss, loss_ref.shape)
        db_ref[...] = jnp.zeros(db_ref.shape, F32)
        dm_ref[...] = jnp.zeros(dm_ref.shape, F32)
        for j, (block, row) in enumerate(MOD_SRC):
            db_ref[j:j + 1, :] = both(block, row)
            dm_ref[CTX_ROW, j:j + 1, :] = tot[block + row:block + row + 1, :]
            for r in range(N_DEV):
                dm_ref[r, j:j + 1, :] = g_ref[r, block + 8 + row:block + 8 + row + 1, :]
        gn_ref[...] = jnp.zeros(gn_ref.shape, F32)
        gn_ref[0:1, :] = both(PACK_N1, 2)
        gn_ref[8:9, :] = both(PACK_N2, 2)
        gn_ref[16:17, :] = both(PACK_N3, 2)
        gn_ref[24:25, :] = both(PACK_HEAD, 1)
        cv_ref[...] = jnp.zeros(cv_ref.shape, F32)
        for r in range(3):
            cv_ref[r:r + 1, :] = both(PACK_CONV, r)
        qk_ref[...] = jnp.zeros(qk_ref.shape, F32)
        qk_ref[0:1, 0:HEAD_DIM] = both(PACK_QK, 0)[:, 0:HEAD_DIM]
        qk_ref[0:1, HEAD_DIM:2 * HEAD_DIM] = both(PACK_QK, 1)[:, 0:HEAD_DIM]

    return pl.pallas_call(
        body, name=name,
        out_shape=[jax.ShapeDtypeStruct((8, 128), F32), jax.ShapeDtypeStruct((16, D), F32),
                   jax.ShapeDtypeStruct((32, D), F32), jax.ShapeDtypeStruct((8, D), F32),
                   jax.ShapeDtypeStruct((8, D), F32), jax.ShapeDtypeStruct((16, 16, D), F32)],
        in_specs=_whole(1), out_specs=_whole(6),
        compiler_params=pltpu.CompilerParams(vmem_limit_bytes=VMEM_LIMIT),
    )(gathered)


def _wmod_grad(cin, dm_sh, w_sh, name):
    def body(c_ref, d_ref, w_ref, gw_ref, cp_ref):
        a = _silu(c_ref[...]).astype(BF16)
        d = d_ref[...].astype(BF16)
        gw_ref[...] = lax.dot_general(a, d, (((0,), (0,)), ((), ())), preferred_element_type=F32)
        cp_ref[...] = lax.dot_general(d, w_ref[...].astype(BF16), (((1,), (1,)), ((), ())),
                                      preferred_element_type=F32)

    return pl.pallas_call(
        body, name=name,
        out_shape=[jax.ShapeDtypeStruct(w_sh.shape, F32), jax.ShapeDtypeStruct(cin.shape, F32)],
        in_specs=_whole(3), out_specs=_whole(2),
        compiler_params=pltpu.CompilerParams(vmem_limit_bytes=VMEM_LIMIT),
    )(cin, dm_sh, w_sh)


def _cctx_grad(parts, c_ctx8, name):
    def body(p_ref, c_ref, o_ref):
        tot = p_ref[0] + p_ref[2] + p_ref[4] + p_ref[6]
        cv = c_ref[...]
        sig = jax.nn.sigmoid(cv)
        rows = lax.broadcasted_iota(jnp.int32, tot.shape, 0)
        o_ref[...] = jnp.where(rows == 0, tot * (sig * (1.0 + cv * (1.0 - sig))), 0.0)

    return pl.pallas_call(
        body, name=name, out_shape=jax.ShapeDtypeStruct(c_ctx8.shape, F32),
        in_specs=_whole(2), out_specs=pl.BlockSpec(memory_space=pltpu.VMEM),
    )(parts, c_ctx8)


def _pad_rows(a, rows):
    return jnp.pad(a, ((0, rows - a.shape[0]), (0, 0)))


def _pack_small(c_ctx, b_mod, n1, n2, n3, final_g, gq, gk, conv_sh, D):
    misc = jnp.concatenate([gq, gk, conv_sh.reshape(1, -1)], axis=1)
    return jnp.concatenate([_pad_rows(c_ctx[None], 8), _pad_rows(b_mod.reshape(N_MOD, D), 16), _pad_rows(n1, 8),
                            _pad_rows(n2, 8), _pad_rows(n3, 8), _pad_rows(final_g[None], 8), _pad_rows(misc, 8)], axis=0)


def _unpack_small(p, D, conv_shape):
    misc = p[56:57]
    return dict(c_ctx=p[0], b_mod=p[8:8 + N_MOD].reshape(1, N_MOD * D), norm1_g=p[24:25], norm2_g=p[32:33],
                norm3_g=p[40:41], final_g=p[48], q_norm_g=misc[:, 0:HEAD_DIM], k_norm_g=misc[:, HEAD_DIM:2 * HEAD_DIM],
                conv_w=misc[:, 2 * HEAD_DIM:].reshape(conv_shape))


WEIGHT_ORDER = ("c_ctx", "w_mod", "b_mod", "norm1_g", "norm2_g", "norm3_g", "ffn1_w_in", "ffn1_w_out", "w_in",
                "conv_w", "q_norm_g", "k_norm_g", "w_branch_conv", "w_branch_attn", "w_out", "ffn2_w_in",
                "ffn2_w_out", "final_g")
BIG = ("ffn1_w_in", "ffn1_w_out", "w_in", "w_branch_conv", "w_branch_attn", "w_out", "ffn2_w_in", "ffn2_w_out")
COLUMN_SHARDED = ("ffn1_w_in", "w_in", "ffn2_w_in")


def kernel(x, c, ctx, c_ctx, w_mod, b_mod, norm1_g, norm2_g, norm3_g, ffn1_w_in, ffn1_w_out, w_in, conv_w, q_norm_g, k_norm_g, w_branch_conv, w_branch_attn, w_out, ffn2_w_in, ffn2_w_out, final_g, loss_target, m_c_ctx, m_w_mod, m_b_mod, m_norm1_g, m_norm2_g, m_norm3_g, m_ffn1_w_in, m_ffn1_w_out, m_w_in, m_conv_w, m_q_norm_g, m_k_norm_g, m_w_branch_conv, m_w_branch_attn, m_w_out, m_ffn2_w_in, m_ffn2_w_out, m_final_g, v_c_ctx, v_w_mod, v_b_mod, v_norm1_g, v_norm2_g, v_norm3_g, v_ffn1_w_in, v_ffn1_w_out, v_w_in, v_conv_w, v_q_norm_g, v_k_norm_g, v_w_branch_conv, v_w_branch_attn, v_w_out, v_ffn2_w_in, v_ffn2_w_out, v_final_g):
    w = dict(c_ctx=c_ctx, w_mod=w_mod, b_mod=b_mod, norm1_g=norm1_g, norm2_g=norm2_g, norm3_g=norm3_g,
             ffn1_w_in=ffn1_w_in, ffn1_w_out=ffn1_w_out, w_in=w_in, conv_w=conv_w, q_norm_g=q_norm_g,
             k_norm_g=k_norm_g, w_branch_conv=w_branch_conv, w_branch_attn=w_branch_attn, w_out=w_out,
             ffn2_w_in=ffn2_w_in, ffn2_w_out=ffn2_w_out, final_g=final_g)
    m = dict(c_ctx=m_c_ctx, w_mod=m_w_mod, b_mod=m_b_mod, norm1_g=m_norm1_g, norm2_g=m_norm2_g, norm3_g=m_norm3_g,
             ffn1_w_in=m_ffn1_w_in, ffn1_w_out=m_ffn1_w_out, w_in=m_w_in, conv_w=m_conv_w, q_norm_g=m_q_norm_g,
             k_norm_g=m_k_norm_g, w_branch_conv=m_w_branch_conv, w_branch_attn=m_w_branch_attn, w_out=m_w_out,
             ffn2_w_in=m_ffn2_w_in, ffn2_w_out=m_ffn2_w_out, final_g=m_final_g)
    v = dict(c_ctx=v_c_ctx, w_mod=v_w_mod, b_mod=v_b_mod, norm1_g=v_norm1_g, norm2_g=v_norm2_g, norm3_g=v_norm3_g,
             ffn1_w_in=v_ffn1_w_in, ffn1_w_out=v_ffn1_w_out, w_in=v_w_in, conv_w=v_conv_w, q_norm_g=v_q_norm_g,
             k_norm_g=v_k_norm_g, w_branch_conv=v_w_branch_conv, w_branch_attn=v_w_branch_attn, w_out=v_w_out,
             ffn2_w_in=v_ffn2_w_in, ffn2_w_out=v_ffn2_w_out, final_g=v_final_g)

    xi, yi, ci = _place()
    dev = 4 * xi + 2 * yi + ci
    shard = 2 * xi + yi
    idx = jnp.stack([ci, shard, 2 * (1 - xi) + yi, 2 * xi + (1 - yi), 2 * (1 - xi) + (1 - yi)]).astype(jnp.int32)
    D = x.shape[-1]
    ctx_len = ctx.shape[1]
    assert ctx_len == ROW and c.shape == (1, D)
    mcols = w_mod.shape[2]
    ccols = conv_w.shape[2]

    c_all = _allgather8(jnp.broadcast_to(c, (8, D)), "ag_c")[:, 0, :]
    cin = jnp.concatenate([c_all, _pad_rows(c_ctx[None], 8)], axis=0)
    b_sh = lax.dynamic_slice(b_mod, (0, shard * mcols), (1, mcols))
    mod_sh = _mod_rows(cin, w_mod[0], b_sh, "mod_rows")
    conv_rows = jnp.pad(conv_w[0], ((0, 8 - conv_w.shape[1]), (0, mcols - ccols)))
    mod_all = _allgather8(jnp.concatenate([mod_sh, conv_rows], axis=0), "ag_mod")
    mod_full = jnp.concatenate([mod_all[2 * s, :16] for s in range(N_CHIPS)], axis=1)
    conv_full = jnp.concatenate([mod_all[2 * s, 16:16 + conv_w.shape[1], :ccols] for s in range(N_CHIPS)], axis=1)
    mod_lat = lax.dynamic_slice(mod_full, (dev, 0), (1, N_MOD * D)).reshape(N_MOD, D)
    mod_ctx = mod_full[CTX_ROW].reshape(N_MOD, D)
    mods = jnp.stack([_pad_rows(mod_ctx, 16), _pad_rows(mod_lat, 16)])

    def place(names, token):
        return [_place_shard(w[n][0], idx, n in COLUMN_SHARDED, "place_" + n, token) for n in names]

    ffn1_gather, ffn1_token = _gather_begin(place(BIG[:2], mod_all[0, :8, :HEAD_DIM]), "ffn1")
    fulls_rest = place(BIG[2:], ffn1_token)
    ffn1_w = _gather_end(ffn1_gather, fulls_rest[-1][:16, :HEAD_DIM], "ffn1")
    hooks = _Exchanges(fulls_rest, idx)

    xcat = (ctx[0], x[0])
    norm1_first = _after(norm1_g, hooks.token, "after_ag_rest")
    grad_x, grads, accs = _local_step(xcat, loss_target[0], mods, (norm1_first, norm2_g, norm3_g), final_g[None],
                                      q_norm_g, k_norm_g, conv_full, ffn1_w, hooks, ctx_len)
    g = {}

    pack = jnp.concatenate([a.reshape(2 * ACC_ROWS, D) for a in accs], axis=0)
    gathered = _allgather8(pack, "ag_small")
    loss8, db_mod, g_norms, g_conv, g_qk, dm = _small_reduce(gathered, "small_reduce")
    dm_sh = lax.dynamic_slice(dm[:, :N_MOD, :].reshape(16, N_MOD * D), (0, shard * mcols), (16, mcols))
    g_wmod, cpart = _wmod_grad(cin, dm_sh, w_mod[0], "wmod_grad")
    g["w_mod"] = g_wmod[None]
    cparts = _allgather8(cpart[CTX_ROW:CTX_ROW + 8], "ag_cctx")
    g_cctx = _cctx_grad(cparts, _pad_rows(c_ctx[None], 8), "cctx_grad")
    g_conv_sh = lax.dynamic_slice(g_conv, (0, shard * ccols), (conv_w.shape[1], ccols))
    g_misc = jnp.concatenate([g_qk[0:1, 0:2 * HEAD_DIM], g_conv_sh.reshape(1, -1)], axis=1)
    g_pack = jnp.concatenate([g_cctx, db_mod, g_norms, _pad_rows(g_misc, 8)], axis=0)

    def packed(p):
        return _pack_small(p["c_ctx"], p["b_mod"], p["norm1_g"], p["norm2_g"], p["norm3_g"], p["final_g"],
                           p["q_norm_g"], p["k_norm_g"], p["conv_w"][0], D)

    d_pack, m_pack, v_pack = _adamw(packed(w), g_pack, packed(m), packed(v), "adamw_small")

    g.update(_unpack_small(g_pack, D, conv_w.shape))
    delta = _unpack_small(d_pack, D, conv_w.shape)
    new_m = _unpack_small(m_pack, D, conv_w.shape)
    new_v = _unpack_small(v_pack, D, conv_w.shape)

    def update(n, g2):
        if n in COLUMN_SHARDED:
            g2, d2, m2, v2 = _adamw_transposed(w[n][0], g2, m[n][0], v[n][0], "adamw_" + n)
        else:
            d2, m2, v2 = _adamw(w[n][0], g2, m[n][0], v[n][0], "adamw_" + n)
        g[n], delta[n], new_m[n], new_v[n] = g2[None], d2[None], m2[None], v2[None]
        return v2

    token_d = hooks.reduce_early([grads[0]], "d", token=d_pack[:8, :HEAD_DIM])
    h_wbc, h_wba, h_wo, h_w2i, h_w2o, h_wi, h_w1o = hooks.finish("abc", token_d)
    done = _pair_swap([h_w1o, h_wi, h_wbc, h_wba, h_wo, h_w2i, h_w2o], "rs_pair_swap")
    last = update("w_mod", g_wmod)
    for n, r in zip(BIG[1:], done):
        last = update(n, r)
    (h_w1i,) = hooks.finish("d", last)
    update(BIG[0], _pair_swap([h_w1i], "rs_pair_swap_d")[0])

    loss = loss8[0, 0]
    return (loss, grad_x[None], *[g[n] for n in WEIGHT_ORDER], *[delta[n] for n in WEIGHT_ORDER],
            *[new_m[n] for n in WEIGHT_ORDER], *[new_v[n] for n in WEIGHT_ORDER])
```

```python
import functools

import jax
import jax.numpy as jnp
from jax import lax
from jax.experimental import pallas as pl
from jax.experimental.pallas import tpu as pltpu

F32 = jnp.float32
BF16 = jnp.bfloat16

HEAD_DIM = 128
N_Q_HEADS = 8
N_KV_HEADS = 2
GROUP = N_Q_HEADS // N_KV_HEADS
GRID_W = 64
ROPE_THETA = 10000.0
EPS = 1e-6
ATTN_SCALE = HEAD_DIM ** -0.5

ADAM_LR = 0.001
ADAM_B1 = 0.9
ADAM_B2 = 0.999
ADAM_EPS = 1e-08
ADAM_WD = 0.01
ADAM_STEP = 10

ROW = 256
HALO = 16
ACC_ROWS = 8
N_CHIPS = 4
N_DEV = 8
MESH = pl.DeviceIdType.MESH
VMEM_LIMIT = 48 * 1024 * 1024
ADAMW_BLOCK_BYTES = 1024 * 1024


def _pick(n, prefs):
    for p in prefs:
        if n % p == 0:
            return p
    return n


def _params(sem):
    return pltpu.CompilerParams(dimension_semantics=sem, vmem_limit_bytes=VMEM_LIMIT)


def _stream(i):
    return jnp.minimum(i, 1)


def _matmul(a, b, mode, out_dtype, name, tm=None, tn=None, tk=None, token=None):
    if mode == "nn":
        (M, K), (K2, N) = a.shape, b.shape
    elif mode == "nt":
        (M, K), (N, K2) = a.shape, b.shape
    else:
        (K, M), (K2, N) = a.shape, b.shape
    assert K == K2, (a.shape, b.shape, mode)
    tm = tm or _pick(M, (1664, 1408, 1024, 512, 256, 128) if mode == "tn" else (1408, 768, 512, 256, 128))
    tn = tn or _pick(N, (1664, 1408, 1024, 512, 256, 128))
    tk = tk or _pick(K, (1664, 1408, 1024, 768, 512, 256, 128))
    nk = K // tk
    if mode == "tn":
        a_spec = pl.BlockSpec((tk, tm), lambda i, j, k: (k, i))
    else:
        a_spec = pl.BlockSpec((tm, tk), lambda i, j, k: (i, k))
    if mode == "nt":
        b_spec = pl.BlockSpec((tn, tk), lambda i, j, k: (j, k))
    else:
        b_spec = pl.BlockSpec((tk, tn), lambda i, j, k: (k, j))
    dims = {"nn": ((1,), (0,)), "nt": ((1,), (1,)), "tn": ((0,), (0,))}[mode]
    use_scratch = nk > 1 and out_dtype != F32

    extra = [] if token is None else [token]

    def body(a_ref, b_ref, *rest):
        o_ref, scratch = rest[len(extra)], rest[len(extra) + 1:]
        p = lax.dot_general(a_ref[...].astype(BF16), b_ref[...].astype(BF16), (dims, ((), ())),
                            preferred_element_type=F32)
        if nk == 1:
            o_ref[...] = p.astype(o_ref.dtype)
            return
        acc_ref = scratch[0] if use_scratch else o_ref
        k = pl.program_id(2)

        @pl.when(k == 0)
        def _():
            acc_ref[...] = p

        @pl.when(k > 0)
        def _():
            acc_ref[...] += p

        if use_scratch:
            @pl.when(k == nk - 1)
            def _():
                o_ref[...] = acc_ref[...].astype(o_ref.dtype)

    return pl.pallas_call(
        body, name=name,
        grid=(M // tm, N // tn, nk),
        in_specs=[a_spec, b_spec] + [pl.BlockSpec(t.shape, lambda i, j, k: (0, 0)) for t in extra],
        out_specs=pl.BlockSpec((tm, tn), lambda i, j, k: (i, j)),
        out_shape=jax.ShapeDtypeStruct((M, N), out_dtype),
        scratch_shapes=[pltpu.VMEM((tm, tn), F32)] if use_scratch else [],
        compiler_params=_params(("parallel", "parallel", "arbitrary")),
    )(a, b, *extra)


def _row_spec(width, col=0):
    return pl.BlockSpec((ROW, width), lambda i, col=col: (i, col))


def _mods_spec(D):
    return pl.BlockSpec((1, 16, D), lambda i: (_stream(i), 0, 0))


def _acc_spec(D):
    return pl.BlockSpec((1, ACC_ROWS, D), lambda i: (_stream(i), 0, 0))


def _vec_spec(rows, D):
    return pl.BlockSpec((rows, D), lambda i: (0, 0))


def _acc_init(acc_ref):
    i = pl.program_id(0)

    @pl.when(i <= 1)
    def _():
        acc_ref[...] = jnp.zeros_like(acc_ref)


def _acc_add(acc_ref, row, val):
    acc_ref[0, row:row + 1, :] += jnp.sum(val, axis=0, keepdims=True)


def _rows_operand(x):
    if not isinstance(x, tuple):
        return [_row_spec(x.shape[1])], [x], x.shape
    ctx, lat = x
    D = lat.shape[1]
    assert ctx.shape == (ROW, D)
    specs = [pl.BlockSpec((ROW, D), lambda i: (0, 0)), pl.BlockSpec((ROW, D), lambda i: (jnp.maximum(i - 1, 0), 0))]
    return specs, [ctx, lat], (ROW + lat.shape[0], D)


def _rows_tile(refs):
    if len(refs) == 1:
        return refs[0][...]
    return jnp.where(pl.program_id(0) == 0, refs[0][...], refs[1][...])


def _norm_tile_fwd(x, m, g, shift_idx, scale_idx):
    inv = lax.rsqrt(jnp.mean(x * x, axis=-1, keepdims=True) + EPS)
    y = (x * inv) * g
    return (y * (1.0 + m[scale_idx:scale_idx + 1, :]) + m[shift_idx:shift_idx + 1, :]).astype(BF16)


def _norm_tile_bwd(x, dh, dres, m, g, shift_idx, scale_idx, acc_ref):
    inv = lax.rsqrt(jnp.mean(x * x, axis=-1, keepdims=True) + EPS)
    xn = x * inv
    dy = dh * (1.0 + m[scale_idx:scale_idx + 1, :])
    dxn = dy * g
    _acc_add(acc_ref, 0, dh)
    _acc_add(acc_ref, 1, dh * (xn * g))
    _acc_add(acc_ref, 2, dy * xn)
    return inv * (dxn - xn * jnp.mean(dxn * xn, axis=-1, keepdims=True)) + dres


def _gate_tile_bwd(dx, branch, m, gate, acc_ref):
    gate_idx, fac = gate
    _acc_add(acc_ref, 3, fac * dx * branch)
    return ((fac * m[gate_idx:gate_idx + 1, :]) * dx).astype(BF16)


_NT = (((1,), (1,)), ((), ()))


def _ffn_chunk(F):
    return _pick(F, (1408, 512, 256, 128))


def _resident():
    return pl.BlockSpec(memory_space=pltpu.VMEM)


def _ffn_tile_fwd(hv, wi_ref, wo_ref, u_ref, s_ref, F, cw):
    acc = jnp.zeros((hv.shape[0], wo_ref.shape[1]), F32)
    for j in range(F // cw):
        a = lax.dot_general(hv, wi_ref[j * cw:(j + 1) * cw, :], _NT, preferred_element_type=F32)
        b = lax.dot_general(hv, wi_ref[F + j * cw:F + (j + 1) * cw, :], _NT, preferred_element_type=F32)
        s = ((a * jax.nn.sigmoid(a)) * b).astype(BF16)
        u_ref[:, j * cw:(j + 1) * cw] = a.astype(BF16)
        u_ref[:, F + j * cw:F + (j + 1) * cw] = b.astype(BF16)
        s_ref[:, j * cw:(j + 1) * cw] = s
        acc = acc + jnp.dot(s, wo_ref[j * cw:(j + 1) * cw, :], preferred_element_type=F32)
    return acc


def _norm_ffn_fwd(xprev, branch, mods, g, gate, shift_idx, scale_idx, w_in_t, w_out, name, head=None):
    x_specs, x_args, (T, D) = _rows_operand(xprev)
    F = w_out.shape[0]
    cw = _ffn_chunk(F)
    has_res = branch is not None
    n_in = len(x_args) + int(has_res) + 4 + (2 if head else 0)

    def body(*refs):
        ins, outs = list(refs[:n_in]), list(refs[n_in:])
        x = _rows_tile([ins.pop(0) for _ in x_args])
        f_ref = ins.pop(0) if has_res else None
        m_ref, g_ref, wi_ref, wo_ref = ins[:4]
        xo_ref = outs.pop(0) if has_res else None
        h_ref, u_ref, s_ref = outs[:3]
        m = m_ref[0]
        if has_res:
            gate_idx, fac = gate
            x = x + (fac * m[gate_idx:gate_idx + 1, :]) * f_ref[...]
            xo_ref[...] = x
        hv = _norm_tile_fwd(x, m, g_ref[...], shift_idx, scale_idx)
        h_ref[...] = hv
        f = _ffn_tile_fwd(hv, wi_ref, wo_ref, u_ref, s_ref, F, cw)
        if head is None:
            outs[3][...] = f
            return
        fg_ref, t_ref = ins[4:6]
        dx_ref, df_ref, acc_ref = outs[3:6]
        _acc_init(acc_ref)
        lat = (pl.program_id(0) > 0).astype(F32)
        gate8 = 0.5 * m[8:9, :]
        x3 = x + gate8 * f
        inv3 = lax.rsqrt(jnp.mean(x3 * x3, axis=-1, keepdims=True) + EPS)
        xn = x3 * inv3
        fg = fg_ref[...]
        e = (xn * fg - t_ref[...]) * lat
        dy = e * (1.0 / D)
        dxn = dy * fg
        dx = inv3 * (dxn - xn * jnp.mean(dxn * xn, axis=-1, keepdims=True))
        dx_ref[...] = dx
        df_ref[...] = (gate8 * dx).astype(BF16)
        _acc_add(acc_ref, 0, (0.5 / D) * e * e)
        _acc_add(acc_ref, 1, dy * xn)
        _acc_add(acc_ref, 2, 0.5 * dx * f)

    in_specs = x_specs + ([_row_spec(D)] if has_res else []) + \
               [_mods_spec(D), _vec_spec(1, D), _resident(), _resident()]
    args = x_args + ([branch] if has_res else []) + [mods, g, w_in_t, w_out]
    out_specs = ([_row_spec(D)] if has_res else []) + [_row_spec(D), _row_spec(2 * F), _row_spec(F)]
    out_shape = ([jax.ShapeDtypeStruct((T, D), F32)] if has_res else []) + \
                [jax.ShapeDtypeStruct((T, D), BF16), jax.ShapeDtypeStruct((T, 2 * F), BF16),
                 jax.ShapeDtypeStruct((T, F), BF16)]
    if head is None:
        out_specs += [_row_spec(D)]
        out_shape += [jax.ShapeDtypeStruct((T, D), F32)]
    else:
        in_specs += [_vec_spec(1, D), pl.BlockSpec((ROW, D), lambda i: (jnp.maximum(i - 1, 0), 0))]
        args += list(head)
        out_specs += [_row_spec(D), _row_spec(D), _acc_spec(D)]
        out_shape += [jax.ShapeDtypeStruct((T, D), F32), jax.ShapeDtypeStruct((T, D), BF16),
                      jax.ShapeDtypeStruct((2, ACC_ROWS, D), F32)]
    out = pl.pallas_call(
        body, name=name, grid=(T // ROW,), in_specs=in_specs, out_specs=out_specs, out_shape=out_shape,
        compiler_params=_params(("arbitrary",) if head else ("parallel",)),
    )(*args)
    return tuple(out) if has_res else (None,) + tuple(out)


def _ffn_norm_bwd(df, u, w_in_t, w_out, x, dres, mods, g, shift_idx, scale_idx, gate, branch, name,
                  skip_first_tile=False):
    T, D = df.shape
    F = w_out.shape[0]
    cw = _ffn_chunk(F)
    nt = T // ROW
    has_gate = gate is not None
    x_specs, x_args, _ = _rows_operand(x)
    n_in = 7 + len(x_args) + int(has_gate)

    def body(*refs):
        ins, outs = list(refs[:n_in]), list(refs[n_in:])
        df_ref, u_ref, wi_ref, wo_ref = ins[:4]
        x_refs = ins[4:4 + len(x_args)]
        dr_ref = ins[4 + len(x_args)]
        b_ref = ins[5 + len(x_args)] if has_gate else None
        m_ref, g_ref = ins[-2:]
        du_ref, dx_ref = outs[:2]
        db_ref = outs[2] if has_gate else None
        acc_ref = outs[-1]
        _acc_init(acc_ref)
        dfv = df_ref[...]
        dh = jnp.zeros((ROW, D), F32)
        for j in range(F // cw):
            ds = lax.dot_general(dfv, wo_ref[j * cw:(j + 1) * cw, :], _NT, preferred_element_type=F32)
            a = u_ref[:, j * cw:(j + 1) * cw].astype(F32)
            b = u_ref[:, F + j * cw:F + (j + 1) * cw].astype(F32)
            sig = jax.nn.sigmoid(a)
            da = (ds * b * (sig * (1.0 + a * (1.0 - sig)))).astype(BF16)
            db = (ds * (a * sig)).astype(BF16)
            du_ref[:, j * cw:(j + 1) * cw] = da
            du_ref[:, F + j * cw:F + (j + 1) * cw] = db
            dh = dh + jnp.dot(da, wi_ref[j * cw:(j + 1) * cw, :], preferred_element_type=F32)
            dh = dh + jnp.dot(db, wi_ref[F + j * cw:F + (j + 1) * cw, :], preferred_element_type=F32)
        m = m_ref[0]
        dx = _norm_tile_bwd(_rows_tile(x_refs), dh, dr_ref[...], m, g_ref[...], shift_idx, scale_idx, acc_ref)
        dx_ref[...] = dx
        if has_gate:
            db_ref[...] = _gate_tile_bwd(dx, b_ref[...], m, gate, acc_ref)

    in_specs = [_row_spec(D), _row_spec(2 * F), _resident(), _resident()] + x_specs + [_row_spec(D)] + \
               ([_row_spec(D)] if has_gate else []) + [_mods_spec(D), _vec_spec(1, D)]
    args = [df, u, w_in_t, w_out] + x_args + [dres] + ([branch] if has_gate else []) + [mods, g]
    if skip_first_tile:
        dx_spec = pl.BlockSpec((ROW, D), lambda i: (jnp.maximum(i - 1, 0), 0))
        dx_shape = jax.ShapeDtypeStruct((T - ROW, D), F32)
    else:
        dx_spec = _row_spec(D)
        dx_shape = jax.ShapeDtypeStruct((T, D), F32)
    out_specs = [_row_spec(2 * F), dx_spec] + ([_row_spec(D)] if has_gate else []) + [_acc_spec(D)]
    out_shape = [jax.ShapeDtypeStruct((T, 2 * F), BF16), dx_shape] + \
                ([jax.ShapeDtypeStruct((T, D), BF16)] if has_gate else []) + \
                [jax.ShapeDtypeStruct((2, ACC_ROWS, D), F32)]
    out = pl.pallas_call(
        body, name=name, grid=(nt,), in_specs=in_specs, out_specs=out_specs, out_shape=out_shape,
        compiler_params=_params(("arbitrary",)),
    )(*args)
    if has_gate:
        return tuple(out)
    return out[0], out[1], None, out[2]


def _halo_specs(width, col, nt):
    per = ROW // HALO
    prev = pl.BlockSpec((HALO, width), lambda i, col=col: (jnp.maximum(i * per - 1, 0), col))
    nxt = pl.BlockSpec((HALO, width), lambda i, col=col: (jnp.minimum((i + 1) * per, nt * per - 1), col))
    return prev, nxt


def _f32(ref):
    return ref[...].astype(F32)


def _last_row(halo_ref):
    return halo_ref[HALO - 1:HALO, :].astype(F32)


def _first_row(halo_ref):
    return halo_ref[0:1, :].astype(F32)


def _shift_rows(v, prev_row, next_row):
    rows = lax.broadcasted_iota(jnp.int32, v.shape, 0)
    down = jnp.where(rows == 0, prev_row, pltpu.roll(v, 1, 0))
    up = jnp.where(rows == v.shape[0] - 1, next_row, pltpu.roll(v, v.shape[0] - 1, 0))
    return down, up


def _conv_fwd(P, conv_w, D, name):
    T = P.shape[0]
    nt = T // ROW
    cg_p, cg_n = _halo_specs(D, 1, nt)
    vc_p, vc_n = _halo_specs(D, 2, nt)

    def body(bg_ref, cg_ref, vc_ref, cgp_ref, vcp_ref, cgn_ref, vcn_ref, w_ref, y_ref):
        i = pl.program_id(0)
        has_prev = (i != 1).astype(F32)
        has_next = (i != nt - 1).astype(F32)
        u = _f32(cg_ref) * _f32(vc_ref)
        up_row = _last_row(cgp_ref) * _last_row(vcp_ref) * has_prev
        un_row = _first_row(cgn_ref) * _first_row(vcn_ref) * has_next
        um1, up1 = _shift_rows(u, up_row, un_row)
        w = w_ref[...]
        conv = um1 * w[0:1, :] + u * w[1:2, :] + up1 * w[2:3, :]
        y_ref[...] = (_f32(bg_ref) * conv).astype(BF16)

    return pl.pallas_call(
        body, name=name, grid=(nt,),
        in_specs=[_row_spec(D, 0), _row_spec(D, 1), _row_spec(D, 2), cg_p, vc_p, cg_n, vc_n, _vec_spec(3, D)],
        out_specs=_row_spec(D),
        out_shape=jax.ShapeDtypeStruct((T, D), BF16),
        compiler_params=_params(("parallel",)),
    )(P, P, P, P, P, P, P, conv_w)


def _conv_bwd_operands(P, dy, conv_w, D):
    nt = P.shape[0] // ROW
    bg_p, bg_n = _halo_specs(D, 0, nt)
    cg_p, cg_n = _halo_specs(D, 1, nt)
    vc_p, vc_n = _halo_specs(D, 2, nt)
    dy_p, dy_n = _halo_specs(D, 0, nt)
    specs = [_row_spec(D, 0), _row_spec(D, 1), _row_spec(D, 2), _row_spec(D, 0),
             bg_p, cg_p, vc_p, dy_p, bg_n, cg_n, vc_n, dy_n, _vec_spec(3, D)]
    return specs, [P, P, P, dy, P, P, P, dy, P, P, P, dy, conv_w]


def _conv_tile_bwd(refs, o_ref, acc_ref, D, nt):
    (bg_ref, cg_ref, vc_ref, dy_ref, bgp_ref, cgp_ref, vcp_ref, dyp_ref,
     bgn_ref, cgn_ref, vcn_ref, dyn_ref, w_ref) = refs
    i = pl.program_id(0)
    lat = (i > 0).astype(F32)
    has_prev = (i != 1).astype(F32)
    has_next = (i != nt - 1).astype(F32)
    bg = _f32(bg_ref)
    cg = _f32(cg_ref)
    vc = _f32(vc_ref)
    dyv = dy_ref[...] * lat
    u = cg * vc
    up_row = _last_row(cgp_ref) * _last_row(vcp_ref) * has_prev
    un_row = _first_row(cgn_ref) * _first_row(vcn_ref) * has_next
    um1, up1 = _shift_rows(u, up_row, un_row)
    w = w_ref[...]
    conv = um1 * w[0:1, :] + u * w[1:2, :] + up1 * w[2:3, :]
    dc = dyv * bg
    dcp_row = _last_row(dyp_ref) * _last_row(bgp_ref) * has_prev
    dcn_row = _first_row(dyn_ref) * _first_row(bgn_ref) * has_next
    dcm1, dcp1 = _shift_rows(dc, dcp_row, dcn_row)
    du = dcp1 * w[0:1, :] + dc * w[1:2, :] + dcm1 * w[2:3, :]
    o_ref[:, 0:D] = (dyv * conv).astype(BF16)
    o_ref[:, D:2 * D] = (du * vc * lat).astype(BF16)
    o_ref[:, 2 * D:3 * D] = (du * cg * lat).astype(BF16)
    _acc_add(acc_ref, 0, dc * um1)
    _acc_add(acc_ref, 1, dc * u)
    _acc_add(acc_ref, 2, dc * up1)


def _rope_tables(ctx_len, seq):
    n_freq = HEAD_DIM // 4
    rows = seq // GRID_W
    inv = ROPE_THETA ** (-jnp.arange(n_freq, dtype=F32) / n_freq)
    ar = jnp.arange(rows, dtype=F32)[:, None] * inv
    ac = jnp.arange(GRID_W, dtype=F32)[:, None] * inv

    def per_row(a):
        return jnp.repeat(a, GRID_W, axis=0)

    def per_col(a):
        return jnp.tile(a, (rows, 1))

    cos_t = jnp.concatenate([per_row(jnp.cos(ar)), per_row(jnp.cos(ar)), per_col(jnp.cos(ac)), per_col(jnp.cos(ac))], axis=1)
    sin_t = jnp.concatenate([per_row(-jnp.sin(ar)), per_row(jnp.sin(ar)), per_col(-jnp.sin(ac)), per_col(jnp.sin(ac))], axis=1)
    cos_t = jnp.concatenate([jnp.ones((ctx_len, HEAD_DIM), F32), cos_t], axis=0)
    sin_t = jnp.concatenate([jnp.zeros((ctx_len, HEAD_DIM), F32), sin_t], axis=0)
    return cos_t, sin_t


def _swap_halves(y):
    lanes = lax.broadcasted_iota(jnp.int32, y.shape, 1)
    first = (lanes % 64) < 32
    return jnp.where(first, pltpu.roll(y, HEAD_DIM - 32, 1), pltpu.roll(y, 32, 1))


def _to_row(col, n):
    return jnp.transpose(jnp.broadcast_to(col, (n, HEAD_DIM)))[0:1, :]


LOG2E = 1.4426950408889634
ATTN_PART_LANES = 256
ATTN_QUERY_ROWS = 768
ATTN_VMEM_LIMIT = 60 * 1024 * 1024


def _flash_fwd(q, k, v, name, tq=None, tk=None):
    T = q.shape[0]
    tq = tq or _pick(T, (ATTN_QUERY_ROWS, ROW))
    parts = GROUP * tq // ATTN_PART_LANES
    tk = tk or _pick(T, (2816, 1408, 768, 512, 256))
    ck = tk
    nk = T // tk
    GW = GROUP * HEAD_DIM

    def body(q_ref, k_ref, v_ref, o_ref, lse_ref, qs_ref, m_ref, l_ref, acc_ref, st_ref):
        ki = pl.program_id(2)

        @pl.when(ki == 0)
        def _():
            for g in range(GROUP):
                qs_ref[g * tq:(g + 1) * tq, :] = q_ref[:, g * HEAD_DIM:(g + 1) * HEAD_DIM]
            m_ref[...] = jnp.full(m_ref.shape, -jnp.inf, F32)
            l_ref[...] = jnp.zeros(l_ref.shape, F32)
            acc_ref[...] = jnp.zeros(acc_ref.shape, F32)

        w = ATTN_PART_LANES
        nck = tk // ck

        def lanes(p):
            return slice(p * w, (p + 1) * w)

        def keys(c):
            return slice(c * ck, (c + 1) * ck)

        def fold(a):
            return a.reshape(ck // 8, 8, w)

        def scores(p, c):
            st = lax.dot_general(k_ref[keys(c), :], qs_ref[lanes(p), :], _NT,
                                 preferred_element_type=F32) * (ATTN_SCALE * LOG2E)
            st_ref[keys(c), lanes(p)] = st
            return jnp.max(fold(st), axis=0)

        def new_max(p, partial):
            m_prev = m_ref[:, lanes(p)]
            m_new = jnp.maximum(m_prev, jnp.max(functools.reduce(jnp.maximum, partial), axis=0, keepdims=True))
            m_ref[:, lanes(p)] = m_new
            return m_new, jnp.exp2(m_prev - m_new)

        def weights(p, c, m_new):
            pt = jnp.exp2(st_ref[keys(c), lanes(p)] - m_new)
            pv = lax.dot_general(v_ref[keys(c), :], pt.astype(BF16), (((0,), (0,)), ((), ())),
                                 preferred_element_type=F32)
            return jnp.sum(fold(pt), axis=0), pv

        partial = [scores(0, c) for c in range(nck)]
        for p in range(parts):
            m_new, alpha = new_max(p, partial)
            partial, sums, pvs = [], [], []
            for c in range(nck):
                if p + 1 < parts:
                    partial.append(scores(p + 1, c))
                s8, pv = weights(p, c, m_new)
                sums.append(s8)
                pvs.append(pv)
            l_ref[:, lanes(p)] = alpha * l_ref[:, lanes(p)] + jnp.sum(sum(sums), axis=0, keepdims=True)
            acc_ref[:, lanes(p)] = alpha * acc_ref[:, lanes(p)] + sum(pvs)

        @pl.when(ki == nk - 1)
        def _():
            out = jnp.transpose(acc_ref[...] / l_ref[...])
            lse = m_ref[...] + jnp.log2(l_ref[...])
            for g in range(GROUP):
                o_ref[:, g * HEAD_DIM:(g + 1) * HEAD_DIM] = out[g * tq:(g + 1) * tq, :]
                lse_ref[0, g:g + 1, :] = lse[:, g * tq:(g + 1) * tq]

    return pl.pallas_call(
        body, name=name, grid=(N_KV_HEADS, T // tq, nk),
        in_specs=[pl.BlockSpec((tq, GW), lambda h, i, j: (i, h)),
                  pl.BlockSpec((tk, HEAD_DIM), lambda h, i, j: (j, h)),
                  pl.BlockSpec((tk, HEAD_DIM), lambda h, i, j: (j, h))],
        out_specs=[pl.BlockSpec((tq, GW), lambda h, i, j: (i, h)),
                   pl.BlockSpec((1, GROUP, tq), lambda h, i, j: (h, 0, i))],
        out_shape=[jax.ShapeDtypeStruct((T, N_Q_HEADS * HEAD_DIM), F32),
                   jax.ShapeDtypeStruct((N_KV_HEADS, GROUP, T), F32)],
        scratch_shapes=[pltpu.VMEM((GROUP * tq, HEAD_DIM), BF16), pltpu.VMEM((1, GROUP * tq), F32),
                        pltpu.VMEM((1, GROUP * tq), F32), pltpu.VMEM((HEAD_DIM, GROUP * tq), F32),
                        pltpu.VMEM((tk, GROUP * tq), F32)],
        compiler_params=pltpu.CompilerParams(dimension_semantics=("parallel", "parallel", "arbitrary"),
                                             vmem_limit_bytes=ATTN_VMEM_LIMIT),
    )(q, k, v)


def _flash_bwd(q, k, v, do, lse, delta, name, tq=None, tk=None, token=None):
    T = q.shape[0]
    tq = tq or _pick(T, (ATTN_QUERY_ROWS, ROW))
    tk = tk or _pick(T, (1408, 768, 512, 256))
    nk = T // tk
    GW = GROUP * HEAD_DIM
    nt = (((1,), (1,)), ((), ()))
    extra = [] if token is None else [token]

    def body(q_ref, do_ref, k_ref, v_ref, lse_ref, dl_ref, *rest):
        dq_ref, dk_ref, dv_ref, qs_ref, dos_ref, dqt_ref = rest[len(extra):]
        qi = pl.program_id(1)
        ki = pl.program_id(2)

        @pl.when(ki == 0)
        def _():
            for g in range(GROUP):
                qs_ref[g * tq:(g + 1) * tq, :] = q_ref[:, g * HEAD_DIM:(g + 1) * HEAD_DIM]
                dos_ref[g * tq:(g + 1) * tq, :] = do_ref[:, g * HEAD_DIM:(g + 1) * HEAD_DIM]
            dqt_ref[...] = jnp.zeros(dqt_ref.shape, F32)

        kk = k_ref[...]
        vv = v_ref[...]

        def lanes(p):
            return slice(p * tq, (p + 1) * tq)

        def products(p):
            st = lax.dot_general(kk, qs_ref[lanes(p), :], nt, preferred_element_type=F32)
            dpt = lax.dot_general(vv, dos_ref[lanes(p), :], nt, preferred_element_type=F32)
            return st, dpt

        dk_c = jnp.zeros((tk, HEAD_DIM), F32)
        dv_c = jnp.zeros((tk, HEAD_DIM), F32)
        ahead = products(0)
        for p in range(GROUP):
            st, dpt = ahead
            if p + 1 < GROUP:
                ahead = products(p + 1)
            pt = jnp.exp2(st * (ATTN_SCALE * LOG2E) - lse_ref[0, p:p + 1, :])
            dst = ((pt * (dpt - dl_ref[0, p:p + 1, :])) * ATTN_SCALE).astype(BF16)
            dv_c = dv_c + jnp.dot(pt.astype(BF16), dos_ref[lanes(p), :], preferred_element_type=F32)
            dk_c = dk_c + jnp.dot(dst, qs_ref[lanes(p), :], preferred_element_type=F32)
            dqt_ref[:, lanes(p)] += lax.dot_general(kk, dst, (((0,), (0,)), ((), ())), preferred_element_type=F32)
        rows = pl.ds(pl.multiple_of(ki * tk, tk), tk)

        @pl.when(qi == 0)
        def _():
            dk_ref[rows, :] = dk_c
            dv_ref[rows, :] = dv_c

        @pl.when(qi > 0)
        def _():
            dk_ref[rows, :] += dk_c
            dv_ref[rows, :] += dv_c

        @pl.when(ki == nk - 1)
        def _():
            dqv = jnp.transpose(dqt_ref[...])
            for g in range(GROUP):
                dq_ref[:, g * HEAD_DIM:(g + 1) * HEAD_DIM] = dqv[g * tq:(g + 1) * tq, :]

    return pl.pallas_call(
        body, name=name, grid=(N_KV_HEADS, T // tq, nk),
        in_specs=[pl.BlockSpec((tq, GW), lambda h, i, j: (i, h)),
                  pl.BlockSpec((tq, GW), lambda h, i, j: (i, h)),
                  pl.BlockSpec((tk, HEAD_DIM), lambda h, i, j: (j, h)),
                  pl.BlockSpec((tk, HEAD_DIM), lambda h, i, j: (j, h)),
                  pl.BlockSpec((1, GROUP, tq), lambda h, i, j: (h, 0, i)),
                  pl.BlockSpec((1, GROUP, tq), lambda h, i, j: (h, 0, i))] +
                 [pl.BlockSpec(t.shape, lambda h, i, j: (0, 0)) for t in extra],
        out_specs=[pl.BlockSpec((tq, GW), lambda h, i, j: (i, h)),
                   pl.BlockSpec((T, HEAD_DIM), lambda h, i, j: (0, h)),
                   pl.BlockSpec((T, HEAD_DIM), lambda h, i, j: (0, h))],
        out_shape=[jax.ShapeDtypeStruct((T, N_Q_HEADS * HEAD_DIM), F32),
                   jax.ShapeDtypeStruct((T, N_KV_HEADS * HEAD_DIM), F32),
                   jax.ShapeDtypeStruct((T, N_KV_HEADS * HEAD_DIM), F32)],
        scratch_shapes=[pltpu.VMEM((GROUP * tq, HEAD_DIM), BF16), pltpu.VMEM((GROUP * tq, HEAD_DIM), BF16),
                        pltpu.VMEM((HEAD_DIM, GROUP * tq), F32)],
        compiler_params=pltpu.CompilerParams(dimension_semantics=("arbitrary", "arbitrary", "arbitrary"),
                                             vmem_limit_bytes=ATTN_VMEM_LIMIT),
    )(q, do, k, v, lse, delta, *extra)


def _gate_specs(D):
    w = D // 2
    first = (3 * D + (N_Q_HEADS + 2 * N_KV_HEADS) * HEAD_DIM) // w
    return [pl.BlockSpec((ROW, w), lambda i, c=first + j: (i, c)) for j in range(4)]


def _merge_fwd(yc, o, P, wbc, wba, wo, D, name):
    T = yc.shape[0]
    w = D // 2

    def body(yc_ref, o_ref, g0, g1, g2, g3, wbc_ref, wba_ref, wo_ref, a1_ref, a2_ref, z_ref, mo_ref):
        a1 = jnp.dot(yc_ref[...], wbc_ref[...], preferred_element_type=F32)
        a2 = jnp.dot(o_ref[...].astype(BF16), wba_ref[...], preferred_element_type=F32)
        a1_ref[...] = a1
        a2_ref[...] = a2
        for j, (gc, ga) in enumerate(((g0, g2), (g1, g3))):
            sl = slice(j * w, (j + 1) * w)
            z = jax.nn.sigmoid(_f32(gc)) * a1[:, sl] + jax.nn.sigmoid(_f32(ga)) * a2[:, sl]
            z_ref[:, sl] = z.astype(BF16)
        mo_ref[...] = jnp.dot(z_ref[...], wo_ref[...], preferred_element_type=F32)

    return pl.pallas_call(
        body, name=name, grid=(T // ROW,),
        in_specs=[_row_spec(D), _row_spec(D)] + _gate_specs(D) + [_resident()] * 3,
        out_specs=[_row_spec(D)] * 4,
        out_shape=[jax.ShapeDtypeStruct((T, D), F32), jax.ShapeDtypeStruct((T, D), F32),
                   jax.ShapeDtypeStruct((T, D), BF16), jax.ShapeDtypeStruct((T, D), F32)],
        compiler_params=_params(("parallel",)),
    )(yc, o, P, P, P, P, wbc, wba, wo)


def _merge_bwd(dmo, a1, a2, o, P, wbc, wba, wo, D, name):
    T = a1.shape[0]
    w = D // 2

    def body(dmo_ref, a1_ref, a2_ref, o_ref, g0, g1, g2, g3, wbc_ref, wba_ref, wo_ref,
             d1_ref, d2_ref, dg_ref, dyc_ref, dob_ref, dl_ref):
        dz = lax.dot_general(dmo_ref[...], wo_ref[...], _NT, preferred_element_type=F32)
        for j, (gc, ga) in enumerate(((g0, g2), (g1, g3))):
            sl = slice(j * w, (j + 1) * w)
            dzs = dz[:, sl]
            sc = jax.nn.sigmoid(_f32(gc))
            sa = jax.nn.sigmoid(_f32(ga))
            d1_ref[:, sl] = (dzs * sc).astype(BF16)
            d2_ref[:, sl] = (dzs * sa).astype(BF16)
            dg_ref[:, j * w:(j + 1) * w] = (dzs * a1_ref[:, sl] * (sc * (1.0 - sc))).astype(BF16)
            dg_ref[:, D + j * w:D + (j + 1) * w] = (dzs * a2_ref[:, sl] * (sa * (1.0 - sa))).astype(BF16)
        dyc_ref[...] = lax.dot_general(d1_ref[...], wbc_ref[...], _NT, preferred_element_type=F32)
        dov = lax.dot_general(d2_ref[...], wba_ref[...], _NT, preferred_element_type=F32)
        dob_ref[...] = dov.astype(BF16)
        prod = dov * o_ref[...]
        for h in range(N_Q_HEADS):
            d = jnp.sum(prod[:, h * HEAD_DIM:(h + 1) * HEAD_DIM], axis=1, keepdims=True)
            dl_ref[h // GROUP, (h % GROUP):(h % GROUP) + 1, :] = _to_row(d, ROW)

    return pl.pallas_call(
        body, name=name, grid=(T // ROW,),
        in_specs=[_row_spec(D)] * 4 + _gate_specs(D) + [_resident()] * 3,
        out_specs=[_row_spec(D), _row_spec(D), _row_spec(2 * D), _row_spec(D), _row_spec(D),
                   pl.BlockSpec((N_KV_HEADS, GROUP, ROW), lambda i: (0, 0, i))],
        out_shape=[jax.ShapeDtypeStruct((T, D), BF16), jax.ShapeDtypeStruct((T, D), BF16),
                   jax.ShapeDtypeStruct((T, 2 * D), BF16), jax.ShapeDtypeStruct((T, D), F32),
                   jax.ShapeDtypeStruct((T, D), BF16), jax.ShapeDtypeStruct((N_KV_HEADS, GROUP, T), F32)],
        compiler_params=_params(("parallel",)),
    )(dmo, a1, a2, o, P, P, P, P, wbc, wba, wo)


def _adamw_math(w, g, m, v):
    m = ADAM_B1 * m + (1.0 - ADAM_B1) * g
    v = ADAM_B2 * v + (1.0 - ADAM_B2) * (g * g)
    m_hat = m / (1.0 - ADAM_B1 ** ADAM_STEP)
    v_hat = v / (1.0 - ADAM_B2 ** ADAM_STEP)
    delta = -ADAM_LR * (m_hat / (jnp.sqrt(v_hat) + ADAM_EPS) + ADAM_WD * w)
    return delta, m, v


def _adamw(w, g, m, v, name):
    R, C = w.shape
    tr = _pick(R, tuple(t for t in (256, 128, 64, 32, 16, 8) if t * C * 4 <= ADAMW_BLOCK_BYTES))

    def body(w_ref, g_ref, m_ref, v_ref, d_ref, mo_ref, vo_ref):
        d, mn, vn = _adamw_math(w_ref[...], g_ref[...], m_ref[...], v_ref[...])
        d_ref[...] = d
        mo_ref[...] = mn
        vo_ref[...] = vn

    spec = pl.BlockSpec((tr, C), lambda i: (i, 0))
    return pl.pallas_call(
        body, name=name, grid=(R // tr,),
        in_specs=[spec] * 4, out_specs=[spec] * 3,
        out_shape=[jax.ShapeDtypeStruct((R, C), F32)] * 3,
        compiler_params=_params(("parallel",)),
    )(w, g, m, v)


def _norm_mix_in_fwd(xprev, branch, mods, g, gate, shift_idx, scale_idx, w_t, gq, gk, cos_t, sin_t, name):
    x_specs, x_args, (T, D) = _rows_operand(xprev)
    N = w_t.shape[0]
    QW = N_Q_HEADS * HEAD_DIM
    KW = N_KV_HEADS * HEAD_DIM
    q0, k0, v0 = 3 * D, 3 * D + QW, 3 * D + QW + KW
    edges = [0, D, 2 * D, q0, k0, v0 + KW] + list(range(v0 + KW + D, N + 1, D))
    assert edges[-1] == N

    def body(*refs):
        f_ref, m_ref, g_ref, w_ref, gq_ref, gk_ref, c_ref, s_ref = refs[len(x_args):len(x_args) + 8]
        xo_ref, h_ref, p_ref, qo_ref, ko_ref, vo_ref = refs[len(x_args) + 8:]
        m = m_ref[0]
        gate_idx, fac = gate
        x = _rows_tile(refs[:len(x_args)]) + (fac * m[gate_idx:gate_idx + 1, :]) * f_ref[...]
        xo_ref[...] = x
        hv = _norm_tile_fwd(x, m, g_ref[...], shift_idx, scale_idx)
        h_ref[...] = hv
        c = c_ref[...]
        s = s_ref[...]

        def head(xh, gain):
            inv = lax.rsqrt(jnp.mean(xh * xh, axis=-1, keepdims=True) + EPS)
            y = (xh * inv) * gain
            return y * c + _swap_halves(y) * s

        for lo, hi in zip(edges[:-1], edges[1:]):
            pb = lax.dot_general(hv, w_ref[lo:hi, :], _NT, preferred_element_type=F32).astype(BF16)
            p_ref[:, lo:hi] = pb
            if lo == q0:
                for h in range(N_Q_HEADS):
                    sl = slice(h * HEAD_DIM, (h + 1) * HEAD_DIM)
                    qo_ref[:, sl] = head(pb[:, sl].astype(F32), gq_ref[...]).astype(BF16)
            elif lo == k0:
                for h in range(N_KV_HEADS):
                    sl = slice(h * HEAD_DIM, (h + 1) * HEAD_DIM)
                    ko_ref[:, sl] = head(pb[:, sl].astype(F32), gk_ref[...]).astype(BF16)
                vo_ref[...] = pb[:, KW:2 * KW]

    return pl.pallas_call(
        body, name=name, grid=(T // ROW,),
        in_specs=x_specs + [_row_spec(D), _mods_spec(D), _vec_spec(1, D), _resident(),
                            _vec_spec(1, HEAD_DIM), _vec_spec(1, HEAD_DIM), _row_spec(HEAD_DIM), _row_spec(HEAD_DIM)],
        out_specs=[_row_spec(D), _row_spec(D), _row_spec(N), _row_spec(QW), _row_spec(KW), _row_spec(KW)],
        out_shape=[jax.ShapeDtypeStruct((T, D), F32), jax.ShapeDtypeStruct((T, D), BF16),
                   jax.ShapeDtypeStruct((T, N), BF16), jax.ShapeDtypeStruct((T, QW), BF16),
                   jax.ShapeDtypeStruct((T, KW), BF16), jax.ShapeDtypeStruct((T, KW), BF16)],
        compiler_params=_params(("parallel",)),
    )(*x_args, branch, mods, g, w_t, gq, gk, cos_t, sin_t)


def _mix_in_norm_bwd(dyc, dgt, P, conv_w, dq, dk, dv, gq, gk, cos_t, sin_t, w_t, x, dres, mods, g, shift_idx,
                     scale_idx, gate, branch, name):
    T, D = x.shape
    nt = T // ROW
    QW = N_Q_HEADS * HEAD_DIM
    KW = N_KV_HEADS * HEAD_DIM
    q0, g0 = 3 * D, 3 * D + QW + 2 * KW
    assert g0 + dgt.shape[1] == w_t.shape[0]
    conv_specs, conv_args = _conv_bwd_operands(P, dyc, conv_w, D)
    nc = len(conv_args)

    def body(*refs):
        (dg_ref, q_ref, k_ref, dq_ref, dk_ref, dv_ref, gq_ref, gk_ref, c_ref, s_ref,
         w_ref, x_ref, dr_ref, b_ref, m_ref, g_ref,
         dx_ref, db_ref, acc_ref, dc_ref, cacc_ref, o_ref, qacc_ref) = refs[nc:]
        _acc_init(acc_ref)
        _acc_init(cacc_ref)
        _acc_init(qacc_ref)
        _conv_tile_bwd(refs[:nc], dc_ref, cacc_ref, D, nt)
        dh = jnp.dot(dc_ref[...], w_ref[0:q0, :], preferred_element_type=F32)
        c = c_ref[...]
        s = s_ref[...]

        def head(xh, d, gain):
            dyv = d * c + _swap_halves(d * s)
            inv = lax.rsqrt(jnp.mean(xh * xh, axis=-1, keepdims=True) + EPS)
            xn = xh * inv
            dxn = dyv * gain
            dxh = inv * (dxn - xn * jnp.mean(dxn * xn, axis=-1, keepdims=True))
            return dxh, jnp.sum(dyv * xn, axis=0, keepdims=True)

        dgq = jnp.zeros((1, HEAD_DIM), F32)
        for h in range(N_Q_HEADS):
            sl = slice(h * HEAD_DIM, (h + 1) * HEAD_DIM)
            dxh, dgh = head(q_ref[:, sl].astype(F32), dq_ref[:, sl], gq_ref[...])
            o_ref[:, sl] = dxh.astype(BF16)
            dgq = dgq + dgh
        dh = dh + jnp.dot(dg_ref[...], w_ref[g0:, :], preferred_element_type=F32)
        dgk = jnp.zeros((1, HEAD_DIM), F32)
        for h in range(N_KV_HEADS):
            sl = slice(h * HEAD_DIM, (h + 1) * HEAD_DIM)
            dxh, dgh = head(k_ref[:, sl].astype(F32), dk_ref[:, sl], gk_ref[...])
            o_ref[:, QW + h * HEAD_DIM:QW + (h + 1) * HEAD_DIM] = dxh.astype(BF16)
            dgk = dgk + dgh
        o_ref[:, QW + KW:QW + 2 * KW] = dv_ref[...].astype(BF16)
        qacc_ref[0, 0:1, 0:HEAD_DIM] += dgq
        qacc_ref[0, 1:2, 0:HEAD_DIM] += dgk
        dh = dh + jnp.dot(o_ref[...], w_ref[q0:g0, :], preferred_element_type=F32)
        m = m_ref[0]
        dx = _norm_tile_bwd(x_ref[...], dh, dr_ref[...], m, g_ref[...], shift_idx, scale_idx, acc_ref)
        dx_ref[...] = dx
        db_ref[...] = _gate_tile_bwd(dx, b_ref[...], m, gate, acc_ref)

    return pl.pallas_call(
        body, name=name, grid=(T // ROW,),
        in_specs=conv_specs +
                 [_row_spec(dgt.shape[1]), _row_spec(QW, q0 // QW), _row_spec(KW, (q0 + QW) // KW),
                  _row_spec(QW), _row_spec(KW), _row_spec(KW), _vec_spec(1, HEAD_DIM), _vec_spec(1, HEAD_DIM),
                  _row_spec(HEAD_DIM), _row_spec(HEAD_DIM),
                  _resident(), _row_spec(D), _row_spec(D), _row_spec(D), _mods_spec(D), _vec_spec(1, D)],
        out_specs=[_row_spec(D), _row_spec(D), _acc_spec(D), _row_spec(q0), _acc_spec(D),
                   _row_spec(QW + 2 * KW), _acc_spec(D)],
        out_shape=[jax.ShapeDtypeStruct((T, D), F32), jax.ShapeDtypeStruct((T, D), BF16),
                   jax.ShapeDtypeStruct((2, ACC_ROWS, D), F32), jax.ShapeDtypeStruct((T, q0), BF16),
                   jax.ShapeDtypeStruct((2, ACC_ROWS, D), F32), jax.ShapeDtypeStruct((T, QW + 2 * KW), BF16),
                   jax.ShapeDtypeStruct((2, ACC_ROWS, D), F32)],
        compiler_params=_params(("arbitrary",)),
    )(*conv_args, dgt, P, P, dq, dk, dv, gq, gk, cos_t, sin_t, w_t, x, dres, branch, mods, g)


def _adamw_transposed(w, gt, m, v, name):
    R, C = w.shape
    tc = 128

    def body(w_ref, g_ref, m_ref, v_ref, go_ref, d_ref, mo_ref, vo_ref):
        g = jnp.transpose(g_ref[...])
        d, mn, vn = _adamw_math(w_ref[...], g, m_ref[...], v_ref[...])
        go_ref[...] = g
        d_ref[...] = d
        mo_ref[...] = mn
        vo_ref[...] = vn

    spec = pl.BlockSpec((R, tc), lambda j: (0, j))
    return pl.pallas_call(
        body, name=name, grid=(C // tc,),
        in_specs=[spec, pl.BlockSpec((tc, R), lambda j: (j, 0)), spec, spec], out_specs=[spec] * 4,
        out_shape=[jax.ShapeDtypeStruct((R, C), F32)] * 4,
        compiler_params=_params(("parallel",)),
    )(w, gt, m, v)


class _NoExchange:
    def __init__(self, rest):
        self.rest = rest

    def rest_weights(self, after):
        return self.rest

    def reduce_early(self, grads, tag):
        return None


def _local_step(xcat, target, mods, norm_g, final_g, gq, gk, conv_w, ffn1_w, hooks, ctx_len):
    T, D = _rows_operand(xcat)[2]
    w1i, w1o = ffn1_w
    g1, g2, g3 = norm_g
    cos_t, sin_t = _rope_tables(ctx_len, T - ctx_len)

    def after(value, token, name):
        return value if token is None else _after(value, token, name)

    _, h1, u1, s1, f1 = _norm_ffn_fwd(xcat, None, mods, g1, None, 0, 1, w1i, w1o, "f_ffn1")
    wi, wbc, wba, wo, w2i, w2o = hooks.rest_weights(f1)
    x1, h2, P, qn, kn, vb = _norm_mix_in_fwd(xcat, f1, mods, g2, (2, 0.5), 3, 4, wi, gq, gk, cos_t, sin_t, "f_mix_in")
    yc = _conv_fwd(P, conv_w, D, "f_conv")
    o, lse = _flash_fwd(qn, kn, vb, "f_attn")
    a1, a2, z, mo = _merge_fwd(yc, o, P, wbc, wba, wo, D, "f_merge")
    x2, h3, u2, s2, dx3, df2, acc_head = _norm_ffn_fwd(x1, mo, mods, g3, (5, 1.0), 6, 7, w2i, w2o, "f_ffn2",
                                                       head=(final_g, target))

    du2, dx2, dmo, acc_n3 = _ffn_norm_bwd(df2, u2, w2i, w2o, x2, dx3, mods, g3, 6, 7, (5, 1.0), mo, "b_ffn2")
    g_w2o = _matmul(s2, df2, "tn", BF16, "b_ffn2_out_dw")
    g_w2i = _matmul(du2, h3, "tn", BF16, "b_ffn2_in_dw")

    g_wo = _matmul(z, dmo, "tn", BF16, "b_mix_out_dw")
    da1, da2, dgt, dyc, dob, delta = _merge_bwd(dmo, a1, a2, o, P, wbc, wba, wo, D, "b_merge")
    g_wbc = _matmul(yc, da1, "tn", BF16, "b_branch_conv_dw")
    g_wba = _matmul(o, da2, "tn", BF16, "b_branch_attn_dw")
    token_a = hooks.reduce_early([g_wbc, g_wba, g_wo, g_w2i, g_w2o], "a")
    dq, dk, dv = _flash_bwd(qn, kn, vb, dob, lse, delta, "b_attn", token=token_a)
    dx1, df1, acc_n2, dconv, acc_conv, dqkv, acc_qk = _mix_in_norm_bwd(
        dyc, dgt, P, conv_w, dq, dk, dv, gq, gk, cos_t, sin_t, wi, x1, dx2, mods, g2, 3, 4, (2, 0.5), f1, "b_mix_in")
    d_parts = (dconv, dqkv, dgt)
    g_wi = jnp.concatenate([_matmul(dp, h2, "tn", BF16, f"b_mix_in_dw_{i}") for i, dp in enumerate(d_parts)], axis=0)
    g1_b = after(g1, hooks.reduce_early([g_wi], "b"), "after_rs_b")

    du1, grad_x, _, acc_n1 = _ffn_norm_bwd(df1, u1, w1i, w1o, xcat, dx1, mods, g1_b, 0, 1, None, None, "b_ffn1",
                                           skip_first_tile=True)
    g_w1o = _matmul(s1, df1, "tn", BF16, "b_ffn1_out_dw")
    g_w1i = _matmul(du1, h1, "tn", BF16, "b_ffn1_in_dw", token=hooks.reduce_early([g_w1o], "c"))

    grads = (g_w1i, g_w1o, g_wi, g_wbc, g_wba, g_wo, g_w2i, g_w2o)
    accs = (acc_head, acc_n3, acc_n2, acc_n1, acc_conv, acc_qk)
    return grad_x, grads, accs


def _place():
    return lax.axis_index("x"), lax.axis_index("y"), lax.axis_index("c")


def _other_chips(x, y):
    return [(1 - x, y), (x, 1 - y), (1 - x, 1 - y)]


def _allgather8(v, name):
    R, N = v.shape

    def body(v_ref, out_ref, send_sems, recv_sems, local_sem):
        x, y, c = _place()
        me, sibling = (x, y, c), (x, y, 1 - c)
        chips = _other_chips(x, y)

        def blk(px, py, pc):
            return out_ref.at[4 * px + 2 * py + pc]

        def copy(k, block, to, src=None):
            return pltpu.make_async_remote_copy(
                src_ref=blk(*block) if src is None else src, dst_ref=blk(*block),
                send_sem=send_sems.at[k], recv_sem=recv_sems.at[k], device_id=to, device_id_type=MESH)

        mine = pltpu.make_async_copy(v_ref, blk(*me), local_sem)
        mine.start()
        first = [copy(0, me, sibling, src=v_ref)]
        first += [copy(1 + j, me, (*chip, c), src=v_ref) for j, chip in enumerate(chips)]
        for cp in first:
            cp.start()
        passed = [copy(4 + j, (*chip, c), sibling) for j, chip in enumerate(chips)]
        for j, chip in enumerate(chips):
            copy(1 + j, (*chip, c), me).wait_recv()
            passed[j].start()
        copy(0, sibling, me).wait_recv()
        for j, chip in enumerate(chips):
            copy(4 + j, (*chip, 1 - c), me).wait_recv()
        for cp in first + passed:
            cp.wait_send()
        mine.wait()

    return pl.pallas_call(
        body, name=name,
        out_shape=jax.ShapeDtypeStruct((N_DEV, R, N), v.dtype),
        in_specs=[pl.BlockSpec(memory_space=pltpu.VMEM)],
        out_specs=pl.BlockSpec(memory_space=pltpu.VMEM),
        scratch_shapes=[pltpu.SemaphoreType.DMA((7,)), pltpu.SemaphoreType.DMA((7,)), pltpu.SemaphoreType.DMA],
        compiler_params=pltpu.CompilerParams(vmem_limit_bytes=VMEM_LIMIT),
    )(v)


def _any_specs(n):
    return [pl.BlockSpec(memory_space=pl.ANY)] * n


def _pair_exchange(grads, name):
    n = len(grads)

    def body(*refs):
        g, land = refs[:n], refs[n:2 * n]
        send_sems, recv_sems = refs[2 * n:]
        x, y, c = _place()
        sibling = (x, y, 1 - c)
        copies = []
        for t in range(n):
            half = grads[t].shape[0] // (2 * N_CHIPS)
            for s in range(N_CHIPS):
                cp = pltpu.make_async_remote_copy(
                    src_ref=g[t].at[pl.ds((2 * s + 1 - c) * half, half), :], dst_ref=land[t].at[s],
                    send_sem=send_sems.at[N_CHIPS * t + s], recv_sem=recv_sems.at[N_CHIPS * t + s],
                    device_id=sibling, device_id_type=MESH)
                cp.start()
                copies.append(cp)
        for cp in copies:
            cp.wait_recv()
        for cp in copies:
            cp.wait_send()

    return pl.pallas_call(
        body, name=name,
        out_shape=[jax.ShapeDtypeStruct((N_CHIPS, a.shape[0] // (2 * N_CHIPS), a.shape[1]), a.dtype) for a in grads],
        in_specs=_any_specs(n), out_specs=_any_specs(n),
        scratch_shapes=[pltpu.SemaphoreType.DMA((N_CHIPS * n,)), pltpu.SemaphoreType.DMA((N_CHIPS * n,))],
    )(*grads)


def _place_shard(w2, idx, transpose, name, token):
    if transpose:
        D, rs = w2.shape
        tr = 128
        in_spec = pl.BlockSpec((D, tr), lambda i, idx: (0, i))
    else:
        rs, D = w2.shape
        tr = _pick(rs, (352, 256, 128, 64, 32, 16))
        in_spec = pl.BlockSpec((tr, D), lambda i, idx: (i, 0))
    steps = rs // tr

    def body(idx_ref, w_ref, t_ref, o_ref):
        v = w_ref[...]
        o_ref[...] = (jnp.transpose(v) if transpose else v).astype(BF16)

    return pl.pallas_call(
        body, name=name,
        grid_spec=pltpu.PrefetchScalarGridSpec(
            num_scalar_prefetch=1, grid=(steps,),
            in_specs=[in_spec, pl.BlockSpec(token.shape, lambda i, idx: (0, 0))],
            out_specs=pl.BlockSpec((tr, D), lambda i, idx: (idx[1] * steps + i, 0))),
        out_shape=jax.ShapeDtypeStruct((N_CHIPS * rs, D), BF16),
        compiler_params=_params(("arbitrary",)),
    )(idx, w2, token)


def _pair_sum(g, landed, idx, name, token=None):
    _, half, D = landed.shape
    g4 = g.reshape(N_CHIPS, 2, half, D)
    tr = _pick(half, (416, 352, 128))
    extra = [] if token is None else [token]

    def body(idx_ref, g_ref, l_ref, *rest):
        rest[-1][...] = (g_ref[0].astype(F32) + l_ref[...].astype(F32)).astype(BF16)

    return pl.pallas_call(
        body, name=name,
        grid_spec=pltpu.PrefetchScalarGridSpec(
            num_scalar_prefetch=1, grid=(N_CHIPS, half // tr),
            in_specs=[pl.BlockSpec((1, 1, tr, D), lambda s, i, idx: (idx[1 + s], idx[0], i, 0)),
                      pl.BlockSpec((1, tr, D), lambda s, i, idx: (idx[1 + s], i, 0))] +
                     [pl.BlockSpec(t.shape, lambda s, i, idx: (0, 0)) for t in extra],
            out_specs=pl.BlockSpec((1, tr, D), lambda s, i, idx: (s, i, 0))),
        out_shape=jax.ShapeDtypeStruct((N_CHIPS, half, D), BF16),
        compiler_params=_params(("arbitrary", "arbitrary")),
    )(idx, g4, landed, *extra)


_HBM = pl.BlockSpec(memory_space=pltpu.HBM)
_SEM = pl.BlockSpec(memory_space=pltpu.SEMAPHORE)
_EFFECT = pltpu.SideEffectType.DATAFLOW_SIDE_EFFECTING


def _in_hbm(a):
    return pltpu.with_memory_space_constraint(a, pltpu.HBM)


def _split_copies(n, per, make):
    def start(nbuf, name, bufs):
        def body(*refs):
            ins = refs[:nbuf]
            send_sems, recv_sems = refs[nbuf], refs[nbuf + 1]
            token = refs[-1]
            for t in range(n):
                for j in range(per):
                    make(ins, t, j, send_sems.at[per * t + j], recv_sems.at[per * t + j]).start()
            token[...] = jnp.zeros(token.shape, token.dtype)

        out = pl.pallas_call(
            body, name=name,
            out_shape=(pltpu.SemaphoreType.DMA((per * n,)), pltpu.SemaphoreType.DMA((per * n,)),
                       *[pltpu.HBM(b.shape, b.dtype) for b in bufs], jax.ShapeDtypeStruct((8, 128), F32)),
            in_specs=[_HBM] * nbuf,
            out_specs=(_SEM, _SEM, *[_HBM] * nbuf, pl.BlockSpec(memory_space=pltpu.VMEM)),
            input_output_aliases={i: 2 + i for i in range(nbuf)},
            compiler_params=pltpu.CompilerParams(has_side_effects=_EFFECT),
        )(*[_in_hbm(b) for b in bufs])
        return out[0], out[1], list(out[2:2 + nbuf]), out[-1]

    def wait(nbuf, name, send_sems, recv_sems, bufs, after):
        def body(*refs):
            ins = refs[:nbuf]
            ss, rs = refs[nbuf], refs[nbuf + 1]
            for t in range(n):
                for j in range(per):
                    cp = make(ins, t, j, ss.at[per * t + j], rs.at[per * t + j])
                    cp.wait_send()
                    cp.wait_recv()

        return pl.pallas_call(
            body, name=name,
            out_shape=[pltpu.HBM(b.shape, b.dtype) for b in bufs],
            in_specs=[_HBM] * nbuf + [_SEM, _SEM, pl.BlockSpec(memory_space=pl.ANY)],
            out_specs=[_HBM] * nbuf,
            input_output_aliases={i: i for i in range(nbuf)},
            compiler_params=pltpu.CompilerParams(has_side_effects=_EFFECT),
        )(*bufs, send_sems, recv_sems, after)

    return start, wait


def _chip_exchange_split(n):
    def make(bufs, t, j, send_sem, recv_sem):
        x, y, c = _place()
        chip = _other_chips(x, y)[j]
        return pltpu.make_async_remote_copy(src_ref=bufs[t].at[1 + j], dst_ref=bufs[n + t].at[j], send_sem=send_sem,
                                            recv_sem=recv_sem, device_id=(*chip, c), device_id_type=MESH)

    return _split_copies(n, 3, make)


def _weights_gather_split(fulls):
    def make(bufs, t, j, send_sem, recv_sem):
        x, y, c = _place()
        chip = _other_chips(x, y)[j]
        rs = fulls[t].shape[0] // N_CHIPS
        rows = bufs[t].at[pl.ds((2 * x + y) * rs + c * (rs // 2), rs // 2), :]
        return pltpu.make_async_remote_copy(src_ref=rows, dst_ref=rows, send_sem=send_sem, recv_sem=recv_sem,
                                            device_id=(*chip, c), device_id_type=MESH)

    return _split_copies(len(fulls), 3, make)


def _weights_pass_on(fulls, name):
    n = len(fulls)

    def body(*refs):
        full = refs[n:2 * n]
        send_sems, recv_sems = refs[2 * n:]
        x, y, c = _place()
        chips = _other_chips(x, y)

        def copy(t, j, h):
            rs = fulls[t].shape[0] // N_CHIPS
            px, py = chips[j]
            rows = full[t].at[pl.ds((2 * px + py) * rs + h * (rs // 2), rs // 2), :]
            return pltpu.make_async_remote_copy(src_ref=rows, dst_ref=rows, send_sem=send_sems.at[3 * t + j],
                                                recv_sem=recv_sems.at[3 * t + j], device_id=(x, y, 1 - c),
                                                device_id_type=MESH)

        for t in range(n):
            for j in range(3):
                copy(t, j, c).start()
        for t in range(n):
            for j in range(3):
                copy(t, j, 1 - c).wait_recv()
        for t in range(n):
            for j in range(3):
                copy(t, j, c).wait_send()

    return pl.pallas_call(
        body, name=name,
        out_shape=[jax.ShapeDtypeStruct(f.shape, f.dtype) for f in fulls],
        in_specs=_any_specs(n), out_specs=_any_specs(n),
        input_output_aliases={t: t for t in range(n)},
        scratch_shapes=[pltpu.SemaphoreType.DMA((3 * n,)), pltpu.SemaphoreType.DMA((3 * n,))],
    )(*fulls)


def _after(value, token, name):
    def body(v_ref, t_ref, o_ref):
        o_ref[...] = v_ref[...]

    return pl.pallas_call(
        body, name=name, out_shape=jax.ShapeDtypeStruct(value.shape, value.dtype),
        in_specs=_whole(2), out_specs=pl.BlockSpec(memory_space=pltpu.VMEM),
    )(value, token)


def _chip_sum(ps, landed, idx, name):
    _, half, D = ps.shape
    tr = _pick(half, (416, 352, 128))
    steps = half // tr

    def body(idx_ref, p_ref, l_ref, o_ref):
        acc = p_ref[0].astype(F32)
        for j in range(3):
            acc = acc + l_ref[j].astype(F32)
        o_ref[...] = acc

    return pl.pallas_call(
        body, name=name,
        grid_spec=pltpu.PrefetchScalarGridSpec(
            num_scalar_prefetch=1, grid=(steps,),
            in_specs=[pl.BlockSpec((1, tr, D), lambda i, idx: (0, i, 0)),
                      pl.BlockSpec((3, tr, D), lambda i, idx: (0, i, 0))],
            out_specs=pl.BlockSpec((tr, D), lambda i, idx: (idx[0] * steps + i, 0))),
        out_shape=jax.ShapeDtypeStruct((2 * half, D), F32),
        compiler_params=_params(("arbitrary",)),
    )(idx, ps, landed)


def _pair_swap(shards, name):
    n = len(shards)

    def body(*refs):
        full = refs[n:2 * n]
        send_sems, recv_sems = refs[2 * n:]
        x, y, c = _place()

        def half(t, h):
            rows = shards[t].shape[0] // 2
            return full[t].at[pl.ds(h * rows, rows), :]

        def copy(t, h):
            return pltpu.make_async_remote_copy(src_ref=half(t, h), dst_ref=half(t, h), send_sem=send_sems.at[t],
                                                recv_sem=recv_sems.at[t], device_id=(x, y, 1 - c),
                                                device_id_type=MESH)

        for t in range(n):
            copy(t, c).start()
        for t in range(n):
            copy(t, 1 - c).wait_recv()
        for t in range(n):
            copy(t, c).wait_send()

    return pl.pallas_call(
        body, name=name,
        out_shape=[jax.ShapeDtypeStruct(a.shape, a.dtype) for a in shards],
        in_specs=_any_specs(n), out_specs=_any_specs(n),
        input_output_aliases={t: t for t in range(n)},
        scratch_shapes=[pltpu.SemaphoreType.DMA((n,)), pltpu.SemaphoreType.DMA((n,))],
    )(*shards)


def _gather_begin(fulls, tag):
    start, wait = _weights_gather_split(fulls)
    send_sems, recv_sems, bufs, token = start(len(fulls), f"ag_{tag}_start", fulls)
    return (wait, send_sems, recv_sems, bufs), token


def _gather_end(state, after, tag):
    wait, send_sems, recv_sems, bufs = state
    landed = wait(len(bufs), f"ag_{tag}_wait", send_sems, recv_sems, bufs, after)
    return _weights_pass_on(landed, f"ag_{tag}_pass_on")


class _Exchanges:
    def __init__(self, fulls_rest, idx):
        self.idx = idx
        self._rest, self.token = _gather_begin(fulls_rest, "rest")
        self._early = []

    def rest_weights(self, after):
        return _gather_end(self._rest, after, "rest")

    def reduce_early(self, grads, tag, token=None):
        landed = _pair_exchange(grads, "rs_pair_exchange_" + tag)
        sums = [_pair_sum(g, l, self.idx, f"rs_pair_sum_{tag}{t}", token)
                for t, (g, l) in enumerate(zip(grads, landed))]
        zones = [lax.empty((3,) + s.shape[1:], s.dtype) for s in sums]
        start, wait = _chip_exchange_split(len(sums))
        send_sems, recv_sems, bufs, token = start(2 * len(sums), "rs_chip_start_" + tag, sums + zones)
        self._early.append((tag, wait, send_sems, recv_sems, bufs))
        return token

    def finish(self, tags, after):
        halves = []
        for tag, wait, send_sems, recv_sems, bufs in self._early:
            if tag in tags:
                n = len(bufs) // 2
                done = wait(len(bufs), "rs_chip_wait_" + tag, send_sems, recv_sems, bufs, after)
                halves += [_chip_sum(p, l, self.idx, f"rs_chip_sum_{tag}{t}")
                           for t, (p, l) in enumerate(zip(done[:n], done[n:]))]
        return halves


N_MOD = 9
PACK_HEAD, PACK_N3, PACK_N2, PACK_N1, PACK_CONV, PACK_QK = 0, 16, 32, 48, 64, 80
PACK_ROWS = 96
MOD_SRC = ((PACK_N1, 0), (PACK_N1, 1), (PACK_N2, 3), (PACK_N2, 0), (PACK_N2, 1),
           (PACK_N3, 3), (PACK_N3, 0), (PACK_N3, 1), (PACK_HEAD, 2))
CTX_ROW = 8


def _silu(v):
    return v * jax.nn.sigmoid(v)


def _whole(n):
    return [pl.BlockSpec(memory_space=pltpu.VMEM)] * n


def _mod_rows(cin, w_sh, b_sh, name):
    def body(c_ref, w_ref, b_ref, o_ref):
        a = _silu(c_ref[...]).astype(BF16)
        o_ref[...] = jnp.dot(a, w_ref[...].astype(BF16), preferred_element_type=F32) + b_ref[...]

    return pl.pallas_call(
        body, name=name, out_shape=jax.ShapeDtypeStruct((cin.shape[0], w_sh.shape[1]), F32),
        in_specs=_whole(3), out_specs=pl.BlockSpec(memory_space=pltpu.VMEM),
        compiler_params=pltpu.CompilerParams(vmem_limit_bytes=VMEM_LIMIT),
    )(cin, w_sh, b_sh)


def _small_reduce(gathered, name):
    _, _, D = gathered.shape

    def body(g_ref, loss_ref, db_ref, gn_ref, cv_ref, qk_ref, dm_ref):
        tot = g_ref[0]
        for r in range(1, N_DEV):
            tot = tot + g_ref[r]

        def both(block, row):
            return tot[block + row:block + row + 1, :] + tot[block + 8 + row:block + 8 + row + 1, :]

        loss = jnp.sum(both(PACK_HEAD, 0), axis=1, keepdims=True)
        loss_ref[...] = jnp.broadcast_to(loss, loss_ref.shape)
        db_ref[...] = jnp.zeros(db_ref.shape, F32)
        dm_ref[...] = jnp.zeros(dm_ref.shape, F32)
        for j, (block, row) in enumerate(MOD_SRC):
            db_ref[j:j + 1, :] = both(block, row)
            dm_ref[CTX_ROW, j:j + 1, :] = tot[block + row:block + row + 1, :]
            for r in range(N_DEV):
                dm_ref[r, j:j + 1, :] = g_ref[r, block + 8 + row:block + 8 + row + 1, :]
        gn_ref[...] = jnp.zeros(gn_ref.shape, F32)
        gn_ref[0:1, :] = both(PACK_N1, 2)
        gn_ref[8:9, :] = both(PACK_N2, 2)
        gn_ref[16:17, :] = both(PACK_N3, 2)
        gn_ref[24:25, :] = both(PACK_HEAD, 1)
        cv_ref[...] = jnp.zeros(cv_ref.shape, F32)
        for r in range(3):
            cv_ref[r:r + 1, :] = both(PACK_CONV, r)
        qk_ref[...] = jnp.zeros(qk_ref.shape, F32)
        qk_ref[0:1, 0:HEAD_DIM] = both(PACK_QK, 0)[:, 0:HEAD_DIM]
        qk_ref[0:1, HEAD_DIM:2 * HEAD_DIM] = both(PACK_QK, 1)[:, 0:HEAD_DIM]

    return pl.pallas_call(
        body, name=name,
        out_shape=[jax.ShapeDtypeStruct((8, 128), F32), jax.ShapeDtypeStruct((16, D), F32),
                   jax.ShapeDtypeStruct((32, D), F32), jax.ShapeDtypeStruct((8, D), F32),
                   jax.ShapeDtypeStruct((8, D), F32), jax.ShapeDtypeStruct((16, 16, D), F32)],
        in_specs=_whole(1), out_specs=_whole(6),
        compiler_params=pltpu.CompilerParams(vmem_limit_bytes=VMEM_LIMIT),
    )(gathered)


def _wmod_grad(cin, dm_sh, w_sh, name):
    def body(c_ref, d_ref, w_ref, gw_ref, cp_ref):
        a = _silu(c_ref[...]).astype(BF16)
        d = d_ref[...].astype(BF16)
        gw_ref[...] = lax.dot_general(a, d, (((0,), (0,)), ((), ())), preferred_element_type=F32)
        cp_ref[...] = lax.dot_general(d, w_ref[...].astype(BF16), (((1,), (1,)), ((), ())),
                                      preferred_element_type=F32)

    return pl.pallas_call(
        body, name=name,
        out_shape=[jax.ShapeDtypeStruct(w_sh.shape, F32), jax.ShapeDtypeStruct(cin.shape, F32)],
        in_specs=_whole(3), out_specs=_whole(2),
        compiler_params=pltpu.CompilerParams(vmem_limit_bytes=VMEM_LIMIT),
    )(cin, dm_sh, w_sh)


def _cctx_grad(parts, c_ctx8, name):
    def body(p_ref, c_ref, o_ref):
        tot = p_ref[0] + p_ref[2] + p_ref[4] + p_ref[6]
        cv = c_ref[...]
        sig = jax.nn.sigmoid(cv)
        rows = lax.broadcasted_iota(jnp.int32, tot.shape, 0)
        o_ref[...] = jnp.where(rows == 0, tot * (sig * (1.0 + cv * (1.0 - sig))), 0.0)

    return pl.pallas_call(
        body, name=name, out_shape=jax.ShapeDtypeStruct(c_ctx8.shape, F32),
        in_specs=_whole(2), out_specs=pl.BlockSpec(memory_space=pltpu.VMEM),
    )(parts, c_ctx8)


def _pad_rows(a, rows):
    return jnp.pad(a, ((0, rows - a.shape[0]), (0, 0)))


def _pack_small(c_ctx, b_mod, n1, n2, n3, final_g, gq, gk, conv_sh, D):
    misc = jnp.concatenate([gq, gk, conv_sh.reshape(1, -1)], axis=1)
    return jnp.concatenate([_pad_rows(c_ctx[None], 8), _pad_rows(b_mod.reshape(N_MOD, D), 16), _pad_rows(n1, 8),
                            _pad_rows(n2, 8), _pad_rows(n3, 8), _pad_rows(final_g[None], 8), _pad_rows(misc, 8)], axis=0)


def _unpack_small(p, D, conv_shape):
    misc = p[56:57]
    return dict(c_ctx=p[0], b_mod=p[8:8 + N_MOD].reshape(1, N_MOD * D), norm1_g=p[24:25], norm2_g=p[32:33],
                norm3_g=p[40:41], final_g=p[48], q_norm_g=misc[:, 0:HEAD_DIM], k_norm_g=misc[:, HEAD_DIM:2 * HEAD_DIM],
                conv_w=misc[:, 2 * HEAD_DIM:].reshape(conv_shape))


WEIGHT_ORDER = ("c_ctx", "w_mod", "b_mod", "norm1_g", "norm2_g", "norm3_g", "ffn1_w_in", "ffn1_w_out", "w_in",
                "conv_w", "q_norm_g", "k_norm_g", "w_branch_conv", "w_branch_attn", "w_out", "ffn2_w_in",
                "ffn2_w_out", "final_g")
BIG = ("ffn1_w_in", "ffn1_w_out", "w_in", "w_branch_conv", "w_branch_attn", "w_out", "ffn2_w_in", "ffn2_w_out")
COLUMN_SHARDED = ("ffn1_w_in", "w_in", "ffn2_w_in")


def kernel(x, c, ctx, c_ctx, w_mod, b_mod, norm1_g, norm2_g, norm3_g, ffn1_w_in, ffn1_w_out, w_in, conv_w, q_norm_g, k_norm_g, w_branch_conv, w_branch_attn, w_out, ffn2_w_in, ffn2_w_out, final_g, loss_target, m_c_ctx, m_w_mod, m_b_mod, m_norm1_g, m_norm2_g, m_norm3_g, m_ffn1_w_in, m_ffn1_w_out, m_w_in, m_conv_w, m_q_norm_g, m_k_norm_g, m_w_branch_conv, m_w_branch_attn, m_w_out, m_ffn2_w_in, m_ffn2_w_out, m_final_g, v_c_ctx, v_w_mod, v_b_mod, v_norm1_g, v_norm2_g, v_norm3_g, v_ffn1_w_in, v_ffn1_w_out, v_w_in, v_conv_w, v_q_norm_g, v_k_norm_g, v_w_branch_conv, v_w_branch_attn, v_w_out, v_ffn2_w_in, v_ffn2_w_out, v_final_g):
    w = dict(c_ctx=c_ctx, w_mod=w_mod, b_mod=b_mod, norm1_g=norm1_g, norm2_g=norm2_g, norm3_g=norm3_g,
             ffn1_w_in=ffn1_w_in, ffn1_w_out=ffn1_w_out, w_in=w_in, conv_w=conv_w, q_norm_g=q_norm_g,
             k_norm_g=k_norm_g, w_branch_conv=w_branch_conv, w_branch_attn=w_branch_attn, w_out=w_out,
             ffn2_w_in=ffn2_w_in, ffn2_w_out=ffn2_w_out, final_g=final_g)
    m = dict(c_ctx=m_c_ctx, w_mod=m_w_mod, b_mod=m_b_mod, norm1_g=m_norm1_g, norm2_g=m_norm2_g, norm3_g=m_norm3_g,
             ffn1_w_in=m_ffn1_w_in, ffn1_w_out=m_ffn1_w_out, w_in=m_w_in, conv_w=m_conv_w, q_norm_g=m_q_norm_g,
             k_norm_g=m_k_norm_g, w_branch_conv=m_w_branch_conv, w_branch_attn=m_w_branch_attn, w_out=m_w_out,
             ffn2_w_in=m_ffn2_w_in, ffn2_w_out=m_ffn2_w_out, final_g=m_final_g)
    v = dict(c_ctx=v_c_ctx, w_mod=v_w_mod, b_mod=v_b_mod, norm1_g=v_norm1_g, norm2_g=v_norm2_g, norm3_g=v_norm3_g,
             ffn1_w_in=v_ffn1_w_in, ffn1_w_out=v_ffn1_w_out, w_in=v_w_in, conv_w=v_conv_w, q_norm_g=v_q_norm_g,
             k_norm_g=v_k_norm_g, w_branch_conv=v_w_branch_conv, w_branch_attn=v_w_branch_attn, w_out=v_w_out,
             ffn2_w_in=v_ffn2_w_in, ffn2_w_out=v_ffn2_w_out, final_g=v_final_g)

    xi, yi, ci = _place()
    dev = 4 * xi + 2 * yi + ci
    shard = 2 * xi + yi
    idx = jnp.stack([ci, shard, 2 * (1 - xi) + yi, 2 * xi + (1 - yi), 2 * (1 - xi) + (1 - yi)]).astype(jnp.int32)
    D = x.shape[-1]
    ctx_len = ctx.shape[1]
    assert ctx_len == ROW and c.shape == (1, D)
    mcols = w_mod.shape[2]
    ccols = conv_w.shape[2]

    c_all = _allgather8(jnp.broadcast_to(c, (8, D)), "ag_c")[:, 0, :]
    cin = jnp.concatenate([c_all, _pad_rows(c_ctx[None], 8)], axis=0)
    b_sh = lax.dynamic_slice(b_mod, (0, shard * mcols), (1, mcols))
    mod_sh = _mod_rows(cin, w_mod[0], b_sh, "mod_rows")
    conv_rows = jnp.pad(conv_w[0], ((0, 8 - conv_w.shape[1]), (0, mcols - ccols)))
    mod_all = _allgather8(jnp.concatenate([mod_sh, conv_rows], axis=0), "ag_mod")
    mod_full = jnp.concatenate([mod_all[2 * s, :16] for s in range(N_CHIPS)], axis=1)
    conv_full = jnp.concatenate([mod_all[2 * s, 16:16 + conv_w.shape[1], :ccols] for s in range(N_CHIPS)], axis=1)
    mod_lat = lax.dynamic_slice(mod_full, (dev, 0), (1, N_MOD * D)).reshape(N_MOD, D)
    mod_ctx = mod_full[CTX_ROW].reshape(N_MOD, D)
    mods = jnp.stack([_pad_rows(mod_ctx, 16), _pad_rows(mod_lat, 16)])

    def place(names, token):
        return [_place_shard(w[n][0], idx, n in COLUMN_SHARDED, "place_" + n, token) for n in names]

    ffn1_gather, ffn1_token = _gather_begin(place(BIG[:2], mod_all[0, :8, :HEAD_DIM]), "ffn1")
    fulls_rest = place(BIG[2:], ffn1_token)
    ffn1_w = _gather_end(ffn1_gather, fulls_rest[-1][:16, :HEAD_DIM], "ffn1")
    hooks = _Exchanges(fulls_rest, idx)

    xcat = (ctx[0], x[0])
    norm1_first = _after(norm1_g, hooks.token, "after_ag_rest")
    grad_x, grads, accs = _local_step(xcat, loss_target[0], mods, (norm1_first, norm2_g, norm3_g), final_g[None],
                                      q_norm_g, k_norm_g, conv_full, ffn1_w, hooks, ctx_len)
    g = {}

    pack = jnp.concatenate([a.reshape(2 * ACC_ROWS, D) for a in accs], axis=0)
    gathered = _allgather8(pack, "ag_small")
    loss8, db_mod, g_norms, g_conv, g_qk, dm = _small_reduce(gathered, "small_reduce")
    dm_sh = lax.dynamic_slice(dm[:, :N_MOD, :].reshape(16, N_MOD * D), (0, shard * mcols), (16, mcols))
    g_wmod, cpart = _wmod_grad(cin, dm_sh, w_mod[0], "wmod_grad")
    g["w_mod"] = g_wmod[None]
    cparts = _allgather8(cpart[CTX_ROW:CTX_ROW + 8], "ag_cctx")
    g_cctx = _cctx_grad(cparts, _pad_rows(c_ctx[None], 8), "cctx_grad")
    g_conv_sh = lax.dynamic_slice(g_conv, (0, shard * ccols), (conv_w.shape[1], ccols))
    g_misc = jnp.concatenate([g_qk[0:1, 0:2 * HEAD_DIM], g_conv_sh.reshape(1, -1)], axis=1)
    g_pack = jnp.concatenate([g_cctx, db_mod, g_norms, _pad_rows(g_misc, 8)], axis=0)

    def packed(p):
        return _pack_small(p["c_ctx"], p["b_mod"], p["norm1_g"], p["norm2_g"], p["norm3_g"], p["final_g"],
                           p["q_norm_g"], p["k_norm_g"], p["conv_w"][0], D)

    d_pack, m_pack, v_pack = _adamw(packed(w), g_pack, packed(m), packed(v), "adamw_small")

    g.update(_unpack_small(g_pack, D, conv_w.shape))
    delta = _unpack_small(d_pack, D, conv_w.shape)
    new_m = _unpack_small(m_pack, D, conv_w.shape)
    new_v = _unpack_small(v_pack, D, conv_w.shape)

    def update(n, g2):
        if n in COLUMN_SHARDED:
            g2, d2, m2, v2 = _adamw_transposed(w[n][0], g2, m[n][0], v[n][0], "adamw_" + n)
        else:
            d2, m2, v2 = _adamw(w[n][0], g2, m[n][0], v[n][0], "adamw_" + n)
        g[n], delta[n], new_m[n], new_v[n] = g2[None], d2[None], m2[None], v2[None]
        return v2

    token_d = hooks.reduce_early([grads[0]], "d", token=d_pack[:8, :HEAD_DIM])
    h_wbc, h_wba, h_wo, h_w2i, h_w2o, h_wi, h_w1o = hooks.finish("abc", token_d)
    done = _pair_swap([h_w1o, h_wi, h_wbc, h_wba, h_wo, h_w2i, h_w2o], "rs_pair_swap")
    last = update("w_mod", g_wmod)
    for n, r in zip(BIG[1:], done):
        last = update(n, r)
    (h_w1i,) = hooks.finish("d", last)
    update(BIG[0], _pair_swap([h_w1i], "rs_pair_swap_d")[0])

    loss = loss8[0, 0]
    return (loss, grad_x[None], *[g[n] for n in WEIGHT_ORDER], *[delta[n] for n in WEIGHT_ORDER],
            *[new_m[n] for n in WEIGHT_ORDER], *[new_v[n] for n in WEIGHT_ORDER])
```

```python
import functools

import jax
import jax.numpy as jnp
from jax import lax
from jax.experimental import pallas as pl
from jax.experimental.pallas import tpu as pltpu

F32 = jnp.float32
BF16 = jnp.bfloat16

HEAD_DIM = 128
N_Q_HEADS = 8
N_KV_HEADS = 2
GROUP = N_Q_HEADS // N_KV_HEADS
GRID_W = 64
ROPE_THETA = 10000.0
EPS = 1e-6
ATTN_SCALE = HEAD_DIM ** -0.5

ADAM_LR = 0.001
ADAM_B1 = 0.9
ADAM_B2 = 0.999
ADAM_EPS = 1e-08
ADAM_WD = 0.01
ADAM_STEP = 10

ROW = 256
HALO = 16
ACC_ROWS = 8
N_CHIPS = 4
N_DEV = 8
MESH = pl.DeviceIdType.MESH
VMEM_LIMIT = 48 * 1024 * 1024
ADAMW_BLOCK_BYTES = 1024 * 1024


def _pick(n, prefs):
    for p in prefs:
        if n % p == 0:
            return p
    return n


def _params(sem):
    return pltpu.CompilerParams(dimension_semantics=sem, vmem_limit_bytes=VMEM_LIMIT)


def _stream(i):
    return jnp.minimum(i, 1)


def _matmul(a, b, mode, out_dtype, name, tm=None, tn=None, tk=None, token=None):
    if mode == "nn":
        (M, K), (K2, N) = a.shape, b.shape
    elif mode == "nt":
        (M, K), (N, K2) = a.shape, b.shape
    else:
        (K, M), (K2, N) = a.shape, b.shape
    assert K == K2, (a.shape, b.shape, mode)
    tm = tm or _pick(M, (1664, 1408, 1024, 512, 256, 128) if mode == "tn" else (1408, 768, 512, 256, 128))
    tn = tn or _pick(N, (1664, 1408, 1024, 512, 256, 128))
    tk = tk or _pick(K, (1664, 1408, 1024, 768, 512, 256, 128))
    nk = K // tk
    if mode == "tn":
        a_spec = pl.BlockSpec((tk, tm), lambda i, j, k: (k, i))
    else:
        a_spec = pl.BlockSpec((tm, tk), lambda i, j, k: (i, k))
    if mode == "nt":
        b_spec = pl.BlockSpec((tn, tk), lambda i, j, k: (j, k))
    else:
        b_spec = pl.BlockSpec((tk, tn), lambda i, j, k: (k, j))
    dims = {"nn": ((1,), (0,)), "nt": ((1,), (1,)), "tn": ((0,), (0,))}[mode]
    use_scratch = nk > 1 and out_dtype != F32

    extra = [] if token is None else [token]

    def body(a_ref, b_ref, *rest):
        o_ref, scratch = rest[len(extra)], rest[len(extra) + 1:]
        p = lax.dot_general(a_ref[...].astype(BF16), b_ref[...].astype(BF16), (dims, ((), ())),
                            preferred_element_type=F32)
        if nk == 1:
            o_ref[...] = p.astype(o_ref.dtype)
            return
        acc_ref = scratch[0] if use_scratch else o_ref
        k = pl.program_id(2)

        @pl.when(k == 0)
        def _():
            acc_ref[...] = p

        @pl.when(k > 0)
        def _():
            acc_ref[...] += p

        if use_scratch:
            @pl.when(k == nk - 1)
            def _():
                o_ref[...] = acc_ref[...].astype(o_ref.dtype)

    return pl.pallas_call(
        body, name=name,
        grid=(M // tm, N // tn, nk),
        in_specs=[a_spec, b_spec] + [pl.BlockSpec(t.shape, lambda i, j, k: (0, 0)) for t in extra],
        out_specs=pl.BlockSpec((tm, tn), lambda i, j, k: (i, j)),
        out_shape=jax.ShapeDtypeStruct((M, N), out_dtype),
        scratch_shapes=[pltpu.VMEM((tm, tn), F32)] if use_scratch else [],
        compiler_params=_params(("parallel", "parallel", "arbitrary")),
    )(a, b, *extra)


def _row_spec(width, col=0):
    return pl.BlockSpec((ROW, width), lambda i, col=col: (i, col))


def _mods_spec(D):
    return pl.BlockSpec((1, 16, D), lambda i: (_stream(i), 0, 0))


def _acc_spec(D):
    return pl.BlockSpec((1, ACC_ROWS, D), lambda i: (_stream(i), 0, 0))


def _vec_spec(rows, D):
    return pl.BlockSpec((rows, D), lambda i: (0, 0))


def _acc_init(acc_ref):
    i = pl.program_id(0)

    @pl.when(i <= 1)
    def _():
        acc_ref[...] = jnp.zeros_like(acc_ref)


def _acc_add(acc_ref, row, val):
    acc_ref[0, row:row + 1, :] += jnp.sum(val, axis=0, keepdims=True)


def _rows_operand(x):
    if not isinstance(x, tuple):
        return [_row_spec(x.shape[1])], [x], x.shape
    ctx, lat = x
    D = lat.shape[1]
    assert ctx.shape == (ROW, D)
    specs = [pl.BlockSpec((ROW, D), lambda i: (0, 0)), pl.BlockSpec((ROW, D), lambda i: (jnp.maximum(i - 1, 0), 0))]
    return specs, [ctx, lat], (ROW + lat.shape[0], D)


def _rows_tile(refs):
    if len(refs) == 1:
        return refs[0][...]
    return jnp.where(pl.program_id(0) == 0, refs[0][...], refs[1][...])


def _norm_tile_fwd(x, m, g, shift_idx, scale_idx):
    inv = lax.rsqrt(jnp.mean(x * x, axis=-1, keepdims=True) + EPS)
    y = (x * inv) * g
    return (y * (1.0 + m[scale_idx:scale_idx + 1, :]) + m[shift_idx:shift_idx + 1, :]).astype(BF16)


def _norm_tile_bwd(x, dh, dres, m, g, shift_idx, scale_idx, acc_ref):
    inv = lax.rsqrt(jnp.mean(x * x, axis=-1, keepdims=True) + EPS)
    xn = x * inv
    dy = dh * (1.0 + m[scale_idx:scale_idx + 1, :])
    dxn = dy * g
    _acc_add(acc_ref, 0, dh)
    _acc_add(acc_ref, 1, dh * (xn * g))
    _acc_add(acc_ref, 2, dy * xn)
    return inv * (dxn - xn * jnp.mean(dxn * xn, axis=-1, keepdims=True)) + dres


def _gate_tile_bwd(dx, branch, m, gate, acc_ref):
    gate_idx, fac = gate
    _acc_add(acc_ref, 3, fac * dx * branch)
    return ((fac * m[gate_idx:gate_idx + 1, :]) * dx).astype(BF16)


_NT = (((1,), (1,)), ((), ()))


def _ffn_chunk(F):
    return _pick(F, (1408, 512, 256, 128))


def _resident():
    return pl.BlockSpec(memory_space=pltpu.VMEM)


def _ffn_tile_fwd(hv, wi_ref, wo_ref, u_ref, s_ref, F, cw):
    acc = jnp.zeros((hv.shape[0], wo_ref.shape[1]), F32)
    for j in range(F // cw):
        a = lax.dot_general(hv, wi_ref[j * cw:(j + 1) * cw, :], _NT, preferred_element_type=F32)
        b = lax.dot_general(hv, wi_ref[F + j * cw:F + (j + 1) * cw, :], _NT, preferred_element_type=F32)
        s = ((a * jax.nn.sigmoid(a)) * b).astype(BF16)
        u_ref[:, j * cw:(j + 1) * cw] = a.astype(BF16)
        u_ref[:, F + j * cw:F + (j + 1) * cw] = b.astype(BF16)
        s_ref[:, j * cw:(j + 1) * cw] = s
        acc = acc + jnp.dot(s, wo_ref[j * cw:(j + 1) * cw, :], preferred_element_type=F32)
    return acc


def _norm_ffn_fwd(xprev, branch, mods, g, gate, shift_idx, scale_idx, w_in_t, w_out, name, head=None):
    x_specs, x_args, (T, D) = _rows_operand(xprev)
    F = w_out.shape[0]
    cw = _ffn_chunk(F)
    has_res = branch is not None
    n_in = len(x_args) + int(has_res) + 4 + (2 if head else 0)

    def body(*refs):
        ins, outs = list(refs[:n_in]), list(refs[n_in:])
        x = _rows_tile([ins.pop(0) for _ in x_args])
        f_ref = ins.pop(0) if has_res else None
        m_ref, g_ref, wi_ref, wo_ref = ins[:4]
        xo_ref = outs.pop(0) if has_res else None
        h_ref, u_ref, s_ref = outs[:3]
        m = m_ref[0]
        if has_res:
            gate_idx, fac = gate
            x = x + (fac * m[gate_idx:gate_idx + 1, :]) * f_ref[...]
            xo_ref[...] = x
        hv = _norm_tile_fwd(x, m, g_ref[...], shift_idx, scale_idx)
        h_ref[...] = hv
        f = _ffn_tile_fwd(hv, wi_ref, wo_ref, u_ref, s_ref, F, cw)
        if head is None:
            outs[3][...] = f
            return
        fg_ref, t_ref = ins[4:6]
        dx_ref, df_ref, acc_ref = outs[3:6]
        _acc_init(acc_ref)
        lat = (pl.program_id(0) > 0).astype(F32)
        gate8 = 0.5 * m[8:9, :]
        x3 = x + gate8 * f
        inv3 = lax.rsqrt(jnp.mean(x3 * x3, axis=-1, keepdims=True) + EPS)
        xn = x3 * inv3
        fg = fg_ref[...]
        e = (xn * fg - t_ref[...]) * lat
        dy = e * (1.0 / D)
        dxn = dy * fg
        dx = inv3 * (dxn - xn * jnp.mean(dxn * xn, axis=-1, keepdims=True))
        dx_ref[...] = dx
        df_ref[...] = (gate8 * dx).astype(BF16)
        _acc_add(acc_ref, 0, (0.5 / D) * e * e)
        _acc_add(acc_ref, 1, dy * xn)
        _acc_add(acc_ref, 2, 0.5 * dx * f)

    in_specs = x_specs + ([_row_spec(D)] if has_res else []) + \
               [_mods_spec(D), _vec_spec(1, D), _resident(), _resident()]
    args = x_args + ([branch] if has_res else []) + [mods, g, w_in_t, w_out]
    out_specs = ([_row_spec(D)] if has_res else []) + [_row_spec(D), _row_spec(2 * F), _row_spec(F)]
    out_shape = ([jax.ShapeDtypeStruct((T, D), F32)] if has_res else []) + \
                [jax.ShapeDtypeStruct((T, D), BF16), jax.ShapeDtypeStruct((T, 2 * F), BF16),
                 jax.ShapeDtypeStruct((T, F), BF16)]
    if head is None:
        out_specs += [_row_spec(D)]
        out_shape += [jax.ShapeDtypeStruct((T, D), F32)]
    else:
        in_specs += [_vec_spec(1, D), pl.BlockSpec((ROW, D), lambda i: (jnp.maximum(i - 1, 0), 0))]
        args += list(head)
        out_specs += [_row_spec(D), _row_spec(D), _acc_spec(D)]
        out_shape += [jax.ShapeDtypeStruct((T, D), F32), jax.ShapeDtypeStruct((T, D), BF16),
                      jax.ShapeDtypeStruct((2, ACC_ROWS, D), F32)]
    out = pl.pallas_call(
        body, name=name, grid=(T // ROW,), in_specs=in_specs, out_specs=out_specs, out_shape=out_shape,
        compiler_params=_params(("arbitrary",) if head else ("parallel",)),
    )(*args)
    return tuple(out) if has_res else (None,) + tuple(out)


def _ffn_norm_bwd(df, u, w_in_t, w_out, x, dres, mods, g, shift_idx, scale_idx, gate, branch, name,
                  skip_first_tile=False):
    T, D = df.shape
    F = w_out.shape[0]
    cw = _ffn_chunk(F)
    nt = T // ROW
    has_gate = gate is not None
    x_specs, x_args, _ = _rows_operand(x)
    n_in = 7 + len(x_args) + int(has_gate)

    def body(*refs):
        ins, outs = list(refs[:n_in]), list(refs[n_in:])
        df_ref, u_ref, wi_ref, wo_ref = ins[:4]
        x_refs = ins[4:4 + len(x_args)]
        dr_ref = ins[4 + len(x_args)]
        b_ref = ins[5 + len(x_args)] if has_gate else None
        m_ref, g_ref = ins[-2:]
        du_ref, dx_ref = outs[:2]
        db_ref = outs[2] if has_gate else None
        acc_ref = outs[-1]
        _acc_init(acc_ref)
        dfv = df_ref[...]
        dh = jnp.zeros((ROW, D), F32)
        for j in range(F // cw):
            ds = lax.dot_general(dfv, wo_ref[j * cw:(j + 1) * cw, :], _NT, preferred_element_type=F32)
            a = u_ref[:, j * cw:(j + 1) * cw].astype(F32)
            b = u_ref[:, F + j * cw:F + (j + 1) * cw].astype(F32)
            sig = jax.nn.sigmoid(a)
            da = (ds * b * (sig * (1.0 + a * (1.0 - sig)))).astype(BF16)
            db = (ds * (a * sig)).astype(BF16)
            du_ref[:, j * cw:(j + 1) * cw] = da
            du_ref[:, F + j * cw:F + (j + 1) * cw] = db
            dh = dh + jnp.dot(da, wi_ref[j * cw:(j + 1) * cw, :], preferred_element_type=F32)
            dh = dh + jnp.dot(db, wi_ref[F + j * cw:F + (j + 1) * cw, :], preferred_element_type=F32)
        m = m_ref[0]
        dx = _norm_tile_bwd(_rows_tile(x_refs), dh, dr_ref[...], m, g_ref[...], shift_idx, scale_idx, acc_ref)
        dx_ref[...] = dx
        if has_gate:
            db_ref[...] = _gate_tile_bwd(dx, b_ref[...], m, gate, acc_ref)

    in_specs = [_row_spec(D), _row_spec(2 * F), _resident(), _resident()] + x_specs + [_row_spec(D)] + \
               ([_row_spec(D)] if has_gate else []) + [_mods_spec(D), _vec_spec(1, D)]
    args = [df, u, w_in_t, w_out] + x_args + [dres] + ([branch] if has_gate else []) + [mods, g]
    if skip_first_tile:
        dx_spec = pl.BlockSpec((ROW, D), lambda i: (jnp.maximum(i - 1, 0), 0))
        dx_shape = jax.ShapeDtypeStruct((T - ROW, D), F32)
    else:
        dx_spec = _row_spec(D)
        dx_shape = jax.ShapeDtypeStruct((T, D), F32)
    out_specs = [_row_spec(2 * F), dx_spec] + ([_row_spec(D)] if has_gate else []) + [_acc_spec(D)]
    out_shape = [jax.ShapeDtypeStruct((T, 2 * F), BF16), dx_shape] + \
                ([jax.ShapeDtypeStruct((T, D), BF16)] if has_gate else []) + \
                [jax.ShapeDtypeStruct((2, ACC_ROWS, D), F32)]
    out = pl.pallas_call(
        body, name=name, grid=(nt,), in_specs=in_specs, out_specs=out_specs, out_shape=out_shape,
        compiler_params=_params(("arbitrary",)),
    )(*args)
    if has_gate:
        return tuple(out)
    return out[0], out[1], None, out[2]


def _halo_specs(width, col, nt):
    per = ROW // HALO
    prev = pl.BlockSpec((HALO, width), lambda i, col=col: (jnp.maximum(i * per - 1, 0), col))
    nxt = pl.BlockSpec((HALO, width), lambda i, col=col: (jnp.minimum((i + 1) * per, nt * per - 1), col))
    return prev, nxt


def _f32(ref):
    return ref[...].astype(F32)


def _last_row(halo_ref):
    return halo_ref[HALO - 1:HALO, :].astype(F32)


def _first_row(halo_ref):
    return halo_ref[0:1, :].astype(F32)


def _shift_rows(v, prev_row, next_row):
    rows = lax.broadcasted_iota(jnp.int32, v.shape, 0)
    down = jnp.where(rows == 0, prev_row, pltpu.roll(v, 1, 0))
    up = jnp.where(rows == v.shape[0] - 1, next_row, pltpu.roll(v, v.shape[0] - 1, 0))
    return down, up


def _conv_fwd_operands(P, conv_w, D):
    nt = P.shape[0] // ROW
    cg_p, cg_n = _halo_specs(D, 1, nt)
    vc_p, vc_n = _halo_specs(D, 2, nt)
    specs = [_row_spec(D, 0), _row_spec(D, 1), _row_spec(D, 2), cg_p, vc_p, cg_n, vc_n, _vec_spec(3, D)]
    return specs, [P, P, P, P, P, P, P, conv_w]


def _conv_tile_fwd(refs, nt):
    bg_ref, cg_ref, vc_ref, cgp_ref, vcp_ref, cgn_ref, vcn_ref, w_ref = refs
    i = pl.program_id(0)
    has_prev = (i != 1).astype(F32)
    has_next = (i != nt - 1).astype(F32)
    u = _f32(cg_ref) * _f32(vc_ref)
    up_row = _last_row(cgp_ref) * _last_row(vcp_ref) * has_prev
    un_row = _first_row(cgn_ref) * _first_row(vcn_ref) * has_next
    um1, up1 = _shift_rows(u, up_row, un_row)
    w = w_ref[...]
    conv = um1 * w[0:1, :] + u * w[1:2, :] + up1 * w[2:3, :]
    return (_f32(bg_ref) * conv).astype(BF16)


def _conv_bwd_operands(P, dy, conv_w, D):
    nt = P.shape[0] // ROW
    bg_p, bg_n = _halo_specs(D, 0, nt)
    cg_p, cg_n = _halo_specs(D, 1, nt)
    vc_p, vc_n = _halo_specs(D, 2, nt)
    dy_p, dy_n = _halo_specs(D, 0, nt)
    specs = [_row_spec(D, 0), _row_spec(D, 1), _row_spec(D, 2), _row_spec(D, 0),
             bg_p, cg_p, vc_p, dy_p, bg_n, cg_n, vc_n, dy_n, _vec_spec(3, D)]
    return specs, [P, P, P, dy, P, P, P, dy, P, P, P, dy, conv_w]


def _conv_tile_bwd(refs, o_ref, acc_ref, D, nt):
    (bg_ref, cg_ref, vc_ref, dy_ref, bgp_ref, cgp_ref, vcp_ref, dyp_ref,
     bgn_ref, cgn_ref, vcn_ref, dyn_ref, w_ref) = refs
    i = pl.program_id(0)
    lat = (i > 0).astype(F32)
    has_prev = (i != 1).astype(F32)
    has_next = (i != nt - 1).astype(F32)
    bg = _f32(bg_ref)
    cg = _f32(cg_ref)
    vc = _f32(vc_ref)
    dyv = dy_ref[...] * lat
    u = cg * vc
    up_row = _last_row(cgp_ref) * _last_row(vcp_ref) * has_prev
    un_row = _first_row(cgn_ref) * _first_row(vcn_ref) * has_next
    um1, up1 = _shift_rows(u, up_row, un_row)
    w = w_ref[...]
    conv = um1 * w[0:1, :] + u * w[1:2, :] + up1 * w[2:3, :]
    dc = dyv * bg
    dcp_row = _last_row(dyp_ref) * _last_row(bgp_ref) * has_prev
    dcn_row = _first_row(dyn_ref) * _first_row(bgn_ref) * has_next
    dcm1, dcp1 = _shift_rows(dc, dcp_row, dcn_row)
    du = dcp1 * w[0:1, :] + dc * w[1:2, :] + dcm1 * w[2:3, :]
    o_ref[:, 0:D] = (dyv * conv).astype(BF16)
    o_ref[:, D:2 * D] = (du * vc * lat).astype(BF16)
    o_ref[:, 2 * D:3 * D] = (du * cg * lat).astype(BF16)
    _acc_add(acc_ref, 0, dc * um1)
    _acc_add(acc_ref, 1, dc * u)
    _acc_add(acc_ref, 2, dc * up1)


def _rope_tables(ctx_len, seq):
    n_freq = HEAD_DIM // 4
    rows = seq // GRID_W
    inv = ROPE_THETA ** (-jnp.arange(n_freq, dtype=F32) / n_freq)
    ar = jnp.arange(rows, dtype=F32)[:, None] * inv
    ac = jnp.arange(GRID_W, dtype=F32)[:, None] * inv

    def per_row(a):
        return jnp.repeat(a, GRID_W, axis=0)

    def per_col(a):
        return jnp.tile(a, (rows, 1))

    cos_t = jnp.concatenate([per_row(jnp.cos(ar)), per_row(jnp.cos(ar)), per_col(jnp.cos(ac)), per_col(jnp.cos(ac))], axis=1)
    sin_t = jnp.concatenate([per_row(-jnp.sin(ar)), per_row(jnp.sin(ar)), per_col(-jnp.sin(ac)), per_col(jnp.sin(ac))], axis=1)
    cos_t = jnp.concatenate([jnp.ones((ctx_len, HEAD_DIM), F32), cos_t], axis=0)
    sin_t = jnp.concatenate([jnp.zeros((ctx_len, HEAD_DIM), F32), sin_t], axis=0)
    return cos_t, sin_t


def _swap_halves(y):
    lanes = lax.broadcasted_iota(jnp.int32, y.shape, 1)
    first = (lanes % 64) < 32
    return jnp.where(first, pltpu.roll(y, HEAD_DIM - 32, 1), pltpu.roll(y, 32, 1))


def _to_row(col, n):
    return jnp.transpose(jnp.broadcast_to(col, (n, HEAD_DIM)))[0:1, :]


LOG2E = 1.4426950408889634
ATTN_PART_LANES = 256
ATTN_QUERY_ROWS = 768
ATTN_VMEM_LIMIT = 60 * 1024 * 1024


def _flash_fwd(q, k, v, name, tq=None, tk=None):
    T = q.shape[0]
    tq = tq or _pick(T, (ATTN_QUERY_ROWS, ROW))
    parts = GROUP * tq // ATTN_PART_LANES
    tk = tk or _pick(T, (2816, 1408, 768, 512, 256))
    ck = tk
    nk = T // tk
    GW = GROUP * HEAD_DIM

    def body(q_ref, k_ref, v_ref, o_ref, lse_ref, qs_ref, m_ref, l_ref, acc_ref, st_ref):
        ki = pl.program_id(2)

        @pl.when(ki == 0)
        def _():
            for g in range(GROUP):
                qs_ref[g * tq:(g + 1) * tq, :] = q_ref[:, g * HEAD_DIM:(g + 1) * HEAD_DIM]
            m_ref[...] = jnp.full(m_ref.shape, -jnp.inf, F32)
            l_ref[...] = jnp.zeros(l_ref.shape, F32)
            acc_ref[...] = jnp.zeros(acc_ref.shape, F32)

        w = ATTN_PART_LANES
        nck = tk // ck

        def lanes(p):
            return slice(p * w, (p + 1) * w)

        def keys(c):
            return slice(c * ck, (c + 1) * ck)

        def fold(a):
            return a.reshape(ck // 8, 8, w)

        def scores(p, c):
            st = lax.dot_general(k_ref[keys(c), :], qs_ref[lanes(p), :], _NT,
                                 preferred_element_type=F32) * (ATTN_SCALE * LOG2E)
            st_ref[keys(c), lanes(p)] = st
            return jnp.max(fold(st), axis=0)

        def new_max(p, partial):
            m_prev = m_ref[:, lanes(p)]
            m_new = jnp.maximum(m_prev, jnp.max(functools.reduce(jnp.maximum, partial), axis=0, keepdims=True))
            m_ref[:, lanes(p)] = m_new
            return m_new, jnp.exp2(m_prev - m_new)

        def weights(p, c, m_new):
            pt = jnp.exp2(st_ref[keys(c), lanes(p)] - m_new)
            pv = lax.dot_general(v_ref[keys(c), :], pt.astype(BF16), (((0,), (0,)), ((), ())),
                                 preferred_element_type=F32)
            return jnp.sum(fold(pt), axis=0), pv

        partial = [scores(0, c) for c in range(nck)]
        for p in range(parts):
            m_new, alpha = new_max(p, partial)
            partial, sums, pvs = [], [], []
            for c in range(nck):
                if p + 1 < parts:
                    partial.append(scores(p + 1, c))
                s8, pv = weights(p, c, m_new)
                sums.append(s8)
                pvs.append(pv)
            l_ref[:, lanes(p)] = alpha * l_ref[:, lanes(p)] + jnp.sum(sum(sums), axis=0, keepdims=True)
            acc_ref[:, lanes(p)] = alpha * acc_ref[:, lanes(p)] + sum(pvs)

        @pl.when(ki == nk - 1)
        def _():
            out = jnp.transpose(acc_ref[...] / l_ref[...])
            lse = m_ref[...] + jnp.log2(l_ref[...])
            for g in range(GROUP):
                o_ref[:, g * HEAD_DIM:(g + 1) * HEAD_DIM] = out[g * tq:(g + 1) * tq, :]
                lse_ref[0, g:g + 1, :] = lse[:, g * tq:(g + 1) * tq]

    return pl.pallas_call(
        body, name=name, grid=(N_KV_HEADS, T // tq, nk),
        in_specs=[pl.BlockSpec((tq, GW), lambda h, i, j: (i, h)),
                  pl.BlockSpec((tk, HEAD_DIM), lambda h, i, j: (j, h)),
                  pl.BlockSpec((tk, HEAD_DIM), lambda h, i, j: (j, h))],
        out_specs=[pl.BlockSpec((tq, GW), lambda h, i, j: (i, h)),
                   pl.BlockSpec((1, GROUP, tq), lambda h, i, j: (h, 0, i))],
        out_shape=[jax.ShapeDtypeStruct((T, N_Q_HEADS * HEAD_DIM), F32),
                   jax.ShapeDtypeStruct((N_KV_HEADS, GROUP, T), F32)],
        scratch_shapes=[pltpu.VMEM((GROUP * tq, HEAD_DIM), BF16), pltpu.VMEM((1, GROUP * tq), F32),
                        pltpu.VMEM((1, GROUP * tq), F32), pltpu.VMEM((HEAD_DIM, GROUP * tq), F32),
                        pltpu.VMEM((tk, GROUP * tq), F32)],
        compiler_params=pltpu.CompilerParams(dimension_semantics=("parallel", "parallel", "arbitrary"),
                                             vmem_limit_bytes=ATTN_VMEM_LIMIT),
    )(q, k, v)


def _flash_bwd(q, k, v, do, lse, delta, name, tq=None, tk=None, token=None):
    T = q.shape[0]
    tq = tq or _pick(T, (ATTN_QUERY_ROWS, ROW))
    tk = tk or _pick(T, (1408, 768, 512, 256))
    nk = T // tk
    GW = GROUP * HEAD_DIM
    nt = (((1,), (1,)), ((), ()))
    extra = [] if token is None else [token]

    def body(q_ref, do_ref, k_ref, v_ref, lse_ref, dl_ref, *rest):
        dq_ref, dk_ref, dv_ref, qs_ref, dos_ref, dqt_ref = rest[len(extra):]
        qi = pl.program_id(1)
        ki = pl.program_id(2)

        @pl.when(ki == 0)
        def _():
            for g in range(GROUP):
                qs_ref[g * tq:(g + 1) * tq, :] = q_ref[:, g * HEAD_DIM:(g + 1) * HEAD_DIM]
                dos_ref[g * tq:(g + 1) * tq, :] = do_ref[:, g * HEAD_DIM:(g + 1) * HEAD_DIM]
            dqt_ref[...] = jnp.zeros(dqt_ref.shape, F32)

        kk = k_ref[...]
        vv = v_ref[...]

        def lanes(p):
            return slice(p * tq, (p + 1) * tq)

        def products(p):
            st = lax.dot_general(kk, qs_ref[lanes(p), :], nt, preferred_element_type=F32)
            dpt = lax.dot_general(vv, dos_ref[lanes(p), :], nt, preferred_element_type=F32)
            return st, dpt

        dk_c = jnp.zeros((tk, HEAD_DIM), F32)
        dv_c = jnp.zeros((tk, HEAD_DIM), F32)
        ahead = products(0)
        for p in range(GROUP):
            st, dpt = ahead
            if p + 1 < GROUP:
                ahead = products(p + 1)
            pt = jnp.exp2(st * (ATTN_SCALE * LOG2E) - lse_ref[0, p:p + 1, :])
            dst = ((pt * (dpt - dl_ref[0, p:p + 1, :])) * ATTN_SCALE).astype(BF16)
            dv_c = dv_c + jnp.dot(pt.astype(BF16), dos_ref[lanes(p), :], preferred_element_type=F32)
            dk_c = dk_c + jnp.dot(dst, qs_ref[lanes(p), :], preferred_element_type=F32)
            dqt_ref[:, lanes(p)] += lax.dot_general(kk, dst, (((0,), (0,)), ((), ())), preferred_element_type=F32)
        rows = pl.ds(pl.multiple_of(ki * tk, tk), tk)

        @pl.when(qi == 0)
        def _():
            dk_ref[rows, :] = dk_c
            dv_ref[rows, :] = dv_c

        @pl.when(qi > 0)
        def _():
            dk_ref[rows, :] += dk_c
            dv_ref[rows, :] += dv_c

        @pl.when(ki == nk - 1)
        def _():
            dqv = jnp.transpose(dqt_ref[...])
            for g in range(GROUP):
                dq_ref[:, g * HEAD_DIM:(g + 1) * HEAD_DIM] = dqv[g * tq:(g + 1) * tq, :]

    return pl.pallas_call(
        body, name=name, grid=(N_KV_HEADS, T // tq, nk),
        in_specs=[pl.BlockSpec((tq, GW), lambda h, i, j: (i, h)),
                  pl.BlockSpec((tq, GW), lambda h, i, j: (i, h)),
                  pl.BlockSpec((tk, HEAD_DIM), lambda h, i, j: (j, h)),
                  pl.BlockSpec((tk, HEAD_DIM), lambda h, i, j: (j, h)),
                  pl.BlockSpec((1, GROUP, tq), lambda h, i, j: (h, 0, i)),
                  pl.BlockSpec((1, GROUP, tq), lambda h, i, j: (h, 0, i))] +
                 [pl.BlockSpec(t.shape, lambda h, i, j: (0, 0)) for t in extra],
        out_specs=[pl.BlockSpec((tq, GW), lambda h, i, j: (i, h)),
                   pl.BlockSpec((T, HEAD_DIM), lambda h, i, j: (0, h)),
                   pl.BlockSpec((T, HEAD_DIM), lambda h, i, j: (0, h))],
        out_shape=[jax.ShapeDtypeStruct((T, N_Q_HEADS * HEAD_DIM), F32),
                   jax.ShapeDtypeStruct((T, N_KV_HEADS * HEAD_DIM), F32),
                   jax.ShapeDtypeStruct((T, N_KV_HEADS * HEAD_DIM), F32)],
        scratch_shapes=[pltpu.VMEM((GROUP * tq, HEAD_DIM), BF16), pltpu.VMEM((GROUP * tq, HEAD_DIM), BF16),
                        pltpu.VMEM((HEAD_DIM, GROUP * tq), F32)],
        compiler_params=pltpu.CompilerParams(dimension_semantics=("arbitrary", "arbitrary", "arbitrary"),
                                             vmem_limit_bytes=ATTN_VMEM_LIMIT),
    )(q, do, k, v, lse, delta, *extra)


def _gate_specs(D):
    w = D // 2
    first = (3 * D + (N_Q_HEADS + 2 * N_KV_HEADS) * HEAD_DIM) // w
    return [pl.BlockSpec((ROW, w), lambda i, c=first + j: (i, c)) for j in range(4)]


def _merge_fwd(o, P, conv_w, wbc, wba, wo, D, name):
    T = o.shape[0]
    nt = T // ROW
    w = D // 2
    conv_specs, conv_args = _conv_fwd_operands(P, conv_w, D)
    nc = len(conv_args)

    def body(*refs):
        o_ref, g0, g1, g2, g3, wbc_ref, wba_ref, wo_ref, yc_ref, a1_ref, a2_ref, z_ref, mo_ref = refs[nc:]
        yc_ref[...] = _conv_tile_fwd(refs[:nc], nt)
        a1 = jnp.dot(yc_ref[...], wbc_ref[...], preferred_element_type=F32)
        a2 = jnp.dot(o_ref[...].astype(BF16), wba_ref[...], preferred_element_type=F32)
        a1_ref[...] = a1
        a2_ref[...] = a2
        for j, (gc, ga) in enumerate(((g0, g2), (g1, g3))):
            sl = slice(j * w, (j + 1) * w)
            z = jax.nn.sigmoid(_f32(gc)) * a1[:, sl] + jax.nn.sigmoid(_f32(ga)) * a2[:, sl]
            z_ref[:, sl] = z.astype(BF16)
        mo_ref[...] = jnp.dot(z_ref[...], wo_ref[...], preferred_element_type=F32)

    return pl.pallas_call(
        body, name=name, grid=(T // ROW,),
        in_specs=conv_specs + [_row_spec(D)] + _gate_specs(D) + [_resident()] * 3,
        out_specs=[_row_spec(D)] * 5,
        out_shape=[jax.ShapeDtypeStruct((T, D), BF16), jax.ShapeDtypeStruct((T, D), F32),
                   jax.ShapeDtypeStruct((T, D), F32), jax.ShapeDtypeStruct((T, D), BF16),
                   jax.ShapeDtypeStruct((T, D), F32)],
        compiler_params=_params(("parallel",)),
    )(*conv_args, o, P, P, P, P, wbc, wba, wo)


def _merge_bwd(dmo, a1, a2, o, P, wbc, wba, wo, D, name):
    T = a1.shape[0]
    w = D // 2

    def body(dmo_ref, a1_ref, a2_ref, o_ref, g0, g1, g2, g3, wbc_ref, wba_ref, wo_ref,
             d1_ref, d2_ref, dg_ref, dyc_ref, dob_ref, dl_ref):
        dz = lax.dot_general(dmo_ref[...], wo_ref[...], _NT, preferred_element_type=F32)
        for j, (gc, ga) in enumerate(((g0, g2), (g1, g3))):
            sl = slice(j * w, (j + 1) * w)
            dzs = dz[:, sl]
            sc = jax.nn.sigmoid(_f32(gc))
            sa = jax.nn.sigmoid(_f32(ga))
            d1_ref[:, sl] = (dzs * sc).astype(BF16)
            d2_ref[:, sl] = (dzs * sa).astype(BF16)
            dg_ref[:, j * w:(j + 1) * w] = (dzs * a1_ref[:, sl] * (sc * (1.0 - sc))).astype(BF16)
            dg_ref[:, D + j * w:D + (j + 1) * w] = (dzs * a2_ref[:, sl] * (sa * (1.0 - sa))).astype(BF16)
        dyc_ref[...] = lax.dot_general(d1_ref[...], wbc_ref[...], _NT, preferred_element_type=F32)
        dov = lax.dot_general(d2_ref[...], wba_ref[...], _NT, preferred_element_type=F32)
        dob_ref[...] = dov.astype(BF16)
        prod = dov * o_ref[...]
        for h in range(N_Q_HEADS):
            d = jnp.sum(prod[:, h * HEAD_DIM:(h + 1) * HEAD_DIM], axis=1, keepdims=True)
            dl_ref[h // GROUP, (h % GROUP):(h % GROUP) + 1, :] = _to_row(d, ROW)

    return pl.pallas_call(
        body, name=name, grid=(T // ROW,),
        in_specs=[_row_spec(D)] * 4 + _gate_specs(D) + [_resident()] * 3,
        out_specs=[_row_spec(D), _row_spec(D), _row_spec(2 * D), _row_spec(D), _row_spec(D),
                   pl.BlockSpec((N_KV_HEADS, GROUP, ROW), lambda i: (0, 0, i))],
        out_shape=[jax.ShapeDtypeStruct((T, D), BF16), jax.ShapeDtypeStruct((T, D), BF16),
                   jax.ShapeDtypeStruct((T, 2 * D), BF16), jax.ShapeDtypeStruct((T, D), F32),
                   jax.ShapeDtypeStruct((T, D), BF16), jax.ShapeDtypeStruct((N_KV_HEADS, GROUP, T), F32)],
        compiler_params=_params(("parallel",)),
    )(dmo, a1, a2, o, P, P, P, P, wbc, wba, wo)


def _adamw_math(w, g, m, v):
    m = ADAM_B1 * m + (1.0 - ADAM_B1) * g
    v = ADAM_B2 * v + (1.0 - ADAM_B2) * (g * g)
    m_hat = m / (1.0 - ADAM_B1 ** ADAM_STEP)
    v_hat = v / (1.0 - ADAM_B2 ** ADAM_STEP)
    delta = -ADAM_LR * (m_hat / (jnp.sqrt(v_hat) + ADAM_EPS) + ADAM_WD * w)
    return delta, m, v


def _adamw(w, g, m, v, name):
    R, C = w.shape
    tr = _pick(R, tuple(t for t in (256, 128, 64, 32, 16, 8) if t * C * 4 <= ADAMW_BLOCK_BYTES))

    def body(w_ref, g_ref, m_ref, v_ref, d_ref, mo_ref, vo_ref):
        d, mn, vn = _adamw_math(w_ref[...], g_ref[...], m_ref[...], v_ref[...])
        d_ref[...] = d
        mo_ref[...] = mn
        vo_ref[...] = vn

    spec = pl.BlockSpec((tr, C), lambda i: (i, 0))
    return pl.pallas_call(
        body, name=name, grid=(R // tr,),
        in_specs=[spec] * 4, out_specs=[spec] * 3,
        out_shape=[jax.ShapeDtypeStruct((R, C), F32)] * 3,
        compiler_params=_params(("parallel",)),
    )(w, g, m, v)


def _norm_mix_in_fwd(xprev, branch, mods, g, gate, shift_idx, scale_idx, w_t, gq, gk, cos_t, sin_t, name):
    x_specs, x_args, (T, D) = _rows_operand(xprev)
    N = w_t.shape[0]
    QW = N_Q_HEADS * HEAD_DIM
    KW = N_KV_HEADS * HEAD_DIM
    q0, k0, v0 = 3 * D, 3 * D + QW, 3 * D + QW + KW
    edges = [0, D, 2 * D, q0, k0, v0 + KW] + list(range(v0 + KW + D, N + 1, D))
    assert edges[-1] == N

    def body(*refs):
        f_ref, m_ref, g_ref, w_ref, gq_ref, gk_ref, c_ref, s_ref = refs[len(x_args):len(x_args) + 8]
        xo_ref, h_ref, p_ref, qo_ref, ko_ref, vo_ref = refs[len(x_args) + 8:]
        m = m_ref[0]
        gate_idx, fac = gate
        x = _rows_tile(refs[:len(x_args)]) + (fac * m[gate_idx:gate_idx + 1, :]) * f_ref[...]
        xo_ref[...] = x
        hv = _norm_tile_fwd(x, m, g_ref[...], shift_idx, scale_idx)
        h_ref[...] = hv
        c = c_ref[...]
        s = s_ref[...]

        def head(xh, gain):
            inv = lax.rsqrt(jnp.mean(xh * xh, axis=-1, keepdims=True) + EPS)
            y = (xh * inv) * gain
            return y * c + _swap_halves(y) * s

        for lo, hi in zip(edges[:-1], edges[1:]):
            pb = lax.dot_general(hv, w_ref[lo:hi, :], _NT, preferred_element_type=F32).astype(BF16)
            p_ref[:, lo:hi] = pb
            if lo == q0:
                for h in range(N_Q_HEADS):
                    sl = slice(h * HEAD_DIM, (h + 1) * HEAD_DIM)
                    qo_ref[:, sl] = head(pb[:, sl].astype(F32), gq_ref[...]).astype(BF16)
            elif lo == k0:
                for h in range(N_KV_HEADS):
                    sl = slice(h * HEAD_DIM, (h + 1) * HEAD_DIM)
                    ko_ref[:, sl] = head(pb[:, sl].astype(F32), gk_ref[...]).astype(BF16)
                vo_ref[...] = pb[:, KW:2 * KW]

    return pl.pallas_call(
        body, name=name, grid=(T // ROW,),
        in_specs=x_specs + [_row_spec(D), _mods_spec(D), _vec_spec(1, D), _resident(),
                            _vec_spec(1, HEAD_DIM), _vec_spec(1, HEAD_DIM), _row_spec(HEAD_DIM), _row_spec(HEAD_DIM)],
        out_specs=[_row_spec(D), _row_spec(D), _row_spec(N), _row_spec(QW), _row_spec(KW), _row_spec(KW)],
        out_shape=[jax.ShapeDtypeStruct((T, D), F32), jax.ShapeDtypeStruct((T, D), BF16),
                   jax.ShapeDtypeStruct((T, N), BF16), jax.ShapeDtypeStruct((T, QW), BF16),
                   jax.ShapeDtypeStruct((T, KW), BF16), jax.ShapeDtypeStruct((T, KW), BF16)],
        compiler_params=_params(("parallel",)),
    )(*x_args, branch, mods, g, w_t, gq, gk, cos_t, sin_t)


def _mix_in_norm_bwd(dyc, dgt, P, conv_w, dq, dk, dv, gq, gk, cos_t, sin_t, w_t, x, dres, mods, g, shift_idx,
                     scale_idx, gate, branch, name):
    T, D = x.shape
    nt = T // ROW
    QW = N_Q_HEADS * HEAD_DIM
    KW = N_KV_HEADS * HEAD_DIM
    q0, g0 = 3 * D, 3 * D + QW + 2 * KW
    assert g0 + dgt.shape[1] == w_t.shape[0]
    conv_specs, conv_args = _conv_bwd_operands(P, dyc, conv_w, D)
    nc = len(conv_args)

    def body(*refs):
        (dg_ref, q_ref, k_ref, dq_ref, dk_ref, dv_ref, gq_ref, gk_ref, c_ref, s_ref,
         w_ref, x_ref, dr_ref, b_ref, m_ref, g_ref,
         dx_ref, db_ref, acc_ref, dc_ref, cacc_ref, o_ref, qacc_ref) = refs[nc:]
        _acc_init(acc_ref)
        _acc_init(cacc_ref)
        _acc_init(qacc_ref)
        _conv_tile_bwd(refs[:nc], dc_ref, cacc_ref, D, nt)
        dh = jnp.dot(dc_ref[...], w_ref[0:q0, :], preferred_element_type=F32)
        c = c_ref[...]
        s = s_ref[...]

        def head(xh, d, gain):
            dyv = d * c + _swap_halves(d * s)
            inv = lax.rsqrt(jnp.mean(xh * xh, axis=-1, keepdims=True) + EPS)
            xn = xh * inv
            dxn = dyv * gain
            dxh = inv * (dxn - xn * jnp.mean(dxn * xn, axis=-1, keepdims=True))
            return dxh, jnp.sum(dyv * xn, axis=0, keepdims=True)

        dgq = jnp.zeros((1, HEAD_DIM), F32)
        for h in range(N_Q_HEADS):
            sl = slice(h * HEAD_DIM, (h + 1) * HEAD_DIM)
            dxh, dgh = head(q_ref[:, sl].astype(F32), dq_ref[:, sl], gq_ref[...])
            o_ref[:, sl] = dxh.astype(BF16)
            dgq = dgq + dgh
        dh = dh + jnp.dot(dg_ref[...], w_ref[g0:, :], preferred_element_type=F32)
        dgk = jnp.zeros((1, HEAD_DIM), F32)
        for h in range(N_KV_HEADS):
            sl = slice(h * HEAD_DIM, (h + 1) * HEAD_DIM)
            dxh, dgh = head(k_ref[:, sl].astype(F32), dk_ref[:, sl], gk_ref[...])
            o_ref[:, QW + h * HEAD_DIM:QW + (h + 1) * HEAD_DIM] = dxh.astype(BF16)
            dgk = dgk + dgh
        o_ref[:, QW + KW:QW + 2 * KW] = dv_ref[...].astype(BF16)
        qacc_ref[0, 0:1, 0:HEAD_DIM] += dgq
        qacc_ref[0, 1:2, 0:HEAD_DIM] += dgk
        dh = dh + jnp.dot(o_ref[...], w_ref[q0:g0, :], preferred_element_type=F32)
        m = m_ref[0]
        dx = _norm_tile_bwd(x_ref[...], dh, dr_ref[...], m, g_ref[...], shift_idx, scale_idx, acc_ref)
        dx_ref[...] = dx
        db_ref[...] = _gate_tile_bwd(dx, b_ref[...], m, gate, acc_ref)

    return pl.pallas_call(
        body, name=name, grid=(T // ROW,),
        in_specs=conv_specs +
                 [_row_spec(dgt.shape[1]), _row_spec(QW, q0 // QW), _row_spec(KW, (q0 + QW) // KW),
                  _row_spec(QW), _row_spec(KW), _row_spec(KW), _vec_spec(1, HEAD_DIM), _vec_spec(1, HEAD_DIM),
                  _row_spec(HEAD_DIM), _row_spec(HEAD_DIM),
                  _resident(), _row_spec(D), _row_spec(D), _row_spec(D), _mods_spec(D), _vec_spec(1, D)],
        out_specs=[_row_spec(D), _row_spec(D), _acc_spec(D), _row_spec(q0), _acc_spec(D),
                   _row_spec(QW + 2 * KW), _acc_spec(D)],
        out_shape=[jax.ShapeDtypeStruct((T, D), F32), jax.ShapeDtypeStruct((T, D), BF16),
                   jax.ShapeDtypeStruct((2, ACC_ROWS, D), F32), jax.ShapeDtypeStruct((T, q0), BF16),
                   jax.ShapeDtypeStruct((2, ACC_ROWS, D), F32), jax.ShapeDtypeStruct((T, QW + 2 * KW), BF16),
                   jax.ShapeDtypeStruct((2, ACC_ROWS, D), F32)],
        compiler_params=_params(("arbitrary",)),
    )(*conv_args, dgt, P, P, dq, dk, dv, gq, gk, cos_t, sin_t, w_t, x, dres, branch, mods, g)


def _adamw_transposed(w, gt, m, v, name):
    R, C = w.shape
    tc = 128

    def body(w_ref, g_ref, m_ref, v_ref, go_ref, d_ref, mo_ref, vo_ref):
        g = jnp.transpose(g_ref[...])
        d, mn, vn = _adamw_math(w_ref[...], g, m_ref[...], v_ref[...])
        go_ref[...] = g
        d_ref[...] = d
        mo_ref[...] = mn
        vo_ref[...] = vn

    spec = pl.BlockSpec((R, tc), lambda j: (0, j))
    return pl.pallas_call(
        body, name=name, grid=(C // tc,),
        in_specs=[spec, pl.BlockSpec((tc, R), lambda j: (j, 0)), spec, spec], out_specs=[spec] * 4,
        out_shape=[jax.ShapeDtypeStruct((R, C), F32)] * 4,
        compiler_params=_params(("parallel",)),
    )(w, gt, m, v)


class _NoExchange:
    def __init__(self, rest):
        self.rest = rest

    def rest_weights(self, after):
        return self.rest

    def reduce_early(self, grads, tag):
        return None


def _local_step(xcat, target, mods, norm_g, final_g, gq, gk, conv_w, ffn1_w, hooks, ctx_len):
    T, D = _rows_operand(xcat)[2]
    w1i, w1o = ffn1_w
    g1, g2, g3 = norm_g
    cos_t, sin_t = _rope_tables(ctx_len, T - ctx_len)

    def after(value, token, name):
        return value if token is None else _after(value, token, name)

    _, h1, u1, s1, f1 = _norm_ffn_fwd(xcat, None, mods, g1, None, 0, 1, w1i, w1o, "f_ffn1")
    wi, wbc, wba, wo, w2i, w2o = hooks.rest_weights(f1)
    x1, h2, P, qn, kn, vb = _norm_mix_in_fwd(xcat, f1, mods, g2, (2, 0.5), 3, 4, wi, gq, gk, cos_t, sin_t, "f_mix_in")
    o, lse = _flash_fwd(qn, kn, vb, "f_attn")
    yc, a1, a2, z, mo = _merge_fwd(o, P, conv_w, wbc, wba, wo, D, "f_merge")
    x2, h3, u2, s2, dx3, df2, acc_head = _norm_ffn_fwd(x1, mo, mods, g3, (5, 1.0), 6, 7, w2i, w2o, "f_ffn2",
                                                       head=(final_g, target))

    du2, dx2, dmo, acc_n3 = _ffn_norm_bwd(df2, u2, w2i, w2o, x2, dx3, mods, g3, 6, 7, (5, 1.0), mo, "b_ffn2")
    g_w2o = _matmul(s2, df2, "tn", BF16, "b_ffn2_out_dw")
    g_w2i = _matmul(du2, h3, "tn", BF16, "b_ffn2_in_dw")

    g_wo = _matmul(z, dmo, "tn", BF16, "b_mix_out_dw")
    da1, da2, dgt, dyc, dob, delta = _merge_bwd(dmo, a1, a2, o, P, wbc, wba, wo, D, "b_merge")
    g_wbc = _matmul(yc, da1, "tn", BF16, "b_branch_conv_dw")
    g_wba = _matmul(o, da2, "tn", BF16, "b_branch_attn_dw")
    token_a = hooks.reduce_early([g_wbc, g_wba, g_wo, g_w2i, g_w2o], "a")
    dq, dk, dv = _flash_bwd(qn, kn, vb, dob, lse, delta, "b_attn", token=token_a)
    dx1, df1, acc_n2, dconv, acc_conv, dqkv, acc_qk = _mix_in_norm_bwd(
        dyc, dgt, P, conv_w, dq, dk, dv, gq, gk, cos_t, sin_t, wi, x1, dx2, mods, g2, 3, 4, (2, 0.5), f1, "b_mix_in")
    d_parts = (dconv, dqkv, dgt)
    g_wi = jnp.concatenate([_matmul(dp, h2, "tn", BF16, f"b_mix_in_dw_{i}") for i, dp in enumerate(d_parts)], axis=0)
    g1_b = after(g1, hooks.reduce_early([g_wi], "b"), "after_rs_b")

    du1, grad_x, _, acc_n1 = _ffn_norm_bwd(df1, u1, w1i, w1o, xcat, dx1, mods, g1_b, 0, 1, None, None, "b_ffn1",
                                           skip_first_tile=True)
    g_w1o = _matmul(s1, df1, "tn", BF16, "b_ffn1_out_dw")
    g_w1i = _matmul(du1, h1, "tn", BF16, "b_ffn1_in_dw", token=hooks.reduce_early([g_w1o], "c"))

    grads = (g_w1i, g_w1o, g_wi, g_wbc, g_wba, g_wo, g_w2i, g_w2o)
    accs = (acc_head, acc_n3, acc_n2, acc_n1, acc_conv, acc_qk)
    return grad_x, grads, accs


def _place():
    return lax.axis_index("x"), lax.axis_index("y"), lax.axis_index("c")


def _other_chips(x, y):
    return [(1 - x, y), (x, 1 - y), (1 - x, 1 - y)]


def _allgather8(v, name):
    R, N = v.shape

    def body(v_ref, out_ref, send_sems, recv_sems, local_sem):
        x, y, c = _place()
        me, sibling = (x, y, c), (x, y, 1 - c)
        chips = _other_chips(x, y)

        def blk(px, py, pc):
            return out_ref.at[4 * px + 2 * py + pc]

        def copy(k, block, to, src=None):
            return pltpu.make_async_remote_copy(
                src_ref=blk(*block) if src is None else src, dst_ref=blk(*block),
                send_sem=send_sems.at[k], recv_sem=recv_sems.at[k], device_id=to, device_id_type=MESH)

        mine = pltpu.make_async_copy(v_ref, blk(*me), local_sem)
        mine.start()
        first = [copy(0, me, sibling, src=v_ref)]
        first += [copy(1 + j, me, (*chip, c), src=v_ref) for j, chip in enumerate(chips)]
        for cp in first:
            cp.start()
        passed = [copy(4 + j, (*chip, c), sibling) for j, chip in enumerate(chips)]
        for j, chip in enumerate(chips):
            copy(1 + j, (*chip, c), me).wait_recv()
            passed[j].start()
        copy(0, sibling, me).wait_recv()
        for j, chip in enumerate(chips):
            copy(4 + j, (*chip, 1 - c), me).wait_recv()
        for cp in first + passed:
            cp.wait_send()
        mine.wait()

    return pl.pallas_call(
        body, name=name,
        out_shape=jax.ShapeDtypeStruct((N_DEV, R, N), v.dtype),
        in_specs=[pl.BlockSpec(memory_space=pltpu.VMEM)],
        out_specs=pl.BlockSpec(memory_space=pltpu.VMEM),
        scratch_shapes=[pltpu.SemaphoreType.DMA((7,)), pltpu.SemaphoreType.DMA((7,)), pltpu.SemaphoreType.DMA],
        compiler_params=pltpu.CompilerParams(vmem_limit_bytes=VMEM_LIMIT),
    )(v)


def _any_specs(n):
    return [pl.BlockSpec(memory_space=pl.ANY)] * n


def _pair_exchange(grads, name):
    n = len(grads)

    def body(*refs):
        g, land = refs[:n], refs[n:2 * n]
        send_sems, recv_sems = refs[2 * n:]
        x, y, c = _place()
        sibling = (x, y, 1 - c)
        copies = []
        for t in range(n):
            half = grads[t].shape[0] // (2 * N_CHIPS)
            for s in range(N_CHIPS):
                cp = pltpu.make_async_remote_copy(
                    src_ref=g[t].at[pl.ds((2 * s + 1 - c) * half, half), :], dst_ref=land[t].at[s],
                    send_sem=send_sems.at[N_CHIPS * t + s], recv_sem=recv_sems.at[N_CHIPS * t + s],
                    device_id=sibling, device_id_type=MESH)
                cp.start()
                copies.append(cp)
        for cp in copies:
            cp.wait_recv()
        for cp in copies:
            cp.wait_send()

    return pl.pallas_call(
        body, name=name,
        out_shape=[jax.ShapeDtypeStruct((N_CHIPS, a.shape[0] // (2 * N_CHIPS), a.shape[1]), a.dtype) for a in grads],
        in_specs=_any_specs(n), out_specs=_any_specs(n),
        scratch_shapes=[pltpu.SemaphoreType.DMA((N_CHIPS * n,)), pltpu.SemaphoreType.DMA((N_CHIPS * n,))],
    )(*grads)


def _place_shard(w2, idx, transpose, name, token):
    if transpose:
        D, rs = w2.shape
        tr = 128
        in_spec = pl.BlockSpec((D, tr), lambda i, idx: (0, i))
    else:
        rs, D = w2.shape
        tr = _pick(rs, (352, 256, 128, 64, 32, 16))
        in_spec = pl.BlockSpec((tr, D), lambda i, idx: (i, 0))
    steps = rs // tr

    def body(idx_ref, w_ref, t_ref, o_ref):
        v = w_ref[...]
        o_ref[...] = (jnp.transpose(v) if transpose else v).astype(BF16)

    return pl.pallas_call(
        body, name=name,
        grid_spec=pltpu.PrefetchScalarGridSpec(
            num_scalar_prefetch=1, grid=(steps,),
            in_specs=[in_spec, pl.BlockSpec(token.shape, lambda i, idx: (0, 0))],
            out_specs=pl.BlockSpec((tr, D), lambda i, idx: (idx[1] * steps + i, 0))),
        out_shape=jax.ShapeDtypeStruct((N_CHIPS * rs, D), BF16),
        compiler_params=_params(("arbitrary",)),
    )(idx, w2, token)


def _pair_sum(g, landed, idx, name, token=None):
    _, half, D = landed.shape
    g4 = g.reshape(N_CHIPS, 2, half, D)
    tr = _pick(half, (416, 352, 128))
    extra = [] if token is None else [token]

    def body(idx_ref, g_ref, l_ref, *rest):
        rest[-1][...] = (g_ref[0].astype(F32) + l_ref[...].astype(F32)).astype(BF16)

    return pl.pallas_call(
        body, name=name,
        grid_spec=pltpu.PrefetchScalarGridSpec(
            num_scalar_prefetch=1, grid=(N_CHIPS, half // tr),
            in_specs=[pl.BlockSpec((1, 1, tr, D), lambda s, i, idx: (idx[1 + s], idx[0], i, 0)),
                      pl.BlockSpec((1, tr, D), lambda s, i, idx: (idx[1 + s], i, 0))] +
                     [pl.BlockSpec(t.shape, lambda s, i, idx: (0, 0)) for t in extra],
            out_specs=pl.BlockSpec((1, tr, D), lambda s, i, idx: (s, i, 0))),
        out_shape=jax.ShapeDtypeStruct((N_CHIPS, half, D), BF16),
        compiler_params=_params(("arbitrary", "arbitrary")),
    )(idx, g4, landed, *extra)


_HBM = pl.BlockSpec(memory_space=pltpu.HBM)
_SEM = pl.BlockSpec(memory_space=pltpu.SEMAPHORE)
_EFFECT = pltpu.SideEffectType.DATAFLOW_SIDE_EFFECTING


def _in_hbm(a):
    return pltpu.with_memory_space_constraint(a, pltpu.HBM)


def _split_copies(n, per, make):
    def start(nbuf, name, bufs):
        def body(*refs):
            ins = refs[:nbuf]
            send_sems, recv_sems = refs[nbuf], refs[nbuf + 1]
            token = refs[-1]
            for t in range(n):
                for j in range(per):
                    make(ins, t, j, send_sems.at[per * t + j], recv_sems.at[per * t + j]).start()
            token[...] = jnp.zeros(token.shape, token.dtype)

        out = pl.pallas_call(
            body, name=name,
            out_shape=(pltpu.SemaphoreType.DMA((per * n,)), pltpu.SemaphoreType.DMA((per * n,)),
                       *[pltpu.HBM(b.shape, b.dtype) for b in bufs], jax.ShapeDtypeStruct((8, 128), F32)),
            in_specs=[_HBM] * nbuf,
            out_specs=(_SEM, _SEM, *[_HBM] * nbuf, pl.BlockSpec(memory_space=pltpu.VMEM)),
            input_output_aliases={i: 2 + i for i in range(nbuf)},
            compiler_params=pltpu.CompilerParams(has_side_effects=_EFFECT),
        )(*[_in_hbm(b) for b in bufs])
        return out[0], out[1], list(out[2:2 + nbuf]), out[-1]

    def wait(nbuf, name, send_sems, recv_sems, bufs, after):
        def body(*refs):
            ins = refs[:nbuf]
            ss, rs = refs[nbuf], refs[nbuf + 1]
            for t in range(n):
                for j in range(per):
                    cp = make(ins, t, j, ss.at[per * t + j], rs.at[per * t + j])
                    cp.wait_send()
                    cp.wait_recv()

        return pl.pallas_call(
            body, name=name,
            out_shape=[pltpu.HBM(b.shape, b.dtype) for b in bufs],
            in_specs=[_HBM] * nbuf + [_SEM, _SEM, pl.BlockSpec(memory_space=pl.ANY)],
            out_specs=[_HBM] * nbuf,
            input_output_aliases={i: i for i in range(nbuf)},
            compiler_params=pltpu.CompilerParams(has_side_effects=_EFFECT),
        )(*bufs, send_sems, recv_sems, after)

    return start, wait


def _chip_exchange_split(n):
    def make(bufs, t, j, send_sem, recv_sem):
        x, y, c = _place()
        chip = _other_chips(x, y)[j]
        return pltpu.make_async_remote_copy(src_ref=bufs[t].at[1 + j], dst_ref=bufs[n + t].at[j], send_sem=send_sem,
                                            recv_sem=recv_sem, device_id=(*chip, c), device_id_type=MESH)

    return _split_copies(n, 3, make)


def _weights_gather_split(fulls):
    def make(bufs, t, j, send_sem, recv_sem):
        x, y, c = _place()
        chip = _other_chips(x, y)[j]
        rs = fulls[t].shape[0] // N_CHIPS
        rows = bufs[t].at[pl.ds((2 * x + y) * rs + c * (rs // 2), rs // 2), :]
        return pltpu.make_async_remote_copy(src_ref=rows, dst_ref=rows, send_sem=send_sem, recv_sem=recv_sem,
                                            device_id=(*chip, c), device_id_type=MESH)

    return _split_copies(len(fulls), 3, make)


def _weights_pass_on(fulls, name):
    n = len(fulls)

    def body(*refs):
        full = refs[n:2 * n]
        send_sems, recv_sems = refs[2 * n:]
        x, y, c = _place()
        chips = _other_chips(x, y)

        def copy(t, j, h):
            rs = fulls[t].shape[0] // N_CHIPS
            px, py = chips[j]
            rows = full[t].at[pl.ds((2 * px + py) * rs + h * (rs // 2), rs // 2), :]
            return pltpu.make_async_remote_copy(src_ref=rows, dst_ref=rows, send_sem=send_sems.at[3 * t + j],
                                                recv_sem=recv_sems.at[3 * t + j], device_id=(x, y, 1 - c),
                                                device_id_type=MESH)

        for t in range(n):
            for j in range(3):
                copy(t, j, c).start()
        for t in range(n):
            for j in range(3):
                copy(t, j, 1 - c).wait_recv()
        for t in range(n):
            for j in range(3):
                copy(t, j, c).wait_send()

    return pl.pallas_call(
        body, name=name,
        out_shape=[jax.ShapeDtypeStruct(f.shape, f.dtype) for f in fulls],
        in_specs=_any_specs(n), out_specs=_any_specs(n),
        input_output_aliases={t: t for t in range(n)},
        scratch_shapes=[pltpu.SemaphoreType.DMA((3 * n,)), pltpu.SemaphoreType.DMA((3 * n,))],
    )(*fulls)


def _after(value, token, name):
    def body(v_ref, t_ref, o_ref):
        o_ref[...] = v_ref[...]

    return pl.pallas_call(
        body, name=name, out_shape=jax.ShapeDtypeStruct(value.shape, value.dtype),
        in_specs=_whole(2), out_specs=pl.BlockSpec(memory_space=pltpu.VMEM),
    )(value, token)


def _chip_sum(ps, landed, idx, name):
    _, half, D = ps.shape
    tr = _pick(half, (416, 352, 128))
    steps = half // tr

    def body(idx_ref, p_ref, l_ref, o_ref):
        acc = p_ref[0].astype(F32)
        for j in range(3):
            acc = acc + l_ref[j].astype(F32)
        o_ref[...] = acc

    return pl.pallas_call(
        body, name=name,
        grid_spec=pltpu.PrefetchScalarGridSpec(
            num_scalar_prefetch=1, grid=(steps,),
            in_specs=[pl.BlockSpec((1, tr, D), lambda i, idx: (0, i, 0)),
                      pl.BlockSpec((3, tr, D), lambda i, idx: (0, i, 0))],
            out_specs=pl.BlockSpec((tr, D), lambda i, idx: (idx[0] * steps + i, 0))),
        out_shape=jax.ShapeDtypeStruct((2 * half, D), F32),
        compiler_params=_params(("arbitrary",)),
    )(idx, ps, landed)


def _pair_swap(shards, name):
    n = len(shards)

    def body(*refs):
        full = refs[n:2 * n]
        send_sems, recv_sems = refs[2 * n:]
        x, y, c = _place()

        def half(t, h):
            rows = shards[t].shape[0] // 2
            return full[t].at[pl.ds(h * rows, rows), :]

        def copy(t, h):
            return pltpu.make_async_remote_copy(src_ref=half(t, h), dst_ref=half(t, h), send_sem=send_sems.at[t],
                                                recv_sem=recv_sems.at[t], device_id=(x, y, 1 - c),
                                                device_id_type=MESH)

        for t in range(n):
            copy(t, c).start()
        for t in range(n):
            copy(t, 1 - c).wait_recv()
        for t in range(n):
            copy(t, c).wait_send()

    return pl.pallas_call(
        body, name=name,
        out_shape=[jax.ShapeDtypeStruct(a.shape, a.dtype) for a in shards],
        in_specs=_any_specs(n), out_specs=_any_specs(n),
        input_output_aliases={t: t for t in range(n)},
        scratch_shapes=[pltpu.SemaphoreType.DMA((n,)), pltpu.SemaphoreType.DMA((n,))],
    )(*shards)


def _gather_begin(fulls, tag):
    start, wait = _weights_gather_split(fulls)
    send_sems, recv_sems, bufs, token = start(len(fulls), f"ag_{tag}_start", fulls)
    return (wait, send_sems, recv_sems, bufs), token


def _gather_end(state, after, tag):
    wait, send_sems, recv_sems, bufs = state
    landed = wait(len(bufs), f"ag_{tag}_wait", send_sems, recv_sems, bufs, after)
    return _weights_pass_on(landed, f"ag_{tag}_pass_on")


class _Exchanges:
    def __init__(self, fulls_rest, idx):
        self.idx = idx
        self._rest, self.token = _gather_begin(fulls_rest, "rest")
        self._early = []

    def rest_weights(self, after):
        return _gather_end(self._rest, after, "rest")

    def reduce_early(self, grads, tag, token=None):
        landed = _pair_exchange(grads, "rs_pair_exchange_" + tag)
        sums = [_pair_sum(g, l, self.idx, f"rs_pair_sum_{tag}{t}", token)
                for t, (g, l) in enumerate(zip(grads, landed))]
        zones = [lax.empty((3,) + s.shape[1:], s.dtype) for s in sums]
        start, wait = _chip_exchange_split(len(sums))
        send_sems, recv_sems, bufs, token = start(2 * len(sums), "rs_chip_start_" + tag, sums + zones)
        self._early.append((tag, wait, send_sems, recv_sems, bufs))
        return token

    def finish(self, tags, after):
        halves = []
        for tag, wait, send_sems, recv_sems, bufs in self._early:
            if tag in tags:
                n = len(bufs) // 2
                done = wait(len(bufs), "rs_chip_wait_" + tag, send_sems, recv_sems, bufs, after)
                halves += [_chip_sum(p, l, self.idx, f"rs_chip_sum_{tag}{t}")
                           for t, (p, l) in enumerate(zip(done[:n], done[n:]))]
        return halves


N_MOD = 9
PACK_HEAD, PACK_N3, PACK_N2, PACK_N1, PACK_CONV, PACK_QK = 0, 16, 32, 48, 64, 80
PACK_ROWS = 96
MOD_SRC = ((PACK_N1, 0), (PACK_N1, 1), (PACK_N2, 3), (PACK_N2, 0), (PACK_N2, 1),
           (PACK_N3, 3), (PACK_N3, 0), (PACK_N3, 1), (PACK_HEAD, 2))
CTX_ROW = 8


def _silu(v):
    return v * jax.nn.sigmoid(v)


def _whole(n):
    return [pl.BlockSpec(memory_space=pltpu.VMEM)] * n


def _mod_rows(cin, w_sh, b_sh, name):
    def body(c_ref, w_ref, b_ref, o_ref):
        a = _silu(c_ref[...]).astype(BF16)
        o_ref[...] = jnp.dot(a, w_ref[...].astype(BF16), preferred_element_type=F32) + b_ref[...]

    return pl.pallas_call(
        body, name=name, out_shape=jax.ShapeDtypeStruct((cin.shape[0], w_sh.shape[1]), F32),
        in_specs=_whole(3), out_specs=pl.BlockSpec(memory_space=pltpu.VMEM),
        compiler_params=pltpu.CompilerParams(vmem_limit_bytes=VMEM_LIMIT),
    )(cin, w_sh, b_sh)


def _small_reduce(gathered, name):
    _, _, D = gathered.shape

    def body(g_ref, loss_ref, db_ref, gn_ref, cv_ref, qk_ref, dm_ref):
        tot = g_ref[0]
        for r in range(1, N_DEV):
            tot = tot + g_ref[r]

        def both(block, row):
            return tot[block + row:block + row + 1, :] + tot[block + 8 + row:block + 8 + row + 1, :]

        loss = jnp.sum(both(PACK_HEAD, 0), axis=1, keepdims=True)
        loss_ref[...] = jnp.broadcast_to(loss, loss_ref.shape)
        db_ref[...] = jnp.zeros(db_ref.shape, F32)
        dm_ref[...] = jnp.zeros(dm_ref.shape, F32)
        for j, (block, row) in enumerate(MOD_SRC):
            db_ref[j:j + 1, :] = both(block, row)
            dm_ref[CTX_ROW, j:j + 1, :] = tot[block + row:block + row + 1, :]
            for r in range(N_DEV):
                dm_ref[r, j:j + 1, :] = g_ref[r, block + 8 + row:block + 8 + row + 1, :]
        gn_ref[...] = jnp.zeros(gn_ref.shape, F32)
        gn_ref[0:1, :] = both(PACK_N1, 2)
        gn_ref[8:9, :] = both(PACK_N2, 2)
        gn_ref[16:17, :] = both(PACK_N3, 2)
        gn_ref[24:25, :] = both(PACK_HEAD, 1)
        cv_ref[...] = jnp.zeros(cv_ref.shape, F32)
        for r in range(3):
            cv_ref[r:r + 1, :] = both(PACK_CONV, r)
        qk_ref[...] = jnp.zeros(qk_ref.shape, F32)
        qk_ref[0:1, 0:HEAD_DIM] = both(PACK_QK, 0)[:, 0:HEAD_DIM]
        qk_ref[0:1, HEAD_DIM:2 * HEAD_DIM] = both(PACK_QK, 1)[:, 0:HEAD_DIM]

    return pl.pallas_call(
        body, name=name,
        out_shape=[jax.ShapeDtypeStruct((8, 128), F32), jax.ShapeDtypeStruct((16, D), F32),
                   jax.ShapeDtypeStruct((32, D), F32), jax.ShapeDtypeStruct((8, D), F32),
                   jax.ShapeDtypeStruct((8, D), F32), jax.ShapeDtypeStruct((16, 16, D), F32)],
        in_specs=_whole(1), out_specs=_whole(6),
        compiler_params=pltpu.CompilerParams(vmem_limit_bytes=VMEM_LIMIT),
    )(gathered)


def _wmod_grad(cin, dm_sh, w_sh, name):
    def body(c_ref, d_ref, w_ref, gw_ref, cp_ref):
        a = _silu(c_ref[...]).astype(BF16)
        d = d_ref[...].astype(BF16)
        gw_ref[...] = lax.dot_general(a, d, (((0,), (0,)), ((), ())), preferred_element_type=F32)
        cp_ref[...] = lax.dot_general(d, w_ref[...].astype(BF16), (((1,), (1,)), ((), ())),
                                      preferred_element_type=F32)

    return pl.pallas_call(
        body, name=name,
        out_shape=[jax.ShapeDtypeStruct(w_sh.shape, F32), jax.ShapeDtypeStruct(cin.shape, F32)],
        in_specs=_whole(3), out_specs=_whole(2),
        compiler_params=pltpu.CompilerParams(vmem_limit_bytes=VMEM_LIMIT),
    )(cin, dm_sh, w_sh)


def _cctx_grad(parts, c_ctx8, name):
    def body(p_ref, c_ref, o_ref):
        tot = p_ref[0] + p_ref[2] + p_ref[4] + p_ref[6]
        cv = c_ref[...]
        sig = jax.nn.sigmoid(cv)
        rows = lax.broadcasted_iota(jnp.int32, tot.shape, 0)
        o_ref[...] = jnp.where(rows == 0, tot * (sig * (1.0 + cv * (1.0 - sig))), 0.0)

    return pl.pallas_call(
        body, name=name, out_shape=jax.ShapeDtypeStruct(c_ctx8.shape, F32),
        in_specs=_whole(2), out_specs=pl.BlockSpec(memory_space=pltpu.VMEM),
    )(parts, c_ctx8)


def _pad_rows(a, rows):
    return jnp.pad(a, ((0, rows - a.shape[0]), (0, 0)))


def _pack_small(c_ctx, b_mod, n1, n2, n3, final_g, gq, gk, conv_sh, D):
    misc = jnp.concatenate([gq, gk, conv_sh.reshape(1, -1)], axis=1)
    return jnp.concatenate([_pad_rows(c_ctx[None], 8), _pad_rows(b_mod.reshape(N_MOD, D), 16), _pad_rows(n1, 8),
                            _pad_rows(n2, 8), _pad_rows(n3, 8), _pad_rows(final_g[None], 8), _pad_rows(misc, 8)], axis=0)


def _unpack_small(p, D, conv_shape):
    misc = p[56:57]
    return dict(c_ctx=p[0], b_mod=p[8:8 + N_MOD].reshape(1, N_MOD * D), norm1_g=p[24:25], norm2_g=p[32:33],
                norm3_g=p[40:41], final_g=p[48], q_norm_g=misc[:, 0:HEAD_DIM], k_norm_g=misc[:, HEAD_DIM:2 * HEAD_DIM],
                conv_w=misc[:, 2 * HEAD_DIM:].reshape(conv_shape))


WEIGHT_ORDER = ("c_ctx", "w_mod", "b_mod", "norm1_g", "norm2_g", "norm3_g", "ffn1_w_in", "ffn1_w_out", "w_in",
                "conv_w", "q_norm_g", "k_norm_g", "w_branch_conv", "w_branch_attn", "w_out", "ffn2_w_in",
                "ffn2_w_out", "final_g")
BIG = ("ffn1_w_in", "ffn1_w_out", "w_in", "w_branch_conv", "w_branch_attn", "w_out", "ffn2_w_in", "ffn2_w_out")
COLUMN_SHARDED = ("ffn1_w_in", "w_in", "ffn2_w_in")


def kernel(x, c, ctx, c_ctx, w_mod, b_mod, norm1_g, norm2_g, norm3_g, ffn1_w_in, ffn1_w_out, w_in, conv_w, q_norm_g, k_norm_g, w_branch_conv, w_branch_attn, w_out, ffn2_w_in, ffn2_w_out, final_g, loss_target, m_c_ctx, m_w_mod, m_b_mod, m_norm1_g, m_norm2_g, m_norm3_g, m_ffn1_w_in, m_ffn1_w_out, m_w_in, m_conv_w, m_q_norm_g, m_k_norm_g, m_w_branch_conv, m_w_branch_attn, m_w_out, m_ffn2_w_in, m_ffn2_w_out, m_final_g, v_c_ctx, v_w_mod, v_b_mod, v_norm1_g, v_norm2_g, v_norm3_g, v_ffn1_w_in, v_ffn1_w_out, v_w_in, v_conv_w, v_q_norm_g, v_k_norm_g, v_w_branch_conv, v_w_branch_attn, v_w_out, v_ffn2_w_in, v_ffn2_w_out, v_final_g):
    w = dict(c_ctx=c_ctx, w_mod=w_mod, b_mod=b_mod, norm1_g=norm1_g, norm2_g=norm2_g, norm3_g=norm3_g,
             ffn1_w_in=ffn1_w_in, ffn1_w_out=ffn1_w_out, w_in=w_in, conv_w=conv_w, q_norm_g=q_norm_g,
             k_norm_g=k_norm_g, w_branch_conv=w_branch_conv, w_branch_attn=w_branch_attn, w_out=w_out,
             ffn2_w_in=ffn2_w_in, ffn2_w_out=ffn2_w_out, final_g=final_g)
    m = dict(c_ctx=m_c_ctx, w_mod=m_w_mod, b_mod=m_b_mod, norm1_g=m_norm1_g, norm2_g=m_norm2_g, norm3_g=m_norm3_g,
             ffn1_w_in=m_ffn1_w_in, ffn1_w_out=m_ffn1_w_out, w_in=m_w_in, conv_w=m_conv_w, q_norm_g=m_q_norm_g,
             k_norm_g=m_k_norm_g, w_branch_conv=m_w_branch_conv, w_branch_attn=m_w_branch_attn, w_out=m_w_out,
             ffn2_w_in=m_ffn2_w_in, ffn2_w_out=m_ffn2_w_out, final_g=m_final_g)
    v = dict(c_ctx=v_c_ctx, w_mod=v_w_mod, b_mod=v_b_mod, norm1_g=v_norm1_g, norm2_g=v_norm2_g, norm3_g=v_norm3_g,
             ffn1_w_in=v_ffn1_w_in, ffn1_w_out=v_ffn1_w_out, w_in=v_w_in, conv_w=v_conv_w, q_norm_g=v_q_norm_g,
             k_norm_g=v_k_norm_g, w_branch_conv=v_w_branch_conv, w_branch_attn=v_w_branch_attn, w_out=v_w_out,
             ffn2_w_in=v_ffn2_w_in, ffn2_w_out=v_ffn2_w_out, final_g=v_final_g)

    xi, yi, ci = _place()
    dev = 4 * xi + 2 * yi + ci
    shard = 2 * xi + yi
    idx = jnp.stack([ci, shard, 2 * (1 - xi) + yi, 2 * xi + (1 - yi), 2 * (1 - xi) + (1 - yi)]).astype(jnp.int32)
    D = x.shape[-1]
    ctx_len = ctx.shape[1]
    assert ctx_len == ROW and c.shape == (1, D)
    mcols = w_mod.shape[2]
    ccols = conv_w.shape[2]

    def place(names, token):
        fulls = []
        for n in names:
            fulls.append(_place_shard(w[n][0], idx, n in COLUMN_SHARDED, "place_" + n, token))
            token = fulls[-1][:16, :HEAD_DIM]
        return fulls

    ffn1_gather, ffn1_token = _gather_begin(place(BIG[:2], c), "ffn1")
    fulls_rest = place(BIG[2:], ffn1_token)

    c8 = _after(jnp.broadcast_to(c, (8, D)), fulls_rest[-1][:16, :HEAD_DIM], "after_place")
    c_all = _allgather8(c8, "ag_c")[:, 0, :]
    cin = jnp.concatenate([c_all, _pad_rows(c_ctx[None], 8)], axis=0)
    b_sh = lax.dynamic_slice(b_mod, (0, shard * mcols), (1, mcols))
    mod_sh = _mod_rows(cin, w_mod[0], b_sh, "mod_rows")
    conv_rows = jnp.pad(conv_w[0], ((0, 8 - conv_w.shape[1]), (0, mcols - ccols)))
    mod_all = _allgather8(jnp.concatenate([mod_sh, conv_rows], axis=0), "ag_mod")
    mod_full = jnp.concatenate([mod_all[2 * s, :16] for s in range(N_CHIPS)], axis=1)
    conv_full = jnp.concatenate([mod_all[2 * s, 16:16 + conv_w.shape[1], :ccols] for s in range(N_CHIPS)], axis=1)
    mod_lat = lax.dynamic_slice(mod_full, (dev, 0), (1, N_MOD * D)).reshape(N_MOD, D)
    mod_ctx = mod_full[CTX_ROW].reshape(N_MOD, D)
    mods = jnp.stack([_pad_rows(mod_ctx, 16), _pad_rows(mod_lat, 16)])

    ffn1_w = _gather_end(ffn1_gather, mods, "ffn1")
    hooks = _Exchanges(fulls_rest, idx)

    xcat = (ctx[0], x[0])
    norm1_first = _after(norm1_g, hooks.token, "after_ag_rest")
    grad_x, grads, accs = _local_step(xcat, loss_target[0], mods, (norm1_first, norm2_g, norm3_g), final_g[None],
                                      q_norm_g, k_norm_g, conv_full, ffn1_w, hooks, ctx_len)
    g = {}

    pack = jnp.concatenate([a.reshape(2 * ACC_ROWS, D) for a in accs], axis=0)
    gathered = _allgather8(pack, "ag_small")
    loss8, db_mod, g_norms, g_conv, g_qk, dm = _small_reduce(gathered, "small_reduce")
    dm_sh = lax.dynamic_slice(dm[:, :N_MOD, :].reshape(16, N_MOD * D), (0, shard * mcols), (16, mcols))
    g_wmod, cpart = _wmod_grad(cin, dm_sh, w_mod[0], "wmod_grad")
    g["w_mod"] = g_wmod[None]
    cparts = _allgather8(cpart[CTX_ROW:CTX_ROW + 8], "ag_cctx")
    g_cctx = _cctx_grad(cparts, _pad_rows(c_ctx[None], 8), "cctx_grad")
    g_conv_sh = lax.dynamic_slice(g_conv, (0, shard * ccols), (conv_w.shape[1], ccols))
    g_misc = jnp.concatenate([g_qk[0:1, 0:2 * HEAD_DIM], g_conv_sh.reshape(1, -1)], axis=1)
    g_pack = jnp.concatenate([g_cctx, db_mod, g_norms, _pad_rows(g_misc, 8)], axis=0)

    def packed(p):
        return _pack_small(p["c_ctx"], p["b_mod"], p["norm1_g"], p["norm2_g"], p["norm3_g"], p["final_g"],
                           p["q_norm_g"], p["k_norm_g"], p["conv_w"][0], D)

    d_pack, m_pack, v_pack = _adamw(packed(w), g_pack, packed(m), packed(v), "adamw_small")

    g.update(_unpack_small(g_pack, D, conv_w.shape))
    delta = _unpack_small(d_pack, D, conv_w.shape)
    new_m = _unpack_small(m_pack, D, conv_w.shape)
    new_v = _unpack_small(v_pack, D, conv_w.shape)

    def update(n, g2):
        if n in COLUMN_SHARDED:
            g2, d2, m2, v2 = _adamw_transposed(w[n][0], g2, m[n][0], v[n][0], "adamw_" + n)
        else:
            d2, m2, v2 = _adamw(w[n][0], g2, m[n][0], v[n][0], "adamw_" + n)
        g[n], delta[n], new_m[n], new_v[n] = g2[None], d2[None], m2[None], v2[None]
        return v2

    token_d = hooks.reduce_early([grads[0]], "d", token=d_pack[:8, :HEAD_DIM])
    h_wbc, h_wba, h_wo, h_w2i, h_w2o, h_wi, h_w1o = hooks.finish("abc", token_d)
    done = _pair_swap([h_w1o, h_wi, h_wbc, h_wba, h_wo, h_w2i, h_w2o], "rs_pair_swap")
    last = update("w_mod", g_wmod)
    for n, r in zip(BIG[1:], done):
        last = update(n, r)
    (h_w1i,) = hooks.finish("d", last)
    update(BIG[0], _pair_swap([h_w1i], "rs_pair_swap_d")[0])

    loss = loss8[0, 0]
    return (loss, grad_x[None], *[g[n] for n in WEIGHT_ORDER], *[delta[n] for n in WEIGHT_ORDER],
            *[new_m[n] for n in WEIGHT_ORDER], *[new_v[n] for n in WEIGHT_ORDER])
```

```python
import functools

import jax
import jax.numpy as jnp
from jax import lax
from jax.experimental import pallas as pl
from jax.experimental.pallas import tpu as pltpu

F32 = jnp.float32
BF16 = jnp.bfloat16

HEAD_DIM = 128
N_Q_HEADS = 8
N_KV_HEADS = 2
GROUP = N_Q_HEADS // N_KV_HEADS
GRID_W = 64
ROPE_THETA = 10000.0
EPS = 1e-6
ATTN_SCALE = HEAD_DIM ** -0.5

ADAM_LR = 0.001
ADAM_B1 = 0.9
ADAM_B2 = 0.999
ADAM_EPS = 1e-08
ADAM_WD = 0.01
ADAM_STEP = 10

ROW = 256
HALO = 16
ACC_ROWS = 8
N_CHIPS = 4
N_DEV = 8
MESH = pl.DeviceIdType.MESH
VMEM_LIMIT = 48 * 1024 * 1024
ADAMW_BLOCK_BYTES = 1024 * 1024


def _pick(n, prefs):
    for p in prefs:
        if n % p == 0:
            return p
    return n


def _params(sem):
    return pltpu.CompilerParams(dimension_semantics=sem, vmem_limit_bytes=VMEM_LIMIT)


def _stream(i):
    return jnp.minimum(i, 1)


def _matmul(a, b, mode, out_dtype, name, tm=None, tn=None, tk=None, token=None):
    if mode == "nn":
        (M, K), (K2, N) = a.shape, b.shape
    elif mode == "nt":
        (M, K), (N, K2) = a.shape, b.shape
    else:
        (K, M), (K2, N) = a.shape, b.shape
    assert K == K2, (a.shape, b.shape, mode)
    tm = tm or _pick(M, (1664, 1408, 1024, 512, 256, 128) if mode == "tn" else (1408, 768, 512, 256, 128))
    tn = tn or _pick(N, (1664, 1408, 1024, 512, 256, 128))
    tk = tk or _pick(K, (2816, 1664, 1408, 1024, 768, 512, 256, 128) if mode == "tn" else
                     (1664, 1408, 1024, 768, 512, 256, 128))
    nk = K // tk
    if mode == "tn":
        a_spec = pl.BlockSpec((tk, tm), lambda i, j, k: (k, i))
    else:
        a_spec = pl.BlockSpec((tm, tk), lambda i, j, k: (i, k))
    if mode == "nt":
        b_spec = pl.BlockSpec((tn, tk), lambda i, j, k: (j, k))
    else:
        b_spec = pl.BlockSpec((tk, tn), lambda i, j, k: (k, j))
    dims = {"nn": ((1,), (0,)), "nt": ((1,), (1,)), "tn": ((0,), (0,))}[mode]
    use_scratch = nk > 1 and out_dtype != F32

    extra = [] if token is None else [token]

    def body(a_ref, b_ref, *rest):
        o_ref, scratch = rest[len(extra)], rest[len(extra) + 1:]
        p = lax.dot_general(a_ref[...].astype(BF16), b_ref[...].astype(BF16), (dims, ((), ())),
                            preferred_element_type=F32)
        if nk == 1:
            o_ref[...] = p.astype(o_ref.dtype)
            return
        acc_ref = scratch[0] if use_scratch else o_ref
        k = pl.program_id(2)

        @pl.when(k == 0)
        def _():
            acc_ref[...] = p

        @pl.when(k > 0)
        def _():
            acc_ref[...] += p

        if use_scratch:
            @pl.when(k == nk - 1)
            def _():
                o_ref[...] = acc_ref[...].astype(o_ref.dtype)

    return pl.pallas_call(
        body, name=name,
        grid=(M // tm, N // tn, nk),
        in_specs=[a_spec, b_spec] + [pl.BlockSpec(t.shape, lambda i, j, k: (0, 0)) for t in extra],
        out_specs=pl.BlockSpec((tm, tn), lambda i, j, k: (i, j)),
        out_shape=jax.ShapeDtypeStruct((M, N), out_dtype),
        scratch_shapes=[pltpu.VMEM((tm, tn), F32)] if use_scratch else [],
        compiler_params=_params(("parallel", "parallel", "arbitrary")),
    )(a, b, *extra)


def _row_spec(width, col=0):
    return pl.BlockSpec((ROW, width), lambda i, col=col: (i, col))


def _mods_spec(D):
    return pl.BlockSpec((1, 16, D), lambda i: (_stream(i), 0, 0))


def _acc_spec(D):
    return pl.BlockSpec((1, ACC_ROWS, D), lambda i: (_stream(i), 0, 0))


def _vec_spec(rows, D):
    return pl.BlockSpec((rows, D), lambda i: (0, 0))


def _acc_init(acc_ref):
    i = pl.program_id(0)

    @pl.when(i <= 1)
    def _():
        acc_ref[...] = jnp.zeros_like(acc_ref)


def _acc_add(acc_ref, row, val):
    acc_ref[0, row:row + 1, :] += jnp.sum(val, axis=0, keepdims=True)


def _rows_operand(x):
    if not isinstance(x, tuple):
        return [_row_spec(x.shape[1])], [x], x.shape
    ctx, lat = x
    D = lat.shape[1]
    assert ctx.shape == (ROW, D)
    specs = [pl.BlockSpec((ROW, D), lambda i: (0, 0)), pl.BlockSpec((ROW, D), lambda i: (jnp.maximum(i - 1, 0), 0))]
    return specs, [ctx, lat], (ROW + lat.shape[0], D)


def _rows_tile(refs):
    if len(refs) == 1:
        return refs[0][...]
    return jnp.where(pl.program_id(0) == 0, refs[0][...], refs[1][...])


def _norm_tile_fwd(x, m, g, shift_idx, scale_idx):
    inv = lax.rsqrt(jnp.mean(x * x, axis=-1, keepdims=True) + EPS)
    y = (x * inv) * g
    return (y * (1.0 + m[scale_idx:scale_idx + 1, :]) + m[shift_idx:shift_idx + 1, :]).astype(BF16)


def _norm_tile_bwd(x, dh, dres, m, g, shift_idx, scale_idx, acc_ref):
    inv = lax.rsqrt(jnp.mean(x * x, axis=-1, keepdims=True) + EPS)
    xn = x * inv
    dy = dh * (1.0 + m[scale_idx:scale_idx + 1, :])
    dxn = dy * g
    _acc_add(acc_ref, 0, dh)
    _acc_add(acc_ref, 1, dh * (xn * g))
    _acc_add(acc_ref, 2, dy * xn)
    return inv * (dxn - xn * jnp.mean(dxn * xn, axis=-1, keepdims=True)) + dres


def _gate_tile_bwd(dx, branch, m, gate, acc_ref):
    gate_idx, fac = gate
    _acc_add(acc_ref, 3, fac * dx * branch)
    return ((fac * m[gate_idx:gate_idx + 1, :]) * dx).astype(BF16)


_NT = (((1,), (1,)), ((), ()))


def _ffn_chunk(F):
    return _pick(F, (2816, 1408, 512, 256, 128))


def _resident():
    return pl.BlockSpec(memory_space=pltpu.VMEM)


def _ffn_tile_fwd(hv, wi_ref, wo_ref, u_ref, s_ref, F, cw):
    acc = jnp.zeros((hv.shape[0], wo_ref.shape[1]), F32)
    for j in range(F // cw):
        a = lax.dot_general(hv, wi_ref[j * cw:(j + 1) * cw, :], _NT, preferred_element_type=F32)
        b = lax.dot_general(hv, wi_ref[F + j * cw:F + (j + 1) * cw, :], _NT, preferred_element_type=F32)
        s = ((a * jax.nn.sigmoid(a)) * b).astype(BF16)
        u_ref[:, j * cw:(j + 1) * cw] = a.astype(BF16)
        u_ref[:, F + j * cw:F + (j + 1) * cw] = b.astype(BF16)
        s_ref[:, j * cw:(j + 1) * cw] = s
        acc = acc + jnp.dot(s, wo_ref[j * cw:(j + 1) * cw, :], preferred_element_type=F32)
    return acc


def _norm_ffn_fwd(xprev, branch, mods, g, gate, shift_idx, scale_idx, w_in_t, w_out, name, head=None):
    x_specs, x_args, (T, D) = _rows_operand(xprev)
    F = w_out.shape[0]
    cw = _ffn_chunk(F)
    has_res = branch is not None
    n_in = len(x_args) + int(has_res) + 4 + (2 if head else 0)

    def body(*refs):
        ins, outs = list(refs[:n_in]), list(refs[n_in:])
        x = _rows_tile([ins.pop(0) for _ in x_args])
        f_ref = ins.pop(0) if has_res else None
        m_ref, g_ref, wi_ref, wo_ref = ins[:4]
        xo_ref = outs.pop(0) if has_res else None
        h_ref, u_ref, s_ref = outs[:3]
        m = m_ref[0]
        if has_res:
            gate_idx, fac = gate
            x = x + (fac * m[gate_idx:gate_idx + 1, :]) * f_ref[...]
            xo_ref[...] = x
        hv = _norm_tile_fwd(x, m, g_ref[...], shift_idx, scale_idx)
        h_ref[...] = hv
        f = _ffn_tile_fwd(hv, wi_ref, wo_ref, u_ref, s_ref, F, cw)
        if head is None:
            outs[3][...] = f
            return
        fg_ref, t_ref = ins[4:6]
        dx_ref, df_ref, acc_ref = outs[3:6]
        _acc_init(acc_ref)
        lat = (pl.program_id(0) > 0).astype(F32)
        gate8 = 0.5 * m[8:9, :]
        x3 = x + gate8 * f
        inv3 = lax.rsqrt(jnp.mean(x3 * x3, axis=-1, keepdims=True) + EPS)
        xn = x3 * inv3
        fg = fg_ref[...]
        e = (xn * fg - t_ref[...]) * lat
        dy = e * (1.0 / D)
        dxn = dy * fg
        dx = inv3 * (dxn - xn * jnp.mean(dxn * xn, axis=-1, keepdims=True))
        dx_ref[...] = dx
        df_ref[...] = (gate8 * dx).astype(BF16)
        _acc_add(acc_ref, 0, (0.5 / D) * e * e)
        _acc_add(acc_ref, 1, dy * xn)
        _acc_add(acc_ref, 2, 0.5 * dx * f)

    in_specs = x_specs + ([_row_spec(D)] if has_res else []) + \
               [_mods_spec(D), _vec_spec(1, D), _resident(), _resident()]
    args = x_args + ([branch] if has_res else []) + [mods, g, w_in_t, w_out]
    out_specs = ([_row_spec(D)] if has_res else []) + [_row_spec(D), _row_spec(2 * F), _row_spec(F)]
    out_shape = ([jax.ShapeDtypeStruct((T, D), F32)] if has_res else []) + \
                [jax.ShapeDtypeStruct((T, D), BF16), jax.ShapeDtypeStruct((T, 2 * F), BF16),
                 jax.ShapeDtypeStruct((T, F), BF16)]
    if head is None:
        out_specs += [_row_spec(D)]
        out_shape += [jax.ShapeDtypeStruct((T, D), F32)]
    else:
        in_specs += [_vec_spec(1, D), pl.BlockSpec((ROW, D), lambda i: (jnp.maximum(i - 1, 0), 0))]
        args += list(head)
        out_specs += [_row_spec(D), _row_spec(D), _acc_spec(D)]
        out_shape += [jax.ShapeDtypeStruct((T, D), F32), jax.ShapeDtypeStruct((T, D), BF16),
                      jax.ShapeDtypeStruct((2, ACC_ROWS, D), F32)]
    out = pl.pallas_call(
        body, name=name, grid=(T // ROW,), in_specs=in_specs, out_specs=out_specs, out_shape=out_shape,
        compiler_params=_params(("arbitrary",) if head else ("parallel",)),
    )(*args)
    return tuple(out) if has_res else (None,) + tuple(out)


def _ffn_norm_bwd(df, u, w_in_t, w_out, x, dres, mods, g, shift_idx, scale_idx, gate, branch, name,
                  skip_first_tile=False):
    T, D = df.shape
    F = w_out.shape[0]
    cw = _ffn_chunk(F)
    nt = T // ROW
    has_gate = gate is not None
    x_specs, x_args, _ = _rows_operand(x)
    n_in = 7 + len(x_args) + int(has_gate)

    def body(*refs):
        ins, outs = list(refs[:n_in]), list(refs[n_in:])
        df_ref, u_ref, wi_ref, wo_ref = ins[:4]
        x_refs = ins[4:4 + len(x_args)]
        dr_ref = ins[4 + len(x_args)]
        b_ref = ins[5 + len(x_args)] if has_gate else None
        m_ref, g_ref = ins[-2:]
        du_ref, dx_ref = outs[:2]
        db_ref = outs[2] if has_gate else None
        acc_ref = outs[-1]
        _acc_init(acc_ref)
        dfv = df_ref[...]
        dh = jnp.zeros((ROW, D), F32)
        for j in range(F // cw):
            ds = lax.dot_general(dfv, wo_ref[j * cw:(j + 1) * cw, :], _NT, preferred_element_type=F32)
            a = u_ref[:, j * cw:(j + 1) * cw].astype(F32)
            b = u_ref[:, F + j * cw:F + (j + 1) * cw].astype(F32)
            sig = jax.nn.sigmoid(a)
            da = (ds * b * (sig * (1.0 + a * (1.0 - sig)))).astype(BF16)
            db = (ds * (a * sig)).astype(BF16)
            du_ref[:, j * cw:(j + 1) * cw] = da
            du_ref[:, F + j * cw:F + (j + 1) * cw] = db
            dh = dh + jnp.dot(da, wi_ref[j * cw:(j + 1) * cw, :], preferred_element_type=F32)
            dh = dh + jnp.dot(db, wi_ref[F + j * cw:F + (j + 1) * cw, :], preferred_element_type=F32)
        m = m_ref[0]
        dx = _norm_tile_bwd(_rows_tile(x_refs), dh, dr_ref[...], m, g_ref[...], shift_idx, scale_idx, acc_ref)
        dx_ref[...] = dx
        if has_gate:
            db_ref[...] = _gate_tile_bwd(dx, b_ref[...], m, gate, acc_ref)

    in_specs = [_row_spec(D), _row_spec(2 * F), _resident(), _resident()] + x_specs + [_row_spec(D)] + \
               ([_row_spec(D)] if has_gate else []) + [_mods_spec(D), _vec_spec(1, D)]
    args = [df, u, w_in_t, w_out] + x_args + [dres] + ([branch] if has_gate else []) + [mods, g]
    if skip_first_tile:
        dx_spec = pl.BlockSpec((ROW, D), lambda i: (jnp.maximum(i - 1, 0), 0))
        dx_shape = jax.ShapeDtypeStruct((T - ROW, D), F32)
    else:
        dx_spec = _row_spec(D)
        dx_shape = jax.ShapeDtypeStruct((T, D), F32)
    out_specs = [_row_spec(2 * F), dx_spec] + ([_row_spec(D)] if has_gate else []) + [_acc_spec(D)]
    out_shape = [jax.ShapeDtypeStruct((T, 2 * F), BF16), dx_shape] + \
                ([jax.ShapeDtypeStruct((T, D), BF16)] if has_gate else []) + \
                [jax.ShapeDtypeStruct((2, ACC_ROWS, D), F32)]
    out = pl.pallas_call(
        body, name=name, grid=(nt,), in_specs=in_specs, out_specs=out_specs, out_shape=out_shape,
        compiler_params=_params(("arbitrary",)),
    )(*args)
    if has_gate:
        return tuple(out)
    return out[0], out[1], None, out[2]


def _halo_specs(width, col, nt):
    per = ROW // HALO
    prev = pl.BlockSpec((HALO, width), lambda i, col=col: (jnp.maximum(i * per - 1, 0), col))
    nxt = pl.BlockSpec((HALO, width), lambda i, col=col: (jnp.minimum((i + 1) * per, nt * per - 1), col))
    return prev, nxt


def _f32(ref):
    return ref[...].astype(F32)


def _last_row(halo_ref):
    return halo_ref[HALO - 1:HALO, :].astype(F32)


def _first_row(halo_ref):
    return halo_ref[0:1, :].astype(F32)


def _shift_rows(v, prev_row, next_row):
    rows = lax.broadcasted_iota(jnp.int32, v.shape, 0)
    down = jnp.where(rows == 0, prev_row, pltpu.roll(v, 1, 0))
    up = jnp.where(rows == v.shape[0] - 1, next_row, pltpu.roll(v, v.shape[0] - 1, 0))
    return down, up


def _conv_fwd_operands(P, conv_w, D):
    nt = P.shape[0] // ROW
    cg_p, cg_n = _halo_specs(D, 1, nt)
    vc_p, vc_n = _halo_specs(D, 2, nt)
    specs = [_row_spec(D, 0), _row_spec(D, 1), _row_spec(D, 2), cg_p, vc_p, cg_n, vc_n, _vec_spec(3, D)]
    return specs, [P, P, P, P, P, P, P, conv_w]


def _conv_tile_fwd(refs, nt):
    bg_ref, cg_ref, vc_ref, cgp_ref, vcp_ref, cgn_ref, vcn_ref, w_ref = refs
    i = pl.program_id(0)
    has_prev = (i != 1).astype(F32)
    has_next = (i != nt - 1).astype(F32)
    u = _f32(cg_ref) * _f32(vc_ref)
    up_row = _last_row(cgp_ref) * _last_row(vcp_ref) * has_prev
    un_row = _first_row(cgn_ref) * _first_row(vcn_ref) * has_next
    um1, up1 = _shift_rows(u, up_row, un_row)
    w = w_ref[...]
    conv = um1 * w[0:1, :] + u * w[1:2, :] + up1 * w[2:3, :]
    return (_f32(bg_ref) * conv).astype(BF16)


def _conv_bwd_operands(P, dy, conv_w, D):
    nt = P.shape[0] // ROW
    bg_p, bg_n = _halo_specs(D, 0, nt)
    cg_p, cg_n = _halo_specs(D, 1, nt)
    vc_p, vc_n = _halo_specs(D, 2, nt)
    dy_p, dy_n = _halo_specs(D, 0, nt)
    specs = [_row_spec(D, 0), _row_spec(D, 1), _row_spec(D, 2), _row_spec(D, 0),
             bg_p, cg_p, vc_p, dy_p, bg_n, cg_n, vc_n, dy_n, _vec_spec(3, D)]
    return specs, [P, P, P, dy, P, P, P, dy, P, P, P, dy, conv_w]


def _conv_tile_bwd(refs, o_ref, acc_ref, D, nt):
    (bg_ref, cg_ref, vc_ref, dy_ref, bgp_ref, cgp_ref, vcp_ref, dyp_ref,
     bgn_ref, cgn_ref, vcn_ref, dyn_ref, w_ref) = refs
    i = pl.program_id(0)
    lat = (i > 0).astype(F32)
    has_prev = (i != 1).astype(F32)
    has_next = (i != nt - 1).astype(F32)
    bg = _f32(bg_ref)
    cg = _f32(cg_ref)
    vc = _f32(vc_ref)
    dyv = dy_ref[...] * lat
    u = cg * vc
    up_row = _last_row(cgp_ref) * _last_row(vcp_ref) * has_prev
    un_row = _first_row(cgn_ref) * _first_row(vcn_ref) * has_next
    um1, up1 = _shift_rows(u, up_row, un_row)
    w = w_ref[...]
    conv = um1 * w[0:1, :] + u * w[1:2, :] + up1 * w[2:3, :]
    dc = dyv * bg
    dcp_row = _last_row(dyp_ref) * _last_row(bgp_ref) * has_prev
    dcn_row = _first_row(dyn_ref) * _first_row(bgn_ref) * has_next
    dcm1, dcp1 = _shift_rows(dc, dcp_row, dcn_row)
    du = dcp1 * w[0:1, :] + dc * w[1:2, :] + dcm1 * w[2:3, :]
    o_ref[:, 0:D] = (dyv * conv).astype(BF16)
    o_ref[:, D:2 * D] = (du * vc * lat).astype(BF16)
    o_ref[:, 2 * D:3 * D] = (du * cg * lat).astype(BF16)
    _acc_add(acc_ref, 0, dc * um1)
    _acc_add(acc_ref, 1, dc * u)
    _acc_add(acc_ref, 2, dc * up1)


def _rope_tables(ctx_len, seq):
    n_freq = HEAD_DIM // 4
    rows = seq // GRID_W
    inv = ROPE_THETA ** (-jnp.arange(n_freq, dtype=F32) / n_freq)
    ar = jnp.arange(rows, dtype=F32)[:, None] * inv
    ac = jnp.arange(GRID_W, dtype=F32)[:, None] * inv

    def per_row(a):
        return jnp.repeat(a, GRID_W, axis=0)

    def per_col(a):
        return jnp.tile(a, (rows, 1))

    cos_t = jnp.concatenate([per_row(jnp.cos(ar)), per_row(jnp.cos(ar)), per_col(jnp.cos(ac)), per_col(jnp.cos(ac))], axis=1)
    sin_t = jnp.concatenate([per_row(-jnp.sin(ar)), per_row(jnp.sin(ar)), per_col(-jnp.sin(ac)), per_col(jnp.sin(ac))], axis=1)
    cos_t = jnp.concatenate([jnp.ones((ctx_len, HEAD_DIM), F32), cos_t], axis=0)
    sin_t = jnp.concatenate([jnp.zeros((ctx_len, HEAD_DIM), F32), sin_t], axis=0)
    return cos_t, sin_t


def _swap_halves(y):
    lanes = lax.broadcasted_iota(jnp.int32, y.shape, 1)
    first = (lanes % 64) < 32
    return jnp.where(first, pltpu.roll(y, HEAD_DIM - 32, 1), pltpu.roll(y, 32, 1))


def _to_row(col, n):
    return jnp.transpose(jnp.broadcast_to(col, (n, HEAD_DIM)))[0:1, :]


LOG2E = 1.4426950408889634
ATTN_PART_LANES = 256
ATTN_QUERY_ROWS = 768
ATTN_VMEM_LIMIT = 60 * 1024 * 1024


def _flash_fwd(q, k, v, name, tq=None, tk=None):
    T = q.shape[0]
    tq = tq or _pick(T, (ATTN_QUERY_ROWS, ROW))
    parts = GROUP * tq // ATTN_PART_LANES
    tk = tk or _pick(T, (2816, 1408, 768, 512, 256))
    ck = tk
    nk = T // tk
    GW = GROUP * HEAD_DIM

    def body(q_ref, k_ref, v_ref, o_ref, lse_ref, qs_ref, m_ref, l_ref, acc_ref, st_ref):
        ki = pl.program_id(2)

        @pl.when(ki == 0)
        def _():
            for g in range(GROUP):
                qs_ref[g * tq:(g + 1) * tq, :] = q_ref[:, g * HEAD_DIM:(g + 1) * HEAD_DIM]
            m_ref[...] = jnp.full(m_ref.shape, -jnp.inf, F32)
            l_ref[...] = jnp.zeros(l_ref.shape, F32)
            acc_ref[...] = jnp.zeros(acc_ref.shape, F32)

        w = ATTN_PART_LANES
        nck = tk // ck

        def lanes(p):
            return slice(p * w, (p + 1) * w)

        def keys(c):
            return slice(c * ck, (c + 1) * ck)

        def fold(a):
            return a.reshape(ck // 8, 8, w)

        def scores(p, c):
            st = lax.dot_general(k_ref[keys(c), :], qs_ref[lanes(p), :], _NT,
                                 preferred_element_type=F32) * (ATTN_SCALE * LOG2E)
            st_ref[keys(c), lanes(p)] = st
            return jnp.max(fold(st), axis=0)

        def new_max(p, partial):
            m_prev = m_ref[:, lanes(p)]
            m_new = jnp.maximum(m_prev, jnp.max(functools.reduce(jnp.maximum, partial), axis=0, keepdims=True))
            m_ref[:, lanes(p)] = m_new
            return m_new, jnp.exp2(m_prev - m_new)

        def weights(p, c, m_new):
            pt = jnp.exp2(st_ref[keys(c), lanes(p)] - m_new)
            pv = lax.dot_general(v_ref[keys(c), :], pt.astype(BF16), (((0,), (0,)), ((), ())),
                                 preferred_element_type=F32)
            return jnp.sum(fold(pt), axis=0), pv

        partial = [scores(0, c) for c in range(nck)]
        for p in range(parts):
            m_new, alpha = new_max(p, partial)
            partial, sums, pvs = [], [], []
            for c in range(nck):
                if p + 1 < parts:
                    partial.append(scores(p + 1, c))
                s8, pv = weights(p, c, m_new)
                sums.append(s8)
                pvs.append(pv)
            l_ref[:, lanes(p)] = alpha * l_ref[:, lanes(p)] + jnp.sum(sum(sums), axis=0, keepdims=True)
            acc_ref[:, lanes(p)] = alpha * acc_ref[:, lanes(p)] + sum(pvs)

        @pl.when(ki == nk - 1)
        def _():
            out = jnp.transpose(acc_ref[...] / l_ref[...])
            lse = m_ref[...] + jnp.log2(l_ref[...])
            for g in range(GROUP):
                o_ref[:, g * HEAD_DIM:(g + 1) * HEAD_DIM] = out[g * tq:(g + 1) * tq, :]
                lse_ref[0, g:g + 1, :] = lse[:, g * tq:(g + 1) * tq]

    return pl.pallas_call(
        body, name=name, grid=(N_KV_HEADS, T // tq, nk),
        in_specs=[pl.BlockSpec((tq, GW), lambda h, i, j: (i, h)),
                  pl.BlockSpec((tk, HEAD_DIM), lambda h, i, j: (j, h)),
                  pl.BlockSpec((tk, HEAD_DIM), lambda h, i, j: (j, h))],
        out_specs=[pl.BlockSpec((tq, GW), lambda h, i, j: (i, h)),
                   pl.BlockSpec((1, GROUP, tq), lambda h, i, j: (h, 0, i))],
        out_shape=[jax.ShapeDtypeStruct((T, N_Q_HEADS * HEAD_DIM), F32),
                   jax.ShapeDtypeStruct((N_KV_HEADS, GROUP, T), F32)],
        scratch_shapes=[pltpu.VMEM((GROUP * tq, HEAD_DIM), BF16), pltpu.VMEM((1, GROUP * tq), F32),
                        pltpu.VMEM((1, GROUP * tq), F32), pltpu.VMEM((HEAD_DIM, GROUP * tq), F32),
                        pltpu.VMEM((tk, GROUP * tq), F32)],
        compiler_params=pltpu.CompilerParams(dimension_semantics=("parallel", "parallel", "arbitrary"),
                                             vmem_limit_bytes=ATTN_VMEM_LIMIT),
    )(q, k, v)


def _flash_bwd(q, k, v, do, lse, delta, name, tq=None, tk=None, token=None):
    T = q.shape[0]
    tq = tq or _pick(T, (ATTN_QUERY_ROWS, ROW))
    tk = tk or _pick(T, (1408, 768, 512, 256))
    nk = T // tk
    GW = GROUP * HEAD_DIM
    nt = (((1,), (1,)), ((), ()))
    extra = [] if token is None else [token]

    def body(q_ref, do_ref, k_ref, v_ref, lse_ref, dl_ref, *rest):
        dq_ref, dk_ref, dv_ref, qs_ref, dos_ref, dqt_ref = rest[len(extra):]
        qi = pl.program_id(1)
        ki = pl.program_id(2)

        @pl.when(ki == 0)
        def _():
            for g in range(GROUP):
                qs_ref[g * tq:(g + 1) * tq, :] = q_ref[:, g * HEAD_DIM:(g + 1) * HEAD_DIM]
                dos_ref[g * tq:(g + 1) * tq, :] = do_ref[:, g * HEAD_DIM:(g + 1) * HEAD_DIM]
            dqt_ref[...] = jnp.zeros(dqt_ref.shape, F32)

        kk = k_ref[...]
        vv = v_ref[...]

        def lanes(p):
            return slice(p * tq, (p + 1) * tq)

        def products(p):
            st = lax.dot_general(kk, qs_ref[lanes(p), :], nt, preferred_element_type=F32)
            dpt = lax.dot_general(vv, dos_ref[lanes(p), :], nt, preferred_element_type=F32)
            return st, dpt

        dk_c = jnp.zeros((tk, HEAD_DIM), F32)
        dv_c = jnp.zeros((tk, HEAD_DIM), F32)
        ahead = products(0)
        for p in range(GROUP):
            st, dpt = ahead
            if p + 1 < GROUP:
                ahead = products(p + 1)
            pt = jnp.exp2(st * (ATTN_SCALE * LOG2E) - lse_ref[0, p:p + 1, :])
            dst = ((pt * (dpt - dl_ref[0, p:p + 1, :])) * ATTN_SCALE).astype(BF16)
            dv_c = dv_c + jnp.dot(pt.astype(BF16), dos_ref[lanes(p), :], preferred_element_type=F32)
            dk_c = dk_c + jnp.dot(dst, qs_ref[lanes(p), :], preferred_element_type=F32)
            dqt_ref[:, lanes(p)] += lax.dot_general(kk, dst, (((0,), (0,)), ((), ())), preferred_element_type=F32)
        rows = pl.ds(pl.multiple_of(ki * tk, tk), tk)

        @pl.when(qi == 0)
        def _():
            dk_ref[rows, :] = dk_c
            dv_ref[rows, :] = dv_c

        @pl.when(qi > 0)
        def _():
            dk_ref[rows, :] += dk_c
            dv_ref[rows, :] += dv_c

        @pl.when(ki == nk - 1)
        def _():
            dqv = jnp.transpose(dqt_ref[...])
            for g in range(GROUP):
                dq_ref[:, g * HEAD_DIM:(g + 1) * HEAD_DIM] = dqv[g * tq:(g + 1) * tq, :]

    return pl.pallas_call(
        body, name=name, grid=(N_KV_HEADS, T // tq, nk),
        in_specs=[pl.BlockSpec((tq, GW), lambda h, i, j: (i, h)),
                  pl.BlockSpec((tq, GW), lambda h, i, j: (i, h)),
                  pl.BlockSpec((tk, HEAD_DIM), lambda h, i, j: (j, h)),
                  pl.BlockSpec((tk, HEAD_DIM), lambda h, i, j: (j, h)),
                  pl.BlockSpec((1, GROUP, tq), lambda h, i, j: (h, 0, i)),
                  pl.BlockSpec((1, GROUP, tq), lambda h, i, j: (h, 0, i))] +
                 [pl.BlockSpec(t.shape, lambda h, i, j: (0, 0)) for t in extra],
        out_specs=[pl.BlockSpec((tq, GW), lambda h, i, j: (i, h)),
                   pl.BlockSpec((T, HEAD_DIM), lambda h, i, j: (0, h)),
                   pl.BlockSpec((T, HEAD_DIM), lambda h, i, j: (0, h))],
        out_shape=[jax.ShapeDtypeStruct((T, N_Q_HEADS * HEAD_DIM), F32),
                   jax.ShapeDtypeStruct((T, N_KV_HEADS * HEAD_DIM), F32),
                   jax.ShapeDtypeStruct((T, N_KV_HEADS * HEAD_DIM), F32)],
        scratch_shapes=[pltpu.VMEM((GROUP * tq, HEAD_DIM), BF16), pltpu.VMEM((GROUP * tq, HEAD_DIM), BF16),
                        pltpu.VMEM((HEAD_DIM, GROUP * tq), F32)],
        compiler_params=pltpu.CompilerParams(dimension_semantics=("arbitrary", "arbitrary", "arbitrary"),
                                             vmem_limit_bytes=ATTN_VMEM_LIMIT),
    )(q, do, k, v, lse, delta, *extra)


def _gate_specs(D):
    w = D // 2
    first = (3 * D + (N_Q_HEADS + 2 * N_KV_HEADS) * HEAD_DIM) // w
    return [pl.BlockSpec((ROW, w), lambda i, c=first + j: (i, c)) for j in range(4)]


def _merge_fwd(o, P, conv_w, wbc, wba, wo, D, name):
    T = o.shape[0]
    nt = T // ROW
    w = D // 2
    conv_specs, conv_args = _conv_fwd_operands(P, conv_w, D)
    nc = len(conv_args)

    def body(*refs):
        o_ref, g0, g1, g2, g3, wbc_ref, wba_ref, wo_ref, yc_ref, a1_ref, a2_ref, z_ref, mo_ref = refs[nc:]
        yc_ref[...] = _conv_tile_fwd(refs[:nc], nt)
        a1 = jnp.dot(yc_ref[...], wbc_ref[...], preferred_element_type=F32)
        a2 = jnp.dot(o_ref[...].astype(BF16), wba_ref[...], preferred_element_type=F32)
        a1_ref[...] = a1
        a2_ref[...] = a2
        for j, (gc, ga) in enumerate(((g0, g2), (g1, g3))):
            sl = slice(j * w, (j + 1) * w)
            z = jax.nn.sigmoid(_f32(gc)) * a1[:, sl] + jax.nn.sigmoid(_f32(ga)) * a2[:, sl]
            z_ref[:, sl] = z.astype(BF16)
        mo_ref[...] = jnp.dot(z_ref[...], wo_ref[...], preferred_element_type=F32)

    return pl.pallas_call(
        body, name=name, grid=(T // ROW,),
        in_specs=conv_specs + [_row_spec(D)] + _gate_specs(D) + [_resident()] * 3,
        out_specs=[_row_spec(D)] * 5,
        out_shape=[jax.ShapeDtypeStruct((T, D), BF16), jax.ShapeDtypeStruct((T, D), F32),
                   jax.ShapeDtypeStruct((T, D), F32), jax.ShapeDtypeStruct((T, D), BF16),
                   jax.ShapeDtypeStruct((T, D), F32)],
        compiler_params=_params(("parallel",)),
    )(*conv_args, o, P, P, P, P, wbc, wba, wo)


def _merge_bwd(dmo, a1, a2, o, P, wbc, wba, wo, D, name):
    T = a1.shape[0]
    w = D // 2

    def body(dmo_ref, a1_ref, a2_ref, o_ref, g0, g1, g2, g3, wbc_ref, wba_ref, wo_ref,
             d1_ref, d2_ref, dg_ref, dyc_ref, dob_ref, dl_ref):
        dz = lax.dot_general(dmo_ref[...], wo_ref[...], _NT, preferred_element_type=F32)
        for j, (gc, ga) in enumerate(((g0, g2), (g1, g3))):
            sl = slice(j * w, (j + 1) * w)
            dzs = dz[:, sl]
            sc = jax.nn.sigmoid(_f32(gc))
            sa = jax.nn.sigmoid(_f32(ga))
            d1_ref[:, sl] = (dzs * sc).astype(BF16)
            d2_ref[:, sl] = (dzs * sa).astype(BF16)
            dg_ref[:, j * w:(j + 1) * w] = (dzs * a1_ref[:, sl] * (sc * (1.0 - sc))).astype(BF16)
            dg_ref[:, D + j * w:D + (j + 1) * w] = (dzs * a2_ref[:, sl] * (sa * (1.0 - sa))).astype(BF16)
        dyc_ref[...] = lax.dot_general(d1_ref[...], wbc_ref[...], _NT, preferred_element_type=F32)
        dov = lax.dot_general(d2_ref[...], wba_ref[...], _NT, preferred_element_type=F32)
        dob_ref[...] = dov.astype(BF16)
        prod = dov * o_ref[...]
        for h in range(N_Q_HEADS):
            d = jnp.sum(prod[:, h * HEAD_DIM:(h + 1) * HEAD_DIM], axis=1, keepdims=True)
            dl_ref[h // GROUP, (h % GROUP):(h % GROUP) + 1, :] = _to_row(d, ROW)

    return pl.pallas_call(
        body, name=name, grid=(T // ROW,),
        in_specs=[_row_spec(D)] * 4 + _gate_specs(D) + [_resident()] * 3,
        out_specs=[_row_spec(D), _row_spec(D), _row_spec(2 * D), _row_spec(D), _row_spec(D),
                   pl.BlockSpec((N_KV_HEADS, GROUP, ROW), lambda i: (0, 0, i))],
        out_shape=[jax.ShapeDtypeStruct((T, D), BF16), jax.ShapeDtypeStruct((T, D), BF16),
                   jax.ShapeDtypeStruct((T, 2 * D), BF16), jax.ShapeDtypeStruct((T, D), F32),
                   jax.ShapeDtypeStruct((T, D), BF16), jax.ShapeDtypeStruct((N_KV_HEADS, GROUP, T), F32)],
        compiler_params=_params(("parallel",)),
    )(dmo, a1, a2, o, P, P, P, P, wbc, wba, wo)


def _adamw_math(w, g, m, v):
    m = ADAM_B1 * m + (1.0 - ADAM_B1) * g
    v = ADAM_B2 * v + (1.0 - ADAM_B2) * (g * g)
    m_hat = m / (1.0 - ADAM_B1 ** ADAM_STEP)
    v_hat = v / (1.0 - ADAM_B2 ** ADAM_STEP)
    delta = -ADAM_LR * (m_hat / (jnp.sqrt(v_hat) + ADAM_EPS) + ADAM_WD * w)
    return delta, m, v


def _adamw(w, g, m, v, name):
    R, C = w.shape
    tr = _pick(R, tuple(t for t in (256, 128, 64, 32, 16, 8) if t * C * 4 <= ADAMW_BLOCK_BYTES))

    def body(w_ref, g_ref, m_ref, v_ref, d_ref, mo_ref, vo_ref):
        d, mn, vn = _adamw_math(w_ref[...], g_ref[...], m_ref[...], v_ref[...])
        d_ref[...] = d
        mo_ref[...] = mn
        vo_ref[...] = vn

    spec = pl.BlockSpec((tr, C), lambda i: (i, 0))
    return pl.pallas_call(
        body, name=name, grid=(R // tr,),
        in_specs=[spec] * 4, out_specs=[spec] * 3,
        out_shape=[jax.ShapeDtypeStruct((R, C), F32)] * 3,
        compiler_params=_params(("parallel",)),
    )(w, g, m, v)


def _norm_mix_in_fwd(xprev, branch, mods, g, gate, shift_idx, scale_idx, w_t, gq, gk, cos_t, sin_t, name):
    x_specs, x_args, (T, D) = _rows_operand(xprev)
    N = w_t.shape[0]
    QW = N_Q_HEADS * HEAD_DIM
    KW = N_KV_HEADS * HEAD_DIM
    q0, k0, v0 = 3 * D, 3 * D + QW, 3 * D + QW + KW
    edges = [0, D, 2 * D, q0, k0, v0 + KW] + list(range(v0 + KW + D, N + 1, D))
    assert edges[-1] == N

    def body(*refs):
        f_ref, m_ref, g_ref, w_ref, gq_ref, gk_ref, c_ref, s_ref = refs[len(x_args):len(x_args) + 8]
        xo_ref, h_ref, p_ref, qo_ref, ko_ref, vo_ref = refs[len(x_args) + 8:]
        m = m_ref[0]
        gate_idx, fac = gate
        x = _rows_tile(refs[:len(x_args)]) + (fac * m[gate_idx:gate_idx + 1, :]) * f_ref[...]
        xo_ref[...] = x
        hv = _norm_tile_fwd(x, m, g_ref[...], shift_idx, scale_idx)
        h_ref[...] = hv
        c = c_ref[...]
        s = s_ref[...]

        def head(xh, gain):
            inv = lax.rsqrt(jnp.mean(xh * xh, axis=-1, keepdims=True) + EPS)
            y = (xh * inv) * gain
            return y * c + _swap_halves(y) * s

        for lo, hi in zip(edges[:-1], edges[1:]):
            pb = lax.dot_general(hv, w_ref[lo:hi, :], _NT, preferred_element_type=F32).astype(BF16)
            p_ref[:, lo:hi] = pb
            if lo == q0:
                for h in range(N_Q_HEADS):
                    sl = slice(h * HEAD_DIM, (h + 1) * HEAD_DIM)
                    qo_ref[:, sl] = head(pb[:, sl].astype(F32), gq_ref[...]).astype(BF16)
            elif lo == k0:
                for h in range(N_KV_HEADS):
                    sl = slice(h * HEAD_DIM, (h + 1) * HEAD_DIM)
                    ko_ref[:, sl] = head(pb[:, sl].astype(F32), gk_ref[...]).astype(BF16)
                vo_ref[...] = pb[:, KW:2 * KW]

    return pl.pallas_call(
        body, name=name, grid=(T // ROW,),
        in_specs=x_specs + [_row_spec(D), _mods_spec(D), _vec_spec(1, D), _resident(),
                            _vec_spec(1, HEAD_DIM), _vec_spec(1, HEAD_DIM), _row_spec(HEAD_DIM), _row_spec(HEAD_DIM)],
        out_specs=[_row_spec(D), _row_spec(D), _row_spec(N), _row_spec(QW), _row_spec(KW), _row_spec(KW)],
        out_shape=[jax.ShapeDtypeStruct((T, D), F32), jax.ShapeDtypeStruct((T, D), BF16),
                   jax.ShapeDtypeStruct((T, N), BF16), jax.ShapeDtypeStruct((T, QW), BF16),
                   jax.ShapeDtypeStruct((T, KW), BF16), jax.ShapeDtypeStruct((T, KW), BF16)],
        compiler_params=_params(("parallel",)),
    )(*x_args, branch, mods, g, w_t, gq, gk, cos_t, sin_t)


def _mix_in_norm_bwd(dyc, dgt, P, conv_w, dq, dk, dv, gq, gk, cos_t, sin_t, w_t, x, dres, mods, g, shift_idx,
                     scale_idx, gate, branch, name):
    T, D = x.shape
    nt = T // ROW
    QW = N_Q_HEADS * HEAD_DIM
    KW = N_KV_HEADS * HEAD_DIM
    q0, g0 = 3 * D, 3 * D + QW + 2 * KW
    assert g0 + dgt.shape[1] == w_t.shape[0]
    conv_specs, conv_args = _conv_bwd_operands(P, dyc, conv_w, D)
    nc = len(conv_args)

    def body(*refs):
        (dg_ref, q_ref, k_ref, dq_ref, dk_ref, dv_ref, gq_ref, gk_ref, c_ref, s_ref,
         w_ref, x_ref, dr_ref, b_ref, m_ref, g_ref,
         dx_ref, db_ref, acc_ref, dc_ref, cacc_ref, o_ref, qacc_ref) = refs[nc:]
        _acc_init(acc_ref)
        _acc_init(cacc_ref)
        _acc_init(qacc_ref)
        _conv_tile_bwd(refs[:nc], dc_ref, cacc_ref, D, nt)
        dh = jnp.dot(dc_ref[...], w_ref[0:q0, :], preferred_element_type=F32)
        c = c_ref[...]
        s = s_ref[...]

        def head(xh, d, gain):
            dyv = d * c + _swap_halves(d * s)
            inv = lax.rsqrt(jnp.mean(xh * xh, axis=-1, keepdims=True) + EPS)
            xn = xh * inv
            dxn = dyv * gain
            dxh = inv * (dxn - xn * jnp.mean(dxn * xn, axis=-1, keepdims=True))
            return dxh, jnp.sum(dyv * xn, axis=0, keepdims=True)

        dgq = jnp.zeros((1, HEAD_DIM), F32)
        for h in range(N_Q_HEADS):
            sl = slice(h * HEAD_DIM, (h + 1) * HEAD_DIM)
            dxh, dgh = head(q_ref[:, sl].astype(F32), dq_ref[:, sl], gq_ref[...])
            o_ref[:, sl] = dxh.astype(BF16)
            dgq = dgq + dgh
        dh = dh + jnp.dot(dg_ref[...], w_ref[g0:, :], preferred_element_type=F32)
        dgk = jnp.zeros((1, HEAD_DIM), F32)
        for h in range(N_KV_HEADS):
            sl = slice(h * HEAD_DIM, (h + 1) * HEAD_DIM)
            dxh, dgh = head(k_ref[:, sl].astype(F32), dk_ref[:, sl], gk_ref[...])
            o_ref[:, QW + h * HEAD_DIM:QW + (h + 1) * HEAD_DIM] = dxh.astype(BF16)
            dgk = dgk + dgh
        o_ref[:, QW + KW:QW + 2 * KW] = dv_ref[...].astype(BF16)
        qacc_ref[0, 0:1, 0:HEAD_DIM] += dgq
        qacc_ref[0, 1:2, 0:HEAD_DIM] += dgk
        dh = dh + jnp.dot(o_ref[...], w_ref[q0:g0, :], preferred_element_type=F32)
        m = m_ref[0]
        dx = _norm_tile_bwd(x_ref[...], dh, dr_ref[...], m, g_ref[...], shift_idx, scale_idx, acc_ref)
        dx_ref[...] = dx
        db_ref[...] = _gate_tile_bwd(dx, b_ref[...], m, gate, acc_ref)

    return pl.pallas_call(
        body, name=name, grid=(T // ROW,),
        in_specs=conv_specs +
                 [_row_spec(dgt.shape[1]), _row_spec(QW, q0 // QW), _row_spec(KW, (q0 + QW) // KW),
                  _row_spec(QW), _row_spec(KW), _row_spec(KW), _vec_spec(1, HEAD_DIM), _vec_spec(1, HEAD_DIM),
                  _row_spec(HEAD_DIM), _row_spec(HEAD_DIM),
                  _resident(), _row_spec(D), _row_spec(D), _row_spec(D), _mods_spec(D), _vec_spec(1, D)],
        out_specs=[_row_spec(D), _row_spec(D), _acc_spec(D), _row_spec(q0), _acc_spec(D),
                   _row_spec(QW + 2 * KW), _acc_spec(D)],
        out_shape=[jax.ShapeDtypeStruct((T, D), F32), jax.ShapeDtypeStruct((T, D), BF16),
                   jax.ShapeDtypeStruct((2, ACC_ROWS, D), F32), jax.ShapeDtypeStruct((T, q0), BF16),
                   jax.ShapeDtypeStruct((2, ACC_ROWS, D), F32), jax.ShapeDtypeStruct((T, QW + 2 * KW), BF16),
                   jax.ShapeDtypeStruct((2, ACC_ROWS, D), F32)],
        compiler_params=_params(("arbitrary",)),
    )(*conv_args, dgt, P, P, dq, dk, dv, gq, gk, cos_t, sin_t, w_t, x, dres, branch, mods, g)


def _adamw_transposed(w, gt, m, v, name):
    R, C = w.shape
    tc = 128

    def body(w_ref, g_ref, m_ref, v_ref, go_ref, d_ref, mo_ref, vo_ref):
        g = jnp.transpose(g_ref[...])
        d, mn, vn = _adamw_math(w_ref[...], g, m_ref[...], v_ref[...])
        go_ref[...] = g
        d_ref[...] = d
        mo_ref[...] = mn
        vo_ref[...] = vn

    spec = pl.BlockSpec((R, tc), lambda j: (0, j))
    return pl.pallas_call(
        body, name=name, grid=(C // tc,),
        in_specs=[spec, pl.BlockSpec((tc, R), lambda j: (j, 0)), spec, spec], out_specs=[spec] * 4,
        out_shape=[jax.ShapeDtypeStruct((R, C), F32)] * 4,
        compiler_params=_params(("parallel",)),
    )(w, gt, m, v)


class _NoExchange:
    def __init__(self, rest):
        self.rest = rest

    def rest_weights(self, after):
        return self.rest

    def reduce_early(self, grads, tag):
        return None


def _local_step(xcat, target, mods, norm_g, final_g, gq, gk, conv_w, ffn1_w, hooks, ctx_len):
    T, D = _rows_operand(xcat)[2]
    w1i, w1o = ffn1_w
    g1, g2, g3 = norm_g
    cos_t, sin_t = _rope_tables(ctx_len, T - ctx_len)

    def after(value, token, name):
        return value if token is None else _after(value, token, name)

    _, h1, u1, s1, f1 = _norm_ffn_fwd(xcat, None, mods, g1, None, 0, 1, w1i, w1o, "f_ffn1")
    wi, wbc, wba, wo, w2i, w2o = hooks.rest_weights(f1)
    x1, h2, P, qn, kn, vb = _norm_mix_in_fwd(xcat, f1, mods, g2, (2, 0.5), 3, 4, wi, gq, gk, cos_t, sin_t, "f_mix_in")
    o, lse = _flash_fwd(qn, kn, vb, "f_attn")
    yc, a1, a2, z, mo = _merge_fwd(o, P, conv_w, wbc, wba, wo, D, "f_merge")
    x2, h3, u2, s2, dx3, df2, acc_head = _norm_ffn_fwd(x1, mo, mods, g3, (5, 1.0), 6, 7, w2i, w2o, "f_ffn2",
                                                       head=(final_g, target))

    du2, dx2, dmo, acc_n3 = _ffn_norm_bwd(df2, u2, w2i, w2o, x2, dx3, mods, g3, 6, 7, (5, 1.0), mo, "b_ffn2")
    g_w2o = _matmul(s2, df2, "tn", BF16, "b_ffn2_out_dw")
    g_w2i = _matmul(du2, h3, "tn", BF16, "b_ffn2_in_dw")

    g_wo = _matmul(z, dmo, "tn", BF16, "b_mix_out_dw")
    da1, da2, dgt, dyc, dob, delta = _merge_bwd(dmo, a1, a2, o, P, wbc, wba, wo, D, "b_merge")
    g_wbc = _matmul(yc, da1, "tn", BF16, "b_branch_conv_dw")
    g_wba = _matmul(o, da2, "tn", BF16, "b_branch_attn_dw")
    token_a = hooks.reduce_early([g_wbc, g_wba, g_wo, g_w2i, g_w2o], "a")
    dq, dk, dv = _flash_bwd(qn, kn, vb, dob, lse, delta, "b_attn", token=token_a)
    dx1, df1, acc_n2, dconv, acc_conv, dqkv, acc_qk = _mix_in_norm_bwd(
        dyc, dgt, P, conv_w, dq, dk, dv, gq, gk, cos_t, sin_t, wi, x1, dx2, mods, g2, 3, 4, (2, 0.5), f1, "b_mix_in")
    d_parts = (dconv, dqkv, dgt)
    g_wi = jnp.concatenate([_matmul(dp, h2, "tn", BF16, f"b_mix_in_dw_{i}") for i, dp in enumerate(d_parts)], axis=0)
    g1_b = after(g1, hooks.reduce_early([g_wi], "b"), "after_rs_b")

    du1, grad_x, _, acc_n1 = _ffn_norm_bwd(df1, u1, w1i, w1o, xcat, dx1, mods, g1_b, 0, 1, None, None, "b_ffn1",
                                           skip_first_tile=True)
    g_w1o = _matmul(s1, df1, "tn", BF16, "b_ffn1_out_dw")
    g_w1i = _matmul(du1, h1, "tn", BF16, "b_ffn1_in_dw", token=hooks.reduce_early([g_w1o], "c"))

    grads = (g_w1i, g_w1o, g_wi, g_wbc, g_wba, g_wo, g_w2i, g_w2o)
    accs = (acc_head, acc_n3, acc_n2, acc_n1, acc_conv, acc_qk)
    return grad_x, grads, accs


def _place():
    return lax.axis_index("x"), lax.axis_index("y"), lax.axis_index("c")


def _other_chips(x, y):
    return [(1 - x, y), (x, 1 - y), (1 - x, 1 - y)]


def _allgather8(v, name):
    R, N = v.shape

    def body(v_ref, out_ref, send_sems, recv_sems, local_sem):
        x, y, c = _place()
        me, sibling = (x, y, c), (x, y, 1 - c)
        chips = _other_chips(x, y)

        def blk(px, py, pc):
            return out_ref.at[4 * px + 2 * py + pc]

        def copy(k, block, to, src=None):
            return pltpu.make_async_remote_copy(
                src_ref=blk(*block) if src is None else src, dst_ref=blk(*block),
                send_sem=send_sems.at[k], recv_sem=recv_sems.at[k], device_id=to, device_id_type=MESH)

        mine = pltpu.make_async_copy(v_ref, blk(*me), local_sem)
        mine.start()
        first = [copy(0, me, sibling, src=v_ref)]
        first += [copy(1 + j, me, (*chip, c), src=v_ref) for j, chip in enumerate(chips)]
        for cp in first:
            cp.start()
        passed = [copy(4 + j, (*chip, c), sibling) for j, chip in enumerate(chips)]
        for j, chip in enumerate(chips):
            copy(1 + j, (*chip, c), me).wait_recv()
            passed[j].start()
        copy(0, sibling, me).wait_recv()
        for j, chip in enumerate(chips):
            copy(4 + j, (*chip, 1 - c), me).wait_recv()
        for cp in first + passed:
            cp.wait_send()
        mine.wait()

    return pl.pallas_call(
        body, name=name,
        out_shape=jax.ShapeDtypeStruct((N_DEV, R, N), v.dtype),
        in_specs=[pl.BlockSpec(memory_space=pltpu.VMEM)],
        out_specs=pl.BlockSpec(memory_space=pltpu.VMEM),
        scratch_shapes=[pltpu.SemaphoreType.DMA((7,)), pltpu.SemaphoreType.DMA((7,)), pltpu.SemaphoreType.DMA],
        compiler_params=pltpu.CompilerParams(vmem_limit_bytes=VMEM_LIMIT),
    )(v)


def _any_specs(n):
    return [pl.BlockSpec(memory_space=pl.ANY)] * n


def _pair_exchange(grads, name):
    n = len(grads)

    def body(*refs):
        g, land = refs[:n], refs[n:2 * n]
        send_sems, recv_sems = refs[2 * n:]
        x, y, c = _place()
        sibling = (x, y, 1 - c)
        copies = []
        for t in range(n):
            half = grads[t].shape[0] // (2 * N_CHIPS)
            for s in range(N_CHIPS):
                cp = pltpu.make_async_remote_copy(
                    src_ref=g[t].at[pl.ds((2 * s + 1 - c) * half, half), :], dst_ref=land[t].at[s],
                    send_sem=send_sems.at[N_CHIPS * t + s], recv_sem=recv_sems.at[N_CHIPS * t + s],
                    device_id=sibling, device_id_type=MESH)
                cp.start()
                copies.append(cp)
        for cp in copies:
            cp.wait_recv()
        for cp in copies:
            cp.wait_send()

    return pl.pallas_call(
        body, name=name,
        out_shape=[jax.ShapeDtypeStruct((N_CHIPS, a.shape[0] // (2 * N_CHIPS), a.shape[1]), a.dtype) for a in grads],
        in_specs=_any_specs(n), out_specs=_any_specs(n),
        scratch_shapes=[pltpu.SemaphoreType.DMA((N_CHIPS * n,)), pltpu.SemaphoreType.DMA((N_CHIPS * n,))],
    )(*grads)


def _place_shard(w2, idx, transpose, name, token):
    if transpose:
        D, rs = w2.shape
        tr = 128
        in_spec = pl.BlockSpec((D, tr), lambda i, idx: (0, i))
    else:
        rs, D = w2.shape
        tr = _pick(rs, (352, 256, 128, 64, 32, 16))
        in_spec = pl.BlockSpec((tr, D), lambda i, idx: (i, 0))
    steps = rs // tr

    def body(idx_ref, w_ref, t_ref, o_ref):
        v = w_ref[...]
        o_ref[...] = (jnp.transpose(v) if transpose else v).astype(BF16)

    return pl.pallas_call(
        body, name=name,
        grid_spec=pltpu.PrefetchScalarGridSpec(
            num_scalar_prefetch=1, grid=(steps,),
            in_specs=[in_spec, pl.BlockSpec(token.shape, lambda i, idx: (0, 0))],
            out_specs=pl.BlockSpec((tr, D), lambda i, idx: (idx[1] * steps + i, 0))),
        out_shape=jax.ShapeDtypeStruct((N_CHIPS * rs, D), BF16),
        compiler_params=_params(("arbitrary",)),
    )(idx, w2, token)


def _pair_sum(g, landed, idx, name, token=None):
    _, half, D = landed.shape
    g4 = g.reshape(N_CHIPS, 2, half, D)
    tr = _pick(half, (416, 352, 128))
    extra = [] if token is None else [token]

    def body(idx_ref, g_ref, l_ref, *rest):
        rest[-1][...] = (g_ref[0].astype(F32) + l_ref[...].astype(F32)).astype(BF16)

    return pl.pallas_call(
        body, name=name,
        grid_spec=pltpu.PrefetchScalarGridSpec(
            num_scalar_prefetch=1, grid=(N_CHIPS, half // tr),
            in_specs=[pl.BlockSpec((1, 1, tr, D), lambda s, i, idx: (idx[1 + s], idx[0], i, 0)),
                      pl.BlockSpec((1, tr, D), lambda s, i, idx: (idx[1 + s], i, 0))] +
                     [pl.BlockSpec(t.shape, lambda s, i, idx: (0, 0)) for t in extra],
            out_specs=pl.BlockSpec((1, tr, D), lambda s, i, idx: (s, i, 0))),
        out_shape=jax.ShapeDtypeStruct((N_CHIPS, half, D), BF16),
        compiler_params=_params(("arbitrary", "arbitrary")),
    )(idx, g4, landed, *extra)


_HBM = pl.BlockSpec(memory_space=pltpu.HBM)
_SEM = pl.BlockSpec(memory_space=pltpu.SEMAPHORE)
_EFFECT = pltpu.SideEffectType.DATAFLOW_SIDE_EFFECTING


def _in_hbm(a):
    return pltpu.with_memory_space_constraint(a, pltpu.HBM)


def _split_copies(n, per, make):
    def start(nbuf, name, bufs):
        def body(*refs):
            ins = refs[:nbuf]
            send_sems, recv_sems = refs[nbuf], refs[nbuf + 1]
            token = refs[-1]
            for t in range(n):
                for j in range(per):
                    make(ins, t, j, send_sems.at[per * t + j], recv_sems.at[per * t + j]).start()
            token[...] = jnp.zeros(token.shape, token.dtype)

        out = pl.pallas_call(
            body, name=name,
            out_shape=(pltpu.SemaphoreType.DMA((per * n,)), pltpu.SemaphoreType.DMA((per * n,)),
                       *[pltpu.HBM(b.shape, b.dtype) for b in bufs], jax.ShapeDtypeStruct((8, 128), F32)),
            in_specs=[_HBM] * nbuf,
            out_specs=(_SEM, _SEM, *[_HBM] * nbuf, pl.BlockSpec(memory_space=pltpu.VMEM)),
            input_output_aliases={i: 2 + i for i in range(nbuf)},
            compiler_params=pltpu.CompilerParams(has_side_effects=_EFFECT),
        )(*[_in_hbm(b) for b in bufs])
        return out[0], out[1], list(out[2:2 + nbuf]), out[-1]

    def wait(nbuf, name, send_sems, recv_sems, bufs, after):
        def body(*refs):
            ins = refs[:nbuf]
            ss, rs = refs[nbuf], refs[nbuf + 1]
            for t in range(n):
                for j in range(per):
                    cp = make(ins, t, j, ss.at[per * t + j], rs.at[per * t + j])
                    cp.wait_send()
                    cp.wait_recv()

        return pl.pallas_call(
            body, name=name,
            out_shape=[pltpu.HBM(b.shape, b.dtype) for b in bufs],
            in_specs=[_HBM] * nbuf + [_SEM, _SEM, pl.BlockSpec(memory_space=pl.ANY)],
            out_specs=[_HBM] * nbuf,
            input_output_aliases={i: i for i in range(nbuf)},
            compiler_params=pltpu.CompilerParams(has_side_effects=_EFFECT),
        )(*bufs, send_sems, recv_sems, after)

    return start, wait


def _chip_exchange_split(n):
    def make(bufs, t, j, send_sem, recv_sem):
        x, y, c = _place()
        chip = _other_chips(x, y)[j]
        return pltpu.make_async_remote_copy(src_ref=bufs[t].at[1 + j], dst_ref=bufs[n + t].at[j], send_sem=send_sem,
                                            recv_sem=recv_sem, device_id=(*chip, c), device_id_type=MESH)

    return _split_copies(n, 3, make)


def _weights_gather_split(fulls):
    def make(bufs, t, j, send_sem, recv_sem):
        x, y, c = _place()
        chip = _other_chips(x, y)[j]
        rs = fulls[t].shape[0] // N_CHIPS
        rows = bufs[t].at[pl.ds((2 * x + y) * rs + c * (rs // 2), rs // 2), :]
        return pltpu.make_async_remote_copy(src_ref=rows, dst_ref=rows, send_sem=send_sem, recv_sem=recv_sem,
                                            device_id=(*chip, c), device_id_type=MESH)

    return _split_copies(len(fulls), 3, make)


def _weights_pass_on(fulls, name):
    n = len(fulls)

    def body(*refs):
        full = refs[n:2 * n]
        send_sems, recv_sems = refs[2 * n:]
        x, y, c = _place()
        chips = _other_chips(x, y)

        def copy(t, j, h):
            rs = fulls[t].shape[0] // N_CHIPS
            px, py = chips[j]
            rows = full[t].at[pl.ds((2 * px + py) * rs + h * (rs // 2), rs // 2), :]
            return pltpu.make_async_remote_copy(src_ref=rows, dst_ref=rows, send_sem=send_sems.at[3 * t + j],
                                                recv_sem=recv_sems.at[3 * t + j], device_id=(x, y, 1 - c),
                                                device_id_type=MESH)

        for t in range(n):
            for j in range(3):
                copy(t, j, c).start()
        for t in range(n):
            for j in range(3):
                copy(t, j, 1 - c).wait_recv()
        for t in range(n):
            for j in range(3):
                copy(t, j, c).wait_send()

    return pl.pallas_call(
        body, name=name,
        out_shape=[jax.ShapeDtypeStruct(f.shape, f.dtype) for f in fulls],
        in_specs=_any_specs(n), out_specs=_any_specs(n),
        input_output_aliases={t: t for t in range(n)},
        scratch_shapes=[pltpu.SemaphoreType.DMA((3 * n,)), pltpu.SemaphoreType.DMA((3 * n,))],
    )(*fulls)


def _after(value, token, name):
    def body(v_ref, t_ref, o_ref):
        o_ref[...] = v_ref[...]

    return pl.pallas_call(
        body, name=name, out_shape=jax.ShapeDtypeStruct(value.shape, value.dtype),
        in_specs=_whole(2), out_specs=pl.BlockSpec(memory_space=pltpu.VMEM),
    )(value, token)


def _chip_sum(ps, landed, idx, name):
    _, half, D = ps.shape
    tr = _pick(half, (416, 352, 128))
    steps = half // tr

    def body(idx_ref, p_ref, l_ref, o_ref):
        acc = p_ref[0].astype(F32)
        for j in range(3):
            acc = acc + l_ref[j].astype(F32)
        o_ref[...] = acc

    return pl.pallas_call(
        body, name=name,
        grid_spec=pltpu.PrefetchScalarGridSpec(
            num_scalar_prefetch=1, grid=(steps,),
            in_specs=[pl.BlockSpec((1, tr, D), lambda i, idx: (0, i, 0)),
                      pl.BlockSpec((3, tr, D), lambda i, idx: (0, i, 0))],
            out_specs=pl.BlockSpec((tr, D), lambda i, idx: (idx[0] * steps + i, 0))),
        out_shape=jax.ShapeDtypeStruct((2 * half, D), F32),
        compiler_params=_params(("arbitrary",)),
    )(idx, ps, landed)


def _pair_swap(shards, name):
    n = len(shards)

    def body(*refs):
        full = refs[n:2 * n]
        send_sems, recv_sems = refs[2 * n:]
        x, y, c = _place()

        def half(t, h):
            rows = shards[t].shape[0] // 2
            return full[t].at[pl.ds(h * rows, rows), :]

        def copy(t, h):
            return pltpu.make_async_remote_copy(src_ref=half(t, h), dst_ref=half(t, h), send_sem=send_sems.at[t],
                                                recv_sem=recv_sems.at[t], device_id=(x, y, 1 - c),
                                                device_id_type=MESH)

        for t in range(n):
            copy(t, c).start()
        for t in range(n):
            copy(t, 1 - c).wait_recv()
        for t in range(n):
            copy(t, c).wait_send()

    return pl.pallas_call(
        body, name=name,
        out_shape=[jax.ShapeDtypeStruct(a.shape, a.dtype) for a in shards],
        in_specs=_any_specs(n), out_specs=_any_specs(n),
        input_output_aliases={t: t for t in range(n)},
        scratch_shapes=[pltpu.SemaphoreType.DMA((n,)), pltpu.SemaphoreType.DMA((n,))],
    )(*shards)


def _gather_begin(fulls, tag):
    start, wait = _weights_gather_split(fulls)
    send_sems, recv_sems, bufs, token = start(len(fulls), f"ag_{tag}_start", fulls)
    return (wait, send_sems, recv_sems, bufs), token


def _gather_end(state, after, tag):
    wait, send_sems, recv_sems, bufs = state
    landed = wait(len(bufs), f"ag_{tag}_wait", send_sems, recv_sems, bufs, after)
    return _weights_pass_on(landed, f"ag_{tag}_pass_on")


class _Exchanges:
    def __init__(self, fulls_rest, idx):
        self.idx = idx
        self._rest, self.token = _gather_begin(fulls_rest, "rest")
        self._early = []

    def rest_weights(self, after):
        return _gather_end(self._rest, after, "rest")

    def reduce_early(self, grads, tag, token=None):
        landed = _pair_exchange(grads, "rs_pair_exchange_" + tag)
        sums = [_pair_sum(g, l, self.idx, f"rs_pair_sum_{tag}{t}", token)
                for t, (g, l) in enumerate(zip(grads, landed))]
        zones = [lax.empty((3,) + s.shape[1:], s.dtype) for s in sums]
        start, wait = _chip_exchange_split(len(sums))
        send_sems, recv_sems, bufs, token = start(2 * len(sums), "rs_chip_start_" + tag, sums + zones)
        self._early.append((tag, wait, send_sems, recv_sems, bufs))
        return token

    def finish(self, tags, after):
        halves = []
        for tag, wait, send_sems, recv_sems, bufs in self._early:
            if tag in tags:
                n = len(bufs) // 2
                done = wait(len(bufs), "rs_chip_wait_" + tag, send_sems, recv_sems, bufs, after)
                halves += [_chip_sum(p, l, self.idx, f"rs_chip_sum_{tag}{t}")
                           for t, (p, l) in enumerate(zip(done[:n], done[n:]))]
        return halves


N_MOD = 9
PACK_HEAD, PACK_N3, PACK_N2, PACK_N1, PACK_CONV, PACK_QK = 0, 16, 32, 48, 64, 80
PACK_ROWS = 96
MOD_SRC = ((PACK_N1, 0), (PACK_N1, 1), (PACK_N2, 3), (PACK_N2, 0), (PACK_N2, 1),
           (PACK_N3, 3), (PACK_N3, 0), (PACK_N3, 1), (PACK_HEAD, 2))
CTX_ROW = 8


def _silu(v):
    return v * jax.nn.sigmoid(v)


def _whole(n):
    return [pl.BlockSpec(memory_space=pltpu.VMEM)] * n


def _mod_rows(cin, w_sh, b_sh, name):
    def body(c_ref, w_ref, b_ref, o_ref):
        a = _silu(c_ref[...]).astype(BF16)
        o_ref[...] = jnp.dot(a, w_ref[...].astype(BF16), preferred_element_type=F32) + b_ref[...]

    return pl.pallas_call(
        body, name=name, out_shape=jax.ShapeDtypeStruct((cin.shape[0], w_sh.shape[1]), F32),
        in_specs=_whole(3), out_specs=pl.BlockSpec(memory_space=pltpu.VMEM),
        compiler_params=pltpu.CompilerParams(vmem_limit_bytes=VMEM_LIMIT),
    )(cin, w_sh, b_sh)


def _small_reduce(gathered, name):
    _, _, D = gathered.shape

    def body(g_ref, loss_ref, db_ref, gn_ref, cv_ref, qk_ref, dm_ref):
        tot = g_ref[0]
        for r in range(1, N_DEV):
            tot = tot + g_ref[r]

        def both(block, row):
            return tot[block + row:block + row + 1, :] + tot[block + 8 + row:block + 8 + row + 1, :]

        loss = jnp.sum(both(PACK_HEAD, 0), axis=1, keepdims=True)
        loss_ref[...] = jnp.broadcast_to(loss, loss_ref.shape)
        db_ref[...] = jnp.zeros(db_ref.shape, F32)
        dm_ref[...] = jnp.zeros(dm_ref.shape, F32)
        for j, (block, row) in enumerate(MOD_SRC):
            db_ref[j:j + 1, :] = both(block, row)
            dm_ref[CTX_ROW, j:j + 1, :] = tot[block + row:block + row + 1, :]
            for r in range(N_DEV):
                dm_ref[r, j:j + 1, :] = g_ref[r, block + 8 + row:block + 8 + row + 1, :]
        gn_ref[...] = jnp.zeros(gn_ref.shape, F32)
        gn_ref[0:1, :] = both(PACK_N1, 2)
        gn_ref[8:9, :] = both(PACK_N2, 2)
        gn_ref[16:17, :] = both(PACK_N3, 2)
        gn_ref[24:25, :] = both(PACK_HEAD, 1)
        cv_ref[...] = jnp.zeros(cv_ref.shape, F32)
        for r in range(3):
            cv_ref[r:r + 1, :] = both(PACK_CONV, r)
        qk_ref[...] = jnp.zeros(qk_ref.shape, F32)
        qk_ref[0:1, 0:HEAD_DIM] = both(PACK_QK, 0)[:, 0:HEAD_DIM]
        qk_ref[0:1, HEAD_DIM:2 * HEAD_DIM] = both(PACK_QK, 1)[:, 0:HEAD_DIM]

    return pl.pallas_call(
        body, name=name,
        out_shape=[jax.ShapeDtypeStruct((8, 128), F32), jax.ShapeDtypeStruct((16, D), F32),
                   jax.ShapeDtypeStruct((32, D), F32), jax.ShapeDtypeStruct((8, D), F32),
                   jax.ShapeDtypeStruct((8, D), F32), jax.ShapeDtypeStruct((16, 16, D), F32)],
        in_specs=_whole(1), out_specs=_whole(6),
        compiler_params=pltpu.CompilerParams(vmem_limit_bytes=VMEM_LIMIT),
    )(gathered)


def _wmod_grad(cin, dm_sh, w_sh, name):
    def body(c_ref, d_ref, w_ref, gw_ref, cp_ref):
        a = _silu(c_ref[...]).astype(BF16)
        d = d_ref[...].astype(BF16)
        gw_ref[...] = lax.dot_general(a, d, (((0,), (0,)), ((), ())), preferred_element_type=F32)
        cp_ref[...] = lax.dot_general(d, w_ref[...].astype(BF16), (((1,), (1,)), ((), ())),
                                      preferred_element_type=F32)

    return pl.pallas_call(
        body, name=name,
        out_shape=[jax.ShapeDtypeStruct(w_sh.shape, F32), jax.ShapeDtypeStruct(cin.shape, F32)],
        in_specs=_whole(3), out_specs=_whole(2),
        compiler_params=pltpu.CompilerParams(vmem_limit_bytes=VMEM_LIMIT),
    )(cin, dm_sh, w_sh)


def _cctx_grad(parts, c_ctx8, name):
    def body(p_ref, c_ref, o_ref):
        tot = p_ref[0] + p_ref[2] + p_ref[4] + p_ref[6]
        cv = c_ref[...]
        sig = jax.nn.sigmoid(cv)
        rows = lax.broadcasted_iota(jnp.int32, tot.shape, 0)
        o_ref[...] = jnp.where(rows == 0, tot * (sig * (1.0 + cv * (1.0 - sig))), 0.0)

    return pl.pallas_call(
        body, name=name, out_shape=jax.ShapeDtypeStruct(c_ctx8.shape, F32),
        in_specs=_whole(2), out_specs=pl.BlockSpec(memory_space=pltpu.VMEM),
    )(parts, c_ctx8)


def _pad_rows(a, rows):
    return jnp.pad(a, ((0, rows - a.shape[0]), (0, 0)))


def _pack_small(c_ctx, b_mod, n1, n2, n3, final_g, gq, gk, conv_sh, D):
    misc = jnp.concatenate([gq, gk, conv_sh.reshape(1, -1)], axis=1)
    return jnp.concatenate([_pad_rows(c_ctx[None], 8), _pad_rows(b_mod.reshape(N_MOD, D), 16), _pad_rows(n1, 8),
                            _pad_rows(n2, 8), _pad_rows(n3, 8), _pad_rows(final_g[None], 8), _pad_rows(misc, 8)], axis=0)


def _unpack_small(p, D, conv_shape):
    misc = p[56:57]
    return dict(c_ctx=p[0], b_mod=p[8:8 + N_MOD].reshape(1, N_MOD * D), norm1_g=p[24:25], norm2_g=p[32:33],
                norm3_g=p[40:41], final_g=p[48], q_norm_g=misc[:, 0:HEAD_DIM], k_norm_g=misc[:, HEAD_DIM:2 * HEAD_DIM],
                conv_w=misc[:, 2 * HEAD_DIM:].reshape(conv_shape))


WEIGHT_ORDER = ("c_ctx", "w_mod", "b_mod", "norm1_g", "norm2_g", "norm3_g", "ffn1_w_in", "ffn1_w_out", "w_in",
                "conv_w", "q_norm_g", "k_norm_g", "w_branch_conv", "w_branch_attn", "w_out", "ffn2_w_in",
                "ffn2_w_out", "final_g")
BIG = ("ffn1_w_in", "ffn1_w_out", "w_in", "w_branch_conv", "w_branch_attn", "w_out", "ffn2_w_in", "ffn2_w_out")
COLUMN_SHARDED = ("ffn1_w_in", "w_in", "ffn2_w_in")


def kernel(x, c, ctx, c_ctx, w_mod, b_mod, norm1_g, norm2_g, norm3_g, ffn1_w_in, ffn1_w_out, w_in, conv_w, q_norm_g, k_norm_g, w_branch_conv, w_branch_attn, w_out, ffn2_w_in, ffn2_w_out, final_g, loss_target, m_c_ctx, m_w_mod, m_b_mod, m_norm1_g, m_norm2_g, m_norm3_g, m_ffn1_w_in, m_ffn1_w_out, m_w_in, m_conv_w, m_q_norm_g, m_k_norm_g, m_w_branch_conv, m_w_branch_attn, m_w_out, m_ffn2_w_in, m_ffn2_w_out, m_final_g, v_c_ctx, v_w_mod, v_b_mod, v_norm1_g, v_norm2_g, v_norm3_g, v_ffn1_w_in, v_ffn1_w_out, v_w_in, v_conv_w, v_q_norm_g, v_k_norm_g, v_w_branch_conv, v_w_branch_attn, v_w_out, v_ffn2_w_in, v_ffn2_w_out, v_final_g):
    w = dict(c_ctx=c_ctx, w_mod=w_mod, b_mod=b_mod, norm1_g=norm1_g, norm2_g=norm2_g, norm3_g=norm3_g,
             ffn1_w_in=ffn1_w_in, ffn1_w_out=ffn1_w_out, w_in=w_in, conv_w=conv_w, q_norm_g=q_norm_g,
             k_norm_g=k_norm_g, w_branch_conv=w_branch_conv, w_branch_attn=w_branch_attn, w_out=w_out,
             ffn2_w_in=ffn2_w_in, ffn2_w_out=ffn2_w_out, final_g=final_g)
    m = dict(c_ctx=m_c_ctx, w_mod=m_w_mod, b_mod=m_b_mod, norm1_g=m_norm1_g, norm2_g=m_norm2_g, norm3_g=m_norm3_g,
             ffn1_w_in=m_ffn1_w_in, ffn1_w_out=m_ffn1_w_out, w_in=m_w_in, conv_w=m_conv_w, q_norm_g=m_q_norm_g,
             k_norm_g=m_k_norm_g, w_branch_conv=m_w_branch_conv, w_branch_attn=m_w_branch_attn, w_out=m_w_out,
             ffn2_w_in=m_ffn2_w_in, ffn2_w_out=m_ffn2_w_out, final_g=m_final_g)
    v = dict(c_ctx=v_c_ctx, w_mod=v_w_mod, b_mod=v_b_mod, norm1_g=v_norm1_g, norm2_g=v_norm2_g, norm3_g=v_norm3_g,
             ffn1_w_in=v_ffn1_w_in, ffn1_w_out=v_ffn1_w_out, w_in=v_w_in, conv_w=v_conv_w, q_norm_g=v_q_norm_g,
             k_norm_g=v_k_norm_g, w_branch_conv=v_w_branch_conv, w_branch_attn=v_w_branch_attn, w_out=v_w_out,
             ffn2_w_in=v_ffn2_w_in, ffn2_w_out=v_ffn2_w_out, final_g=v_final_g)

    xi, yi, ci = _place()
    dev = 4 * xi + 2 * yi + ci
    shard = 2 * xi + yi
    idx = jnp.stack([ci, shard, 2 * (1 - xi) + yi, 2 * xi + (1 - yi), 2 * (1 - xi) + (1 - yi)]).astype(jnp.int32)
    D = x.shape[-1]
    ctx_len = ctx.shape[1]
    assert ctx_len == ROW and c.shape == (1, D)
    mcols = w_mod.shape[2]
    ccols = conv_w.shape[2]

    def place(names, token):
        fulls = []
        for n in names:
            fulls.append(_place_shard(w[n][0], idx, n in COLUMN_SHARDED, "place_" + n, token))
            token = fulls[-1][:16, :HEAD_DIM]
        return fulls

    ffn1_gather, ffn1_token = _gather_begin(place(BIG[:2], c), "ffn1")
    fulls_rest = place(BIG[2:], ffn1_token)

    c8 = _after(jnp.broadcast_to(c, (8, D)), fulls_rest[-1][:16, :HEAD_DIM], "after_place")
    c_all = _allgather8(c8, "ag_c")[:, 0, :]
    cin = jnp.concatenate([c_all, _pad_rows(c_ctx[None], 8)], axis=0)
    b_sh = lax.dynamic_slice(b_mod, (0, shard * mcols), (1, mcols))
    mod_sh = _mod_rows(cin, w_mod[0], b_sh, "mod_rows")
    conv_rows = jnp.pad(conv_w[0], ((0, 8 - conv_w.shape[1]), (0, mcols - ccols)))
    mod_all = _allgather8(jnp.concatenate([mod_sh, conv_rows], axis=0), "ag_mod")
    mod_full = jnp.concatenate([mod_all[2 * s, :16] for s in range(N_CHIPS)], axis=1)
    conv_full = jnp.concatenate([mod_all[2 * s, 16:16 + conv_w.shape[1], :ccols] for s in range(N_CHIPS)], axis=1)
    mod_lat = lax.dynamic_slice(mod_full, (dev, 0), (1, N_MOD * D)).reshape(N_MOD, D)
    mod_ctx = mod_full[CTX_ROW].reshape(N_MOD, D)
    mods = jnp.stack([_pad_rows(mod_ctx, 16), _pad_rows(mod_lat, 16)])

    ffn1_w = _gather_end(ffn1_gather, mods, "ffn1")
    hooks = _Exchanges(fulls_rest, idx)

    xcat = (ctx[0], x[0])
    norm1_first = _after(norm1_g, hooks.token, "after_ag_rest")
    grad_x, grads, accs = _local_step(xcat, loss_target[0], mods, (norm1_first, norm2_g, norm3_g), final_g[None],
                                      q_norm_g, k_norm_g, conv_full, ffn1_w, hooks, ctx_len)
    g = {}

    pack = jnp.concatenate([a.reshape(2 * ACC_ROWS, D) for a in accs], axis=0)
    gathered = _allgather8(pack, "ag_small")
    loss8, db_mod, g_norms, g_conv, g_qk, dm = _small_reduce(gathered, "small_reduce")
    dm_sh = lax.dynamic_slice(dm[:, :N_MOD, :].reshape(16, N_MOD * D), (0, shard * mcols), (16, mcols))
    g_wmod, cpart = _wmod_grad(cin, dm_sh, w_mod[0], "wmod_grad")
    g["w_mod"] = g_wmod[None]
    cparts = _allgather8(cpart[CTX_ROW:CTX_ROW + 8], "ag_cctx")
    g_cctx = _cctx_grad(cparts, _pad_rows(c_ctx[None], 8), "cctx_grad")
    g_conv_sh = lax.dynamic_slice(g_conv, (0, shard * ccols), (conv_w.shape[1], ccols))
    g_misc = jnp.concatenate([g_qk[0:1, 0:2 * HEAD_DIM], g_conv_sh.reshape(1, -1)], axis=1)
    g_pack = jnp.concatenate([g_cctx, db_mod, g_norms, _pad_rows(g_misc, 8)], axis=0)

    def packed(p):
        return _pack_small(p["c_ctx"], p["b_mod"], p["norm1_g"], p["norm2_g"], p["norm3_g"], p["final_g"],
                           p["q_norm_g"], p["k_norm_g"], p["conv_w"][0], D)

    d_pack, m_pack, v_pack = _adamw(packed(w), g_pack, packed(m), packed(v), "adamw_small")

    g.update(_unpack_small(g_pack, D, conv_w.shape))
    delta = _unpack_small(d_pack, D, conv_w.shape)
    new_m = _unpack_small(m_pack, D, conv_w.shape)
    new_v = _unpack_small(v_pack, D, conv_w.shape)

    def update(n, g2):
        if n in COLUMN_SHARDED:
            g2, d2, m2, v2 = _adamw_transposed(w[n][0], g2, m[n][0], v[n][0], "adamw_" + n)
        else:
            d2, m2, v2 = _adamw(w[n][0], g2, m[n][0], v[n][0], "adamw_" + n)
        g[n], delta[n], new_m[n], new_v[n] = g2[None], d2[None], m2[None], v2[None]
        return v2

    token_d = hooks.reduce_early([grads[0]], "d", token=d_pack[:8, :HEAD_DIM])
    h_wbc, h_wba, h_wo, h_w2i, h_w2o, h_wi, h_w1o = hooks.finish("abc", token_d)
    done = _pair_swap([h_w1o, h_wi, h_wbc, h_wba, h_wo, h_w2i, h_w2o], "rs_pair_swap")
    last = update("w_mod", g_wmod)
    for n, r in zip(BIG[1:], done):
        last = update(n, r)
    (h_w1i,) = hooks.finish("d", last)
    update(BIG[0], _pair_swap([h_w1i], "rs_pair_swap_d")[0])

    loss = loss8[0, 0]
    return (loss, grad_x[None], *[g[n] for n in WEIGHT_ORDER], *[delta[n] for n in WEIGHT_ORDER],
            *[new_m[n] for n in WEIGHT_ORDER], *[new_v[n] for n in WEIGHT_ORDER])
```

```python
import functools

import jax
import jax.numpy as jnp
from jax import lax
from jax.experimental import pallas as pl
from jax.experimental.pallas import tpu as pltpu

F32 = jnp.float32
BF16 = jnp.bfloat16

HEAD_DIM = 128
N_Q_HEADS = 8
N_KV_HEADS = 2
GROUP = N_Q_HEADS // N_KV_HEADS
GRID_W = 64
ROPE_THETA = 10000.0
EPS = 1e-6
ATTN_SCALE = HEAD_DIM ** -0.5

ADAM_LR = 0.001
ADAM_B1 = 0.9
ADAM_B2 = 0.999
ADAM_EPS = 1e-08
ADAM_WD = 0.01
ADAM_STEP = 10

ROW = 256
HALO = 16
ACC_ROWS = 8
N_CHIPS = 4
N_DEV = 8
MESH = pl.DeviceIdType.MESH
VMEM_LIMIT = 48 * 1024 * 1024
ADAMW_BLOCK_BYTES = 1024 * 1024


def _pick(n, prefs):
    for p in prefs:
        if n % p == 0:
            return p
    return n


def _params(sem):
    return pltpu.CompilerParams(dimension_semantics=sem, vmem_limit_bytes=VMEM_LIMIT)


def _stream(i):
    return jnp.minimum(i, 1)


def _grad_matmul(a, b, name, token=None):
    (T, M), (T2, N) = a.shape, b.shape
    assert T == T2, (a.shape, b.shape)
    tm = _pick(M, (1664, 1408, 1024, 512, 256, 128))
    tk = _pick(T, (2816, 1408, 768, 512, 256))
    nk = T // tk
    extra = [] if token is None else [token]

    def body(a_ref, b_ref, *rest):
        o_ref, acc_ref = rest[len(extra):]
        p = lax.dot_general(a_ref[...].astype(BF16), b_ref[...].astype(BF16), (((0,), (0,)), ((), ())),
                            preferred_element_type=F32)
        k = pl.program_id(1)

        @pl.when(k == 0)
        def _():
            acc_ref[...] = p

        @pl.when(k > 0)
        def _():
            acc_ref[...] += p

        @pl.when(k == nk - 1)
        def _():
            o_ref[...] = acc_ref[...].astype(BF16)

    return pl.pallas_call(
        body, name=name, grid=(M // tm, nk),
        in_specs=[pl.BlockSpec((tk, tm), lambda i, k: (k, i)), pl.BlockSpec((tk, N), lambda i, k: (k, 0))] +
                 [pl.BlockSpec(t.shape, lambda i, k: (0, 0)) for t in extra],
        out_specs=pl.BlockSpec((tm, N), lambda i, k: (i, 0)),
        out_shape=jax.ShapeDtypeStruct((M, N), BF16),
        scratch_shapes=[pltpu.VMEM((tm, N), F32)],
        compiler_params=_params(("parallel", "arbitrary")),
    )(a, b, *extra)


def _row_spec(width, col=0):
    return pl.BlockSpec((ROW, width), lambda i, col=col: (i, col))


def _mods_spec(D):
    return pl.BlockSpec((1, 16, D), lambda i: (_stream(i), 0, 0))


def _acc_spec(D):
    return pl.BlockSpec((1, ACC_ROWS, D), lambda i: (_stream(i), 0, 0))


def _vec_spec(rows, D):
    return pl.BlockSpec((rows, D), lambda i: (0, 0))


def _acc_init(acc_ref):
    i = pl.program_id(0)

    @pl.when(i <= 1)
    def _():
        acc_ref[...] = jnp.zeros_like(acc_ref)


def _acc_add(acc_ref, row, val):
    acc_ref[0, row:row + 1, :] += jnp.sum(val, axis=0, keepdims=True)


def _rows_operand(x):
    if not isinstance(x, tuple):
        return [_row_spec(x.shape[1])], [x], x.shape
    ctx, lat = x
    D = lat.shape[1]
    assert ctx.shape == (ROW, D)
    specs = [pl.BlockSpec((ROW, D), lambda i: (0, 0)), pl.BlockSpec((ROW, D), lambda i: (jnp.maximum(i - 1, 0), 0))]
    return specs, [ctx, lat], (ROW + lat.shape[0], D)


def _rows_tile(refs):
    if len(refs) == 1:
        return refs[0][...]
    return jnp.where(pl.program_id(0) == 0, refs[0][...], refs[1][...])


def _norm_tile_fwd(x, m, g, shift_idx, scale_idx):
    inv = lax.rsqrt(jnp.mean(x * x, axis=-1, keepdims=True) + EPS)
    y = (x * inv) * g
    return (y * (1.0 + m[scale_idx:scale_idx + 1, :]) + m[shift_idx:shift_idx + 1, :]).astype(BF16)


def _norm_tile_bwd(x, dh, dres, m, g, shift_idx, scale_idx, acc_ref):
    inv = lax.rsqrt(jnp.mean(x * x, axis=-1, keepdims=True) + EPS)
    xn = x * inv
    dy = dh * (1.0 + m[scale_idx:scale_idx + 1, :])
    dxn = dy * g
    _acc_add(acc_ref, 0, dh)
    _acc_add(acc_ref, 1, dh * (xn * g))
    _acc_add(acc_ref, 2, dy * xn)
    return inv * (dxn - xn * jnp.mean(dxn * xn, axis=-1, keepdims=True)) + dres


def _gate_tile_bwd(dx, branch, m, gate, acc_ref):
    gate_idx, fac = gate
    _acc_add(acc_ref, 3, fac * dx * branch)
    return ((fac * m[gate_idx:gate_idx + 1, :]) * dx).astype(BF16)


_NT = (((1,), (1,)), ((), ()))


def _ffn_chunk(F):
    return _pick(F, (2816, 1408, 512, 256, 128))


def _resident():
    return pl.BlockSpec(memory_space=pltpu.VMEM)


def _ffn_tile_fwd(hv, wi_ref, wo_ref, u_ref, s_ref, F, cw):
    acc = jnp.zeros((hv.shape[0], wo_ref.shape[1]), F32)
    for j in range(F // cw):
        a = lax.dot_general(hv, wi_ref[j * cw:(j + 1) * cw, :], _NT, preferred_element_type=F32)
        b = lax.dot_general(hv, wi_ref[F + j * cw:F + (j + 1) * cw, :], _NT, preferred_element_type=F32)
        s = ((a * jax.nn.sigmoid(a)) * b).astype(BF16)
        u_ref[:, j * cw:(j + 1) * cw] = a.astype(BF16)
        u_ref[:, F + j * cw:F + (j + 1) * cw] = b.astype(BF16)
        s_ref[:, j * cw:(j + 1) * cw] = s
        acc = acc + jnp.dot(s, wo_ref[j * cw:(j + 1) * cw, :], preferred_element_type=F32)
    return acc


def _norm_ffn_fwd(xprev, branch, mods, g, gate, shift_idx, scale_idx, w_in_t, w_out, name, head=None):
    x_specs, x_args, (T, D) = _rows_operand(xprev)
    F = w_out.shape[0]
    cw = _ffn_chunk(F)
    has_res = branch is not None
    n_in = len(x_args) + int(has_res) + 4 + (2 if head else 0)

    def body(*refs):
        ins, outs = list(refs[:n_in]), list(refs[n_in:])
        x = _rows_tile([ins.pop(0) for _ in x_args])
        f_ref = ins.pop(0) if has_res else None
        m_ref, g_ref, wi_ref, wo_ref = ins[:4]
        xo_ref = outs.pop(0) if has_res else None
        h_ref, u_ref, s_ref = outs[:3]
        m = m_ref[0]
        if has_res:
            gate_idx, fac = gate
            x = x + (fac * m[gate_idx:gate_idx + 1, :]) * f_ref[...]
            xo_ref[...] = x
        hv = _norm_tile_fwd(x, m, g_ref[...], shift_idx, scale_idx)
        h_ref[...] = hv
        f = _ffn_tile_fwd(hv, wi_ref, wo_ref, u_ref, s_ref, F, cw)
        if head is None:
            outs[3][...] = f
            return
        fg_ref, t_ref = ins[4:6]
        dx_ref, df_ref, acc_ref = outs[3:6]
        _acc_init(acc_ref)
        lat = (pl.program_id(0) > 0).astype(F32)
        gate8 = 0.5 * m[8:9, :]
        x3 = x + gate8 * f
        inv3 = lax.rsqrt(jnp.mean(x3 * x3, axis=-1, keepdims=True) + EPS)
        xn = x3 * inv3
        fg = fg_ref[...]
        e = (xn * fg - t_ref[...]) * lat
        dy = e * (1.0 / D)
        dxn = dy * fg
        dx = inv3 * (dxn - xn * jnp.mean(dxn * xn, axis=-1, keepdims=True))
        dx_ref[...] = dx
        df_ref[...] = (gate8 * dx).astype(BF16)
        _acc_add(acc_ref, 0, (0.5 / D) * e * e)
        _acc_add(acc_ref, 1, dy * xn)
        _acc_add(acc_ref, 2, 0.5 * dx * f)

    in_specs = x_specs + ([_row_spec(D)] if has_res else []) + \
               [_mods_spec(D), _vec_spec(1, D), _resident(), _resident()]
    args = x_args + ([branch] if has_res else []) + [mods, g, w_in_t, w_out]
    out_specs = ([_row_spec(D)] if has_res else []) + [_row_spec(D), _row_spec(2 * F), _row_spec(F)]
    out_shape = ([jax.ShapeDtypeStruct((T, D), F32)] if has_res else []) + \
                [jax.ShapeDtypeStruct((T, D), BF16), jax.ShapeDtypeStruct((T, 2 * F), BF16),
                 jax.ShapeDtypeStruct((T, F), BF16)]
    if head is None:
        out_specs += [_row_spec(D)]
        out_shape += [jax.ShapeDtypeStruct((T, D), F32)]
    else:
        in_specs += [_vec_spec(1, D), pl.BlockSpec((ROW, D), lambda i: (jnp.maximum(i - 1, 0), 0))]
        args += list(head)
        out_specs += [_row_spec(D), _row_spec(D), _acc_spec(D)]
        out_shape += [jax.ShapeDtypeStruct((T, D), F32), jax.ShapeDtypeStruct((T, D), BF16),
                      jax.ShapeDtypeStruct((2, ACC_ROWS, D), F32)]
    out = pl.pallas_call(
        body, name=name, grid=(T // ROW,), in_specs=in_specs, out_specs=out_specs, out_shape=out_shape,
        compiler_params=_params(("arbitrary",) if head else ("parallel",)),
    )(*args)
    return tuple(out) if has_res else (None,) + tuple(out)


def _ffn_norm_bwd(df, u, w_in_t, w_out, x, dres, mods, g, shift_idx, scale_idx, gate, branch, name,
                  skip_first_tile=False):
    T, D = df.shape
    F = w_out.shape[0]
    cw = _ffn_chunk(F)
    nt = T // ROW
    has_gate = gate is not None
    x_specs, x_args, _ = _rows_operand(x)
    n_in = 7 + len(x_args) + int(has_gate)

    def body(*refs):
        ins, outs = list(refs[:n_in]), list(refs[n_in:])
        df_ref, u_ref, wi_ref, wo_ref = ins[:4]
        x_refs = ins[4:4 + len(x_args)]
        dr_ref = ins[4 + len(x_args)]
        b_ref = ins[5 + len(x_args)] if has_gate else None
        m_ref, g_ref = ins[-2:]
        du_ref, dx_ref = outs[:2]
        db_ref = outs[2] if has_gate else None
        acc_ref = outs[-1]
        _acc_init(acc_ref)
        dfv = df_ref[...]
        dh = jnp.zeros((ROW, D), F32)
        for j in range(F // cw):
            ds = lax.dot_general(dfv, wo_ref[j * cw:(j + 1) * cw, :], _NT, preferred_element_type=F32)
            a = u_ref[:, j * cw:(j + 1) * cw].astype(F32)
            b = u_ref[:, F + j * cw:F + (j + 1) * cw].astype(F32)
            sig = jax.nn.sigmoid(a)
            da = (ds * b * (sig * (1.0 + a * (1.0 - sig)))).astype(BF16)
            db = (ds * (a * sig)).astype(BF16)
            du_ref[:, j * cw:(j + 1) * cw] = da
            du_ref[:, F + j * cw:F + (j + 1) * cw] = db
            dh = dh + jnp.dot(da, wi_ref[j * cw:(j + 1) * cw, :], preferred_element_type=F32)
            dh = dh + jnp.dot(db, wi_ref[F + j * cw:F + (j + 1) * cw, :], preferred_element_type=F32)
        m = m_ref[0]
        dx = _norm_tile_bwd(_rows_tile(x_refs), dh, dr_ref[...], m, g_ref[...], shift_idx, scale_idx, acc_ref)
        dx_ref[...] = dx
        if has_gate:
            db_ref[...] = _gate_tile_bwd(dx, b_ref[...], m, gate, acc_ref)

    in_specs = [_row_spec(D), _row_spec(2 * F), _resident(), _resident()] + x_specs + [_row_spec(D)] + \
               ([_row_spec(D)] if has_gate else []) + [_mods_spec(D), _vec_spec(1, D)]
    args = [df, u, w_in_t, w_out] + x_args + [dres] + ([branch] if has_gate else []) + [mods, g]
    if skip_first_tile:
        dx_spec = pl.BlockSpec((ROW, D), lambda i: (jnp.maximum(i - 1, 0), 0))
        dx_shape = jax.ShapeDtypeStruct((T - ROW, D), F32)
    else:
        dx_spec = _row_spec(D)
        dx_shape = jax.ShapeDtypeStruct((T, D), F32)
    out_specs = [_row_spec(2 * F), dx_spec] + ([_row_spec(D)] if has_gate else []) + [_acc_spec(D)]
    out_shape = [jax.ShapeDtypeStruct((T, 2 * F), BF16), dx_shape] + \
                ([jax.ShapeDtypeStruct((T, D), BF16)] if has_gate else []) + \
                [jax.ShapeDtypeStruct((2, ACC_ROWS, D), F32)]
    out = pl.pallas_call(
        body, name=name, grid=(nt,), in_specs=in_specs, out_specs=out_specs, out_shape=out_shape,
        compiler_params=_params(("arbitrary",)),
    )(*args)
    if has_gate:
        return tuple(out)
    return out[0], out[1], None, out[2]


def _halo_specs(width, col, nt):
    per = ROW // HALO
    prev = pl.BlockSpec((HALO, width), lambda i, col=col: (jnp.maximum(i * per - 1, 0), col))
    nxt = pl.BlockSpec((HALO, width), lambda i, col=col: (jnp.minimum((i + 1) * per, nt * per - 1), col))
    return prev, nxt


def _f32(ref):
    return ref[...].astype(F32)


def _last_row(halo_ref):
    return halo_ref[HALO - 1:HALO, :].astype(F32)


def _first_row(halo_ref):
    return halo_ref[0:1, :].astype(F32)


def _shift_rows(v, prev_row, next_row):
    rows = lax.broadcasted_iota(jnp.int32, v.shape, 0)
    down = jnp.where(rows == 0, prev_row, pltpu.roll(v, 1, 0))
    up = jnp.where(rows == v.shape[0] - 1, next_row, pltpu.roll(v, v.shape[0] - 1, 0))
    return down, up


def _conv_fwd_operands(P, conv_w, D):
    nt = P.shape[0] // ROW
    cg_p, cg_n = _halo_specs(D, 1, nt)
    vc_p, vc_n = _halo_specs(D, 2, nt)
    specs = [_row_spec(D, 0), _row_spec(D, 1), _row_spec(D, 2), cg_p, vc_p, cg_n, vc_n, _vec_spec(3, D)]
    return specs, [P, P, P, P, P, P, P, conv_w]


def _conv_tile_fwd(refs, nt):
    bg_ref, cg_ref, vc_ref, cgp_ref, vcp_ref, cgn_ref, vcn_ref, w_ref = refs
    i = pl.program_id(0)
    has_prev = (i != 1).astype(F32)
    has_next = (i != nt - 1).astype(F32)
    u = _f32(cg_ref) * _f32(vc_ref)
    up_row = _last_row(cgp_ref) * _last_row(vcp_ref) * has_prev
    un_row = _first_row(cgn_ref) * _first_row(vcn_ref) * has_next
    um1, up1 = _shift_rows(u, up_row, un_row)
    w = w_ref[...]
    conv = um1 * w[0:1, :] + u * w[1:2, :] + up1 * w[2:3, :]
    return (_f32(bg_ref) * conv).astype(BF16)


def _conv_bwd_operands(P, dy, conv_w, D):
    nt = P.shape[0] // ROW
    bg_p, bg_n = _halo_specs(D, 0, nt)
    cg_p, cg_n = _halo_specs(D, 1, nt)
    vc_p, vc_n = _halo_specs(D, 2, nt)
    dy_p, dy_n = _halo_specs(D, 0, nt)
    specs = [_row_spec(D, 0), _row_spec(D, 1), _row_spec(D, 2), _row_spec(D, 0),
             bg_p, cg_p, vc_p, dy_p, bg_n, cg_n, vc_n, dy_n, _vec_spec(3, D)]
    return specs, [P, P, P, dy, P, P, P, dy, P, P, P, dy, conv_w]


def _conv_tile_bwd(refs, o_ref, acc_ref, D, nt):
    (bg_ref, cg_ref, vc_ref, dy_ref, bgp_ref, cgp_ref, vcp_ref, dyp_ref,
     bgn_ref, cgn_ref, vcn_ref, dyn_ref, w_ref) = refs
    i = pl.program_id(0)
    lat = (i > 0).astype(F32)
    has_prev = (i != 1).astype(F32)
    has_next = (i != nt - 1).astype(F32)
    bg = _f32(bg_ref)
    cg = _f32(cg_ref)
    vc = _f32(vc_ref)
    dyv = dy_ref[...] * lat
    u = cg * vc
    up_row = _last_row(cgp_ref) * _last_row(vcp_ref) * has_prev
    un_row = _first_row(cgn_ref) * _first_row(vcn_ref) * has_next
    um1, up1 = _shift_rows(u, up_row, un_row)
    w = w_ref[...]
    conv = um1 * w[0:1, :] + u * w[1:2, :] + up1 * w[2:3, :]
    dc = dyv * bg
    dcp_row = _last_row(dyp_ref) * _last_row(bgp_ref) * has_prev
    dcn_row = _first_row(dyn_ref) * _first_row(bgn_ref) * has_next
    dcm1, dcp1 = _shift_rows(dc, dcp_row, dcn_row)
    du = dcp1 * w[0:1, :] + dc * w[1:2, :] + dcm1 * w[2:3, :]
    o_ref[:, 0:D] = (dyv * conv).astype(BF16)
    o_ref[:, D:2 * D] = (du * vc * lat).astype(BF16)
    o_ref[:, 2 * D:3 * D] = (du * cg * lat).astype(BF16)
    _acc_add(acc_ref, 0, dc * um1)
    _acc_add(acc_ref, 1, dc * u)
    _acc_add(acc_ref, 2, dc * up1)


def _rope_tables(ctx_len, seq):
    n_freq = HEAD_DIM // 4
    rows = seq // GRID_W
    inv = ROPE_THETA ** (-jnp.arange(n_freq, dtype=F32) / n_freq)
    ar = jnp.arange(rows, dtype=F32)[:, None] * inv
    ac = jnp.arange(GRID_W, dtype=F32)[:, None] * inv

    def per_row(a):
        return jnp.repeat(a, GRID_W, axis=0)

    def per_col(a):
        return jnp.tile(a, (rows, 1))

    cos_t = jnp.concatenate([per_row(jnp.cos(ar)), per_row(jnp.cos(ar)), per_col(jnp.cos(ac)), per_col(jnp.cos(ac))], axis=1)
    sin_t = jnp.concatenate([per_row(-jnp.sin(ar)), per_row(jnp.sin(ar)), per_col(-jnp.sin(ac)), per_col(jnp.sin(ac))], axis=1)
    cos_t = jnp.concatenate([jnp.ones((ctx_len, HEAD_DIM), F32), cos_t], axis=0)
    sin_t = jnp.concatenate([jnp.zeros((ctx_len, HEAD_DIM), F32), sin_t], axis=0)
    return cos_t, sin_t


def _swap_halves(y):
    lanes = lax.broadcasted_iota(jnp.int32, y.shape, 1)
    first = (lanes % 64) < 32
    return jnp.where(first, pltpu.roll(y, HEAD_DIM - 32, 1), pltpu.roll(y, 32, 1))


def _to_row(col, n):
    return jnp.transpose(jnp.broadcast_to(col, (n, HEAD_DIM)))[0:1, :]


LOG2E = 1.4426950408889634
ATTN_PART_LANES = 256
ATTN_QUERY_ROWS = 768
ATTN_VMEM_LIMIT = 60 * 1024 * 1024


def _flash_fwd(q, k, v, name, tq=None, tk=None):
    T = q.shape[0]
    tq = tq or _pick(T, (ATTN_QUERY_ROWS, ROW))
    parts = GROUP * tq // ATTN_PART_LANES
    tk = tk or _pick(T, (2816, 1408, 768, 512, 256))
    ck = tk
    nk = T // tk
    GW = GROUP * HEAD_DIM

    def body(q_ref, k_ref, v_ref, o_ref, lse_ref, qs_ref, m_ref, l_ref, acc_ref, st_ref):
        ki = pl.program_id(2)

        @pl.when(ki == 0)
        def _():
            for g in range(GROUP):
                qs_ref[g * tq:(g + 1) * tq, :] = q_ref[:, g * HEAD_DIM:(g + 1) * HEAD_DIM]
            m_ref[...] = jnp.full(m_ref.shape, -jnp.inf, F32)
            l_ref[...] = jnp.zeros(l_ref.shape, F32)
            acc_ref[...] = jnp.zeros(acc_ref.shape, F32)

        w = ATTN_PART_LANES
        nck = tk // ck

        def lanes(p):
            return slice(p * w, (p + 1) * w)

        def keys(c):
            return slice(c * ck, (c + 1) * ck)

        def fold(a):
            return a.reshape(ck // 8, 8, w)

        def scores(p, c):
            st = lax.dot_general(k_ref[keys(c), :], qs_ref[lanes(p), :], _NT,
                                 preferred_element_type=F32) * (ATTN_SCALE * LOG2E)
            st_ref[keys(c), lanes(p)] = st
            return jnp.max(fold(st), axis=0)

        def new_max(p, partial):
            m_prev = m_ref[:, lanes(p)]
            m_new = jnp.maximum(m_prev, jnp.max(functools.reduce(jnp.maximum, partial), axis=0, keepdims=True))
            m_ref[:, lanes(p)] = m_new
            return m_new, jnp.exp2(m_prev - m_new)

        def weights(p, c, m_new):
            pt = jnp.exp2(st_ref[keys(c), lanes(p)] - m_new)
            pv = lax.dot_general(v_ref[keys(c), :], pt.astype(BF16), (((0,), (0,)), ((), ())),
                                 preferred_element_type=F32)
            return jnp.sum(fold(pt), axis=0), pv

        partial = [scores(0, c) for c in range(nck)]
        for p in range(parts):
            m_new, alpha = new_max(p, partial)
            partial, sums, pvs = [], [], []
            for c in range(nck):
                if p + 1 < parts:
                    partial.append(scores(p + 1, c))
                s8, pv = weights(p, c, m_new)
                sums.append(s8)
                pvs.append(pv)
            l_ref[:, lanes(p)] = alpha * l_ref[:, lanes(p)] + jnp.sum(sum(sums), axis=0, keepdims=True)
            acc_ref[:, lanes(p)] = alpha * acc_ref[:, lanes(p)] + sum(pvs)

        @pl.when(ki == nk - 1)
        def _():
            out = jnp.transpose(acc_ref[...] / l_ref[...])
            lse = m_ref[...] + jnp.log2(l_ref[...])
            for g in range(GROUP):
                o_ref[:, g * HEAD_DIM:(g + 1) * HEAD_DIM] = out[g * tq:(g + 1) * tq, :]
                lse_ref[0, g:g + 1, :] = lse[:, g * tq:(g + 1) * tq]

    return pl.pallas_call(
        body, name=name, grid=(N_KV_HEADS, T // tq, nk),
        in_specs=[pl.BlockSpec((tq, GW), lambda h, i, j: (i, h)),
                  pl.BlockSpec((tk, HEAD_DIM), lambda h, i, j: (j, h)),
                  pl.BlockSpec((tk, HEAD_DIM), lambda h, i, j: (j, h))],
        out_specs=[pl.BlockSpec((tq, GW), lambda h, i, j: (i, h)),
                   pl.BlockSpec((1, GROUP, tq), lambda h, i, j: (h, 0, i))],
        out_shape=[jax.ShapeDtypeStruct((T, N_Q_HEADS * HEAD_DIM), F32),
                   jax.ShapeDtypeStruct((N_KV_HEADS, GROUP, T), F32)],
        scratch_shapes=[pltpu.VMEM((GROUP * tq, HEAD_DIM), BF16), pltpu.VMEM((1, GROUP * tq), F32),
                        pltpu.VMEM((1, GROUP * tq), F32), pltpu.VMEM((HEAD_DIM, GROUP * tq), F32),
                        pltpu.VMEM((tk, GROUP * tq), F32)],
        compiler_params=pltpu.CompilerParams(dimension_semantics=("parallel", "parallel", "arbitrary"),
                                             vmem_limit_bytes=ATTN_VMEM_LIMIT),
    )(q, k, v)


def _flash_bwd(q, k, v, do, lse, delta, name, tq=None, tk=None, token=None):
    T = q.shape[0]
    tq = tq or _pick(T, (ATTN_QUERY_ROWS, ROW))
    tk = tk or _pick(T, (1408, 768, 512, 256))
    nk = T // tk
    GW = GROUP * HEAD_DIM
    nt = (((1,), (1,)), ((), ()))
    extra = [] if token is None else [token]

    def body(q_ref, do_ref, k_ref, v_ref, lse_ref, dl_ref, *rest):
        dq_ref, dk_ref, dv_ref, qs_ref, dos_ref, dqt_ref = rest[len(extra):]
        qi = pl.program_id(1)
        ki = pl.program_id(2)

        @pl.when(ki == 0)
        def _():
            for g in range(GROUP):
                qs_ref[g * tq:(g + 1) * tq, :] = q_ref[:, g * HEAD_DIM:(g + 1) * HEAD_DIM]
                dos_ref[g * tq:(g + 1) * tq, :] = do_ref[:, g * HEAD_DIM:(g + 1) * HEAD_DIM]
            dqt_ref[...] = jnp.zeros(dqt_ref.shape, F32)

        kk = k_ref[...]
        vv = v_ref[...]

        def lanes(p):
            return slice(p * tq, (p + 1) * tq)

        def products(p):
            st = lax.dot_general(kk, qs_ref[lanes(p), :], nt, preferred_element_type=F32)
            dpt = lax.dot_general(vv, dos_ref[lanes(p), :], nt, preferred_element_type=F32)
            return st, dpt

        dk_c = jnp.zeros((tk, HEAD_DIM), F32)
        dv_c = jnp.zeros((tk, HEAD_DIM), F32)
        ahead = products(0)
        for p in range(GROUP):
            st, dpt = ahead
            if p + 1 < GROUP:
                ahead = products(p + 1)
            pt = jnp.exp2(st * (ATTN_SCALE * LOG2E) - lse_ref[0, p:p + 1, :])
            dst = ((pt * (dpt - dl_ref[0, p:p + 1, :])) * ATTN_SCALE).astype(BF16)
            dv_c = dv_c + jnp.dot(pt.astype(BF16), dos_ref[lanes(p), :], preferred_element_type=F32)
            dk_c = dk_c + jnp.dot(dst, qs_ref[lanes(p), :], preferred_element_type=F32)
            dqt_ref[:, lanes(p)] += lax.dot_general(kk, dst, (((0,), (0,)), ((), ())), preferred_element_type=F32)
        rows = pl.ds(pl.multiple_of(ki * tk, tk), tk)

        @pl.when(qi == 0)
        def _():
            dk_ref[rows, :] = dk_c
            dv_ref[rows, :] = dv_c

        @pl.when(qi > 0)
        def _():
            dk_ref[rows, :] += dk_c
            dv_ref[rows, :] += dv_c

        @pl.when(ki == nk - 1)
        def _():
            dqv = jnp.transpose(dqt_ref[...])
            for g in range(GROUP):
                dq_ref[:, g * HEAD_DIM:(g + 1) * HEAD_DIM] = dqv[g * tq:(g + 1) * tq, :]

    return pl.pallas_call(
        body, name=name, grid=(N_KV_HEADS, T // tq, nk),
        in_specs=[pl.BlockSpec((tq, GW), lambda h, i, j: (i, h)),
                  pl.BlockSpec((tq, GW), lambda h, i, j: (i, h)),
                  pl.BlockSpec((tk, HEAD_DIM), lambda h, i, j: (j, h)),
                  pl.BlockSpec((tk, HEAD_DIM), lambda h, i, j: (j, h)),
                  pl.BlockSpec((1, GROUP, tq), lambda h, i, j: (h, 0, i)),
                  pl.BlockSpec((1, GROUP, tq), lambda h, i, j: (h, 0, i))] +
                 [pl.BlockSpec(t.shape, lambda h, i, j: (0, 0)) for t in extra],
        out_specs=[pl.BlockSpec((tq, GW), lambda h, i, j: (i, h)),
                   pl.BlockSpec((T, HEAD_DIM), lambda h, i, j: (0, h)),
                   pl.BlockSpec((T, HEAD_DIM), lambda h, i, j: (0, h))],
        out_shape=[jax.ShapeDtypeStruct((T, N_Q_HEADS * HEAD_DIM), F32),
                   jax.ShapeDtypeStruct((T, N_KV_HEADS * HEAD_DIM), F32),
                   jax.ShapeDtypeStruct((T, N_KV_HEADS * HEAD_DIM), F32)],
        scratch_shapes=[pltpu.VMEM((GROUP * tq, HEAD_DIM), BF16), pltpu.VMEM((GROUP * tq, HEAD_DIM), BF16),
                        pltpu.VMEM((HEAD_DIM, GROUP * tq), F32)],
        compiler_params=pltpu.CompilerParams(dimension_semantics=("arbitrary", "arbitrary", "arbitrary"),
                                             vmem_limit_bytes=ATTN_VMEM_LIMIT),
    )(q, do, k, v, lse, delta, *extra)


def _gate_specs(D):
    w = D // 2
    first = (3 * D + (N_Q_HEADS + 2 * N_KV_HEADS) * HEAD_DIM) // w
    return [pl.BlockSpec((ROW, w), lambda i, c=first + j: (i, c)) for j in range(4)]


def _merge_fwd(o, P, conv_w, wbc, wba, wo, D, name):
    T = o.shape[0]
    nt = T // ROW
    w = D // 2
    conv_specs, conv_args = _conv_fwd_operands(P, conv_w, D)
    nc = len(conv_args)

    def body(*refs):
        o_ref, g0, g1, g2, g3, wbc_ref, wba_ref, wo_ref, yc_ref, a1_ref, a2_ref, z_ref, mo_ref = refs[nc:]
        yc_ref[...] = _conv_tile_fwd(refs[:nc], nt)
        a1 = jnp.dot(yc_ref[...], wbc_ref[...], preferred_element_type=F32)
        a2 = jnp.dot(o_ref[...].astype(BF16), wba_ref[...], preferred_element_type=F32)
        a1_ref[...] = a1.astype(BF16)
        a2_ref[...] = a2.astype(BF16)
        for j, (gc, ga) in enumerate(((g0, g2), (g1, g3))):
            sl = slice(j * w, (j + 1) * w)
            z = jax.nn.sigmoid(_f32(gc)) * a1[:, sl] + jax.nn.sigmoid(_f32(ga)) * a2[:, sl]
            z_ref[:, sl] = z.astype(BF16)
        mo_ref[...] = jnp.dot(z_ref[...], wo_ref[...], preferred_element_type=F32)

    return pl.pallas_call(
        body, name=name, grid=(T // ROW,),
        in_specs=conv_specs + [_row_spec(D)] + _gate_specs(D) + [_resident()] * 3,
        out_specs=[_row_spec(D)] * 5,
        out_shape=[jax.ShapeDtypeStruct((T, D), BF16), jax.ShapeDtypeStruct((T, D), BF16),
                   jax.ShapeDtypeStruct((T, D), BF16), jax.ShapeDtypeStruct((T, D), BF16),
                   jax.ShapeDtypeStruct((T, D), F32)],
        compiler_params=_params(("parallel",)),
    )(*conv_args, o, P, P, P, P, wbc, wba, wo)


def _merge_bwd(dmo, a1, a2, o, P, wbc, wba, wo, D, name):
    T = a1.shape[0]
    w = D // 2

    def body(dmo_ref, a1_ref, a2_ref, o_ref, g0, g1, g2, g3, wbc_ref, wba_ref, wo_ref,
             d1_ref, d2_ref, dg_ref, dyc_ref, dob_ref, dl_ref):
        dz = lax.dot_general(dmo_ref[...], wo_ref[...], _NT, preferred_element_type=F32)
        for j, (gc, ga) in enumerate(((g0, g2), (g1, g3))):
            sl = slice(j * w, (j + 1) * w)
            dzs = dz[:, sl]
            sc = jax.nn.sigmoid(_f32(gc))
            sa = jax.nn.sigmoid(_f32(ga))
            d1_ref[:, sl] = (dzs * sc).astype(BF16)
            d2_ref[:, sl] = (dzs * sa).astype(BF16)
            dg_ref[:, j * w:(j + 1) * w] = (dzs * a1_ref[:, sl].astype(F32) * (sc * (1.0 - sc))).astype(BF16)
            dg_ref[:, D + j * w:D + (j + 1) * w] = (dzs * a2_ref[:, sl].astype(F32) * (sa * (1.0 - sa))).astype(BF16)
        dyc_ref[...] = lax.dot_general(d1_ref[...], wbc_ref[...], _NT, preferred_element_type=F32)
        dov = lax.dot_general(d2_ref[...], wba_ref[...], _NT, preferred_element_type=F32)
        dob_ref[...] = dov.astype(BF16)
        prod = dov * o_ref[...]
        for h in range(N_Q_HEADS):
            d = jnp.sum(prod[:, h * HEAD_DIM:(h + 1) * HEAD_DIM], axis=1, keepdims=True)
            dl_ref[h // GROUP, (h % GROUP):(h % GROUP) + 1, :] = _to_row(d, ROW)

    return pl.pallas_call(
        body, name=name, grid=(T // ROW,),
        in_specs=[_row_spec(D)] * 4 + _gate_specs(D) + [_resident()] * 3,
        out_specs=[_row_spec(D), _row_spec(D), _row_spec(2 * D), _row_spec(D), _row_spec(D),
                   pl.BlockSpec((N_KV_HEADS, GROUP, ROW), lambda i: (0, 0, i))],
        out_shape=[jax.ShapeDtypeStruct((T, D), BF16), jax.ShapeDtypeStruct((T, D), BF16),
                   jax.ShapeDtypeStruct((T, 2 * D), BF16), jax.ShapeDtypeStruct((T, D), F32),
                   jax.ShapeDtypeStruct((T, D), BF16), jax.ShapeDtypeStruct((N_KV_HEADS, GROUP, T), F32)],
        compiler_params=_params(("parallel",)),
    )(dmo, a1, a2, o, P, P, P, P, wbc, wba, wo)


def _adamw_math(w, g, m, v):
    m = ADAM_B1 * m + (1.0 - ADAM_B1) * g
    v = ADAM_B2 * v + (1.0 - ADAM_B2) * (g * g)
    m_hat = m / (1.0 - ADAM_B1 ** ADAM_STEP)
    v_hat = v / (1.0 - ADAM_B2 ** ADAM_STEP)
    delta = -ADAM_LR * (m_hat / (jnp.sqrt(v_hat) + ADAM_EPS) + ADAM_WD * w)
    return delta, m, v


def _adamw(w, g, m, v, name):
    R, C = w.shape
    tr = _pick(R, tuple(t for t in (256, 128, 64, 32, 16, 8) if t * C * 4 <= ADAMW_BLOCK_BYTES))

    def body(w_ref, g_ref, m_ref, v_ref, d_ref, mo_ref, vo_ref):
        d, mn, vn = _adamw_math(w_ref[...], g_ref[...], m_ref[...], v_ref[...])
        d_ref[...] = d
        mo_ref[...] = mn
        vo_ref[...] = vn

    spec = pl.BlockSpec((tr, C), lambda i: (i, 0))
    return pl.pallas_call(
        body, name=name, grid=(R // tr,),
        in_specs=[spec] * 4, out_specs=[spec] * 3,
        out_shape=[jax.ShapeDtypeStruct((R, C), F32)] * 3,
        compiler_params=_params(("parallel",)),
    )(w, g, m, v)


def _norm_mix_in_fwd(xprev, branch, mods, g, gate, shift_idx, scale_idx, w_t, gq, gk, cos_t, sin_t, name):
    x_specs, x_args, (T, D) = _rows_operand(xprev)
    N = w_t.shape[0]
    QW = N_Q_HEADS * HEAD_DIM
    KW = N_KV_HEADS * HEAD_DIM
    q0, k0, v0 = 3 * D, 3 * D + QW, 3 * D + QW + KW
    edges = [0, D, 2 * D, q0, k0, v0 + KW] + list(range(v0 + KW + D, N + 1, D))
    assert edges[-1] == N

    def body(*refs):
        f_ref, m_ref, g_ref, w_ref, gq_ref, gk_ref, c_ref, s_ref = refs[len(x_args):len(x_args) + 8]
        xo_ref, h_ref, p_ref, qo_ref, ko_ref, vo_ref = refs[len(x_args) + 8:]
        m = m_ref[0]
        gate_idx, fac = gate
        x = _rows_tile(refs[:len(x_args)]) + (fac * m[gate_idx:gate_idx + 1, :]) * f_ref[...]
        xo_ref[...] = x
        hv = _norm_tile_fwd(x, m, g_ref[...], shift_idx, scale_idx)
        h_ref[...] = hv
        c = c_ref[...]
        s = s_ref[...]

        def head(xh, gain):
            inv = lax.rsqrt(jnp.mean(xh * xh, axis=-1, keepdims=True) + EPS)
            y = (xh * inv) * gain
            return y * c + _swap_halves(y) * s

        for lo, hi in zip(edges[:-1], edges[1:]):
            pb = lax.dot_general(hv, w_ref[lo:hi, :], _NT, preferred_element_type=F32).astype(BF16)
            p_ref[:, lo:hi] = pb
            if lo == q0:
                for h in range(N_Q_HEADS):
                    sl = slice(h * HEAD_DIM, (h + 1) * HEAD_DIM)
                    qo_ref[:, sl] = head(pb[:, sl].astype(F32), gq_ref[...]).astype(BF16)
            elif lo == k0:
                for h in range(N_KV_HEADS):
                    sl = slice(h * HEAD_DIM, (h + 1) * HEAD_DIM)
                    ko_ref[:, sl] = head(pb[:, sl].astype(F32), gk_ref[...]).astype(BF16)
                vo_ref[...] = pb[:, KW:2 * KW]

    return pl.pallas_call(
        body, name=name, grid=(T // ROW,),
        in_specs=x_specs + [_row_spec(D), _mods_spec(D), _vec_spec(1, D), _resident(),
                            _vec_spec(1, HEAD_DIM), _vec_spec(1, HEAD_DIM), _row_spec(HEAD_DIM), _row_spec(HEAD_DIM)],
        out_specs=[_row_spec(D), _row_spec(D), _row_spec(N), _row_spec(QW), _row_spec(KW), _row_spec(KW)],
        out_shape=[jax.ShapeDtypeStruct((T, D), F32), jax.ShapeDtypeStruct((T, D), BF16),
                   jax.ShapeDtypeStruct((T, N), BF16), jax.ShapeDtypeStruct((T, QW), BF16),
                   jax.ShapeDtypeStruct((T, KW), BF16), jax.ShapeDtypeStruct((T, KW), BF16)],
        compiler_params=_params(("parallel",)),
    )(*x_args, branch, mods, g, w_t, gq, gk, cos_t, sin_t)


def _mix_in_norm_bwd(dyc, dgt, P, conv_w, dq, dk, dv, gq, gk, cos_t, sin_t, w_t, x, dres, mods, g, shift_idx,
                     scale_idx, gate, branch, name):
    T, D = x.shape
    nt = T // ROW
    QW = N_Q_HEADS * HEAD_DIM
    KW = N_KV_HEADS * HEAD_DIM
    q0, g0 = 3 * D, 3 * D + QW + 2 * KW
    assert g0 + dgt.shape[1] == w_t.shape[0]
    conv_specs, conv_args = _conv_bwd_operands(P, dyc, conv_w, D)
    nc = len(conv_args)

    def body(*refs):
        (dg_ref, q_ref, k_ref, dq_ref, dk_ref, dv_ref, gq_ref, gk_ref, c_ref, s_ref,
         w_ref, x_ref, dr_ref, b_ref, m_ref, g_ref,
         dx_ref, db_ref, acc_ref, dc_ref, cacc_ref, o_ref, qacc_ref) = refs[nc:]
        _acc_init(acc_ref)
        _acc_init(cacc_ref)
        _acc_init(qacc_ref)
        _conv_tile_bwd(refs[:nc], dc_ref, cacc_ref, D, nt)
        dh = jnp.dot(dc_ref[...], w_ref[0:q0, :], preferred_element_type=F32)
        c = c_ref[...]
        s = s_ref[...]

        def head(xh, d, gain):
            dyv = d * c + _swap_halves(d * s)
            inv = lax.rsqrt(jnp.mean(xh * xh, axis=-1, keepdims=True) + EPS)
            xn = xh * inv
            dxn = dyv * gain
            dxh = inv * (dxn - xn * jnp.mean(dxn * xn, axis=-1, keepdims=True))
            return dxh, jnp.sum(dyv * xn, axis=0, keepdims=True)

        dgq = jnp.zeros((1, HEAD_DIM), F32)
        for h in range(N_Q_HEADS):
            sl = slice(h * HEAD_DIM, (h + 1) * HEAD_DIM)
            dxh, dgh = head(q_ref[:, sl].astype(F32), dq_ref[:, sl], gq_ref[...])
            o_ref[:, sl] = dxh.astype(BF16)
            dgq = dgq + dgh
        dh = dh + jnp.dot(dg_ref[...], w_ref[g0:, :], preferred_element_type=F32)
        dgk = jnp.zeros((1, HEAD_DIM), F32)
        for h in range(N_KV_HEADS):
            sl = slice(h * HEAD_DIM, (h + 1) * HEAD_DIM)
            dxh, dgh = head(k_ref[:, sl].astype(F32), dk_ref[:, sl], gk_ref[...])
            o_ref[:, QW + h * HEAD_DIM:QW + (h + 1) * HEAD_DIM] = dxh.astype(BF16)
            dgk = dgk + dgh
        o_ref[:, QW + KW:QW + 2 * KW] = dv_ref[...].astype(BF16)
        qacc_ref[0, 0:1, 0:HEAD_DIM] += dgq
        qacc_ref[0, 1:2, 0:HEAD_DIM] += dgk
        dh = dh + jnp.dot(o_ref[...], w_ref[q0:g0, :], preferred_element_type=F32)
        m = m_ref[0]
        dx = _norm_tile_bwd(x_ref[...], dh, dr_ref[...], m, g_ref[...], shift_idx, scale_idx, acc_ref)
        dx_ref[...] = dx
        db_ref[...] = _gate_tile_bwd(dx, b_ref[...], m, gate, acc_ref)

    return pl.pallas_call(
        body, name=name, grid=(T // ROW,),
        in_specs=conv_specs +
                 [_row_spec(dgt.shape[1]), _row_spec(QW, q0 // QW), _row_spec(KW, (q0 + QW) // KW),
                  _row_spec(QW), _row_spec(KW), _row_spec(KW), _vec_spec(1, HEAD_DIM), _vec_spec(1, HEAD_DIM),
                  _row_spec(HEAD_DIM), _row_spec(HEAD_DIM),
                  _resident(), _row_spec(D), _row_spec(D), _row_spec(D), _mods_spec(D), _vec_spec(1, D)],
        out_specs=[_row_spec(D), _row_spec(D), _acc_spec(D), _row_spec(q0), _acc_spec(D),
                   _row_spec(QW + 2 * KW), _acc_spec(D)],
        out_shape=[jax.ShapeDtypeStruct((T, D), F32), jax.ShapeDtypeStruct((T, D), BF16),
                   jax.ShapeDtypeStruct((2, ACC_ROWS, D), F32), jax.ShapeDtypeStruct((T, q0), BF16),
                   jax.ShapeDtypeStruct((2, ACC_ROWS, D), F32), jax.ShapeDtypeStruct((T, QW + 2 * KW), BF16),
                   jax.ShapeDtypeStruct((2, ACC_ROWS, D), F32)],
        compiler_params=_params(("arbitrary",)),
    )(*conv_args, dgt, P, P, dq, dk, dv, gq, gk, cos_t, sin_t, w_t, x, dres, branch, mods, g)


def _adamw_transposed(w, gt, m, v, name):
    R, C = w.shape
    tc = 128

    def body(w_ref, g_ref, m_ref, v_ref, go_ref, d_ref, mo_ref, vo_ref):
        g = jnp.transpose(g_ref[...])
        d, mn, vn = _adamw_math(w_ref[...], g, m_ref[...], v_ref[...])
        go_ref[...] = g
        d_ref[...] = d
        mo_ref[...] = mn
        vo_ref[...] = vn

    spec = pl.BlockSpec((R, tc), lambda j: (0, j))
    return pl.pallas_call(
        body, name=name, grid=(C // tc,),
        in_specs=[spec, pl.BlockSpec((tc, R), lambda j: (j, 0)), spec, spec], out_specs=[spec] * 4,
        out_shape=[jax.ShapeDtypeStruct((R, C), F32)] * 4,
        compiler_params=_params(("parallel",)),
    )(w, gt, m, v)


class _NoExchange:
    def __init__(self, rest):
        self.rest = rest

    def rest_weights(self, after):
        return self.rest

    def reduce_early(self, grads, tag):
        return None


def _local_step(xcat, target, mods, norm_g, final_g, gq, gk, conv_w, ffn1_w, hooks, ctx_len):
    T, D = _rows_operand(xcat)[2]
    w1i, w1o = ffn1_w
    g1, g2, g3 = norm_g
    cos_t, sin_t = _rope_tables(ctx_len, T - ctx_len)

    def after(value, token, name):
        return value if token is None else _after(value, token, name)

    _, h1, u1, s1, f1 = _norm_ffn_fwd(xcat, None, mods, g1, None, 0, 1, w1i, w1o, "f_ffn1")
    wi, wbc, wba, wo, w2i, w2o = hooks.rest_weights(f1)
    x1, h2, P, qn, kn, vb = _norm_mix_in_fwd(xcat, f1, mods, g2, (2, 0.5), 3, 4, wi, gq, gk, cos_t, sin_t, "f_mix_in")
    o, lse = _flash_fwd(qn, kn, vb, "f_attn")
    yc, a1, a2, z, mo = _merge_fwd(o, P, conv_w, wbc, wba, wo, D, "f_merge")
    x2, h3, u2, s2, dx3, df2, acc_head = _norm_ffn_fwd(x1, mo, mods, g3, (5, 1.0), 6, 7, w2i, w2o, "f_ffn2",
                                                       head=(final_g, target))

    du2, dx2, dmo, acc_n3 = _ffn_norm_bwd(df2, u2, w2i, w2o, x2, dx3, mods, g3, 6, 7, (5, 1.0), mo, "b_ffn2")
    g_w2o = _grad_matmul(s2, df2, "b_ffn2_out_dw")
    g_w2i = _grad_matmul(du2, h3, "b_ffn2_in_dw")

    g_wo = _grad_matmul(z, dmo, "b_mix_out_dw")
    da1, da2, dgt, dyc, dob, delta = _merge_bwd(dmo, a1, a2, o, P, wbc, wba, wo, D, "b_merge")
    g_wbc = _grad_matmul(yc, da1, "b_branch_conv_dw")
    g_wba = _grad_matmul(o, da2, "b_branch_attn_dw")
    token_a = hooks.reduce_early([g_wbc, g_wba, g_wo, g_w2i, g_w2o], "a")
    dq, dk, dv = _flash_bwd(qn, kn, vb, dob, lse, delta, "b_attn", token=token_a)
    dx1, df1, acc_n2, dconv, acc_conv, dqkv, acc_qk = _mix_in_norm_bwd(
        dyc, dgt, P, conv_w, dq, dk, dv, gq, gk, cos_t, sin_t, wi, x1, dx2, mods, g2, 3, 4, (2, 0.5), f1, "b_mix_in")
    d_parts = (dconv, dqkv, dgt)
    g_wi = jnp.concatenate([_grad_matmul(dp, h2, f"b_mix_in_dw_{i}") for i, dp in enumerate(d_parts)], axis=0)
    g1_b = after(g1, hooks.reduce_early([g_wi], "b"), "after_rs_b")

    du1, grad_x, _, acc_n1 = _ffn_norm_bwd(df1, u1, w1i, w1o, xcat, dx1, mods, g1_b, 0, 1, None, None, "b_ffn1",
                                           skip_first_tile=True)
    g_w1o = _grad_matmul(s1, df1, "b_ffn1_out_dw")
    g_w1i = _grad_matmul(du1, h1, "b_ffn1_in_dw", token=hooks.reduce_early([g_w1o], "c"))

    grads = (g_w1i, g_w1o, g_wi, g_wbc, g_wba, g_wo, g_w2i, g_w2o)
    accs = (acc_head, acc_n3, acc_n2, acc_n1, acc_conv, acc_qk)
    return grad_x, grads, accs


def _place():
    return lax.axis_index("x"), lax.axis_index("y"), lax.axis_index("c")


def _other_chips(x, y):
    return [(1 - x, y), (x, 1 - y), (1 - x, 1 - y)]


def _allgather8(v, name):
    R, N = v.shape

    def body(v_ref, out_ref, send_sems, recv_sems, local_sem):
        x, y, c = _place()
        me, sibling = (x, y, c), (x, y, 1 - c)
        chips = _other_chips(x, y)

        def blk(px, py, pc):
            return out_ref.at[4 * px + 2 * py + pc]

        def copy(k, block, to, src=None):
            return pltpu.make_async_remote_copy(
                src_ref=blk(*block) if src is None else src, dst_ref=blk(*block),
                send_sem=send_sems.at[k], recv_sem=recv_sems.at[k], device_id=to, device_id_type=MESH)

        mine = pltpu.make_async_copy(v_ref, blk(*me), local_sem)
        mine.start()
        first = [copy(0, me, sibling, src=v_ref)]
        first += [copy(1 + j, me, (*chip, c), src=v_ref) for j, chip in enumerate(chips)]
        for cp in first:
            cp.start()
        passed = [copy(4 + j, (*chip, c), sibling) for j, chip in enumerate(chips)]
        for j, chip in enumerate(chips):
            copy(1 + j, (*chip, c), me).wait_recv()
            passed[j].start()
        copy(0, sibling, me).wait_recv()
        for j, chip in enumerate(chips):
            copy(4 + j, (*chip, 1 - c), me).wait_recv()
        for cp in first + passed:
            cp.wait_send()
        mine.wait()

    return pl.pallas_call(
        body, name=name,
        out_shape=jax.ShapeDtypeStruct((N_DEV, R, N), v.dtype),
        in_specs=[pl.BlockSpec(memory_space=pltpu.VMEM)],
        out_specs=pl.BlockSpec(memory_space=pltpu.VMEM),
        scratch_shapes=[pltpu.SemaphoreType.DMA((7,)), pltpu.SemaphoreType.DMA((7,)), pltpu.SemaphoreType.DMA],
        compiler_params=pltpu.CompilerParams(vmem_limit_bytes=VMEM_LIMIT),
    )(v)


def _any_specs(n):
    return [pl.BlockSpec(memory_space=pl.ANY)] * n


def _pair_exchange(grads, name):
    n = len(grads)

    def body(*refs):
        g, land = refs[:n], refs[n:2 * n]
        send_sems, recv_sems = refs[2 * n:]
        x, y, c = _place()
        sibling = (x, y, 1 - c)
        copies = []
        for t in range(n):
            half = grads[t].shape[0] // (2 * N_CHIPS)
            for s in range(N_CHIPS):
                cp = pltpu.make_async_remote_copy(
                    src_ref=g[t].at[pl.ds((2 * s + 1 - c) * half, half), :], dst_ref=land[t].at[s],
                    send_sem=send_sems.at[N_CHIPS * t + s], recv_sem=recv_sems.at[N_CHIPS * t + s],
                    device_id=sibling, device_id_type=MESH)
                cp.start()
                copies.append(cp)
        for cp in copies:
            cp.wait_recv()
        for cp in copies:
            cp.wait_send()

    return pl.pallas_call(
        body, name=name,
        out_shape=[jax.ShapeDtypeStruct((N_CHIPS, a.shape[0] // (2 * N_CHIPS), a.shape[1]), a.dtype) for a in grads],
        in_specs=_any_specs(n), out_specs=_any_specs(n),
        scratch_shapes=[pltpu.SemaphoreType.DMA((N_CHIPS * n,)), pltpu.SemaphoreType.DMA((N_CHIPS * n,))],
    )(*grads)


def _place_shard(w2, idx, transpose, name, token):
    if transpose:
        D, rs = w2.shape
        tr = 128
        in_spec = pl.BlockSpec((D, tr), lambda i, idx: (0, i))
    else:
        rs, D = w2.shape
        tr = _pick(rs, (352, 256, 128, 64, 32, 16))
        in_spec = pl.BlockSpec((tr, D), lambda i, idx: (i, 0))
    steps = rs // tr

    def body(idx_ref, w_ref, t_ref, o_ref):
        v = w_ref[...]
        o_ref[...] = (jnp.transpose(v) if transpose else v).astype(BF16)

    return pl.pallas_call(
        body, name=name,
        grid_spec=pltpu.PrefetchScalarGridSpec(
            num_scalar_prefetch=1, grid=(steps,),
            in_specs=[in_spec, pl.BlockSpec(token.shape, lambda i, idx: (0, 0))],
            out_specs=pl.BlockSpec((tr, D), lambda i, idx: (idx[1] * steps + i, 0))),
        out_shape=jax.ShapeDtypeStruct((N_CHIPS * rs, D), BF16),
        compiler_params=_params(("arbitrary",)),
    )(idx, w2, token)


def _pair_sum(g, landed, idx, name, token=None):
    _, half, D = landed.shape
    g4 = g.reshape(N_CHIPS, 2, half, D)
    tr = _pick(half, (416, 352, 128))
    extra = [] if token is None else [token]

    def body(idx_ref, g_ref, l_ref, *rest):
        rest[-1][...] = (g_ref[0].astype(F32) + l_ref[...].astype(F32)).astype(BF16)

    return pl.pallas_call(
        body, name=name,
        grid_spec=pltpu.PrefetchScalarGridSpec(
            num_scalar_prefetch=1, grid=(N_CHIPS, half // tr),
            in_specs=[pl.BlockSpec((1, 1, tr, D), lambda s, i, idx: (idx[1 + s], idx[0], i, 0)),
                      pl.BlockSpec((1, tr, D), lambda s, i, idx: (idx[1 + s], i, 0))] +
                     [pl.BlockSpec(t.shape, lambda s, i, idx: (0, 0)) for t in extra],
            out_specs=pl.BlockSpec((1, tr, D), lambda s, i, idx: (s, i, 0))),
        out_shape=jax.ShapeDtypeStruct((N_CHIPS, half, D), BF16),
        compiler_params=_params(("arbitrary", "arbitrary")),
    )(idx, g4, landed, *extra)


_HBM = pl.BlockSpec(memory_space=pltpu.HBM)
_SEM = pl.BlockSpec(memory_space=pltpu.SEMAPHORE)
_EFFECT = pltpu.SideEffectType.DATAFLOW_SIDE_EFFECTING


def _in_hbm(a):
    return pltpu.with_memory_space_constraint(a, pltpu.HBM)


def _split_copies(n, per, make):
    def start(nbuf, name, bufs):
        def body(*refs):
            ins = refs[:nbuf]
            send_sems, recv_sems = refs[nbuf], refs[nbuf + 1]
            token = refs[-1]
            for t in range(n):
                for j in range(per):
                    make(ins, t, j, send_sems.at[per * t + j], recv_sems.at[per * t + j]).start()
            token[...] = jnp.zeros(token.shape, token.dtype)

        out = pl.pallas_call(
            body, name=name,
            out_shape=(pltpu.SemaphoreType.DMA((per * n,)), pltpu.SemaphoreType.DMA((per * n,)),
                       *[pltpu.HBM(b.shape, b.dtype) for b in bufs], jax.ShapeDtypeStruct((8, 128), F32)),
            in_specs=[_HBM] * nbuf,
            out_specs=(_SEM, _SEM, *[_HBM] * nbuf, pl.BlockSpec(memory_space=pltpu.VMEM)),
            input_output_aliases={i: 2 + i for i in range(nbuf)},
            compiler_params=pltpu.CompilerParams(has_side_effects=_EFFECT),
        )(*[_in_hbm(b) for b in bufs])
        return out[0], out[1], list(out[2:2 + nbuf]), out[-1]

    def wait(nbuf, name, send_sems, recv_sems, bufs, after):
        def body(*refs):
            ins = refs[:nbuf]
            ss, rs = refs[nbuf], refs[nbuf + 1]
            for t in range(n):
                for j in range(per):
                    cp = make(ins, t, j, ss.at[per * t + j], rs.at[per * t + j])
                    cp.wait_send()
                    cp.wait_recv()

        return pl.pallas_call(
            body, name=name,
            out_shape=[pltpu.HBM(b.shape, b.dtype) for b in bufs],
            in_specs=[_HBM] * nbuf + [_SEM, _SEM, pl.BlockSpec(memory_space=pl.ANY)],
            out_specs=[_HBM] * nbuf,
            input_output_aliases={i: i for i in range(nbuf)},
            compiler_params=pltpu.CompilerParams(has_side_effects=_EFFECT),
        )(*bufs, send_sems, recv_sems, after)

    return start, wait


def _chip_exchange_split(n):
    def make(bufs, t, j, send_sem, recv_sem):
        x, y, c = _place()
        chip = _other_chips(x, y)[j]
        return pltpu.make_async_remote_copy(src_ref=bufs[t].at[1 + j], dst_ref=bufs[n + t].at[j], send_sem=send_sem,
                                            recv_sem=recv_sem, device_id=(*chip, c), device_id_type=MESH)

    return _split_copies(n, 3, make)


def _weights_gather_split(fulls):
    def make(bufs, t, j, send_sem, recv_sem):
        x, y, c = _place()
        chip = _other_chips(x, y)[j]
        rs = fulls[t].shape[0] // N_CHIPS
        rows = bufs[t].at[pl.ds((2 * x + y) * rs + c * (rs // 2), rs // 2), :]
        return pltpu.make_async_remote_copy(src_ref=rows, dst_ref=rows, send_sem=send_sem, recv_sem=recv_sem,
                                            device_id=(*chip, c), device_id_type=MESH)

    return _split_copies(len(fulls), 3, make)


def _weights_pass_on(fulls, name):
    n = len(fulls)

    def body(*refs):
        full = refs[n:2 * n]
        send_sems, recv_sems = refs[2 * n:]
        x, y, c = _place()
        chips = _other_chips(x, y)

        def copy(t, j, h):
            rs = fulls[t].shape[0] // N_CHIPS
            px, py = chips[j]
            rows = full[t].at[pl.ds((2 * px + py) * rs + h * (rs // 2), rs // 2), :]
            return pltpu.make_async_remote_copy(src_ref=rows, dst_ref=rows, send_sem=send_sems.at[3 * t + j],
                                                recv_sem=recv_sems.at[3 * t + j], device_id=(x, y, 1 - c),
                                                device_id_type=MESH)

        for t in range(n):
            for j in range(3):
                copy(t, j, c).start()
        for t in range(n):
            for j in range(3):
                copy(t, j, 1 - c).wait_recv()
        for t in range(n):
            for j in range(3):
                copy(t, j, c).wait_send()

    return pl.pallas_call(
        body, name=name,
        out_shape=[jax.ShapeDtypeStruct(f.shape, f.dtype) for f in fulls],
        in_specs=_any_specs(n), out_specs=_any_specs(n),
        input_output_aliases={t: t for t in range(n)},
        scratch_shapes=[pltpu.SemaphoreType.DMA((3 * n,)), pltpu.SemaphoreType.DMA((3 * n,))],
    )(*fulls)


def _after(value, token, name):
    def body(v_ref, t_ref, o_ref):
        o_ref[...] = v_ref[...]

    return pl.pallas_call(
        body, name=name, out_shape=jax.ShapeDtypeStruct(value.shape, value.dtype),
        in_specs=_whole(2), out_specs=pl.BlockSpec(memory_space=pltpu.VMEM),
    )(value, token)


def _chip_sum(ps, landed, idx, name):
    _, half, D = ps.shape
    tr = _pick(half, (416, 352, 128))
    steps = half // tr

    def body(idx_ref, p_ref, l_ref, o_ref):
        acc = p_ref[0].astype(F32)
        for j in range(3):
            acc = acc + l_ref[j].astype(F32)
        o_ref[...] = acc

    return pl.pallas_call(
        body, name=name,
        grid_spec=pltpu.PrefetchScalarGridSpec(
            num_scalar_prefetch=1, grid=(steps,),
            in_specs=[pl.BlockSpec((1, tr, D), lambda i, idx: (0, i, 0)),
                      pl.BlockSpec((3, tr, D), lambda i, idx: (0, i, 0))],
            out_specs=pl.BlockSpec((tr, D), lambda i, idx: (idx[0] * steps + i, 0))),
        out_shape=jax.ShapeDtypeStruct((2 * half, D), F32),
        compiler_params=_params(("arbitrary",)),
    )(idx, ps, landed)


def _pair_swap(shards, name):
    n = len(shards)

    def body(*refs):
        full = refs[n:2 * n]
        send_sems, recv_sems = refs[2 * n:]
        x, y, c = _place()

        def half(t, h):
            rows = shards[t].shape[0] // 2
            return full[t].at[pl.ds(h * rows, rows), :]

        def copy(t, h):
            return pltpu.make_async_remote_copy(src_ref=half(t, h), dst_ref=half(t, h), send_sem=send_sems.at[t],
                                                recv_sem=recv_sems.at[t], device_id=(x, y, 1 - c),
                                                device_id_type=MESH)

        for t in range(n):
            copy(t, c).start()
        for t in range(n):
            copy(t, 1 - c).wait_recv()
        for t in range(n):
            copy(t, c).wait_send()

    return pl.pallas_call(
        body, name=name,
        out_shape=[jax.ShapeDtypeStruct(a.shape, a.dtype) for a in shards],
        in_specs=_any_specs(n), out_specs=_any_specs(n),
        input_output_aliases={t: t for t in range(n)},
        scratch_shapes=[pltpu.SemaphoreType.DMA((n,)), pltpu.SemaphoreType.DMA((n,))],
    )(*shards)


def _gather_begin(fulls, tag):
    start, wait = _weights_gather_split(fulls)
    send_sems, recv_sems, bufs, token = start(len(fulls), f"ag_{tag}_start", fulls)
    return (wait, send_sems, recv_sems, bufs), token


def _gather_end(state, after, tag):
    wait, send_sems, recv_sems, bufs = state
    landed = wait(len(bufs), f"ag_{tag}_wait", send_sems, recv_sems, bufs, after)
    return _weights_pass_on(landed, f"ag_{tag}_pass_on")


class _Exchanges:
    def __init__(self, fulls_rest, idx):
        self.idx = idx
        self._rest, self.token = _gather_begin(fulls_rest, "rest")
        self._early = []

    def rest_weights(self, after):
        return _gather_end(self._rest, after, "rest")

    def reduce_early(self, grads, tag, token=None):
        landed = _pair_exchange(grads, "rs_pair_exchange_" + tag)
        sums = [_pair_sum(g, l, self.idx, f"rs_pair_sum_{tag}{t}", token)
                for t, (g, l) in enumerate(zip(grads, landed))]
        zones = [lax.empty((3,) + s.shape[1:], s.dtype) for s in sums]
        start, wait = _chip_exchange_split(len(sums))
        send_sems, recv_sems, bufs, token = start(2 * len(sums), "rs_chip_start_" + tag, sums + zones)
        self._early.append((tag, wait, send_sems, recv_sems, bufs))
        return token

    def finish(self, tags, after):
        halves = []
        for tag, wait, send_sems, recv_sems, bufs in self._early:
            if tag in tags:
                n = len(bufs) // 2
                done = wait(len(bufs), "rs_chip_wait_" + tag, send_sems, recv_sems, bufs, after)
                halves += [_chip_sum(p, l, self.idx, f"rs_chip_sum_{tag}{t}")
                           for t, (p, l) in enumerate(zip(done[:n], done[n:]))]
        return halves


N_MOD = 9
PACK_HEAD, PACK_N3, PACK_N2, PACK_N1, PACK_CONV, PACK_QK = 0, 16, 32, 48, 64, 80
PACK_ROWS = 96
MOD_SRC = ((PACK_N1, 0), (PACK_N1, 1), (PACK_N2, 3), (PACK_N2, 0), (PACK_N2, 1),
           (PACK_N3, 3), (PACK_N3, 0), (PACK_N3, 1), (PACK_HEAD, 2))
CTX_ROW = 8


def _silu(v):
    return v * jax.nn.sigmoid(v)


def _whole(n):
    return [pl.BlockSpec(memory_space=pltpu.VMEM)] * n


def _mod_rows(cin, w_sh, b_sh, name):
    def body(c_ref, w_ref, b_ref, o_ref):
        a = _silu(c_ref[...]).astype(BF16)
        o_ref[...] = jnp.dot(a, w_ref[...].astype(BF16), preferred_element_type=F32) + b_ref[...]

    return pl.pallas_call(
        body, name=name, out_shape=jax.ShapeDtypeStruct((cin.shape[0], w_sh.shape[1]), F32),
        in_specs=_whole(3), out_specs=pl.BlockSpec(memory_space=pltpu.VMEM),
        compiler_params=pltpu.CompilerParams(vmem_limit_bytes=VMEM_LIMIT),
    )(cin, w_sh, b_sh)


def _small_reduce(gathered, name):
    _, _, D = gathered.shape

    def body(g_ref, loss_ref, db_ref, gn_ref, cv_ref, qk_ref, dm_ref):
        tot = g_ref[0]
        for r in range(1, N_DEV):
            tot = tot + g_ref[r]

        def both(block, row):
            return tot[block + row:block + row + 1, :] + tot[block + 8 + row:block + 8 + row + 1, :]

        loss = jnp.sum(both(PACK_HEAD, 0), axis=1, keepdims=True)
        loss_ref[...] = jnp.broadcast_to(loss, loss_ref.shape)
        db_ref[...] = jnp.zeros(db_ref.shape, F32)
        dm_ref[...] = jnp.zeros(dm_ref.shape, F32)
        for j, (block, row) in enumerate(MOD_SRC):
            db_ref[j:j + 1, :] = both(block, row)
            dm_ref[CTX_ROW, j:j + 1, :] = tot[block + row:block + row + 1, :]
            for r in range(N_DEV):
                dm_ref[r, j:j + 1, :] = g_ref[r, block + 8 + row:block + 8 + row + 1, :]
        gn_ref[...] = jnp.zeros(gn_ref.shape, F32)
        gn_ref[0:1, :] = both(PACK_N1, 2)
        gn_ref[8:9, :] = both(PACK_N2, 2)
        gn_ref[16:17, :] = both(PACK_N3, 2)
        gn_ref[24:25, :] = both(PACK_HEAD, 1)
        cv_ref[...] = jnp.zeros(cv_ref.shape, F32)
        for r in range(3):
            cv_ref[r:r + 1, :] = both(PACK_CONV, r)
        qk_ref[...] = jnp.zeros(qk_ref.shape, F32)
        qk_ref[0:1, 0:HEAD_DIM] = both(PACK_QK, 0)[:, 0:HEAD_DIM]
        qk_ref[0:1, HEAD_DIM:2 * HEAD_DIM] = both(PACK_QK, 1)[:, 0:HEAD_DIM]

    return pl.pallas_call(
        body, name=name,
        out_shape=[jax.ShapeDtypeStruct((8, 128), F32), jax.ShapeDtypeStruct((16, D), F32),
                   jax.ShapeDtypeStruct((32, D), F32), jax.ShapeDtypeStruct((8, D), F32),
                   jax.ShapeDtypeStruct((8, D), F32), jax.ShapeDtypeStruct((16, 16, D), F32)],
        in_specs=_whole(1), out_specs=_whole(6),
        compiler_params=pltpu.CompilerParams(vmem_limit_bytes=VMEM_LIMIT),
    )(gathered)


def _wmod_grad(cin, dm_sh, w_sh, name):
    def body(c_ref, d_ref, w_ref, gw_ref, cp_ref):
        a = _silu(c_ref[...]).astype(BF16)
        d = d_ref[...].astype(BF16)
        gw_ref[...] = lax.dot_general(a, d, (((0,), (0,)), ((), ())), preferred_element_type=F32)
        cp_ref[...] = lax.dot_general(d, w_ref[...].astype(BF16), (((1,), (1,)), ((), ())),
                                      preferred_element_type=F32)

    return pl.pallas_call(
        body, name=name,
        out_shape=[jax.ShapeDtypeStruct(w_sh.shape, F32), jax.ShapeDtypeStruct(cin.shape, F32)],
        in_specs=_whole(3), out_specs=_whole(2),
        compiler_params=pltpu.CompilerParams(vmem_limit_bytes=VMEM_LIMIT),
    )(cin, dm_sh, w_sh)


def _cctx_grad(parts, c_ctx8, name):
    def body(p_ref, c_ref, o_ref):
        tot = p_ref[0] + p_ref[2] + p_ref[4] + p_ref[6]
        cv = c_ref[...]
        sig = jax.nn.sigmoid(cv)
        rows = lax.broadcasted_iota(jnp.int32, tot.shape, 0)
        o_ref[...] = jnp.where(rows == 0, tot * (sig * (1.0 + cv * (1.0 - sig))), 0.0)

    return pl.pallas_call(
        body, name=name, out_shape=jax.ShapeDtypeStruct(c_ctx8.shape, F32),
        in_specs=_whole(2), out_specs=pl.BlockSpec(memory_space=pltpu.VMEM),
    )(parts, c_ctx8)


def _pad_rows(a, rows):
    return jnp.pad(a, ((0, rows - a.shape[0]), (0, 0)))


def _pack_small(c_ctx, b_mod, n1, n2, n3, final_g, gq, gk, conv_sh, D):
    misc = jnp.concatenate([gq, gk, conv_sh.reshape(1, -1)], axis=1)
    return jnp.concatenate([_pad_rows(c_ctx[None], 8), _pad_rows(b_mod.reshape(N_MOD, D), 16), _pad_rows(n1, 8),
                            _pad_rows(n2, 8), _pad_rows(n3, 8), _pad_rows(final_g[None], 8), _pad_rows(misc, 8)], axis=0)


def _unpack_small(p, D, conv_shape):
    misc = p[56:57]
    return dict(c_ctx=p[0], b_mod=p[8:8 + N_MOD].reshape(1, N_MOD * D), norm1_g=p[24:25], norm2_g=p[32:33],
                norm3_g=p[40:41], final_g=p[48], q_norm_g=misc[:, 0:HEAD_DIM], k_norm_g=misc[:, HEAD_DIM:2 * HEAD_DIM],
                conv_w=misc[:, 2 * HEAD_DIM:].reshape(conv_shape))


WEIGHT_ORDER = ("c_ctx", "w_mod", "b_mod", "norm1_g", "norm2_g", "norm3_g", "ffn1_w_in", "ffn1_w_out", "w_in",
                "conv_w", "q_norm_g", "k_norm_g", "w_branch_conv", "w_branch_attn", "w_out", "ffn2_w_in",
                "ffn2_w_out", "final_g")
BIG = ("ffn1_w_in", "ffn1_w_out", "w_in", "w_branch_conv", "w_branch_attn", "w_out", "ffn2_w_in", "ffn2_w_out")
COLUMN_SHARDED = ("ffn1_w_in", "w_in", "ffn2_w_in")


def kernel(x, c, ctx, c_ctx, w_mod, b_mod, norm1_g, norm2_g, norm3_g, ffn1_w_in, ffn1_w_out, w_in, conv_w, q_norm_g, k_norm_g, w_branch_conv, w_branch_attn, w_out, ffn2_w_in, ffn2_w_out, final_g, loss_target, m_c_ctx, m_w_mod, m_b_mod, m_norm1_g, m_norm2_g, m_norm3_g, m_ffn1_w_in, m_ffn1_w_out, m_w_in, m_conv_w, m_q_norm_g, m_k_norm_g, m_w_branch_conv, m_w_branch_attn, m_w_out, m_ffn2_w_in, m_ffn2_w_out, m_final_g, v_c_ctx, v_w_mod, v_b_mod, v_norm1_g, v_norm2_g, v_norm3_g, v_ffn1_w_in, v_ffn1_w_out, v_w_in, v_conv_w, v_q_norm_g, v_k_norm_g, v_w_branch_conv, v_w_branch_attn, v_w_out, v_ffn2_w_in, v_ffn2_w_out, v_final_g):
    w = dict(c_ctx=c_ctx, w_mod=w_mod, b_mod=b_mod, norm1_g=norm1_g, norm2_g=norm2_g, norm3_g=norm3_g,
             ffn1_w_in=ffn1_w_in, ffn1_w_out=ffn1_w_out, w_in=w_in, conv_w=conv_w, q_norm_g=q_norm_g,
             k_norm_g=k_norm_g, w_branch_conv=w_branch_conv, w_branch_attn=w_branch_attn, w_out=w_out,
             ffn2_w_in=ffn2_w_in, ffn2_w_out=ffn2_w_out, final_g=final_g)
    m = dict(c_ctx=m_c_ctx, w_mod=m_w_mod, b_mod=m_b_mod, norm1_g=m_norm1_g, norm2_g=m_norm2_g, norm3_g=m_norm3_g,
             ffn1_w_in=m_ffn1_w_in, ffn1_w_out=m_ffn1_w_out, w_in=m_w_in, conv_w=m_conv_w, q_norm_g=m_q_norm_g,
             k_norm_g=m_k_norm_g, w_branch_conv=m_w_branch_conv, w_branch_attn=m_w_branch_attn, w_out=m_w_out,
             ffn2_w_in=m_ffn2_w_in, ffn2_w_out=m_ffn2_w_out, final_g=m_final_g)
    v = dict(c_ctx=v_c_ctx, w_mod=v_w_mod, b_mod=v_b_mod, norm1_g=v_norm1_g, norm2_g=v_norm2_g, norm3_g=v_norm3_g,
             ffn1_w_in=v_ffn1_w_in, ffn1_w_out=v_ffn1_w_out, w_in=v_w_in, conv_w=v_conv_w, q_norm_g=v_q_norm_g,
             k_norm_g=v_k_norm_g, w_branch_conv=v_w_branch_conv, w_branch_attn=v_w_branch_attn, w_out=v_w_out,
             ffn2_w_in=v_ffn2_w_in, ffn2_w_out=v_ffn2_w_out, final_g=v_final_g)

    xi, yi, ci = _place()
    dev = 4 * xi + 2 * yi + ci
    shard = 2 * xi + yi
    idx = jnp.stack([ci, shard, 2 * (1 - xi) + yi, 2 * xi + (1 - yi), 2 * (1 - xi) + (1 - yi)]).astype(jnp.int32)
    D = x.shape[-1]
    ctx_len = ctx.shape[1]
    assert ctx_len == ROW and c.shape == (1, D)
    mcols = w_mod.shape[2]
    ccols = conv_w.shape[2]

    def place(names, token):
        fulls = []
        for n in names:
            fulls.append(_place_shard(w[n][0], idx, n in COLUMN_SHARDED, "place_" + n, token))
            token = fulls[-1][:16, :HEAD_DIM]
        return fulls

    ffn1_gather, ffn1_token = _gather_begin(place(BIG[:2], c), "ffn1")
    fulls_rest = place(BIG[2:], ffn1_token)

    c8 = _after(jnp.broadcast_to(c, (8, D)), fulls_rest[-1][:16, :HEAD_DIM], "after_place")
    c_all = _allgather8(c8, "ag_c")[:, 0, :]
    cin = jnp.concatenate([c_all, _pad_rows(c_ctx[None], 8)], axis=0)
    b_sh = lax.dynamic_slice(b_mod, (0, shard * mcols), (1, mcols))
    mod_sh = _mod_rows(cin, w_mod[0], b_sh, "mod_rows")
    conv_rows = jnp.pad(conv_w[0], ((0, 8 - conv_w.shape[1]), (0, mcols - ccols)))
    mod_all = _allgather8(jnp.concatenate([mod_sh, conv_rows], axis=0), "ag_mod")
    mod_full = jnp.concatenate([mod_all[2 * s, :16] for s in range(N_CHIPS)], axis=1)
    conv_full = jnp.concatenate([mod_all[2 * s, 16:16 + conv_w.shape[1], :ccols] for s in range(N_CHIPS)], axis=1)
    mod_lat = lax.dynamic_slice(mod_full, (dev, 0), (1, N_MOD * D)).reshape(N_MOD, D)
    mod_ctx = mod_full[CTX_ROW].reshape(N_MOD, D)
    mods = jnp.stack([_pad_rows(mod_ctx, 16), _pad_rows(mod_lat, 16)])

    ffn1_w = _gather_end(ffn1_gather, mods, "ffn1")
    hooks = _Exchanges(fulls_rest, idx)

    xcat = (ctx[0], x[0])
    norm1_first = _after(norm1_g, hooks.token, "after_ag_rest")
    grad_x, grads, accs = _local_step(xcat, loss_target[0], mods, (norm1_first, norm2_g, norm3_g), final_g[None],
                                      q_norm_g, k_norm_g, conv_full, ffn1_w, hooks, ctx_len)
    g = {}

    pack = jnp.concatenate([a.reshape(2 * ACC_ROWS, D) for a in accs], axis=0)
    gathered = _allgather8(pack, "ag_small")
    loss8, db_mod, g_norms, g_conv, g_qk, dm = _small_reduce(gathered, "small_reduce")
    dm_sh = lax.dynamic_slice(dm[:, :N_MOD, :].reshape(16, N_MOD * D), (0, shard * mcols), (16, mcols))
    g_wmod, cpart = _wmod_grad(cin, dm_sh, w_mod[0], "wmod_grad")
    g["w_mod"] = g_wmod[None]
    cparts = _allgather8(cpart[CTX_ROW:CTX_ROW + 8], "ag_cctx")
    g_cctx = _cctx_grad(cparts, _pad_rows(c_ctx[None], 8), "cctx_grad")
    g_conv_sh = lax.dynamic_slice(g_conv, (0, shard * ccols), (conv_w.shape[1], ccols))
    g_misc = jnp.concatenate([g_qk[0:1, 0:2 * HEAD_DIM], g_conv_sh.reshape(1, -1)], axis=1)
    g_pack = jnp.concatenate([g_cctx, db_mod, g_norms, _pad_rows(g_misc, 8)], axis=0)

    def packed(p):
        return _pack_small(p["c_ctx"], p["b_mod"], p["norm1_g"], p["norm2_g"], p["norm3_g"], p["final_g"],
                           p["q_norm_g"], p["k_norm_g"], p["conv_w"][0], D)

    d_pack, m_pack, v_pack = _adamw(packed(w), g_pack, packed(m), packed(v), "adamw_small")

    g.update(_unpack_small(g_pack, D, conv_w.shape))
    delta = _unpack_small(d_pack, D, conv_w.shape)
    new_m = _unpack_small(m_pack, D, conv_w.shape)
    new_v = _unpack_small(v_pack, D, conv_w.shape)

    def update(n, g2):
        if n in COLUMN_SHARDED:
            g2, d2, m2, v2 = _adamw_transposed(w[n][0], g2, m[n][0], v[n][0], "adamw_" + n)
        else:
            d2, m2, v2 = _adamw(w[n][0], g2, m[n][0], v[n][0], "adamw_" + n)
        g[n], delta[n], new_m[n], new_v[n] = g2[None], d2[None], m2[None], v2[None]
        return v2

    token_d = hooks.reduce_early([grads[0]], "d", token=d_pack[:8, :HEAD_DIM])
    h_wbc, h_wba, h_wo, h_w2i, h_w2o, h_wi, h_w1o = hooks.finish("abc", token_d)
    done = _pair_swap([h_w1o, h_wi, h_wbc, h_wba, h_wo, h_w2i, h_w2o], "rs_pair_swap")
    last = update("w_mod", g_wmod)
    for n, r in zip(BIG[1:], done):
        last = update(n, r)
    (h_w1i,) = hooks.finish("d", last)
    update(BIG[0], _pair_swap([h_w1i], "rs_pair_swap_d")[0])

    loss = loss8[0, 0]
    return (loss, grad_x[None], *[g[n] for n in WEIGHT_ORDER], *[delta[n] for n in WEIGHT_ORDER],
            *[new_m[n] for n in WEIGHT_ORDER], *[new_v[n] for n in WEIGHT_ORDER])
```

```python
import functools

import jax
import jax.numpy as jnp
from jax import lax
from jax.experimental import pallas as pl
from jax.experimental.pallas import tpu as pltpu

F32 = jnp.float32
BF16 = jnp.bfloat16

HEAD_DIM = 128
N_Q_HEADS = 8
N_KV_HEADS = 2
GROUP = N_Q_HEADS // N_KV_HEADS
GRID_W = 64
ROPE_THETA = 10000.0
EPS = 1e-6
ATTN_SCALE = HEAD_DIM ** -0.5

ADAM_LR = 0.001
ADAM_B1 = 0.9
ADAM_B2 = 0.999
ADAM_EPS = 1e-08
ADAM_WD = 0.01
ADAM_STEP = 10

LANES = 128
ROW = 256
HALO = 16
ACC_ROWS = 8
N_CHIPS = 4
N_DEV = 8
MESH = pl.DeviceIdType.MESH
VMEM_LIMIT = 48 * 1024 * 1024
ADAMW_BLOCK_BYTES = 1024 * 1024


def _pick(n, prefs):
    for p in prefs:
        if n % p == 0:
            return p
    return n


def _params(sem):
    return pltpu.CompilerParams(dimension_semantics=sem, vmem_limit_bytes=VMEM_LIMIT)


def _stream(i):
    return jnp.minimum(i, 1)


def _grad_matmul(a, b, name, token=None):
    (T, M), (T2, N) = a.shape, b.shape
    assert T == T2, (a.shape, b.shape)
    tm = _pick(M, (1664, 1408, 1024, 512, 256, 128))
    tk = _pick(T, (2816, 1408, 768, 512, 256))
    nk = T // tk
    extra = [] if token is None else [token]

    def body(a_ref, b_ref, *rest):
        o_ref, acc_ref = rest[len(extra):]
        p = lax.dot_general(a_ref[...].astype(BF16), b_ref[...].astype(BF16), (((0,), (0,)), ((), ())),
                            preferred_element_type=F32)
        k = pl.program_id(1)

        @pl.when(k == 0)
        def _():
            acc_ref[...] = p

        @pl.when(k > 0)
        def _():
            acc_ref[...] += p

        @pl.when(k == nk - 1)
        def _():
            o_ref[...] = acc_ref[...].astype(BF16)

    return pl.pallas_call(
        body, name=name, grid=(M // tm, nk),
        in_specs=[pl.BlockSpec((tk, tm), lambda i, k: (k, i)), pl.BlockSpec((tk, N), lambda i, k: (k, 0))] +
                 [pl.BlockSpec(t.shape, lambda i, k: (0, 0)) for t in extra],
        out_specs=pl.BlockSpec((tm, N), lambda i, k: (i, 0)),
        out_shape=jax.ShapeDtypeStruct((M, N), BF16),
        scratch_shapes=[pltpu.VMEM((tm, N), F32)],
        compiler_params=_params(("parallel", "arbitrary")),
    )(a, b, *extra)


def _row_spec(width, col=0):
    return pl.BlockSpec((ROW, width), lambda i, col=col: (i, col))


def _mods_spec(D):
    return pl.BlockSpec((1, 16, D), lambda i: (_stream(i), 0, 0))


def _acc_spec(D):
    return pl.BlockSpec((1, ACC_ROWS, D), lambda i: (_stream(i), 0, 0))


def _vec_spec(rows, D):
    return pl.BlockSpec((rows, D), lambda i: (0, 0))


def _acc_init(acc_ref):
    i = pl.program_id(0)

    @pl.when(i <= 1)
    def _():
        acc_ref[...] = jnp.zeros_like(acc_ref)


def _acc_add(acc_ref, row, val):
    acc_ref[0, row:row + 1, :] += jnp.sum(val, axis=0, keepdims=True)


def _rows_operand(x):
    if not isinstance(x, tuple):
        return [_row_spec(x.shape[1])], [x], x.shape
    ctx, lat = x
    D = lat.shape[1]
    assert ctx.shape == (ROW, D)
    specs = [pl.BlockSpec((ROW, D), lambda i: (0, 0)), pl.BlockSpec((ROW, D), lambda i: (jnp.maximum(i - 1, 0), 0))]
    return specs, [ctx, lat], (ROW + lat.shape[0], D)


def _rows_tile(refs):
    if len(refs) == 1:
        return refs[0][...]
    return jnp.where(pl.program_id(0) == 0, refs[0][...], refs[1][...])


def _norm_tile_fwd(x, m, g, shift_idx, scale_idx):
    inv = lax.rsqrt(jnp.mean(x * x, axis=-1, keepdims=True) + EPS)
    y = (x * inv) * g
    return (y * (1.0 + m[scale_idx:scale_idx + 1, :]) + m[shift_idx:shift_idx + 1, :]).astype(BF16)


def _norm_tile_bwd(x, dh, dres, m, g, shift_idx, scale_idx, acc_ref):
    inv = lax.rsqrt(jnp.mean(x * x, axis=-1, keepdims=True) + EPS)
    xn = x * inv
    dy = dh * (1.0 + m[scale_idx:scale_idx + 1, :])
    dxn = dy * g
    _acc_add(acc_ref, 0, dh)
    _acc_add(acc_ref, 1, dh * (xn * g))
    _acc_add(acc_ref, 2, dy * xn)
    return inv * (dxn - xn * jnp.mean(dxn * xn, axis=-1, keepdims=True)) + dres


def _gate_tile_bwd(dx, branch, m, gate, acc_ref):
    gate_idx, fac = gate
    _acc_add(acc_ref, 3, fac * dx * branch)
    return ((fac * m[gate_idx:gate_idx + 1, :]) * dx).astype(BF16)


_NT = (((1,), (1,)), ((), ()))


def _ffn_chunk(F):
    return _pick(F, (2816, 1408, 512, 256, 128))


def _resident():
    return pl.BlockSpec(memory_space=pltpu.VMEM)


def _ffn_tile_fwd(hv, wi_ref, wo_ref, u_ref, s_ref, F, cw):
    acc = jnp.zeros((hv.shape[0], wo_ref.shape[1]), F32)
    for j in range(F // cw):
        a = lax.dot_general(hv, wi_ref[j * cw:(j + 1) * cw, :], _NT, preferred_element_type=F32)
        b = lax.dot_general(hv, wi_ref[F + j * cw:F + (j + 1) * cw, :], _NT, preferred_element_type=F32)
        s = ((a * jax.nn.sigmoid(a)) * b).astype(BF16)
        u_ref[:, j * cw:(j + 1) * cw] = a.astype(BF16)
        u_ref[:, F + j * cw:F + (j + 1) * cw] = b.astype(BF16)
        s_ref[:, j * cw:(j + 1) * cw] = s
        acc = acc + jnp.dot(s, wo_ref[j * cw:(j + 1) * cw, :], preferred_element_type=F32)
    return acc


def _norm_ffn_fwd(xprev, branch, mods, g, gate, shift_idx, scale_idx, w_in_t, w_out, name, head=None):
    x_specs, x_args, (T, D) = _rows_operand(xprev)
    F = w_out.shape[0]
    cw = _ffn_chunk(F)
    has_res = branch is not None
    n_in = len(x_args) + int(has_res) + 4 + (2 if head else 0)

    def body(*refs):
        ins, outs = list(refs[:n_in]), list(refs[n_in:])
        x = _rows_tile([ins.pop(0) for _ in x_args])
        f_ref = ins.pop(0) if has_res else None
        m_ref, g_ref, wi_ref, wo_ref = ins[:4]
        xo_ref = outs.pop(0) if has_res else None
        h_ref, u_ref, s_ref = outs[:3]
        m = m_ref[0]
        if has_res:
            gate_idx, fac = gate
            x = x + (fac * m[gate_idx:gate_idx + 1, :]) * f_ref[...]
            xo_ref[...] = x
        hv = _norm_tile_fwd(x, m, g_ref[...], shift_idx, scale_idx)
        h_ref[...] = hv
        f = _ffn_tile_fwd(hv, wi_ref, wo_ref, u_ref, s_ref, F, cw)
        if head is None:
            outs[3][...] = f
            return
        fg_ref, t_ref = ins[4:6]
        dx_ref, df_ref, acc_ref = outs[3:6]
        _acc_init(acc_ref)
        lat = (pl.program_id(0) > 0).astype(F32)
        gate8 = 0.5 * m[8:9, :]
        x3 = x + gate8 * f
        inv3 = lax.rsqrt(jnp.mean(x3 * x3, axis=-1, keepdims=True) + EPS)
        xn = x3 * inv3
        fg = fg_ref[...]
        e = (xn * fg - t_ref[...]) * lat
        dy = e * (1.0 / D)
        dxn = dy * fg
        dx = inv3 * (dxn - xn * jnp.mean(dxn * xn, axis=-1, keepdims=True))
        dx_ref[...] = dx
        df_ref[...] = (gate8 * dx).astype(BF16)
        _acc_add(acc_ref, 0, (0.5 / D) * e * e)
        _acc_add(acc_ref, 1, dy * xn)
        _acc_add(acc_ref, 2, 0.5 * dx * f)

    in_specs = x_specs + ([_row_spec(D)] if has_res else []) + \
               [_mods_spec(D), _vec_spec(1, D), _resident(), _resident()]
    args = x_args + ([branch] if has_res else []) + [mods, g, w_in_t, w_out]
    out_specs = ([_row_spec(D)] if has_res else []) + [_row_spec(D), _row_spec(2 * F), _row_spec(F)]
    out_shape = ([jax.ShapeDtypeStruct((T, D), F32)] if has_res else []) + \
                [jax.ShapeDtypeStruct((T, D), BF16), jax.ShapeDtypeStruct((T, 2 * F), BF16),
                 jax.ShapeDtypeStruct((T, F), BF16)]
    if head is None:
        out_specs += [_row_spec(D)]
        out_shape += [jax.ShapeDtypeStruct((T, D), F32)]
    else:
        in_specs += [_vec_spec(1, D), pl.BlockSpec((ROW, D), lambda i: (jnp.maximum(i - 1, 0), 0))]
        args += list(head)
        out_specs += [_row_spec(D), _row_spec(D), _acc_spec(D)]
        out_shape += [jax.ShapeDtypeStruct((T, D), F32), jax.ShapeDtypeStruct((T, D), BF16),
                      jax.ShapeDtypeStruct((2, ACC_ROWS, D), F32)]
    out = pl.pallas_call(
        body, name=name, grid=(T // ROW,), in_specs=in_specs, out_specs=out_specs, out_shape=out_shape,
        compiler_params=_params(("arbitrary",) if head else ("parallel",)),
    )(*args)
    return tuple(out) if has_res else (None,) + tuple(out)


def _ffn_norm_bwd(df, u, w_in_t, w_out, x, dres, mods, g, shift_idx, scale_idx, gate, branch, name,
                  skip_first_tile=False):
    T, D = df.shape
    F = w_out.shape[0]
    cw = _ffn_chunk(F)
    nt = T // ROW
    has_gate = gate is not None
    x_specs, x_args, _ = _rows_operand(x)
    n_in = 7 + len(x_args) + int(has_gate)

    def body(*refs):
        ins, outs = list(refs[:n_in]), list(refs[n_in:])
        df_ref, u_ref, wi_ref, wo_ref = ins[:4]
        x_refs = ins[4:4 + len(x_args)]
        dr_ref = ins[4 + len(x_args)]
        b_ref = ins[5 + len(x_args)] if has_gate else None
        m_ref, g_ref = ins[-2:]
        du_ref, dx_ref = outs[:2]
        db_ref = outs[2] if has_gate else None
        acc_ref = outs[-1]
        _acc_init(acc_ref)
        dfv = df_ref[...]
        dh = jnp.zeros((ROW, D), F32)
        for j in range(F // cw):
            ds = lax.dot_general(dfv, wo_ref[j * cw:(j + 1) * cw, :], _NT, preferred_element_type=F32)
            a = u_ref[:, j * cw:(j + 1) * cw].astype(F32)
            b = u_ref[:, F + j * cw:F + (j + 1) * cw].astype(F32)
            sig = jax.nn.sigmoid(a)
            da = (ds * b * (sig * (1.0 + a * (1.0 - sig)))).astype(BF16)
            db = (ds * (a * sig)).astype(BF16)
            du_ref[:, j * cw:(j + 1) * cw] = da
            du_ref[:, F + j * cw:F + (j + 1) * cw] = db
            dh = dh + jnp.dot(da, wi_ref[j * cw:(j + 1) * cw, :], preferred_element_type=F32)
            dh = dh + jnp.dot(db, wi_ref[F + j * cw:F + (j + 1) * cw, :], preferred_element_type=F32)
        m = m_ref[0]
        dx = _norm_tile_bwd(_rows_tile(x_refs), dh, dr_ref[...], m, g_ref[...], shift_idx, scale_idx, acc_ref)
        dx_ref[...] = dx
        if has_gate:
            db_ref[...] = _gate_tile_bwd(dx, b_ref[...], m, gate, acc_ref)

    in_specs = [_row_spec(D), _row_spec(2 * F), _resident(), _resident()] + x_specs + [_row_spec(D)] + \
               ([_row_spec(D)] if has_gate else []) + [_mods_spec(D), _vec_spec(1, D)]
    args = [df, u, w_in_t, w_out] + x_args + [dres] + ([branch] if has_gate else []) + [mods, g]
    if skip_first_tile:
        dx_spec = pl.BlockSpec((ROW, D), lambda i: (jnp.maximum(i - 1, 0), 0))
        dx_shape = jax.ShapeDtypeStruct((T - ROW, D), F32)
    else:
        dx_spec = _row_spec(D)
        dx_shape = jax.ShapeDtypeStruct((T, D), F32)
    out_specs = [_row_spec(2 * F), dx_spec] + ([_row_spec(D)] if has_gate else []) + [_acc_spec(D)]
    out_shape = [jax.ShapeDtypeStruct((T, 2 * F), BF16), dx_shape] + \
                ([jax.ShapeDtypeStruct((T, D), BF16)] if has_gate else []) + \
                [jax.ShapeDtypeStruct((2, ACC_ROWS, D), F32)]
    out = pl.pallas_call(
        body, name=name, grid=(nt,), in_specs=in_specs, out_specs=out_specs, out_shape=out_shape,
        compiler_params=_params(("arbitrary",)),
    )(*args)
    if has_gate:
        return tuple(out)
    return out[0], out[1], None, out[2]


def _halo_specs(width, col, nt):
    per = ROW // HALO
    prev = pl.BlockSpec((HALO, width), lambda i, col=col: (jnp.maximum(i * per - 1, 0), col))
    nxt = pl.BlockSpec((HALO, width), lambda i, col=col: (jnp.minimum((i + 1) * per, nt * per - 1), col))
    return prev, nxt


def _f32(ref):
    return ref[...].astype(F32)


def _last_row(halo_ref):
    return halo_ref[HALO - 1:HALO, :].astype(F32)


def _first_row(halo_ref):
    return halo_ref[0:1, :].astype(F32)


def _shift_rows(v, prev_row, next_row):
    rows = lax.broadcasted_iota(jnp.int32, v.shape, 0)
    down = jnp.where(rows == 0, prev_row, pltpu.roll(v, 1, 0))
    up = jnp.where(rows == v.shape[0] - 1, next_row, pltpu.roll(v, v.shape[0] - 1, 0))
    return down, up


def _conv_fwd_operands(P, conv_w, D):
    nt = P.shape[0] // ROW
    cg_p, cg_n = _halo_specs(D, 1, nt)
    vc_p, vc_n = _halo_specs(D, 2, nt)
    specs = [_row_spec(D, 0), _row_spec(D, 1), _row_spec(D, 2), cg_p, vc_p, cg_n, vc_n, _vec_spec(3, D)]
    return specs, [P, P, P, P, P, P, P, conv_w]


def _conv_tile_fwd(refs, nt):
    bg_ref, cg_ref, vc_ref, cgp_ref, vcp_ref, cgn_ref, vcn_ref, w_ref = refs
    i = pl.program_id(0)
    has_prev = (i != 1).astype(F32)
    has_next = (i != nt - 1).astype(F32)
    u = _f32(cg_ref) * _f32(vc_ref)
    up_row = _last_row(cgp_ref) * _last_row(vcp_ref) * has_prev
    un_row = _first_row(cgn_ref) * _first_row(vcn_ref) * has_next
    um1, up1 = _shift_rows(u, up_row, un_row)
    w = w_ref[...]
    conv = um1 * w[0:1, :] + u * w[1:2, :] + up1 * w[2:3, :]
    return (_f32(bg_ref) * conv).astype(BF16)


def _conv_bwd_operands(P, dy, conv_w, D):
    nt = P.shape[0] // ROW
    bg_p, bg_n = _halo_specs(D, 0, nt)
    cg_p, cg_n = _halo_specs(D, 1, nt)
    vc_p, vc_n = _halo_specs(D, 2, nt)
    dy_p, dy_n = _halo_specs(D, 0, nt)
    specs = [_row_spec(D, 0), _row_spec(D, 1), _row_spec(D, 2), _row_spec(D, 0),
             bg_p, cg_p, vc_p, dy_p, bg_n, cg_n, vc_n, dy_n, _vec_spec(3, D)]
    return specs, [P, P, P, dy, P, P, P, dy, P, P, P, dy, conv_w]


def _conv_tile_bwd(refs, o_ref, acc_ref, D, nt):
    (bg_ref, cg_ref, vc_ref, dy_ref, bgp_ref, cgp_ref, vcp_ref, dyp_ref,
     bgn_ref, cgn_ref, vcn_ref, dyn_ref, w_ref) = refs
    i = pl.program_id(0)
    lat = (i > 0).astype(F32)
    has_prev = (i != 1).astype(F32)
    has_next = (i != nt - 1).astype(F32)
    bg = _f32(bg_ref)
    cg = _f32(cg_ref)
    vc = _f32(vc_ref)
    dyv = dy_ref[...] * lat
    u = cg * vc
    up_row = _last_row(cgp_ref) * _last_row(vcp_ref) * has_prev
    un_row = _first_row(cgn_ref) * _first_row(vcn_ref) * has_next
    um1, up1 = _shift_rows(u, up_row, un_row)
    w = w_ref[...]
    conv = um1 * w[0:1, :] + u * w[1:2, :] + up1 * w[2:3, :]
    dc = dyv * bg
    dcp_row = _last_row(dyp_ref) * _last_row(bgp_ref) * has_prev
    dcn_row = _first_row(dyn_ref) * _first_row(bgn_ref) * has_next
    dcm1, dcp1 = _shift_rows(dc, dcp_row, dcn_row)
    du = dcp1 * w[0:1, :] + dc * w[1:2, :] + dcm1 * w[2:3, :]
    o_ref[:, 0:D] = (dyv * conv).astype(BF16)
    o_ref[:, D:2 * D] = (du * vc * lat).astype(BF16)
    o_ref[:, 2 * D:3 * D] = (du * cg * lat).astype(BF16)
    _acc_add(acc_ref, 0, dc * um1)
    _acc_add(acc_ref, 1, dc * u)
    _acc_add(acc_ref, 2, dc * up1)


def _rope_tables(ctx_len, seq):
    n_freq = HEAD_DIM // 4
    rows = seq // GRID_W
    inv = ROPE_THETA ** (-jnp.arange(n_freq, dtype=F32) / n_freq)
    ar = jnp.arange(rows, dtype=F32)[:, None] * inv
    ac = jnp.arange(GRID_W, dtype=F32)[:, None] * inv

    def per_row(a):
        return jnp.repeat(a, GRID_W, axis=0)

    def per_col(a):
        return jnp.tile(a, (rows, 1))

    cos_t = jnp.concatenate([per_row(jnp.cos(ar)), per_row(jnp.cos(ar)), per_col(jnp.cos(ac)), per_col(jnp.cos(ac))], axis=1)
    sin_t = jnp.concatenate([per_row(-jnp.sin(ar)), per_row(jnp.sin(ar)), per_col(-jnp.sin(ac)), per_col(jnp.sin(ac))], axis=1)
    cos_t = jnp.concatenate([jnp.ones((ctx_len, HEAD_DIM), F32), cos_t], axis=0)
    sin_t = jnp.concatenate([jnp.zeros((ctx_len, HEAD_DIM), F32), sin_t], axis=0)
    return cos_t, sin_t


def _swap_halves(y):
    lanes = lax.broadcasted_iota(jnp.int32, y.shape, 1)
    first = (lanes % 64) < 32
    return jnp.where(first, pltpu.roll(y, HEAD_DIM - 32, 1), pltpu.roll(y, 32, 1))


def _to_row(col, n):
    return jnp.transpose(jnp.broadcast_to(col, (n, HEAD_DIM)))[0:1, :]


LOG2E = 1.4426950408889634
ATTN_PART_LANES = 256
ATTN_QUERY_ROWS = 768
ATTN_VMEM_LIMIT = 60 * 1024 * 1024


def _flash_fwd(q, k, v, name, tq=None, tk=None):
    T = q.shape[0]
    tq = tq or _pick(T, (ATTN_QUERY_ROWS, ROW))
    parts = GROUP * tq // ATTN_PART_LANES
    tk = tk or _pick(T, (2816, 1408, 768, 512, 256))
    ck = tk
    nk = T // tk
    GW = GROUP * HEAD_DIM

    def body(q_ref, k_ref, v_ref, o_ref, lse_ref, qs_ref, m_ref, l_ref, acc_ref, st_ref):
        ki = pl.program_id(2)

        @pl.when(ki == 0)
        def _():
            for g in range(GROUP):
                qs_ref[g * tq:(g + 1) * tq, :] = q_ref[:, g * HEAD_DIM:(g + 1) * HEAD_DIM]
            m_ref[...] = jnp.full(m_ref.shape, -jnp.inf, F32)
            l_ref[...] = jnp.zeros(l_ref.shape, F32)
            acc_ref[...] = jnp.zeros(acc_ref.shape, F32)

        w = ATTN_PART_LANES
        nck = tk // ck

        def lanes(p):
            return slice(p * w, (p + 1) * w)

        def keys(c):
            return slice(c * ck, (c + 1) * ck)

        def fold(a):
            return a.reshape(ck // 8, 8, w)

        def scores(p, c):
            st = lax.dot_general(k_ref[keys(c), :], qs_ref[lanes(p), :], _NT,
                                 preferred_element_type=F32) * (ATTN_SCALE * LOG2E)
            st_ref[keys(c), lanes(p)] = st
            return jnp.max(fold(st), axis=0)

        def new_max(p, partial):
            m_prev = m_ref[:, lanes(p)]
            m_new = jnp.maximum(m_prev, jnp.max(functools.reduce(jnp.maximum, partial), axis=0, keepdims=True))
            m_ref[:, lanes(p)] = m_new
            return m_new, jnp.exp2(m_prev - m_new)

        def weights(p, c, m_new):
            pt = jnp.exp2(st_ref[keys(c), lanes(p)] - m_new)
            pv = lax.dot_general(v_ref[keys(c), :], pt.astype(BF16), (((0,), (0,)), ((), ())),
                                 preferred_element_type=F32)
            return jnp.sum(fold(pt), axis=0), pv

        partial = [scores(0, c) for c in range(nck)]
        for p in range(parts):
            m_new, alpha = new_max(p, partial)
            partial, sums, pvs = [], [], []
            for c in range(nck):
                if p + 1 < parts:
                    partial.append(scores(p + 1, c))
                s8, pv = weights(p, c, m_new)
                sums.append(s8)
                pvs.append(pv)
            l_ref[:, lanes(p)] = alpha * l_ref[:, lanes(p)] + jnp.sum(sum(sums), axis=0, keepdims=True)
            acc_ref[:, lanes(p)] = alpha * acc_ref[:, lanes(p)] + sum(pvs)

        @pl.when(ki == nk - 1)
        def _():
            out = jnp.transpose(acc_ref[...] / l_ref[...])
            lse = m_ref[...] + jnp.log2(l_ref[...])
            for g in range(GROUP):
                o_ref[:, g * HEAD_DIM:(g + 1) * HEAD_DIM] = out[g * tq:(g + 1) * tq, :]
                lse_ref[0, g:g + 1, :] = lse[:, g * tq:(g + 1) * tq]

    return pl.pallas_call(
        body, name=name, grid=(N_KV_HEADS, T // tq, nk),
        in_specs=[pl.BlockSpec((tq, GW), lambda h, i, j: (i, h)),
                  pl.BlockSpec((tk, HEAD_DIM), lambda h, i, j: (j, h)),
                  pl.BlockSpec((tk, HEAD_DIM), lambda h, i, j: (j, h))],
        out_specs=[pl.BlockSpec((tq, GW), lambda h, i, j: (i, h)),
                   pl.BlockSpec((1, GROUP, tq), lambda h, i, j: (h, 0, i))],
        out_shape=[jax.ShapeDtypeStruct((T, N_Q_HEADS * HEAD_DIM), F32),
                   jax.ShapeDtypeStruct((N_KV_HEADS, GROUP, T), F32)],
        scratch_shapes=[pltpu.VMEM((GROUP * tq, HEAD_DIM), BF16), pltpu.VMEM((1, GROUP * tq), F32),
                        pltpu.VMEM((1, GROUP * tq), F32), pltpu.VMEM((HEAD_DIM, GROUP * tq), F32),
                        pltpu.VMEM((tk, GROUP * tq), F32)],
        compiler_params=pltpu.CompilerParams(dimension_semantics=("parallel", "parallel", "arbitrary"),
                                             vmem_limit_bytes=ATTN_VMEM_LIMIT),
    )(q, k, v)


def _flash_bwd(q, k, v, do, lse, delta, name, tq=None, tk=None, token=None):
    T = q.shape[0]
    tq = tq or _pick(T, (ATTN_QUERY_ROWS, ROW))
    tk = tk or _pick(T, (1408, 768, 512, 256))
    nk = T // tk
    GW = GROUP * HEAD_DIM
    nt = (((1,), (1,)), ((), ()))
    extra = [] if token is None else [token]

    def body(q_ref, do_ref, k_ref, v_ref, lse_ref, dl_ref, *rest):
        dq_ref, dk_ref, dv_ref, qs_ref, dos_ref, dqt_ref = rest[len(extra):]
        qi = pl.program_id(1)
        ki = pl.program_id(2)

        @pl.when(ki == 0)
        def _():
            for g in range(GROUP):
                qs_ref[g * tq:(g + 1) * tq, :] = q_ref[:, g * HEAD_DIM:(g + 1) * HEAD_DIM]
                dos_ref[g * tq:(g + 1) * tq, :] = do_ref[:, g * HEAD_DIM:(g + 1) * HEAD_DIM]
            dqt_ref[...] = jnp.zeros(dqt_ref.shape, F32)

        kk = k_ref[...]
        vv = v_ref[...]

        def lanes(p):
            return slice(p * tq, (p + 1) * tq)

        def products(p):
            st = lax.dot_general(kk, qs_ref[lanes(p), :], nt, preferred_element_type=F32)
            dpt = lax.dot_general(vv, dos_ref[lanes(p), :], nt, preferred_element_type=F32)
            return st, dpt

        dk_c = jnp.zeros((tk, HEAD_DIM), F32)
        dv_c = jnp.zeros((tk, HEAD_DIM), F32)
        ahead = products(0)
        for p in range(GROUP):
            st, dpt = ahead
            if p + 1 < GROUP:
                ahead = products(p + 1)
            pt = jnp.exp2(st * (ATTN_SCALE * LOG2E) - lse_ref[0, p:p + 1, :])
            dst = ((pt * (dpt - dl_ref[0, p:p + 1, :])) * ATTN_SCALE).astype(BF16)
            dv_c = dv_c + jnp.dot(pt.astype(BF16), dos_ref[lanes(p), :], preferred_element_type=F32)
            dk_c = dk_c + jnp.dot(dst, qs_ref[lanes(p), :], preferred_element_type=F32)
            dqt_ref[:, lanes(p)] += lax.dot_general(kk, dst, (((0,), (0,)), ((), ())), preferred_element_type=F32)
        rows = pl.ds(pl.multiple_of(ki * tk, tk), tk)

        @pl.when(qi == 0)
        def _():
            dk_ref[rows, :] = dk_c
            dv_ref[rows, :] = dv_c

        @pl.when(qi > 0)
        def _():
            dk_ref[rows, :] += dk_c
            dv_ref[rows, :] += dv_c

        @pl.when(ki == nk - 1)
        def _():
            dqv = jnp.transpose(dqt_ref[...])
            for g in range(GROUP):
                dq_ref[:, g * HEAD_DIM:(g + 1) * HEAD_DIM] = dqv[g * tq:(g + 1) * tq, :]

    return pl.pallas_call(
        body, name=name, grid=(N_KV_HEADS, T // tq, nk),
        in_specs=[pl.BlockSpec((tq, GW), lambda h, i, j: (i, h)),
                  pl.BlockSpec((tq, GW), lambda h, i, j: (i, h)),
                  pl.BlockSpec((tk, HEAD_DIM), lambda h, i, j: (j, h)),
                  pl.BlockSpec((tk, HEAD_DIM), lambda h, i, j: (j, h)),
                  pl.BlockSpec((1, GROUP, tq), lambda h, i, j: (h, 0, i)),
                  pl.BlockSpec((1, GROUP, tq), lambda h, i, j: (h, 0, i))] +
                 [pl.BlockSpec(t.shape, lambda h, i, j: (0, 0)) for t in extra],
        out_specs=[pl.BlockSpec((tq, GW), lambda h, i, j: (i, h)),
                   pl.BlockSpec((T, HEAD_DIM), lambda h, i, j: (0, h)),
                   pl.BlockSpec((T, HEAD_DIM), lambda h, i, j: (0, h))],
        out_shape=[jax.ShapeDtypeStruct((T, N_Q_HEADS * HEAD_DIM), F32),
                   jax.ShapeDtypeStruct((T, N_KV_HEADS * HEAD_DIM), F32),
                   jax.ShapeDtypeStruct((T, N_KV_HEADS * HEAD_DIM), F32)],
        scratch_shapes=[pltpu.VMEM((GROUP * tq, HEAD_DIM), BF16), pltpu.VMEM((GROUP * tq, HEAD_DIM), BF16),
                        pltpu.VMEM((HEAD_DIM, GROUP * tq), F32)],
        compiler_params=pltpu.CompilerParams(dimension_semantics=("arbitrary", "arbitrary", "arbitrary"),
                                             vmem_limit_bytes=ATTN_VMEM_LIMIT),
    )(q, do, k, v, lse, delta, *extra)


def _gate_specs(D):
    w = D // 2
    first = (3 * D + (N_Q_HEADS + 2 * N_KV_HEADS) * HEAD_DIM) // w
    return [pl.BlockSpec((ROW, w), lambda i, c=first + j: (i, c)) for j in range(4)]


def _merge_fwd(o, P, conv_w, wbc, wba, wo, D, name):
    T = o.shape[0]
    nt = T // ROW
    w = D // 2
    conv_specs, conv_args = _conv_fwd_operands(P, conv_w, D)
    nc = len(conv_args)

    def body(*refs):
        o_ref, g0, g1, g2, g3, wbc_ref, wba_ref, wo_ref, yc_ref, a1_ref, a2_ref, z_ref, mo_ref = refs[nc:]
        yc_ref[...] = _conv_tile_fwd(refs[:nc], nt)
        a1 = jnp.dot(yc_ref[...], wbc_ref[...], preferred_element_type=F32)
        a2 = jnp.dot(o_ref[...].astype(BF16), wba_ref[...], preferred_element_type=F32)
        a1_ref[...] = a1.astype(BF16)
        a2_ref[...] = a2.astype(BF16)
        for j, (gc, ga) in enumerate(((g0, g2), (g1, g3))):
            sl = slice(j * w, (j + 1) * w)
            z = jax.nn.sigmoid(_f32(gc)) * a1[:, sl] + jax.nn.sigmoid(_f32(ga)) * a2[:, sl]
            z_ref[:, sl] = z.astype(BF16)
        mo_ref[...] = jnp.dot(z_ref[...], wo_ref[...], preferred_element_type=F32)

    return pl.pallas_call(
        body, name=name, grid=(T // ROW,),
        in_specs=conv_specs + [_row_spec(D)] + _gate_specs(D) + [_resident()] * 3,
        out_specs=[_row_spec(D)] * 5,
        out_shape=[jax.ShapeDtypeStruct((T, D), BF16), jax.ShapeDtypeStruct((T, D), BF16),
                   jax.ShapeDtypeStruct((T, D), BF16), jax.ShapeDtypeStruct((T, D), BF16),
                   jax.ShapeDtypeStruct((T, D), F32)],
        compiler_params=_params(("parallel",)),
    )(*conv_args, o, P, P, P, P, wbc, wba, wo)


def _merge_bwd(dmo, a1, a2, o, P, wbc, wba, wo, D, name):
    T = a1.shape[0]
    w = D // 2

    def body(dmo_ref, a1_ref, a2_ref, o_ref, g0, g1, g2, g3, wbc_ref, wba_ref, wo_ref,
             d1_ref, d2_ref, dg_ref, dyc_ref, dob_ref, dl_ref):
        dz = lax.dot_general(dmo_ref[...], wo_ref[...], _NT, preferred_element_type=F32)
        for j, (gc, ga) in enumerate(((g0, g2), (g1, g3))):
            sl = slice(j * w, (j + 1) * w)
            dzs = dz[:, sl]
            sc = jax.nn.sigmoid(_f32(gc))
            sa = jax.nn.sigmoid(_f32(ga))
            d1_ref[:, sl] = (dzs * sc).astype(BF16)
            d2_ref[:, sl] = (dzs * sa).astype(BF16)
            dg_ref[:, j * w:(j + 1) * w] = (dzs * a1_ref[:, sl].astype(F32) * (sc * (1.0 - sc))).astype(BF16)
            dg_ref[:, D + j * w:D + (j + 1) * w] = (dzs * a2_ref[:, sl].astype(F32) * (sa * (1.0 - sa))).astype(BF16)
        dyc_ref[...] = lax.dot_general(d1_ref[...], wbc_ref[...], _NT, preferred_element_type=F32)
        dov = lax.dot_general(d2_ref[...], wba_ref[...], _NT, preferred_element_type=F32)
        dob_ref[...] = dov.astype(BF16)
        prod = dov * o_ref[...]
        for h in range(N_Q_HEADS):
            d = jnp.sum(prod[:, h * HEAD_DIM:(h + 1) * HEAD_DIM], axis=1, keepdims=True)
            dl_ref[h // GROUP, (h % GROUP):(h % GROUP) + 1, :] = _to_row(d, ROW)

    return pl.pallas_call(
        body, name=name, grid=(T // ROW,),
        in_specs=[_row_spec(D)] * 4 + _gate_specs(D) + [_resident()] * 3,
        out_specs=[_row_spec(D), _row_spec(D), _row_spec(2 * D), _row_spec(D), _row_spec(D),
                   pl.BlockSpec((N_KV_HEADS, GROUP, ROW), lambda i: (0, 0, i))],
        out_shape=[jax.ShapeDtypeStruct((T, D), BF16), jax.ShapeDtypeStruct((T, D), BF16),
                   jax.ShapeDtypeStruct((T, 2 * D), BF16), jax.ShapeDtypeStruct((T, D), F32),
                   jax.ShapeDtypeStruct((T, D), BF16), jax.ShapeDtypeStruct((N_KV_HEADS, GROUP, T), F32)],
        compiler_params=_params(("parallel",)),
    )(dmo, a1, a2, o, P, P, P, P, wbc, wba, wo)


def _adamw_math(w, g, m, v):
    m = ADAM_B1 * m + (1.0 - ADAM_B1) * g
    v = ADAM_B2 * v + (1.0 - ADAM_B2) * (g * g)
    m_hat = m / (1.0 - ADAM_B1 ** ADAM_STEP)
    v_hat = v / (1.0 - ADAM_B2 ** ADAM_STEP)
    delta = -ADAM_LR * (m_hat / (jnp.sqrt(v_hat) + ADAM_EPS) + ADAM_WD * w)
    return delta, m, v


def _adamw(w, g, m, v, name):
    R, C = w.shape
    tr = _pick(R, tuple(t for t in (256, 128, 64, 32, 16, 8) if t * C * 4 <= ADAMW_BLOCK_BYTES))

    def body(w_ref, g_ref, m_ref, v_ref, d_ref, mo_ref, vo_ref):
        d, mn, vn = _adamw_math(w_ref[...], g_ref[...], m_ref[...], v_ref[...])
        d_ref[...] = d
        mo_ref[...] = mn
        vo_ref[...] = vn

    spec = pl.BlockSpec((tr, C), lambda i: (i, 0))
    return pl.pallas_call(
        body, name=name, grid=(R // tr,),
        in_specs=[spec] * 4, out_specs=[spec] * 3,
        out_shape=[jax.ShapeDtypeStruct((R, C), F32)] * 3,
        compiler_params=_params(("parallel",)),
    )(w, g, m, v)


def _norm_mix_in_fwd(xprev, branch, mods, g, gate, shift_idx, scale_idx, w_t, gq, gk, cos_t, sin_t, name):
    x_specs, x_args, (T, D) = _rows_operand(xprev)
    N = w_t.shape[0]
    QW = N_Q_HEADS * HEAD_DIM
    KW = N_KV_HEADS * HEAD_DIM
    q0, k0, v0 = 3 * D, 3 * D + QW, 3 * D + QW + KW
    edges = [0, D, 2 * D, q0, k0, v0 + KW] + list(range(v0 + KW + D, N + 1, D))
    assert edges[-1] == N

    def body(*refs):
        f_ref, m_ref, g_ref, w_ref, gq_ref, gk_ref, c_ref, s_ref = refs[len(x_args):len(x_args) + 8]
        xo_ref, h_ref, p_ref, qo_ref, ko_ref, vo_ref = refs[len(x_args) + 8:]
        m = m_ref[0]
        gate_idx, fac = gate
        x = _rows_tile(refs[:len(x_args)]) + (fac * m[gate_idx:gate_idx + 1, :]) * f_ref[...]
        xo_ref[...] = x
        hv = _norm_tile_fwd(x, m, g_ref[...], shift_idx, scale_idx)
        h_ref[...] = hv
        c = c_ref[...]
        s = s_ref[...]

        def head(xh, gain):
            inv = lax.rsqrt(jnp.mean(xh * xh, axis=-1, keepdims=True) + EPS)
            y = (xh * inv) * gain
            return y * c + _swap_halves(y) * s

        for lo, hi in zip(edges[:-1], edges[1:]):
            pb = lax.dot_general(hv, w_ref[lo:hi, :], _NT, preferred_element_type=F32).astype(BF16)
            p_ref[:, lo:hi] = pb
            if lo == q0:
                for h in range(N_Q_HEADS):
                    sl = slice(h * HEAD_DIM, (h + 1) * HEAD_DIM)
                    qo_ref[:, sl] = head(pb[:, sl].astype(F32), gq_ref[...]).astype(BF16)
            elif lo == k0:
                for h in range(N_KV_HEADS):
                    sl = slice(h * HEAD_DIM, (h + 1) * HEAD_DIM)
                    ko_ref[:, sl] = head(pb[:, sl].astype(F32), gk_ref[...]).astype(BF16)
                vo_ref[...] = pb[:, KW:2 * KW]

    return pl.pallas_call(
        body, name=name, grid=(T // ROW,),
        in_specs=x_specs + [_row_spec(D), _mods_spec(D), _vec_spec(1, D), _resident(),
                            _vec_spec(1, HEAD_DIM), _vec_spec(1, HEAD_DIM), _row_spec(HEAD_DIM), _row_spec(HEAD_DIM)],
        out_specs=[_row_spec(D), _row_spec(D), _row_spec(N), _row_spec(QW), _row_spec(KW), _row_spec(KW)],
        out_shape=[jax.ShapeDtypeStruct((T, D), F32), jax.ShapeDtypeStruct((T, D), BF16),
                   jax.ShapeDtypeStruct((T, N), BF16), jax.ShapeDtypeStruct((T, QW), BF16),
                   jax.ShapeDtypeStruct((T, KW), BF16), jax.ShapeDtypeStruct((T, KW), BF16)],
        compiler_params=_params(("parallel",)),
    )(*x_args, branch, mods, g, w_t, gq, gk, cos_t, sin_t)


def _mix_in_norm_bwd(dyc, dgt, P, conv_w, dq, dk, dv, gq, gk, cos_t, sin_t, w_t, x, dres, mods, g, shift_idx,
                     scale_idx, gate, branch, name):
    T, D = x.shape
    nt = T // ROW
    QW = N_Q_HEADS * HEAD_DIM
    KW = N_KV_HEADS * HEAD_DIM
    q0, g0 = 3 * D, 3 * D + QW + 2 * KW
    assert g0 + dgt.shape[1] == w_t.shape[0]
    conv_specs, conv_args = _conv_bwd_operands(P, dyc, conv_w, D)
    nc = len(conv_args)

    def body(*refs):
        (dg_ref, q_ref, k_ref, dq_ref, dk_ref, dv_ref, gq_ref, gk_ref, c_ref, s_ref,
         w_ref, x_ref, dr_ref, b_ref, m_ref, g_ref,
         dx_ref, db_ref, acc_ref, dc_ref, cacc_ref, o_ref, qacc_ref) = refs[nc:]
        _acc_init(acc_ref)
        _acc_init(cacc_ref)
        _acc_init(qacc_ref)
        _conv_tile_bwd(refs[:nc], dc_ref, cacc_ref, D, nt)
        dh = jnp.dot(dc_ref[...], w_ref[0:q0, :], preferred_element_type=F32)
        c = c_ref[...]
        s = s_ref[...]

        def head(xh, d, gain):
            dyv = d * c + _swap_halves(d * s)
            inv = lax.rsqrt(jnp.mean(xh * xh, axis=-1, keepdims=True) + EPS)
            xn = xh * inv
            dxn = dyv * gain
            dxh = inv * (dxn - xn * jnp.mean(dxn * xn, axis=-1, keepdims=True))
            return dxh, jnp.sum(dyv * xn, axis=0, keepdims=True)

        dgq = jnp.zeros((1, HEAD_DIM), F32)
        for h in range(N_Q_HEADS):
            sl = slice(h * HEAD_DIM, (h + 1) * HEAD_DIM)
            dxh, dgh = head(q_ref[:, sl].astype(F32), dq_ref[:, sl], gq_ref[...])
            o_ref[:, sl] = dxh.astype(BF16)
            dgq = dgq + dgh
        dh = dh + jnp.dot(dg_ref[...], w_ref[g0:, :], preferred_element_type=F32)
        dgk = jnp.zeros((1, HEAD_DIM), F32)
        for h in range(N_KV_HEADS):
            sl = slice(h * HEAD_DIM, (h + 1) * HEAD_DIM)
            dxh, dgh = head(k_ref[:, sl].astype(F32), dk_ref[:, sl], gk_ref[...])
            o_ref[:, QW + h * HEAD_DIM:QW + (h + 1) * HEAD_DIM] = dxh.astype(BF16)
            dgk = dgk + dgh
        o_ref[:, QW + KW:QW + 2 * KW] = dv_ref[...].astype(BF16)
        qacc_ref[0, 0:1, 0:HEAD_DIM] += dgq
        qacc_ref[0, 1:2, 0:HEAD_DIM] += dgk
        dh = dh + jnp.dot(o_ref[...], w_ref[q0:g0, :], preferred_element_type=F32)
        m = m_ref[0]
        dx = _norm_tile_bwd(x_ref[...], dh, dr_ref[...], m, g_ref[...], shift_idx, scale_idx, acc_ref)
        dx_ref[...] = dx
        db_ref[...] = _gate_tile_bwd(dx, b_ref[...], m, gate, acc_ref)

    return pl.pallas_call(
        body, name=name, grid=(T // ROW,),
        in_specs=conv_specs +
                 [_row_spec(dgt.shape[1]), _row_spec(QW, q0 // QW), _row_spec(KW, (q0 + QW) // KW),
                  _row_spec(QW), _row_spec(KW), _row_spec(KW), _vec_spec(1, HEAD_DIM), _vec_spec(1, HEAD_DIM),
                  _row_spec(HEAD_DIM), _row_spec(HEAD_DIM),
                  _resident(), _row_spec(D), _row_spec(D), _row_spec(D), _mods_spec(D), _vec_spec(1, D)],
        out_specs=[_row_spec(D), _row_spec(D), _acc_spec(D), _row_spec(q0), _acc_spec(D),
                   _row_spec(QW + 2 * KW), _acc_spec(D)],
        out_shape=[jax.ShapeDtypeStruct((T, D), F32), jax.ShapeDtypeStruct((T, D), BF16),
                   jax.ShapeDtypeStruct((2, ACC_ROWS, D), F32), jax.ShapeDtypeStruct((T, q0), BF16),
                   jax.ShapeDtypeStruct((2, ACC_ROWS, D), F32), jax.ShapeDtypeStruct((T, QW + 2 * KW), BF16),
                   jax.ShapeDtypeStruct((2, ACC_ROWS, D), F32)],
        compiler_params=_params(("arbitrary",)),
    )(*conv_args, dgt, P, P, dq, dk, dv, gq, gk, cos_t, sin_t, w_t, x, dres, branch, mods, g)


def _adamw_transposed(w, gt, m, v, name):
    R, C = w.shape
    tc = LANES

    def body(w_ref, g_ref, m_ref, v_ref, go_ref, d_ref, mo_ref, vo_ref):
        g = jnp.transpose(g_ref[...])
        d, mn, vn = _adamw_math(w_ref[...], g, m_ref[...], v_ref[...])
        go_ref[...] = g
        d_ref[...] = d
        mo_ref[...] = mn
        vo_ref[...] = vn

    spec = pl.BlockSpec((R, tc), lambda j: (0, j))
    return pl.pallas_call(
        body, name=name, grid=(C // tc,),
        in_specs=[spec, pl.BlockSpec((tc, R), lambda j: (j, 0)), spec, spec], out_specs=[spec] * 4,
        out_shape=[jax.ShapeDtypeStruct((R, C), F32)] * 4,
        compiler_params=_params(("parallel",)),
    )(w, gt, m, v)


class _NoExchange:
    def __init__(self, rest):
        self.rest = rest

    def rest_weights(self, after):
        return self.rest

    def reduce_early(self, grads, tag):
        return None


def _local_step(xcat, target, mods, norm_g, final_g, gq, gk, conv_w, ffn1_w, hooks, rope):
    T, D = _rows_operand(xcat)[2]
    w1i, w1o = ffn1_w
    g1, g2, g3 = norm_g
    cos_t, sin_t = rope

    def after(value, token, name):
        return value if token is None else _after(value, token, name)

    _, h1, u1, s1, f1 = _norm_ffn_fwd(xcat, None, mods, g1, None, 0, 1, w1i, w1o, "f_ffn1")
    wi, wbc, wba, wo, w2i, w2o = hooks.rest_weights(f1)
    x1, h2, P, qn, kn, vb = _norm_mix_in_fwd(xcat, f1, mods, g2, (2, 0.5), 3, 4, wi, gq, gk, cos_t, sin_t, "f_mix_in")
    o, lse = _flash_fwd(qn, kn, vb, "f_attn")
    yc, a1, a2, z, mo = _merge_fwd(o, P, conv_w, wbc, wba, wo, D, "f_merge")
    x2, h3, u2, s2, dx3, df2, acc_head = _norm_ffn_fwd(x1, mo, mods, g3, (5, 1.0), 6, 7, w2i, w2o, "f_ffn2",
                                                       head=(final_g, target))

    du2, dx2, dmo, acc_n3 = _ffn_norm_bwd(df2, u2, w2i, w2o, x2, dx3, mods, g3, 6, 7, (5, 1.0), mo, "b_ffn2")
    g_w2o = _grad_matmul(s2, df2, "b_ffn2_out_dw")
    g_w2i = _grad_matmul(du2, h3, "b_ffn2_in_dw")

    g_wo = _grad_matmul(z, dmo, "b_mix_out_dw")
    da1, da2, dgt, dyc, dob, delta = _merge_bwd(dmo, a1, a2, o, P, wbc, wba, wo, D, "b_merge")
    g_wbc = _grad_matmul(yc, da1, "b_branch_conv_dw")
    g_wba = _grad_matmul(o, da2, "b_branch_attn_dw")
    token_a = hooks.reduce_early([g_wbc, g_wba, g_wo, g_w2i, g_w2o], "a")
    dq, dk, dv = _flash_bwd(qn, kn, vb, dob, lse, delta, "b_attn", token=token_a)
    dx1, df1, acc_n2, dconv, acc_conv, dqkv, acc_qk = _mix_in_norm_bwd(
        dyc, dgt, P, conv_w, dq, dk, dv, gq, gk, cos_t, sin_t, wi, x1, dx2, mods, g2, 3, 4, (2, 0.5), f1, "b_mix_in")
    d_parts = (dconv, dqkv, dgt)
    g_wi = jnp.concatenate([_grad_matmul(dp, h2, f"b_mix_in_dw_{i}") for i, dp in enumerate(d_parts)], axis=0)
    g1_b = after(g1, hooks.reduce_early([g_wi], "b"), "after_rs_b")

    du1, grad_x, _, acc_n1 = _ffn_norm_bwd(df1, u1, w1i, w1o, xcat, dx1, mods, g1_b, 0, 1, None, None, "b_ffn1",
                                           skip_first_tile=True)
    g_w1o = _grad_matmul(s1, df1, "b_ffn1_out_dw")
    g_w1i = _grad_matmul(du1, h1, "b_ffn1_in_dw", token=hooks.reduce_early([g_w1o], "c"))

    grads = (g_w1i, g_w1o, g_wi, g_wbc, g_wba, g_wo, g_w2i, g_w2o)
    accs = (acc_head, acc_n3, acc_n2, acc_n1, acc_conv, acc_qk)
    return grad_x, grads, accs


def _place():
    return lax.axis_index("x"), lax.axis_index("y"), lax.axis_index("c")


def _other_chips(x, y):
    return [(1 - x, y), (x, 1 - y), (1 - x, 1 - y)]


def _allgather8(v, name):
    R, N = v.shape

    def body(v_ref, out_ref, send_sems, recv_sems, local_sem):
        x, y, c = _place()
        me, sibling = (x, y, c), (x, y, 1 - c)
        chips = _other_chips(x, y)

        def blk(px, py, pc):
            return out_ref.at[4 * px + 2 * py + pc]

        def copy(k, block, to, src=None):
            return pltpu.make_async_remote_copy(
                src_ref=blk(*block) if src is None else src, dst_ref=blk(*block),
                send_sem=send_sems.at[k], recv_sem=recv_sems.at[k], device_id=to, device_id_type=MESH)

        mine = pltpu.make_async_copy(v_ref, blk(*me), local_sem)
        mine.start()
        first = [copy(0, me, sibling, src=v_ref)]
        first += [copy(1 + j, me, (*chip, c), src=v_ref) for j, chip in enumerate(chips)]
        for cp in first:
            cp.start()
        passed = [copy(4 + j, (*chip, c), sibling) for j, chip in enumerate(chips)]
        for j, chip in enumerate(chips):
            copy(1 + j, (*chip, c), me).wait_recv()
            passed[j].start()
        copy(0, sibling, me).wait_recv()
        for j, chip in enumerate(chips):
            copy(4 + j, (*chip, 1 - c), me).wait_recv()
        for cp in first + passed:
            cp.wait_send()
        mine.wait()

    return pl.pallas_call(
        body, name=name,
        out_shape=jax.ShapeDtypeStruct((N_DEV, R, N), v.dtype),
        in_specs=[pl.BlockSpec(memory_space=pltpu.VMEM)],
        out_specs=pl.BlockSpec(memory_space=pltpu.VMEM),
        scratch_shapes=[pltpu.SemaphoreType.DMA((7,)), pltpu.SemaphoreType.DMA((7,)), pltpu.SemaphoreType.DMA],
        compiler_params=pltpu.CompilerParams(vmem_limit_bytes=VMEM_LIMIT),
    )(v)


def _any_specs(n):
    return [pl.BlockSpec(memory_space=pl.ANY)] * n


def _pair_exchange(grads, name):
    n = len(grads)

    def body(*refs):
        g, land = refs[:n], refs[n:2 * n]
        send_sems, recv_sems = refs[2 * n:]
        x, y, c = _place()
        sibling = (x, y, 1 - c)
        copies = []
        for t in range(n):
            half = grads[t].shape[0] // (2 * N_CHIPS)
            for s in range(N_CHIPS):
                cp = pltpu.make_async_remote_copy(
                    src_ref=g[t].at[pl.ds((2 * s + 1 - c) * half, half), :], dst_ref=land[t].at[s],
                    send_sem=send_sems.at[N_CHIPS * t + s], recv_sem=recv_sems.at[N_CHIPS * t + s],
                    device_id=sibling, device_id_type=MESH)
                cp.start()
                copies.append(cp)
        for cp in copies:
            cp.wait_recv()
        for cp in copies:
            cp.wait_send()

    return pl.pallas_call(
        body, name=name,
        out_shape=[jax.ShapeDtypeStruct((N_CHIPS, a.shape[0] // (2 * N_CHIPS), a.shape[1]), a.dtype) for a in grads],
        in_specs=_any_specs(n), out_specs=_any_specs(n),
        scratch_shapes=[pltpu.SemaphoreType.DMA((N_CHIPS * n,)), pltpu.SemaphoreType.DMA((N_CHIPS * n,))],
    )(*grads)


def _place_shard(w2, idx, transpose, name, token):
    if transpose:
        D, rs = w2.shape
        tr = LANES
        in_spec = pl.BlockSpec((D, tr), lambda i, idx: (0, i))
    else:
        rs, D = w2.shape
        tr = _pick(rs, (352, 256, 128, 64, 32, 16))
        in_spec = pl.BlockSpec((tr, D), lambda i, idx: (i, 0))
    steps = rs // tr

    def body(idx_ref, w_ref, t_ref, o_ref):
        v = w_ref[...]
        o_ref[...] = (jnp.transpose(v) if transpose else v).astype(BF16)

    return pl.pallas_call(
        body, name=name,
        grid_spec=pltpu.PrefetchScalarGridSpec(
            num_scalar_prefetch=1, grid=(steps,),
            in_specs=[in_spec, pl.BlockSpec(token.shape, lambda i, idx: (0, 0))],
            out_specs=pl.BlockSpec((tr, D), lambda i, idx: (idx[1] * steps + i, 0))),
        out_shape=jax.ShapeDtypeStruct((N_CHIPS * rs, D), BF16),
        compiler_params=_params(("arbitrary",)),
    )(idx, w2, token)


def _pair_sum(g, landed, idx, name, token=None):
    _, half, D = landed.shape
    g4 = g.reshape(N_CHIPS, 2, half, D)
    tr = _pick(half, (416, 352, 128))
    extra = [] if token is None else [token]

    def body(idx_ref, g_ref, l_ref, *rest):
        rest[-1][...] = (g_ref[0].astype(F32) + l_ref[...].astype(F32)).astype(BF16)

    return pl.pallas_call(
        body, name=name,
        grid_spec=pltpu.PrefetchScalarGridSpec(
            num_scalar_prefetch=1, grid=(N_CHIPS, half // tr),
            in_specs=[pl.BlockSpec((1, 1, tr, D), lambda s, i, idx: (idx[1 + s], idx[0], i, 0)),
                      pl.BlockSpec((1, tr, D), lambda s, i, idx: (idx[1 + s], i, 0))] +
                     [pl.BlockSpec(t.shape, lambda s, i, idx: (0, 0)) for t in extra],
            out_specs=pl.BlockSpec((1, tr, D), lambda s, i, idx: (s, i, 0))),
        out_shape=jax.ShapeDtypeStruct((N_CHIPS, half, D), BF16),
        compiler_params=_params(("arbitrary", "arbitrary")),
    )(idx, g4, landed, *extra)


_HBM = pl.BlockSpec(memory_space=pltpu.HBM)
_SEM = pl.BlockSpec(memory_space=pltpu.SEMAPHORE)
_EFFECT = pltpu.SideEffectType.DATAFLOW_SIDE_EFFECTING


def _in_hbm(a):
    return pltpu.with_memory_space_constraint(a, pltpu.HBM)


def _split_copies(n, per, make):
    def start(nbuf, name, bufs):
        def body(*refs):
            ins = refs[:nbuf]
            send_sems, recv_sems = refs[nbuf], refs[nbuf + 1]
            token = refs[-1]
            for t in range(n):
                for j in range(per):
                    make(ins, t, j, send_sems.at[per * t + j], recv_sems.at[per * t + j]).start()
            token[...] = jnp.zeros(token.shape, token.dtype)

        out = pl.pallas_call(
            body, name=name,
            out_shape=(pltpu.SemaphoreType.DMA((per * n,)), pltpu.SemaphoreType.DMA((per * n,)),
                       *[pltpu.HBM(b.shape, b.dtype) for b in bufs], jax.ShapeDtypeStruct((8, 128), F32)),
            in_specs=[_HBM] * nbuf,
            out_specs=(_SEM, _SEM, *[_HBM] * nbuf, pl.BlockSpec(memory_space=pltpu.VMEM)),
            input_output_aliases={i: 2 + i for i in range(nbuf)},
            compiler_params=pltpu.CompilerParams(has_side_effects=_EFFECT),
        )(*[_in_hbm(b) for b in bufs])
        return out[0], out[1], list(out[2:2 + nbuf]), out[-1]

    def wait(nbuf, name, send_sems, recv_sems, bufs, after):
        def body(*refs):
            ins = refs[:nbuf]
            ss, rs = refs[nbuf], refs[nbuf + 1]
            for t in range(n):
                for j in range(per):
                    cp = make(ins, t, j, ss.at[per * t + j], rs.at[per * t + j])
                    cp.wait_send()
                    cp.wait_recv()

        return pl.pallas_call(
            body, name=name,
            out_shape=[pltpu.HBM(b.shape, b.dtype) for b in bufs],
            in_specs=[_HBM] * nbuf + [_SEM, _SEM, pl.BlockSpec(memory_space=pl.ANY)],
            out_specs=[_HBM] * nbuf,
            input_output_aliases={i: i for i in range(nbuf)},
            compiler_params=pltpu.CompilerParams(has_side_effects=_EFFECT),
        )(*bufs, send_sems, recv_sems, after)

    return start, wait


def _chip_exchange_split(n):
    def make(bufs, t, j, send_sem, recv_sem):
        x, y, c = _place()
        chip = _other_chips(x, y)[j]
        return pltpu.make_async_remote_copy(src_ref=bufs[t].at[1 + j], dst_ref=bufs[n + t].at[j], send_sem=send_sem,
                                            recv_sem=recv_sem, device_id=(*chip, c), device_id_type=MESH)

    return _split_copies(n, 3, make)


def _weights_gather_split(fulls):
    def make(bufs, t, j, send_sem, recv_sem):
        x, y, c = _place()
        chip = _other_chips(x, y)[j]
        rs = fulls[t].shape[0] // N_CHIPS
        rows = bufs[t].at[pl.ds((2 * x + y) * rs + c * (rs // 2), rs // 2), :]
        return pltpu.make_async_remote_copy(src_ref=rows, dst_ref=rows, send_sem=send_sem, recv_sem=recv_sem,
                                            device_id=(*chip, c), device_id_type=MESH)

    return _split_copies(len(fulls), 3, make)


def _weights_pass_on(fulls, name):
    n = len(fulls)

    def body(*refs):
        full = refs[n:2 * n]
        send_sems, recv_sems = refs[2 * n:]
        x, y, c = _place()
        chips = _other_chips(x, y)

        def copy(t, j, h):
            rs = fulls[t].shape[0] // N_CHIPS
            px, py = chips[j]
            rows = full[t].at[pl.ds((2 * px + py) * rs + h * (rs // 2), rs // 2), :]
            return pltpu.make_async_remote_copy(src_ref=rows, dst_ref=rows, send_sem=send_sems.at[3 * t + j],
                                                recv_sem=recv_sems.at[3 * t + j], device_id=(x, y, 1 - c),
                                                device_id_type=MESH)

        for t in range(n):
            for j in range(3):
                copy(t, j, c).start()
        for t in range(n):
            for j in range(3):
                copy(t, j, 1 - c).wait_recv()
        for t in range(n):
            for j in range(3):
                copy(t, j, c).wait_send()

    return pl.pallas_call(
        body, name=name,
        out_shape=[jax.ShapeDtypeStruct(f.shape, f.dtype) for f in fulls],
        in_specs=_any_specs(n), out_specs=_any_specs(n),
        input_output_aliases={t: t for t in range(n)},
        scratch_shapes=[pltpu.SemaphoreType.DMA((3 * n,)), pltpu.SemaphoreType.DMA((3 * n,))],
    )(*fulls)


def _after(value, token, name):
    def body(v_ref, t_ref, o_ref):
        o_ref[...] = v_ref[...]

    return pl.pallas_call(
        body, name=name, out_shape=jax.ShapeDtypeStruct(value.shape, value.dtype),
        in_specs=_whole(2), out_specs=pl.BlockSpec(memory_space=pltpu.VMEM),
    )(value, token)


def _chip_sum(ps, landed, idx, name):
    _, half, D = ps.shape
    tr = _pick(half, (416, 352, 128))
    steps = half // tr

    def body(idx_ref, p_ref, l_ref, o_ref):
        acc = p_ref[0].astype(F32)
        for j in range(3):
            acc = acc + l_ref[j].astype(F32)
        o_ref[...] = acc

    return pl.pallas_call(
        body, name=name,
        grid_spec=pltpu.PrefetchScalarGridSpec(
            num_scalar_prefetch=1, grid=(steps,),
            in_specs=[pl.BlockSpec((1, tr, D), lambda i, idx: (0, i, 0)),
                      pl.BlockSpec((3, tr, D), lambda i, idx: (0, i, 0))],
            out_specs=pl.BlockSpec((tr, D), lambda i, idx: (idx[0] * steps + i, 0))),
        out_shape=jax.ShapeDtypeStruct((2 * half, D), F32),
        compiler_params=_params(("arbitrary",)),
    )(idx, ps, landed)


def _pair_swap(shards, name):
    n = len(shards)

    def body(*refs):
        full = refs[n:2 * n]
        send_sems, recv_sems = refs[2 * n:]
        x, y, c = _place()

        def half(t, h):
            rows = shards[t].shape[0] // 2
            return full[t].at[pl.ds(h * rows, rows), :]

        def copy(t, h):
            return pltpu.make_async_remote_copy(src_ref=half(t, h), dst_ref=half(t, h), send_sem=send_sems.at[t],
                                                recv_sem=recv_sems.at[t], device_id=(x, y, 1 - c),
                                                device_id_type=MESH)

        for t in range(n):
            copy(t, c).start()
        for t in range(n):
            copy(t, 1 - c).wait_recv()
        for t in range(n):
            copy(t, c).wait_send()

    return pl.pallas_call(
        body, name=name,
        out_shape=[jax.ShapeDtypeStruct(a.shape, a.dtype) for a in shards],
        in_specs=_any_specs(n), out_specs=_any_specs(n),
        input_output_aliases={t: t for t in range(n)},
        scratch_shapes=[pltpu.SemaphoreType.DMA((n,)), pltpu.SemaphoreType.DMA((n,))],
    )(*shards)


def _gather_begin(fulls, tag):
    start, wait = _weights_gather_split(fulls)
    send_sems, recv_sems, bufs, token = start(len(fulls), f"ag_{tag}_start", fulls)
    return (wait, send_sems, recv_sems, bufs), token


def _gather_end(state, after, tag):
    wait, send_sems, recv_sems, bufs = state
    landed = wait(len(bufs), f"ag_{tag}_wait", send_sems, recv_sems, bufs, after)
    return _weights_pass_on(landed, f"ag_{tag}_pass_on")


class _Exchanges:
    def __init__(self, fulls_rest, idx):
        self.idx = idx
        self._rest, self.token = _gather_begin(fulls_rest, "rest")
        self._early = []

    def rest_weights(self, after):
        return _gather_end(self._rest, after, "rest")

    def reduce_early(self, grads, tag, token=None):
        landed = _pair_exchange(grads, "rs_pair_exchange_" + tag)
        sums = [_pair_sum(g, l, self.idx, f"rs_pair_sum_{tag}{t}", token)
                for t, (g, l) in enumerate(zip(grads, landed))]
        zones = [lax.empty((3,) + s.shape[1:], s.dtype) for s in sums]
        start, wait = _chip_exchange_split(len(sums))
        send_sems, recv_sems, bufs, token = start(2 * len(sums), "rs_chip_start_" + tag, sums + zones)
        self._early.append((tag, wait, send_sems, recv_sems, bufs))
        return token

    def finish(self, tags, after):
        halves = []
        for tag, wait, send_sems, recv_sems, bufs in self._early:
            if tag in tags:
                n = len(bufs) // 2
                done = wait(len(bufs), "rs_chip_wait_" + tag, send_sems, recv_sems, bufs, after)
                halves += [_chip_sum(p, l, self.idx, f"rs_chip_sum_{tag}{t}")
                           for t, (p, l) in enumerate(zip(done[:n], done[n:]))]
        return halves


N_MOD = 9
PACK_HEAD, PACK_N3, PACK_N2, PACK_N1, PACK_CONV, PACK_QK = 0, 16, 32, 48, 64, 80
MOD_SRC = ((PACK_N1, 0), (PACK_N1, 1), (PACK_N2, 3), (PACK_N2, 0), (PACK_N2, 1),
           (PACK_N3, 3), (PACK_N3, 0), (PACK_N3, 1), (PACK_HEAD, 2))
CTX_ROW = 8


def _silu(v):
    return v * jax.nn.sigmoid(v)


def _whole(n):
    return [pl.BlockSpec(memory_space=pltpu.VMEM)] * n


def _mod_rows(cin, w_sh, b_sh, name):
    def body(c_ref, w_ref, b_ref, o_ref):
        a = _silu(c_ref[...]).astype(BF16)
        o_ref[...] = jnp.dot(a, w_ref[...].astype(BF16), preferred_element_type=F32) + b_ref[...]

    return pl.pallas_call(
        body, name=name, out_shape=jax.ShapeDtypeStruct((cin.shape[0], w_sh.shape[1]), F32),
        in_specs=_whole(3), out_specs=pl.BlockSpec(memory_space=pltpu.VMEM),
        compiler_params=pltpu.CompilerParams(vmem_limit_bytes=VMEM_LIMIT),
    )(cin, w_sh, b_sh)


def _small_reduce(gathered, name):
    _, _, D = gathered.shape

    def body(g_ref, loss_ref, db_ref, gn_ref, cv_ref, qk_ref, dm_ref):
        tot = g_ref[0]
        for r in range(1, N_DEV):
            tot = tot + g_ref[r]

        def both(block, row):
            return tot[block + row:block + row + 1, :] + tot[block + 8 + row:block + 8 + row + 1, :]

        loss = jnp.sum(both(PACK_HEAD, 0), axis=1, keepdims=True)
        loss_ref[...] = jnp.broadcast_to(loss, loss_ref.shape)
        db_ref[...] = jnp.zeros(db_ref.shape, F32)
        dm_ref[...] = jnp.zeros(dm_ref.shape, F32)
        for j, (block, row) in enumerate(MOD_SRC):
            db_ref[j:j + 1, :] = both(block, row)
            dm_ref[CTX_ROW, j:j + 1, :] = tot[block + row:block + row + 1, :]
            for r in range(N_DEV):
                dm_ref[r, j:j + 1, :] = g_ref[r, block + 8 + row:block + 8 + row + 1, :]
        gn_ref[...] = jnp.zeros(gn_ref.shape, F32)
        gn_ref[0:1, :] = both(PACK_N1, 2)
        gn_ref[8:9, :] = both(PACK_N2, 2)
        gn_ref[16:17, :] = both(PACK_N3, 2)
        gn_ref[24:25, :] = both(PACK_HEAD, 1)
        cv_ref[...] = jnp.zeros(cv_ref.shape, F32)
        for r in range(3):
            cv_ref[r:r + 1, :] = both(PACK_CONV, r)
        qk_ref[...] = jnp.zeros(qk_ref.shape, F32)
        qk_ref[0:1, 0:HEAD_DIM] = both(PACK_QK, 0)[:, 0:HEAD_DIM]
        qk_ref[0:1, HEAD_DIM:2 * HEAD_DIM] = both(PACK_QK, 1)[:, 0:HEAD_DIM]

    return pl.pallas_call(
        body, name=name,
        out_shape=[jax.ShapeDtypeStruct((8, 128), F32), jax.ShapeDtypeStruct((16, D), F32),
                   jax.ShapeDtypeStruct((32, D), F32), jax.ShapeDtypeStruct((8, D), F32),
                   jax.ShapeDtypeStruct((8, D), F32), jax.ShapeDtypeStruct((16, 16, D), F32)],
        in_specs=_whole(1), out_specs=_whole(6),
        compiler_params=pltpu.CompilerParams(vmem_limit_bytes=VMEM_LIMIT),
    )(gathered)


def _wmod_grad(cin, dm_sh, w_sh, name):
    def body(c_ref, d_ref, w_ref, gw_ref, cp_ref):
        a = _silu(c_ref[...]).astype(BF16)
        d = d_ref[...].astype(BF16)
        gw_ref[...] = lax.dot_general(a, d, (((0,), (0,)), ((), ())), preferred_element_type=F32)
        cp_ref[...] = lax.dot_general(d, w_ref[...].astype(BF16), (((1,), (1,)), ((), ())),
                                      preferred_element_type=F32)

    return pl.pallas_call(
        body, name=name,
        out_shape=[jax.ShapeDtypeStruct(w_sh.shape, F32), jax.ShapeDtypeStruct(cin.shape, F32)],
        in_specs=_whole(3), out_specs=_whole(2),
        compiler_params=pltpu.CompilerParams(vmem_limit_bytes=VMEM_LIMIT),
    )(cin, dm_sh, w_sh)


def _cctx_grad(parts, c_ctx8, name):
    def body(p_ref, c_ref, o_ref):
        tot = p_ref[0] + p_ref[2] + p_ref[4] + p_ref[6]
        cv = c_ref[...]
        sig = jax.nn.sigmoid(cv)
        rows = lax.broadcasted_iota(jnp.int32, tot.shape, 0)
        o_ref[...] = jnp.where(rows == 0, tot * (sig * (1.0 + cv * (1.0 - sig))), 0.0)

    return pl.pallas_call(
        body, name=name, out_shape=jax.ShapeDtypeStruct(c_ctx8.shape, F32),
        in_specs=_whole(2), out_specs=pl.BlockSpec(memory_space=pltpu.VMEM),
    )(parts, c_ctx8)


def _pad_rows(a, rows):
    return jnp.pad(a, ((0, rows - a.shape[0]), (0, 0)))


def _pack_small(c_ctx, b_mod, n1, n2, n3, final_g, gq, gk, conv_sh, D):
    misc = jnp.concatenate([gq, gk, conv_sh.reshape(1, -1)], axis=1)
    return jnp.concatenate([_pad_rows(c_ctx[None], 8), _pad_rows(b_mod.reshape(N_MOD, D), 16), _pad_rows(n1, 8),
                            _pad_rows(n2, 8), _pad_rows(n3, 8), _pad_rows(final_g[None], 8), _pad_rows(misc, 8)], axis=0)


def _unpack_small(p, D, conv_shape):
    misc = p[56:57]
    return dict(c_ctx=p[0], b_mod=p[8:8 + N_MOD].reshape(1, N_MOD * D), norm1_g=p[24:25], norm2_g=p[32:33],
                norm3_g=p[40:41], final_g=p[48], q_norm_g=misc[:, 0:HEAD_DIM], k_norm_g=misc[:, HEAD_DIM:2 * HEAD_DIM],
                conv_w=misc[:, 2 * HEAD_DIM:].reshape(conv_shape))


WEIGHT_ORDER = ("c_ctx", "w_mod", "b_mod", "norm1_g", "norm2_g", "norm3_g", "ffn1_w_in", "ffn1_w_out", "w_in",
                "conv_w", "q_norm_g", "k_norm_g", "w_branch_conv", "w_branch_attn", "w_out", "ffn2_w_in",
                "ffn2_w_out", "final_g")
BIG = ("ffn1_w_in", "ffn1_w_out", "w_in", "w_branch_conv", "w_branch_attn", "w_out", "ffn2_w_in", "ffn2_w_out")
COLUMN_SHARDED = ("ffn1_w_in", "w_in", "ffn2_w_in")


def kernel(x, c, ctx, c_ctx, w_mod, b_mod, norm1_g, norm2_g, norm3_g, ffn1_w_in, ffn1_w_out, w_in, conv_w, q_norm_g, k_norm_g, w_branch_conv, w_branch_attn, w_out, ffn2_w_in, ffn2_w_out, final_g, loss_target, m_c_ctx, m_w_mod, m_b_mod, m_norm1_g, m_norm2_g, m_norm3_g, m_ffn1_w_in, m_ffn1_w_out, m_w_in, m_conv_w, m_q_norm_g, m_k_norm_g, m_w_branch_conv, m_w_branch_attn, m_w_out, m_ffn2_w_in, m_ffn2_w_out, m_final_g, v_c_ctx, v_w_mod, v_b_mod, v_norm1_g, v_norm2_g, v_norm3_g, v_ffn1_w_in, v_ffn1_w_out, v_w_in, v_conv_w, v_q_norm_g, v_k_norm_g, v_w_branch_conv, v_w_branch_attn, v_w_out, v_ffn2_w_in, v_ffn2_w_out, v_final_g):
    w = dict(c_ctx=c_ctx, w_mod=w_mod, b_mod=b_mod, norm1_g=norm1_g, norm2_g=norm2_g, norm3_g=norm3_g,
             ffn1_w_in=ffn1_w_in, ffn1_w_out=ffn1_w_out, w_in=w_in, conv_w=conv_w, q_norm_g=q_norm_g,
             k_norm_g=k_norm_g, w_branch_conv=w_branch_conv, w_branch_attn=w_branch_attn, w_out=w_out,
             ffn2_w_in=ffn2_w_in, ffn2_w_out=ffn2_w_out, final_g=final_g)
    m = dict(c_ctx=m_c_ctx, w_mod=m_w_mod, b_mod=m_b_mod, norm1_g=m_norm1_g, norm2_g=m_norm2_g, norm3_g=m_norm3_g,
             ffn1_w_in=m_ffn1_w_in, ffn1_w_out=m_ffn1_w_out, w_in=m_w_in, conv_w=m_conv_w, q_norm_g=m_q_norm_g,
             k_norm_g=m_k_norm_g, w_branch_conv=m_w_branch_conv, w_branch_attn=m_w_branch_attn, w_out=m_w_out,
             ffn2_w_in=m_ffn2_w_in, ffn2_w_out=m_ffn2_w_out, final_g=m_final_g)
    v = dict(c_ctx=v_c_ctx, w_mod=v_w_mod, b_mod=v_b_mod, norm1_g=v_norm1_g, norm2_g=v_norm2_g, norm3_g=v_norm3_g,
             ffn1_w_in=v_ffn1_w_in, ffn1_w_out=v_ffn1_w_out, w_in=v_w_in, conv_w=v_conv_w, q_norm_g=v_q_norm_g,
             k_norm_g=v_k_norm_g, w_branch_conv=v_w_branch_conv, w_branch_attn=v_w_branch_attn, w_out=v_w_out,
             ffn2_w_in=v_ffn2_w_in, ffn2_w_out=v_ffn2_w_out, final_g=v_final_g)

    xi, yi, ci = _place()
    dev = 4 * xi + 2 * yi + ci
    shard = 2 * xi + yi
    idx = jnp.stack([ci, shard, 2 * (1 - xi) + yi, 2 * xi + (1 - yi), 2 * (1 - xi) + (1 - yi)]).astype(jnp.int32)
    D = x.shape[-1]
    ctx_len = ctx.shape[1]
    assert ctx_len == ROW and c.shape == (1, D)
    mcols = w_mod.shape[2]
    ccols = conv_w.shape[2]

    def place(names, token):
        fulls = []
        for n in names:
            fulls.append(_place_shard(w[n][0], idx, n in COLUMN_SHARDED, "place_" + n, token))
            token = fulls[-1][:16, :HEAD_DIM]
        return fulls

    ffn1_gather, ffn1_token = _gather_begin(place(BIG[:2], c), "ffn1")
    fulls_rest = place(BIG[2:], ffn1_token)

    rope = _rope_tables(ctx_len, x.shape[1])
    c8 = jnp.broadcast_to(c, (8, D))
    for token, name in ((fulls_rest[-1][:16, :HEAD_DIM], "after_place"), (rope[0], "after_rope_cos"),
                        (rope[1], "after_rope_sin")):
        c8 = _after(c8, token, name)
    c_all = _allgather8(c8, "ag_c")[:, 0, :]
    cin = jnp.concatenate([c_all, _pad_rows(c_ctx[None], 8)], axis=0)
    b_sh = lax.dynamic_slice(b_mod, (0, shard * mcols), (1, mcols))
    mod_sh = _mod_rows(cin, w_mod[0], b_sh, "mod_rows")
    conv_rows = jnp.pad(conv_w[0], ((0, 8 - conv_w.shape[1]), (0, mcols - ccols)))
    mod_all = _allgather8(jnp.concatenate([mod_sh, conv_rows], axis=0), "ag_mod")
    mod_full = jnp.concatenate([mod_all[2 * s, :16] for s in range(N_CHIPS)], axis=1)
    conv_full = jnp.concatenate([mod_all[2 * s, 16:16 + conv_w.shape[1], :ccols] for s in range(N_CHIPS)], axis=1)
    mod_lat = lax.dynamic_slice(mod_full, (dev, 0), (1, N_MOD * D)).reshape(N_MOD, D)
    mod_ctx = mod_full[CTX_ROW].reshape(N_MOD, D)
    mods = jnp.stack([_pad_rows(mod_ctx, 16), _pad_rows(mod_lat, 16)])

    ffn1_w = _gather_end(ffn1_gather, mods, "ffn1")
    hooks = _Exchanges(fulls_rest, idx)

    xcat = (ctx[0], x[0])
    norm1_first = _after(norm1_g, hooks.token, "after_ag_rest")
    grad_x, grads, accs = _local_step(xcat, loss_target[0], mods, (norm1_first, norm2_g, norm3_g), final_g[None],
                                      q_norm_g, k_norm_g, conv_full, ffn1_w, hooks, rope)
    g = {}

    pack = jnp.concatenate([a.reshape(2 * ACC_ROWS, D) for a in accs], axis=0)
    gathered = _allgather8(pack, "ag_small")
    loss8, db_mod, g_norms, g_conv, g_qk, dm = _small_reduce(gathered, "small_reduce")
    dm_sh = lax.dynamic_slice(dm[:, :N_MOD, :].reshape(16, N_MOD * D), (0, shard * mcols), (16, mcols))
    g_wmod, cpart = _wmod_grad(cin, dm_sh, w_mod[0], "wmod_grad")
    g["w_mod"] = g_wmod[None]
    cparts = _allgather8(cpart[CTX_ROW:CTX_ROW + 8], "ag_cctx")
    g_cctx = _cctx_grad(cparts, _pad_rows(c_ctx[None], 8), "cctx_grad")
    g_conv_sh = lax.dynamic_slice(g_conv, (0, shard * ccols), (conv_w.shape[1], ccols))
    g_misc = jnp.concatenate([g_qk[0:1, 0:2 * HEAD_DIM], g_conv_sh.reshape(1, -1)], axis=1)
    g_pack = jnp.concatenate([g_cctx, db_mod, g_norms, _pad_rows(g_misc, 8)], axis=0)

    def packed(p):
        return _pack_small(p["c_ctx"], p["b_mod"], p["norm1_g"], p["norm2_g"], p["norm3_g"], p["final_g"],
                           p["q_norm_g"], p["k_norm_g"], p["conv_w"][0], D)

    d_pack, m_pack, v_pack = _adamw(packed(w), g_pack, packed(m), packed(v), "adamw_small")

    g.update(_unpack_small(g_pack, D, conv_w.shape))
    delta = _unpack_small(d_pack, D, conv_w.shape)
    new_m = _unpack_small(m_pack, D, conv_w.shape)
    new_v = _unpack_small(v_pack, D, conv_w.shape)

    def update(n, g2):
        if n in COLUMN_SHARDED:
            g2, d2, m2, v2 = _adamw_transposed(w[n][0], g2, m[n][0], v[n][0], "adamw_" + n)
        else:
            d2, m2, v2 = _adamw(w[n][0], g2, m[n][0], v[n][0], "adamw_" + n)
        g[n], delta[n], new_m[n], new_v[n] = g2[None], d2[None], m2[None], v2[None]
        return v2

    token_d = hooks.reduce_early([grads[0]], "d", token=d_pack[:8, :HEAD_DIM])
    h_wbc, h_wba, h_wo, h_w2i, h_w2o, h_wi, h_w1o = hooks.finish("abc", token_d)
    done = _pair_swap([h_w1o, h_wi, h_wbc, h_wba, h_wo, h_w2i, h_w2o], "rs_pair_swap")
    last = update("w_mod", g_wmod)
    for n, r in zip(BIG[1:], done):
        last = update(n, r)
    (h_w1i,) = hooks.finish("d", last)
    update(BIG[0], _pair_swap([h_w1i], "rs_pair_swap_d")[0])

    loss = loss8[0, 0]
    return (loss, grad_x[None], *[g[n] for n in WEIGHT_ORDER], *[delta[n] for n in WEIGHT_ORDER],
            *[new_m[n] for n in WEIGHT_ORDER], *[new_v[n] for n in WEIGHT_ORDER])
```

```python
import functools

import jax
import jax.numpy as jnp
from jax import lax
from jax.experimental import pallas as pl
from jax.experimental.pallas import tpu as pltpu

F32 = jnp.float32
BF16 = jnp.bfloat16

HEAD_DIM = 128
N_Q_HEADS = 8
N_KV_HEADS = 2
GROUP = N_Q_HEADS // N_KV_HEADS
GRID_W = 64
ROPE_THETA = 10000.0
EPS = 1e-6
ATTN_SCALE = HEAD_DIM ** -0.5

ADAM_LR = 0.001
ADAM_B1 = 0.9
ADAM_B2 = 0.999
ADAM_EPS = 1e-08
ADAM_WD = 0.01
ADAM_STEP = 10

LANES = 128
ROW = 256
HALO = 16
ACC_ROWS = 8
N_CHIPS = 4
N_DEV = 8
MESH = pl.DeviceIdType.MESH
VMEM_LIMIT = 48 * 1024 * 1024
ADAMW_BLOCK_BYTES = 1024 * 1024


def _pick(n, prefs):
    for p in prefs:
        if n % p == 0:
            return p
    return n


def _params(sem):
    return pltpu.CompilerParams(dimension_semantics=sem, vmem_limit_bytes=VMEM_LIMIT)


def _stream(i):
    return jnp.minimum(i, 1)


def _grad_matmul(a, b, name, token=None):
    (T, M), (T2, N) = a.shape, b.shape
    assert T == T2, (a.shape, b.shape)
    tm = _pick(M, (1664, 1408, 1024, 512, 256, 128))
    tk = _pick(T, (2816, 1408, 768, 512, 256))
    nk = T // tk
    extra = [] if token is None else [token]

    def body(a_ref, b_ref, *rest):
        o_ref, acc_ref = rest[len(extra):]
        p = lax.dot_general(a_ref[...].astype(BF16), b_ref[...].astype(BF16), (((0,), (0,)), ((), ())),
                            preferred_element_type=F32)
        k = pl.program_id(1)

        @pl.when(k == 0)
        def _():
            acc_ref[...] = p

        @pl.when(k > 0)
        def _():
            acc_ref[...] += p

        @pl.when(k == nk - 1)
        def _():
            o_ref[...] = acc_ref[...].astype(BF16)

    return pl.pallas_call(
        body, name=name, grid=(M // tm, nk),
        in_specs=[pl.BlockSpec((tk, tm), lambda i, k: (k, i)), pl.BlockSpec((tk, N), lambda i, k: (k, 0))] +
                 [pl.BlockSpec(t.shape, lambda i, k: (0, 0)) for t in extra],
        out_specs=pl.BlockSpec((tm, N), lambda i, k: (i, 0)),
        out_shape=jax.ShapeDtypeStruct((M, N), BF16),
        scratch_shapes=[pltpu.VMEM((tm, N), F32)],
        compiler_params=_params(("parallel", "arbitrary")),
    )(a, b, *extra)


def _row_spec(width, col=0):
    return pl.BlockSpec((ROW, width), lambda i, col=col: (i, col))


def _mods_spec(D):
    return pl.BlockSpec((1, 16, D), lambda i: (_stream(i), 0, 0))


def _acc_spec(D):
    return pl.BlockSpec((1, ACC_ROWS, D), lambda i: (_stream(i), 0, 0))


def _vec_spec(rows, D):
    return pl.BlockSpec((rows, D), lambda i: (0, 0))


def _acc_init(acc_ref):
    i = pl.program_id(0)

    @pl.when(i <= 1)
    def _():
        acc_ref[...] = jnp.zeros_like(acc_ref)


def _acc_add(acc_ref, row, val):
    acc_ref[0, row:row + 1, :] += jnp.sum(val, axis=0, keepdims=True)


def _rows_operand(x):
    if not isinstance(x, tuple):
        return [_row_spec(x.shape[1])], [x], x.shape
    ctx, lat = x
    D = lat.shape[1]
    assert ctx.shape == (ROW, D)
    specs = [pl.BlockSpec((ROW, D), lambda i: (0, 0)), pl.BlockSpec((ROW, D), lambda i: (jnp.maximum(i - 1, 0), 0))]
    return specs, [ctx, lat], (ROW + lat.shape[0], D)


def _rows_tile(refs):
    if len(refs) == 1:
        return refs[0][...]
    return jnp.where(pl.program_id(0) == 0, refs[0][...], refs[1][...])


def _norm_tile_fwd(x, m, g, shift_idx, scale_idx):
    inv = lax.rsqrt(jnp.mean(x * x, axis=-1, keepdims=True) + EPS)
    y = (x * inv) * g
    return (y * (1.0 + m[scale_idx:scale_idx + 1, :]) + m[shift_idx:shift_idx + 1, :]).astype(BF16)


def _norm_tile_bwd(x, dh, dres, m, g, shift_idx, scale_idx, acc_ref):
    inv = lax.rsqrt(jnp.mean(x * x, axis=-1, keepdims=True) + EPS)
    xn = x * inv
    dy = dh * (1.0 + m[scale_idx:scale_idx + 1, :])
    dxn = dy * g
    _acc_add(acc_ref, 0, dh)
    _acc_add(acc_ref, 1, dh * (xn * g))
    _acc_add(acc_ref, 2, dy * xn)
    return inv * (dxn - xn * jnp.mean(dxn * xn, axis=-1, keepdims=True)) + dres


def _gate_tile_bwd(dx, branch, m, gate, acc_ref):
    gate_idx, fac = gate
    _acc_add(acc_ref, 3, fac * dx * branch)
    return ((fac * m[gate_idx:gate_idx + 1, :]) * dx).astype(BF16)


_NT = (((1,), (1,)), ((), ()))


def _ffn_chunk(F):
    return _pick(F, (2816, 1408, 512, 256, 128))


def _resident():
    return pl.BlockSpec(memory_space=pltpu.VMEM)


def _ffn_tile_fwd(hv, wi_ref, wo_ref, u_ref, s_ref, F, cw):
    acc = jnp.zeros((hv.shape[0], wo_ref.shape[1]), F32)
    for j in range(F // cw):
        a = lax.dot_general(hv, wi_ref[j * cw:(j + 1) * cw, :], _NT, preferred_element_type=F32)
        b = lax.dot_general(hv, wi_ref[F + j * cw:F + (j + 1) * cw, :], _NT, preferred_element_type=F32)
        s = ((a * jax.nn.sigmoid(a)) * b).astype(BF16)
        u_ref[:, j * cw:(j + 1) * cw] = a.astype(BF16)
        u_ref[:, F + j * cw:F + (j + 1) * cw] = b.astype(BF16)
        s_ref[:, j * cw:(j + 1) * cw] = s
        acc = acc + jnp.dot(s, wo_ref[j * cw:(j + 1) * cw, :], preferred_element_type=F32)
    return acc


def _norm_ffn_fwd(xprev, branch, mods, g, gate, shift_idx, scale_idx, w_in_t, w_out, name, head=None):
    x_specs, x_args, (T, D) = _rows_operand(xprev)
    F = w_out.shape[0]
    cw = _ffn_chunk(F)
    has_res = branch is not None
    n_in = len(x_args) + int(has_res) + 4 + (2 if head else 0)

    def body(*refs):
        ins, outs = list(refs[:n_in]), list(refs[n_in:])
        x = _rows_tile([ins.pop(0) for _ in x_args])
        f_ref = ins.pop(0) if has_res else None
        m_ref, g_ref, wi_ref, wo_ref = ins[:4]
        xo_ref = outs.pop(0) if has_res else None
        h_ref, u_ref, s_ref = outs[:3]
        m = m_ref[0]
        if has_res:
            gate_idx, fac = gate
            x = x + (fac * m[gate_idx:gate_idx + 1, :]) * f_ref[...]
            xo_ref[...] = x
        hv = _norm_tile_fwd(x, m, g_ref[...], shift_idx, scale_idx)
        h_ref[...] = hv
        f = _ffn_tile_fwd(hv, wi_ref, wo_ref, u_ref, s_ref, F, cw)
        if head is None:
            outs[3][...] = f
            return
        fg_ref, t_ref = ins[4:6]
        dx_ref, df_ref, acc_ref = outs[3:6]
        _acc_init(acc_ref)
        lat = (pl.program_id(0) > 0).astype(F32)
        gate8 = 0.5 * m[8:9, :]
        x3 = x + gate8 * f
        inv3 = lax.rsqrt(jnp.mean(x3 * x3, axis=-1, keepdims=True) + EPS)
        xn = x3 * inv3
        fg = fg_ref[...]
        e = (xn * fg - t_ref[...]) * lat
        dy = e * (1.0 / D)
        dxn = dy * fg
        dx = inv3 * (dxn - xn * jnp.mean(dxn * xn, axis=-1, keepdims=True))
        dx_ref[...] = dx
        df_ref[...] = (gate8 * dx).astype(BF16)
        _acc_add(acc_ref, 0, (0.5 / D) * e * e)
        _acc_add(acc_ref, 1, dy * xn)
        _acc_add(acc_ref, 2, 0.5 * dx * f)

    in_specs = x_specs + ([_row_spec(D)] if has_res else []) + \
               [_mods_spec(D), _vec_spec(1, D), _resident(), _resident()]
    args = x_args + ([branch] if has_res else []) + [mods, g, w_in_t, w_out]
    out_specs = ([_row_spec(D)] if has_res else []) + [_row_spec(D), _row_spec(2 * F), _row_spec(F)]
    out_shape = ([jax.ShapeDtypeStruct((T, D), F32)] if has_res else []) + \
                [jax.ShapeDtypeStruct((T, D), BF16), jax.ShapeDtypeStruct((T, 2 * F), BF16),
                 jax.ShapeDtypeStruct((T, F), BF16)]
    if head is None:
        out_specs += [_row_spec(D)]
        out_shape += [jax.ShapeDtypeStruct((T, D), F32)]
    else:
        in_specs += [_vec_spec(1, D), pl.BlockSpec((ROW, D), lambda i: (jnp.maximum(i - 1, 0), 0))]
        args += list(head)
        out_specs += [_row_spec(D), _row_spec(D), _acc_spec(D)]
        out_shape += [jax.ShapeDtypeStruct((T, D), F32), jax.ShapeDtypeStruct((T, D), BF16),
                      jax.ShapeDtypeStruct((2, ACC_ROWS, D), F32)]
    out = pl.pallas_call(
        body, name=name, grid=(T // ROW,), in_specs=in_specs, out_specs=out_specs, out_shape=out_shape,
        compiler_params=_params(("arbitrary",) if head else ("parallel",)),
    )(*args)
    return tuple(out) if has_res else (None,) + tuple(out)


def _ffn_norm_bwd(df, u, w_in_t, w_out, x, dres, mods, g, shift_idx, scale_idx, gate, branch, name,
                  skip_first_tile=False):
    T, D = df.shape
    F = w_out.shape[0]
    cw = _ffn_chunk(F)
    nt = T // ROW
    has_gate = gate is not None
    x_specs, x_args, _ = _rows_operand(x)
    n_in = 7 + len(x_args) + int(has_gate)

    def body(*refs):
        ins, outs = list(refs[:n_in]), list(refs[n_in:])
        df_ref, u_ref, wi_ref, wo_ref = ins[:4]
        x_refs = ins[4:4 + len(x_args)]
        dr_ref = ins[4 + len(x_args)]
        b_ref = ins[5 + len(x_args)] if has_gate else None
        m_ref, g_ref = ins[-2:]
        du_ref, dx_ref = outs[:2]
        db_ref = outs[2] if has_gate else None
        acc_ref = outs[-1]
        _acc_init(acc_ref)
        dfv = df_ref[...]
        dh = jnp.zeros((ROW, D), F32)
        for j in range(F // cw):
            ds = lax.dot_general(dfv, wo_ref[j * cw:(j + 1) * cw, :], _NT, preferred_element_type=F32)
            a = u_ref[:, j * cw:(j + 1) * cw].astype(F32)
            b = u_ref[:, F + j * cw:F + (j + 1) * cw].astype(F32)
            sig = jax.nn.sigmoid(a)
            da = (ds * b * (sig * (1.0 + a * (1.0 - sig)))).astype(BF16)
            db = (ds * (a * sig)).astype(BF16)
            du_ref[:, j * cw:(j + 1) * cw] = da
            du_ref[:, F + j * cw:F + (j + 1) * cw] = db
            dh = dh + jnp.dot(da, wi_ref[j * cw:(j + 1) * cw, :], preferred_element_type=F32)
            dh = dh + jnp.dot(db, wi_ref[F + j * cw:F + (j + 1) * cw, :], preferred_element_type=F32)
        m = m_ref[0]
        dx = _norm_tile_bwd(_rows_tile(x_refs), dh, dr_ref[...], m, g_ref[...], shift_idx, scale_idx, acc_ref)
        dx_ref[...] = dx
        if has_gate:
            db_ref[...] = _gate_tile_bwd(dx, b_ref[...], m, gate, acc_ref)

    in_specs = [_row_spec(D), _row_spec(2 * F), _resident(), _resident()] + x_specs + [_row_spec(D)] + \
               ([_row_spec(D)] if has_gate else []) + [_mods_spec(D), _vec_spec(1, D)]
    args = [df, u, w_in_t, w_out] + x_args + [dres] + ([branch] if has_gate else []) + [mods, g]
    if skip_first_tile:
        dx_spec = pl.BlockSpec((ROW, D), lambda i: (jnp.maximum(i - 1, 0), 0))
        dx_shape = jax.ShapeDtypeStruct((T - ROW, D), F32)
    else:
        dx_spec = _row_spec(D)
        dx_shape = jax.ShapeDtypeStruct((T, D), F32)
    out_specs = [_row_spec(2 * F), dx_spec] + ([_row_spec(D)] if has_gate else []) + [_acc_spec(D)]
    out_shape = [jax.ShapeDtypeStruct((T, 2 * F), BF16), dx_shape] + \
                ([jax.ShapeDtypeStruct((T, D), BF16)] if has_gate else []) + \
                [jax.ShapeDtypeStruct((2, ACC_ROWS, D), F32)]
    out = pl.pallas_call(
        body, name=name, grid=(nt,), in_specs=in_specs, out_specs=out_specs, out_shape=out_shape,
        compiler_params=_params(("arbitrary",)),
    )(*args)
    if has_gate:
        return tuple(out)
    return out[0], out[1], None, out[2]


def _halo_specs(width, col, nt):
    per = ROW // HALO
    prev = pl.BlockSpec((HALO, width), lambda i, col=col: (jnp.maximum(i * per - 1, 0), col))
    nxt = pl.BlockSpec((HALO, width), lambda i, col=col: (jnp.minimum((i + 1) * per, nt * per - 1), col))
    return prev, nxt


def _f32(ref):
    return ref[...].astype(F32)


def _last_row(halo_ref):
    return halo_ref[HALO - 1:HALO, :].astype(F32)


def _first_row(halo_ref):
    return halo_ref[0:1, :].astype(F32)


def _shift_rows(v, prev_row, next_row):
    rows = lax.broadcasted_iota(jnp.int32, v.shape, 0)
    down = jnp.where(rows == 0, prev_row, pltpu.roll(v, 1, 0))
    up = jnp.where(rows == v.shape[0] - 1, next_row, pltpu.roll(v, v.shape[0] - 1, 0))
    return down, up


def _conv_fwd_operands(P, conv_w, D):
    nt = P.shape[0] // ROW
    cg_p, cg_n = _halo_specs(D, 1, nt)
    vc_p, vc_n = _halo_specs(D, 2, nt)
    specs = [_row_spec(D, 0), _row_spec(D, 1), _row_spec(D, 2), cg_p, vc_p, cg_n, vc_n, _vec_spec(3, D)]
    return specs, [P, P, P, P, P, P, P, conv_w]


def _conv_tile_fwd(refs, nt):
    bg_ref, cg_ref, vc_ref, cgp_ref, vcp_ref, cgn_ref, vcn_ref, w_ref = refs
    i = pl.program_id(0)
    has_prev = (i != 1).astype(F32)
    has_next = (i != nt - 1).astype(F32)
    u = _f32(cg_ref) * _f32(vc_ref)
    up_row = _last_row(cgp_ref) * _last_row(vcp_ref) * has_prev
    un_row = _first_row(cgn_ref) * _first_row(vcn_ref) * has_next
    um1, up1 = _shift_rows(u, up_row, un_row)
    w = w_ref[...]
    conv = um1 * w[0:1, :] + u * w[1:2, :] + up1 * w[2:3, :]
    return (_f32(bg_ref) * conv).astype(BF16)


def _conv_bwd_operands(P, dy, conv_w, D):
    nt = P.shape[0] // ROW
    bg_p, bg_n = _halo_specs(D, 0, nt)
    cg_p, cg_n = _halo_specs(D, 1, nt)
    vc_p, vc_n = _halo_specs(D, 2, nt)
    dy_p, dy_n = _halo_specs(D, 0, nt)
    specs = [_row_spec(D, 0), _row_spec(D, 1), _row_spec(D, 2), _row_spec(D, 0),
             bg_p, cg_p, vc_p, dy_p, bg_n, cg_n, vc_n, dy_n, _vec_spec(3, D)]
    return specs, [P, P, P, dy, P, P, P, dy, P, P, P, dy, conv_w]


def _conv_tile_bwd(refs, o_ref, acc_ref, D, nt):
    (bg_ref, cg_ref, vc_ref, dy_ref, bgp_ref, cgp_ref, vcp_ref, dyp_ref,
     bgn_ref, cgn_ref, vcn_ref, dyn_ref, w_ref) = refs
    i = pl.program_id(0)
    lat = (i > 0).astype(F32)
    has_prev = (i != 1).astype(F32)
    has_next = (i != nt - 1).astype(F32)
    bg = _f32(bg_ref)
    cg = _f32(cg_ref)
    vc = _f32(vc_ref)
    dyv = dy_ref[...] * lat
    u = cg * vc
    up_row = _last_row(cgp_ref) * _last_row(vcp_ref) * has_prev
    un_row = _first_row(cgn_ref) * _first_row(vcn_ref) * has_next
    um1, up1 = _shift_rows(u, up_row, un_row)
    w = w_ref[...]
    conv = um1 * w[0:1, :] + u * w[1:2, :] + up1 * w[2:3, :]
    dc = dyv * bg
    dcp_row = _last_row(dyp_ref) * _last_row(bgp_ref) * has_prev
    dcn_row = _first_row(dyn_ref) * _first_row(bgn_ref) * has_next
    dcm1, dcp1 = _shift_rows(dc, dcp_row, dcn_row)
    du = dcp1 * w[0:1, :] + dc * w[1:2, :] + dcm1 * w[2:3, :]
    o_ref[:, 0:D] = (dyv * conv).astype(BF16)
    o_ref[:, D:2 * D] = (du * vc * lat).astype(BF16)
    o_ref[:, 2 * D:3 * D] = (du * cg * lat).astype(BF16)
    _acc_add(acc_ref, 0, dc * um1)
    _acc_add(acc_ref, 1, dc * u)
    _acc_add(acc_ref, 2, dc * up1)


def _rope_tables(ctx_len, seq):
    n_freq = HEAD_DIM // 4
    rows = seq // GRID_W
    inv = ROPE_THETA ** (-jnp.arange(n_freq, dtype=F32) / n_freq)
    ar = jnp.arange(rows, dtype=F32)[:, None] * inv
    ac = jnp.arange(GRID_W, dtype=F32)[:, None] * inv

    def per_row(a):
        return jnp.repeat(a, GRID_W, axis=0)

    def per_col(a):
        return jnp.tile(a, (rows, 1))

    cos_t = jnp.concatenate([per_row(jnp.cos(ar)), per_row(jnp.cos(ar)), per_col(jnp.cos(ac)), per_col(jnp.cos(ac))], axis=1)
    sin_t = jnp.concatenate([per_row(-jnp.sin(ar)), per_row(jnp.sin(ar)), per_col(-jnp.sin(ac)), per_col(jnp.sin(ac))], axis=1)
    cos_t = jnp.concatenate([jnp.ones((ctx_len, HEAD_DIM), F32), cos_t], axis=0)
    sin_t = jnp.concatenate([jnp.zeros((ctx_len, HEAD_DIM), F32), sin_t], axis=0)
    return cos_t, sin_t


def _swap_halves(y):
    lanes = lax.broadcasted_iota(jnp.int32, y.shape, 1)
    first = (lanes % 64) < 32
    return jnp.where(first, pltpu.roll(y, HEAD_DIM - 32, 1), pltpu.roll(y, 32, 1))


def _to_row(col, n):
    return jnp.transpose(jnp.broadcast_to(col, (n, HEAD_DIM)))[0:1, :]


LOG2E = 1.4426950408889634
ATTN_PART_LANES = 256
ATTN_QUERY_ROWS = 768
ATTN_VMEM_LIMIT = 60 * 1024 * 1024


def _flash_fwd(q, k, v, name, tq=None, tk=None):
    T = q.shape[0]
    tq = tq or _pick(T, (ATTN_QUERY_ROWS, ROW))
    parts = GROUP * tq // ATTN_PART_LANES
    tk = tk or _pick(T, (2816, 1408, 768, 512, 256))
    ck = tk
    nk = T // tk
    GW = GROUP * HEAD_DIM

    def body(q_ref, k_ref, v_ref, o_ref, lse_ref, qs_ref, m_ref, l_ref, acc_ref, st_ref):
        ki = pl.program_id(2)

        @pl.when(ki == 0)
        def _():
            for g in range(GROUP):
                qs_ref[g * tq:(g + 1) * tq, :] = q_ref[:, g * HEAD_DIM:(g + 1) * HEAD_DIM]
            m_ref[...] = jnp.full(m_ref.shape, -jnp.inf, F32)
            l_ref[...] = jnp.zeros(l_ref.shape, F32)
            acc_ref[...] = jnp.zeros(acc_ref.shape, F32)

        w = ATTN_PART_LANES
        nck = tk // ck

        def lanes(p):
            return slice(p * w, (p + 1) * w)

        def keys(c):
            return slice(c * ck, (c + 1) * ck)

        def fold(a):
            return a.reshape(ck // 8, 8, w)

        def scores(p, c):
            st = lax.dot_general(k_ref[keys(c), :], qs_ref[lanes(p), :], _NT,
                                 preferred_element_type=F32) * (ATTN_SCALE * LOG2E)
            st_ref[keys(c), lanes(p)] = st
            return jnp.max(fold(st), axis=0)

        def new_max(p, partial):
            m_prev = m_ref[:, lanes(p)]
            m_new = jnp.maximum(m_prev, jnp.max(functools.reduce(jnp.maximum, partial), axis=0, keepdims=True))
            m_ref[:, lanes(p)] = m_new
            return m_new, jnp.exp2(m_prev - m_new)

        def weights(p, c, m_new):
            pt = jnp.exp2(st_ref[keys(c), lanes(p)] - m_new)
            pv = lax.dot_general(v_ref[keys(c), :], pt.astype(BF16), (((0,), (0,)), ((), ())),
                                 preferred_element_type=F32)
            return jnp.sum(fold(pt), axis=0), pv

        partial = [scores(0, c) for c in range(nck)]
        for p in range(parts):
            m_new, alpha = new_max(p, partial)
            partial, sums, pvs = [], [], []
            for c in range(nck):
                if p + 1 < parts:
                    partial.append(scores(p + 1, c))
                s8, pv = weights(p, c, m_new)
                sums.append(s8)
                pvs.append(pv)
            l_ref[:, lanes(p)] = alpha * l_ref[:, lanes(p)] + jnp.sum(sum(sums), axis=0, keepdims=True)
            acc_ref[:, lanes(p)] = alpha * acc_ref[:, lanes(p)] + sum(pvs)

        @pl.when(ki == nk - 1)
        def _():
            out = jnp.transpose(acc_ref[...] / l_ref[...])
            lse = m_ref[...] + jnp.log2(l_ref[...])
            for g in range(GROUP):
                o_ref[:, g * HEAD_DIM:(g + 1) * HEAD_DIM] = out[g * tq:(g + 1) * tq, :]
                lse_ref[0, g:g + 1, :] = lse[:, g * tq:(g + 1) * tq]

    return pl.pallas_call(
        body, name=name, grid=(N_KV_HEADS, T // tq, nk),
        in_specs=[pl.BlockSpec((tq, GW), lambda h, i, j: (i, h)),
                  pl.BlockSpec((tk, HEAD_DIM), lambda h, i, j: (j, h)),
                  pl.BlockSpec((tk, HEAD_DIM), lambda h, i, j: (j, h))],
        out_specs=[pl.BlockSpec((tq, GW), lambda h, i, j: (i, h)),
                   pl.BlockSpec((1, GROUP, tq), lambda h, i, j: (h, 0, i))],
        out_shape=[jax.ShapeDtypeStruct((T, N_Q_HEADS * HEAD_DIM), F32),
                   jax.ShapeDtypeStruct((N_KV_HEADS, GROUP, T), F32)],
        scratch_shapes=[pltpu.VMEM((GROUP * tq, HEAD_DIM), BF16), pltpu.VMEM((1, GROUP * tq), F32),
                        pltpu.VMEM((1, GROUP * tq), F32), pltpu.VMEM((HEAD_DIM, GROUP * tq), F32),
                        pltpu.VMEM((tk, GROUP * tq), F32)],
        compiler_params=pltpu.CompilerParams(dimension_semantics=("parallel", "parallel", "arbitrary"),
                                             vmem_limit_bytes=ATTN_VMEM_LIMIT),
    )(q, k, v)


def _flash_bwd(q, k, v, do, lse, delta, name, tq=None, tk=None, token=None):
    T = q.shape[0]
    tq = tq or _pick(T, (ATTN_QUERY_ROWS, ROW))
    tk = tk or _pick(T, (1408, 768, 512, 256))
    nk = T // tk
    GW = GROUP * HEAD_DIM
    nt = (((1,), (1,)), ((), ()))
    extra = [] if token is None else [token]

    def body(q_ref, do_ref, k_ref, v_ref, lse_ref, dl_ref, *rest):
        dq_ref, dk_ref, dv_ref, qs_ref, dos_ref, dqt_ref = rest[len(extra):]
        qi = pl.program_id(1)
        ki = pl.program_id(2)

        @pl.when(ki == 0)
        def _():
            for g in range(GROUP):
                qs_ref[g * tq:(g + 1) * tq, :] = q_ref[:, g * HEAD_DIM:(g + 1) * HEAD_DIM]
                dos_ref[g * tq:(g + 1) * tq, :] = do_ref[:, g * HEAD_DIM:(g + 1) * HEAD_DIM]
            dqt_ref[...] = jnp.zeros(dqt_ref.shape, F32)

        kk = k_ref[...]
        vv = v_ref[...]

        def lanes(p):
            return slice(p * tq, (p + 1) * tq)

        def products(p):
            st = lax.dot_general(kk, qs_ref[lanes(p), :], nt, preferred_element_type=F32)
            dpt = lax.dot_general(vv, dos_ref[lanes(p), :], nt, preferred_element_type=F32)
            return st, dpt

        dk_c = jnp.zeros((tk, HEAD_DIM), F32)
        dv_c = jnp.zeros((tk, HEAD_DIM), F32)
        ahead = products(0)
        for p in range(GROUP):
            st, dpt = ahead
            if p + 1 < GROUP:
                ahead = products(p + 1)
            pt = jnp.exp2(st * (ATTN_SCALE * LOG2E) - lse_ref[0, p:p + 1, :])
            dst = ((pt * (dpt - dl_ref[0, p:p + 1, :])) * ATTN_SCALE).astype(BF16)
            dv_c = dv_c + jnp.dot(pt.astype(BF16), dos_ref[lanes(p), :], preferred_element_type=F32)
            dk_c = dk_c + jnp.dot(dst, qs_ref[lanes(p), :], preferred_element_type=F32)
            dqt_ref[:, lanes(p)] += lax.dot_general(kk, dst, (((0,), (0,)), ((), ())), preferred_element_type=F32)
        rows = pl.ds(pl.multiple_of(ki * tk, tk), tk)

        @pl.when(qi == 0)
        def _():
            dk_ref[rows, :] = dk_c
            dv_ref[rows, :] = dv_c

        @pl.when(qi > 0)
        def _():
            dk_ref[rows, :] += dk_c
            dv_ref[rows, :] += dv_c

        @pl.when(ki == nk - 1)
        def _():
            dqv = jnp.transpose(dqt_ref[...])
            for g in range(GROUP):
                dq_ref[:, g * HEAD_DIM:(g + 1) * HEAD_DIM] = dqv[g * tq:(g + 1) * tq, :]

    return pl.pallas_call(
        body, name=name, grid=(N_KV_HEADS, T // tq, nk),
        in_specs=[pl.BlockSpec((tq, GW), lambda h, i, j: (i, h)),
                  pl.BlockSpec((tq, GW), lambda h, i, j: (i, h)),
                  pl.BlockSpec((tk, HEAD_DIM), lambda h, i, j: (j, h)),
                  pl.BlockSpec((tk, HEAD_DIM), lambda h, i, j: (j, h)),
                  pl.BlockSpec((1, GROUP, tq), lambda h, i, j: (h, 0, i)),
                  pl.BlockSpec((1, GROUP, tq), lambda h, i, j: (h, 0, i))] +
                 [pl.BlockSpec(t.shape, lambda h, i, j: (0, 0)) for t in extra],
        out_specs=[pl.BlockSpec((tq, GW), lambda h, i, j: (i, h)),
                   pl.BlockSpec((T, HEAD_DIM), lambda h, i, j: (0, h)),
                   pl.BlockSpec((T, HEAD_DIM), lambda h, i, j: (0, h))],
        out_shape=[jax.ShapeDtypeStruct((T, N_Q_HEADS * HEAD_DIM), F32),
                   jax.ShapeDtypeStruct((T, N_KV_HEADS * HEAD_DIM), F32),
                   jax.ShapeDtypeStruct((T, N_KV_HEADS * HEAD_DIM), F32)],
        scratch_shapes=[pltpu.VMEM((GROUP * tq, HEAD_DIM), BF16), pltpu.VMEM((GROUP * tq, HEAD_DIM), BF16),
                        pltpu.VMEM((HEAD_DIM, GROUP * tq), F32)],
        compiler_params=pltpu.CompilerParams(dimension_semantics=("arbitrary", "arbitrary", "arbitrary"),
                                             vmem_limit_bytes=ATTN_VMEM_LIMIT),
    )(q, do, k, v, lse, delta, *extra)


def _gate_specs(D):
    w = D // 2
    first = (3 * D + (N_Q_HEADS + 2 * N_KV_HEADS) * HEAD_DIM) // w
    return [pl.BlockSpec((ROW, w), lambda i, c=first + j: (i, c)) for j in range(4)]


def _merge_fwd(o, P, conv_w, wbc, wba, wo, D, name):
    T = o.shape[0]
    nt = T // ROW
    w = D // 2
    conv_specs, conv_args = _conv_fwd_operands(P, conv_w, D)
    nc = len(conv_args)

    def body(*refs):
        o_ref, g0, g1, g2, g3, wbc_ref, wba_ref, wo_ref, yc_ref, a1_ref, a2_ref, z_ref, mo_ref = refs[nc:]
        yc_ref[...] = _conv_tile_fwd(refs[:nc], nt)
        a1 = jnp.dot(yc_ref[...], wbc_ref[...], preferred_element_type=F32)
        a2 = jnp.dot(o_ref[...].astype(BF16), wba_ref[...], preferred_element_type=F32)
        a1_ref[...] = a1.astype(BF16)
        a2_ref[...] = a2.astype(BF16)
        for j, (gc, ga) in enumerate(((g0, g2), (g1, g3))):
            sl = slice(j * w, (j + 1) * w)
            z = jax.nn.sigmoid(_f32(gc)) * a1[:, sl] + jax.nn.sigmoid(_f32(ga)) * a2[:, sl]
            z_ref[:, sl] = z.astype(BF16)
        mo_ref[...] = jnp.dot(z_ref[...], wo_ref[...], preferred_element_type=F32)

    return pl.pallas_call(
        body, name=name, grid=(T // ROW,),
        in_specs=conv_specs + [_row_spec(D)] + _gate_specs(D) + [_resident()] * 3,
        out_specs=[_row_spec(D)] * 5,
        out_shape=[jax.ShapeDtypeStruct((T, D), BF16), jax.ShapeDtypeStruct((T, D), BF16),
                   jax.ShapeDtypeStruct((T, D), BF16), jax.ShapeDtypeStruct((T, D), BF16),
                   jax.ShapeDtypeStruct((T, D), F32)],
        compiler_params=_params(("parallel",)),
    )(*conv_args, o, P, P, P, P, wbc, wba, wo)


def _merge_bwd(dmo, a1, a2, o, P, wbc, wba, wo, D, name):
    T = a1.shape[0]
    w = D // 2

    def body(dmo_ref, a1_ref, a2_ref, o_ref, g0, g1, g2, g3, wbc_ref, wba_ref, wo_ref,
             d1_ref, d2_ref, dg_ref, dyc_ref, dob_ref, dl_ref):
        dz = lax.dot_general(dmo_ref[...], wo_ref[...], _NT, preferred_element_type=F32)
        for j, (gc, ga) in enumerate(((g0, g2), (g1, g3))):
            sl = slice(j * w, (j + 1) * w)
            dzs = dz[:, sl]
            sc = jax.nn.sigmoid(_f32(gc))
            sa = jax.nn.sigmoid(_f32(ga))
            d1_ref[:, sl] = (dzs * sc).astype(BF16)
            d2_ref[:, sl] = (dzs * sa).astype(BF16)
            dg_ref[:, j * w:(j + 1) * w] = (dzs * a1_ref[:, sl].astype(F32) * (sc * (1.0 - sc))).astype(BF16)
            dg_ref[:, D + j * w:D + (j + 1) * w] = (dzs * a2_ref[:, sl].astype(F32) * (sa * (1.0 - sa))).astype(BF16)
        dyc_ref[...] = lax.dot_general(d1_ref[...], wbc_ref[...], _NT, preferred_element_type=F32)
        dov = lax.dot_general(d2_ref[...], wba_ref[...], _NT, preferred_element_type=F32)
        dob_ref[...] = dov.astype(BF16)
        prod = dov * o_ref[...]
        for h in range(N_Q_HEADS):
            d = jnp.sum(prod[:, h * HEAD_DIM:(h + 1) * HEAD_DIM], axis=1, keepdims=True)
            dl_ref[h // GROUP, (h % GROUP):(h % GROUP) + 1, :] = _to_row(d, ROW)

    return pl.pallas_call(
        body, name=name, grid=(T // ROW,),
        in_specs=[_row_spec(D)] * 4 + _gate_specs(D) + [_resident()] * 3,
        out_specs=[_row_spec(D), _row_spec(D), _row_spec(2 * D), _row_spec(D), _row_spec(D),
                   pl.BlockSpec((N_KV_HEADS, GROUP, ROW), lambda i: (0, 0, i))],
        out_shape=[jax.ShapeDtypeStruct((T, D), BF16), jax.ShapeDtypeStruct((T, D), BF16),
                   jax.ShapeDtypeStruct((T, 2 * D), BF16), jax.ShapeDtypeStruct((T, D), F32),
                   jax.ShapeDtypeStruct((T, D), BF16), jax.ShapeDtypeStruct((N_KV_HEADS, GROUP, T), F32)],
        compiler_params=_params(("parallel",)),
    )(dmo, a1, a2, o, P, P, P, P, wbc, wba, wo)


def _adamw_math(w, g, m, v):
    m = ADAM_B1 * m + (1.0 - ADAM_B1) * g
    v = ADAM_B2 * v + (1.0 - ADAM_B2) * (g * g)
    m_hat = m / (1.0 - ADAM_B1 ** ADAM_STEP)
    v_hat = v / (1.0 - ADAM_B2 ** ADAM_STEP)
    delta = -ADAM_LR * (m_hat / (jnp.sqrt(v_hat) + ADAM_EPS) + ADAM_WD * w)
    return delta, m, v


def _adamw(w, g, m, v, name):
    R, C = w.shape
    tr = _pick(R, tuple(t for t in (256, 128, 64, 32, 16, 8) if t * C * 4 <= ADAMW_BLOCK_BYTES))

    def body(w_ref, g_ref, m_ref, v_ref, d_ref, mo_ref, vo_ref):
        d, mn, vn = _adamw_math(w_ref[...], g_ref[...], m_ref[...], v_ref[...])
        d_ref[...] = d
        mo_ref[...] = mn
        vo_ref[...] = vn

    spec = pl.BlockSpec((tr, C), lambda i: (i, 0))
    return pl.pallas_call(
        body, name=name, grid=(R // tr,),
        in_specs=[spec] * 4, out_specs=[spec] * 3,
        out_shape=[jax.ShapeDtypeStruct((R, C), F32)] * 3,
        compiler_params=_params(("parallel",)),
    )(w, g, m, v)


def _norm_mix_in_fwd(xprev, branch, mods, g, gate, shift_idx, scale_idx, w_t, gq, gk, cos_t, sin_t, name):
    x_specs, x_args, (T, D) = _rows_operand(xprev)
    N = w_t.shape[0]
    QW = N_Q_HEADS * HEAD_DIM
    KW = N_KV_HEADS * HEAD_DIM
    q0, k0, v0 = 3 * D, 3 * D + QW, 3 * D + QW + KW
    edges = [0, D, 2 * D, q0, k0, v0 + KW] + list(range(v0 + KW + D, N + 1, D))
    assert edges[-1] == N

    def body(*refs):
        f_ref, m_ref, g_ref, w_ref, gq_ref, gk_ref, c_ref, s_ref = refs[len(x_args):len(x_args) + 8]
        xo_ref, h_ref, p_ref, qo_ref, ko_ref, vo_ref = refs[len(x_args) + 8:]
        m = m_ref[0]
        gate_idx, fac = gate
        x = _rows_tile(refs[:len(x_args)]) + (fac * m[gate_idx:gate_idx + 1, :]) * f_ref[...]
        xo_ref[...] = x
        hv = _norm_tile_fwd(x, m, g_ref[...], shift_idx, scale_idx)
        h_ref[...] = hv
        c = c_ref[...]
        s = s_ref[...]

        def head(xh, gain):
            inv = lax.rsqrt(jnp.mean(xh * xh, axis=-1, keepdims=True) + EPS)
            y = (xh * inv) * gain
            return y * c + _swap_halves(y) * s

        for lo, hi in zip(edges[:-1], edges[1:]):
            pb = lax.dot_general(hv, w_ref[lo:hi, :], _NT, preferred_element_type=F32).astype(BF16)
            p_ref[:, lo:hi] = pb
            if lo == q0:
                for h in range(N_Q_HEADS):
                    sl = slice(h * HEAD_DIM, (h + 1) * HEAD_DIM)
                    qo_ref[:, sl] = head(pb[:, sl].astype(F32), gq_ref[...]).astype(BF16)
            elif lo == k0:
                for h in range(N_KV_HEADS):
                    sl = slice(h * HEAD_DIM, (h + 1) * HEAD_DIM)
                    ko_ref[:, sl] = head(pb[:, sl].astype(F32), gk_ref[...]).astype(BF16)
                vo_ref[...] = pb[:, KW:2 * KW]

    return pl.pallas_call(
        body, name=name, grid=(T // ROW,),
        in_specs=x_specs + [_row_spec(D), _mods_spec(D), _vec_spec(1, D), _resident(),
                            _vec_spec(1, HEAD_DIM), _vec_spec(1, HEAD_DIM), _row_spec(HEAD_DIM), _row_spec(HEAD_DIM)],
        out_specs=[_row_spec(D), _row_spec(D), _row_spec(N), _row_spec(QW), _row_spec(KW), _row_spec(KW)],
        out_shape=[jax.ShapeDtypeStruct((T, D), F32), jax.ShapeDtypeStruct((T, D), BF16),
                   jax.ShapeDtypeStruct((T, N), BF16), jax.ShapeDtypeStruct((T, QW), BF16),
                   jax.ShapeDtypeStruct((T, KW), BF16), jax.ShapeDtypeStruct((T, KW), BF16)],
        compiler_params=_params(("parallel",)),
    )(*x_args, branch, mods, g, w_t, gq, gk, cos_t, sin_t)


def _mix_in_norm_bwd(dyc, dgt, P, conv_w, dq, dk, dv, gq, gk, cos_t, sin_t, w_t, x, dres, mods, g, shift_idx,
                     scale_idx, gate, branch, name):
    T, D = x.shape
    nt = T // ROW
    QW = N_Q_HEADS * HEAD_DIM
    KW = N_KV_HEADS * HEAD_DIM
    q0, g0 = 3 * D, 3 * D + QW + 2 * KW
    assert g0 + dgt.shape[1] == w_t.shape[0]
    conv_specs, conv_args = _conv_bwd_operands(P, dyc, conv_w, D)
    nc = len(conv_args)

    def body(*refs):
        (dg_ref, q_ref, k_ref, dq_ref, dk_ref, dv_ref, gq_ref, gk_ref, c_ref, s_ref,
         w_ref, x_ref, dr_ref, b_ref, m_ref, g_ref,
         dx_ref, db_ref, acc_ref, dc_ref, cacc_ref, o_ref, qacc_ref) = refs[nc:]
        _acc_init(acc_ref)
        _acc_init(cacc_ref)
        _acc_init(qacc_ref)
        _conv_tile_bwd(refs[:nc], dc_ref, cacc_ref, D, nt)
        dh = jnp.dot(dc_ref[...], w_ref[0:q0, :], preferred_element_type=F32)
        c = c_ref[...]
        s = s_ref[...]

        def head(xh, d, gain):
            dyv = d * c + _swap_halves(d * s)
            inv = lax.rsqrt(jnp.mean(xh * xh, axis=-1, keepdims=True) + EPS)
            xn = xh * inv
            dxn = dyv * gain
            dxh = inv * (dxn - xn * jnp.mean(dxn * xn, axis=-1, keepdims=True))
            return dxh, jnp.sum(dyv * xn, axis=0, keepdims=True)

        dgq = jnp.zeros((1, HEAD_DIM), F32)
        for h in range(N_Q_HEADS):
            sl = slice(h * HEAD_DIM, (h + 1) * HEAD_DIM)
            dxh, dgh = head(q_ref[:, sl].astype(F32), dq_ref[:, sl], gq_ref[...])
            o_ref[:, sl] = dxh.astype(BF16)
            dgq = dgq + dgh
        dh = dh + jnp.dot(dg_ref[...], w_ref[g0:, :], preferred_element_type=F32)
        dgk = jnp.zeros((1, HEAD_DIM), F32)
        for h in range(N_KV_HEADS):
            sl = slice(h * HEAD_DIM, (h + 1) * HEAD_DIM)
            dxh, dgh = head(k_ref[:, sl].astype(F32), dk_ref[:, sl], gk_ref[...])
            o_ref[:, QW + h * HEAD_DIM:QW + (h + 1) * HEAD_DIM] = dxh.astype(BF16)
            dgk = dgk + dgh
        o_ref[:, QW + KW:QW + 2 * KW] = dv_ref[...].astype(BF16)
        qacc_ref[0, 0:1, 0:HEAD_DIM] += dgq
        qacc_ref[0, 1:2, 0:HEAD_DIM] += dgk
        dh = dh + jnp.dot(o_ref[...], w_ref[q0:g0, :], preferred_element_type=F32)
        m = m_ref[0]
        dx = _norm_tile_bwd(x_ref[...], dh, dr_ref[...], m, g_ref[...], shift_idx, scale_idx, acc_ref)
        dx_ref[...] = dx
        db_ref[...] = _gate_tile_bwd(dx, b_ref[...], m, gate, acc_ref)

    return pl.pallas_call(
        body, name=name, grid=(T // ROW,),
        in_specs=conv_specs +
                 [_row_spec(dgt.shape[1]), _row_spec(QW, q0 // QW), _row_spec(KW, (q0 + QW) // KW),
                  _row_spec(QW), _row_spec(KW), _row_spec(KW), _vec_spec(1, HEAD_DIM), _vec_spec(1, HEAD_DIM),
                  _row_spec(HEAD_DIM), _row_spec(HEAD_DIM),
                  _resident(), _row_spec(D), _row_spec(D), _row_spec(D), _mods_spec(D), _vec_spec(1, D)],
        out_specs=[_row_spec(D), _row_spec(D), _acc_spec(D), _row_spec(q0), _acc_spec(D),
                   _row_spec(QW + 2 * KW), _acc_spec(D)],
        out_shape=[jax.ShapeDtypeStruct((T, D), F32), jax.ShapeDtypeStruct((T, D), BF16),
                   jax.ShapeDtypeStruct((2, ACC_ROWS, D), F32), jax.ShapeDtypeStruct((T, q0), BF16),
                   jax.ShapeDtypeStruct((2, ACC_ROWS, D), F32), jax.ShapeDtypeStruct((T, QW + 2 * KW), BF16),
                   jax.ShapeDtypeStruct((2, ACC_ROWS, D), F32)],
        compiler_params=_params(("arbitrary",)),
    )(*conv_args, dgt, P, P, dq, dk, dv, gq, gk, cos_t, sin_t, w_t, x, dres, branch, mods, g)


def _adamw_transposed(w, gt, m, v, name):
    R, C = w.shape
    tc = LANES

    def body(w_ref, g_ref, m_ref, v_ref, go_ref, d_ref, mo_ref, vo_ref):
        g = jnp.transpose(g_ref[...])
        d, mn, vn = _adamw_math(w_ref[...], g, m_ref[...], v_ref[...])
        go_ref[...] = g
        d_ref[...] = d
        mo_ref[...] = mn
        vo_ref[...] = vn

    spec = pl.BlockSpec((R, tc), lambda j: (0, j))
    return pl.pallas_call(
        body, name=name, grid=(C // tc,),
        in_specs=[spec, pl.BlockSpec((tc, R), lambda j: (j, 0)), spec, spec], out_specs=[spec] * 4,
        out_shape=[jax.ShapeDtypeStruct((R, C), F32)] * 4,
        compiler_params=_params(("parallel",)),
    )(w, gt, m, v)


class _NoExchange:
    def __init__(self, rest):
        self.rest = rest

    def rest_weights(self, after):
        return self.rest

    def reduce_early(self, grads, tag):
        return None


def _local_step(xcat, target, mods, norm_g, final_g, gq, gk, conv_w, ffn1_w, hooks, rope):
    T, D = _rows_operand(xcat)[2]
    w1i, w1o = ffn1_w
    g1, g2, g3 = norm_g
    cos_t, sin_t = rope

    def after(value, token, name):
        return value if token is None else _after(value, token, name)

    _, h1, u1, s1, f1 = _norm_ffn_fwd(xcat, None, mods, g1, None, 0, 1, w1i, w1o, "f_ffn1")
    wi, wbc, wba, wo, w2i, w2o = hooks.rest_weights(f1)
    x1, h2, P, qn, kn, vb = _norm_mix_in_fwd(xcat, f1, mods, g2, (2, 0.5), 3, 4, wi, gq, gk, cos_t, sin_t, "f_mix_in")
    o, lse = _flash_fwd(qn, kn, vb, "f_attn")
    yc, a1, a2, z, mo = _merge_fwd(o, P, conv_w, wbc, wba, wo, D, "f_merge")
    x2, h3, u2, s2, dx3, df2, acc_head = _norm_ffn_fwd(x1, mo, mods, g3, (5, 1.0), 6, 7, w2i, w2o, "f_ffn2",
                                                       head=(final_g, target))

    du2, dx2, dmo, acc_n3 = _ffn_norm_bwd(df2, u2, w2i, w2o, x2, dx3, mods, g3, 6, 7, (5, 1.0), mo, "b_ffn2")
    g_w2o = _grad_matmul(s2, df2, "b_ffn2_out_dw")
    g_w2i = _grad_matmul(du2, h3, "b_ffn2_in_dw")

    g_wo = _grad_matmul(z, dmo, "b_mix_out_dw")
    da1, da2, dgt, dyc, dob, delta = _merge_bwd(dmo, a1, a2, o, P, wbc, wba, wo, D, "b_merge")
    g_wbc = _grad_matmul(yc, da1, "b_branch_conv_dw")
    g_wba = _grad_matmul(o, da2, "b_branch_attn_dw")
    token_a = hooks.reduce_early([g_wbc, g_wba, g_wo, g_w2i, g_w2o], "a")
    dq, dk, dv = _flash_bwd(qn, kn, vb, dob, lse, delta, "b_attn", token=token_a)
    dx1, df1, acc_n2, dconv, acc_conv, dqkv, acc_qk = _mix_in_norm_bwd(
        dyc, dgt, P, conv_w, dq, dk, dv, gq, gk, cos_t, sin_t, wi, x1, dx2, mods, g2, 3, 4, (2, 0.5), f1, "b_mix_in")
    d_parts = (dconv, dqkv, dgt)
    g_wi = jnp.concatenate([_grad_matmul(dp, h2, f"b_mix_in_dw_{i}") for i, dp in enumerate(d_parts)], axis=0)
    g1_b = after(g1, hooks.reduce_early([g_wi], "b"), "after_rs_b")

    du1, grad_x, _, acc_n1 = _ffn_norm_bwd(df1, u1, w1i, w1o, xcat, dx1, mods, g1_b, 0, 1, None, None, "b_ffn1",
                                           skip_first_tile=True)
    g_w1o = _grad_matmul(s1, df1, "b_ffn1_out_dw")
    g_w1i = _grad_matmul(du1, h1, "b_ffn1_in_dw", token=hooks.reduce_early([g_w1o], "c"))

    grads = (g_w1i, g_w1o, g_wi, g_wbc, g_wba, g_wo, g_w2i, g_w2o)
    accs = (acc_head, acc_n3, acc_n2, acc_n1, acc_conv, acc_qk)
    return grad_x, grads, accs


def _place():
    return lax.axis_index("x"), lax.axis_index("y"), lax.axis_index("c")


def _other_chips(x, y):
    return [(1 - x, y), (x, 1 - y), (1 - x, 1 - y)]


def _allgather8(v, name):
    R, N = v.shape

    def body(v_ref, out_ref, send_sems, recv_sems, local_sem):
        x, y, c = _place()
        me, sibling = (x, y, c), (x, y, 1 - c)
        chips = _other_chips(x, y)

        def blk(px, py, pc):
            return out_ref.at[4 * px + 2 * py + pc]

        def copy(k, block, to, src=None):
            return pltpu.make_async_remote_copy(
                src_ref=blk(*block) if src is None else src, dst_ref=blk(*block),
                send_sem=send_sems.at[k], recv_sem=recv_sems.at[k], device_id=to, device_id_type=MESH)

        mine = pltpu.make_async_copy(v_ref, blk(*me), local_sem)
        mine.start()
        first = [copy(0, me, sibling, src=v_ref)]
        first += [copy(1 + j, me, (*chip, c), src=v_ref) for j, chip in enumerate(chips)]
        for cp in first:
            cp.start()
        passed = [copy(4 + j, (*chip, c), sibling) for j, chip in enumerate(chips)]
        for j, chip in enumerate(chips):
            copy(1 + j, (*chip, c), me).wait_recv()
            passed[j].start()
        copy(0, sibling, me).wait_recv()
        for j, chip in enumerate(chips):
            copy(4 + j, (*chip, 1 - c), me).wait_recv()
        for cp in first + passed:
            cp.wait_send()
        mine.wait()

    return pl.pallas_call(
        body, name=name,
        out_shape=jax.ShapeDtypeStruct((N_DEV, R, N), v.dtype),
        in_specs=[pl.BlockSpec(memory_space=pltpu.VMEM)],
        out_specs=pl.BlockSpec(memory_space=pltpu.VMEM),
        scratch_shapes=[pltpu.SemaphoreType.DMA((7,)), pltpu.SemaphoreType.DMA((7,)), pltpu.SemaphoreType.DMA],
        compiler_params=pltpu.CompilerParams(vmem_limit_bytes=VMEM_LIMIT),
    )(v)


def _any_specs(n):
    return [pl.BlockSpec(memory_space=pl.ANY)] * n


def _pair_exchange(grads, name):
    n = len(grads)

    def body(*refs):
        g, land = refs[:n], refs[n:2 * n]
        send_sems, recv_sems = refs[2 * n:]
        x, y, c = _place()
        sibling = (x, y, 1 - c)
        copies = []
        for t in range(n):
            half = grads[t].shape[0] // (2 * N_CHIPS)
            for s in range(N_CHIPS):
                cp = pltpu.make_async_remote_copy(
                    src_ref=g[t].at[pl.ds((2 * s + 1 - c) * half, half), :], dst_ref=land[t].at[s],
                    send_sem=send_sems.at[N_CHIPS * t + s], recv_sem=recv_sems.at[N_CHIPS * t + s],
                    device_id=sibling, device_id_type=MESH)
                cp.start()
                copies.append(cp)
        for cp in copies:
            cp.wait_recv()
        for cp in copies:
            cp.wait_send()

    return pl.pallas_call(
        body, name=name,
        out_shape=[jax.ShapeDtypeStruct((N_CHIPS, a.shape[0] // (2 * N_CHIPS), a.shape[1]), a.dtype) for a in grads],
        in_specs=_any_specs(n), out_specs=_any_specs(n),
        scratch_shapes=[pltpu.SemaphoreType.DMA((N_CHIPS * n,)), pltpu.SemaphoreType.DMA((N_CHIPS * n,))],
    )(*grads)


def _place_shard(w2, idx, transpose, name, token):
    if transpose:
        D, rs = w2.shape
        tr = LANES
        in_spec = pl.BlockSpec((D, tr), lambda i, idx: (0, i))
    else:
        rs, D = w2.shape
        tr = _pick(rs, (352, 256, 128, 64, 32, 16))
        in_spec = pl.BlockSpec((tr, D), lambda i, idx: (i, 0))
    steps = rs // tr

    def body(idx_ref, w_ref, t_ref, o_ref):
        v = w_ref[...]
        o_ref[...] = (jnp.transpose(v) if transpose else v).astype(BF16)

    return pl.pallas_call(
        body, name=name,
        grid_spec=pltpu.PrefetchScalarGridSpec(
            num_scalar_prefetch=1, grid=(steps,),
            in_specs=[in_spec, pl.BlockSpec(token.shape, lambda i, idx: (0, 0))],
            out_specs=pl.BlockSpec((tr, D), lambda i, idx: (idx[1] * steps + i, 0))),
        out_shape=jax.ShapeDtypeStruct((N_CHIPS * rs, D), BF16),
        compiler_params=_params(("arbitrary",)),
    )(idx, w2, token)


def _pair_sum(g, landed, idx, name, token=None):
    _, half, D = landed.shape
    g4 = g.reshape(N_CHIPS, 2, half, D)
    tr = _pick(half, (416, 352, 128))
    extra = [] if token is None else [token]

    def body(idx_ref, g_ref, l_ref, *rest):
        rest[-1][...] = (g_ref[0].astype(F32) + l_ref[...].astype(F32)).astype(BF16)

    return pl.pallas_call(
        body, name=name,
        grid_spec=pltpu.PrefetchScalarGridSpec(
            num_scalar_prefetch=1, grid=(N_CHIPS, half // tr),
            in_specs=[pl.BlockSpec((1, 1, tr, D), lambda s, i, idx: (idx[1 + s], idx[0], i, 0)),
                      pl.BlockSpec((1, tr, D), lambda s, i, idx: (idx[1 + s], i, 0))] +
                     [pl.BlockSpec(t.shape, lambda s, i, idx: (0, 0)) for t in extra],
            out_specs=pl.BlockSpec((1, tr, D), lambda s, i, idx: (s, i, 0))),
        out_shape=jax.ShapeDtypeStruct((N_CHIPS, half, D), BF16),
        compiler_params=_params(("arbitrary", "arbitrary")),
    )(idx, g4, landed, *extra)


_HBM = pl.BlockSpec(memory_space=pltpu.HBM)
_SEM = pl.BlockSpec(memory_space=pltpu.SEMAPHORE)
_EFFECT = pltpu.SideEffectType.DATAFLOW_SIDE_EFFECTING


def _in_hbm(a):
    return pltpu.with_memory_space_constraint(a, pltpu.HBM)


def _split_copies(n, per, make):
    def start(nbuf, name, bufs, after=None):
        extra = [] if after is None else [after]

        def body(*refs):
            ins = refs[:nbuf]
            send_sems, recv_sems = refs[nbuf + len(extra)], refs[nbuf + len(extra) + 1]
            token = refs[-1]
            for t in range(n):
                for j in range(per):
                    make(ins, t, j, send_sems.at[per * t + j], recv_sems.at[per * t + j]).start()
            token[...] = jnp.zeros(token.shape, token.dtype)

        out = pl.pallas_call(
            body, name=name,
            out_shape=(pltpu.SemaphoreType.DMA((per * n,)), pltpu.SemaphoreType.DMA((per * n,)),
                       *[pltpu.HBM(b.shape, b.dtype) for b in bufs], jax.ShapeDtypeStruct((8, 128), F32)),
            in_specs=[_HBM] * nbuf + [pl.BlockSpec(memory_space=pl.ANY)] * len(extra),
            out_specs=(_SEM, _SEM, *[_HBM] * nbuf, pl.BlockSpec(memory_space=pltpu.VMEM)),
            input_output_aliases={i: 2 + i for i in range(nbuf)},
            compiler_params=pltpu.CompilerParams(has_side_effects=_EFFECT),
        )(*[_in_hbm(b) for b in bufs], *extra)
        return out[0], out[1], list(out[2:2 + nbuf]), out[-1]

    def wait(nbuf, name, send_sems, recv_sems, bufs, after):
        def body(*refs):
            ins = refs[:nbuf]
            ss, rs = refs[nbuf], refs[nbuf + 1]
            for t in range(n):
                for j in range(per):
                    cp = make(ins, t, j, ss.at[per * t + j], rs.at[per * t + j])
                    cp.wait_send()
                    cp.wait_recv()

        return pl.pallas_call(
            body, name=name,
            out_shape=[pltpu.HBM(b.shape, b.dtype) for b in bufs],
            in_specs=[_HBM] * nbuf + [_SEM, _SEM, pl.BlockSpec(memory_space=pl.ANY)],
            out_specs=[_HBM] * nbuf,
            input_output_aliases={i: i for i in range(nbuf)},
            compiler_params=pltpu.CompilerParams(has_side_effects=_EFFECT),
        )(*bufs, send_sems, recv_sems, after)

    return start, wait


RS_PEERS = N_DEV - 1


def _reduce_exchange_split(grads):
    n = len(grads)

    def make(bufs, t, j, send_sem, recv_sem):
        x, y, c = _place()
        chip = (x, y) if j == 6 else _other_chips(x, y)[j % 3]
        core = c if j < 3 else 1 - c
        half = grads[t].shape[0] // (2 * N_CHIPS)
        piece = bufs[t].at[pl.ds((2 * (2 * chip[0] + chip[1]) + core) * half, half), :]
        return pltpu.make_async_remote_copy(src_ref=piece, dst_ref=bufs[n + t].at[j], send_sem=send_sem,
                                            recv_sem=recv_sem, device_id=(*chip, core), device_id_type=MESH)

    return _split_copies(n, RS_PEERS, make)


def _reduce_sum(g, landed, idx, name):
    _, half, D = landed.shape
    g4 = g.reshape(N_CHIPS, 2, half, D)
    tr = _pick(half, (416, 352, 128))
    steps = half // tr

    def body(idx_ref, g_ref, l_ref, o_ref):
        acc = g_ref[0, 0].astype(F32)
        for j in range(RS_PEERS):
            acc = acc + l_ref[j].astype(F32)
        o_ref[...] = acc

    return pl.pallas_call(
        body, name=name,
        grid_spec=pltpu.PrefetchScalarGridSpec(
            num_scalar_prefetch=1, grid=(steps,),
            in_specs=[pl.BlockSpec((1, 1, tr, D), lambda i, idx: (idx[1], idx[0], i, 0)),
                      pl.BlockSpec((RS_PEERS, tr, D), lambda i, idx: (0, i, 0))],
            out_specs=pl.BlockSpec((tr, D), lambda i, idx: (idx[0] * steps + i, 0))),
        out_shape=jax.ShapeDtypeStruct((2 * half, D), F32),
        compiler_params=_params(("arbitrary",)),
    )(idx, g4, landed)


def _chip_exchange_split(n):
    def make(bufs, t, j, send_sem, recv_sem):
        x, y, c = _place()
        chip = _other_chips(x, y)[j]
        return pltpu.make_async_remote_copy(src_ref=bufs[t].at[1 + j], dst_ref=bufs[n + t].at[j], send_sem=send_sem,
                                            recv_sem=recv_sem, device_id=(*chip, c), device_id_type=MESH)

    return _split_copies(n, 3, make)


def _weights_gather_split(fulls):
    def make(bufs, t, j, send_sem, recv_sem):
        x, y, c = _place()
        chip = _other_chips(x, y)[j]
        rs = fulls[t].shape[0] // N_CHIPS
        rows = bufs[t].at[pl.ds((2 * x + y) * rs + c * (rs // 2), rs // 2), :]
        return pltpu.make_async_remote_copy(src_ref=rows, dst_ref=rows, send_sem=send_sem, recv_sem=recv_sem,
                                            device_id=(*chip, c), device_id_type=MESH)

    return _split_copies(len(fulls), 3, make)


def _weights_pass_on(fulls, name):
    n = len(fulls)

    def body(*refs):
        full = refs[n:2 * n]
        send_sems, recv_sems = refs[2 * n:]
        x, y, c = _place()
        chips = _other_chips(x, y)

        def copy(t, j, h):
            rs = fulls[t].shape[0] // N_CHIPS
            px, py = chips[j]
            rows = full[t].at[pl.ds((2 * px + py) * rs + h * (rs // 2), rs // 2), :]
            return pltpu.make_async_remote_copy(src_ref=rows, dst_ref=rows, send_sem=send_sems.at[3 * t + j],
                                                recv_sem=recv_sems.at[3 * t + j], device_id=(x, y, 1 - c),
                                                device_id_type=MESH)

        for t in range(n):
            for j in range(3):
                copy(t, j, c).start()
        for t in range(n):
            for j in range(3):
                copy(t, j, 1 - c).wait_recv()
        for t in range(n):
            for j in range(3):
                copy(t, j, c).wait_send()

    return pl.pallas_call(
        body, name=name,
        out_shape=[jax.ShapeDtypeStruct(f.shape, f.dtype) for f in fulls],
        in_specs=_any_specs(n), out_specs=_any_specs(n),
        input_output_aliases={t: t for t in range(n)},
        scratch_shapes=[pltpu.SemaphoreType.DMA((3 * n,)), pltpu.SemaphoreType.DMA((3 * n,))],
    )(*fulls)


def _after(value, token, name):
    def body(v_ref, t_ref, o_ref):
        o_ref[...] = v_ref[...]

    return pl.pallas_call(
        body, name=name, out_shape=jax.ShapeDtypeStruct(value.shape, value.dtype),
        in_specs=_whole(2), out_specs=pl.BlockSpec(memory_space=pltpu.VMEM),
    )(value, token)


def _chip_sum(ps, landed, idx, name):
    _, half, D = ps.shape
    tr = _pick(half, (416, 352, 128))
    steps = half // tr

    def body(idx_ref, p_ref, l_ref, o_ref):
        acc = p_ref[0].astype(F32)
        for j in range(3):
            acc = acc + l_ref[j].astype(F32)
        o_ref[...] = acc

    return pl.pallas_call(
        body, name=name,
        grid_spec=pltpu.PrefetchScalarGridSpec(
            num_scalar_prefetch=1, grid=(steps,),
            in_specs=[pl.BlockSpec((1, tr, D), lambda i, idx: (0, i, 0)),
                      pl.BlockSpec((3, tr, D), lambda i, idx: (0, i, 0))],
            out_specs=pl.BlockSpec((tr, D), lambda i, idx: (idx[0] * steps + i, 0))),
        out_shape=jax.ShapeDtypeStruct((2 * half, D), F32),
        compiler_params=_params(("arbitrary",)),
    )(idx, ps, landed)


def _pair_swap(shards, name):
    n = len(shards)

    def body(*refs):
        full = refs[n:2 * n]
        send_sems, recv_sems = refs[2 * n:]
        x, y, c = _place()

        def half(t, h):
            rows = shards[t].shape[0] // 2
            return full[t].at[pl.ds(h * rows, rows), :]

        def copy(t, h):
            return pltpu.make_async_remote_copy(src_ref=half(t, h), dst_ref=half(t, h), send_sem=send_sems.at[t],
                                                recv_sem=recv_sems.at[t], device_id=(x, y, 1 - c),
                                                device_id_type=MESH)

        for t in range(n):
            copy(t, c).start()
        for t in range(n):
            copy(t, 1 - c).wait_recv()
        for t in range(n):
            copy(t, c).wait_send()

    return pl.pallas_call(
        body, name=name,
        out_shape=[jax.ShapeDtypeStruct(a.shape, a.dtype) for a in shards],
        in_specs=_any_specs(n), out_specs=_any_specs(n),
        input_output_aliases={t: t for t in range(n)},
        scratch_shapes=[pltpu.SemaphoreType.DMA((n,)), pltpu.SemaphoreType.DMA((n,))],
    )(*shards)


def _gather_begin(fulls, tag):
    start, wait = _weights_gather_split(fulls)
    send_sems, recv_sems, bufs, token = start(len(fulls), f"ag_{tag}_start", fulls)
    return (wait, send_sems, recv_sems, bufs), token


def _gather_end(state, after, tag):
    wait, send_sems, recv_sems, bufs = state
    landed = wait(len(bufs), f"ag_{tag}_wait", send_sems, recv_sems, bufs, after)
    return _weights_pass_on(landed, f"ag_{tag}_pass_on")


class _Exchanges:
    def __init__(self, fulls_rest, idx):
        self.idx = idx
        self._rest, self.token = _gather_begin(fulls_rest, "rest")
        self._early = []

    def rest_weights(self, after):
        return _gather_end(self._rest, after, "rest")

    def reduce_early(self, grads, tag, token=None):
        zones = [lax.empty((RS_PEERS, g.shape[0] // (2 * N_CHIPS), g.shape[1]), g.dtype) for g in grads]
        start, wait = _reduce_exchange_split(grads)
        send_sems, recv_sems, bufs, token = start(2 * len(grads), "rs_start_" + tag, list(grads) + zones, token)
        self._early.append((tag, wait, send_sems, recv_sems, bufs))
        return token

    def finish(self, tags, after):
        halves = []
        for tag, wait, send_sems, recv_sems, bufs in self._early:
            if tag in tags:
                n = len(bufs) // 2
                done = wait(len(bufs), "rs_wait_" + tag, send_sems, recv_sems, bufs, after)
                halves += [_reduce_sum(g, l, self.idx, f"rs_sum_{tag}{t}")
                           for t, (g, l) in enumerate(zip(done[:n], done[n:]))]
        return halves


N_MOD = 9
PACK_HEAD, PACK_N3, PACK_N2, PACK_N1, PACK_CONV, PACK_QK = 0, 16, 32, 48, 64, 80
MOD_SRC = ((PACK_N1, 0), (PACK_N1, 1), (PACK_N2, 3), (PACK_N2, 0), (PACK_N2, 1),
           (PACK_N3, 3), (PACK_N3, 0), (PACK_N3, 1), (PACK_HEAD, 2))
CTX_ROW = 8


def _silu(v):
    return v * jax.nn.sigmoid(v)


def _whole(n):
    return [pl.BlockSpec(memory_space=pltpu.VMEM)] * n


def _mod_rows(cin, w_sh, b_sh, name):
    def body(c_ref, w_ref, b_ref, o_ref):
        a = _silu(c_ref[...]).astype(BF16)
        o_ref[...] = jnp.dot(a, w_ref[...].astype(BF16), preferred_element_type=F32) + b_ref[...]

    return pl.pallas_call(
        body, name=name, out_shape=jax.ShapeDtypeStruct((cin.shape[0], w_sh.shape[1]), F32),
        in_specs=_whole(3), out_specs=pl.BlockSpec(memory_space=pltpu.VMEM),
        compiler_params=pltpu.CompilerParams(vmem_limit_bytes=VMEM_LIMIT),
    )(cin, w_sh, b_sh)


def _small_reduce(gathered, name):
    _, _, D = gathered.shape

    def body(g_ref, loss_ref, db_ref, gn_ref, cv_ref, qk_ref, dm_ref):
        tot = g_ref[0]
        for r in range(1, N_DEV):
            tot = tot + g_ref[r]

        def both(block, row):
            return tot[block + row:block + row + 1, :] + tot[block + 8 + row:block + 8 + row + 1, :]

        loss = jnp.sum(both(PACK_HEAD, 0), axis=1, keepdims=True)
        loss_ref[...] = jnp.broadcast_to(loss, loss_ref.shape)
        db_ref[...] = jnp.zeros(db_ref.shape, F32)
        dm_ref[...] = jnp.zeros(dm_ref.shape, F32)
        for j, (block, row) in enumerate(MOD_SRC):
            db_ref[j:j + 1, :] = both(block, row)
            dm_ref[CTX_ROW, j:j + 1, :] = tot[block + row:block + row + 1, :]
            for r in range(N_DEV):
                dm_ref[r, j:j + 1, :] = g_ref[r, block + 8 + row:block + 8 + row + 1, :]
        gn_ref[...] = jnp.zeros(gn_ref.shape, F32)
        gn_ref[0:1, :] = both(PACK_N1, 2)
        gn_ref[8:9, :] = both(PACK_N2, 2)
        gn_ref[16:17, :] = both(PACK_N3, 2)
        gn_ref[24:25, :] = both(PACK_HEAD, 1)
        cv_ref[...] = jnp.zeros(cv_ref.shape, F32)
        for r in range(3):
            cv_ref[r:r + 1, :] = both(PACK_CONV, r)
        qk_ref[...] = jnp.zeros(qk_ref.shape, F32)
        qk_ref[0:1, 0:HEAD_DIM] = both(PACK_QK, 0)[:, 0:HEAD_DIM]
        qk_ref[0:1, HEAD_DIM:2 * HEAD_DIM] = both(PACK_QK, 1)[:, 0:HEAD_DIM]

    return pl.pallas_call(
        body, name=name,
        out_shape=[jax.ShapeDtypeStruct((8, 128), F32), jax.ShapeDtypeStruct((16, D), F32),
                   jax.ShapeDtypeStruct((32, D), F32), jax.ShapeDtypeStruct((8, D), F32),
                   jax.ShapeDtypeStruct((8, D), F32), jax.ShapeDtypeStruct((16, 16, D), F32)],
        in_specs=_whole(1), out_specs=_whole(6),
        compiler_params=pltpu.CompilerParams(vmem_limit_bytes=VMEM_LIMIT),
    )(gathered)


def _wmod_grad(cin, dm_sh, w_sh, name):
    def body(c_ref, d_ref, w_ref, gw_ref, cp_ref):
        a = _silu(c_ref[...]).astype(BF16)
        d = d_ref[...].astype(BF16)
        gw_ref[...] = lax.dot_general(a, d, (((0,), (0,)), ((), ())), preferred_element_type=F32)
        cp_ref[...] = lax.dot_general(d, w_ref[...].astype(BF16), (((1,), (1,)), ((), ())),
                                      preferred_element_type=F32)

    return pl.pallas_call(
        body, name=name,
        out_shape=[jax.ShapeDtypeStruct(w_sh.shape, F32), jax.ShapeDtypeStruct(cin.shape, F32)],
        in_specs=_whole(3), out_specs=_whole(2),
        compiler_params=pltpu.CompilerParams(vmem_limit_bytes=VMEM_LIMIT),
    )(cin, dm_sh, w_sh)


def _cctx_grad(parts, c_ctx8, name):
    def body(p_ref, c_ref, o_ref):
        tot = p_ref[0] + p_ref[2] + p_ref[4] + p_ref[6]
        cv = c_ref[...]
        sig = jax.nn.sigmoid(cv)
        rows = lax.broadcasted_iota(jnp.int32, tot.shape, 0)
        o_ref[...] = jnp.where(rows == 0, tot * (sig * (1.0 + cv * (1.0 - sig))), 0.0)

    return pl.pallas_call(
        body, name=name, out_shape=jax.ShapeDtypeStruct(c_ctx8.shape, F32),
        in_specs=_whole(2), out_specs=pl.BlockSpec(memory_space=pltpu.VMEM),
    )(parts, c_ctx8)


def _pad_rows(a, rows):
    return jnp.pad(a, ((0, rows - a.shape[0]), (0, 0)))


def _pack_small(c_ctx, b_mod, n1, n2, n3, final_g, gq, gk, conv_sh, D):
    misc = jnp.concatenate([gq, gk, conv_sh.reshape(1, -1)], axis=1)
    return jnp.concatenate([_pad_rows(c_ctx[None], 8), _pad_rows(b_mod.reshape(N_MOD, D), 16), _pad_rows(n1, 8),
                            _pad_rows(n2, 8), _pad_rows(n3, 8), _pad_rows(final_g[None], 8), _pad_rows(misc, 8)], axis=0)


def _unpack_small(p, D, conv_shape):
    misc = p[56:57]
    return dict(c_ctx=p[0], b_mod=p[8:8 + N_MOD].reshape(1, N_MOD * D), norm1_g=p[24:25], norm2_g=p[32:33],
                norm3_g=p[40:41], final_g=p[48], q_norm_g=misc[:, 0:HEAD_DIM], k_norm_g=misc[:, HEAD_DIM:2 * HEAD_DIM],
                conv_w=misc[:, 2 * HEAD_DIM:].reshape(conv_shape))


WEIGHT_ORDER = ("c_ctx", "w_mod", "b_mod", "norm1_g", "norm2_g", "norm3_g", "ffn1_w_in", "ffn1_w_out", "w_in",
                "conv_w", "q_norm_g", "k_norm_g", "w_branch_conv", "w_branch_attn", "w_out", "ffn2_w_in",
                "ffn2_w_out", "final_g")
BIG = ("ffn1_w_in", "ffn1_w_out", "w_in", "w_branch_conv", "w_branch_attn", "w_out", "ffn2_w_in", "ffn2_w_out")
COLUMN_SHARDED = ("ffn1_w_in", "w_in", "ffn2_w_in")


def kernel(x, c, ctx, c_ctx, w_mod, b_mod, norm1_g, norm2_g, norm3_g, ffn1_w_in, ffn1_w_out, w_in, conv_w, q_norm_g, k_norm_g, w_branch_conv, w_branch_attn, w_out, ffn2_w_in, ffn2_w_out, final_g, loss_target, m_c_ctx, m_w_mod, m_b_mod, m_norm1_g, m_norm2_g, m_norm3_g, m_ffn1_w_in, m_ffn1_w_out, m_w_in, m_conv_w, m_q_norm_g, m_k_norm_g, m_w_branch_conv, m_w_branch_attn, m_w_out, m_ffn2_w_in, m_ffn2_w_out, m_final_g, v_c_ctx, v_w_mod, v_b_mod, v_norm1_g, v_norm2_g, v_norm3_g, v_ffn1_w_in, v_ffn1_w_out, v_w_in, v_conv_w, v_q_norm_g, v_k_norm_g, v_w_branch_conv, v_w_branch_attn, v_w_out, v_ffn2_w_in, v_ffn2_w_out, v_final_g):
    w = dict(c_ctx=c_ctx, w_mod=w_mod, b_mod=b_mod, norm1_g=norm1_g, norm2_g=norm2_g, norm3_g=norm3_g,
             ffn1_w_in=ffn1_w_in, ffn1_w_out=ffn1_w_out, w_in=w_in, conv_w=conv_w, q_norm_g=q_norm_g,
             k_norm_g=k_norm_g, w_branch_conv=w_branch_conv, w_branch_attn=w_branch_attn, w_out=w_out,
             ffn2_w_in=ffn2_w_in, ffn2_w_out=ffn2_w_out, final_g=final_g)
    m = dict(c_ctx=m_c_ctx, w_mod=m_w_mod, b_mod=m_b_mod, norm1_g=m_norm1_g, norm2_g=m_norm2_g, norm3_g=m_norm3_g,
             ffn1_w_in=m_ffn1_w_in, ffn1_w_out=m_ffn1_w_out, w_in=m_w_in, conv_w=m_conv_w, q_norm_g=m_q_norm_g,
             k_norm_g=m_k_norm_g, w_branch_conv=m_w_branch_conv, w_branch_attn=m_w_branch_attn, w_out=m_w_out,
             ffn2_w_in=m_ffn2_w_in, ffn2_w_out=m_ffn2_w_out, final_g=m_final_g)
    v = dict(c_ctx=v_c_ctx, w_mod=v_w_mod, b_mod=v_b_mod, norm1_g=v_norm1_g, norm2_g=v_norm2_g, norm3_g=v_norm3_g,
             ffn1_w_in=v_ffn1_w_in, ffn1_w_out=v_ffn1_w_out, w_in=v_w_in, conv_w=v_conv_w, q_norm_g=v_q_norm_g,
             k_norm_g=v_k_norm_g, w_branch_conv=v_w_branch_conv, w_branch_attn=v_w_branch_attn, w_out=v_w_out,
             ffn2_w_in=v_ffn2_w_in, ffn2_w_out=v_ffn2_w_out, final_g=v_final_g)

    xi, yi, ci = _place()
    dev = 4 * xi + 2 * yi + ci
    shard = 2 * xi + yi
    idx = jnp.stack([ci, shard, 2 * (1 - xi) + yi, 2 * xi + (1 - yi), 2 * (1 - xi) + (1 - yi)]).astype(jnp.int32)
    D = x.shape[-1]
    ctx_len = ctx.shape[1]
    assert ctx_len == ROW and c.shape == (1, D)
    mcols = w_mod.shape[2]
    ccols = conv_w.shape[2]

    def place(names, token):
        fulls = []
        for n in names:
            fulls.append(_place_shard(w[n][0], idx, n in COLUMN_SHARDED, "place_" + n, token))
            token = fulls[-1][:16, :HEAD_DIM]
        return fulls

    ffn1_gather, ffn1_token = _gather_begin(place(BIG[:2], c), "ffn1")
    fulls_rest = place(BIG[2:], ffn1_token)

    rope = _rope_tables(ctx_len, x.shape[1])
    c8 = jnp.broadcast_to(c, (8, D))
    for token, name in ((fulls_rest[-1][:16, :HEAD_DIM], "after_place"), (rope[0], "after_rope_cos"),
                        (rope[1], "after_rope_sin")):
        c8 = _after(c8, token, name)
    c_all = _allgather8(c8, "ag_c")[:, 0, :]
    cin = jnp.concatenate([c_all, _pad_rows(c_ctx[None], 8)], axis=0)
    b_sh = lax.dynamic_slice(b_mod, (0, shard * mcols), (1, mcols))
    mod_sh = _mod_rows(cin, w_mod[0], b_sh, "mod_rows")
    conv_rows = jnp.pad(conv_w[0], ((0, 8 - conv_w.shape[1]), (0, mcols - ccols)))
    mod_all = _allgather8(jnp.concatenate([mod_sh, conv_rows], axis=0), "ag_mod")
    mod_full = jnp.concatenate([mod_all[2 * s, :16] for s in range(N_CHIPS)], axis=1)
    conv_full = jnp.concatenate([mod_all[2 * s, 16:16 + conv_w.shape[1], :ccols] for s in range(N_CHIPS)], axis=1)
    mod_lat = lax.dynamic_slice(mod_full, (dev, 0), (1, N_MOD * D)).reshape(N_MOD, D)
    mod_ctx = mod_full[CTX_ROW].reshape(N_MOD, D)
    mods = jnp.stack([_pad_rows(mod_ctx, 16), _pad_rows(mod_lat, 16)])

    ffn1_w = _gather_end(ffn1_gather, mods, "ffn1")
    hooks = _Exchanges(fulls_rest, idx)

    xcat = (ctx[0], x[0])
    norm1_first = _after(norm1_g, hooks.token, "after_ag_rest")
    grad_x, grads, accs = _local_step(xcat, loss_target[0], mods, (norm1_first, norm2_g, norm3_g), final_g[None],
                                      q_norm_g, k_norm_g, conv_full, ffn1_w, hooks, rope)
    g = {}

    pack = jnp.concatenate([a.reshape(2 * ACC_ROWS, D) for a in accs], axis=0)
    gathered = _allgather8(pack, "ag_small")
    loss8, db_mod, g_norms, g_conv, g_qk, dm = _small_reduce(gathered, "small_reduce")
    dm_sh = lax.dynamic_slice(dm[:, :N_MOD, :].reshape(16, N_MOD * D), (0, shard * mcols), (16, mcols))
    g_wmod, cpart = _wmod_grad(cin, dm_sh, w_mod[0], "wmod_grad")
    g["w_mod"] = g_wmod[None]
    cparts = _allgather8(cpart[CTX_ROW:CTX_ROW + 8], "ag_cctx")
    g_cctx = _cctx_grad(cparts, _pad_rows(c_ctx[None], 8), "cctx_grad")
    g_conv_sh = lax.dynamic_slice(g_conv, (0, shard * ccols), (conv_w.shape[1], ccols))
    g_misc = jnp.concatenate([g_qk[0:1, 0:2 * HEAD_DIM], g_conv_sh.reshape(1, -1)], axis=1)
    g_pack = jnp.concatenate([g_cctx, db_mod, g_norms, _pad_rows(g_misc, 8)], axis=0)

    def packed(p):
        return _pack_small(p["c_ctx"], p["b_mod"], p["norm1_g"], p["norm2_g"], p["norm3_g"], p["final_g"],
                           p["q_norm_g"], p["k_norm_g"], p["conv_w"][0], D)

    d_pack, m_pack, v_pack = _adamw(packed(w), g_pack, packed(m), packed(v), "adamw_small")

    g.update(_unpack_small(g_pack, D, conv_w.shape))
    delta = _unpack_small(d_pack, D, conv_w.shape)
    new_m = _unpack_small(m_pack, D, conv_w.shape)
    new_v = _unpack_small(v_pack, D, conv_w.shape)

    def update(n, g2):
        if n in COLUMN_SHARDED:
            g2, d2, m2, v2 = _adamw_transposed(w[n][0], g2, m[n][0], v[n][0], "adamw_" + n)
        else:
            d2, m2, v2 = _adamw(w[n][0], g2, m[n][0], v[n][0], "adamw_" + n)
        g[n], delta[n], new_m[n], new_v[n] = g2[None], d2[None], m2[None], v2[None]
        return v2

    token_d = hooks.reduce_early([grads[0]], "d", token=d_pack[:8, :HEAD_DIM])
    h_wbc, h_wba, h_wo, h_w2i, h_w2o, h_wi, h_w1o = hooks.finish("abc", token_d)
    done = _pair_swap([h_w1o, h_wi, h_wbc, h_wba, h_wo, h_w2i, h_w2o], "rs_pair_swap")
    last = update("w_mod", g_wmod)
    for n, r in zip(BIG[1:], done):
        last = update(n, r)
    (h_w1i,) = hooks.finish("d", last)
    update(BIG[0], _pair_swap([h_w1i], "rs_pair_swap_d")[0])

    loss = loss8[0, 0]
    return (loss, grad_x[None], *[g[n] for n in WEIGHT_ORDER], *[delta[n] for n in WEIGHT_ORDER],
            *[new_m[n] for n in WEIGHT_ORDER], *[new_v[n] for n in WEIGHT_ORDER])
```

```python
import functools

import jax
import jax.numpy as jnp
from jax import lax
from jax.experimental import pallas as pl
from jax.experimental.pallas import tpu as pltpu

F32 = jnp.float32
BF16 = jnp.bfloat16

HEAD_DIM = 128
N_Q_HEADS = 8
N_KV_HEADS = 2
GROUP = N_Q_HEADS // N_KV_HEADS
GRID_W = 64
ROPE_THETA = 10000.0
EPS = 1e-6
ATTN_SCALE = HEAD_DIM ** -0.5

ADAM_LR = 0.001
ADAM_B1 = 0.9
ADAM_B2 = 0.999
ADAM_EPS = 1e-08
ADAM_WD = 0.01
ADAM_STEP = 10

LANES = 128
ROW = 256
HALO = 16
ACC_ROWS = 8
N_CHIPS = 4
N_DEV = 8
MESH = pl.DeviceIdType.MESH
VMEM_LIMIT = 48 * 1024 * 1024
ADAMW_BLOCK_BYTES = 1024 * 1024


def _pick(n, prefs):
    for p in prefs:
        if n % p == 0:
            return p
    return n


def _params(sem):
    return pltpu.CompilerParams(dimension_semantics=sem, vmem_limit_bytes=VMEM_LIMIT)


def _stream(i):
    return jnp.minimum(i, 1)


def _grad_matmul(a, b, name, token=None):
    (T, M), (T2, N) = a.shape, b.shape
    assert T == T2, (a.shape, b.shape)
    tm = _pick(M, (1664, 1408, 1024, 512, 256, 128))
    tk = _pick(T, (2816, 1408, 768, 512, 256))
    nk = T // tk
    extra = [] if token is None else [token]

    def body(a_ref, b_ref, *rest):
        o_ref, acc_ref = rest[len(extra):]
        p = lax.dot_general(a_ref[...].astype(BF16), b_ref[...].astype(BF16), (((0,), (0,)), ((), ())),
                            preferred_element_type=F32)
        k = pl.program_id(1)

        @pl.when(k == 0)
        def _():
            acc_ref[...] = p

        @pl.when(k > 0)
        def _():
            acc_ref[...] += p

        @pl.when(k == nk - 1)
        def _():
            o_ref[...] = acc_ref[...].astype(BF16)

    return pl.pallas_call(
        body, name=name, grid=(M // tm, nk),
        in_specs=[pl.BlockSpec((tk, tm), lambda i, k: (k, i)), pl.BlockSpec((tk, N), lambda i, k: (k, 0))] +
                 [pl.BlockSpec(t.shape, lambda i, k: (0, 0)) for t in extra],
        out_specs=pl.BlockSpec((tm, N), lambda i, k: (i, 0)),
        out_shape=jax.ShapeDtypeStruct((M, N), BF16),
        scratch_shapes=[pltpu.VMEM((tm, N), F32)],
        compiler_params=_params(("parallel", "arbitrary")),
    )(a, b, *extra)


def _row_spec(width, col=0):
    return pl.BlockSpec((ROW, width), lambda i, col=col: (i, col))


def _mods_spec(D):
    return pl.BlockSpec((1, 16, D), lambda i: (_stream(i), 0, 0))


def _acc_spec(D):
    return pl.BlockSpec((1, ACC_ROWS, D), lambda i: (_stream(i), 0, 0))


def _vec_spec(rows, D):
    return pl.BlockSpec((rows, D), lambda i: (0, 0))


def _acc_init(acc_ref):
    i = pl.program_id(0)

    @pl.when(i <= 1)
    def _():
        acc_ref[...] = jnp.zeros_like(acc_ref)


def _acc_add(acc_ref, row, val):
    acc_ref[0, row:row + 1, :] += jnp.sum(val, axis=0, keepdims=True)


def _rows_operand(x):
    if not isinstance(x, tuple):
        return [_row_spec(x.shape[1])], [x], x.shape
    ctx, lat = x
    D = lat.shape[1]
    assert ctx.shape == (ROW, D)
    specs = [pl.BlockSpec((ROW, D), lambda i: (0, 0)), pl.BlockSpec((ROW, D), lambda i: (jnp.maximum(i - 1, 0), 0))]
    return specs, [ctx, lat], (ROW + lat.shape[0], D)


def _rows_tile(refs):
    if len(refs) == 1:
        return refs[0][...]
    return jnp.where(pl.program_id(0) == 0, refs[0][...], refs[1][...])


def _norm_tile_fwd(x, m, g, shift_idx, scale_idx):
    inv = lax.rsqrt(jnp.mean(x * x, axis=-1, keepdims=True) + EPS)
    y = (x * inv) * g
    return (y * (1.0 + m[scale_idx:scale_idx + 1, :]) + m[shift_idx:shift_idx + 1, :]).astype(BF16)


def _norm_tile_bwd(x, dh, dres, m, g, shift_idx, scale_idx, acc_ref):
    inv = lax.rsqrt(jnp.mean(x * x, axis=-1, keepdims=True) + EPS)
    xn = x * inv
    dy = dh * (1.0 + m[scale_idx:scale_idx + 1, :])
    dxn = dy * g
    _acc_add(acc_ref, 0, dh)
    _acc_add(acc_ref, 1, dh * (xn * g))
    _acc_add(acc_ref, 2, dy * xn)
    return inv * (dxn - xn * jnp.mean(dxn * xn, axis=-1, keepdims=True)) + dres


def _gate_tile_bwd(dx, branch, m, gate, acc_ref):
    gate_idx, fac = gate
    _acc_add(acc_ref, 3, fac * dx * branch)
    return ((fac * m[gate_idx:gate_idx + 1, :]) * dx).astype(BF16)


_NT = (((1,), (1,)), ((), ()))


def _ffn_chunk(F):
    return _pick(F, (2816, 1408, 512, 256, 128))


def _resident():
    return pl.BlockSpec(memory_space=pltpu.VMEM)


def _ffn_tile_fwd(hv, wi_ref, wo_ref, u_ref, s_ref, F, cw):
    acc = jnp.zeros((hv.shape[0], wo_ref.shape[1]), F32)
    for j in range(F // cw):
        a = lax.dot_general(hv, wi_ref[j * cw:(j + 1) * cw, :], _NT, preferred_element_type=F32)
        b = lax.dot_general(hv, wi_ref[F + j * cw:F + (j + 1) * cw, :], _NT, preferred_element_type=F32)
        s = ((a * jax.nn.sigmoid(a)) * b).astype(BF16)
        u_ref[:, j * cw:(j + 1) * cw] = a.astype(BF16)
        u_ref[:, F + j * cw:F + (j + 1) * cw] = b.astype(BF16)
        s_ref[:, j * cw:(j + 1) * cw] = s
        acc = acc + jnp.dot(s, wo_ref[j * cw:(j + 1) * cw, :], preferred_element_type=F32)
    return acc


def _norm_ffn_fwd(xprev, branch, mods, g, gate, shift_idx, scale_idx, w_in_t, w_out, name, head=None):
    x_specs, x_args, (T, D) = _rows_operand(xprev)
    F = w_out.shape[0]
    cw = _ffn_chunk(F)
    has_res = branch is not None
    n_in = len(x_args) + int(has_res) + 4 + (2 if head else 0)

    def body(*refs):
        ins, outs = list(refs[:n_in]), list(refs[n_in:])
        x = _rows_tile([ins.pop(0) for _ in x_args])
        f_ref = ins.pop(0) if has_res else None
        m_ref, g_ref, wi_ref, wo_ref = ins[:4]
        xo_ref = outs.pop(0) if has_res else None
        h_ref, u_ref, s_ref = outs[:3]
        m = m_ref[0]
        if has_res:
            gate_idx, fac = gate
            x = x + (fac * m[gate_idx:gate_idx + 1, :]) * f_ref[...]
            xo_ref[...] = x
        hv = _norm_tile_fwd(x, m, g_ref[...], shift_idx, scale_idx)
        h_ref[...] = hv
        f = _ffn_tile_fwd(hv, wi_ref, wo_ref, u_ref, s_ref, F, cw)
        if head is None:
            outs[3][...] = f
            return
        fg_ref, t_ref = ins[4:6]
        dx_ref, df_ref, acc_ref = outs[3:6]
        _acc_init(acc_ref)
        lat = (pl.program_id(0) > 0).astype(F32)
        gate8 = 0.5 * m[8:9, :]
        x3 = x + gate8 * f
        inv3 = lax.rsqrt(jnp.mean(x3 * x3, axis=-1, keepdims=True) + EPS)
        xn = x3 * inv3
        fg = fg_ref[...]
        e = (xn * fg - t_ref[...]) * lat
        dy = e * (1.0 / D)
        dxn = dy * fg
        dx = inv3 * (dxn - xn * jnp.mean(dxn * xn, axis=-1, keepdims=True))
        dx_ref[...] = dx
        df_ref[...] = (gate8 * dx).astype(BF16)
        _acc_add(acc_ref, 0, (0.5 / D) * e * e)
        _acc_add(acc_ref, 1, dy * xn)
        _acc_add(acc_ref, 2, 0.5 * dx * f)

    in_specs = x_specs + ([_row_spec(D)] if has_res else []) + \
               [_mods_spec(D), _vec_spec(1, D), _resident(), _resident()]
    args = x_args + ([branch] if has_res else []) + [mods, g, w_in_t, w_out]
    out_specs = ([_row_spec(D)] if has_res else []) + [_row_spec(D), _row_spec(2 * F), _row_spec(F)]
    out_shape = ([jax.ShapeDtypeStruct((T, D), F32)] if has_res else []) + \
                [jax.ShapeDtypeStruct((T, D), BF16), jax.ShapeDtypeStruct((T, 2 * F), BF16),
                 jax.ShapeDtypeStruct((T, F), BF16)]
    if head is None:
        out_specs += [_row_spec(D)]
        out_shape += [jax.ShapeDtypeStruct((T, D), F32)]
    else:
        in_specs += [_vec_spec(1, D), pl.BlockSpec((ROW, D), lambda i: (jnp.maximum(i - 1, 0), 0))]
        args += list(head)
        out_specs += [_row_spec(D), _row_spec(D), _acc_spec(D)]
        out_shape += [jax.ShapeDtypeStruct((T, D), F32), jax.ShapeDtypeStruct((T, D), BF16),
                      jax.ShapeDtypeStruct((2, ACC_ROWS, D), F32)]
    out = pl.pallas_call(
        body, name=name, grid=(T // ROW,), in_specs=in_specs, out_specs=out_specs, out_shape=out_shape,
        compiler_params=_params(("arbitrary",) if head else ("parallel",)),
    )(*args)
    return tuple(out) if has_res else (None,) + tuple(out)


def _ffn_norm_bwd(df, u, w_in_t, w_out, x, dres, mods, g, shift_idx, scale_idx, gate, branch, name,
                  skip_first_tile=False):
    T, D = df.shape
    F = w_out.shape[0]
    cw = _ffn_chunk(F)
    nt = T // ROW
    has_gate = gate is not None
    x_specs, x_args, _ = _rows_operand(x)
    n_in = 7 + len(x_args) + int(has_gate)

    def body(*refs):
        ins, outs = list(refs[:n_in]), list(refs[n_in:])
        df_ref, u_ref, wi_ref, wo_ref = ins[:4]
        x_refs = ins[4:4 + len(x_args)]
        dr_ref = ins[4 + len(x_args)]
        b_ref = ins[5 + len(x_args)] if has_gate else None
        m_ref, g_ref = ins[-2:]
        du_ref, dx_ref = outs[:2]
        db_ref = outs[2] if has_gate else None
        acc_ref = outs[-1]
        _acc_init(acc_ref)
        dfv = df_ref[...]
        dh = jnp.zeros((ROW, D), F32)
        for j in range(F // cw):
            ds = lax.dot_general(dfv, wo_ref[j * cw:(j + 1) * cw, :], _NT, preferred_element_type=F32)
            a = u_ref[:, j * cw:(j + 1) * cw].astype(F32)
            b = u_ref[:, F + j * cw:F + (j + 1) * cw].astype(F32)
            sig = jax.nn.sigmoid(a)
            da = (ds * b * (sig * (1.0 + a * (1.0 - sig)))).astype(BF16)
            db = (ds * (a * sig)).astype(BF16)
            du_ref[:, j * cw:(j + 1) * cw] = da
            du_ref[:, F + j * cw:F + (j + 1) * cw] = db
            dh = dh + jnp.dot(da, wi_ref[j * cw:(j + 1) * cw, :], preferred_element_type=F32)
            dh = dh + jnp.dot(db, wi_ref[F + j * cw:F + (j + 1) * cw, :], preferred_element_type=F32)
        m = m_ref[0]
        dx = _norm_tile_bwd(_rows_tile(x_refs), dh, dr_ref[...], m, g_ref[...], shift_idx, scale_idx, acc_ref)
        dx_ref[...] = dx
        if has_gate:
            db_ref[...] = _gate_tile_bwd(dx, b_ref[...], m, gate, acc_ref)

    in_specs = [_row_spec(D), _row_spec(2 * F), _resident(), _resident()] + x_specs + [_row_spec(D)] + \
               ([_row_spec(D)] if has_gate else []) + [_mods_spec(D), _vec_spec(1, D)]
    args = [df, u, w_in_t, w_out] + x_args + [dres] + ([branch] if has_gate else []) + [mods, g]
    if skip_first_tile:
        dx_spec = pl.BlockSpec((ROW, D), lambda i: (jnp.maximum(i - 1, 0), 0))
        dx_shape = jax.ShapeDtypeStruct((T - ROW, D), F32)
    else:
        dx_spec = _row_spec(D)
        dx_shape = jax.ShapeDtypeStruct((T, D), F32)
    out_specs = [_row_spec(2 * F), dx_spec] + ([_row_spec(D)] if has_gate else []) + [_acc_spec(D)]
    out_shape = [jax.ShapeDtypeStruct((T, 2 * F), BF16), dx_shape] + \
                ([jax.ShapeDtypeStruct((T, D), BF16)] if has_gate else []) + \
                [jax.ShapeDtypeStruct((2, ACC_ROWS, D), F32)]
    out = pl.pallas_call(
        body, name=name, grid=(nt,), in_specs=in_specs, out_specs=out_specs, out_shape=out_shape,
        compiler_params=_params(("arbitrary",)),
    )(*args)
    if has_gate:
        return tuple(out)
    return out[0], out[1], None, out[2]


def _halo_specs(width, col, nt):
    per = ROW // HALO
    prev = pl.BlockSpec((HALO, width), lambda i, col=col: (jnp.maximum(i * per - 1, 0), col))
    nxt = pl.BlockSpec((HALO, width), lambda i, col=col: (jnp.minimum((i + 1) * per, nt * per - 1), col))
    return prev, nxt


def _f32(ref):
    return ref[...].astype(F32)


def _last_row(halo_ref):
    return halo_ref[HALO - 1:HALO, :].astype(F32)


def _first_row(halo_ref):
    return halo_ref[0:1, :].astype(F32)


def _shift_rows(v, prev_row, next_row):
    rows = lax.broadcasted_iota(jnp.int32, v.shape, 0)
    down = jnp.where(rows == 0, prev_row, pltpu.roll(v, 1, 0))
    up = jnp.where(rows == v.shape[0] - 1, next_row, pltpu.roll(v, v.shape[0] - 1, 0))
    return down, up


def _conv_fwd_operands(P, conv_w, D):
    nt = P.shape[0] // ROW
    cg_p, cg_n = _halo_specs(D, 1, nt)
    vc_p, vc_n = _halo_specs(D, 2, nt)
    specs = [_row_spec(D, 0), _row_spec(D, 1), _row_spec(D, 2), cg_p, vc_p, cg_n, vc_n, _vec_spec(3, D)]
    return specs, [P, P, P, P, P, P, P, conv_w]


def _conv_tile_fwd(refs, nt):
    bg_ref, cg_ref, vc_ref, cgp_ref, vcp_ref, cgn_ref, vcn_ref, w_ref = refs
    i = pl.program_id(0)
    has_prev = (i != 1).astype(F32)
    has_next = (i != nt - 1).astype(F32)
    u = _f32(cg_ref) * _f32(vc_ref)
    up_row = _last_row(cgp_ref) * _last_row(vcp_ref) * has_prev
    un_row = _first_row(cgn_ref) * _first_row(vcn_ref) * has_next
    um1, up1 = _shift_rows(u, up_row, un_row)
    w = w_ref[...]
    conv = um1 * w[0:1, :] + u * w[1:2, :] + up1 * w[2:3, :]
    return (_f32(bg_ref) * conv).astype(BF16)


def _conv_bwd_operands(P, dy, conv_w, D):
    nt = P.shape[0] // ROW
    bg_p, bg_n = _halo_specs(D, 0, nt)
    cg_p, cg_n = _halo_specs(D, 1, nt)
    vc_p, vc_n = _halo_specs(D, 2, nt)
    dy_p, dy_n = _halo_specs(D, 0, nt)
    specs = [_row_spec(D, 0), _row_spec(D, 1), _row_spec(D, 2), _row_spec(D, 0),
             bg_p, cg_p, vc_p, dy_p, bg_n, cg_n, vc_n, dy_n, _vec_spec(3, D)]
    return specs, [P, P, P, dy, P, P, P, dy, P, P, P, dy, conv_w]


def _conv_tile_bwd(refs, o_ref, acc_ref, D, nt):
    (bg_ref, cg_ref, vc_ref, dy_ref, bgp_ref, cgp_ref, vcp_ref, dyp_ref,
     bgn_ref, cgn_ref, vcn_ref, dyn_ref, w_ref) = refs
    i = pl.program_id(0)
    lat = (i > 0).astype(F32)
    has_prev = (i != 1).astype(F32)
    has_next = (i != nt - 1).astype(F32)
    bg = _f32(bg_ref)
    cg = _f32(cg_ref)
    vc = _f32(vc_ref)
    dyv = dy_ref[...] * lat
    u = cg * vc
    up_row = _last_row(cgp_ref) * _last_row(vcp_ref) * has_prev
    un_row = _first_row(cgn_ref) * _first_row(vcn_ref) * has_next
    um1, up1 = _shift_rows(u, up_row, un_row)
    w = w_ref[...]
    conv = um1 * w[0:1, :] + u * w[1:2, :] + up1 * w[2:3, :]
    dc = dyv * bg
    dcp_row = _last_row(dyp_ref) * _last_row(bgp_ref) * has_prev
    dcn_row = _first_row(dyn_ref) * _first_row(bgn_ref) * has_next
    dcm1, dcp1 = _shift_rows(dc, dcp_row, dcn_row)
    du = dcp1 * w[0:1, :] + dc * w[1:2, :] + dcm1 * w[2:3, :]
    o_ref[:, 0:D] = (dyv * conv).astype(BF16)
    o_ref[:, D:2 * D] = (du * vc * lat).astype(BF16)
    o_ref[:, 2 * D:3 * D] = (du * cg * lat).astype(BF16)
    _acc_add(acc_ref, 0, dc * um1)
    _acc_add(acc_ref, 1, dc * u)
    _acc_add(acc_ref, 2, dc * up1)


def _rope_tables(ctx_len, seq):
    n_freq = HEAD_DIM // 4
    rows = seq // GRID_W
    inv = ROPE_THETA ** (-jnp.arange(n_freq, dtype=F32) / n_freq)
    ar = jnp.arange(rows, dtype=F32)[:, None] * inv
    ac = jnp.arange(GRID_W, dtype=F32)[:, None] * inv

    def per_row(a):
        return jnp.repeat(a, GRID_W, axis=0)

    def per_col(a):
        return jnp.tile(a, (rows, 1))

    cos_t = jnp.concatenate([per_row(jnp.cos(ar)), per_row(jnp.cos(ar)), per_col(jnp.cos(ac)), per_col(jnp.cos(ac))], axis=1)
    sin_t = jnp.concatenate([per_row(-jnp.sin(ar)), per_row(jnp.sin(ar)), per_col(-jnp.sin(ac)), per_col(jnp.sin(ac))], axis=1)
    cos_t = jnp.concatenate([jnp.ones((ctx_len, HEAD_DIM), F32), cos_t], axis=0)
    sin_t = jnp.concatenate([jnp.zeros((ctx_len, HEAD_DIM), F32), sin_t], axis=0)
    return cos_t, sin_t


def _swap_halves(y):
    lanes = lax.broadcasted_iota(jnp.int32, y.shape, 1)
    first = (lanes % 64) < 32
    return jnp.where(first, pltpu.roll(y, HEAD_DIM - 32, 1), pltpu.roll(y, 32, 1))


def _to_row(col, n):
    return jnp.transpose(jnp.broadcast_to(col, (n, HEAD_DIM)))[0:1, :]


LOG2E = 1.4426950408889634
ATTN_PART_LANES = 256
ATTN_QUERY_ROWS = 768
ATTN_VMEM_LIMIT = 60 * 1024 * 1024


def _flash_fwd(q, k, v, name, tq=None, tk=None):
    T = q.shape[0]
    tq = tq or _pick(T, (ATTN_QUERY_ROWS, ROW))
    parts = GROUP * tq // ATTN_PART_LANES
    tk = tk or _pick(T, (2816, 1408, 768, 512, 256))
    ck = tk
    nk = T // tk
    GW = GROUP * HEAD_DIM

    def body(q_ref, k_ref, v_ref, o_ref, lse_ref, qs_ref, m_ref, l_ref, acc_ref, st_ref):
        ki = pl.program_id(2)

        @pl.when(ki == 0)
        def _():
            for g in range(GROUP):
                qs_ref[g * tq:(g + 1) * tq, :] = q_ref[:, g * HEAD_DIM:(g + 1) * HEAD_DIM]
            m_ref[...] = jnp.full(m_ref.shape, -jnp.inf, F32)
            l_ref[...] = jnp.zeros(l_ref.shape, F32)
            acc_ref[...] = jnp.zeros(acc_ref.shape, F32)

        w = ATTN_PART_LANES
        nck = tk // ck

        def lanes(p):
            return slice(p * w, (p + 1) * w)

        def keys(c):
            return slice(c * ck, (c + 1) * ck)

        def fold(a):
            return a.reshape(ck // 8, 8, w)

        def scores(p, c):
            st = lax.dot_general(k_ref[keys(c), :], qs_ref[lanes(p), :], _NT,
                                 preferred_element_type=F32) * (ATTN_SCALE * LOG2E)
            st_ref[keys(c), lanes(p)] = st
            return jnp.max(fold(st), axis=0)

        def new_max(p, partial):
            m_prev = m_ref[:, lanes(p)]
            m_new = jnp.maximum(m_prev, jnp.max(functools.reduce(jnp.maximum, partial), axis=0, keepdims=True))
            m_ref[:, lanes(p)] = m_new
            return m_new, jnp.exp2(m_prev - m_new)

        def weights(p, c, m_new):
            pt = jnp.exp2(st_ref[keys(c), lanes(p)] - m_new)
            pv = lax.dot_general(v_ref[keys(c), :], pt.astype(BF16), (((0,), (0,)), ((), ())),
                                 preferred_element_type=F32)
            return jnp.sum(fold(pt), axis=0), pv

        partial = [scores(0, c) for c in range(nck)]
        for p in range(parts):
            m_new, alpha = new_max(p, partial)
            partial, sums, pvs = [], [], []
            for c in range(nck):
                if p + 1 < parts:
                    partial.append(scores(p + 1, c))
                s8, pv = weights(p, c, m_new)
                sums.append(s8)
                pvs.append(pv)
            l_ref[:, lanes(p)] = alpha * l_ref[:, lanes(p)] + jnp.sum(sum(sums), axis=0, keepdims=True)
            acc_ref[:, lanes(p)] = alpha * acc_ref[:, lanes(p)] + sum(pvs)

        @pl.when(ki == nk - 1)
        def _():
            out = jnp.transpose(acc_ref[...] / l_ref[...])
            lse = m_ref[...] + jnp.log2(l_ref[...])
            for g in range(GROUP):
                o_ref[:, g * HEAD_DIM:(g + 1) * HEAD_DIM] = out[g * tq:(g + 1) * tq, :]
                lse_ref[0, g:g + 1, :] = lse[:, g * tq:(g + 1) * tq]

    return pl.pallas_call(
        body, name=name, grid=(N_KV_HEADS, T // tq, nk),
        in_specs=[pl.BlockSpec((tq, GW), lambda h, i, j: (i, h)),
                  pl.BlockSpec((tk, HEAD_DIM), lambda h, i, j: (j, h)),
                  pl.BlockSpec((tk, HEAD_DIM), lambda h, i, j: (j, h))],
        out_specs=[pl.BlockSpec((tq, GW), lambda h, i, j: (i, h)),
                   pl.BlockSpec((1, GROUP, tq), lambda h, i, j: (h, 0, i))],
        out_shape=[jax.ShapeDtypeStruct((T, N_Q_HEADS * HEAD_DIM), F32),
                   jax.ShapeDtypeStruct((N_KV_HEADS, GROUP, T), F32)],
        scratch_shapes=[pltpu.VMEM((GROUP * tq, HEAD_DIM), BF16), pltpu.VMEM((1, GROUP * tq), F32),
                        pltpu.VMEM((1, GROUP * tq), F32), pltpu.VMEM((HEAD_DIM, GROUP * tq), F32),
                        pltpu.VMEM((tk, GROUP * tq), F32)],
        compiler_params=pltpu.CompilerParams(dimension_semantics=("parallel", "parallel", "arbitrary"),
                                             vmem_limit_bytes=ATTN_VMEM_LIMIT),
    )(q, k, v)


def _flash_bwd(q, k, v, do, lse, delta, name, tq=None, tk=None, token=None):
    T = q.shape[0]
    tq = tq or _pick(T, (ATTN_QUERY_ROWS, ROW))
    tk = tk or _pick(T, (1408, 768, 512, 256))
    nk = T // tk
    GW = GROUP * HEAD_DIM
    nt = (((1,), (1,)), ((), ()))
    extra = [] if token is None else [token]

    def body(q_ref, do_ref, k_ref, v_ref, lse_ref, dl_ref, *rest):
        dq_ref, dk_ref, dv_ref, qs_ref, dos_ref, dqt_ref = rest[len(extra):]
        qi = pl.program_id(1)
        ki = pl.program_id(2)

        @pl.when(ki == 0)
        def _():
            for g in range(GROUP):
                qs_ref[g * tq:(g + 1) * tq, :] = q_ref[:, g * HEAD_DIM:(g + 1) * HEAD_DIM]
                dos_ref[g * tq:(g + 1) * tq, :] = do_ref[:, g * HEAD_DIM:(g + 1) * HEAD_DIM]
            dqt_ref[...] = jnp.zeros(dqt_ref.shape, F32)

        kk = k_ref[...]
        vv = v_ref[...]

        def lanes(p):
            return slice(p * tq, (p + 1) * tq)

        def products(p):
            st = lax.dot_general(kk, qs_ref[lanes(p), :], nt, preferred_element_type=F32)
            dpt = lax.dot_general(vv, dos_ref[lanes(p), :], nt, preferred_element_type=F32)
            return st, dpt

        dk_c = jnp.zeros((tk, HEAD_DIM), F32)
        dv_c = jnp.zeros((tk, HEAD_DIM), F32)
        ahead = products(0)
        for p in range(GROUP):
            st, dpt = ahead
            if p + 1 < GROUP:
                ahead = products(p + 1)
            pt = jnp.exp2(st * (ATTN_SCALE * LOG2E) - lse_ref[0, p:p + 1, :])
            dst = ((pt * (dpt - dl_ref[0, p:p + 1, :])) * ATTN_SCALE).astype(BF16)
            dv_c = dv_c + jnp.dot(pt.astype(BF16), dos_ref[lanes(p), :], preferred_element_type=F32)
            dk_c = dk_c + jnp.dot(dst, qs_ref[lanes(p), :], preferred_element_type=F32)
            dqt_ref[:, lanes(p)] += lax.dot_general(kk, dst, (((0,), (0,)), ((), ())), preferred_element_type=F32)
        rows = pl.ds(pl.multiple_of(ki * tk, tk), tk)

        @pl.when(qi == 0)
        def _():
            dk_ref[rows, :] = dk_c
            dv_ref[rows, :] = dv_c

        @pl.when(qi > 0)
        def _():
            dk_ref[rows, :] += dk_c
            dv_ref[rows, :] += dv_c

        @pl.when(ki == nk - 1)
        def _():
            dqv = jnp.transpose(dqt_ref[...])
            for g in range(GROUP):
                dq_ref[:, g * HEAD_DIM:(g + 1) * HEAD_DIM] = dqv[g * tq:(g + 1) * tq, :]

    return pl.pallas_call(
        body, name=name, grid=(N_KV_HEADS, T // tq, nk),
        in_specs=[pl.BlockSpec((tq, GW), lambda h, i, j: (i, h)),
                  pl.BlockSpec((tq, GW), lambda h, i, j: (i, h)),
                  pl.BlockSpec((tk, HEAD_DIM), lambda h, i, j: (j, h)),
                  pl.BlockSpec((tk, HEAD_DIM), lambda h, i, j: (j, h)),
                  pl.BlockSpec((1, GROUP, tq), lambda h, i, j: (h, 0, i)),
                  pl.BlockSpec((1, GROUP, tq), lambda h, i, j: (h, 0, i))] +
                 [pl.BlockSpec(t.shape, lambda h, i, j: (0, 0)) for t in extra],
        out_specs=[pl.BlockSpec((tq, GW), lambda h, i, j: (i, h)),
                   pl.BlockSpec((T, HEAD_DIM), lambda h, i, j: (0, h)),
                   pl.BlockSpec((T, HEAD_DIM), lambda h, i, j: (0, h))],
        out_shape=[jax.ShapeDtypeStruct((T, N_Q_HEADS * HEAD_DIM), F32),
                   jax.ShapeDtypeStruct((T, N_KV_HEADS * HEAD_DIM), F32),
                   jax.ShapeDtypeStruct((T, N_KV_HEADS * HEAD_DIM), F32)],
        scratch_shapes=[pltpu.VMEM((GROUP * tq, HEAD_DIM), BF16), pltpu.VMEM((GROUP * tq, HEAD_DIM), BF16),
                        pltpu.VMEM((HEAD_DIM, GROUP * tq), F32)],
        compiler_params=pltpu.CompilerParams(dimension_semantics=("arbitrary", "arbitrary", "arbitrary"),
                                             vmem_limit_bytes=ATTN_VMEM_LIMIT),
    )(q, do, k, v, lse, delta, *extra)


def _gate_specs(D):
    w = D // 2
    first = (3 * D + (N_Q_HEADS + 2 * N_KV_HEADS) * HEAD_DIM) // w
    return [pl.BlockSpec((ROW, w), lambda i, c=first + j: (i, c)) for j in range(4)]


def _merge_fwd(o, P, conv_w, wbc, wba, wo, D, name):
    T = o.shape[0]
    nt = T // ROW
    w = D // 2
    conv_specs, conv_args = _conv_fwd_operands(P, conv_w, D)
    nc = len(conv_args)

    def body(*refs):
        o_ref, g0, g1, g2, g3, wbc_ref, wba_ref, wo_ref, yc_ref, a1_ref, a2_ref, z_ref, mo_ref = refs[nc:]
        yc_ref[...] = _conv_tile_fwd(refs[:nc], nt)
        a1 = jnp.dot(yc_ref[...], wbc_ref[...], preferred_element_type=F32)
        a2 = jnp.dot(o_ref[...].astype(BF16), wba_ref[...], preferred_element_type=F32)
        a1_ref[...] = a1.astype(BF16)
        a2_ref[...] = a2.astype(BF16)
        for j, (gc, ga) in enumerate(((g0, g2), (g1, g3))):
            sl = slice(j * w, (j + 1) * w)
            z = jax.nn.sigmoid(_f32(gc)) * a1[:, sl] + jax.nn.sigmoid(_f32(ga)) * a2[:, sl]
            z_ref[:, sl] = z.astype(BF16)
        mo_ref[...] = jnp.dot(z_ref[...], wo_ref[...], preferred_element_type=F32)

    return pl.pallas_call(
        body, name=name, grid=(T // ROW,),
        in_specs=conv_specs + [_row_spec(D)] + _gate_specs(D) + [_resident()] * 3,
        out_specs=[_row_spec(D)] * 5,
        out_shape=[jax.ShapeDtypeStruct((T, D), BF16), jax.ShapeDtypeStruct((T, D), BF16),
                   jax.ShapeDtypeStruct((T, D), BF16), jax.ShapeDtypeStruct((T, D), BF16),
                   jax.ShapeDtypeStruct((T, D), F32)],
        compiler_params=_params(("parallel",)),
    )(*conv_args, o, P, P, P, P, wbc, wba, wo)


def _merge_bwd(dmo, a1, a2, o, P, wbc, wba, wo, D, name):
    T = a1.shape[0]
    w = D // 2

    def body(dmo_ref, a1_ref, a2_ref, o_ref, g0, g1, g2, g3, wbc_ref, wba_ref, wo_ref,
             d1_ref, d2_ref, dg_ref, dyc_ref, dob_ref, dl_ref):
        dz = lax.dot_general(dmo_ref[...], wo_ref[...], _NT, preferred_element_type=F32)
        for j, (gc, ga) in enumerate(((g0, g2), (g1, g3))):
            sl = slice(j * w, (j + 1) * w)
            dzs = dz[:, sl]
            sc = jax.nn.sigmoid(_f32(gc))
            sa = jax.nn.sigmoid(_f32(ga))
            d1_ref[:, sl] = (dzs * sc).astype(BF16)
            d2_ref[:, sl] = (dzs * sa).astype(BF16)
            dg_ref[:, j * w:(j + 1) * w] = (dzs * a1_ref[:, sl].astype(F32) * (sc * (1.0 - sc))).astype(BF16)
            dg_ref[:, D + j * w:D + (j + 1) * w] = (dzs * a2_ref[:, sl].astype(F32) * (sa * (1.0 - sa))).astype(BF16)
        dyc_ref[...] = lax.dot_general(d1_ref[...], wbc_ref[...], _NT, preferred_element_type=F32)
        dov = lax.dot_general(d2_ref[...], wba_ref[...], _NT, preferred_element_type=F32)
        dob_ref[...] = dov.astype(BF16)
        prod = dov * o_ref[...]
        for h in range(N_Q_HEADS):
            d = jnp.sum(prod[:, h * HEAD_DIM:(h + 1) * HEAD_DIM], axis=1, keepdims=True)
            dl_ref[h // GROUP, (h % GROUP):(h % GROUP) + 1, :] = _to_row(d, ROW)

    return pl.pallas_call(
        body, name=name, grid=(T // ROW,),
        in_specs=[_row_spec(D)] * 4 + _gate_specs(D) + [_resident()] * 3,
        out_specs=[_row_spec(D), _row_spec(D), _row_spec(2 * D), _row_spec(D), _row_spec(D),
                   pl.BlockSpec((N_KV_HEADS, GROUP, ROW), lambda i: (0, 0, i))],
        out_shape=[jax.ShapeDtypeStruct((T, D), BF16), jax.ShapeDtypeStruct((T, D), BF16),
                   jax.ShapeDtypeStruct((T, 2 * D), BF16), jax.ShapeDtypeStruct((T, D), F32),
                   jax.ShapeDtypeStruct((T, D), BF16), jax.ShapeDtypeStruct((N_KV_HEADS, GROUP, T), F32)],
        compiler_params=_params(("parallel",)),
    )(dmo, a1, a2, o, P, P, P, P, wbc, wba, wo)


def _adamw_math(w, g, m, v):
    m = ADAM_B1 * m + (1.0 - ADAM_B1) * g
    v = ADAM_B2 * v + (1.0 - ADAM_B2) * (g * g)
    m_hat = m / (1.0 - ADAM_B1 ** ADAM_STEP)
    v_hat = v / (1.0 - ADAM_B2 ** ADAM_STEP)
    delta = -ADAM_LR * (m_hat / (jnp.sqrt(v_hat) + ADAM_EPS) + ADAM_WD * w)
    return delta, m, v


def _adamw(w, g, m, v, name):
    R, C = w.shape
    tr = _pick(R, tuple(t for t in (256, 128, 64, 32, 16, 8) if t * C * 4 <= ADAMW_BLOCK_BYTES))

    def body(w_ref, g_ref, m_ref, v_ref, d_ref, mo_ref, vo_ref):
        d, mn, vn = _adamw_math(w_ref[...], g_ref[...], m_ref[...], v_ref[...])
        d_ref[...] = d
        mo_ref[...] = mn
        vo_ref[...] = vn

    spec = pl.BlockSpec((tr, C), lambda i: (i, 0))
    return pl.pallas_call(
        body, name=name, grid=(R // tr,),
        in_specs=[spec] * 4, out_specs=[spec] * 3,
        out_shape=[jax.ShapeDtypeStruct((R, C), F32)] * 3,
        compiler_params=_params(("parallel",)),
    )(w, g, m, v)


def _norm_mix_in_fwd(xprev, branch, mods, g, gate, shift_idx, scale_idx, w_t, gq, gk, cos_t, sin_t, name):
    x_specs, x_args, (T, D) = _rows_operand(xprev)
    N = w_t.shape[0]
    QW = N_Q_HEADS * HEAD_DIM
    KW = N_KV_HEADS * HEAD_DIM
    q0, k0, v0 = 3 * D, 3 * D + QW, 3 * D + QW + KW
    edges = [0, D, 2 * D, q0, k0, v0 + KW] + list(range(v0 + KW + D, N + 1, D))
    assert edges[-1] == N

    def body(*refs):
        f_ref, m_ref, g_ref, w_ref, gq_ref, gk_ref, c_ref, s_ref = refs[len(x_args):len(x_args) + 8]
        xo_ref, h_ref, p_ref, qo_ref, ko_ref, vo_ref = refs[len(x_args) + 8:]
        m = m_ref[0]
        gate_idx, fac = gate
        x = _rows_tile(refs[:len(x_args)]) + (fac * m[gate_idx:gate_idx + 1, :]) * f_ref[...]
        xo_ref[...] = x
        hv = _norm_tile_fwd(x, m, g_ref[...], shift_idx, scale_idx)
        h_ref[...] = hv
        c = c_ref[...]
        s = s_ref[...]

        def head(xh, gain):
            inv = lax.rsqrt(jnp.mean(xh * xh, axis=-1, keepdims=True) + EPS)
            y = (xh * inv) * gain
            return y * c + _swap_halves(y) * s

        for lo, hi in zip(edges[:-1], edges[1:]):
            pb = lax.dot_general(hv, w_ref[lo:hi, :], _NT, preferred_element_type=F32).astype(BF16)
            p_ref[:, lo:hi] = pb
            if lo == q0:
                for h in range(N_Q_HEADS):
                    sl = slice(h * HEAD_DIM, (h + 1) * HEAD_DIM)
                    qo_ref[:, sl] = head(pb[:, sl].astype(F32), gq_ref[...]).astype(BF16)
            elif lo == k0:
                for h in range(N_KV_HEADS):
                    sl = slice(h * HEAD_DIM, (h + 1) * HEAD_DIM)
                    ko_ref[:, sl] = head(pb[:, sl].astype(F32), gk_ref[...]).astype(BF16)
                vo_ref[...] = pb[:, KW:2 * KW]

    return pl.pallas_call(
        body, name=name, grid=(T // ROW,),
        in_specs=x_specs + [_row_spec(D), _mods_spec(D), _vec_spec(1, D), _resident(),
                            _vec_spec(1, HEAD_DIM), _vec_spec(1, HEAD_DIM), _row_spec(HEAD_DIM), _row_spec(HEAD_DIM)],
        out_specs=[_row_spec(D), _row_spec(D), _row_spec(N), _row_spec(QW), _row_spec(KW), _row_spec(KW)],
        out_shape=[jax.ShapeDtypeStruct((T, D), F32), jax.ShapeDtypeStruct((T, D), BF16),
                   jax.ShapeDtypeStruct((T, N), BF16), jax.ShapeDtypeStruct((T, QW), BF16),
                   jax.ShapeDtypeStruct((T, KW), BF16), jax.ShapeDtypeStruct((T, KW), BF16)],
        compiler_params=_params(("parallel",)),
    )(*x_args, branch, mods, g, w_t, gq, gk, cos_t, sin_t)


def _mix_in_norm_bwd(dyc, dgt, P, conv_w, dq, dk, dv, gq, gk, cos_t, sin_t, w_t, x, dres, mods, g, shift_idx,
                     scale_idx, gate, branch, name):
    T, D = x.shape
    nt = T // ROW
    QW = N_Q_HEADS * HEAD_DIM
    KW = N_KV_HEADS * HEAD_DIM
    q0, g0 = 3 * D, 3 * D + QW + 2 * KW
    assert g0 + dgt.shape[1] == w_t.shape[0]
    conv_specs, conv_args = _conv_bwd_operands(P, dyc, conv_w, D)
    nc = len(conv_args)

    def body(*refs):
        (dg_ref, q_ref, k_ref, dq_ref, dk_ref, dv_ref, gq_ref, gk_ref, c_ref, s_ref,
         w_ref, x_ref, dr_ref, b_ref, m_ref, g_ref,
         dx_ref, db_ref, acc_ref, dc_ref, cacc_ref, o_ref, qacc_ref) = refs[nc:]
        _acc_init(acc_ref)
        _acc_init(cacc_ref)
        _acc_init(qacc_ref)
        _conv_tile_bwd(refs[:nc], dc_ref, cacc_ref, D, nt)
        dh = jnp.dot(dc_ref[...], w_ref[0:q0, :], preferred_element_type=F32)
        c = c_ref[...]
        s = s_ref[...]

        def head(xh, d, gain):
            dyv = d * c + _swap_halves(d * s)
            inv = lax.rsqrt(jnp.mean(xh * xh, axis=-1, keepdims=True) + EPS)
            xn = xh * inv
            dxn = dyv * gain
            dxh = inv * (dxn - xn * jnp.mean(dxn * xn, axis=-1, keepdims=True))
            return dxh, jnp.sum(dyv * xn, axis=0, keepdims=True)

        dgq = jnp.zeros((1, HEAD_DIM), F32)
        for h in range(N_Q_HEADS):
            sl = slice(h * HEAD_DIM, (h + 1) * HEAD_DIM)
            dxh, dgh = head(q_ref[:, sl].astype(F32), dq_ref[:, sl], gq_ref[...])
            o_ref[:, sl] = dxh.astype(BF16)
            dgq = dgq + dgh
        dh = dh + jnp.dot(dg_ref[...], w_ref[g0:, :], preferred_element_type=F32)
        dgk = jnp.zeros((1, HEAD_DIM), F32)
        for h in range(N_KV_HEADS):
            sl = slice(h * HEAD_DIM, (h + 1) * HEAD_DIM)
            dxh, dgh = head(k_ref[:, sl].astype(F32), dk_ref[:, sl], gk_ref[...])
            o_ref[:, QW + h * HEAD_DIM:QW + (h + 1) * HEAD_DIM] = dxh.astype(BF16)
            dgk = dgk + dgh
        o_ref[:, QW + KW:QW + 2 * KW] = dv_ref[...].astype(BF16)
        qacc_ref[0, 0:1, 0:HEAD_DIM] += dgq
        qacc_ref[0, 1:2, 0:HEAD_DIM] += dgk
        dh = dh + jnp.dot(o_ref[...], w_ref[q0:g0, :], preferred_element_type=F32)
        m = m_ref[0]
        dx = _norm_tile_bwd(x_ref[...], dh, dr_ref[...], m, g_ref[...], shift_idx, scale_idx, acc_ref)
        dx_ref[...] = dx
        db_ref[...] = _gate_tile_bwd(dx, b_ref[...], m, gate, acc_ref)

    return pl.pallas_call(
        body, name=name, grid=(T // ROW,),
        in_specs=conv_specs +
                 [_row_spec(dgt.shape[1]), _row_spec(QW, q0 // QW), _row_spec(KW, (q0 + QW) // KW),
                  _row_spec(QW), _row_spec(KW), _row_spec(KW), _vec_spec(1, HEAD_DIM), _vec_spec(1, HEAD_DIM),
                  _row_spec(HEAD_DIM), _row_spec(HEAD_DIM),
                  _resident(), _row_spec(D), _row_spec(D), _row_spec(D), _mods_spec(D), _vec_spec(1, D)],
        out_specs=[_row_spec(D), _row_spec(D), _acc_spec(D), _row_spec(q0), _acc_spec(D),
                   _row_spec(QW + 2 * KW), _acc_spec(D)],
        out_shape=[jax.ShapeDtypeStruct((T, D), F32), jax.ShapeDtypeStruct((T, D), BF16),
                   jax.ShapeDtypeStruct((2, ACC_ROWS, D), F32), jax.ShapeDtypeStruct((T, q0), BF16),
                   jax.ShapeDtypeStruct((2, ACC_ROWS, D), F32), jax.ShapeDtypeStruct((T, QW + 2 * KW), BF16),
                   jax.ShapeDtypeStruct((2, ACC_ROWS, D), F32)],
        compiler_params=_params(("arbitrary",)),
    )(*conv_args, dgt, P, P, dq, dk, dv, gq, gk, cos_t, sin_t, w_t, x, dres, branch, mods, g)


def _adamw_transposed(w, gt, m, v, name):
    R, C = w.shape
    tc = LANES

    def body(w_ref, g_ref, m_ref, v_ref, go_ref, d_ref, mo_ref, vo_ref):
        g = jnp.transpose(g_ref[...])
        d, mn, vn = _adamw_math(w_ref[...], g, m_ref[...], v_ref[...])
        go_ref[...] = g
        d_ref[...] = d
        mo_ref[...] = mn
        vo_ref[...] = vn

    spec = pl.BlockSpec((R, tc), lambda j: (0, j))
    return pl.pallas_call(
        body, name=name, grid=(C // tc,),
        in_specs=[spec, pl.BlockSpec((tc, R), lambda j: (j, 0)), spec, spec], out_specs=[spec] * 4,
        out_shape=[jax.ShapeDtypeStruct((R, C), F32)] * 4,
        compiler_params=_params(("parallel",)),
    )(w, gt, m, v)


class _NoExchange:
    def __init__(self, rest):
        self.rest = rest

    def rest_weights(self, after):
        return self.rest

    def reduce_early(self, grads, tag):
        return None


def _local_step(xcat, target, mods, norm_g, final_g, gq, gk, conv_w, ffn1_w, hooks, rope):
    T, D = _rows_operand(xcat)[2]
    w1i, w1o = ffn1_w
    g1, g2, g3 = norm_g
    cos_t, sin_t = rope

    def after(value, token, name):
        return value if token is None else _after(value, token, name)

    _, h1, u1, s1, f1 = _norm_ffn_fwd(xcat, None, mods, g1, None, 0, 1, w1i, w1o, "f_ffn1")
    wi, wbc, wba, wo, w2i, w2o = hooks.rest_weights(f1)
    x1, h2, P, qn, kn, vb = _norm_mix_in_fwd(xcat, f1, mods, g2, (2, 0.5), 3, 4, wi, gq, gk, cos_t, sin_t, "f_mix_in")
    o, lse = _flash_fwd(qn, kn, vb, "f_attn")
    yc, a1, a2, z, mo = _merge_fwd(o, P, conv_w, wbc, wba, wo, D, "f_merge")
    x2, h3, u2, s2, dx3, df2, acc_head = _norm_ffn_fwd(x1, mo, mods, g3, (5, 1.0), 6, 7, w2i, w2o, "f_ffn2",
                                                       head=(final_g, target))

    du2, dx2, dmo, acc_n3 = _ffn_norm_bwd(df2, u2, w2i, w2o, x2, dx3, mods, g3, 6, 7, (5, 1.0), mo, "b_ffn2")
    g_w2o = _grad_matmul(s2, df2, "b_ffn2_out_dw")
    g_w2i = _grad_matmul(du2, h3, "b_ffn2_in_dw")

    g_wo = _grad_matmul(z, dmo, "b_mix_out_dw")
    da1, da2, dgt, dyc, dob, delta = _merge_bwd(dmo, a1, a2, o, P, wbc, wba, wo, D, "b_merge")
    g_wbc = _grad_matmul(yc, da1, "b_branch_conv_dw")
    g_wba = _grad_matmul(o, da2, "b_branch_attn_dw")
    token_a = hooks.reduce_early([g_wbc, g_wba, g_wo, g_w2i, g_w2o], "a")
    dq, dk, dv = _flash_bwd(qn, kn, vb, dob, lse, delta, "b_attn", token=token_a)
    dx1, df1, acc_n2, dconv, acc_conv, dqkv, acc_qk = _mix_in_norm_bwd(
        dyc, dgt, P, conv_w, dq, dk, dv, gq, gk, cos_t, sin_t, wi, x1, dx2, mods, g2, 3, 4, (2, 0.5), f1, "b_mix_in")
    d_parts = (dconv, dqkv, dgt)
    g_wi = jnp.concatenate([_grad_matmul(dp, h2, f"b_mix_in_dw_{i}") for i, dp in enumerate(d_parts)], axis=0)
    g1_b = after(g1, hooks.reduce_early([g_wi], "b"), "after_rs_b")

    du1, grad_x, _, acc_n1 = _ffn_norm_bwd(df1, u1, w1i, w1o, xcat, dx1, mods, g1_b, 0, 1, None, None, "b_ffn1",
                                           skip_first_tile=True)
    g_w1o = _grad_matmul(s1, df1, "b_ffn1_out_dw")
    g_w1i = _grad_matmul(du1, h1, "b_ffn1_in_dw", token=hooks.reduce_early([g_w1o], "c"))

    grads = (g_w1i, g_w1o, g_wi, g_wbc, g_wba, g_wo, g_w2i, g_w2o)
    accs = (acc_head, acc_n3, acc_n2, acc_n1, acc_conv, acc_qk)
    return grad_x, grads, accs


def _place():
    return lax.axis_index("x"), lax.axis_index("y"), lax.axis_index("c")


def _other_chips(x, y):
    return [(1 - x, y), (x, 1 - y), (1 - x, 1 - y)]


def _allgather8(v, name):
    R, N = v.shape

    def body(v_ref, out_ref, send_sems, recv_sems, local_sem):
        x, y, c = _place()
        me, sibling = (x, y, c), (x, y, 1 - c)
        chips = _other_chips(x, y)

        def blk(px, py, pc):
            return out_ref.at[4 * px + 2 * py + pc]

        def copy(k, block, to, src=None):
            return pltpu.make_async_remote_copy(
                src_ref=blk(*block) if src is None else src, dst_ref=blk(*block),
                send_sem=send_sems.at[k], recv_sem=recv_sems.at[k], device_id=to, device_id_type=MESH)

        mine = pltpu.make_async_copy(v_ref, blk(*me), local_sem)
        mine.start()
        first = [copy(0, me, sibling, src=v_ref)]
        first += [copy(1 + j, me, (*chip, c), src=v_ref) for j, chip in enumerate(chips)]
        for cp in first:
            cp.start()
        passed = [copy(4 + j, (*chip, c), sibling) for j, chip in enumerate(chips)]
        for j, chip in enumerate(chips):
            copy(1 + j, (*chip, c), me).wait_recv()
            passed[j].start()
        copy(0, sibling, me).wait_recv()
        for j, chip in enumerate(chips):
            copy(4 + j, (*chip, 1 - c), me).wait_recv()
        for cp in first + passed:
            cp.wait_send()
        mine.wait()

    return pl.pallas_call(
        body, name=name,
        out_shape=jax.ShapeDtypeStruct((N_DEV, R, N), v.dtype),
        in_specs=[pl.BlockSpec(memory_space=pltpu.VMEM)],
        out_specs=pl.BlockSpec(memory_space=pltpu.VMEM),
        scratch_shapes=[pltpu.SemaphoreType.DMA((7,)), pltpu.SemaphoreType.DMA((7,)), pltpu.SemaphoreType.DMA],
        compiler_params=pltpu.CompilerParams(vmem_limit_bytes=VMEM_LIMIT),
    )(v)


def _any_specs(n):
    return [pl.BlockSpec(memory_space=pl.ANY)] * n


def _place_shard(w2, idx, transpose, name, token):
    if transpose:
        D, rs = w2.shape
        tr = LANES
        in_spec = pl.BlockSpec((D, tr), lambda i, idx: (0, i))
    else:
        rs, D = w2.shape
        tr = _pick(rs, (352, 256, 128, 64, 32, 16))
        in_spec = pl.BlockSpec((tr, D), lambda i, idx: (i, 0))
    steps = rs // tr

    def body(idx_ref, w_ref, t_ref, o_ref):
        v = w_ref[...]
        o_ref[...] = (jnp.transpose(v) if transpose else v).astype(BF16)

    return pl.pallas_call(
        body, name=name,
        grid_spec=pltpu.PrefetchScalarGridSpec(
            num_scalar_prefetch=1, grid=(steps,),
            in_specs=[in_spec, pl.BlockSpec(token.shape, lambda i, idx: (0, 0))],
            out_specs=pl.BlockSpec((tr, D), lambda i, idx: (idx[1] * steps + i, 0))),
        out_shape=jax.ShapeDtypeStruct((N_CHIPS * rs, D), BF16),
        compiler_params=_params(("arbitrary",)),
    )(idx, w2, token)


_HBM = pl.BlockSpec(memory_space=pltpu.HBM)
_SEM = pl.BlockSpec(memory_space=pltpu.SEMAPHORE)
_EFFECT = pltpu.SideEffectType.DATAFLOW_SIDE_EFFECTING


def _in_hbm(a):
    return pltpu.with_memory_space_constraint(a, pltpu.HBM)


def _split_copies(n, per, make):
    def start(nbuf, name, bufs, after=None):
        extra = [] if after is None else [after]

        def body(*refs):
            ins = refs[:nbuf]
            send_sems, recv_sems = refs[nbuf + len(extra)], refs[nbuf + len(extra) + 1]
            token = refs[-1]
            for t in range(n):
                for j in range(per):
                    make(ins, t, j, send_sems.at[per * t + j], recv_sems.at[per * t + j]).start()
            token[...] = jnp.zeros(token.shape, token.dtype)

        out = pl.pallas_call(
            body, name=name,
            out_shape=(pltpu.SemaphoreType.DMA((per * n,)), pltpu.SemaphoreType.DMA((per * n,)),
                       *[pltpu.HBM(b.shape, b.dtype) for b in bufs], jax.ShapeDtypeStruct((8, 128), F32)),
            in_specs=[_HBM] * nbuf + [pl.BlockSpec(memory_space=pl.ANY)] * len(extra),
            out_specs=(_SEM, _SEM, *[_HBM] * nbuf, pl.BlockSpec(memory_space=pltpu.VMEM)),
            input_output_aliases={i: 2 + i for i in range(nbuf)},
            compiler_params=pltpu.CompilerParams(has_side_effects=_EFFECT),
        )(*[_in_hbm(b) for b in bufs], *extra)
        return out[0], out[1], list(out[2:2 + nbuf]), out[-1]

    def wait(nbuf, name, send_sems, recv_sems, bufs, after):
        afters = list(after) if isinstance(after, (list, tuple)) else [after]

        def body(*refs):
            ins = refs[:nbuf]
            ss, rs = refs[nbuf], refs[nbuf + 1]
            for t in range(n):
                for j in range(per):
                    cp = make(ins, t, j, ss.at[per * t + j], rs.at[per * t + j])
                    cp.wait_send()
                    cp.wait_recv()

        return pl.pallas_call(
            body, name=name,
            out_shape=[pltpu.HBM(b.shape, b.dtype) for b in bufs],
            in_specs=[_HBM] * nbuf + [_SEM, _SEM] + [pl.BlockSpec(memory_space=pl.ANY)] * len(afters),
            out_specs=[_HBM] * nbuf,
            input_output_aliases={i: i for i in range(nbuf)},
            compiler_params=pltpu.CompilerParams(has_side_effects=_EFFECT),
        )(*bufs, send_sems, recv_sems, *afters)

    return start, wait


RS_PEERS = N_DEV - 1


def _reduce_exchange_split(grads):
    n = len(grads)

    def make(bufs, t, j, send_sem, recv_sem):
        x, y, c = _place()
        chip = (x, y) if j == 6 else _other_chips(x, y)[j % 3]
        core = c if j < 3 else 1 - c
        half = grads[t].shape[0] // (2 * N_CHIPS)
        piece = bufs[t].at[pl.ds((2 * (2 * chip[0] + chip[1]) + core) * half, half), :]
        return pltpu.make_async_remote_copy(src_ref=piece, dst_ref=bufs[n + t].at[j], send_sem=send_sem,
                                            recv_sem=recv_sem, device_id=(*chip, core), device_id_type=MESH)

    return _split_copies(n, RS_PEERS, make)


def _reduce_sum(g, landed, idx, name):
    _, half, D = landed.shape
    g4 = g.reshape(N_CHIPS, 2, half, D)
    tr = _pick(half, (416, 352, 128))
    steps = half // tr

    def body(idx_ref, g_ref, l_ref, o_ref):
        acc = g_ref[0, 0].astype(F32)
        for j in range(RS_PEERS):
            acc = acc + l_ref[j].astype(F32)
        o_ref[...] = acc

    return pl.pallas_call(
        body, name=name,
        grid_spec=pltpu.PrefetchScalarGridSpec(
            num_scalar_prefetch=1, grid=(steps,),
            in_specs=[pl.BlockSpec((1, 1, tr, D), lambda i, idx: (idx[1], idx[0], i, 0)),
                      pl.BlockSpec((RS_PEERS, tr, D), lambda i, idx: (0, i, 0))],
            out_specs=pl.BlockSpec((tr, D), lambda i, idx: (idx[0] * steps + i, 0))),
        out_shape=jax.ShapeDtypeStruct((2 * half, D), F32),
        compiler_params=_params(("arbitrary",)),
    )(idx, g4, landed)


def _weights_gather_split(fulls):
    def make(bufs, t, j, send_sem, recv_sem):
        x, y, c = _place()
        chip = _other_chips(x, y)[j]
        rs = fulls[t].shape[0] // N_CHIPS
        rows = bufs[t].at[pl.ds((2 * x + y) * rs + c * (rs // 2), rs // 2), :]
        return pltpu.make_async_remote_copy(src_ref=rows, dst_ref=rows, send_sem=send_sem, recv_sem=recv_sem,
                                            device_id=(*chip, c), device_id_type=MESH)

    return _split_copies(len(fulls), 3, make)


def _weights_pass_on(fulls, name):
    n = len(fulls)

    def body(*refs):
        full = refs[n:2 * n]
        send_sems, recv_sems = refs[2 * n:]
        x, y, c = _place()
        chips = _other_chips(x, y)

        def copy(t, j, h):
            rs = fulls[t].shape[0] // N_CHIPS
            px, py = chips[j]
            rows = full[t].at[pl.ds((2 * px + py) * rs + h * (rs // 2), rs // 2), :]
            return pltpu.make_async_remote_copy(src_ref=rows, dst_ref=rows, send_sem=send_sems.at[3 * t + j],
                                                recv_sem=recv_sems.at[3 * t + j], device_id=(x, y, 1 - c),
                                                device_id_type=MESH)

        for t in range(n):
            for j in range(3):
                copy(t, j, c).start()
        for t in range(n):
            for j in range(3):
                copy(t, j, 1 - c).wait_recv()
        for t in range(n):
            for j in range(3):
                copy(t, j, c).wait_send()

    return pl.pallas_call(
        body, name=name,
        out_shape=[jax.ShapeDtypeStruct(f.shape, f.dtype) for f in fulls],
        in_specs=_any_specs(n), out_specs=_any_specs(n),
        input_output_aliases={t: t for t in range(n)},
        scratch_shapes=[pltpu.SemaphoreType.DMA((3 * n,)), pltpu.SemaphoreType.DMA((3 * n,))],
    )(*fulls)


def _after(value, token, name):
    def body(v_ref, t_ref, o_ref):
        o_ref[...] = v_ref[...]

    return pl.pallas_call(
        body, name=name, out_shape=jax.ShapeDtypeStruct(value.shape, value.dtype),
        in_specs=_whole(2), out_specs=pl.BlockSpec(memory_space=pltpu.VMEM),
    )(value, token)


def _pair_swap(shards, name):
    n = len(shards)

    def body(*refs):
        full = refs[n:2 * n]
        send_sems, recv_sems = refs[2 * n:]
        x, y, c = _place()

        def half(t, h):
            rows = shards[t].shape[0] // 2
            return full[t].at[pl.ds(h * rows, rows), :]

        def copy(t, h):
            return pltpu.make_async_remote_copy(src_ref=half(t, h), dst_ref=half(t, h), send_sem=send_sems.at[t],
                                                recv_sem=recv_sems.at[t], device_id=(x, y, 1 - c),
                                                device_id_type=MESH)

        for t in range(n):
            copy(t, c).start()
        for t in range(n):
            copy(t, 1 - c).wait_recv()
        for t in range(n):
            copy(t, c).wait_send()

    return pl.pallas_call(
        body, name=name,
        out_shape=[jax.ShapeDtypeStruct(a.shape, a.dtype) for a in shards],
        in_specs=_any_specs(n), out_specs=_any_specs(n),
        input_output_aliases={t: t for t in range(n)},
        scratch_shapes=[pltpu.SemaphoreType.DMA((n,)), pltpu.SemaphoreType.DMA((n,))],
    )(*shards)


def _gather_begin(fulls, tag):
    start, wait = _weights_gather_split(fulls)
    send_sems, recv_sems, bufs, token = start(len(fulls), f"ag_{tag}_start", fulls)
    return (wait, send_sems, recv_sems, bufs), token


def _gather_end(state, after, tag):
    wait, send_sems, recv_sems, bufs = state
    landed = wait(len(bufs), f"ag_{tag}_wait", send_sems, recv_sems, bufs, after)
    return _weights_pass_on(landed, f"ag_{tag}_pass_on")


class _Exchanges:
    def __init__(self, fulls_rest, idx):
        self.idx = idx
        self._rest, self.token = _gather_begin(fulls_rest, "rest")
        self._early = []

    def rest_weights(self, after):
        return _gather_end(self._rest, after, "rest")

    def reduce_early(self, grads, tag, token=None):
        zones = [lax.empty((RS_PEERS, g.shape[0] // (2 * N_CHIPS), g.shape[1]), g.dtype) for g in grads]
        start, wait = _reduce_exchange_split(grads)
        send_sems, recv_sems, bufs, token = start(2 * len(grads), "rs_start_" + tag, list(grads) + zones, token)
        self._early.append((tag, wait, send_sems, recv_sems, bufs))
        return token

    def finish(self, tags, after):
        halves = []
        for tag, wait, send_sems, recv_sems, bufs in self._early:
            if tag in tags:
                n = len(bufs) // 2
                done = wait(len(bufs), "rs_wait_" + tag, send_sems, recv_sems, bufs, after)
                halves += [_reduce_sum(g, l, self.idx, f"rs_sum_{tag}{t}")
                           for t, (g, l) in enumerate(zip(done[:n], done[n:]))]
        return halves


N_MOD = 9
PACK_HEAD, PACK_N3, PACK_N2, PACK_N1, PACK_CONV, PACK_QK = 0, 16, 32, 48, 64, 80
MOD_SRC = ((PACK_N1, 0), (PACK_N1, 1), (PACK_N2, 3), (PACK_N2, 0), (PACK_N2, 1),
           (PACK_N3, 3), (PACK_N3, 0), (PACK_N3, 1), (PACK_HEAD, 2))
CTX_ROW = 8


def _silu(v):
    return v * jax.nn.sigmoid(v)


def _whole(n):
    return [pl.BlockSpec(memory_space=pltpu.VMEM)] * n


def _mod_rows(cin, w_sh, b_sh, name):
    def body(c_ref, w_ref, b_ref, o_ref):
        a = _silu(c_ref[...]).astype(BF16)
        o_ref[...] = jnp.dot(a, w_ref[...].astype(BF16), preferred_element_type=F32) + b_ref[...]

    return pl.pallas_call(
        body, name=name, out_shape=jax.ShapeDtypeStruct((cin.shape[0], w_sh.shape[1]), F32),
        in_specs=_whole(3), out_specs=pl.BlockSpec(memory_space=pltpu.VMEM),
        compiler_params=pltpu.CompilerParams(vmem_limit_bytes=VMEM_LIMIT),
    )(cin, w_sh, b_sh)


def _small_reduce(gathered, name):
    _, _, D = gathered.shape

    def body(g_ref, loss_ref, db_ref, gn_ref, cv_ref, qk_ref, dm_ref):
        tot = g_ref[0]
        for r in range(1, N_DEV):
            tot = tot + g_ref[r]

        def both(block, row):
            return tot[block + row:block + row + 1, :] + tot[block + 8 + row:block + 8 + row + 1, :]

        loss = jnp.sum(both(PACK_HEAD, 0), axis=1, keepdims=True)
        loss_ref[...] = jnp.broadcast_to(loss, loss_ref.shape)
        db_ref[...] = jnp.zeros(db_ref.shape, F32)
        dm_ref[...] = jnp.zeros(dm_ref.shape, F32)
        for j, (block, row) in enumerate(MOD_SRC):
            db_ref[j:j + 1, :] = both(block, row)
            dm_ref[CTX_ROW, j:j + 1, :] = tot[block + row:block + row + 1, :]
            for r in range(N_DEV):
                dm_ref[r, j:j + 1, :] = g_ref[r, block + 8 + row:block + 8 + row + 1, :]
        gn_ref[...] = jnp.zeros(gn_ref.shape, F32)
        gn_ref[0:1, :] = both(PACK_N1, 2)
        gn_ref[8:9, :] = both(PACK_N2, 2)
        gn_ref[16:17, :] = both(PACK_N3, 2)
        gn_ref[24:25, :] = both(PACK_HEAD, 1)
        cv_ref[...] = jnp.zeros(cv_ref.shape, F32)
        for r in range(3):
            cv_ref[r:r + 1, :] = both(PACK_CONV, r)
        qk_ref[...] = jnp.zeros(qk_ref.shape, F32)
        qk_ref[0:1, 0:HEAD_DIM] = both(PACK_QK, 0)[:, 0:HEAD_DIM]
        qk_ref[0:1, HEAD_DIM:2 * HEAD_DIM] = both(PACK_QK, 1)[:, 0:HEAD_DIM]

    return pl.pallas_call(
        body, name=name,
        out_shape=[jax.ShapeDtypeStruct((8, 128), F32), jax.ShapeDtypeStruct((16, D), F32),
                   jax.ShapeDtypeStruct((32, D), F32), jax.ShapeDtypeStruct((8, D), F32),
                   jax.ShapeDtypeStruct((8, D), F32), jax.ShapeDtypeStruct((16, 16, D), F32)],
        in_specs=_whole(1), out_specs=_whole(6),
        compiler_params=pltpu.CompilerParams(vmem_limit_bytes=VMEM_LIMIT),
    )(gathered)


def _wmod_grad(cin, dm_sh, w_sh, name):
    def body(c_ref, d_ref, w_ref, gw_ref, cp_ref):
        a = _silu(c_ref[...]).astype(BF16)
        d = d_ref[...].astype(BF16)
        gw_ref[...] = lax.dot_general(a, d, (((0,), (0,)), ((), ())), preferred_element_type=F32)
        cp_ref[...] = lax.dot_general(d, w_ref[...].astype(BF16), (((1,), (1,)), ((), ())),
                                      preferred_element_type=F32)

    return pl.pallas_call(
        body, name=name,
        out_shape=[jax.ShapeDtypeStruct(w_sh.shape, F32), jax.ShapeDtypeStruct(cin.shape, F32)],
        in_specs=_whole(3), out_specs=_whole(2),
        compiler_params=pltpu.CompilerParams(vmem_limit_bytes=VMEM_LIMIT),
    )(cin, dm_sh, w_sh)


def _cctx_grad(parts, c_ctx8, name):
    def body(p_ref, c_ref, o_ref):
        tot = p_ref[0] + p_ref[2] + p_ref[4] + p_ref[6]
        cv = c_ref[...]
        sig = jax.nn.sigmoid(cv)
        rows = lax.broadcasted_iota(jnp.int32, tot.shape, 0)
        o_ref[...] = jnp.where(rows == 0, tot * (sig * (1.0 + cv * (1.0 - sig))), 0.0)

    return pl.pallas_call(
        body, name=name, out_shape=jax.ShapeDtypeStruct(c_ctx8.shape, F32),
        in_specs=_whole(2), out_specs=pl.BlockSpec(memory_space=pltpu.VMEM),
    )(parts, c_ctx8)


def _pad_rows(a, rows):
    return jnp.pad(a, ((0, rows - a.shape[0]), (0, 0)))


def _pack_small(c_ctx, b_mod, n1, n2, n3, final_g, gq, gk, conv_sh, D):
    misc = jnp.concatenate([gq, gk, conv_sh.reshape(1, -1)], axis=1)
    return jnp.concatenate([_pad_rows(c_ctx[None], 8), _pad_rows(b_mod.reshape(N_MOD, D), 16), _pad_rows(n1, 8),
                            _pad_rows(n2, 8), _pad_rows(n3, 8), _pad_rows(final_g[None], 8), _pad_rows(misc, 8)], axis=0)


def _unpack_small(p, D, conv_shape):
    misc = p[56:57]
    return dict(c_ctx=p[0], b_mod=p[8:8 + N_MOD].reshape(1, N_MOD * D), norm1_g=p[24:25], norm2_g=p[32:33],
                norm3_g=p[40:41], final_g=p[48], q_norm_g=misc[:, 0:HEAD_DIM], k_norm_g=misc[:, HEAD_DIM:2 * HEAD_DIM],
                conv_w=misc[:, 2 * HEAD_DIM:].reshape(conv_shape))


WEIGHT_ORDER = ("c_ctx", "w_mod", "b_mod", "norm1_g", "norm2_g", "norm3_g", "ffn1_w_in", "ffn1_w_out", "w_in",
                "conv_w", "q_norm_g", "k_norm_g", "w_branch_conv", "w_branch_attn", "w_out", "ffn2_w_in",
                "ffn2_w_out", "final_g")
BIG = ("ffn1_w_in", "ffn1_w_out", "w_in", "w_branch_conv", "w_branch_attn", "w_out", "ffn2_w_in", "ffn2_w_out")
COLUMN_SHARDED = ("ffn1_w_in", "w_in", "ffn2_w_in")


def kernel(x, c, ctx, c_ctx, w_mod, b_mod, norm1_g, norm2_g, norm3_g, ffn1_w_in, ffn1_w_out, w_in, conv_w, q_norm_g, k_norm_g, w_branch_conv, w_branch_attn, w_out, ffn2_w_in, ffn2_w_out, final_g, loss_target, m_c_ctx, m_w_mod, m_b_mod, m_norm1_g, m_norm2_g, m_norm3_g, m_ffn1_w_in, m_ffn1_w_out, m_w_in, m_conv_w, m_q_norm_g, m_k_norm_g, m_w_branch_conv, m_w_branch_attn, m_w_out, m_ffn2_w_in, m_ffn2_w_out, m_final_g, v_c_ctx, v_w_mod, v_b_mod, v_norm1_g, v_norm2_g, v_norm3_g, v_ffn1_w_in, v_ffn1_w_out, v_w_in, v_conv_w, v_q_norm_g, v_k_norm_g, v_w_branch_conv, v_w_branch_attn, v_w_out, v_ffn2_w_in, v_ffn2_w_out, v_final_g):
    w = dict(c_ctx=c_ctx, w_mod=w_mod, b_mod=b_mod, norm1_g=norm1_g, norm2_g=norm2_g, norm3_g=norm3_g,
             ffn1_w_in=ffn1_w_in, ffn1_w_out=ffn1_w_out, w_in=w_in, conv_w=conv_w, q_norm_g=q_norm_g,
             k_norm_g=k_norm_g, w_branch_conv=w_branch_conv, w_branch_attn=w_branch_attn, w_out=w_out,
             ffn2_w_in=ffn2_w_in, ffn2_w_out=ffn2_w_out, final_g=final_g)
    m = dict(c_ctx=m_c_ctx, w_mod=m_w_mod, b_mod=m_b_mod, norm1_g=m_norm1_g, norm2_g=m_norm2_g, norm3_g=m_norm3_g,
             ffn1_w_in=m_ffn1_w_in, ffn1_w_out=m_ffn1_w_out, w_in=m_w_in, conv_w=m_conv_w, q_norm_g=m_q_norm_g,
             k_norm_g=m_k_norm_g, w_branch_conv=m_w_branch_conv, w_branch_attn=m_w_branch_attn, w_out=m_w_out,
             ffn2_w_in=m_ffn2_w_in, ffn2_w_out=m_ffn2_w_out, final_g=m_final_g)
    v = dict(c_ctx=v_c_ctx, w_mod=v_w_mod, b_mod=v_b_mod, norm1_g=v_norm1_g, norm2_g=v_norm2_g, norm3_g=v_norm3_g,
             ffn1_w_in=v_ffn1_w_in, ffn1_w_out=v_ffn1_w_out, w_in=v_w_in, conv_w=v_conv_w, q_norm_g=v_q_norm_g,
             k_norm_g=v_k_norm_g, w_branch_conv=v_w_branch_conv, w_branch_attn=v_w_branch_attn, w_out=v_w_out,
             ffn2_w_in=v_ffn2_w_in, ffn2_w_out=v_ffn2_w_out, final_g=v_final_g)

    xi, yi, ci = _place()
    dev = 4 * xi + 2 * yi + ci
    shard = 2 * xi + yi
    idx = jnp.stack([ci, shard, 2 * (1 - xi) + yi, 2 * xi + (1 - yi), 2 * (1 - xi) + (1 - yi)]).astype(jnp.int32)
    D = x.shape[-1]
    ctx_len = ctx.shape[1]
    assert ctx_len == ROW and c.shape == (1, D)
    mcols = w_mod.shape[2]
    ccols = conv_w.shape[2]

    def place(names, token):
        fulls = []
        for n in names:
            fulls.append(_place_shard(w[n][0], idx, n in COLUMN_SHARDED, "place_" + n, token))
            token = fulls[-1][:16, :HEAD_DIM]
        return fulls

    ffn1_gather, ffn1_token = _gather_begin(place(BIG[:2], c), "ffn1")
    fulls_rest = place(BIG[2:], ffn1_token)

    rope = _rope_tables(ctx_len, x.shape[1])
    c8 = jnp.broadcast_to(c, (8, D))
    for token, name in ((fulls_rest[-1][:16, :HEAD_DIM], "after_place"), (rope[0], "after_rope_cos"),
                        (rope[1], "after_rope_sin")):
        c8 = _after(c8, token, name)
    c_all = _allgather8(c8, "ag_c")[:, 0, :]
    cin = jnp.concatenate([c_all, _pad_rows(c_ctx[None], 8)], axis=0)
    b_sh = lax.dynamic_slice(b_mod, (0, shard * mcols), (1, mcols))
    mod_sh = _mod_rows(cin, w_mod[0], b_sh, "mod_rows")
    conv_rows = jnp.pad(conv_w[0], ((0, 8 - conv_w.shape[1]), (0, mcols - ccols)))
    mod_all = _allgather8(jnp.concatenate([mod_sh, conv_rows], axis=0), "ag_mod")
    mod_full = jnp.concatenate([mod_all[2 * s, :16] for s in range(N_CHIPS)], axis=1)
    conv_full = jnp.concatenate([mod_all[2 * s, 16:16 + conv_w.shape[1], :ccols] for s in range(N_CHIPS)], axis=1)
    mod_lat = lax.dynamic_slice(mod_full, (dev, 0), (1, N_MOD * D)).reshape(N_MOD, D)
    mod_ctx = mod_full[CTX_ROW].reshape(N_MOD, D)
    mods = jnp.stack([_pad_rows(mod_ctx, 16), _pad_rows(mod_lat, 16)])

    ffn1_w = _gather_end(ffn1_gather, mods, "ffn1")
    hooks = _Exchanges(fulls_rest, idx)

    xcat = (ctx[0], x[0])
    norm1_first = _after(norm1_g, hooks.token, "after_ag_rest")
    grad_x, grads, accs = _local_step(xcat, loss_target[0], mods, (norm1_first, norm2_g, norm3_g), final_g[None],
                                      q_norm_g, k_norm_g, conv_full, ffn1_w, hooks, rope)
    g = {}

    pack = jnp.concatenate([a.reshape(2 * ACC_ROWS, D) for a in accs], axis=0)
    gathered = _allgather8(pack, "ag_small")
    loss8, db_mod, g_norms, g_conv, g_qk, dm = _small_reduce(gathered, "small_reduce")
    dm_sh = lax.dynamic_slice(dm[:, :N_MOD, :].reshape(16, N_MOD * D), (0, shard * mcols), (16, mcols))
    g_wmod, cpart = _wmod_grad(cin, dm_sh, w_mod[0], "wmod_grad")
    g["w_mod"] = g_wmod[None]
    cparts = _allgather8(cpart[CTX_ROW:CTX_ROW + 8], "ag_cctx")
    g_cctx = _cctx_grad(cparts, _pad_rows(c_ctx[None], 8), "cctx_grad")
    g_conv_sh = lax.dynamic_slice(g_conv, (0, shard * ccols), (conv_w.shape[1], ccols))
    g_misc = jnp.concatenate([g_qk[0:1, 0:2 * HEAD_DIM], g_conv_sh.reshape(1, -1)], axis=1)
    g_pack = jnp.concatenate([g_cctx, db_mod, g_norms, _pad_rows(g_misc, 8)], axis=0)

    def packed(p):
        return _pack_small(p["c_ctx"], p["b_mod"], p["norm1_g"], p["norm2_g"], p["norm3_g"], p["final_g"],
                           p["q_norm_g"], p["k_norm_g"], p["conv_w"][0], D)

    d_pack, m_pack, v_pack = _adamw(packed(w), g_pack, packed(m), packed(v), "adamw_small")

    g.update(_unpack_small(g_pack, D, conv_w.shape))
    delta = _unpack_small(d_pack, D, conv_w.shape)
    new_m = _unpack_small(m_pack, D, conv_w.shape)
    new_v = _unpack_small(v_pack, D, conv_w.shape)

    def update(n, g2):
        if n in COLUMN_SHARDED:
            g2, d2, m2, v2 = _adamw_transposed(w[n][0], g2, m[n][0], v[n][0], "adamw_" + n)
        else:
            d2, m2, v2 = _adamw(w[n][0], g2, m[n][0], v[n][0], "adamw_" + n)
        g[n], delta[n], new_m[n], new_v[n] = g2[None], d2[None], m2[None], v2[None]
        return v2

    token_d = hooks.reduce_early([grads[0]], "d", token=d_pack[:8, :HEAD_DIM])
    h_wbc, h_wba, h_wo, h_w2i, h_w2o, h_wi, h_w1o = hooks.finish("abc", token_d)
    done = _pair_swap([h_w1o, h_wi, h_wbc, h_wba, h_wo, h_w2i, h_w2o], "rs_pair_swap")
    updated = [update("w_mod", g_wmod)] + [update(n, r) for n, r in zip(BIG[1:], done)]
    (h_w1i,) = hooks.finish("d", updated)
    update(BIG[0], _pair_swap([h_w1i], "rs_pair_swap_d")[0])

    loss = loss8[0, 0]
    return (loss, grad_x[None], *[g[n] for n in WEIGHT_ORDER], *[delta[n] for n in WEIGHT_ORDER],
            *[new_m[n] for n in WEIGHT_ORDER], *[new_v[n] for n in WEIGHT_ORDER])
```

```python
import functools

import jax
import jax.numpy as jnp
from jax import lax
from jax.experimental import pallas as pl
from jax.experimental.pallas import tpu as pltpu

F32 = jnp.float32
BF16 = jnp.bfloat16

HEAD_DIM = 128
N_Q_HEADS = 8
N_KV_HEADS = 2
GROUP = N_Q_HEADS // N_KV_HEADS
GRID_W = 64
ROPE_THETA = 10000.0
EPS = 1e-6
ATTN_SCALE = HEAD_DIM ** -0.5

ADAM_LR = 0.001
ADAM_B1 = 0.9
ADAM_B2 = 0.999
ADAM_EPS = 1e-08
ADAM_WD = 0.01
ADAM_STEP = 10

LANES = 128
ROW = 256
HALO = 16
ACC_ROWS = 8
N_CHIPS = 4
N_DEV = 8
MESH = pl.DeviceIdType.MESH
VMEM_LIMIT = 48 * 1024 * 1024
ADAMW_BLOCK_BYTES = 1024 * 1024


def _pick(n, prefs):
    for p in prefs:
        if n % p == 0:
            return p
    return n


def _params(sem):
    return pltpu.CompilerParams(dimension_semantics=sem, vmem_limit_bytes=VMEM_LIMIT)


def _stream(i):
    return jnp.minimum(i, 1)


def _grad_matmul(a, b, name, token=None):
    (T, M), (T2, N) = a.shape, b.shape
    assert T == T2, (a.shape, b.shape)
    tm = _pick(M, (1664, 1408, 1024, 512, 256, 128))
    tk = _pick(T, (2816, 1408, 768, 512, 256))
    nk = T // tk
    extra = [] if token is None else [token]

    def body(a_ref, b_ref, *rest):
        o_ref, acc_ref = rest[len(extra):]
        p = lax.dot_general(a_ref[...].astype(BF16), b_ref[...].astype(BF16), (((0,), (0,)), ((), ())),
                            preferred_element_type=F32)
        k = pl.program_id(1)

        @pl.when(k == 0)
        def _():
            acc_ref[...] = p

        @pl.when(k > 0)
        def _():
            acc_ref[...] += p

        @pl.when(k == nk - 1)
        def _():
            o_ref[...] = acc_ref[...].astype(BF16)

    return pl.pallas_call(
        body, name=name, grid=(M // tm, nk),
        in_specs=[pl.BlockSpec((tk, tm), lambda i, k: (k, i)), pl.BlockSpec((tk, N), lambda i, k: (k, 0))] +
                 [pl.BlockSpec(t.shape, lambda i, k: (0, 0)) for t in extra],
        out_specs=pl.BlockSpec((tm, N), lambda i, k: (i, 0)),
        out_shape=jax.ShapeDtypeStruct((M, N), BF16),
        scratch_shapes=[pltpu.VMEM((tm, N), F32)],
        compiler_params=_params(("parallel", "arbitrary")),
    )(a, b, *extra)


def _row_spec(width, col=0):
    return pl.BlockSpec((ROW, width), lambda i, col=col: (i, col))


def _mods_spec(D):
    return pl.BlockSpec((1, 16, D), lambda i: (_stream(i), 0, 0))


def _acc_spec(D):
    return pl.BlockSpec((1, ACC_ROWS, D), lambda i: (_stream(i), 0, 0))


def _vec_spec(rows, D):
    return pl.BlockSpec((rows, D), lambda i: (0, 0))


def _acc_init(acc_ref):
    i = pl.program_id(0)

    @pl.when(i <= 1)
    def _():
        acc_ref[...] = jnp.zeros_like(acc_ref)


def _acc_add(acc_ref, row, val):
    acc_ref[0, row:row + 1, :] += jnp.sum(val, axis=0, keepdims=True)


def _rows_operand(x):
    if not isinstance(x, tuple):
        return [_row_spec(x.shape[1])], [x], x.shape
    ctx, lat = x
    D = lat.shape[1]
    assert ctx.shape == (ROW, D)
    specs = [pl.BlockSpec((ROW, D), lambda i: (0, 0)), pl.BlockSpec((ROW, D), lambda i: (jnp.maximum(i - 1, 0), 0))]
    return specs, [ctx, lat], (ROW + lat.shape[0], D)


def _rows_tile(refs):
    if len(refs) == 1:
        return refs[0][...]
    return jnp.where(pl.program_id(0) == 0, refs[0][...], refs[1][...])


def _norm_tile_fwd(x, m, g, shift_idx, scale_idx):
    inv = lax.rsqrt(jnp.mean(x * x, axis=-1, keepdims=True) + EPS)
    y = (x * inv) * g
    return (y * (1.0 + m[scale_idx:scale_idx + 1, :]) + m[shift_idx:shift_idx + 1, :]).astype(BF16)


def _norm_tile_bwd(x, dh, dres, m, g, shift_idx, scale_idx, acc_ref):
    inv = lax.rsqrt(jnp.mean(x * x, axis=-1, keepdims=True) + EPS)
    xn = x * inv
    dy = dh * (1.0 + m[scale_idx:scale_idx + 1, :])
    dxn = dy * g
    _acc_add(acc_ref, 0, dh)
    _acc_add(acc_ref, 1, dh * (xn * g))
    _acc_add(acc_ref, 2, dy * xn)
    return inv * (dxn - xn * jnp.mean(dxn * xn, axis=-1, keepdims=True)) + dres


def _gate_tile_bwd(dx, branch, m, gate, acc_ref):
    gate_idx, fac = gate
    _acc_add(acc_ref, 3, fac * dx * branch)
    return ((fac * m[gate_idx:gate_idx + 1, :]) * dx).astype(BF16)


_NT = (((1,), (1,)), ((), ()))


def _ffn_chunk(F):
    return _pick(F, (2816, 1408, 512, 256, 128))


def _resident():
    return pl.BlockSpec(memory_space=pltpu.VMEM)


def _ffn_tile_fwd(hv, wi_ref, wo_ref, u_ref, s_ref, F, cw):
    acc = jnp.zeros((hv.shape[0], wo_ref.shape[1]), F32)
    for j in range(F // cw):
        a = lax.dot_general(hv, wi_ref[j * cw:(j + 1) * cw, :], _NT, preferred_element_type=F32)
        b = lax.dot_general(hv, wi_ref[F + j * cw:F + (j + 1) * cw, :], _NT, preferred_element_type=F32)
        s = ((a * jax.nn.sigmoid(a)) * b).astype(BF16)
        u_ref[:, j * cw:(j + 1) * cw] = a.astype(BF16)
        u_ref[:, F + j * cw:F + (j + 1) * cw] = b.astype(BF16)
        s_ref[:, j * cw:(j + 1) * cw] = s
        acc = acc + jnp.dot(s, wo_ref[j * cw:(j + 1) * cw, :], preferred_element_type=F32)
    return acc


def _norm_ffn_fwd(xprev, branch, mods, g, gate, shift_idx, scale_idx, w_in_t, w_out, name, head=None):
    x_specs, x_args, (T, D) = _rows_operand(xprev)
    F = w_out.shape[0]
    cw = _ffn_chunk(F)
    has_res = branch is not None
    n_in = len(x_args) + int(has_res) + 4 + (2 if head else 0)

    def body(*refs):
        ins, outs = list(refs[:n_in]), list(refs[n_in:])
        x = _rows_tile([ins.pop(0) for _ in x_args])
        f_ref = ins.pop(0) if has_res else None
        m_ref, g_ref, wi_ref, wo_ref = ins[:4]
        xo_ref = outs.pop(0) if has_res else None
        h_ref, u_ref, s_ref = outs[:3]
        m = m_ref[0]
        if has_res:
            gate_idx, fac = gate
            x = x + (fac * m[gate_idx:gate_idx + 1, :]) * f_ref[...]
            xo_ref[...] = x
        hv = _norm_tile_fwd(x, m, g_ref[...], shift_idx, scale_idx)
        h_ref[...] = hv
        f = _ffn_tile_fwd(hv, wi_ref, wo_ref, u_ref, s_ref, F, cw)
        if head is None:
            outs[3][...] = f
            return
        fg_ref, t_ref = ins[4:6]
        dx_ref, df_ref, acc_ref = outs[3:6]
        _acc_init(acc_ref)
        lat = (pl.program_id(0) > 0).astype(F32)
        gate8 = 0.5 * m[8:9, :]
        x3 = x + gate8 * f
        inv3 = lax.rsqrt(jnp.mean(x3 * x3, axis=-1, keepdims=True) + EPS)
        xn = x3 * inv3
        fg = fg_ref[...]
        e = (xn * fg - t_ref[...]) * lat
        dy = e * (1.0 / D)
        dxn = dy * fg
        dx = inv3 * (dxn - xn * jnp.mean(dxn * xn, axis=-1, keepdims=True))
        dx_ref[...] = dx
        df_ref[...] = (gate8 * dx).astype(BF16)
        _acc_add(acc_ref, 0, (0.5 / D) * e * e)
        _acc_add(acc_ref, 1, dy * xn)
        _acc_add(acc_ref, 2, 0.5 * dx * f)

    in_specs = x_specs + ([_row_spec(D)] if has_res else []) + \
               [_mods_spec(D), _vec_spec(1, D), _resident(), _resident()]
    args = x_args + ([branch] if has_res else []) + [mods, g, w_in_t, w_out]
    out_specs = ([_row_spec(D)] if has_res else []) + [_row_spec(D), _row_spec(2 * F), _row_spec(F)]
    out_shape = ([jax.ShapeDtypeStruct((T, D), F32)] if has_res else []) + \
                [jax.ShapeDtypeStruct((T, D), BF16), jax.ShapeDtypeStruct((T, 2 * F), BF16),
                 jax.ShapeDtypeStruct((T, F), BF16)]
    if head is None:
        out_specs += [_row_spec(D)]
        out_shape += [jax.ShapeDtypeStruct((T, D), F32)]
    else:
        in_specs += [_vec_spec(1, D), pl.BlockSpec((ROW, D), lambda i: (jnp.maximum(i - 1, 0), 0))]
        args += list(head)
        out_specs += [_row_spec(D), _row_spec(D), _acc_spec(D)]
        out_shape += [jax.ShapeDtypeStruct((T, D), F32), jax.ShapeDtypeStruct((T, D), BF16),
                      jax.ShapeDtypeStruct((2, ACC_ROWS, D), F32)]
    out = pl.pallas_call(
        body, name=name, grid=(T // ROW,), in_specs=in_specs, out_specs=out_specs, out_shape=out_shape,
        compiler_params=_params(("arbitrary",) if head else ("parallel",)),
    )(*args)
    return tuple(out) if has_res else (None,) + tuple(out)


def _ffn_norm_bwd(df, u, w_in_t, w_out, x, dres, mods, g, shift_idx, scale_idx, gate, branch, name,
                  skip_first_tile=False):
    T, D = df.shape
    F = w_out.shape[0]
    cw = _ffn_chunk(F)
    nt = T // ROW
    has_gate = gate is not None
    x_specs, x_args, _ = _rows_operand(x)
    n_in = 7 + len(x_args) + int(has_gate)

    def body(*refs):
        ins, outs = list(refs[:n_in]), list(refs[n_in:])
        df_ref, u_ref, wi_ref, wo_ref = ins[:4]
        x_refs = ins[4:4 + len(x_args)]
        dr_ref = ins[4 + len(x_args)]
        b_ref = ins[5 + len(x_args)] if has_gate else None
        m_ref, g_ref = ins[-2:]
        du_ref, dx_ref = outs[:2]
        db_ref = outs[2] if has_gate else None
        acc_ref = outs[-1]
        _acc_init(acc_ref)
        dfv = df_ref[...]
        dh = jnp.zeros((ROW, D), F32)
        for j in range(F // cw):
            ds = lax.dot_general(dfv, wo_ref[j * cw:(j + 1) * cw, :], _NT, preferred_element_type=F32)
            a = u_ref[:, j * cw:(j + 1) * cw].astype(F32)
            b = u_ref[:, F + j * cw:F + (j + 1) * cw].astype(F32)
            sig = jax.nn.sigmoid(a)
            da = (ds * b * (sig * (1.0 + a * (1.0 - sig)))).astype(BF16)
            db = (ds * (a * sig)).astype(BF16)
            du_ref[:, j * cw:(j + 1) * cw] = da
            du_ref[:, F + j * cw:F + (j + 1) * cw] = db
            dh = dh + jnp.dot(da, wi_ref[j * cw:(j + 1) * cw, :], preferred_element_type=F32)
            dh = dh + jnp.dot(db, wi_ref[F + j * cw:F + (j + 1) * cw, :], preferred_element_type=F32)
        m = m_ref[0]
        dx = _norm_tile_bwd(_rows_tile(x_refs), dh, dr_ref[...], m, g_ref[...], shift_idx, scale_idx, acc_ref)
        dx_ref[...] = dx
        if has_gate:
            db_ref[...] = _gate_tile_bwd(dx, b_ref[...], m, gate, acc_ref)

    in_specs = [_row_spec(D), _row_spec(2 * F), _resident(), _resident()] + x_specs + [_row_spec(D)] + \
               ([_row_spec(D)] if has_gate else []) + [_mods_spec(D), _vec_spec(1, D)]
    args = [df, u, w_in_t, w_out] + x_args + [dres] + ([branch] if has_gate else []) + [mods, g]
    if skip_first_tile:
        dx_spec = pl.BlockSpec((ROW, D), lambda i: (jnp.maximum(i - 1, 0), 0))
        dx_shape = jax.ShapeDtypeStruct((T - ROW, D), F32)
    else:
        dx_spec = _row_spec(D)
        dx_shape = jax.ShapeDtypeStruct((T, D), F32)
    out_specs = [_row_spec(2 * F), dx_spec] + ([_row_spec(D)] if has_gate else []) + [_acc_spec(D)]
    out_shape = [jax.ShapeDtypeStruct((T, 2 * F), BF16), dx_shape] + \
                ([jax.ShapeDtypeStruct((T, D), BF16)] if has_gate else []) + \
                [jax.ShapeDtypeStruct((2, ACC_ROWS, D), F32)]
    out = pl.pallas_call(
        body, name=name, grid=(nt,), in_specs=in_specs, out_specs=out_specs, out_shape=out_shape,
        compiler_params=_params(("arbitrary",)),
    )(*args)
    if has_gate:
        return tuple(out)
    return out[0], out[1], None, out[2]


def _halo_specs(width, col, nt):
    per = ROW // HALO
    prev = pl.BlockSpec((HALO, width), lambda i, col=col: (jnp.maximum(i * per - 1, 0), col))
    nxt = pl.BlockSpec((HALO, width), lambda i, col=col: (jnp.minimum((i + 1) * per, nt * per - 1), col))
    return prev, nxt


def _f32(ref):
    return ref[...].astype(F32)


def _last_row(halo_ref):
    return halo_ref[HALO - 1:HALO, :].astype(F32)


def _first_row(halo_ref):
    return halo_ref[0:1, :].astype(F32)


def _shift_rows(v, prev_row, next_row):
    rows = lax.broadcasted_iota(jnp.int32, v.shape, 0)
    down = jnp.where(rows == 0, prev_row, pltpu.roll(v, 1, 0))
    up = jnp.where(rows == v.shape[0] - 1, next_row, pltpu.roll(v, v.shape[0] - 1, 0))
    return down, up


def _conv_fwd_operands(P, conv_w, D):
    nt = P.shape[0] // ROW
    cg_p, cg_n = _halo_specs(D, 1, nt)
    vc_p, vc_n = _halo_specs(D, 2, nt)
    specs = [_row_spec(D, 0), _row_spec(D, 1), _row_spec(D, 2), cg_p, vc_p, cg_n, vc_n, _vec_spec(3, D)]
    return specs, [P, P, P, P, P, P, P, conv_w]


def _conv_tile_fwd(refs, nt):
    bg_ref, cg_ref, vc_ref, cgp_ref, vcp_ref, cgn_ref, vcn_ref, w_ref = refs
    i = pl.program_id(0)
    has_prev = (i != 1).astype(F32)
    has_next = (i != nt - 1).astype(F32)
    u = _f32(cg_ref) * _f32(vc_ref)
    up_row = _last_row(cgp_ref) * _last_row(vcp_ref) * has_prev
    un_row = _first_row(cgn_ref) * _first_row(vcn_ref) * has_next
    um1, up1 = _shift_rows(u, up_row, un_row)
    w = w_ref[...]
    conv = um1 * w[0:1, :] + u * w[1:2, :] + up1 * w[2:3, :]
    return (_f32(bg_ref) * conv).astype(BF16)


def _conv_bwd_operands(P, dy, conv_w, D):
    nt = P.shape[0] // ROW
    bg_p, bg_n = _halo_specs(D, 0, nt)
    cg_p, cg_n = _halo_specs(D, 1, nt)
    vc_p, vc_n = _halo_specs(D, 2, nt)
    dy_p, dy_n = _halo_specs(D, 0, nt)
    specs = [_row_spec(D, 0), _row_spec(D, 1), _row_spec(D, 2), _row_spec(D, 0),
             bg_p, cg_p, vc_p, dy_p, bg_n, cg_n, vc_n, dy_n, _vec_spec(3, D)]
    return specs, [P, P, P, dy, P, P, P, dy, P, P, P, dy, conv_w]


def _conv_tile_bwd(refs, o_ref, acc_ref, D, nt):
    (bg_ref, cg_ref, vc_ref, dy_ref, bgp_ref, cgp_ref, vcp_ref, dyp_ref,
     bgn_ref, cgn_ref, vcn_ref, dyn_ref, w_ref) = refs
    i = pl.program_id(0)
    lat = (i > 0).astype(F32)
    has_prev = (i != 1).astype(F32)
    has_next = (i != nt - 1).astype(F32)
    bg = _f32(bg_ref)
    cg = _f32(cg_ref)
    vc = _f32(vc_ref)
    dyv = dy_ref[...] * lat
    u = cg * vc
    up_row = _last_row(cgp_ref) * _last_row(vcp_ref) * has_prev
    un_row = _first_row(cgn_ref) * _first_row(vcn_ref) * has_next
    um1, up1 = _shift_rows(u, up_row, un_row)
    w = w_ref[...]
    conv = um1 * w[0:1, :] + u * w[1:2, :] + up1 * w[2:3, :]
    dc = dyv * bg
    dcp_row = _last_row(dyp_ref) * _last_row(bgp_ref) * has_prev
    dcn_row = _first_row(dyn_ref) * _first_row(bgn_ref) * has_next
    dcm1, dcp1 = _shift_rows(dc, dcp_row, dcn_row)
    du = dcp1 * w[0:1, :] + dc * w[1:2, :] + dcm1 * w[2:3, :]
    o_ref[:, 0:D] = (dyv * conv).astype(BF16)
    o_ref[:, D:2 * D] = (du * vc * lat).astype(BF16)
    o_ref[:, 2 * D:3 * D] = (du * cg * lat).astype(BF16)
    _acc_add(acc_ref, 0, dc * um1)
    _acc_add(acc_ref, 1, dc * u)
    _acc_add(acc_ref, 2, dc * up1)


def _rope_tables(ctx_len, seq):
    n_freq = HEAD_DIM // 4
    rows = seq // GRID_W
    inv = ROPE_THETA ** (-jnp.arange(n_freq, dtype=F32) / n_freq)
    ar = jnp.arange(rows, dtype=F32)[:, None] * inv
    ac = jnp.arange(GRID_W, dtype=F32)[:, None] * inv

    def per_row(a):
        return jnp.repeat(a, GRID_W, axis=0)

    def per_col(a):
        return jnp.tile(a, (rows, 1))

    cos_t = jnp.concatenate([per_row(jnp.cos(ar)), per_row(jnp.cos(ar)), per_col(jnp.cos(ac)), per_col(jnp.cos(ac))], axis=1)
    sin_t = jnp.concatenate([per_row(-jnp.sin(ar)), per_row(jnp.sin(ar)), per_col(-jnp.sin(ac)), per_col(jnp.sin(ac))], axis=1)
    cos_t = jnp.concatenate([jnp.ones((ctx_len, HEAD_DIM), F32), cos_t], axis=0)
    sin_t = jnp.concatenate([jnp.zeros((ctx_len, HEAD_DIM), F32), sin_t], axis=0)
    return cos_t, sin_t


def _swap_halves(y):
    lanes = lax.broadcasted_iota(jnp.int32, y.shape, 1)
    first = (lanes % 64) < 32
    return jnp.where(first, pltpu.roll(y, HEAD_DIM - 32, 1), pltpu.roll(y, 32, 1))


def _to_row(col, n):
    return jnp.transpose(jnp.broadcast_to(col, (n, HEAD_DIM)))[0:1, :]


LOG2E = 1.4426950408889634
ATTN_PART_LANES = 256
ATTN_QUERY_ROWS = 768
ATTN_VMEM_LIMIT = 60 * 1024 * 1024


def _flash_fwd(q, k, v, name, tq=None, tk=None):
    T = q.shape[0]
    tq = tq or _pick(T, (ATTN_QUERY_ROWS, ROW))
    parts = GROUP * tq // ATTN_PART_LANES
    tk = tk or _pick(T, (2816, 1408, 768, 512, 256))
    ck = tk
    nk = T // tk
    GW = GROUP * HEAD_DIM

    def body(q_ref, k_ref, v_ref, o_ref, lse_ref, qs_ref, m_ref, l_ref, acc_ref, st_ref):
        ki = pl.program_id(2)

        @pl.when(ki == 0)
        def _():
            for g in range(GROUP):
                qs_ref[g * tq:(g + 1) * tq, :] = q_ref[:, g * HEAD_DIM:(g + 1) * HEAD_DIM]
            m_ref[...] = jnp.full(m_ref.shape, -jnp.inf, F32)
            l_ref[...] = jnp.zeros(l_ref.shape, F32)
            acc_ref[...] = jnp.zeros(acc_ref.shape, F32)

        w = ATTN_PART_LANES
        nck = tk // ck

        def lanes(p):
            return slice(p * w, (p + 1) * w)

        def keys(c):
            return slice(c * ck, (c + 1) * ck)

        def fold(a):
            return a.reshape(ck // 8, 8, w)

        def scores(p, c):
            st = lax.dot_general(k_ref[keys(c), :], qs_ref[lanes(p), :], _NT,
                                 preferred_element_type=F32) * (ATTN_SCALE * LOG2E)
            st_ref[keys(c), lanes(p)] = st
            return jnp.max(fold(st), axis=0)

        def new_max(p, partial):
            m_prev = m_ref[:, lanes(p)]
            m_new = jnp.maximum(m_prev, jnp.max(functools.reduce(jnp.maximum, partial), axis=0, keepdims=True))
            m_ref[:, lanes(p)] = m_new
            return m_new, jnp.exp2(m_prev - m_new)

        def weights(p, c, m_new):
            pt = jnp.exp2(st_ref[keys(c), lanes(p)] - m_new)
            pv = lax.dot_general(v_ref[keys(c), :], pt.astype(BF16), (((0,), (0,)), ((), ())),
                                 preferred_element_type=F32)
            return jnp.sum(fold(pt), axis=0), pv

        partial = [scores(0, c) for c in range(nck)]
        for p in range(parts):
            m_new, alpha = new_max(p, partial)
            partial, sums, pvs = [], [], []
            for c in range(nck):
                if p + 1 < parts:
                    partial.append(scores(p + 1, c))
                s8, pv = weights(p, c, m_new)
                sums.append(s8)
                pvs.append(pv)
            l_ref[:, lanes(p)] = alpha * l_ref[:, lanes(p)] + jnp.sum(sum(sums), axis=0, keepdims=True)
            acc_ref[:, lanes(p)] = alpha * acc_ref[:, lanes(p)] + sum(pvs)

        @pl.when(ki == nk - 1)
        def _():
            out = jnp.transpose(acc_ref[...] / l_ref[...])
            lse = m_ref[...] + jnp.log2(l_ref[...])
            for g in range(GROUP):
                o_ref[:, g * HEAD_DIM:(g + 1) * HEAD_DIM] = out[g * tq:(g + 1) * tq, :]
                lse_ref[0, g:g + 1, :] = lse[:, g * tq:(g + 1) * tq]

    return pl.pallas_call(
        body, name=name, grid=(N_KV_HEADS, T // tq, nk),
        in_specs=[pl.BlockSpec((tq, GW), lambda h, i, j: (i, h)),
                  pl.BlockSpec((tk, HEAD_DIM), lambda h, i, j: (j, h)),
                  pl.BlockSpec((tk, HEAD_DIM), lambda h, i, j: (j, h))],
        out_specs=[pl.BlockSpec((tq, GW), lambda h, i, j: (i, h)),
                   pl.BlockSpec((1, GROUP, tq), lambda h, i, j: (h, 0, i))],
        out_shape=[jax.ShapeDtypeStruct((T, N_Q_HEADS * HEAD_DIM), F32),
                   jax.ShapeDtypeStruct((N_KV_HEADS, GROUP, T), F32)],
        scratch_shapes=[pltpu.VMEM((GROUP * tq, HEAD_DIM), BF16), pltpu.VMEM((1, GROUP * tq), F32),
                        pltpu.VMEM((1, GROUP * tq), F32), pltpu.VMEM((HEAD_DIM, GROUP * tq), F32),
                        pltpu.VMEM((tk, GROUP * tq), F32)],
        compiler_params=pltpu.CompilerParams(dimension_semantics=("parallel", "parallel", "arbitrary"),
                                             vmem_limit_bytes=ATTN_VMEM_LIMIT),
    )(q, k, v)


def _flash_bwd(q, k, v, do, lse, delta, name, tq=None, tk=None, token=None):
    T = q.shape[0]
    tq = tq or _pick(T, (ATTN_QUERY_ROWS, ROW))
    tk = tk or _pick(T, (1408, 768, 512, 256))
    nk = T // tk
    GW = GROUP * HEAD_DIM
    nt = (((1,), (1,)), ((), ()))
    extra = [] if token is None else [token]

    def body(q_ref, do_ref, k_ref, v_ref, lse_ref, dl_ref, *rest):
        dq_ref, dk_ref, dv_ref, qs_ref, dos_ref, dqt_ref = rest[len(extra):]
        qi = pl.program_id(1)
        ki = pl.program_id(2)

        @pl.when(ki == 0)
        def _():
            for g in range(GROUP):
                qs_ref[g * tq:(g + 1) * tq, :] = q_ref[:, g * HEAD_DIM:(g + 1) * HEAD_DIM]
                dos_ref[g * tq:(g + 1) * tq, :] = do_ref[:, g * HEAD_DIM:(g + 1) * HEAD_DIM]
            dqt_ref[...] = jnp.zeros(dqt_ref.shape, F32)

        kk = k_ref[...]
        vv = v_ref[...]

        def lanes(p):
            return slice(p * tq, (p + 1) * tq)

        def products(p):
            st = lax.dot_general(kk, qs_ref[lanes(p), :], nt, preferred_element_type=F32)
            dpt = lax.dot_general(vv, dos_ref[lanes(p), :], nt, preferred_element_type=F32)
            return st, dpt

        dk_c = jnp.zeros((tk, HEAD_DIM), F32)
        dv_c = jnp.zeros((tk, HEAD_DIM), F32)
        ahead = products(0)
        for p in range(GROUP):
            st, dpt = ahead
            if p + 1 < GROUP:
                ahead = products(p + 1)
            pt = jnp.exp2(st * (ATTN_SCALE * LOG2E) - lse_ref[0, p:p + 1, :])
            dst = ((pt * (dpt - dl_ref[0, p:p + 1, :])) * ATTN_SCALE).astype(BF16)
            dv_c = dv_c + jnp.dot(pt.astype(BF16), dos_ref[lanes(p), :], preferred_element_type=F32)
            dk_c = dk_c + jnp.dot(dst, qs_ref[lanes(p), :], preferred_element_type=F32)
            dqt_ref[:, lanes(p)] += lax.dot_general(kk, dst, (((0,), (0,)), ((), ())), preferred_element_type=F32)
        rows = pl.ds(pl.multiple_of(ki * tk, tk), tk)

        @pl.when(qi == 0)
        def _():
            dk_ref[rows, :] = dk_c
            dv_ref[rows, :] = dv_c

        @pl.when(qi > 0)
        def _():
            dk_ref[rows, :] += dk_c
            dv_ref[rows, :] += dv_c

        @pl.when(ki == nk - 1)
        def _():
            dqv = jnp.transpose(dqt_ref[...])
            for g in range(GROUP):
                dq_ref[:, g * HEAD_DIM:(g + 1) * HEAD_DIM] = dqv[g * tq:(g + 1) * tq, :]

    return pl.pallas_call(
        body, name=name, grid=(N_KV_HEADS, T // tq, nk),
        in_specs=[pl.BlockSpec((tq, GW), lambda h, i, j: (i, h)),
                  pl.BlockSpec((tq, GW), lambda h, i, j: (i, h)),
                  pl.BlockSpec((tk, HEAD_DIM), lambda h, i, j: (j, h)),
                  pl.BlockSpec((tk, HEAD_DIM), lambda h, i, j: (j, h)),
                  pl.BlockSpec((1, GROUP, tq), lambda h, i, j: (h, 0, i)),
                  pl.BlockSpec((1, GROUP, tq), lambda h, i, j: (h, 0, i))] +
                 [pl.BlockSpec(t.shape, lambda h, i, j: (0, 0)) for t in extra],
        out_specs=[pl.BlockSpec((tq, GW), lambda h, i, j: (i, h)),
                   pl.BlockSpec((T, HEAD_DIM), lambda h, i, j: (0, h)),
                   pl.BlockSpec((T, HEAD_DIM), lambda h, i, j: (0, h))],
        out_shape=[jax.ShapeDtypeStruct((T, N_Q_HEADS * HEAD_DIM), F32),
                   jax.ShapeDtypeStruct((T, N_KV_HEADS * HEAD_DIM), F32),
                   jax.ShapeDtypeStruct((T, N_KV_HEADS * HEAD_DIM), F32)],
        scratch_shapes=[pltpu.VMEM((GROUP * tq, HEAD_DIM), BF16), pltpu.VMEM((GROUP * tq, HEAD_DIM), BF16),
                        pltpu.VMEM((HEAD_DIM, GROUP * tq), F32)],
        compiler_params=pltpu.CompilerParams(dimension_semantics=("arbitrary", "arbitrary", "arbitrary"),
                                             vmem_limit_bytes=ATTN_VMEM_LIMIT),
    )(q, do, k, v, lse, delta, *extra)


def _gate_specs(D):
    w = D // 2
    first = (3 * D + (N_Q_HEADS + 2 * N_KV_HEADS) * HEAD_DIM) // w
    return [pl.BlockSpec((ROW, w), lambda i, c=first + j: (i, c)) for j in range(4)]


def _merge_fwd(o, P, conv_w, wbc, wba, wo, D, name):
    T = o.shape[0]
    nt = T // ROW
    w = D // 2
    conv_specs, conv_args = _conv_fwd_operands(P, conv_w, D)
    nc = len(conv_args)

    def body(*refs):
        o_ref, g0, g1, g2, g3, wbc_ref, wba_ref, wo_ref, yc_ref, a1_ref, a2_ref, z_ref, mo_ref = refs[nc:]
        yc_ref[...] = _conv_tile_fwd(refs[:nc], nt)
        a1 = jnp.dot(yc_ref[...], wbc_ref[...], preferred_element_type=F32)
        a2 = jnp.dot(o_ref[...].astype(BF16), wba_ref[...], preferred_element_type=F32)
        a1_ref[...] = a1.astype(BF16)
        a2_ref[...] = a2.astype(BF16)
        for j, (gc, ga) in enumerate(((g0, g2), (g1, g3))):
            sl = slice(j * w, (j + 1) * w)
            z = jax.nn.sigmoid(_f32(gc)) * a1[:, sl] + jax.nn.sigmoid(_f32(ga)) * a2[:, sl]
            z_ref[:, sl] = z.astype(BF16)
        mo_ref[...] = jnp.dot(z_ref[...], wo_ref[...], preferred_element_type=F32)

    return pl.pallas_call(
        body, name=name, grid=(T // ROW,),
        in_specs=conv_specs + [_row_spec(D)] + _gate_specs(D) + [_resident()] * 3,
        out_specs=[_row_spec(D)] * 5,
        out_shape=[jax.ShapeDtypeStruct((T, D), BF16), jax.ShapeDtypeStruct((T, D), BF16),
                   jax.ShapeDtypeStruct((T, D), BF16), jax.ShapeDtypeStruct((T, D), BF16),
                   jax.ShapeDtypeStruct((T, D), F32)],
        compiler_params=_params(("parallel",)),
    )(*conv_args, o, P, P, P, P, wbc, wba, wo)


def _merge_bwd(dmo, a1, a2, o, P, wbc, wba, wo, D, name):
    T = a1.shape[0]
    w = D // 2

    def body(dmo_ref, a1_ref, a2_ref, o_ref, g0, g1, g2, g3, wbc_ref, wba_ref, wo_ref,
             d1_ref, d2_ref, dg_ref, dyc_ref, dob_ref, dl_ref):
        dz = lax.dot_general(dmo_ref[...], wo_ref[...], _NT, preferred_element_type=F32)
        for j, (gc, ga) in enumerate(((g0, g2), (g1, g3))):
            sl = slice(j * w, (j + 1) * w)
            dzs = dz[:, sl]
            sc = jax.nn.sigmoid(_f32(gc))
            sa = jax.nn.sigmoid(_f32(ga))
            d1_ref[:, sl] = (dzs * sc).astype(BF16)
            d2_ref[:, sl] = (dzs * sa).astype(BF16)
            dg_ref[:, j * w:(j + 1) * w] = (dzs * a1_ref[:, sl].astype(F32) * (sc * (1.0 - sc))).astype(BF16)
            dg_ref[:, D + j * w:D + (j + 1) * w] = (dzs * a2_ref[:, sl].astype(F32) * (sa * (1.0 - sa))).astype(BF16)
        dyc_ref[...] = lax.dot_general(d1_ref[...], wbc_ref[...], _NT, preferred_element_type=F32)
        dov = lax.dot_general(d2_ref[...], wba_ref[...], _NT, preferred_element_type=F32)
        dob_ref[...] = dov.astype(BF16)
        prod = dov * o_ref[...]
        for h in range(N_Q_HEADS):
            d = jnp.sum(prod[:, h * HEAD_DIM:(h + 1) * HEAD_DIM], axis=1, keepdims=True)
            dl_ref[h // GROUP, (h % GROUP):(h % GROUP) + 1, :] = _to_row(d, ROW)

    return pl.pallas_call(
        body, name=name, grid=(T // ROW,),
        in_specs=[_row_spec(D)] * 4 + _gate_specs(D) + [_resident()] * 3,
        out_specs=[_row_spec(D), _row_spec(D), _row_spec(2 * D), _row_spec(D), _row_spec(D),
                   pl.BlockSpec((N_KV_HEADS, GROUP, ROW), lambda i: (0, 0, i))],
        out_shape=[jax.ShapeDtypeStruct((T, D), BF16), jax.ShapeDtypeStruct((T, D), BF16),
                   jax.ShapeDtypeStruct((T, 2 * D), BF16), jax.ShapeDtypeStruct((T, D), F32),
                   jax.ShapeDtypeStruct((T, D), BF16), jax.ShapeDtypeStruct((N_KV_HEADS, GROUP, T), F32)],
        compiler_params=_params(("parallel",)),
    )(dmo, a1, a2, o, P, P, P, P, wbc, wba, wo)


def _adamw_math(w, g, m, v):
    m = ADAM_B1 * m + (1.0 - ADAM_B1) * g
    v = ADAM_B2 * v + (1.0 - ADAM_B2) * (g * g)
    m_hat = m / (1.0 - ADAM_B1 ** ADAM_STEP)
    v_hat = v / (1.0 - ADAM_B2 ** ADAM_STEP)
    delta = -ADAM_LR * (m_hat / (jnp.sqrt(v_hat) + ADAM_EPS) + ADAM_WD * w)
    return delta, m, v


def _adamw(w, g, m, v, name):
    R, C = w.shape
    tr = _pick(R, tuple(t for t in (256, 128, 64, 32, 16, 8) if t * C * 4 <= ADAMW_BLOCK_BYTES))

    def body(w_ref, g_ref, m_ref, v_ref, d_ref, mo_ref, vo_ref):
        d, mn, vn = _adamw_math(w_ref[...], g_ref[...], m_ref[...], v_ref[...])
        d_ref[...] = d
        mo_ref[...] = mn
        vo_ref[...] = vn

    spec = pl.BlockSpec((tr, C), lambda i: (i, 0))
    return pl.pallas_call(
        body, name=name, grid=(R // tr,),
        in_specs=[spec] * 4, out_specs=[spec] * 3,
        out_shape=[jax.ShapeDtypeStruct((R, C), F32)] * 3,
        compiler_params=_params(("parallel",)),
    )(w, g, m, v)


def _norm_mix_in_fwd(xprev, branch, mods, g, gate, shift_idx, scale_idx, w_t, gq, gk, cos_t, sin_t, name):
    x_specs, x_args, (T, D) = _rows_operand(xprev)
    N = w_t.shape[0]
    QW = N_Q_HEADS * HEAD_DIM
    KW = N_KV_HEADS * HEAD_DIM
    q0, k0, v0 = 3 * D, 3 * D + QW, 3 * D + QW + KW
    edges = [0, D, 2 * D, q0, k0, v0 + KW] + list(range(v0 + KW + D, N + 1, D))
    assert edges[-1] == N

    def body(*refs):
        f_ref, m_ref, g_ref, w_ref, gq_ref, gk_ref, c_ref, s_ref = refs[len(x_args):len(x_args) + 8]
        xo_ref, h_ref, p_ref, qo_ref, ko_ref, vo_ref = refs[len(x_args) + 8:]
        m = m_ref[0]
        gate_idx, fac = gate
        x = _rows_tile(refs[:len(x_args)]) + (fac * m[gate_idx:gate_idx + 1, :]) * f_ref[...]
        xo_ref[...] = x
        hv = _norm_tile_fwd(x, m, g_ref[...], shift_idx, scale_idx)
        h_ref[...] = hv
        c = c_ref[...]
        s = s_ref[...]

        def head(xh, gain):
            inv = lax.rsqrt(jnp.mean(xh * xh, axis=-1, keepdims=True) + EPS)
            y = (xh * inv) * gain
            return y * c + _swap_halves(y) * s

        for lo, hi in zip(edges[:-1], edges[1:]):
            pb = lax.dot_general(hv, w_ref[lo:hi, :], _NT, preferred_element_type=F32).astype(BF16)
            p_ref[:, lo:hi] = pb
            if lo == q0:
                for h in range(N_Q_HEADS):
                    sl = slice(h * HEAD_DIM, (h + 1) * HEAD_DIM)
                    qo_ref[:, sl] = head(pb[:, sl].astype(F32), gq_ref[...]).astype(BF16)
            elif lo == k0:
                for h in range(N_KV_HEADS):
                    sl = slice(h * HEAD_DIM, (h + 1) * HEAD_DIM)
                    ko_ref[:, sl] = head(pb[:, sl].astype(F32), gk_ref[...]).astype(BF16)
                vo_ref[...] = pb[:, KW:2 * KW]

    return pl.pallas_call(
        body, name=name, grid=(T // ROW,),
        in_specs=x_specs + [_row_spec(D), _mods_spec(D), _vec_spec(1, D), _resident(),
                            _vec_spec(1, HEAD_DIM), _vec_spec(1, HEAD_DIM), _row_spec(HEAD_DIM), _row_spec(HEAD_DIM)],
        out_specs=[_row_spec(D), _row_spec(D), _row_spec(N), _row_spec(QW), _row_spec(KW), _row_spec(KW)],
        out_shape=[jax.ShapeDtypeStruct((T, D), F32), jax.ShapeDtypeStruct((T, D), BF16),
                   jax.ShapeDtypeStruct((T, N), BF16), jax.ShapeDtypeStruct((T, QW), BF16),
                   jax.ShapeDtypeStruct((T, KW), BF16), jax.ShapeDtypeStruct((T, KW), BF16)],
        compiler_params=_params(("parallel",)),
    )(*x_args, branch, mods, g, w_t, gq, gk, cos_t, sin_t)


def _mix_in_norm_bwd(dyc, dgt, P, conv_w, dq, dk, dv, gq, gk, cos_t, sin_t, w_t, x, dres, mods, g, shift_idx,
                     scale_idx, gate, branch, name):
    T, D = x.shape
    nt = T // ROW
    QW = N_Q_HEADS * HEAD_DIM
    KW = N_KV_HEADS * HEAD_DIM
    q0, g0 = 3 * D, 3 * D + QW + 2 * KW
    assert g0 + dgt.shape[1] == w_t.shape[0]
    conv_specs, conv_args = _conv_bwd_operands(P, dyc, conv_w, D)
    nc = len(conv_args)

    def body(*refs):
        (dg_ref, q_ref, k_ref, dq_ref, dk_ref, dv_ref, gq_ref, gk_ref, c_ref, s_ref,
         w_ref, x_ref, dr_ref, b_ref, m_ref, g_ref,
         dx_ref, db_ref, acc_ref, dc_ref, cacc_ref, o_ref, qacc_ref) = refs[nc:]
        _acc_init(acc_ref)
        _acc_init(cacc_ref)
        _acc_init(qacc_ref)
        _conv_tile_bwd(refs[:nc], dc_ref, cacc_ref, D, nt)
        dh = jnp.dot(dc_ref[...], w_ref[0:q0, :], preferred_element_type=F32)
        c = c_ref[...]
        s = s_ref[...]

        def head(xh, d, gain):
            dyv = d * c + _swap_halves(d * s)
            inv = lax.rsqrt(jnp.mean(xh * xh, axis=-1, keepdims=True) + EPS)
            xn = xh * inv
            dxn = dyv * gain
            dxh = inv * (dxn - xn * jnp.mean(dxn * xn, axis=-1, keepdims=True))
            return dxh, jnp.sum(dyv * xn, axis=0, keepdims=True)

        dgq = jnp.zeros((1, HEAD_DIM), F32)
        for h in range(N_Q_HEADS):
            sl = slice(h * HEAD_DIM, (h + 1) * HEAD_DIM)
            dxh, dgh = head(q_ref[:, sl].astype(F32), dq_ref[:, sl], gq_ref[...])
            o_ref[:, sl] = dxh.astype(BF16)
            dgq = dgq + dgh
        dh = dh + jnp.dot(dg_ref[...], w_ref[g0:, :], preferred_element_type=F32)
        dgk = jnp.zeros((1, HEAD_DIM), F32)
        for h in range(N_KV_HEADS):
            sl = slice(h * HEAD_DIM, (h + 1) * HEAD_DIM)
            dxh, dgh = head(k_ref[:, sl].astype(F32), dk_ref[:, sl], gk_ref[...])
            o_ref[:, QW + h * HEAD_DIM:QW + (h + 1) * HEAD_DIM] = dxh.astype(BF16)
            dgk = dgk + dgh
        o_ref[:, QW + KW:QW + 2 * KW] = dv_ref[...].astype(BF16)
        qacc_ref[0, 0:1, 0:HEAD_DIM] += dgq
        qacc_ref[0, 1:2, 0:HEAD_DIM] += dgk
        dh = dh + jnp.dot(o_ref[...], w_ref[q0:g0, :], preferred_element_type=F32)
        m = m_ref[0]
        dx = _norm_tile_bwd(x_ref[...], dh, dr_ref[...], m, g_ref[...], shift_idx, scale_idx, acc_ref)
        dx_ref[...] = dx
        db_ref[...] = _gate_tile_bwd(dx, b_ref[...], m, gate, acc_ref)

    return pl.pallas_call(
        body, name=name, grid=(T // ROW,),
        in_specs=conv_specs +
                 [_row_spec(dgt.shape[1]), _row_spec(QW, q0 // QW), _row_spec(KW, (q0 + QW) // KW),
                  _row_spec(QW), _row_spec(KW), _row_spec(KW), _vec_spec(1, HEAD_DIM), _vec_spec(1, HEAD_DIM),
                  _row_spec(HEAD_DIM), _row_spec(HEAD_DIM),
                  _resident(), _row_spec(D), _row_spec(D), _row_spec(D), _mods_spec(D), _vec_spec(1, D)],
        out_specs=[_row_spec(D), _row_spec(D), _acc_spec(D), _row_spec(q0), _acc_spec(D),
                   _row_spec(QW + 2 * KW), _acc_spec(D)],
        out_shape=[jax.ShapeDtypeStruct((T, D), F32), jax.ShapeDtypeStruct((T, D), BF16),
                   jax.ShapeDtypeStruct((2, ACC_ROWS, D), F32), jax.ShapeDtypeStruct((T, q0), BF16),
                   jax.ShapeDtypeStruct((2, ACC_ROWS, D), F32), jax.ShapeDtypeStruct((T, QW + 2 * KW), BF16),
                   jax.ShapeDtypeStruct((2, ACC_ROWS, D), F32)],
        compiler_params=_params(("arbitrary",)),
    )(*conv_args, dgt, P, P, dq, dk, dv, gq, gk, cos_t, sin_t, w_t, x, dres, branch, mods, g)


def _adamw_transposed(w, gt, m, v, name):
    R, C = w.shape
    tc = LANES

    def body(w_ref, g_ref, m_ref, v_ref, go_ref, d_ref, mo_ref, vo_ref):
        g = jnp.transpose(g_ref[...])
        d, mn, vn = _adamw_math(w_ref[...], g, m_ref[...], v_ref[...])
        go_ref[...] = g
        d_ref[...] = d
        mo_ref[...] = mn
        vo_ref[...] = vn

    spec = pl.BlockSpec((R, tc), lambda j: (0, j))
    return pl.pallas_call(
        body, name=name, grid=(C // tc,),
        in_specs=[spec, pl.BlockSpec((tc, R), lambda j: (j, 0)), spec, spec], out_specs=[spec] * 4,
        out_shape=[jax.ShapeDtypeStruct((R, C), F32)] * 4,
        compiler_params=_params(("parallel",)),
    )(w, gt, m, v)


class _NoExchange:
    def __init__(self, rest):
        self.rest = rest

    def rest_weights(self, after):
        return self.rest[0], lambda first, after2: self.rest

    def reduce_early(self, grads, tag):
        return None


def _local_step(xcat, target, mods, norm_g, final_g, gq, gk, conv_w, ffn1_w, hooks, rope):
    T, D = _rows_operand(xcat)[2]
    w1i, w1o = ffn1_w
    g1, g2, g3 = norm_g
    cos_t, sin_t = rope

    def after(value, token, name):
        return value if token is None else _after(value, token, name)

    _, h1, u1, s1, f1 = _norm_ffn_fwd(xcat, None, mods, g1, None, 0, 1, w1i, w1o, "f_ffn1")
    wi, more_weights = hooks.rest_weights(f1)
    x1, h2, P, qn, kn, vb = _norm_mix_in_fwd(xcat, f1, mods, g2, (2, 0.5), 3, 4, wi, gq, gk, cos_t, sin_t, "f_mix_in")
    wi, wbc, wba, wo, w2i, w2o = more_weights(wi, qn)
    o, lse = _flash_fwd(qn, kn, vb, "f_attn")
    yc, a1, a2, z, mo = _merge_fwd(o, P, conv_w, wbc, wba, wo, D, "f_merge")
    x2, h3, u2, s2, dx3, df2, acc_head = _norm_ffn_fwd(x1, mo, mods, g3, (5, 1.0), 6, 7, w2i, w2o, "f_ffn2",
                                                       head=(final_g, target))

    du2, dx2, dmo, acc_n3 = _ffn_norm_bwd(df2, u2, w2i, w2o, x2, dx3, mods, g3, 6, 7, (5, 1.0), mo, "b_ffn2")
    g_w2o = _grad_matmul(s2, df2, "b_ffn2_out_dw")
    g_w2i = _grad_matmul(du2, h3, "b_ffn2_in_dw")

    g_wo = _grad_matmul(z, dmo, "b_mix_out_dw")
    da1, da2, dgt, dyc, dob, delta = _merge_bwd(dmo, a1, a2, o, P, wbc, wba, wo, D, "b_merge")
    g_wbc = _grad_matmul(yc, da1, "b_branch_conv_dw")
    g_wba = _grad_matmul(o, da2, "b_branch_attn_dw")
    token_a = hooks.reduce_early([g_wbc, g_wba, g_wo, g_w2i, g_w2o], "a")
    dq, dk, dv = _flash_bwd(qn, kn, vb, dob, lse, delta, "b_attn", token=token_a)
    dx1, df1, acc_n2, dconv, acc_conv, dqkv, acc_qk = _mix_in_norm_bwd(
        dyc, dgt, P, conv_w, dq, dk, dv, gq, gk, cos_t, sin_t, wi, x1, dx2, mods, g2, 3, 4, (2, 0.5), f1, "b_mix_in")
    d_parts = (dconv, dqkv, dgt)
    g_wi = jnp.concatenate([_grad_matmul(dp, h2, f"b_mix_in_dw_{i}") for i, dp in enumerate(d_parts)], axis=0)
    g1_b = after(g1, hooks.reduce_early([g_wi], "b"), "after_rs_b")

    du1, grad_x, _, acc_n1 = _ffn_norm_bwd(df1, u1, w1i, w1o, xcat, dx1, mods, g1_b, 0, 1, None, None, "b_ffn1",
                                           skip_first_tile=True)
    g_w1o = _grad_matmul(s1, df1, "b_ffn1_out_dw")
    g_w1i = _grad_matmul(du1, h1, "b_ffn1_in_dw", token=hooks.reduce_early([g_w1o], "c"))

    grads = (g_w1i, g_w1o, g_wi, g_wbc, g_wba, g_wo, g_w2i, g_w2o)
    accs = (acc_head, acc_n3, acc_n2, acc_n1, acc_conv, acc_qk)
    return grad_x, grads, accs


def _place():
    return lax.axis_index("x"), lax.axis_index("y"), lax.axis_index("c")


def _other_chips(x, y):
    return [(1 - x, y), (x, 1 - y), (1 - x, 1 - y)]


def _allgather8(v, name):
    R, N = v.shape

    def body(v_ref, out_ref, send_sems, recv_sems, local_sem):
        x, y, c = _place()
        me, sibling = (x, y, c), (x, y, 1 - c)
        chips = _other_chips(x, y)

        def blk(px, py, pc):
            return out_ref.at[4 * px + 2 * py + pc]

        def copy(k, block, to, src=None):
            return pltpu.make_async_remote_copy(
                src_ref=blk(*block) if src is None else src, dst_ref=blk(*block),
                send_sem=send_sems.at[k], recv_sem=recv_sems.at[k], device_id=to, device_id_type=MESH)

        mine = pltpu.make_async_copy(v_ref, blk(*me), local_sem)
        mine.start()
        first = [copy(0, me, sibling, src=v_ref)]
        first += [copy(1 + j, me, (*chip, c), src=v_ref) for j, chip in enumerate(chips)]
        for cp in first:
            cp.start()
        passed = [copy(4 + j, (*chip, c), sibling) for j, chip in enumerate(chips)]
        for j, chip in enumerate(chips):
            copy(1 + j, (*chip, c), me).wait_recv()
            passed[j].start()
        copy(0, sibling, me).wait_recv()
        for j, chip in enumerate(chips):
            copy(4 + j, (*chip, 1 - c), me).wait_recv()
        for cp in first + passed:
            cp.wait_send()
        mine.wait()

    return pl.pallas_call(
        body, name=name,
        out_shape=jax.ShapeDtypeStruct((N_DEV, R, N), v.dtype),
        in_specs=[pl.BlockSpec(memory_space=pltpu.VMEM)],
        out_specs=pl.BlockSpec(memory_space=pltpu.VMEM),
        scratch_shapes=[pltpu.SemaphoreType.DMA((7,)), pltpu.SemaphoreType.DMA((7,)), pltpu.SemaphoreType.DMA],
        compiler_params=pltpu.CompilerParams(vmem_limit_bytes=VMEM_LIMIT),
    )(v)


def _any_specs(n):
    return [pl.BlockSpec(memory_space=pl.ANY)] * n


def _place_shard(w2, idx, transpose, name, token):
    if transpose:
        D, rs = w2.shape
        tr = LANES
        in_spec = pl.BlockSpec((D, tr), lambda i, idx: (0, i))
    else:
        rs, D = w2.shape
        tr = _pick(rs, (352, 256, 128, 64, 32, 16))
        in_spec = pl.BlockSpec((tr, D), lambda i, idx: (i, 0))
    steps = rs // tr

    def body(idx_ref, w_ref, t_ref, o_ref):
        v = w_ref[...]
        o_ref[...] = (jnp.transpose(v) if transpose else v).astype(BF16)

    return pl.pallas_call(
        body, name=name,
        grid_spec=pltpu.PrefetchScalarGridSpec(
            num_scalar_prefetch=1, grid=(steps,),
            in_specs=[in_spec, pl.BlockSpec(token.shape, lambda i, idx: (0, 0))],
            out_specs=pl.BlockSpec((tr, D), lambda i, idx: (idx[1] * steps + i, 0))),
        out_shape=jax.ShapeDtypeStruct((N_CHIPS * rs, D), BF16),
        compiler_params=_params(("arbitrary",)),
    )(idx, w2, token)


_HBM = pl.BlockSpec(memory_space=pltpu.HBM)
_SEM = pl.BlockSpec(memory_space=pltpu.SEMAPHORE)
_EFFECT = pltpu.SideEffectType.DATAFLOW_SIDE_EFFECTING


def _in_hbm(a):
    return pltpu.with_memory_space_constraint(a, pltpu.HBM)


def _split_copies(n, per, make):
    def start(nbuf, name, bufs, after=None):
        extra = [] if after is None else [after]

        def body(*refs):
            ins = refs[:nbuf]
            send_sems, recv_sems = refs[nbuf + len(extra)], refs[nbuf + len(extra) + 1]
            token = refs[-1]
            for t in range(n):
                for j in range(per):
                    make(ins, t, j, send_sems.at[per * t + j], recv_sems.at[per * t + j]).start()
            token[...] = jnp.zeros(token.shape, token.dtype)

        out = pl.pallas_call(
            body, name=name,
            out_shape=(pltpu.SemaphoreType.DMA((per * n,)), pltpu.SemaphoreType.DMA((per * n,)),
                       *[pltpu.HBM(b.shape, b.dtype) for b in bufs], jax.ShapeDtypeStruct((8, 128), F32)),
            in_specs=[_HBM] * nbuf + [pl.BlockSpec(memory_space=pl.ANY)] * len(extra),
            out_specs=(_SEM, _SEM, *[_HBM] * nbuf, pl.BlockSpec(memory_space=pltpu.VMEM)),
            input_output_aliases={i: 2 + i for i in range(nbuf)},
            compiler_params=pltpu.CompilerParams(has_side_effects=_EFFECT),
        )(*[_in_hbm(b) for b in bufs], *extra)
        return out[0], out[1], list(out[2:2 + nbuf]), out[-1]

    def wait(nbuf, name, send_sems, recv_sems, bufs, after, tensors=range(n)):
        afters = list(after) if isinstance(after, (list, tuple)) else [after]

        def body(*refs):
            ins = refs[:nbuf]
            ss, rs = refs[nbuf], refs[nbuf + 1]
            for t in tensors:
                for j in range(per):
                    cp = make(ins, t, j, ss.at[per * t + j], rs.at[per * t + j])
                    cp.wait_send()
                    cp.wait_recv()

        return pl.pallas_call(
            body, name=name,
            out_shape=[pltpu.HBM(b.shape, b.dtype) for b in bufs],
            in_specs=[_HBM] * nbuf + [_SEM, _SEM] + [pl.BlockSpec(memory_space=pl.ANY)] * len(afters),
            out_specs=[_HBM] * nbuf,
            input_output_aliases={i: i for i in range(nbuf)},
            compiler_params=pltpu.CompilerParams(has_side_effects=_EFFECT),
        )(*bufs, send_sems, recv_sems, *afters)

    return start, wait


RS_PEERS = N_DEV - 1


def _reduce_exchange_split(grads):
    n = len(grads)

    def make(bufs, t, j, send_sem, recv_sem):
        x, y, c = _place()
        chip = (x, y) if j == 6 else _other_chips(x, y)[j % 3]
        core = c if j < 3 else 1 - c
        half = grads[t].shape[0] // (2 * N_CHIPS)
        piece = bufs[t].at[pl.ds((2 * (2 * chip[0] + chip[1]) + core) * half, half), :]
        return pltpu.make_async_remote_copy(src_ref=piece, dst_ref=bufs[n + t].at[j], send_sem=send_sem,
                                            recv_sem=recv_sem, device_id=(*chip, core), device_id_type=MESH)

    return _split_copies(n, RS_PEERS, make)


def _reduce_sum(g, landed, idx, name):
    _, half, D = landed.shape
    g4 = g.reshape(N_CHIPS, 2, half, D)
    tr = _pick(half, (416, 352, 128))
    steps = half // tr

    def body(idx_ref, g_ref, l_ref, o_ref):
        acc = g_ref[0, 0].astype(F32)
        for j in range(RS_PEERS):
            acc = acc + l_ref[j].astype(F32)
        o_ref[...] = acc

    return pl.pallas_call(
        body, name=name,
        grid_spec=pltpu.PrefetchScalarGridSpec(
            num_scalar_prefetch=1, grid=(steps,),
            in_specs=[pl.BlockSpec((1, 1, tr, D), lambda i, idx: (idx[1], idx[0], i, 0)),
                      pl.BlockSpec((RS_PEERS, tr, D), lambda i, idx: (0, i, 0))],
            out_specs=pl.BlockSpec((tr, D), lambda i, idx: (idx[0] * steps + i, 0))),
        out_shape=jax.ShapeDtypeStruct((2 * half, D), F32),
        compiler_params=_params(("arbitrary",)),
    )(idx, g4, landed)


def _weights_gather_split(fulls):
    def make(bufs, t, j, send_sem, recv_sem):
        x, y, c = _place()
        chip = _other_chips(x, y)[j]
        rs = fulls[t].shape[0] // N_CHIPS
        rows = bufs[t].at[pl.ds((2 * x + y) * rs + c * (rs // 2), rs // 2), :]
        return pltpu.make_async_remote_copy(src_ref=rows, dst_ref=rows, send_sem=send_sem, recv_sem=recv_sem,
                                            device_id=(*chip, c), device_id_type=MESH)

    return _split_copies(len(fulls), 3, make)


def _weights_pass_on(fulls, name):
    n = len(fulls)

    def body(*refs):
        full = refs[n:2 * n]
        send_sems, recv_sems = refs[2 * n:]
        x, y, c = _place()
        chips = _other_chips(x, y)

        def copy(t, j, h):
            rs = fulls[t].shape[0] // N_CHIPS
            px, py = chips[j]
            rows = full[t].at[pl.ds((2 * px + py) * rs + h * (rs // 2), rs // 2), :]
            return pltpu.make_async_remote_copy(src_ref=rows, dst_ref=rows, send_sem=send_sems.at[3 * t + j],
                                                recv_sem=recv_sems.at[3 * t + j], device_id=(x, y, 1 - c),
                                                device_id_type=MESH)

        for t in range(n):
            for j in range(3):
                copy(t, j, c).start()
        for t in range(n):
            for j in range(3):
                copy(t, j, 1 - c).wait_recv()
        for t in range(n):
            for j in range(3):
                copy(t, j, c).wait_send()

    return pl.pallas_call(
        body, name=name,
        out_shape=[jax.ShapeDtypeStruct(f.shape, f.dtype) for f in fulls],
        in_specs=_any_specs(n), out_specs=_any_specs(n),
        input_output_aliases={t: t for t in range(n)},
        scratch_shapes=[pltpu.SemaphoreType.DMA((3 * n,)), pltpu.SemaphoreType.DMA((3 * n,))],
    )(*fulls)


def _after(value, token, name):
    def body(v_ref, t_ref, o_ref):
        o_ref[...] = v_ref[...]

    return pl.pallas_call(
        body, name=name, out_shape=jax.ShapeDtypeStruct(value.shape, value.dtype),
        in_specs=_whole(2), out_specs=pl.BlockSpec(memory_space=pltpu.VMEM),
    )(value, token)


def _pair_swap(shards, name):
    n = len(shards)

    def body(*refs):
        full = refs[n:2 * n]
        send_sems, recv_sems = refs[2 * n:]
        x, y, c = _place()

        def half(t, h):
            rows = shards[t].shape[0] // 2
            return full[t].at[pl.ds(h * rows, rows), :]

        def copy(t, h):
            return pltpu.make_async_remote_copy(src_ref=half(t, h), dst_ref=half(t, h), send_sem=send_sems.at[t],
                                                recv_sem=recv_sems.at[t], device_id=(x, y, 1 - c),
                                                device_id_type=MESH)

        for t in range(n):
            copy(t, c).start()
        for t in range(n):
            copy(t, 1 - c).wait_recv()
        for t in range(n):
            copy(t, c).wait_send()

    return pl.pallas_call(
        body, name=name,
        out_shape=[jax.ShapeDtypeStruct(a.shape, a.dtype) for a in shards],
        in_specs=_any_specs(n), out_specs=_any_specs(n),
        input_output_aliases={t: t for t in range(n)},
        scratch_shapes=[pltpu.SemaphoreType.DMA((n,)), pltpu.SemaphoreType.DMA((n,))],
    )(*shards)


def _gather_begin(fulls, tag):
    start, wait = _weights_gather_split(fulls)
    send_sems, recv_sems, bufs, token = start(len(fulls), f"ag_{tag}_start", fulls)
    return (wait, send_sems, recv_sems, bufs), token


def _gather_end(state, after, tag):
    wait, send_sems, recv_sems, bufs = state
    landed = wait(len(bufs), f"ag_{tag}_wait", send_sems, recv_sems, bufs, after)
    return _weights_pass_on(landed, f"ag_{tag}_pass_on")


class _Exchanges:
    def __init__(self, fulls_rest, idx):
        self.idx = idx
        self._rest, self.token = _gather_begin(fulls_rest, "rest")
        self._early = []

    def rest_weights(self, after):
        wait, send_sems, recv_sems, bufs = self._rest
        n = len(bufs)
        landed = wait(n, "ag_rest_wait_0", send_sems, recv_sems, bufs, after, tensors=(0,))
        (first,) = _weights_pass_on(landed[:1], "ag_rest_pass_on_0")

        def more(first, after2):
            done = wait(n, "ag_rest_wait_1", send_sems, recv_sems, [first, *landed[1:]], after2, tensors=range(1, n))
            return [done[0], *_weights_pass_on(done[1:], "ag_rest_pass_on_1")]

        return first, more

    def reduce_early(self, grads, tag, token=None):
        zones = [lax.empty((RS_PEERS, g.shape[0] // (2 * N_CHIPS), g.shape[1]), g.dtype) for g in grads]
        start, wait = _reduce_exchange_split(grads)
        send_sems, recv_sems, bufs, token = start(2 * len(grads), "rs_start_" + tag, list(grads) + zones, token)
        self._early.append((tag, wait, send_sems, recv_sems, bufs))
        return token

    def finish(self, tags, after):
        halves = []
        for tag, wait, send_sems, recv_sems, bufs in self._early:
            if tag in tags:
                n = len(bufs) // 2
                done = wait(len(bufs), "rs_wait_" + tag, send_sems, recv_sems, bufs, after)
                halves += [_reduce_sum(g, l, self.idx, f"rs_sum_{tag}{t}")
                           for t, (g, l) in enumerate(zip(done[:n], done[n:]))]
        return halves


N_MOD = 9
PACK_HEAD, PACK_N3, PACK_N2, PACK_N1, PACK_CONV, PACK_QK = 0, 16, 32, 48, 64, 80
MOD_SRC = ((PACK_N1, 0), (PACK_N1, 1), (PACK_N2, 3), (PACK_N2, 0), (PACK_N2, 1),
           (PACK_N3, 3), (PACK_N3, 0), (PACK_N3, 1), (PACK_HEAD, 2))
CTX_ROW = 8


def _silu(v):
    return v * jax.nn.sigmoid(v)


def _whole(n):
    return [pl.BlockSpec(memory_space=pltpu.VMEM)] * n


def _mod_rows(cin, w_sh, b_sh, name):
    def body(c_ref, w_ref, b_ref, o_ref):
        a = _silu(c_ref[...]).astype(BF16)
        o_ref[...] = jnp.dot(a, w_ref[...].astype(BF16), preferred_element_type=F32) + b_ref[...]

    return pl.pallas_call(
        body, name=name, out_shape=jax.ShapeDtypeStruct((cin.shape[0], w_sh.shape[1]), F32),
        in_specs=_whole(3), out_specs=pl.BlockSpec(memory_space=pltpu.VMEM),
        compiler_params=pltpu.CompilerParams(vmem_limit_bytes=VMEM_LIMIT),
    )(cin, w_sh, b_sh)


def _small_reduce(gathered, name):
    _, _, D = gathered.shape

    def body(g_ref, loss_ref, db_ref, gn_ref, cv_ref, qk_ref, dm_ref):
        tot = g_ref[0]
        for r in range(1, N_DEV):
            tot = tot + g_ref[r]

        def both(block, row):
            return tot[block + row:block + row + 1, :] + tot[block + 8 + row:block + 8 + row + 1, :]

        loss = jnp.sum(both(PACK_HEAD, 0), axis=1, keepdims=True)
        loss_ref[...] = jnp.broadcast_to(loss, loss_ref.shape)
        db_ref[...] = jnp.zeros(db_ref.shape, F32)
        dm_ref[...] = jnp.zeros(dm_ref.shape, F32)
        for j, (block, row) in enumerate(MOD_SRC):
            db_ref[j:j + 1, :] = both(block, row)
            dm_ref[CTX_ROW, j:j + 1, :] = tot[block + row:block + row + 1, :]
            for r in range(N_DEV):
                dm_ref[r, j:j + 1, :] = g_ref[r, block + 8 + row:block + 8 + row + 1, :]
        gn_ref[...] = jnp.zeros(gn_ref.shape, F32)
        gn_ref[0:1, :] = both(PACK_N1, 2)
        gn_ref[8:9, :] = both(PACK_N2, 2)
        gn_ref[16:17, :] = both(PACK_N3, 2)
        gn_ref[24:25, :] = both(PACK_HEAD, 1)
        cv_ref[...] = jnp.zeros(cv_ref.shape, F32)
        for r in range(3):
            cv_ref[r:r + 1, :] = both(PACK_CONV, r)
        qk_ref[...] = jnp.zeros(qk_ref.shape, F32)
        qk_ref[0:1, 0:HEAD_DIM] = both(PACK_QK, 0)[:, 0:HEAD_DIM]
        qk_ref[0:1, HEAD_DIM:2 * HEAD_DIM] = both(PACK_QK, 1)[:, 0:HEAD_DIM]

    return pl.pallas_call(
        body, name=name,
        out_shape=[jax.ShapeDtypeStruct((8, 128), F32), jax.ShapeDtypeStruct((16, D), F32),
                   jax.ShapeDtypeStruct((32, D), F32), jax.ShapeDtypeStruct((8, D), F32),
                   jax.ShapeDtypeStruct((8, D), F32), jax.ShapeDtypeStruct((16, 16, D), F32)],
        in_specs=_whole(1), out_specs=_whole(6),
        compiler_params=pltpu.CompilerParams(vmem_limit_bytes=VMEM_LIMIT),
    )(gathered)


def _wmod_grad(cin, dm_sh, w_sh, name):
    def body(c_ref, d_ref, w_ref, gw_ref, cp_ref):
        a = _silu(c_ref[...]).astype(BF16)
        d = d_ref[...].astype(BF16)
        gw_ref[...] = lax.dot_general(a, d, (((0,), (0,)), ((), ())), preferred_element_type=F32)
        cp_ref[...] = lax.dot_general(d, w_ref[...].astype(BF16), (((1,), (1,)), ((), ())),
                                      preferred_element_type=F32)

    return pl.pallas_call(
        body, name=name,
        out_shape=[jax.ShapeDtypeStruct(w_sh.shape, F32), jax.ShapeDtypeStruct(cin.shape, F32)],
        in_specs=_whole(3), out_specs=_whole(2),
        compiler_params=pltpu.CompilerParams(vmem_limit_bytes=VMEM_LIMIT),
    )(cin, dm_sh, w_sh)


def _cctx_grad(parts, c_ctx8, name):
    def body(p_ref, c_ref, o_ref):
        tot = p_ref[0] + p_ref[2] + p_ref[4] + p_ref[6]
        cv = c_ref[...]
        sig = jax.nn.sigmoid(cv)
        rows = lax.broadcasted_iota(jnp.int32, tot.shape, 0)
        o_ref[...] = jnp.where(rows == 0, tot * (sig * (1.0 + cv * (1.0 - sig))), 0.0)

    return pl.pallas_call(
        body, name=name, out_shape=jax.ShapeDtypeStruct(c_ctx8.shape, F32),
        in_specs=_whole(2), out_specs=pl.BlockSpec(memory_space=pltpu.VMEM),
    )(parts, c_ctx8)


def _pad_rows(a, rows):
    return jnp.pad(a, ((0, rows - a.shape[0]), (0, 0)))


def _pack_small(c_ctx, b_mod, n1, n2, n3, final_g, gq, gk, conv_sh, D):
    misc = jnp.concatenate([gq, gk, conv_sh.reshape(1, -1)], axis=1)
    return jnp.concatenate([_pad_rows(c_ctx[None], 8), _pad_rows(b_mod.reshape(N_MOD, D), 16), _pad_rows(n1, 8),
                            _pad_rows(n2, 8), _pad_rows(n3, 8), _pad_rows(final_g[None], 8), _pad_rows(misc, 8)], axis=0)


def _unpack_small(p, D, conv_shape):
    misc = p[56:57]
    return dict(c_ctx=p[0], b_mod=p[8:8 + N_MOD].reshape(1, N_MOD * D), norm1_g=p[24:25], norm2_g=p[32:33],
                norm3_g=p[40:41], final_g=p[48], q_norm_g=misc[:, 0:HEAD_DIM], k_norm_g=misc[:, HEAD_DIM:2 * HEAD_DIM],
                conv_w=misc[:, 2 * HEAD_DIM:].reshape(conv_shape))


WEIGHT_ORDER = ("c_ctx", "w_mod", "b_mod", "norm1_g", "norm2_g", "norm3_g", "ffn1_w_in", "ffn1_w_out", "w_in",
                "conv_w", "q_norm_g", "k_norm_g", "w_branch_conv", "w_branch_attn", "w_out", "ffn2_w_in",
                "ffn2_w_out", "final_g")
BIG = ("ffn1_w_in", "ffn1_w_out", "w_in", "w_branch_conv", "w_branch_attn", "w_out", "ffn2_w_in", "ffn2_w_out")
COLUMN_SHARDED = ("ffn1_w_in", "w_in", "ffn2_w_in")


def kernel(x, c, ctx, c_ctx, w_mod, b_mod, norm1_g, norm2_g, norm3_g, ffn1_w_in, ffn1_w_out, w_in, conv_w, q_norm_g, k_norm_g, w_branch_conv, w_branch_attn, w_out, ffn2_w_in, ffn2_w_out, final_g, loss_target, m_c_ctx, m_w_mod, m_b_mod, m_norm1_g, m_norm2_g, m_norm3_g, m_ffn1_w_in, m_ffn1_w_out, m_w_in, m_conv_w, m_q_norm_g, m_k_norm_g, m_w_branch_conv, m_w_branch_attn, m_w_out, m_ffn2_w_in, m_ffn2_w_out, m_final_g, v_c_ctx, v_w_mod, v_b_mod, v_norm1_g, v_norm2_g, v_norm3_g, v_ffn1_w_in, v_ffn1_w_out, v_w_in, v_conv_w, v_q_norm_g, v_k_norm_g, v_w_branch_conv, v_w_branch_attn, v_w_out, v_ffn2_w_in, v_ffn2_w_out, v_final_g):
    w = dict(c_ctx=c_ctx, w_mod=w_mod, b_mod=b_mod, norm1_g=norm1_g, norm2_g=norm2_g, norm3_g=norm3_g,
             ffn1_w_in=ffn1_w_in, ffn1_w_out=ffn1_w_out, w_in=w_in, conv_w=conv_w, q_norm_g=q_norm_g,
             k_norm_g=k_norm_g, w_branch_conv=w_branch_conv, w_branch_attn=w_branch_attn, w_out=w_out,
             ffn2_w_in=ffn2_w_in, ffn2_w_out=ffn2_w_out, final_g=final_g)
    m = dict(c_ctx=m_c_ctx, w_mod=m_w_mod, b_mod=m_b_mod, norm1_g=m_norm1_g, norm2_g=m_norm2_g, norm3_g=m_norm3_g,
             ffn1_w_in=m_ffn1_w_in, ffn1_w_out=m_ffn1_w_out, w_in=m_w_in, conv_w=m_conv_w, q_norm_g=m_q_norm_g,
             k_norm_g=m_k_norm_g, w_branch_conv=m_w_branch_conv, w_branch_attn=m_w_branch_attn, w_out=m_w_out,
             ffn2_w_in=m_ffn2_w_in, ffn2_w_out=m_ffn2_w_out, final_g=m_final_g)
    v = dict(c_ctx=v_c_ctx, w_mod=v_w_mod, b_mod=v_b_mod, norm1_g=v_norm1_g, norm2_g=v_norm2_g, norm3_g=v_norm3_g,
             ffn1_w_in=v_ffn1_w_in, ffn1_w_out=v_ffn1_w_out, w_in=v_w_in, conv_w=v_conv_w, q_norm_g=v_q_norm_g,
             k_norm_g=v_k_norm_g, w_branch_conv=v_w_branch_conv, w_branch_attn=v_w_branch_attn, w_out=v_w_out,
             ffn2_w_in=v_ffn2_w_in, ffn2_w_out=v_ffn2_w_out, final_g=v_final_g)

    xi, yi, ci = _place()
    dev = 4 * xi + 2 * yi + ci
    shard = 2 * xi + yi
    idx = jnp.stack([ci, shard, 2 * (1 - xi) + yi, 2 * xi + (1 - yi), 2 * (1 - xi) + (1 - yi)]).astype(jnp.int32)
    D = x.shape[-1]
    ctx_len = ctx.shape[1]
    assert ctx_len == ROW and c.shape == (1, D)
    mcols = w_mod.shape[2]
    ccols = conv_w.shape[2]

    def place(names, token):
        fulls = []
        for n in names:
            fulls.append(_place_shard(w[n][0], idx, n in COLUMN_SHARDED, "place_" + n, token))
            token = fulls[-1][:16, :HEAD_DIM]
        return fulls

    ffn1_gather, ffn1_token = _gather_begin(place(BIG[:2], c), "ffn1")
    fulls_rest = place(BIG[2:], ffn1_token)

    rope = _rope_tables(ctx_len, x.shape[1])
    c8 = jnp.broadcast_to(c, (8, D))
    for token, name in ((fulls_rest[-1][:16, :HEAD_DIM], "after_place"), (rope[0], "after_rope_cos"),
                        (rope[1], "after_rope_sin")):
        c8 = _after(c8, token, name)
    c_all = _allgather8(c8, "ag_c")[:, 0, :]
    cin = jnp.concatenate([c_all, _pad_rows(c_ctx[None], 8)], axis=0)
    b_sh = lax.dynamic_slice(b_mod, (0, shard * mcols), (1, mcols))
    mod_sh = _mod_rows(cin, w_mod[0], b_sh, "mod_rows")
    conv_rows = jnp.pad(conv_w[0], ((0, 8 - conv_w.shape[1]), (0, mcols - ccols)))
    mod_all = _allgather8(jnp.concatenate([mod_sh, conv_rows], axis=0), "ag_mod")
    mod_full = jnp.concatenate([mod_all[2 * s, :16] for s in range(N_CHIPS)], axis=1)
    conv_full = jnp.concatenate([mod_all[2 * s, 16:16 + conv_w.shape[1], :ccols] for s in range(N_CHIPS)], axis=1)
    mod_lat = lax.dynamic_slice(mod_full, (dev, 0), (1, N_MOD * D)).reshape(N_MOD, D)
    mod_ctx = mod_full[CTX_ROW].reshape(N_MOD, D)
    mods = jnp.stack([_pad_rows(mod_ctx, 16), _pad_rows(mod_lat, 16)])

    ffn1_w = _gather_end(ffn1_gather, mods, "ffn1")
    hooks = _Exchanges(fulls_rest, idx)

    xcat = (ctx[0], x[0])
    norm1_first = _after(norm1_g, hooks.token, "after_ag_rest")
    grad_x, grads, accs = _local_step(xcat, loss_target[0], mods, (norm1_first, norm2_g, norm3_g), final_g[None],
                                      q_norm_g, k_norm_g, conv_full, ffn1_w, hooks, rope)
    g = {}

    pack = jnp.concatenate([a.reshape(2 * ACC_ROWS, D) for a in accs], axis=0)
    gathered = _allgather8(pack, "ag_small")
    loss8, db_mod, g_norms, g_conv, g_qk, dm = _small_reduce(gathered, "small_reduce")
    dm_sh = lax.dynamic_slice(dm[:, :N_MOD, :].reshape(16, N_MOD * D), (0, shard * mcols), (16, mcols))
    g_wmod, cpart = _wmod_grad(cin, dm_sh, w_mod[0], "wmod_grad")
    g["w_mod"] = g_wmod[None]
    cparts = _allgather8(cpart[CTX_ROW:CTX_ROW + 8], "ag_cctx")
    g_cctx = _cctx_grad(cparts, _pad_rows(c_ctx[None], 8), "cctx_grad")
    g_conv_sh = lax.dynamic_slice(g_conv, (0, shard * ccols), (conv_w.shape[1], ccols))
    g_misc = jnp.concatenate([g_qk[0:1, 0:2 * HEAD_DIM], g_conv_sh.reshape(1, -1)], axis=1)
    g_pack = jnp.concatenate([g_cctx, db_mod, g_norms, _pad_rows(g_misc, 8)], axis=0)

    def packed(p):
        return _pack_small(p["c_ctx"], p["b_mod"], p["norm1_g"], p["norm2_g"], p["norm3_g"], p["final_g"],
                           p["q_norm_g"], p["k_norm_g"], p["conv_w"][0], D)

    d_pack, m_pack, v_pack = _adamw(packed(w), g_pack, packed(m), packed(v), "adamw_small")

    g.update(_unpack_small(g_pack, D, conv_w.shape))
    delta = _unpack_small(d_pack, D, conv_w.shape)
    new_m = _unpack_small(m_pack, D, conv_w.shape)
    new_v = _unpack_small(v_pack, D, conv_w.shape)

    def update(n, g2):
        if n in COLUMN_SHARDED:
            g2, d2, m2, v2 = _adamw_transposed(w[n][0], g2, m[n][0], v[n][0], "adamw_" + n)
        else:
            d2, m2, v2 = _adamw(w[n][0], g2, m[n][0], v[n][0], "adamw_" + n)
        g[n], delta[n], new_m[n], new_v[n] = g2[None], d2[None], m2[None], v2[None]
        return v2

    token_d = hooks.reduce_early([grads[0]], "d", token=d_pack[:8, :HEAD_DIM])
    h_wbc, h_wba, h_wo, h_w2i, h_w2o, h_wi, h_w1o = hooks.finish("abc", token_d)
    done = _pair_swap([h_w1o, h_wi, h_wbc, h_wba, h_wo, h_w2i, h_w2o], "rs_pair_swap")
    updated = [update("w_mod", g_wmod)] + [update(n, r) for n, r in zip(BIG[1:], done)]
    (h_w1i,) = hooks.finish("d", updated)
    update(BIG[0], _pair_swap([h_w1i], "rs_pair_swap_d")[0])

    loss = loss8[0, 0]
    return (loss, grad_x[None], *[g[n] for n in WEIGHT_ORDER], *[delta[n] for n in WEIGHT_ORDER],
            *[new_m[n] for n in WEIGHT_ORDER], *[new_v[n] for n in WEIGHT_ORDER])
```

```python
import functools

import jax
import jax.numpy as jnp
from jax import lax
from jax.experimental import pallas as pl
from jax.experimental.pallas import tpu as pltpu

F32 = jnp.float32
BF16 = jnp.bfloat16

HEAD_DIM = 128
N_Q_HEADS = 8
N_KV_HEADS = 2
GROUP = N_Q_HEADS // N_KV_HEADS
GRID_W = 64
ROPE_THETA = 10000.0
EPS = 1e-6
ATTN_SCALE = HEAD_DIM ** -0.5

ADAM_LR = 0.001
ADAM_B1 = 0.9
ADAM_B2 = 0.999
ADAM_EPS = 1e-08
ADAM_WD = 0.01
ADAM_STEP = 10

LANES = 128
ROW = 256
HALO = 16
ACC_ROWS = 8
N_CHIPS = 4
N_DEV = 8
MESH = pl.DeviceIdType.MESH
VMEM_LIMIT = 48 * 1024 * 1024
ADAMW_BLOCK_BYTES = 1024 * 1024


def _pick(n, prefs):
    for p in prefs:
        if n % p == 0:
            return p
    return n


def _params(sem):
    return pltpu.CompilerParams(dimension_semantics=sem, vmem_limit_bytes=VMEM_LIMIT)


def _stream(i):
    return jnp.minimum(i, 1)


def _grad_matmul(a, b, name, token=None):
    (T, M), (T2, N) = a.shape, b.shape
    assert T == T2, (a.shape, b.shape)
    tm = _pick(M, (1664, 1408, 1024, 512, 256, 128))
    tk = _pick(T, (2816, 1408, 768, 512, 256))
    nk = T // tk
    extra = [] if token is None else [token]

    def body(a_ref, b_ref, *rest):
        o_ref, acc_ref = rest[len(extra):]
        p = lax.dot_general(a_ref[...].astype(BF16), b_ref[...].astype(BF16), (((0,), (0,)), ((), ())),
                            preferred_element_type=F32)
        k = pl.program_id(1)

        @pl.when(k == 0)
        def _():
            acc_ref[...] = p

        @pl.when(k > 0)
        def _():
            acc_ref[...] += p

        @pl.when(k == nk - 1)
        def _():
            o_ref[...] = acc_ref[...].astype(BF16)

    return pl.pallas_call(
        body, name=name, grid=(M // tm, nk),
        in_specs=[pl.BlockSpec((tk, tm), lambda i, k: (k, i)), pl.BlockSpec((tk, N), lambda i, k: (k, 0))] +
                 [pl.BlockSpec(t.shape, lambda i, k: (0, 0)) for t in extra],
        out_specs=pl.BlockSpec((tm, N), lambda i, k: (i, 0)),
        out_shape=jax.ShapeDtypeStruct((M, N), BF16),
        scratch_shapes=[pltpu.VMEM((tm, N), F32)],
        compiler_params=_params(("parallel", "arbitrary")),
    )(a, b, *extra)


def _row_spec(width, col=0):
    return pl.BlockSpec((ROW, width), lambda i, col=col: (i, col))


def _mods_spec(D):
    return pl.BlockSpec((1, 16, D), lambda i: (_stream(i), 0, 0))


def _acc_spec(D):
    return pl.BlockSpec((1, ACC_ROWS, D), lambda i: (_stream(i), 0, 0))


def _vec_spec(rows, D):
    return pl.BlockSpec((rows, D), lambda i: (0, 0))


def _acc_init(acc_ref):
    i = pl.program_id(0)

    @pl.when(i <= 1)
    def _():
        acc_ref[...] = jnp.zeros_like(acc_ref)


def _acc_add(acc_ref, row, val):
    acc_ref[0, row:row + 1, :] += jnp.sum(val, axis=0, keepdims=True)


def _rows_operand(x):
    if not isinstance(x, tuple):
        return [_row_spec(x.shape[1])], [x], x.shape
    ctx, lat = x
    D = lat.shape[1]
    assert ctx.shape == (ROW, D)
    specs = [pl.BlockSpec((ROW, D), lambda i: (0, 0)), pl.BlockSpec((ROW, D), lambda i: (jnp.maximum(i - 1, 0), 0))]
    return specs, [ctx, lat], (ROW + lat.shape[0], D)


def _rows_tile(refs):
    if len(refs) == 1:
        return refs[0][...]
    return jnp.where(pl.program_id(0) == 0, refs[0][...], refs[1][...])


def _norm_tile_fwd(x, m, g, shift_idx, scale_idx):
    inv = lax.rsqrt(jnp.mean(x * x, axis=-1, keepdims=True) + EPS)
    y = (x * inv) * g
    return (y * (1.0 + m[scale_idx:scale_idx + 1, :]) + m[shift_idx:shift_idx + 1, :]).astype(BF16)


def _norm_tile_bwd(x, dh, dres, m, g, shift_idx, scale_idx, acc_ref):
    inv = lax.rsqrt(jnp.mean(x * x, axis=-1, keepdims=True) + EPS)
    xn = x * inv
    dy = dh * (1.0 + m[scale_idx:scale_idx + 1, :])
    dxn = dy * g
    _acc_add(acc_ref, 0, dh)
    _acc_add(acc_ref, 1, dh * (xn * g))
    _acc_add(acc_ref, 2, dy * xn)
    return inv * (dxn - xn * jnp.mean(dxn * xn, axis=-1, keepdims=True)) + dres


def _gate_tile_bwd(dx, branch, m, gate, acc_ref):
    gate_idx, fac = gate
    _acc_add(acc_ref, 3, fac * dx * branch)
    return ((fac * m[gate_idx:gate_idx + 1, :]) * dx).astype(BF16)


_NT = (((1,), (1,)), ((), ()))


def _ffn_chunk(F):
    return _pick(F, (2816, 1408, 512, 256, 128))


def _resident():
    return pl.BlockSpec(memory_space=pltpu.VMEM)


def _ffn_tile_fwd(hv, wi_ref, wo_ref, u_ref, s_ref, F, cw):
    acc = jnp.zeros((hv.shape[0], wo_ref.shape[1]), F32)
    for j in range(F // cw):
        a = lax.dot_general(hv, wi_ref[j * cw:(j + 1) * cw, :], _NT, preferred_element_type=F32)
        b = lax.dot_general(hv, wi_ref[F + j * cw:F + (j + 1) * cw, :], _NT, preferred_element_type=F32)
        s = ((a * jax.nn.sigmoid(a)) * b).astype(BF16)
        u_ref[:, j * cw:(j + 1) * cw] = a.astype(BF16)
        u_ref[:, F + j * cw:F + (j + 1) * cw] = b.astype(BF16)
        s_ref[:, j * cw:(j + 1) * cw] = s
        acc = acc + jnp.dot(s, wo_ref[j * cw:(j + 1) * cw, :], preferred_element_type=F32)
    return acc


def _norm_ffn_fwd(xprev, branch, mods, g, gate, shift_idx, scale_idx, w_in_t, w_out, name, head=None):
    x_specs, x_args, (T, D) = _rows_operand(xprev)
    F = w_out.shape[0]
    cw = _ffn_chunk(F)
    has_res = branch is not None
    n_in = len(x_args) + int(has_res) + 4 + (2 if head else 0)

    def body(*refs):
        ins, outs = list(refs[:n_in]), list(refs[n_in:])
        x = _rows_tile([ins.pop(0) for _ in x_args])
        f_ref = ins.pop(0) if has_res else None
        m_ref, g_ref, wi_ref, wo_ref = ins[:4]
        xo_ref = outs.pop(0) if has_res else None
        h_ref, u_ref, s_ref = outs[:3]
        m = m_ref[0]
        if has_res:
            gate_idx, fac = gate
            x = x + (fac * m[gate_idx:gate_idx + 1, :]) * f_ref[...]
            xo_ref[...] = x
        hv = _norm_tile_fwd(x, m, g_ref[...], shift_idx, scale_idx)
        h_ref[...] = hv
        f = _ffn_tile_fwd(hv, wi_ref, wo_ref, u_ref, s_ref, F, cw)
        if head is None:
            outs[3][...] = f
            return
        fg_ref, t_ref = ins[4:6]
        dx_ref, df_ref, acc_ref = outs[3:6]
        _acc_init(acc_ref)
        lat = (pl.program_id(0) > 0).astype(F32)
        gate8 = 0.5 * m[8:9, :]
        x3 = x + gate8 * f
        inv3 = lax.rsqrt(jnp.mean(x3 * x3, axis=-1, keepdims=True) + EPS)
        xn = x3 * inv3
        fg = fg_ref[...]
        e = (xn * fg - t_ref[...]) * lat
        dy = e * (1.0 / D)
        dxn = dy * fg
        dx = inv3 * (dxn - xn * jnp.mean(dxn * xn, axis=-1, keepdims=True))
        dx_ref[...] = dx
        df_ref[...] = (gate8 * dx).astype(BF16)
        _acc_add(acc_ref, 0, (0.5 / D) * e * e)
        _acc_add(acc_ref, 1, dy * xn)
        _acc_add(acc_ref, 2, 0.5 * dx * f)

    in_specs = x_specs + ([_row_spec(D)] if has_res else []) + \
               [_mods_spec(D), _vec_spec(1, D), _resident(), _resident()]
    args = x_args + ([branch] if has_res else []) + [mods, g, w_in_t, w_out]
    out_specs = ([_row_spec(D)] if has_res else []) + [_row_spec(D), _row_spec(2 * F), _row_spec(F)]
    out_shape = ([jax.ShapeDtypeStruct((T, D), F32)] if has_res else []) + \
                [jax.ShapeDtypeStruct((T, D), BF16), jax.ShapeDtypeStruct((T, 2 * F), BF16),
                 jax.ShapeDtypeStruct((T, F), BF16)]
    if head is None:
        out_specs += [_row_spec(D)]
        out_shape += [jax.ShapeDtypeStruct((T, D), F32)]
    else:
        in_specs += [_vec_spec(1, D), pl.BlockSpec((ROW, D), lambda i: (jnp.maximum(i - 1, 0), 0))]
        args += list(head)
        out_specs += [_row_spec(D), _row_spec(D), _acc_spec(D)]
        out_shape += [jax.ShapeDtypeStruct((T, D), F32), jax.ShapeDtypeStruct((T, D), BF16),
                      jax.ShapeDtypeStruct((2, ACC_ROWS, D), F32)]
    out = pl.pallas_call(
        body, name=name, grid=(T // ROW,), in_specs=in_specs, out_specs=out_specs, out_shape=out_shape,
        compiler_params=_params(("arbitrary",) if head else ("parallel",)),
    )(*args)
    return tuple(out) if has_res else (None,) + tuple(out)


def _ffn_norm_bwd(df, u, w_in_t, w_out, x, dres, mods, g, shift_idx, scale_idx, gate, branch, name,
                  skip_first_tile=False):
    T, D = df.shape
    F = w_out.shape[0]
    cw = _ffn_chunk(F)
    nt = T // ROW
    has_gate = gate is not None
    x_specs, x_args, _ = _rows_operand(x)
    n_in = 7 + len(x_args) + int(has_gate)

    def body(*refs):
        ins, outs = list(refs[:n_in]), list(refs[n_in:])
        df_ref, u_ref, wi_ref, wo_ref = ins[:4]
        x_refs = ins[4:4 + len(x_args)]
        dr_ref = ins[4 + len(x_args)]
        b_ref = ins[5 + len(x_args)] if has_gate else None
        m_ref, g_ref = ins[-2:]
        du_ref, dx_ref = outs[:2]
        db_ref = outs[2] if has_gate else None
        acc_ref = outs[-1]
        _acc_init(acc_ref)
        dfv = df_ref[...]
        dh = jnp.zeros((ROW, D), F32)
        for j in range(F // cw):
            ds = lax.dot_general(dfv, wo_ref[j * cw:(j + 1) * cw, :], _NT, preferred_element_type=F32)
            a = u_ref[:, j * cw:(j + 1) * cw].astype(F32)
            b = u_ref[:, F + j * cw:F + (j + 1) * cw].astype(F32)
            sig = jax.nn.sigmoid(a)
            da = (ds * b * (sig * (1.0 + a * (1.0 - sig)))).astype(BF16)
            db = (ds * (a * sig)).astype(BF16)
            du_ref[:, j * cw:(j + 1) * cw] = da
            du_ref[:, F + j * cw:F + (j + 1) * cw] = db
            dh = dh + jnp.dot(da, wi_ref[j * cw:(j + 1) * cw, :], preferred_element_type=F32)
            dh = dh + jnp.dot(db, wi_ref[F + j * cw:F + (j + 1) * cw, :], preferred_element_type=F32)
        m = m_ref[0]
        dx = _norm_tile_bwd(_rows_tile(x_refs), dh, dr_ref[...], m, g_ref[...], shift_idx, scale_idx, acc_ref)
        dx_ref[...] = dx
        if has_gate:
            db_ref[...] = _gate_tile_bwd(dx, b_ref[...], m, gate, acc_ref)

    in_specs = [_row_spec(D), _row_spec(2 * F), _resident(), _resident()] + x_specs + [_row_spec(D)] + \
               ([_row_spec(D)] if has_gate else []) + [_mods_spec(D), _vec_spec(1, D)]
    args = [df, u, w_in_t, w_out] + x_args + [dres] + ([branch] if has_gate else []) + [mods, g]
    if skip_first_tile:
        dx_spec = pl.BlockSpec((ROW, D), lambda i: (jnp.maximum(i - 1, 0), 0))
        dx_shape = jax.ShapeDtypeStruct((T - ROW, D), F32)
    else:
        dx_spec = _row_spec(D)
        dx_shape = jax.ShapeDtypeStruct((T, D), F32)
    out_specs = [_row_spec(2 * F), dx_spec] + ([_row_spec(D)] if has_gate else []) + [_acc_spec(D)]
    out_shape = [jax.ShapeDtypeStruct((T, 2 * F), BF16), dx_shape] + \
                ([jax.ShapeDtypeStruct((T, D), BF16)] if has_gate else []) + \
                [jax.ShapeDtypeStruct((2, ACC_ROWS, D), F32)]
    out = pl.pallas_call(
        body, name=name, grid=(nt,), in_specs=in_specs, out_specs=out_specs, out_shape=out_shape,
        compiler_params=_params(("arbitrary",)),
    )(*args)
    if has_gate:
        return tuple(out)
    return out[0], out[1], None, out[2]


def _halo_specs(width, col, nt):
    per = ROW // HALO
    prev = pl.BlockSpec((HALO, width), lambda i, col=col: (jnp.maximum(i * per - 1, 0), col))
    nxt = pl.BlockSpec((HALO, width), lambda i, col=col: (jnp.minimum((i + 1) * per, nt * per - 1), col))
    return prev, nxt


def _f32(ref):
    return ref[...].astype(F32)


def _last_row(halo_ref):
    return halo_ref[HALO - 1:HALO, :].astype(F32)


def _first_row(halo_ref):
    return halo_ref[0:1, :].astype(F32)


def _shift_rows(v, prev_row, next_row):
    rows = lax.broadcasted_iota(jnp.int32, v.shape, 0)
    down = jnp.where(rows == 0, prev_row, pltpu.roll(v, 1, 0))
    up = jnp.where(rows == v.shape[0] - 1, next_row, pltpu.roll(v, v.shape[0] - 1, 0))
    return down, up


def _conv_fwd_operands(P, conv_w, D):
    nt = P.shape[0] // ROW
    cg_p, cg_n = _halo_specs(D, 1, nt)
    vc_p, vc_n = _halo_specs(D, 2, nt)
    specs = [_row_spec(D, 0), _row_spec(D, 1), _row_spec(D, 2), cg_p, vc_p, cg_n, vc_n, _vec_spec(3, D)]
    return specs, [P, P, P, P, P, P, P, conv_w]


def _conv_tile_fwd(refs, nt):
    bg_ref, cg_ref, vc_ref, cgp_ref, vcp_ref, cgn_ref, vcn_ref, w_ref = refs
    i = pl.program_id(0)
    has_prev = (i != 1).astype(F32)
    has_next = (i != nt - 1).astype(F32)
    u = _f32(cg_ref) * _f32(vc_ref)
    up_row = _last_row(cgp_ref) * _last_row(vcp_ref) * has_prev
    un_row = _first_row(cgn_ref) * _first_row(vcn_ref) * has_next
    um1, up1 = _shift_rows(u, up_row, un_row)
    w = w_ref[...]
    conv = um1 * w[0:1, :] + u * w[1:2, :] + up1 * w[2:3, :]
    return (_f32(bg_ref) * conv).astype(BF16)


def _conv_bwd_operands(P, dy, conv_w, D):
    nt = P.shape[0] // ROW
    bg_p, bg_n = _halo_specs(D, 0, nt)
    cg_p, cg_n = _halo_specs(D, 1, nt)
    vc_p, vc_n = _halo_specs(D, 2, nt)
    dy_p, dy_n = _halo_specs(D, 0, nt)
    specs = [_row_spec(D, 0), _row_spec(D, 1), _row_spec(D, 2), _row_spec(D, 0),
             bg_p, cg_p, vc_p, dy_p, bg_n, cg_n, vc_n, dy_n, _vec_spec(3, D)]
    return specs, [P, P, P, dy, P, P, P, dy, P, P, P, dy, conv_w]


def _conv_tile_bwd(refs, o_ref, acc_ref, D, nt):
    (bg_ref, cg_ref, vc_ref, dy_ref, bgp_ref, cgp_ref, vcp_ref, dyp_ref,
     bgn_ref, cgn_ref, vcn_ref, dyn_ref, w_ref) = refs
    i = pl.program_id(0)
    lat = (i > 0).astype(F32)
    has_prev = (i != 1).astype(F32)
    has_next = (i != nt - 1).astype(F32)
    bg = _f32(bg_ref)
    cg = _f32(cg_ref)
    vc = _f32(vc_ref)
    dyv = dy_ref[...] * lat
    u = cg * vc
    up_row = _last_row(cgp_ref) * _last_row(vcp_ref) * has_prev
    un_row = _first_row(cgn_ref) * _first_row(vcn_ref) * has_next
    um1, up1 = _shift_rows(u, up_row, un_row)
    w = w_ref[...]
    conv = um1 * w[0:1, :] + u * w[1:2, :] + up1 * w[2:3, :]
    dc = dyv * bg
    dcp_row = _last_row(dyp_ref) * _last_row(bgp_ref) * has_prev
    dcn_row = _first_row(dyn_ref) * _first_row(bgn_ref) * has_next
    dcm1, dcp1 = _shift_rows(dc, dcp_row, dcn_row)
    du = dcp1 * w[0:1, :] + dc * w[1:2, :] + dcm1 * w[2:3, :]
    o_ref[:, 0:D] = (dyv * conv).astype(BF16)
    o_ref[:, D:2 * D] = (du * vc * lat).astype(BF16)
    o_ref[:, 2 * D:3 * D] = (du * cg * lat).astype(BF16)
    _acc_add(acc_ref, 0, dc * um1)
    _acc_add(acc_ref, 1, dc * u)
    _acc_add(acc_ref, 2, dc * up1)


def _rope_tables(ctx_len, seq):
    n_freq = HEAD_DIM // 4
    rows = seq // GRID_W
    inv = ROPE_THETA ** (-jnp.arange(n_freq, dtype=F32) / n_freq)
    ar = jnp.arange(rows, dtype=F32)[:, None] * inv
    ac = jnp.arange(GRID_W, dtype=F32)[:, None] * inv

    def per_row(a):
        return jnp.repeat(a, GRID_W, axis=0)

    def per_col(a):
        return jnp.tile(a, (rows, 1))

    cos_t = jnp.concatenate([per_row(jnp.cos(ar)), per_row(jnp.cos(ar)), per_col(jnp.cos(ac)), per_col(jnp.cos(ac))], axis=1)
    sin_t = jnp.concatenate([per_row(-jnp.sin(ar)), per_row(jnp.sin(ar)), per_col(-jnp.sin(ac)), per_col(jnp.sin(ac))], axis=1)
    cos_t = jnp.concatenate([jnp.ones((ctx_len, HEAD_DIM), F32), cos_t], axis=0)
    sin_t = jnp.concatenate([jnp.zeros((ctx_len, HEAD_DIM), F32), sin_t], axis=0)
    return cos_t, sin_t


def _swap_halves(y):
    lanes = lax.broadcasted_iota(jnp.int32, y.shape, 1)
    first = (lanes % 64) < 32
    return jnp.where(first, pltpu.roll(y, HEAD_DIM - 32, 1), pltpu.roll(y, 32, 1))


def _to_row(col, n):
    return jnp.transpose(jnp.broadcast_to(col, (n, HEAD_DIM)))[0:1, :]


LOG2E = 1.4426950408889634
ATTN_PART_LANES = 256
ATTN_QUERY_ROWS = 768
ATTN_VMEM_LIMIT = 60 * 1024 * 1024


def _flash_fwd(q, k, v, name, tq=None, tk=None, token=None):
    extra = [] if token is None else [token]
    T = q.shape[0]
    tq = tq or _pick(T, (ATTN_QUERY_ROWS, ROW))
    parts = GROUP * tq // ATTN_PART_LANES
    tk = tk or _pick(T, (2816, 1408, 768, 512, 256))
    ck = tk
    nk = T // tk
    GW = GROUP * HEAD_DIM

    def body(q_ref, k_ref, v_ref, *rest):
        o_ref, lse_ref, qs_ref, m_ref, l_ref, acc_ref, st_ref = rest[len(extra):]
        ki = pl.program_id(2)

        @pl.when(ki == 0)
        def _():
            for g in range(GROUP):
                qs_ref[g * tq:(g + 1) * tq, :] = q_ref[:, g * HEAD_DIM:(g + 1) * HEAD_DIM]
            m_ref[...] = jnp.full(m_ref.shape, -jnp.inf, F32)
            l_ref[...] = jnp.zeros(l_ref.shape, F32)
            acc_ref[...] = jnp.zeros(acc_ref.shape, F32)

        w = ATTN_PART_LANES
        nck = tk // ck

        def lanes(p):
            return slice(p * w, (p + 1) * w)

        def keys(c):
            return slice(c * ck, (c + 1) * ck)

        def fold(a):
            return a.reshape(ck // 8, 8, w)

        def scores(p, c):
            st = lax.dot_general(k_ref[keys(c), :], qs_ref[lanes(p), :], _NT,
                                 preferred_element_type=F32) * (ATTN_SCALE * LOG2E)
            st_ref[keys(c), lanes(p)] = st
            return jnp.max(fold(st), axis=0)

        def new_max(p, partial):
            m_prev = m_ref[:, lanes(p)]
            m_new = jnp.maximum(m_prev, jnp.max(functools.reduce(jnp.maximum, partial), axis=0, keepdims=True))
            m_ref[:, lanes(p)] = m_new
            return m_new, jnp.exp2(m_prev - m_new)

        def weights(p, c, m_new):
            pt = jnp.exp2(st_ref[keys(c), lanes(p)] - m_new)
            pv = lax.dot_general(v_ref[keys(c), :], pt.astype(BF16), (((0,), (0,)), ((), ())),
                                 preferred_element_type=F32)
            return jnp.sum(fold(pt), axis=0), pv

        partial = [scores(0, c) for c in range(nck)]
        for p in range(parts):
            m_new, alpha = new_max(p, partial)
            partial, sums, pvs = [], [], []
            for c in range(nck):
                if p + 1 < parts:
                    partial.append(scores(p + 1, c))
                s8, pv = weights(p, c, m_new)
                sums.append(s8)
                pvs.append(pv)
            l_ref[:, lanes(p)] = alpha * l_ref[:, lanes(p)] + jnp.sum(sum(sums), axis=0, keepdims=True)
            acc_ref[:, lanes(p)] = alpha * acc_ref[:, lanes(p)] + sum(pvs)

        @pl.when(ki == nk - 1)
        def _():
            out = jnp.transpose(acc_ref[...] / l_ref[...])
            lse = m_ref[...] + jnp.log2(l_ref[...])
            for g in range(GROUP):
                o_ref[:, g * HEAD_DIM:(g + 1) * HEAD_DIM] = out[g * tq:(g + 1) * tq, :]
                lse_ref[0, g:g + 1, :] = lse[:, g * tq:(g + 1) * tq]

    return pl.pallas_call(
        body, name=name, grid=(N_KV_HEADS, T // tq, nk),
        in_specs=[pl.BlockSpec((tq, GW), lambda h, i, j: (i, h)),
                  pl.BlockSpec((tk, HEAD_DIM), lambda h, i, j: (j, h)),
                  pl.BlockSpec((tk, HEAD_DIM), lambda h, i, j: (j, h))] +
                 [pl.BlockSpec(t.shape, lambda h, i, j: (0, 0)) for t in extra],
        out_specs=[pl.BlockSpec((tq, GW), lambda h, i, j: (i, h)),
                   pl.BlockSpec((1, GROUP, tq), lambda h, i, j: (h, 0, i))],
        out_shape=[jax.ShapeDtypeStruct((T, N_Q_HEADS * HEAD_DIM), F32),
                   jax.ShapeDtypeStruct((N_KV_HEADS, GROUP, T), F32)],
        scratch_shapes=[pltpu.VMEM((GROUP * tq, HEAD_DIM), BF16), pltpu.VMEM((1, GROUP * tq), F32),
                        pltpu.VMEM((1, GROUP * tq), F32), pltpu.VMEM((HEAD_DIM, GROUP * tq), F32),
                        pltpu.VMEM((tk, GROUP * tq), F32)],
        compiler_params=pltpu.CompilerParams(dimension_semantics=("parallel", "parallel", "arbitrary"),
                                             vmem_limit_bytes=ATTN_VMEM_LIMIT),
    )(q, k, v, *extra)


def _flash_bwd(q, k, v, do, lse, delta, name, tq=None, tk=None, token=None):
    T = q.shape[0]
    tq = tq or _pick(T, (ATTN_QUERY_ROWS, ROW))
    tk = tk or _pick(T, (1408, 768, 512, 256))
    nk = T // tk
    GW = GROUP * HEAD_DIM
    nt = (((1,), (1,)), ((), ()))
    extra = [] if token is None else [token]

    def body(q_ref, do_ref, k_ref, v_ref, lse_ref, dl_ref, *rest):
        dq_ref, dk_ref, dv_ref, qs_ref, dos_ref, dqt_ref = rest[len(extra):]
        qi = pl.program_id(1)
        ki = pl.program_id(2)

        @pl.when(ki == 0)
        def _():
            for g in range(GROUP):
                qs_ref[g * tq:(g + 1) * tq, :] = q_ref[:, g * HEAD_DIM:(g + 1) * HEAD_DIM]
                dos_ref[g * tq:(g + 1) * tq, :] = do_ref[:, g * HEAD_DIM:(g + 1) * HEAD_DIM]
            dqt_ref[...] = jnp.zeros(dqt_ref.shape, F32)

        kk = k_ref[...]
        vv = v_ref[...]

        def lanes(p):
            return slice(p * tq, (p + 1) * tq)

        def products(p):
            st = lax.dot_general(kk, qs_ref[lanes(p), :], nt, preferred_element_type=F32)
            dpt = lax.dot_general(vv, dos_ref[lanes(p), :], nt, preferred_element_type=F32)
            return st, dpt

        dk_c = jnp.zeros((tk, HEAD_DIM), F32)
        dv_c = jnp.zeros((tk, HEAD_DIM), F32)
        ahead = products(0)
        for p in range(GROUP):
            st, dpt = ahead
            if p + 1 < GROUP:
                ahead = products(p + 1)
            pt = jnp.exp2(st * (ATTN_SCALE * LOG2E) - lse_ref[0, p:p + 1, :])
            dst = ((pt * (dpt - dl_ref[0, p:p + 1, :])) * ATTN_SCALE).astype(BF16)
            dv_c = dv_c + jnp.dot(pt.astype(BF16), dos_ref[lanes(p), :], preferred_element_type=F32)
            dk_c = dk_c + jnp.dot(dst, qs_ref[lanes(p), :], preferred_element_type=F32)
            dqt_ref[:, lanes(p)] += lax.dot_general(kk, dst, (((0,), (0,)), ((), ())), preferred_element_type=F32)
        rows = pl.ds(pl.multiple_of(ki * tk, tk), tk)

        @pl.when(qi == 0)
        def _():
            dk_ref[rows, :] = dk_c
            dv_ref[rows, :] = dv_c

        @pl.when(qi > 0)
        def _():
            dk_ref[rows, :] += dk_c
            dv_ref[rows, :] += dv_c

        @pl.when(ki == nk - 1)
        def _():
            dqv = jnp.transpose(dqt_ref[...])
            for g in range(GROUP):
                dq_ref[:, g * HEAD_DIM:(g + 1) * HEAD_DIM] = dqv[g * tq:(g + 1) * tq, :]

    return pl.pallas_call(
        body, name=name, grid=(N_KV_HEADS, T // tq, nk),
        in_specs=[pl.BlockSpec((tq, GW), lambda h, i, j: (i, h)),
                  pl.BlockSpec((tq, GW), lambda h, i, j: (i, h)),
                  pl.BlockSpec((tk, HEAD_DIM), lambda h, i, j: (j, h)),
                  pl.BlockSpec((tk, HEAD_DIM), lambda h, i, j: (j, h)),
                  pl.BlockSpec((1, GROUP, tq), lambda h, i, j: (h, 0, i)),
                  pl.BlockSpec((1, GROUP, tq), lambda h, i, j: (h, 0, i))] +
                 [pl.BlockSpec(t.shape, lambda h, i, j: (0, 0)) for t in extra],
        out_specs=[pl.BlockSpec((tq, GW), lambda h, i, j: (i, h)),
                   pl.BlockSpec((T, HEAD_DIM), lambda h, i, j: (0, h)),
                   pl.BlockSpec((T, HEAD_DIM), lambda h, i, j: (0, h))],
        out_shape=[jax.ShapeDtypeStruct((T, N_Q_HEADS * HEAD_DIM), F32),
                   jax.ShapeDtypeStruct((T, N_KV_HEADS * HEAD_DIM), F32),
                   jax.ShapeDtypeStruct((T, N_KV_HEADS * HEAD_DIM), F32)],
        scratch_shapes=[pltpu.VMEM((GROUP * tq, HEAD_DIM), BF16), pltpu.VMEM((GROUP * tq, HEAD_DIM), BF16),
                        pltpu.VMEM((HEAD_DIM, GROUP * tq), F32)],
        compiler_params=pltpu.CompilerParams(dimension_semantics=("arbitrary", "arbitrary", "arbitrary"),
                                             vmem_limit_bytes=ATTN_VMEM_LIMIT),
    )(q, do, k, v, lse, delta, *extra)


def _gate_specs(D):
    w = D // 2
    first = (3 * D + (N_Q_HEADS + 2 * N_KV_HEADS) * HEAD_DIM) // w
    return [pl.BlockSpec((ROW, w), lambda i, c=first + j: (i, c)) for j in range(4)]


def _merge_fwd(o, P, conv_w, wbc, wba, wo, D, name):
    T = o.shape[0]
    nt = T // ROW
    w = D // 2
    conv_specs, conv_args = _conv_fwd_operands(P, conv_w, D)
    nc = len(conv_args)

    def body(*refs):
        o_ref, g0, g1, g2, g3, wbc_ref, wba_ref, wo_ref, yc_ref, a1_ref, a2_ref, z_ref, mo_ref = refs[nc:]
        yc_ref[...] = _conv_tile_fwd(refs[:nc], nt)
        a1 = jnp.dot(yc_ref[...], wbc_ref[...], preferred_element_type=F32)
        a2 = jnp.dot(o_ref[...].astype(BF16), wba_ref[...], preferred_element_type=F32)
        a1_ref[...] = a1.astype(BF16)
        a2_ref[...] = a2.astype(BF16)
        for j, (gc, ga) in enumerate(((g0, g2), (g1, g3))):
            sl = slice(j * w, (j + 1) * w)
            z = jax.nn.sigmoid(_f32(gc)) * a1[:, sl] + jax.nn.sigmoid(_f32(ga)) * a2[:, sl]
            z_ref[:, sl] = z.astype(BF16)
        mo_ref[...] = jnp.dot(z_ref[...], wo_ref[...], preferred_element_type=F32)

    return pl.pallas_call(
        body, name=name, grid=(T // ROW,),
        in_specs=conv_specs + [_row_spec(D)] + _gate_specs(D) + [_resident()] * 3,
        out_specs=[_row_spec(D)] * 5,
        out_shape=[jax.ShapeDtypeStruct((T, D), BF16), jax.ShapeDtypeStruct((T, D), BF16),
                   jax.ShapeDtypeStruct((T, D), BF16), jax.ShapeDtypeStruct((T, D), BF16),
                   jax.ShapeDtypeStruct((T, D), F32)],
        compiler_params=_params(("parallel",)),
    )(*conv_args, o, P, P, P, P, wbc, wba, wo)


def _merge_bwd(dmo, a1, a2, o, P, wbc, wba, wo, D, name):
    T = a1.shape[0]
    w = D // 2

    def body(dmo_ref, a1_ref, a2_ref, o_ref, g0, g1, g2, g3, wbc_ref, wba_ref, wo_ref,
             d1_ref, d2_ref, dg_ref, dyc_ref, dob_ref, dl_ref):
        dz = lax.dot_general(dmo_ref[...], wo_ref[...], _NT, preferred_element_type=F32)
        for j, (gc, ga) in enumerate(((g0, g2), (g1, g3))):
            sl = slice(j * w, (j + 1) * w)
            dzs = dz[:, sl]
            sc = jax.nn.sigmoid(_f32(gc))
            sa = jax.nn.sigmoid(_f32(ga))
            d1_ref[:, sl] = (dzs * sc).astype(BF16)
            d2_ref[:, sl] = (dzs * sa).astype(BF16)
            dg_ref[:, j * w:(j + 1) * w] = (dzs * a1_ref[:, sl].astype(F32) * (sc * (1.0 - sc))).astype(BF16)
            dg_ref[:, D + j * w:D + (j + 1) * w] = (dzs * a2_ref[:, sl].astype(F32) * (sa * (1.0 - sa))).astype(BF16)
        dyc_ref[...] = lax.dot_general(d1_ref[...], wbc_ref[...], _NT, preferred_element_type=F32)
        dov = lax.dot_general(d2_ref[...], wba_ref[...], _NT, preferred_element_type=F32)
        dob_ref[...] = dov.astype(BF16)
        prod = dov * o_ref[...]
        for h in range(N_Q_HEADS):
            d = jnp.sum(prod[:, h * HEAD_DIM:(h + 1) * HEAD_DIM], axis=1, keepdims=True)
            dl_ref[h // GROUP, (h % GROUP):(h % GROUP) + 1, :] = _to_row(d, ROW)

    return pl.pallas_call(
        body, name=name, grid=(T // ROW,),
        in_specs=[_row_spec(D)] * 4 + _gate_specs(D) + [_resident()] * 3,
        out_specs=[_row_spec(D), _row_spec(D), _row_spec(2 * D), _row_spec(D), _row_spec(D),
                   pl.BlockSpec((N_KV_HEADS, GROUP, ROW), lambda i: (0, 0, i))],
        out_shape=[jax.ShapeDtypeStruct((T, D), BF16), jax.ShapeDtypeStruct((T, D), BF16),
                   jax.ShapeDtypeStruct((T, 2 * D), BF16), jax.ShapeDtypeStruct((T, D), F32),
                   jax.ShapeDtypeStruct((T, D), BF16), jax.ShapeDtypeStruct((N_KV_HEADS, GROUP, T), F32)],
        compiler_params=_params(("parallel",)),
    )(dmo, a1, a2, o, P, P, P, P, wbc, wba, wo)


def _adamw_math(w, g, m, v):
    m = ADAM_B1 * m + (1.0 - ADAM_B1) * g
    v = ADAM_B2 * v + (1.0 - ADAM_B2) * (g * g)
    m_hat = m / (1.0 - ADAM_B1 ** ADAM_STEP)
    v_hat = v / (1.0 - ADAM_B2 ** ADAM_STEP)
    delta = -ADAM_LR * (m_hat / (jnp.sqrt(v_hat) + ADAM_EPS) + ADAM_WD * w)
    return delta, m, v


def _adamw(w, g, m, v, name):
    R, C = w.shape
    tr = _pick(R, tuple(t for t in (256, 128, 64, 32, 16, 8) if t * C * 4 <= ADAMW_BLOCK_BYTES))

    def body(w_ref, g_ref, m_ref, v_ref, d_ref, mo_ref, vo_ref):
        d, mn, vn = _adamw_math(w_ref[...], g_ref[...], m_ref[...], v_ref[...])
        d_ref[...] = d
        mo_ref[...] = mn
        vo_ref[...] = vn

    spec = pl.BlockSpec((tr, C), lambda i: (i, 0))
    return pl.pallas_call(
        body, name=name, grid=(R // tr,),
        in_specs=[spec] * 4, out_specs=[spec] * 3,
        out_shape=[jax.ShapeDtypeStruct((R, C), F32)] * 3,
        compiler_params=_params(("parallel",)),
    )(w, g, m, v)


def _norm_mix_in_fwd(xprev, branch, mods, g, gate, shift_idx, scale_idx, w_t, gq, gk, cos_t, sin_t, name):
    x_specs, x_args, (T, D) = _rows_operand(xprev)
    N = w_t.shape[0]
    QW = N_Q_HEADS * HEAD_DIM
    KW = N_KV_HEADS * HEAD_DIM
    q0, k0, v0 = 3 * D, 3 * D + QW, 3 * D + QW + KW
    edges = [0, D, 2 * D, q0, k0, v0 + KW] + list(range(v0 + KW + D, N + 1, D))
    assert edges[-1] == N

    def body(*refs):
        f_ref, m_ref, g_ref, w_ref, gq_ref, gk_ref, c_ref, s_ref = refs[len(x_args):len(x_args) + 8]
        xo_ref, h_ref, p_ref, qo_ref, ko_ref, vo_ref = refs[len(x_args) + 8:]
        m = m_ref[0]
        gate_idx, fac = gate
        x = _rows_tile(refs[:len(x_args)]) + (fac * m[gate_idx:gate_idx + 1, :]) * f_ref[...]
        xo_ref[...] = x
        hv = _norm_tile_fwd(x, m, g_ref[...], shift_idx, scale_idx)
        h_ref[...] = hv
        c = c_ref[...]
        s = s_ref[...]

        def head(xh, gain):
            inv = lax.rsqrt(jnp.mean(xh * xh, axis=-1, keepdims=True) + EPS)
            y = (xh * inv) * gain
            return y * c + _swap_halves(y) * s

        for lo, hi in zip(edges[:-1], edges[1:]):
            pb = lax.dot_general(hv, w_ref[lo:hi, :], _NT, preferred_element_type=F32).astype(BF16)
            p_ref[:, lo:hi] = pb
            if lo == q0:
                for h in range(N_Q_HEADS):
                    sl = slice(h * HEAD_DIM, (h + 1) * HEAD_DIM)
                    qo_ref[:, sl] = head(pb[:, sl].astype(F32), gq_ref[...]).astype(BF16)
            elif lo == k0:
                for h in range(N_KV_HEADS):
                    sl = slice(h * HEAD_DIM, (h + 1) * HEAD_DIM)
                    ko_ref[:, sl] = head(pb[:, sl].astype(F32), gk_ref[...]).astype(BF16)
                vo_ref[...] = pb[:, KW:2 * KW]

    return pl.pallas_call(
        body, name=name, grid=(T // ROW,),
        in_specs=x_specs + [_row_spec(D), _mods_spec(D), _vec_spec(1, D), _resident(),
                            _vec_spec(1, HEAD_DIM), _vec_spec(1, HEAD_DIM), _row_spec(HEAD_DIM), _row_spec(HEAD_DIM)],
        out_specs=[_row_spec(D), _row_spec(D), _row_spec(N), _row_spec(QW), _row_spec(KW), _row_spec(KW)],
        out_shape=[jax.ShapeDtypeStruct((T, D), F32), jax.ShapeDtypeStruct((T, D), BF16),
                   jax.ShapeDtypeStruct((T, N), BF16), jax.ShapeDtypeStruct((T, QW), BF16),
                   jax.ShapeDtypeStruct((T, KW), BF16), jax.ShapeDtypeStruct((T, KW), BF16)],
        compiler_params=_params(("parallel",)),
    )(*x_args, branch, mods, g, w_t, gq, gk, cos_t, sin_t)


def _mix_in_norm_bwd(dyc, dgt, P, conv_w, dq, dk, dv, gq, gk, cos_t, sin_t, w_t, x, dres, mods, g, shift_idx,
                     scale_idx, gate, branch, name):
    T, D = x.shape
    nt = T // ROW
    QW = N_Q_HEADS * HEAD_DIM
    KW = N_KV_HEADS * HEAD_DIM
    q0, g0 = 3 * D, 3 * D + QW + 2 * KW
    assert g0 + dgt.shape[1] == w_t.shape[0]
    conv_specs, conv_args = _conv_bwd_operands(P, dyc, conv_w, D)
    nc = len(conv_args)

    def body(*refs):
        (dg_ref, q_ref, k_ref, dq_ref, dk_ref, dv_ref, gq_ref, gk_ref, c_ref, s_ref,
         w_ref, x_ref, dr_ref, b_ref, m_ref, g_ref,
         dx_ref, db_ref, acc_ref, dc_ref, cacc_ref, o_ref, qacc_ref) = refs[nc:]
        _acc_init(acc_ref)
        _acc_init(cacc_ref)
        _acc_init(qacc_ref)
        _conv_tile_bwd(refs[:nc], dc_ref, cacc_ref, D, nt)
        dh = jnp.dot(dc_ref[...], w_ref[0:q0, :], preferred_element_type=F32)
        c = c_ref[...]
        s = s_ref[...]

        def head(xh, d, gain):
            dyv = d * c + _swap_halves(d * s)
            inv = lax.rsqrt(jnp.mean(xh * xh, axis=-1, keepdims=True) + EPS)
            xn = xh * inv
            dxn = dyv * gain
            dxh = inv * (dxn - xn * jnp.mean(dxn * xn, axis=-1, keepdims=True))
            return dxh, jnp.sum(dyv * xn, axis=0, keepdims=True)

        dgq = jnp.zeros((1, HEAD_DIM), F32)
        for h in range(N_Q_HEADS):
            sl = slice(h * HEAD_DIM, (h + 1) * HEAD_DIM)
            dxh, dgh = head(q_ref[:, sl].astype(F32), dq_ref[:, sl], gq_ref[...])
            o_ref[:, sl] = dxh.astype(BF16)
            dgq = dgq + dgh
        dh = dh + jnp.dot(dg_ref[...], w_ref[g0:, :], preferred_element_type=F32)
        dgk = jnp.zeros((1, HEAD_DIM), F32)
        for h in range(N_KV_HEADS):
            sl = slice(h * HEAD_DIM, (h + 1) * HEAD_DIM)
            dxh, dgh = head(k_ref[:, sl].astype(F32), dk_ref[:, sl], gk_ref[...])
            o_ref[:, QW + h * HEAD_DIM:QW + (h + 1) * HEAD_DIM] = dxh.astype(BF16)
            dgk = dgk + dgh
        o_ref[:, QW + KW:QW + 2 * KW] = dv_ref[...].astype(BF16)
        qacc_ref[0, 0:1, 0:HEAD_DIM] += dgq
        qacc_ref[0, 1:2, 0:HEAD_DIM] += dgk
        dh = dh + jnp.dot(o_ref[...], w_ref[q0:g0, :], preferred_element_type=F32)
        m = m_ref[0]
        dx = _norm_tile_bwd(x_ref[...], dh, dr_ref[...], m, g_ref[...], shift_idx, scale_idx, acc_ref)
        dx_ref[...] = dx
        db_ref[...] = _gate_tile_bwd(dx, b_ref[...], m, gate, acc_ref)

    return pl.pallas_call(
        body, name=name, grid=(T // ROW,),
        in_specs=conv_specs +
                 [_row_spec(dgt.shape[1]), _row_spec(QW, q0 // QW), _row_spec(KW, (q0 + QW) // KW),
                  _row_spec(QW), _row_spec(KW), _row_spec(KW), _vec_spec(1, HEAD_DIM), _vec_spec(1, HEAD_DIM),
                  _row_spec(HEAD_DIM), _row_spec(HEAD_DIM),
                  _resident(), _row_spec(D), _row_spec(D), _row_spec(D), _mods_spec(D), _vec_spec(1, D)],
        out_specs=[_row_spec(D), _row_spec(D), _acc_spec(D), _row_spec(q0), _acc_spec(D),
                   _row_spec(QW + 2 * KW), _acc_spec(D)],
        out_shape=[jax.ShapeDtypeStruct((T, D), F32), jax.ShapeDtypeStruct((T, D), BF16),
                   jax.ShapeDtypeStruct((2, ACC_ROWS, D), F32), jax.ShapeDtypeStruct((T, q0), BF16),
                   jax.ShapeDtypeStruct((2, ACC_ROWS, D), F32), jax.ShapeDtypeStruct((T, QW + 2 * KW), BF16),
                   jax.ShapeDtypeStruct((2, ACC_ROWS, D), F32)],
        compiler_params=_params(("arbitrary",)),
    )(*conv_args, dgt, P, P, dq, dk, dv, gq, gk, cos_t, sin_t, w_t, x, dres, branch, mods, g)


def _adamw_transposed(w, gt, m, v, name):
    R, C = w.shape
    tc = LANES

    def body(w_ref, g_ref, m_ref, v_ref, go_ref, d_ref, mo_ref, vo_ref):
        g = jnp.transpose(g_ref[...])
        d, mn, vn = _adamw_math(w_ref[...], g, m_ref[...], v_ref[...])
        go_ref[...] = g
        d_ref[...] = d
        mo_ref[...] = mn
        vo_ref[...] = vn

    spec = pl.BlockSpec((R, tc), lambda j: (0, j))
    return pl.pallas_call(
        body, name=name, grid=(C // tc,),
        in_specs=[spec, pl.BlockSpec((tc, R), lambda j: (j, 0)), spec, spec], out_specs=[spec] * 4,
        out_shape=[jax.ShapeDtypeStruct((R, C), F32)] * 4,
        compiler_params=_params(("parallel",)),
    )(w, gt, m, v)


class _NoExchange:
    def __init__(self, rest):
        self.rest = rest

    def rest_weights(self, after):
        return self.rest[0], lambda first, after2: (None, lambda after3: self.rest)

    def reduce_early(self, grads, tag):
        return None


def _local_step(xcat, target, mods, norm_g, final_g, gq, gk, conv_w, ffn1_w, hooks, rope):
    T, D = _rows_operand(xcat)[2]
    w1i, w1o = ffn1_w
    g1, g2, g3 = norm_g
    cos_t, sin_t = rope

    def after(value, token, name):
        return value if token is None else _after(value, token, name)

    _, h1, u1, s1, f1 = _norm_ffn_fwd(xcat, None, mods, g1, None, 0, 1, w1i, w1o, "f_ffn1")
    wi, more_weights = hooks.rest_weights(f1)
    x1, h2, P, qn, kn, vb = _norm_mix_in_fwd(xcat, f1, mods, g2, (2, 0.5), 3, 4, wi, gq, gk, cos_t, sin_t, "f_mix_in")
    token_w, last_weights = more_weights(wi, qn)
    o, lse = _flash_fwd(qn, kn, vb, "f_attn", token=token_w)
    wi, wbc, wba, wo, w2i, w2o = last_weights(o)
    yc, a1, a2, z, mo = _merge_fwd(o, P, conv_w, wbc, wba, wo, D, "f_merge")
    x2, h3, u2, s2, dx3, df2, acc_head = _norm_ffn_fwd(x1, mo, mods, g3, (5, 1.0), 6, 7, w2i, w2o, "f_ffn2",
                                                       head=(final_g, target))

    du2, dx2, dmo, acc_n3 = _ffn_norm_bwd(df2, u2, w2i, w2o, x2, dx3, mods, g3, 6, 7, (5, 1.0), mo, "b_ffn2")
    g_w2o = _grad_matmul(s2, df2, "b_ffn2_out_dw")
    g_w2i = _grad_matmul(du2, h3, "b_ffn2_in_dw")

    g_wo = _grad_matmul(z, dmo, "b_mix_out_dw")
    da1, da2, dgt, dyc, dob, delta = _merge_bwd(dmo, a1, a2, o, P, wbc, wba, wo, D, "b_merge")
    g_wbc = _grad_matmul(yc, da1, "b_branch_conv_dw")
    g_wba = _grad_matmul(o, da2, "b_branch_attn_dw")
    token_a = hooks.reduce_early([g_wbc, g_wba, g_wo, g_w2i, g_w2o], "a")
    dq, dk, dv = _flash_bwd(qn, kn, vb, dob, lse, delta, "b_attn", token=token_a)
    dx1, df1, acc_n2, dconv, acc_conv, dqkv, acc_qk = _mix_in_norm_bwd(
        dyc, dgt, P, conv_w, dq, dk, dv, gq, gk, cos_t, sin_t, wi, x1, dx2, mods, g2, 3, 4, (2, 0.5), f1, "b_mix_in")
    d_parts = (dconv, dqkv, dgt)
    g_wi = jnp.concatenate([_grad_matmul(dp, h2, f"b_mix_in_dw_{i}") for i, dp in enumerate(d_parts)], axis=0)
    g1_b = after(g1, hooks.reduce_early([g_wi], "b"), "after_rs_b")

    du1, grad_x, _, acc_n1 = _ffn_norm_bwd(df1, u1, w1i, w1o, xcat, dx1, mods, g1_b, 0, 1, None, None, "b_ffn1",
                                           skip_first_tile=True)
    g_w1o = _grad_matmul(s1, df1, "b_ffn1_out_dw")
    g_w1i = _grad_matmul(du1, h1, "b_ffn1_in_dw", token=hooks.reduce_early([g_w1o], "c"))

    grads = (g_w1i, g_w1o, g_wi, g_wbc, g_wba, g_wo, g_w2i, g_w2o)
    accs = (acc_head, acc_n3, acc_n2, acc_n1, acc_conv, acc_qk)
    return grad_x, grads, accs


def _place():
    return lax.axis_index("x"), lax.axis_index("y"), lax.axis_index("c")


def _other_chips(x, y):
    return [(1 - x, y), (x, 1 - y), (1 - x, 1 - y)]


def _allgather8(v, name):
    R, N = v.shape

    def body(v_ref, out_ref, send_sems, recv_sems, local_sem):
        x, y, c = _place()
        me, sibling = (x, y, c), (x, y, 1 - c)
        chips = _other_chips(x, y)

        def blk(px, py, pc):
            return out_ref.at[4 * px + 2 * py + pc]

        def copy(k, block, to, src=None):
            return pltpu.make_async_remote_copy(
                src_ref=blk(*block) if src is None else src, dst_ref=blk(*block),
                send_sem=send_sems.at[k], recv_sem=recv_sems.at[k], device_id=to, device_id_type=MESH)

        mine = pltpu.make_async_copy(v_ref, blk(*me), local_sem)
        mine.start()
        first = [copy(0, me, sibling, src=v_ref)]
        first += [copy(1 + j, me, (*chip, c), src=v_ref) for j, chip in enumerate(chips)]
        for cp in first:
            cp.start()
        passed = [copy(4 + j, (*chip, c), sibling) for j, chip in enumerate(chips)]
        for j, chip in enumerate(chips):
            copy(1 + j, (*chip, c), me).wait_recv()
            passed[j].start()
        copy(0, sibling, me).wait_recv()
        for j, chip in enumerate(chips):
            copy(4 + j, (*chip, 1 - c), me).wait_recv()
        for cp in first + passed:
            cp.wait_send()
        mine.wait()

    return pl.pallas_call(
        body, name=name,
        out_shape=jax.ShapeDtypeStruct((N_DEV, R, N), v.dtype),
        in_specs=[pl.BlockSpec(memory_space=pltpu.VMEM)],
        out_specs=pl.BlockSpec(memory_space=pltpu.VMEM),
        scratch_shapes=[pltpu.SemaphoreType.DMA((7,)), pltpu.SemaphoreType.DMA((7,)), pltpu.SemaphoreType.DMA],
        compiler_params=pltpu.CompilerParams(vmem_limit_bytes=VMEM_LIMIT),
    )(v)


def _any_specs(n):
    return [pl.BlockSpec(memory_space=pl.ANY)] * n


def _place_shard(w2, idx, transpose, name, token):
    if transpose:
        D, rs = w2.shape
        tr = LANES
        in_spec = pl.BlockSpec((D, tr), lambda i, idx: (0, i))
    else:
        rs, D = w2.shape
        tr = _pick(rs, (352, 256, 128, 64, 32, 16))
        in_spec = pl.BlockSpec((tr, D), lambda i, idx: (i, 0))
    steps = rs // tr

    def body(idx_ref, w_ref, t_ref, o_ref):
        v = w_ref[...]
        o_ref[...] = (jnp.transpose(v) if transpose else v).astype(BF16)

    return pl.pallas_call(
        body, name=name,
        grid_spec=pltpu.PrefetchScalarGridSpec(
            num_scalar_prefetch=1, grid=(steps,),
            in_specs=[in_spec, pl.BlockSpec(token.shape, lambda i, idx: (0, 0))],
            out_specs=pl.BlockSpec((tr, D), lambda i, idx: (idx[1] * steps + i, 0))),
        out_shape=jax.ShapeDtypeStruct((N_CHIPS * rs, D), BF16),
        compiler_params=_params(("arbitrary",)),
    )(idx, w2, token)


_HBM = pl.BlockSpec(memory_space=pltpu.HBM)
_SEM = pl.BlockSpec(memory_space=pltpu.SEMAPHORE)
_EFFECT = pltpu.SideEffectType.DATAFLOW_SIDE_EFFECTING


def _in_hbm(a):
    return pltpu.with_memory_space_constraint(a, pltpu.HBM)


def _split_copies(n, per, make):
    def start(nbuf, name, bufs, after=None):
        extra = [] if after is None else [after]

        def body(*refs):
            ins = refs[:nbuf]
            send_sems, recv_sems = refs[nbuf + len(extra)], refs[nbuf + len(extra) + 1]
            token = refs[-1]
            for t in range(n):
                for j in range(per):
                    make(ins, t, j, send_sems.at[per * t + j], recv_sems.at[per * t + j]).start()
            token[...] = jnp.zeros(token.shape, token.dtype)

        out = pl.pallas_call(
            body, name=name,
            out_shape=(pltpu.SemaphoreType.DMA((per * n,)), pltpu.SemaphoreType.DMA((per * n,)),
                       *[pltpu.HBM(b.shape, b.dtype) for b in bufs], jax.ShapeDtypeStruct((8, 128), F32)),
            in_specs=[_HBM] * nbuf + [pl.BlockSpec(memory_space=pl.ANY)] * len(extra),
            out_specs=(_SEM, _SEM, *[_HBM] * nbuf, pl.BlockSpec(memory_space=pltpu.VMEM)),
            input_output_aliases={i: 2 + i for i in range(nbuf)},
            compiler_params=pltpu.CompilerParams(has_side_effects=_EFFECT),
        )(*[_in_hbm(b) for b in bufs], *extra)
        return out[0], out[1], list(out[2:2 + nbuf]), out[-1]

    def wait(nbuf, name, send_sems, recv_sems, bufs, after, tensors=range(n)):
        afters = list(after) if isinstance(after, (list, tuple)) else [after]

        def body(*refs):
            ins = refs[:nbuf]
            ss, rs = refs[nbuf], refs[nbuf + 1]
            for t in tensors:
                for j in range(per):
                    cp = make(ins, t, j, ss.at[per * t + j], rs.at[per * t + j])
                    cp.wait_send()
                    cp.wait_recv()

        return pl.pallas_call(
            body, name=name,
            out_shape=[pltpu.HBM(b.shape, b.dtype) for b in bufs],
            in_specs=[_HBM] * nbuf + [_SEM, _SEM] + [pl.BlockSpec(memory_space=pl.ANY)] * len(afters),
            out_specs=[_HBM] * nbuf,
            input_output_aliases={i: i for i in range(nbuf)},
            compiler_params=pltpu.CompilerParams(has_side_effects=_EFFECT),
        )(*bufs, send_sems, recv_sems, *afters)

    return start, wait


RS_PEERS = N_DEV - 1


def _reduce_exchange_split(grads):
    n = len(grads)

    def make(bufs, t, j, send_sem, recv_sem):
        x, y, c = _place()
        chip = (x, y) if j == 6 else _other_chips(x, y)[j % 3]
        core = c if j < 3 else 1 - c
        half = grads[t].shape[0] // (2 * N_CHIPS)
        piece = bufs[t].at[pl.ds((2 * (2 * chip[0] + chip[1]) + core) * half, half), :]
        return pltpu.make_async_remote_copy(src_ref=piece, dst_ref=bufs[n + t].at[j], send_sem=send_sem,
                                            recv_sem=recv_sem, device_id=(*chip, core), device_id_type=MESH)

    return _split_copies(n, RS_PEERS, make)


def _reduce_sum(g, landed, idx, name):
    _, half, D = landed.shape
    g4 = g.reshape(N_CHIPS, 2, half, D)
    tr = _pick(half, (416, 352, 128))
    steps = half // tr

    def body(idx_ref, g_ref, l_ref, o_ref):
        acc = g_ref[0, 0].astype(F32)
        for j in range(RS_PEERS):
            acc = acc + l_ref[j].astype(F32)
        o_ref[...] = acc

    return pl.pallas_call(
        body, name=name,
        grid_spec=pltpu.PrefetchScalarGridSpec(
            num_scalar_prefetch=1, grid=(steps,),
            in_specs=[pl.BlockSpec((1, 1, tr, D), lambda i, idx: (idx[1], idx[0], i, 0)),
                      pl.BlockSpec((RS_PEERS, tr, D), lambda i, idx: (0, i, 0))],
            out_specs=pl.BlockSpec((tr, D), lambda i, idx: (idx[0] * steps + i, 0))),
        out_shape=jax.ShapeDtypeStruct((2 * half, D), F32),
        compiler_params=_params(("arbitrary",)),
    )(idx, g4, landed)


def _weights_gather_split(fulls):
    def make(bufs, t, j, send_sem, recv_sem):
        x, y, c = _place()
        chip = _other_chips(x, y)[j]
        rs = fulls[t].shape[0] // N_CHIPS
        rows = bufs[t].at[pl.ds((2 * x + y) * rs + c * (rs // 2), rs // 2), :]
        return pltpu.make_async_remote_copy(src_ref=rows, dst_ref=rows, send_sem=send_sem, recv_sem=recv_sem,
                                            device_id=(*chip, c), device_id_type=MESH)

    return _split_copies(len(fulls), 3, make)


def _weights_pass_on_split(fulls):
    def make(bufs, t, j, send_sem, recv_sem):
        x, y, c = _place()
        px, py = _other_chips(x, y)[j]
        rs = fulls[t].shape[0] // N_CHIPS
        rows = bufs[t].at[pl.ds((2 * px + py) * rs + c * (rs // 2), rs // 2), :]
        return pltpu.make_async_remote_copy(src_ref=rows, dst_ref=rows, send_sem=send_sem, recv_sem=recv_sem,
                                            device_id=(x, y, 1 - c), device_id_type=MESH)

    return _split_copies(len(fulls), 3, make)


def _weights_pass_on(fulls, name):
    n = len(fulls)

    def body(*refs):
        full = refs[n:2 * n]
        send_sems, recv_sems = refs[2 * n:]
        x, y, c = _place()
        chips = _other_chips(x, y)

        def copy(t, j, h):
            rs = fulls[t].shape[0] // N_CHIPS
            px, py = chips[j]
            rows = full[t].at[pl.ds((2 * px + py) * rs + h * (rs // 2), rs // 2), :]
            return pltpu.make_async_remote_copy(src_ref=rows, dst_ref=rows, send_sem=send_sems.at[3 * t + j],
                                                recv_sem=recv_sems.at[3 * t + j], device_id=(x, y, 1 - c),
                                                device_id_type=MESH)

        for t in range(n):
            for j in range(3):
                copy(t, j, c).start()
        for t in range(n):
            for j in range(3):
                copy(t, j, 1 - c).wait_recv()
        for t in range(n):
            for j in range(3):
                copy(t, j, c).wait_send()

    return pl.pallas_call(
        body, name=name,
        out_shape=[jax.ShapeDtypeStruct(f.shape, f.dtype) for f in fulls],
        in_specs=_any_specs(n), out_specs=_any_specs(n),
        input_output_aliases={t: t for t in range(n)},
        scratch_shapes=[pltpu.SemaphoreType.DMA((3 * n,)), pltpu.SemaphoreType.DMA((3 * n,))],
    )(*fulls)


def _after(value, token, name):
    def body(v_ref, t_ref, o_ref):
        o_ref[...] = v_ref[...]

    return pl.pallas_call(
        body, name=name, out_shape=jax.ShapeDtypeStruct(value.shape, value.dtype),
        in_specs=_whole(2), out_specs=pl.BlockSpec(memory_space=pltpu.VMEM),
    )(value, token)


def _pair_swap(shards, name):
    n = len(shards)

    def body(*refs):
        full = refs[n:2 * n]
        send_sems, recv_sems = refs[2 * n:]
        x, y, c = _place()

        def half(t, h):
            rows = shards[t].shape[0] // 2
            return full[t].at[pl.ds(h * rows, rows), :]

        def copy(t, h):
            return pltpu.make_async_remote_copy(src_ref=half(t, h), dst_ref=half(t, h), send_sem=send_sems.at[t],
                                                recv_sem=recv_sems.at[t], device_id=(x, y, 1 - c),
                                                device_id_type=MESH)

        for t in range(n):
            copy(t, c).start()
        for t in range(n):
            copy(t, 1 - c).wait_recv()
        for t in range(n):
            copy(t, c).wait_send()

    return pl.pallas_call(
        body, name=name,
        out_shape=[jax.ShapeDtypeStruct(a.shape, a.dtype) for a in shards],
        in_specs=_any_specs(n), out_specs=_any_specs(n),
        input_output_aliases={t: t for t in range(n)},
        scratch_shapes=[pltpu.SemaphoreType.DMA((n,)), pltpu.SemaphoreType.DMA((n,))],
    )(*shards)


def _gather_begin(fulls, tag):
    start, wait = _weights_gather_split(fulls)
    send_sems, recv_sems, bufs, token = start(len(fulls), f"ag_{tag}_start", fulls)
    return (wait, send_sems, recv_sems, bufs), token


def _gather_end(state, after, tag):
    wait, send_sems, recv_sems, bufs = state
    landed = wait(len(bufs), f"ag_{tag}_wait", send_sems, recv_sems, bufs, after)
    return _weights_pass_on(landed, f"ag_{tag}_pass_on")


class _Exchanges:
    def __init__(self, fulls_rest, idx):
        self.idx = idx
        self._rest, self.token = _gather_begin(fulls_rest, "rest")
        self._early = []

    def rest_weights(self, after):
        wait, send_sems, recv_sems, bufs = self._rest
        n = len(bufs)
        landed = wait(n, "ag_rest_wait_0", send_sems, recv_sems, bufs, after, tensors=(0,))
        (first,) = _weights_pass_on(landed[:1], "ag_rest_pass_on_0")

        def more(first, after2):
            done = wait(n, "ag_rest_wait_1", send_sems, recv_sems, [first, *landed[1:]], after2, tensors=range(1, n))
            pass_start, pass_wait = _weights_pass_on_split(done[1:])
            pass_send, pass_recv, passing, token = pass_start(n - 1, "ag_rest_pass_on_start", done[1:])

            def last(after3):
                return [done[0], *pass_wait(n - 1, "ag_rest_pass_on_wait", pass_send, pass_recv, passing, after3)]

            return token, last

        return first, more

    def reduce_early(self, grads, tag, token=None):
        zones = [lax.empty((RS_PEERS, g.shape[0] // (2 * N_CHIPS), g.shape[1]), g.dtype) for g in grads]
        start, wait = _reduce_exchange_split(grads)
        send_sems, recv_sems, bufs, token = start(2 * len(grads), "rs_start_" + tag, list(grads) + zones, token)
        self._early.append((tag, wait, send_sems, recv_sems, bufs))
        return token

    def finish(self, tags, after):
        halves = []
        for tag, wait, send_sems, recv_sems, bufs in self._early:
            if tag in tags:
                n = len(bufs) // 2
                done = wait(len(bufs), "rs_wait_" + tag, send_sems, recv_sems, bufs, after)
                halves += [_reduce_sum(g, l, self.idx, f"rs_sum_{tag}{t}")
                           for t, (g, l) in enumerate(zip(done[:n], done[n:]))]
        return halves


N_MOD = 9
PACK_HEAD, PACK_N3, PACK_N2, PACK_N1, PACK_CONV, PACK_QK = 0, 16, 32, 48, 64, 80
MOD_SRC = ((PACK_N1, 0), (PACK_N1, 1), (PACK_N2, 3), (PACK_N2, 0), (PACK_N2, 1),
           (PACK_N3, 3), (PACK_N3, 0), (PACK_N3, 1), (PACK_HEAD, 2))
CTX_ROW = 8


def _silu(v):
    return v * jax.nn.sigmoid(v)


def _whole(n):
    return [pl.BlockSpec(memory_space=pltpu.VMEM)] * n


def _mod_rows(cin, w_sh, b_sh, name):
    def body(c_ref, w_ref, b_ref, o_ref):
        a = _silu(c_ref[...]).astype(BF16)
        o_ref[...] = jnp.dot(a, w_ref[...].astype(BF16), preferred_element_type=F32) + b_ref[...]

    return pl.pallas_call(
        body, name=name, out_shape=jax.ShapeDtypeStruct((cin.shape[0], w_sh.shape[1]), F32),
        in_specs=_whole(3), out_specs=pl.BlockSpec(memory_space=pltpu.VMEM),
        compiler_params=pltpu.CompilerParams(vmem_limit_bytes=VMEM_LIMIT),
    )(cin, w_sh, b_sh)


def _small_reduce(gathered, name):
    _, _, D = gathered.shape

    def body(g_ref, loss_ref, db_ref, gn_ref, cv_ref, qk_ref, dm_ref):
        tot = g_ref[0]
        for r in range(1, N_DEV):
            tot = tot + g_ref[r]

        def both(block, row):
            return tot[block + row:block + row + 1, :] + tot[block + 8 + row:block + 8 + row + 1, :]

        loss = jnp.sum(both(PACK_HEAD, 0), axis=1, keepdims=True)
        loss_ref[...] = jnp.broadcast_to(loss, loss_ref.shape)
        db_ref[...] = jnp.zeros(db_ref.shape, F32)
        dm_ref[...] = jnp.zeros(dm_ref.shape, F32)
        for j, (block, row) in enumerate(MOD_SRC):
            db_ref[j:j + 1, :] = both(block, row)
            dm_ref[CTX_ROW, j:j + 1, :] = tot[block + row:block + row + 1, :]
            for r in range(N_DEV):
                dm_ref[r, j:j + 1, :] = g_ref[r, block + 8 + row:block + 8 + row + 1, :]
        gn_ref[...] = jnp.zeros(gn_ref.shape, F32)
        gn_ref[0:1, :] = both(PACK_N1, 2)
        gn_ref[8:9, :] = both(PACK_N2, 2)
        gn_ref[16:17, :] = both(PACK_N3, 2)
        gn_ref[24:25, :] = both(PACK_HEAD, 1)
        cv_ref[...] = jnp.zeros(cv_ref.shape, F32)
        for r in range(3):
            cv_ref[r:r + 1, :] = both(PACK_CONV, r)
        qk_ref[...] = jnp.zeros(qk_ref.shape, F32)
        qk_ref[0:1, 0:HEAD_DIM] = both(PACK_QK, 0)[:, 0:HEAD_DIM]
        qk_ref[0:1, HEAD_DIM:2 * HEAD_DIM] = both(PACK_QK, 1)[:, 0:HEAD_DIM]

    return pl.pallas_call(
        body, name=name,
        out_shape=[jax.ShapeDtypeStruct((8, 128), F32), jax.ShapeDtypeStruct((16, D), F32),
                   jax.ShapeDtypeStruct((32, D), F32), jax.ShapeDtypeStruct((8, D), F32),
                   jax.ShapeDtypeStruct((8, D), F32), jax.ShapeDtypeStruct((16, 16, D), F32)],
        in_specs=_whole(1), out_specs=_whole(6),
        compiler_params=pltpu.CompilerParams(vmem_limit_bytes=VMEM_LIMIT),
    )(gathered)


def _wmod_grad(cin, dm_sh, w_sh, name):
    def body(c_ref, d_ref, w_ref, gw_ref, cp_ref):
        a = _silu(c_ref[...]).astype(BF16)
        d = d_ref[...].astype(BF16)
        gw_ref[...] = lax.dot_general(a, d, (((0,), (0,)), ((), ())), preferred_element_type=F32)
        cp_ref[...] = lax.dot_general(d, w_ref[...].astype(BF16), (((1,), (1,)), ((), ())),
                                      preferred_element_type=F32)

    return pl.pallas_call(
        body, name=name,
        out_shape=[jax.ShapeDtypeStruct(w_sh.shape, F32), jax.ShapeDtypeStruct(cin.shape, F32)],
        in_specs=_whole(3), out_specs=_whole(2),
        compiler_params=pltpu.CompilerParams(vmem_limit_bytes=VMEM_LIMIT),
    )(cin, dm_sh, w_sh)


def _cctx_grad(parts, c_ctx8, name):
    def body(p_ref, c_ref, o_ref):
        tot = p_ref[0] + p_ref[2] + p_ref[4] + p_ref[6]
        cv = c_ref[...]
        sig = jax.nn.sigmoid(cv)
        rows = lax.broadcasted_iota(jnp.int32, tot.shape, 0)
        o_ref[...] = jnp.where(rows == 0, tot * (sig * (1.0 + cv * (1.0 - sig))), 0.0)

    return pl.pallas_call(
        body, name=name, out_shape=jax.ShapeDtypeStruct(c_ctx8.shape, F32),
        in_specs=_whole(2), out_specs=pl.BlockSpec(memory_space=pltpu.VMEM),
    )(parts, c_ctx8)


def _pad_rows(a, rows):
    return jnp.pad(a, ((0, rows - a.shape[0]), (0, 0)))


def _pack_small(c_ctx, b_mod, n1, n2, n3, final_g, gq, gk, conv_sh, D):
    misc = jnp.concatenate([gq, gk, conv_sh.reshape(1, -1)], axis=1)
    return jnp.concatenate([_pad_rows(c_ctx[None], 8), _pad_rows(b_mod.reshape(N_MOD, D), 16), _pad_rows(n1, 8),
                            _pad_rows(n2, 8), _pad_rows(n3, 8), _pad_rows(final_g[None], 8), _pad_rows(misc, 8)], axis=0)


def _unpack_small(p, D, conv_shape):
    misc = p[56:57]
    return dict(c_ctx=p[0], b_mod=p[8:8 + N_MOD].reshape(1, N_MOD * D), norm1_g=p[24:25], norm2_g=p[32:33],
                norm3_g=p[40:41], final_g=p[48], q_norm_g=misc[:, 0:HEAD_DIM], k_norm_g=misc[:, HEAD_DIM:2 * HEAD_DIM],
                conv_w=misc[:, 2 * HEAD_DIM:].reshape(conv_shape))


WEIGHT_ORDER = ("c_ctx", "w_mod", "b_mod", "norm1_g", "norm2_g", "norm3_g", "ffn1_w_in", "ffn1_w_out", "w_in",
                "conv_w", "q_norm_g", "k_norm_g", "w_branch_conv", "w_branch_attn", "w_out", "ffn2_w_in",
                "ffn2_w_out", "final_g")
BIG = ("ffn1_w_in", "ffn1_w_out", "w_in", "w_branch_conv", "w_branch_attn", "w_out", "ffn2_w_in", "ffn2_w_out")
COLUMN_SHARDED = ("ffn1_w_in", "w_in", "ffn2_w_in")


def kernel(x, c, ctx, c_ctx, w_mod, b_mod, norm1_g, norm2_g, norm3_g, ffn1_w_in, ffn1_w_out, w_in, conv_w, q_norm_g, k_norm_g, w_branch_conv, w_branch_attn, w_out, ffn2_w_in, ffn2_w_out, final_g, loss_target, m_c_ctx, m_w_mod, m_b_mod, m_norm1_g, m_norm2_g, m_norm3_g, m_ffn1_w_in, m_ffn1_w_out, m_w_in, m_conv_w, m_q_norm_g, m_k_norm_g, m_w_branch_conv, m_w_branch_attn, m_w_out, m_ffn2_w_in, m_ffn2_w_out, m_final_g, v_c_ctx, v_w_mod, v_b_mod, v_norm1_g, v_norm2_g, v_norm3_g, v_ffn1_w_in, v_ffn1_w_out, v_w_in, v_conv_w, v_q_norm_g, v_k_norm_g, v_w_branch_conv, v_w_branch_attn, v_w_out, v_ffn2_w_in, v_ffn2_w_out, v_final_g):
    w = dict(c_ctx=c_ctx, w_mod=w_mod, b_mod=b_mod, norm1_g=norm1_g, norm2_g=norm2_g, norm3_g=norm3_g,
             ffn1_w_in=ffn1_w_in, ffn1_w_out=ffn1_w_out, w_in=w_in, conv_w=conv_w, q_norm_g=q_norm_g,
             k_norm_g=k_norm_g, w_branch_conv=w_branch_conv, w_branch_attn=w_branch_attn, w_out=w_out,
             ffn2_w_in=ffn2_w_in, ffn2_w_out=ffn2_w_out, final_g=final_g)
    m = dict(c_ctx=m_c_ctx, w_mod=m_w_mod, b_mod=m_b_mod, norm1_g=m_norm1_g, norm2_g=m_norm2_g, norm3_g=m_norm3_g,
             ffn1_w_in=m_ffn1_w_in, ffn1_w_out=m_ffn1_w_out, w_in=m_w_in, conv_w=m_conv_w, q_norm_g=m_q_norm_g,
             k_norm_g=m_k_norm_g, w_branch_conv=m_w_branch_conv, w_branch_attn=m_w_branch_attn, w_out=m_w_out,
             ffn2_w_in=m_ffn2_w_in, ffn2_w_out=m_ffn2_w_out, final_g=m_final_g)
    v = dict(c_ctx=v_c_ctx, w_mod=v_w_mod, b_mod=v_b_mod, norm1_g=v_norm1_g, norm2_g=v_norm2_g, norm3_g=v_norm3_g,
             ffn1_w_in=v_ffn1_w_in, ffn1_w_out=v_ffn1_w_out, w_in=v_w_in, conv_w=v_conv_w, q_norm_g=v_q_norm_g,
             k_norm_g=v_k_norm_g, w_branch_conv=v_w_branch_conv, w_branch_attn=v_w_branch_attn, w_out=v_w_out,
             ffn2_w_in=v_ffn2_w_in, ffn2_w_out=v_ffn2_w_out, final_g=v_final_g)

    xi, yi, ci = _place()
    dev = 4 * xi + 2 * yi + ci
    shard = 2 * xi + yi
    idx = jnp.stack([ci, shard, 2 * (1 - xi) + yi, 2 * xi + (1 - yi), 2 * (1 - xi) + (1 - yi)]).astype(jnp.int32)
    D = x.shape[-1]
    ctx_len = ctx.shape[1]
    assert ctx_len == ROW and c.shape == (1, D)
    mcols = w_mod.shape[2]
    ccols = conv_w.shape[2]

    def place(names, token):
        fulls = []
        for n in names:
            fulls.append(_place_shard(w[n][0], idx, n in COLUMN_SHARDED, "place_" + n, token))
            token = fulls[-1][:16, :HEAD_DIM]
        return fulls

    ffn1_gather, ffn1_token = _gather_begin(place(BIG[:2], c), "ffn1")
    fulls_rest = place(BIG[2:], ffn1_token)

    rope = _rope_tables(ctx_len, x.shape[1])
    c8 = jnp.broadcast_to(c, (8, D))
    for token, name in ((fulls_rest[-1][:16, :HEAD_DIM], "after_place"), (rope[0], "after_rope_cos"),
                        (rope[1], "after_rope_sin")):
        c8 = _after(c8, token, name)
    c_all = _allgather8(c8, "ag_c")[:, 0, :]
    cin = jnp.concatenate([c_all, _pad_rows(c_ctx[None], 8)], axis=0)
    b_sh = lax.dynamic_slice(b_mod, (0, shard * mcols), (1, mcols))
    mod_sh = _mod_rows(cin, w_mod[0], b_sh, "mod_rows")
    conv_rows = jnp.pad(conv_w[0], ((0, 8 - conv_w.shape[1]), (0, mcols - ccols)))
    mod_all = _allgather8(jnp.concatenate([mod_sh, conv_rows], axis=0), "ag_mod")
    mod_full = jnp.concatenate([mod_all[2 * s, :16] for s in range(N_CHIPS)], axis=1)
    conv_full = jnp.concatenate([mod_all[2 * s, 16:16 + conv_w.shape[1], :ccols] for s in range(N_CHIPS)], axis=1)
    mod_lat = lax.dynamic_slice(mod_full, (dev, 0), (1, N_MOD * D)).reshape(N_MOD, D)
    mod_ctx = mod_full[CTX_ROW].reshape(N_MOD, D)
    mods = jnp.stack([_pad_rows(mod_ctx, 16), _pad_rows(mod_lat, 16)])

    ffn1_w = _gather_end(ffn1_gather, mods, "ffn1")
    hooks = _Exchanges(fulls_rest, idx)

    xcat = (ctx[0], x[0])
    norm1_first = _after(norm1_g, hooks.token, "after_ag_rest")
    grad_x, grads, accs = _local_step(xcat, loss_target[0], mods, (norm1_first, norm2_g, norm3_g), final_g[None],
                                      q_norm_g, k_norm_g, conv_full, ffn1_w, hooks, rope)
    g = {}

    pack = jnp.concatenate([a.reshape(2 * ACC_ROWS, D) for a in accs], axis=0)
    gathered = _allgather8(pack, "ag_small")
    loss8, db_mod, g_norms, g_conv, g_qk, dm = _small_reduce(gathered, "small_reduce")
    dm_sh = lax.dynamic_slice(dm[:, :N_MOD, :].reshape(16, N_MOD * D), (0, shard * mcols), (16, mcols))
    g_wmod, cpart = _wmod_grad(cin, dm_sh, w_mod[0], "wmod_grad")
    g["w_mod"] = g_wmod[None]
    cparts = _allgather8(cpart[CTX_ROW:CTX_ROW + 8], "ag_cctx")
    g_cctx = _cctx_grad(cparts, _pad_rows(c_ctx[None], 8), "cctx_grad")
    g_conv_sh = lax.dynamic_slice(g_conv, (0, shard * ccols), (conv_w.shape[1], ccols))
    g_misc = jnp.concatenate([g_qk[0:1, 0:2 * HEAD_DIM], g_conv_sh.reshape(1, -1)], axis=1)
    g_pack = jnp.concatenate([g_cctx, db_mod, g_norms, _pad_rows(g_misc, 8)], axis=0)

    def packed(p):
        return _pack_small(p["c_ctx"], p["b_mod"], p["norm1_g"], p["norm2_g"], p["norm3_g"], p["final_g"],
                           p["q_norm_g"], p["k_norm_g"], p["conv_w"][0], D)

    d_pack, m_pack, v_pack = _adamw(packed(w), g_pack, packed(m), packed(v), "adamw_small")

    g.update(_unpack_small(g_pack, D, conv_w.shape))
    delta = _unpack_small(d_pack, D, conv_w.shape)
    new_m = _unpack_small(m_pack, D, conv_w.shape)
    new_v = _unpack_small(v_pack, D, conv_w.shape)

    def update(n, g2):
        if n in COLUMN_SHARDED:
            g2, d2, m2, v2 = _adamw_transposed(w[n][0], g2, m[n][0], v[n][0], "adamw_" + n)
        else:
            d2, m2, v2 = _adamw(w[n][0], g2, m[n][0], v[n][0], "adamw_" + n)
        g[n], delta[n], new_m[n], new_v[n] = g2[None], d2[None], m2[None], v2[None]
        return v2

    token_d = hooks.reduce_early([grads[0]], "d", token=d_pack[:8, :HEAD_DIM])
    h_wbc, h_wba, h_wo, h_w2i, h_w2o, h_wi, h_w1o = hooks.finish("abc", token_d)
    done = _pair_swap([h_w1o, h_wi, h_wbc, h_wba, h_wo, h_w2i, h_w2o], "rs_pair_swap")
    updated = [update("w_mod", g_wmod)] + [update(n, r) for n, r in zip(BIG[1:], done)]
    (h_w1i,) = hooks.finish("d", updated)
    update(BIG[0], _pair_swap([h_w1i], "rs_pair_swap_d")[0])

    loss = loss8[0, 0]
    return (loss, grad_x[None], *[g[n] for n in WEIGHT_ORDER], *[delta[n] for n in WEIGHT_ORDER],
            *[new_m[n] for n in WEIGHT_ORDER], *[new_v[n] for n in WEIGHT_ORDER])
```

```python
import functools

import jax
import jax.numpy as jnp
from jax import lax
from jax.experimental import pallas as pl
from jax.experimental.pallas import tpu as pltpu

F32 = jnp.float32
BF16 = jnp.bfloat16

HEAD_DIM = 128
N_Q_HEADS = 8
N_KV_HEADS = 2
GROUP = N_Q_HEADS // N_KV_HEADS
GRID_W = 64
ROPE_THETA = 10000.0
EPS = 1e-6
ATTN_SCALE = HEAD_DIM ** -0.5

ADAM_LR = 0.001
ADAM_B1 = 0.9
ADAM_B2 = 0.999
ADAM_EPS = 1e-08
ADAM_WD = 0.01
ADAM_STEP = 10

LANES = 128
ROW = 256
HALO = 16
ACC_ROWS = 8
N_CHIPS = 4
N_DEV = 8
MESH = pl.DeviceIdType.MESH
VMEM_LIMIT = 48 * 1024 * 1024
ADAMW_BLOCK_BYTES = 1024 * 1024


def _pick(n, prefs):
    for p in prefs:
        if n % p == 0:
            return p
    return n


def _params(sem):
    return pltpu.CompilerParams(dimension_semantics=sem, vmem_limit_bytes=VMEM_LIMIT)


def _stream(i):
    return jnp.minimum(i, 1)


def _grad_matmul(a, b, name, token=None):
    (T, M), (T2, N) = a.shape, b.shape
    assert T == T2, (a.shape, b.shape)
    tm = _pick(M, (1664, 1408, 1024, 512, 256, 128))
    tk = _pick(T, (2816, 1408, 768, 512, 256))
    nk = T // tk
    extra = [] if token is None else [token]

    def body(a_ref, b_ref, *rest):
        o_ref, acc_ref = rest[len(extra):]
        p = lax.dot_general(a_ref[...].astype(BF16), b_ref[...].astype(BF16), (((0,), (0,)), ((), ())),
                            preferred_element_type=F32)
        k = pl.program_id(1)

        @pl.when(k == 0)
        def _():
            acc_ref[...] = p

        @pl.when(k > 0)
        def _():
            acc_ref[...] += p

        @pl.when(k == nk - 1)
        def _():
            o_ref[...] = acc_ref[...].astype(BF16)

    return pl.pallas_call(
        body, name=name, grid=(M // tm, nk),
        in_specs=[pl.BlockSpec((tk, tm), lambda i, k: (k, i)), pl.BlockSpec((tk, N), lambda i, k: (k, 0))] +
                 [pl.BlockSpec(t.shape, lambda i, k: (0, 0)) for t in extra],
        out_specs=pl.BlockSpec((tm, N), lambda i, k: (i, 0)),
        out_shape=jax.ShapeDtypeStruct((M, N), BF16),
        scratch_shapes=[pltpu.VMEM((tm, N), F32)],
        compiler_params=_params(("parallel", "arbitrary")),
    )(a, b, *extra)


def _row_spec(width, col=0):
    return pl.BlockSpec((ROW, width), lambda i, col=col: (i, col))


def _mods_spec(D):
    return pl.BlockSpec((1, 16, D), lambda i: (_stream(i), 0, 0))


def _acc_spec(D):
    return pl.BlockSpec((1, ACC_ROWS, D), lambda i: (_stream(i), 0, 0))


def _vec_spec(rows, D):
    return pl.BlockSpec((rows, D), lambda i: (0, 0))


def _acc_init(acc_ref):
    i = pl.program_id(0)

    @pl.when(i <= 1)
    def _():
        acc_ref[...] = jnp.zeros_like(acc_ref)


def _acc_add(acc_ref, row, val):
    acc_ref[0, row:row + 1, :] += jnp.sum(val, axis=0, keepdims=True)


def _rows_operand(x):
    if not isinstance(x, tuple):
        return [_row_spec(x.shape[1])], [x], x.shape
    ctx, lat = x
    D = lat.shape[1]
    assert ctx.shape == (ROW, D)
    specs = [pl.BlockSpec((ROW, D), lambda i: (0, 0)), pl.BlockSpec((ROW, D), lambda i: (jnp.maximum(i - 1, 0), 0))]
    return specs, [ctx, lat], (ROW + lat.shape[0], D)


def _rows_tile(refs):
    if len(refs) == 1:
        return refs[0][...]
    return jnp.where(pl.program_id(0) == 0, refs[0][...], refs[1][...])


def _norm_tile_fwd(x, m, g, shift_idx, scale_idx):
    inv = lax.rsqrt(jnp.mean(x * x, axis=-1, keepdims=True) + EPS)
    y = (x * inv) * g
    return (y * (1.0 + m[scale_idx:scale_idx + 1, :]) + m[shift_idx:shift_idx + 1, :]).astype(BF16)


def _norm_tile_bwd(x, dh, dres, m, g, shift_idx, scale_idx, acc_ref):
    inv = lax.rsqrt(jnp.mean(x * x, axis=-1, keepdims=True) + EPS)
    xn = x * inv
    dy = dh * (1.0 + m[scale_idx:scale_idx + 1, :])
    dxn = dy * g
    _acc_add(acc_ref, 0, dh)
    _acc_add(acc_ref, 1, dh * (xn * g))
    _acc_add(acc_ref, 2, dy * xn)
    return inv * (dxn - xn * jnp.mean(dxn * xn, axis=-1, keepdims=True)) + dres


def _gate_tile_bwd(dx, branch, m, gate, acc_ref):
    gate_idx, fac = gate
    _acc_add(acc_ref, 3, fac * dx * branch)
    return ((fac * m[gate_idx:gate_idx + 1, :]) * dx).astype(BF16)


_NT = (((1,), (1,)), ((), ()))


def _ffn_chunk(F):
    return _pick(F, (2816, 1408, 512, 256, 128))


def _resident():
    return pl.BlockSpec(memory_space=pltpu.VMEM)


def _ffn_tile_fwd(hv, wi_ref, wo_ref, u_ref, s_ref, F, cw):
    acc = jnp.zeros((hv.shape[0], wo_ref.shape[1]), F32)
    for j in range(F // cw):
        a = lax.dot_general(hv, wi_ref[j * cw:(j + 1) * cw, :], _NT, preferred_element_type=F32)
        b = lax.dot_general(hv, wi_ref[F + j * cw:F + (j + 1) * cw, :], _NT, preferred_element_type=F32)
        s = ((a * jax.nn.sigmoid(a)) * b).astype(BF16)
        u_ref[:, j * cw:(j + 1) * cw] = a.astype(BF16)
        u_ref[:, F + j * cw:F + (j + 1) * cw] = b.astype(BF16)
        s_ref[:, j * cw:(j + 1) * cw] = s
        acc = acc + jnp.dot(s, wo_ref[j * cw:(j + 1) * cw, :], preferred_element_type=F32)
    return acc


def _norm_ffn_fwd(xprev, branch, mods, g, gate, shift_idx, scale_idx, w_in_t, w_out, name, head=None):
    x_specs, x_args, (T, D) = _rows_operand(xprev)
    F = w_out.shape[0]
    cw = _ffn_chunk(F)
    has_res = branch is not None
    n_in = len(x_args) + int(has_res) + 4 + (2 if head else 0)

    def body(*refs):
        ins, outs = list(refs[:n_in]), list(refs[n_in:])
        x = _rows_tile([ins.pop(0) for _ in x_args])
        f_ref = ins.pop(0) if has_res else None
        m_ref, g_ref, wi_ref, wo_ref = ins[:4]
        xo_ref = outs.pop(0) if has_res else None
        h_ref, u_ref, s_ref = outs[:3]
        m = m_ref[0]
        if has_res:
            gate_idx, fac = gate
            x = x + (fac * m[gate_idx:gate_idx + 1, :]) * f_ref[...]
            xo_ref[...] = x
        hv = _norm_tile_fwd(x, m, g_ref[...], shift_idx, scale_idx)
        h_ref[...] = hv
        f = _ffn_tile_fwd(hv, wi_ref, wo_ref, u_ref, s_ref, F, cw)
        if head is None:
            outs[3][...] = f
            return
        fg_ref, t_ref = ins[4:6]
        dx_ref, df_ref, acc_ref = outs[3:6]
        _acc_init(acc_ref)
        lat = (pl.program_id(0) > 0).astype(F32)
        gate8 = 0.5 * m[8:9, :]
        x3 = x + gate8 * f
        inv3 = lax.rsqrt(jnp.mean(x3 * x3, axis=-1, keepdims=True) + EPS)
        xn = x3 * inv3
        fg = fg_ref[...]
        e = (xn * fg - t_ref[...]) * lat
        dy = e * (1.0 / D)
        dxn = dy * fg
        dx = inv3 * (dxn - xn * jnp.mean(dxn * xn, axis=-1, keepdims=True))
        dx_ref[...] = dx
        df_ref[...] = (gate8 * dx).astype(BF16)
        _acc_add(acc_ref, 0, (0.5 / D) * e * e)
        _acc_add(acc_ref, 1, dy * xn)
        _acc_add(acc_ref, 2, 0.5 * dx * f)

    in_specs = x_specs + ([_row_spec(D)] if has_res else []) + \
               [_mods_spec(D), _vec_spec(1, D), _resident(), _resident()]
    args = x_args + ([branch] if has_res else []) + [mods, g, w_in_t, w_out]
    out_specs = ([_row_spec(D)] if has_res else []) + [_row_spec(D), _row_spec(2 * F), _row_spec(F)]
    out_shape = ([jax.ShapeDtypeStruct((T, D), F32)] if has_res else []) + \
                [jax.ShapeDtypeStruct((T, D), BF16), jax.ShapeDtypeStruct((T, 2 * F), BF16),
                 jax.ShapeDtypeStruct((T, F), BF16)]
    if head is None:
        out_specs += [_row_spec(D)]
        out_shape += [jax.ShapeDtypeStruct((T, D), F32)]
    else:
        in_specs += [_vec_spec(1, D), pl.BlockSpec((ROW, D), lambda i: (jnp.maximum(i - 1, 0), 0))]
        args += list(head)
        out_specs += [_row_spec(D), _row_spec(D), _acc_spec(D)]
        out_shape += [jax.ShapeDtypeStruct((T, D), F32), jax.ShapeDtypeStruct((T, D), BF16),
                      jax.ShapeDtypeStruct((2, ACC_ROWS, D), F32)]
    out = pl.pallas_call(
        body, name=name, grid=(T // ROW,), in_specs=in_specs, out_specs=out_specs, out_shape=out_shape,
        compiler_params=_params(("arbitrary",) if head else ("parallel",)),
    )(*args)
    return tuple(out) if has_res else (None,) + tuple(out)


def _ffn_norm_bwd(df, u, w_in_t, w_out, x, dres, mods, g, shift_idx, scale_idx, gate, branch, name,
                  skip_first_tile=False):
    T, D = df.shape
    F = w_out.shape[0]
    cw = _ffn_chunk(F)
    nt = T // ROW
    has_gate = gate is not None
    x_specs, x_args, _ = _rows_operand(x)
    n_in = 7 + len(x_args) + int(has_gate)

    def body(*refs):
        ins, outs = list(refs[:n_in]), list(refs[n_in:])
        df_ref, u_ref, wi_ref, wo_ref = ins[:4]
        x_refs = ins[4:4 + len(x_args)]
        dr_ref = ins[4 + len(x_args)]
        b_ref = ins[5 + len(x_args)] if has_gate else None
        m_ref, g_ref = ins[-2:]
        du_ref, dx_ref = outs[:2]
        db_ref = outs[2] if has_gate else None
        acc_ref = outs[-1]
        _acc_init(acc_ref)
        dfv = df_ref[...]
        dh = jnp.zeros((ROW, D), F32)
        for j in range(F // cw):
            ds = lax.dot_general(dfv, wo_ref[j * cw:(j + 1) * cw, :], _NT, preferred_element_type=F32)
            a = u_ref[:, j * cw:(j + 1) * cw].astype(F32)
            b = u_ref[:, F + j * cw:F + (j + 1) * cw].astype(F32)
            sig = jax.nn.sigmoid(a)
            da = (ds * b * (sig * (1.0 + a * (1.0 - sig)))).astype(BF16)
            db = (ds * (a * sig)).astype(BF16)
            du_ref[:, j * cw:(j + 1) * cw] = da
            du_ref[:, F + j * cw:F + (j + 1) * cw] = db
            dh = dh + jnp.dot(da, wi_ref[j * cw:(j + 1) * cw, :], preferred_element_type=F32)
            dh = dh + jnp.dot(db, wi_ref[F + j * cw:F + (j + 1) * cw, :], preferred_element_type=F32)
        m = m_ref[0]
        dx = _norm_tile_bwd(_rows_tile(x_refs), dh, dr_ref[...], m, g_ref[...], shift_idx, scale_idx, acc_ref)
        dx_ref[...] = dx
        if has_gate:
            db_ref[...] = _gate_tile_bwd(dx, b_ref[...], m, gate, acc_ref)

    in_specs = [_row_spec(D), _row_spec(2 * F), _resident(), _resident()] + x_specs + [_row_spec(D)] + \
               ([_row_spec(D)] if has_gate else []) + [_mods_spec(D), _vec_spec(1, D)]
    args = [df, u, w_in_t, w_out] + x_args + [dres] + ([branch] if has_gate else []) + [mods, g]
    if skip_first_tile:
        dx_spec = pl.BlockSpec((ROW, D), lambda i: (jnp.maximum(i - 1, 0), 0))
        dx_shape = jax.ShapeDtypeStruct((T - ROW, D), F32)
    else:
        dx_spec = _row_spec(D)
        dx_shape = jax.ShapeDtypeStruct((T, D), F32)
    out_specs = [_row_spec(2 * F), dx_spec] + ([_row_spec(D)] if has_gate else []) + [_acc_spec(D)]
    out_shape = [jax.ShapeDtypeStruct((T, 2 * F), BF16), dx_shape] + \
                ([jax.ShapeDtypeStruct((T, D), BF16)] if has_gate else []) + \
                [jax.ShapeDtypeStruct((2, ACC_ROWS, D), F32)]
    out = pl.pallas_call(
        body, name=name, grid=(nt,), in_specs=in_specs, out_specs=out_specs, out_shape=out_shape,
        compiler_params=_params(("arbitrary",)),
    )(*args)
    if has_gate:
        return tuple(out)
    return out[0], out[1], None, out[2]


def _halo_specs(width, col, nt):
    per = ROW // HALO
    prev = pl.BlockSpec((HALO, width), lambda i, col=col: (jnp.maximum(i * per - 1, 0), col))
    nxt = pl.BlockSpec((HALO, width), lambda i, col=col: (jnp.minimum((i + 1) * per, nt * per - 1), col))
    return prev, nxt


def _f32(ref):
    return ref[...].astype(F32)


def _last_row(halo_ref):
    return halo_ref[HALO - 1:HALO, :].astype(F32)


def _first_row(halo_ref):
    return halo_ref[0:1, :].astype(F32)


def _shift_rows(v, prev_row, next_row):
    rows = lax.broadcasted_iota(jnp.int32, v.shape, 0)
    down = jnp.where(rows == 0, prev_row, pltpu.roll(v, 1, 0))
    up = jnp.where(rows == v.shape[0] - 1, next_row, pltpu.roll(v, v.shape[0] - 1, 0))
    return down, up


def _conv_fwd_operands(P, conv_w, D):
    nt = P.shape[0] // ROW
    cg_p, cg_n = _halo_specs(D, 1, nt)
    vc_p, vc_n = _halo_specs(D, 2, nt)
    specs = [_row_spec(D, 0), _row_spec(D, 1), _row_spec(D, 2), cg_p, vc_p, cg_n, vc_n, _vec_spec(3, D)]
    return specs, [P, P, P, P, P, P, P, conv_w]


def _conv_tile_fwd(refs, nt):
    bg_ref, cg_ref, vc_ref, cgp_ref, vcp_ref, cgn_ref, vcn_ref, w_ref = refs
    i = pl.program_id(0)
    has_prev = (i != 1).astype(F32)
    has_next = (i != nt - 1).astype(F32)
    u = _f32(cg_ref) * _f32(vc_ref)
    up_row = _last_row(cgp_ref) * _last_row(vcp_ref) * has_prev
    un_row = _first_row(cgn_ref) * _first_row(vcn_ref) * has_next
    um1, up1 = _shift_rows(u, up_row, un_row)
    w = w_ref[...]
    conv = um1 * w[0:1, :] + u * w[1:2, :] + up1 * w[2:3, :]
    return (_f32(bg_ref) * conv).astype(BF16)


def _conv_bwd_operands(P, dy, conv_w, D):
    nt = P.shape[0] // ROW
    bg_p, bg_n = _halo_specs(D, 0, nt)
    cg_p, cg_n = _halo_specs(D, 1, nt)
    vc_p, vc_n = _halo_specs(D, 2, nt)
    dy_p, dy_n = _halo_specs(D, 0, nt)
    specs = [_row_spec(D, 0), _row_spec(D, 1), _row_spec(D, 2), _row_spec(D, 0),
             bg_p, cg_p, vc_p, dy_p, bg_n, cg_n, vc_n, dy_n, _vec_spec(3, D)]
    return specs, [P, P, P, dy, P, P, P, dy, P, P, P, dy, conv_w]


def _conv_tile_bwd(refs, o_ref, acc_ref, D, nt):
    (bg_ref, cg_ref, vc_ref, dy_ref, bgp_ref, cgp_ref, vcp_ref, dyp_ref,
     bgn_ref, cgn_ref, vcn_ref, dyn_ref, w_ref) = refs
    i = pl.program_id(0)
    lat = (i > 0).astype(F32)
    has_prev = (i != 1).astype(F32)
    has_next = (i != nt - 1).astype(F32)
    bg = _f32(bg_ref)
    cg = _f32(cg_ref)
    vc = _f32(vc_ref)
    dyv = dy_ref[...] * lat
    u = cg * vc
    up_row = _last_row(cgp_ref) * _last_row(vcp_ref) * has_prev
    un_row = _first_row(cgn_ref) * _first_row(vcn_ref) * has_next
    um1, up1 = _shift_rows(u, up_row, un_row)
    w = w_ref[...]
    conv = um1 * w[0:1, :] + u * w[1:2, :] + up1 * w[2:3, :]
    dc = dyv * bg
    dcp_row = _last_row(dyp_ref) * _last_row(bgp_ref) * has_prev
    dcn_row = _first_row(dyn_ref) * _first_row(bgn_ref) * has_next
    dcm1, dcp1 = _shift_rows(dc, dcp_row, dcn_row)
    du = dcp1 * w[0:1, :] + dc * w[1:2, :] + dcm1 * w[2:3, :]
    o_ref[:, 0:D] = (dyv * conv).astype(BF16)
    o_ref[:, D:2 * D] = (du * vc * lat).astype(BF16)
    o_ref[:, 2 * D:3 * D] = (du * cg * lat).astype(BF16)
    _acc_add(acc_ref, 0, dc * um1)
    _acc_add(acc_ref, 1, dc * u)
    _acc_add(acc_ref, 2, dc * up1)


def _rope_tables(ctx_len, seq):
    n_freq = HEAD_DIM // 4
    rows = seq // GRID_W
    inv = ROPE_THETA ** (-jnp.arange(n_freq, dtype=F32) / n_freq)
    ar = jnp.arange(rows, dtype=F32)[:, None] * inv
    ac = jnp.arange(GRID_W, dtype=F32)[:, None] * inv

    def per_row(a):
        return jnp.repeat(a, GRID_W, axis=0)

    def per_col(a):
        return jnp.tile(a, (rows, 1))

    cos_t = jnp.concatenate([per_row(jnp.cos(ar)), per_row(jnp.cos(ar)), per_col(jnp.cos(ac)), per_col(jnp.cos(ac))], axis=1)
    sin_t = jnp.concatenate([per_row(-jnp.sin(ar)), per_row(jnp.sin(ar)), per_col(-jnp.sin(ac)), per_col(jnp.sin(ac))], axis=1)
    cos_t = jnp.concatenate([jnp.ones((ctx_len, HEAD_DIM), F32), cos_t], axis=0)
    sin_t = jnp.concatenate([jnp.zeros((ctx_len, HEAD_DIM), F32), sin_t], axis=0)
    return cos_t, sin_t


def _swap_halves(y):
    lanes = lax.broadcasted_iota(jnp.int32, y.shape, 1)
    first = (lanes % 64) < 32
    return jnp.where(first, pltpu.roll(y, HEAD_DIM - 32, 1), pltpu.roll(y, 32, 1))


def _to_row(col, n):
    return jnp.transpose(jnp.broadcast_to(col, (n, HEAD_DIM)))[0:1, :]


LOG2E = 1.4426950408889634
ATTN_PART_LANES = 256
ATTN_QUERY_ROWS = 768
ATTN_VMEM_LIMIT = 60 * 1024 * 1024


def _flash_fwd(q, k, v, name, tq=None, tk=None, token=None):
    extra = [] if token is None else [token]
    T = q.shape[0]
    tq = tq or _pick(T, (ATTN_QUERY_ROWS, ROW))
    parts = GROUP * tq // ATTN_PART_LANES
    tk = tk or _pick(T, (2816, 1408, 768, 512, 256))
    ck = tk
    nk = T // tk
    GW = GROUP * HEAD_DIM

    def body(q_ref, k_ref, v_ref, *rest):
        o_ref, lse_ref, qs_ref, m_ref, l_ref, acc_ref, st_ref = rest[len(extra):]
        ki = pl.program_id(2)

        @pl.when(ki == 0)
        def _():
            for g in range(GROUP):
                qs_ref[g * tq:(g + 1) * tq, :] = q_ref[:, g * HEAD_DIM:(g + 1) * HEAD_DIM]
            m_ref[...] = jnp.full(m_ref.shape, -jnp.inf, F32)
            l_ref[...] = jnp.zeros(l_ref.shape, F32)
            acc_ref[...] = jnp.zeros(acc_ref.shape, F32)

        w = ATTN_PART_LANES
        nck = tk // ck

        def lanes(p):
            return slice(p * w, (p + 1) * w)

        def keys(c):
            return slice(c * ck, (c + 1) * ck)

        def fold(a):
            return a.reshape(ck // 8, 8, w)

        def scores(p, c):
            st = lax.dot_general(k_ref[keys(c), :], qs_ref[lanes(p), :], _NT,
                                 preferred_element_type=F32) * (ATTN_SCALE * LOG2E)
            st_ref[keys(c), lanes(p)] = st
            return jnp.max(fold(st), axis=0)

        def new_max(p, partial):
            m_prev = m_ref[:, lanes(p)]
            m_new = jnp.maximum(m_prev, jnp.max(functools.reduce(jnp.maximum, partial), axis=0, keepdims=True))
            m_ref[:, lanes(p)] = m_new
            return m_new, jnp.exp2(m_prev - m_new)

        def weights(p, c, m_new):
            pt = jnp.exp2(st_ref[keys(c), lanes(p)] - m_new)
            pv = lax.dot_general(v_ref[keys(c), :], pt.astype(BF16), (((0,), (0,)), ((), ())),
                                 preferred_element_type=F32)
            return jnp.sum(fold(pt), axis=0), pv

        partial = [scores(0, c) for c in range(nck)]
        for p in range(parts):
            m_new, alpha = new_max(p, partial)
            partial, sums, pvs = [], [], []
            for c in range(nck):
                if p + 1 < parts:
                    partial.append(scores(p + 1, c))
                s8, pv = weights(p, c, m_new)
                sums.append(s8)
                pvs.append(pv)
            l_ref[:, lanes(p)] = alpha * l_ref[:, lanes(p)] + jnp.sum(sum(sums), axis=0, keepdims=True)
            acc_ref[:, lanes(p)] = alpha * acc_ref[:, lanes(p)] + sum(pvs)

        @pl.when(ki == nk - 1)
        def _():
            out = jnp.transpose(acc_ref[...] / l_ref[...])
            lse = m_ref[...] + jnp.log2(l_ref[...])
            for g in range(GROUP):
                o_ref[:, g * HEAD_DIM:(g + 1) * HEAD_DIM] = out[g * tq:(g + 1) * tq, :]
                lse_ref[0, g:g + 1, :] = lse[:, g * tq:(g + 1) * tq]

    return pl.pallas_call(
        body, name=name, grid=(N_KV_HEADS, T // tq, nk),
        in_specs=[pl.BlockSpec((tq, GW), lambda h, i, j: (i, h)),
                  pl.BlockSpec((tk, HEAD_DIM), lambda h, i, j: (j, h)),
                  pl.BlockSpec((tk, HEAD_DIM), lambda h, i, j: (j, h))] +
                 [pl.BlockSpec(t.shape, lambda h, i, j: (0, 0)) for t in extra],
        out_specs=[pl.BlockSpec((tq, GW), lambda h, i, j: (i, h)),
                   pl.BlockSpec((1, GROUP, tq), lambda h, i, j: (h, 0, i))],
        out_shape=[jax.ShapeDtypeStruct((T, N_Q_HEADS * HEAD_DIM), F32),
                   jax.ShapeDtypeStruct((N_KV_HEADS, GROUP, T), F32)],
        scratch_shapes=[pltpu.VMEM((GROUP * tq, HEAD_DIM), BF16), pltpu.VMEM((1, GROUP * tq), F32),
                        pltpu.VMEM((1, GROUP * tq), F32), pltpu.VMEM((HEAD_DIM, GROUP * tq), F32),
                        pltpu.VMEM((tk, GROUP * tq), F32)],
        compiler_params=pltpu.CompilerParams(dimension_semantics=("parallel", "parallel", "arbitrary"),
                                             vmem_limit_bytes=ATTN_VMEM_LIMIT),
    )(q, k, v, *extra)


def _flash_bwd(q, k, v, do, lse, delta, name, tq=None, tk=None, token=None):
    T = q.shape[0]
    tq = tq or _pick(T, (ATTN_QUERY_ROWS, ROW))
    tk = tk or _pick(T, (1408, 768, 512, 256))
    nk = T // tk
    GW = GROUP * HEAD_DIM
    nt = (((1,), (1,)), ((), ()))
    extra = [] if token is None else [token]

    def body(q_ref, do_ref, k_ref, v_ref, lse_ref, dl_ref, *rest):
        dq_ref, dk_ref, dv_ref, qs_ref, dos_ref, dqt_ref = rest[len(extra):]
        qi = pl.program_id(1)
        ki = pl.program_id(2)

        @pl.when(ki == 0)
        def _():
            for g in range(GROUP):
                qs_ref[g * tq:(g + 1) * tq, :] = q_ref[:, g * HEAD_DIM:(g + 1) * HEAD_DIM]
                dos_ref[g * tq:(g + 1) * tq, :] = do_ref[:, g * HEAD_DIM:(g + 1) * HEAD_DIM]
            dqt_ref[...] = jnp.zeros(dqt_ref.shape, F32)

        kk = k_ref[...]
        vv = v_ref[...]

        def lanes(p):
            return slice(p * tq, (p + 1) * tq)

        def products(p):
            st = lax.dot_general(kk, qs_ref[lanes(p), :], nt, preferred_element_type=F32)
            dpt = lax.dot_general(vv, dos_ref[lanes(p), :], nt, preferred_element_type=F32)
            return st, dpt

        dk_c = jnp.zeros((tk, HEAD_DIM), F32)
        dv_c = jnp.zeros((tk, HEAD_DIM), F32)
        ahead = products(0)
        for p in range(GROUP):
            st, dpt = ahead
            if p + 1 < GROUP:
                ahead = products(p + 1)
            pt = jnp.exp2(st * (ATTN_SCALE * LOG2E) - lse_ref[0, p:p + 1, :])
            dst = ((pt * (dpt - dl_ref[0, p:p + 1, :])) * ATTN_SCALE).astype(BF16)
            dv_c = dv_c + jnp.dot(pt.astype(BF16), dos_ref[lanes(p), :], preferred_element_type=F32)
            dk_c = dk_c + jnp.dot(dst, qs_ref[lanes(p), :], preferred_element_type=F32)
            dqt_ref[:, lanes(p)] += lax.dot_general(kk, dst, (((0,), (0,)), ((), ())), preferred_element_type=F32)
        rows = pl.ds(pl.multiple_of(ki * tk, tk), tk)

        @pl.when(qi == 0)
        def _():
            dk_ref[rows, :] = dk_c
            dv_ref[rows, :] = dv_c

        @pl.when(qi > 0)
        def _():
            dk_ref[rows, :] += dk_c
            dv_ref[rows, :] += dv_c

        @pl.when(ki == nk - 1)
        def _():
            dqv = jnp.transpose(dqt_ref[...])
            for g in range(GROUP):
                dq_ref[:, g * HEAD_DIM:(g + 1) * HEAD_DIM] = dqv[g * tq:(g + 1) * tq, :]

    return pl.pallas_call(
        body, name=name, grid=(N_KV_HEADS, T // tq, nk),
        in_specs=[pl.BlockSpec((tq, GW), lambda h, i, j: (i, h)),
                  pl.BlockSpec((tq, GW), lambda h, i, j: (i, h)),
                  pl.BlockSpec((tk, HEAD_DIM), lambda h, i, j: (j, h)),
                  pl.BlockSpec((tk, HEAD_DIM), lambda h, i, j: (j, h)),
                  pl.BlockSpec((1, GROUP, tq), lambda h, i, j: (h, 0, i)),
                  pl.BlockSpec((1, GROUP, tq), lambda h, i, j: (h, 0, i))] +
                 [pl.BlockSpec(t.shape, lambda h, i, j: (0, 0)) for t in extra],
        out_specs=[pl.BlockSpec((tq, GW), lambda h, i, j: (i, h)),
                   pl.BlockSpec((T, HEAD_DIM), lambda h, i, j: (0, h)),
                   pl.BlockSpec((T, HEAD_DIM), lambda h, i, j: (0, h))],
        out_shape=[jax.ShapeDtypeStruct((T, N_Q_HEADS * HEAD_DIM), F32),
                   jax.ShapeDtypeStruct((T, N_KV_HEADS * HEAD_DIM), F32),
                   jax.ShapeDtypeStruct((T, N_KV_HEADS * HEAD_DIM), F32)],
        scratch_shapes=[pltpu.VMEM((GROUP * tq, HEAD_DIM), BF16), pltpu.VMEM((GROUP * tq, HEAD_DIM), BF16),
                        pltpu.VMEM((HEAD_DIM, GROUP * tq), F32)],
        compiler_params=pltpu.CompilerParams(dimension_semantics=("arbitrary", "arbitrary", "arbitrary"),
                                             vmem_limit_bytes=ATTN_VMEM_LIMIT),
    )(q, do, k, v, lse, delta, *extra)


def _gate_specs(D):
    w = D // 2
    first = (3 * D + (N_Q_HEADS + 2 * N_KV_HEADS) * HEAD_DIM) // w
    return [pl.BlockSpec((ROW, w), lambda i, c=first + j: (i, c)) for j in range(4)]


def _merge_fwd(o, P, conv_w, wbc, wba, wo, D, name):
    T = o.shape[0]
    nt = T // ROW
    w = D // 2
    conv_specs, conv_args = _conv_fwd_operands(P, conv_w, D)
    nc = len(conv_args)

    def body(*refs):
        o_ref, g0, g1, g2, g3, wbc_ref, wba_ref, wo_ref, yc_ref, a1_ref, a2_ref, z_ref, mo_ref = refs[nc:]
        yc_ref[...] = _conv_tile_fwd(refs[:nc], nt)
        a1 = jnp.dot(yc_ref[...], wbc_ref[...], preferred_element_type=F32)
        a2 = jnp.dot(o_ref[...].astype(BF16), wba_ref[...], preferred_element_type=F32)
        a1_ref[...] = a1.astype(BF16)
        a2_ref[...] = a2.astype(BF16)
        for j, (gc, ga) in enumerate(((g0, g2), (g1, g3))):
            sl = slice(j * w, (j + 1) * w)
            z = jax.nn.sigmoid(_f32(gc)) * a1[:, sl] + jax.nn.sigmoid(_f32(ga)) * a2[:, sl]
            z_ref[:, sl] = z.astype(BF16)
        mo_ref[...] = jnp.dot(z_ref[...], wo_ref[...], preferred_element_type=F32)

    return pl.pallas_call(
        body, name=name, grid=(T // ROW,),
        in_specs=conv_specs + [_row_spec(D)] + _gate_specs(D) + [_resident()] * 3,
        out_specs=[_row_spec(D)] * 5,
        out_shape=[jax.ShapeDtypeStruct((T, D), BF16), jax.ShapeDtypeStruct((T, D), BF16),
                   jax.ShapeDtypeStruct((T, D), BF16), jax.ShapeDtypeStruct((T, D), BF16),
                   jax.ShapeDtypeStruct((T, D), F32)],
        compiler_params=_params(("parallel",)),
    )(*conv_args, o, P, P, P, P, wbc, wba, wo)


def _merge_bwd(dmo, a1, a2, o, P, wbc, wba, wo, D, name):
    T = a1.shape[0]
    w = D // 2

    def body(dmo_ref, a1_ref, a2_ref, o_ref, g0, g1, g2, g3, wbc_ref, wba_ref, wo_ref,
             d1_ref, d2_ref, dg_ref, dyc_ref, dob_ref, dl_ref):
        dz = lax.dot_general(dmo_ref[...], wo_ref[...], _NT, preferred_element_type=F32)
        for j, (gc, ga) in enumerate(((g0, g2), (g1, g3))):
            sl = slice(j * w, (j + 1) * w)
            dzs = dz[:, sl]
            sc = jax.nn.sigmoid(_f32(gc))
            sa = jax.nn.sigmoid(_f32(ga))
            d1_ref[:, sl] = (dzs * sc).astype(BF16)
            d2_ref[:, sl] = (dzs * sa).astype(BF16)
            dg_ref[:, j * w:(j + 1) * w] = (dzs * a1_ref[:, sl].astype(F32) * (sc * (1.0 - sc))).astype(BF16)
            dg_ref[:, D + j * w:D + (j + 1) * w] = (dzs * a2_ref[:, sl].astype(F32) * (sa * (1.0 - sa))).astype(BF16)
        dyc_ref[...] = lax.dot_general(d1_ref[...], wbc_ref[...], _NT, preferred_element_type=F32)
        dov = lax.dot_general(d2_ref[...], wba_ref[...], _NT, preferred_element_type=F32)
        dob_ref[...] = dov.astype(BF16)
        prod = dov * o_ref[...]
        for h in range(N_Q_HEADS):
            d = jnp.sum(prod[:, h * HEAD_DIM:(h + 1) * HEAD_DIM], axis=1, keepdims=True)
            dl_ref[h // GROUP, (h % GROUP):(h % GROUP) + 1, :] = _to_row(d, ROW)

    return pl.pallas_call(
        body, name=name, grid=(T // ROW,),
        in_specs=[_row_spec(D)] * 4 + _gate_specs(D) + [_resident()] * 3,
        out_specs=[_row_spec(D), _row_spec(D), _row_spec(2 * D), _row_spec(D), _row_spec(D),
                   pl.BlockSpec((N_KV_HEADS, GROUP, ROW), lambda i: (0, 0, i))],
        out_shape=[jax.ShapeDtypeStruct((T, D), BF16), jax.ShapeDtypeStruct((T, D), BF16),
                   jax.ShapeDtypeStruct((T, 2 * D), BF16), jax.ShapeDtypeStruct((T, D), F32),
                   jax.ShapeDtypeStruct((T, D), BF16), jax.ShapeDtypeStruct((N_KV_HEADS, GROUP, T), F32)],
        compiler_params=_params(("parallel",)),
    )(dmo, a1, a2, o, P, P, P, P, wbc, wba, wo)


def _adamw_math(w, g, m, v):
    m = ADAM_B1 * m + (1.0 - ADAM_B1) * g
    v = ADAM_B2 * v + (1.0 - ADAM_B2) * (g * g)
    m_hat = m / (1.0 - ADAM_B1 ** ADAM_STEP)
    v_hat = v / (1.0 - ADAM_B2 ** ADAM_STEP)
    delta = -ADAM_LR * (m_hat / (jnp.sqrt(v_hat) + ADAM_EPS) + ADAM_WD * w)
    return delta, m, v


def _adamw(w, g, m, v, name):
    R, C = w.shape
    tr = _pick(R, tuple(t for t in (256, 128, 64, 32, 16, 8) if t * C * 4 <= ADAMW_BLOCK_BYTES))

    def body(w_ref, g_ref, m_ref, v_ref, d_ref, mo_ref, vo_ref):
        d, mn, vn = _adamw_math(w_ref[...], g_ref[...], m_ref[...], v_ref[...])
        d_ref[...] = d
        mo_ref[...] = mn
        vo_ref[...] = vn

    spec = pl.BlockSpec((tr, C), lambda i: (i, 0))
    return pl.pallas_call(
        body, name=name, grid=(R // tr,),
        in_specs=[spec] * 4, out_specs=[spec] * 3,
        out_shape=[jax.ShapeDtypeStruct((R, C), F32)] * 3,
        compiler_params=_params(("parallel",)),
    )(w, g, m, v)


def _norm_mix_in_fwd(xprev, branch, mods, g, gate, shift_idx, scale_idx, w_t, gq, gk, cos_t, sin_t, name):
    x_specs, x_args, (T, D) = _rows_operand(xprev)
    N = w_t.shape[0]
    QW = N_Q_HEADS * HEAD_DIM
    KW = N_KV_HEADS * HEAD_DIM
    q0, k0, v0 = 3 * D, 3 * D + QW, 3 * D + QW + KW
    edges = [0, D, 2 * D, q0, k0, v0 + KW] + list(range(v0 + KW + D, N + 1, D))
    assert edges[-1] == N

    def body(*refs):
        f_ref, m_ref, g_ref, w_ref, gq_ref, gk_ref, c_ref, s_ref = refs[len(x_args):len(x_args) + 8]
        xo_ref, h_ref, p_ref, qo_ref, ko_ref, vo_ref = refs[len(x_args) + 8:]
        m = m_ref[0]
        gate_idx, fac = gate
        x = _rows_tile(refs[:len(x_args)]) + (fac * m[gate_idx:gate_idx + 1, :]) * f_ref[...]
        xo_ref[...] = x
        hv = _norm_tile_fwd(x, m, g_ref[...], shift_idx, scale_idx)
        h_ref[...] = hv
        c = c_ref[...]
        s = s_ref[...]

        def head(xh, gain):
            inv = lax.rsqrt(jnp.mean(xh * xh, axis=-1, keepdims=True) + EPS)
            y = (xh * inv) * gain
            return y * c + _swap_halves(y) * s

        for lo, hi in zip(edges[:-1], edges[1:]):
            pb = lax.dot_general(hv, w_ref[lo:hi, :], _NT, preferred_element_type=F32).astype(BF16)
            p_ref[:, lo:hi] = pb
            if lo == q0:
                for h in range(N_Q_HEADS):
                    sl = slice(h * HEAD_DIM, (h + 1) * HEAD_DIM)
                    qo_ref[:, sl] = head(pb[:, sl].astype(F32), gq_ref[...]).astype(BF16)
            elif lo == k0:
                for h in range(N_KV_HEADS):
                    sl = slice(h * HEAD_DIM, (h + 1) * HEAD_DIM)
                    ko_ref[:, sl] = head(pb[:, sl].astype(F32), gk_ref[...]).astype(BF16)
                vo_ref[...] = pb[:, KW:2 * KW]

    return pl.pallas_call(
        body, name=name, grid=(T // ROW,),
        in_specs=x_specs + [_row_spec(D), _mods_spec(D), _vec_spec(1, D), _resident(),
                            _vec_spec(1, HEAD_DIM), _vec_spec(1, HEAD_DIM), _row_spec(HEAD_DIM), _row_spec(HEAD_DIM)],
        out_specs=[_row_spec(D), _row_spec(D), _row_spec(N), _row_spec(QW), _row_spec(KW), _row_spec(KW)],
        out_shape=[jax.ShapeDtypeStruct((T, D), F32), jax.ShapeDtypeStruct((T, D), BF16),
                   jax.ShapeDtypeStruct((T, N), BF16), jax.ShapeDtypeStruct((T, QW), BF16),
                   jax.ShapeDtypeStruct((T, KW), BF16), jax.ShapeDtypeStruct((T, KW), BF16)],
        compiler_params=_params(("parallel",)),
    )(*x_args, branch, mods, g, w_t, gq, gk, cos_t, sin_t)


def _mix_in_norm_bwd(dyc, dgt, P, conv_w, dq, dk, dv, gq, gk, cos_t, sin_t, w_t, x, dres, mods, g, shift_idx,
                     scale_idx, gate, branch, name):
    T, D = x.shape
    nt = T // ROW
    QW = N_Q_HEADS * HEAD_DIM
    KW = N_KV_HEADS * HEAD_DIM
    q0, g0 = 3 * D, 3 * D + QW + 2 * KW
    assert g0 + dgt.shape[1] == w_t.shape[0]
    conv_specs, conv_args = _conv_bwd_operands(P, dyc, conv_w, D)
    nc = len(conv_args)

    def body(*refs):
        (dg_ref, q_ref, k_ref, dq_ref, dk_ref, dv_ref, gq_ref, gk_ref, c_ref, s_ref,
         w_ref, x_ref, dr_ref, b_ref, m_ref, g_ref,
         dx_ref, db_ref, acc_ref, dc_ref, cacc_ref, o_ref, qacc_ref) = refs[nc:]
        _acc_init(acc_ref)
        _acc_init(cacc_ref)
        _acc_init(qacc_ref)
        _conv_tile_bwd(refs[:nc], dc_ref, cacc_ref, D, nt)
        dh = jnp.dot(dc_ref[...], w_ref[0:q0, :], preferred_element_type=F32)
        c = c_ref[...]
        s = s_ref[...]

        def head(xh, d, gain):
            dyv = d * c + _swap_halves(d * s)
            inv = lax.rsqrt(jnp.mean(xh * xh, axis=-1, keepdims=True) + EPS)
            xn = xh * inv
            dxn = dyv * gain
            dxh = inv * (dxn - xn * jnp.mean(dxn * xn, axis=-1, keepdims=True))
            return dxh, jnp.sum(dyv * xn, axis=0, keepdims=True)

        dgq = jnp.zeros((1, HEAD_DIM), F32)
        for h in range(N_Q_HEADS):
            sl = slice(h * HEAD_DIM, (h + 1) * HEAD_DIM)
            dxh, dgh = head(q_ref[:, sl].astype(F32), dq_ref[:, sl], gq_ref[...])
            o_ref[:, sl] = dxh.astype(BF16)
            dgq = dgq + dgh
        dh = dh + jnp.dot(dg_ref[...], w_ref[g0:, :], preferred_element_type=F32)
        dgk = jnp.zeros((1, HEAD_DIM), F32)
        for h in range(N_KV_HEADS):
            sl = slice(h * HEAD_DIM, (h + 1) * HEAD_DIM)
            dxh, dgh = head(k_ref[:, sl].astype(F32), dk_ref[:, sl], gk_ref[...])
            o_ref[:, QW + h * HEAD_DIM:QW + (h + 1) * HEAD_DIM] = dxh.astype(BF16)
            dgk = dgk + dgh
        o_ref[:, QW + KW:QW + 2 * KW] = dv_ref[...].astype(BF16)
        qacc_ref[0, 0:1, 0:HEAD_DIM] += dgq
        qacc_ref[0, 1:2, 0:HEAD_DIM] += dgk
        dh = dh + jnp.dot(o_ref[...], w_ref[q0:g0, :], preferred_element_type=F32)
        m = m_ref[0]
        dx = _norm_tile_bwd(x_ref[...], dh, dr_ref[...], m, g_ref[...], shift_idx, scale_idx, acc_ref)
        dx_ref[...] = dx
        db_ref[...] = _gate_tile_bwd(dx, b_ref[...], m, gate, acc_ref)

    return pl.pallas_call(
        body, name=name, grid=(T // ROW,),
        in_specs=conv_specs +
                 [_row_spec(dgt.shape[1]), _row_spec(QW, q0 // QW), _row_spec(KW, (q0 + QW) // KW),
                  _row_spec(QW), _row_spec(KW), _row_spec(KW), _vec_spec(1, HEAD_DIM), _vec_spec(1, HEAD_DIM),
                  _row_spec(HEAD_DIM), _row_spec(HEAD_DIM),
                  _resident(), _row_spec(D), _row_spec(D), _row_spec(D), _mods_spec(D), _vec_spec(1, D)],
        out_specs=[_row_spec(D), _row_spec(D), _acc_spec(D), _row_spec(q0), _acc_spec(D),
                   _row_spec(QW + 2 * KW), _acc_spec(D)],
        out_shape=[jax.ShapeDtypeStruct((T, D), F32), jax.ShapeDtypeStruct((T, D), BF16),
                   jax.ShapeDtypeStruct((2, ACC_ROWS, D), F32), jax.ShapeDtypeStruct((T, q0), BF16),
                   jax.ShapeDtypeStruct((2, ACC_ROWS, D), F32), jax.ShapeDtypeStruct((T, QW + 2 * KW), BF16),
                   jax.ShapeDtypeStruct((2, ACC_ROWS, D), F32)],
        compiler_params=_params(("arbitrary",)),
    )(*conv_args, dgt, P, P, dq, dk, dv, gq, gk, cos_t, sin_t, w_t, x, dres, branch, mods, g)


def _adamw_transposed(w, gt, m, v, name):
    R, C = w.shape
    tc = LANES

    def body(w_ref, g_ref, m_ref, v_ref, go_ref, d_ref, mo_ref, vo_ref):
        g = jnp.transpose(g_ref[...])
        d, mn, vn = _adamw_math(w_ref[...], g, m_ref[...], v_ref[...])
        go_ref[...] = g
        d_ref[...] = d
        mo_ref[...] = mn
        vo_ref[...] = vn

    spec = pl.BlockSpec((R, tc), lambda j: (0, j))
    return pl.pallas_call(
        body, name=name, grid=(C // tc,),
        in_specs=[spec, pl.BlockSpec((tc, R), lambda j: (j, 0)), spec, spec], out_specs=[spec] * 4,
        out_shape=[jax.ShapeDtypeStruct((R, C), F32)] * 4,
        compiler_params=_params(("parallel",)),
    )(w, gt, m, v)


class _NoExchange:
    def __init__(self, rest):
        self.rest = rest

    def rest_weights(self, after):
        return self.rest[0], lambda first, after2: (None, lambda after3: self.rest)

    def reduce_early(self, grads, tag):
        return None


def _local_step(xcat, target, mods, norm_g, final_g, gq, gk, conv_w, ffn1_w, hooks, rope):
    T, D = _rows_operand(xcat)[2]
    w1i, w1o = ffn1_w
    g1, g2, g3 = norm_g
    cos_t, sin_t = rope

    def after(value, token, name):
        return value if token is None else _after(value, token, name)

    _, h1, u1, s1, f1 = _norm_ffn_fwd(xcat, None, mods, g1, None, 0, 1, w1i, w1o, "f_ffn1")
    wi, more_weights = hooks.rest_weights(f1)
    x1, h2, P, qn, kn, vb = _norm_mix_in_fwd(xcat, f1, mods, g2, (2, 0.5), 3, 4, wi, gq, gk, cos_t, sin_t, "f_mix_in")
    token_w, last_weights = more_weights(wi, qn)
    o, lse = _flash_fwd(qn, kn, vb, "f_attn", token=token_w)
    wi, wbc, wba, wo, w2i, w2o = last_weights(o)
    yc, a1, a2, z, mo = _merge_fwd(o, P, conv_w, wbc, wba, wo, D, "f_merge")
    x2, h3, u2, s2, dx3, df2, acc_head = _norm_ffn_fwd(x1, mo, mods, g3, (5, 1.0), 6, 7, w2i, w2o, "f_ffn2",
                                                       head=(final_g, target))

    du2, dx2, dmo, acc_n3 = _ffn_norm_bwd(df2, u2, w2i, w2o, x2, dx3, mods, g3, 6, 7, (5, 1.0), mo, "b_ffn2")
    g_w2o = _grad_matmul(s2, df2, "b_ffn2_out_dw")
    g_w2i = _grad_matmul(du2, h3, "b_ffn2_in_dw")

    g_wo = _grad_matmul(z, dmo, "b_mix_out_dw")
    da1, da2, dgt, dyc, dob, delta = _merge_bwd(dmo, a1, a2, o, P, wbc, wba, wo, D, "b_merge")
    g_wbc = _grad_matmul(yc, da1, "b_branch_conv_dw")
    g_wba = _grad_matmul(o, da2, "b_branch_attn_dw")
    token_a = hooks.reduce_early([g_wbc, g_wba, g_wo, g_w2i, g_w2o], "a")
    dq, dk, dv = _flash_bwd(qn, kn, vb, dob, lse, delta, "b_attn", token=token_a)
    dx1, df1, acc_n2, dconv, acc_conv, dqkv, acc_qk = _mix_in_norm_bwd(
        dyc, dgt, P, conv_w, dq, dk, dv, gq, gk, cos_t, sin_t, wi, x1, dx2, mods, g2, 3, 4, (2, 0.5), f1, "b_mix_in")
    d_parts = (dconv, dqkv, dgt)
    g_wi = jnp.concatenate([_grad_matmul(dp, h2, f"b_mix_in_dw_{i}") for i, dp in enumerate(d_parts)], axis=0)
    g1_b = after(g1, hooks.reduce_early([g_wi], "b"), "after_rs_b")

    du1, grad_x, _, acc_n1 = _ffn_norm_bwd(df1, u1, w1i, w1o, xcat, dx1, mods, g1_b, 0, 1, None, None, "b_ffn1",
                                           skip_first_tile=True)
    g_w1o = _grad_matmul(s1, df1, "b_ffn1_out_dw")
    g_w1i = _grad_matmul(du1, h1, "b_ffn1_in_dw", token=hooks.reduce_early([g_w1o], "c"))

    grads = (g_w1i, g_w1o, g_wi, g_wbc, g_wba, g_wo, g_w2i, g_w2o)
    accs = (acc_head, acc_n3, acc_n2, acc_n1, acc_conv, acc_qk)
    return grad_x, grads, accs


def _place():
    return lax.axis_index("x"), lax.axis_index("y"), lax.axis_index("c")


def _other_chips(x, y):
    return [(1 - x, y), (x, 1 - y), (1 - x, 1 - y)]


def _allgather8(v, name):
    R, N = v.shape

    def body(v_ref, out_ref, send_sems, recv_sems, local_sem):
        x, y, c = _place()
        me, sibling = (x, y, c), (x, y, 1 - c)
        chips = _other_chips(x, y)

        def blk(px, py, pc):
            return out_ref.at[4 * px + 2 * py + pc]

        def copy(k, block, to, src=None):
            return pltpu.make_async_remote_copy(
                src_ref=blk(*block) if src is None else src, dst_ref=blk(*block),
                send_sem=send_sems.at[k], recv_sem=recv_sems.at[k], device_id=to, device_id_type=MESH)

        mine = pltpu.make_async_copy(v_ref, blk(*me), local_sem)
        mine.start()
        first = [copy(0, me, sibling, src=v_ref)]
        first += [copy(1 + j, me, (*chip, c), src=v_ref) for j, chip in enumerate(chips)]
        for cp in first:
            cp.start()
        passed = [copy(4 + j, (*chip, c), sibling) for j, chip in enumerate(chips)]
        for j, chip in enumerate(chips):
            copy(1 + j, (*chip, c), me).wait_recv()
            passed[j].start()
        copy(0, sibling, me).wait_recv()
        for j, chip in enumerate(chips):
            copy(4 + j, (*chip, 1 - c), me).wait_recv()
        for cp in first + passed:
            cp.wait_send()
        mine.wait()

    return pl.pallas_call(
        body, name=name,
        out_shape=jax.ShapeDtypeStruct((N_DEV, R, N), v.dtype),
        in_specs=[pl.BlockSpec(memory_space=pltpu.VMEM)],
        out_specs=pl.BlockSpec(memory_space=pltpu.VMEM),
        scratch_shapes=[pltpu.SemaphoreType.DMA((7,)), pltpu.SemaphoreType.DMA((7,)), pltpu.SemaphoreType.DMA],
        compiler_params=pltpu.CompilerParams(vmem_limit_bytes=VMEM_LIMIT),
    )(v)


def _any_specs(n):
    return [pl.BlockSpec(memory_space=pl.ANY)] * n


def _place_shard(w2, idx, transpose, name, token):
    if transpose:
        D, rs = w2.shape
        tr = LANES
        in_spec = pl.BlockSpec((D, tr), lambda i, idx: (0, i))
    else:
        rs, D = w2.shape
        tr = _pick(rs, (352, 256, 128, 64, 32, 16))
        in_spec = pl.BlockSpec((tr, D), lambda i, idx: (i, 0))
    steps = rs // tr

    def body(idx_ref, w_ref, t_ref, o_ref):
        v = w_ref[...]
        o_ref[...] = (jnp.transpose(v) if transpose else v).astype(BF16)

    return pl.pallas_call(
        body, name=name,
        grid_spec=pltpu.PrefetchScalarGridSpec(
            num_scalar_prefetch=1, grid=(steps,),
            in_specs=[in_spec, pl.BlockSpec(token.shape, lambda i, idx: (0, 0))],
            out_specs=pl.BlockSpec((tr, D), lambda i, idx: (idx[1] * steps + i, 0))),
        out_shape=jax.ShapeDtypeStruct((N_CHIPS * rs, D), BF16),
        compiler_params=_params(("arbitrary",)),
    )(idx, w2, token)


_HBM = pl.BlockSpec(memory_space=pltpu.HBM)
_SEM = pl.BlockSpec(memory_space=pltpu.SEMAPHORE)
_EFFECT = pltpu.SideEffectType.DATAFLOW_SIDE_EFFECTING


def _in_hbm(a):
    return pltpu.with_memory_space_constraint(a, pltpu.HBM)


def _split_copies(n, per, make):
    def start(nbuf, name, bufs, after=None):
        extra = [] if after is None else [after]

        def body(*refs):
            ins = refs[:nbuf]
            send_sems, recv_sems = refs[nbuf + len(extra)], refs[nbuf + len(extra) + 1]
            token = refs[-1]
            for t in range(n):
                for j in range(per):
                    make(ins, t, j, send_sems.at[per * t + j], recv_sems.at[per * t + j]).start()
            token[...] = jnp.zeros(token.shape, token.dtype)

        out = pl.pallas_call(
            body, name=name,
            out_shape=(pltpu.SemaphoreType.DMA((per * n,)), pltpu.SemaphoreType.DMA((per * n,)),
                       *[pltpu.HBM(b.shape, b.dtype) for b in bufs], jax.ShapeDtypeStruct((8, 128), F32)),
            in_specs=[_HBM] * nbuf + [pl.BlockSpec(memory_space=pl.ANY)] * len(extra),
            out_specs=(_SEM, _SEM, *[_HBM] * nbuf, pl.BlockSpec(memory_space=pltpu.VMEM)),
            input_output_aliases={i: 2 + i for i in range(nbuf)},
            compiler_params=pltpu.CompilerParams(has_side_effects=_EFFECT),
        )(*[_in_hbm(b) for b in bufs], *extra)
        return out[0], out[1], list(out[2:2 + nbuf]), out[-1]

    def wait(nbuf, name, send_sems, recv_sems, bufs, after, tensors=range(n)):
        afters = list(after) if isinstance(after, (list, tuple)) else [after]

        def body(*refs):
            ins = refs[:nbuf]
            ss, rs = refs[nbuf], refs[nbuf + 1]
            for t in tensors:
                for j in range(per):
                    cp = make(ins, t, j, ss.at[per * t + j], rs.at[per * t + j])
                    cp.wait_send()
                    cp.wait_recv()

        return pl.pallas_call(
            body, name=name,
            out_shape=[pltpu.HBM(b.shape, b.dtype) for b in bufs],
            in_specs=[_HBM] * nbuf + [_SEM, _SEM] + [pl.BlockSpec(memory_space=pl.ANY)] * len(afters),
            out_specs=[_HBM] * nbuf,
            input_output_aliases={i: i for i in range(nbuf)},
            compiler_params=pltpu.CompilerParams(has_side_effects=_EFFECT),
        )(*bufs, send_sems, recv_sems, *afters)

    return start, wait


RS_PEERS = N_DEV - 1


def _reduce_exchange_split(grads):
    n = len(grads)

    def make(bufs, t, j, send_sem, recv_sem):
        x, y, c = _place()
        chip = (x, y) if j == 6 else _other_chips(x, y)[j % 3]
        core = c if j < 3 else 1 - c
        half = grads[t].shape[0] // (2 * N_CHIPS)
        piece = bufs[t].at[pl.ds((2 * (2 * chip[0] + chip[1]) + core) * half, half), :]
        return pltpu.make_async_remote_copy(src_ref=piece, dst_ref=bufs[n + t].at[j], send_sem=send_sem,
                                            recv_sem=recv_sem, device_id=(*chip, core), device_id_type=MESH)

    return _split_copies(n, RS_PEERS, make)


def _reduce_sum(g, landed, idx, name):
    _, half, D = landed.shape
    g4 = g.reshape(N_CHIPS, 2, half, D)
    tr = _pick(half, (416, 352, 128))
    steps = half // tr

    def body(idx_ref, g_ref, l_ref, o_ref):
        acc = g_ref[0, 0].astype(F32)
        for j in range(RS_PEERS):
            acc = acc + l_ref[j].astype(F32)
        o_ref[...] = acc

    return pl.pallas_call(
        body, name=name,
        grid_spec=pltpu.PrefetchScalarGridSpec(
            num_scalar_prefetch=1, grid=(steps,),
            in_specs=[pl.BlockSpec((1, 1, tr, D), lambda i, idx: (idx[1], idx[0], i, 0)),
                      pl.BlockSpec((RS_PEERS, tr, D), lambda i, idx: (0, i, 0))],
            out_specs=pl.BlockSpec((tr, D), lambda i, idx: (idx[0] * steps + i, 0))),
        out_shape=jax.ShapeDtypeStruct((2 * half, D), F32),
        compiler_params=_params(("arbitrary",)),
    )(idx, g4, landed)


def _weights_gather_split(fulls):
    def make(bufs, t, j, send_sem, recv_sem):
        x, y, c = _place()
        chip = _other_chips(x, y)[j]
        rs = fulls[t].shape[0] // N_CHIPS
        rows = bufs[t].at[pl.ds((2 * x + y) * rs + c * (rs // 2), rs // 2), :]
        return pltpu.make_async_remote_copy(src_ref=rows, dst_ref=rows, send_sem=send_sem, recv_sem=recv_sem,
                                            device_id=(*chip, c), device_id_type=MESH)

    return _split_copies(len(fulls), 3, make)


def _weights_pass_on_split(fulls):
    def make(bufs, t, j, send_sem, recv_sem):
        x, y, c = _place()
        px, py = _other_chips(x, y)[j]
        rs = fulls[t].shape[0] // N_CHIPS
        rows = bufs[t].at[pl.ds((2 * px + py) * rs + c * (rs // 2), rs // 2), :]
        return pltpu.make_async_remote_copy(src_ref=rows, dst_ref=rows, send_sem=send_sem, recv_sem=recv_sem,
                                            device_id=(x, y, 1 - c), device_id_type=MESH)

    return _split_copies(len(fulls), 3, make)


def _weights_pass_on(fulls, name):
    n = len(fulls)

    def body(*refs):
        full = refs[n:2 * n]
        send_sems, recv_sems = refs[2 * n:]
        x, y, c = _place()
        chips = _other_chips(x, y)

        def copy(t, j, h):
            rs = fulls[t].shape[0] // N_CHIPS
            px, py = chips[j]
            rows = full[t].at[pl.ds((2 * px + py) * rs + h * (rs // 2), rs // 2), :]
            return pltpu.make_async_remote_copy(src_ref=rows, dst_ref=rows, send_sem=send_sems.at[3 * t + j],
                                                recv_sem=recv_sems.at[3 * t + j], device_id=(x, y, 1 - c),
                                                device_id_type=MESH)

        for t in range(n):
            for j in range(3):
                copy(t, j, c).start()
        for t in range(n):
            for j in range(3):
                copy(t, j, 1 - c).wait_recv()
        for t in range(n):
            for j in range(3):
                copy(t, j, c).wait_send()

    return pl.pallas_call(
        body, name=name,
        out_shape=[jax.ShapeDtypeStruct(f.shape, f.dtype) for f in fulls],
        in_specs=_any_specs(n), out_specs=_any_specs(n),
        input_output_aliases={t: t for t in range(n)},
        scratch_shapes=[pltpu.SemaphoreType.DMA((3 * n,)), pltpu.SemaphoreType.DMA((3 * n,))],
    )(*fulls)


def _after(value, token, name):
    def body(v_ref, t_ref, o_ref):
        o_ref[...] = v_ref[...]

    return pl.pallas_call(
        body, name=name, out_shape=jax.ShapeDtypeStruct(value.shape, value.dtype),
        in_specs=_whole(2), out_specs=pl.BlockSpec(memory_space=pltpu.VMEM),
    )(value, token)


def _pair_swap(shards, name):
    n = len(shards)

    def body(*refs):
        full = refs[n:2 * n]
        send_sems, recv_sems = refs[2 * n:]
        x, y, c = _place()

        def half(t, h):
            rows = shards[t].shape[0] // 2
            return full[t].at[pl.ds(h * rows, rows), :]

        def copy(t, h):
            return pltpu.make_async_remote_copy(src_ref=half(t, h), dst_ref=half(t, h), send_sem=send_sems.at[t],
                                                recv_sem=recv_sems.at[t], device_id=(x, y, 1 - c),
                                                device_id_type=MESH)

        for t in range(n):
            copy(t, c).start()
        for t in range(n):
            copy(t, 1 - c).wait_recv()
        for t in range(n):
            copy(t, c).wait_send()

    return pl.pallas_call(
        body, name=name,
        out_shape=[jax.ShapeDtypeStruct(a.shape, a.dtype) for a in shards],
        in_specs=_any_specs(n), out_specs=_any_specs(n),
        input_output_aliases={t: t for t in range(n)},
        scratch_shapes=[pltpu.SemaphoreType.DMA((n,)), pltpu.SemaphoreType.DMA((n,))],
    )(*shards)


def _pair_swap_split(shards):
    def make(bufs, t, j, send_sem, recv_sem):
        x, y, c = _place()
        rows = shards[t].shape[0] // 2
        half = bufs[t].at[pl.ds(c * rows, rows), :]
        return pltpu.make_async_remote_copy(src_ref=half, dst_ref=half, send_sem=send_sem, recv_sem=recv_sem,
                                            device_id=(x, y, 1 - c), device_id_type=MESH)

    return _split_copies(len(shards), 1, make)


def _gather_begin(fulls, tag):
    start, wait = _weights_gather_split(fulls)
    send_sems, recv_sems, bufs, token = start(len(fulls), f"ag_{tag}_start", fulls)
    return (wait, send_sems, recv_sems, bufs), token


def _gather_end(state, after, tag):
    wait, send_sems, recv_sems, bufs = state
    landed = wait(len(bufs), f"ag_{tag}_wait", send_sems, recv_sems, bufs, after)
    return _weights_pass_on(landed, f"ag_{tag}_pass_on")


class _Exchanges:
    def __init__(self, fulls_rest, idx):
        self.idx = idx
        self._rest, self.token = _gather_begin(fulls_rest, "rest")
        self._early = []

    def rest_weights(self, after):
        wait, send_sems, recv_sems, bufs = self._rest
        n = len(bufs)
        landed = wait(n, "ag_rest_wait_0", send_sems, recv_sems, bufs, after, tensors=(0,))
        (first,) = _weights_pass_on(landed[:1], "ag_rest_pass_on_0")

        def more(first, after2):
            done = wait(n, "ag_rest_wait_1", send_sems, recv_sems, [first, *landed[1:]], after2, tensors=range(1, n))
            pass_start, pass_wait = _weights_pass_on_split(done[1:])
            pass_send, pass_recv, passing, token = pass_start(n - 1, "ag_rest_pass_on_start", done[1:])

            def last(after3):
                return [done[0], *pass_wait(n - 1, "ag_rest_pass_on_wait", pass_send, pass_recv, passing, after3)]

            return token, last

        return first, more

    def reduce_early(self, grads, tag, token=None):
        zones = [lax.empty((RS_PEERS, g.shape[0] // (2 * N_CHIPS), g.shape[1]), g.dtype) for g in grads]
        start, wait = _reduce_exchange_split(grads)
        send_sems, recv_sems, bufs, token = start(2 * len(grads), "rs_start_" + tag, list(grads) + zones, token)
        self._early.append((tag, wait, send_sems, recv_sems, bufs))
        return token

    def finish(self, tags, after):
        halves = []
        for tag, wait, send_sems, recv_sems, bufs in self._early:
            if tag in tags:
                n = len(bufs) // 2
                done = wait(len(bufs), "rs_wait_" + tag, send_sems, recv_sems, bufs, after)
                halves += [_reduce_sum(g, l, self.idx, f"rs_sum_{tag}{t}")
                           for t, (g, l) in enumerate(zip(done[:n], done[n:]))]
        return halves


N_MOD = 9
PACK_HEAD, PACK_N3, PACK_N2, PACK_N1, PACK_CONV, PACK_QK = 0, 16, 32, 48, 64, 80
MOD_SRC = ((PACK_N1, 0), (PACK_N1, 1), (PACK_N2, 3), (PACK_N2, 0), (PACK_N2, 1),
           (PACK_N3, 3), (PACK_N3, 0), (PACK_N3, 1), (PACK_HEAD, 2))
CTX_ROW = 8


def _silu(v):
    return v * jax.nn.sigmoid(v)


def _whole(n):
    return [pl.BlockSpec(memory_space=pltpu.VMEM)] * n


def _mod_rows(cin, w_sh, b_sh, name):
    def body(c_ref, w_ref, b_ref, o_ref):
        a = _silu(c_ref[...]).astype(BF16)
        o_ref[...] = jnp.dot(a, w_ref[...].astype(BF16), preferred_element_type=F32) + b_ref[...]

    return pl.pallas_call(
        body, name=name, out_shape=jax.ShapeDtypeStruct((cin.shape[0], w_sh.shape[1]), F32),
        in_specs=_whole(3), out_specs=pl.BlockSpec(memory_space=pltpu.VMEM),
        compiler_params=pltpu.CompilerParams(vmem_limit_bytes=VMEM_LIMIT),
    )(cin, w_sh, b_sh)


def _small_reduce(gathered, name):
    _, _, D = gathered.shape

    def body(g_ref, loss_ref, db_ref, gn_ref, cv_ref, qk_ref, dm_ref):
        tot = g_ref[0]
        for r in range(1, N_DEV):
            tot = tot + g_ref[r]

        def both(block, row):
            return tot[block + row:block + row + 1, :] + tot[block + 8 + row:block + 8 + row + 1, :]

        loss = jnp.sum(both(PACK_HEAD, 0), axis=1, keepdims=True)
        loss_ref[...] = jnp.broadcast_to(loss, loss_ref.shape)
        db_ref[...] = jnp.zeros(db_ref.shape, F32)
        dm_ref[...] = jnp.zeros(dm_ref.shape, F32)
        for j, (block, row) in enumerate(MOD_SRC):
            db_ref[j:j + 1, :] = both(block, row)
            dm_ref[CTX_ROW, j:j + 1, :] = tot[block + row:block + row + 1, :]
            for r in range(N_DEV):
                dm_ref[r, j:j + 1, :] = g_ref[r, block + 8 + row:block + 8 + row + 1, :]
        gn_ref[...] = jnp.zeros(gn_ref.shape, F32)
        gn_ref[0:1, :] = both(PACK_N1, 2)
        gn_ref[8:9, :] = both(PACK_N2, 2)
        gn_ref[16:17, :] = both(PACK_N3, 2)
        gn_ref[24:25, :] = both(PACK_HEAD, 1)
        cv_ref[...] = jnp.zeros(cv_ref.shape, F32)
        for r in range(3):
            cv_ref[r:r + 1, :] = both(PACK_CONV, r)
        qk_ref[...] = jnp.zeros(qk_ref.shape, F32)
        qk_ref[0:1, 0:HEAD_DIM] = both(PACK_QK, 0)[:, 0:HEAD_DIM]
        qk_ref[0:1, HEAD_DIM:2 * HEAD_DIM] = both(PACK_QK, 1)[:, 0:HEAD_DIM]

    return pl.pallas_call(
        body, name=name,
        out_shape=[jax.ShapeDtypeStruct((8, 128), F32), jax.ShapeDtypeStruct((16, D), F32),
                   jax.ShapeDtypeStruct((32, D), F32), jax.ShapeDtypeStruct((8, D), F32),
                   jax.ShapeDtypeStruct((8, D), F32), jax.ShapeDtypeStruct((16, 16, D), F32)],
        in_specs=_whole(1), out_specs=_whole(6),
        compiler_params=pltpu.CompilerParams(vmem_limit_bytes=VMEM_LIMIT),
    )(gathered)


def _wmod_grad(cin, dm_sh, w_sh, name):
    def body(c_ref, d_ref, w_ref, gw_ref, cp_ref):
        a = _silu(c_ref[...]).astype(BF16)
        d = d_ref[...].astype(BF16)
        gw_ref[...] = lax.dot_general(a, d, (((0,), (0,)), ((), ())), preferred_element_type=F32)
        cp_ref[...] = lax.dot_general(d, w_ref[...].astype(BF16), (((1,), (1,)), ((), ())),
                                      preferred_element_type=F32)

    return pl.pallas_call(
        body, name=name,
        out_shape=[jax.ShapeDtypeStruct(w_sh.shape, F32), jax.ShapeDtypeStruct(cin.shape, F32)],
        in_specs=_whole(3), out_specs=_whole(2),
        compiler_params=pltpu.CompilerParams(vmem_limit_bytes=VMEM_LIMIT),
    )(cin, dm_sh, w_sh)


def _cctx_grad(parts, c_ctx8, name):
    def body(p_ref, c_ref, o_ref):
        tot = p_ref[0] + p_ref[2] + p_ref[4] + p_ref[6]
        cv = c_ref[...]
        sig = jax.nn.sigmoid(cv)
        rows = lax.broadcasted_iota(jnp.int32, tot.shape, 0)
        o_ref[...] = jnp.where(rows == 0, tot * (sig * (1.0 + cv * (1.0 - sig))), 0.0)

    return pl.pallas_call(
        body, name=name, out_shape=jax.ShapeDtypeStruct(c_ctx8.shape, F32),
        in_specs=_whole(2), out_specs=pl.BlockSpec(memory_space=pltpu.VMEM),
    )(parts, c_ctx8)


def _pad_rows(a, rows):
    return jnp.pad(a, ((0, rows - a.shape[0]), (0, 0)))


def _pack_small(c_ctx, b_mod, n1, n2, n3, final_g, gq, gk, conv_sh, D):
    misc = jnp.concatenate([gq, gk, conv_sh.reshape(1, -1)], axis=1)
    return jnp.concatenate([_pad_rows(c_ctx[None], 8), _pad_rows(b_mod.reshape(N_MOD, D), 16), _pad_rows(n1, 8),
                            _pad_rows(n2, 8), _pad_rows(n3, 8), _pad_rows(final_g[None], 8), _pad_rows(misc, 8)], axis=0)


def _unpack_small(p, D, conv_shape):
    misc = p[56:57]
    return dict(c_ctx=p[0], b_mod=p[8:8 + N_MOD].reshape(1, N_MOD * D), norm1_g=p[24:25], norm2_g=p[32:33],
                norm3_g=p[40:41], final_g=p[48], q_norm_g=misc[:, 0:HEAD_DIM], k_norm_g=misc[:, HEAD_DIM:2 * HEAD_DIM],
                conv_w=misc[:, 2 * HEAD_DIM:].reshape(conv_shape))


WEIGHT_ORDER = ("c_ctx", "w_mod", "b_mod", "norm1_g", "norm2_g", "norm3_g", "ffn1_w_in", "ffn1_w_out", "w_in",
                "conv_w", "q_norm_g", "k_norm_g", "w_branch_conv", "w_branch_attn", "w_out", "ffn2_w_in",
                "ffn2_w_out", "final_g")
BIG = ("ffn1_w_in", "ffn1_w_out", "w_in", "w_branch_conv", "w_branch_attn", "w_out", "ffn2_w_in", "ffn2_w_out")
COLUMN_SHARDED = ("ffn1_w_in", "w_in", "ffn2_w_in")


def kernel(x, c, ctx, c_ctx, w_mod, b_mod, norm1_g, norm2_g, norm3_g, ffn1_w_in, ffn1_w_out, w_in, conv_w, q_norm_g, k_norm_g, w_branch_conv, w_branch_attn, w_out, ffn2_w_in, ffn2_w_out, final_g, loss_target, m_c_ctx, m_w_mod, m_b_mod, m_norm1_g, m_norm2_g, m_norm3_g, m_ffn1_w_in, m_ffn1_w_out, m_w_in, m_conv_w, m_q_norm_g, m_k_norm_g, m_w_branch_conv, m_w_branch_attn, m_w_out, m_ffn2_w_in, m_ffn2_w_out, m_final_g, v_c_ctx, v_w_mod, v_b_mod, v_norm1_g, v_norm2_g, v_norm3_g, v_ffn1_w_in, v_ffn1_w_out, v_w_in, v_conv_w, v_q_norm_g, v_k_norm_g, v_w_branch_conv, v_w_branch_attn, v_w_out, v_ffn2_w_in, v_ffn2_w_out, v_final_g):
    w = dict(c_ctx=c_ctx, w_mod=w_mod, b_mod=b_mod, norm1_g=norm1_g, norm2_g=norm2_g, norm3_g=norm3_g,
             ffn1_w_in=ffn1_w_in, ffn1_w_out=ffn1_w_out, w_in=w_in, conv_w=conv_w, q_norm_g=q_norm_g,
             k_norm_g=k_norm_g, w_branch_conv=w_branch_conv, w_branch_attn=w_branch_attn, w_out=w_out,
             ffn2_w_in=ffn2_w_in, ffn2_w_out=ffn2_w_out, final_g=final_g)
    m = dict(c_ctx=m_c_ctx, w_mod=m_w_mod, b_mod=m_b_mod, norm1_g=m_norm1_g, norm2_g=m_norm2_g, norm3_g=m_norm3_g,
             ffn1_w_in=m_ffn1_w_in, ffn1_w_out=m_ffn1_w_out, w_in=m_w_in, conv_w=m_conv_w, q_norm_g=m_q_norm_g,
             k_norm_g=m_k_norm_g, w_branch_conv=m_w_branch_conv, w_branch_attn=m_w_branch_attn, w_out=m_w_out,
             ffn2_w_in=m_ffn2_w_in, ffn2_w_out=m_ffn2_w_out, final_g=m_final_g)
    v = dict(c_ctx=v_c_ctx, w_mod=v_w_mod, b_mod=v_b_mod, norm1_g=v_norm1_g, norm2_g=v_norm2_g, norm3_g=v_norm3_g,
             ffn1_w_in=v_ffn1_w_in, ffn1_w_out=v_ffn1_w_out, w_in=v_w_in, conv_w=v_conv_w, q_norm_g=v_q_norm_g,
             k_norm_g=v_k_norm_g, w_branch_conv=v_w_branch_conv, w_branch_attn=v_w_branch_attn, w_out=v_w_out,
             ffn2_w_in=v_ffn2_w_in, ffn2_w_out=v_ffn2_w_out, final_g=v_final_g)

    xi, yi, ci = _place()
    dev = 4 * xi + 2 * yi + ci
    shard = 2 * xi + yi
    idx = jnp.stack([ci, shard, 2 * (1 - xi) + yi, 2 * xi + (1 - yi), 2 * (1 - xi) + (1 - yi)]).astype(jnp.int32)
    D = x.shape[-1]
    ctx_len = ctx.shape[1]
    assert ctx_len == ROW and c.shape == (1, D)
    mcols = w_mod.shape[2]
    ccols = conv_w.shape[2]

    def place(names, token):
        fulls = []
        for n in names:
            fulls.append(_place_shard(w[n][0], idx, n in COLUMN_SHARDED, "place_" + n, token))
            token = fulls[-1][:16, :HEAD_DIM]
        return fulls

    ffn1_gather, ffn1_token = _gather_begin(place(BIG[:2], c), "ffn1")
    fulls_rest = place(BIG[2:], ffn1_token)

    rope = _rope_tables(ctx_len, x.shape[1])
    c8 = jnp.broadcast_to(c, (8, D))
    for token, name in ((fulls_rest[-1][:16, :HEAD_DIM], "after_place"), (rope[0], "after_rope_cos"),
                        (rope[1], "after_rope_sin")):
        c8 = _after(c8, token, name)
    c_all = _allgather8(c8, "ag_c")[:, 0, :]
    cin = jnp.concatenate([c_all, _pad_rows(c_ctx[None], 8)], axis=0)
    b_sh = lax.dynamic_slice(b_mod, (0, shard * mcols), (1, mcols))
    mod_sh = _mod_rows(cin, w_mod[0], b_sh, "mod_rows")
    conv_rows = jnp.pad(conv_w[0], ((0, 8 - conv_w.shape[1]), (0, mcols - ccols)))
    mod_all = _allgather8(jnp.concatenate([mod_sh, conv_rows], axis=0), "ag_mod")
    mod_full = jnp.concatenate([mod_all[2 * s, :16] for s in range(N_CHIPS)], axis=1)
    conv_full = jnp.concatenate([mod_all[2 * s, 16:16 + conv_w.shape[1], :ccols] for s in range(N_CHIPS)], axis=1)
    mod_lat = lax.dynamic_slice(mod_full, (dev, 0), (1, N_MOD * D)).reshape(N_MOD, D)
    mod_ctx = mod_full[CTX_ROW].reshape(N_MOD, D)
    mods = jnp.stack([_pad_rows(mod_ctx, 16), _pad_rows(mod_lat, 16)])

    ffn1_w = _gather_end(ffn1_gather, mods, "ffn1")
    hooks = _Exchanges(fulls_rest, idx)

    xcat = (ctx[0], x[0])
    norm1_first = _after(norm1_g, hooks.token, "after_ag_rest")
    grad_x, grads, accs = _local_step(xcat, loss_target[0], mods, (norm1_first, norm2_g, norm3_g), final_g[None],
                                      q_norm_g, k_norm_g, conv_full, ffn1_w, hooks, rope)
    g = {}

    pack = jnp.concatenate([a.reshape(2 * ACC_ROWS, D) for a in accs], axis=0)
    gathered = _allgather8(pack, "ag_small")
    loss8, db_mod, g_norms, g_conv, g_qk, dm = _small_reduce(gathered, "small_reduce")
    dm_sh = lax.dynamic_slice(dm[:, :N_MOD, :].reshape(16, N_MOD * D), (0, shard * mcols), (16, mcols))
    g_wmod, cpart = _wmod_grad(cin, dm_sh, w_mod[0], "wmod_grad")
    g["w_mod"] = g_wmod[None]
    cparts = _allgather8(cpart[CTX_ROW:CTX_ROW + 8], "ag_cctx")
    g_cctx = _cctx_grad(cparts, _pad_rows(c_ctx[None], 8), "cctx_grad")
    g_conv_sh = lax.dynamic_slice(g_conv, (0, shard * ccols), (conv_w.shape[1], ccols))
    g_misc = jnp.concatenate([g_qk[0:1, 0:2 * HEAD_DIM], g_conv_sh.reshape(1, -1)], axis=1)
    g_pack = jnp.concatenate([g_cctx, db_mod, g_norms, _pad_rows(g_misc, 8)], axis=0)

    def packed(p):
        return _pack_small(p["c_ctx"], p["b_mod"], p["norm1_g"], p["norm2_g"], p["norm3_g"], p["final_g"],
                           p["q_norm_g"], p["k_norm_g"], p["conv_w"][0], D)

    d_pack, m_pack, v_pack = _adamw(packed(w), g_pack, packed(m), packed(v), "adamw_small")

    g.update(_unpack_small(g_pack, D, conv_w.shape))
    delta = _unpack_small(d_pack, D, conv_w.shape)
    new_m = _unpack_small(m_pack, D, conv_w.shape)
    new_v = _unpack_small(v_pack, D, conv_w.shape)

    def update(n, g2):
        if n in COLUMN_SHARDED:
            g2, d2, m2, v2 = _adamw_transposed(w[n][0], g2, m[n][0], v[n][0], "adamw_" + n)
        else:
            d2, m2, v2 = _adamw(w[n][0], g2, m[n][0], v[n][0], "adamw_" + n)
        g[n], delta[n], new_m[n], new_v[n] = g2[None], d2[None], m2[None], v2[None]
        return v2

    token_d = hooks.reduce_early([grads[0]], "d", token=d_pack[:8, :HEAD_DIM])
    h_wbc, h_wba, h_wo, h_w2i, h_w2o, h_wi, h_w1o = hooks.finish("abc", token_d)
    halves = [h_w1o, h_wi, h_wbc, h_wba, h_wo, h_w2i, h_w2o]
    swap_start, swap_wait = _pair_swap_split(halves)
    swap_send, swap_recv, swapping, _ = swap_start(len(halves), "rs_pair_swap_start", halves)
    v_wmod = update("w_mod", g_wmod)
    done = swap_wait(len(halves), "rs_pair_swap_wait", swap_send, swap_recv, swapping, v_wmod)
    updated = [v_wmod] + [update(n, r) for n, r in zip(BIG[1:], done)]
    (h_w1i,) = hooks.finish("d", updated)
    update(BIG[0], _pair_swap([h_w1i], "rs_pair_swap_d")[0])

    loss = loss8[0, 0]
    return (loss, grad_x[None], *[g[n] for n in WEIGHT_ORDER], *[delta[n] for n in WEIGHT_ORDER],
            *[new_m[n] for n in WEIGHT_ORDER], *[new_v[n] for n in WEIGHT_ORDER])
```
